```python
import math
import jax, jax.numpy as jnp
from jax import lax
import numpy as np

D_MODEL = 1024
BATCH = 16
SEQ = 2048
DEPTH = 1

CHUNK = 64
EPS = 1e-6
S5_WIDTH = 512
S5_GROUP = 16
S5_GROUPS = S5_WIDTH // S5_GROUP
S5_STATE = 64
HG_WIDTH = 512
HG_HEAD_DIM = 128
HG_HEADS = HG_WIDTH // HG_HEAD_DIM
D_FF = 2816
CONV_W = 3
N_IN = S5_WIDTH + 4 * HG_WIDTH + 2 * D_MODEL

kernel_name = "hybrid_s5_hgrn2_convffn_block"


def rmsnorm(x, gain):
    x32 = x.astype(jnp.float32)
    y = x32 * lax.rsqrt(jnp.mean(x32 * x32, axis=-1, keepdims=True) + EPS)
    return (y * gain.astype(jnp.float32)).astype(x.dtype)


def s5_mixer(u, a_re, a_im, log_dt, b_re, b_im, c_re, c_im, d_skip):
    bsz, seq, _ = u.shape
    f32 = jnp.float32
    ug = u.astype(f32).reshape(bsz, seq, S5_GROUPS, S5_GROUP)
    a_re = a_re.astype(f32); a_im = a_im.astype(f32)
    dt = jnp.exp(log_dt.astype(f32))[:, None]
    mag = jnp.exp(a_re * dt)
    ang = a_im * dt
    lb_re = mag * jnp.cos(ang)
    lb_im = mag * jnp.sin(ang)
    den = a_re * a_re + a_im * a_im
    n_re = lb_re - 1.0
    n_im = lb_im
    co_re = ((n_re * a_re + n_im * a_im) / den)[..., None]
    co_im = ((n_im * a_re - n_re * a_im) / den)[..., None]
    b_re = b_re.astype(f32); b_im = b_im.astype(f32)
    bb_re = co_re * b_re - co_im * b_im
    bb_im = co_re * b_im + co_im * b_re
    bu_re = jnp.einsum('bsgh,gph->bsgp', ug, bb_re)
    bu_im = jnp.einsum('bsgh,gph->bsgp', ug, bb_im)
    al_re = jnp.broadcast_to(lb_re, (1, seq, S5_GROUPS, S5_STATE))
    al_im = jnp.broadcast_to(lb_im, (1, seq, S5_GROUPS, S5_STATE))

    def combine(left, right):
        ar1, ai1, br1, bi1 = left
        ar2, ai2, br2, bi2 = right
        return (ar2 * ar1 - ai2 * ai1,
                ar2 * ai1 + ai2 * ar1,
                ar2 * br1 - ai2 * bi1 + br2,
                ar2 * bi1 + ai2 * br1 + bi2)

    _, _, x_re, x_im = lax.associative_scan(combine, (al_re, al_im, bu_re, bu_im), axis=1)
    y = (jnp.einsum('bsgp,ghp->bsgh', x_re, c_re.astype(f32))
         - jnp.einsum('bsgp,ghp->bsgh', x_im, c_im.astype(f32))
         + d_skip.astype(f32).reshape(S5_GROUPS, S5_GROUP) * ug)
    return y.reshape(bsz, seq, S5_WIDTH).astype(u.dtype)


def hgrn2_mixer(q, f_raw, i, g, lb, norm_gain):
    bsz, seq, _ = q.shape
    nc = seq // CHUNK
    f32 = jnp.float32
    lb = lb.astype(f32).reshape(HG_HEADS, HG_HEAD_DIM)
    f = lb + (1.0 - lb) * jax.nn.sigmoid(f_raw.astype(f32).reshape(bsz, seq, HG_HEADS, HG_HEAD_DIM))
    log_f = jnp.log(f)
    k = 1.0 - f
    qa = jax.nn.silu(q.astype(f32)) * (HG_HEAD_DIM ** -0.5)

    def blocks(t):
        return t.astype(f32).reshape(bsz, nc, CHUNK, HG_HEADS, HG_HEAD_DIM).transpose(1, 0, 2, 3, 4)

    causal = jnp.tril(jnp.ones((CHUNK, CHUNK), dtype=bool))[None, :, :, None, None]

    def step(state, xs):
        qc, kc, ic, lfc = xs
        bcum = jnp.cumsum(lfc, axis=1)
        o_inter = jnp.einsum('bthk,bhkv->bthv', qc * jnp.exp(bcum), state)
        decay = jnp.exp(jnp.where(causal, bcum[:, :, None] - bcum[:, None, :], -jnp.inf))
        scores = jnp.sum(qc[:, :, None] * kc[:, None, :] * decay, axis=-1)
        o_intra = jnp.einsum('btsh,bshv->bthv', scores, ic)
        b_last = bcum[:, -1]
        state = (jnp.exp(b_last)[..., None] * state
                 + jnp.einsum('bshk,bshv->bhkv', kc * jnp.exp(b_last[:, None] - bcum), ic))
        return state, o_inter + o_intra

    state0 = jnp.zeros((bsz, HG_HEADS, HG_HEAD_DIM, HG_HEAD_DIM), f32)
    _, o = lax.scan(step, state0, (blocks(qa), blocks(k), blocks(i), blocks(log_f)))
    o = o.transpose(1, 0, 2, 3, 4).reshape(bsz, seq, HG_HEADS, HG_HEAD_DIM)
    o = rmsnorm(o, norm_gain.reshape(HG_HEADS, HG_HEAD_DIM))
    o = o.reshape(bsz, seq, HG_WIDTH) * jax.nn.silu(g.astype(f32))
    return o.astype(q.dtype)


def conv_ffn(u, w_up, w_conv, b_conv, w_down):
    seq = u.shape[1]
    h = u @ w_up
    hp = jnp.pad(h, ((0, 0), (CONV_W - 1, 0), (0, 0)))
    hc = hp[:, 0:seq] * w_conv[0]
    for j in range(1, CONV_W):
        hc = hc + hp[:, j:j + seq] * w_conv[j]
    hc = hc + b_conv
    gate, val = jnp.split(hc, 2, axis=-1)
    return (jax.nn.silu(gate) * val) @ w_down


def _fwd_setup_inputs(seed: int = 0) -> dict:
    key = jax.random.key(seed)
    ks = jax.random.split(key, 24)
    f32 = jnp.float32
    L = DEPTH
    nrm = lambda k, shp, s: jax.random.normal(k, shp, f32) * s
    x = jax.random.normal(ks[0], (BATCH, SEQ, D_MODEL), f32)
    g_mix = 1.0 + nrm(ks[1], (L, D_MODEL), 0.01)
    w_in = nrm(ks[2], (L, D_MODEL, N_IN), D_MODEL ** -0.5)
    s5_a_re = -0.5 * (1.0 + nrm(ks[3], (L, S5_GROUPS, S5_STATE), 0.02))
    s5_a_im = jnp.broadcast_to(jnp.pi * jnp.arange(S5_STATE, dtype=f32), (L, S5_GROUPS, S5_STATE)) + nrm(ks[4], (L, S5_GROUPS, S5_STATE), 0.01)
    s5_log_dt = jax.random.uniform(ks[5], (L, S5_GROUPS), f32, math.log(1e-3), math.log(1e-1))
    s5_b_re = nrm(ks[6], (L, S5_GROUPS, S5_STATE, S5_GROUP), (2 * S5_GROUP) ** -0.5)
    s5_b_im = nrm(ks[7], (L, S5_GROUPS, S5_STATE, S5_GROUP), (2 * S5_GROUP) ** -0.5)
    s5_c_re = nrm(ks[8], (L, S5_GROUPS, S5_GROUP, S5_STATE), (2 * S5_STATE) ** -0.5)
    s5_c_im = nrm(ks[9], (L, S5_GROUPS, S5_GROUP, S5_STATE), (2 * S5_STATE) ** -0.5)
    s5_d = nrm(ks[10], (L, S5_WIDTH), 1.0)
    w_glu = nrm(ks[11], (L, S5_WIDTH, S5_WIDTH), S5_WIDTH ** -0.5)
    b_glu = nrm(ks[12], (L, S5_WIDTH), 0.01)
    hg_lb_logits = nrm(ks[13], (L + 1, HG_WIDTH), 0.1)
    hg_norm_gain = 1.0 + nrm(ks[14], (L, HG_WIDTH), 0.01)
    w_pa = nrm(ks[15], (L, S5_WIDTH, D_MODEL), S5_WIDTH ** -0.5)
    w_pb = nrm(ks[16], (L, HG_WIDTH, D_MODEL), HG_WIDTH ** -0.5)
    w_out = nrm(ks[17], (L, D_MODEL, D_MODEL), D_MODEL ** -0.5)
    g_ffn = 1.0 + nrm(ks[18], (L, D_MODEL), 0.01)
    w_up = nrm(ks[19], (L, D_MODEL, 2 * D_FF), D_MODEL ** -0.5)
    w_conv = nrm(ks[20], (L, CONV_W, 2 * D_FF), CONV_W ** -0.5)
    b_conv = nrm(ks[21], (L, 2 * D_FF), 0.01)
    w_down = nrm(ks[22], (L, D_FF, D_MODEL), D_FF ** -0.5)
    g_final = 1.0 + nrm(ks[23], (D_MODEL,), 0.01)
    return {"x": x, "g_mix": g_mix, "w_in": w_in, "s5_a_re": s5_a_re, "s5_a_im": s5_a_im,
            "s5_log_dt": s5_log_dt, "s5_b_re": s5_b_re, "s5_b_im": s5_b_im, "s5_c_re": s5_c_re,
            "s5_c_im": s5_c_im, "s5_d": s5_d, "w_glu": w_glu, "b_glu": b_glu,
            "hg_lb_logits": hg_lb_logits, "hg_norm_gain": hg_norm_gain, "w_pa": w_pa, "w_pb": w_pb,
            "w_out": w_out, "g_ffn": g_ffn, "w_up": w_up, "w_conv": w_conv, "b_conv": b_conv,
            "w_down": w_down, "g_final": g_final}


def _fwd_reference(x, g_mix, w_in, s5_a_re, s5_a_im, s5_log_dt, s5_b_re, s5_b_im, s5_c_re, s5_c_im,
              s5_d, w_glu, b_glu, hg_lb_logits, hg_norm_gain, w_pa, w_pb, w_out, g_ffn, w_up,
              w_conv, b_conv, w_down, g_final):
    lb_all = jnp.cumsum(jax.nn.softmax(hg_lb_logits.astype(jnp.float32), axis=0), axis=0)
    splits = [S5_WIDTH + j * HG_WIDTH for j in range(5)] + [S5_WIDTH + 4 * HG_WIDTH + D_MODEL]
    for l in range(DEPTH):
        u = rmsnorm(x, g_mix[l])
        z = u @ w_in[l]
        za, zq, zf, zi, zg, zga, zgb = jnp.split(z, splits, axis=-1)
        ya = s5_mixer(za, s5_a_re[l], s5_a_im[l], s5_log_dt[l], s5_b_re[l], s5_b_im[l],
                      s5_c_re[l], s5_c_im[l], s5_d[l])
        ya = jax.nn.gelu(ya)
        ya = ya * jax.nn.sigmoid(ya @ w_glu[l] + b_glu[l])
        yb = hgrn2_mixer(zq, zf, zi, zg, lb_all[l], hg_norm_gain[l])
        m = jax.nn.sigmoid(zga) * (ya @ w_pa[l]) + jax.nn.sigmoid(zgb) * (yb @ w_pb[l])
        x = x + m @ w_out[l]
        x = x + conv_ffn(rmsnorm(x, g_ffn[l]), w_up[l], w_conv[l], b_conv[l], w_down[l])
    return rmsnorm(x, g_final)


import jax as _jax
import jax.numpy as _jnp

TWIN_FORMAT = 'train_step'
FWD_PARAMS = ['x', 'g_mix', 'w_in', 's5_a_re', 's5_a_im', 's5_log_dt', 's5_b_re', 's5_b_im', 's5_c_re', 's5_c_im', 's5_d', 'w_glu', 'b_glu', 'hg_lb_logits', 'hg_norm_gain', 'w_pa', 'w_pb', 'w_out', 'g_ffn', 'w_up', 'w_conv', 'b_conv', 'w_down', 'g_final']
TWIN_WEIGHTS = ['g_mix', 'w_in', 's5_a_re', 's5_a_im', 's5_log_dt', 's5_b_re', 's5_b_im', 's5_c_re', 's5_c_im', 's5_d', 'w_glu', 'b_glu', 'hg_lb_logits', 'hg_norm_gain', 'w_pa', 'w_pb', 'w_out', 'g_ffn', 'w_up', 'w_conv', 'b_conv', 'w_down', 'g_final']
TWIN_DIFF_INPUT = 'x'
TWIN_INPUTS = ['x', 'g_mix', 'w_in', 's5_a_re', 's5_a_im', 's5_log_dt', 's5_b_re', 's5_b_im', 's5_c_re', 's5_c_im', 's5_d', 'w_glu', 'b_glu', 'hg_lb_logits', 'hg_norm_gain', 'w_pa', 'w_pb', 'w_out', 'g_ffn', 'w_up', 'w_conv', 'b_conv', 'w_down', 'g_final', 'loss_target', 'm_g_mix', 'm_w_in', 'm_s5_a_re', 'm_s5_a_im', 'm_s5_log_dt', 'm_s5_b_re', 'm_s5_b_im', 'm_s5_c_re', 'm_s5_c_im', 'm_s5_d', 'm_w_glu', 'm_b_glu', 'm_hg_lb_logits', 'm_hg_norm_gain', 'm_w_pa', 'm_w_pb', 'm_w_out', 'm_g_ffn', 'm_w_up', 'm_w_conv', 'm_b_conv', 'm_w_down', 'm_g_final', 'v_g_mix', 'v_w_in', 'v_s5_a_re', 'v_s5_a_im', 'v_s5_log_dt', 'v_s5_b_re', 'v_s5_b_im', 'v_s5_c_re', 'v_s5_c_im', 'v_s5_d', 'v_w_glu', 'v_b_glu', 'v_hg_lb_logits', 'v_hg_norm_gain', 'v_w_pa', 'v_w_pb', 'v_w_out', 'v_g_ffn', 'v_w_up', 'v_w_conv', 'v_b_conv', 'v_w_down', 'v_g_final']
TWIN_OUTPUTS = ['loss', 'grad_x', 'grad_g_mix', 'grad_w_in', 'grad_s5_a_re', 'grad_s5_a_im', 'grad_s5_log_dt', 'grad_s5_b_re', 'grad_s5_b_im', 'grad_s5_c_re', 'grad_s5_c_im', 'grad_s5_d', 'grad_w_glu', 'grad_b_glu', 'grad_hg_lb_logits', 'grad_hg_norm_gain', 'grad_w_pa', 'grad_w_pb', 'grad_w_out', 'grad_g_ffn', 'grad_w_up', 'grad_w_conv', 'grad_b_conv', 'grad_w_down', 'grad_g_final', 'delta_g_mix', 'delta_w_in', 'delta_s5_a_re', 'delta_s5_a_im', 'delta_s5_log_dt', 'delta_s5_b_re', 'delta_s5_b_im', 'delta_s5_c_re', 'delta_s5_c_im', 'delta_s5_d', 'delta_w_glu', 'delta_b_glu', 'delta_hg_lb_logits', 'delta_hg_norm_gain', 'delta_w_pa', 'delta_w_pb', 'delta_w_out', 'delta_g_ffn', 'delta_w_up', 'delta_w_conv', 'delta_b_conv', 'delta_w_down', 'delta_g_final', 'new_m_g_mix', 'new_m_w_in', 'new_m_s5_a_re', 'new_m_s5_a_im', 'new_m_s5_log_dt', 'new_m_s5_b_re', 'new_m_s5_b_im', 'new_m_s5_c_re', 'new_m_s5_c_im', 'new_m_s5_d', 'new_m_w_glu', 'new_m_b_glu', 'new_m_hg_lb_logits', 'new_m_hg_norm_gain', 'new_m_w_pa', 'new_m_w_pb', 'new_m_w_out', 'new_m_g_ffn', 'new_m_w_up', 'new_m_w_conv', 'new_m_b_conv', 'new_m_w_down', 'new_m_g_final', 'new_v_g_mix', 'new_v_w_in', 'new_v_s5_a_re', 'new_v_s5_a_im', 'new_v_s5_log_dt', 'new_v_s5_b_re', 'new_v_s5_b_im', 'new_v_s5_c_re', 'new_v_s5_c_im', 'new_v_s5_d', 'new_v_w_glu', 'new_v_b_glu', 'new_v_hg_lb_logits', 'new_v_hg_norm_gain', 'new_v_w_pa', 'new_v_w_pb', 'new_v_w_out', 'new_v_g_ffn', 'new_v_w_up', 'new_v_w_conv', 'new_v_b_conv', 'new_v_w_down', 'new_v_g_final']
TWIN_LEAF_KINDS = {'loss': 'loss', 'grad_x': 'grad_x', 'grad_g_mix': 'grad_w', 'grad_w_in': 'grad_w', 'grad_s5_a_re': 'grad_w', 'grad_s5_a_im': 'grad_w', 'grad_s5_log_dt': 'grad_w', 'grad_s5_b_re': 'grad_w', 'grad_s5_b_im': 'grad_w', 'grad_s5_c_re': 'grad_w', 'grad_s5_c_im': 'grad_w', 'grad_s5_d': 'grad_w', 'grad_w_glu': 'grad_w', 'grad_b_glu': 'grad_w', 'grad_hg_lb_logits': 'grad_w', 'grad_hg_norm_gain': 'grad_w', 'grad_w_pa': 'grad_w', 'grad_w_pb': 'grad_w', 'grad_w_out': 'grad_w', 'grad_g_ffn': 'grad_w', 'grad_w_up': 'grad_w', 'grad_w_conv': 'grad_w', 'grad_b_conv': 'grad_w', 'grad_w_down': 'grad_w', 'grad_g_final': 'grad_w', 'delta_g_mix': 'delta_w', 'delta_w_in': 'delta_w', 'delta_s5_a_re': 'delta_w', 'delta_s5_a_im': 'delta_w', 'delta_s5_log_dt': 'delta_w', 'delta_s5_b_re': 'delta_w', 'delta_s5_b_im': 'delta_w', 'delta_s5_c_re': 'delta_w', 'delta_s5_c_im': 'delta_w', 'delta_s5_d': 'delta_w', 'delta_w_glu': 'delta_w', 'delta_b_glu': 'delta_w', 'delta_hg_lb_logits': 'delta_w', 'delta_hg_norm_gain': 'delta_w', 'delta_w_pa': 'delta_w', 'delta_w_pb': 'delta_w', 'delta_w_out': 'delta_w', 'delta_g_ffn': 'delta_w', 'delta_w_up': 'delta_w', 'delta_w_conv': 'delta_w', 'delta_b_conv': 'delta_w', 'delta_w_down': 'delta_w', 'delta_g_final': 'delta_w', 'new_m_g_mix': 'new_m', 'new_m_w_in': 'new_m', 'new_m_s5_a_re': 'new_m', 'new_m_s5_a_im': 'new_m', 'new_m_s5_log_dt': 'new_m', 'new_m_s5_b_re': 'new_m', 'new_m_s5_b_im': 'new_m', 'new_m_s5_c_re': 'new_m', 'new_m_s5_c_im': 'new_m', 'new_m_s5_d': 'new_m', 'new_m_w_glu': 'new_m', 'new_m_b_glu': 'new_m', 'new_m_hg_lb_logits': 'new_m', 'new_m_hg_norm_gain': 'new_m', 'new_m_w_pa': 'new_m', 'new_m_w_pb': 'new_m', 'new_m_w_out': 'new_m', 'new_m_g_ffn': 'new_m', 'new_m_w_up': 'new_m', 'new_m_w_conv': 'new_m', 'new_m_b_conv': 'new_m', 'new_m_w_down': 'new_m', 'new_m_g_final': 'new_m', 'new_v_g_mix': 'new_v', 'new_v_w_in': 'new_v', 'new_v_s5_a_re': 'new_v', 'new_v_s5_a_im': 'new_v', 'new_v_s5_log_dt': 'new_v', 'new_v_s5_b_re': 'new_v', 'new_v_s5_b_im': 'new_v', 'new_v_s5_c_re': 'new_v', 'new_v_s5_c_im': 'new_v', 'new_v_s5_d': 'new_v', 'new_v_w_glu': 'new_v', 'new_v_b_glu': 'new_v', 'new_v_hg_lb_logits': 'new_v', 'new_v_hg_norm_gain': 'new_v', 'new_v_w_pa': 'new_v', 'new_v_w_pb': 'new_v', 'new_v_w_out': 'new_v', 'new_v_g_ffn': 'new_v', 'new_v_w_up': 'new_v', 'new_v_w_conv': 'new_v', 'new_v_b_conv': 'new_v', 'new_v_w_down': 'new_v', 'new_v_g_final': 'new_v'}


def _forward(args):
    return _fwd_reference(*[args[k] for k in FWD_PARAMS])


def _output_shape():
    out = _jax.eval_shape(lambda: _forward(_fwd_setup_inputs(0)))
    return out.shape, out.dtype

N_MICROBATCH = 1
ADAM_LR = 0.001
ADAM_B1 = 0.9
ADAM_B2 = 0.999
ADAM_EPS = 1e-08
ADAM_WD = 0.01
ADAM_STEP = 10
PER_EXAMPLE_BATCH_AXIS = {'x': 0, 'loss_target': 0}
SHARED_INPUTS = []
_WEIGHT_DTYPES = {'g_mix': _jnp.float32, 'w_in': _jnp.float32, 's5_a_re': _jnp.float32, 's5_a_im': _jnp.float32, 's5_log_dt': _jnp.float32, 's5_b_re': _jnp.float32, 's5_b_im': _jnp.float32, 's5_c_re': _jnp.float32, 's5_c_im': _jnp.float32, 's5_d': _jnp.float32, 'w_glu': _jnp.float32, 'b_glu': _jnp.float32, 'hg_lb_logits': _jnp.float32, 'hg_norm_gain': _jnp.float32, 'w_pa': _jnp.float32, 'w_pb': _jnp.float32, 'w_out': _jnp.float32, 'g_ffn': _jnp.float32, 'w_up': _jnp.float32, 'w_conv': _jnp.float32, 'b_conv': _jnp.float32, 'w_down': _jnp.float32, 'g_final': _jnp.float32}
MOMENT_SCALE = {'g_mix': 9.659296e-02, 'w_in': 4.678006e-02, 's5_a_re': 2.413377e-03, 's5_a_im': 2.853812e-03, 's5_log_dt': 1.272032e+00, 's5_b_re': 1.869921e-03, 's5_b_im': 1.803229e-03, 's5_c_re': 3.552808e-03, 's5_c_im': 3.697594e-03, 's5_d': 5.681358e-02, 'w_glu': 1.566587e-02, 'b_glu': 2.211446e-02, 'hg_lb_logits': 8.766811e-03, 'hg_norm_gain': 8.458009e-02, 'w_pa': 3.652053e-02, 'w_pb': 6.003515e-02, 'w_out': 7.013830e-02, 'g_ffn': 1.313171e-01, 'w_up': 5.353316e-02, 'w_conv': 5.484242e-02, 'b_conv': 5.320261e-02, 'w_down': 8.734193e-02, 'g_final': 3.201198e+01}


def _to_microbatches(a, axis):
    t = _jnp.moveaxis(a, axis, 0)
    t = t.reshape((N_MICROBATCH, t.shape[0] // N_MICROBATCH) + t.shape[1:])
    return _jnp.moveaxis(t, 1, axis + 1)


def setup_inputs(seed: int = 0) -> dict:
    inp = _fwd_setup_inputs(seed)
    key = _jax.random.fold_in(_jax.random.key(seed), 7919)
    shape, _ = _output_shape()
    out = dict(inp)
    out["loss_target"] = _jax.random.normal(_jax.random.fold_in(key, 0), shape, _jnp.float32)
    for i, name in enumerate(TWIN_WEIGHTS):
        w = inp[name].astype(_jnp.float32)
        if MOMENT_SCALE is None:
            s = _jnp.sqrt(_jnp.mean(_jnp.square(w)) + 1e-30)
        else:
            s = MOMENT_SCALE[name]
        km, kv = _jax.random.split(_jax.random.fold_in(key, i + 1))
        out[name] = w
        out["m_" + name] = s * _jax.random.normal(km, w.shape, _jnp.float32)
        out["v_" + name] = (s * s) * _jax.random.uniform(kv, w.shape, _jnp.float32, 0.5, 1.5)
    if N_MICROBATCH > 1:
        for name, axis in PER_EXAMPLE_BATCH_AXIS.items():
            out[name] = _to_microbatches(out[name], axis)
    return {'x': out['x'], 'g_mix': out['g_mix'], 'w_in': out['w_in'], 's5_a_re': out['s5_a_re'], 's5_a_im': out['s5_a_im'], 's5_log_dt': out['s5_log_dt'], 's5_b_re': out['s5_b_re'], 's5_b_im': out['s5_b_im'], 's5_c_re': out['s5_c_re'], 's5_c_im': out['s5_c_im'], 's5_d': out['s5_d'], 'w_glu': out['w_glu'], 'b_glu': out['b_glu'], 'hg_lb_logits': out['hg_lb_logits'], 'hg_norm_gain': out['hg_norm_gain'], 'w_pa': out['w_pa'], 'w_pb': out['w_pb'], 'w_out': out['w_out'], 'g_ffn': out['g_ffn'], 'w_up': out['w_up'], 'w_conv': out['w_conv'], 'b_conv': out['b_conv'], 'w_down': out['w_down'], 'g_final': out['g_final'], 'loss_target': out['loss_target'], 'm_g_mix': out['m_g_mix'], 'm_w_in': out['m_w_in'], 'm_s5_a_re': out['m_s5_a_re'], 'm_s5_a_im': out['m_s5_a_im'], 'm_s5_log_dt': out['m_s5_log_dt'], 'm_s5_b_re': out['m_s5_b_re'], 'm_s5_b_im': out['m_s5_b_im'], 'm_s5_c_re': out['m_s5_c_re'], 'm_s5_c_im': out['m_s5_c_im'], 'm_s5_d': out['m_s5_d'], 'm_w_glu': out['m_w_glu'], 'm_b_glu': out['m_b_glu'], 'm_hg_lb_logits': out['m_hg_lb_logits'], 'm_hg_norm_gain': out['m_hg_norm_gain'], 'm_w_pa': out['m_w_pa'], 'm_w_pb': out['m_w_pb'], 'm_w_out': out['m_w_out'], 'm_g_ffn': out['m_g_ffn'], 'm_w_up': out['m_w_up'], 'm_w_conv': out['m_w_conv'], 'm_b_conv': out['m_b_conv'], 'm_w_down': out['m_w_down'], 'm_g_final': out['m_g_final'], 'v_g_mix': out['v_g_mix'], 'v_w_in': out['v_w_in'], 'v_s5_a_re': out['v_s5_a_re'], 'v_s5_a_im': out['v_s5_a_im'], 'v_s5_log_dt': out['v_s5_log_dt'], 'v_s5_b_re': out['v_s5_b_re'], 'v_s5_b_im': out['v_s5_b_im'], 'v_s5_c_re': out['v_s5_c_re'], 'v_s5_c_im': out['v_s5_c_im'], 'v_s5_d': out['v_s5_d'], 'v_w_glu': out['v_w_glu'], 'v_b_glu': out['v_b_glu'], 'v_hg_lb_logits': out['v_hg_lb_logits'], 'v_hg_norm_gain': out['v_hg_norm_gain'], 'v_w_pa': out['v_w_pa'], 'v_w_pb': out['v_w_pb'], 'v_w_out': out['v_w_out'], 'v_g_ffn': out['v_g_ffn'], 'v_w_up': out['v_w_up'], 'v_w_conv': out['v_w_conv'], 'v_b_conv': out['v_b_conv'], 'v_w_down': out['v_w_down'], 'v_g_final': out['v_g_final']}


def _loss(weights, diff, rest, loss_target):
    with _jax.named_scope("forward"):
        args = {**rest, TWIN_DIFF_INPUT: diff, **{k: w.astype(_WEIGHT_DTYPES[k]) for k, w in weights.items()}}
        y = _forward(args)
    with _jax.named_scope("loss_head"):
        err = _jnp.square(y.astype(_jnp.float32) - loss_target)
        return 0.5 * _jnp.sum(_jnp.mean(err, axis=-1)) if err.ndim else 0.5 * err


def _adamw(w, g, m, v):
    m = ADAM_B1 * m + (1.0 - ADAM_B1) * g
    v = ADAM_B2 * v + (1.0 - ADAM_B2) * _jnp.square(g)
    m_hat = m / (1.0 - ADAM_B1 ** ADAM_STEP)
    v_hat = v / (1.0 - ADAM_B2 ** ADAM_STEP)
    delta = -ADAM_LR * (m_hat / (_jnp.sqrt(v_hat) + ADAM_EPS) + ADAM_WD * w)
    return delta, m, v


def reference(x, g_mix, w_in, s5_a_re, s5_a_im, s5_log_dt, s5_b_re, s5_b_im, s5_c_re, s5_c_im, s5_d, w_glu, b_glu, hg_lb_logits, hg_norm_gain, w_pa, w_pb, w_out, g_ffn, w_up, w_conv, b_conv, w_down, g_final, loss_target, m_g_mix, m_w_in, m_s5_a_re, m_s5_a_im, m_s5_log_dt, m_s5_b_re, m_s5_b_im, m_s5_c_re, m_s5_c_im, m_s5_d, m_w_glu, m_b_glu, m_hg_lb_logits, m_hg_norm_gain, m_w_pa, m_w_pb, m_w_out, m_g_ffn, m_w_up, m_w_conv, m_b_conv, m_w_down, m_g_final, v_g_mix, v_w_in, v_s5_a_re, v_s5_a_im, v_s5_log_dt, v_s5_b_re, v_s5_b_im, v_s5_c_re, v_s5_c_im, v_s5_d, v_w_glu, v_b_glu, v_hg_lb_logits, v_hg_norm_gain, v_w_pa, v_w_pb, v_w_out, v_g_ffn, v_w_up, v_w_conv, v_b_conv, v_w_down, v_g_final):
    given = dict(x=x, g_mix=g_mix, w_in=w_in, s5_a_re=s5_a_re, s5_a_im=s5_a_im, s5_log_dt=s5_log_dt, s5_b_re=s5_b_re, s5_b_im=s5_b_im, s5_c_re=s5_c_re, s5_c_im=s5_c_im, s5_d=s5_d, w_glu=w_glu, b_glu=b_glu, hg_lb_logits=hg_lb_logits, hg_norm_gain=hg_norm_gain, w_pa=w_pa, w_pb=w_pb, w_out=w_out, g_ffn=g_ffn, w_up=w_up, w_conv=w_conv, b_conv=b_conv, w_down=w_down, g_final=g_final, loss_target=loss_target, m_g_mix=m_g_mix, m_w_in=m_w_in, m_s5_a_re=m_s5_a_re, m_s5_a_im=m_s5_a_im, m_s5_log_dt=m_s5_log_dt, m_s5_b_re=m_s5_b_re, m_s5_b_im=m_s5_b_im, m_s5_c_re=m_s5_c_re, m_s5_c_im=m_s5_c_im, m_s5_d=m_s5_d, m_w_glu=m_w_glu, m_b_glu=m_b_glu, m_hg_lb_logits=m_hg_lb_logits, m_hg_norm_gain=m_hg_norm_gain, m_w_pa=m_w_pa, m_w_pb=m_w_pb, m_w_out=m_w_out, m_g_ffn=m_g_ffn, m_w_up=m_w_up, m_w_conv=m_w_conv, m_b_conv=m_b_conv, m_w_down=m_w_down, m_g_final=m_g_final, v_g_mix=v_g_mix, v_w_in=v_w_in, v_s5_a_re=v_s5_a_re, v_s5_a_im=v_s5_a_im, v_s5_log_dt=v_s5_log_dt, v_s5_b_re=v_s5_b_re, v_s5_b_im=v_s5_b_im, v_s5_c_re=v_s5_c_re, v_s5_c_im=v_s5_c_im, v_s5_d=v_s5_d, v_w_glu=v_w_glu, v_b_glu=v_b_glu, v_hg_lb_logits=v_hg_lb_logits, v_hg_norm_gain=v_hg_norm_gain, v_w_pa=v_w_pa, v_w_pb=v_w_pb, v_w_out=v_w_out, v_g_ffn=v_g_ffn, v_w_up=v_w_up, v_w_conv=v_w_conv, v_b_conv=v_b_conv, v_w_down=v_w_down, v_g_final=v_g_final)
    weights = {n: given[n] for n in TWIN_WEIGHTS}
    shared = {n: given[n] for n in SHARED_INPUTS}
    per_example = {n: given[n] for n in ['x']}
    grad_fn = _jax.value_and_grad(_loss, argnums=(0, 1))

    def one_microbatch(ex, loss_target):
        ex = dict(ex)
        diff = ex.pop(TWIN_DIFF_INPUT)
        return grad_fn(weights, diff, {**shared, **ex}, loss_target)

    if N_MICROBATCH == 1:
        loss, (grad_w, grad_x) = one_microbatch(per_example, given["loss_target"])
    else:
        def body(carry, xs):
            loss_sum, grad_sum = carry
            l_k, (gw_k, gx_k) = one_microbatch(xs[0], xs[1])
            with _jax.named_scope("update"):
                return (loss_sum + l_k, _jax.tree.map(_jnp.add, grad_sum, gw_k)), gx_k

        init = (_jnp.zeros((), _jnp.float32), _jax.tree.map(_jnp.zeros_like, weights))
        (loss, grad_w), grad_x = _jax.lax.scan(body, init, (per_example, given["loss_target"]))
    with _jax.named_scope("update"):
        delta_w, new_m, new_v = {}, {}, {}
        for n in TWIN_WEIGHTS:
            delta_w[n], new_m[n], new_v[n] = _adamw(weights[n], grad_w[n], given["m_" + n], given["v_" + n])
    return (loss, grad_x, *[grad_w[n] for n in TWIN_WEIGHTS], *[delta_w[n] for n in TWIN_WEIGHTS],
            *[new_m[n] for n in TWIN_WEIGHTS], *[new_v[n] for n in TWIN_WEIGHTS])
```

```python
import functools
import math

import jax
import jax.numpy as jnp
from jax import lax
from jax.experimental import pallas as pl
from jax.experimental.pallas import tpu as pltpu

F32 = jnp.float32
BF16 = jnp.bfloat16

D_MODEL = 1024
S5_WIDTH = 512
S5_GROUP = 16
S5_GROUPS = 32
S5_STATE = 64
S5_N = S5_GROUPS * S5_STATE
HG_WIDTH = 512
HG_HEAD = 128
HG_HEADS = 4
D_FF = 2816
CONV_W = 3
CHUNK = 64
N_IN = S5_WIDTH + 4 * HG_WIDTH + 2 * D_MODEL
EPS = 1e-6
QSCALE = HG_HEAD ** -0.5

ADAM_LR = 0.001
ADAM_B1 = 0.9
ADAM_B2 = 0.999
ADAM_EPS = 1e-08
ADAM_WD = 0.01
ADAM_STEP = 10

N_DEV = 8
V7X_VMEM_BYTES = 64 * 1024 * 1024
VMEM_LIMIT = V7X_VMEM_BYTES * 7 // 8
SUBLANES = 8
PACK_W = 1024

SMALL = (
    ("g_mix", (1, D_MODEL)),
    ("s5_a_re", (1, S5_GROUPS, S5_STATE)),
    ("s5_a_im", (1, S5_GROUPS, S5_STATE)),
    ("s5_log_dt", (1, S5_GROUPS)),
    ("s5_b_re", (1, S5_GROUPS, S5_STATE, S5_GROUP)),
    ("s5_b_im", (1, S5_GROUPS, S5_STATE, S5_GROUP)),
    ("s5_c_re", (1, S5_GROUPS, S5_GROUP, S5_STATE)),
    ("s5_c_im", (1, S5_GROUPS, S5_GROUP, S5_STATE)),
    ("s5_d", (1, S5_WIDTH)),
    ("b_glu", (1, S5_WIDTH)),
    ("hg_lb_logits", (2, HG_WIDTH)),
    ("hg_norm_gain", (1, HG_WIDTH)),
    ("g_ffn", (1, D_MODEL)),
    ("b_conv", (1, 2 * D_FF)),
    ("g_final", (D_MODEL,)),
)
SMALL_ROWS = 144
WEIGHT_ORDER = ("g_mix", "w_in", "s5_a_re", "s5_a_im", "s5_log_dt", "s5_b_re", "s5_b_im", "s5_c_re", "s5_c_im",
                "s5_d", "w_glu", "b_glu", "hg_lb_logits", "hg_norm_gain", "w_pa", "w_pb", "w_out", "g_ffn",
                "w_up", "w_conv", "b_conv", "w_down", "g_final")


def _pcall(body, name, grid, in_specs, out_specs, out_shape, scratch=()):
    return pl.pallas_call(
        body, name=name, grid=grid, in_specs=in_specs, out_specs=out_specs, out_shape=out_shape,
        scratch_shapes=list(scratch),
        compiler_params=pltpu.CompilerParams(dimension_semantics=("arbitrary",) * len(grid),
                                             vmem_limit_bytes=VMEM_LIMIT),
    )


def _full(shape):
    return pl.BlockSpec(shape, lambda *_: (0,) * len(shape))


def _sds(shape, dtype=F32):
    return jax.ShapeDtypeStruct(shape, dtype)


def _dot(a, b):
    return jnp.dot(a.astype(BF16), b.astype(BF16), preferred_element_type=F32)


def _dot_nt(a, b):
    return lax.dot_general(a.astype(BF16), b.astype(BF16), (((1,), (1,)), ((), ())), preferred_element_type=F32)


def _dot_tn(a, b):
    return lax.dot_general(a.astype(BF16), b.astype(BF16), (((0,), (0,)), ((), ())), preferred_element_type=F32)


def _hdot(a, b):
    return jnp.dot(a, b, preferred_element_type=F32, precision=lax.Precision.HIGHEST)


def _hdot_tn(a, b):
    return lax.dot_general(a, b, (((0,), (0,)), ((), ())), preferred_element_type=F32,
                           precision=lax.Precision.HIGHEST)


def _sigmoid(x):
    return jax.nn.sigmoid(x)


GELU_C = math.sqrt(2.0 / math.pi)
GELU_A = 0.044715


def _gelu(x):
    return 0.5 * x * (1.0 + jnp.tanh(GELU_C * (x + GELU_A * (x * x * x))))


def _gelu_grad(x):
    t = jnp.tanh(GELU_C * (x + GELU_A * (x * x * x)))
    return 0.5 * (1.0 + t) + 0.5 * x * (1.0 - t * t) * (GELU_C * (1.0 + 3.0 * GELU_A * x * x))


def _cumsum_rows(v, reverse=False):
    n = v.shape[0]
    row = lax.broadcasted_iota(jnp.int32, v.shape, 0)
    s = 1
    while s < n:
        if reverse:
            v = v + jnp.where(row < n - s, pltpu.roll(v, n - s, axis=0), 0.0)
        else:
            v = v + jnp.where(row >= s, pltpu.roll(v, s, axis=0), 0.0)
        s *= 2
    return v


def _token_tile(seq):
    return min(256, seq)


def _s5_disc(a_re, a_im, ldt, b_re, b_im):
    dt = jnp.exp(ldt)
    mag = jnp.exp(a_re * dt)
    ang = a_im * dt
    lb_re = mag * jnp.cos(ang)
    lb_im = mag * jnp.sin(ang)
    den = a_re * a_re + a_im * a_im
    n_re = lb_re - 1.0
    n_im = lb_im
    co_re = (n_re * a_re + n_im * a_im) / den
    co_im = (n_im * a_re - n_re * a_im) / den
    bb_re = co_re * b_re - co_im * b_im
    bb_im = co_re * b_im + co_im * b_re
    return lb_re, lb_im, bb_re, bb_im


def _params_fwd(a_re, a_im, ldt, b_re, b_im, logits):
    def body(are, aim, ld, bre, bim, lg, lr_o, li_o, bbr_o, bbi_o, lb_o):
        lr, li, bbr, bbi = _s5_disc(are[...], aim[...], ld[...], bre[...], bim[...])
        lr_o[...] = lr
        li_o[...] = li
        bbr_o[...] = bbr
        bbi_o[...] = bbi
        lb_o[...] = _sigmoid(lg[0:1, :] - lg[1:2, :])

    col, mat = (S5_N, 1), (S5_N, S5_GROUP)
    return _pcall(body, "params_fwd", (1,),
                  [_full(col), _full(col), _full(col), _full(mat), _full(mat), _full((2, HG_WIDTH))],
                  [_full(col), _full(col), _full(mat), _full(mat), _full((1, HG_WIDTH))],
                  [_sds(col), _sds(col), _sds(mat), _sds(mat), _sds((1, HG_WIDTH))])(a_re, a_im, ldt, b_re, b_im, logits)


def _params_bwd(a_re, a_im, ldt, b_re, b_im, logits, dlr, dli, dbbr, dbbi, dlb):
    def body(are, aim, ld, bre, bim, lg, dlr_r, dli_r, dbbr_r, dbbi_r, dlb_r,
             dare_o, daim_o, dld_o, dbre_o, dbim_o, dlg_o):
        _, vjp = jax.vjp(_s5_disc, are[...], aim[...], ld[...], bre[...], bim[...])
        dare, daim, dld, dbre, dbim = vjp((dlr_r[...], dli_r[...], dbbr_r[...], dbbi_r[...]))
        dare_o[...] = dare
        daim_o[...] = daim
        dbre_o[...] = dbre
        dbim_o[...] = dbim
        for g in range(S5_GROUPS):
            dld_o[g:g + 1, :] = jnp.sum(dld[g * S5_STATE:(g + 1) * S5_STATE, :], axis=0, keepdims=True)
        lb = _sigmoid(lg[0:1, :] - lg[1:2, :])
        d0 = dlb_r[...] * lb * (1.0 - lb)
        dlg_o[0:1, :] = d0
        dlg_o[1:2, :] = -d0

    col, mat = (S5_N, 1), (S5_N, S5_GROUP)
    return _pcall(body, "params_bwd", (1,),
                  [_full(col), _full(col), _full(col), _full(mat), _full(mat), _full((2, HG_WIDTH)),
                   _full(col), _full(col), _full(mat), _full(mat), _full((1, HG_WIDTH))],
                  [_full(col), _full(col), _full((S5_GROUPS, 1)), _full(mat), _full(mat), _full((2, HG_WIDTH))],
                  [_sds(col), _sds(col), _sds((S5_GROUPS, 1)), _sds(mat), _sds(mat), _sds((2, HG_WIDTH))],
                  )(a_re, a_im, ldt, b_re, b_im, logits, dlr, dli, dbbr, dbbi, dlb)


def _blockdiag(m):
    g, r, c = m.shape
    eye = jnp.eye(g, dtype=m.dtype)
    return (m[:, :, None, :] * eye[:, None, :, None]).reshape(g * r, g * c)


def _diag_blocks(full, r, c):
    g = full.shape[0] // r
    on_diag = jnp.arange(g)[:, None, None, None] == jnp.arange(g)[None, None, :, None]
    return jnp.sum(jnp.where(on_diag, full.reshape(g, r, g, c), 0.0), axis=2)


def _in_proj(x, g_mix, w_in, tm):
    t = x.shape[0]

    def body(x_ref, g_ref, w_ref, u_ref, za_ref, zh_ref, zg_ref):
        xv = x_ref[...]
        r = lax.rsqrt(jnp.mean(xv * xv, axis=-1, keepdims=True) + EPS)
        u = (xv * r * g_ref[...]).astype(BF16)
        u_ref[...] = u
        za_ref[...] = jnp.dot(u, w_ref[:, 0:S5_WIDTH], preferred_element_type=F32)
        zh_ref[...] = jnp.dot(u, w_ref[:, S5_WIDTH:S5_WIDTH + 4 * HG_WIDTH], preferred_element_type=F32)
        zg_ref[...] = jnp.dot(u, w_ref[:, S5_WIDTH + 4 * HG_WIDTH:], preferred_element_type=F32)

    row = lambda w: pl.BlockSpec((tm, w), lambda i: (i, 0))
    return _pcall(body, "in_proj", (t // tm,),
                  [row(D_MODEL), _full((1, D_MODEL)), _full((D_MODEL, N_IN))],
                  [row(D_MODEL), row(S5_WIDTH), row(4 * HG_WIDTH), row(2 * D_MODEL)],
                  [_sds((t, D_MODEL), BF16), _sds((t, S5_WIDTH)), _sds((t, 4 * HG_WIDTH)), _sds((t, 2 * D_MODEL))],
                  )(x, g_mix, w_in)


S5_LANES = 512


def _s5_fwd(za, bbt, lam, ct, dskip, nb, seq, ts):
    t = za.shape[0]
    nts = seq // ts

    def body(za_ref, bbt_ref, lam_ref, ct_ref, d_ref, xs_ref, y_ref, st_ref):
        @pl.when(pl.program_id(1) == 0)
        def _():
            st_ref[...] = jnp.zeros_like(st_ref)

        zav = za_ref[...]
        xs_ref[...] = _dot(zav, bbt_ref[...])
        for cc in range(S5_N // S5_LANES):
            re = slice(cc * S5_LANES, (cc + 1) * S5_LANES)
            im = slice(S5_N + cc * S5_LANES, S5_N + (cc + 1) * S5_LANES)
            lr = lam_ref[0:1, re]
            li = lam_ref[1:2, re]

            def step(i, carry):
                xr, xi = carry
                br = xs_ref[pl.ds(i, 1), re]
                bi = xs_ref[pl.ds(i, 1), im]
                nxr = lr * xr - li * xi + br
                nxi = lr * xi + li * xr + bi
                xs_ref[pl.ds(i, 1), re] = nxr
                xs_ref[pl.ds(i, 1), im] = nxi
                return nxr, nxi

            xr, xi = lax.fori_loop(0, ts, step, (st_ref[0:1, re], st_ref[1:2, re]), unroll=8)
            st_ref[0:1, re] = xr
            st_ref[1:2, re] = xi
        y_ref[...] = _dot(xs_ref[...], ct_ref[...]) + d_ref[...] * zav

    tok = lambda w: pl.BlockSpec((ts, w), lambda b, j: (b * nts + j, 0))
    return _pcall(body, "s5_fwd", (nb, nts),
                  [tok(S5_WIDTH), _full((S5_WIDTH, 2 * S5_N)), _full((2, S5_N)), _full((2 * S5_N, S5_WIDTH)),
                   _full((1, S5_WIDTH))],
                  [tok(2 * S5_N), tok(S5_WIDTH)],
                  [_sds((t, 2 * S5_N)), _sds((t, S5_WIDTH))],
                  scratch=[pltpu.VMEM((2, S5_N), F32)])(za, bbt, lam, ct, dskip)


def _hgrn_gates(zq, zf, lbh):
    sf = _sigmoid(zf)
    f = lbh + (1.0 - lbh) * sf
    sq = _sigmoid(zq)
    qa = zq * sq * QSCALE
    bc = _cumsum_rows(jnp.log(f))
    bm = bc[CHUNK // 2 - 1:CHUNK // 2, :]
    bl = bc[CHUNK - 1:CHUNK, :]
    return sf, f, sq, qa, bc, bm, bl


def _hgrn_fwd(zh, lb, nb, seq):
    nc = seq // CHUNK

    def body(zh_ref, lb_ref, o_ref, sts_ref, st_ref):
        @pl.when(pl.program_id(0) == 0)
        def _():
            st_ref[...] = jnp.zeros_like(st_ref)

        causal = (lax.broadcasted_iota(jnp.int32, (CHUNK, CHUNK), 0)
                  >= lax.broadcasted_iota(jnp.int32, (CHUNK, CHUNK), 1))
        for b in range(nb):
            for h in range(HG_HEADS):
                hs = slice(h * HG_HEAD, (h + 1) * HG_HEAD)
                zq = zh_ref[b, :, h * HG_HEAD:(h + 1) * HG_HEAD]
                zf = zh_ref[b, :, HG_WIDTH + h * HG_HEAD:HG_WIDTH + (h + 1) * HG_HEAD]
                zi = zh_ref[b, :, 2 * HG_WIDTH + h * HG_HEAD:2 * HG_WIDTH + (h + 1) * HG_HEAD]
                _, f, _, qa, bc, bm, bl = _hgrn_gates(zq, zf, lb_ref[:, hs])
                k = 1.0 - f
                qt = qa * jnp.exp(bc - bm)
                kt = k * jnp.exp(bm - bc)
                qb = qa * jnp.exp(bc)
                kd = k * jnp.exp(bl - bc)
                st = st_ref[b, h]
                sts_ref[b, 0, h] = st
                a = jnp.where(causal, _dot_nt(qt, kt), 0.0)
                o_ref[b, :, hs] = _dot(a, zi) + _dot_nt(qb, st)
                st_ref[b, h] = st * jnp.exp(bl) + _dot_tn(zi, kd)

    return _pcall(body, "hgrn_fwd", (nc,),
                  [pl.BlockSpec((nb, CHUNK, 4 * HG_WIDTH), lambda c: (0, c, 0)), _full((1, HG_WIDTH))],
                  [pl.BlockSpec((nb, CHUNK, HG_WIDTH), lambda c: (0, c, 0)),
                   pl.BlockSpec((nb, 1, HG_HEADS, HG_HEAD, HG_HEAD), lambda c: (0, c, 0, 0, 0))],
                  [_sds((nb, seq, HG_WIDTH)), _sds((nb, nc, HG_HEADS, HG_HEAD, HG_HEAD))],
                  scratch=[pltpu.VMEM((nb, HG_HEADS, HG_HEAD, HG_HEAD), F32)])(zh, lb)


def _head_rms(o):
    parts = []
    for h in range(HG_HEADS):
        oh = o[:, h * HG_HEAD:(h + 1) * HG_HEAD]
        r = lax.rsqrt(jnp.mean(oh * oh, axis=-1, keepdims=True) + EPS)
        parts.append(jnp.broadcast_to(r, oh.shape))
    return jnp.concatenate(parts, axis=1)


def _head_mean(v):
    parts = []
    for h in range(HG_HEADS):
        vh = v[:, h * HG_HEAD:(h + 1) * HG_HEAD]
        parts.append(jnp.broadcast_to(jnp.mean(vh, axis=-1, keepdims=True), vh.shape))
    return jnp.concatenate(parts, axis=1)


def _mix_fwd(x, y0, o, zh, zgt, w_glu, b_glu, gain, w_pa, w_pb, w_out, g_ffn, tm):
    t = x.shape[0]

    def body(x_ref, y0_ref, o_ref, zg_ref, zgt_ref, wglu_ref, bglu_ref, gain_ref, wpa_ref, wpb_ref, wout_ref,
             gffn_ref, x1_ref, u2_ref, pa_ref, pb_ref, ya2_ref, yb_ref):
        ya1 = _gelu(y0_ref[...])
        s = _sigmoid(_dot(ya1, wglu_ref[...]) + bglu_ref[...])
        ya2 = (ya1 * s).astype(BF16)
        ov = o_ref[...]
        zg = zg_ref[...]
        yb = (ov * _head_rms(ov) * gain_ref[...] * (zg * _sigmoid(zg))).astype(BF16)
        ya2_ref[...] = ya2
        yb_ref[...] = yb
        pa = jnp.dot(ya2, wpa_ref[...], preferred_element_type=F32)
        pb = jnp.dot(yb, wpb_ref[...], preferred_element_type=F32)
        pa_ref[...] = pa
        pb_ref[...] = pb
        m = _sigmoid(zgt_ref[:, 0:D_MODEL]) * pa + _sigmoid(zgt_ref[:, D_MODEL:]) * pb
        x1 = x_ref[...] + _dot(m, wout_ref[...])
        x1_ref[...] = x1
        r = lax.rsqrt(jnp.mean(x1 * x1, axis=-1, keepdims=True) + EPS)
        u2_ref[...] = (x1 * r * gffn_ref[...]).astype(BF16)

    row = lambda w: pl.BlockSpec((tm, w), lambda i: (i, 0))
    return _pcall(body, "mix_fwd", (t // tm,),
                  [row(D_MODEL), row(S5_WIDTH), row(HG_WIDTH), pl.BlockSpec((tm, HG_WIDTH), lambda i: (i, 3)),
                   row(2 * D_MODEL), _full((S5_WIDTH, S5_WIDTH)), _full((1, S5_WIDTH)), _full((1, HG_WIDTH)),
                   _full((S5_WIDTH, D_MODEL)), _full((HG_WIDTH, D_MODEL)), _full((D_MODEL, D_MODEL)),
                   _full((1, D_MODEL))],
                  [row(D_MODEL), row(D_MODEL), row(D_MODEL), row(D_MODEL), row(S5_WIDTH), row(HG_WIDTH)],
                  [_sds((t, D_MODEL)), _sds((t, D_MODEL), BF16), _sds((t, D_MODEL)), _sds((t, D_MODEL)),
                   _sds((t, S5_WIDTH), BF16), _sds((t, HG_WIDTH), BF16)],
                  )(x, y0, o, zh, zgt, w_glu, b_glu, gain, w_pa, w_pb, w_out, g_ffn)


FF_COLS = 256
FF_UP_TILE = 1408


def _ffn_up(u2, w_up, tm):
    t = u2.shape[0]
    n = 2 * D_FF

    def body(u_ref, w_ref, h_ref):
        h_ref[...] = jnp.dot(u_ref[...], w_ref[...], preferred_element_type=F32)

    return _pcall(body, "ffn_up", (n // FF_UP_TILE, t // tm),
                  [pl.BlockSpec((tm, D_MODEL), lambda j, i: (i, 0)),
                   pl.BlockSpec((D_MODEL, FF_UP_TILE), lambda j, i: (0, j))],
                  pl.BlockSpec((tm, FF_UP_TILE), lambda j, i: (i, j)),
                  _sds((t, n)))(u2, w_up)


def _conv_cols(h_ref, halo_ref, valid, wc_ref, bc_ref, c0, tm):
    cs = slice(c0, c0 + FF_COLS)
    cur = h_ref[:, cs]
    prev = jnp.where(valid, halo_ref[:, cs], 0.0)
    full = jnp.concatenate([prev, cur], axis=0)
    h1 = pltpu.roll(full, 1, axis=0)[SUBLANES:]
    h2 = pltpu.roll(full, 2, axis=0)[SUBLANES:]
    hc = h2 * wc_ref[0:1, cs] + h1 * wc_ref[1:2, cs] + cur * wc_ref[2:3, cs] + bc_ref[:, cs]
    return hc, cur, h1, h2


def _ffn_down_loss(h, x1, tgt, w_conv, b_conv, w_down, g_final, seq, tm):
    t = h.shape[0]
    tps = seq // tm
    n = 2 * D_FF

    def body(h_ref, halo_ref, x1_ref, tgt_ref, wc_ref, bc_ref, wd_ref, gf_ref,
             a_ref, dx2_ref, loss_ref, dgf_ref):
        i = pl.program_id(0)

        @pl.when(i == 0)
        def _():
            loss_ref[...] = jnp.zeros_like(loss_ref)
            dgf_ref[...] = jnp.zeros_like(dgf_ref)

        valid = (i % tps) != 0
        x2 = x1_ref[...]
        for j in range(D_FF // FF_COLS):
            gate = _conv_cols(h_ref, halo_ref, valid, wc_ref, bc_ref, j * FF_COLS, tm)[0]
            val = _conv_cols(h_ref, halo_ref, valid, wc_ref, bc_ref, D_FF + j * FF_COLS, tm)[0]
            a = (gate * _sigmoid(gate) * val).astype(BF16)
            a_ref[:, j * FF_COLS:(j + 1) * FF_COLS] = a
            x2 = x2 + jnp.dot(a, wd_ref[j * FF_COLS:(j + 1) * FF_COLS, :], preferred_element_type=F32)
        r = lax.rsqrt(jnp.mean(x2 * x2, axis=-1, keepdims=True) + EPS)
        xn = x2 * r
        g = gf_ref[...]
        e = xn * g - tgt_ref[...]
        loss_ref[...] += (0.5 / D_MODEL) * jnp.sum(e * e).reshape(1, 1)
        dy = e * (1.0 / D_MODEL)
        dgf_ref[...] += jnp.sum(dy * xn, axis=0, keepdims=True)
        dxn = dy * g
        dx2_ref[...] = r * (dxn - xn * jnp.mean(dxn * xn, axis=-1, keepdims=True))

    row = lambda w: pl.BlockSpec((tm, w), lambda i: (i, 0))
    halo = pl.BlockSpec((SUBLANES, n), lambda i: (jnp.maximum(i * (tm // SUBLANES) - 1, 0), 0))
    return _pcall(body, "ffn_down_loss", (t // tm,),
                  [row(n), halo, row(D_MODEL), row(D_MODEL), _full((CONV_W, n)), _full((1, n)),
                   _full((D_FF, D_MODEL)), _full((1, D_MODEL))],
                  [row(D_FF), row(D_MODEL), _full((1, 1)), _full((1, D_MODEL))],
                  [_sds((t, D_FF), BF16), _sds((t, D_MODEL)), _sds((1, 1)), _sds((1, D_MODEL))],
                  )(h, h, x1, tgt, w_conv, b_conv, w_down, g_final)


def _matmul_tn(a, b, name, tm, tk):
    t, m = a.shape
    n = b.shape[1]

    def body(a_ref, b_ref, o_ref):
        @pl.when(pl.program_id(1) == 0)
        def _():
            o_ref[...] = jnp.zeros_like(o_ref)

        o_ref[...] += _dot_tn(a_ref[...], b_ref[...])

    return _pcall(body, name, (m // tm, t // tk),
                  [pl.BlockSpec((tk, tm), lambda i, k: (k, i)), pl.BlockSpec((tk, n), lambda i, k: (k, 0))],
                  pl.BlockSpec((tm, n), lambda i, k: (i, 0)), _sds((m, n)))(a, b)


def _ffn_bwd_act(dx2, h, w_conv, b_conv, w_down, seq, tm):
    t = h.shape[0]
    tps = seq // tm
    n = 2 * D_FF

    def body(dx2_ref, h_ref, halo_ref, wc_ref, bc_ref, wd_ref, dhc_ref, dwc_ref, dbc_ref):
        i = pl.program_id(0)

        @pl.when(i == 0)
        def _():
            dwc_ref[...] = jnp.zeros_like(dwc_ref)
            dbc_ref[...] = jnp.zeros_like(dbc_ref)

        valid = (i % tps) != 0
        dx2 = dx2_ref[...].astype(BF16)

        def conv_grads(c0, dhc, taps):
            cs = slice(c0, c0 + FF_COLS)
            dhc_ref[:, cs] = dhc
            dbc_ref[:, cs] += jnp.sum(dhc, axis=0, keepdims=True)
            for w, tap in enumerate(taps):
                dwc_ref[w:w + 1, cs] += jnp.sum(tap * dhc, axis=0, keepdims=True)

        for j in range(D_FF // FF_COLS):
            gate, g0, g1, g2 = _conv_cols(h_ref, halo_ref, valid, wc_ref, bc_ref, j * FF_COLS, tm)
            val, v0, v1, v2 = _conv_cols(h_ref, halo_ref, valid, wc_ref, bc_ref, D_FF + j * FF_COLS, tm)
            da = _dot_nt(dx2, wd_ref[j * FF_COLS:(j + 1) * FF_COLS, :])
            sg = _sigmoid(gate)
            conv_grads(j * FF_COLS, da * val * (sg * (1.0 + gate * (1.0 - sg))), (g2, g1, g0))
            conv_grads(D_FF + j * FF_COLS, da * (gate * sg), (v2, v1, v0))

    row = lambda w: pl.BlockSpec((tm, w), lambda i: (i, 0))
    halo = pl.BlockSpec((SUBLANES, n), lambda i: (jnp.maximum(i * (tm // SUBLANES) - 1, 0), 0))
    return _pcall(body, "ffn_bwd_act", (t // tm,),
                  [row(D_MODEL), row(n), halo, _full((CONV_W, n)), _full((1, n)), _full((D_FF, D_MODEL))],
                  [row(n), _full((CONV_W, n)), _full((1, n))],
                  [_sds((t, n)), _sds((CONV_W, n)), _sds((1, n))],
                  )(dx2, h, h, w_conv, b_conv, w_down)


def _ffn_bwd_up(dhc, dx2, x1, w_conv, w_up, g_ffn, seq, tm):
    t = dhc.shape[0]
    tps = seq // tm
    n = 2 * D_FF
    last = t // SUBLANES - 1

    def body(dhc_ref, halo_ref, dx2_ref, x1_ref, wc_ref, wu_ref, gf_ref, dh_ref, dx1_ref, dgf_ref):
        i = pl.program_id(0)

        @pl.when(i == 0)
        def _():
            dgf_ref[...] = jnp.zeros_like(dgf_ref)

        valid = ((i + 1) % tps) != 0
        du2 = jnp.zeros((tm, D_MODEL), F32)
        for j in range(n // FF_COLS):
            cs = slice(j * FF_COLS, (j + 1) * FF_COLS)
            cur = dhc_ref[:, cs]
            nxt = jnp.where(valid, halo_ref[:, cs], 0.0)
            full = jnp.concatenate([cur, nxt], axis=0)
            d1 = pltpu.roll(full, tm + SUBLANES - 1, axis=0)[:tm]
            d2 = pltpu.roll(full, tm + SUBLANES - 2, axis=0)[:tm]
            dh = (cur * wc_ref[2:3, cs] + d1 * wc_ref[1:2, cs] + d2 * wc_ref[0:1, cs]).astype(BF16)
            dh_ref[:, cs] = dh
            du2 = du2 + _dot_nt(dh, wu_ref[:, cs])
        x1 = x1_ref[...]
        r = lax.rsqrt(jnp.mean(x1 * x1, axis=-1, keepdims=True) + EPS)
        xn = x1 * r
        dgf_ref[...] += jnp.sum(du2 * xn, axis=0, keepdims=True)
        dxn = du2 * gf_ref[...]
        dx1_ref[...] = dx2_ref[...] + r * (dxn - xn * jnp.mean(dxn * xn, axis=-1, keepdims=True))

    row = lambda w: pl.BlockSpec((tm, w), lambda i: (i, 0))
    halo = pl.BlockSpec((SUBLANES, n), lambda i: (jnp.minimum((i + 1) * (tm // SUBLANES), last), 0))
    return _pcall(body, "ffn_bwd_up", (t // tm,),
                  [row(n), halo, row(D_MODEL), row(D_MODEL), _full((CONV_W, n)), _full((D_MODEL, n)),
                   _full((1, D_MODEL))],
                  [row(n), row(D_MODEL), _full((1, D_MODEL))],
                  [_sds((t, n), BF16), _sds((t, D_MODEL)), _sds((1, D_MODEL))],
                  )(dhc, dhc, dx2, x1, w_conv, w_up, g_ffn)


def _mix_bwd(dx1, y0, o, zh, zgt, pa, pb, w_glu, b_glu, gain, w_pa, w_pb, w_out, tm):
    t = dx1.shape[0]

    def body(dx1_ref, y0_ref, o_ref, zg_ref, zgt_ref, pa_ref, pb_ref, wglu_ref, bglu_ref, gain_ref, wpa_ref,
             wpb_ref, wout_ref,
             dy0_ref, do_ref, dzg_ref, dzgt_ref, m_ref, dpa_ref, dpb_ref, ya1_ref, dpre_ref, dbglu_ref, dgain_ref):
        @pl.when(pl.program_id(0) == 0)
        def _():
            dbglu_ref[...] = jnp.zeros_like(dbglu_ref)
            dgain_ref[...] = jnp.zeros_like(dgain_ref)

        dm = _dot_nt(dx1_ref[...], wout_ref[...])
        sga = _sigmoid(zgt_ref[:, 0:D_MODEL])
        sgb = _sigmoid(zgt_ref[:, D_MODEL:])
        pa = pa_ref[...]
        pb = pb_ref[...]
        m_ref[...] = (sga * pa + sgb * pb).astype(BF16)
        dzgt_ref[:, 0:D_MODEL] = dm * pa * sga * (1.0 - sga)
        dzgt_ref[:, D_MODEL:] = dm * pb * sgb * (1.0 - sgb)
        dpa = (dm * sga).astype(BF16)
        dpb = (dm * sgb).astype(BF16)
        dpa_ref[...] = dpa
        dpb_ref[...] = dpb
        dya2 = _dot_nt(dpa, wpa_ref[...])
        dyb = _dot_nt(dpb, wpb_ref[...])
        y0 = y0_ref[...]
        ya1 = _gelu(y0)
        ya1_ref[...] = ya1.astype(BF16)
        s = _sigmoid(_dot(ya1, wglu_ref[...]) + bglu_ref[...])
        dpre = dya2 * ya1 * s * (1.0 - s)
        dpre_ref[...] = dpre.astype(BF16)
        dbglu_ref[...] += jnp.sum(dpre, axis=0, keepdims=True)
        dya1 = dya2 * s + _dot_nt(dpre, wglu_ref[...])
        dy0_ref[...] = dya1 * _gelu_grad(y0)
        ov = o_ref[...]
        zg = zg_ref[...]
        oh = ov * _head_rms(ov)
        on = oh * gain_ref[...]
        sz = _sigmoid(zg)
        dzg_ref[...] = dyb * on * (sz * (1.0 + zg * (1.0 - sz)))
        don = dyb * (zg * sz)
        dgain_ref[...] += jnp.sum(don * oh, axis=0, keepdims=True)
        doh = don * gain_ref[...]
        do_ref[...] = _head_rms(ov) * (doh - oh * _head_mean(doh * oh))

    row = lambda w: pl.BlockSpec((tm, w), lambda i: (i, 0))
    return _pcall(body, "mix_bwd", (t // tm,),
                  [row(D_MODEL), row(S5_WIDTH), row(HG_WIDTH), pl.BlockSpec((tm, HG_WIDTH), lambda i: (i, 3)),
                   row(2 * D_MODEL), row(D_MODEL), row(D_MODEL), _full((S5_WIDTH, S5_WIDTH)), _full((1, S5_WIDTH)),
                   _full((1, HG_WIDTH)), _full((S5_WIDTH, D_MODEL)), _full((HG_WIDTH, D_MODEL)),
                   _full((D_MODEL, D_MODEL))],
                  [row(S5_WIDTH), row(HG_WIDTH), row(HG_WIDTH), row(2 * D_MODEL), row(D_MODEL), row(D_MODEL),
                   row(D_MODEL), row(S5_WIDTH), row(S5_WIDTH), _full((1, S5_WIDTH)), _full((1, HG_WIDTH))],
                  [_sds((t, S5_WIDTH)), _sds((t, HG_WIDTH)), _sds((t, HG_WIDTH)), _sds((t, 2 * D_MODEL)),
                   _sds((t, D_MODEL), BF16), _sds((t, D_MODEL), BF16), _sds((t, D_MODEL), BF16),
                   _sds((t, S5_WIDTH), BF16), _sds((t, S5_WIDTH), BF16), _sds((1, S5_WIDTH)), _sds((1, HG_WIDTH))],
                  )(dx1, y0, o, zh, zgt, pa, pb, w_glu, b_glu, gain, w_pa, w_pb, w_out)


def _s5_bwd(dy0, za, xs, cg, bbdt, lam, dskip, nb, seq, ts):
    t = za.shape[0]
    nts = seq // ts

    def body(dy0_ref, za_ref, xs_ref, halo_ref, cg_ref, bbdt_ref, lam_ref, d_ref,
             dza_ref, a_ref, dlam_ref, dd_ref, acc_ref, st_ref):
        b, j = pl.program_id(0), pl.program_id(1)

        @pl.when((b == 0) & (j == 0))
        def _():
            dlam_ref[...] = jnp.zeros_like(dlam_ref)
            dd_ref[...] = jnp.zeros_like(dd_ref)

        @pl.when(j == 0)
        def _():
            st_ref[...] = jnp.zeros_like(st_ref)

        dy0 = dy0_ref[...]
        acc_ref[...] = _dot(dy0, cg_ref[...])
        for cc in range(S5_N // S5_LANES):
            re = slice(cc * S5_LANES, (cc + 1) * S5_LANES)
            im = slice(S5_N + cc * S5_LANES, S5_N + (cc + 1) * S5_LANES)
            lr = lam_ref[0:1, re]
            li = lam_ref[1:2, re]

            def step(i, carry):
                ar, ai = carry
                row = ts - 1 - i
                nar = acc_ref[pl.ds(row, 1), re] + lr * ar + li * ai
                nai = acc_ref[pl.ds(row, 1), im] - li * ar + lr * ai
                acc_ref[pl.ds(row, 1), re] = nar
                acc_ref[pl.ds(row, 1), im] = nai
                return nar, nai

            ar, ai = lax.fori_loop(0, ts, step, (st_ref[0:1, re], st_ref[1:2, re]), unroll=8)
            st_ref[0:1, re] = ar
            st_ref[1:2, re] = ai
        av = acc_ref[...]
        a_ref[...] = av.astype(BF16)
        first = jnp.where(j == nts - 1, 0.0, halo_ref[SUBLANES - 1:SUBLANES, :])
        rows = lax.broadcasted_iota(jnp.int32, (ts, 2 * S5_N), 0)
        xp = jnp.where(rows == 0, first, pltpu.roll(xs_ref[...], 1, axis=0))
        ar, ai = av[:, :S5_N], av[:, S5_N:]
        xr, xi = xp[:, :S5_N], xp[:, S5_N:]
        dlam_ref[0:1, :] += jnp.sum(ar * xr + ai * xi, axis=0, keepdims=True)
        dlam_ref[1:2, :] += jnp.sum(ai * xr - ar * xi, axis=0, keepdims=True)
        zav = za_ref[...]
        dza_ref[...] = _dot(av, bbdt_ref[...]) + d_ref[...] * dy0
        dd_ref[...] += jnp.sum(dy0 * zav, axis=0, keepdims=True)

    tile = lambda b, j: b * nts + (nts - 1 - j)
    tok = lambda w: pl.BlockSpec((ts, w), lambda b, j: (tile(b, j), 0))
    halo = pl.BlockSpec((SUBLANES, 2 * S5_N),
                        lambda b, j: (jnp.maximum(tile(b, j) * (ts // SUBLANES) - 1, 0), 0))
    return _pcall(body, "s5_bwd", (nb, nts),
                  [tok(S5_WIDTH), tok(S5_WIDTH), tok(2 * S5_N), halo, _full((S5_WIDTH, 2 * S5_N)),
                   _full((2 * S5_N, S5_WIDTH)), _full((2, S5_N)), _full((1, S5_WIDTH))],
                  [tok(S5_WIDTH), tok(2 * S5_N), _full((2, S5_N)), _full((1, S5_WIDTH))],
                  [_sds((t, S5_WIDTH)), _sds((t, 2 * S5_N), BF16), _sds((2, S5_N)), _sds((1, S5_WIDTH))],
                  scratch=[pltpu.VMEM((ts, 2 * S5_N), F32), pltpu.VMEM((2, S5_N), F32)],
                  )(dy0, za, xs, xs, cg, bbdt, lam, dskip)


def _hgrn_bwd(zh, do, sts, lb, nb, seq):
    nc = seq // CHUNK

    def body(zh_ref, do_ref, sts_ref, lb_ref, dz_ref, dlb_ref, dst_ref):
        @pl.when(pl.program_id(0) == 0)
        def _():
            dst_ref[...] = jnp.zeros_like(dst_ref)
            dlb_ref[...] = jnp.zeros_like(dlb_ref)

        row = lax.broadcasted_iota(jnp.int32, (CHUNK, CHUNK), 0)
        causal = row >= lax.broadcasted_iota(jnp.int32, (CHUNK, CHUNK), 1)
        last_row = lax.broadcasted_iota(jnp.int32, (CHUNK, HG_HEAD), 0) == CHUNK - 1
        for b in range(nb):
            for h in range(HG_HEADS):
                hs = slice(h * HG_HEAD, (h + 1) * HG_HEAD)
                zq = zh_ref[b, :, h * HG_HEAD:(h + 1) * HG_HEAD]
                zf = zh_ref[b, :, HG_WIDTH + h * HG_HEAD:HG_WIDTH + (h + 1) * HG_HEAD]
                zi = zh_ref[b, :, 2 * HG_WIDTH + h * HG_HEAD:2 * HG_WIDTH + (h + 1) * HG_HEAD]
                lbh = lb_ref[:, hs]
                sf, f, sq, qa, bc, bm, bl = _hgrn_gates(zq, zf, lbh)
                k = 1.0 - f
                e_qt = jnp.exp(bc - bm)
                e_kt = jnp.exp(bm - bc)
                e_b = jnp.exp(bc)
                e_kd = jnp.exp(bl - bc)
                e_l = jnp.exp(bl)
                qt, kt, qb, kd = qa * e_qt, k * e_kt, qa * e_b, k * e_kd
                a = jnp.where(causal, _dot_nt(qt, kt), 0.0)
                st = sts_ref[b, 0, h]
                dst = dst_ref[b, h]
                dov = do_ref[b, :, hs]
                da = jnp.where(causal, _dot_nt(dov, zi), 0.0)
                dqt = _hdot(da, kt)
                dkt = _hdot_tn(da, qt)
                dqb = _hdot(dov, st)
                di = _dot_tn(a, dov) + _dot_nt(kd, dst)
                dkd = _hdot(zi, dst)
                de_l = jnp.sum(dst * st, axis=0, keepdims=True)
                dst_ref[b, h] = dst * e_l + _dot_tn(dov, qb)
                dqa = dqt * e_qt + dqb * e_b
                dk = dkt * e_kt + dkd * e_kd
                dbl = jnp.sum(dkd * kd, axis=0, keepdims=True) + de_l * e_l
                db = dqt * qt - dkt * kt + dqb * qb - dkd * kd + jnp.where(last_row, dbl, 0.0)
                df = _cumsum_rows(db, reverse=True) / f - dk
                dz_ref[b, :, h * HG_HEAD:(h + 1) * HG_HEAD] = dqa * QSCALE * (sq * (1.0 + zq * (1.0 - sq)))
                dz_ref[b, :, HG_WIDTH + h * HG_HEAD:HG_WIDTH + (h + 1) * HG_HEAD] = df * (1.0 - lbh) * sf * (1.0 - sf)
                dz_ref[b, :, 2 * HG_WIDTH + h * HG_HEAD:2 * HG_WIDTH + (h + 1) * HG_HEAD] = di
                dlb_ref[:, hs] += jnp.sum(df * (1.0 - sf), axis=0, keepdims=True)

    rev = lambda c: nc - 1 - c
    return _pcall(body, "hgrn_bwd", (nc,),
                  [pl.BlockSpec((nb, CHUNK, 4 * HG_WIDTH), lambda c: (0, rev(c), 0)),
                   pl.BlockSpec((nb, CHUNK, HG_WIDTH), lambda c: (0, rev(c), 0)),
                   pl.BlockSpec((nb, 1, HG_HEADS, HG_HEAD, HG_HEAD), lambda c: (0, rev(c), 0, 0, 0)),
                   _full((1, HG_WIDTH))],
                  [pl.BlockSpec((nb, CHUNK, 3 * HG_WIDTH), lambda c: (0, rev(c), 0)), _full((1, HG_WIDTH))],
                  [_sds((nb, seq, 3 * HG_WIDTH)), _sds((1, HG_WIDTH))],
                  scratch=[pltpu.VMEM((nb, HG_HEADS, HG_HEAD, HG_HEAD), F32)])(zh, do, sts, lb)


def _in_proj_bwd(dza, dzh, dzg, dzgt, dx1, x, g_mix, w_in, tm):
    t = x.shape[0]

    def body(dza_ref, dzh_ref, dzg_ref, dzgt_ref, dx1_ref, x_ref, g_ref, w_ref, dz_ref, dx_ref, dg_ref):
        @pl.when(pl.program_id(0) == 0)
        def _():
            dg_ref[...] = jnp.zeros_like(dg_ref)

        c1, c2, c3 = S5_WIDTH, S5_WIDTH + 3 * HG_WIDTH, S5_WIDTH + 4 * HG_WIDTH
        dz_ref[:, 0:c1] = dza_ref[...].astype(BF16)
        dz_ref[:, c1:c2] = dzh_ref[...].astype(BF16)
        dz_ref[:, c2:c3] = dzg_ref[...].astype(BF16)
        dz_ref[:, c3:] = dzgt_ref[...].astype(BF16)
        du = _dot_nt(dz_ref[...], w_ref[...])
        xv = x_ref[...]
        r = lax.rsqrt(jnp.mean(xv * xv, axis=-1, keepdims=True) + EPS)
        xn = xv * r
        dg_ref[...] += jnp.sum(du * xn, axis=0, keepdims=True)
        dxn = du * g_ref[...]
        dx_ref[...] = dx1_ref[...] + r * (dxn - xn * jnp.mean(dxn * xn, axis=-1, keepdims=True))

    row = lambda w: pl.BlockSpec((tm, w), lambda i: (i, 0))
    return _pcall(body, "in_proj_bwd", (t // tm,),
                  [row(S5_WIDTH), row(3 * HG_WIDTH), row(HG_WIDTH), row(2 * D_MODEL), row(D_MODEL), row(D_MODEL),
                   _full((1, D_MODEL)), _full((D_MODEL, N_IN))],
                  [row(N_IN), row(D_MODEL), _full((1, D_MODEL))],
                  [_sds((t, N_IN), BF16), _sds((t, D_MODEL)), _sds((1, D_MODEL))],
                  )(dza, dzh, dzg, dzgt, dx1, x, g_mix, w_in)


def _local_step(x3, tgt3, wts, sp):
    nb, seq, _ = x3.shape
    t = nb * seq
    tm = _token_tile(seq)
    x = x3.reshape(t, D_MODEL)
    tgt = tgt3.reshape(t, D_MODEL)
    row = lambda v: v.reshape(1, -1)

    a_re = sp["s5_a_re"].reshape(S5_N, 1)
    a_im = sp["s5_a_im"].reshape(S5_N, 1)
    ldt = jnp.repeat(sp["s5_log_dt"].reshape(S5_GROUPS), S5_STATE).reshape(S5_N, 1)
    b_re = sp["s5_b_re"].reshape(S5_N, S5_GROUP)
    b_im = sp["s5_b_im"].reshape(S5_N, S5_GROUP)
    lr, li, bb_re, bb_im, lb = _params_fwd(a_re, a_im, ldt, b_re, b_im, sp["hg_lb_logits"])
    lam = jnp.concatenate([lr.reshape(1, S5_N), li.reshape(1, S5_N)], axis=0)
    gps = lambda m: m.reshape(S5_GROUPS, S5_STATE, S5_GROUP)
    bbt = jnp.concatenate([_blockdiag(gps(bb_re).transpose(0, 2, 1)), _blockdiag(gps(bb_im).transpose(0, 2, 1))],
                          axis=1).astype(BF16)
    ct = jnp.concatenate([_blockdiag(sp["s5_c_re"].transpose(0, 2, 1)), -_blockdiag(sp["s5_c_im"].transpose(0, 2, 1))],
                         axis=0).astype(BF16)

    g_mix, g_ffn, g_final = row(sp["g_mix"]), row(sp["g_ffn"]), row(sp["g_final"])
    b_glu, gain, dskip, b_conv = row(sp["b_glu"]), row(sp["hg_norm_gain"]), row(sp["s5_d"]), row(sp["b_conv"])

    u, za, zh, zgt = _in_proj(x, g_mix, wts["w_in"], tm)
    xs, y0 = _s5_fwd(za, bbt, lam, ct, dskip, nb, seq, tm)
    o3, sts = _hgrn_fwd(zh.reshape(nb, seq, 4 * HG_WIDTH), lb, nb, seq)
    o = o3.reshape(t, HG_WIDTH)
    x1, u2, pa, pb, ya2, yb = _mix_fwd(x, y0, o, zh, zgt, wts["w_glu"], b_glu, gain, wts["w_pa"], wts["w_pb"],
                                       wts["w_out"], g_ffn, tm)
    h = _ffn_up(u2, wts["w_up"], tm)
    a, dx2, loss, dg_final = _ffn_down_loss(h, x1, tgt, wts["w_conv"], b_conv, wts["w_down"], g_final, seq, tm)

    tk = min(512, t)
    dhc, dw_conv, db_conv = _ffn_bwd_act(dx2, h, wts["w_conv"], b_conv, wts["w_down"], seq, tm)
    dw_down = _matmul_tn(a, dx2, "dw_down", 256, tk)
    dh, dx1, dg_ffn = _ffn_bwd_up(dhc, dx2, x1, wts["w_conv"], wts["w_up"], g_ffn, seq, tm)
    dw_up = _matmul_tn(u2, dh, "dw_up", 256, tk)
    (dy0, do, dzg, dzgt, m, dpa, dpb, ya1, dpre, db_glu, dgain) = _mix_bwd(
        dx1, y0, o, zh, zgt, pa, pb, wts["w_glu"], b_glu, gain, wts["w_pa"], wts["w_pb"], wts["w_out"], tm)
    dw_out = _matmul_tn(m, dx1, "dw_out", 256, tk)
    dw_pa = _matmul_tn(ya2, dpa, "dw_pa", 256, tk)
    dw_pb = _matmul_tn(yb, dpb, "dw_pb", 256, tk)
    dw_glu = _matmul_tn(ya1, dpre, "dw_glu", 256, tk)
    dza, a_s5, dlam, dd = _s5_bwd(dy0, za, xs, ct.T, bbt.T, lam, dskip, nb, seq, tm)
    dbb_full = _matmul_tn(a_s5, za, "dbb_s5", 512, tk)
    dc_full = _matmul_tn(xs, dy0, "dc_s5", 512, tk)
    dzh3, dlb = _hgrn_bwd(zh.reshape(nb, seq, 4 * HG_WIDTH), do.reshape(nb, seq, HG_WIDTH), sts, lb, nb, seq)
    dz, dx, dg_mix = _in_proj_bwd(dza, dzh3.reshape(t, 3 * HG_WIDTH), dzg, dzgt, dx1, x, g_mix, wts["w_in"], tm)
    dw_in = _matmul_tn(u, dz, "dw_in", 256, tk)

    dbb_re = _diag_blocks(dbb_full[:S5_N], S5_STATE, S5_GROUP).reshape(S5_N, S5_GROUP)
    dbb_im = _diag_blocks(dbb_full[S5_N:], S5_STATE, S5_GROUP).reshape(S5_N, S5_GROUP)
    dc_re = _diag_blocks(dc_full[:S5_N], S5_STATE, S5_GROUP).transpose(0, 2, 1)
    dc_im = -_diag_blocks(dc_full[S5_N:], S5_STATE, S5_GROUP).transpose(0, 2, 1)
    da_re, da_im, dldt, db_re, db_im, dlogits = _params_bwd(
        a_re, a_im, ldt, b_re, b_im, sp["hg_lb_logits"],
        dlam[0].reshape(S5_N, 1), dlam[1].reshape(S5_N, 1), dbb_re, dbb_im, dlb)

    big = {"w_in": dw_in, "w_glu": dw_glu, "w_pa": dw_pa, "w_pb": dw_pb, "w_out": dw_out, "w_up": dw_up,
           "w_down": dw_down, "w_conv": dw_conv}
    small = {"g_mix": dg_mix, "s5_a_re": da_re, "s5_a_im": da_im, "s5_log_dt": dldt, "s5_b_re": db_re,
             "s5_b_im": db_im, "s5_c_re": dc_re, "s5_c_im": dc_im, "s5_d": dd, "b_glu": db_glu,
             "hg_lb_logits": dlogits, "hg_norm_gain": dgain, "g_ffn": dg_ffn, "b_conv": db_conv,
             "g_final": dg_final}
    return dx.reshape(nb, seq, D_MODEL), big, small, loss


def _exchange(name, operands):
    n = len(operands)
    out_shape = []
    for arr, per_peer in operands:
        blk = arr.shape[1:] if per_peer else arr.shape
        out_shape.append(_sds((N_DEV,) + tuple(blk), arr.dtype))

    def body(*refs):
        srcs, dsts = refs[:n], refs[n:2 * n]
        send_sems, recv_sems, local_sems = refs[2 * n:]
        x, y, c = lax.axis_index("x"), lax.axis_index("y"), lax.axis_index("c")
        me = 4 * x + 2 * y + c

        def flip(v, bit):
            return 1 - v if bit else v

        def source(i, slot):
            return srcs[i].at[slot] if operands[i][1] else srcs[i]

        local = [pltpu.make_async_copy(source(i, me), dsts[i].at[me], local_sems.at[i]) for i in range(n)]
        for cp in local:
            cp.start()
        remote, arrivals = [], []
        for k in range(1, N_DEV):
            peer = (flip(x, k & 4), flip(y, k & 2), flip(c, k & 1))
            slot = 4 * peer[0] + 2 * peer[1] + peer[2]
            for i in range(n):
                s = (k - 1) * n + i
                remote.append(pltpu.make_async_remote_copy(
                    src_ref=source(i, slot), dst_ref=dsts[i].at[me], send_sem=send_sems.at[s],
                    recv_sem=recv_sems.at[s], device_id=peer, device_id_type=pl.DeviceIdType.MESH))
                arrivals.append(pltpu.make_async_remote_copy(
                    src_ref=source(i, slot), dst_ref=dsts[i].at[slot], send_sem=send_sems.at[s],
                    recv_sem=recv_sems.at[s], device_id=peer, device_id_type=pl.DeviceIdType.MESH))
        for cp in remote:
            cp.start()
        for cp in arrivals:
            cp.wait_recv()
        for cp in remote:
            cp.wait_send()
        for cp in local:
            cp.wait()

    hbm = pl.BlockSpec(memory_space=pl.ANY)
    return pl.pallas_call(
        body, name=name, in_specs=[hbm] * n, out_specs=[hbm] * n, out_shape=out_shape,
        scratch_shapes=[pltpu.SemaphoreType.DMA(((N_DEV - 1) * n,)), pltpu.SemaphoreType.DMA(((N_DEV - 1) * n,)),
                        pltpu.SemaphoreType.DMA((n,))],
    )(*[arr for arr, _ in operands])


def _join_cols(parts, name, tr):
    _, r, c = parts.shape

    def body(p_ref, o_ref):
        for j in range(N_DEV):
            o_ref[:, j * c:(j + 1) * c] = p_ref[j]

    return _pcall(body, name, (r // tr,), [pl.BlockSpec((N_DEV, tr, c), lambda i: (0, i, 0))],
                  pl.BlockSpec((tr, N_DEV * c), lambda i: (i, 0)), _sds((r, N_DEV * c), parts.dtype))(parts)


def _split_cols(full, name, tr):
    r, c = full.shape[0], full.shape[1] // N_DEV

    def body(f_ref, o_ref):
        for j in range(N_DEV):
            o_ref[j] = f_ref[:, j * c:(j + 1) * c]

    return _pcall(body, name, (r // tr,), [pl.BlockSpec((tr, N_DEV * c), lambda i: (i, 0))],
                  pl.BlockSpec((N_DEV, tr, c), lambda i: (0, i, 0)), _sds((N_DEV, r, c), full.dtype))(full)


def _adamw(parts, w, m, v, name, tile):
    rows, cols = w.shape

    def body(p_ref, w_ref, m_ref, v_ref, g_out, d_out, m_out, v_out):
        g = p_ref[0]
        for k in range(1, N_DEV):
            g = g + p_ref[k]
        m1 = ADAM_B1 * m_ref[...] + (1.0 - ADAM_B1) * g
        v1 = ADAM_B2 * v_ref[...] + (1.0 - ADAM_B2) * (g * g)
        m_hat = m1 / (1.0 - ADAM_B1 ** ADAM_STEP)
        v_hat = v1 / (1.0 - ADAM_B2 ** ADAM_STEP)
        g_out[...] = g
        d_out[...] = -ADAM_LR * (m_hat / (jnp.sqrt(v_hat) + ADAM_EPS) + ADAM_WD * w_ref[...])
        m_out[...] = m1
        v_out[...] = v1

    row = pl.BlockSpec((tile, cols), lambda i: (i, 0))
    return _pcall(body, name, (rows // tile,),
                  [pl.BlockSpec((N_DEV, tile, cols), lambda i: (0, i, 0)), row, row, row],
                  [row, row, row, row], [_sds((rows, cols))] * 4)(parts, w, m, v)


BIG = {
    "w_in": ((D_MODEL, N_IN // N_DEV), True, 256),
    "w_glu": ((S5_WIDTH // N_DEV, S5_WIDTH), False, S5_WIDTH // N_DEV),
    "w_pa": ((S5_WIDTH, D_MODEL // N_DEV), True, S5_WIDTH),
    "w_pb": ((HG_WIDTH, D_MODEL // N_DEV), True, HG_WIDTH),
    "w_out": ((D_MODEL // N_DEV, D_MODEL), False, D_MODEL // N_DEV),
    "w_up": ((D_MODEL, 2 * D_FF // N_DEV), True, 256),
    "w_conv": ((CONV_W, 2 * D_FF // N_DEV), True, CONV_W),
    "w_down": ((D_FF // N_DEV, D_MODEL), False, D_FF // N_DEV // 2),
}
UNALIGNED_COLS = ("w_in", "w_up", "w_conv")


def _join_shards(n, parts):
    (a, b), by_cols, _ = BIG[n]
    if not by_cols:
        return parts.reshape(N_DEV * a, b)
    if n in UNALIGNED_COLS:
        return _join_cols(parts, "join_" + n, min(a, 256))
    return parts.transpose(1, 0, 2).reshape(a, N_DEV * b)


def _split_shards(n, full):
    (a, b), by_cols, _ = BIG[n]
    if not by_cols:
        return full.reshape(N_DEV, a, b)
    if n in UNALIGNED_COLS:
        return _split_cols(full, "split_" + n, min(a, 256))
    return full.reshape(a, N_DEV, b).transpose(1, 0, 2)


def _pack_small(d, extra):
    flat = jnp.concatenate([d[n].reshape(-1) for n, _ in SMALL] + [extra.reshape(-1)])
    return jnp.pad(flat, (0, SMALL_ROWS * PACK_W - flat.shape[0])).reshape(SMALL_ROWS, PACK_W)


def _unpack_small(p):
    flat = p.reshape(-1)
    out, off = {}, 0
    for n, shp in SMALL:
        size = math.prod(shp)
        out[n] = flat[off:off + size].reshape(shp)
        off += size
    return out, flat[off]


def kernel(x, g_mix, w_in, s5_a_re, s5_a_im, s5_log_dt, s5_b_re, s5_b_im, s5_c_re, s5_c_im, s5_d, w_glu, b_glu, hg_lb_logits, hg_norm_gain, w_pa, w_pb, w_out, g_ffn, w_up, w_conv, b_conv, w_down, g_final, loss_target, m_g_mix, m_w_in, m_s5_a_re, m_s5_a_im, m_s5_log_dt, m_s5_b_re, m_s5_b_im, m_s5_c_re, m_s5_c_im, m_s5_d, m_w_glu, m_b_glu, m_hg_lb_logits, m_hg_norm_gain, m_w_pa, m_w_pb, m_w_out, m_g_ffn, m_w_up, m_w_conv, m_b_conv, m_w_down, m_g_final, v_g_mix, v_w_in, v_s5_a_re, v_s5_a_im, v_s5_log_dt, v_s5_b_re, v_s5_b_im, v_s5_c_re, v_s5_c_im, v_s5_d, v_w_glu, v_b_glu, v_hg_lb_logits, v_hg_norm_gain, v_w_pa, v_w_pb, v_w_out, v_g_ffn, v_w_up, v_w_conv, v_b_conv, v_w_down, v_g_final):
    given = dict(locals())
    big_names = list(BIG)
    small_names = [n for n, _ in SMALL]

    pay = [given[n][0] if n == "w_conv" else given[n][0].astype(BF16) for n in big_names]
    got = _exchange("gather_weights", [(p, False) for p in pay])
    wts = {n: _join_shards(n, g) for n, g in zip(big_names, got)}

    sp = {n: (given[n] if n in ("g_final", "hg_lb_logits") else given[n][0]) for n in small_names}
    dx, big, small, loss = _local_step(x, loss_target, wts, sp)

    send = [(_split_shards(n, big[n]), True) for n in big_names] + [(_pack_small(small, loss), False)]
    parts = _exchange("exchange_grads", send)

    res = {}
    for n, p in zip(big_names, parts):
        shard, _, tile = BIG[n]
        r4 = _adamw(p, given[n][0], given["m_" + n][0], given["v_" + n][0], "adamw_" + n, tile)
        res[n] = [r.reshape((1,) + shard) for r in r4]
    zero = jnp.zeros((1,), F32)
    rs4 = _adamw(parts[-1], _pack_small({n: given[n] for n in small_names}, zero),
                 _pack_small({n: given["m_" + n] for n in small_names}, zero),
                 _pack_small({n: given["v_" + n] for n in small_names}, zero), "adamw_small", SMALL_ROWS)
    small4 = [_unpack_small(r) for r in rs4]
    for n in small_names:
        res[n] = [us[n] for us, _ in small4]
    total_loss = small4[0][1]
    return (total_loss, dx, *[res[n][0] for n in WEIGHT_ORDER], *[res[n][1] for n in WEIGHT_ORDER],
            *[res[n][2] for n in WEIGHT_ORDER], *[res[n][3] for n in WEIGHT_ORDER])
```

```python
import functools
import math

import jax
import jax.numpy as jnp
from jax import lax
from jax.experimental import pallas as pl
from jax.experimental.pallas import tpu as pltpu

F32 = jnp.float32
BF16 = jnp.bfloat16

D_MODEL = 1024
S5_WIDTH = 512
S5_GROUP = 16
S5_GROUPS = 32
S5_STATE = 64
S5_N = S5_GROUPS * S5_STATE
HG_WIDTH = 512
HG_HEAD = 128
HG_HEADS = 4
D_FF = 2816
CONV_W = 3
CHUNK = 64
N_IN = S5_WIDTH + 4 * HG_WIDTH + 2 * D_MODEL
EPS = 1e-6
QSCALE = HG_HEAD ** -0.5

ADAM_LR = 0.001
ADAM_B1 = 0.9
ADAM_B2 = 0.999
ADAM_EPS = 1e-08
ADAM_WD = 0.01
ADAM_STEP = 10

N_DEV = 8
V7X_VMEM_BYTES = 64 * 1024 * 1024
VMEM_LIMIT = V7X_VMEM_BYTES * 7 // 8
SUBLANES = 8
PACK_W = 1024

SMALL = (
    ("g_mix", (1, D_MODEL)),
    ("s5_a_re", (1, S5_GROUPS, S5_STATE)),
    ("s5_a_im", (1, S5_GROUPS, S5_STATE)),
    ("s5_log_dt", (1, S5_GROUPS)),
    ("s5_b_re", (1, S5_GROUPS, S5_STATE, S5_GROUP)),
    ("s5_b_im", (1, S5_GROUPS, S5_STATE, S5_GROUP)),
    ("s5_c_re", (1, S5_GROUPS, S5_GROUP, S5_STATE)),
    ("s5_c_im", (1, S5_GROUPS, S5_GROUP, S5_STATE)),
    ("s5_d", (1, S5_WIDTH)),
    ("b_glu", (1, S5_WIDTH)),
    ("hg_lb_logits", (2, HG_WIDTH)),
    ("hg_norm_gain", (1, HG_WIDTH)),
    ("g_ffn", (1, D_MODEL)),
    ("b_conv", (1, 2 * D_FF)),
    ("g_final", (D_MODEL,)),
)
SMALL_ROWS = 144
WEIGHT_ORDER = ("g_mix", "w_in", "s5_a_re", "s5_a_im", "s5_log_dt", "s5_b_re", "s5_b_im", "s5_c_re", "s5_c_im",
                "s5_d", "w_glu", "b_glu", "hg_lb_logits", "hg_norm_gain", "w_pa", "w_pb", "w_out", "g_ffn",
                "w_up", "w_conv", "b_conv", "w_down", "g_final")


def _pcall(body, name, grid, in_specs, out_specs, out_shape, scratch=()):
    return pl.pallas_call(
        body, name=name, grid=grid, in_specs=in_specs, out_specs=out_specs, out_shape=out_shape,
        scratch_shapes=list(scratch),
        compiler_params=pltpu.CompilerParams(dimension_semantics=("arbitrary",) * len(grid),
                                             vmem_limit_bytes=VMEM_LIMIT),
    )


def _full(shape):
    return pl.BlockSpec(shape, lambda *_: (0,) * len(shape))


def _sds(shape, dtype=F32):
    return jax.ShapeDtypeStruct(shape, dtype)


def _dot(a, b):
    return jnp.dot(a.astype(BF16), b.astype(BF16), preferred_element_type=F32)


def _dot_nt(a, b):
    return lax.dot_general(a.astype(BF16), b.astype(BF16), (((1,), (1,)), ((), ())), preferred_element_type=F32)


def _dot_tn(a, b):
    return lax.dot_general(a.astype(BF16), b.astype(BF16), (((0,), (0,)), ((), ())), preferred_element_type=F32)


def _hdot(a, b):
    return jnp.dot(a, b, preferred_element_type=F32, precision=lax.Precision.HIGHEST)


def _hdot_tn(a, b):
    return lax.dot_general(a, b, (((0,), (0,)), ((), ())), preferred_element_type=F32,
                           precision=lax.Precision.HIGHEST)


def _sigmoid(x):
    return jax.nn.sigmoid(x)


GELU_C = math.sqrt(2.0 / math.pi)
GELU_A = 0.044715


def _gelu(x):
    return 0.5 * x * (1.0 + jnp.tanh(GELU_C * (x + GELU_A * (x * x * x))))


def _gelu_grad(x):
    t = jnp.tanh(GELU_C * (x + GELU_A * (x * x * x)))
    return 0.5 * (1.0 + t) + 0.5 * x * (1.0 - t * t) * (GELU_C * (1.0 + 3.0 * GELU_A * x * x))


def _cumsum_rows(v, reverse=False):
    n = v.shape[0]
    row = lax.broadcasted_iota(jnp.int32, v.shape, 0)
    s = 1
    while s < n:
        if reverse:
            v = v + jnp.where(row < n - s, pltpu.roll(v, n - s, axis=0), 0.0)
        else:
            v = v + jnp.where(row >= s, pltpu.roll(v, s, axis=0), 0.0)
        s *= 2
    return v


def _token_tile(seq):
    return min(256, seq)


def _s5_disc(a_re, a_im, ldt, b_re, b_im):
    dt = jnp.exp(ldt)
    mag = jnp.exp(a_re * dt)
    ang = a_im * dt
    lb_re = mag * jnp.cos(ang)
    lb_im = mag * jnp.sin(ang)
    den = a_re * a_re + a_im * a_im
    n_re = lb_re - 1.0
    n_im = lb_im
    co_re = (n_re * a_re + n_im * a_im) / den
    co_im = (n_im * a_re - n_re * a_im) / den
    bb_re = co_re * b_re - co_im * b_im
    bb_im = co_re * b_im + co_im * b_re
    return lb_re, lb_im, bb_re, bb_im


def _params_fwd(a_re, a_im, ldt, b_re, b_im, logits):
    def body(are, aim, ld, bre, bim, lg, lr_o, li_o, bbr_o, bbi_o, lb_o):
        lr, li, bbr, bbi = _s5_disc(are[...], aim[...], ld[...], bre[...], bim[...])
        lr_o[...] = lr
        li_o[...] = li
        bbr_o[...] = bbr
        bbi_o[...] = bbi
        lb_o[...] = _sigmoid(lg[0:1, :] - lg[1:2, :])

    col, mat = (S5_N, 1), (S5_N, S5_GROUP)
    return _pcall(body, "params_fwd", (1,),
                  [_full(col), _full(col), _full(col), _full(mat), _full(mat), _full((2, HG_WIDTH))],
                  [_full(col), _full(col), _full(mat), _full(mat), _full((1, HG_WIDTH))],
                  [_sds(col), _sds(col), _sds(mat), _sds(mat), _sds((1, HG_WIDTH))])(a_re, a_im, ldt, b_re, b_im, logits)


def _params_bwd(a_re, a_im, ldt, b_re, b_im, logits, dlr, dli, dbbr, dbbi, dlb):
    def body(are, aim, ld, bre, bim, lg, dlr_r, dli_r, dbbr_r, dbbi_r, dlb_r,
             dare_o, daim_o, dld_o, dbre_o, dbim_o, dlg_o):
        _, vjp = jax.vjp(_s5_disc, are[...], aim[...], ld[...], bre[...], bim[...])
        dare, daim, dld, dbre, dbim = vjp((dlr_r[...], dli_r[...], dbbr_r[...], dbbi_r[...]))
        dare_o[...] = dare
        daim_o[...] = daim
        dbre_o[...] = dbre
        dbim_o[...] = dbim
        for g in range(S5_GROUPS):
            dld_o[g:g + 1, :] = jnp.sum(dld[g * S5_STATE:(g + 1) * S5_STATE, :], axis=0, keepdims=True)
        lb = _sigmoid(lg[0:1, :] - lg[1:2, :])
        d0 = dlb_r[...] * lb * (1.0 - lb)
        dlg_o[0:1, :] = d0
        dlg_o[1:2, :] = -d0

    col, mat = (S5_N, 1), (S5_N, S5_GROUP)
    return _pcall(body, "params_bwd", (1,),
                  [_full(col), _full(col), _full(col), _full(mat), _full(mat), _full((2, HG_WIDTH)),
                   _full(col), _full(col), _full(mat), _full(mat), _full((1, HG_WIDTH))],
                  [_full(col), _full(col), _full((S5_GROUPS, 1)), _full(mat), _full(mat), _full((2, HG_WIDTH))],
                  [_sds(col), _sds(col), _sds((S5_GROUPS, 1)), _sds(mat), _sds(mat), _sds((2, HG_WIDTH))],
                  )(a_re, a_im, ldt, b_re, b_im, logits, dlr, dli, dbbr, dbbi, dlb)


def _blockdiag(m):
    g, r, c = m.shape
    eye = jnp.eye(g, dtype=m.dtype)
    return (m[:, :, None, :] * eye[:, None, :, None]).reshape(g * r, g * c)


def _diag_blocks(band, r, c):
    g, nb = band.shape[0] // r, band.shape[1] // c
    on_diag = (jnp.arange(g) % nb)[:, None, None, None] == jnp.arange(nb)[None, None, :, None]
    return jnp.sum(jnp.where(on_diag, band.reshape(g, r, nb, c), 0.0), axis=2)


def _in_proj(x, g_mix, w_in, tm):
    t = x.shape[0]

    def body(x_ref, g_ref, w_ref, u_ref, za_ref, zh_ref, zg_ref):
        xv = x_ref[...]
        r = lax.rsqrt(jnp.mean(xv * xv, axis=-1, keepdims=True) + EPS)
        u = (xv * r * g_ref[...]).astype(BF16)
        u_ref[...] = u
        za_ref[...] = jnp.dot(u, w_ref[:, 0:S5_WIDTH], preferred_element_type=F32)
        zh_ref[...] = jnp.dot(u, w_ref[:, S5_WIDTH:S5_WIDTH + 4 * HG_WIDTH], preferred_element_type=F32)
        zg_ref[...] = jnp.dot(u, w_ref[:, S5_WIDTH + 4 * HG_WIDTH:], preferred_element_type=F32)

    row = lambda w: pl.BlockSpec((tm, w), lambda i: (i, 0))
    return _pcall(body, "in_proj", (t // tm,),
                  [row(D_MODEL), _full((1, D_MODEL)), _full((D_MODEL, N_IN))],
                  [row(D_MODEL), row(S5_WIDTH), row(4 * HG_WIDTH), row(2 * D_MODEL)],
                  [_sds((t, D_MODEL), BF16), _sds((t, S5_WIDTH)), _sds((t, 4 * HG_WIDTH)), _sds((t, 2 * D_MODEL))],
                  )(x, g_mix, w_in)


S5_LANES = 512


def _s5_fwd(za, bbt, lam, ct, dskip, nb, seq, ts):
    t = za.shape[0]
    nts = seq // ts

    def body(za_ref, bbt_ref, lam_ref, ct_ref, d_ref, xs_ref, y_ref, st_ref):
        @pl.when(pl.program_id(1) == 0)
        def _():
            st_ref[...] = jnp.zeros_like(st_ref)

        zav = za_ref[...]
        xs_ref[...] = _dot(zav, bbt_ref[...])
        for cc in range(S5_N // S5_LANES):
            re = slice(cc * S5_LANES, (cc + 1) * S5_LANES)
            im = slice(S5_N + cc * S5_LANES, S5_N + (cc + 1) * S5_LANES)
            lr = lam_ref[0:1, re]
            li = lam_ref[1:2, re]

            def step(i, carry):
                xr, xi = carry
                br = xs_ref[pl.ds(i, 1), re]
                bi = xs_ref[pl.ds(i, 1), im]
                nxr = lr * xr - li * xi + br
                nxi = lr * xi + li * xr + bi
                xs_ref[pl.ds(i, 1), re] = nxr
                xs_ref[pl.ds(i, 1), im] = nxi
                return nxr, nxi

            xr, xi = lax.fori_loop(0, ts, step, (st_ref[0:1, re], st_ref[1:2, re]), unroll=8)
            st_ref[0:1, re] = xr
            st_ref[1:2, re] = xi
        y_ref[...] = _dot(xs_ref[...], ct_ref[...]) + d_ref[...] * zav

    tok = lambda w: pl.BlockSpec((ts, w), lambda b, j: (b * nts + j, 0))
    return _pcall(body, "s5_fwd", (nb, nts),
                  [tok(S5_WIDTH), _full((S5_WIDTH, 2 * S5_N)), _full((2, S5_N)), _full((2 * S5_N, S5_WIDTH)),
                   _full((1, S5_WIDTH))],
                  [tok(2 * S5_N), tok(S5_WIDTH)],
                  [_sds((t, 2 * S5_N)), _sds((t, S5_WIDTH))],
                  scratch=[pltpu.VMEM((2, S5_N), F32)])(za, bbt, lam, ct, dskip)


def _hgrn_gates(zq, zf, lbh):
    sf = _sigmoid(zf)
    f = lbh + (1.0 - lbh) * sf
    sq = _sigmoid(zq)
    qa = zq * sq * QSCALE
    bc = _cumsum_rows(jnp.log(f))
    bm = bc[CHUNK // 2 - 1:CHUNK // 2, :]
    bl = bc[CHUNK - 1:CHUNK, :]
    return sf, f, sq, qa, bc, bm, bl


def _hgrn_fwd(zh, lb, nb, seq):
    nc = seq // CHUNK

    def body(zh_ref, lb_ref, o_ref, sts_ref, st_ref):
        @pl.when(pl.program_id(0) == 0)
        def _():
            st_ref[...] = jnp.zeros_like(st_ref)

        causal = (lax.broadcasted_iota(jnp.int32, (CHUNK, CHUNK), 0)
                  >= lax.broadcasted_iota(jnp.int32, (CHUNK, CHUNK), 1))
        for b in range(nb):
            for h in range(HG_HEADS):
                hs = slice(h * HG_HEAD, (h + 1) * HG_HEAD)
                zq = zh_ref[b, :, h * HG_HEAD:(h + 1) * HG_HEAD]
                zf = zh_ref[b, :, HG_WIDTH + h * HG_HEAD:HG_WIDTH + (h + 1) * HG_HEAD]
                zi = zh_ref[b, :, 2 * HG_WIDTH + h * HG_HEAD:2 * HG_WIDTH + (h + 1) * HG_HEAD]
                _, f, _, qa, bc, bm, bl = _hgrn_gates(zq, zf, lb_ref[:, hs])
                k = 1.0 - f
                qt = qa * jnp.exp(bc - bm)
                kt = k * jnp.exp(bm - bc)
                qb = qa * jnp.exp(bc)
                kd = k * jnp.exp(bl - bc)
                st = st_ref[b, h]
                sts_ref[b, 0, h] = st
                a = jnp.where(causal, _dot_nt(qt, kt), 0.0)
                o_ref[b, :, hs] = _dot(a, zi) + _dot_nt(qb, st)
                st_ref[b, h] = st * jnp.exp(bl) + _dot_tn(zi, kd)

    return _pcall(body, "hgrn_fwd", (nc,),
                  [pl.BlockSpec((nb, CHUNK, 4 * HG_WIDTH), lambda c: (0, c, 0)), _full((1, HG_WIDTH))],
                  [pl.BlockSpec((nb, CHUNK, HG_WIDTH), lambda c: (0, c, 0)),
                   pl.BlockSpec((nb, 1, HG_HEADS, HG_HEAD, HG_HEAD), lambda c: (0, c, 0, 0, 0))],
                  [_sds((nb, seq, HG_WIDTH)), _sds((nb, nc, HG_HEADS, HG_HEAD, HG_HEAD))],
                  scratch=[pltpu.VMEM((nb, HG_HEADS, HG_HEAD, HG_HEAD), F32)])(zh, lb)


def _head_rms(o):
    parts = []
    for h in range(HG_HEADS):
        oh = o[:, h * HG_HEAD:(h + 1) * HG_HEAD]
        r = lax.rsqrt(jnp.mean(oh * oh, axis=-1, keepdims=True) + EPS)
        parts.append(jnp.broadcast_to(r, oh.shape))
    return jnp.concatenate(parts, axis=1)


def _head_mean(v):
    parts = []
    for h in range(HG_HEADS):
        vh = v[:, h * HG_HEAD:(h + 1) * HG_HEAD]
        parts.append(jnp.broadcast_to(jnp.mean(vh, axis=-1, keepdims=True), vh.shape))
    return jnp.concatenate(parts, axis=1)


def _mix_fwd(x, y0, o, zh, zgt, w_glu, b_glu, gain, w_pa, w_pb, w_out, g_ffn, tm):
    t = x.shape[0]

    def body(x_ref, y0_ref, o_ref, zg_ref, zgt_ref, wglu_ref, bglu_ref, gain_ref, wpa_ref, wpb_ref, wout_ref,
             gffn_ref, x1_ref, u2_ref, pa_ref, pb_ref, ya2_ref, yb_ref):
        ya1 = _gelu(y0_ref[...])
        s = _sigmoid(_dot(ya1, wglu_ref[...]) + bglu_ref[...])
        ya2 = (ya1 * s).astype(BF16)
        ov = o_ref[...]
        zg = zg_ref[...]
        yb = (ov * _head_rms(ov) * gain_ref[...] * (zg * _sigmoid(zg))).astype(BF16)
        ya2_ref[...] = ya2
        yb_ref[...] = yb
        pa = jnp.dot(ya2, wpa_ref[...], preferred_element_type=F32)
        pb = jnp.dot(yb, wpb_ref[...], preferred_element_type=F32)
        pa_ref[...] = pa
        pb_ref[...] = pb
        m = _sigmoid(zgt_ref[:, 0:D_MODEL]) * pa + _sigmoid(zgt_ref[:, D_MODEL:]) * pb
        x1 = x_ref[...] + _dot(m, wout_ref[...])
        x1_ref[...] = x1
        r = lax.rsqrt(jnp.mean(x1 * x1, axis=-1, keepdims=True) + EPS)
        u2_ref[...] = (x1 * r * gffn_ref[...]).astype(BF16)

    row = lambda w: pl.BlockSpec((tm, w), lambda i: (i, 0))
    return _pcall(body, "mix_fwd", (t // tm,),
                  [row(D_MODEL), row(S5_WIDTH), row(HG_WIDTH), pl.BlockSpec((tm, HG_WIDTH), lambda i: (i, 3)),
                   row(2 * D_MODEL), _full((S5_WIDTH, S5_WIDTH)), _full((1, S5_WIDTH)), _full((1, HG_WIDTH)),
                   _full((S5_WIDTH, D_MODEL)), _full((HG_WIDTH, D_MODEL)), _full((D_MODEL, D_MODEL)),
                   _full((1, D_MODEL))],
                  [row(D_MODEL), row(D_MODEL), row(D_MODEL), row(D_MODEL), row(S5_WIDTH), row(HG_WIDTH)],
                  [_sds((t, D_MODEL)), _sds((t, D_MODEL), BF16), _sds((t, D_MODEL)), _sds((t, D_MODEL)),
                   _sds((t, S5_WIDTH), BF16), _sds((t, HG_WIDTH), BF16)],
                  )(x, y0, o, zh, zgt, w_glu, b_glu, gain, w_pa, w_pb, w_out, g_ffn)


FF_COLS = 256
FF_UP_TILE = 1408


def _ffn_up(u2, w_up, tm):
    t = u2.shape[0]
    n = 2 * D_FF

    def body(u_ref, w_ref, h_ref):
        h_ref[...] = jnp.dot(u_ref[...], w_ref[...], preferred_element_type=F32)

    return _pcall(body, "ffn_up", (n // FF_UP_TILE, t // tm),
                  [pl.BlockSpec((tm, D_MODEL), lambda j, i: (i, 0)),
                   pl.BlockSpec((D_MODEL, FF_UP_TILE), lambda j, i: (0, j))],
                  pl.BlockSpec((tm, FF_UP_TILE), lambda j, i: (i, j)),
                  _sds((t, n)))(u2, w_up)


def _conv_cols(h_ref, halo_ref, valid, wc_ref, bc_ref, c0, tm):
    cs = slice(c0, c0 + FF_COLS)
    cur = h_ref[:, cs]
    prev = jnp.where(valid, halo_ref[:, cs], 0.0)
    full = jnp.concatenate([prev, cur], axis=0)
    h1 = pltpu.roll(full, 1, axis=0)[SUBLANES:]
    h2 = pltpu.roll(full, 2, axis=0)[SUBLANES:]
    hc = h2 * wc_ref[0:1, cs] + h1 * wc_ref[1:2, cs] + cur * wc_ref[2:3, cs] + bc_ref[:, cs]
    return hc, cur, h1, h2


def _ffn_down_loss(h, x1, tgt, w_conv, b_conv, w_down, g_final, seq, tm):
    t = h.shape[0]
    tps = seq // tm
    n = 2 * D_FF

    def body(h_ref, halo_ref, x1_ref, tgt_ref, wc_ref, bc_ref, wd_ref, gf_ref,
             a_ref, dx2_ref, loss_ref, dgf_ref):
        i = pl.program_id(0)

        @pl.when(i == 0)
        def _():
            loss_ref[...] = jnp.zeros_like(loss_ref)
            dgf_ref[...] = jnp.zeros_like(dgf_ref)

        valid = (i % tps) != 0
        x2 = x1_ref[...]
        for j in range(D_FF // FF_COLS):
            gate = _conv_cols(h_ref, halo_ref, valid, wc_ref, bc_ref, j * FF_COLS, tm)[0]
            val = _conv_cols(h_ref, halo_ref, valid, wc_ref, bc_ref, D_FF + j * FF_COLS, tm)[0]
            a = (gate * _sigmoid(gate) * val).astype(BF16)
            a_ref[:, j * FF_COLS:(j + 1) * FF_COLS] = a
            x2 = x2 + jnp.dot(a, wd_ref[j * FF_COLS:(j + 1) * FF_COLS, :], preferred_element_type=F32)
        r = lax.rsqrt(jnp.mean(x2 * x2, axis=-1, keepdims=True) + EPS)
        xn = x2 * r
        g = gf_ref[...]
        e = xn * g - tgt_ref[...]
        loss_ref[...] += (0.5 / D_MODEL) * jnp.sum(e * e).reshape(1, 1)
        dy = e * (1.0 / D_MODEL)
        dgf_ref[...] += jnp.sum(dy * xn, axis=0, keepdims=True)
        dxn = dy * g
        dx2_ref[...] = r * (dxn - xn * jnp.mean(dxn * xn, axis=-1, keepdims=True))

    row = lambda w: pl.BlockSpec((tm, w), lambda i: (i, 0))
    halo = pl.BlockSpec((SUBLANES, n), lambda i: (jnp.maximum(i * (tm // SUBLANES) - 1, 0), 0))
    return _pcall(body, "ffn_down_loss", (t // tm,),
                  [row(n), halo, row(D_MODEL), row(D_MODEL), _full((CONV_W, n)), _full((1, n)),
                   _full((D_FF, D_MODEL)), _full((1, D_MODEL))],
                  [row(D_FF), row(D_MODEL), _full((1, 1)), _full((1, D_MODEL))],
                  [_sds((t, D_FF), BF16), _sds((t, D_MODEL)), _sds((1, 1)), _sds((1, D_MODEL))],
                  )(h, h, x1, tgt, w_conv, b_conv, w_down, g_final)


def _matmul_tn(a, b, name, tm, tk, out_dtype=F32, band=None):
    t, m = a.shape
    n = b.shape[1] if band is None else band
    nbands = 1 if band is None else b.shape[1] // band
    nk = t // tk

    def body(a_ref, b_ref, o_ref, acc_ref):
        k = pl.program_id(1)

        @pl.when(k == 0)
        def _():
            acc_ref[...] = jnp.zeros_like(acc_ref)

        acc_ref[...] += _dot_tn(a_ref[...], b_ref[...])

        @pl.when(k == nk - 1)
        def _():
            o_ref[...] = acc_ref[...].astype(out_dtype)

    return _pcall(body, name, (m // tm, nk),
                  [pl.BlockSpec((tk, tm), lambda i, k: (k, i)), pl.BlockSpec((tk, n), lambda i, k: (k, i % nbands))],
                  pl.BlockSpec((tm, n), lambda i, k: (i, 0)), _sds((m, n), out_dtype),
                  scratch=[pltpu.VMEM((tm, n), F32)])(a, b)


def _ffn_bwd_act(dx2, h, w_conv, b_conv, w_down, seq, tm):
    t = h.shape[0]
    tps = seq // tm
    n = 2 * D_FF

    def body(dx2_ref, h_ref, halo_ref, wc_ref, bc_ref, wd_ref, dhc_ref, dwc_ref, dbc_ref):
        i = pl.program_id(0)

        @pl.when(i == 0)
        def _():
            dwc_ref[...] = jnp.zeros_like(dwc_ref)
            dbc_ref[...] = jnp.zeros_like(dbc_ref)

        valid = (i % tps) != 0
        dx2 = dx2_ref[...].astype(BF16)

        def conv_grads(c0, dhc, taps):
            cs = slice(c0, c0 + FF_COLS)
            dhc_ref[:, cs] = dhc
            dbc_ref[:, cs] += jnp.sum(dhc, axis=0, keepdims=True)
            for w, tap in enumerate(taps):
                dwc_ref[w:w + 1, cs] += jnp.sum(tap * dhc, axis=0, keepdims=True)

        for j in range(D_FF // FF_COLS):
            gate, g0, g1, g2 = _conv_cols(h_ref, halo_ref, valid, wc_ref, bc_ref, j * FF_COLS, tm)
            val, v0, v1, v2 = _conv_cols(h_ref, halo_ref, valid, wc_ref, bc_ref, D_FF + j * FF_COLS, tm)
            da = _dot_nt(dx2, wd_ref[j * FF_COLS:(j + 1) * FF_COLS, :])
            sg = _sigmoid(gate)
            conv_grads(j * FF_COLS, da * val * (sg * (1.0 + gate * (1.0 - sg))), (g2, g1, g0))
            conv_grads(D_FF + j * FF_COLS, da * (gate * sg), (v2, v1, v0))

    row = lambda w: pl.BlockSpec((tm, w), lambda i: (i, 0))
    halo = pl.BlockSpec((SUBLANES, n), lambda i: (jnp.maximum(i * (tm // SUBLANES) - 1, 0), 0))
    return _pcall(body, "ffn_bwd_act", (t // tm,),
                  [row(D_MODEL), row(n), halo, _full((CONV_W, n)), _full((1, n)), _full((D_FF, D_MODEL))],
                  [row(n), _full((CONV_W, n)), _full((1, n))],
                  [_sds((t, n)), _sds((CONV_W, n)), _sds((1, n))],
                  )(dx2, h, h, w_conv, b_conv, w_down)


def _ffn_bwd_up(dhc, dx2, x1, w_conv, w_up, g_ffn, seq, tm):
    t = dhc.shape[0]
    tps = seq // tm
    n = 2 * D_FF
    last = t // SUBLANES - 1

    def body(dhc_ref, halo_ref, dx2_ref, x1_ref, wc_ref, wu_ref, gf_ref, dh_ref, dx1_ref, dgf_ref):
        i = pl.program_id(0)

        @pl.when(i == 0)
        def _():
            dgf_ref[...] = jnp.zeros_like(dgf_ref)

        valid = ((i + 1) % tps) != 0
        du2 = jnp.zeros((tm, D_MODEL), F32)
        for j in range(n // FF_COLS):
            cs = slice(j * FF_COLS, (j + 1) * FF_COLS)
            cur = dhc_ref[:, cs]
            nxt = jnp.where(valid, halo_ref[:, cs], 0.0)
            full = jnp.concatenate([cur, nxt], axis=0)
            d1 = pltpu.roll(full, tm + SUBLANES - 1, axis=0)[:tm]
            d2 = pltpu.roll(full, tm + SUBLANES - 2, axis=0)[:tm]
            dh = (cur * wc_ref[2:3, cs] + d1 * wc_ref[1:2, cs] + d2 * wc_ref[0:1, cs]).astype(BF16)
            dh_ref[:, cs] = dh
            du2 = du2 + _dot_nt(dh, wu_ref[:, cs])
        x1 = x1_ref[...]
        r = lax.rsqrt(jnp.mean(x1 * x1, axis=-1, keepdims=True) + EPS)
        xn = x1 * r
        dgf_ref[...] += jnp.sum(du2 * xn, axis=0, keepdims=True)
        dxn = du2 * gf_ref[...]
        dx1_ref[...] = dx2_ref[...] + r * (dxn - xn * jnp.mean(dxn * xn, axis=-1, keepdims=True))

    row = lambda w: pl.BlockSpec((tm, w), lambda i: (i, 0))
    halo = pl.BlockSpec((SUBLANES, n), lambda i: (jnp.minimum((i + 1) * (tm // SUBLANES), last), 0))
    return _pcall(body, "ffn_bwd_up", (t // tm,),
                  [row(n), halo, row(D_MODEL), row(D_MODEL), _full((CONV_W, n)), _full((D_MODEL, n)),
                   _full((1, D_MODEL))],
                  [row(n), row(D_MODEL), _full((1, D_MODEL))],
                  [_sds((t, n), BF16), _sds((t, D_MODEL)), _sds((1, D_MODEL))],
                  )(dhc, dhc, dx2, x1, w_conv, w_up, g_ffn)


def _mix_bwd(dx1, y0, o, zh, zgt, pa, pb, w_glu, b_glu, gain, w_pa, w_pb, w_out, tm):
    t = dx1.shape[0]

    def body(dx1_ref, y0_ref, o_ref, zg_ref, zgt_ref, pa_ref, pb_ref, wglu_ref, bglu_ref, gain_ref, wpa_ref,
             wpb_ref, wout_ref,
             dy0_ref, do_ref, dzg_ref, dzgt_ref, m_ref, dpa_ref, dpb_ref, ya1_ref, dpre_ref, dbglu_ref, dgain_ref):
        @pl.when(pl.program_id(0) == 0)
        def _():
            dbglu_ref[...] = jnp.zeros_like(dbglu_ref)
            dgain_ref[...] = jnp.zeros_like(dgain_ref)

        dm = _dot_nt(dx1_ref[...], wout_ref[...])
        sga = _sigmoid(zgt_ref[:, 0:D_MODEL])
        sgb = _sigmoid(zgt_ref[:, D_MODEL:])
        pa = pa_ref[...]
        pb = pb_ref[...]
        m_ref[...] = (sga * pa + sgb * pb).astype(BF16)
        dzgt_ref[:, 0:D_MODEL] = dm * pa * sga * (1.0 - sga)
        dzgt_ref[:, D_MODEL:] = dm * pb * sgb * (1.0 - sgb)
        dpa = (dm * sga).astype(BF16)
        dpb = (dm * sgb).astype(BF16)
        dpa_ref[...] = dpa
        dpb_ref[...] = dpb
        dya2 = _dot_nt(dpa, wpa_ref[...])
        dyb = _dot_nt(dpb, wpb_ref[...])
        y0 = y0_ref[...]
        ya1 = _gelu(y0)
        ya1_ref[...] = ya1.astype(BF16)
        s = _sigmoid(_dot(ya1, wglu_ref[...]) + bglu_ref[...])
        dpre = dya2 * ya1 * s * (1.0 - s)
        dpre_ref[...] = dpre.astype(BF16)
        dbglu_ref[...] += jnp.sum(dpre, axis=0, keepdims=True)
        dya1 = dya2 * s + _dot_nt(dpre, wglu_ref[...])
        dy0_ref[...] = dya1 * _gelu_grad(y0)
        ov = o_ref[...]
        zg = zg_ref[...]
        oh = ov * _head_rms(ov)
        on = oh * gain_ref[...]
        sz = _sigmoid(zg)
        dzg_ref[...] = dyb * on * (sz * (1.0 + zg * (1.0 - sz)))
        don = dyb * (zg * sz)
        dgain_ref[...] += jnp.sum(don * oh, axis=0, keepdims=True)
        doh = don * gain_ref[...]
        do_ref[...] = _head_rms(ov) * (doh - oh * _head_mean(doh * oh))

    row = lambda w: pl.BlockSpec((tm, w), lambda i: (i, 0))
    return _pcall(body, "mix_bwd", (t // tm,),
                  [row(D_MODEL), row(S5_WIDTH), row(HG_WIDTH), pl.BlockSpec((tm, HG_WIDTH), lambda i: (i, 3)),
                   row(2 * D_MODEL), row(D_MODEL), row(D_MODEL), _full((S5_WIDTH, S5_WIDTH)), _full((1, S5_WIDTH)),
                   _full((1, HG_WIDTH)), _full((S5_WIDTH, D_MODEL)), _full((HG_WIDTH, D_MODEL)),
                   _full((D_MODEL, D_MODEL))],
                  [row(S5_WIDTH), row(HG_WIDTH), row(HG_WIDTH), row(2 * D_MODEL), row(D_MODEL), row(D_MODEL),
                   row(D_MODEL), row(S5_WIDTH), row(S5_WIDTH), _full((1, S5_WIDTH)), _full((1, HG_WIDTH))],
                  [_sds((t, S5_WIDTH)), _sds((t, HG_WIDTH)), _sds((t, HG_WIDTH)), _sds((t, 2 * D_MODEL)),
                   _sds((t, D_MODEL), BF16), _sds((t, D_MODEL), BF16), _sds((t, D_MODEL), BF16),
                   _sds((t, S5_WIDTH), BF16), _sds((t, S5_WIDTH), BF16), _sds((1, S5_WIDTH)), _sds((1, HG_WIDTH))],
                  )(dx1, y0, o, zh, zgt, pa, pb, w_glu, b_glu, gain, w_pa, w_pb, w_out)


def _s5_bwd(dy0, za, xs, cg, bbdt, lam, dskip, nb, seq, ts):
    t = za.shape[0]
    nts = seq // ts

    def body(dy0_ref, za_ref, xs_ref, halo_ref, cg_ref, bbdt_ref, lam_ref, d_ref,
             dza_ref, a_ref, dlam_ref, dd_ref, acc_ref, st_ref):
        b, j = pl.program_id(0), pl.program_id(1)

        @pl.when((b == 0) & (j == 0))
        def _():
            dlam_ref[...] = jnp.zeros_like(dlam_ref)
            dd_ref[...] = jnp.zeros_like(dd_ref)

        @pl.when(j == 0)
        def _():
            st_ref[...] = jnp.zeros_like(st_ref)

        dy0 = dy0_ref[...]
        acc_ref[...] = _dot(dy0, cg_ref[...])
        for cc in range(S5_N // S5_LANES):
            re = slice(cc * S5_LANES, (cc + 1) * S5_LANES)
            im = slice(S5_N + cc * S5_LANES, S5_N + (cc + 1) * S5_LANES)
            lr = lam_ref[0:1, re]
            li = lam_ref[1:2, re]

            def step(i, carry):
                ar, ai = carry
                row = ts - 1 - i
                nar = acc_ref[pl.ds(row, 1), re] + lr * ar + li * ai
                nai = acc_ref[pl.ds(row, 1), im] - li * ar + lr * ai
                acc_ref[pl.ds(row, 1), re] = nar
                acc_ref[pl.ds(row, 1), im] = nai
                return nar, nai

            ar, ai = lax.fori_loop(0, ts, step, (st_ref[0:1, re], st_ref[1:2, re]), unroll=8)
            st_ref[0:1, re] = ar
            st_ref[1:2, re] = ai
        av = acc_ref[...]
        a_ref[...] = av.astype(BF16)
        first = jnp.where(j == nts - 1, 0.0, halo_ref[SUBLANES - 1:SUBLANES, :])
        rows = lax.broadcasted_iota(jnp.int32, (ts, 2 * S5_N), 0)
        xp = jnp.where(rows == 0, first, pltpu.roll(xs_ref[...], 1, axis=0))
        ar, ai = av[:, :S5_N], av[:, S5_N:]
        xr, xi = xp[:, :S5_N], xp[:, S5_N:]
        dlam_ref[0:1, :] += jnp.sum(ar * xr + ai * xi, axis=0, keepdims=True)
        dlam_ref[1:2, :] += jnp.sum(ai * xr - ar * xi, axis=0, keepdims=True)
        zav = za_ref[...]
        dza_ref[...] = _dot(av, bbdt_ref[...]) + d_ref[...] * dy0
        dd_ref[...] += jnp.sum(dy0 * zav, axis=0, keepdims=True)

    tile = lambda b, j: b * nts + (nts - 1 - j)
    tok = lambda w: pl.BlockSpec((ts, w), lambda b, j: (tile(b, j), 0))
    halo = pl.BlockSpec((SUBLANES, 2 * S5_N),
                        lambda b, j: (jnp.maximum(tile(b, j) * (ts // SUBLANES) - 1, 0), 0))
    return _pcall(body, "s5_bwd", (nb, nts),
                  [tok(S5_WIDTH), tok(S5_WIDTH), tok(2 * S5_N), halo, _full((S5_WIDTH, 2 * S5_N)),
                   _full((2 * S5_N, S5_WIDTH)), _full((2, S5_N)), _full((1, S5_WIDTH))],
                  [tok(S5_WIDTH), tok(2 * S5_N), _full((2, S5_N)), _full((1, S5_WIDTH))],
                  [_sds((t, S5_WIDTH)), _sds((t, 2 * S5_N), BF16), _sds((2, S5_N)), _sds((1, S5_WIDTH))],
                  scratch=[pltpu.VMEM((ts, 2 * S5_N), F32), pltpu.VMEM((2, S5_N), F32)],
                  )(dy0, za, xs, xs, cg, bbdt, lam, dskip)


def _hgrn_bwd(zh, do, sts, lb, nb, seq):
    nc = seq // CHUNK

    def body(zh_ref, do_ref, sts_ref, lb_ref, dz_ref, dlb_ref, dst_ref):
        @pl.when(pl.program_id(0) == 0)
        def _():
            dst_ref[...] = jnp.zeros_like(dst_ref)
            dlb_ref[...] = jnp.zeros_like(dlb_ref)

        row = lax.broadcasted_iota(jnp.int32, (CHUNK, CHUNK), 0)
        causal = row >= lax.broadcasted_iota(jnp.int32, (CHUNK, CHUNK), 1)
        last_row = lax.broadcasted_iota(jnp.int32, (CHUNK, HG_HEAD), 0) == CHUNK - 1
        for b in range(nb):
            for h in range(HG_HEADS):
                hs = slice(h * HG_HEAD, (h + 1) * HG_HEAD)
                zq = zh_ref[b, :, h * HG_HEAD:(h + 1) * HG_HEAD]
                zf = zh_ref[b, :, HG_WIDTH + h * HG_HEAD:HG_WIDTH + (h + 1) * HG_HEAD]
                zi = zh_ref[b, :, 2 * HG_WIDTH + h * HG_HEAD:2 * HG_WIDTH + (h + 1) * HG_HEAD]
                lbh = lb_ref[:, hs]
                sf, f, sq, qa, bc, bm, bl = _hgrn_gates(zq, zf, lbh)
                k = 1.0 - f
                e_qt = jnp.exp(bc - bm)
                e_kt = jnp.exp(bm - bc)
                e_b = jnp.exp(bc)
                e_kd = jnp.exp(bl - bc)
                e_l = jnp.exp(bl)
                qt, kt, qb, kd = qa * e_qt, k * e_kt, qa * e_b, k * e_kd
                a = jnp.where(causal, _dot_nt(qt, kt), 0.0)
                st = sts_ref[b, 0, h]
                dst = dst_ref[b, h]
                dov = do_ref[b, :, hs]
                da = jnp.where(causal, _dot_nt(dov, zi), 0.0)
                dqt = _hdot(da, kt)
                dkt = _hdot_tn(da, qt)
                dqb = _hdot(dov, st)
                di = _dot_tn(a, dov) + _dot_nt(kd, dst)
                dkd = _hdot(zi, dst)
                de_l = jnp.sum(dst * st, axis=0, keepdims=True)
                dst_ref[b, h] = dst * e_l + _dot_tn(dov, qb)
                dqa = dqt * e_qt + dqb * e_b
                dk = dkt * e_kt + dkd * e_kd
                dbl = jnp.sum(dkd * kd, axis=0, keepdims=True) + de_l * e_l
                db = dqt * qt - dkt * kt + dqb * qb - dkd * kd + jnp.where(last_row, dbl, 0.0)
                df = _cumsum_rows(db, reverse=True) / f - dk
                dz_ref[b, :, h * HG_HEAD:(h + 1) * HG_HEAD] = dqa * QSCALE * (sq * (1.0 + zq * (1.0 - sq)))
                dz_ref[b, :, HG_WIDTH + h * HG_HEAD:HG_WIDTH + (h + 1) * HG_HEAD] = df * (1.0 - lbh) * sf * (1.0 - sf)
                dz_ref[b, :, 2 * HG_WIDTH + h * HG_HEAD:2 * HG_WIDTH + (h + 1) * HG_HEAD] = di
                dlb_ref[:, hs] += jnp.sum(df * (1.0 - sf), axis=0, keepdims=True)

    rev = lambda c: nc - 1 - c
    return _pcall(body, "hgrn_bwd", (nc,),
                  [pl.BlockSpec((nb, CHUNK, 4 * HG_WIDTH), lambda c: (0, rev(c), 0)),
                   pl.BlockSpec((nb, CHUNK, HG_WIDTH), lambda c: (0, rev(c), 0)),
                   pl.BlockSpec((nb, 1, HG_HEADS, HG_HEAD, HG_HEAD), lambda c: (0, rev(c), 0, 0, 0)),
                   _full((1, HG_WIDTH))],
                  [pl.BlockSpec((nb, CHUNK, 3 * HG_WIDTH), lambda c: (0, rev(c), 0)), _full((1, HG_WIDTH))],
                  [_sds((nb, seq, 3 * HG_WIDTH)), _sds((1, HG_WIDTH))],
                  scratch=[pltpu.VMEM((nb, HG_HEADS, HG_HEAD, HG_HEAD), F32)])(zh, do, sts, lb)


def _in_proj_bwd(dza, dzh, dzg, dzgt, dx1, x, g_mix, w_in, tm):
    t = x.shape[0]

    def body(dza_ref, dzh_ref, dzg_ref, dzgt_ref, dx1_ref, x_ref, g_ref, w_ref, dz_ref, dx_ref, dg_ref):
        @pl.when(pl.program_id(0) == 0)
        def _():
            dg_ref[...] = jnp.zeros_like(dg_ref)

        c1, c2, c3 = S5_WIDTH, S5_WIDTH + 3 * HG_WIDTH, S5_WIDTH + 4 * HG_WIDTH
        dz_ref[:, 0:c1] = dza_ref[...].astype(BF16)
        dz_ref[:, c1:c2] = dzh_ref[...].astype(BF16)
        dz_ref[:, c2:c3] = dzg_ref[...].astype(BF16)
        dz_ref[:, c3:] = dzgt_ref[...].astype(BF16)
        du = _dot_nt(dz_ref[...], w_ref[...])
        xv = x_ref[...]
        r = lax.rsqrt(jnp.mean(xv * xv, axis=-1, keepdims=True) + EPS)
        xn = xv * r
        dg_ref[...] += jnp.sum(du * xn, axis=0, keepdims=True)
        dxn = du * g_ref[...]
        dx_ref[...] = dx1_ref[...] + r * (dxn - xn * jnp.mean(dxn * xn, axis=-1, keepdims=True))

    row = lambda w: pl.BlockSpec((tm, w), lambda i: (i, 0))
    return _pcall(body, "in_proj_bwd", (t // tm,),
                  [row(S5_WIDTH), row(3 * HG_WIDTH), row(HG_WIDTH), row(2 * D_MODEL), row(D_MODEL), row(D_MODEL),
                   _full((1, D_MODEL)), _full((D_MODEL, N_IN))],
                  [row(N_IN), row(D_MODEL), _full((1, D_MODEL))],
                  [_sds((t, N_IN), BF16), _sds((t, D_MODEL)), _sds((1, D_MODEL))],
                  )(dza, dzh, dzg, dzgt, dx1, x, g_mix, w_in)


def _local_step(x3, tgt3, w_in, rest_weights, sp, emit):
    nb, seq, _ = x3.shape
    t = nb * seq
    tm = _token_tile(seq)
    x = x3.reshape(t, D_MODEL)
    tgt = tgt3.reshape(t, D_MODEL)
    row = lambda v: v.reshape(1, -1)

    a_re = sp["s5_a_re"].reshape(S5_N, 1)
    a_im = sp["s5_a_im"].reshape(S5_N, 1)
    ldt = jnp.repeat(sp["s5_log_dt"].reshape(S5_GROUPS), S5_STATE).reshape(S5_N, 1)
    b_re = sp["s5_b_re"].reshape(S5_N, S5_GROUP)
    b_im = sp["s5_b_im"].reshape(S5_N, S5_GROUP)
    lr, li, bb_re, bb_im, lb = _params_fwd(a_re, a_im, ldt, b_re, b_im, sp["hg_lb_logits"])
    lam = jnp.concatenate([lr.reshape(1, S5_N), li.reshape(1, S5_N)], axis=0)
    gps = lambda m: m.reshape(S5_GROUPS, S5_STATE, S5_GROUP)
    bbt = jnp.concatenate([_blockdiag(gps(bb_re).transpose(0, 2, 1)), _blockdiag(gps(bb_im).transpose(0, 2, 1))],
                          axis=1).astype(BF16)
    ct = jnp.concatenate([_blockdiag(sp["s5_c_re"].transpose(0, 2, 1)), -_blockdiag(sp["s5_c_im"].transpose(0, 2, 1))],
                         axis=0).astype(BF16)

    g_mix, g_ffn, g_final = row(sp["g_mix"]), row(sp["g_ffn"]), row(sp["g_final"])
    b_glu, gain, dskip, b_conv = row(sp["b_glu"]), row(sp["hg_norm_gain"]), row(sp["s5_d"]), row(sp["b_conv"])

    u, za, zh, zgt = _in_proj(x, g_mix, w_in, tm)
    xs, y0 = _s5_fwd(za, bbt, lam, ct, dskip, nb, seq, tm)
    o3, sts = _hgrn_fwd(zh.reshape(nb, seq, 4 * HG_WIDTH), lb, nb, seq)
    o = o3.reshape(t, HG_WIDTH)
    wts = rest_weights(o3)
    x1, u2, pa, pb, ya2, yb = _mix_fwd(x, y0, o, zh, zgt, wts["w_glu"], b_glu, gain, wts["w_pa"], wts["w_pb"],
                                       wts["w_out"], g_ffn, tm)
    h = _ffn_up(u2, wts["w_up"], tm)
    a, dx2, loss, dg_final = _ffn_down_loss(h, x1, tgt, wts["w_conv"], b_conv, wts["w_down"], g_final, seq, tm)

    tk = min(512, t)
    wgrad = functools.partial(_matmul_tn, tm=256, tk=tk, out_dtype=BF16)
    dhc, dw_conv, db_conv = _ffn_bwd_act(dx2, h, wts["w_conv"], b_conv, wts["w_down"], seq, tm)
    sent = emit({"w_conv": dw_conv, "w_down": wgrad(a, dx2, "dw_down")})
    dh, dx1, dg_ffn = _ffn_bwd_up(dhc, dx2, x1, wts["w_conv"], wts["w_up"], g_ffn + sent, seq, tm)
    sent = emit({"w_up": wgrad(u2, dh, "dw_up")})
    (dy0, do, dzg, dzgt, m, dpa, dpb, ya1, dpre, db_glu, dgain) = _mix_bwd(
        dx1, y0, o, zh, zgt, pa, pb, wts["w_glu"], b_glu + sent, gain, wts["w_pa"], wts["w_pb"], wts["w_out"], tm)
    sent = emit({"w_out": wgrad(m, dx1, "dw_out"), "w_pa": wgrad(ya2, dpa, "dw_pa"),
                 "w_pb": wgrad(yb, dpb, "dw_pb"), "w_glu": wgrad(ya1, dpre, "dw_glu")})
    dza, a_s5, dlam, dd = _s5_bwd(dy0, za, xs, ct.T, bbt.T, lam, dskip + sent, nb, seq, tm)
    band = HG_HEAD
    dbb_band = _matmul_tn(a_s5, za, "dbb_s5", 512, tk, band=band)
    dc_band = _matmul_tn(xs, dy0, "dc_s5", 512, tk, band=band)
    dzh3, dlb = _hgrn_bwd(zh.reshape(nb, seq, 4 * HG_WIDTH), do.reshape(nb, seq, HG_WIDTH), sts, lb, nb, seq)
    dz, dx, dg_mix = _in_proj_bwd(dza, dzh3.reshape(t, 3 * HG_WIDTH), dzg, dzgt, dx1, x, g_mix, w_in, tm)
    dw_in = wgrad(u, dz, "dw_in")

    dbb_re = _diag_blocks(dbb_band[:S5_N], S5_STATE, S5_GROUP).reshape(S5_N, S5_GROUP)
    dbb_im = _diag_blocks(dbb_band[S5_N:], S5_STATE, S5_GROUP).reshape(S5_N, S5_GROUP)
    dc_re = _diag_blocks(dc_band[:S5_N], S5_STATE, S5_GROUP).transpose(0, 2, 1)
    dc_im = -_diag_blocks(dc_band[S5_N:], S5_STATE, S5_GROUP).transpose(0, 2, 1)
    da_re, da_im, dldt, db_re, db_im, dlogits = _params_bwd(
        a_re, a_im, ldt, b_re, b_im, sp["hg_lb_logits"],
        dlam[0].reshape(S5_N, 1), dlam[1].reshape(S5_N, 1), dbb_re, dbb_im, dlb)

    small = {"g_mix": dg_mix, "s5_a_re": da_re, "s5_a_im": da_im, "s5_log_dt": dldt, "s5_b_re": db_re,
             "s5_b_im": db_im, "s5_c_re": dc_re, "s5_c_im": dc_im, "s5_d": dd, "b_glu": db_glu,
             "hg_lb_logits": dlogits, "hg_norm_gain": dgain, "g_ffn": dg_ffn, "b_conv": db_conv,
             "g_final": dg_final}
    return dx.reshape(nb, seq, D_MODEL), dw_in, small, loss


def _exchange(name, operands):
    n = len(operands)
    out_shape = []
    for arr, per_peer in operands:
        blk = arr.shape[1:] if per_peer else arr.shape
        out_shape.append(_sds((N_DEV,) + tuple(blk), arr.dtype))

    def body(*refs):
        srcs, dsts = refs[:n], refs[n:2 * n]
        send_sems, recv_sems, local_sems = refs[2 * n:]
        x, y, c = lax.axis_index("x"), lax.axis_index("y"), lax.axis_index("c")
        me = 4 * x + 2 * y + c

        def flip(v, bit):
            return 1 - v if bit else v

        def source(i, slot):
            return srcs[i].at[slot] if operands[i][1] else srcs[i]

        local = [pltpu.make_async_copy(source(i, me), dsts[i].at[me], local_sems.at[i]) for i in range(n)]
        for cp in local:
            cp.start()
        remote, arrivals = [], []
        for k in range(1, N_DEV):
            peer = (flip(x, k & 4), flip(y, k & 2), flip(c, k & 1))
            slot = 4 * peer[0] + 2 * peer[1] + peer[2]
            for i in range(n):
                s = (k - 1) * n + i
                remote.append(pltpu.make_async_remote_copy(
                    src_ref=source(i, slot), dst_ref=dsts[i].at[me], send_sem=send_sems.at[s],
                    recv_sem=recv_sems.at[s], device_id=peer, device_id_type=pl.DeviceIdType.MESH))
                arrivals.append(pltpu.make_async_remote_copy(
                    src_ref=source(i, slot), dst_ref=dsts[i].at[slot], send_sem=send_sems.at[s],
                    recv_sem=recv_sems.at[s], device_id=peer, device_id_type=pl.DeviceIdType.MESH))
        for cp in remote:
            cp.start()
        for cp in arrivals:
            cp.wait_recv()
        for cp in remote:
            cp.wait_send()
        for cp in local:
            cp.wait()

    hbm = pl.BlockSpec(memory_space=pl.ANY)
    return pl.pallas_call(
        body, name=name, in_specs=[hbm] * n, out_specs=[hbm] * n, out_shape=out_shape,
        scratch_shapes=[pltpu.SemaphoreType.DMA(((N_DEV - 1) * n,)), pltpu.SemaphoreType.DMA(((N_DEV - 1) * n,)),
                        pltpu.SemaphoreType.DMA((n,))],
    )(*[arr for arr, _ in operands])


def _mesh_peers():
    x, y, c = lax.axis_index("x"), lax.axis_index("y"), lax.axis_index("c")
    peers = []
    for k in range(1, N_DEV):
        px, py, pc = (1 - x if k & 4 else x), (1 - y if k & 2 else y), (1 - c if k & 1 else c)
        peers.append((k, (px, py, pc), 4 * px + 2 * py + pc))
    return 4 * x + 2 * y + c, peers


_HBM = pl.BlockSpec(memory_space=pltpu.HBM)
_SEM = pl.BlockSpec(memory_space=pltpu.SEMAPHORE)


def _exchange_start(name, operands, after):
    n = len(operands)
    me = 4 * lax.axis_index("x") + 2 * lax.axis_index("y") + lax.axis_index("c")
    flags = [per_peer for _, per_peer in operands]
    srcs, lands = [], []
    for arr, per_peer in operands:
        own = lax.dynamic_index_in_dim(arr, me, 0, keepdims=True) if per_peer else arr[None]
        land = lax.dynamic_update_slice_in_dim(lax.empty((N_DEV,) + own.shape[1:], arr.dtype), own, me, 0)
        srcs.append(pltpu.with_memory_space_constraint(arr, pltpu.HBM))
        lands.append(pltpu.with_memory_space_constraint(land, pltpu.HBM))
    copies = (N_DEV - 1) * n

    def body(*refs):
        src_refs, land_refs = refs[:n], refs[n:2 * n]
        send_sems, recv_sems = refs[2 * n + 1], refs[2 * n + 2]
        token = refs[-1]
        my_slab, peers = _mesh_peers()
        for k, peer, slab in peers:
            for i in range(n):
                s = (k - 1) * n + i
                pltpu.make_async_remote_copy(
                    src_ref=src_refs[i].at[slab] if flags[i] else src_refs[i], dst_ref=land_refs[i].at[my_slab],
                    send_sem=send_sems.at[s], recv_sem=recv_sems.at[s], device_id=peer,
                    device_id_type=pl.DeviceIdType.MESH).start()
        token[...] = jnp.zeros_like(token)

    outs = pl.pallas_call(
        body, name=name,
        out_shape=(pltpu.SemaphoreType.DMA((copies,)), pltpu.SemaphoreType.DMA((copies,)),
                   *[pltpu.HBM(a.shape, a.dtype) for a in srcs], *[pltpu.HBM(a.shape, a.dtype) for a in lands],
                   _sds((SUBLANES, 128))),
        in_specs=[_HBM] * (2 * n) + [pl.BlockSpec(memory_space=pl.ANY)],
        out_specs=(_SEM, _SEM, *[_HBM] * (2 * n), pl.BlockSpec(memory_space=pltpu.VMEM)),
        input_output_aliases={i: 2 + i for i in range(2 * n)},
        compiler_params=pltpu.CompilerParams(has_side_effects=pltpu.SideEffectType.DATAFLOW_SIDE_EFFECTING),
    )(*srcs, *lands, after)
    state = (flags, outs[0], outs[1], outs[2:2 + n], outs[2 + n:2 + 2 * n])
    return state, outs[-1][0, 0]


def _exchange_wait(name, state, after):
    flags, send_sems, recv_sems, srcs, lands = state
    n = len(flags)

    def body(*refs):
        src_refs, land_refs = refs[:n], refs[n:2 * n]
        send_ref, recv_ref = refs[2 * n], refs[2 * n + 1]
        _, peers = _mesh_peers()
        for k, peer, slab in peers:
            for i in range(n):
                s = (k - 1) * n + i
                copy = pltpu.make_async_remote_copy(
                    src_ref=src_refs[i].at[slab] if flags[i] else src_refs[i], dst_ref=land_refs[i].at[slab],
                    send_sem=send_ref.at[s], recv_sem=recv_ref.at[s], device_id=peer,
                    device_id_type=pl.DeviceIdType.MESH)
                copy.wait_send()
                copy.wait_recv()

    outs = pl.pallas_call(
        body, name=name,
        out_shape=(*[pltpu.HBM(a.shape, a.dtype) for a in srcs], *[pltpu.HBM(a.shape, a.dtype) for a in lands]),
        in_specs=[_HBM] * (2 * n) + [_SEM, _SEM, pl.BlockSpec(memory_space=pl.ANY)],
        out_specs=tuple([_HBM] * (2 * n)),
        input_output_aliases={i: i for i in range(2 * n)},
        compiler_params=pltpu.CompilerParams(has_side_effects=pltpu.SideEffectType.DATAFLOW_SIDE_EFFECTING),
    )(*srcs, *lands, send_sems, recv_sems, after)
    return list(outs[n:])


def _join_cols(parts, name, tr):
    _, r, c = parts.shape

    def body(p_ref, o_ref):
        for j in range(N_DEV):
            o_ref[:, j * c:(j + 1) * c] = p_ref[j]

    return _pcall(body, name, (r // tr,), [pl.BlockSpec((N_DEV, tr, c), lambda i: (0, i, 0))],
                  pl.BlockSpec((tr, N_DEV * c), lambda i: (i, 0)), _sds((r, N_DEV * c), parts.dtype))(parts)


def _split_cols(full, name, tr):
    r, c = full.shape[0], full.shape[1] // N_DEV

    def body(f_ref, o_ref):
        for j in range(N_DEV):
            o_ref[j] = f_ref[:, j * c:(j + 1) * c]

    return _pcall(body, name, (r // tr,), [pl.BlockSpec((tr, N_DEV * c), lambda i: (i, 0))],
                  pl.BlockSpec((N_DEV, tr, c), lambda i: (0, i, 0)), _sds((N_DEV, r, c), full.dtype))(full)


def _adamw(parts, w, m, v, name, tile):
    rows, cols = w.shape

    def body(p_ref, w_ref, m_ref, v_ref, g_out, d_out, m_out, v_out):
        g = p_ref[0].astype(F32)
        for k in range(1, N_DEV):
            g = g + p_ref[k].astype(F32)
        m1 = ADAM_B1 * m_ref[...] + (1.0 - ADAM_B1) * g
        v1 = ADAM_B2 * v_ref[...] + (1.0 - ADAM_B2) * (g * g)
        m_hat = m1 / (1.0 - ADAM_B1 ** ADAM_STEP)
        v_hat = v1 / (1.0 - ADAM_B2 ** ADAM_STEP)
        g_out[...] = g
        d_out[...] = -ADAM_LR * (m_hat / (jnp.sqrt(v_hat) + ADAM_EPS) + ADAM_WD * w_ref[...])
        m_out[...] = m1
        v_out[...] = v1

    row = pl.BlockSpec((tile, cols), lambda i: (i, 0))
    return _pcall(body, name, (rows // tile,),
                  [pl.BlockSpec((N_DEV, tile, cols), lambda i: (0, i, 0)), row, row, row],
                  [row, row, row, row], [_sds((rows, cols))] * 4)(parts, w, m, v)


BIG = {
    "w_in": ((D_MODEL, N_IN // N_DEV), True, 256),
    "w_glu": ((S5_WIDTH // N_DEV, S5_WIDTH), False, S5_WIDTH // N_DEV),
    "w_pa": ((S5_WIDTH, D_MODEL // N_DEV), True, S5_WIDTH),
    "w_pb": ((HG_WIDTH, D_MODEL // N_DEV), True, HG_WIDTH),
    "w_out": ((D_MODEL // N_DEV, D_MODEL), False, D_MODEL // N_DEV),
    "w_up": ((D_MODEL, 2 * D_FF // N_DEV), True, 256),
    "w_conv": ((CONV_W, 2 * D_FF // N_DEV), True, CONV_W),
    "w_down": ((D_FF // N_DEV, D_MODEL), False, D_FF // N_DEV // 2),
}
UNALIGNED_COLS = ("w_in", "w_up", "w_conv")


def _join_shards(n, parts):
    (a, b), by_cols, _ = BIG[n]
    if not by_cols:
        return parts.reshape(N_DEV * a, b)
    if n in UNALIGNED_COLS:
        return _join_cols(parts, "join_" + n, min(a, 256))
    return parts.transpose(1, 0, 2).reshape(a, N_DEV * b)


def _split_shards(n, full):
    (a, b), by_cols, _ = BIG[n]
    if not by_cols:
        return full.reshape(N_DEV, a, b)
    if n in UNALIGNED_COLS:
        return _split_cols(full, "split_" + n, min(a, 256))
    return full.reshape(a, N_DEV, b).transpose(1, 0, 2)


def _pack_small(d, extra):
    flat = jnp.concatenate([d[n].reshape(-1) for n, _ in SMALL] + [extra.reshape(-1)])
    return jnp.pad(flat, (0, SMALL_ROWS * PACK_W - flat.shape[0])).reshape(SMALL_ROWS, PACK_W)


def _unpack_small(p):
    flat = p.reshape(-1)
    out, off = {}, 0
    for n, shp in SMALL:
        size = math.prod(shp)
        out[n] = flat[off:off + size].reshape(shp)
        off += size
    return out, flat[off]


def kernel(x, g_mix, w_in, s5_a_re, s5_a_im, s5_log_dt, s5_b_re, s5_b_im, s5_c_re, s5_c_im, s5_d, w_glu, b_glu, hg_lb_logits, hg_norm_gain, w_pa, w_pb, w_out, g_ffn, w_up, w_conv, b_conv, w_down, g_final, loss_target, m_g_mix, m_w_in, m_s5_a_re, m_s5_a_im, m_s5_log_dt, m_s5_b_re, m_s5_b_im, m_s5_c_re, m_s5_c_im, m_s5_d, m_w_glu, m_b_glu, m_hg_lb_logits, m_hg_norm_gain, m_w_pa, m_w_pb, m_w_out, m_g_ffn, m_w_up, m_w_conv, m_b_conv, m_w_down, m_g_final, v_g_mix, v_w_in, v_s5_a_re, v_s5_a_im, v_s5_log_dt, v_s5_b_re, v_s5_b_im, v_s5_c_re, v_s5_c_im, v_s5_d, v_w_glu, v_b_glu, v_hg_lb_logits, v_hg_norm_gain, v_w_pa, v_w_pb, v_w_out, v_g_ffn, v_w_up, v_w_conv, v_b_conv, v_w_down, v_g_final):
    given = dict(locals())
    big_names = list(BIG)
    small_names = [n for n, _ in SMALL]

    pay = {n: given[n][0] if n == "w_conv" else given[n][0].astype(BF16) for n in big_names}
    rest = [n for n in big_names if n != "w_in"]
    got_in, = _exchange("gather_w_in", [(pay["w_in"], False)])
    gather, started = _exchange_start("gather_rest_start", [(pay[n], False) for n in rest], got_in)

    def rest_weights(after):
        got = _exchange_wait("gather_rest_wait", gather, after)
        return {n: _join_shards(n, g) for n, g in zip(rest, got)}

    in_flight = []

    def emit(grads):
        names = list(grads)
        state, sent = _exchange_start("grads_" + names[0] + "_start",
                                      [(_split_shards(n, grads[n]), True) for n in names], grads[names[0]])
        in_flight.append((names, state))
        return sent

    sp = {n: (given[n] if n in ("g_final", "hg_lb_logits") else given[n][0]) for n in small_names}
    sp["g_mix"] = sp["g_mix"] + started
    dx, dw_in, small, loss = _local_step(x, loss_target, _join_shards("w_in", got_in), rest_weights, sp, emit)

    parts = dict(zip(["w_in", "small"],
                     _exchange("exchange_last", [(_split_shards("w_in", dw_in), True), (_pack_small(small, loss), False)])))
    for names, state in in_flight:
        parts.update(zip(names, _exchange_wait("grads_" + names[0] + "_wait", state, parts["small"])))

    res = {}
    for n in big_names:
        shard, _, tile = BIG[n]
        r4 = _adamw(parts[n], given[n][0], given["m_" + n][0], given["v_" + n][0], "adamw_" + n, tile)
        res[n] = [r.reshape((1,) + shard) for r in r4]
    zero = jnp.zeros((1,), F32)
    rs4 = _adamw(parts["small"], _pack_small({n: given[n] for n in small_names}, zero),
                 _pack_small({n: given["m_" + n] for n in small_names}, zero),
                 _pack_small({n: given["v_" + n] for n in small_names}, zero), "adamw_small", SMALL_ROWS)
    small4 = [_unpack_small(r) for r in rs4]
    for n in small_names:
        res[n] = [us[n] for us, _ in small4]
    total_loss = small4[0][1]
    return (total_loss, dx, *[res[n][0] for n in WEIGHT_ORDER], *[res[n][1] for n in WEIGHT_ORDER],
            *[res[n][2] for n in WEIGHT_ORDER], *[res[n][3] for n in WEIGHT_ORDER])
```

```python
import functools
import math

import jax
import jax.numpy as jnp
from jax import lax
from jax.experimental import pallas as pl
from jax.experimental.pallas import tpu as pltpu

F32 = jnp.float32
BF16 = jnp.bfloat16

D_MODEL = 1024
S5_WIDTH = 512
S5_GROUP = 16
S5_GROUPS = 32
S5_STATE = 64
S5_N = S5_GROUPS * S5_STATE
HG_WIDTH = 512
HG_HEAD = 128
HG_HEADS = 4
D_FF = 2816
CONV_W = 3
CHUNK = 64
N_IN = S5_WIDTH + 4 * HG_WIDTH + 2 * D_MODEL
EPS = 1e-6
QSCALE = HG_HEAD ** -0.5

ADAM_LR = 0.001
ADAM_B1 = 0.9
ADAM_B2 = 0.999
ADAM_EPS = 1e-08
ADAM_WD = 0.01
ADAM_STEP = 10

N_DEV = 8
V7X_VMEM_BYTES = 64 * 1024 * 1024
VMEM_LIMIT = V7X_VMEM_BYTES * 7 // 8
SUBLANES = 8
PACK_W = 1024

SMALL = (
    ("g_mix", (1, D_MODEL)),
    ("s5_a_re", (1, S5_GROUPS, S5_STATE)),
    ("s5_a_im", (1, S5_GROUPS, S5_STATE)),
    ("s5_log_dt", (1, S5_GROUPS)),
    ("s5_b_re", (1, S5_GROUPS, S5_STATE, S5_GROUP)),
    ("s5_b_im", (1, S5_GROUPS, S5_STATE, S5_GROUP)),
    ("s5_c_re", (1, S5_GROUPS, S5_GROUP, S5_STATE)),
    ("s5_c_im", (1, S5_GROUPS, S5_GROUP, S5_STATE)),
    ("s5_d", (1, S5_WIDTH)),
    ("b_glu", (1, S5_WIDTH)),
    ("hg_lb_logits", (2, HG_WIDTH)),
    ("hg_norm_gain", (1, HG_WIDTH)),
    ("g_ffn", (1, D_MODEL)),
    ("b_conv", (1, 2 * D_FF)),
    ("g_final", (D_MODEL,)),
)
SMALL_ROWS = 144
WEIGHT_ORDER = ("g_mix", "w_in", "s5_a_re", "s5_a_im", "s5_log_dt", "s5_b_re", "s5_b_im", "s5_c_re", "s5_c_im",
                "s5_d", "w_glu", "b_glu", "hg_lb_logits", "hg_norm_gain", "w_pa", "w_pb", "w_out", "g_ffn",
                "w_up", "w_conv", "b_conv", "w_down", "g_final")


def _pcall(body, name, grid, in_specs, out_specs, out_shape, scratch=()):
    return pl.pallas_call(
        body, name=name, grid=grid, in_specs=in_specs, out_specs=out_specs, out_shape=out_shape,
        scratch_shapes=list(scratch),
        compiler_params=pltpu.CompilerParams(dimension_semantics=("arbitrary",) * len(grid),
                                             vmem_limit_bytes=VMEM_LIMIT),
    )


def _full(shape):
    return pl.BlockSpec(shape, lambda *_: (0,) * len(shape))


def _sds(shape, dtype=F32):
    return jax.ShapeDtypeStruct(shape, dtype)


def _dot(a, b):
    return jnp.dot(a.astype(BF16), b.astype(BF16), preferred_element_type=F32)


def _dot_nt(a, b):
    return lax.dot_general(a.astype(BF16), b.astype(BF16), (((1,), (1,)), ((), ())), preferred_element_type=F32)


def _dot_tn(a, b):
    return lax.dot_general(a.astype(BF16), b.astype(BF16), (((0,), (0,)), ((), ())), preferred_element_type=F32)


def _hdot(a, b):
    return jnp.dot(a, b, preferred_element_type=F32, precision=lax.Precision.HIGHEST)


def _hdot_tn(a, b):
    return lax.dot_general(a, b, (((0,), (0,)), ((), ())), preferred_element_type=F32,
                           precision=lax.Precision.HIGHEST)


def _sigmoid(x):
    return jax.nn.sigmoid(x)


GELU_C = math.sqrt(2.0 / math.pi)
GELU_A = 0.044715


def _gelu(x):
    return 0.5 * x * (1.0 + jnp.tanh(GELU_C * (x + GELU_A * (x * x * x))))


def _gelu_grad(x):
    t = jnp.tanh(GELU_C * (x + GELU_A * (x * x * x)))
    return 0.5 * (1.0 + t) + 0.5 * x * (1.0 - t * t) * (GELU_C * (1.0 + 3.0 * GELU_A * x * x))


def _cumsum_rows(v, reverse=False):
    n = v.shape[0]
    row = lax.broadcasted_iota(jnp.int32, v.shape, 0)
    s = 1
    while s < n:
        if reverse:
            v = v + jnp.where(row < n - s, pltpu.roll(v, n - s, axis=0), 0.0)
        else:
            v = v + jnp.where(row >= s, pltpu.roll(v, s, axis=0), 0.0)
        s *= 2
    return v


def _token_tile(seq):
    return min(256, seq)


def _s5_disc(a_re, a_im, ldt, b_re, b_im):
    dt = jnp.exp(ldt)
    mag = jnp.exp(a_re * dt)
    ang = a_im * dt
    lb_re = mag * jnp.cos(ang)
    lb_im = mag * jnp.sin(ang)
    den = a_re * a_re + a_im * a_im
    n_re = lb_re - 1.0
    n_im = lb_im
    co_re = (n_re * a_re + n_im * a_im) / den
    co_im = (n_im * a_re - n_re * a_im) / den
    bb_re = co_re * b_re - co_im * b_im
    bb_im = co_re * b_im + co_im * b_re
    return lb_re, lb_im, bb_re, bb_im


def _params_fwd(a_re, a_im, ldt, b_re, b_im, logits):
    def body(are, aim, ld, bre, bim, lg, lr_o, li_o, bbr_o, bbi_o, lb_o):
        lr, li, bbr, bbi = _s5_disc(are[...], aim[...], ld[...], bre[...], bim[...])
        lr_o[...] = lr
        li_o[...] = li
        bbr_o[...] = bbr
        bbi_o[...] = bbi
        lb_o[...] = _sigmoid(lg[0:1, :] - lg[1:2, :])

    col, mat = (S5_N, 1), (S5_N, S5_GROUP)
    return _pcall(body, "params_fwd", (1,),
                  [_full(col), _full(col), _full(col), _full(mat), _full(mat), _full((2, HG_WIDTH))],
                  [_full(col), _full(col), _full(mat), _full(mat), _full((1, HG_WIDTH))],
                  [_sds(col), _sds(col), _sds(mat), _sds(mat), _sds((1, HG_WIDTH))])(a_re, a_im, ldt, b_re, b_im, logits)


def _params_bwd(a_re, a_im, ldt, b_re, b_im, logits, dlr, dli, dbbr, dbbi, dlb):
    def body(are, aim, ld, bre, bim, lg, dlr_r, dli_r, dbbr_r, dbbi_r, dlb_r,
             dare_o, daim_o, dld_o, dbre_o, dbim_o, dlg_o):
        _, vjp = jax.vjp(_s5_disc, are[...], aim[...], ld[...], bre[...], bim[...])
        dare, daim, dld, dbre, dbim = vjp((dlr_r[...], dli_r[...], dbbr_r[...], dbbi_r[...]))
        dare_o[...] = dare
        daim_o[...] = daim
        dbre_o[...] = dbre
        dbim_o[...] = dbim
        for g in range(S5_GROUPS):
            dld_o[g:g + 1, :] = jnp.sum(dld[g * S5_STATE:(g + 1) * S5_STATE, :], axis=0, keepdims=True)
        lb = _sigmoid(lg[0:1, :] - lg[1:2, :])
        d0 = dlb_r[...] * lb * (1.0 - lb)
        dlg_o[0:1, :] = d0
        dlg_o[1:2, :] = -d0

    col, mat = (S5_N, 1), (S5_N, S5_GROUP)
    return _pcall(body, "params_bwd", (1,),
                  [_full(col), _full(col), _full(col), _full(mat), _full(mat), _full((2, HG_WIDTH)),
                   _full(col), _full(col), _full(mat), _full(mat), _full((1, HG_WIDTH))],
                  [_full(col), _full(col), _full((S5_GROUPS, 1)), _full(mat), _full(mat), _full((2, HG_WIDTH))],
                  [_sds(col), _sds(col), _sds((S5_GROUPS, 1)), _sds(mat), _sds(mat), _sds((2, HG_WIDTH))],
                  )(a_re, a_im, ldt, b_re, b_im, logits, dlr, dli, dbbr, dbbi, dlb)


def _blockdiag(m):
    g, r, c = m.shape
    eye = jnp.eye(g, dtype=m.dtype)
    return (m[:, :, None, :] * eye[:, None, :, None]).reshape(g * r, g * c)


def _diag_blocks(band, r, c):
    g, nb = band.shape[0] // r, band.shape[1] // c
    on_diag = (jnp.arange(g) % nb)[:, None, None, None] == jnp.arange(nb)[None, None, :, None]
    return jnp.sum(jnp.where(on_diag, band.reshape(g, r, nb, c), 0.0), axis=2)


def _in_proj(x, g_mix, w_in, tm):
    t = x.shape[0]

    def body(x_ref, g_ref, w_ref, u_ref, za_ref, zh_ref, zg_ref):
        xv = x_ref[...]
        r = lax.rsqrt(jnp.mean(xv * xv, axis=-1, keepdims=True) + EPS)
        u = (xv * r * g_ref[...]).astype(BF16)
        u_ref[...] = u
        za_ref[...] = jnp.dot(u, w_ref[:, 0:S5_WIDTH], preferred_element_type=F32)
        zh_ref[...] = jnp.dot(u, w_ref[:, S5_WIDTH:S5_WIDTH + 4 * HG_WIDTH], preferred_element_type=F32)
        zg_ref[...] = jnp.dot(u, w_ref[:, S5_WIDTH + 4 * HG_WIDTH:], preferred_element_type=F32)

    row = lambda w: pl.BlockSpec((tm, w), lambda i: (i, 0))
    return _pcall(body, "in_proj", (t // tm,),
                  [row(D_MODEL), _full((1, D_MODEL)), _full((D_MODEL, N_IN))],
                  [row(D_MODEL), row(S5_WIDTH), row(4 * HG_WIDTH), row(2 * D_MODEL)],
                  [_sds((t, D_MODEL), BF16), _sds((t, S5_WIDTH)), _sds((t, 4 * HG_WIDTH)), _sds((t, 2 * D_MODEL))],
                  )(x, g_mix, w_in)


S5_LANES = 512


def _s5_fwd(za, bbt, lam, ct, dskip, nb, seq, ts):
    t = za.shape[0]
    nts = seq // ts

    def body(za_ref, bbt_ref, lam_ref, ct_ref, d_ref, xs_ref, y_ref, st_ref):
        @pl.when(pl.program_id(1) == 0)
        def _():
            st_ref[...] = jnp.zeros_like(st_ref)

        zav = za_ref[...]
        xs_ref[...] = _dot(zav, bbt_ref[...])
        for cc in range(S5_N // S5_LANES):
            re = slice(cc * S5_LANES, (cc + 1) * S5_LANES)
            im = slice(S5_N + cc * S5_LANES, S5_N + (cc + 1) * S5_LANES)
            lr = lam_ref[0:1, re]
            li = lam_ref[1:2, re]

            def step(i, carry):
                xr, xi = carry
                br = xs_ref[pl.ds(i, 1), re]
                bi = xs_ref[pl.ds(i, 1), im]
                nxr = lr * xr - li * xi + br
                nxi = lr * xi + li * xr + bi
                xs_ref[pl.ds(i, 1), re] = nxr
                xs_ref[pl.ds(i, 1), im] = nxi
                return nxr, nxi

            xr, xi = lax.fori_loop(0, ts, step, (st_ref[0:1, re], st_ref[1:2, re]), unroll=8)
            st_ref[0:1, re] = xr
            st_ref[1:2, re] = xi
        y_ref[...] = _dot(xs_ref[...], ct_ref[...]) + d_ref[...] * zav

    tok = lambda w: pl.BlockSpec((ts, w), lambda b, j: (b * nts + j, 0))
    return _pcall(body, "s5_fwd", (nb, nts),
                  [tok(S5_WIDTH), _full((S5_WIDTH, 2 * S5_N)), _full((2, S5_N)), _full((2 * S5_N, S5_WIDTH)),
                   _full((1, S5_WIDTH))],
                  [tok(2 * S5_N), tok(S5_WIDTH)],
                  [_sds((t, 2 * S5_N)), _sds((t, S5_WIDTH))],
                  scratch=[pltpu.VMEM((2, S5_N), F32)])(za, bbt, lam, ct, dskip)


def _hgrn_gates(zq, zf, lbh):
    sf = _sigmoid(zf)
    f = lbh + (1.0 - lbh) * sf
    sq = _sigmoid(zq)
    qa = zq * sq * QSCALE
    bc = _cumsum_rows(jnp.log(f))
    bm = bc[CHUNK // 2 - 1:CHUNK // 2, :]
    bl = bc[CHUNK - 1:CHUNK, :]
    return sf, f, sq, qa, bc, bm, bl


def _hgrn_fwd(zh, lb, nb, seq):
    nc = seq // CHUNK

    def body(zh_ref, lb_ref, o_ref, sts_ref, st_ref):
        @pl.when(pl.program_id(0) == 0)
        def _():
            st_ref[...] = jnp.zeros_like(st_ref)

        causal = (lax.broadcasted_iota(jnp.int32, (CHUNK, CHUNK), 0)
                  >= lax.broadcasted_iota(jnp.int32, (CHUNK, CHUNK), 1))
        for b in range(nb):
            for h in range(HG_HEADS):
                hs = slice(h * HG_HEAD, (h + 1) * HG_HEAD)
                zq = zh_ref[b, :, h * HG_HEAD:(h + 1) * HG_HEAD]
                zf = zh_ref[b, :, HG_WIDTH + h * HG_HEAD:HG_WIDTH + (h + 1) * HG_HEAD]
                zi = zh_ref[b, :, 2 * HG_WIDTH + h * HG_HEAD:2 * HG_WIDTH + (h + 1) * HG_HEAD]
                _, f, _, qa, bc, bm, bl = _hgrn_gates(zq, zf, lb_ref[:, hs])
                k = 1.0 - f
                qt = qa * jnp.exp(bc - bm)
                kt = k * jnp.exp(bm - bc)
                qb = qa * jnp.exp(bc)
                kd = k * jnp.exp(bl - bc)
                st = st_ref[b, h]
                sts_ref[b, 0, h] = st
                a = jnp.where(causal, _dot_nt(qt, kt), 0.0)
                o_ref[b, :, hs] = _dot(a, zi) + _dot_nt(qb, st)
                st_ref[b, h] = st * jnp.exp(bl) + _dot_tn(zi, kd)

    return _pcall(body, "hgrn_fwd", (nc,),
                  [pl.BlockSpec((nb, CHUNK, 4 * HG_WIDTH), lambda c: (0, c, 0)), _full((1, HG_WIDTH))],
                  [pl.BlockSpec((nb, CHUNK, HG_WIDTH), lambda c: (0, c, 0)),
                   pl.BlockSpec((nb, 1, HG_HEADS, HG_HEAD, HG_HEAD), lambda c: (0, c, 0, 0, 0))],
                  [_sds((nb, seq, HG_WIDTH)), _sds((nb, nc, HG_HEADS, HG_HEAD, HG_HEAD))],
                  scratch=[pltpu.VMEM((nb, HG_HEADS, HG_HEAD, HG_HEAD), F32)])(zh, lb)


def _head_rms(o):
    parts = []
    for h in range(HG_HEADS):
        oh = o[:, h * HG_HEAD:(h + 1) * HG_HEAD]
        r = lax.rsqrt(jnp.mean(oh * oh, axis=-1, keepdims=True) + EPS)
        parts.append(jnp.broadcast_to(r, oh.shape))
    return jnp.concatenate(parts, axis=1)


def _head_mean(v):
    parts = []
    for h in range(HG_HEADS):
        vh = v[:, h * HG_HEAD:(h + 1) * HG_HEAD]
        parts.append(jnp.broadcast_to(jnp.mean(vh, axis=-1, keepdims=True), vh.shape))
    return jnp.concatenate(parts, axis=1)


def _mix_fwd(x, y0, o, zh, zgt, w_glu, b_glu, gain, w_pa, w_pb, w_out, g_ffn, tm):
    t = x.shape[0]

    def body(x_ref, y0_ref, o_ref, zg_ref, zgt_ref, wglu_ref, bglu_ref, gain_ref, wpa_ref, wpb_ref, wout_ref,
             gffn_ref, x1_ref, u2_ref, pa_ref, pb_ref, ya2_ref, yb_ref):
        ya1 = _gelu(y0_ref[...])
        s = _sigmoid(_dot(ya1, wglu_ref[...]) + bglu_ref[...])
        ya2 = (ya1 * s).astype(BF16)
        ov = o_ref[...]
        zg = zg_ref[...]
        yb = (ov * _head_rms(ov) * gain_ref[...] * (zg * _sigmoid(zg))).astype(BF16)
        ya2_ref[...] = ya2
        yb_ref[...] = yb
        pa = jnp.dot(ya2, wpa_ref[...], preferred_element_type=F32)
        pb = jnp.dot(yb, wpb_ref[...], preferred_element_type=F32)
        pa_ref[...] = pa
        pb_ref[...] = pb
        m = _sigmoid(zgt_ref[:, 0:D_MODEL]) * pa + _sigmoid(zgt_ref[:, D_MODEL:]) * pb
        x1 = x_ref[...] + _dot(m, wout_ref[...])
        x1_ref[...] = x1
        r = lax.rsqrt(jnp.mean(x1 * x1, axis=-1, keepdims=True) + EPS)
        u2_ref[...] = (x1 * r * gffn_ref[...]).astype(BF16)

    row = lambda w: pl.BlockSpec((tm, w), lambda i: (i, 0))
    return _pcall(body, "mix_fwd", (t // tm,),
                  [row(D_MODEL), row(S5_WIDTH), row(HG_WIDTH), pl.BlockSpec((tm, HG_WIDTH), lambda i: (i, 3)),
                   row(2 * D_MODEL), _full((S5_WIDTH, S5_WIDTH)), _full((1, S5_WIDTH)), _full((1, HG_WIDTH)),
                   _full((S5_WIDTH, D_MODEL)), _full((HG_WIDTH, D_MODEL)), _full((D_MODEL, D_MODEL)),
                   _full((1, D_MODEL))],
                  [row(D_MODEL), row(D_MODEL), row(D_MODEL), row(D_MODEL), row(S5_WIDTH), row(HG_WIDTH)],
                  [_sds((t, D_MODEL)), _sds((t, D_MODEL), BF16), _sds((t, D_MODEL)), _sds((t, D_MODEL)),
                   _sds((t, S5_WIDTH), BF16), _sds((t, HG_WIDTH), BF16)],
                  )(x, y0, o, zh, zgt, w_glu, b_glu, gain, w_pa, w_pb, w_out, g_ffn)


FF_COLS = 256
FF_UP_TILE = 1408


def _ffn_up(u2, w_up, tm):
    t = u2.shape[0]
    n = 2 * D_FF

    def body(u_ref, w_ref, h_ref):
        h_ref[...] = jnp.dot(u_ref[...], w_ref[...], preferred_element_type=F32)

    return _pcall(body, "ffn_up", (n // FF_UP_TILE, t // tm),
                  [pl.BlockSpec((tm, D_MODEL), lambda j, i: (i, 0)),
                   pl.BlockSpec((D_MODEL, FF_UP_TILE), lambda j, i: (0, j))],
                  pl.BlockSpec((tm, FF_UP_TILE), lambda j, i: (i, j)),
                  _sds((t, n)))(u2, w_up)


def _conv_cols(h_ref, halo_ref, valid, wc_ref, bc_ref, c0, tm):
    cs = slice(c0, c0 + FF_COLS)
    cur = h_ref[:, cs]
    prev = jnp.where(valid, halo_ref[:, cs], 0.0)
    full = jnp.concatenate([prev, cur], axis=0)
    h1 = pltpu.roll(full, 1, axis=0)[SUBLANES:]
    h2 = pltpu.roll(full, 2, axis=0)[SUBLANES:]
    hc = h2 * wc_ref[0:1, cs] + h1 * wc_ref[1:2, cs] + cur * wc_ref[2:3, cs] + bc_ref[:, cs]
    return hc, cur, h1, h2


def _ffn_down_loss(h, x1, tgt, w_conv, b_conv, w_down, g_final, seq, tm):
    t = h.shape[0]
    tps = seq // tm
    n = 2 * D_FF

    def body(h_ref, halo_ref, x1_ref, tgt_ref, wc_ref, bc_ref, wd_ref, gf_ref,
             a_ref, dx2_ref, dx2b_ref, loss_ref, dgf_ref):
        i = pl.program_id(0)

        @pl.when(i == 0)
        def _():
            loss_ref[...] = jnp.zeros_like(loss_ref)
            dgf_ref[...] = jnp.zeros_like(dgf_ref)

        valid = (i % tps) != 0
        x2 = x1_ref[...]
        for j in range(D_FF // FF_COLS):
            gate = _conv_cols(h_ref, halo_ref, valid, wc_ref, bc_ref, j * FF_COLS, tm)[0]
            val = _conv_cols(h_ref, halo_ref, valid, wc_ref, bc_ref, D_FF + j * FF_COLS, tm)[0]
            a = (gate * _sigmoid(gate) * val).astype(BF16)
            a_ref[:, j * FF_COLS:(j + 1) * FF_COLS] = a
            x2 = x2 + jnp.dot(a, wd_ref[j * FF_COLS:(j + 1) * FF_COLS, :], preferred_element_type=F32)
        r = lax.rsqrt(jnp.mean(x2 * x2, axis=-1, keepdims=True) + EPS)
        xn = x2 * r
        g = gf_ref[...]
        e = xn * g - tgt_ref[...]
        loss_ref[...] += (0.5 / D_MODEL) * jnp.sum(e * e).reshape(1, 1)
        dy = e * (1.0 / D_MODEL)
        dgf_ref[...] += jnp.sum(dy * xn, axis=0, keepdims=True)
        dxn = dy * g
        dx2 = r * (dxn - xn * jnp.mean(dxn * xn, axis=-1, keepdims=True))
        dx2_ref[...] = dx2
        dx2b_ref[...] = dx2.astype(BF16)

    row = lambda w: pl.BlockSpec((tm, w), lambda i: (i, 0))
    halo = pl.BlockSpec((SUBLANES, n), lambda i: (jnp.maximum(i * (tm // SUBLANES) - 1, 0), 0))
    return _pcall(body, "ffn_down_loss", (t // tm,),
                  [row(n), halo, row(D_MODEL), row(D_MODEL), _full((CONV_W, n)), _full((1, n)),
                   _full((D_FF, D_MODEL)), _full((1, D_MODEL))],
                  [row(D_FF), row(D_MODEL), row(D_MODEL), _full((1, 1)), _full((1, D_MODEL))],
                  [_sds((t, D_FF), BF16), _sds((t, D_MODEL)), _sds((t, D_MODEL), BF16), _sds((1, 1)),
                   _sds((1, D_MODEL))],
                  )(h, h, x1, tgt, w_conv, b_conv, w_down, g_final)


def _wgrad(a, b, name, tn, out_dtype=F32, transpose_out=False, band=None):
    t, m = a.shape
    n = b.shape[1] if band is None else band
    nbands = 1 if band is None else b.shape[1] // band

    def body(a_ref, b_ref, o_ref):
        r = _dot_tn(a_ref[...], b_ref[...])
        o_ref[...] = (r.T if transpose_out else r).astype(out_dtype)

    if transpose_out:
        out_spec, out_shape = pl.BlockSpec((n, tn), lambda i: (0, i)), _sds((n, m), out_dtype)
    else:
        out_spec, out_shape = pl.BlockSpec((tn, n), lambda i: (i, 0)), _sds((m, n), out_dtype)
    return _pcall(body, name, (m // tn,),
                  [pl.BlockSpec((t, tn), lambda i: (0, i)), pl.BlockSpec((t, n), lambda i: (0, i % nbands))],
                  out_spec, out_shape)(a, b)


def _ffn_bwd_act(dx2, h, w_conv, b_conv, w_down, seq, tm):
    t = h.shape[0]
    tps = seq // tm
    n = 2 * D_FF

    def body(dx2_ref, h_ref, halo_ref, wc_ref, bc_ref, wd_ref, dhc_ref, dwc_ref, dbc_ref):
        i = pl.program_id(0)

        @pl.when(i == 0)
        def _():
            dwc_ref[...] = jnp.zeros_like(dwc_ref)
            dbc_ref[...] = jnp.zeros_like(dbc_ref)

        valid = (i % tps) != 0
        dx2 = dx2_ref[...].astype(BF16)

        def conv_grads(c0, dhc, taps):
            cs = slice(c0, c0 + FF_COLS)
            dhc_ref[:, cs] = dhc
            dbc_ref[:, cs] += jnp.sum(dhc, axis=0, keepdims=True)
            for w, tap in enumerate(taps):
                dwc_ref[w:w + 1, cs] += jnp.sum(tap * dhc, axis=0, keepdims=True)

        for j in range(D_FF // FF_COLS):
            gate, g0, g1, g2 = _conv_cols(h_ref, halo_ref, valid, wc_ref, bc_ref, j * FF_COLS, tm)
            val, v0, v1, v2 = _conv_cols(h_ref, halo_ref, valid, wc_ref, bc_ref, D_FF + j * FF_COLS, tm)
            da = _dot_nt(dx2, wd_ref[j * FF_COLS:(j + 1) * FF_COLS, :])
            sg = _sigmoid(gate)
            conv_grads(j * FF_COLS, da * val * (sg * (1.0 + gate * (1.0 - sg))), (g2, g1, g0))
            conv_grads(D_FF + j * FF_COLS, da * (gate * sg), (v2, v1, v0))

    row = lambda w: pl.BlockSpec((tm, w), lambda i: (i, 0))
    halo = pl.BlockSpec((SUBLANES, n), lambda i: (jnp.maximum(i * (tm // SUBLANES) - 1, 0), 0))
    return _pcall(body, "ffn_bwd_act", (t // tm,),
                  [row(D_MODEL), row(n), halo, _full((CONV_W, n)), _full((1, n)), _full((D_FF, D_MODEL))],
                  [row(n), _full((CONV_W, n)), _full((1, n))],
                  [_sds((t, n)), _sds((CONV_W, n)), _sds((1, n))],
                  )(dx2, h, h, w_conv, b_conv, w_down)


def _ffn_bwd_up(dhc, dx2, x1, w_conv, w_up, g_ffn, seq, tm):
    t = dhc.shape[0]
    tps = seq // tm
    n = 2 * D_FF
    last = t // SUBLANES - 1

    def body(dhc_ref, halo_ref, dx2_ref, x1_ref, wc_ref, wu_ref, gf_ref, dh_ref, dx1_ref, dx1b_ref, dgf_ref):
        i = pl.program_id(0)

        @pl.when(i == 0)
        def _():
            dgf_ref[...] = jnp.zeros_like(dgf_ref)

        valid = ((i + 1) % tps) != 0
        du2 = jnp.zeros((tm, D_MODEL), F32)
        for j in range(n // FF_COLS):
            cs = slice(j * FF_COLS, (j + 1) * FF_COLS)
            cur = dhc_ref[:, cs]
            nxt = jnp.where(valid, halo_ref[:, cs], 0.0)
            full = jnp.concatenate([cur, nxt], axis=0)
            d1 = pltpu.roll(full, tm + SUBLANES - 1, axis=0)[:tm]
            d2 = pltpu.roll(full, tm + SUBLANES - 2, axis=0)[:tm]
            dh = (cur * wc_ref[2:3, cs] + d1 * wc_ref[1:2, cs] + d2 * wc_ref[0:1, cs]).astype(BF16)
            dh_ref[:, cs] = dh
            du2 = du2 + _dot_nt(dh, wu_ref[:, cs])
        x1 = x1_ref[...]
        r = lax.rsqrt(jnp.mean(x1 * x1, axis=-1, keepdims=True) + EPS)
        xn = x1 * r
        dgf_ref[...] += jnp.sum(du2 * xn, axis=0, keepdims=True)
        dxn = du2 * gf_ref[...]
        dx1 = dx2_ref[...] + r * (dxn - xn * jnp.mean(dxn * xn, axis=-1, keepdims=True))
        dx1_ref[...] = dx1
        dx1b_ref[...] = dx1.astype(BF16)

    row = lambda w: pl.BlockSpec((tm, w), lambda i: (i, 0))
    halo = pl.BlockSpec((SUBLANES, n), lambda i: (jnp.minimum((i + 1) * (tm // SUBLANES), last), 0))
    return _pcall(body, "ffn_bwd_up", (t // tm,),
                  [row(n), halo, row(D_MODEL), row(D_MODEL), _full((CONV_W, n)), _full((D_MODEL, n)),
                   _full((1, D_MODEL))],
                  [row(n), row(D_MODEL), row(D_MODEL), _full((1, D_MODEL))],
                  [_sds((t, n), BF16), _sds((t, D_MODEL)), _sds((t, D_MODEL), BF16), _sds((1, D_MODEL))],
                  )(dhc, dhc, dx2, x1, w_conv, w_up, g_ffn)


def _mix_bwd(dx1, y0, o, zh, zgt, pa, pb, w_glu, b_glu, gain, w_pa, w_pb, w_out, tm):
    t = dx1.shape[0]

    def body(dx1_ref, y0_ref, o_ref, zg_ref, zgt_ref, pa_ref, pb_ref, wglu_ref, bglu_ref, gain_ref, wpa_ref,
             wpb_ref, wout_ref,
             dy0_ref, do_ref, dzg_ref, dzgt_ref, m_ref, dpa_ref, dpb_ref, ya1_ref, dpre_ref, dbglu_ref, dgain_ref):
        @pl.when(pl.program_id(0) == 0)
        def _():
            dbglu_ref[...] = jnp.zeros_like(dbglu_ref)
            dgain_ref[...] = jnp.zeros_like(dgain_ref)

        dm = _dot_nt(dx1_ref[...], wout_ref[...])
        sga = _sigmoid(zgt_ref[:, 0:D_MODEL])
        sgb = _sigmoid(zgt_ref[:, D_MODEL:])
        pa = pa_ref[...]
        pb = pb_ref[...]
        m_ref[...] = (sga * pa + sgb * pb).astype(BF16)
        dzgt_ref[:, 0:D_MODEL] = dm * pa * sga * (1.0 - sga)
        dzgt_ref[:, D_MODEL:] = dm * pb * sgb * (1.0 - sgb)
        dpa = (dm * sga).astype(BF16)
        dpb = (dm * sgb).astype(BF16)
        dpa_ref[...] = dpa
        dpb_ref[...] = dpb
        dya2 = _dot_nt(dpa, wpa_ref[...])
        dyb = _dot_nt(dpb, wpb_ref[...])
        y0 = y0_ref[...]
        ya1 = _gelu(y0)
        ya1_ref[...] = ya1.astype(BF16)
        s = _sigmoid(_dot(ya1, wglu_ref[...]) + bglu_ref[...])
        dpre = dya2 * ya1 * s * (1.0 - s)
        dpre_ref[...] = dpre.astype(BF16)
        dbglu_ref[...] += jnp.sum(dpre, axis=0, keepdims=True)
        dya1 = dya2 * s + _dot_nt(dpre, wglu_ref[...])
        dy0_ref[...] = dya1 * _gelu_grad(y0)
        ov = o_ref[...]
        zg = zg_ref[...]
        oh = ov * _head_rms(ov)
        on = oh * gain_ref[...]
        sz = _sigmoid(zg)
        dzg_ref[...] = dyb * on * (sz * (1.0 + zg * (1.0 - sz)))
        don = dyb * (zg * sz)
        dgain_ref[...] += jnp.sum(don * oh, axis=0, keepdims=True)
        doh = don * gain_ref[...]
        do_ref[...] = _head_rms(ov) * (doh - oh * _head_mean(doh * oh))

    row = lambda w: pl.BlockSpec((tm, w), lambda i: (i, 0))
    return _pcall(body, "mix_bwd", (t // tm,),
                  [row(D_MODEL), row(S5_WIDTH), row(HG_WIDTH), pl.BlockSpec((tm, HG_WIDTH), lambda i: (i, 3)),
                   row(2 * D_MODEL), row(D_MODEL), row(D_MODEL), _full((S5_WIDTH, S5_WIDTH)), _full((1, S5_WIDTH)),
                   _full((1, HG_WIDTH)), _full((S5_WIDTH, D_MODEL)), _full((HG_WIDTH, D_MODEL)),
                   _full((D_MODEL, D_MODEL))],
                  [row(S5_WIDTH), row(HG_WIDTH), row(HG_WIDTH), row(2 * D_MODEL), row(D_MODEL), row(D_MODEL),
                   row(D_MODEL), row(S5_WIDTH), row(S5_WIDTH), _full((1, S5_WIDTH)), _full((1, HG_WIDTH))],
                  [_sds((t, S5_WIDTH)), _sds((t, HG_WIDTH)), _sds((t, HG_WIDTH)), _sds((t, 2 * D_MODEL)),
                   _sds((t, D_MODEL), BF16), _sds((t, D_MODEL), BF16), _sds((t, D_MODEL), BF16),
                   _sds((t, S5_WIDTH), BF16), _sds((t, S5_WIDTH), BF16), _sds((1, S5_WIDTH)), _sds((1, HG_WIDTH))],
                  )(dx1, y0, o, zh, zgt, pa, pb, w_glu, b_glu, gain, w_pa, w_pb, w_out)


def _s5_bwd(dy0, za, xs, cg, bbdt, lam, dskip, nb, seq, ts):
    t = za.shape[0]
    nts = seq // ts

    def body(dy0_ref, za_ref, xs_ref, halo_ref, cg_ref, bbdt_ref, lam_ref, d_ref,
             dza_ref, a_ref, dlam_ref, dd_ref, acc_ref, st_ref):
        b, j = pl.program_id(0), pl.program_id(1)

        @pl.when((b == 0) & (j == 0))
        def _():
            dlam_ref[...] = jnp.zeros_like(dlam_ref)
            dd_ref[...] = jnp.zeros_like(dd_ref)

        @pl.when(j == 0)
        def _():
            st_ref[...] = jnp.zeros_like(st_ref)

        dy0 = dy0_ref[...]
        acc_ref[...] = _dot(dy0, cg_ref[...])
        for cc in range(S5_N // S5_LANES):
            re = slice(cc * S5_LANES, (cc + 1) * S5_LANES)
            im = slice(S5_N + cc * S5_LANES, S5_N + (cc + 1) * S5_LANES)
            lr = lam_ref[0:1, re]
            li = lam_ref[1:2, re]

            def step(i, carry):
                ar, ai = carry
                row = ts - 1 - i
                nar = acc_ref[pl.ds(row, 1), re] + lr * ar + li * ai
                nai = acc_ref[pl.ds(row, 1), im] - li * ar + lr * ai
                acc_ref[pl.ds(row, 1), re] = nar
                acc_ref[pl.ds(row, 1), im] = nai
                return nar, nai

            ar, ai = lax.fori_loop(0, ts, step, (st_ref[0:1, re], st_ref[1:2, re]), unroll=8)
            st_ref[0:1, re] = ar
            st_ref[1:2, re] = ai
        av = acc_ref[...]
        a_ref[...] = av.astype(BF16)
        first = jnp.where(j == nts - 1, 0.0, halo_ref[SUBLANES - 1:SUBLANES, :])
        rows = lax.broadcasted_iota(jnp.int32, (ts, 2 * S5_N), 0)
        xp = jnp.where(rows == 0, first, pltpu.roll(xs_ref[...], 1, axis=0))
        ar, ai = av[:, :S5_N], av[:, S5_N:]
        xr, xi = xp[:, :S5_N], xp[:, S5_N:]
        dlam_ref[0:1, :] += jnp.sum(ar * xr + ai * xi, axis=0, keepdims=True)
        dlam_ref[1:2, :] += jnp.sum(ai * xr - ar * xi, axis=0, keepdims=True)
        zav = za_ref[...]
        dza_ref[...] = _dot(av, bbdt_ref[...]) + d_ref[...] * dy0
        dd_ref[...] += jnp.sum(dy0 * zav, axis=0, keepdims=True)

    tile = lambda b, j: b * nts + (nts - 1 - j)
    tok = lambda w: pl.BlockSpec((ts, w), lambda b, j: (tile(b, j), 0))
    halo = pl.BlockSpec((SUBLANES, 2 * S5_N),
                        lambda b, j: (jnp.maximum(tile(b, j) * (ts // SUBLANES) - 1, 0), 0))
    return _pcall(body, "s5_bwd", (nb, nts),
                  [tok(S5_WIDTH), tok(S5_WIDTH), tok(2 * S5_N), halo, _full((S5_WIDTH, 2 * S5_N)),
                   _full((2 * S5_N, S5_WIDTH)), _full((2, S5_N)), _full((1, S5_WIDTH))],
                  [tok(S5_WIDTH), tok(2 * S5_N), _full((2, S5_N)), _full((1, S5_WIDTH))],
                  [_sds((t, S5_WIDTH)), _sds((t, 2 * S5_N), BF16), _sds((2, S5_N)), _sds((1, S5_WIDTH))],
                  scratch=[pltpu.VMEM((ts, 2 * S5_N), F32), pltpu.VMEM((2, S5_N), F32)],
                  )(dy0, za, xs, xs, cg, bbdt, lam, dskip)


def _hgrn_bwd(zh, do, sts, lb, nb, seq):
    nc = seq // CHUNK

    def body(zh_ref, do_ref, sts_ref, lb_ref, dz_ref, dlb_ref, dst_ref):
        @pl.when(pl.program_id(0) == 0)
        def _():
            dst_ref[...] = jnp.zeros_like(dst_ref)
            dlb_ref[...] = jnp.zeros_like(dlb_ref)

        row = lax.broadcasted_iota(jnp.int32, (CHUNK, CHUNK), 0)
        causal = row >= lax.broadcasted_iota(jnp.int32, (CHUNK, CHUNK), 1)
        last_row = lax.broadcasted_iota(jnp.int32, (CHUNK, HG_HEAD), 0) == CHUNK - 1
        for b in range(nb):
            for h in range(HG_HEADS):
                hs = slice(h * HG_HEAD, (h + 1) * HG_HEAD)
                zq = zh_ref[b, :, h * HG_HEAD:(h + 1) * HG_HEAD]
                zf = zh_ref[b, :, HG_WIDTH + h * HG_HEAD:HG_WIDTH + (h + 1) * HG_HEAD]
                zi = zh_ref[b, :, 2 * HG_WIDTH + h * HG_HEAD:2 * HG_WIDTH + (h + 1) * HG_HEAD]
                lbh = lb_ref[:, hs]
                sf, f, sq, qa, bc, bm, bl = _hgrn_gates(zq, zf, lbh)
                k = 1.0 - f
                e_qt = jnp.exp(bc - bm)
                e_kt = jnp.exp(bm - bc)
                e_b = jnp.exp(bc)
                e_kd = jnp.exp(bl - bc)
                e_l = jnp.exp(bl)
                qt, kt, qb, kd = qa * e_qt, k * e_kt, qa * e_b, k * e_kd
                a = jnp.where(causal, _dot_nt(qt, kt), 0.0)
                st = sts_ref[b, 0, h]
                dst = dst_ref[b, h]
                dov = do_ref[b, :, hs]
                da = jnp.where(causal, _dot_nt(dov, zi), 0.0)
                dqt = _hdot(da, kt)
                dkt = _hdot_tn(da, qt)
                dqb = _hdot(dov, st)
                di = _dot_tn(a, dov) + _dot_nt(kd, dst)
                dkd = _hdot(zi, dst)
                de_l = jnp.sum(dst * st, axis=0, keepdims=True)
                dst_ref[b, h] = dst * e_l + _dot_tn(dov, qb)
                dqa = dqt * e_qt + dqb * e_b
                dk = dkt * e_kt + dkd * e_kd
                dbl = jnp.sum(dkd * kd, axis=0, keepdims=True) + de_l * e_l
                db = dqt * qt - dkt * kt + dqb * qb - dkd * kd + jnp.where(last_row, dbl, 0.0)
                df = _cumsum_rows(db, reverse=True) / f - dk
                dz_ref[b, :, h * HG_HEAD:(h + 1) * HG_HEAD] = dqa * QSCALE * (sq * (1.0 + zq * (1.0 - sq)))
                dz_ref[b, :, HG_WIDTH + h * HG_HEAD:HG_WIDTH + (h + 1) * HG_HEAD] = df * (1.0 - lbh) * sf * (1.0 - sf)
                dz_ref[b, :, 2 * HG_WIDTH + h * HG_HEAD:2 * HG_WIDTH + (h + 1) * HG_HEAD] = di
                dlb_ref[:, hs] += jnp.sum(df * (1.0 - sf), axis=0, keepdims=True)

    rev = lambda c: nc - 1 - c
    return _pcall(body, "hgrn_bwd", (nc,),
                  [pl.BlockSpec((nb, CHUNK, 4 * HG_WIDTH), lambda c: (0, rev(c), 0)),
                   pl.BlockSpec((nb, CHUNK, HG_WIDTH), lambda c: (0, rev(c), 0)),
                   pl.BlockSpec((nb, 1, HG_HEADS, HG_HEAD, HG_HEAD), lambda c: (0, rev(c), 0, 0, 0)),
                   _full((1, HG_WIDTH))],
                  [pl.BlockSpec((nb, CHUNK, 3 * HG_WIDTH), lambda c: (0, rev(c), 0)), _full((1, HG_WIDTH))],
                  [_sds((nb, seq, 3 * HG_WIDTH)), _sds((1, HG_WIDTH))],
                  scratch=[pltpu.VMEM((nb, HG_HEADS, HG_HEAD, HG_HEAD), F32)])(zh, do, sts, lb)


def _in_proj_bwd(dza, dzh, dzg, dzgt, dx1, x, g_mix, w_in, tm):
    t = x.shape[0]

    def body(dza_ref, dzh_ref, dzg_ref, dzgt_ref, dx1_ref, x_ref, g_ref, w_ref, dz_ref, dx_ref, dg_ref):
        @pl.when(pl.program_id(0) == 0)
        def _():
            dg_ref[...] = jnp.zeros_like(dg_ref)

        c1, c2, c3 = S5_WIDTH, S5_WIDTH + 3 * HG_WIDTH, S5_WIDTH + 4 * HG_WIDTH
        dz_ref[:, 0:c1] = dza_ref[...].astype(BF16)
        dz_ref[:, c1:c2] = dzh_ref[...].astype(BF16)
        dz_ref[:, c2:c3] = dzg_ref[...].astype(BF16)
        dz_ref[:, c3:] = dzgt_ref[...].astype(BF16)
        du = _dot_nt(dz_ref[...], w_ref[...])
        xv = x_ref[...]
        r = lax.rsqrt(jnp.mean(xv * xv, axis=-1, keepdims=True) + EPS)
        xn = xv * r
        dg_ref[...] += jnp.sum(du * xn, axis=0, keepdims=True)
        dxn = du * g_ref[...]
        dx_ref[...] = dx1_ref[...] + r * (dxn - xn * jnp.mean(dxn * xn, axis=-1, keepdims=True))

    row = lambda w: pl.BlockSpec((tm, w), lambda i: (i, 0))
    return _pcall(body, "in_proj_bwd", (t // tm,),
                  [row(S5_WIDTH), row(3 * HG_WIDTH), row(HG_WIDTH), row(2 * D_MODEL), row(D_MODEL), row(D_MODEL),
                   _full((1, D_MODEL)), _full((D_MODEL, N_IN))],
                  [row(N_IN), row(D_MODEL), _full((1, D_MODEL))],
                  [_sds((t, N_IN), BF16), _sds((t, D_MODEL)), _sds((1, D_MODEL))],
                  )(dza, dzh, dzg, dzgt, dx1, x, g_mix, w_in)


def _local_step(x3, tgt3, weights, sp, emit, emit_small):
    nb, seq, _ = x3.shape
    t = nb * seq
    tm = _token_tile(seq)
    x = x3.reshape(t, D_MODEL)
    tgt = tgt3.reshape(t, D_MODEL)
    row = lambda v: v.reshape(1, -1)

    a_re = sp["s5_a_re"].reshape(S5_N, 1)
    a_im = sp["s5_a_im"].reshape(S5_N, 1)
    ldt = jnp.repeat(sp["s5_log_dt"].reshape(S5_GROUPS), S5_STATE).reshape(S5_N, 1)
    b_re = sp["s5_b_re"].reshape(S5_N, S5_GROUP)
    b_im = sp["s5_b_im"].reshape(S5_N, S5_GROUP)
    lr, li, bb_re, bb_im, lb = _params_fwd(a_re, a_im, ldt, b_re, b_im, sp["hg_lb_logits"])
    lam = jnp.concatenate([lr.reshape(1, S5_N), li.reshape(1, S5_N)], axis=0)
    gps = lambda m: m.reshape(S5_GROUPS, S5_STATE, S5_GROUP)
    bbt = jnp.concatenate([_blockdiag(gps(bb_re).transpose(0, 2, 1)), _blockdiag(gps(bb_im).transpose(0, 2, 1))],
                          axis=1).astype(BF16)
    ct = jnp.concatenate([_blockdiag(sp["s5_c_re"].transpose(0, 2, 1)), -_blockdiag(sp["s5_c_im"].transpose(0, 2, 1))],
                         axis=0).astype(BF16)

    g_mix, g_ffn, g_final = row(sp["g_mix"]), row(sp["g_ffn"]), row(sp["g_final"])
    b_glu, gain, dskip, b_conv = row(sp["b_glu"]), row(sp["hg_norm_gain"]), row(sp["s5_d"]), row(sp["b_conv"])

    w_in = weights("in", ct)["w_in"]
    u, za, zh, zgt = _in_proj(x, g_mix, w_in, tm)
    xs, y0 = _s5_fwd(za, bbt, lam, ct, dskip, nb, seq, tm)
    o3, sts = _hgrn_fwd(zh.reshape(nb, seq, 4 * HG_WIDTH), lb, nb, seq)
    o = o3.reshape(t, HG_WIDTH)
    wm = weights("mix", o3)
    x1, u2, pa, pb, ya2, yb = _mix_fwd(x, y0, o, zh, zgt, wm["w_glu"], b_glu, gain, wm["w_pa"], wm["w_pb"],
                                       wm["w_out"], g_ffn, tm)
    wf = weights("ffn", u2)
    h = _ffn_up(u2, wf["w_up"], tm)
    a, dx2, dx2b, loss, dg_final = _ffn_down_loss(h, x1, tgt, wf["w_conv"], b_conv, wf["w_down"], g_final, seq, tm)

    wgrad = functools.partial(_wgrad, tn=256, out_dtype=BF16)
    dhc, dw_conv, db_conv = _ffn_bwd_act(dx2, h, wf["w_conv"], b_conv, wf["w_down"], seq, tm)
    sent = emit({"w_conv": dw_conv, "w_down": wgrad(a, dx2b, "dw_down")})
    dh, dx1, dx1b, dg_ffn = _ffn_bwd_up(dhc, dx2, x1, wf["w_conv"], wf["w_up"], g_ffn + sent, seq, tm)
    sent = emit({"w_up": wgrad(dh, u2, "dw_up", transpose_out=True)})
    (dy0, do, dzg, dzgt, m, dpa, dpb, ya1, dpre, db_glu, dgain) = _mix_bwd(
        dx1, y0, o, zh, zgt, pa, pb, wm["w_glu"], b_glu + sent, gain, wm["w_pa"], wm["w_pb"], wm["w_out"], tm)
    sent = emit({"w_out": wgrad(m, dx1b, "dw_out"), "w_pa": wgrad(ya2, dpa, "dw_pa"),
                 "w_pb": wgrad(yb, dpb, "dw_pb"), "w_glu": wgrad(ya1, dpre, "dw_glu")})
    dzh3, dlb = _hgrn_bwd(zh.reshape(nb, seq, 4 * HG_WIDTH), do.reshape(nb, seq, HG_WIDTH), sts, lb + sent, nb, seq)
    dza, a_s5, dlam, dd = _s5_bwd(dy0, za, xs, ct.T, bbt.T, lam, dskip, nb, seq, tm)
    band = HG_HEAD
    dbb_band = _wgrad(a_s5, za, "dbb_s5", 512, band=band)
    dc_band = _wgrad(xs, dy0, "dc_s5", 512, band=band)

    dbb_re = _diag_blocks(dbb_band[:S5_N], S5_STATE, S5_GROUP).reshape(S5_N, S5_GROUP)
    dbb_im = _diag_blocks(dbb_band[S5_N:], S5_STATE, S5_GROUP).reshape(S5_N, S5_GROUP)
    dc_re = _diag_blocks(dc_band[:S5_N], S5_STATE, S5_GROUP).transpose(0, 2, 1)
    dc_im = -_diag_blocks(dc_band[S5_N:], S5_STATE, S5_GROUP).transpose(0, 2, 1)
    da_re, da_im, dldt, db_re, db_im, dlogits = _params_bwd(
        a_re, a_im, ldt, b_re, b_im, sp["hg_lb_logits"],
        dlam[0].reshape(S5_N, 1), dlam[1].reshape(S5_N, 1), dbb_re, dbb_im, dlb)
    sent = emit_small({"s5_a_re": da_re, "s5_a_im": da_im, "s5_log_dt": dldt, "s5_b_re": db_re, "s5_b_im": db_im,
                       "s5_c_re": dc_re, "s5_c_im": dc_im, "s5_d": dd, "b_glu": db_glu, "hg_lb_logits": dlogits,
                       "hg_norm_gain": dgain, "g_ffn": dg_ffn, "b_conv": db_conv, "g_final": dg_final, "loss": loss})

    dz, dx, dg_mix = _in_proj_bwd(dza, dzh3.reshape(t, 3 * HG_WIDTH), dzg, dzgt, dx1, x, g_mix + sent, w_in, tm)
    dw_in = wgrad(dz, u, "dw_in", transpose_out=True)
    return dx.reshape(nb, seq, D_MODEL), dw_in, dg_mix


def _mesh_peers():
    x, y, c = lax.axis_index("x"), lax.axis_index("y"), lax.axis_index("c")
    peers = []
    for k in range(1, N_DEV):
        px, py, pc = (1 - x if k & 4 else x), (1 - y if k & 2 else y), (1 - c if k & 1 else c)
        peers.append((k, (px, py, pc), 4 * px + 2 * py + pc))
    return 4 * x + 2 * y + c, peers


_HBM = pl.BlockSpec(memory_space=pltpu.HBM)
_SEM = pl.BlockSpec(memory_space=pltpu.SEMAPHORE)


def _exchange_start(name, operands, after):
    n = len(operands)
    me = 4 * lax.axis_index("x") + 2 * lax.axis_index("y") + lax.axis_index("c")
    flags = [per_peer for _, per_peer in operands]
    srcs, lands = [], []
    for arr, per_peer in operands:
        own = lax.dynamic_index_in_dim(arr, me, 0, keepdims=True) if per_peer else arr[None]
        land = lax.dynamic_update_slice_in_dim(lax.empty((N_DEV,) + own.shape[1:], arr.dtype), own, me, 0)
        srcs.append(pltpu.with_memory_space_constraint(arr, pltpu.HBM))
        lands.append(pltpu.with_memory_space_constraint(land, pltpu.HBM))
    copies = (N_DEV - 1) * n

    def body(*refs):
        src_refs, land_refs = refs[:n], refs[n:2 * n]
        send_sems, recv_sems = refs[2 * n + 1], refs[2 * n + 2]
        token = refs[-1]
        my_slab, peers = _mesh_peers()
        for k, peer, slab in peers:
            for i in range(n):
                s = (k - 1) * n + i
                pltpu.make_async_remote_copy(
                    src_ref=src_refs[i].at[slab] if flags[i] else src_refs[i], dst_ref=land_refs[i].at[my_slab],
                    send_sem=send_sems.at[s], recv_sem=recv_sems.at[s], device_id=peer,
                    device_id_type=pl.DeviceIdType.MESH).start()
        token[...] = jnp.zeros_like(token)

    outs = pl.pallas_call(
        body, name=name,
        out_shape=(pltpu.SemaphoreType.DMA((copies,)), pltpu.SemaphoreType.DMA((copies,)),
                   *[pltpu.HBM(a.shape, a.dtype) for a in srcs], *[pltpu.HBM(a.shape, a.dtype) for a in lands],
                   _sds((SUBLANES, 128))),
        in_specs=[_HBM] * (2 * n) + [pl.BlockSpec(memory_space=pl.ANY)],
        out_specs=(_SEM, _SEM, *[_HBM] * (2 * n), pl.BlockSpec(memory_space=pltpu.VMEM)),
        input_output_aliases={i: 2 + i for i in range(2 * n)},
        compiler_params=pltpu.CompilerParams(has_side_effects=pltpu.SideEffectType.DATAFLOW_SIDE_EFFECTING),
    )(*srcs, *lands, after)
    state = (flags, outs[0], outs[1], outs[2:2 + n], outs[2 + n:2 + 2 * n])
    return state, outs[-1]


def _exchange_wait(name, state, after):
    flags, send_sems, recv_sems, srcs, lands = state
    n = len(flags)

    def body(*refs):
        src_refs, land_refs = refs[:n], refs[n:2 * n]
        send_ref, recv_ref = refs[2 * n], refs[2 * n + 1]
        _, peers = _mesh_peers()
        for k, peer, slab in peers:
            for i in range(n):
                s = (k - 1) * n + i
                copy = pltpu.make_async_remote_copy(
                    src_ref=src_refs[i].at[slab] if flags[i] else src_refs[i], dst_ref=land_refs[i].at[slab],
                    send_sem=send_ref.at[s], recv_sem=recv_ref.at[s], device_id=peer,
                    device_id_type=pl.DeviceIdType.MESH)
                copy.wait_send()
                copy.wait_recv()

    outs = pl.pallas_call(
        body, name=name,
        out_shape=(*[pltpu.HBM(a.shape, a.dtype) for a in srcs], *[pltpu.HBM(a.shape, a.dtype) for a in lands]),
        in_specs=[_HBM] * (2 * n) + [_SEM, _SEM, pl.BlockSpec(memory_space=pl.ANY)],
        out_specs=tuple([_HBM] * (2 * n)),
        input_output_aliases={i: i for i in range(2 * n)},
        compiler_params=pltpu.CompilerParams(has_side_effects=pltpu.SideEffectType.DATAFLOW_SIDE_EFFECTING),
    )(*srcs, *lands, send_sems, recv_sems, after)
    return list(outs[n:])


def _join_cols(parts, name, tr):
    _, r, c = parts.shape

    def body(p_ref, o_ref):
        for j in range(N_DEV):
            o_ref[:, j * c:(j + 1) * c] = p_ref[j]

    return _pcall(body, name, (r // tr,), [pl.BlockSpec((N_DEV, tr, c), lambda i: (0, i, 0))],
                  pl.BlockSpec((tr, N_DEV * c), lambda i: (i, 0)), _sds((r, N_DEV * c), parts.dtype))(parts)


def _split_cols(full, name, tr):
    r, c = full.shape[0], full.shape[1] // N_DEV

    def body(f_ref, o_ref):
        for j in range(N_DEV):
            o_ref[j] = f_ref[:, j * c:(j + 1) * c]

    return _pcall(body, name, (r // tr,), [pl.BlockSpec((tr, N_DEV * c), lambda i: (i, 0))],
                  pl.BlockSpec((N_DEV, tr, c), lambda i: (0, i, 0)), _sds((N_DEV, r, c), full.dtype))(full)


def _adamw(parts, w, m, v, name, tile):
    rows, cols = w.shape

    def body(p_ref, w_ref, m_ref, v_ref, g_out, d_out, m_out, v_out):
        g = p_ref[0].astype(F32)
        for k in range(1, N_DEV):
            g = g + p_ref[k].astype(F32)
        m1 = ADAM_B1 * m_ref[...] + (1.0 - ADAM_B1) * g
        v1 = ADAM_B2 * v_ref[...] + (1.0 - ADAM_B2) * (g * g)
        m_hat = m1 / (1.0 - ADAM_B1 ** ADAM_STEP)
        v_hat = v1 / (1.0 - ADAM_B2 ** ADAM_STEP)
        g_out[...] = g
        d_out[...] = -ADAM_LR * (m_hat / (jnp.sqrt(v_hat) + ADAM_EPS) + ADAM_WD * w_ref[...])
        m_out[...] = m1
        v_out[...] = v1

    row = pl.BlockSpec((tile, cols), lambda i: (i, 0))
    return _pcall(body, name, (rows // tile,),
                  [pl.BlockSpec((N_DEV, tile, cols), lambda i: (0, i, 0)), row, row, row],
                  [row, row, row, row], [_sds((rows, cols))] * 4)(parts, w, m, v)


BIG = {
    "w_in": ((D_MODEL, N_IN // N_DEV), True, 256),
    "w_glu": ((S5_WIDTH // N_DEV, S5_WIDTH), False, S5_WIDTH // N_DEV),
    "w_pa": ((S5_WIDTH, D_MODEL // N_DEV), True, S5_WIDTH),
    "w_pb": ((HG_WIDTH, D_MODEL // N_DEV), True, HG_WIDTH),
    "w_out": ((D_MODEL // N_DEV, D_MODEL), False, D_MODEL // N_DEV),
    "w_up": ((D_MODEL, 2 * D_FF // N_DEV), True, 256),
    "w_conv": ((CONV_W, 2 * D_FF // N_DEV), True, CONV_W),
    "w_down": ((D_FF // N_DEV, D_MODEL), False, D_FF // N_DEV // 2),
}
UNALIGNED_COLS = ("w_in", "w_up", "w_conv")


def _join_shards(n, parts):
    (a, b), by_cols, _ = BIG[n]
    if not by_cols:
        return parts.reshape(N_DEV * a, b)
    if n in UNALIGNED_COLS:
        return _join_cols(parts, "join_" + n, min(a, 256))
    return parts.transpose(1, 0, 2).reshape(a, N_DEV * b)


def _split_shards(n, full):
    (a, b), by_cols, _ = BIG[n]
    if not by_cols:
        return full.reshape(N_DEV, a, b)
    if n in UNALIGNED_COLS:
        return _split_cols(full, "split_" + n, min(a, 256))
    return full.reshape(a, N_DEV, b).transpose(1, 0, 2)


PACKED = tuple((n, shp) for n, shp in SMALL if n != "g_mix") + (("loss", (1,)),)


def _pack_small(d):
    flat = jnp.concatenate([d[n].reshape(-1) for n, _ in PACKED])
    return jnp.pad(flat, (0, SMALL_ROWS * PACK_W - flat.shape[0])).reshape(SMALL_ROWS, PACK_W)


def _unpack_small(p):
    flat = p.reshape(-1)
    out, off = {}, 0
    for n, shp in PACKED:
        size = math.prod(shp)
        out[n] = flat[off:off + size].reshape(shp)
        off += size
    return out


def kernel(x, g_mix, w_in, s5_a_re, s5_a_im, s5_log_dt, s5_b_re, s5_b_im, s5_c_re, s5_c_im, s5_d, w_glu, b_glu, hg_lb_logits, hg_norm_gain, w_pa, w_pb, w_out, g_ffn, w_up, w_conv, b_conv, w_down, g_final, loss_target, m_g_mix, m_w_in, m_s5_a_re, m_s5_a_im, m_s5_log_dt, m_s5_b_re, m_s5_b_im, m_s5_c_re, m_s5_c_im, m_s5_d, m_w_glu, m_b_glu, m_hg_lb_logits, m_hg_norm_gain, m_w_pa, m_w_pb, m_w_out, m_g_ffn, m_w_up, m_w_conv, m_b_conv, m_w_down, m_g_final, v_g_mix, v_w_in, v_s5_a_re, v_s5_a_im, v_s5_log_dt, v_s5_b_re, v_s5_b_im, v_s5_c_re, v_s5_c_im, v_s5_d, v_w_glu, v_b_glu, v_hg_lb_logits, v_hg_norm_gain, v_w_pa, v_w_pb, v_w_out, v_g_ffn, v_w_up, v_w_conv, v_b_conv, v_w_down, v_g_final):
    given = dict(locals())
    small_names = [n for n, _ in SMALL]

    pay = {n: given[n][0] if n == "w_conv" else given[n][0].astype(BF16) for n in BIG}
    groups = {"in": ["w_in"], "mix": ["w_glu", "w_pa", "w_pb", "w_out"], "ffn": ["w_up", "w_down", "w_conv"]}
    gathers, order = {}, pay["w_in"]
    for grp, names in groups.items():
        gathers[grp], order = _exchange_start("gather_" + grp + "_start", [(pay[n], False) for n in names], order)

    def weights(grp, after):
        got = _exchange_wait("gather_" + grp + "_wait", gathers[grp], after)
        return {n: _join_shards(n, g) for n, g in zip(groups[grp], got)}

    in_flight = []

    def emit(grads):
        names = list(grads)
        state, token = _exchange_start("grads_" + names[0] + "_start",
                                       [(_split_shards(n, grads[n]), True) for n in names], grads[names[0]])
        in_flight.append((names, state))
        return token[0, 0]

    def emit_small(grads):
        pack = _pack_small(grads)
        state, token = _exchange_start("grads_small_start", [(pack, False)], pack)
        in_flight.append((["small"], state))
        return token[0, 0]

    sp = {n: (given[n] if n in ("g_final", "hg_lb_logits") else given[n][0]) for n in small_names}
    sp["g_mix"] = sp["g_mix"] + order[0, 0]
    dx, dw_in, dg_mix = _local_step(x, loss_target, weights, sp, emit, emit_small)
    last, token = _exchange_start("grads_w_in_start", [(_split_shards("w_in", dw_in), True), (dg_mix, False)], dw_in)

    res = {}

    def update_big(n, part):
        shard, _, tile = BIG[n]
        r4 = _adamw(part, given[n][0], given["m_" + n][0], given["v_" + n][0], "adamw_" + n, tile)
        res[n] = [r.reshape((1,) + shard) for r in r4]
        return r4[0]

    after = token
    for names, state in in_flight:
        for n, part in zip(names, _exchange_wait("grads_" + names[0] + "_wait", state, after)):
            if n != "small":
                after = update_big(n, part)
                continue
            zero = jnp.zeros((1,), F32)
            rs4 = _adamw(part, _pack_small({**{k: given[k] for k in small_names}, "loss": zero}),
                         _pack_small({**{k: given["m_" + k] for k in small_names}, "loss": zero}),
                         _pack_small({**{k: given["v_" + k] for k in small_names}, "loss": zero}),
                         "adamw_small", SMALL_ROWS)
            small4 = [_unpack_small(r) for r in rs4]
            for k in small_names:
                if k != "g_mix":
                    res[k] = [us[k] for us in small4]
            total_loss = small4[0]["loss"][0]
            after = rs4[0]
    part_in, part_gmix = _exchange_wait("grads_w_in_wait", last, after)
    update_big("w_in", part_in)
    res["g_mix"] = _adamw(part_gmix, given["g_mix"], m_g_mix, v_g_mix, "adamw_g_mix", 1)
    return (total_loss, dx, *[res[n][0] for n in WEIGHT_ORDER], *[res[n][1] for n in WEIGHT_ORDER],
            *[res[n][2] for n in WEIGHT_ORDER], *[res[n][3] for n in WEIGHT_ORDER])
```

```python
import functools
import math

import jax
import jax.numpy as jnp
from jax import lax
from jax.experimental import pallas as pl
from jax.experimental.pallas import tpu as pltpu

F32 = jnp.float32
BF16 = jnp.bfloat16

D_MODEL = 1024
S5_WIDTH = 512
S5_GROUP = 16
S5_GROUPS = 32
S5_STATE = 64
S5_N = S5_GROUPS * S5_STATE
HG_WIDTH = 512
HG_HEAD = 128
HG_HEADS = 4
D_FF = 2816
CONV_W = 3
CHUNK = 64
N_IN = S5_WIDTH + 4 * HG_WIDTH + 2 * D_MODEL
EPS = 1e-6
QSCALE = HG_HEAD ** -0.5

ADAM_LR = 0.001
ADAM_B1 = 0.9
ADAM_B2 = 0.999
ADAM_EPS = 1e-08
ADAM_WD = 0.01
ADAM_STEP = 10

N_DEV = 8
V7X_VMEM_BYTES = 64 * 1024 * 1024
VMEM_LIMIT = V7X_VMEM_BYTES * 7 // 8
SUBLANES = 8
PACK_W = 1024

SMALL = (
    ("g_mix", (1, D_MODEL)),
    ("s5_a_re", (1, S5_GROUPS, S5_STATE)),
    ("s5_a_im", (1, S5_GROUPS, S5_STATE)),
    ("s5_log_dt", (1, S5_GROUPS)),
    ("s5_b_re", (1, S5_GROUPS, S5_STATE, S5_GROUP)),
    ("s5_b_im", (1, S5_GROUPS, S5_STATE, S5_GROUP)),
    ("s5_c_re", (1, S5_GROUPS, S5_GROUP, S5_STATE)),
    ("s5_c_im", (1, S5_GROUPS, S5_GROUP, S5_STATE)),
    ("s5_d", (1, S5_WIDTH)),
    ("b_glu", (1, S5_WIDTH)),
    ("hg_lb_logits", (2, HG_WIDTH)),
    ("hg_norm_gain", (1, HG_WIDTH)),
    ("g_ffn", (1, D_MODEL)),
    ("b_conv", (1, 2 * D_FF)),
    ("g_final", (D_MODEL,)),
)
SMALL_ROWS = 144
WEIGHT_ORDER = ("g_mix", "w_in", "s5_a_re", "s5_a_im", "s5_log_dt", "s5_b_re", "s5_b_im", "s5_c_re", "s5_c_im",
                "s5_d", "w_glu", "b_glu", "hg_lb_logits", "hg_norm_gain", "w_pa", "w_pb", "w_out", "g_ffn",
                "w_up", "w_conv", "b_conv", "w_down", "g_final")


def _pcall(body, name, grid, in_specs, out_specs, out_shape, scratch=()):
    return pl.pallas_call(
        body, name=name, grid=grid, in_specs=in_specs, out_specs=out_specs, out_shape=out_shape,
        scratch_shapes=list(scratch),
        compiler_params=pltpu.CompilerParams(dimension_semantics=("arbitrary",) * len(grid),
                                             vmem_limit_bytes=VMEM_LIMIT),
    )


def _full(shape):
    return pl.BlockSpec(shape, lambda *_: (0,) * len(shape))


def _sds(shape, dtype=F32):
    return jax.ShapeDtypeStruct(shape, dtype)


def _dot(a, b):
    return jnp.dot(a.astype(BF16), b.astype(BF16), preferred_element_type=F32)


def _dot_nt(a, b):
    return lax.dot_general(a.astype(BF16), b.astype(BF16), (((1,), (1,)), ((), ())), preferred_element_type=F32)


def _dot_tn(a, b):
    return lax.dot_general(a.astype(BF16), b.astype(BF16), (((0,), (0,)), ((), ())), preferred_element_type=F32)


def _hdot(a, b):
    return jnp.dot(a, b, preferred_element_type=F32, precision=lax.Precision.HIGHEST)


def _hdot_tn(a, b):
    return lax.dot_general(a, b, (((0,), (0,)), ((), ())), preferred_element_type=F32,
                           precision=lax.Precision.HIGHEST)


def _sigmoid(x):
    return jax.nn.sigmoid(x)


GELU_C = math.sqrt(2.0 / math.pi)
GELU_A = 0.044715


def _gelu(x):
    return 0.5 * x * (1.0 + jnp.tanh(GELU_C * (x + GELU_A * (x * x * x))))


def _gelu_grad(x):
    t = jnp.tanh(GELU_C * (x + GELU_A * (x * x * x)))
    return 0.5 * (1.0 + t) + 0.5 * x * (1.0 - t * t) * (GELU_C * (1.0 + 3.0 * GELU_A * x * x))


def _cumsum_rows(v, reverse=False):
    n = v.shape[0]
    row = lax.broadcasted_iota(jnp.int32, v.shape, 0)
    s = 1
    while s < n:
        if reverse:
            v = v + jnp.where(row < n - s, pltpu.roll(v, n - s, axis=0), 0.0)
        else:
            v = v + jnp.where(row >= s, pltpu.roll(v, s, axis=0), 0.0)
        s *= 2
    return v


def _token_tile(seq):
    return min(256, seq)


def _s5_disc(a_re, a_im, ldt, b_re, b_im):
    dt = jnp.exp(ldt)
    mag = jnp.exp(a_re * dt)
    ang = a_im * dt
    lb_re = mag * jnp.cos(ang)
    lb_im = mag * jnp.sin(ang)
    den = a_re * a_re + a_im * a_im
    n_re = lb_re - 1.0
    n_im = lb_im
    co_re = (n_re * a_re + n_im * a_im) / den
    co_im = (n_im * a_re - n_re * a_im) / den
    bb_re = co_re * b_re - co_im * b_im
    bb_im = co_re * b_im + co_im * b_re
    return lb_re, lb_im, bb_re, bb_im


def _params_fwd(a_re, a_im, ldt, b_re, b_im, logits):
    def body(are, aim, ld, bre, bim, lg, lr_o, li_o, bbr_o, bbi_o, lb_o):
        lr, li, bbr, bbi = _s5_disc(are[...], aim[...], ld[...], bre[...], bim[...])
        lr_o[...] = lr
        li_o[...] = li
        bbr_o[...] = bbr
        bbi_o[...] = bbi
        lb_o[...] = _sigmoid(lg[0:1, :] - lg[1:2, :])

    col, mat = (S5_N, 1), (S5_N, S5_GROUP)
    return _pcall(body, "params_fwd", (1,),
                  [_full(col), _full(col), _full(col), _full(mat), _full(mat), _full((2, HG_WIDTH))],
                  [_full(col), _full(col), _full(mat), _full(mat), _full((1, HG_WIDTH))],
                  [_sds(col), _sds(col), _sds(mat), _sds(mat), _sds((1, HG_WIDTH))])(a_re, a_im, ldt, b_re, b_im, logits)


def _params_bwd(a_re, a_im, ldt, b_re, b_im, logits, dlr, dli, dbbr, dbbi, dlb):
    def body(are, aim, ld, bre, bim, lg, dlr_r, dli_r, dbbr_r, dbbi_r, dlb_r,
             dare_o, daim_o, dld_o, dbre_o, dbim_o, dlg_o):
        _, vjp = jax.vjp(_s5_disc, are[...], aim[...], ld[...], bre[...], bim[...])
        dare, daim, dld, dbre, dbim = vjp((dlr_r[...], dli_r[...], dbbr_r[...], dbbi_r[...]))
        dare_o[...] = dare
        daim_o[...] = daim
        dbre_o[...] = dbre
        dbim_o[...] = dbim
        for g in range(S5_GROUPS):
            dld_o[g:g + 1, :] = jnp.sum(dld[g * S5_STATE:(g + 1) * S5_STATE, :], axis=0, keepdims=True)
        lb = _sigmoid(lg[0:1, :] - lg[1:2, :])
        d0 = dlb_r[...] * lb * (1.0 - lb)
        dlg_o[0:1, :] = d0
        dlg_o[1:2, :] = -d0

    col, mat = (S5_N, 1), (S5_N, S5_GROUP)
    return _pcall(body, "params_bwd", (1,),
                  [_full(col), _full(col), _full(col), _full(mat), _full(mat), _full((2, HG_WIDTH)),
                   _full(col), _full(col), _full(mat), _full(mat), _full((1, HG_WIDTH))],
                  [_full(col), _full(col), _full((S5_GROUPS, 1)), _full(mat), _full(mat), _full((2, HG_WIDTH))],
                  [_sds(col), _sds(col), _sds((S5_GROUPS, 1)), _sds(mat), _sds(mat), _sds((2, HG_WIDTH))],
                  )(a_re, a_im, ldt, b_re, b_im, logits, dlr, dli, dbbr, dbbi, dlb)


def _blockdiag(m):
    g, r, c = m.shape
    eye = jnp.eye(g, dtype=m.dtype)
    return (m[:, :, None, :] * eye[:, None, :, None]).reshape(g * r, g * c)


def _diag_blocks(band, r, c):
    g, nb = band.shape[0] // r, band.shape[1] // c
    on_diag = (jnp.arange(g) % nb)[:, None, None, None] == jnp.arange(nb)[None, None, :, None]
    return jnp.sum(jnp.where(on_diag, band.reshape(g, r, nb, c), 0.0), axis=2)


def _in_proj(x, g_mix, w_in, tm):
    t = x.shape[0]

    def body(x_ref, g_ref, w_ref, u_ref, za_ref, zh_ref, zg_ref):
        xv = x_ref[...]
        r = lax.rsqrt(jnp.mean(xv * xv, axis=-1, keepdims=True) + EPS)
        u = (xv * r * g_ref[...]).astype(BF16)
        u_ref[...] = u
        za_ref[...] = jnp.dot(u, w_ref[:, 0:S5_WIDTH], preferred_element_type=F32)
        zh_ref[...] = jnp.dot(u, w_ref[:, S5_WIDTH:S5_WIDTH + 4 * HG_WIDTH], preferred_element_type=F32)
        zg_ref[...] = jnp.dot(u, w_ref[:, S5_WIDTH + 4 * HG_WIDTH:], preferred_element_type=F32)

    row = lambda w: pl.BlockSpec((tm, w), lambda i: (i, 0))
    return _pcall(body, "in_proj", (t // tm,),
                  [row(D_MODEL), _full((1, D_MODEL)), _full((D_MODEL, N_IN))],
                  [row(D_MODEL), row(S5_WIDTH), row(4 * HG_WIDTH), row(2 * D_MODEL)],
                  [_sds((t, D_MODEL), BF16), _sds((t, S5_WIDTH)), _sds((t, 4 * HG_WIDTH)), _sds((t, 2 * D_MODEL))],
                  )(x, g_mix, w_in)


S5_LANES = 512
S5_BANDS = 4


def _band(q):
    return (slice(q * S5_WIDTH // S5_BANDS, (q + 1) * S5_WIDTH // S5_BANDS),
            slice(q * S5_N // S5_BANDS, (q + 1) * S5_N // S5_BANDS))


def _im(st):
    return slice(S5_N + st.start, S5_N + st.stop)


def _s5_fwd(za, bbt, lam, ct, dskip, nb, seq, ts):
    t = za.shape[0]
    nts = seq // ts

    def body(za_ref, bbt_ref, lam_ref, ct_ref, d_ref, xs_ref, y_ref, st_ref):
        @pl.when(pl.program_id(1) == 0)
        def _():
            st_ref[...] = jnp.zeros_like(st_ref)

        zav = za_ref[...]
        for q in range(S5_BANDS):
            ch, st = _band(q)
            xs_ref[:, st] = _dot(zav[:, ch], bbt_ref[ch, st])
            xs_ref[:, _im(st)] = _dot(zav[:, ch], bbt_ref[ch, _im(st)])
        for cc in range(S5_N // S5_LANES):
            re = slice(cc * S5_LANES, (cc + 1) * S5_LANES)
            im = slice(S5_N + cc * S5_LANES, S5_N + (cc + 1) * S5_LANES)
            lr = lam_ref[0:1, re]
            li = lam_ref[1:2, re]

            def step(i, carry):
                xr, xi = carry
                br = xs_ref[pl.ds(i, 1), re]
                bi = xs_ref[pl.ds(i, 1), im]
                nxr = lr * xr - li * xi + br
                nxi = lr * xi + li * xr + bi
                xs_ref[pl.ds(i, 1), re] = nxr
                xs_ref[pl.ds(i, 1), im] = nxi
                return nxr, nxi

            xr, xi = lax.fori_loop(0, ts, step, (st_ref[0:1, re], st_ref[1:2, re]), unroll=8)
            st_ref[0:1, re] = xr
            st_ref[1:2, re] = xi
        for q in range(S5_BANDS):
            ch, st = _band(q)
            y_ref[:, ch] = (_dot(xs_ref[:, st], ct_ref[st, ch]) + _dot(xs_ref[:, _im(st)], ct_ref[_im(st), ch])
                            + d_ref[:, ch] * zav[:, ch])

    tok = lambda w: pl.BlockSpec((ts, w), lambda b, j: (b * nts + j, 0))
    return _pcall(body, "s5_fwd", (nb, nts),
                  [tok(S5_WIDTH), _full((S5_WIDTH, 2 * S5_N)), _full((2, S5_N)), _full((2 * S5_N, S5_WIDTH)),
                   _full((1, S5_WIDTH))],
                  [tok(2 * S5_N), tok(S5_WIDTH)],
                  [_sds((t, 2 * S5_N)), _sds((t, S5_WIDTH))],
                  scratch=[pltpu.VMEM((2, S5_N), F32)])(za, bbt, lam, ct, dskip)


def _hgrn_gates(zq, zf, lbh):
    sf = _sigmoid(zf)
    f = lbh + (1.0 - lbh) * sf
    sq = _sigmoid(zq)
    qa = zq * sq * QSCALE
    bc = _cumsum_rows(jnp.log(f))
    bm = bc[CHUNK // 2 - 1:CHUNK // 2, :]
    bl = bc[CHUNK - 1:CHUNK, :]
    return sf, f, sq, qa, bc, bm, bl


def _hgrn_fwd(zh, lb, nb, seq):
    nc = seq // CHUNK

    def body(zh_ref, lb_ref, o_ref, sts_ref, st_ref):
        @pl.when(pl.program_id(0) == 0)
        def _():
            st_ref[...] = jnp.zeros_like(st_ref)

        causal = (lax.broadcasted_iota(jnp.int32, (CHUNK, CHUNK), 0)
                  >= lax.broadcasted_iota(jnp.int32, (CHUNK, CHUNK), 1))
        for b in range(nb):
            for h in range(HG_HEADS):
                hs = slice(h * HG_HEAD, (h + 1) * HG_HEAD)
                zq = zh_ref[b, :, h * HG_HEAD:(h + 1) * HG_HEAD]
                zf = zh_ref[b, :, HG_WIDTH + h * HG_HEAD:HG_WIDTH + (h + 1) * HG_HEAD]
                zi = zh_ref[b, :, 2 * HG_WIDTH + h * HG_HEAD:2 * HG_WIDTH + (h + 1) * HG_HEAD]
                _, f, _, qa, bc, bm, bl = _hgrn_gates(zq, zf, lb_ref[:, hs])
                k = 1.0 - f
                qt = qa * jnp.exp(bc - bm)
                kt = k * jnp.exp(bm - bc)
                qb = qa * jnp.exp(bc)
                kd = k * jnp.exp(bl - bc)
                st = st_ref[b, h]
                sts_ref[b, 0, h] = st
                a = jnp.where(causal, _dot_nt(qt, kt), 0.0)
                o_ref[b, :, hs] = _dot(a, zi) + _dot_nt(qb, st)
                st_ref[b, h] = st * jnp.exp(bl) + _dot_tn(zi, kd)

    return _pcall(body, "hgrn_fwd", (nc,),
                  [pl.BlockSpec((nb, CHUNK, 4 * HG_WIDTH), lambda c: (0, c, 0)), _full((1, HG_WIDTH))],
                  [pl.BlockSpec((nb, CHUNK, HG_WIDTH), lambda c: (0, c, 0)),
                   pl.BlockSpec((nb, 1, HG_HEADS, HG_HEAD, HG_HEAD), lambda c: (0, c, 0, 0, 0))],
                  [_sds((nb, seq, HG_WIDTH)), _sds((nb, nc, HG_HEADS, HG_HEAD, HG_HEAD))],
                  scratch=[pltpu.VMEM((nb, HG_HEADS, HG_HEAD, HG_HEAD), F32)])(zh, lb)


def _head_rms(o):
    parts = []
    for h in range(HG_HEADS):
        oh = o[:, h * HG_HEAD:(h + 1) * HG_HEAD]
        r = lax.rsqrt(jnp.mean(oh * oh, axis=-1, keepdims=True) + EPS)
        parts.append(jnp.broadcast_to(r, oh.shape))
    return jnp.concatenate(parts, axis=1)


def _head_mean(v):
    parts = []
    for h in range(HG_HEADS):
        vh = v[:, h * HG_HEAD:(h + 1) * HG_HEAD]
        parts.append(jnp.broadcast_to(jnp.mean(vh, axis=-1, keepdims=True), vh.shape))
    return jnp.concatenate(parts, axis=1)


def _mix_fwd(x, y0, o, zh, zgt, w_glu, b_glu, gain, w_pa, w_pb, w_out, g_ffn, tm):
    t = x.shape[0]

    def body(x_ref, y0_ref, o_ref, zg_ref, zgt_ref, wglu_ref, bglu_ref, gain_ref, wpa_ref, wpb_ref, wout_ref,
             gffn_ref, x1_ref, u2_ref, pa_ref, pb_ref, ya2_ref, yb_ref):
        ya1 = _gelu(y0_ref[...])
        s = _sigmoid(_dot(ya1, wglu_ref[...]) + bglu_ref[...])
        ya2 = (ya1 * s).astype(BF16)
        ov = o_ref[...]
        zg = zg_ref[...]
        yb = (ov * _head_rms(ov) * gain_ref[...] * (zg * _sigmoid(zg))).astype(BF16)
        ya2_ref[...] = ya2
        yb_ref[...] = yb
        pa = jnp.dot(ya2, wpa_ref[...], preferred_element_type=F32)
        pb = jnp.dot(yb, wpb_ref[...], preferred_element_type=F32)
        pa_ref[...] = pa.astype(BF16)
        pb_ref[...] = pb.astype(BF16)
        m = _sigmoid(zgt_ref[:, 0:D_MODEL]) * pa + _sigmoid(zgt_ref[:, D_MODEL:]) * pb
        x1 = x_ref[...] + _dot(m, wout_ref[...])
        x1_ref[...] = x1
        r = lax.rsqrt(jnp.mean(x1 * x1, axis=-1, keepdims=True) + EPS)
        u2_ref[...] = (x1 * r * gffn_ref[...]).astype(BF16)

    row = lambda w: pl.BlockSpec((tm, w), lambda i: (i, 0))
    return _pcall(body, "mix_fwd", (t // tm,),
                  [row(D_MODEL), row(S5_WIDTH), row(HG_WIDTH), pl.BlockSpec((tm, HG_WIDTH), lambda i: (i, 3)),
                   row(2 * D_MODEL), _full((S5_WIDTH, S5_WIDTH)), _full((1, S5_WIDTH)), _full((1, HG_WIDTH)),
                   _full((S5_WIDTH, D_MODEL)), _full((HG_WIDTH, D_MODEL)), _full((D_MODEL, D_MODEL)),
                   _full((1, D_MODEL))],
                  [row(D_MODEL), row(D_MODEL), row(D_MODEL), row(D_MODEL), row(S5_WIDTH), row(HG_WIDTH)],
                  [_sds((t, D_MODEL)), _sds((t, D_MODEL), BF16), _sds((t, D_MODEL), BF16), _sds((t, D_MODEL), BF16),
                   _sds((t, S5_WIDTH), BF16), _sds((t, HG_WIDTH), BF16)],
                  )(x, y0, o, zh, zgt, w_glu, b_glu, gain, w_pa, w_pb, w_out, g_ffn)


FF_COLS = 256
FF_UP_TILE = 1408


def _ffn_up(u2, w_up, tm):
    t = u2.shape[0]
    n = 2 * D_FF

    def body(u_ref, w_ref, h_ref):
        h_ref[...] = jnp.dot(u_ref[...], w_ref[...], preferred_element_type=F32).astype(BF16)

    return _pcall(body, "ffn_up", (n // FF_UP_TILE, t // tm),
                  [pl.BlockSpec((tm, D_MODEL), lambda j, i: (i, 0)),
                   pl.BlockSpec((D_MODEL, FF_UP_TILE), lambda j, i: (0, j))],
                  pl.BlockSpec((tm, FF_UP_TILE), lambda j, i: (i, j)),
                  _sds((t, n), BF16))(u2, w_up)


HALO = 16


def _conv_cols(h_ref, halo_ref, valid, wc_ref, bc_ref, c0):
    cs = slice(c0, c0 + FF_COLS)
    cur = h_ref[:, cs].astype(F32)
    prev = jnp.where(valid, halo_ref[:, cs].astype(F32), 0.0)
    full = jnp.concatenate([prev, cur], axis=0)
    h1 = pltpu.roll(full, 1, axis=0)[HALO:]
    h2 = pltpu.roll(full, 2, axis=0)[HALO:]
    return h2 * wc_ref[0:1, cs] + h1 * wc_ref[1:2, cs] + cur * wc_ref[2:3, cs] + bc_ref[:, cs]


def _ffn_down_loss(h, x1, tgt, w_conv, b_conv, w_down, g_final, seq, tm):
    t = h.shape[0]
    tps = seq // tm
    n = 2 * D_FF

    def body(h_ref, halo_ref, x1_ref, tgt_ref, wc_ref, bc_ref, wd_ref, gf_ref,
             hc_ref, a_ref, dx2_ref, dx2b_ref, loss_ref, dgf_ref):
        i = pl.program_id(0)

        @pl.when(i == 0)
        def _():
            loss_ref[...] = jnp.zeros_like(loss_ref)
            dgf_ref[...] = jnp.zeros_like(dgf_ref)

        valid = (i % tps) != 0
        x2 = x1_ref[...]
        for j in range(D_FF // FF_COLS):
            gate = _conv_cols(h_ref, halo_ref, valid, wc_ref, bc_ref, j * FF_COLS)
            val = _conv_cols(h_ref, halo_ref, valid, wc_ref, bc_ref, D_FF + j * FF_COLS)
            hc_ref[:, j * FF_COLS:(j + 1) * FF_COLS] = gate.astype(BF16)
            hc_ref[:, D_FF + j * FF_COLS:D_FF + (j + 1) * FF_COLS] = val.astype(BF16)
            a = (gate * _sigmoid(gate) * val).astype(BF16)
            a_ref[:, j * FF_COLS:(j + 1) * FF_COLS] = a
            x2 = x2 + jnp.dot(a, wd_ref[j * FF_COLS:(j + 1) * FF_COLS, :], preferred_element_type=F32)
        r = lax.rsqrt(jnp.mean(x2 * x2, axis=-1, keepdims=True) + EPS)
        xn = x2 * r
        g = gf_ref[...]
        e = xn * g - tgt_ref[...]
        loss_ref[...] += (0.5 / D_MODEL) * jnp.sum(e * e).reshape(1, 1)
        dy = e * (1.0 / D_MODEL)
        dgf_ref[...] += jnp.sum(dy * xn, axis=0, keepdims=True)
        dxn = dy * g
        dx2 = r * (dxn - xn * jnp.mean(dxn * xn, axis=-1, keepdims=True))
        dx2_ref[...] = dx2
        dx2b_ref[...] = dx2.astype(BF16)

    row = lambda w: pl.BlockSpec((tm, w), lambda i: (i, 0))
    halo = pl.BlockSpec((HALO, n), lambda i: (jnp.maximum(i * (tm // HALO) - 1, 0), 0))
    return _pcall(body, "ffn_down_loss", (t // tm,),
                  [row(n), halo, row(D_MODEL), row(D_MODEL), _full((CONV_W, n)), _full((1, n)),
                   _full((D_FF, D_MODEL)), _full((1, D_MODEL))],
                  [row(n), row(D_FF), row(D_MODEL), row(D_MODEL), _full((1, 1)), _full((1, D_MODEL))],
                  [_sds((t, n), BF16), _sds((t, D_FF), BF16), _sds((t, D_MODEL)), _sds((t, D_MODEL), BF16),
                   _sds((1, 1)), _sds((1, D_MODEL))],
                  )(h, h, x1, tgt, w_conv, b_conv, w_down, g_final)


def _wgrad(a, b, name, tn, out_dtype=F32, transpose_out=False, band=None):
    t, m = a.shape
    n = b.shape[1] if band is None else band
    nbands = 1 if band is None else b.shape[1] // band

    def body(a_ref, b_ref, o_ref):
        r = _dot_tn(a_ref[...], b_ref[...])
        o_ref[...] = (r.T if transpose_out else r).astype(out_dtype)

    if transpose_out:
        out_spec, out_shape = pl.BlockSpec((n, tn), lambda i: (0, i)), _sds((n, m), out_dtype)
    else:
        out_spec, out_shape = pl.BlockSpec((tn, n), lambda i: (i, 0)), _sds((m, n), out_dtype)
    return _pcall(body, name, (m // tn,),
                  [pl.BlockSpec((t, tn), lambda i: (0, i)), pl.BlockSpec((t, n), lambda i: (0, i % nbands))],
                  out_spec, out_shape)(a, b)


def _ffn_bwd_act(dx2b, hc, w_down, tm):
    t = hc.shape[0]
    n = 2 * D_FF

    def body(dx2_ref, hc_ref, wd_ref, dhc_ref, dbc_ref):
        @pl.when(pl.program_id(0) == 0)
        def _():
            dbc_ref[...] = jnp.zeros_like(dbc_ref)

        dx2 = dx2_ref[...]
        for j in range(D_FF // FF_COLS):
            gs = slice(j * FF_COLS, (j + 1) * FF_COLS)
            vs = slice(D_FF + j * FF_COLS, D_FF + (j + 1) * FF_COLS)
            gate = hc_ref[:, gs].astype(F32)
            val = hc_ref[:, vs].astype(F32)
            da = _dot_nt(dx2, wd_ref[gs, :])
            sg = _sigmoid(gate)
            dgate = da * val * (sg * (1.0 + gate * (1.0 - sg)))
            dval = da * (gate * sg)
            dhc_ref[:, gs] = dgate.astype(BF16)
            dhc_ref[:, vs] = dval.astype(BF16)
            dbc_ref[:, gs] += jnp.sum(dgate, axis=0, keepdims=True)
            dbc_ref[:, vs] += jnp.sum(dval, axis=0, keepdims=True)

    row = lambda w: pl.BlockSpec((tm, w), lambda i: (i, 0))
    return _pcall(body, "ffn_bwd_act", (t // tm,),
                  [row(D_MODEL), row(n), _full((D_FF, D_MODEL))],
                  [row(n), _full((1, n))],
                  [_sds((t, n), BF16), _sds((1, n))],
                  )(dx2b, hc, w_down)


def _ffn_bwd_up(dhc, h, dx2, x1, w_conv, w_up, g_ffn, seq, tm):
    t = dhc.shape[0]
    tps = seq // tm
    n = 2 * D_FF
    last = t // HALO - 1

    def body(dhc_ref, halo_ref, h_ref, dx2_ref, x1_ref, wc_ref, wu_ref, gf_ref,
             dh_ref, dx1_ref, dx1b_ref, dgf_ref, dwc_ref):
        i = pl.program_id(0)

        @pl.when(i == 0)
        def _():
            dgf_ref[...] = jnp.zeros_like(dgf_ref)
            dwc_ref[...] = jnp.zeros_like(dwc_ref)

        valid = ((i + 1) % tps) != 0
        du2 = jnp.zeros((tm, D_MODEL), F32)
        for j in range(n // FF_COLS):
            cs = slice(j * FF_COLS, (j + 1) * FF_COLS)
            cur = dhc_ref[:, cs].astype(F32)
            nxt = jnp.where(valid, halo_ref[:, cs].astype(F32), 0.0)
            full = jnp.concatenate([cur, nxt], axis=0)
            d1 = pltpu.roll(full, tm + HALO - 1, axis=0)[:tm]
            d2 = pltpu.roll(full, tm + HALO - 2, axis=0)[:tm]
            dh = (cur * wc_ref[2:3, cs] + d1 * wc_ref[1:2, cs] + d2 * wc_ref[0:1, cs]).astype(BF16)
            dh_ref[:, cs] = dh
            du2 = du2 + _dot_nt(dh, wu_ref[:, cs])
            hv = h_ref[:, cs].astype(F32)
            dwc_ref[0:1, cs] += jnp.sum(hv * d2, axis=0, keepdims=True)
            dwc_ref[1:2, cs] += jnp.sum(hv * d1, axis=0, keepdims=True)
            dwc_ref[2:3, cs] += jnp.sum(hv * cur, axis=0, keepdims=True)
        x1 = x1_ref[...]
        r = lax.rsqrt(jnp.mean(x1 * x1, axis=-1, keepdims=True) + EPS)
        xn = x1 * r
        dgf_ref[...] += jnp.sum(du2 * xn, axis=0, keepdims=True)
        dxn = du2 * gf_ref[...]
        dx1 = dx2_ref[...] + r * (dxn - xn * jnp.mean(dxn * xn, axis=-1, keepdims=True))
        dx1_ref[...] = dx1
        dx1b_ref[...] = dx1.astype(BF16)

    row = lambda w: pl.BlockSpec((tm, w), lambda i: (i, 0))
    halo = pl.BlockSpec((HALO, n), lambda i: (jnp.minimum((i + 1) * (tm // HALO), last), 0))
    return _pcall(body, "ffn_bwd_up", (t // tm,),
                  [row(n), halo, row(n), row(D_MODEL), row(D_MODEL), _full((CONV_W, n)), _full((D_MODEL, n)),
                   _full((1, D_MODEL))],
                  [row(n), row(D_MODEL), row(D_MODEL), _full((1, D_MODEL)), _full((CONV_W, n))],
                  [_sds((t, n), BF16), _sds((t, D_MODEL)), _sds((t, D_MODEL), BF16), _sds((1, D_MODEL)),
                   _sds((CONV_W, n))],
                  )(dhc, dhc, h, dx2, x1, w_conv, w_up, g_ffn)


def _mix_bwd(dx1, y0, o, zh, zgt, pa, pb, w_glu, b_glu, gain, w_pa, w_pb, w_out, tm):
    t = dx1.shape[0]

    def body(dx1_ref, y0_ref, o_ref, zg_ref, zgt_ref, pa_ref, pb_ref, wglu_ref, bglu_ref, gain_ref, wpa_ref,
             wpb_ref, wout_ref,
             dy0_ref, do_ref, dzg_ref, dzgt_ref, m_ref, dpa_ref, dpb_ref, ya1_ref, dpre_ref, dbglu_ref, dgain_ref):
        @pl.when(pl.program_id(0) == 0)
        def _():
            dbglu_ref[...] = jnp.zeros_like(dbglu_ref)
            dgain_ref[...] = jnp.zeros_like(dgain_ref)

        dm = _dot_nt(dx1_ref[...], wout_ref[...])
        sga = _sigmoid(zgt_ref[:, 0:D_MODEL])
        sgb = _sigmoid(zgt_ref[:, D_MODEL:])
        pa = pa_ref[...].astype(F32)
        pb = pb_ref[...].astype(F32)
        m_ref[...] = (sga * pa + sgb * pb).astype(BF16)
        dzgt_ref[:, 0:D_MODEL] = (dm * pa * sga * (1.0 - sga)).astype(BF16)
        dzgt_ref[:, D_MODEL:] = (dm * pb * sgb * (1.0 - sgb)).astype(BF16)
        dpa = (dm * sga).astype(BF16)
        dpb = (dm * sgb).astype(BF16)
        dpa_ref[...] = dpa
        dpb_ref[...] = dpb
        dya2 = _dot_nt(dpa, wpa_ref[...])
        dyb = _dot_nt(dpb, wpb_ref[...])
        y0 = y0_ref[...]
        ya1 = _gelu(y0)
        ya1_ref[...] = ya1.astype(BF16)
        s = _sigmoid(_dot(ya1, wglu_ref[...]) + bglu_ref[...])
        dpre = dya2 * ya1 * s * (1.0 - s)
        dpre_ref[...] = dpre.astype(BF16)
        dbglu_ref[...] += jnp.sum(dpre, axis=0, keepdims=True)
        dya1 = dya2 * s + _dot_nt(dpre, wglu_ref[...])
        dy0_ref[...] = dya1 * _gelu_grad(y0)
        ov = o_ref[...]
        zg = zg_ref[...]
        oh = ov * _head_rms(ov)
        on = oh * gain_ref[...]
        sz = _sigmoid(zg)
        dzg_ref[...] = (dyb * on * (sz * (1.0 + zg * (1.0 - sz)))).astype(BF16)
        don = dyb * (zg * sz)
        dgain_ref[...] += jnp.sum(don * oh, axis=0, keepdims=True)
        doh = don * gain_ref[...]
        do_ref[...] = _head_rms(ov) * (doh - oh * _head_mean(doh * oh))

    row = lambda w: pl.BlockSpec((tm, w), lambda i: (i, 0))
    return _pcall(body, "mix_bwd", (t // tm,),
                  [row(D_MODEL), row(S5_WIDTH), row(HG_WIDTH), pl.BlockSpec((tm, HG_WIDTH), lambda i: (i, 3)),
                   row(2 * D_MODEL), row(D_MODEL), row(D_MODEL), _full((S5_WIDTH, S5_WIDTH)), _full((1, S5_WIDTH)),
                   _full((1, HG_WIDTH)), _full((S5_WIDTH, D_MODEL)), _full((HG_WIDTH, D_MODEL)),
                   _full((D_MODEL, D_MODEL))],
                  [row(S5_WIDTH), row(HG_WIDTH), row(HG_WIDTH), row(2 * D_MODEL), row(D_MODEL), row(D_MODEL),
                   row(D_MODEL), row(S5_WIDTH), row(S5_WIDTH), _full((1, S5_WIDTH)), _full((1, HG_WIDTH))],
                  [_sds((t, S5_WIDTH)), _sds((t, HG_WIDTH)), _sds((t, HG_WIDTH), BF16), _sds((t, 2 * D_MODEL), BF16),
                   _sds((t, D_MODEL), BF16), _sds((t, D_MODEL), BF16), _sds((t, D_MODEL), BF16),
                   _sds((t, S5_WIDTH), BF16), _sds((t, S5_WIDTH), BF16), _sds((1, S5_WIDTH)), _sds((1, HG_WIDTH))],
                  )(dx1, y0, o, zh, zgt, pa, pb, w_glu, b_glu, gain, w_pa, w_pb, w_out)


def _s5_bwd(dy0, za, xs, cg, bbdt, lam, dskip, nb, seq, ts):
    t = za.shape[0]
    nts = seq // ts

    def body(dy0_ref, za_ref, xs_ref, halo_ref, cg_ref, bbdt_ref, lam_ref, d_ref,
             dza_ref, a_ref, dlam_ref, dd_ref, acc_ref, st_ref):
        b, j = pl.program_id(0), pl.program_id(1)

        @pl.when((b == 0) & (j == 0))
        def _():
            dlam_ref[...] = jnp.zeros_like(dlam_ref)
            dd_ref[...] = jnp.zeros_like(dd_ref)

        @pl.when(j == 0)
        def _():
            st_ref[...] = jnp.zeros_like(st_ref)

        dy0 = dy0_ref[...]
        for q in range(S5_BANDS):
            ch, st = _band(q)
            acc_ref[:, st] = _dot(dy0[:, ch], cg_ref[ch, st])
            acc_ref[:, _im(st)] = _dot(dy0[:, ch], cg_ref[ch, _im(st)])
        for cc in range(S5_N // S5_LANES):
            re = slice(cc * S5_LANES, (cc + 1) * S5_LANES)
            im = slice(S5_N + cc * S5_LANES, S5_N + (cc + 1) * S5_LANES)
            lr = lam_ref[0:1, re]
            li = lam_ref[1:2, re]

            def step(i, carry):
                ar, ai = carry
                row = ts - 1 - i
                nar = acc_ref[pl.ds(row, 1), re] + lr * ar + li * ai
                nai = acc_ref[pl.ds(row, 1), im] - li * ar + lr * ai
                acc_ref[pl.ds(row, 1), re] = nar
                acc_ref[pl.ds(row, 1), im] = nai
                return nar, nai

            ar, ai = lax.fori_loop(0, ts, step, (st_ref[0:1, re], st_ref[1:2, re]), unroll=8)
            st_ref[0:1, re] = ar
            st_ref[1:2, re] = ai
        av = acc_ref[...]
        a_ref[...] = av.astype(BF16)
        first = jnp.where(j == nts - 1, 0.0, halo_ref[SUBLANES - 1:SUBLANES, :])
        rows = lax.broadcasted_iota(jnp.int32, (ts, 2 * S5_N), 0)
        xp = jnp.where(rows == 0, first, pltpu.roll(xs_ref[...], 1, axis=0))
        ar, ai = av[:, :S5_N], av[:, S5_N:]
        xr, xi = xp[:, :S5_N], xp[:, S5_N:]
        dlam_ref[0:1, :] += jnp.sum(ar * xr + ai * xi, axis=0, keepdims=True)
        dlam_ref[1:2, :] += jnp.sum(ai * xr - ar * xi, axis=0, keepdims=True)
        for q in range(S5_BANDS):
            ch, st = _band(q)
            dza_ref[:, ch] = (_dot(a_ref[:, st], bbdt_ref[st, ch]) + _dot(a_ref[:, _im(st)], bbdt_ref[_im(st), ch])
                              + d_ref[:, ch] * dy0[:, ch]).astype(BF16)
        dd_ref[...] += jnp.sum(dy0 * za_ref[...], axis=0, keepdims=True)

    tile = lambda b, j: b * nts + (nts - 1 - j)
    tok = lambda w: pl.BlockSpec((ts, w), lambda b, j: (tile(b, j), 0))
    halo = pl.BlockSpec((SUBLANES, 2 * S5_N),
                        lambda b, j: (jnp.maximum(tile(b, j) * (ts // SUBLANES) - 1, 0), 0))
    return _pcall(body, "s5_bwd", (nb, nts),
                  [tok(S5_WIDTH), tok(S5_WIDTH), tok(2 * S5_N), halo, _full((S5_WIDTH, 2 * S5_N)),
                   _full((2 * S5_N, S5_WIDTH)), _full((2, S5_N)), _full((1, S5_WIDTH))],
                  [tok(S5_WIDTH), tok(2 * S5_N), _full((2, S5_N)), _full((1, S5_WIDTH))],
                  [_sds((t, S5_WIDTH), BF16), _sds((t, 2 * S5_N), BF16), _sds((2, S5_N)), _sds((1, S5_WIDTH))],
                  scratch=[pltpu.VMEM((ts, 2 * S5_N), F32), pltpu.VMEM((2, S5_N), F32)],
                  )(dy0, za, xs, xs, cg, bbdt, lam, dskip)


def _hgrn_bwd(zh, do, sts, lb, nb, seq):
    nc = seq // CHUNK

    def body(zh_ref, do_ref, sts_ref, lb_ref, dz_ref, dlb_ref, dst_ref):
        @pl.when(pl.program_id(0) == 0)
        def _():
            dst_ref[...] = jnp.zeros_like(dst_ref)
            dlb_ref[...] = jnp.zeros_like(dlb_ref)

        row = lax.broadcasted_iota(jnp.int32, (CHUNK, CHUNK), 0)
        causal = row >= lax.broadcasted_iota(jnp.int32, (CHUNK, CHUNK), 1)
        last_row = lax.broadcasted_iota(jnp.int32, (CHUNK, HG_HEAD), 0) == CHUNK - 1
        for b in range(nb):
            for h in range(HG_HEADS):
                hs = slice(h * HG_HEAD, (h + 1) * HG_HEAD)
                zq = zh_ref[b, :, h * HG_HEAD:(h + 1) * HG_HEAD]
                zf = zh_ref[b, :, HG_WIDTH + h * HG_HEAD:HG_WIDTH + (h + 1) * HG_HEAD]
                zi = zh_ref[b, :, 2 * HG_WIDTH + h * HG_HEAD:2 * HG_WIDTH + (h + 1) * HG_HEAD]
                lbh = lb_ref[:, hs]
                sf, f, sq, qa, bc, bm, bl = _hgrn_gates(zq, zf, lbh)
                k = 1.0 - f
                e_qt = jnp.exp(bc - bm)
                e_kt = jnp.exp(bm - bc)
                e_b = jnp.exp(bc)
                e_kd = jnp.exp(bl - bc)
                e_l = jnp.exp(bl)
                qt, kt, qb, kd = qa * e_qt, k * e_kt, qa * e_b, k * e_kd
                a = jnp.where(causal, _dot_nt(qt, kt), 0.0)
                st = sts_ref[b, 0, h]
                dst = dst_ref[b, h]
                dov = do_ref[b, :, hs]
                da = jnp.where(causal, _dot_nt(dov, zi), 0.0)
                dqt = _hdot(da, kt)
                dkt = _hdot_tn(da, qt)
                dqb = _hdot(dov, st)
                di = _dot_tn(a, dov) + _dot_nt(kd, dst)
                dkd = _hdot(zi, dst)
                de_l = jnp.sum(dst * st, axis=0, keepdims=True)
                dst_ref[b, h] = dst * e_l + _dot_tn(dov, qb)
                dqa = dqt * e_qt + dqb * e_b
                dk = dkt * e_kt + dkd * e_kd
                dbl = jnp.sum(dkd * kd, axis=0, keepdims=True) + de_l * e_l
                db = dqt * qt - dkt * kt + dqb * qb - dkd * kd + jnp.where(last_row, dbl, 0.0)
                df = _cumsum_rows(db, reverse=True) / f - dk
                dzq = dqa * QSCALE * (sq * (1.0 + zq * (1.0 - sq)))
                dzf = df * (1.0 - lbh) * sf * (1.0 - sf)
                dz_ref[b, :, h * HG_HEAD:(h + 1) * HG_HEAD] = dzq.astype(BF16)
                dz_ref[b, :, HG_WIDTH + h * HG_HEAD:HG_WIDTH + (h + 1) * HG_HEAD] = dzf.astype(BF16)
                dz_ref[b, :, 2 * HG_WIDTH + h * HG_HEAD:2 * HG_WIDTH + (h + 1) * HG_HEAD] = di.astype(BF16)
                dlb_ref[:, hs] += jnp.sum(df * (1.0 - sf), axis=0, keepdims=True)

    rev = lambda c: nc - 1 - c
    return _pcall(body, "hgrn_bwd", (nc,),
                  [pl.BlockSpec((nb, CHUNK, 4 * HG_WIDTH), lambda c: (0, rev(c), 0)),
                   pl.BlockSpec((nb, CHUNK, HG_WIDTH), lambda c: (0, rev(c), 0)),
                   pl.BlockSpec((nb, 1, HG_HEADS, HG_HEAD, HG_HEAD), lambda c: (0, rev(c), 0, 0, 0)),
                   _full((1, HG_WIDTH))],
                  [pl.BlockSpec((nb, CHUNK, 3 * HG_WIDTH), lambda c: (0, rev(c), 0)), _full((1, HG_WIDTH))],
                  [_sds((nb, seq, 3 * HG_WIDTH), BF16), _sds((1, HG_WIDTH))],
                  scratch=[pltpu.VMEM((nb, HG_HEADS, HG_HEAD, HG_HEAD), F32)])(zh, do, sts, lb)


def _in_proj_bwd(dza, dzh, dzg, dzgt, dx1, x, g_mix, w_in, tm):
    t = x.shape[0]

    def body(dza_ref, dzh_ref, dzg_ref, dzgt_ref, dx1_ref, x_ref, g_ref, w_ref, dz_ref, dx_ref, dg_ref):
        @pl.when(pl.program_id(0) == 0)
        def _():
            dg_ref[...] = jnp.zeros_like(dg_ref)

        c1, c2, c3 = S5_WIDTH, S5_WIDTH + 3 * HG_WIDTH, S5_WIDTH + 4 * HG_WIDTH
        dz_ref[:, 0:c1] = dza_ref[...]
        dz_ref[:, c1:c2] = dzh_ref[...]
        dz_ref[:, c2:c3] = dzg_ref[...]
        dz_ref[:, c3:] = dzgt_ref[...]
        du = _dot_nt(dz_ref[...], w_ref[...])
        xv = x_ref[...]
        r = lax.rsqrt(jnp.mean(xv * xv, axis=-1, keepdims=True) + EPS)
        xn = xv * r
        dg_ref[...] += jnp.sum(du * xn, axis=0, keepdims=True)
        dxn = du * g_ref[...]
        dx_ref[...] = dx1_ref[...] + r * (dxn - xn * jnp.mean(dxn * xn, axis=-1, keepdims=True))

    row = lambda w: pl.BlockSpec((tm, w), lambda i: (i, 0))
    return _pcall(body, "in_proj_bwd", (t // tm,),
                  [row(S5_WIDTH), row(3 * HG_WIDTH), row(HG_WIDTH), row(2 * D_MODEL), row(D_MODEL), row(D_MODEL),
                   _full((1, D_MODEL)), _full((D_MODEL, N_IN))],
                  [row(N_IN), row(D_MODEL), _full((1, D_MODEL))],
                  [_sds((t, N_IN), BF16), _sds((t, D_MODEL)), _sds((1, D_MODEL))],
                  )(dza, dzh, dzg, dzgt, dx1, x, g_mix, w_in)


def _local_step(x3, tgt3, weights, sp, emit, emit_small):
    nb, seq, _ = x3.shape
    t = nb * seq
    tm = _token_tile(seq)
    x = x3.reshape(t, D_MODEL)
    tgt = tgt3.reshape(t, D_MODEL)
    row = lambda v: v.reshape(1, -1)

    a_re = sp["s5_a_re"].reshape(S5_N, 1)
    a_im = sp["s5_a_im"].reshape(S5_N, 1)
    ldt = jnp.repeat(sp["s5_log_dt"].reshape(S5_GROUPS), S5_STATE).reshape(S5_N, 1)
    b_re = sp["s5_b_re"].reshape(S5_N, S5_GROUP)
    b_im = sp["s5_b_im"].reshape(S5_N, S5_GROUP)
    lr, li, bb_re, bb_im, lb = _params_fwd(a_re, a_im, ldt, b_re, b_im, sp["hg_lb_logits"])
    lam = jnp.concatenate([lr.reshape(1, S5_N), li.reshape(1, S5_N)], axis=0)
    gps = lambda m: m.reshape(S5_GROUPS, S5_STATE, S5_GROUP)
    bbt = jnp.concatenate([_blockdiag(gps(bb_re).transpose(0, 2, 1)), _blockdiag(gps(bb_im).transpose(0, 2, 1))],
                          axis=1).astype(BF16)
    ct = jnp.concatenate([_blockdiag(sp["s5_c_re"].transpose(0, 2, 1)), -_blockdiag(sp["s5_c_im"].transpose(0, 2, 1))],
                         axis=0).astype(BF16)

    g_mix, g_ffn, g_final = row(sp["g_mix"]), row(sp["g_ffn"]), row(sp["g_final"])
    b_glu, gain, dskip, b_conv = row(sp["b_glu"]), row(sp["hg_norm_gain"]), row(sp["s5_d"]), row(sp["b_conv"])

    w_in = weights("in", ct)["w_in"]
    u, za, zh, zgt = _in_proj(x, g_mix, w_in, tm)
    xs, y0 = _s5_fwd(za, bbt, lam, ct, dskip, nb, seq, tm)
    o3, sts = _hgrn_fwd(zh.reshape(nb, seq, 4 * HG_WIDTH), lb, nb, seq)
    o = o3.reshape(t, HG_WIDTH)
    wm = weights("mix", o3)
    x1, u2, pa, pb, ya2, yb = _mix_fwd(x, y0, o, zh, zgt, wm["w_glu"], b_glu, gain, wm["w_pa"], wm["w_pb"],
                                       wm["w_out"], g_ffn, tm)
    wf = weights("ffn", u2)
    h = _ffn_up(u2, wf["w_up"], tm)
    hc, a, dx2, dx2b, loss, dg_final = _ffn_down_loss(h, x1, tgt, wf["w_conv"], b_conv, wf["w_down"], g_final,
                                                      seq, tm)

    wgrad = functools.partial(_wgrad, tn=256, out_dtype=BF16)
    dhc, db_conv = _ffn_bwd_act(dx2b, hc, wf["w_down"], tm)
    sent = emit({"w_down": wgrad(a, dx2b, "dw_down")})
    dh, dx1, dx1b, dg_ffn, dw_conv = _ffn_bwd_up(dhc, h, dx2, x1, wf["w_conv"], wf["w_up"], g_ffn + sent, seq, tm)
    sent = emit({"w_up": wgrad(dh, u2, "dw_up", transpose_out=True), "w_conv": dw_conv})
    (dy0, do, dzg, dzgt, m, dpa, dpb, ya1, dpre, db_glu, dgain) = _mix_bwd(
        dx1, y0, o, zh, zgt, pa, pb, wm["w_glu"], b_glu + sent, gain, wm["w_pa"], wm["w_pb"], wm["w_out"], tm)
    sent = emit({"w_out": wgrad(m, dx1b, "dw_out"), "w_pa": wgrad(ya2, dpa, "dw_pa"),
                 "w_pb": wgrad(yb, dpb, "dw_pb"), "w_glu": wgrad(ya1, dpre, "dw_glu")})
    dzh3, dlb = _hgrn_bwd(zh.reshape(nb, seq, 4 * HG_WIDTH), do.reshape(nb, seq, HG_WIDTH), sts, lb + sent, nb, seq)
    dza, a_s5, dlam, dd = _s5_bwd(dy0, za, xs, ct.T, bbt.T, lam, dskip, nb, seq, tm)
    band = HG_HEAD
    dbb_band = _wgrad(a_s5, za, "dbb_s5", 512, band=band)
    dc_band = _wgrad(xs, dy0, "dc_s5", 512, band=band)

    dbb_re = _diag_blocks(dbb_band[:S5_N], S5_STATE, S5_GROUP).reshape(S5_N, S5_GROUP)
    dbb_im = _diag_blocks(dbb_band[S5_N:], S5_STATE, S5_GROUP).reshape(S5_N, S5_GROUP)
    dc_re = _diag_blocks(dc_band[:S5_N], S5_STATE, S5_GROUP).transpose(0, 2, 1)
    dc_im = -_diag_blocks(dc_band[S5_N:], S5_STATE, S5_GROUP).transpose(0, 2, 1)
    da_re, da_im, dldt, db_re, db_im, dlogits = _params_bwd(
        a_re, a_im, ldt, b_re, b_im, sp["hg_lb_logits"],
        dlam[0].reshape(S5_N, 1), dlam[1].reshape(S5_N, 1), dbb_re, dbb_im, dlb)
    sent = emit_small({"s5_a_re": da_re, "s5_a_im": da_im, "s5_log_dt": dldt, "s5_b_re": db_re, "s5_b_im": db_im,
                       "s5_c_re": dc_re, "s5_c_im": dc_im, "s5_d": dd, "b_glu": db_glu, "hg_lb_logits": dlogits,
                       "hg_norm_gain": dgain, "g_ffn": dg_ffn, "b_conv": db_conv, "g_final": dg_final, "loss": loss})

    dz, dx, dg_mix = _in_proj_bwd(dza, dzh3.reshape(t, 3 * HG_WIDTH), dzg, dzgt, dx1, x, g_mix + sent, w_in, tm)
    dw_in = wgrad(dz, u, "dw_in", transpose_out=True)
    return dx.reshape(nb, seq, D_MODEL), dw_in, dg_mix


def _mesh_peers():
    x, y, c = lax.axis_index("x"), lax.axis_index("y"), lax.axis_index("c")
    peers = []
    for k in range(1, N_DEV):
        px, py, pc = (1 - x if k & 4 else x), (1 - y if k & 2 else y), (1 - c if k & 1 else c)
        peers.append((k, (px, py, pc), 4 * px + 2 * py + pc))
    return 4 * x + 2 * y + c, peers


_HBM = pl.BlockSpec(memory_space=pltpu.HBM)
_SEM = pl.BlockSpec(memory_space=pltpu.SEMAPHORE)


def _exchange_start(name, operands, after):
    n = len(operands)
    me = 4 * lax.axis_index("x") + 2 * lax.axis_index("y") + lax.axis_index("c")
    flags = [per_peer for _, per_peer in operands]
    srcs, lands = [], []
    for arr, per_peer in operands:
        own = lax.dynamic_index_in_dim(arr, me, 0, keepdims=True) if per_peer else arr[None]
        land = lax.dynamic_update_slice_in_dim(lax.empty((N_DEV,) + own.shape[1:], arr.dtype), own, me, 0)
        srcs.append(pltpu.with_memory_space_constraint(arr, pltpu.HBM))
        lands.append(pltpu.with_memory_space_constraint(land, pltpu.HBM))
    copies = (N_DEV - 1) * n

    def body(*refs):
        src_refs, land_refs = refs[:n], refs[n:2 * n]
        send_sems, recv_sems = refs[2 * n + 1], refs[2 * n + 2]
        token = refs[-1]
        my_slab, peers = _mesh_peers()
        for k, peer, slab in peers:
            for i in range(n):
                s = (k - 1) * n + i
                pltpu.make_async_remote_copy(
                    src_ref=src_refs[i].at[slab] if flags[i] else src_refs[i], dst_ref=land_refs[i].at[my_slab],
                    send_sem=send_sems.at[s], recv_sem=recv_sems.at[s], device_id=peer,
                    device_id_type=pl.DeviceIdType.MESH).start()
        token[...] = jnp.zeros_like(token)

    outs = pl.pallas_call(
        body, name=name,
        out_shape=(pltpu.SemaphoreType.DMA((copies,)), pltpu.SemaphoreType.DMA((copies,)),
                   *[pltpu.HBM(a.shape, a.dtype) for a in srcs], *[pltpu.HBM(a.shape, a.dtype) for a in lands],
                   _sds((SUBLANES, 128))),
        in_specs=[_HBM] * (2 * n) + [pl.BlockSpec(memory_space=pl.ANY)],
        out_specs=(_SEM, _SEM, *[_HBM] * (2 * n), pl.BlockSpec(memory_space=pltpu.VMEM)),
        input_output_aliases={i: 2 + i for i in range(2 * n)},
        compiler_params=pltpu.CompilerParams(has_side_effects=pltpu.SideEffectType.DATAFLOW_SIDE_EFFECTING),
    )(*srcs, *lands, after)
    state = (flags, outs[0], outs[1], outs[2:2 + n], outs[2 + n:2 + 2 * n])
    return state, outs[-1]


def _exchange_wait(name, state, after):
    flags, send_sems, recv_sems, srcs, lands = state
    n = len(flags)

    def body(*refs):
        src_refs, land_refs = refs[:n], refs[n:2 * n]
        send_ref, recv_ref = refs[2 * n], refs[2 * n + 1]
        _, peers = _mesh_peers()
        for k, peer, slab in peers:
            for i in range(n):
                s = (k - 1) * n + i
                copy = pltpu.make_async_remote_copy(
                    src_ref=src_refs[i].at[slab] if flags[i] else src_refs[i], dst_ref=land_refs[i].at[slab],
                    send_sem=send_ref.at[s], recv_sem=recv_ref.at[s], device_id=peer,
                    device_id_type=pl.DeviceIdType.MESH)
                copy.wait_send()
                copy.wait_recv()

    outs = pl.pallas_call(
        body, name=name,
        out_shape=(*[pltpu.HBM(a.shape, a.dtype) for a in srcs], *[pltpu.HBM(a.shape, a.dtype) for a in lands]),
        in_specs=[_HBM] * (2 * n) + [_SEM, _SEM, pl.BlockSpec(memory_space=pl.ANY)],
        out_specs=tuple([_HBM] * (2 * n)),
        input_output_aliases={i: i for i in range(2 * n)},
        compiler_params=pltpu.CompilerParams(has_side_effects=pltpu.SideEffectType.DATAFLOW_SIDE_EFFECTING),
    )(*srcs, *lands, send_sems, recv_sems, after)
    return list(outs[n:])


def _join_cols(parts, name, tr):
    _, r, c = parts.shape

    def body(p_ref, o_ref):
        for j in range(N_DEV):
            o_ref[:, j * c:(j + 1) * c] = p_ref[j]

    return _pcall(body, name, (r // tr,), [pl.BlockSpec((N_DEV, tr, c), lambda i: (0, i, 0))],
                  pl.BlockSpec((tr, N_DEV * c), lambda i: (i, 0)), _sds((r, N_DEV * c), parts.dtype))(parts)


def _split_cols(full, name, tr):
    r, c = full.shape[0], full.shape[1] // N_DEV

    def body(f_ref, o_ref):
        for j in range(N_DEV):
            o_ref[j] = f_ref[:, j * c:(j + 1) * c]

    return _pcall(body, name, (r // tr,), [pl.BlockSpec((tr, N_DEV * c), lambda i: (i, 0))],
                  pl.BlockSpec((N_DEV, tr, c), lambda i: (0, i, 0)), _sds((N_DEV, r, c), full.dtype))(full)


def _adamw(parts, w, m, v, name, tile):
    rows, cols = w.shape

    def body(p_ref, w_ref, m_ref, v_ref, g_out, d_out, m_out, v_out):
        g = p_ref[0].astype(F32)
        for k in range(1, N_DEV):
            g = g + p_ref[k].astype(F32)
        m1 = ADAM_B1 * m_ref[...] + (1.0 - ADAM_B1) * g
        v1 = ADAM_B2 * v_ref[...] + (1.0 - ADAM_B2) * (g * g)
        m_hat = m1 / (1.0 - ADAM_B1 ** ADAM_STEP)
        v_hat = v1 / (1.0 - ADAM_B2 ** ADAM_STEP)
        g_out[...] = g
        d_out[...] = -ADAM_LR * (m_hat / (jnp.sqrt(v_hat) + ADAM_EPS) + ADAM_WD * w_ref[...])
        m_out[...] = m1
        v_out[...] = v1

    row = pl.BlockSpec((tile, cols), lambda i: (i, 0))
    return _pcall(body, name, (rows // tile,),
                  [pl.BlockSpec((N_DEV, tile, cols), lambda i: (0, i, 0)), row, row, row],
                  [row, row, row, row], [_sds((rows, cols))] * 4)(parts, w, m, v)


BIG = {
    "w_in": ((D_MODEL, N_IN // N_DEV), True, 256),
    "w_glu": ((S5_WIDTH // N_DEV, S5_WIDTH), False, S5_WIDTH // N_DEV),
    "w_pa": ((S5_WIDTH, D_MODEL // N_DEV), True, S5_WIDTH),
    "w_pb": ((HG_WIDTH, D_MODEL // N_DEV), True, HG_WIDTH),
    "w_out": ((D_MODEL // N_DEV, D_MODEL), False, D_MODEL // N_DEV),
    "w_up": ((D_MODEL, 2 * D_FF // N_DEV), True, 256),
    "w_conv": ((CONV_W, 2 * D_FF // N_DEV), True, CONV_W),
    "w_down": ((D_FF // N_DEV, D_MODEL), False, D_FF // N_DEV // 2),
}
UNALIGNED_COLS = ("w_in", "w_up", "w_conv")


def _join_shards(n, parts):
    (a, b), by_cols, _ = BIG[n]
    if not by_cols:
        return parts.reshape(N_DEV * a, b)
    if n in UNALIGNED_COLS:
        return _join_cols(parts, "join_" + n, min(a, 256))
    return parts.transpose(1, 0, 2).reshape(a, N_DEV * b)


def _split_shards(n, full):
    (a, b), by_cols, _ = BIG[n]
    if not by_cols:
        return full.reshape(N_DEV, a, b)
    if n in UNALIGNED_COLS:
        return _split_cols(full, "split_" + n, min(a, 256))
    return full.reshape(a, N_DEV, b).transpose(1, 0, 2)


PACKED = tuple((n, shp) for n, shp in SMALL if n != "g_mix") + (("loss", (1,)),)


def _pack_small(d):
    flat = jnp.concatenate([d[n].reshape(-1) for n, _ in PACKED])
    return jnp.pad(flat, (0, SMALL_ROWS * PACK_W - flat.shape[0])).reshape(SMALL_ROWS, PACK_W)


def _unpack_small(p):
    flat = p.reshape(-1)
    out, off = {}, 0
    for n, shp in PACKED:
        size = math.prod(shp)
        out[n] = flat[off:off + size].reshape(shp)
        off += size
    return out


def kernel(x, g_mix, w_in, s5_a_re, s5_a_im, s5_log_dt, s5_b_re, s5_b_im, s5_c_re, s5_c_im, s5_d, w_glu, b_glu, hg_lb_logits, hg_norm_gain, w_pa, w_pb, w_out, g_ffn, w_up, w_conv, b_conv, w_down, g_final, loss_target, m_g_mix, m_w_in, m_s5_a_re, m_s5_a_im, m_s5_log_dt, m_s5_b_re, m_s5_b_im, m_s5_c_re, m_s5_c_im, m_s5_d, m_w_glu, m_b_glu, m_hg_lb_logits, m_hg_norm_gain, m_w_pa, m_w_pb, m_w_out, m_g_ffn, m_w_up, m_w_conv, m_b_conv, m_w_down, m_g_final, v_g_mix, v_w_in, v_s5_a_re, v_s5_a_im, v_s5_log_dt, v_s5_b_re, v_s5_b_im, v_s5_c_re, v_s5_c_im, v_s5_d, v_w_glu, v_b_glu, v_hg_lb_logits, v_hg_norm_gain, v_w_pa, v_w_pb, v_w_out, v_g_ffn, v_w_up, v_w_conv, v_b_conv, v_w_down, v_g_final):
    given = dict(locals())
    small_names = [n for n, _ in SMALL]

    pay = {n: given[n][0] if n == "w_conv" else given[n][0].astype(BF16) for n in BIG}
    groups = {"in": ["w_in"], "mix": ["w_glu", "w_pa", "w_pb", "w_out"], "ffn": ["w_up", "w_down", "w_conv"]}
    gathers, order = {}, pay["w_in"]
    for grp, names in groups.items():
        gathers[grp], order = _exchange_start("gather_" + grp + "_start", [(pay[n], False) for n in names], order)

    def weights(grp, after):
        got = _exchange_wait("gather_" + grp + "_wait", gathers[grp], after)
        return {n: _join_shards(n, g) for n, g in zip(groups[grp], got)}

    in_flight = []

    def emit(grads):
        names = list(grads)
        state, token = _exchange_start("grads_" + names[0] + "_start",
                                       [(_split_shards(n, grads[n]), True) for n in names], grads[names[0]])
        in_flight.append((names, state))
        return token[0, 0]

    def emit_small(grads):
        pack = _pack_small(grads)
        state, token = _exchange_start("grads_small_start", [(pack, False)], pack)
        in_flight.append((["small"], state))
        return token[0, 0]

    sp = {n: (given[n] if n in ("g_final", "hg_lb_logits") else given[n][0]) for n in small_names}
    sp["g_mix"] = sp["g_mix"] + order[0, 0]
    dx, dw_in, dg_mix = _local_step(x, loss_target, weights, sp, emit, emit_small)
    last, token = _exchange_start("grads_w_in_start", [(_split_shards("w_in", dw_in), True), (dg_mix, False)], dw_in)

    res = {}

    def update_big(n, part):
        shard, _, tile = BIG[n]
        r4 = _adamw(part, given[n][0], given["m_" + n][0], given["v_" + n][0], "adamw_" + n, tile)
        res[n] = [r.reshape((1,) + shard) for r in r4]
        return r4[0]

    after = token
    for names, state in in_flight:
        for n, part in zip(names, _exchange_wait("grads_" + names[0] + "_wait", state, after)):
            if n != "small":
                after = update_big(n, part)
                continue
            zero = jnp.zeros((1,), F32)
            rs4 = _adamw(part, _pack_small({**{k: given[k] for k in small_names}, "loss": zero}),
                         _pack_small({**{k: given["m_" + k] for k in small_names}, "loss": zero}),
                         _pack_small({**{k: given["v_" + k] for k in small_names}, "loss": zero}),
                         "adamw_small", SMALL_ROWS)
            small4 = [_unpack_small(r) for r in rs4]
            for k in small_names:
                if k != "g_mix":
                    res[k] = [us[k] for us in small4]
            total_loss = small4[0]["loss"][0]
            after = rs4[0]
    part_in, part_gmix = _exchange_wait("grads_w_in_wait", last, after)
    update_big("w_in", part_in)
    res["g_mix"] = _adamw(part_gmix, given["g_mix"], m_g_mix, v_g_mix, "adamw_g_mix", 1)
    return (total_loss, dx, *[res[n][0] for n in WEIGHT_ORDER], *[res[n][1] for n in WEIGHT_ORDER],
            *[res[n][2] for n in WEIGHT_ORDER], *[res[n][3] for n in WEIGHT_ORDER])
```

```python
import functools
import math

import jax
import jax.numpy as jnp
from jax import lax
from jax.experimental import pallas as pl
from jax.experimental.pallas import tpu as pltpu

F32 = jnp.float32
BF16 = jnp.bfloat16

D_MODEL = 1024
S5_WIDTH = 512
S5_GROUP = 16
S5_GROUPS = 32
S5_STATE = 64
S5_N = S5_GROUPS * S5_STATE
HG_WIDTH = 512
HG_HEAD = 128
HG_HEADS = 4
D_FF = 2816
CONV_W = 3
CHUNK = 64
N_IN = S5_WIDTH + 4 * HG_WIDTH + 2 * D_MODEL
EPS = 1e-6
QSCALE = HG_HEAD ** -0.5

ADAM_LR = 0.001
ADAM_B1 = 0.9
ADAM_B2 = 0.999
ADAM_EPS = 1e-08
ADAM_WD = 0.01
ADAM_STEP = 10

N_DEV = 8
V7X_VMEM_BYTES = 64 * 1024 * 1024
VMEM_LIMIT = V7X_VMEM_BYTES * 7 // 8
SUBLANES = 8
PACK_W = 1024

SMALL = (
    ("g_mix", (1, D_MODEL)),
    ("s5_a_re", (1, S5_GROUPS, S5_STATE)),
    ("s5_a_im", (1, S5_GROUPS, S5_STATE)),
    ("s5_log_dt", (1, S5_GROUPS)),
    ("s5_b_re", (1, S5_GROUPS, S5_STATE, S5_GROUP)),
    ("s5_b_im", (1, S5_GROUPS, S5_STATE, S5_GROUP)),
    ("s5_c_re", (1, S5_GROUPS, S5_GROUP, S5_STATE)),
    ("s5_c_im", (1, S5_GROUPS, S5_GROUP, S5_STATE)),
    ("s5_d", (1, S5_WIDTH)),
    ("b_glu", (1, S5_WIDTH)),
    ("hg_lb_logits", (2, HG_WIDTH)),
    ("hg_norm_gain", (1, HG_WIDTH)),
    ("g_ffn", (1, D_MODEL)),
    ("b_conv", (1, 2 * D_FF)),
    ("g_final", (D_MODEL,)),
)
SMALL_ROWS = 144
WEIGHT_ORDER = ("g_mix", "w_in", "s5_a_re", "s5_a_im", "s5_log_dt", "s5_b_re", "s5_b_im", "s5_c_re", "s5_c_im",
                "s5_d", "w_glu", "b_glu", "hg_lb_logits", "hg_norm_gain", "w_pa", "w_pb", "w_out", "g_ffn",
                "w_up", "w_conv", "b_conv", "w_down", "g_final")


def _pcall(body, name, grid, in_specs, out_specs, out_shape, scratch=()):
    return pl.pallas_call(
        body, name=name, grid=grid, in_specs=in_specs, out_specs=out_specs, out_shape=out_shape,
        scratch_shapes=list(scratch),
        compiler_params=pltpu.CompilerParams(dimension_semantics=("arbitrary",) * len(grid),
                                             vmem_limit_bytes=VMEM_LIMIT),
    )


def _full(shape):
    return pl.BlockSpec(shape, lambda *_: (0,) * len(shape))


def _sds(shape, dtype=F32):
    return jax.ShapeDtypeStruct(shape, dtype)


def _dot(a, b):
    return jnp.dot(a.astype(BF16), b.astype(BF16), preferred_element_type=F32)


def _dot_nt(a, b):
    return lax.dot_general(a.astype(BF16), b.astype(BF16), (((1,), (1,)), ((), ())), preferred_element_type=F32)


def _dot_tn(a, b):
    return lax.dot_general(a.astype(BF16), b.astype(BF16), (((0,), (0,)), ((), ())), preferred_element_type=F32)


def _hdot(a, b):
    return jnp.dot(a, b, preferred_element_type=F32, precision=lax.Precision.HIGHEST)


def _hdot_tn(a, b):
    return lax.dot_general(a, b, (((0,), (0,)), ((), ())), preferred_element_type=F32,
                           precision=lax.Precision.HIGHEST)


def _sigmoid(x):
    return jax.nn.sigmoid(x)


GELU_C = math.sqrt(2.0 / math.pi)
GELU_A = 0.044715


def _gelu(x):
    return 0.5 * x * (1.0 + jnp.tanh(GELU_C * (x + GELU_A * (x * x * x))))


def _gelu_grad(x):
    t = jnp.tanh(GELU_C * (x + GELU_A * (x * x * x)))
    return 0.5 * (1.0 + t) + 0.5 * x * (1.0 - t * t) * (GELU_C * (1.0 + 3.0 * GELU_A * x * x))


def _cumsum_rows(v, reverse=False):
    n = v.shape[0]
    row = lax.broadcasted_iota(jnp.int32, v.shape, 0)
    s = 1
    while s < n:
        if reverse:
            v = v + jnp.where(row < n - s, pltpu.roll(v, n - s, axis=0), 0.0)
        else:
            v = v + jnp.where(row >= s, pltpu.roll(v, s, axis=0), 0.0)
        s *= 2
    return v


def _token_tile(seq):
    return min(256, seq)


def _s5_disc(a_re, a_im, ldt, b_re, b_im):
    dt = jnp.exp(ldt)
    mag = jnp.exp(a_re * dt)
    ang = a_im * dt
    lb_re = mag * jnp.cos(ang)
    lb_im = mag * jnp.sin(ang)
    den = a_re * a_re + a_im * a_im
    n_re = lb_re - 1.0
    n_im = lb_im
    co_re = (n_re * a_re + n_im * a_im) / den
    co_im = (n_im * a_re - n_re * a_im) / den
    bb_re = co_re * b_re - co_im * b_im
    bb_im = co_re * b_im + co_im * b_re
    return lb_re, lb_im, bb_re, bb_im


def _params_fwd(a_re, a_im, ldt, b_re, b_im, logits):
    def body(are, aim, ld, bre, bim, lg, lr_o, li_o, bbr_o, bbi_o, lb_o):
        lr, li, bbr, bbi = _s5_disc(are[...], aim[...], ld[...], bre[...], bim[...])
        lr_o[...] = lr
        li_o[...] = li
        bbr_o[...] = bbr
        bbi_o[...] = bbi
        lb_o[...] = _sigmoid(lg[0:1, :] - lg[1:2, :])

    col, mat = (S5_N, 1), (S5_N, S5_GROUP)
    return _pcall(body, "params_fwd", (1,),
                  [_full(col), _full(col), _full(col), _full(mat), _full(mat), _full((2, HG_WIDTH))],
                  [_full(col), _full(col), _full(mat), _full(mat), _full((1, HG_WIDTH))],
                  [_sds(col), _sds(col), _sds(mat), _sds(mat), _sds((1, HG_WIDTH))])(a_re, a_im, ldt, b_re, b_im, logits)


def _params_bwd(a_re, a_im, ldt, b_re, b_im, logits, dlr, dli, dbbr, dbbi, dlb):
    def body(are, aim, ld, bre, bim, lg, dlr_r, dli_r, dbbr_r, dbbi_r, dlb_r,
             dare_o, daim_o, dld_o, dbre_o, dbim_o, dlg_o):
        _, vjp = jax.vjp(_s5_disc, are[...], aim[...], ld[...], bre[...], bim[...])
        dare, daim, dld, dbre, dbim = vjp((dlr_r[...], dli_r[...], dbbr_r[...], dbbi_r[...]))
        dare_o[...] = dare
        daim_o[...] = daim
        dbre_o[...] = dbre
        dbim_o[...] = dbim
        for g in range(S5_GROUPS):
            dld_o[g:g + 1, :] = jnp.sum(dld[g * S5_STATE:(g + 1) * S5_STATE, :], axis=0, keepdims=True)
        lb = _sigmoid(lg[0:1, :] - lg[1:2, :])
        d0 = dlb_r[...] * lb * (1.0 - lb)
        dlg_o[0:1, :] = d0
        dlg_o[1:2, :] = -d0

    col, mat = (S5_N, 1), (S5_N, S5_GROUP)
    return _pcall(body, "params_bwd", (1,),
                  [_full(col), _full(col), _full(col), _full(mat), _full(mat), _full((2, HG_WIDTH)),
                   _full(col), _full(col), _full(mat), _full(mat), _full((1, HG_WIDTH))],
                  [_full(col), _full(col), _full((S5_GROUPS, 1)), _full(mat), _full(mat), _full((2, HG_WIDTH))],
                  [_sds(col), _sds(col), _sds((S5_GROUPS, 1)), _sds(mat), _sds(mat), _sds((2, HG_WIDTH))],
                  )(a_re, a_im, ldt, b_re, b_im, logits, dlr, dli, dbbr, dbbi, dlb)


def _blockdiag(m):
    g, r, c = m.shape
    eye = jnp.eye(g, dtype=m.dtype)
    return (m[:, :, None, :] * eye[:, None, :, None]).reshape(g * r, g * c)


def _diag_blocks(band, r, c):
    g, nb = band.shape[0] // r, band.shape[1] // c
    on_diag = (jnp.arange(g) % nb)[:, None, None, None] == jnp.arange(nb)[None, None, :, None]
    return jnp.sum(jnp.where(on_diag, band.reshape(g, r, nb, c), 0.0), axis=2)


def _in_proj(x, g_mix, w_in, tm):
    t = x.shape[0]

    def body(x_ref, g_ref, w_ref, u_ref, za_ref, zh_ref, zg_ref):
        xv = x_ref[...]
        r = lax.rsqrt(jnp.mean(xv * xv, axis=-1, keepdims=True) + EPS)
        u = (xv * r * g_ref[...]).astype(BF16)
        u_ref[...] = u
        za_ref[...] = jnp.dot(u, w_ref[:, 0:S5_WIDTH], preferred_element_type=F32)
        zh_ref[...] = jnp.dot(u, w_ref[:, S5_WIDTH:S5_WIDTH + 4 * HG_WIDTH], preferred_element_type=F32)
        zg_ref[...] = jnp.dot(u, w_ref[:, S5_WIDTH + 4 * HG_WIDTH:], preferred_element_type=F32)

    row = lambda w: pl.BlockSpec((tm, w), lambda i: (i, 0))
    return _pcall(body, "in_proj", (t // tm,),
                  [row(D_MODEL), _full((1, D_MODEL)), _full((D_MODEL, N_IN))],
                  [row(D_MODEL), row(S5_WIDTH), row(4 * HG_WIDTH), row(2 * D_MODEL)],
                  [_sds((t, D_MODEL), BF16), _sds((t, S5_WIDTH)), _sds((t, 4 * HG_WIDTH)), _sds((t, 2 * D_MODEL))],
                  )(x, g_mix, w_in)


S5_LANES = 512
S5_BANDS = 4


def _band(q):
    return (slice(q * S5_WIDTH // S5_BANDS, (q + 1) * S5_WIDTH // S5_BANDS),
            slice(q * S5_N // S5_BANDS, (q + 1) * S5_N // S5_BANDS))


def _im(st):
    return slice(S5_N + st.start, S5_N + st.stop)


SCAN_UNROLL = 8


def _complex_scan(buf_ref, lam_ref, st_ref, ts, reverse):
    chunks = [slice(cc * S5_LANES, (cc + 1) * S5_LANES) for cc in range(S5_N // S5_LANES)]
    nch = len(chunks)
    wr = [lam_ref[0:1, re] for re in chunks]
    wi = [-lam_ref[1:2, re] if reverse else lam_ref[1:2, re] for re in chunks]

    def block(ib, carry):
        vr, vi = list(carry[:nch]), list(carry[nch:])
        first = ts - SCAN_UNROLL - ib * SCAN_UNROLL if reverse else ib * SCAN_UNROLL
        first = pl.multiple_of(first, SCAN_UNROLL)
        for k in range(SCAN_UNROLL):
            row = pl.ds(first + (SCAN_UNROLL - 1 - k if reverse else k), 1)
            for cc, re in enumerate(chunks):
                nr = wr[cc] * vr[cc] - wi[cc] * vi[cc] + buf_ref[row, re]
                ni = wr[cc] * vi[cc] + wi[cc] * vr[cc] + buf_ref[row, _im(re)]
                buf_ref[row, re] = nr
                buf_ref[row, _im(re)] = ni
                vr[cc], vi[cc] = nr, ni
        return tuple(vr + vi)

    init = tuple(st_ref[0:1, re] for re in chunks) + tuple(st_ref[1:2, re] for re in chunks)
    last = lax.fori_loop(0, ts // SCAN_UNROLL, block, init)
    for cc, re in enumerate(chunks):
        st_ref[0:1, re] = last[cc]
        st_ref[1:2, re] = last[nch + cc]


def _s5_fwd(za, bbt, lam, ct, dskip, nb, seq, ts):
    t = za.shape[0]
    nts = seq // ts

    def body(za_ref, bbt_ref, lam_ref, ct_ref, d_ref, xs_ref, y_ref, st_ref):
        @pl.when(pl.program_id(1) == 0)
        def _():
            st_ref[...] = jnp.zeros_like(st_ref)

        zav = za_ref[...]
        for q in range(S5_BANDS):
            ch, st = _band(q)
            xs_ref[:, st] = _dot(zav[:, ch], bbt_ref[ch, st])
            xs_ref[:, _im(st)] = _dot(zav[:, ch], bbt_ref[ch, _im(st)])
        _complex_scan(xs_ref, lam_ref, st_ref, ts, reverse=False)
        for q in range(S5_BANDS):
            ch, st = _band(q)
            y_ref[:, ch] = (_dot(xs_ref[:, st], ct_ref[st, ch]) + _dot(xs_ref[:, _im(st)], ct_ref[_im(st), ch])
                            + d_ref[:, ch] * zav[:, ch])

    tok = lambda w: pl.BlockSpec((ts, w), lambda b, j: (b * nts + j, 0))
    return _pcall(body, "s5_fwd", (nb, nts),
                  [tok(S5_WIDTH), _full((S5_WIDTH, 2 * S5_N)), _full((2, S5_N)), _full((2 * S5_N, S5_WIDTH)),
                   _full((1, S5_WIDTH))],
                  [tok(2 * S5_N), tok(S5_WIDTH)],
                  [_sds((t, 2 * S5_N)), _sds((t, S5_WIDTH))],
                  scratch=[pltpu.VMEM((2, S5_N), F32)])(za, bbt, lam, ct, dskip)


def _hgrn_gates(zq, zf, lbh):
    sf = _sigmoid(zf)
    f = lbh + (1.0 - lbh) * sf
    sq = _sigmoid(zq)
    qa = zq * sq * QSCALE
    bc = _cumsum_rows(jnp.log(f))
    bm = bc[CHUNK // 2 - 1:CHUNK // 2, :]
    bl = bc[CHUNK - 1:CHUNK, :]
    return sf, f, sq, qa, bc, bm, bl


def _hgrn_fwd(zh, lb, nb, seq):
    nc = seq // CHUNK

    def body(zh_ref, lb_ref, o_ref, sts_ref, st_ref):
        @pl.when(pl.program_id(0) == 0)
        def _():
            st_ref[...] = jnp.zeros_like(st_ref)

        causal = (lax.broadcasted_iota(jnp.int32, (CHUNK, CHUNK), 0)
                  >= lax.broadcasted_iota(jnp.int32, (CHUNK, CHUNK), 1))
        for b in range(nb):
            for h in range(HG_HEADS):
                hs = slice(h * HG_HEAD, (h + 1) * HG_HEAD)
                zq = zh_ref[b, :, h * HG_HEAD:(h + 1) * HG_HEAD]
                zf = zh_ref[b, :, HG_WIDTH + h * HG_HEAD:HG_WIDTH + (h + 1) * HG_HEAD]
                zi = zh_ref[b, :, 2 * HG_WIDTH + h * HG_HEAD:2 * HG_WIDTH + (h + 1) * HG_HEAD]
                _, f, _, qa, bc, bm, bl = _hgrn_gates(zq, zf, lb_ref[:, hs])
                k = 1.0 - f
                qt = qa * jnp.exp(bc - bm)
                kt = k * jnp.exp(bm - bc)
                qb = qa * jnp.exp(bc)
                kd = k * jnp.exp(bl - bc)
                st = st_ref[b, h]
                sts_ref[b, 0, h] = st
                a = jnp.where(causal, _dot_nt(qt, kt), 0.0)
                o_ref[b, :, hs] = _dot(a, zi) + _dot_nt(qb, st)
                st_ref[b, h] = st * jnp.exp(bl) + _dot_tn(zi, kd)

    return _pcall(body, "hgrn_fwd", (nc,),
                  [pl.BlockSpec((nb, CHUNK, 4 * HG_WIDTH), lambda c: (0, c, 0)), _full((1, HG_WIDTH))],
                  [pl.BlockSpec((nb, CHUNK, HG_WIDTH), lambda c: (0, c, 0)),
                   pl.BlockSpec((nb, 1, HG_HEADS, HG_HEAD, HG_HEAD), lambda c: (0, c, 0, 0, 0))],
                  [_sds((nb, seq, HG_WIDTH)), _sds((nb, nc, HG_HEADS, HG_HEAD, HG_HEAD))],
                  scratch=[pltpu.VMEM((nb, HG_HEADS, HG_HEAD, HG_HEAD), F32)])(zh, lb)


def _head_rms(o):
    parts = []
    for h in range(HG_HEADS):
        oh = o[:, h * HG_HEAD:(h + 1) * HG_HEAD]
        r = lax.rsqrt(jnp.mean(oh * oh, axis=-1, keepdims=True) + EPS)
        parts.append(jnp.broadcast_to(r, oh.shape))
    return jnp.concatenate(parts, axis=1)


def _head_mean(v):
    parts = []
    for h in range(HG_HEADS):
        vh = v[:, h * HG_HEAD:(h + 1) * HG_HEAD]
        parts.append(jnp.broadcast_to(jnp.mean(vh, axis=-1, keepdims=True), vh.shape))
    return jnp.concatenate(parts, axis=1)


def _mix_fwd(x, y0, o, zh, zgt, w_glu, b_glu, gain, w_pa, w_pb, w_out, g_ffn, tm):
    t = x.shape[0]

    def body(x_ref, y0_ref, o_ref, zg_ref, zgt_ref, wglu_ref, bglu_ref, gain_ref, wpa_ref, wpb_ref, wout_ref,
             gffn_ref, x1_ref, u2_ref, pa_ref, pb_ref, ya2_ref, yb_ref):
        ya1 = _gelu(y0_ref[...])
        s = _sigmoid(_dot(ya1, wglu_ref[...]) + bglu_ref[...])
        ya2 = (ya1 * s).astype(BF16)
        ov = o_ref[...]
        zg = zg_ref[...]
        yb = (ov * _head_rms(ov) * gain_ref[...] * (zg * _sigmoid(zg))).astype(BF16)
        ya2_ref[...] = ya2
        yb_ref[...] = yb
        pa = jnp.dot(ya2, wpa_ref[...], preferred_element_type=F32)
        pb = jnp.dot(yb, wpb_ref[...], preferred_element_type=F32)
        pa_ref[...] = pa.astype(BF16)
        pb_ref[...] = pb.astype(BF16)
        m = _sigmoid(zgt_ref[:, 0:D_MODEL]) * pa + _sigmoid(zgt_ref[:, D_MODEL:]) * pb
        x1 = x_ref[...] + _dot(m, wout_ref[...])
        x1_ref[...] = x1
        r = lax.rsqrt(jnp.mean(x1 * x1, axis=-1, keepdims=True) + EPS)
        u2_ref[...] = (x1 * r * gffn_ref[...]).astype(BF16)

    row = lambda w: pl.BlockSpec((tm, w), lambda i: (i, 0))
    return _pcall(body, "mix_fwd", (t // tm,),
                  [row(D_MODEL), row(S5_WIDTH), row(HG_WIDTH), pl.BlockSpec((tm, HG_WIDTH), lambda i: (i, 3)),
                   row(2 * D_MODEL), _full((S5_WIDTH, S5_WIDTH)), _full((1, S5_WIDTH)), _full((1, HG_WIDTH)),
                   _full((S5_WIDTH, D_MODEL)), _full((HG_WIDTH, D_MODEL)), _full((D_MODEL, D_MODEL)),
                   _full((1, D_MODEL))],
                  [row(D_MODEL), row(D_MODEL), row(D_MODEL), row(D_MODEL), row(S5_WIDTH), row(HG_WIDTH)],
                  [_sds((t, D_MODEL)), _sds((t, D_MODEL), BF16), _sds((t, D_MODEL), BF16), _sds((t, D_MODEL), BF16),
                   _sds((t, S5_WIDTH), BF16), _sds((t, HG_WIDTH), BF16)],
                  )(x, y0, o, zh, zgt, w_glu, b_glu, gain, w_pa, w_pb, w_out, g_ffn)


FF_COLS = 256
FF_UP_TILE = 1408


def _ffn_up(u2, w_up, tm):
    t = u2.shape[0]
    n = 2 * D_FF

    def body(u_ref, w_ref, h_ref):
        h_ref[...] = jnp.dot(u_ref[...], w_ref[...], preferred_element_type=F32).astype(BF16)

    return _pcall(body, "ffn_up", (n // FF_UP_TILE, t // tm),
                  [pl.BlockSpec((tm, D_MODEL), lambda j, i: (i, 0)),
                   pl.BlockSpec((D_MODEL, FF_UP_TILE), lambda j, i: (0, j))],
                  pl.BlockSpec((tm, FF_UP_TILE), lambda j, i: (i, j)),
                  _sds((t, n), BF16))(u2, w_up)


HALO = 16


def _conv_cols(h_ref, halo_ref, valid, wc_ref, bc_ref, c0):
    cs = slice(c0, c0 + FF_COLS)
    cur = h_ref[:, cs].astype(F32)
    prev = jnp.where(valid, halo_ref[:, cs].astype(F32), 0.0)
    full = jnp.concatenate([prev, cur], axis=0)
    h1 = pltpu.roll(full, 1, axis=0)[HALO:]
    h2 = pltpu.roll(full, 2, axis=0)[HALO:]
    return h2 * wc_ref[0:1, cs] + h1 * wc_ref[1:2, cs] + cur * wc_ref[2:3, cs] + bc_ref[:, cs]


def _ffn_down_loss(h, x1, tgt, w_conv, b_conv, w_down, g_final, seq, tm):
    t = h.shape[0]
    tps = seq // tm
    n = 2 * D_FF

    def body(h_ref, halo_ref, x1_ref, tgt_ref, wc_ref, bc_ref, wd_ref, gf_ref,
             hc_ref, a_ref, dx2_ref, dx2b_ref, loss_ref, dgf_ref):
        i = pl.program_id(0)

        @pl.when(i == 0)
        def _():
            loss_ref[...] = jnp.zeros_like(loss_ref)
            dgf_ref[...] = jnp.zeros_like(dgf_ref)

        valid = (i % tps) != 0
        x2 = x1_ref[...]
        for j in range(D_FF // FF_COLS):
            gate = _conv_cols(h_ref, halo_ref, valid, wc_ref, bc_ref, j * FF_COLS)
            val = _conv_cols(h_ref, halo_ref, valid, wc_ref, bc_ref, D_FF + j * FF_COLS)
            hc_ref[:, j * FF_COLS:(j + 1) * FF_COLS] = gate.astype(BF16)
            hc_ref[:, D_FF + j * FF_COLS:D_FF + (j + 1) * FF_COLS] = val.astype(BF16)
            a = (gate * _sigmoid(gate) * val).astype(BF16)
            a_ref[:, j * FF_COLS:(j + 1) * FF_COLS] = a
            x2 = x2 + jnp.dot(a, wd_ref[j * FF_COLS:(j + 1) * FF_COLS, :], preferred_element_type=F32)
        r = lax.rsqrt(jnp.mean(x2 * x2, axis=-1, keepdims=True) + EPS)
        xn = x2 * r
        g = gf_ref[...]
        e = xn * g - tgt_ref[...]
        loss_ref[...] += (0.5 / D_MODEL) * jnp.sum(e * e).reshape(1, 1)
        dy = e * (1.0 / D_MODEL)
        dgf_ref[...] += jnp.sum(dy * xn, axis=0, keepdims=True)
        dxn = dy * g
        dx2 = r * (dxn - xn * jnp.mean(dxn * xn, axis=-1, keepdims=True))
        dx2_ref[...] = dx2
        dx2b_ref[...] = dx2.astype(BF16)

    row = lambda w: pl.BlockSpec((tm, w), lambda i: (i, 0))
    halo = pl.BlockSpec((HALO, n), lambda i: (jnp.maximum(i * (tm // HALO) - 1, 0), 0))
    return _pcall(body, "ffn_down_loss", (t // tm,),
                  [row(n), halo, row(D_MODEL), row(D_MODEL), _full((CONV_W, n)), _full((1, n)),
                   _full((D_FF, D_MODEL)), _full((1, D_MODEL))],
                  [row(n), row(D_FF), row(D_MODEL), row(D_MODEL), _full((1, 1)), _full((1, D_MODEL))],
                  [_sds((t, n), BF16), _sds((t, D_FF), BF16), _sds((t, D_MODEL)), _sds((t, D_MODEL), BF16),
                   _sds((1, 1)), _sds((1, D_MODEL))],
                  )(h, h, x1, tgt, w_conv, b_conv, w_down, g_final)


def _wgrad(a, b, name, tn, out_dtype=F32, transpose_out=False, band=None):
    t, m = a.shape
    n = b.shape[1] if band is None else band
    nbands = 1 if band is None else b.shape[1] // band

    def body(a_ref, b_ref, o_ref):
        r = _dot_tn(a_ref[...], b_ref[...])
        o_ref[...] = (r.T if transpose_out else r).astype(out_dtype)

    if transpose_out:
        out_spec, out_shape = pl.BlockSpec((n, tn), lambda i: (0, i)), _sds((n, m), out_dtype)
    else:
        out_spec, out_shape = pl.BlockSpec((tn, n), lambda i: (i, 0)), _sds((m, n), out_dtype)
    return _pcall(body, name, (m // tn,),
                  [pl.BlockSpec((t, tn), lambda i: (0, i)), pl.BlockSpec((t, n), lambda i: (0, i % nbands))],
                  out_spec, out_shape)(a, b)


def _ffn_bwd_act(dx2b, hc, w_down, tm):
    t = hc.shape[0]
    n = 2 * D_FF

    def body(dx2_ref, hc_ref, wd_ref, dhc_ref, dbc_ref):
        @pl.when(pl.program_id(0) == 0)
        def _():
            dbc_ref[...] = jnp.zeros_like(dbc_ref)

        dx2 = dx2_ref[...]
        for j in range(D_FF // FF_COLS):
            gs = slice(j * FF_COLS, (j + 1) * FF_COLS)
            vs = slice(D_FF + j * FF_COLS, D_FF + (j + 1) * FF_COLS)
            gate = hc_ref[:, gs].astype(F32)
            val = hc_ref[:, vs].astype(F32)
            da = _dot_nt(dx2, wd_ref[gs, :])
            sg = _sigmoid(gate)
            dgate = da * val * (sg * (1.0 + gate * (1.0 - sg)))
            dval = da * (gate * sg)
            dhc_ref[:, gs] = dgate.astype(BF16)
            dhc_ref[:, vs] = dval.astype(BF16)
            dbc_ref[:, gs] += jnp.sum(dgate, axis=0, keepdims=True)
            dbc_ref[:, vs] += jnp.sum(dval, axis=0, keepdims=True)

    row = lambda w: pl.BlockSpec((tm, w), lambda i: (i, 0))
    return _pcall(body, "ffn_bwd_act", (t // tm,),
                  [row(D_MODEL), row(n), _full((D_FF, D_MODEL))],
                  [row(n), _full((1, n))],
                  [_sds((t, n), BF16), _sds((1, n))],
                  )(dx2b, hc, w_down)


def _ffn_bwd_up(dhc, h, dx2, x1, w_conv, w_up, g_ffn, seq, tm):
    t = dhc.shape[0]
    tps = seq // tm
    n = 2 * D_FF
    last = t // HALO - 1

    def body(dhc_ref, halo_ref, h_ref, dx2_ref, x1_ref, wc_ref, wu_ref, gf_ref,
             dh_ref, dx1_ref, dx1b_ref, dgf_ref, dwc_ref):
        i = pl.program_id(0)

        @pl.when(i == 0)
        def _():
            dgf_ref[...] = jnp.zeros_like(dgf_ref)
            dwc_ref[...] = jnp.zeros_like(dwc_ref)

        valid = ((i + 1) % tps) != 0
        du2 = jnp.zeros((tm, D_MODEL), F32)
        for j in range(n // FF_COLS):
            cs = slice(j * FF_COLS, (j + 1) * FF_COLS)
            cur = dhc_ref[:, cs].astype(F32)
            nxt = jnp.where(valid, halo_ref[:, cs].astype(F32), 0.0)
            full = jnp.concatenate([cur, nxt], axis=0)
            d1 = pltpu.roll(full, tm + HALO - 1, axis=0)[:tm]
            d2 = pltpu.roll(full, tm + HALO - 2, axis=0)[:tm]
            dh = (cur * wc_ref[2:3, cs] + d1 * wc_ref[1:2, cs] + d2 * wc_ref[0:1, cs]).astype(BF16)
            dh_ref[:, cs] = dh
            du2 = du2 + _dot_nt(dh, wu_ref[:, cs])
            hv = h_ref[:, cs].astype(F32)
            dwc_ref[0:1, cs] += jnp.sum(hv * d2, axis=0, keepdims=True)
            dwc_ref[1:2, cs] += jnp.sum(hv * d1, axis=0, keepdims=True)
            dwc_ref[2:3, cs] += jnp.sum(hv * cur, axis=0, keepdims=True)
        x1 = x1_ref[...]
        r = lax.rsqrt(jnp.mean(x1 * x1, axis=-1, keepdims=True) + EPS)
        xn = x1 * r
        dgf_ref[...] += jnp.sum(du2 * xn, axis=0, keepdims=True)
        dxn = du2 * gf_ref[...]
        dx1 = dx2_ref[...] + r * (dxn - xn * jnp.mean(dxn * xn, axis=-1, keepdims=True))
        dx1_ref[...] = dx1
        dx1b_ref[...] = dx1.astype(BF16)

    row = lambda w: pl.BlockSpec((tm, w), lambda i: (i, 0))
    halo = pl.BlockSpec((HALO, n), lambda i: (jnp.minimum((i + 1) * (tm // HALO), last), 0))
    return _pcall(body, "ffn_bwd_up", (t // tm,),
                  [row(n), halo, row(n), row(D_MODEL), row(D_MODEL), _full((CONV_W, n)), _full((D_MODEL, n)),
                   _full((1, D_MODEL))],
                  [row(n), row(D_MODEL), row(D_MODEL), _full((1, D_MODEL)), _full((CONV_W, n))],
                  [_sds((t, n), BF16), _sds((t, D_MODEL)), _sds((t, D_MODEL), BF16), _sds((1, D_MODEL)),
                   _sds((CONV_W, n))],
                  )(dhc, dhc, h, dx2, x1, w_conv, w_up, g_ffn)


def _mix_bwd(dx1, y0, o, zh, zgt, pa, pb, w_glu, b_glu, gain, w_pa, w_pb, w_out, tm):
    t = dx1.shape[0]

    def body(dx1_ref, y0_ref, o_ref, zg_ref, zgt_ref, pa_ref, pb_ref, wglu_ref, bglu_ref, gain_ref, wpa_ref,
             wpb_ref, wout_ref,
             dy0_ref, do_ref, dzg_ref, dzgt_ref, m_ref, dpa_ref, dpb_ref, ya1_ref, dpre_ref, dbglu_ref, dgain_ref):
        @pl.when(pl.program_id(0) == 0)
        def _():
            dbglu_ref[...] = jnp.zeros_like(dbglu_ref)
            dgain_ref[...] = jnp.zeros_like(dgain_ref)

        dm = _dot_nt(dx1_ref[...], wout_ref[...])
        sga = _sigmoid(zgt_ref[:, 0:D_MODEL])
        sgb = _sigmoid(zgt_ref[:, D_MODEL:])
        pa = pa_ref[...].astype(F32)
        pb = pb_ref[...].astype(F32)
        m_ref[...] = (sga * pa + sgb * pb).astype(BF16)
        dzgt_ref[:, 0:D_MODEL] = (dm * pa * sga * (1.0 - sga)).astype(BF16)
        dzgt_ref[:, D_MODEL:] = (dm * pb * sgb * (1.0 - sgb)).astype(BF16)
        dpa = (dm * sga).astype(BF16)
        dpb = (dm * sgb).astype(BF16)
        dpa_ref[...] = dpa
        dpb_ref[...] = dpb
        dya2 = _dot_nt(dpa, wpa_ref[...])
        dyb = _dot_nt(dpb, wpb_ref[...])
        y0 = y0_ref[...]
        ya1 = _gelu(y0)
        ya1_ref[...] = ya1.astype(BF16)
        s = _sigmoid(_dot(ya1, wglu_ref[...]) + bglu_ref[...])
        dpre = dya2 * ya1 * s * (1.0 - s)
        dpre_ref[...] = dpre.astype(BF16)
        dbglu_ref[...] += jnp.sum(dpre, axis=0, keepdims=True)
        dya1 = dya2 * s + _dot_nt(dpre, wglu_ref[...])
        dy0_ref[...] = dya1 * _gelu_grad(y0)
        ov = o_ref[...]
        zg = zg_ref[...]
        oh = ov * _head_rms(ov)
        on = oh * gain_ref[...]
        sz = _sigmoid(zg)
        dzg_ref[...] = (dyb * on * (sz * (1.0 + zg * (1.0 - sz)))).astype(BF16)
        don = dyb * (zg * sz)
        dgain_ref[...] += jnp.sum(don * oh, axis=0, keepdims=True)
        doh = don * gain_ref[...]
        do_ref[...] = _head_rms(ov) * (doh - oh * _head_mean(doh * oh))

    row = lambda w: pl.BlockSpec((tm, w), lambda i: (i, 0))
    return _pcall(body, "mix_bwd", (t // tm,),
                  [row(D_MODEL), row(S5_WIDTH), row(HG_WIDTH), pl.BlockSpec((tm, HG_WIDTH), lambda i: (i, 3)),
                   row(2 * D_MODEL), row(D_MODEL), row(D_MODEL), _full((S5_WIDTH, S5_WIDTH)), _full((1, S5_WIDTH)),
                   _full((1, HG_WIDTH)), _full((S5_WIDTH, D_MODEL)), _full((HG_WIDTH, D_MODEL)),
                   _full((D_MODEL, D_MODEL))],
                  [row(S5_WIDTH), row(HG_WIDTH), row(HG_WIDTH), row(2 * D_MODEL), row(D_MODEL), row(D_MODEL),
                   row(D_MODEL), row(S5_WIDTH), row(S5_WIDTH), _full((1, S5_WIDTH)), _full((1, HG_WIDTH))],
                  [_sds((t, S5_WIDTH)), _sds((t, HG_WIDTH)), _sds((t, HG_WIDTH), BF16), _sds((t, 2 * D_MODEL), BF16),
                   _sds((t, D_MODEL), BF16), _sds((t, D_MODEL), BF16), _sds((t, D_MODEL), BF16),
                   _sds((t, S5_WIDTH), BF16), _sds((t, S5_WIDTH), BF16), _sds((1, S5_WIDTH)), _sds((1, HG_WIDTH))],
                  )(dx1, y0, o, zh, zgt, pa, pb, w_glu, b_glu, gain, w_pa, w_pb, w_out)


def _s5_bwd(dy0, za, xs, cg, bbdt, lam, dskip, nb, seq, ts):
    t = za.shape[0]
    nts = seq // ts

    def body(dy0_ref, za_ref, xs_ref, halo_ref, cg_ref, bbdt_ref, lam_ref, d_ref,
             dza_ref, a_ref, dlam_ref, dd_ref, acc_ref, st_ref):
        b, j = pl.program_id(0), pl.program_id(1)

        @pl.when((b == 0) & (j == 0))
        def _():
            dlam_ref[...] = jnp.zeros_like(dlam_ref)
            dd_ref[...] = jnp.zeros_like(dd_ref)

        @pl.when(j == 0)
        def _():
            st_ref[...] = jnp.zeros_like(st_ref)

        dy0 = dy0_ref[...]
        for q in range(S5_BANDS):
            ch, st = _band(q)
            acc_ref[:, st] = _dot(dy0[:, ch], cg_ref[ch, st])
            acc_ref[:, _im(st)] = _dot(dy0[:, ch], cg_ref[ch, _im(st)])
        _complex_scan(acc_ref, lam_ref, st_ref, ts, reverse=True)
        av = acc_ref[...]
        a_ref[...] = av.astype(BF16)
        first = jnp.where(j == nts - 1, 0.0, halo_ref[SUBLANES - 1:SUBLANES, :])
        rows = lax.broadcasted_iota(jnp.int32, (ts, 2 * S5_N), 0)
        xp = jnp.where(rows == 0, first, pltpu.roll(xs_ref[...], 1, axis=0))
        ar, ai = av[:, :S5_N], av[:, S5_N:]
        xr, xi = xp[:, :S5_N], xp[:, S5_N:]
        dlam_ref[0:1, :] += jnp.sum(ar * xr + ai * xi, axis=0, keepdims=True)
        dlam_ref[1:2, :] += jnp.sum(ai * xr - ar * xi, axis=0, keepdims=True)
        for q in range(S5_BANDS):
            ch, st = _band(q)
            dza_ref[:, ch] = (_dot(a_ref[:, st], bbdt_ref[st, ch]) + _dot(a_ref[:, _im(st)], bbdt_ref[_im(st), ch])
                              + d_ref[:, ch] * dy0[:, ch]).astype(BF16)
        dd_ref[...] += jnp.sum(dy0 * za_ref[...], axis=0, keepdims=True)

    tile = lambda b, j: b * nts + (nts - 1 - j)
    tok = lambda w: pl.BlockSpec((ts, w), lambda b, j: (tile(b, j), 0))
    halo = pl.BlockSpec((SUBLANES, 2 * S5_N),
                        lambda b, j: (jnp.maximum(tile(b, j) * (ts // SUBLANES) - 1, 0), 0))
    return _pcall(body, "s5_bwd", (nb, nts),
                  [tok(S5_WIDTH), tok(S5_WIDTH), tok(2 * S5_N), halo, _full((S5_WIDTH, 2 * S5_N)),
                   _full((2 * S5_N, S5_WIDTH)), _full((2, S5_N)), _full((1, S5_WIDTH))],
                  [tok(S5_WIDTH), tok(2 * S5_N), _full((2, S5_N)), _full((1, S5_WIDTH))],
                  [_sds((t, S5_WIDTH), BF16), _sds((t, 2 * S5_N), BF16), _sds((2, S5_N)), _sds((1, S5_WIDTH))],
                  scratch=[pltpu.VMEM((ts, 2 * S5_N), F32), pltpu.VMEM((2, S5_N), F32)],
                  )(dy0, za, xs, xs, cg, bbdt, lam, dskip)


def _hgrn_bwd(zh, do, sts, lb, nb, seq):
    nc = seq // CHUNK

    def body(zh_ref, do_ref, sts_ref, lb_ref, dz_ref, dlb_ref, dst_ref):
        @pl.when(pl.program_id(0) == 0)
        def _():
            dst_ref[...] = jnp.zeros_like(dst_ref)
            dlb_ref[...] = jnp.zeros_like(dlb_ref)

        row = lax.broadcasted_iota(jnp.int32, (CHUNK, CHUNK), 0)
        causal = row >= lax.broadcasted_iota(jnp.int32, (CHUNK, CHUNK), 1)
        last_row = lax.broadcasted_iota(jnp.int32, (CHUNK, HG_HEAD), 0) == CHUNK - 1
        for b in range(nb):
            for h in range(HG_HEADS):
                hs = slice(h * HG_HEAD, (h + 1) * HG_HEAD)
                zq = zh_ref[b, :, h * HG_HEAD:(h + 1) * HG_HEAD]
                zf = zh_ref[b, :, HG_WIDTH + h * HG_HEAD:HG_WIDTH + (h + 1) * HG_HEAD]
                zi = zh_ref[b, :, 2 * HG_WIDTH + h * HG_HEAD:2 * HG_WIDTH + (h + 1) * HG_HEAD]
                lbh = lb_ref[:, hs]
                sf, f, sq, qa, bc, bm, bl = _hgrn_gates(zq, zf, lbh)
                k = 1.0 - f
                e_qt = jnp.exp(bc - bm)
                e_kt = jnp.exp(bm - bc)
                e_b = jnp.exp(bc)
                e_kd = jnp.exp(bl - bc)
                e_l = jnp.exp(bl)
                qt, kt, qb, kd = qa * e_qt, k * e_kt, qa * e_b, k * e_kd
                a = jnp.where(causal, _dot_nt(qt, kt), 0.0)
                st = sts_ref[b, 0, h]
                dst = dst_ref[b, h]
                dov = do_ref[b, :, hs]
                da = jnp.where(causal, _dot_nt(dov, zi), 0.0)
                dqt = _hdot(da, kt)
                dkt = _hdot_tn(da, qt)
                dqb = _hdot(dov, st)
                di = _dot_tn(a, dov) + _dot_nt(kd, dst)
                dkd = _hdot(zi, dst)
                de_l = jnp.sum(dst * st, axis=0, keepdims=True)
                dst_ref[b, h] = dst * e_l + _dot_tn(dov, qb)
                dqa = dqt * e_qt + dqb * e_b
                dk = dkt * e_kt + dkd * e_kd
                dbl = jnp.sum(dkd * kd, axis=0, keepdims=True) + de_l * e_l
                db = dqt * qt - dkt * kt + dqb * qb - dkd * kd + jnp.where(last_row, dbl, 0.0)
                df = _cumsum_rows(db, reverse=True) / f - dk
                dzq = dqa * QSCALE * (sq * (1.0 + zq * (1.0 - sq)))
                dzf = df * (1.0 - lbh) * sf * (1.0 - sf)
                dz_ref[b, :, h * HG_HEAD:(h + 1) * HG_HEAD] = dzq.astype(BF16)
                dz_ref[b, :, HG_WIDTH + h * HG_HEAD:HG_WIDTH + (h + 1) * HG_HEAD] = dzf.astype(BF16)
                dz_ref[b, :, 2 * HG_WIDTH + h * HG_HEAD:2 * HG_WIDTH + (h + 1) * HG_HEAD] = di.astype(BF16)
                dlb_ref[:, hs] += jnp.sum(df * (1.0 - sf), axis=0, keepdims=True)

    rev = lambda c: nc - 1 - c
    return _pcall(body, "hgrn_bwd", (nc,),
                  [pl.BlockSpec((nb, CHUNK, 4 * HG_WIDTH), lambda c: (0, rev(c), 0)),
                   pl.BlockSpec((nb, CHUNK, HG_WIDTH), lambda c: (0, rev(c), 0)),
                   pl.BlockSpec((nb, 1, HG_HEADS, HG_HEAD, HG_HEAD), lambda c: (0, rev(c), 0, 0, 0)),
                   _full((1, HG_WIDTH))],
                  [pl.BlockSpec((nb, CHUNK, 3 * HG_WIDTH), lambda c: (0, rev(c), 0)), _full((1, HG_WIDTH))],
                  [_sds((nb, seq, 3 * HG_WIDTH), BF16), _sds((1, HG_WIDTH))],
                  scratch=[pltpu.VMEM((nb, HG_HEADS, HG_HEAD, HG_HEAD), F32)])(zh, do, sts, lb)


def _in_proj_bwd(dza, dzh, dzg, dzgt, dx1, x, g_mix, w_in, tm):
    t = x.shape[0]

    def body(dza_ref, dzh_ref, dzg_ref, dzgt_ref, dx1_ref, x_ref, g_ref, w_ref, dz_ref, dx_ref, dg_ref):
        @pl.when(pl.program_id(0) == 0)
        def _():
            dg_ref[...] = jnp.zeros_like(dg_ref)

        c1, c2, c3 = S5_WIDTH, S5_WIDTH + 3 * HG_WIDTH, S5_WIDTH + 4 * HG_WIDTH
        dz_ref[:, 0:c1] = dza_ref[...]
        dz_ref[:, c1:c2] = dzh_ref[...]
        dz_ref[:, c2:c3] = dzg_ref[...]
        dz_ref[:, c3:] = dzgt_ref[...]
        du = _dot_nt(dz_ref[...], w_ref[...])
        xv = x_ref[...]
        r = lax.rsqrt(jnp.mean(xv * xv, axis=-1, keepdims=True) + EPS)
        xn = xv * r
        dg_ref[...] += jnp.sum(du * xn, axis=0, keepdims=True)
        dxn = du * g_ref[...]
        dx_ref[...] = dx1_ref[...] + r * (dxn - xn * jnp.mean(dxn * xn, axis=-1, keepdims=True))

    row = lambda w: pl.BlockSpec((tm, w), lambda i: (i, 0))
    return _pcall(body, "in_proj_bwd", (t // tm,),
                  [row(S5_WIDTH), row(3 * HG_WIDTH), row(HG_WIDTH), row(2 * D_MODEL), row(D_MODEL), row(D_MODEL),
                   _full((1, D_MODEL)), _full((D_MODEL, N_IN))],
                  [row(N_IN), row(D_MODEL), _full((1, D_MODEL))],
                  [_sds((t, N_IN), BF16), _sds((t, D_MODEL)), _sds((1, D_MODEL))],
                  )(dza, dzh, dzg, dzgt, dx1, x, g_mix, w_in)


def _local_step(x3, tgt3, weights, sp, emit, emit_small):
    nb, seq, _ = x3.shape
    t = nb * seq
    tm = _token_tile(seq)
    x = x3.reshape(t, D_MODEL)
    tgt = tgt3.reshape(t, D_MODEL)
    row = lambda v: v.reshape(1, -1)

    a_re = sp["s5_a_re"].reshape(S5_N, 1)
    a_im = sp["s5_a_im"].reshape(S5_N, 1)
    ldt = jnp.repeat(sp["s5_log_dt"].reshape(S5_GROUPS), S5_STATE).reshape(S5_N, 1)
    b_re = sp["s5_b_re"].reshape(S5_N, S5_GROUP)
    b_im = sp["s5_b_im"].reshape(S5_N, S5_GROUP)
    lr, li, bb_re, bb_im, lb = _params_fwd(a_re, a_im, ldt, b_re, b_im, sp["hg_lb_logits"])
    lam = jnp.concatenate([lr.reshape(1, S5_N), li.reshape(1, S5_N)], axis=0)
    gps = lambda m: m.reshape(S5_GROUPS, S5_STATE, S5_GROUP)
    bbt = jnp.concatenate([_blockdiag(gps(bb_re).transpose(0, 2, 1)), _blockdiag(gps(bb_im).transpose(0, 2, 1))],
                          axis=1).astype(BF16)
    ct = jnp.concatenate([_blockdiag(sp["s5_c_re"].transpose(0, 2, 1)), -_blockdiag(sp["s5_c_im"].transpose(0, 2, 1))],
                         axis=0).astype(BF16)
    cg = jnp.concatenate([_blockdiag(sp["s5_c_re"]), -_blockdiag(sp["s5_c_im"])], axis=1).astype(BF16)
    bbdt = jnp.concatenate([_blockdiag(gps(bb_re)), _blockdiag(gps(bb_im))], axis=0).astype(BF16)

    g_mix, g_ffn, g_final = row(sp["g_mix"]), row(sp["g_ffn"]), row(sp["g_final"])
    b_glu, gain, dskip, b_conv = row(sp["b_glu"]), row(sp["hg_norm_gain"]), row(sp["s5_d"]), row(sp["b_conv"])

    w_in = weights("in", ct)["w_in"]
    u, za, zh, zgt = _in_proj(x, g_mix, w_in, tm)
    xs, y0 = _s5_fwd(za, bbt, lam, ct, dskip, nb, seq, tm)
    o3, sts = _hgrn_fwd(zh.reshape(nb, seq, 4 * HG_WIDTH), lb, nb, seq)
    o = o3.reshape(t, HG_WIDTH)
    wm = weights("mix", o3)
    x1, u2, pa, pb, ya2, yb = _mix_fwd(x, y0, o, zh, zgt, wm["w_glu"], b_glu, gain, wm["w_pa"], wm["w_pb"],
                                       wm["w_out"], g_ffn, tm)
    wf = weights("ffn", u2)
    h = _ffn_up(u2, wf["w_up"], tm)
    hc, a, dx2, dx2b, loss, dg_final = _ffn_down_loss(h, x1, tgt, wf["w_conv"], b_conv, wf["w_down"], g_final,
                                                      seq, tm)

    wgrad = functools.partial(_wgrad, tn=256, out_dtype=BF16)
    dhc, db_conv = _ffn_bwd_act(dx2b, hc, wf["w_down"], tm)
    sent = emit({"w_down": wgrad(a, dx2b, "dw_down")})
    dh, dx1, dx1b, dg_ffn, dw_conv = _ffn_bwd_up(dhc, h, dx2, x1, wf["w_conv"], wf["w_up"], g_ffn + sent, seq, tm)
    sent = emit({"w_up": wgrad(dh, u2, "dw_up", transpose_out=True), "w_conv": dw_conv})
    (dy0, do, dzg, dzgt, m, dpa, dpb, ya1, dpre, db_glu, dgain) = _mix_bwd(
        dx1, y0, o, zh, zgt, pa, pb, wm["w_glu"], b_glu + sent, gain, wm["w_pa"], wm["w_pb"], wm["w_out"], tm)
    sent = emit({"w_out": wgrad(m, dx1b, "dw_out"), "w_pa": wgrad(ya2, dpa, "dw_pa"),
                 "w_pb": wgrad(yb, dpb, "dw_pb"), "w_glu": wgrad(ya1, dpre, "dw_glu")})
    dzh3, dlb = _hgrn_bwd(zh.reshape(nb, seq, 4 * HG_WIDTH), do.reshape(nb, seq, HG_WIDTH), sts, lb + sent, nb, seq)
    dza, a_s5, dlam, dd = _s5_bwd(dy0, za, xs, cg, bbdt, lam, dskip, nb, seq, tm)
    band = HG_HEAD
    dbb_band = _wgrad(a_s5, za, "dbb_s5", 512, band=band)
    dc_band = _wgrad(xs, dy0, "dc_s5", 512, band=band)

    dbb_re = _diag_blocks(dbb_band[:S5_N], S5_STATE, S5_GROUP).reshape(S5_N, S5_GROUP)
    dbb_im = _diag_blocks(dbb_band[S5_N:], S5_STATE, S5_GROUP).reshape(S5_N, S5_GROUP)
    dc_re = _diag_blocks(dc_band[:S5_N], S5_STATE, S5_GROUP).transpose(0, 2, 1)
    dc_im = -_diag_blocks(dc_band[S5_N:], S5_STATE, S5_GROUP).transpose(0, 2, 1)
    da_re, da_im, dldt, db_re, db_im, dlogits = _params_bwd(
        a_re, a_im, ldt, b_re, b_im, sp["hg_lb_logits"],
        dlam[0].reshape(S5_N, 1), dlam[1].reshape(S5_N, 1), dbb_re, dbb_im, dlb)
    sent = emit_small({"s5_a_re": da_re, "s5_a_im": da_im, "s5_log_dt": dldt, "s5_b_re": db_re, "s5_b_im": db_im,
                       "s5_c_re": dc_re, "s5_c_im": dc_im, "s5_d": dd, "b_glu": db_glu, "hg_lb_logits": dlogits,
                       "hg_norm_gain": dgain, "g_ffn": dg_ffn, "b_conv": db_conv, "g_final": dg_final, "loss": loss})

    dz, dx, dg_mix = _in_proj_bwd(dza, dzh3.reshape(t, 3 * HG_WIDTH), dzg, dzgt, dx1, x, g_mix + sent, w_in, tm)
    dw_in = wgrad(dz, u, "dw_in", transpose_out=True)
    return dx.reshape(nb, seq, D_MODEL), dw_in, dg_mix


def _mesh_peers():
    x, y, c = lax.axis_index("x"), lax.axis_index("y"), lax.axis_index("c")
    peers = []
    for k in range(1, N_DEV):
        px, py, pc = (1 - x if k & 4 else x), (1 - y if k & 2 else y), (1 - c if k & 1 else c)
        peers.append((k, (px, py, pc), 4 * px + 2 * py + pc))
    return 4 * x + 2 * y + c, peers


_HBM = pl.BlockSpec(memory_space=pltpu.HBM)
_SEM = pl.BlockSpec(memory_space=pltpu.SEMAPHORE)


def _exchange_start(name, operands, after):
    n = len(operands)
    me = 4 * lax.axis_index("x") + 2 * lax.axis_index("y") + lax.axis_index("c")
    flags = [per_peer for _, per_peer in operands]
    srcs, lands = [], []
    for arr, per_peer in operands:
        own = lax.dynamic_index_in_dim(arr, me, 0, keepdims=True) if per_peer else arr[None]
        land = lax.dynamic_update_slice_in_dim(lax.empty((N_DEV,) + own.shape[1:], arr.dtype), own, me, 0)
        srcs.append(pltpu.with_memory_space_constraint(arr, pltpu.HBM))
        lands.append(pltpu.with_memory_space_constraint(land, pltpu.HBM))
    copies = (N_DEV - 1) * n

    def body(*refs):
        src_refs, land_refs = refs[:n], refs[n:2 * n]
        send_sems, recv_sems = refs[2 * n + 1], refs[2 * n + 2]
        token = refs[-1]
        my_slab, peers = _mesh_peers()
        for k, peer, slab in peers:
            for i in range(n):
                s = (k - 1) * n + i
                pltpu.make_async_remote_copy(
                    src_ref=src_refs[i].at[slab] if flags[i] else src_refs[i], dst_ref=land_refs[i].at[my_slab],
                    send_sem=send_sems.at[s], recv_sem=recv_sems.at[s], device_id=peer,
                    device_id_type=pl.DeviceIdType.MESH).start()
        token[...] = jnp.zeros_like(token)

    outs = pl.pallas_call(
        body, name=name,
        out_shape=(pltpu.SemaphoreType.DMA((copies,)), pltpu.SemaphoreType.DMA((copies,)),
                   *[pltpu.HBM(a.shape, a.dtype) for a in srcs], *[pltpu.HBM(a.shape, a.dtype) for a in lands],
                   _sds((SUBLANES, 128))),
        in_specs=[_HBM] * (2 * n) + [pl.BlockSpec(memory_space=pl.ANY)],
        out_specs=(_SEM, _SEM, *[_HBM] * (2 * n), pl.BlockSpec(memory_space=pltpu.VMEM)),
        input_output_aliases={i: 2 + i for i in range(2 * n)},
        compiler_params=pltpu.CompilerParams(has_side_effects=pltpu.SideEffectType.DATAFLOW_SIDE_EFFECTING),
    )(*srcs, *lands, after)
    state = (flags, outs[0], outs[1], outs[2:2 + n], outs[2 + n:2 + 2 * n])
    return state, outs[-1]


def _exchange_wait(name, state, after):
    flags, send_sems, recv_sems, srcs, lands = state
    n = len(flags)

    def body(*refs):
        src_refs, land_refs = refs[:n], refs[n:2 * n]
        send_ref, recv_ref = refs[2 * n], refs[2 * n + 1]
        _, peers = _mesh_peers()
        for k, peer, slab in peers:
            for i in range(n):
                s = (k - 1) * n + i
                copy = pltpu.make_async_remote_copy(
                    src_ref=src_refs[i].at[slab] if flags[i] else src_refs[i], dst_ref=land_refs[i].at[slab],
                    send_sem=send_ref.at[s], recv_sem=recv_ref.at[s], device_id=peer,
                    device_id_type=pl.DeviceIdType.MESH)
                copy.wait_send()
                copy.wait_recv()

    outs = pl.pallas_call(
        body, name=name,
        out_shape=(*[pltpu.HBM(a.shape, a.dtype) for a in srcs], *[pltpu.HBM(a.shape, a.dtype) for a in lands]),
        in_specs=[_HBM] * (2 * n) + [_SEM, _SEM, pl.BlockSpec(memory_space=pl.ANY)],
        out_specs=tuple([_HBM] * (2 * n)),
        input_output_aliases={i: i for i in range(2 * n)},
        compiler_params=pltpu.CompilerParams(has_side_effects=pltpu.SideEffectType.DATAFLOW_SIDE_EFFECTING),
    )(*srcs, *lands, send_sems, recv_sems, after)
    return list(outs[n:])


def _join_cols(parts, name, tr):
    _, r, c = parts.shape

    def body(p_ref, o_ref):
        for j in range(N_DEV):
            o_ref[:, j * c:(j + 1) * c] = p_ref[j]

    return _pcall(body, name, (r // tr,), [pl.BlockSpec((N_DEV, tr, c), lambda i: (0, i, 0))],
                  pl.BlockSpec((tr, N_DEV * c), lambda i: (i, 0)), _sds((r, N_DEV * c), parts.dtype))(parts)


def _split_cols(full, name, tr):
    r, c = full.shape[0], full.shape[1] // N_DEV

    def body(f_ref, o_ref):
        for j in range(N_DEV):
            o_ref[j] = f_ref[:, j * c:(j + 1) * c]

    return _pcall(body, name, (r // tr,), [pl.BlockSpec((tr, N_DEV * c), lambda i: (i, 0))],
                  pl.BlockSpec((N_DEV, tr, c), lambda i: (0, i, 0)), _sds((N_DEV, r, c), full.dtype))(full)


def _adamw(parts, w, m, v, name, tile):
    _, rows, cols = w.shape

    def body(p_ref, w_ref, m_ref, v_ref, g_out, d_out, m_out, v_out):
        g = p_ref[0].astype(F32)
        for k in range(1, N_DEV):
            g = g + p_ref[k].astype(F32)
        m1 = ADAM_B1 * m_ref[0] + (1.0 - ADAM_B1) * g
        v1 = ADAM_B2 * v_ref[0] + (1.0 - ADAM_B2) * (g * g)
        m_hat = m1 / (1.0 - ADAM_B1 ** ADAM_STEP)
        v_hat = v1 / (1.0 - ADAM_B2 ** ADAM_STEP)
        g_out[0] = g
        d_out[0] = -ADAM_LR * (m_hat / (jnp.sqrt(v_hat) + ADAM_EPS) + ADAM_WD * w_ref[0])
        m_out[0] = m1
        v_out[0] = v1

    row = pl.BlockSpec((1, tile, cols), lambda i: (0, i, 0))
    return _pcall(body, name, (rows // tile,),
                  [pl.BlockSpec((N_DEV, tile, cols), lambda i: (0, i, 0)), row, row, row],
                  [row, row, row, row], [_sds((1, rows, cols))] * 4)(parts, w, m, v)


BIG = {
    "w_in": ((D_MODEL, N_IN // N_DEV), True, 256),
    "w_glu": ((S5_WIDTH // N_DEV, S5_WIDTH), False, S5_WIDTH // N_DEV),
    "w_pa": ((S5_WIDTH, D_MODEL // N_DEV), True, S5_WIDTH),
    "w_pb": ((HG_WIDTH, D_MODEL // N_DEV), True, HG_WIDTH),
    "w_out": ((D_MODEL // N_DEV, D_MODEL), False, D_MODEL // N_DEV),
    "w_up": ((D_MODEL, 2 * D_FF // N_DEV), True, 256),
    "w_conv": ((CONV_W, 2 * D_FF // N_DEV), True, CONV_W),
    "w_down": ((D_FF // N_DEV, D_MODEL), False, D_FF // N_DEV // 2),
}
UNALIGNED_COLS = ("w_in", "w_up", "w_conv")


def _join_shards(n, parts):
    (a, b), by_cols, _ = BIG[n]
    if not by_cols:
        return parts.reshape(N_DEV * a, b)
    if n in UNALIGNED_COLS:
        return _join_cols(parts, "join_" + n, min(a, 256))
    return parts.transpose(1, 0, 2).reshape(a, N_DEV * b)


def _split_shards(n, full):
    (a, b), by_cols, _ = BIG[n]
    if not by_cols:
        return full.reshape(N_DEV, a, b)
    if n in UNALIGNED_COLS:
        return _split_cols(full, "split_" + n, min(a, 256))
    return full.reshape(a, N_DEV, b).transpose(1, 0, 2)


PACKED = tuple((n, shp) for n, shp in SMALL if n != "g_mix") + (("loss", (1,)),)


def _pack_small(d):
    flat = jnp.concatenate([d[n].reshape(-1) for n, _ in PACKED])
    return jnp.pad(flat, (0, SMALL_ROWS * PACK_W - flat.shape[0])).reshape(SMALL_ROWS, PACK_W)


def _unpack_small(p):
    flat = p.reshape(-1)
    out, off = {}, 0
    for n, shp in PACKED:
        size = math.prod(shp)
        out[n] = flat[off:off + size].reshape(shp)
        off += size
    return out


def kernel(x, g_mix, w_in, s5_a_re, s5_a_im, s5_log_dt, s5_b_re, s5_b_im, s5_c_re, s5_c_im, s5_d, w_glu, b_glu, hg_lb_logits, hg_norm_gain, w_pa, w_pb, w_out, g_ffn, w_up, w_conv, b_conv, w_down, g_final, loss_target, m_g_mix, m_w_in, m_s5_a_re, m_s5_a_im, m_s5_log_dt, m_s5_b_re, m_s5_b_im, m_s5_c_re, m_s5_c_im, m_s5_d, m_w_glu, m_b_glu, m_hg_lb_logits, m_hg_norm_gain, m_w_pa, m_w_pb, m_w_out, m_g_ffn, m_w_up, m_w_conv, m_b_conv, m_w_down, m_g_final, v_g_mix, v_w_in, v_s5_a_re, v_s5_a_im, v_s5_log_dt, v_s5_b_re, v_s5_b_im, v_s5_c_re, v_s5_c_im, v_s5_d, v_w_glu, v_b_glu, v_hg_lb_logits, v_hg_norm_gain, v_w_pa, v_w_pb, v_w_out, v_g_ffn, v_w_up, v_w_conv, v_b_conv, v_w_down, v_g_final):
    given = dict(locals())
    small_names = [n for n, _ in SMALL]

    pay = {n: given[n][0] if n == "w_conv" else given[n][0].astype(BF16) for n in BIG}
    groups = {"in": ["w_in"], "mix": ["w_glu", "w_pa", "w_pb", "w_out"], "ffn": ["w_up", "w_down", "w_conv"]}
    gathers, order = {}, pay["w_in"]
    for grp, names in groups.items():
        gathers[grp], order = _exchange_start("gather_" + grp + "_start", [(pay[n], False) for n in names], order)

    def weights(grp, after):
        got = _exchange_wait("gather_" + grp + "_wait", gathers[grp], after)
        return {n: _join_shards(n, g) for n, g in zip(groups[grp], got)}

    in_flight = []

    def emit(grads):
        names = list(grads)
        state, token = _exchange_start("grads_" + names[0] + "_start",
                                       [(_split_shards(n, grads[n]), True) for n in names], grads[names[0]])
        in_flight.append((names, state))
        return token[0, 0]

    def emit_small(grads):
        pack = _pack_small(grads)
        state, token = _exchange_start("grads_small_start", [(pack, False)], pack)
        in_flight.append((["small"], state))
        return token[0, 0]

    sp = {n: (given[n] if n in ("g_final", "hg_lb_logits") else given[n][0]) for n in small_names}
    sp["g_mix"] = sp["g_mix"] + order[0, 0]
    dx, dw_in, dg_mix = _local_step(x, loss_target, weights, sp, emit, emit_small)
    last, token = _exchange_start("grads_w_in_start", [(_split_shards("w_in", dw_in), True), (dg_mix, False)], dw_in)

    res = {}

    def update_big(n, part):
        res[n] = _adamw(part, given[n], given["m_" + n], given["v_" + n], "adamw_" + n, BIG[n][2])
        return res[n][0]

    after = token
    for names, state in in_flight:
        for n, part in zip(names, _exchange_wait("grads_" + names[0] + "_wait", state, after)):
            if n != "small":
                after = update_big(n, part)
                continue
            zero = jnp.zeros((1,), F32)
            rs4 = _adamw(part, _pack_small({**{k: given[k] for k in small_names}, "loss": zero})[None],
                         _pack_small({**{k: given["m_" + k] for k in small_names}, "loss": zero})[None],
                         _pack_small({**{k: given["v_" + k] for k in small_names}, "loss": zero})[None],
                         "adamw_small", SMALL_ROWS)
            small4 = [_unpack_small(r) for r in rs4]
            for k in small_names:
                if k != "g_mix":
                    res[k] = [us[k] for us in small4]
            total_loss = small4[0]["loss"][0]
            after = rs4[0]
    part_in, part_gmix = _exchange_wait("grads_w_in_wait", last, after)
    update_big("w_in", part_in)
    res["g_mix"] = [r[0] for r in _adamw(part_gmix, g_mix[None], m_g_mix[None], v_g_mix[None], "adamw_g_mix", 1)]
    return (total_loss, dx, *[res[n][0] for n in WEIGHT_ORDER], *[res[n][1] for n in WEIGHT_ORDER],
            *[res[n][2] for n in WEIGHT_ORDER], *[res[n][3] for n in WEIGHT_ORDER])
```

```python
import functools
import math

import jax
import jax.numpy as jnp
from jax import lax
from jax.experimental import pallas as pl
from jax.experimental.pallas import tpu as pltpu

F32 = jnp.float32
BF16 = jnp.bfloat16

D_MODEL = 1024
S5_WIDTH = 512
S5_GROUP = 16
S5_GROUPS = 32
S5_STATE = 64
S5_N = S5_GROUPS * S5_STATE
HG_WIDTH = 512
HG_HEAD = 128
HG_HEADS = 4
D_FF = 2816
CONV_W = 3
CHUNK = 64
N_IN = S5_WIDTH + 4 * HG_WIDTH + 2 * D_MODEL
EPS = 1e-6
QSCALE = HG_HEAD ** -0.5

ADAM_LR = 0.001
ADAM_B1 = 0.9
ADAM_B2 = 0.999
ADAM_EPS = 1e-08
ADAM_WD = 0.01
ADAM_STEP = 10

N_DEV = 8
V7X_VMEM_BYTES = 64 * 1024 * 1024
VMEM_LIMIT = V7X_VMEM_BYTES * 7 // 8
SUBLANES = 8
PACK_W = 1024

SMALL = (
    ("g_mix", (1, D_MODEL)),
    ("s5_a_re", (1, S5_GROUPS, S5_STATE)),
    ("s5_a_im", (1, S5_GROUPS, S5_STATE)),
    ("s5_log_dt", (1, S5_GROUPS)),
    ("s5_b_re", (1, S5_GROUPS, S5_STATE, S5_GROUP)),
    ("s5_b_im", (1, S5_GROUPS, S5_STATE, S5_GROUP)),
    ("s5_c_re", (1, S5_GROUPS, S5_GROUP, S5_STATE)),
    ("s5_c_im", (1, S5_GROUPS, S5_GROUP, S5_STATE)),
    ("s5_d", (1, S5_WIDTH)),
    ("b_glu", (1, S5_WIDTH)),
    ("hg_lb_logits", (2, HG_WIDTH)),
    ("hg_norm_gain", (1, HG_WIDTH)),
    ("g_ffn", (1, D_MODEL)),
    ("b_conv", (1, 2 * D_FF)),
    ("g_final", (D_MODEL,)),
)
SMALL_ROWS = 144
WEIGHT_ORDER = ("g_mix", "w_in", "s5_a_re", "s5_a_im", "s5_log_dt", "s5_b_re", "s5_b_im", "s5_c_re", "s5_c_im",
                "s5_d", "w_glu", "b_glu", "hg_lb_logits", "hg_norm_gain", "w_pa", "w_pb", "w_out", "g_ffn",
                "w_up", "w_conv", "b_conv", "w_down", "g_final")


def _pcall(body, name, grid, in_specs, out_specs, out_shape, scratch=()):
    return pl.pallas_call(
        body, name=name, grid=grid, in_specs=in_specs, out_specs=out_specs, out_shape=out_shape,
        scratch_shapes=list(scratch),
        compiler_params=pltpu.CompilerParams(dimension_semantics=("arbitrary",) * len(grid),
                                             vmem_limit_bytes=VMEM_LIMIT),
    )


def _full(shape):
    return pl.BlockSpec(shape, lambda *_: (0,) * len(shape))


def _sds(shape, dtype=F32):
    return jax.ShapeDtypeStruct(shape, dtype)


def _dot(a, b):
    return jnp.dot(a.astype(BF16), b.astype(BF16), preferred_element_type=F32)


def _dot_nt(a, b):
    return lax.dot_general(a.astype(BF16), b.astype(BF16), (((1,), (1,)), ((), ())), preferred_element_type=F32)


def _dot_tn(a, b):
    return lax.dot_general(a.astype(BF16), b.astype(BF16), (((0,), (0,)), ((), ())), preferred_element_type=F32)


def _hdot(a, b):
    return jnp.dot(a, b, preferred_element_type=F32, precision=lax.Precision.HIGHEST)


def _hdot_tn(a, b):
    return lax.dot_general(a, b, (((0,), (0,)), ((), ())), preferred_element_type=F32,
                           precision=lax.Precision.HIGHEST)


def _sigmoid(x):
    return jax.nn.sigmoid(x)


GELU_C = math.sqrt(2.0 / math.pi)
GELU_A = 0.044715


def _gelu(x):
    return 0.5 * x * (1.0 + jnp.tanh(GELU_C * (x + GELU_A * (x * x * x))))


def _gelu_grad(x):
    t = jnp.tanh(GELU_C * (x + GELU_A * (x * x * x)))
    return 0.5 * (1.0 + t) + 0.5 * x * (1.0 - t * t) * (GELU_C * (1.0 + 3.0 * GELU_A * x * x))


def _cumsum_rows(v, reverse=False):
    n = v.shape[0]
    row = lax.broadcasted_iota(jnp.int32, v.shape, 0)
    s = 1
    while s < n:
        if reverse:
            v = v + jnp.where(row < n - s, pltpu.roll(v, n - s, axis=0), 0.0)
        else:
            v = v + jnp.where(row >= s, pltpu.roll(v, s, axis=0), 0.0)
        s *= 2
    return v


def _token_tile(seq):
    return min(256, seq)


def _s5_disc(a_re, a_im, ldt, b_re, b_im):
    dt = jnp.exp(ldt)
    mag = jnp.exp(a_re * dt)
    ang = a_im * dt
    lb_re = mag * jnp.cos(ang)
    lb_im = mag * jnp.sin(ang)
    den = a_re * a_re + a_im * a_im
    n_re = lb_re - 1.0
    n_im = lb_im
    co_re = (n_re * a_re + n_im * a_im) / den
    co_im = (n_im * a_re - n_re * a_im) / den
    bb_re = co_re * b_re - co_im * b_im
    bb_im = co_re * b_im + co_im * b_re
    return lb_re, lb_im, bb_re, bb_im


def _params_fwd(a_re, a_im, ldt, b_re, b_im, logits):
    def body(are, aim, ld, bre, bim, lg, lr_o, li_o, bbr_o, bbi_o, lb_o):
        lr, li, bbr, bbi = _s5_disc(are[...], aim[...], ld[...], bre[...], bim[...])
        lr_o[...] = lr
        li_o[...] = li
        bbr_o[...] = bbr
        bbi_o[...] = bbi
        lb_o[...] = _sigmoid(lg[0:1, :] - lg[1:2, :])

    col, mat = (S5_N, 1), (S5_N, S5_GROUP)
    return _pcall(body, "params_fwd", (1,),
                  [_full(col), _full(col), _full(col), _full(mat), _full(mat), _full((2, HG_WIDTH))],
                  [_full(col), _full(col), _full(mat), _full(mat), _full((1, HG_WIDTH))],
                  [_sds(col), _sds(col), _sds(mat), _sds(mat), _sds((1, HG_WIDTH))])(a_re, a_im, ldt, b_re, b_im, logits)


def _params_bwd(a_re, a_im, ldt, b_re, b_im, logits, dlr, dli, dbbr, dbbi, dlb):
    def body(are, aim, ld, bre, bim, lg, dlr_r, dli_r, dbbr_r, dbbi_r, dlb_r,
             dare_o, daim_o, dld_o, dbre_o, dbim_o, dlg_o):
        _, vjp = jax.vjp(_s5_disc, are[...], aim[...], ld[...], bre[...], bim[...])
        dare, daim, dld, dbre, dbim = vjp((dlr_r[...], dli_r[...], dbbr_r[...], dbbi_r[...]))
        dare_o[...] = dare
        daim_o[...] = daim
        dbre_o[...] = dbre
        dbim_o[...] = dbim
        for g in range(S5_GROUPS):
            dld_o[g:g + 1, :] = jnp.sum(dld[g * S5_STATE:(g + 1) * S5_STATE, :], axis=0, keepdims=True)
        lb = _sigmoid(lg[0:1, :] - lg[1:2, :])
        d0 = dlb_r[...] * lb * (1.0 - lb)
        dlg_o[0:1, :] = d0
        dlg_o[1:2, :] = -d0

    col, mat = (S5_N, 1), (S5_N, S5_GROUP)
    return _pcall(body, "params_bwd", (1,),
                  [_full(col), _full(col), _full(col), _full(mat), _full(mat), _full((2, HG_WIDTH)),
                   _full(col), _full(col), _full(mat), _full(mat), _full((1, HG_WIDTH))],
                  [_full(col), _full(col), _full((S5_GROUPS, 1)), _full(mat), _full(mat), _full((2, HG_WIDTH))],
                  [_sds(col), _sds(col), _sds((S5_GROUPS, 1)), _sds(mat), _sds(mat), _sds((2, HG_WIDTH))],
                  )(a_re, a_im, ldt, b_re, b_im, logits, dlr, dli, dbbr, dbbi, dlb)


def _band_blocks(m):
    g, r, c = m.shape
    gb = g // S5_BANDS
    m4 = m.astype(BF16).reshape(S5_BANDS, gb, r, c)
    on_diag = jnp.eye(gb, dtype=bool)[None, :, None, :, None]
    return jnp.where(on_diag, m4[:, :, :, None, :], 0).reshape(S5_BANDS, gb * r, gb * c)


def _diag_blocks(band, r, c):
    g, nb = band.shape[0] // r, band.shape[1] // c
    on_diag = (jnp.arange(g) % nb)[:, None, None, None] == jnp.arange(nb)[None, None, :, None]
    return jnp.sum(jnp.where(on_diag, band.reshape(g, r, nb, c), 0.0), axis=2)


def _in_proj(x, g_mix, w_in, tm):
    t = x.shape[0]

    def body(x_ref, g_ref, w_ref, u_ref, za_ref, zh_ref, zg_ref):
        xv = x_ref[...]
        r = lax.rsqrt(jnp.mean(xv * xv, axis=-1, keepdims=True) + EPS)
        u = (xv * r * g_ref[...]).astype(BF16)
        u_ref[...] = u
        za_ref[...] = jnp.dot(u, w_ref[:, 0:S5_WIDTH], preferred_element_type=F32)
        zh_ref[...] = jnp.dot(u, w_ref[:, S5_WIDTH:S5_WIDTH + 4 * HG_WIDTH], preferred_element_type=F32)
        zg_ref[...] = jnp.dot(u, w_ref[:, S5_WIDTH + 4 * HG_WIDTH:], preferred_element_type=F32)

    row = lambda w: pl.BlockSpec((tm, w), lambda i: (i, 0))
    return _pcall(body, "in_proj", (t // tm,),
                  [row(D_MODEL), _full((1, D_MODEL)), _full((D_MODEL, N_IN))],
                  [row(D_MODEL), row(S5_WIDTH), row(4 * HG_WIDTH), row(2 * D_MODEL)],
                  [_sds((t, D_MODEL), BF16), _sds((t, S5_WIDTH)), _sds((t, 4 * HG_WIDTH)), _sds((t, 2 * D_MODEL))],
                  )(x, g_mix, w_in)


S5_LANES = 512
S5_BANDS = 4


def _band(q):
    return (slice(q * S5_WIDTH // S5_BANDS, (q + 1) * S5_WIDTH // S5_BANDS),
            slice(q * S5_N // S5_BANDS, (q + 1) * S5_N // S5_BANDS))


def _im(st):
    return slice(S5_N + st.start, S5_N + st.stop)


SCAN_UNROLL = 8


def _complex_scan(buf_ref, lam_ref, st_ref, ts, reverse):
    chunks = [slice(cc * S5_LANES, (cc + 1) * S5_LANES) for cc in range(S5_N // S5_LANES)]
    nch = len(chunks)
    wr = [lam_ref[0:1, re] for re in chunks]
    wi = [-lam_ref[1:2, re] if reverse else lam_ref[1:2, re] for re in chunks]

    def block(ib, carry):
        vr, vi = list(carry[:nch]), list(carry[nch:])
        first = ts - SCAN_UNROLL - ib * SCAN_UNROLL if reverse else ib * SCAN_UNROLL
        first = pl.multiple_of(first, SCAN_UNROLL)
        for k in range(SCAN_UNROLL):
            row = pl.ds(first + (SCAN_UNROLL - 1 - k if reverse else k), 1)
            for cc, re in enumerate(chunks):
                nr = wr[cc] * vr[cc] - wi[cc] * vi[cc] + buf_ref[row, re]
                ni = wr[cc] * vi[cc] + wi[cc] * vr[cc] + buf_ref[row, _im(re)]
                buf_ref[row, re] = nr
                buf_ref[row, _im(re)] = ni
                vr[cc], vi[cc] = nr, ni
        return tuple(vr + vi)

    init = tuple(st_ref[0:1, re] for re in chunks) + tuple(st_ref[1:2, re] for re in chunks)
    last = lax.fori_loop(0, ts // SCAN_UNROLL, block, init)
    for cc, re in enumerate(chunks):
        st_ref[0:1, re] = last[cc]
        st_ref[1:2, re] = last[nch + cc]


BAND_CH = S5_WIDTH // S5_BANDS
BAND_ST = S5_N // S5_BANDS


def _s5_fwd(za, b_bands, lam, c_bands, dskip, nb, seq, ts):
    t = za.shape[0]
    nts = seq // ts

    def body(za_ref, br_ref, bi_ref, lam_ref, cr_ref, ci_ref, d_ref, xs_ref, y_ref, st_ref):
        @pl.when(pl.program_id(1) == 0)
        def _():
            st_ref[...] = jnp.zeros_like(st_ref)

        zav = za_ref[...]
        for q in range(S5_BANDS):
            ch, st = _band(q)
            xs_ref[:, st] = _dot(zav[:, ch], br_ref[q])
            xs_ref[:, _im(st)] = _dot(zav[:, ch], bi_ref[q])
        _complex_scan(xs_ref, lam_ref, st_ref, ts, reverse=False)
        for q in range(S5_BANDS):
            ch, st = _band(q)
            y_ref[:, ch] = (_dot(xs_ref[:, st], cr_ref[q]) + _dot(xs_ref[:, _im(st)], ci_ref[q])
                            + d_ref[:, ch] * zav[:, ch])

    tok = lambda w: pl.BlockSpec((ts, w), lambda b, j: (b * nts + j, 0))
    to_st, to_ch = _full((S5_BANDS, BAND_CH, BAND_ST)), _full((S5_BANDS, BAND_ST, BAND_CH))
    return _pcall(body, "s5_fwd", (nb, nts),
                  [tok(S5_WIDTH), to_st, to_st, _full((2, S5_N)), to_ch, to_ch, _full((1, S5_WIDTH))],
                  [tok(2 * S5_N), tok(S5_WIDTH)],
                  [_sds((t, 2 * S5_N)), _sds((t, S5_WIDTH))],
                  scratch=[pltpu.VMEM((2, S5_N), F32)])(za, *b_bands, lam, *c_bands, dskip)


def _hgrn_gates(zq, zf, lbh):
    sf = _sigmoid(zf)
    f = lbh + (1.0 - lbh) * sf
    sq = _sigmoid(zq)
    qa = zq * sq * QSCALE
    bc = _cumsum_rows(jnp.log(f))
    bm = bc[CHUNK // 2 - 1:CHUNK // 2, :]
    bl = bc[CHUNK - 1:CHUNK, :]
    return sf, f, sq, qa, bc, bm, bl


def _hgrn_fwd(zh, lb, nb, seq):
    nc = seq // CHUNK

    def body(zh_ref, lb_ref, o_ref, sts_ref, st_ref):
        @pl.when(pl.program_id(0) == 0)
        def _():
            st_ref[...] = jnp.zeros_like(st_ref)

        causal = (lax.broadcasted_iota(jnp.int32, (CHUNK, CHUNK), 0)
                  >= lax.broadcasted_iota(jnp.int32, (CHUNK, CHUNK), 1))
        for b in range(nb):
            for h in range(HG_HEADS):
                hs = slice(h * HG_HEAD, (h + 1) * HG_HEAD)
                zq = zh_ref[b, :, h * HG_HEAD:(h + 1) * HG_HEAD]
                zf = zh_ref[b, :, HG_WIDTH + h * HG_HEAD:HG_WIDTH + (h + 1) * HG_HEAD]
                zi = zh_ref[b, :, 2 * HG_WIDTH + h * HG_HEAD:2 * HG_WIDTH + (h + 1) * HG_HEAD]
                _, f, _, qa, bc, bm, bl = _hgrn_gates(zq, zf, lb_ref[:, hs])
                k = 1.0 - f
                qt = qa * jnp.exp(bc - bm)
                kt = k * jnp.exp(bm - bc)
                qb = qa * jnp.exp(bc)
                kd = k * jnp.exp(bl - bc)
                st = st_ref[b, h]
                sts_ref[b, 0, h] = st
                a = jnp.where(causal, _dot_nt(qt, kt), 0.0)
                o_ref[b, :, hs] = _dot(a, zi) + _dot_nt(qb, st)
                st_ref[b, h] = st * jnp.exp(bl) + _dot_tn(zi, kd)

    return _pcall(body, "hgrn_fwd", (nc,),
                  [pl.BlockSpec((nb, CHUNK, 4 * HG_WIDTH), lambda c: (0, c, 0)), _full((1, HG_WIDTH))],
                  [pl.BlockSpec((nb, CHUNK, HG_WIDTH), lambda c: (0, c, 0)),
                   pl.BlockSpec((nb, 1, HG_HEADS, HG_HEAD, HG_HEAD), lambda c: (0, c, 0, 0, 0))],
                  [_sds((nb, seq, HG_WIDTH)), _sds((nb, nc, HG_HEADS, HG_HEAD, HG_HEAD))],
                  scratch=[pltpu.VMEM((nb, HG_HEADS, HG_HEAD, HG_HEAD), F32)])(zh, lb)


def _head_rms(o):
    parts = []
    for h in range(HG_HEADS):
        oh = o[:, h * HG_HEAD:(h + 1) * HG_HEAD]
        r = lax.rsqrt(jnp.mean(oh * oh, axis=-1, keepdims=True) + EPS)
        parts.append(jnp.broadcast_to(r, oh.shape))
    return jnp.concatenate(parts, axis=1)


def _head_mean(v):
    parts = []
    for h in range(HG_HEADS):
        vh = v[:, h * HG_HEAD:(h + 1) * HG_HEAD]
        parts.append(jnp.broadcast_to(jnp.mean(vh, axis=-1, keepdims=True), vh.shape))
    return jnp.concatenate(parts, axis=1)


def _mix_fwd(x, y0, o, zh, zgt, w_glu, b_glu, gain, w_pa, w_pb, w_out, g_ffn, tm):
    t = x.shape[0]

    def body(x_ref, y0_ref, o_ref, zg_ref, zgt_ref, wglu_ref, bglu_ref, gain_ref, wpa_ref, wpb_ref, wout_ref,
             gffn_ref, x1_ref, u2_ref, pa_ref, pb_ref, ya2_ref, yb_ref):
        ya1 = _gelu(y0_ref[...])
        s = _sigmoid(_dot(ya1, wglu_ref[...]) + bglu_ref[...])
        ya2 = (ya1 * s).astype(BF16)
        ov = o_ref[...]
        zg = zg_ref[...]
        yb = (ov * _head_rms(ov) * gain_ref[...] * (zg * _sigmoid(zg))).astype(BF16)
        ya2_ref[...] = ya2
        yb_ref[...] = yb
        pa = jnp.dot(ya2, wpa_ref[...], preferred_element_type=F32)
        pb = jnp.dot(yb, wpb_ref[...], preferred_element_type=F32)
        pa_ref[...] = pa.astype(BF16)
        pb_ref[...] = pb.astype(BF16)
        m = _sigmoid(zgt_ref[:, 0:D_MODEL]) * pa + _sigmoid(zgt_ref[:, D_MODEL:]) * pb
        x1 = x_ref[...] + _dot(m, wout_ref[...])
        x1_ref[...] = x1
        r = lax.rsqrt(jnp.mean(x1 * x1, axis=-1, keepdims=True) + EPS)
        u2_ref[...] = (x1 * r * gffn_ref[...]).astype(BF16)

    row = lambda w: pl.BlockSpec((tm, w), lambda i: (i, 0))
    return _pcall(body, "mix_fwd", (t // tm,),
                  [row(D_MODEL), row(S5_WIDTH), row(HG_WIDTH), pl.BlockSpec((tm, HG_WIDTH), lambda i: (i, 3)),
                   row(2 * D_MODEL), _full((S5_WIDTH, S5_WIDTH)), _full((1, S5_WIDTH)), _full((1, HG_WIDTH)),
                   _full((S5_WIDTH, D_MODEL)), _full((HG_WIDTH, D_MODEL)), _full((D_MODEL, D_MODEL)),
                   _full((1, D_MODEL))],
                  [row(D_MODEL), row(D_MODEL), row(D_MODEL), row(D_MODEL), row(S5_WIDTH), row(HG_WIDTH)],
                  [_sds((t, D_MODEL)), _sds((t, D_MODEL), BF16), _sds((t, D_MODEL), BF16), _sds((t, D_MODEL), BF16),
                   _sds((t, S5_WIDTH), BF16), _sds((t, HG_WIDTH), BF16)],
                  )(x, y0, o, zh, zgt, w_glu, b_glu, gain, w_pa, w_pb, w_out, g_ffn)


FF_COLS = 256
FF_UP_TILE = 1408


def _ffn_up(u2, w_up, tm):
    t = u2.shape[0]
    n = 2 * D_FF

    def body(u_ref, w_ref, h_ref):
        h_ref[...] = jnp.dot(u_ref[...], w_ref[...], preferred_element_type=F32).astype(BF16)

    return _pcall(body, "ffn_up", (n // FF_UP_TILE, t // tm),
                  [pl.BlockSpec((tm, D_MODEL), lambda j, i: (i, 0)),
                   pl.BlockSpec((D_MODEL, FF_UP_TILE), lambda j, i: (0, j))],
                  pl.BlockSpec((tm, FF_UP_TILE), lambda j, i: (i, j)),
                  _sds((t, n), BF16))(u2, w_up)


HALO = 16


def _conv_cols(h_ref, halo_ref, valid, wc_ref, bc_ref, c0):
    cs = slice(c0, c0 + FF_COLS)
    cur = h_ref[:, cs].astype(F32)
    prev = jnp.where(valid, halo_ref[:, cs].astype(F32), 0.0)
    full = jnp.concatenate([prev, cur], axis=0)
    h1 = pltpu.roll(full, 1, axis=0)[HALO:]
    h2 = pltpu.roll(full, 2, axis=0)[HALO:]
    return h2 * wc_ref[0:1, cs] + h1 * wc_ref[1:2, cs] + cur * wc_ref[2:3, cs] + bc_ref[:, cs]


def _ffn_down_loss(h, x1, tgt, w_conv, b_conv, w_down, g_final, seq, tm):
    t = h.shape[0]
    tps = seq // tm
    n = 2 * D_FF

    def body(h_ref, halo_ref, x1_ref, tgt_ref, wc_ref, bc_ref, wd_ref, gf_ref,
             hc_ref, a_ref, dx2_ref, dx2b_ref, loss_ref, dgf_ref):
        i = pl.program_id(0)

        @pl.when(i == 0)
        def _():
            loss_ref[...] = jnp.zeros_like(loss_ref)
            dgf_ref[...] = jnp.zeros_like(dgf_ref)

        valid = (i % tps) != 0
        x2 = x1_ref[...]
        for j in range(D_FF // FF_COLS):
            gate = _conv_cols(h_ref, halo_ref, valid, wc_ref, bc_ref, j * FF_COLS)
            val = _conv_cols(h_ref, halo_ref, valid, wc_ref, bc_ref, D_FF + j * FF_COLS)
            hc_ref[:, j * FF_COLS:(j + 1) * FF_COLS] = gate.astype(BF16)
            hc_ref[:, D_FF + j * FF_COLS:D_FF + (j + 1) * FF_COLS] = val.astype(BF16)
            a = (gate * _sigmoid(gate) * val).astype(BF16)
            a_ref[:, j * FF_COLS:(j + 1) * FF_COLS] = a
            x2 = x2 + jnp.dot(a, wd_ref[j * FF_COLS:(j + 1) * FF_COLS, :], preferred_element_type=F32)
        r = lax.rsqrt(jnp.mean(x2 * x2, axis=-1, keepdims=True) + EPS)
        xn = x2 * r
        g = gf_ref[...]
        e = xn * g - tgt_ref[...]
        loss_ref[...] += (0.5 / D_MODEL) * jnp.sum(e * e).reshape(1, 1)
        dy = e * (1.0 / D_MODEL)
        dgf_ref[...] += jnp.sum(dy * xn, axis=0, keepdims=True)
        dxn = dy * g
        dx2 = r * (dxn - xn * jnp.mean(dxn * xn, axis=-1, keepdims=True))
        dx2_ref[...] = dx2
        dx2b_ref[...] = dx2.astype(BF16)

    row = lambda w: pl.BlockSpec((tm, w), lambda i: (i, 0))
    halo = pl.BlockSpec((HALO, n), lambda i: (jnp.maximum(i * (tm // HALO) - 1, 0), 0))
    return _pcall(body, "ffn_down_loss", (t // tm,),
                  [row(n), halo, row(D_MODEL), row(D_MODEL), _full((CONV_W, n)), _full((1, n)),
                   _full((D_FF, D_MODEL)), _full((1, D_MODEL))],
                  [row(n), row(D_FF), row(D_MODEL), row(D_MODEL), _full((1, 1)), _full((1, D_MODEL))],
                  [_sds((t, n), BF16), _sds((t, D_FF), BF16), _sds((t, D_MODEL)), _sds((t, D_MODEL), BF16),
                   _sds((1, 1)), _sds((1, D_MODEL))],
                  )(h, h, x1, tgt, w_conv, b_conv, w_down, g_final)


def _wgrad(a, b, name, tn, out_dtype=F32, transpose_out=False, band=None):
    t, m = a.shape
    n = b.shape[1] if band is None else band
    nbands = 1 if band is None else b.shape[1] // band

    def body(a_ref, b_ref, o_ref):
        r = _dot_tn(a_ref[...], b_ref[...])
        o_ref[...] = (r.T if transpose_out else r).astype(out_dtype)

    if transpose_out:
        out_spec, out_shape = pl.BlockSpec((n, tn), lambda i: (0, i)), _sds((n, m), out_dtype)
    else:
        out_spec, out_shape = pl.BlockSpec((tn, n), lambda i: (i, 0)), _sds((m, n), out_dtype)
    return _pcall(body, name, (m // tn,),
                  [pl.BlockSpec((t, tn), lambda i: (0, i)), pl.BlockSpec((t, n), lambda i: (0, i % nbands))],
                  out_spec, out_shape)(a, b)


def _ffn_bwd_act(dx2b, hc, w_down, tm):
    t = hc.shape[0]
    n = 2 * D_FF

    def body(dx2_ref, hc_ref, wd_ref, dhc_ref, dbc_ref):
        @pl.when(pl.program_id(0) == 0)
        def _():
            dbc_ref[...] = jnp.zeros_like(dbc_ref)

        dx2 = dx2_ref[...]
        for j in range(D_FF // FF_COLS):
            gs = slice(j * FF_COLS, (j + 1) * FF_COLS)
            vs = slice(D_FF + j * FF_COLS, D_FF + (j + 1) * FF_COLS)
            gate = hc_ref[:, gs].astype(F32)
            val = hc_ref[:, vs].astype(F32)
            da = _dot_nt(dx2, wd_ref[gs, :])
            sg = _sigmoid(gate)
            dgate = da * val * (sg * (1.0 + gate * (1.0 - sg)))
            dval = da * (gate * sg)
            dhc_ref[:, gs] = dgate.astype(BF16)
            dhc_ref[:, vs] = dval.astype(BF16)
            dbc_ref[:, gs] += jnp.sum(dgate, axis=0, keepdims=True)
            dbc_ref[:, vs] += jnp.sum(dval, axis=0, keepdims=True)

    row = lambda w: pl.BlockSpec((tm, w), lambda i: (i, 0))
    return _pcall(body, "ffn_bwd_act", (t // tm,),
                  [row(D_MODEL), row(n), _full((D_FF, D_MODEL))],
                  [row(n), _full((1, n))],
                  [_sds((t, n), BF16), _sds((1, n))],
                  )(dx2b, hc, w_down)


def _ffn_bwd_up(dhc, h, dx2, x1, w_conv, w_up, g_ffn, seq, tm):
    t = dhc.shape[0]
    tps = seq // tm
    n = 2 * D_FF
    last = t // HALO - 1

    def body(dhc_ref, halo_ref, h_ref, dx2_ref, x1_ref, wc_ref, wu_ref, gf_ref,
             dh_ref, dx1_ref, dx1b_ref, dgf_ref, dwc_ref):
        i = pl.program_id(0)

        @pl.when(i == 0)
        def _():
            dgf_ref[...] = jnp.zeros_like(dgf_ref)
            dwc_ref[...] = jnp.zeros_like(dwc_ref)

        valid = ((i + 1) % tps) != 0
        du2 = jnp.zeros((tm, D_MODEL), F32)
        for j in range(n // FF_COLS):
            cs = slice(j * FF_COLS, (j + 1) * FF_COLS)
            cur = dhc_ref[:, cs].astype(F32)
            nxt = jnp.where(valid, halo_ref[:, cs].astype(F32), 0.0)
            full = jnp.concatenate([cur, nxt], axis=0)
            d1 = pltpu.roll(full, tm + HALO - 1, axis=0)[:tm]
            d2 = pltpu.roll(full, tm + HALO - 2, axis=0)[:tm]
            dh = (cur * wc_ref[2:3, cs] + d1 * wc_ref[1:2, cs] + d2 * wc_ref[0:1, cs]).astype(BF16)
            dh_ref[:, cs] = dh
            du2 = du2 + _dot_nt(dh, wu_ref[:, cs])
            hv = h_ref[:, cs].astype(F32)
            dwc_ref[0:1, cs] += jnp.sum(hv * d2, axis=0, keepdims=True)
            dwc_ref[1:2, cs] += jnp.sum(hv * d1, axis=0, keepdims=True)
            dwc_ref[2:3, cs] += jnp.sum(hv * cur, axis=0, keepdims=True)
        x1 = x1_ref[...]
        r = lax.rsqrt(jnp.mean(x1 * x1, axis=-1, keepdims=True) + EPS)
        xn = x1 * r
        dgf_ref[...] += jnp.sum(du2 * xn, axis=0, keepdims=True)
        dxn = du2 * gf_ref[...]
        dx1 = dx2_ref[...] + r * (dxn - xn * jnp.mean(dxn * xn, axis=-1, keepdims=True))
        dx1_ref[...] = dx1
        dx1b_ref[...] = dx1.astype(BF16)

    row = lambda w: pl.BlockSpec((tm, w), lambda i: (i, 0))
    halo = pl.BlockSpec((HALO, n), lambda i: (jnp.minimum((i + 1) * (tm // HALO), last), 0))
    return _pcall(body, "ffn_bwd_up", (t // tm,),
                  [row(n), halo, row(n), row(D_MODEL), row(D_MODEL), _full((CONV_W, n)), _full((D_MODEL, n)),
                   _full((1, D_MODEL))],
                  [row(n), row(D_MODEL), row(D_MODEL), _full((1, D_MODEL)), _full((CONV_W, n))],
                  [_sds((t, n), BF16), _sds((t, D_MODEL)), _sds((t, D_MODEL), BF16), _sds((1, D_MODEL)),
                   _sds((CONV_W, n))],
                  )(dhc, dhc, h, dx2, x1, w_conv, w_up, g_ffn)


def _mix_bwd(dx1, y0, o, zh, zgt, pa, pb, w_glu, b_glu, gain, w_pa, w_pb, w_out, tm):
    t = dx1.shape[0]

    def body(dx1_ref, y0_ref, o_ref, zg_ref, zgt_ref, pa_ref, pb_ref, wglu_ref, bglu_ref, gain_ref, wpa_ref,
             wpb_ref, wout_ref,
             dy0_ref, do_ref, dzg_ref, dzgt_ref, m_ref, dpa_ref, dpb_ref, ya1_ref, dpre_ref, dbglu_ref, dgain_ref):
        @pl.when(pl.program_id(0) == 0)
        def _():
            dbglu_ref[...] = jnp.zeros_like(dbglu_ref)
            dgain_ref[...] = jnp.zeros_like(dgain_ref)

        dm = _dot_nt(dx1_ref[...], wout_ref[...])
        sga = _sigmoid(zgt_ref[:, 0:D_MODEL])
        sgb = _sigmoid(zgt_ref[:, D_MODEL:])
        pa = pa_ref[...].astype(F32)
        pb = pb_ref[...].astype(F32)
        m_ref[...] = (sga * pa + sgb * pb).astype(BF16)
        dzgt_ref[:, 0:D_MODEL] = (dm * pa * sga * (1.0 - sga)).astype(BF16)
        dzgt_ref[:, D_MODEL:] = (dm * pb * sgb * (1.0 - sgb)).astype(BF16)
        dpa = (dm * sga).astype(BF16)
        dpb = (dm * sgb).astype(BF16)
        dpa_ref[...] = dpa
        dpb_ref[...] = dpb
        dya2 = _dot_nt(dpa, wpa_ref[...])
        dyb = _dot_nt(dpb, wpb_ref[...])
        y0 = y0_ref[...]
        ya1 = _gelu(y0)
        ya1_ref[...] = ya1.astype(BF16)
        s = _sigmoid(_dot(ya1, wglu_ref[...]) + bglu_ref[...])
        dpre = dya2 * ya1 * s * (1.0 - s)
        dpre_ref[...] = dpre.astype(BF16)
        dbglu_ref[...] += jnp.sum(dpre, axis=0, keepdims=True)
        dya1 = dya2 * s + _dot_nt(dpre, wglu_ref[...])
        dy0_ref[...] = dya1 * _gelu_grad(y0)
        ov = o_ref[...]
        zg = zg_ref[...]
        oh = ov * _head_rms(ov)
        on = oh * gain_ref[...]
        sz = _sigmoid(zg)
        dzg_ref[...] = (dyb * on * (sz * (1.0 + zg * (1.0 - sz)))).astype(BF16)
        don = dyb * (zg * sz)
        dgain_ref[...] += jnp.sum(don * oh, axis=0, keepdims=True)
        doh = don * gain_ref[...]
        do_ref[...] = _head_rms(ov) * (doh - oh * _head_mean(doh * oh))

    row = lambda w: pl.BlockSpec((tm, w), lambda i: (i, 0))
    return _pcall(body, "mix_bwd", (t // tm,),
                  [row(D_MODEL), row(S5_WIDTH), row(HG_WIDTH), pl.BlockSpec((tm, HG_WIDTH), lambda i: (i, 3)),
                   row(2 * D_MODEL), row(D_MODEL), row(D_MODEL), _full((S5_WIDTH, S5_WIDTH)), _full((1, S5_WIDTH)),
                   _full((1, HG_WIDTH)), _full((S5_WIDTH, D_MODEL)), _full((HG_WIDTH, D_MODEL)),
                   _full((D_MODEL, D_MODEL))],
                  [row(S5_WIDTH), row(HG_WIDTH), row(HG_WIDTH), row(2 * D_MODEL), row(D_MODEL), row(D_MODEL),
                   row(D_MODEL), row(S5_WIDTH), row(S5_WIDTH), _full((1, S5_WIDTH)), _full((1, HG_WIDTH))],
                  [_sds((t, S5_WIDTH)), _sds((t, HG_WIDTH)), _sds((t, HG_WIDTH), BF16), _sds((t, 2 * D_MODEL), BF16),
                   _sds((t, D_MODEL), BF16), _sds((t, D_MODEL), BF16), _sds((t, D_MODEL), BF16),
                   _sds((t, S5_WIDTH), BF16), _sds((t, S5_WIDTH), BF16), _sds((1, S5_WIDTH)), _sds((1, HG_WIDTH))],
                  )(dx1, y0, o, zh, zgt, pa, pb, w_glu, b_glu, gain, w_pa, w_pb, w_out)


def _s5_bwd(dy0, za, xs, c_bands, b_bands, lam, dskip, nb, seq, ts):
    t = za.shape[0]
    nts = seq // ts

    def body(dy0_ref, za_ref, xs_ref, halo_ref, cr_ref, ci_ref, br_ref, bi_ref, lam_ref, d_ref,
             dza_ref, a_ref, dlam_ref, dd_ref, acc_ref, st_ref):
        b, j = pl.program_id(0), pl.program_id(1)

        @pl.when((b == 0) & (j == 0))
        def _():
            dlam_ref[...] = jnp.zeros_like(dlam_ref)
            dd_ref[...] = jnp.zeros_like(dd_ref)

        @pl.when(j == 0)
        def _():
            st_ref[...] = jnp.zeros_like(st_ref)

        dy0 = dy0_ref[...]
        for q in range(S5_BANDS):
            ch, st = _band(q)
            acc_ref[:, st] = _dot(dy0[:, ch], cr_ref[q])
            acc_ref[:, _im(st)] = _dot(dy0[:, ch], ci_ref[q])
        _complex_scan(acc_ref, lam_ref, st_ref, ts, reverse=True)
        av = acc_ref[...]
        a_ref[...] = av.astype(BF16)
        first = jnp.where(j == nts - 1, 0.0, halo_ref[SUBLANES - 1:SUBLANES, :])
        rows = lax.broadcasted_iota(jnp.int32, (ts, 2 * S5_N), 0)
        xp = jnp.where(rows == 0, first, pltpu.roll(xs_ref[...], 1, axis=0))
        ar, ai = av[:, :S5_N], av[:, S5_N:]
        xr, xi = xp[:, :S5_N], xp[:, S5_N:]
        dlam_ref[0:1, :] += jnp.sum(ar * xr + ai * xi, axis=0, keepdims=True)
        dlam_ref[1:2, :] += jnp.sum(ai * xr - ar * xi, axis=0, keepdims=True)
        for q in range(S5_BANDS):
            ch, st = _band(q)
            dza_ref[:, ch] = (_dot(a_ref[:, st], br_ref[q]) + _dot(a_ref[:, _im(st)], bi_ref[q])
                              + d_ref[:, ch] * dy0[:, ch]).astype(BF16)
        dd_ref[...] += jnp.sum(dy0 * za_ref[...], axis=0, keepdims=True)

    tile = lambda b, j: b * nts + (nts - 1 - j)
    tok = lambda w: pl.BlockSpec((ts, w), lambda b, j: (tile(b, j), 0))
    halo = pl.BlockSpec((SUBLANES, 2 * S5_N),
                        lambda b, j: (jnp.maximum(tile(b, j) * (ts // SUBLANES) - 1, 0), 0))
    to_st, to_ch = _full((S5_BANDS, BAND_CH, BAND_ST)), _full((S5_BANDS, BAND_ST, BAND_CH))
    return _pcall(body, "s5_bwd", (nb, nts),
                  [tok(S5_WIDTH), tok(S5_WIDTH), tok(2 * S5_N), halo, to_st, to_st, to_ch, to_ch,
                   _full((2, S5_N)), _full((1, S5_WIDTH))],
                  [tok(S5_WIDTH), tok(2 * S5_N), _full((2, S5_N)), _full((1, S5_WIDTH))],
                  [_sds((t, S5_WIDTH), BF16), _sds((t, 2 * S5_N), BF16), _sds((2, S5_N)), _sds((1, S5_WIDTH))],
                  scratch=[pltpu.VMEM((ts, 2 * S5_N), F32), pltpu.VMEM((2, S5_N), F32)],
                  )(dy0, za, xs, xs, *c_bands, *b_bands, lam, dskip)


def _hgrn_bwd(zh, do, sts, lb, nb, seq):
    nc = seq // CHUNK

    def body(zh_ref, do_ref, sts_ref, lb_ref, dz_ref, dlb_ref, dst_ref):
        @pl.when(pl.program_id(0) == 0)
        def _():
            dst_ref[...] = jnp.zeros_like(dst_ref)
            dlb_ref[...] = jnp.zeros_like(dlb_ref)

        row = lax.broadcasted_iota(jnp.int32, (CHUNK, CHUNK), 0)
        causal = row >= lax.broadcasted_iota(jnp.int32, (CHUNK, CHUNK), 1)
        last_row = lax.broadcasted_iota(jnp.int32, (CHUNK, HG_HEAD), 0) == CHUNK - 1
        for b in range(nb):
            for h in range(HG_HEADS):
                hs = slice(h * HG_HEAD, (h + 1) * HG_HEAD)
                zq = zh_ref[b, :, h * HG_HEAD:(h + 1) * HG_HEAD]
                zf = zh_ref[b, :, HG_WIDTH + h * HG_HEAD:HG_WIDTH + (h + 1) * HG_HEAD]
                zi = zh_ref[b, :, 2 * HG_WIDTH + h * HG_HEAD:2 * HG_WIDTH + (h + 1) * HG_HEAD]
                lbh = lb_ref[:, hs]
                sf, f, sq, qa, bc, bm, bl = _hgrn_gates(zq, zf, lbh)
                k = 1.0 - f
                e_qt = jnp.exp(bc - bm)
                e_kt = jnp.exp(bm - bc)
                e_b = jnp.exp(bc)
                e_kd = jnp.exp(bl - bc)
                e_l = jnp.exp(bl)
                qt, kt, qb, kd = qa * e_qt, k * e_kt, qa * e_b, k * e_kd
                a = jnp.where(causal, _dot_nt(qt, kt), 0.0)
                st = sts_ref[b, 0, h]
                dst = dst_ref[b, h]
                dov = do_ref[b, :, hs]
                da = jnp.where(causal, _dot_nt(dov, zi), 0.0)
                dqt = _hdot(da, kt)
                dkt = _hdot_tn(da, qt)
                dqb = _hdot(dov, st)
                di = _dot_tn(a, dov) + _dot_nt(kd, dst)
                dkd = _hdot(zi, dst)
                de_l = jnp.sum(dst * st, axis=0, keepdims=True)
                dst_ref[b, h] = dst * e_l + _dot_tn(dov, qb)
                dqa = dqt * e_qt + dqb * e_b
                dk = dkt * e_kt + dkd * e_kd
                dbl = jnp.sum(dkd * kd, axis=0, keepdims=True) + de_l * e_l
                db = dqt * qt - dkt * kt + dqb * qb - dkd * kd + jnp.where(last_row, dbl, 0.0)
                df = _cumsum_rows(db, reverse=True) / f - dk
                dzq = dqa * QSCALE * (sq * (1.0 + zq * (1.0 - sq)))
                dzf = df * (1.0 - lbh) * sf * (1.0 - sf)
                dz_ref[b, :, h * HG_HEAD:(h + 1) * HG_HEAD] = dzq.astype(BF16)
                dz_ref[b, :, HG_WIDTH + h * HG_HEAD:HG_WIDTH + (h + 1) * HG_HEAD] = dzf.astype(BF16)
                dz_ref[b, :, 2 * HG_WIDTH + h * HG_HEAD:2 * HG_WIDTH + (h + 1) * HG_HEAD] = di.astype(BF16)
                dlb_ref[:, hs] += jnp.sum(df * (1.0 - sf), axis=0, keepdims=True)

    rev = lambda c: nc - 1 - c
    return _pcall(body, "hgrn_bwd", (nc,),
                  [pl.BlockSpec((nb, CHUNK, 4 * HG_WIDTH), lambda c: (0, rev(c), 0)),
                   pl.BlockSpec((nb, CHUNK, HG_WIDTH), lambda c: (0, rev(c), 0)),
                   pl.BlockSpec((nb, 1, HG_HEADS, HG_HEAD, HG_HEAD), lambda c: (0, rev(c), 0, 0, 0)),
                   _full((1, HG_WIDTH))],
                  [pl.BlockSpec((nb, CHUNK, 3 * HG_WIDTH), lambda c: (0, rev(c), 0)), _full((1, HG_WIDTH))],
                  [_sds((nb, seq, 3 * HG_WIDTH), BF16), _sds((1, HG_WIDTH))],
                  scratch=[pltpu.VMEM((nb, HG_HEADS, HG_HEAD, HG_HEAD), F32)])(zh, do, sts, lb)


def _in_proj_bwd(dza, dzh, dzg, dzgt, dx1, x, g_mix, w_in, tm):
    t = x.shape[0]

    def body(dza_ref, dzh_ref, dzg_ref, dzgt_ref, dx1_ref, x_ref, g_ref, w_ref, dz_ref, dx_ref, dg_ref):
        @pl.when(pl.program_id(0) == 0)
        def _():
            dg_ref[...] = jnp.zeros_like(dg_ref)

        c1, c2, c3 = S5_WIDTH, S5_WIDTH + 3 * HG_WIDTH, S5_WIDTH + 4 * HG_WIDTH
        dz_ref[:, 0:c1] = dza_ref[...]
        dz_ref[:, c1:c2] = dzh_ref[...]
        dz_ref[:, c2:c3] = dzg_ref[...]
        dz_ref[:, c3:] = dzgt_ref[...]
        du = _dot_nt(dz_ref[...], w_ref[...])
        xv = x_ref[...]
        r = lax.rsqrt(jnp.mean(xv * xv, axis=-1, keepdims=True) + EPS)
        xn = xv * r
        dg_ref[...] += jnp.sum(du * xn, axis=0, keepdims=True)
        dxn = du * g_ref[...]
        dx_ref[...] = dx1_ref[...] + r * (dxn - xn * jnp.mean(dxn * xn, axis=-1, keepdims=True))

    row = lambda w: pl.BlockSpec((tm, w), lambda i: (i, 0))
    return _pcall(body, "in_proj_bwd", (t // tm,),
                  [row(S5_WIDTH), row(3 * HG_WIDTH), row(HG_WIDTH), row(2 * D_MODEL), row(D_MODEL), row(D_MODEL),
                   _full((1, D_MODEL)), _full((D_MODEL, N_IN))],
                  [row(N_IN), row(D_MODEL), _full((1, D_MODEL))],
                  [_sds((t, N_IN), BF16), _sds((t, D_MODEL)), _sds((1, D_MODEL))],
                  )(dza, dzh, dzg, dzgt, dx1, x, g_mix, w_in)


def _after(value, token):
    return lax.optimization_barrier((value, token))[0]


def _local_step(x3, tgt3, weights, sp, emit, emit_small):
    nb, seq, _ = x3.shape
    t = nb * seq
    tm = _token_tile(seq)
    x = x3.reshape(t, D_MODEL)
    tgt = tgt3.reshape(t, D_MODEL)
    row = lambda v: v.reshape(1, -1)

    a_re = sp["s5_a_re"].reshape(S5_N, 1)
    a_im = sp["s5_a_im"].reshape(S5_N, 1)
    ldt = jnp.repeat(sp["s5_log_dt"].reshape(S5_GROUPS), S5_STATE).reshape(S5_N, 1)
    b_re = sp["s5_b_re"].reshape(S5_N, S5_GROUP)
    b_im = sp["s5_b_im"].reshape(S5_N, S5_GROUP)
    lr, li, bb_re, bb_im, lb = _params_fwd(a_re, a_im, ldt, b_re, b_im, sp["hg_lb_logits"])
    lam = jnp.concatenate([lr.reshape(1, S5_N), li.reshape(1, S5_N)], axis=0)
    gps = lambda m: m.reshape(S5_GROUPS, S5_STATE, S5_GROUP)
    swap = lambda m: m.transpose(0, 2, 1)
    b_to_st = (_band_blocks(swap(gps(bb_re))), _band_blocks(swap(gps(bb_im))))
    b_to_ch = (_band_blocks(gps(bb_re)), _band_blocks(gps(bb_im)))
    c_to_ch = (_band_blocks(swap(sp["s5_c_re"])), _band_blocks(swap(-sp["s5_c_im"])))
    c_to_st = (_band_blocks(sp["s5_c_re"]), _band_blocks(-sp["s5_c_im"]))

    g_mix, g_ffn, g_final = row(sp["g_mix"]), row(sp["g_ffn"]), row(sp["g_final"])
    b_glu, gain, dskip, b_conv = row(sp["b_glu"]), row(sp["hg_norm_gain"]), row(sp["s5_d"]), row(sp["b_conv"])

    w_in = weights("in", c_to_st[1])["w_in"]
    u, za, zh, zgt = _in_proj(x, g_mix, w_in, tm)
    xs, y0 = _s5_fwd(za, b_to_st, lam, c_to_ch, dskip, nb, seq, tm)
    o3, sts = _hgrn_fwd(zh.reshape(nb, seq, 4 * HG_WIDTH), lb, nb, seq)
    o = o3.reshape(t, HG_WIDTH)
    wm = weights("mix", o3)
    x1, u2, pa, pb, ya2, yb = _mix_fwd(x, y0, o, zh, zgt, wm["w_glu"], b_glu, gain, wm["w_pa"], wm["w_pb"],
                                       wm["w_out"], g_ffn, tm)
    wf = weights("ffn", u2)
    h = _ffn_up(u2, wf["w_up"], min(4 * tm, t))
    hc, a, dx2, dx2b, loss, dg_final = _ffn_down_loss(h, x1, tgt, wf["w_conv"], b_conv, wf["w_down"], g_final,
                                                      seq, tm)

    wgrad = functools.partial(_wgrad, tn=256, out_dtype=BF16)
    dhc, db_conv = _ffn_bwd_act(dx2b, hc, wf["w_down"], tm)
    sent = emit({"w_down": wgrad(a, dx2b, "dw_down")})
    dh, dx1, dx1b, dg_ffn, dw_conv = _ffn_bwd_up(dhc, h, dx2, x1, wf["w_conv"], wf["w_up"], _after(g_ffn, sent),
                                                 seq, tm)
    sent = emit({"w_up": wgrad(dh, u2, "dw_up", transpose_out=True), "w_conv": dw_conv})
    (dy0, do, dzg, dzgt, m, dpa, dpb, ya1, dpre, db_glu, dgain) = _mix_bwd(
        dx1, y0, o, zh, zgt, pa, pb, wm["w_glu"], _after(b_glu, sent), gain, wm["w_pa"], wm["w_pb"], wm["w_out"], tm)
    sent = emit({"w_out": wgrad(m, dx1b, "dw_out"), "w_pa": wgrad(ya2, dpa, "dw_pa"),
                 "w_pb": wgrad(yb, dpb, "dw_pb"), "w_glu": wgrad(ya1, dpre, "dw_glu")})
    dzh3, dlb = _hgrn_bwd(zh.reshape(nb, seq, 4 * HG_WIDTH), do.reshape(nb, seq, HG_WIDTH), sts, _after(lb, sent),
                          nb, seq)
    dza, a_s5, dlam, dd = _s5_bwd(dy0, za, xs, c_to_st, b_to_ch, lam, dskip, nb, seq, tm)
    dz, dx, dg_mix = _in_proj_bwd(dza, dzh3.reshape(t, 3 * HG_WIDTH), dzg, dzgt, dx1, x, g_mix, w_in, tm)
    sent = emit({"w_in": wgrad(dz, u, "dw_in", transpose_out=True)})

    band = HG_HEAD
    dbb_band = _wgrad(a_s5, _after(za, sent), "dbb_s5", 512, band=band)
    dc_band = _wgrad(xs, _after(dy0, sent), "dc_s5", 512, band=band)
    dbb_re = _diag_blocks(dbb_band[:S5_N], S5_STATE, S5_GROUP).reshape(S5_N, S5_GROUP)
    dbb_im = _diag_blocks(dbb_band[S5_N:], S5_STATE, S5_GROUP).reshape(S5_N, S5_GROUP)
    dc_re = _diag_blocks(dc_band[:S5_N], S5_STATE, S5_GROUP).transpose(0, 2, 1)
    dc_im = -_diag_blocks(dc_band[S5_N:], S5_STATE, S5_GROUP).transpose(0, 2, 1)
    da_re, da_im, dldt, db_re, db_im, dlogits = _params_bwd(
        a_re, a_im, ldt, b_re, b_im, sp["hg_lb_logits"],
        dlam[0].reshape(S5_N, 1), dlam[1].reshape(S5_N, 1), dbb_re, dbb_im, dlb)
    emit_small({"g_mix": dg_mix, "s5_a_re": da_re, "s5_a_im": da_im, "s5_log_dt": dldt, "s5_b_re": db_re,
                "s5_b_im": db_im, "s5_c_re": dc_re, "s5_c_im": dc_im, "s5_d": dd, "b_glu": db_glu,
                "hg_lb_logits": dlogits, "hg_norm_gain": dgain, "g_ffn": dg_ffn, "b_conv": db_conv,
                "g_final": dg_final, "loss": loss})
    return dx.reshape(nb, seq, D_MODEL)


def _mesh_peers():
    x, y, c = lax.axis_index("x"), lax.axis_index("y"), lax.axis_index("c")
    peers = []
    for k in range(1, N_DEV):
        px, py, pc = (1 - x if k & 4 else x), (1 - y if k & 2 else y), (1 - c if k & 1 else c)
        peers.append((k, (px, py, pc), 4 * px + 2 * py + pc))
    return 4 * x + 2 * y + c, peers


_HBM = pl.BlockSpec(memory_space=pltpu.HBM)
_SEM = pl.BlockSpec(memory_space=pltpu.SEMAPHORE)


def _exchange_start(name, operands, after):
    n = len(operands)
    me = 4 * lax.axis_index("x") + 2 * lax.axis_index("y") + lax.axis_index("c")
    flags = [per_peer for _, per_peer in operands]
    srcs, lands = [], []
    for arr, per_peer in operands:
        own = lax.dynamic_index_in_dim(arr, me, 0, keepdims=True) if per_peer else arr[None]
        land = lax.dynamic_update_slice_in_dim(lax.empty((N_DEV,) + own.shape[1:], arr.dtype), own, me, 0)
        srcs.append(pltpu.with_memory_space_constraint(arr, pltpu.HBM))
        lands.append(pltpu.with_memory_space_constraint(land, pltpu.HBM))
    copies = (N_DEV - 1) * n

    def body(*refs):
        src_refs, land_refs = refs[:n], refs[n:2 * n]
        send_sems, recv_sems = refs[2 * n + 1], refs[2 * n + 2]
        token = refs[-1]
        my_slab, peers = _mesh_peers()
        for k, peer, slab in peers:
            for i in range(n):
                s = (k - 1) * n + i
                pltpu.make_async_remote_copy(
                    src_ref=src_refs[i].at[slab] if flags[i] else src_refs[i], dst_ref=land_refs[i].at[my_slab],
                    send_sem=send_sems.at[s], recv_sem=recv_sems.at[s], device_id=peer,
                    device_id_type=pl.DeviceIdType.MESH).start()
        token[...] = jnp.zeros_like(token)

    outs = pl.pallas_call(
        body, name=name,
        out_shape=(pltpu.SemaphoreType.DMA((copies,)), pltpu.SemaphoreType.DMA((copies,)),
                   *[pltpu.HBM(a.shape, a.dtype) for a in srcs], *[pltpu.HBM(a.shape, a.dtype) for a in lands],
                   _sds((SUBLANES, 128))),
        in_specs=[_HBM] * (2 * n) + [pl.BlockSpec(memory_space=pl.ANY)],
        out_specs=(_SEM, _SEM, *[_HBM] * (2 * n), pl.BlockSpec(memory_space=pltpu.VMEM)),
        input_output_aliases={i: 2 + i for i in range(2 * n)},
        compiler_params=pltpu.CompilerParams(has_side_effects=pltpu.SideEffectType.DATAFLOW_SIDE_EFFECTING),
    )(*srcs, *lands, after)
    state = (flags, outs[0], outs[1], outs[2:2 + n], outs[2 + n:2 + 2 * n])
    return state, outs[-1]


def _exchange_wait(name, state, after):
    flags, send_sems, recv_sems, srcs, lands = state
    n = len(flags)

    def body(*refs):
        src_refs, land_refs = refs[:n], refs[n:2 * n]
        send_ref, recv_ref = refs[2 * n], refs[2 * n + 1]
        _, peers = _mesh_peers()
        for k, peer, slab in peers:
            for i in range(n):
                s = (k - 1) * n + i
                copy = pltpu.make_async_remote_copy(
                    src_ref=src_refs[i].at[slab] if flags[i] else src_refs[i], dst_ref=land_refs[i].at[slab],
                    send_sem=send_ref.at[s], recv_sem=recv_ref.at[s], device_id=peer,
                    device_id_type=pl.DeviceIdType.MESH)
                copy.wait_send()
                copy.wait_recv()

    outs = pl.pallas_call(
        body, name=name,
        out_shape=(*[pltpu.HBM(a.shape, a.dtype) for a in srcs], *[pltpu.HBM(a.shape, a.dtype) for a in lands]),
        in_specs=[_HBM] * (2 * n) + [_SEM, _SEM, pl.BlockSpec(memory_space=pl.ANY)],
        out_specs=tuple([_HBM] * (2 * n)),
        input_output_aliases={i: i for i in range(2 * n)},
        compiler_params=pltpu.CompilerParams(has_side_effects=pltpu.SideEffectType.DATAFLOW_SIDE_EFFECTING),
    )(*srcs, *lands, send_sems, recv_sems, after)
    return list(outs[n:])


def _join_cols(parts, name, tr):
    _, r, c = parts.shape

    def body(p_ref, o_ref):
        for j in range(N_DEV):
            o_ref[:, j * c:(j + 1) * c] = p_ref[j]

    return _pcall(body, name, (r // tr,), [pl.BlockSpec((N_DEV, tr, c), lambda i: (0, i, 0))],
                  pl.BlockSpec((tr, N_DEV * c), lambda i: (i, 0)), _sds((r, N_DEV * c), parts.dtype))(parts)


def _split_cols(full, name, tr):
    r, c = full.shape[0], full.shape[1] // N_DEV

    def body(f_ref, o_ref):
        for j in range(N_DEV):
            o_ref[j] = f_ref[:, j * c:(j + 1) * c]

    return _pcall(body, name, (r // tr,), [pl.BlockSpec((tr, N_DEV * c), lambda i: (i, 0))],
                  pl.BlockSpec((N_DEV, tr, c), lambda i: (0, i, 0)), _sds((N_DEV, r, c), full.dtype))(full)


def _adamw(parts, w, m, v, name, tile):
    _, rows, cols = w.shape

    def body(p_ref, w_ref, m_ref, v_ref, g_out, d_out, m_out, v_out):
        g = p_ref[0].astype(F32)
        for k in range(1, N_DEV):
            g = g + p_ref[k].astype(F32)
        m1 = ADAM_B1 * m_ref[0] + (1.0 - ADAM_B1) * g
        v1 = ADAM_B2 * v_ref[0] + (1.0 - ADAM_B2) * (g * g)
        m_hat = m1 / (1.0 - ADAM_B1 ** ADAM_STEP)
        v_hat = v1 / (1.0 - ADAM_B2 ** ADAM_STEP)
        g_out[0] = g
        d_out[0] = -ADAM_LR * (m_hat / (jnp.sqrt(v_hat) + ADAM_EPS) + ADAM_WD * w_ref[0])
        m_out[0] = m1
        v_out[0] = v1

    row = pl.BlockSpec((1, tile, cols), lambda i: (0, i, 0))
    return _pcall(body, name, (rows // tile,),
                  [pl.BlockSpec((N_DEV, tile, cols), lambda i: (0, i, 0)), row, row, row],
                  [row, row, row, row], [_sds((1, rows, cols))] * 4)(parts, w, m, v)


BIG = {
    "w_in": ((D_MODEL, N_IN // N_DEV), True, 256),
    "w_glu": ((S5_WIDTH // N_DEV, S5_WIDTH), False, S5_WIDTH // N_DEV),
    "w_pa": ((S5_WIDTH, D_MODEL // N_DEV), True, S5_WIDTH),
    "w_pb": ((HG_WIDTH, D_MODEL // N_DEV), True, HG_WIDTH),
    "w_out": ((D_MODEL // N_DEV, D_MODEL), False, D_MODEL // N_DEV),
    "w_up": ((D_MODEL, 2 * D_FF // N_DEV), True, 256),
    "w_conv": ((CONV_W, 2 * D_FF // N_DEV), True, CONV_W),
    "w_down": ((D_FF // N_DEV, D_MODEL), False, D_FF // N_DEV // 2),
}
UNALIGNED_COLS = ("w_in", "w_up", "w_conv")


def _join_shards(n, parts):
    (a, b), by_cols, _ = BIG[n]
    if not by_cols:
        return parts.reshape(N_DEV * a, b)
    if n in UNALIGNED_COLS:
        return _join_cols(parts, "join_" + n, min(a, 256))
    return parts.transpose(1, 0, 2).reshape(a, N_DEV * b)


def _split_shards(n, full):
    (a, b), by_cols, _ = BIG[n]
    if not by_cols:
        return full.reshape(N_DEV, a, b)
    if n in UNALIGNED_COLS:
        return _split_cols(full, "split_" + n, min(a, 256))
    return full.reshape(a, N_DEV, b).transpose(1, 0, 2)


PACKED = SMALL + (("loss", (1,)),)


def _pack_small(d):
    flat = jnp.concatenate([d[n].reshape(-1) for n, _ in PACKED])
    return jnp.pad(flat, (0, SMALL_ROWS * PACK_W - flat.shape[0])).reshape(SMALL_ROWS, PACK_W)


def _unpack_small(p):
    flat = p.reshape(-1)
    out, off = {}, 0
    for n, shp in PACKED:
        size = math.prod(shp)
        out[n] = flat[off:off + size].reshape(shp)
        off += size
    return out


def kernel(x, g_mix, w_in, s5_a_re, s5_a_im, s5_log_dt, s5_b_re, s5_b_im, s5_c_re, s5_c_im, s5_d, w_glu, b_glu, hg_lb_logits, hg_norm_gain, w_pa, w_pb, w_out, g_ffn, w_up, w_conv, b_conv, w_down, g_final, loss_target, m_g_mix, m_w_in, m_s5_a_re, m_s5_a_im, m_s5_log_dt, m_s5_b_re, m_s5_b_im, m_s5_c_re, m_s5_c_im, m_s5_d, m_w_glu, m_b_glu, m_hg_lb_logits, m_hg_norm_gain, m_w_pa, m_w_pb, m_w_out, m_g_ffn, m_w_up, m_w_conv, m_b_conv, m_w_down, m_g_final, v_g_mix, v_w_in, v_s5_a_re, v_s5_a_im, v_s5_log_dt, v_s5_b_re, v_s5_b_im, v_s5_c_re, v_s5_c_im, v_s5_d, v_w_glu, v_b_glu, v_hg_lb_logits, v_hg_norm_gain, v_w_pa, v_w_pb, v_w_out, v_g_ffn, v_w_up, v_w_conv, v_b_conv, v_w_down, v_g_final):
    given = dict(locals())
    small_names = [n for n, _ in SMALL]

    pay = {n: given[n][0] if n == "w_conv" else given[n][0].astype(BF16) for n in BIG}
    groups = {"in": ["w_in"], "mix": ["w_glu", "w_pa", "w_pb", "w_out"], "ffn": ["w_up", "w_down", "w_conv"]}
    gathers, order = {}, pay["w_in"]
    for grp, names in groups.items():
        gathers[grp], order = _exchange_start("gather_" + grp + "_start", [(pay[n], False) for n in names], order)

    def weights(grp, after):
        got = _exchange_wait("gather_" + grp + "_wait", gathers[grp], after)
        return {n: _join_shards(n, g) for n, g in zip(groups[grp], got)}

    in_flight, started = [], []

    def emit(grads):
        names = list(grads)
        state, token = _exchange_start("grads_" + names[0] + "_start",
                                       [(_split_shards(n, grads[n]), True) for n in names], grads[names[0]])
        in_flight.append((names, state))
        return token

    def emit_small(grads):
        pack = _pack_small(grads)
        state, token = _exchange_start("grads_small_start", [(pack, False)], pack)
        in_flight.append((["small"], state))
        started.append(token)

    sp = {n: (given[n] if n in ("g_final", "hg_lb_logits") else given[n][0]) for n in small_names}
    sp["g_mix"] = _after(sp["g_mix"], order)
    dx = _local_step(x, loss_target, weights, sp, emit, emit_small)

    res = {}
    after = started[-1]
    for names, state in in_flight:
        for n, part in zip(names, _exchange_wait("grads_" + names[0] + "_wait", state, after)):
            if n != "small":
                res[n] = _adamw(part, given[n], given["m_" + n], given["v_" + n], "adamw_" + n, BIG[n][2])
                after = res[n][0]
                continue
            zero = jnp.zeros((1,), F32)
            rs4 = _adamw(part, _pack_small({**{k: given[k] for k in small_names}, "loss": zero})[None],
                         _pack_small({**{k: given["m_" + k] for k in small_names}, "loss": zero})[None],
                         _pack_small({**{k: given["v_" + k] for k in small_names}, "loss": zero})[None],
                         "adamw_small", SMALL_ROWS)
            small4 = [_unpack_small(r) for r in rs4]
            for k in small_names:
                res[k] = [us[k] for us in small4]
            total_loss = small4[0]["loss"][0]
    return (total_loss, dx, *[res[n][0] for n in WEIGHT_ORDER], *[res[n][1] for n in WEIGHT_ORDER],
            *[res[n][2] for n in WEIGHT_ORDER], *[res[n][3] for n in WEIGHT_ORDER])
```

```python
import functools
import math

import jax
import jax.numpy as jnp
from jax import lax
from jax.experimental import pallas as pl
from jax.experimental.pallas import tpu as pltpu

F32 = jnp.float32
BF16 = jnp.bfloat16

D_MODEL = 1024
S5_WIDTH = 512
S5_GROUP = 16
S5_GROUPS = 32
S5_STATE = 64
S5_N = S5_GROUPS * S5_STATE
HG_WIDTH = 512
HG_HEAD = 128
HG_HEADS = 4
D_FF = 2816
CONV_W = 3
CHUNK = 64
N_IN = S5_WIDTH + 4 * HG_WIDTH + 2 * D_MODEL
EPS = 1e-6
QSCALE = HG_HEAD ** -0.5

ADAM_LR = 0.001
ADAM_B1 = 0.9
ADAM_B2 = 0.999
ADAM_EPS = 1e-08
ADAM_WD = 0.01
ADAM_STEP = 10

N_DEV = 8
V7X_VMEM_BYTES = 64 * 1024 * 1024
VMEM_LIMIT = V7X_VMEM_BYTES * 7 // 8
SUBLANES = 8
PACK_W = 1024

SMALL = (
    ("g_mix", (1, D_MODEL)),
    ("s5_a_re", (1, S5_GROUPS, S5_STATE)),
    ("s5_a_im", (1, S5_GROUPS, S5_STATE)),
    ("s5_log_dt", (1, S5_GROUPS)),
    ("s5_b_re", (1, S5_GROUPS, S5_STATE, S5_GROUP)),
    ("s5_b_im", (1, S5_GROUPS, S5_STATE, S5_GROUP)),
    ("s5_c_re", (1, S5_GROUPS, S5_GROUP, S5_STATE)),
    ("s5_c_im", (1, S5_GROUPS, S5_GROUP, S5_STATE)),
    ("s5_d", (1, S5_WIDTH)),
    ("b_glu", (1, S5_WIDTH)),
    ("hg_lb_logits", (2, HG_WIDTH)),
    ("hg_norm_gain", (1, HG_WIDTH)),
    ("g_ffn", (1, D_MODEL)),
    ("b_conv", (1, 2 * D_FF)),
    ("g_final", (D_MODEL,)),
)
SMALL_ROWS = 144
WEIGHT_ORDER = ("g_mix", "w_in", "s5_a_re", "s5_a_im", "s5_log_dt", "s5_b_re", "s5_b_im", "s5_c_re", "s5_c_im",
                "s5_d", "w_glu", "b_glu", "hg_lb_logits", "hg_norm_gain", "w_pa", "w_pb", "w_out", "g_ffn",
                "w_up", "w_conv", "b_conv", "w_down", "g_final")


def _pcall(body, name, grid, in_specs, out_specs, out_shape, scratch=()):
    return pl.pallas_call(
        body, name=name, grid=grid, in_specs=in_specs, out_specs=out_specs, out_shape=out_shape,
        scratch_shapes=list(scratch),
        compiler_params=pltpu.CompilerParams(dimension_semantics=("arbitrary",) * len(grid),
                                             vmem_limit_bytes=VMEM_LIMIT),
    )


def _full(shape):
    return pl.BlockSpec(shape, lambda *_: (0,) * len(shape))


def _sds(shape, dtype=F32):
    return jax.ShapeDtypeStruct(shape, dtype)


def _dot(a, b):
    return jnp.dot(a.astype(BF16), b.astype(BF16), preferred_element_type=F32)


def _dot_nt(a, b):
    return lax.dot_general(a.astype(BF16), b.astype(BF16), (((1,), (1,)), ((), ())), preferred_element_type=F32)


def _dot_tn(a, b):
    return lax.dot_general(a.astype(BF16), b.astype(BF16), (((0,), (0,)), ((), ())), preferred_element_type=F32)


def _hdot(a, b):
    return jnp.dot(a, b, preferred_element_type=F32, precision=lax.Precision.HIGHEST)


def _hdot_tn(a, b):
    return lax.dot_general(a, b, (((0,), (0,)), ((), ())), preferred_element_type=F32,
                           precision=lax.Precision.HIGHEST)


def _sigmoid(x):
    return jax.nn.sigmoid(x)


GELU_C = math.sqrt(2.0 / math.pi)
GELU_A = 0.044715


def _gelu(x):
    return 0.5 * x * (1.0 + jnp.tanh(GELU_C * (x + GELU_A * (x * x * x))))


def _gelu_grad(x):
    t = jnp.tanh(GELU_C * (x + GELU_A * (x * x * x)))
    return 0.5 * (1.0 + t) + 0.5 * x * (1.0 - t * t) * (GELU_C * (1.0 + 3.0 * GELU_A * x * x))


def _cumsum_rows(v, reverse=False):
    n = v.shape[0]
    row = lax.broadcasted_iota(jnp.int32, v.shape, 0)
    s = 1
    while s < n:
        if reverse:
            v = v + jnp.where(row < n - s, pltpu.roll(v, n - s, axis=0), 0.0)
        else:
            v = v + jnp.where(row >= s, pltpu.roll(v, s, axis=0), 0.0)
        s *= 2
    return v


def _token_tile(seq):
    return min(256, seq)


def _s5_disc(a_re, a_im, ldt, b_re, b_im):
    dt = jnp.exp(ldt)
    mag = jnp.exp(a_re * dt)
    ang = a_im * dt
    lb_re = mag * jnp.cos(ang)
    lb_im = mag * jnp.sin(ang)
    den = a_re * a_re + a_im * a_im
    n_re = lb_re - 1.0
    n_im = lb_im
    co_re = (n_re * a_re + n_im * a_im) / den
    co_im = (n_im * a_re - n_re * a_im) / den
    bb_re = co_re * b_re - co_im * b_im
    bb_im = co_re * b_im + co_im * b_re
    return lb_re, lb_im, bb_re, bb_im


def _params_fwd(a_re, a_im, ldt, b_re, b_im, logits):
    def body(are, aim, ld, bre, bim, lg, lr_o, li_o, bbr_o, bbi_o, lb_o):
        lr, li, bbr, bbi = _s5_disc(are[...], aim[...], ld[...], bre[...], bim[...])
        lr_o[...] = lr
        li_o[...] = li
        bbr_o[...] = bbr
        bbi_o[...] = bbi
        lb_o[...] = _sigmoid(lg[0:1, :] - lg[1:2, :])

    col, mat = (S5_N, 1), (S5_N, S5_GROUP)
    return _pcall(body, "params_fwd", (1,),
                  [_full(col), _full(col), _full(col), _full(mat), _full(mat), _full((2, HG_WIDTH))],
                  [_full(col), _full(col), _full(mat), _full(mat), _full((1, HG_WIDTH))],
                  [_sds(col), _sds(col), _sds(mat), _sds(mat), _sds((1, HG_WIDTH))])(a_re, a_im, ldt, b_re, b_im, logits)


def _params_bwd(a_re, a_im, ldt, b_re, b_im, logits, dlr, dli, dbbr, dbbi, dlb):
    def body(are, aim, ld, bre, bim, lg, dlr_r, dli_r, dbbr_r, dbbi_r, dlb_r,
             dare_o, daim_o, dld_o, dbre_o, dbim_o, dlg_o):
        _, vjp = jax.vjp(_s5_disc, are[...], aim[...], ld[...], bre[...], bim[...])
        dare, daim, dld, dbre, dbim = vjp((dlr_r[...], dli_r[...], dbbr_r[...], dbbi_r[...]))
        dare_o[...] = dare
        daim_o[...] = daim
        dbre_o[...] = dbre
        dbim_o[...] = dbim
        for g in range(S5_GROUPS):
            dld_o[g:g + 1, :] = jnp.sum(dld[g * S5_STATE:(g + 1) * S5_STATE, :], axis=0, keepdims=True)
        lb = _sigmoid(lg[0:1, :] - lg[1:2, :])
        d0 = dlb_r[...] * lb * (1.0 - lb)
        dlg_o[0:1, :] = d0
        dlg_o[1:2, :] = -d0

    col, mat = (S5_N, 1), (S5_N, S5_GROUP)
    return _pcall(body, "params_bwd", (1,),
                  [_full(col), _full(col), _full(col), _full(mat), _full(mat), _full((2, HG_WIDTH)),
                   _full(col), _full(col), _full(mat), _full(mat), _full((1, HG_WIDTH))],
                  [_full(col), _full(col), _full((S5_GROUPS, 1)), _full(mat), _full(mat), _full((2, HG_WIDTH))],
                  [_sds(col), _sds(col), _sds((S5_GROUPS, 1)), _sds(mat), _sds(mat), _sds((2, HG_WIDTH))],
                  )(a_re, a_im, ldt, b_re, b_im, logits, dlr, dli, dbbr, dbbi, dlb)


def _band_blocks(m):
    g, r, c = m.shape
    gb = g // S5_BANDS
    m4 = m.astype(BF16).reshape(S5_BANDS, gb, r, c)
    on_diag = jnp.eye(gb, dtype=bool)[None, :, None, :, None]
    return jnp.where(on_diag, m4[:, :, :, None, :], 0).reshape(S5_BANDS, gb * r, gb * c)


def _diag_blocks(band, r, c):
    g, nb = band.shape[0] // r, band.shape[1] // c
    on_diag = (jnp.arange(g) % nb)[:, None, None, None] == jnp.arange(nb)[None, None, :, None]
    return jnp.sum(jnp.where(on_diag, band.reshape(g, r, nb, c), 0.0), axis=2)


def _in_proj(x, g_mix, w_in, tm):
    t = x.shape[0]

    def body(x_ref, g_ref, w_ref, u_ref, za_ref, zh_ref, zg_ref):
        xv = x_ref[...]
        r = lax.rsqrt(jnp.mean(xv * xv, axis=-1, keepdims=True) + EPS)
        u = (xv * r * g_ref[...]).astype(BF16)
        u_ref[...] = u
        za_ref[...] = jnp.dot(u, w_ref[:, 0:S5_WIDTH], preferred_element_type=F32)
        zh_ref[...] = jnp.dot(u, w_ref[:, S5_WIDTH:S5_WIDTH + 4 * HG_WIDTH], preferred_element_type=F32)
        zg_ref[...] = jnp.dot(u, w_ref[:, S5_WIDTH + 4 * HG_WIDTH:], preferred_element_type=F32)

    row = lambda w: pl.BlockSpec((tm, w), lambda i: (i, 0))
    return _pcall(body, "in_proj", (t // tm,),
                  [row(D_MODEL), _full((1, D_MODEL)), _full((D_MODEL, N_IN))],
                  [row(D_MODEL), row(S5_WIDTH), row(4 * HG_WIDTH), row(2 * D_MODEL)],
                  [_sds((t, D_MODEL), BF16), _sds((t, S5_WIDTH)), _sds((t, 4 * HG_WIDTH)), _sds((t, 2 * D_MODEL))],
                  )(x, g_mix, w_in)


S5_LANES = 512
S5_BANDS = 4


def _band(q):
    return (slice(q * S5_WIDTH // S5_BANDS, (q + 1) * S5_WIDTH // S5_BANDS),
            slice(q * S5_N // S5_BANDS, (q + 1) * S5_N // S5_BANDS))


def _im(st):
    return slice(S5_N + st.start, S5_N + st.stop)


SCAN_UNROLL = 8


def _complex_scan(buf_ref, lam_ref, st_ref, ts, reverse):
    chunks = [slice(cc * S5_LANES, (cc + 1) * S5_LANES) for cc in range(S5_N // S5_LANES)]
    nch = len(chunks)
    wr = [lam_ref[0:1, re] for re in chunks]
    wi = [-lam_ref[1:2, re] if reverse else lam_ref[1:2, re] for re in chunks]

    def block(ib, carry):
        vr, vi = list(carry[:nch]), list(carry[nch:])
        first = ts - SCAN_UNROLL - ib * SCAN_UNROLL if reverse else ib * SCAN_UNROLL
        first = pl.multiple_of(first, SCAN_UNROLL)
        for k in range(SCAN_UNROLL):
            row = pl.ds(first + (SCAN_UNROLL - 1 - k if reverse else k), 1)
            for cc, re in enumerate(chunks):
                nr = wr[cc] * vr[cc] - wi[cc] * vi[cc] + buf_ref[row, re]
                ni = wr[cc] * vi[cc] + wi[cc] * vr[cc] + buf_ref[row, _im(re)]
                buf_ref[row, re] = nr
                buf_ref[row, _im(re)] = ni
                vr[cc], vi[cc] = nr, ni
        return tuple(vr + vi)

    init = tuple(st_ref[0:1, re] for re in chunks) + tuple(st_ref[1:2, re] for re in chunks)
    last = lax.fori_loop(0, ts // SCAN_UNROLL, block, init)
    for cc, re in enumerate(chunks):
        st_ref[0:1, re] = last[cc]
        st_ref[1:2, re] = last[nch + cc]


BAND_CH = S5_WIDTH // S5_BANDS
BAND_ST = S5_N // S5_BANDS


def _s5_fwd(za, b_bands, lam, c_bands, dskip, nb, seq, ts):
    t = za.shape[0]
    nts = seq // ts

    def body(za_ref, br_ref, bi_ref, lam_ref, cr_ref, ci_ref, d_ref, xs_ref, y_ref, st_ref):
        @pl.when(pl.program_id(1) == 0)
        def _():
            st_ref[...] = jnp.zeros_like(st_ref)

        zav = za_ref[...]
        for q in range(S5_BANDS):
            ch, st = _band(q)
            xs_ref[:, st] = _dot(zav[:, ch], br_ref[q])
            xs_ref[:, _im(st)] = _dot(zav[:, ch], bi_ref[q])
        _complex_scan(xs_ref, lam_ref, st_ref, ts, reverse=False)
        for q in range(S5_BANDS):
            ch, st = _band(q)
            y_ref[:, ch] = (_dot(xs_ref[:, st], cr_ref[q]) + _dot(xs_ref[:, _im(st)], ci_ref[q])
                            + d_ref[:, ch] * zav[:, ch])

    tok = lambda w: pl.BlockSpec((ts, w), lambda b, j: (b * nts + j, 0))
    to_st, to_ch = _full((S5_BANDS, BAND_CH, BAND_ST)), _full((S5_BANDS, BAND_ST, BAND_CH))
    return _pcall(body, "s5_fwd", (nb, nts),
                  [tok(S5_WIDTH), to_st, to_st, _full((2, S5_N)), to_ch, to_ch, _full((1, S5_WIDTH))],
                  [tok(2 * S5_N), tok(S5_WIDTH)],
                  [_sds((t, 2 * S5_N)), _sds((t, S5_WIDTH))],
                  scratch=[pltpu.VMEM((2, S5_N), F32)])(za, *b_bands, lam, *c_bands, dskip)


def _hgrn_gates(zq, zf, lbh):
    sf = _sigmoid(zf)
    f = lbh + (1.0 - lbh) * sf
    sq = _sigmoid(zq)
    qa = zq * sq * QSCALE
    bc = _cumsum_rows(jnp.log(f))
    bm = bc[CHUNK // 2 - 1:CHUNK // 2, :]
    bl = bc[CHUNK - 1:CHUNK, :]
    return sf, f, sq, qa, bc, bm, bl


def _hgrn_fwd(zh, lb, nb, seq):
    nc = seq // CHUNK

    def body(zh_ref, lb_ref, o_ref, sts_ref, st_ref):
        @pl.when(pl.program_id(0) == 0)
        def _():
            st_ref[...] = jnp.zeros_like(st_ref)

        causal = (lax.broadcasted_iota(jnp.int32, (CHUNK, CHUNK), 0)
                  >= lax.broadcasted_iota(jnp.int32, (CHUNK, CHUNK), 1))
        for b in range(nb):
            for h in range(HG_HEADS):
                hs = slice(h * HG_HEAD, (h + 1) * HG_HEAD)
                zq = zh_ref[b, :, h * HG_HEAD:(h + 1) * HG_HEAD]
                zf = zh_ref[b, :, HG_WIDTH + h * HG_HEAD:HG_WIDTH + (h + 1) * HG_HEAD]
                zi = zh_ref[b, :, 2 * HG_WIDTH + h * HG_HEAD:2 * HG_WIDTH + (h + 1) * HG_HEAD]
                _, f, _, qa, bc, bm, bl = _hgrn_gates(zq, zf, lb_ref[:, hs])
                k = 1.0 - f
                qt = qa * jnp.exp(bc - bm)
                kt = k * jnp.exp(bm - bc)
                qb = qa * jnp.exp(bc)
                kd = k * jnp.exp(bl - bc)
                st = st_ref[b, h]
                sts_ref[b, 0, h] = st
                a = jnp.where(causal, _dot_nt(qt, kt), 0.0)
                o_ref[b, :, hs] = _dot(a, zi) + _dot_nt(qb, st)
                st_ref[b, h] = st * jnp.exp(bl) + _dot_tn(zi, kd)

    return _pcall(body, "hgrn_fwd", (nc,),
                  [pl.BlockSpec((nb, CHUNK, 4 * HG_WIDTH), lambda c: (0, c, 0)), _full((1, HG_WIDTH))],
                  [pl.BlockSpec((nb, CHUNK, HG_WIDTH), lambda c: (0, c, 0)),
                   pl.BlockSpec((nb, 1, HG_HEADS, HG_HEAD, HG_HEAD), lambda c: (0, c, 0, 0, 0))],
                  [_sds((nb, seq, HG_WIDTH)), _sds((nb, nc, HG_HEADS, HG_HEAD, HG_HEAD))],
                  scratch=[pltpu.VMEM((nb, HG_HEADS, HG_HEAD, HG_HEAD), F32)])(zh, lb)


def _head_rms(o):
    parts = []
    for h in range(HG_HEADS):
        oh = o[:, h * HG_HEAD:(h + 1) * HG_HEAD]
        r = lax.rsqrt(jnp.mean(oh * oh, axis=-1, keepdims=True) + EPS)
        parts.append(jnp.broadcast_to(r, oh.shape))
    return jnp.concatenate(parts, axis=1)


def _head_mean(v):
    parts = []
    for h in range(HG_HEADS):
        vh = v[:, h * HG_HEAD:(h + 1) * HG_HEAD]
        parts.append(jnp.broadcast_to(jnp.mean(vh, axis=-1, keepdims=True), vh.shape))
    return jnp.concatenate(parts, axis=1)


def _mix_fwd(x, y0, o, zh, zgt, w_glu, b_glu, gain, w_pa, w_pb, w_out, g_ffn, tm):
    t = x.shape[0]

    def body(x_ref, y0_ref, o_ref, zg_ref, zgt_ref, wglu_ref, bglu_ref, gain_ref, wpa_ref, wpb_ref, wout_ref,
             gffn_ref, x1_ref, u2_ref, pa_ref, pb_ref, ya2_ref, yb_ref):
        ya1 = _gelu(y0_ref[...])
        s = _sigmoid(_dot(ya1, wglu_ref[...]) + bglu_ref[...])
        ya2 = (ya1 * s).astype(BF16)
        ov = o_ref[...]
        zg = zg_ref[...]
        yb = (ov * _head_rms(ov) * gain_ref[...] * (zg * _sigmoid(zg))).astype(BF16)
        ya2_ref[...] = ya2
        yb_ref[...] = yb
        pa = jnp.dot(ya2, wpa_ref[...], preferred_element_type=F32)
        pb = jnp.dot(yb, wpb_ref[...], preferred_element_type=F32)
        pa_ref[...] = pa.astype(BF16)
        pb_ref[...] = pb.astype(BF16)
        m = _sigmoid(zgt_ref[:, 0:D_MODEL]) * pa + _sigmoid(zgt_ref[:, D_MODEL:]) * pb
        x1 = x_ref[...] + _dot(m, wout_ref[...])
        x1_ref[...] = x1
        r = lax.rsqrt(jnp.mean(x1 * x1, axis=-1, keepdims=True) + EPS)
        u2_ref[...] = (x1 * r * gffn_ref[...]).astype(BF16)

    row = lambda w: pl.BlockSpec((tm, w), lambda i: (i, 0))
    return _pcall(body, "mix_fwd", (t // tm,),
                  [row(D_MODEL), row(S5_WIDTH), row(HG_WIDTH), pl.BlockSpec((tm, HG_WIDTH), lambda i: (i, 3)),
                   row(2 * D_MODEL), _full((S5_WIDTH, S5_WIDTH)), _full((1, S5_WIDTH)), _full((1, HG_WIDTH)),
                   _full((S5_WIDTH, D_MODEL)), _full((HG_WIDTH, D_MODEL)), _full((D_MODEL, D_MODEL)),
                   _full((1, D_MODEL))],
                  [row(D_MODEL), row(D_MODEL), row(D_MODEL), row(D_MODEL), row(S5_WIDTH), row(HG_WIDTH)],
                  [_sds((t, D_MODEL)), _sds((t, D_MODEL), BF16), _sds((t, D_MODEL), BF16), _sds((t, D_MODEL), BF16),
                   _sds((t, S5_WIDTH), BF16), _sds((t, HG_WIDTH), BF16)],
                  )(x, y0, o, zh, zgt, w_glu, b_glu, gain, w_pa, w_pb, w_out, g_ffn)


FF_COLS = 256
FF_UP_TILE = 1408


def _ffn_up(u2, w_up, tm):
    t = u2.shape[0]
    n = 2 * D_FF

    def body(u_ref, w_ref, h_ref):
        h_ref[...] = jnp.dot(u_ref[...], w_ref[...], preferred_element_type=F32).astype(BF16)

    return _pcall(body, "ffn_up", (n // FF_UP_TILE, t // tm),
                  [pl.BlockSpec((tm, D_MODEL), lambda j, i: (i, 0)),
                   pl.BlockSpec((D_MODEL, FF_UP_TILE), lambda j, i: (0, j))],
                  pl.BlockSpec((tm, FF_UP_TILE), lambda j, i: (i, j)),
                  _sds((t, n), BF16))(u2, w_up)


HALO = 16


def _conv_cols(h_ref, halo_ref, valid, wc_ref, bc_ref, c0):
    cs = slice(c0, c0 + FF_COLS)
    cur = h_ref[:, cs].astype(F32)
    prev = jnp.where(valid, halo_ref[:, cs].astype(F32), 0.0)
    full = jnp.concatenate([prev, cur], axis=0)
    h1 = pltpu.roll(full, 1, axis=0)[HALO:]
    h2 = pltpu.roll(full, 2, axis=0)[HALO:]
    return h2 * wc_ref[0:1, cs] + h1 * wc_ref[1:2, cs] + cur * wc_ref[2:3, cs] + bc_ref[:, cs]


def _ffn_down_loss(h, x1, tgt, w_conv, b_conv, w_down, g_final, seq, tm):
    t = h.shape[0]
    tps = seq // tm
    n = 2 * D_FF

    def body(h_ref, halo_ref, x1_ref, tgt_ref, wc_ref, bc_ref, wd_ref, gf_ref,
             hc_ref, a_ref, dx2_ref, dx2b_ref, loss_ref, dgf_ref):
        i = pl.program_id(0)

        @pl.when(i == 0)
        def _():
            loss_ref[...] = jnp.zeros_like(loss_ref)
            dgf_ref[...] = jnp.zeros_like(dgf_ref)

        valid = (i % tps) != 0
        x2 = x1_ref[...]
        for j in range(D_FF // FF_COLS):
            gate = _conv_cols(h_ref, halo_ref, valid, wc_ref, bc_ref, j * FF_COLS)
            val = _conv_cols(h_ref, halo_ref, valid, wc_ref, bc_ref, D_FF + j * FF_COLS)
            hc_ref[:, j * FF_COLS:(j + 1) * FF_COLS] = gate.astype(BF16)
            hc_ref[:, D_FF + j * FF_COLS:D_FF + (j + 1) * FF_COLS] = val.astype(BF16)
            a = (gate * _sigmoid(gate) * val).astype(BF16)
            a_ref[:, j * FF_COLS:(j + 1) * FF_COLS] = a
            x2 = x2 + jnp.dot(a, wd_ref[j * FF_COLS:(j + 1) * FF_COLS, :], preferred_element_type=F32)
        r = lax.rsqrt(jnp.mean(x2 * x2, axis=-1, keepdims=True) + EPS)
        xn = x2 * r
        g = gf_ref[...]
        e = xn * g - tgt_ref[...]
        loss_ref[...] += (0.5 / D_MODEL) * jnp.sum(e * e).reshape(1, 1)
        dy = e * (1.0 / D_MODEL)
        dgf_ref[...] += jnp.sum(dy * xn, axis=0, keepdims=True)
        dxn = dy * g
        dx2 = r * (dxn - xn * jnp.mean(dxn * xn, axis=-1, keepdims=True))
        dx2_ref[...] = dx2
        dx2b_ref[...] = dx2.astype(BF16)

    row = lambda w: pl.BlockSpec((tm, w), lambda i: (i, 0))
    halo = pl.BlockSpec((HALO, n), lambda i: (jnp.maximum(i * (tm // HALO) - 1, 0), 0))
    return _pcall(body, "ffn_down_loss", (t // tm,),
                  [row(n), halo, row(D_MODEL), row(D_MODEL), _full((CONV_W, n)), _full((1, n)),
                   _full((D_FF, D_MODEL)), _full((1, D_MODEL))],
                  [row(n), row(D_FF), row(D_MODEL), row(D_MODEL), _full((1, 1)), _full((1, D_MODEL))],
                  [_sds((t, n), BF16), _sds((t, D_FF), BF16), _sds((t, D_MODEL)), _sds((t, D_MODEL), BF16),
                   _sds((1, 1)), _sds((1, D_MODEL))],
                  )(h, h, x1, tgt, w_conv, b_conv, w_down, g_final)


def _wgrad(a, b, name, tn, out_dtype=F32, transpose_out=False, band=None, after=None):
    t, m = a.shape
    n = b.shape[1] if band is None else band
    nbands = 1 if band is None else b.shape[1] // band
    after = b if after is None else after

    def body(a_ref, b_ref, after_ref, o_ref):
        r = _dot_tn(a_ref[...], b_ref[...])
        o_ref[...] = (r.T if transpose_out else r).astype(out_dtype)

    if transpose_out:
        out_spec, out_shape = pl.BlockSpec((n, tn), lambda i: (0, i)), _sds((n, m), out_dtype)
    else:
        out_spec, out_shape = pl.BlockSpec((tn, n), lambda i: (i, 0)), _sds((m, n), out_dtype)
    return _pcall(body, name, (m // tn,),
                  [pl.BlockSpec((t, tn), lambda i: (0, i)), pl.BlockSpec((t, n), lambda i: (0, i % nbands)),
                   pl.BlockSpec(memory_space=pl.ANY)],
                  out_spec, out_shape)(a, b, after)


def _ffn_bwd_act(dx2b, hc, w_down, tm):
    t = hc.shape[0]
    n = 2 * D_FF

    def body(dx2_ref, hc_ref, wd_ref, dhc_ref, dbc_ref):
        @pl.when(pl.program_id(0) == 0)
        def _():
            dbc_ref[...] = jnp.zeros_like(dbc_ref)

        dx2 = dx2_ref[...]
        for j in range(D_FF // FF_COLS):
            gs = slice(j * FF_COLS, (j + 1) * FF_COLS)
            vs = slice(D_FF + j * FF_COLS, D_FF + (j + 1) * FF_COLS)
            gate = hc_ref[:, gs].astype(F32)
            val = hc_ref[:, vs].astype(F32)
            da = _dot_nt(dx2, wd_ref[gs, :])
            sg = _sigmoid(gate)
            dgate = da * val * (sg * (1.0 + gate * (1.0 - sg)))
            dval = da * (gate * sg)
            dhc_ref[:, gs] = dgate.astype(BF16)
            dhc_ref[:, vs] = dval.astype(BF16)
            dbc_ref[:, gs] += jnp.sum(dgate, axis=0, keepdims=True)
            dbc_ref[:, vs] += jnp.sum(dval, axis=0, keepdims=True)

    row = lambda w: pl.BlockSpec((tm, w), lambda i: (i, 0))
    return _pcall(body, "ffn_bwd_act", (t // tm,),
                  [row(D_MODEL), row(n), _full((D_FF, D_MODEL))],
                  [row(n), _full((1, n))],
                  [_sds((t, n), BF16), _sds((1, n))],
                  )(dx2b, hc, w_down)


def _ffn_bwd_up(dhc, h, dx2, x1, w_conv, w_up, g_ffn, seq, tm):
    t = dhc.shape[0]
    tps = seq // tm
    n = 2 * D_FF
    last = t // HALO - 1

    def body(dhc_ref, halo_ref, h_ref, dx2_ref, x1_ref, wc_ref, wu_ref, gf_ref,
             dh_ref, dx1_ref, dx1b_ref, dgf_ref, dwc_ref):
        i = pl.program_id(0)

        @pl.when(i == 0)
        def _():
            dgf_ref[...] = jnp.zeros_like(dgf_ref)
            dwc_ref[...] = jnp.zeros_like(dwc_ref)

        valid = ((i + 1) % tps) != 0
        du2 = jnp.zeros((tm, D_MODEL), F32)
        for j in range(n // FF_COLS):
            cs = slice(j * FF_COLS, (j + 1) * FF_COLS)
            cur = dhc_ref[:, cs].astype(F32)
            nxt = jnp.where(valid, halo_ref[:, cs].astype(F32), 0.0)
            full = jnp.concatenate([cur, nxt], axis=0)
            d1 = pltpu.roll(full, tm + HALO - 1, axis=0)[:tm]
            d2 = pltpu.roll(full, tm + HALO - 2, axis=0)[:tm]
            dh = (cur * wc_ref[2:3, cs] + d1 * wc_ref[1:2, cs] + d2 * wc_ref[0:1, cs]).astype(BF16)
            dh_ref[:, cs] = dh
            du2 = du2 + _dot_nt(dh, wu_ref[:, cs])
            hv = h_ref[:, cs].astype(F32)
            dwc_ref[0:1, cs] += jnp.sum(hv * d2, axis=0, keepdims=True)
            dwc_ref[1:2, cs] += jnp.sum(hv * d1, axis=0, keepdims=True)
            dwc_ref[2:3, cs] += jnp.sum(hv * cur, axis=0, keepdims=True)
        x1 = x1_ref[...]
        r = lax.rsqrt(jnp.mean(x1 * x1, axis=-1, keepdims=True) + EPS)
        xn = x1 * r
        dgf_ref[...] += jnp.sum(du2 * xn, axis=0, keepdims=True)
        dxn = du2 * gf_ref[...]
        dx1 = dx2_ref[...] + r * (dxn - xn * jnp.mean(dxn * xn, axis=-1, keepdims=True))
        dx1_ref[...] = dx1
        dx1b_ref[...] = dx1.astype(BF16)

    row = lambda w: pl.BlockSpec((tm, w), lambda i: (i, 0))
    halo = pl.BlockSpec((HALO, n), lambda i: (jnp.minimum((i + 1) * (tm // HALO), last), 0))
    return _pcall(body, "ffn_bwd_up", (t // tm,),
                  [row(n), halo, row(n), row(D_MODEL), row(D_MODEL), _full((CONV_W, n)), _full((D_MODEL, n)),
                   _full((1, D_MODEL))],
                  [row(n), row(D_MODEL), row(D_MODEL), _full((1, D_MODEL)), _full((CONV_W, n))],
                  [_sds((t, n), BF16), _sds((t, D_MODEL)), _sds((t, D_MODEL), BF16), _sds((1, D_MODEL)),
                   _sds((CONV_W, n))],
                  )(dhc, dhc, h, dx2, x1, w_conv, w_up, g_ffn)


def _mix_bwd(dx1, y0, o, zh, zgt, pa, pb, w_glu, b_glu, gain, w_pa, w_pb, w_out, tm):
    t = dx1.shape[0]

    def body(dx1_ref, y0_ref, o_ref, zg_ref, zgt_ref, pa_ref, pb_ref, wglu_ref, bglu_ref, gain_ref, wpa_ref,
             wpb_ref, wout_ref,
             dy0_ref, do_ref, dzg_ref, dzgt_ref, m_ref, dpa_ref, dpb_ref, ya1_ref, dpre_ref, dbglu_ref, dgain_ref):
        @pl.when(pl.program_id(0) == 0)
        def _():
            dbglu_ref[...] = jnp.zeros_like(dbglu_ref)
            dgain_ref[...] = jnp.zeros_like(dgain_ref)

        dm = _dot_nt(dx1_ref[...], wout_ref[...])
        sga = _sigmoid(zgt_ref[:, 0:D_MODEL])
        sgb = _sigmoid(zgt_ref[:, D_MODEL:])
        pa = pa_ref[...].astype(F32)
        pb = pb_ref[...].astype(F32)
        m_ref[...] = (sga * pa + sgb * pb).astype(BF16)
        dzgt_ref[:, 0:D_MODEL] = (dm * pa * sga * (1.0 - sga)).astype(BF16)
        dzgt_ref[:, D_MODEL:] = (dm * pb * sgb * (1.0 - sgb)).astype(BF16)
        dpa = (dm * sga).astype(BF16)
        dpb = (dm * sgb).astype(BF16)
        dpa_ref[...] = dpa
        dpb_ref[...] = dpb
        dya2 = _dot_nt(dpa, wpa_ref[...])
        dyb = _dot_nt(dpb, wpb_ref[...])
        y0 = y0_ref[...]
        ya1 = _gelu(y0)
        ya1_ref[...] = ya1.astype(BF16)
        s = _sigmoid(_dot(ya1, wglu_ref[...]) + bglu_ref[...])
        dpre = dya2 * ya1 * s * (1.0 - s)
        dpre_ref[...] = dpre.astype(BF16)
        dbglu_ref[...] += jnp.sum(dpre, axis=0, keepdims=True)
        dya1 = dya2 * s + _dot_nt(dpre, wglu_ref[...])
        dy0_ref[...] = dya1 * _gelu_grad(y0)
        ov = o_ref[...]
        zg = zg_ref[...]
        oh = ov * _head_rms(ov)
        on = oh * gain_ref[...]
        sz = _sigmoid(zg)
        dzg_ref[...] = (dyb * on * (sz * (1.0 + zg * (1.0 - sz)))).astype(BF16)
        don = dyb * (zg * sz)
        dgain_ref[...] += jnp.sum(don * oh, axis=0, keepdims=True)
        doh = don * gain_ref[...]
        do_ref[...] = _head_rms(ov) * (doh - oh * _head_mean(doh * oh))

    row = lambda w: pl.BlockSpec((tm, w), lambda i: (i, 0))
    return _pcall(body, "mix_bwd", (t // tm,),
                  [row(D_MODEL), row(S5_WIDTH), row(HG_WIDTH), pl.BlockSpec((tm, HG_WIDTH), lambda i: (i, 3)),
                   row(2 * D_MODEL), row(D_MODEL), row(D_MODEL), _full((S5_WIDTH, S5_WIDTH)), _full((1, S5_WIDTH)),
                   _full((1, HG_WIDTH)), _full((S5_WIDTH, D_MODEL)), _full((HG_WIDTH, D_MODEL)),
                   _full((D_MODEL, D_MODEL))],
                  [row(S5_WIDTH), row(HG_WIDTH), row(HG_WIDTH), row(2 * D_MODEL), row(D_MODEL), row(D_MODEL),
                   row(D_MODEL), row(S5_WIDTH), row(S5_WIDTH), _full((1, S5_WIDTH)), _full((1, HG_WIDTH))],
                  [_sds((t, S5_WIDTH)), _sds((t, HG_WIDTH)), _sds((t, HG_WIDTH), BF16), _sds((t, 2 * D_MODEL), BF16),
                   _sds((t, D_MODEL), BF16), _sds((t, D_MODEL), BF16), _sds((t, D_MODEL), BF16),
                   _sds((t, S5_WIDTH), BF16), _sds((t, S5_WIDTH), BF16), _sds((1, S5_WIDTH)), _sds((1, HG_WIDTH))],
                  )(dx1, y0, o, zh, zgt, pa, pb, w_glu, b_glu, gain, w_pa, w_pb, w_out)


def _s5_bwd(dy0, za, xs, c_bands, b_bands, lam, dskip, nb, seq, ts):
    t = za.shape[0]
    nts = seq // ts

    def body(dy0_ref, za_ref, xs_ref, halo_ref, cr_ref, ci_ref, br_ref, bi_ref, lam_ref, d_ref,
             dza_ref, a_ref, dlam_ref, dd_ref, acc_ref, st_ref):
        b, j = pl.program_id(0), pl.program_id(1)

        @pl.when((b == 0) & (j == 0))
        def _():
            dlam_ref[...] = jnp.zeros_like(dlam_ref)
            dd_ref[...] = jnp.zeros_like(dd_ref)

        @pl.when(j == 0)
        def _():
            st_ref[...] = jnp.zeros_like(st_ref)

        dy0 = dy0_ref[...]
        for q in range(S5_BANDS):
            ch, st = _band(q)
            acc_ref[:, st] = _dot(dy0[:, ch], cr_ref[q])
            acc_ref[:, _im(st)] = _dot(dy0[:, ch], ci_ref[q])
        _complex_scan(acc_ref, lam_ref, st_ref, ts, reverse=True)
        av = acc_ref[...]
        a_ref[...] = av.astype(BF16)
        first = jnp.where(j == nts - 1, 0.0, halo_ref[SUBLANES - 1:SUBLANES, :])
        rows = lax.broadcasted_iota(jnp.int32, (ts, 2 * S5_N), 0)
        xp = jnp.where(rows == 0, first, pltpu.roll(xs_ref[...], 1, axis=0))
        ar, ai = av[:, :S5_N], av[:, S5_N:]
        xr, xi = xp[:, :S5_N], xp[:, S5_N:]
        dlam_ref[0:1, :] += jnp.sum(ar * xr + ai * xi, axis=0, keepdims=True)
        dlam_ref[1:2, :] += jnp.sum(ai * xr - ar * xi, axis=0, keepdims=True)
        for q in range(S5_BANDS):
            ch, st = _band(q)
            dza_ref[:, ch] = (_dot(a_ref[:, st], br_ref[q]) + _dot(a_ref[:, _im(st)], bi_ref[q])
                              + d_ref[:, ch] * dy0[:, ch]).astype(BF16)
        dd_ref[...] += jnp.sum(dy0 * za_ref[...], axis=0, keepdims=True)

    tile = lambda b, j: b * nts + (nts - 1 - j)
    tok = lambda w: pl.BlockSpec((ts, w), lambda b, j: (tile(b, j), 0))
    halo = pl.BlockSpec((SUBLANES, 2 * S5_N),
                        lambda b, j: (jnp.maximum(tile(b, j) * (ts // SUBLANES) - 1, 0), 0))
    to_st, to_ch = _full((S5_BANDS, BAND_CH, BAND_ST)), _full((S5_BANDS, BAND_ST, BAND_CH))
    return _pcall(body, "s5_bwd", (nb, nts),
                  [tok(S5_WIDTH), tok(S5_WIDTH), tok(2 * S5_N), halo, to_st, to_st, to_ch, to_ch,
                   _full((2, S5_N)), _full((1, S5_WIDTH))],
                  [tok(S5_WIDTH), tok(2 * S5_N), _full((2, S5_N)), _full((1, S5_WIDTH))],
                  [_sds((t, S5_WIDTH), BF16), _sds((t, 2 * S5_N), BF16), _sds((2, S5_N)), _sds((1, S5_WIDTH))],
                  scratch=[pltpu.VMEM((ts, 2 * S5_N), F32), pltpu.VMEM((2, S5_N), F32)],
                  )(dy0, za, xs, xs, *c_bands, *b_bands, lam, dskip)


def _hgrn_bwd(zh, do, sts, lb, nb, seq):
    nc = seq // CHUNK

    def body(zh_ref, do_ref, sts_ref, lb_ref, dz_ref, dlb_ref, dst_ref):
        @pl.when(pl.program_id(0) == 0)
        def _():
            dst_ref[...] = jnp.zeros_like(dst_ref)
            dlb_ref[...] = jnp.zeros_like(dlb_ref)

        row = lax.broadcasted_iota(jnp.int32, (CHUNK, CHUNK), 0)
        causal = row >= lax.broadcasted_iota(jnp.int32, (CHUNK, CHUNK), 1)
        last_row = lax.broadcasted_iota(jnp.int32, (CHUNK, HG_HEAD), 0) == CHUNK - 1
        for b in range(nb):
            for h in range(HG_HEADS):
                hs = slice(h * HG_HEAD, (h + 1) * HG_HEAD)
                zq = zh_ref[b, :, h * HG_HEAD:(h + 1) * HG_HEAD]
                zf = zh_ref[b, :, HG_WIDTH + h * HG_HEAD:HG_WIDTH + (h + 1) * HG_HEAD]
                zi = zh_ref[b, :, 2 * HG_WIDTH + h * HG_HEAD:2 * HG_WIDTH + (h + 1) * HG_HEAD]
                lbh = lb_ref[:, hs]
                sf, f, sq, qa, bc, bm, bl = _hgrn_gates(zq, zf, lbh)
                k = 1.0 - f
                e_qt = jnp.exp(bc - bm)
                e_kt = jnp.exp(bm - bc)
                e_b = jnp.exp(bc)
                e_kd = jnp.exp(bl - bc)
                e_l = jnp.exp(bl)
                qt, kt, qb, kd = qa * e_qt, k * e_kt, qa * e_b, k * e_kd
                a = jnp.where(causal, _dot_nt(qt, kt), 0.0)
                st = sts_ref[b, 0, h]
                dst = dst_ref[b, h]
                dov = do_ref[b, :, hs]
                da = jnp.where(causal, _dot_nt(dov, zi), 0.0)
                dqt = _hdot(da, kt)
                dkt = _hdot_tn(da, qt)
                dqb = _hdot(dov, st)
                di = _dot_tn(a, dov) + _dot_nt(kd, dst)
                dkd = _hdot(zi, dst)
                de_l = jnp.sum(dst * st, axis=0, keepdims=True)
                dst_ref[b, h] = dst * e_l + _dot_tn(dov, qb)
                dqa = dqt * e_qt + dqb * e_b
                dk = dkt * e_kt + dkd * e_kd
                dbl = jnp.sum(dkd * kd, axis=0, keepdims=True) + de_l * e_l
                db = dqt * qt - dkt * kt + dqb * qb - dkd * kd + jnp.where(last_row, dbl, 0.0)
                df = _cumsum_rows(db, reverse=True) / f - dk
                dzq = dqa * QSCALE * (sq * (1.0 + zq * (1.0 - sq)))
                dzf = df * (1.0 - lbh) * sf * (1.0 - sf)
                dz_ref[b, :, h * HG_HEAD:(h + 1) * HG_HEAD] = dzq.astype(BF16)
                dz_ref[b, :, HG_WIDTH + h * HG_HEAD:HG_WIDTH + (h + 1) * HG_HEAD] = dzf.astype(BF16)
                dz_ref[b, :, 2 * HG_WIDTH + h * HG_HEAD:2 * HG_WIDTH + (h + 1) * HG_HEAD] = di.astype(BF16)
                dlb_ref[:, hs] += jnp.sum(df * (1.0 - sf), axis=0, keepdims=True)

    rev = lambda c: nc - 1 - c
    return _pcall(body, "hgrn_bwd", (nc,),
                  [pl.BlockSpec((nb, CHUNK, 4 * HG_WIDTH), lambda c: (0, rev(c), 0)),
                   pl.BlockSpec((nb, CHUNK, HG_WIDTH), lambda c: (0, rev(c), 0)),
                   pl.BlockSpec((nb, 1, HG_HEADS, HG_HEAD, HG_HEAD), lambda c: (0, rev(c), 0, 0, 0)),
                   _full((1, HG_WIDTH))],
                  [pl.BlockSpec((nb, CHUNK, 3 * HG_WIDTH), lambda c: (0, rev(c), 0)), _full((1, HG_WIDTH))],
                  [_sds((nb, seq, 3 * HG_WIDTH), BF16), _sds((1, HG_WIDTH))],
                  scratch=[pltpu.VMEM((nb, HG_HEADS, HG_HEAD, HG_HEAD), F32)])(zh, do, sts, lb)


def _in_proj_bwd(dza, dzh, dzg, dzgt, dx1, x, g_mix, w_in, tm):
    t = x.shape[0]

    def body(dza_ref, dzh_ref, dzg_ref, dzgt_ref, dx1_ref, x_ref, g_ref, w_ref, dz_ref, dx_ref, dg_ref):
        @pl.when(pl.program_id(0) == 0)
        def _():
            dg_ref[...] = jnp.zeros_like(dg_ref)

        c1, c2, c3 = S5_WIDTH, S5_WIDTH + 3 * HG_WIDTH, S5_WIDTH + 4 * HG_WIDTH
        dz_ref[:, 0:c1] = dza_ref[...]
        dz_ref[:, c1:c2] = dzh_ref[...]
        dz_ref[:, c2:c3] = dzg_ref[...]
        dz_ref[:, c3:] = dzgt_ref[...]
        du = _dot_nt(dz_ref[...], w_ref[...])
        xv = x_ref[...]
        r = lax.rsqrt(jnp.mean(xv * xv, axis=-1, keepdims=True) + EPS)
        xn = xv * r
        dg_ref[...] += jnp.sum(du * xn, axis=0, keepdims=True)
        dxn = du * g_ref[...]
        dx_ref[...] = dx1_ref[...] + r * (dxn - xn * jnp.mean(dxn * xn, axis=-1, keepdims=True))

    row = lambda w: pl.BlockSpec((tm, w), lambda i: (i, 0))
    return _pcall(body, "in_proj_bwd", (t // tm,),
                  [row(S5_WIDTH), row(3 * HG_WIDTH), row(HG_WIDTH), row(2 * D_MODEL), row(D_MODEL), row(D_MODEL),
                   _full((1, D_MODEL)), _full((D_MODEL, N_IN))],
                  [row(N_IN), row(D_MODEL), _full((1, D_MODEL))],
                  [_sds((t, N_IN), BF16), _sds((t, D_MODEL)), _sds((1, D_MODEL))],
                  )(dza, dzh, dzg, dzgt, dx1, x, g_mix, w_in)


def _tie(*arrays):
    return jnp.zeros((SUBLANES, 128), F32) + sum(a.reshape(-1)[0].astype(F32) for a in arrays)


def _after(value, token):
    return value + token[0, 0]


def _local_step(x3, tgt3, weights, sp, emit, emit_small):
    nb, seq, _ = x3.shape
    t = nb * seq
    tm = _token_tile(seq)
    x = x3.reshape(t, D_MODEL)
    tgt = tgt3.reshape(t, D_MODEL)
    row = lambda v: v.reshape(1, -1)

    a_re = sp["s5_a_re"].reshape(S5_N, 1)
    a_im = sp["s5_a_im"].reshape(S5_N, 1)
    ldt = jnp.repeat(sp["s5_log_dt"].reshape(S5_GROUPS), S5_STATE).reshape(S5_N, 1)
    b_re = sp["s5_b_re"].reshape(S5_N, S5_GROUP)
    b_im = sp["s5_b_im"].reshape(S5_N, S5_GROUP)
    lr, li, bb_re, bb_im, lb = _params_fwd(a_re, a_im, ldt, b_re, b_im, sp["hg_lb_logits"])
    lam = jnp.concatenate([lr.reshape(1, S5_N), li.reshape(1, S5_N)], axis=0)
    gps = lambda m: m.reshape(S5_GROUPS, S5_STATE, S5_GROUP)
    swap = lambda m: m.transpose(0, 2, 1)
    b_to_st = (_band_blocks(swap(gps(bb_re))), _band_blocks(swap(gps(bb_im))))
    b_to_ch = (_band_blocks(gps(bb_re)), _band_blocks(gps(bb_im)))
    c_to_ch = (_band_blocks(swap(sp["s5_c_re"])), _band_blocks(swap(-sp["s5_c_im"])))
    c_to_st = (_band_blocks(sp["s5_c_re"]), _band_blocks(-sp["s5_c_im"]))

    g_mix, g_ffn, g_final = row(sp["g_mix"]), row(sp["g_ffn"]), row(sp["g_final"])
    b_glu, gain, dskip, b_conv = row(sp["b_glu"]), row(sp["hg_norm_gain"]), row(sp["s5_d"]), row(sp["b_conv"])

    w_in = weights("in", c_to_st[1])["w_in"]
    u, za, zh, zgt = _in_proj(x, g_mix, w_in, tm)
    xs, y0 = _s5_fwd(za, b_to_st, lam, c_to_ch, dskip, nb, seq, tm)
    o3, sts = _hgrn_fwd(zh.reshape(nb, seq, 4 * HG_WIDTH), lb, nb, seq)
    o = o3.reshape(t, HG_WIDTH)
    wm = weights("mix", _tie(y0, o3))
    x1, u2, pa, pb, ya2, yb = _mix_fwd(x, y0, o, zh, zgt, wm["w_glu"], b_glu, gain, wm["w_pa"], wm["w_pb"],
                                       wm["w_out"], g_ffn, tm)
    wf = weights("ffn", u2)
    h = _ffn_up(u2, wf["w_up"], min(4 * tm, t))
    hc, a, dx2, dx2b, loss, dg_final = _ffn_down_loss(h, x1, tgt, wf["w_conv"], b_conv, wf["w_down"], g_final,
                                                      seq, tm)

    wgrad = functools.partial(_wgrad, tn=256, out_dtype=BF16)
    dhc, db_conv = _ffn_bwd_act(dx2b, hc, wf["w_down"], tm)
    sent = emit({"w_down": wgrad(a, dx2b, "dw_down")})
    dh, dx1, dx1b, dg_ffn, dw_conv = _ffn_bwd_up(dhc, h, dx2, x1, wf["w_conv"], wf["w_up"], _after(g_ffn, sent),
                                                 seq, tm)
    sent = emit({"w_up": wgrad(dh, u2, "dw_up", transpose_out=True), "w_conv": dw_conv})
    (dy0, do, dzg, dzgt, m, dpa, dpb, ya1, dpre, db_glu, dgain) = _mix_bwd(
        dx1, y0, o, zh, zgt, pa, pb, wm["w_glu"], _after(b_glu, sent), gain, wm["w_pa"], wm["w_pb"], wm["w_out"], tm)
    sent = emit({"w_out": wgrad(m, dx1b, "dw_out"), "w_pa": wgrad(ya2, dpa, "dw_pa"),
                 "w_pb": wgrad(yb, dpb, "dw_pb"), "w_glu": wgrad(ya1, dpre, "dw_glu")})
    dzh3, dlb = _hgrn_bwd(zh.reshape(nb, seq, 4 * HG_WIDTH), do.reshape(nb, seq, HG_WIDTH), sts, _after(lb, sent),
                          nb, seq)
    dza, a_s5, dlam, dd = _s5_bwd(dy0, za, xs, c_to_st, b_to_ch, lam, dskip, nb, seq, tm)
    dz, dx, dg_mix = _in_proj_bwd(dza, dzh3.reshape(t, 3 * HG_WIDTH), dzg, dzgt, dx1, x, g_mix, w_in, tm)
    sent = emit({"w_in": wgrad(dz, u, "dw_in", transpose_out=True)})

    band = HG_HEAD
    dbb_band = _wgrad(a_s5, za, "dbb_s5", 512, band=band, after=sent)
    dc_band = _wgrad(xs, dy0, "dc_s5", 512, band=band, after=sent)
    dbb_re = _diag_blocks(dbb_band[:S5_N], S5_STATE, S5_GROUP).reshape(S5_N, S5_GROUP)
    dbb_im = _diag_blocks(dbb_band[S5_N:], S5_STATE, S5_GROUP).reshape(S5_N, S5_GROUP)
    dc_re = _diag_blocks(dc_band[:S5_N], S5_STATE, S5_GROUP).transpose(0, 2, 1)
    dc_im = -_diag_blocks(dc_band[S5_N:], S5_STATE, S5_GROUP).transpose(0, 2, 1)
    da_re, da_im, dldt, db_re, db_im, dlogits = _params_bwd(
        a_re, a_im, ldt, b_re, b_im, sp["hg_lb_logits"],
        dlam[0].reshape(S5_N, 1), dlam[1].reshape(S5_N, 1), dbb_re, dbb_im, dlb)
    emit_small({"g_mix": dg_mix, "s5_a_re": da_re, "s5_a_im": da_im, "s5_log_dt": dldt, "s5_b_re": db_re,
                "s5_b_im": db_im, "s5_c_re": dc_re, "s5_c_im": dc_im, "s5_d": dd, "b_glu": db_glu,
                "hg_lb_logits": dlogits, "hg_norm_gain": dgain, "g_ffn": dg_ffn, "b_conv": db_conv,
                "g_final": dg_final, "loss": loss})
    return dx.reshape(nb, seq, D_MODEL)


def _mesh_peers():
    x, y, c = lax.axis_index("x"), lax.axis_index("y"), lax.axis_index("c")
    peers = []
    for k in range(1, N_DEV):
        px, py, pc = (1 - x if k & 4 else x), (1 - y if k & 2 else y), (1 - c if k & 1 else c)
        peers.append((k, (px, py, pc), 4 * px + 2 * py + pc))
    return 4 * x + 2 * y + c, peers


_HBM = pl.BlockSpec(memory_space=pltpu.HBM)
_SEM = pl.BlockSpec(memory_space=pltpu.SEMAPHORE)


def _exchange_start(name, operands, after):
    n = len(operands)
    me = 4 * lax.axis_index("x") + 2 * lax.axis_index("y") + lax.axis_index("c")
    flags = [per_peer for _, per_peer in operands]
    srcs, lands = [], []
    for arr, per_peer in operands:
        own = lax.dynamic_index_in_dim(arr, me, 0, keepdims=True) if per_peer else arr[None]
        land = lax.dynamic_update_slice_in_dim(lax.empty((N_DEV,) + own.shape[1:], arr.dtype), own, me, 0)
        srcs.append(pltpu.with_memory_space_constraint(arr, pltpu.HBM))
        lands.append(pltpu.with_memory_space_constraint(land, pltpu.HBM))
    copies = (N_DEV - 1) * n

    def body(*refs):
        src_refs, land_refs = refs[:n], refs[n:2 * n]
        send_sems, recv_sems = refs[2 * n + 1], refs[2 * n + 2]
        token = refs[-1]
        my_slab, peers = _mesh_peers()
        for k, peer, slab in peers:
            for i in range(n):
                s = (k - 1) * n + i
                pltpu.make_async_remote_copy(
                    src_ref=src_refs[i].at[slab] if flags[i] else src_refs[i], dst_ref=land_refs[i].at[my_slab],
                    send_sem=send_sems.at[s], recv_sem=recv_sems.at[s], device_id=peer,
                    device_id_type=pl.DeviceIdType.MESH).start()
        token[...] = jnp.zeros_like(token)

    outs = pl.pallas_call(
        body, name=name,
        out_shape=(pltpu.SemaphoreType.DMA((copies,)), pltpu.SemaphoreType.DMA((copies,)),
                   *[pltpu.HBM(a.shape, a.dtype) for a in srcs], *[pltpu.HBM(a.shape, a.dtype) for a in lands],
                   _sds((SUBLANES, 128))),
        in_specs=[_HBM] * (2 * n) + [pl.BlockSpec(memory_space=pl.ANY)],
        out_specs=(_SEM, _SEM, *[_HBM] * (2 * n), pl.BlockSpec(memory_space=pltpu.VMEM)),
        input_output_aliases={i: 2 + i for i in range(2 * n)},
        compiler_params=pltpu.CompilerParams(has_side_effects=pltpu.SideEffectType.DATAFLOW_SIDE_EFFECTING),
    )(*srcs, *lands, after)
    state = (flags, outs[0], outs[1], outs[2:2 + n], outs[2 + n:2 + 2 * n])
    return state, outs[-1]


def _exchange_wait(name, state, after):
    flags, send_sems, recv_sems, srcs, lands = state
    n = len(flags)

    def body(*refs):
        src_refs, land_refs = refs[:n], refs[n:2 * n]
        send_ref, recv_ref = refs[2 * n], refs[2 * n + 1]
        _, peers = _mesh_peers()
        for k, peer, slab in peers:
            for i in range(n):
                s = (k - 1) * n + i
                copy = pltpu.make_async_remote_copy(
                    src_ref=src_refs[i].at[slab] if flags[i] else src_refs[i], dst_ref=land_refs[i].at[slab],
                    send_sem=send_ref.at[s], recv_sem=recv_ref.at[s], device_id=peer,
                    device_id_type=pl.DeviceIdType.MESH)
                copy.wait_send()
                copy.wait_recv()

    outs = pl.pallas_call(
        body, name=name,
        out_shape=(*[pltpu.HBM(a.shape, a.dtype) for a in srcs], *[pltpu.HBM(a.shape, a.dtype) for a in lands]),
        in_specs=[_HBM] * (2 * n) + [_SEM, _SEM, pl.BlockSpec(memory_space=pl.ANY)],
        out_specs=tuple([_HBM] * (2 * n)),
        input_output_aliases={i: i for i in range(2 * n)},
        compiler_params=pltpu.CompilerParams(has_side_effects=pltpu.SideEffectType.DATAFLOW_SIDE_EFFECTING),
    )(*srcs, *lands, send_sems, recv_sems, after)
    return list(outs[n:])


def _join_cols(parts, name, tr):
    _, r, c = parts.shape

    def body(p_ref, o_ref):
        for j in range(N_DEV):
            o_ref[:, j * c:(j + 1) * c] = p_ref[j]

    return _pcall(body, name, (r // tr,), [pl.BlockSpec((N_DEV, tr, c), lambda i: (0, i, 0))],
                  pl.BlockSpec((tr, N_DEV * c), lambda i: (i, 0)), _sds((r, N_DEV * c), parts.dtype))(parts)


def _split_cols(full, name, tr):
    r, c = full.shape[0], full.shape[1] // N_DEV

    def body(f_ref, o_ref):
        for j in range(N_DEV):
            o_ref[j] = f_ref[:, j * c:(j + 1) * c]

    return _pcall(body, name, (r // tr,), [pl.BlockSpec((tr, N_DEV * c), lambda i: (i, 0))],
                  pl.BlockSpec((N_DEV, tr, c), lambda i: (0, i, 0)), _sds((N_DEV, r, c), full.dtype))(full)


def _adamw(parts, w, m, v, name, tile):
    _, rows, cols = w.shape

    def body(p_ref, w_ref, m_ref, v_ref, g_out, d_out, m_out, v_out):
        g = p_ref[0].astype(F32)
        for k in range(1, N_DEV):
            g = g + p_ref[k].astype(F32)
        m1 = ADAM_B1 * m_ref[0] + (1.0 - ADAM_B1) * g
        v1 = ADAM_B2 * v_ref[0] + (1.0 - ADAM_B2) * (g * g)
        m_hat = m1 / (1.0 - ADAM_B1 ** ADAM_STEP)
        v_hat = v1 / (1.0 - ADAM_B2 ** ADAM_STEP)
        g_out[0] = g
        d_out[0] = -ADAM_LR * (m_hat / (jnp.sqrt(v_hat) + ADAM_EPS) + ADAM_WD * w_ref[0])
        m_out[0] = m1
        v_out[0] = v1

    row = pl.BlockSpec((1, tile, cols), lambda i: (0, i, 0))
    return _pcall(body, name, (rows // tile,),
                  [pl.BlockSpec((N_DEV, tile, cols), lambda i: (0, i, 0)), row, row, row],
                  [row, row, row, row], [_sds((1, rows, cols))] * 4)(parts, w, m, v)


BIG = {
    "w_in": ((D_MODEL, N_IN // N_DEV), True, 256),
    "w_glu": ((S5_WIDTH // N_DEV, S5_WIDTH), False, S5_WIDTH // N_DEV),
    "w_pa": ((S5_WIDTH, D_MODEL // N_DEV), True, S5_WIDTH),
    "w_pb": ((HG_WIDTH, D_MODEL // N_DEV), True, HG_WIDTH),
    "w_out": ((D_MODEL // N_DEV, D_MODEL), False, D_MODEL // N_DEV),
    "w_up": ((D_MODEL, 2 * D_FF // N_DEV), True, 256),
    "w_conv": ((CONV_W, 2 * D_FF // N_DEV), True, CONV_W),
    "w_down": ((D_FF // N_DEV, D_MODEL), False, D_FF // N_DEV // 2),
}
UNALIGNED_COLS = ("w_in", "w_up", "w_conv")


def _join_shards(n, parts):
    (a, b), by_cols, _ = BIG[n]
    if not by_cols:
        return parts.reshape(N_DEV * a, b)
    if n in UNALIGNED_COLS:
        return _join_cols(parts, "join_" + n, min(a, 256))
    return parts.transpose(1, 0, 2).reshape(a, N_DEV * b)


def _split_shards(n, full):
    (a, b), by_cols, _ = BIG[n]
    if not by_cols:
        return full.reshape(N_DEV, a, b)
    if n in UNALIGNED_COLS:
        return _split_cols(full, "split_" + n, min(a, 256))
    return full.reshape(a, N_DEV, b).transpose(1, 0, 2)


PACKED = SMALL + (("loss", (1,)),)


def _pack_small(d):
    flat = jnp.concatenate([d[n].reshape(-1) for n, _ in PACKED])
    return jnp.pad(flat, (0, SMALL_ROWS * PACK_W - flat.shape[0])).reshape(SMALL_ROWS, PACK_W)


def _unpack_small(p):
    flat = p.reshape(-1)
    out, off = {}, 0
    for n, shp in PACKED:
        size = math.prod(shp)
        out[n] = flat[off:off + size].reshape(shp)
        off += size
    return out


def kernel(x, g_mix, w_in, s5_a_re, s5_a_im, s5_log_dt, s5_b_re, s5_b_im, s5_c_re, s5_c_im, s5_d, w_glu, b_glu, hg_lb_logits, hg_norm_gain, w_pa, w_pb, w_out, g_ffn, w_up, w_conv, b_conv, w_down, g_final, loss_target, m_g_mix, m_w_in, m_s5_a_re, m_s5_a_im, m_s5_log_dt, m_s5_b_re, m_s5_b_im, m_s5_c_re, m_s5_c_im, m_s5_d, m_w_glu, m_b_glu, m_hg_lb_logits, m_hg_norm_gain, m_w_pa, m_w_pb, m_w_out, m_g_ffn, m_w_up, m_w_conv, m_b_conv, m_w_down, m_g_final, v_g_mix, v_w_in, v_s5_a_re, v_s5_a_im, v_s5_log_dt, v_s5_b_re, v_s5_b_im, v_s5_c_re, v_s5_c_im, v_s5_d, v_w_glu, v_b_glu, v_hg_lb_logits, v_hg_norm_gain, v_w_pa, v_w_pb, v_w_out, v_g_ffn, v_w_up, v_w_conv, v_b_conv, v_w_down, v_g_final):
    given = dict(locals())
    small_names = [n for n, _ in SMALL]

    pay = {n: given[n][0] if n == "w_conv" else given[n][0].astype(BF16) for n in BIG}
    groups = {"in": ["w_in"], "mix": ["w_glu", "w_pa", "w_pb", "w_out"], "ffn": ["w_up", "w_down", "w_conv"]}
    gathers, order = {}, pay["w_in"]
    for grp, names in groups.items():
        gathers[grp], order = _exchange_start("gather_" + grp + "_start", [(pay[n], False) for n in names], order)

    def weights(grp, after):
        got = _exchange_wait("gather_" + grp + "_wait", gathers[grp], after)
        return {n: _join_shards(n, g) for n, g in zip(groups[grp], got)}

    in_flight, started = [], []

    def emit(grads):
        names = list(grads)
        state, token = _exchange_start("grads_" + names[0] + "_start",
                                       [(_split_shards(n, grads[n]), True) for n in names], grads[names[0]])
        in_flight.append((names, state))
        return token

    def emit_small(grads):
        pack = _pack_small(grads)
        state, token = _exchange_start("grads_small_start", [(pack, False)], pack)
        in_flight.append((["small"], state))
        started.append(token)

    sp = {n: (given[n] if n in ("g_final", "hg_lb_logits") else given[n][0]) for n in small_names}
    sp["g_mix"] = _after(sp["g_mix"], order)
    dx = _local_step(x, loss_target, weights, sp, emit, emit_small)

    res = {}
    after = started[-1]
    for names, state in in_flight:
        parts = _exchange_wait("grads_" + names[0] + "_wait", state, after)
        if names != ["small"]:
            for n, part in zip(names, parts):
                res[n] = _adamw(part, given[n], given["m_" + n], given["v_" + n], "adamw_" + n, BIG[n][2])
            after = _tie(*[res[n][0] for n in names])
            continue
        for part in parts:
            zero = jnp.zeros((1,), F32)
            rs4 = _adamw(part, _pack_small({**{k: given[k] for k in small_names}, "loss": zero})[None],
                         _pack_small({**{k: given["m_" + k] for k in small_names}, "loss": zero})[None],
                         _pack_small({**{k: given["v_" + k] for k in small_names}, "loss": zero})[None],
                         "adamw_small", SMALL_ROWS)
            small4 = [_unpack_small(r) for r in rs4]
            for k in small_names:
                res[k] = [us[k] for us in small4]
            total_loss = small4[0]["loss"][0]
    return (total_loss, dx, *[res[n][0] for n in WEIGHT_ORDER], *[res[n][1] for n in WEIGHT_ORDER],
            *[res[n][2] for n in WEIGHT_ORDER], *[res[n][3] for n in WEIGHT_ORDER])
```

```python
import functools
import math

import jax
import jax.numpy as jnp
from jax import lax
from jax.experimental import pallas as pl
from jax.experimental.pallas import tpu as pltpu

F32 = jnp.float32
BF16 = jnp.bfloat16

D_MODEL = 1024
S5_WIDTH = 512
S5_GROUP = 16
S5_GROUPS = 32
S5_STATE = 64
S5_N = S5_GROUPS * S5_STATE
HG_WIDTH = 512
HG_HEAD = 128
HG_HEADS = 4
D_FF = 2816
CONV_W = 3
CHUNK = 64
N_IN = S5_WIDTH + 4 * HG_WIDTH + 2 * D_MODEL
EPS = 1e-6
QSCALE = HG_HEAD ** -0.5

ADAM_LR = 0.001
ADAM_B1 = 0.9
ADAM_B2 = 0.999
ADAM_EPS = 1e-08
ADAM_WD = 0.01
ADAM_STEP = 10

N_DEV = 8
V7X_VMEM_BYTES = 64 * 1024 * 1024
VMEM_LIMIT = V7X_VMEM_BYTES * 7 // 8
SUBLANES = 8
PACK_W = 1024

SMALL = (
    ("g_mix", (1, D_MODEL)),
    ("s5_a_re", (1, S5_GROUPS, S5_STATE)),
    ("s5_a_im", (1, S5_GROUPS, S5_STATE)),
    ("s5_log_dt", (1, S5_GROUPS)),
    ("s5_b_re", (1, S5_GROUPS, S5_STATE, S5_GROUP)),
    ("s5_b_im", (1, S5_GROUPS, S5_STATE, S5_GROUP)),
    ("s5_c_re", (1, S5_GROUPS, S5_GROUP, S5_STATE)),
    ("s5_c_im", (1, S5_GROUPS, S5_GROUP, S5_STATE)),
    ("s5_d", (1, S5_WIDTH)),
    ("b_glu", (1, S5_WIDTH)),
    ("hg_lb_logits", (2, HG_WIDTH)),
    ("hg_norm_gain", (1, HG_WIDTH)),
    ("g_ffn", (1, D_MODEL)),
    ("b_conv", (1, 2 * D_FF)),
    ("g_final", (D_MODEL,)),
)
SMALL_ROWS = 144
WEIGHT_ORDER = ("g_mix", "w_in", "s5_a_re", "s5_a_im", "s5_log_dt", "s5_b_re", "s5_b_im", "s5_c_re", "s5_c_im",
                "s5_d", "w_glu", "b_glu", "hg_lb_logits", "hg_norm_gain", "w_pa", "w_pb", "w_out", "g_ffn",
                "w_up", "w_conv", "b_conv", "w_down", "g_final")


def _pcall(body, name, grid, in_specs, out_specs, out_shape, scratch=()):
    return pl.pallas_call(
        body, name=name, grid=grid, in_specs=in_specs, out_specs=out_specs, out_shape=out_shape,
        scratch_shapes=list(scratch),
        compiler_params=pltpu.CompilerParams(dimension_semantics=("arbitrary",) * len(grid),
                                             vmem_limit_bytes=VMEM_LIMIT),
    )


def _full(shape):
    return pl.BlockSpec(shape, lambda *_: (0,) * len(shape))


def _sds(shape, dtype=F32):
    return jax.ShapeDtypeStruct(shape, dtype)


def _dot(a, b):
    return jnp.dot(a.astype(BF16), b.astype(BF16), preferred_element_type=F32)


def _dot_nt(a, b):
    return lax.dot_general(a.astype(BF16), b.astype(BF16), (((1,), (1,)), ((), ())), preferred_element_type=F32)


def _dot_tn(a, b):
    return lax.dot_general(a.astype(BF16), b.astype(BF16), (((0,), (0,)), ((), ())), preferred_element_type=F32)


def _hdot(a, b):
    return jnp.dot(a, b, preferred_element_type=F32, precision=lax.Precision.HIGHEST)


def _hdot_tn(a, b):
    return lax.dot_general(a, b, (((0,), (0,)), ((), ())), preferred_element_type=F32,
                           precision=lax.Precision.HIGHEST)


def _sigmoid(x):
    return jax.nn.sigmoid(x)


GELU_C = math.sqrt(2.0 / math.pi)
GELU_A = 0.044715


def _gelu(x):
    return 0.5 * x * (1.0 + jnp.tanh(GELU_C * (x + GELU_A * (x * x * x))))


def _gelu_grad(x):
    t = jnp.tanh(GELU_C * (x + GELU_A * (x * x * x)))
    return 0.5 * (1.0 + t) + 0.5 * x * (1.0 - t * t) * (GELU_C * (1.0 + 3.0 * GELU_A * x * x))


def _cumsum_rows(v, reverse=False):
    n = v.shape[0]
    row = lax.broadcasted_iota(jnp.int32, v.shape, 0)
    s = 1
    while s < n:
        if reverse:
            v = v + jnp.where(row < n - s, pltpu.roll(v, n - s, axis=0), 0.0)
        else:
            v = v + jnp.where(row >= s, pltpu.roll(v, s, axis=0), 0.0)
        s *= 2
    return v


def _token_tile(seq):
    return min(256, seq)


def _s5_disc(a_re, a_im, ldt, b_re, b_im):
    dt = jnp.exp(ldt)
    mag = jnp.exp(a_re * dt)
    ang = a_im * dt
    lb_re = mag * jnp.cos(ang)
    lb_im = mag * jnp.sin(ang)
    den = a_re * a_re + a_im * a_im
    n_re = lb_re - 1.0
    n_im = lb_im
    co_re = (n_re * a_re + n_im * a_im) / den
    co_im = (n_im * a_re - n_re * a_im) / den
    bb_re = co_re * b_re - co_im * b_im
    bb_im = co_re * b_im + co_im * b_re
    return lb_re, lb_im, bb_re, bb_im


def _params_fwd(a_re, a_im, ldt, b_re, b_im, logits):
    def body(are, aim, ld, bre, bim, lg, lr_o, li_o, bbr_o, bbi_o, lb_o):
        lr, li, bbr, bbi = _s5_disc(are[...], aim[...], ld[...], bre[...], bim[...])
        lr_o[...] = lr
        li_o[...] = li
        bbr_o[...] = bbr
        bbi_o[...] = bbi
        lb_o[...] = _sigmoid(lg[0:1, :] - lg[1:2, :])

    col, mat = (S5_N, 1), (S5_N, S5_GROUP)
    return _pcall(body, "params_fwd", (1,),
                  [_full(col), _full(col), _full(col), _full(mat), _full(mat), _full((2, HG_WIDTH))],
                  [_full(col), _full(col), _full(mat), _full(mat), _full((1, HG_WIDTH))],
                  [_sds(col), _sds(col), _sds(mat), _sds(mat), _sds((1, HG_WIDTH))])(a_re, a_im, ldt, b_re, b_im, logits)


def _params_bwd(a_re, a_im, ldt, b_re, b_im, logits, dlr, dli, dbbr, dbbi, dlb):
    def body(are, aim, ld, bre, bim, lg, dlr_r, dli_r, dbbr_r, dbbi_r, dlb_r,
             dare_o, daim_o, dld_o, dbre_o, dbim_o, dlg_o):
        _, vjp = jax.vjp(_s5_disc, are[...], aim[...], ld[...], bre[...], bim[...])
        dare, daim, dld, dbre, dbim = vjp((dlr_r[...], dli_r[...], dbbr_r[...], dbbi_r[...]))
        dare_o[...] = dare
        daim_o[...] = daim
        dbre_o[...] = dbre
        dbim_o[...] = dbim
        for g in range(S5_GROUPS):
            dld_o[g:g + 1, :] = jnp.sum(dld[g * S5_STATE:(g + 1) * S5_STATE, :], axis=0, keepdims=True)
        lb = _sigmoid(lg[0:1, :] - lg[1:2, :])
        d0 = dlb_r[...] * lb * (1.0 - lb)
        dlg_o[0:1, :] = d0
        dlg_o[1:2, :] = -d0

    col, mat = (S5_N, 1), (S5_N, S5_GROUP)
    return _pcall(body, "params_bwd", (1,),
                  [_full(col), _full(col), _full(col), _full(mat), _full(mat), _full((2, HG_WIDTH)),
                   _full(col), _full(col), _full(mat), _full(mat), _full((1, HG_WIDTH))],
                  [_full(col), _full(col), _full((S5_GROUPS, 1)), _full(mat), _full(mat), _full((2, HG_WIDTH))],
                  [_sds(col), _sds(col), _sds((S5_GROUPS, 1)), _sds(mat), _sds(mat), _sds((2, HG_WIDTH))],
                  )(a_re, a_im, ldt, b_re, b_im, logits, dlr, dli, dbbr, dbbi, dlb)


def _band_blocks(m):
    g, r, c = m.shape
    gb = g // S5_BANDS
    m4 = m.astype(BF16).reshape(S5_BANDS, gb, r, c)
    on_diag = jnp.eye(gb, dtype=bool)[None, :, None, :, None]
    return jnp.where(on_diag, m4[:, :, :, None, :], 0).reshape(S5_BANDS, gb * r, gb * c)


def _diag_blocks(band, r, c):
    g, nb = band.shape[0] // r, band.shape[1] // c
    on_diag = (jnp.arange(g) % nb)[:, None, None, None] == jnp.arange(nb)[None, None, :, None]
    return jnp.sum(jnp.where(on_diag, band.reshape(g, r, nb, c), 0.0), axis=2)


def _in_proj(x, g_mix, w_in, tm):
    t = x.shape[0]

    def body(x_ref, g_ref, w_ref, u_ref, za_ref, zh_ref, zg_ref):
        xv = x_ref[...]
        r = lax.rsqrt(jnp.mean(xv * xv, axis=-1, keepdims=True) + EPS)
        u = (xv * r * g_ref[...]).astype(BF16)
        u_ref[...] = u
        za_ref[...] = jnp.dot(u, w_ref[:, 0:S5_WIDTH], preferred_element_type=F32)
        zh_ref[...] = jnp.dot(u, w_ref[:, S5_WIDTH:S5_WIDTH + 4 * HG_WIDTH], preferred_element_type=F32)
        zg_ref[...] = jnp.dot(u, w_ref[:, S5_WIDTH + 4 * HG_WIDTH:], preferred_element_type=F32)

    row = lambda w: pl.BlockSpec((tm, w), lambda i: (i, 0))
    return _pcall(body, "in_proj", (t // tm,),
                  [row(D_MODEL), _full((1, D_MODEL)), _full((D_MODEL, N_IN))],
                  [row(D_MODEL), row(S5_WIDTH), row(4 * HG_WIDTH), row(2 * D_MODEL)],
                  [_sds((t, D_MODEL), BF16), _sds((t, S5_WIDTH)), _sds((t, 4 * HG_WIDTH)), _sds((t, 2 * D_MODEL))],
                  )(x, g_mix, w_in)


S5_LANES = 512
S5_BANDS = 4


def _band(q):
    return (slice(q * S5_WIDTH // S5_BANDS, (q + 1) * S5_WIDTH // S5_BANDS),
            slice(q * S5_N // S5_BANDS, (q + 1) * S5_N // S5_BANDS))


def _im(st):
    return slice(S5_N + st.start, S5_N + st.stop)


SCAN_UNROLL = 8


def _complex_scan(buf_ref, lam_ref, st_ref, ts, reverse):
    chunks = [slice(cc * S5_LANES, (cc + 1) * S5_LANES) for cc in range(S5_N // S5_LANES)]
    nch = len(chunks)
    wr = [lam_ref[0:1, re] for re in chunks]
    wi = [-lam_ref[1:2, re] if reverse else lam_ref[1:2, re] for re in chunks]

    def block(ib, carry):
        vr, vi = list(carry[:nch]), list(carry[nch:])
        first = ts - SCAN_UNROLL - ib * SCAN_UNROLL if reverse else ib * SCAN_UNROLL
        first = pl.multiple_of(first, SCAN_UNROLL)
        for k in range(SCAN_UNROLL):
            row = pl.ds(first + (SCAN_UNROLL - 1 - k if reverse else k), 1)
            for cc, re in enumerate(chunks):
                nr = wr[cc] * vr[cc] - wi[cc] * vi[cc] + buf_ref[row, re]
                ni = wr[cc] * vi[cc] + wi[cc] * vr[cc] + buf_ref[row, _im(re)]
                buf_ref[row, re] = nr
                buf_ref[row, _im(re)] = ni
                vr[cc], vi[cc] = nr, ni
        return tuple(vr + vi)

    init = tuple(st_ref[0:1, re] for re in chunks) + tuple(st_ref[1:2, re] for re in chunks)
    last = lax.fori_loop(0, ts // SCAN_UNROLL, block, init)
    for cc, re in enumerate(chunks):
        st_ref[0:1, re] = last[cc]
        st_ref[1:2, re] = last[nch + cc]


BAND_CH = S5_WIDTH // S5_BANDS
BAND_ST = S5_N // S5_BANDS


def _s5_fwd(za, b_bands, lam, c_bands, dskip, nb, seq, ts):
    t = za.shape[0]
    nts = seq // ts

    def body(za_ref, br_ref, bi_ref, lam_ref, cr_ref, ci_ref, d_ref, xs_ref, y_ref, st_ref):
        @pl.when(pl.program_id(1) == 0)
        def _():
            st_ref[...] = jnp.zeros_like(st_ref)

        zav = za_ref[...]
        for q in range(S5_BANDS):
            ch, st = _band(q)
            xs_ref[:, st] = _dot(zav[:, ch], br_ref[q])
            xs_ref[:, _im(st)] = _dot(zav[:, ch], bi_ref[q])
        _complex_scan(xs_ref, lam_ref, st_ref, ts, reverse=False)
        for q in range(S5_BANDS):
            ch, st = _band(q)
            y_ref[:, ch] = (_dot(xs_ref[:, st], cr_ref[q]) + _dot(xs_ref[:, _im(st)], ci_ref[q])
                            + d_ref[:, ch] * zav[:, ch])

    tok = lambda w: pl.BlockSpec((ts, w), lambda b, j: (b * nts + j, 0))
    to_st, to_ch = _full((S5_BANDS, BAND_CH, BAND_ST)), _full((S5_BANDS, BAND_ST, BAND_CH))
    return _pcall(body, "s5_fwd", (nb, nts),
                  [tok(S5_WIDTH), to_st, to_st, _full((2, S5_N)), to_ch, to_ch, _full((1, S5_WIDTH))],
                  [tok(2 * S5_N), tok(S5_WIDTH)],
                  [_sds((t, 2 * S5_N)), _sds((t, S5_WIDTH))],
                  scratch=[pltpu.VMEM((2, S5_N), F32)])(za, *b_bands, lam, *c_bands, dskip)


def _hgrn_gates(zq, zf, lbh):
    sf = _sigmoid(zf)
    f = lbh + (1.0 - lbh) * sf
    sq = _sigmoid(zq)
    qa = zq * sq * QSCALE
    bc = _cumsum_rows(jnp.log(f))
    bm = bc[CHUNK // 2 - 1:CHUNK // 2, :]
    bl = bc[CHUNK - 1:CHUNK, :]
    return sf, f, sq, qa, bc, bm, bl


def _hgrn_fwd(zh, lb, nb, seq):
    nc = seq // CHUNK

    def body(zh_ref, lb_ref, o_ref, sts_ref, st_ref):
        @pl.when(pl.program_id(0) == 0)
        def _():
            st_ref[...] = jnp.zeros_like(st_ref)

        causal = (lax.broadcasted_iota(jnp.int32, (CHUNK, CHUNK), 0)
                  >= lax.broadcasted_iota(jnp.int32, (CHUNK, CHUNK), 1))
        for b in range(nb):
            for h in range(HG_HEADS):
                hs = slice(h * HG_HEAD, (h + 1) * HG_HEAD)
                zq = zh_ref[b, :, h * HG_HEAD:(h + 1) * HG_HEAD]
                zf = zh_ref[b, :, HG_WIDTH + h * HG_HEAD:HG_WIDTH + (h + 1) * HG_HEAD]
                zi = zh_ref[b, :, 2 * HG_WIDTH + h * HG_HEAD:2 * HG_WIDTH + (h + 1) * HG_HEAD]
                _, f, _, qa, bc, bm, bl = _hgrn_gates(zq, zf, lb_ref[:, hs])
                k = 1.0 - f
                qt = qa * jnp.exp(bc - bm)
                kt = k * jnp.exp(bm - bc)
                qb = qa * jnp.exp(bc)
                kd = k * jnp.exp(bl - bc)
                st = st_ref[b, h]
                sts_ref[b, 0, h] = st
                a = jnp.where(causal, _dot_nt(qt, kt), 0.0)
                o_ref[b, :, hs] = _dot(a, zi) + _dot_nt(qb, st)
                st_ref[b, h] = st * jnp.exp(bl) + _dot_tn(zi, kd)

    return _pcall(body, "hgrn_fwd", (nc,),
                  [pl.BlockSpec((nb, CHUNK, 4 * HG_WIDTH), lambda c: (0, c, 0)), _full((1, HG_WIDTH))],
                  [pl.BlockSpec((nb, CHUNK, HG_WIDTH), lambda c: (0, c, 0)),
                   pl.BlockSpec((nb, 1, HG_HEADS, HG_HEAD, HG_HEAD), lambda c: (0, c, 0, 0, 0))],
                  [_sds((nb, seq, HG_WIDTH)), _sds((nb, nc, HG_HEADS, HG_HEAD, HG_HEAD))],
                  scratch=[pltpu.VMEM((nb, HG_HEADS, HG_HEAD, HG_HEAD), F32)])(zh, lb)


def _head_rms(o):
    parts = []
    for h in range(HG_HEADS):
        oh = o[:, h * HG_HEAD:(h + 1) * HG_HEAD]
        r = lax.rsqrt(jnp.mean(oh * oh, axis=-1, keepdims=True) + EPS)
        parts.append(jnp.broadcast_to(r, oh.shape))
    return jnp.concatenate(parts, axis=1)


def _head_mean(v):
    parts = []
    for h in range(HG_HEADS):
        vh = v[:, h * HG_HEAD:(h + 1) * HG_HEAD]
        parts.append(jnp.broadcast_to(jnp.mean(vh, axis=-1, keepdims=True), vh.shape))
    return jnp.concatenate(parts, axis=1)


def _mix_fwd(x, y0, o, zh, zgt, w_glu, b_glu, gain, w_pa, w_pb, w_out, g_ffn, tm):
    t = x.shape[0]

    def body(x_ref, y0_ref, o_ref, zg_ref, zgt_ref, wglu_ref, bglu_ref, gain_ref, wpa_ref, wpb_ref, wout_ref,
             gffn_ref, x1_ref, u2_ref, pa_ref, pb_ref, ya2_ref, yb_ref):
        ya1 = _gelu(y0_ref[...])
        s = _sigmoid(_dot(ya1, wglu_ref[...]) + bglu_ref[...])
        ya2 = (ya1 * s).astype(BF16)
        ov = o_ref[...]
        zg = zg_ref[...]
        yb = (ov * _head_rms(ov) * gain_ref[...] * (zg * _sigmoid(zg))).astype(BF16)
        ya2_ref[...] = ya2
        yb_ref[...] = yb
        pa = jnp.dot(ya2, wpa_ref[...], preferred_element_type=F32)
        pb = jnp.dot(yb, wpb_ref[...], preferred_element_type=F32)
        pa_ref[...] = pa.astype(BF16)
        pb_ref[...] = pb.astype(BF16)
        m = _sigmoid(zgt_ref[:, 0:D_MODEL]) * pa + _sigmoid(zgt_ref[:, D_MODEL:]) * pb
        x1 = x_ref[...] + _dot(m, wout_ref[...])
        x1_ref[...] = x1
        r = lax.rsqrt(jnp.mean(x1 * x1, axis=-1, keepdims=True) + EPS)
        u2_ref[...] = (x1 * r * gffn_ref[...]).astype(BF16)

    row = lambda w: pl.BlockSpec((tm, w), lambda i: (i, 0))
    return _pcall(body, "mix_fwd", (t // tm,),
                  [row(D_MODEL), row(S5_WIDTH), row(HG_WIDTH), pl.BlockSpec((tm, HG_WIDTH), lambda i: (i, 3)),
                   row(2 * D_MODEL), _full((S5_WIDTH, S5_WIDTH)), _full((1, S5_WIDTH)), _full((1, HG_WIDTH)),
                   _full((S5_WIDTH, D_MODEL)), _full((HG_WIDTH, D_MODEL)), _full((D_MODEL, D_MODEL)),
                   _full((1, D_MODEL))],
                  [row(D_MODEL), row(D_MODEL), row(D_MODEL), row(D_MODEL), row(S5_WIDTH), row(HG_WIDTH)],
                  [_sds((t, D_MODEL)), _sds((t, D_MODEL), BF16), _sds((t, D_MODEL), BF16), _sds((t, D_MODEL), BF16),
                   _sds((t, S5_WIDTH), BF16), _sds((t, HG_WIDTH), BF16)],
                  )(x, y0, o, zh, zgt, w_glu, b_glu, gain, w_pa, w_pb, w_out, g_ffn)


FF_COLS = 256
FF_UP_TILE = 1408


def _ffn_up(u2, w_up, tm):
    t = u2.shape[0]
    n = 2 * D_FF

    def body(u_ref, w_ref, h_ref):
        h_ref[...] = jnp.dot(u_ref[...], w_ref[...], preferred_element_type=F32).astype(BF16)

    return _pcall(body, "ffn_up", (n // FF_UP_TILE, t // tm),
                  [pl.BlockSpec((tm, D_MODEL), lambda j, i: (i, 0)),
                   pl.BlockSpec((D_MODEL, FF_UP_TILE), lambda j, i: (0, j))],
                  pl.BlockSpec((tm, FF_UP_TILE), lambda j, i: (i, j)),
                  _sds((t, n), BF16))(u2, w_up)


HALO = 16


def _conv_cols(h_ref, halo_ref, valid, wc_ref, bc_ref, c0):
    cs = slice(c0, c0 + FF_COLS)
    cur = h_ref[:, cs].astype(F32)
    prev = jnp.where(valid, halo_ref[:, cs].astype(F32), 0.0)
    full = jnp.concatenate([prev, cur], axis=0)
    h1 = pltpu.roll(full, 1, axis=0)[HALO:]
    h2 = pltpu.roll(full, 2, axis=0)[HALO:]
    return h2 * wc_ref[0:1, cs] + h1 * wc_ref[1:2, cs] + cur * wc_ref[2:3, cs] + bc_ref[:, cs]


def _ffn_down_loss(h, x1, tgt, w_conv, b_conv, w_down, g_final, seq, tm):
    t = h.shape[0]
    tps = seq // tm
    n = 2 * D_FF

    def body(h_ref, halo_ref, x1_ref, tgt_ref, wc_ref, bc_ref, wd_ref, gf_ref,
             hc_ref, a_ref, dx2_ref, dx2b_ref, loss_ref, dgf_ref):
        i = pl.program_id(0)

        @pl.when(i == 0)
        def _():
            loss_ref[...] = jnp.zeros_like(loss_ref)
            dgf_ref[...] = jnp.zeros_like(dgf_ref)

        valid = (i % tps) != 0
        x2 = x1_ref[...]
        for j in range(D_FF // FF_COLS):
            gate = _conv_cols(h_ref, halo_ref, valid, wc_ref, bc_ref, j * FF_COLS)
            val = _conv_cols(h_ref, halo_ref, valid, wc_ref, bc_ref, D_FF + j * FF_COLS)
            hc_ref[:, j * FF_COLS:(j + 1) * FF_COLS] = gate.astype(BF16)
            hc_ref[:, D_FF + j * FF_COLS:D_FF + (j + 1) * FF_COLS] = val.astype(BF16)
            a = (gate * _sigmoid(gate) * val).astype(BF16)
            a_ref[:, j * FF_COLS:(j + 1) * FF_COLS] = a
            x2 = x2 + jnp.dot(a, wd_ref[j * FF_COLS:(j + 1) * FF_COLS, :], preferred_element_type=F32)
        r = lax.rsqrt(jnp.mean(x2 * x2, axis=-1, keepdims=True) + EPS)
        xn = x2 * r
        g = gf_ref[...]
        e = xn * g - tgt_ref[...]
        loss_ref[...] += (0.5 / D_MODEL) * jnp.sum(e * e).reshape(1, 1)
        dy = e * (1.0 / D_MODEL)
        dgf_ref[...] += jnp.sum(dy * xn, axis=0, keepdims=True)
        dxn = dy * g
        dx2 = r * (dxn - xn * jnp.mean(dxn * xn, axis=-1, keepdims=True))
        dx2_ref[...] = dx2
        dx2b_ref[...] = dx2.astype(BF16)

    row = lambda w: pl.BlockSpec((tm, w), lambda i: (i, 0))
    halo = pl.BlockSpec((HALO, n), lambda i: (jnp.maximum(i * (tm // HALO) - 1, 0), 0))
    return _pcall(body, "ffn_down_loss", (t // tm,),
                  [row(n), halo, row(D_MODEL), row(D_MODEL), _full((CONV_W, n)), _full((1, n)),
                   _full((D_FF, D_MODEL)), _full((1, D_MODEL))],
                  [row(n), row(D_FF), row(D_MODEL), row(D_MODEL), _full((1, 1)), _full((1, D_MODEL))],
                  [_sds((t, n), BF16), _sds((t, D_FF), BF16), _sds((t, D_MODEL)), _sds((t, D_MODEL), BF16),
                   _sds((1, 1)), _sds((1, D_MODEL))],
                  )(h, h, x1, tgt, w_conv, b_conv, w_down, g_final)


def _wgrad(a, b, name, tn, out_dtype=F32, transpose_out=False, band=None, after=None):
    t, m = a.shape
    n = b.shape[1] if band is None else band
    nbands = 1 if band is None else b.shape[1] // band
    after = b if after is None else after

    def body(a_ref, b_ref, after_ref, o_ref):
        r = _dot_tn(a_ref[...], b_ref[...])
        o_ref[...] = (r.T if transpose_out else r).astype(out_dtype)

    if transpose_out:
        out_spec, out_shape = pl.BlockSpec((n, tn), lambda i: (0, i)), _sds((n, m), out_dtype)
    else:
        out_spec, out_shape = pl.BlockSpec((tn, n), lambda i: (i, 0)), _sds((m, n), out_dtype)
    return _pcall(body, name, (m // tn,),
                  [pl.BlockSpec((t, tn), lambda i: (0, i)), pl.BlockSpec((t, n), lambda i: (0, i % nbands)),
                   pl.BlockSpec(memory_space=pl.ANY)],
                  out_spec, out_shape)(a, b, after)


def _ffn_bwd_act(dx2b, hc, w_down, tm):
    t = hc.shape[0]
    n = 2 * D_FF

    def body(dx2_ref, hc_ref, wd_ref, dhc_ref, dbc_ref):
        @pl.when(pl.program_id(0) == 0)
        def _():
            dbc_ref[...] = jnp.zeros_like(dbc_ref)

        dx2 = dx2_ref[...]
        for j in range(D_FF // FF_COLS):
            gs = slice(j * FF_COLS, (j + 1) * FF_COLS)
            vs = slice(D_FF + j * FF_COLS, D_FF + (j + 1) * FF_COLS)
            gate = hc_ref[:, gs].astype(F32)
            val = hc_ref[:, vs].astype(F32)
            da = _dot_nt(dx2, wd_ref[gs, :])
            sg = _sigmoid(gate)
            dgate = da * val * (sg * (1.0 + gate * (1.0 - sg)))
            dval = da * (gate * sg)
            dhc_ref[:, gs] = dgate.astype(BF16)
            dhc_ref[:, vs] = dval.astype(BF16)
            dbc_ref[:, gs] += jnp.sum(dgate, axis=0, keepdims=True)
            dbc_ref[:, vs] += jnp.sum(dval, axis=0, keepdims=True)

    row = lambda w: pl.BlockSpec((tm, w), lambda i: (i, 0))
    return _pcall(body, "ffn_bwd_act", (t // tm,),
                  [row(D_MODEL), row(n), _full((D_FF, D_MODEL))],
                  [row(n), _full((1, n))],
                  [_sds((t, n), BF16), _sds((1, n))],
                  )(dx2b, hc, w_down)


def _ffn_bwd_up(dhc, h, dx2, x1, w_conv, w_up, g_ffn, seq, tm):
    t = dhc.shape[0]
    tps = seq // tm
    n = 2 * D_FF
    last = t // HALO - 1

    def body(dhc_ref, halo_ref, h_ref, dx2_ref, x1_ref, wc_ref, wu_ref, gf_ref,
             dh_ref, dx1_ref, dx1b_ref, dgf_ref, dwc_ref):
        i = pl.program_id(0)

        @pl.when(i == 0)
        def _():
            dgf_ref[...] = jnp.zeros_like(dgf_ref)
            dwc_ref[...] = jnp.zeros_like(dwc_ref)

        valid = ((i + 1) % tps) != 0
        du2 = jnp.zeros((tm, D_MODEL), F32)
        for j in range(n // FF_COLS):
            cs = slice(j * FF_COLS, (j + 1) * FF_COLS)
            cur = dhc_ref[:, cs].astype(F32)
            nxt = jnp.where(valid, halo_ref[:, cs].astype(F32), 0.0)
            full = jnp.concatenate([cur, nxt], axis=0)
            d1 = pltpu.roll(full, tm + HALO - 1, axis=0)[:tm]
            d2 = pltpu.roll(full, tm + HALO - 2, axis=0)[:tm]
            dh = (cur * wc_ref[2:3, cs] + d1 * wc_ref[1:2, cs] + d2 * wc_ref[0:1, cs]).astype(BF16)
            dh_ref[:, cs] = dh
            du2 = du2 + _dot_nt(dh, wu_ref[:, cs])
            hv = h_ref[:, cs].astype(F32)
            dwc_ref[0:1, cs] += jnp.sum(hv * d2, axis=0, keepdims=True)
            dwc_ref[1:2, cs] += jnp.sum(hv * d1, axis=0, keepdims=True)
            dwc_ref[2:3, cs] += jnp.sum(hv * cur, axis=0, keepdims=True)
        x1 = x1_ref[...]
        r = lax.rsqrt(jnp.mean(x1 * x1, axis=-1, keepdims=True) + EPS)
        xn = x1 * r
        dgf_ref[...] += jnp.sum(du2 * xn, axis=0, keepdims=True)
        dxn = du2 * gf_ref[...]
        dx1 = dx2_ref[...] + r * (dxn - xn * jnp.mean(dxn * xn, axis=-1, keepdims=True))
        dx1_ref[...] = dx1
        dx1b_ref[...] = dx1.astype(BF16)

    row = lambda w: pl.BlockSpec((tm, w), lambda i: (i, 0))
    halo = pl.BlockSpec((HALO, n), lambda i: (jnp.minimum((i + 1) * (tm // HALO), last), 0))
    return _pcall(body, "ffn_bwd_up", (t // tm,),
                  [row(n), halo, row(n), row(D_MODEL), row(D_MODEL), _full((CONV_W, n)), _full((D_MODEL, n)),
                   _full((1, D_MODEL))],
                  [row(n), row(D_MODEL), row(D_MODEL), _full((1, D_MODEL)), _full((CONV_W, n))],
                  [_sds((t, n), BF16), _sds((t, D_MODEL)), _sds((t, D_MODEL), BF16), _sds((1, D_MODEL)),
                   _sds((CONV_W, n))],
                  )(dhc, dhc, h, dx2, x1, w_conv, w_up, g_ffn)


def _mix_bwd(dx1, y0, o, zh, zgt, pa, pb, w_glu, b_glu, gain, w_pa, w_pb, w_out, tm):
    t = dx1.shape[0]

    def body(dx1_ref, y0_ref, o_ref, zg_ref, zgt_ref, pa_ref, pb_ref, wglu_ref, bglu_ref, gain_ref, wpa_ref,
             wpb_ref, wout_ref,
             dy0_ref, do_ref, dzg_ref, dzgt_ref, m_ref, dpa_ref, dpb_ref, ya1_ref, dpre_ref, dbglu_ref, dgain_ref):
        @pl.when(pl.program_id(0) == 0)
        def _():
            dbglu_ref[...] = jnp.zeros_like(dbglu_ref)
            dgain_ref[...] = jnp.zeros_like(dgain_ref)

        dm = _dot_nt(dx1_ref[...], wout_ref[...])
        sga = _sigmoid(zgt_ref[:, 0:D_MODEL])
        sgb = _sigmoid(zgt_ref[:, D_MODEL:])
        pa = pa_ref[...].astype(F32)
        pb = pb_ref[...].astype(F32)
        m_ref[...] = (sga * pa + sgb * pb).astype(BF16)
        dzgt_ref[:, 0:D_MODEL] = (dm * pa * sga * (1.0 - sga)).astype(BF16)
        dzgt_ref[:, D_MODEL:] = (dm * pb * sgb * (1.0 - sgb)).astype(BF16)
        dpa = (dm * sga).astype(BF16)
        dpb = (dm * sgb).astype(BF16)
        dpa_ref[...] = dpa
        dpb_ref[...] = dpb
        dya2 = _dot_nt(dpa, wpa_ref[...])
        dyb = _dot_nt(dpb, wpb_ref[...])
        y0 = y0_ref[...]
        ya1 = _gelu(y0)
        ya1_ref[...] = ya1.astype(BF16)
        s = _sigmoid(_dot(ya1, wglu_ref[...]) + bglu_ref[...])
        dpre = dya2 * ya1 * s * (1.0 - s)
        dpre_ref[...] = dpre.astype(BF16)
        dbglu_ref[...] += jnp.sum(dpre, axis=0, keepdims=True)
        dya1 = dya2 * s + _dot_nt(dpre, wglu_ref[...])
        dy0_ref[...] = dya1 * _gelu_grad(y0)
        ov = o_ref[...]
        zg = zg_ref[...]
        oh = ov * _head_rms(ov)
        on = oh * gain_ref[...]
        sz = _sigmoid(zg)
        dzg_ref[...] = (dyb * on * (sz * (1.0 + zg * (1.0 - sz)))).astype(BF16)
        don = dyb * (zg * sz)
        dgain_ref[...] += jnp.sum(don * oh, axis=0, keepdims=True)
        doh = don * gain_ref[...]
        do_ref[...] = _head_rms(ov) * (doh - oh * _head_mean(doh * oh))

    row = lambda w: pl.BlockSpec((tm, w), lambda i: (i, 0))
    return _pcall(body, "mix_bwd", (t // tm,),
                  [row(D_MODEL), row(S5_WIDTH), row(HG_WIDTH), pl.BlockSpec((tm, HG_WIDTH), lambda i: (i, 3)),
                   row(2 * D_MODEL), row(D_MODEL), row(D_MODEL), _full((S5_WIDTH, S5_WIDTH)), _full((1, S5_WIDTH)),
                   _full((1, HG_WIDTH)), _full((S5_WIDTH, D_MODEL)), _full((HG_WIDTH, D_MODEL)),
                   _full((D_MODEL, D_MODEL))],
                  [row(S5_WIDTH), row(HG_WIDTH), row(HG_WIDTH), row(2 * D_MODEL), row(D_MODEL), row(D_MODEL),
                   row(D_MODEL), row(S5_WIDTH), row(S5_WIDTH), _full((1, S5_WIDTH)), _full((1, HG_WIDTH))],
                  [_sds((t, S5_WIDTH)), _sds((t, HG_WIDTH)), _sds((t, HG_WIDTH), BF16), _sds((t, 2 * D_MODEL), BF16),
                   _sds((t, D_MODEL), BF16), _sds((t, D_MODEL), BF16), _sds((t, D_MODEL), BF16),
                   _sds((t, S5_WIDTH), BF16), _sds((t, S5_WIDTH), BF16), _sds((1, S5_WIDTH)), _sds((1, HG_WIDTH))],
                  )(dx1, y0, o, zh, zgt, pa, pb, w_glu, b_glu, gain, w_pa, w_pb, w_out)


def _s5_bwd(dy0, za, xs, c_bands, b_bands, lam, dskip, nb, seq, ts):
    t = za.shape[0]
    nts = seq // ts

    def body(dy0_ref, za_ref, xs_ref, halo_ref, cr_ref, ci_ref, br_ref, bi_ref, lam_ref, d_ref,
             dza_ref, a_ref, dlam_ref, dd_ref, acc_ref, st_ref):
        b, j = pl.program_id(0), pl.program_id(1)

        @pl.when((b == 0) & (j == 0))
        def _():
            dlam_ref[...] = jnp.zeros_like(dlam_ref)
            dd_ref[...] = jnp.zeros_like(dd_ref)

        @pl.when(j == 0)
        def _():
            st_ref[...] = jnp.zeros_like(st_ref)

        dy0 = dy0_ref[...]
        for q in range(S5_BANDS):
            ch, st = _band(q)
            acc_ref[:, st] = _dot(dy0[:, ch], cr_ref[q])
            acc_ref[:, _im(st)] = _dot(dy0[:, ch], ci_ref[q])
        _complex_scan(acc_ref, lam_ref, st_ref, ts, reverse=True)
        av = acc_ref[...]
        a_ref[...] = av.astype(BF16)
        first = jnp.where(j == nts - 1, 0.0, halo_ref[SUBLANES - 1:SUBLANES, :])
        rows = lax.broadcasted_iota(jnp.int32, (ts, 2 * S5_N), 0)
        xp = jnp.where(rows == 0, first, pltpu.roll(xs_ref[...], 1, axis=0))
        ar, ai = av[:, :S5_N], av[:, S5_N:]
        xr, xi = xp[:, :S5_N], xp[:, S5_N:]
        dlam_ref[0:1, :] += jnp.sum(ar * xr + ai * xi, axis=0, keepdims=True)
        dlam_ref[1:2, :] += jnp.sum(ai * xr - ar * xi, axis=0, keepdims=True)
        for q in range(S5_BANDS):
            ch, st = _band(q)
            dza_ref[:, ch] = (_dot(a_ref[:, st], br_ref[q]) + _dot(a_ref[:, _im(st)], bi_ref[q])
                              + d_ref[:, ch] * dy0[:, ch]).astype(BF16)
        dd_ref[...] += jnp.sum(dy0 * za_ref[...], axis=0, keepdims=True)

    tile = lambda b, j: b * nts + (nts - 1 - j)
    tok = lambda w: pl.BlockSpec((ts, w), lambda b, j: (tile(b, j), 0))
    halo = pl.BlockSpec((SUBLANES, 2 * S5_N),
                        lambda b, j: (jnp.maximum(tile(b, j) * (ts // SUBLANES) - 1, 0), 0))
    to_st, to_ch = _full((S5_BANDS, BAND_CH, BAND_ST)), _full((S5_BANDS, BAND_ST, BAND_CH))
    return _pcall(body, "s5_bwd", (nb, nts),
                  [tok(S5_WIDTH), tok(S5_WIDTH), tok(2 * S5_N), halo, to_st, to_st, to_ch, to_ch,
                   _full((2, S5_N)), _full((1, S5_WIDTH))],
                  [tok(S5_WIDTH), tok(2 * S5_N), _full((2, S5_N)), _full((1, S5_WIDTH))],
                  [_sds((t, S5_WIDTH), BF16), _sds((t, 2 * S5_N), BF16), _sds((2, S5_N)), _sds((1, S5_WIDTH))],
                  scratch=[pltpu.VMEM((ts, 2 * S5_N), F32), pltpu.VMEM((2, S5_N), F32)],
                  )(dy0, za, xs, xs, *c_bands, *b_bands, lam, dskip)


def _hgrn_bwd(zh, do, sts, lb, nb, seq):
    nc = seq // CHUNK

    def body(zh_ref, do_ref, sts_ref, lb_ref, dz_ref, dlb_ref, dst_ref):
        @pl.when(pl.program_id(0) == 0)
        def _():
            dst_ref[...] = jnp.zeros_like(dst_ref)
            dlb_ref[...] = jnp.zeros_like(dlb_ref)

        row = lax.broadcasted_iota(jnp.int32, (CHUNK, CHUNK), 0)
        causal = row >= lax.broadcasted_iota(jnp.int32, (CHUNK, CHUNK), 1)
        last_row = lax.broadcasted_iota(jnp.int32, (CHUNK, HG_HEAD), 0) == CHUNK - 1
        for b in range(nb):
            for h in range(HG_HEADS):
                hs = slice(h * HG_HEAD, (h + 1) * HG_HEAD)
                zq = zh_ref[b, :, h * HG_HEAD:(h + 1) * HG_HEAD]
                zf = zh_ref[b, :, HG_WIDTH + h * HG_HEAD:HG_WIDTH + (h + 1) * HG_HEAD]
                zi = zh_ref[b, :, 2 * HG_WIDTH + h * HG_HEAD:2 * HG_WIDTH + (h + 1) * HG_HEAD]
                lbh = lb_ref[:, hs]
                sf, f, sq, qa, bc, bm, bl = _hgrn_gates(zq, zf, lbh)
                k = 1.0 - f
                e_qt = jnp.exp(bc - bm)
                e_kt = jnp.exp(bm - bc)
                e_b = jnp.exp(bc)
                e_kd = jnp.exp(bl - bc)
                e_l = jnp.exp(bl)
                qt, kt, qb, kd = qa * e_qt, k * e_kt, qa * e_b, k * e_kd
                a = jnp.where(causal, _dot_nt(qt, kt), 0.0)
                st = sts_ref[b, 0, h]
                dst = dst_ref[b, h]
                dov = do_ref[b, :, hs]
                da = jnp.where(causal, _dot_nt(dov, zi), 0.0)
                dqt = _hdot(da, kt)
                dkt = _hdot_tn(da, qt)
                dqb = _dot(dov, st)
                di = _dot_tn(a, dov) + _dot_nt(kd, dst)
                dkd = _dot(zi, dst)
                de_l = jnp.sum(dst * st, axis=0, keepdims=True)
                dst_ref[b, h] = dst * e_l + _dot_tn(dov, qb)
                dqa = dqt * e_qt + dqb * e_b
                dk = dkt * e_kt + dkd * e_kd
                dbl = jnp.sum(dkd * kd, axis=0, keepdims=True) + de_l * e_l
                db = dqt * qt - dkt * kt + dqb * qb - dkd * kd + jnp.where(last_row, dbl, 0.0)
                df = _cumsum_rows(db, reverse=True) / f - dk
                dzq = dqa * QSCALE * (sq * (1.0 + zq * (1.0 - sq)))
                dzf = df * (1.0 - lbh) * sf * (1.0 - sf)
                dz_ref[b, :, h * HG_HEAD:(h + 1) * HG_HEAD] = dzq.astype(BF16)
                dz_ref[b, :, HG_WIDTH + h * HG_HEAD:HG_WIDTH + (h + 1) * HG_HEAD] = dzf.astype(BF16)
                dz_ref[b, :, 2 * HG_WIDTH + h * HG_HEAD:2 * HG_WIDTH + (h + 1) * HG_HEAD] = di.astype(BF16)
                dlb_ref[:, hs] += jnp.sum(df * (1.0 - sf), axis=0, keepdims=True)

    rev = lambda c: nc - 1 - c
    return _pcall(body, "hgrn_bwd", (nc,),
                  [pl.BlockSpec((nb, CHUNK, 4 * HG_WIDTH), lambda c: (0, rev(c), 0)),
                   pl.BlockSpec((nb, CHUNK, HG_WIDTH), lambda c: (0, rev(c), 0)),
                   pl.BlockSpec((nb, 1, HG_HEADS, HG_HEAD, HG_HEAD), lambda c: (0, rev(c), 0, 0, 0)),
                   _full((1, HG_WIDTH))],
                  [pl.BlockSpec((nb, CHUNK, 3 * HG_WIDTH), lambda c: (0, rev(c), 0)), _full((1, HG_WIDTH))],
                  [_sds((nb, seq, 3 * HG_WIDTH), BF16), _sds((1, HG_WIDTH))],
                  scratch=[pltpu.VMEM((nb, HG_HEADS, HG_HEAD, HG_HEAD), F32)])(zh, do, sts, lb)


def _in_proj_bwd(dza, dzh, dzg, dzgt, dx1, x, g_mix, w_in, tm):
    t = x.shape[0]

    def body(dza_ref, dzh_ref, dzg_ref, dzgt_ref, dx1_ref, x_ref, g_ref, w_ref, dz_ref, dx_ref, dg_ref):
        @pl.when(pl.program_id(0) == 0)
        def _():
            dg_ref[...] = jnp.zeros_like(dg_ref)

        c1, c2, c3 = S5_WIDTH, S5_WIDTH + 3 * HG_WIDTH, S5_WIDTH + 4 * HG_WIDTH
        dz_ref[:, 0:c1] = dza_ref[...]
        dz_ref[:, c1:c2] = dzh_ref[...]
        dz_ref[:, c2:c3] = dzg_ref[...]
        dz_ref[:, c3:] = dzgt_ref[...]
        du = _dot_nt(dz_ref[...], w_ref[...])
        xv = x_ref[...]
        r = lax.rsqrt(jnp.mean(xv * xv, axis=-1, keepdims=True) + EPS)
        xn = xv * r
        dg_ref[...] += jnp.sum(du * xn, axis=0, keepdims=True)
        dxn = du * g_ref[...]
        dx_ref[...] = dx1_ref[...] + r * (dxn - xn * jnp.mean(dxn * xn, axis=-1, keepdims=True))

    row = lambda w: pl.BlockSpec((tm, w), lambda i: (i, 0))
    return _pcall(body, "in_proj_bwd", (t // tm,),
                  [row(S5_WIDTH), row(3 * HG_WIDTH), row(HG_WIDTH), row(2 * D_MODEL), row(D_MODEL), row(D_MODEL),
                   _full((1, D_MODEL)), _full((D_MODEL, N_IN))],
                  [row(N_IN), row(D_MODEL), _full((1, D_MODEL))],
                  [_sds((t, N_IN), BF16), _sds((t, D_MODEL)), _sds((1, D_MODEL))],
                  )(dza, dzh, dzg, dzgt, dx1, x, g_mix, w_in)


def _tie(*arrays):
    return jnp.zeros((SUBLANES, 128), F32) + sum(a.reshape(-1)[0].astype(F32) for a in arrays)


def _after(value, token):
    return value + token[0, 0]


def _local_step(x3, tgt3, weights, sp, emit, emit_small):
    nb, seq, _ = x3.shape
    t = nb * seq
    tm = _token_tile(seq)
    x = x3.reshape(t, D_MODEL)
    tgt = tgt3.reshape(t, D_MODEL)
    row = lambda v: v.reshape(1, -1)

    a_re = sp["s5_a_re"].reshape(S5_N, 1)
    a_im = sp["s5_a_im"].reshape(S5_N, 1)
    ldt = jnp.repeat(sp["s5_log_dt"].reshape(S5_GROUPS), S5_STATE).reshape(S5_N, 1)
    b_re = sp["s5_b_re"].reshape(S5_N, S5_GROUP)
    b_im = sp["s5_b_im"].reshape(S5_N, S5_GROUP)
    lr, li, bb_re, bb_im, lb = _params_fwd(a_re, a_im, ldt, b_re, b_im, sp["hg_lb_logits"])
    lam = jnp.concatenate([lr.reshape(1, S5_N), li.reshape(1, S5_N)], axis=0)
    gps = lambda m: m.reshape(S5_GROUPS, S5_STATE, S5_GROUP)
    swap = lambda m: m.transpose(0, 2, 1)
    b_to_st = (_band_blocks(swap(gps(bb_re))), _band_blocks(swap(gps(bb_im))))
    b_to_ch = (_band_blocks(gps(bb_re)), _band_blocks(gps(bb_im)))
    c_to_ch = (_band_blocks(swap(sp["s5_c_re"])), _band_blocks(swap(-sp["s5_c_im"])))
    c_to_st = (_band_blocks(sp["s5_c_re"]), _band_blocks(-sp["s5_c_im"]))

    g_mix, g_ffn, g_final = row(sp["g_mix"]), row(sp["g_ffn"]), row(sp["g_final"])
    b_glu, gain, dskip, b_conv = row(sp["b_glu"]), row(sp["hg_norm_gain"]), row(sp["s5_d"]), row(sp["b_conv"])

    w_in = weights("in", lam, *b_to_st, *b_to_ch, *c_to_ch, *c_to_st)["w_in"]
    u, za, zh, zgt = _in_proj(x, g_mix, w_in, tm)
    xs, y0 = _s5_fwd(za, b_to_st, lam, c_to_ch, dskip, nb, seq, tm)
    o3, sts = _hgrn_fwd(zh.reshape(nb, seq, 4 * HG_WIDTH), lb, nb, seq)
    o = o3.reshape(t, HG_WIDTH)
    wm = weights("mix", y0, o3)
    x1, u2, pa, pb, ya2, yb = _mix_fwd(x, y0, o, zh, zgt, wm["w_glu"], b_glu, gain, wm["w_pa"], wm["w_pb"],
                                       wm["w_out"], g_ffn, tm)
    wf = weights("ffn", u2)
    h = _ffn_up(u2, wf["w_up"], min(4 * tm, t))
    hc, a, dx2, dx2b, loss, dg_final = _ffn_down_loss(h, x1, tgt, wf["w_conv"], b_conv, wf["w_down"], g_final,
                                                      seq, tm)

    wgrad = functools.partial(_wgrad, tn=256, out_dtype=BF16)
    dhc, db_conv = _ffn_bwd_act(dx2b, hc, wf["w_down"], tm)
    sent = emit({"w_down": wgrad(a, dx2b, "dw_down")})
    dh, dx1, dx1b, dg_ffn, dw_conv = _ffn_bwd_up(dhc, h, dx2, x1, wf["w_conv"], wf["w_up"], _after(g_ffn, sent),
                                                 seq, tm)
    sent = emit({"w_up": wgrad(dh, u2, "dw_up", transpose_out=True), "w_conv": dw_conv})
    (dy0, do, dzg, dzgt, m, dpa, dpb, ya1, dpre, db_glu, dgain) = _mix_bwd(
        dx1, y0, o, zh, zgt, pa, pb, wm["w_glu"], _after(b_glu, sent), gain, wm["w_pa"], wm["w_pb"], wm["w_out"], tm)
    sent = emit({"w_out": wgrad(m, dx1b, "dw_out"), "w_pa": wgrad(ya2, dpa, "dw_pa"),
                 "w_pb": wgrad(yb, dpb, "dw_pb"), "w_glu": wgrad(ya1, dpre, "dw_glu")})
    dzh3, dlb = _hgrn_bwd(zh.reshape(nb, seq, 4 * HG_WIDTH), do.reshape(nb, seq, HG_WIDTH), sts, _after(lb, sent),
                          nb, seq)
    dza, a_s5, dlam, dd = _s5_bwd(dy0, za, xs, c_to_st, b_to_ch, lam, dskip, nb, seq, tm)
    dz, dx, dg_mix = _in_proj_bwd(dza, dzh3.reshape(t, 3 * HG_WIDTH), dzg, dzgt, dx1, x, g_mix, w_in, tm)
    sent = emit({"w_in": wgrad(dz, u, "dw_in", transpose_out=True)})

    band = HG_HEAD
    dbb_band = _wgrad(a_s5, za, "dbb_s5", 512, band=band, after=sent)
    dc_band = _wgrad(xs, dy0, "dc_s5", 512, band=band, after=sent)
    dbb_re = _diag_blocks(dbb_band[:S5_N], S5_STATE, S5_GROUP).reshape(S5_N, S5_GROUP)
    dbb_im = _diag_blocks(dbb_band[S5_N:], S5_STATE, S5_GROUP).reshape(S5_N, S5_GROUP)
    dc_re = _diag_blocks(dc_band[:S5_N], S5_STATE, S5_GROUP).transpose(0, 2, 1)
    dc_im = -_diag_blocks(dc_band[S5_N:], S5_STATE, S5_GROUP).transpose(0, 2, 1)
    da_re, da_im, dldt, db_re, db_im, dlogits = _params_bwd(
        a_re, a_im, ldt, b_re, b_im, sp["hg_lb_logits"],
        dlam[0].reshape(S5_N, 1), dlam[1].reshape(S5_N, 1), dbb_re, dbb_im, dlb)
    emit_small({"g_mix": dg_mix, "s5_a_re": da_re, "s5_a_im": da_im, "s5_log_dt": dldt, "s5_b_re": db_re,
                "s5_b_im": db_im, "s5_c_re": dc_re, "s5_c_im": dc_im, "s5_d": dd, "b_glu": db_glu,
                "hg_lb_logits": dlogits, "hg_norm_gain": dgain, "g_ffn": dg_ffn, "b_conv": db_conv,
                "g_final": dg_final, "loss": loss})
    return dx.reshape(nb, seq, D_MODEL)


def _mesh_peers():
    x, y, c = lax.axis_index("x"), lax.axis_index("y"), lax.axis_index("c")
    peers = []
    for k in range(1, N_DEV):
        px, py, pc = (1 - x if k & 4 else x), (1 - y if k & 2 else y), (1 - c if k & 1 else c)
        peers.append((k, (px, py, pc), 4 * px + 2 * py + pc))
    return 4 * x + 2 * y + c, peers


_HBM = pl.BlockSpec(memory_space=pltpu.HBM)
_SEM = pl.BlockSpec(memory_space=pltpu.SEMAPHORE)


def _exchange_start(name, operands, after):
    n = len(operands)
    me = 4 * lax.axis_index("x") + 2 * lax.axis_index("y") + lax.axis_index("c")
    flags = [per_peer for _, per_peer in operands]
    srcs, lands = [], []
    for arr, per_peer in operands:
        own = lax.dynamic_index_in_dim(arr, me, 0, keepdims=True) if per_peer else arr[None]
        land = lax.dynamic_update_slice_in_dim(lax.empty((N_DEV,) + own.shape[1:], arr.dtype), own, me, 0)
        srcs.append(pltpu.with_memory_space_constraint(arr, pltpu.HBM))
        lands.append(pltpu.with_memory_space_constraint(land, pltpu.HBM))
    copies = (N_DEV - 1) * n

    def body(*refs):
        src_refs, land_refs = refs[:n], refs[n:2 * n]
        send_sems, recv_sems = refs[2 * n + 1], refs[2 * n + 2]
        token = refs[-1]
        my_slab, peers = _mesh_peers()
        for k, peer, slab in peers:
            for i in range(n):
                s = (k - 1) * n + i
                pltpu.make_async_remote_copy(
                    src_ref=src_refs[i].at[slab] if flags[i] else src_refs[i], dst_ref=land_refs[i].at[my_slab],
                    send_sem=send_sems.at[s], recv_sem=recv_sems.at[s], device_id=peer,
                    device_id_type=pl.DeviceIdType.MESH).start()
        token[...] = jnp.zeros_like(token)

    outs = pl.pallas_call(
        body, name=name,
        out_shape=(pltpu.SemaphoreType.DMA((copies,)), pltpu.SemaphoreType.DMA((copies,)),
                   *[pltpu.HBM(a.shape, a.dtype) for a in srcs], *[pltpu.HBM(a.shape, a.dtype) for a in lands],
                   _sds((SUBLANES, 128))),
        in_specs=[_HBM] * (2 * n) + [pl.BlockSpec(memory_space=pl.ANY)],
        out_specs=(_SEM, _SEM, *[_HBM] * (2 * n), pl.BlockSpec(memory_space=pltpu.VMEM)),
        input_output_aliases={i: 2 + i for i in range(2 * n)},
        compiler_params=pltpu.CompilerParams(has_side_effects=pltpu.SideEffectType.DATAFLOW_SIDE_EFFECTING),
    )(*srcs, *lands, after)
    state = (flags, outs[0], outs[1], outs[2:2 + n], outs[2 + n:2 + 2 * n])
    return state, outs[-1]


def _exchange_wait(name, state, *after):
    flags, send_sems, recv_sems, srcs, lands = state
    n = len(flags)

    def body(*refs):
        src_refs, land_refs = refs[:n], refs[n:2 * n]
        send_ref, recv_ref = refs[2 * n], refs[2 * n + 1]
        _, peers = _mesh_peers()
        for k, peer, slab in peers:
            for i in range(n):
                s = (k - 1) * n + i
                copy = pltpu.make_async_remote_copy(
                    src_ref=src_refs[i].at[slab] if flags[i] else src_refs[i], dst_ref=land_refs[i].at[slab],
                    send_sem=send_ref.at[s], recv_sem=recv_ref.at[s], device_id=peer,
                    device_id_type=pl.DeviceIdType.MESH)
                copy.wait_send()
                copy.wait_recv()

    outs = pl.pallas_call(
        body, name=name,
        out_shape=(*[pltpu.HBM(a.shape, a.dtype) for a in srcs], *[pltpu.HBM(a.shape, a.dtype) for a in lands]),
        in_specs=[_HBM] * (2 * n) + [_SEM, _SEM] + [pl.BlockSpec(memory_space=pl.ANY)] * len(after),
        out_specs=tuple([_HBM] * (2 * n)),
        input_output_aliases={i: i for i in range(2 * n)},
        compiler_params=pltpu.CompilerParams(has_side_effects=pltpu.SideEffectType.DATAFLOW_SIDE_EFFECTING),
    )(*srcs, *lands, send_sems, recv_sems, *after)
    return list(outs[n:])


def _join_cols(parts, name, tr):
    _, r, c = parts.shape

    def body(p_ref, o_ref):
        for j in range(N_DEV):
            o_ref[:, j * c:(j + 1) * c] = p_ref[j]

    return _pcall(body, name, (r // tr,), [pl.BlockSpec((N_DEV, tr, c), lambda i: (0, i, 0))],
                  pl.BlockSpec((tr, N_DEV * c), lambda i: (i, 0)), _sds((r, N_DEV * c), parts.dtype))(parts)


def _split_cols(full, name, tr):
    r, c = full.shape[0], full.shape[1] // N_DEV

    def body(f_ref, o_ref):
        for j in range(N_DEV):
            o_ref[j] = f_ref[:, j * c:(j + 1) * c]

    return _pcall(body, name, (r // tr,), [pl.BlockSpec((tr, N_DEV * c), lambda i: (i, 0))],
                  pl.BlockSpec((N_DEV, tr, c), lambda i: (0, i, 0)), _sds((N_DEV, r, c), full.dtype))(full)


def _adamw(parts, w, m, v, name, tile):
    _, rows, cols = w.shape

    def body(p_ref, w_ref, m_ref, v_ref, g_out, d_out, m_out, v_out):
        g = p_ref[0].astype(F32)
        for k in range(1, N_DEV):
            g = g + p_ref[k].astype(F32)
        m1 = ADAM_B1 * m_ref[0] + (1.0 - ADAM_B1) * g
        v1 = ADAM_B2 * v_ref[0] + (1.0 - ADAM_B2) * (g * g)
        m_hat = m1 / (1.0 - ADAM_B1 ** ADAM_STEP)
        v_hat = v1 / (1.0 - ADAM_B2 ** ADAM_STEP)
        g_out[0] = g
        d_out[0] = -ADAM_LR * (m_hat / (jnp.sqrt(v_hat) + ADAM_EPS) + ADAM_WD * w_ref[0])
        m_out[0] = m1
        v_out[0] = v1

    row = pl.BlockSpec((1, tile, cols), lambda i: (0, i, 0))
    return _pcall(body, name, (rows // tile,),
                  [pl.BlockSpec((N_DEV, tile, cols), lambda i: (0, i, 0)), row, row, row],
                  [row, row, row, row], [_sds((1, rows, cols))] * 4)(parts, w, m, v)


BIG = {
    "w_in": ((D_MODEL, N_IN // N_DEV), True, 256),
    "w_glu": ((S5_WIDTH // N_DEV, S5_WIDTH), False, S5_WIDTH // N_DEV),
    "w_pa": ((S5_WIDTH, D_MODEL // N_DEV), True, S5_WIDTH),
    "w_pb": ((HG_WIDTH, D_MODEL // N_DEV), True, HG_WIDTH),
    "w_out": ((D_MODEL // N_DEV, D_MODEL), False, D_MODEL // N_DEV),
    "w_up": ((D_MODEL, 2 * D_FF // N_DEV), True, 256),
    "w_conv": ((CONV_W, 2 * D_FF // N_DEV), True, CONV_W),
    "w_down": ((D_FF // N_DEV, D_MODEL), False, D_FF // N_DEV // 2),
}
UNALIGNED_COLS = ("w_in", "w_up", "w_conv")


def _join_shards(n, parts):
    (a, b), by_cols, _ = BIG[n]
    if not by_cols:
        return parts.reshape(N_DEV * a, b)
    if n in UNALIGNED_COLS:
        return _join_cols(parts, "join_" + n, min(a, 256))
    return parts.transpose(1, 0, 2).reshape(a, N_DEV * b)


def _split_shards(n, full):
    (a, b), by_cols, _ = BIG[n]
    if not by_cols:
        return full.reshape(N_DEV, a, b)
    if n in UNALIGNED_COLS:
        return _split_cols(full, "split_" + n, min(a, 256))
    return full.reshape(a, N_DEV, b).transpose(1, 0, 2)


PACKED = SMALL + (("loss", (1,)),)


def _pack_small(d):
    flat = jnp.concatenate([d[n].reshape(-1) for n, _ in PACKED])
    return jnp.pad(flat, (0, SMALL_ROWS * PACK_W - flat.shape[0])).reshape(SMALL_ROWS, PACK_W)


def _unpack_small(p):
    flat = p.reshape(-1)
    out, off = {}, 0
    for n, shp in PACKED:
        size = math.prod(shp)
        out[n] = flat[off:off + size].reshape(shp)
        off += size
    return out


def kernel(x, g_mix, w_in, s5_a_re, s5_a_im, s5_log_dt, s5_b_re, s5_b_im, s5_c_re, s5_c_im, s5_d, w_glu, b_glu, hg_lb_logits, hg_norm_gain, w_pa, w_pb, w_out, g_ffn, w_up, w_conv, b_conv, w_down, g_final, loss_target, m_g_mix, m_w_in, m_s5_a_re, m_s5_a_im, m_s5_log_dt, m_s5_b_re, m_s5_b_im, m_s5_c_re, m_s5_c_im, m_s5_d, m_w_glu, m_b_glu, m_hg_lb_logits, m_hg_norm_gain, m_w_pa, m_w_pb, m_w_out, m_g_ffn, m_w_up, m_w_conv, m_b_conv, m_w_down, m_g_final, v_g_mix, v_w_in, v_s5_a_re, v_s5_a_im, v_s5_log_dt, v_s5_b_re, v_s5_b_im, v_s5_c_re, v_s5_c_im, v_s5_d, v_w_glu, v_b_glu, v_hg_lb_logits, v_hg_norm_gain, v_w_pa, v_w_pb, v_w_out, v_g_ffn, v_w_up, v_w_conv, v_b_conv, v_w_down, v_g_final):
    given = dict(locals())
    small_names = [n for n, _ in SMALL]

    pay = {n: given[n][0] if n == "w_conv" else given[n][0].astype(BF16) for n in BIG}
    groups = {"in": ["w_in"], "mix": ["w_glu", "w_pa", "w_pb", "w_out"], "ffn": ["w_up", "w_down", "w_conv"]}
    gathers, order = {}, pay["w_in"]
    for grp, names in groups.items():
        gathers[grp], order = _exchange_start("gather_" + grp + "_start", [(pay[n], False) for n in names], order)

    zero = jnp.zeros((1,), F32)
    packed = [_pack_small({**{k: given[pre + k] for k in small_names}, "loss": zero})[None] for pre in ("", "m_", "v_")]

    def weights(grp, *after):
        if grp == "in":
            after = (*after, order, *packed)
        got = _exchange_wait("gather_" + grp + "_wait", gathers[grp], *after)
        return {n: _join_shards(n, g) for n, g in zip(groups[grp], got)}

    in_flight, started = [], []

    def emit(grads):
        names = list(grads)
        state, token = _exchange_start("grads_" + names[0] + "_start",
                                       [(_split_shards(n, grads[n]), True) for n in names], grads[names[0]])
        in_flight.append((names, state))
        return token

    def emit_small(grads):
        pack = _pack_small(grads)
        state, token = _exchange_start("grads_small_start", [(pack, False)], pack)
        in_flight.append((["small"], state))
        started.append(token)

    sp = {n: (given[n] if n in ("g_final", "hg_lb_logits") else given[n][0]) for n in small_names}
    sp["g_mix"] = _after(sp["g_mix"], order)
    dx = _local_step(x, loss_target, weights, sp, emit, emit_small)

    res = {}
    after = [started[-1]]
    in_flight.insert(-1, in_flight.pop())
    for names, state in in_flight:
        parts = _exchange_wait("grads_" + names[0] + "_wait", state, *after)
        if names != ["small"]:
            for n, part in zip(names, parts):
                res[n] = _adamw(part, given[n], given["m_" + n], given["v_" + n], "adamw_" + n, BIG[n][2])
            after = [res[n][0] for n in names]
            continue
        rs4 = _adamw(parts[0], *packed, "adamw_small", SMALL_ROWS)
        small4 = [_unpack_small(r) for r in rs4]
        for k in small_names:
            res[k] = [us[k] for us in small4]
        total_loss = small4[0]["loss"][0]
        after = [res[k][j] for j, k in enumerate(("s5_b_re", "s5_b_im", "s5_c_re", "s5_c_im"))]
    return (total_loss, dx, *[res[n][0] for n in WEIGHT_ORDER], *[res[n][1] for n in WEIGHT_ORDER],
            *[res[n][2] for n in WEIGHT_ORDER], *[res[n][3] for n in WEIGHT_ORDER])
```

```python
import functools
import math

import jax
import jax.numpy as jnp
from jax import lax
from jax.experimental import pallas as pl
from jax.experimental.pallas import tpu as pltpu

F32 = jnp.float32
BF16 = jnp.bfloat16

D_MODEL = 1024
S5_WIDTH = 512
S5_GROUP = 16
S5_GROUPS = 32
S5_STATE = 64
S5_N = S5_GROUPS * S5_STATE
HG_WIDTH = 512
HG_HEAD = 128
HG_HEADS = 4
D_FF = 2816
CONV_W = 3
CHUNK = 64
N_IN = S5_WIDTH + 4 * HG_WIDTH + 2 * D_MODEL
EPS = 1e-6
QSCALE = HG_HEAD ** -0.5

ADAM_LR = 0.001
ADAM_B1 = 0.9
ADAM_B2 = 0.999
ADAM_EPS = 1e-08
ADAM_WD = 0.01
ADAM_STEP = 10

N_DEV = 8
V7X_VMEM_BYTES = 64 * 1024 * 1024
VMEM_LIMIT = V7X_VMEM_BYTES * 7 // 8
SUBLANES = 8
PACK_W = 1024

SMALL = (
    ("g_mix", (1, D_MODEL)),
    ("s5_a_re", (1, S5_GROUPS, S5_STATE)),
    ("s5_a_im", (1, S5_GROUPS, S5_STATE)),
    ("s5_log_dt", (1, S5_GROUPS)),
    ("s5_b_re", (1, S5_GROUPS, S5_STATE, S5_GROUP)),
    ("s5_b_im", (1, S5_GROUPS, S5_STATE, S5_GROUP)),
    ("s5_c_re", (1, S5_GROUPS, S5_GROUP, S5_STATE)),
    ("s5_c_im", (1, S5_GROUPS, S5_GROUP, S5_STATE)),
    ("s5_d", (1, S5_WIDTH)),
    ("b_glu", (1, S5_WIDTH)),
    ("hg_lb_logits", (2, HG_WIDTH)),
    ("hg_norm_gain", (1, HG_WIDTH)),
    ("g_ffn", (1, D_MODEL)),
    ("b_conv", (1, 2 * D_FF)),
    ("g_final", (D_MODEL,)),
)
SMALL_ROWS = 144
WEIGHT_ORDER = ("g_mix", "w_in", "s5_a_re", "s5_a_im", "s5_log_dt", "s5_b_re", "s5_b_im", "s5_c_re", "s5_c_im",
                "s5_d", "w_glu", "b_glu", "hg_lb_logits", "hg_norm_gain", "w_pa", "w_pb", "w_out", "g_ffn",
                "w_up", "w_conv", "b_conv", "w_down", "g_final")


def _pcall(body, name, grid, in_specs, out_specs, out_shape, scratch=()):
    return pl.pallas_call(
        body, name=name, grid=grid, in_specs=in_specs, out_specs=out_specs, out_shape=out_shape,
        scratch_shapes=list(scratch),
        compiler_params=pltpu.CompilerParams(dimension_semantics=("arbitrary",) * len(grid),
                                             vmem_limit_bytes=VMEM_LIMIT),
    )


def _full(shape):
    return pl.BlockSpec(shape, lambda *_: (0,) * len(shape))


def _sds(shape, dtype=F32):
    return jax.ShapeDtypeStruct(shape, dtype)


def _dot(a, b):
    return jnp.dot(a.astype(BF16), b.astype(BF16), preferred_element_type=F32)


def _dot_nt(a, b):
    return lax.dot_general(a.astype(BF16), b.astype(BF16), (((1,), (1,)), ((), ())), preferred_element_type=F32)


def _dot_tn(a, b):
    return lax.dot_general(a.astype(BF16), b.astype(BF16), (((0,), (0,)), ((), ())), preferred_element_type=F32)


def _hdot(a, b):
    return jnp.dot(a, b, preferred_element_type=F32, precision=lax.Precision.HIGHEST)


def _hdot_tn(a, b):
    return lax.dot_general(a, b, (((0,), (0,)), ((), ())), preferred_element_type=F32,
                           precision=lax.Precision.HIGHEST)


def _sigmoid(x):
    return jax.nn.sigmoid(x)


GELU_C = math.sqrt(2.0 / math.pi)
GELU_A = 0.044715


def _gelu(x):
    return 0.5 * x * (1.0 + jnp.tanh(GELU_C * (x + GELU_A * (x * x * x))))


def _gelu_grad(x):
    t = jnp.tanh(GELU_C * (x + GELU_A * (x * x * x)))
    return 0.5 * (1.0 + t) + 0.5 * x * (1.0 - t * t) * (GELU_C * (1.0 + 3.0 * GELU_A * x * x))


def _cumsum_rows(v, reverse=False):
    n = v.shape[0]
    row = lax.broadcasted_iota(jnp.int32, v.shape, 0)
    s = 1
    while s < n:
        if reverse:
            v = v + jnp.where(row < n - s, pltpu.roll(v, n - s, axis=0), 0.0)
        else:
            v = v + jnp.where(row >= s, pltpu.roll(v, s, axis=0), 0.0)
        s *= 2
    return v


def _token_tile(seq):
    return min(256, seq)


def _s5_disc(a_re, a_im, ldt, b_re, b_im):
    dt = jnp.exp(ldt)
    mag = jnp.exp(a_re * dt)
    ang = a_im * dt
    lb_re = mag * jnp.cos(ang)
    lb_im = mag * jnp.sin(ang)
    den = a_re * a_re + a_im * a_im
    n_re = lb_re - 1.0
    n_im = lb_im
    co_re = (n_re * a_re + n_im * a_im) / den
    co_im = (n_im * a_re - n_re * a_im) / den
    bb_re = co_re * b_re - co_im * b_im
    bb_im = co_re * b_im + co_im * b_re
    return lb_re, lb_im, bb_re, bb_im


def _params_fwd(a_re, a_im, ldt, b_re, b_im, logits):
    def body(are, aim, ld, bre, bim, lg, lr_o, li_o, bbr_o, bbi_o, lb_o):
        lr, li, bbr, bbi = _s5_disc(are[...], aim[...], ld[...], bre[...], bim[...])
        lr_o[...] = lr
        li_o[...] = li
        bbr_o[...] = bbr
        bbi_o[...] = bbi
        lb_o[...] = _sigmoid(lg[0:1, :] - lg[1:2, :])

    col, mat = (S5_N, 1), (S5_N, S5_GROUP)
    return _pcall(body, "params_fwd", (1,),
                  [_full(col), _full(col), _full(col), _full(mat), _full(mat), _full((2, HG_WIDTH))],
                  [_full(col), _full(col), _full(mat), _full(mat), _full((1, HG_WIDTH))],
                  [_sds(col), _sds(col), _sds(mat), _sds(mat), _sds((1, HG_WIDTH))])(a_re, a_im, ldt, b_re, b_im, logits)


def _params_bwd(a_re, a_im, ldt, b_re, b_im, logits, dlr, dli, dbbr, dbbi, dlb):
    def body(are, aim, ld, bre, bim, lg, dlr_r, dli_r, dbbr_r, dbbi_r, dlb_r,
             dare_o, daim_o, dld_o, dbre_o, dbim_o, dlg_o):
        _, vjp = jax.vjp(_s5_disc, are[...], aim[...], ld[...], bre[...], bim[...])
        dare, daim, dld, dbre, dbim = vjp((dlr_r[...], dli_r[...], dbbr_r[...], dbbi_r[...]))
        dare_o[...] = dare
        daim_o[...] = daim
        dbre_o[...] = dbre
        dbim_o[...] = dbim
        for g in range(S5_GROUPS):
            dld_o[g:g + 1, :] = jnp.sum(dld[g * S5_STATE:(g + 1) * S5_STATE, :], axis=0, keepdims=True)
        lb = _sigmoid(lg[0:1, :] - lg[1:2, :])
        d0 = dlb_r[...] * lb * (1.0 - lb)
        dlg_o[0:1, :] = d0
        dlg_o[1:2, :] = -d0

    col, mat = (S5_N, 1), (S5_N, S5_GROUP)
    return _pcall(body, "params_bwd", (1,),
                  [_full(col), _full(col), _full(col), _full(mat), _full(mat), _full((2, HG_WIDTH)),
                   _full(col), _full(col), _full(mat), _full(mat), _full((1, HG_WIDTH))],
                  [_full(col), _full(col), _full((S5_GROUPS, 1)), _full(mat), _full(mat), _full((2, HG_WIDTH))],
                  [_sds(col), _sds(col), _sds((S5_GROUPS, 1)), _sds(mat), _sds(mat), _sds((2, HG_WIDTH))],
                  )(a_re, a_im, ldt, b_re, b_im, logits, dlr, dli, dbbr, dbbi, dlb)


def _band_blocks(m):
    g, r, c = m.shape
    gb = g // S5_BANDS
    m4 = m.astype(BF16).reshape(S5_BANDS, gb, r, c)
    on_diag = jnp.eye(gb, dtype=bool)[None, :, None, :, None]
    return jnp.where(on_diag, m4[:, :, :, None, :], 0).reshape(S5_BANDS, gb * r, gb * c)


def _diag_blocks(band, r, c):
    g, nb = band.shape[0] // r, band.shape[1] // c
    on_diag = (jnp.arange(g) % nb)[:, None, None, None] == jnp.arange(nb)[None, None, :, None]
    return jnp.sum(jnp.where(on_diag, band.reshape(g, r, nb, c), 0.0), axis=2)


def _in_proj(x, g_mix, w_in, tm):
    t = x.shape[0]

    def body(x_ref, g_ref, w_ref, u_ref, za_ref, zh_ref, zg_ref):
        xv = x_ref[...]
        r = lax.rsqrt(jnp.mean(xv * xv, axis=-1, keepdims=True) + EPS)
        u = (xv * r * g_ref[...]).astype(BF16)
        u_ref[...] = u
        za_ref[...] = jnp.dot(u, w_ref[:, 0:S5_WIDTH], preferred_element_type=F32)
        zh_ref[...] = jnp.dot(u, w_ref[:, S5_WIDTH:S5_WIDTH + 4 * HG_WIDTH], preferred_element_type=F32)
        zg_ref[...] = jnp.dot(u, w_ref[:, S5_WIDTH + 4 * HG_WIDTH:], preferred_element_type=F32).astype(BF16)

    row = lambda w: pl.BlockSpec((tm, w), lambda i: (i, 0))
    return _pcall(body, "in_proj", (t // tm,),
                  [row(D_MODEL), _full((1, D_MODEL)), _full((D_MODEL, N_IN))],
                  [row(D_MODEL), row(S5_WIDTH), row(4 * HG_WIDTH), row(2 * D_MODEL)],
                  [_sds((t, D_MODEL), BF16), _sds((t, S5_WIDTH)), _sds((t, 4 * HG_WIDTH)),
                   _sds((t, 2 * D_MODEL), BF16)],
                  )(x, g_mix, w_in)


S5_LANES = 512
S5_BANDS = 4


def _band(q):
    return (slice(q * S5_WIDTH // S5_BANDS, (q + 1) * S5_WIDTH // S5_BANDS),
            slice(q * S5_N // S5_BANDS, (q + 1) * S5_N // S5_BANDS))


def _im(st):
    return slice(S5_N + st.start, S5_N + st.stop)


SCAN_UNROLL = 8


def _complex_scan(buf_ref, lam_ref, st_ref, ts, reverse):
    chunks = [slice(cc * S5_LANES, (cc + 1) * S5_LANES) for cc in range(S5_N // S5_LANES)]
    nch = len(chunks)
    wr = [lam_ref[0:1, re] for re in chunks]
    wi = [-lam_ref[1:2, re] if reverse else lam_ref[1:2, re] for re in chunks]

    def block(ib, carry):
        vr, vi = list(carry[:nch]), list(carry[nch:])
        first = ts - SCAN_UNROLL - ib * SCAN_UNROLL if reverse else ib * SCAN_UNROLL
        first = pl.multiple_of(first, SCAN_UNROLL)
        for k in range(SCAN_UNROLL):
            row = pl.ds(first + (SCAN_UNROLL - 1 - k if reverse else k), 1)
            for cc, re in enumerate(chunks):
                nr = wr[cc] * vr[cc] - wi[cc] * vi[cc] + buf_ref[row, re]
                ni = wr[cc] * vi[cc] + wi[cc] * vr[cc] + buf_ref[row, _im(re)]
                buf_ref[row, re] = nr
                buf_ref[row, _im(re)] = ni
                vr[cc], vi[cc] = nr, ni
        return tuple(vr + vi)

    init = tuple(st_ref[0:1, re] for re in chunks) + tuple(st_ref[1:2, re] for re in chunks)
    last = lax.fori_loop(0, ts // SCAN_UNROLL, block, init)
    for cc, re in enumerate(chunks):
        st_ref[0:1, re] = last[cc]
        st_ref[1:2, re] = last[nch + cc]


BAND_CH = S5_WIDTH // S5_BANDS
BAND_ST = S5_N // S5_BANDS


def _s5_fwd(za, b_bands, lam, c_bands, dskip, nb, seq, ts):
    t = za.shape[0]
    nts = seq // ts

    def body(za_ref, br_ref, bi_ref, lam_ref, cr_ref, ci_ref, d_ref, xs_ref, y_ref, buf_ref, st_ref):
        @pl.when(pl.program_id(1) == 0)
        def _():
            st_ref[...] = jnp.zeros_like(st_ref)

        zav = za_ref[...]
        for q in range(S5_BANDS):
            ch, st = _band(q)
            buf_ref[:, st] = _dot(zav[:, ch], br_ref[q])
            buf_ref[:, _im(st)] = _dot(zav[:, ch], bi_ref[q])
        _complex_scan(buf_ref, lam_ref, st_ref, ts, reverse=False)
        xs_ref[...] = buf_ref[...].astype(BF16)
        for q in range(S5_BANDS):
            ch, st = _band(q)
            y_ref[:, ch] = (_dot(xs_ref[:, st], cr_ref[q]) + _dot(xs_ref[:, _im(st)], ci_ref[q])
                            + d_ref[:, ch] * zav[:, ch])

    tok = lambda w: pl.BlockSpec((ts, w), lambda b, j: (b * nts + j, 0))
    to_st, to_ch = _full((S5_BANDS, BAND_CH, BAND_ST)), _full((S5_BANDS, BAND_ST, BAND_CH))
    return _pcall(body, "s5_fwd", (nb, nts),
                  [tok(S5_WIDTH), to_st, to_st, _full((2, S5_N)), to_ch, to_ch, _full((1, S5_WIDTH))],
                  [tok(2 * S5_N), tok(S5_WIDTH)],
                  [_sds((t, 2 * S5_N), BF16), _sds((t, S5_WIDTH))],
                  scratch=[pltpu.VMEM((ts, 2 * S5_N), F32), pltpu.VMEM((2, S5_N), F32)],
                  )(za, *b_bands, lam, *c_bands, dskip)


def _hgrn_gates(zq, zf, lbh):
    sf = _sigmoid(zf)
    f = lbh + (1.0 - lbh) * sf
    sq = _sigmoid(zq)
    qa = zq * sq * QSCALE
    bc = _cumsum_rows(jnp.log(f))
    bm = bc[CHUNK // 2 - 1:CHUNK // 2, :]
    bl = bc[CHUNK - 1:CHUNK, :]
    return sf, f, sq, qa, bc, bm, bl


def _hgrn_fwd(zh, lb, nb, seq):
    nc = seq // CHUNK

    def body(zh_ref, lb_ref, o_ref, sts_ref, st_ref):
        @pl.when(pl.program_id(0) == 0)
        def _():
            st_ref[...] = jnp.zeros_like(st_ref)

        causal = (lax.broadcasted_iota(jnp.int32, (CHUNK, CHUNK), 0)
                  >= lax.broadcasted_iota(jnp.int32, (CHUNK, CHUNK), 1))
        for b in range(nb):
            for h in range(HG_HEADS):
                hs = slice(h * HG_HEAD, (h + 1) * HG_HEAD)
                zq = zh_ref[b, :, h * HG_HEAD:(h + 1) * HG_HEAD]
                zf = zh_ref[b, :, HG_WIDTH + h * HG_HEAD:HG_WIDTH + (h + 1) * HG_HEAD]
                zi = zh_ref[b, :, 2 * HG_WIDTH + h * HG_HEAD:2 * HG_WIDTH + (h + 1) * HG_HEAD]
                _, f, _, qa, bc, bm, bl = _hgrn_gates(zq, zf, lb_ref[:, hs])
                k = 1.0 - f
                qt = qa * jnp.exp(bc - bm)
                kt = k * jnp.exp(bm - bc)
                qb = qa * jnp.exp(bc)
                kd = k * jnp.exp(bl - bc)
                st = st_ref[b, h]
                sts_ref[b, 0, h] = st
                a = jnp.where(causal, _dot_nt(qt, kt), 0.0)
                o_ref[b, :, hs] = _dot(a, zi) + _dot_nt(qb, st)
                st_ref[b, h] = st * jnp.exp(bl) + _dot_tn(zi, kd)

    return _pcall(body, "hgrn_fwd", (nc,),
                  [pl.BlockSpec((nb, CHUNK, 4 * HG_WIDTH), lambda c: (0, c, 0)), _full((1, HG_WIDTH))],
                  [pl.BlockSpec((nb, CHUNK, HG_WIDTH), lambda c: (0, c, 0)),
                   pl.BlockSpec((nb, 1, HG_HEADS, HG_HEAD, HG_HEAD), lambda c: (0, c, 0, 0, 0))],
                  [_sds((nb, seq, HG_WIDTH)), _sds((nb, nc, HG_HEADS, HG_HEAD, HG_HEAD))],
                  scratch=[pltpu.VMEM((nb, HG_HEADS, HG_HEAD, HG_HEAD), F32)])(zh, lb)


def _head_rms(o):
    parts = []
    for h in range(HG_HEADS):
        oh = o[:, h * HG_HEAD:(h + 1) * HG_HEAD]
        r = lax.rsqrt(jnp.mean(oh * oh, axis=-1, keepdims=True) + EPS)
        parts.append(jnp.broadcast_to(r, oh.shape))
    return jnp.concatenate(parts, axis=1)


def _head_mean(v):
    parts = []
    for h in range(HG_HEADS):
        vh = v[:, h * HG_HEAD:(h + 1) * HG_HEAD]
        parts.append(jnp.broadcast_to(jnp.mean(vh, axis=-1, keepdims=True), vh.shape))
    return jnp.concatenate(parts, axis=1)


def _mix_fwd(x, y0, o, zh, zgt, w_glu, b_glu, gain, w_pa, w_pb, w_out, g_ffn, tm):
    t = x.shape[0]

    def body(x_ref, y0_ref, o_ref, zg_ref, zgt_ref, wglu_ref, bglu_ref, gain_ref, wpa_ref, wpb_ref, wout_ref,
             gffn_ref, x1_ref, u2_ref, pa_ref, pb_ref, ya2_ref, yb_ref):
        ya1 = _gelu(y0_ref[...])
        s = _sigmoid(_dot(ya1, wglu_ref[...]) + bglu_ref[...])
        ya2 = (ya1 * s).astype(BF16)
        ov = o_ref[...]
        zg = zg_ref[...]
        yb = (ov * _head_rms(ov) * gain_ref[...] * (zg * _sigmoid(zg))).astype(BF16)
        ya2_ref[...] = ya2
        yb_ref[...] = yb
        pa = jnp.dot(ya2, wpa_ref[...], preferred_element_type=F32)
        pb = jnp.dot(yb, wpb_ref[...], preferred_element_type=F32)
        pa_ref[...] = pa.astype(BF16)
        pb_ref[...] = pb.astype(BF16)
        m = (_sigmoid(zgt_ref[:, 0:D_MODEL].astype(F32)) * pa
             + _sigmoid(zgt_ref[:, D_MODEL:].astype(F32)) * pb)
        x1 = x_ref[...] + _dot(m, wout_ref[...])
        x1_ref[...] = x1
        r = lax.rsqrt(jnp.mean(x1 * x1, axis=-1, keepdims=True) + EPS)
        u2_ref[...] = (x1 * r * gffn_ref[...]).astype(BF16)

    row = lambda w: pl.BlockSpec((tm, w), lambda i: (i, 0))
    return _pcall(body, "mix_fwd", (t // tm,),
                  [row(D_MODEL), row(S5_WIDTH), row(HG_WIDTH), pl.BlockSpec((tm, HG_WIDTH), lambda i: (i, 3)),
                   row(2 * D_MODEL), _full((S5_WIDTH, S5_WIDTH)), _full((1, S5_WIDTH)), _full((1, HG_WIDTH)),
                   _full((S5_WIDTH, D_MODEL)), _full((HG_WIDTH, D_MODEL)), _full((D_MODEL, D_MODEL)),
                   _full((1, D_MODEL))],
                  [row(D_MODEL), row(D_MODEL), row(D_MODEL), row(D_MODEL), row(S5_WIDTH), row(HG_WIDTH)],
                  [_sds((t, D_MODEL)), _sds((t, D_MODEL), BF16), _sds((t, D_MODEL), BF16), _sds((t, D_MODEL), BF16),
                   _sds((t, S5_WIDTH), BF16), _sds((t, HG_WIDTH), BF16)],
                  )(x, y0, o, zh, zgt, w_glu, b_glu, gain, w_pa, w_pb, w_out, g_ffn)


FF_COLS = 256
FF_UP_TILE = 1408


def _ffn_up(u2, w_up, tm):
    t = u2.shape[0]
    n = 2 * D_FF

    def body(u_ref, w_ref, h_ref):
        h_ref[...] = jnp.dot(u_ref[...], w_ref[...], preferred_element_type=F32).astype(BF16)

    return _pcall(body, "ffn_up", (n // FF_UP_TILE, t // tm),
                  [pl.BlockSpec((tm, D_MODEL), lambda j, i: (i, 0)),
                   pl.BlockSpec((D_MODEL, FF_UP_TILE), lambda j, i: (0, j))],
                  pl.BlockSpec((tm, FF_UP_TILE), lambda j, i: (i, j)),
                  _sds((t, n), BF16))(u2, w_up)


HALO = 16


def _conv_cols(h_ref, halo_ref, valid, wc_ref, bc_ref, c0):
    cs = slice(c0, c0 + FF_COLS)
    cur = h_ref[:, cs].astype(F32)
    prev = jnp.where(valid, halo_ref[:, cs].astype(F32), 0.0)
    full = jnp.concatenate([prev, cur], axis=0)
    h1 = pltpu.roll(full, 1, axis=0)[HALO:]
    h2 = pltpu.roll(full, 2, axis=0)[HALO:]
    return h2 * wc_ref[0:1, cs] + h1 * wc_ref[1:2, cs] + cur * wc_ref[2:3, cs] + bc_ref[:, cs]


def _ffn_down_loss(h, x1, tgt, w_conv, b_conv, w_down, g_final, seq, tm):
    t = h.shape[0]
    tps = seq // tm
    n = 2 * D_FF

    def body(h_ref, halo_ref, x1_ref, tgt_ref, wc_ref, bc_ref, wd_ref, gf_ref,
             hc_ref, a_ref, dx2_ref, dx2b_ref, loss_ref, dgf_ref):
        i = pl.program_id(0)

        @pl.when(i == 0)
        def _():
            loss_ref[...] = jnp.zeros_like(loss_ref)
            dgf_ref[...] = jnp.zeros_like(dgf_ref)

        valid = (i % tps) != 0
        x2 = x1_ref[...]
        for j in range(D_FF // FF_COLS):
            gate = _conv_cols(h_ref, halo_ref, valid, wc_ref, bc_ref, j * FF_COLS)
            val = _conv_cols(h_ref, halo_ref, valid, wc_ref, bc_ref, D_FF + j * FF_COLS)
            hc_ref[:, j * FF_COLS:(j + 1) * FF_COLS] = gate.astype(BF16)
            hc_ref[:, D_FF + j * FF_COLS:D_FF + (j + 1) * FF_COLS] = val.astype(BF16)
            a = (gate * _sigmoid(gate) * val).astype(BF16)
            a_ref[:, j * FF_COLS:(j + 1) * FF_COLS] = a
            x2 = x2 + jnp.dot(a, wd_ref[j * FF_COLS:(j + 1) * FF_COLS, :], preferred_element_type=F32)
        r = lax.rsqrt(jnp.mean(x2 * x2, axis=-1, keepdims=True) + EPS)
        xn = x2 * r
        g = gf_ref[...]
        e = xn * g - tgt_ref[...]
        loss_ref[...] += (0.5 / D_MODEL) * jnp.sum(e * e).reshape(1, 1)
        dy = e * (1.0 / D_MODEL)
        dgf_ref[...] += jnp.sum(dy * xn, axis=0, keepdims=True)
        dxn = dy * g
        dx2 = r * (dxn - xn * jnp.mean(dxn * xn, axis=-1, keepdims=True))
        dx2_ref[...] = dx2
        dx2b_ref[...] = dx2.astype(BF16)

    row = lambda w: pl.BlockSpec((tm, w), lambda i: (i, 0))
    halo = pl.BlockSpec((HALO, n), lambda i: (jnp.maximum(i * (tm // HALO) - 1, 0), 0))
    return _pcall(body, "ffn_down_loss", (t // tm,),
                  [row(n), halo, row(D_MODEL), row(D_MODEL), _full((CONV_W, n)), _full((1, n)),
                   _full((D_FF, D_MODEL)), _full((1, D_MODEL))],
                  [row(n), row(D_FF), row(D_MODEL), row(D_MODEL), _full((1, 1)), _full((1, D_MODEL))],
                  [_sds((t, n), BF16), _sds((t, D_FF), BF16), _sds((t, D_MODEL)), _sds((t, D_MODEL), BF16),
                   _sds((1, 1)), _sds((1, D_MODEL))],
                  )(h, h, x1, tgt, w_conv, b_conv, w_down, g_final)


def _wgrad(a, b, name, tn, out_dtype=F32, transpose_out=False, band=None, after=None):
    t, m = a.shape
    n = b.shape[1] if band is None else band
    nbands = 1 if band is None else b.shape[1] // band
    after = b if after is None else after

    def body(a_ref, b_ref, after_ref, o_ref):
        r = _dot_tn(a_ref[...], b_ref[...])
        o_ref[...] = (r.T if transpose_out else r).astype(out_dtype)

    if transpose_out:
        out_spec, out_shape = pl.BlockSpec((n, tn), lambda i: (0, i)), _sds((n, m), out_dtype)
    else:
        out_spec, out_shape = pl.BlockSpec((tn, n), lambda i: (i, 0)), _sds((m, n), out_dtype)
    return _pcall(body, name, (m // tn,),
                  [pl.BlockSpec((t, tn), lambda i: (0, i)), pl.BlockSpec((t, n), lambda i: (0, i % nbands)),
                   pl.BlockSpec(memory_space=pl.ANY)],
                  out_spec, out_shape)(a, b, after)


def _ffn_bwd_act(dx2b, hc, w_down, tm):
    t = hc.shape[0]
    n = 2 * D_FF

    def body(dx2_ref, hc_ref, wd_ref, dhc_ref, dbc_ref):
        @pl.when(pl.program_id(0) == 0)
        def _():
            dbc_ref[...] = jnp.zeros_like(dbc_ref)

        dx2 = dx2_ref[...]
        for j in range(D_FF // FF_COLS):
            gs = slice(j * FF_COLS, (j + 1) * FF_COLS)
            vs = slice(D_FF + j * FF_COLS, D_FF + (j + 1) * FF_COLS)
            gate = hc_ref[:, gs].astype(F32)
            val = hc_ref[:, vs].astype(F32)
            da = _dot_nt(dx2, wd_ref[gs, :])
            sg = _sigmoid(gate)
            dgate = da * val * (sg * (1.0 + gate * (1.0 - sg)))
            dval = da * (gate * sg)
            dhc_ref[:, gs] = dgate.astype(BF16)
            dhc_ref[:, vs] = dval.astype(BF16)
            dbc_ref[:, gs] += jnp.sum(dgate, axis=0, keepdims=True)
            dbc_ref[:, vs] += jnp.sum(dval, axis=0, keepdims=True)

    row = lambda w: pl.BlockSpec((tm, w), lambda i: (i, 0))
    return _pcall(body, "ffn_bwd_act", (t // tm,),
                  [row(D_MODEL), row(n), _full((D_FF, D_MODEL))],
                  [row(n), _full((1, n))],
                  [_sds((t, n), BF16), _sds((1, n))],
                  )(dx2b, hc, w_down)


def _ffn_bwd_up(dhc, h, dx2, x1, w_conv, w_up, g_ffn, seq, tm):
    t = dhc.shape[0]
    tps = seq // tm
    n = 2 * D_FF
    last = t // HALO - 1

    def body(dhc_ref, halo_ref, h_ref, dx2_ref, x1_ref, wc_ref, wu_ref, gf_ref,
             dh_ref, dx1_ref, dx1b_ref, dgf_ref, dwc_ref):
        i = pl.program_id(0)

        @pl.when(i == 0)
        def _():
            dgf_ref[...] = jnp.zeros_like(dgf_ref)
            dwc_ref[...] = jnp.zeros_like(dwc_ref)

        valid = ((i + 1) % tps) != 0
        du2 = jnp.zeros((tm, D_MODEL), F32)
        for j in range(n // FF_COLS):
            cs = slice(j * FF_COLS, (j + 1) * FF_COLS)
            cur = dhc_ref[:, cs].astype(F32)
            nxt = jnp.where(valid, halo_ref[:, cs].astype(F32), 0.0)
            full = jnp.concatenate([cur, nxt], axis=0)
            d1 = pltpu.roll(full, tm + HALO - 1, axis=0)[:tm]
            d2 = pltpu.roll(full, tm + HALO - 2, axis=0)[:tm]
            dh = (cur * wc_ref[2:3, cs] + d1 * wc_ref[1:2, cs] + d2 * wc_ref[0:1, cs]).astype(BF16)
            dh_ref[:, cs] = dh
            du2 = du2 + _dot_nt(dh, wu_ref[:, cs])
            hv = h_ref[:, cs].astype(F32)
            dwc_ref[0:1, cs] += jnp.sum(hv * d2, axis=0, keepdims=True)
            dwc_ref[1:2, cs] += jnp.sum(hv * d1, axis=0, keepdims=True)
            dwc_ref[2:3, cs] += jnp.sum(hv * cur, axis=0, keepdims=True)
        x1 = x1_ref[...]
        r = lax.rsqrt(jnp.mean(x1 * x1, axis=-1, keepdims=True) + EPS)
        xn = x1 * r
        dgf_ref[...] += jnp.sum(du2 * xn, axis=0, keepdims=True)
        dxn = du2 * gf_ref[...]
        dx1 = dx2_ref[...] + r * (dxn - xn * jnp.mean(dxn * xn, axis=-1, keepdims=True))
        dx1_ref[...] = dx1
        dx1b_ref[...] = dx1.astype(BF16)

    row = lambda w: pl.BlockSpec((tm, w), lambda i: (i, 0))
    halo = pl.BlockSpec((HALO, n), lambda i: (jnp.minimum((i + 1) * (tm // HALO), last), 0))
    return _pcall(body, "ffn_bwd_up", (t // tm,),
                  [row(n), halo, row(n), row(D_MODEL), row(D_MODEL), _full((CONV_W, n)), _full((D_MODEL, n)),
                   _full((1, D_MODEL))],
                  [row(n), row(D_MODEL), row(D_MODEL), _full((1, D_MODEL)), _full((CONV_W, n))],
                  [_sds((t, n), BF16), _sds((t, D_MODEL)), _sds((t, D_MODEL), BF16), _sds((1, D_MODEL)),
                   _sds((CONV_W, n))],
                  )(dhc, dhc, h, dx2, x1, w_conv, w_up, g_ffn)


def _mix_bwd(dx1, y0, o, zh, zgt, pa, pb, w_glu, b_glu, gain, w_pa, w_pb, w_out, tm):
    t = dx1.shape[0]

    def body(dx1_ref, y0_ref, o_ref, zg_ref, zgt_ref, pa_ref, pb_ref, wglu_ref, bglu_ref, gain_ref, wpa_ref,
             wpb_ref, wout_ref,
             dy0_ref, do_ref, dzg_ref, dzgt_ref, m_ref, dpa_ref, dpb_ref, ya1_ref, dpre_ref, dbglu_ref, dgain_ref):
        @pl.when(pl.program_id(0) == 0)
        def _():
            dbglu_ref[...] = jnp.zeros_like(dbglu_ref)
            dgain_ref[...] = jnp.zeros_like(dgain_ref)

        dm = _dot_nt(dx1_ref[...], wout_ref[...])
        sga = _sigmoid(zgt_ref[:, 0:D_MODEL].astype(F32))
        sgb = _sigmoid(zgt_ref[:, D_MODEL:].astype(F32))
        pa = pa_ref[...].astype(F32)
        pb = pb_ref[...].astype(F32)
        m_ref[...] = (sga * pa + sgb * pb).astype(BF16)
        dzgt_ref[:, 0:D_MODEL] = (dm * pa * sga * (1.0 - sga)).astype(BF16)
        dzgt_ref[:, D_MODEL:] = (dm * pb * sgb * (1.0 - sgb)).astype(BF16)
        dpa = (dm * sga).astype(BF16)
        dpb = (dm * sgb).astype(BF16)
        dpa_ref[...] = dpa
        dpb_ref[...] = dpb
        dya2 = _dot_nt(dpa, wpa_ref[...])
        dyb = _dot_nt(dpb, wpb_ref[...])
        y0 = y0_ref[...]
        ya1 = _gelu(y0)
        ya1_ref[...] = ya1.astype(BF16)
        s = _sigmoid(_dot(ya1, wglu_ref[...]) + bglu_ref[...])
        dpre = dya2 * ya1 * s * (1.0 - s)
        dpre_ref[...] = dpre.astype(BF16)
        dbglu_ref[...] += jnp.sum(dpre, axis=0, keepdims=True)
        dya1 = dya2 * s + _dot_nt(dpre, wglu_ref[...])
        dy0_ref[...] = dya1 * _gelu_grad(y0)
        ov = o_ref[...]
        zg = zg_ref[...]
        oh = ov * _head_rms(ov)
        on = oh * gain_ref[...]
        sz = _sigmoid(zg)
        dzg_ref[...] = (dyb * on * (sz * (1.0 + zg * (1.0 - sz)))).astype(BF16)
        don = dyb * (zg * sz)
        dgain_ref[...] += jnp.sum(don * oh, axis=0, keepdims=True)
        doh = don * gain_ref[...]
        do_ref[...] = _head_rms(ov) * (doh - oh * _head_mean(doh * oh))

    row = lambda w: pl.BlockSpec((tm, w), lambda i: (i, 0))
    return _pcall(body, "mix_bwd", (t // tm,),
                  [row(D_MODEL), row(S5_WIDTH), row(HG_WIDTH), pl.BlockSpec((tm, HG_WIDTH), lambda i: (i, 3)),
                   row(2 * D_MODEL), row(D_MODEL), row(D_MODEL), _full((S5_WIDTH, S5_WIDTH)), _full((1, S5_WIDTH)),
                   _full((1, HG_WIDTH)), _full((S5_WIDTH, D_MODEL)), _full((HG_WIDTH, D_MODEL)),
                   _full((D_MODEL, D_MODEL))],
                  [row(S5_WIDTH), row(HG_WIDTH), row(HG_WIDTH), row(2 * D_MODEL), row(D_MODEL), row(D_MODEL),
                   row(D_MODEL), row(S5_WIDTH), row(S5_WIDTH), _full((1, S5_WIDTH)), _full((1, HG_WIDTH))],
                  [_sds((t, S5_WIDTH)), _sds((t, HG_WIDTH)), _sds((t, HG_WIDTH), BF16), _sds((t, 2 * D_MODEL), BF16),
                   _sds((t, D_MODEL), BF16), _sds((t, D_MODEL), BF16), _sds((t, D_MODEL), BF16),
                   _sds((t, S5_WIDTH), BF16), _sds((t, S5_WIDTH), BF16), _sds((1, S5_WIDTH)), _sds((1, HG_WIDTH))],
                  )(dx1, y0, o, zh, zgt, pa, pb, w_glu, b_glu, gain, w_pa, w_pb, w_out)


def _s5_bwd(dy0, za, xs, c_bands, b_bands, lam, dskip, nb, seq, ts):
    t = za.shape[0]
    nts = seq // ts

    def body(dy0_ref, za_ref, xs_ref, halo_ref, cr_ref, ci_ref, br_ref, bi_ref, lam_ref, d_ref,
             dza_ref, a_ref, dlam_ref, dd_ref, acc_ref, st_ref):
        b, j = pl.program_id(0), pl.program_id(1)

        @pl.when((b == 0) & (j == 0))
        def _():
            dlam_ref[...] = jnp.zeros_like(dlam_ref)
            dd_ref[...] = jnp.zeros_like(dd_ref)

        @pl.when(j == 0)
        def _():
            st_ref[...] = jnp.zeros_like(st_ref)

        dy0 = dy0_ref[...]
        for q in range(S5_BANDS):
            ch, st = _band(q)
            acc_ref[:, st] = _dot(dy0[:, ch], cr_ref[q])
            acc_ref[:, _im(st)] = _dot(dy0[:, ch], ci_ref[q])
        _complex_scan(acc_ref, lam_ref, st_ref, ts, reverse=True)
        av = acc_ref[...]
        a_ref[...] = av.astype(BF16)
        first = jnp.where(j == nts - 1, 0.0, halo_ref[HALO - 1:HALO, :].astype(F32))
        rows = lax.broadcasted_iota(jnp.int32, (ts, 2 * S5_N), 0)
        xp = jnp.where(rows == 0, first, pltpu.roll(xs_ref[...].astype(F32), 1, axis=0))
        ar, ai = av[:, :S5_N], av[:, S5_N:]
        xr, xi = xp[:, :S5_N], xp[:, S5_N:]
        dlam_ref[0:1, :] += jnp.sum(ar * xr + ai * xi, axis=0, keepdims=True)
        dlam_ref[1:2, :] += jnp.sum(ai * xr - ar * xi, axis=0, keepdims=True)
        for q in range(S5_BANDS):
            ch, st = _band(q)
            dza_ref[:, ch] = (_dot(a_ref[:, st], br_ref[q]) + _dot(a_ref[:, _im(st)], bi_ref[q])
                              + d_ref[:, ch] * dy0[:, ch]).astype(BF16)
        dd_ref[...] += jnp.sum(dy0 * za_ref[...], axis=0, keepdims=True)

    tile = lambda b, j: b * nts + (nts - 1 - j)
    tok = lambda w: pl.BlockSpec((ts, w), lambda b, j: (tile(b, j), 0))
    halo = pl.BlockSpec((HALO, 2 * S5_N), lambda b, j: (jnp.maximum(tile(b, j) * (ts // HALO) - 1, 0), 0))
    to_st, to_ch = _full((S5_BANDS, BAND_CH, BAND_ST)), _full((S5_BANDS, BAND_ST, BAND_CH))
    return _pcall(body, "s5_bwd", (nb, nts),
                  [tok(S5_WIDTH), tok(S5_WIDTH), tok(2 * S5_N), halo, to_st, to_st, to_ch, to_ch,
                   _full((2, S5_N)), _full((1, S5_WIDTH))],
                  [tok(S5_WIDTH), tok(2 * S5_N), _full((2, S5_N)), _full((1, S5_WIDTH))],
                  [_sds((t, S5_WIDTH), BF16), _sds((t, 2 * S5_N), BF16), _sds((2, S5_N)), _sds((1, S5_WIDTH))],
                  scratch=[pltpu.VMEM((ts, 2 * S5_N), F32), pltpu.VMEM((2, S5_N), F32)],
                  )(dy0, za, xs, xs, *c_bands, *b_bands, lam, dskip)


def _hgrn_bwd(zh, do, sts, lb, nb, seq):
    nc = seq // CHUNK

    def body(zh_ref, do_ref, sts_ref, lb_ref, dz_ref, dlb_ref, dst_ref):
        @pl.when(pl.program_id(0) == 0)
        def _():
            dst_ref[...] = jnp.zeros_like(dst_ref)
            dlb_ref[...] = jnp.zeros_like(dlb_ref)

        row = lax.broadcasted_iota(jnp.int32, (CHUNK, CHUNK), 0)
        causal = row >= lax.broadcasted_iota(jnp.int32, (CHUNK, CHUNK), 1)
        last_row = lax.broadcasted_iota(jnp.int32, (CHUNK, HG_HEAD), 0) == CHUNK - 1
        for b in range(nb):
            for h in range(HG_HEADS):
                hs = slice(h * HG_HEAD, (h + 1) * HG_HEAD)
                zq = zh_ref[b, :, h * HG_HEAD:(h + 1) * HG_HEAD]
                zf = zh_ref[b, :, HG_WIDTH + h * HG_HEAD:HG_WIDTH + (h + 1) * HG_HEAD]
                zi = zh_ref[b, :, 2 * HG_WIDTH + h * HG_HEAD:2 * HG_WIDTH + (h + 1) * HG_HEAD]
                lbh = lb_ref[:, hs]
                sf, f, sq, qa, bc, bm, bl = _hgrn_gates(zq, zf, lbh)
                k = 1.0 - f
                e_qt = jnp.exp(bc - bm)
                e_kt = jnp.exp(bm - bc)
                e_b = jnp.exp(bc)
                e_kd = jnp.exp(bl - bc)
                e_l = jnp.exp(bl)
                qt, kt, qb, kd = qa * e_qt, k * e_kt, qa * e_b, k * e_kd
                a = jnp.where(causal, _dot_nt(qt, kt), 0.0)
                st = sts_ref[b, 0, h]
                dst = dst_ref[b, h]
                dov = do_ref[b, :, hs]
                da = jnp.where(causal, _dot_nt(dov, zi), 0.0)
                dqt = _hdot(da, kt)
                dkt = _hdot_tn(da, qt)
                dqb = _dot(dov, st)
                di = _dot_tn(a, dov) + _dot_nt(kd, dst)
                dkd = _dot(zi, dst)
                de_l = jnp.sum(dst * st, axis=0, keepdims=True)
                dst_ref[b, h] = dst * e_l + _dot_tn(dov, qb)
                dqa = dqt * e_qt + dqb * e_b
                dk = dkt * e_kt + dkd * e_kd
                dbl = jnp.sum(dkd * kd, axis=0, keepdims=True) + de_l * e_l
                db = dqt * qt - dkt * kt + dqb * qb - dkd * kd + jnp.where(last_row, dbl, 0.0)
                df = _cumsum_rows(db, reverse=True) / f - dk
                dzq = dqa * QSCALE * (sq * (1.0 + zq * (1.0 - sq)))
                dzf = df * (1.0 - lbh) * sf * (1.0 - sf)
                dz_ref[b, :, h * HG_HEAD:(h + 1) * HG_HEAD] = dzq.astype(BF16)
                dz_ref[b, :, HG_WIDTH + h * HG_HEAD:HG_WIDTH + (h + 1) * HG_HEAD] = dzf.astype(BF16)
                dz_ref[b, :, 2 * HG_WIDTH + h * HG_HEAD:2 * HG_WIDTH + (h + 1) * HG_HEAD] = di.astype(BF16)
                dlb_ref[:, hs] += jnp.sum(df * (1.0 - sf), axis=0, keepdims=True)

    rev = lambda c: nc - 1 - c
    return _pcall(body, "hgrn_bwd", (nc,),
                  [pl.BlockSpec((nb, CHUNK, 4 * HG_WIDTH), lambda c: (0, rev(c), 0)),
                   pl.BlockSpec((nb, CHUNK, HG_WIDTH), lambda c: (0, rev(c), 0)),
                   pl.BlockSpec((nb, 1, HG_HEADS, HG_HEAD, HG_HEAD), lambda c: (0, rev(c), 0, 0, 0)),
                   _full((1, HG_WIDTH))],
                  [pl.BlockSpec((nb, CHUNK, 3 * HG_WIDTH), lambda c: (0, rev(c), 0)), _full((1, HG_WIDTH))],
                  [_sds((nb, seq, 3 * HG_WIDTH), BF16), _sds((1, HG_WIDTH))],
                  scratch=[pltpu.VMEM((nb, HG_HEADS, HG_HEAD, HG_HEAD), F32)])(zh, do, sts, lb)


def _in_proj_bwd(dza, dzh, dzg, dzgt, dx1, x, g_mix, w_in, tm):
    t = x.shape[0]

    def body(dza_ref, dzh_ref, dzg_ref, dzgt_ref, dx1_ref, x_ref, g_ref, w_ref, dz_ref, dx_ref, dg_ref):
        @pl.when(pl.program_id(0) == 0)
        def _():
            dg_ref[...] = jnp.zeros_like(dg_ref)

        c1, c2, c3 = S5_WIDTH, S5_WIDTH + 3 * HG_WIDTH, S5_WIDTH + 4 * HG_WIDTH
        dz_ref[:, 0:c1] = dza_ref[...]
        dz_ref[:, c1:c2] = dzh_ref[...]
        dz_ref[:, c2:c3] = dzg_ref[...]
        dz_ref[:, c3:] = dzgt_ref[...]
        du = _dot_nt(dz_ref[...], w_ref[...])
        xv = x_ref[...]
        r = lax.rsqrt(jnp.mean(xv * xv, axis=-1, keepdims=True) + EPS)
        xn = xv * r
        dg_ref[...] += jnp.sum(du * xn, axis=0, keepdims=True)
        dxn = du * g_ref[...]
        dx_ref[...] = dx1_ref[...] + r * (dxn - xn * jnp.mean(dxn * xn, axis=-1, keepdims=True))

    row = lambda w: pl.BlockSpec((tm, w), lambda i: (i, 0))
    return _pcall(body, "in_proj_bwd", (t // tm,),
                  [row(S5_WIDTH), row(3 * HG_WIDTH), row(HG_WIDTH), row(2 * D_MODEL), row(D_MODEL), row(D_MODEL),
                   _full((1, D_MODEL)), _full((D_MODEL, N_IN))],
                  [row(N_IN), row(D_MODEL), _full((1, D_MODEL))],
                  [_sds((t, N_IN), BF16), _sds((t, D_MODEL)), _sds((1, D_MODEL))],
                  )(dza, dzh, dzg, dzgt, dx1, x, g_mix, w_in)


def _tie(*arrays):
    return jnp.zeros((SUBLANES, 128), F32) + sum(a.reshape(-1)[0].astype(F32) for a in arrays)


def _after(value, token):
    return value + token[0, 0]


def _local_step(x3, tgt3, weights, sp, emit, emit_small):
    nb, seq, _ = x3.shape
    t = nb * seq
    tm = _token_tile(seq)
    x = x3.reshape(t, D_MODEL)
    tgt = tgt3.reshape(t, D_MODEL)
    row = lambda v: v.reshape(1, -1)

    a_re = sp["s5_a_re"].reshape(S5_N, 1)
    a_im = sp["s5_a_im"].reshape(S5_N, 1)
    ldt = jnp.repeat(sp["s5_log_dt"].reshape(S5_GROUPS), S5_STATE).reshape(S5_N, 1)
    b_re = sp["s5_b_re"].reshape(S5_N, S5_GROUP)
    b_im = sp["s5_b_im"].reshape(S5_N, S5_GROUP)
    lr, li, bb_re, bb_im, lb = _params_fwd(a_re, a_im, ldt, b_re, b_im, sp["hg_lb_logits"])
    lam = jnp.concatenate([lr.reshape(1, S5_N), li.reshape(1, S5_N)], axis=0)
    gps = lambda m: m.reshape(S5_GROUPS, S5_STATE, S5_GROUP)
    swap = lambda m: m.transpose(0, 2, 1)
    b_to_st = (_band_blocks(swap(gps(bb_re))), _band_blocks(swap(gps(bb_im))))
    b_to_ch = (_band_blocks(gps(bb_re)), _band_blocks(gps(bb_im)))
    c_to_ch = (_band_blocks(swap(sp["s5_c_re"])), _band_blocks(swap(-sp["s5_c_im"])))
    c_to_st = (_band_blocks(sp["s5_c_re"]), _band_blocks(-sp["s5_c_im"]))

    g_mix, g_ffn, g_final = row(sp["g_mix"]), row(sp["g_ffn"]), row(sp["g_final"])
    b_glu, gain, dskip, b_conv = row(sp["b_glu"]), row(sp["hg_norm_gain"]), row(sp["s5_d"]), row(sp["b_conv"])

    w_in = weights("in", lam, *b_to_st, *b_to_ch, *c_to_ch, *c_to_st)["w_in"]
    u, za, zh, zgt = _in_proj(x, g_mix, w_in, tm)
    xs, y0 = _s5_fwd(za, b_to_st, lam, c_to_ch, dskip, nb, seq, tm)
    o3, sts = _hgrn_fwd(zh.reshape(nb, seq, 4 * HG_WIDTH), lb, nb, seq)
    o = o3.reshape(t, HG_WIDTH)
    wm = weights("mix", y0, o3)
    x1, u2, pa, pb, ya2, yb = _mix_fwd(x, y0, o, zh, zgt, wm["w_glu"], b_glu, gain, wm["w_pa"], wm["w_pb"],
                                       wm["w_out"], g_ffn, tm)
    wf = weights("ffn", u2)
    h = _ffn_up(u2, wf["w_up"], min(4 * tm, t))
    hc, a, dx2, dx2b, loss, dg_final = _ffn_down_loss(h, x1, tgt, wf["w_conv"], b_conv, wf["w_down"], g_final,
                                                      seq, tm)

    wgrad = functools.partial(_wgrad, tn=256, out_dtype=BF16)
    dhc, db_conv = _ffn_bwd_act(dx2b, hc, wf["w_down"], tm)
    sent = emit({"w_down": wgrad(a, dx2b, "dw_down")})
    dh, dx1, dx1b, dg_ffn, dw_conv = _ffn_bwd_up(dhc, h, dx2, x1, wf["w_conv"], wf["w_up"], _after(g_ffn, sent),
                                                 seq, tm)
    sent = emit({"w_up": wgrad(dh, u2, "dw_up", transpose_out=True), "w_conv": dw_conv})
    (dy0, do, dzg, dzgt, m, dpa, dpb, ya1, dpre, db_glu, dgain) = _mix_bwd(
        dx1b, y0, o, zh, zgt, pa, pb, wm["w_glu"], _after(b_glu, sent), gain, wm["w_pa"], wm["w_pb"], wm["w_out"], tm)
    sent = emit({"w_out": wgrad(m, dx1b, "dw_out"), "w_pa": wgrad(ya2, dpa, "dw_pa"),
                 "w_pb": wgrad(yb, dpb, "dw_pb"), "w_glu": wgrad(ya1, dpre, "dw_glu")})
    dzh3, dlb = _hgrn_bwd(zh.reshape(nb, seq, 4 * HG_WIDTH), do.reshape(nb, seq, HG_WIDTH), sts, _after(lb, sent),
                          nb, seq)
    dza, a_s5, dlam, dd = _s5_bwd(dy0, za, xs, c_to_st, b_to_ch, lam, dskip, nb, seq, tm)
    dz, dx, dg_mix = _in_proj_bwd(dza, dzh3.reshape(t, 3 * HG_WIDTH), dzg, dzgt, dx1, x, g_mix, w_in, tm)
    sent = emit({"w_in": wgrad(dz, u, "dw_in", transpose_out=True)})

    band = HG_HEAD
    dbb_band = _wgrad(a_s5, za, "dbb_s5", 512, band=band, after=sent)
    dc_band = _wgrad(xs, dy0, "dc_s5", 512, band=band, after=sent)
    dbb_re = _diag_blocks(dbb_band[:S5_N], S5_STATE, S5_GROUP).reshape(S5_N, S5_GROUP)
    dbb_im = _diag_blocks(dbb_band[S5_N:], S5_STATE, S5_GROUP).reshape(S5_N, S5_GROUP)
    dc_re = _diag_blocks(dc_band[:S5_N], S5_STATE, S5_GROUP).transpose(0, 2, 1)
    dc_im = -_diag_blocks(dc_band[S5_N:], S5_STATE, S5_GROUP).transpose(0, 2, 1)
    da_re, da_im, dldt, db_re, db_im, dlogits = _params_bwd(
        a_re, a_im, ldt, b_re, b_im, sp["hg_lb_logits"],
        dlam[0].reshape(S5_N, 1), dlam[1].reshape(S5_N, 1), dbb_re, dbb_im, dlb)
    emit_small({"g_mix": dg_mix, "s5_a_re": da_re, "s5_a_im": da_im, "s5_log_dt": dldt, "s5_b_re": db_re,
                "s5_b_im": db_im, "s5_c_re": dc_re, "s5_c_im": dc_im, "s5_d": dd, "b_glu": db_glu,
                "hg_lb_logits": dlogits, "hg_norm_gain": dgain, "g_ffn": dg_ffn, "b_conv": db_conv,
                "g_final": dg_final, "loss": loss})
    return dx.reshape(nb, seq, D_MODEL)


def _mesh_peers():
    x, y, c = lax.axis_index("x"), lax.axis_index("y"), lax.axis_index("c")
    peers = []
    for k in range(1, N_DEV):
        px, py, pc = (1 - x if k & 4 else x), (1 - y if k & 2 else y), (1 - c if k & 1 else c)
        peers.append((k, (px, py, pc), 4 * px + 2 * py + pc))
    return 4 * x + 2 * y + c, peers


_HBM = pl.BlockSpec(memory_space=pltpu.HBM)
_SEM = pl.BlockSpec(memory_space=pltpu.SEMAPHORE)


def _exchange_start(name, operands, after):
    n = len(operands)
    me = 4 * lax.axis_index("x") + 2 * lax.axis_index("y") + lax.axis_index("c")
    flags = [per_peer for _, per_peer in operands]
    srcs, lands = [], []
    for arr, per_peer in operands:
        own = lax.dynamic_index_in_dim(arr, me, 0, keepdims=True) if per_peer else arr[None]
        land = lax.dynamic_update_slice_in_dim(lax.empty((N_DEV,) + own.shape[1:], arr.dtype), own, me, 0)
        srcs.append(pltpu.with_memory_space_constraint(arr, pltpu.HBM))
        lands.append(pltpu.with_memory_space_constraint(land, pltpu.HBM))
    copies = (N_DEV - 1) * n

    def body(*refs):
        src_refs, land_refs = refs[:n], refs[n:2 * n]
        send_sems, recv_sems = refs[2 * n + 1], refs[2 * n + 2]
        token = refs[-1]
        my_slab, peers = _mesh_peers()
        for k, peer, slab in peers:
            for i in range(n):
                s = (k - 1) * n + i
                pltpu.make_async_remote_copy(
                    src_ref=src_refs[i].at[slab] if flags[i] else src_refs[i], dst_ref=land_refs[i].at[my_slab],
                    send_sem=send_sems.at[s], recv_sem=recv_sems.at[s], device_id=peer,
                    device_id_type=pl.DeviceIdType.MESH).start()
        token[...] = jnp.zeros_like(token)

    outs = pl.pallas_call(
        body, name=name,
        out_shape=(pltpu.SemaphoreType.DMA((copies,)), pltpu.SemaphoreType.DMA((copies,)),
                   *[pltpu.HBM(a.shape, a.dtype) for a in srcs], *[pltpu.HBM(a.shape, a.dtype) for a in lands],
                   _sds((SUBLANES, 128))),
        in_specs=[_HBM] * (2 * n) + [pl.BlockSpec(memory_space=pl.ANY)],
        out_specs=(_SEM, _SEM, *[_HBM] * (2 * n), pl.BlockSpec(memory_space=pltpu.VMEM)),
        input_output_aliases={i: 2 + i for i in range(2 * n)},
        compiler_params=pltpu.CompilerParams(has_side_effects=pltpu.SideEffectType.DATAFLOW_SIDE_EFFECTING),
    )(*srcs, *lands, after)
    state = (flags, outs[0], outs[1], outs[2:2 + n], outs[2 + n:2 + 2 * n])
    return state, outs[-1]


def _exchange_wait(name, state, *after):
    flags, send_sems, recv_sems, srcs, lands = state
    n = len(flags)

    def body(*refs):
        src_refs, land_refs = refs[:n], refs[n:2 * n]
        send_ref, recv_ref = refs[2 * n], refs[2 * n + 1]
        _, peers = _mesh_peers()
        for k, peer, slab in peers:
            for i in range(n):
                s = (k - 1) * n + i
                copy = pltpu.make_async_remote_copy(
                    src_ref=src_refs[i].at[slab] if flags[i] else src_refs[i], dst_ref=land_refs[i].at[slab],
                    send_sem=send_ref.at[s], recv_sem=recv_ref.at[s], device_id=peer,
                    device_id_type=pl.DeviceIdType.MESH)
                copy.wait_send()
                copy.wait_recv()

    outs = pl.pallas_call(
        body, name=name,
        out_shape=(*[pltpu.HBM(a.shape, a.dtype) for a in srcs], *[pltpu.HBM(a.shape, a.dtype) for a in lands]),
        in_specs=[_HBM] * (2 * n) + [_SEM, _SEM] + [pl.BlockSpec(memory_space=pl.ANY)] * len(after),
        out_specs=tuple([_HBM] * (2 * n)),
        input_output_aliases={i: i for i in range(2 * n)},
        compiler_params=pltpu.CompilerParams(has_side_effects=pltpu.SideEffectType.DATAFLOW_SIDE_EFFECTING),
    )(*srcs, *lands, send_sems, recv_sems, *after)
    return list(outs[n:])


def _join_cols(parts, name, tr):
    _, r, c = parts.shape

    def body(p_ref, o_ref):
        for j in range(N_DEV):
            o_ref[:, j * c:(j + 1) * c] = p_ref[j]

    return _pcall(body, name, (r // tr,), [pl.BlockSpec((N_DEV, tr, c), lambda i: (0, i, 0))],
                  pl.BlockSpec((tr, N_DEV * c), lambda i: (i, 0)), _sds((r, N_DEV * c), parts.dtype))(parts)


def _split_cols(full, name, tr):
    r, c = full.shape[0], full.shape[1] // N_DEV

    def body(f_ref, o_ref):
        for j in range(N_DEV):
            o_ref[j] = f_ref[:, j * c:(j + 1) * c]

    return _pcall(body, name, (r // tr,), [pl.BlockSpec((tr, N_DEV * c), lambda i: (i, 0))],
                  pl.BlockSpec((N_DEV, tr, c), lambda i: (0, i, 0)), _sds((N_DEV, r, c), full.dtype))(full)


def _adamw(parts, w, m, v, name, tile):
    _, rows, cols = w.shape

    def body(p_ref, w_ref, m_ref, v_ref, g_out, d_out, m_out, v_out):
        g = p_ref[0].astype(F32)
        for k in range(1, N_DEV):
            g = g + p_ref[k].astype(F32)
        m1 = ADAM_B1 * m_ref[0] + (1.0 - ADAM_B1) * g
        v1 = ADAM_B2 * v_ref[0] + (1.0 - ADAM_B2) * (g * g)
        m_hat = m1 / (1.0 - ADAM_B1 ** ADAM_STEP)
        v_hat = v1 / (1.0 - ADAM_B2 ** ADAM_STEP)
        g_out[0] = g
        d_out[0] = -ADAM_LR * (m_hat / (jnp.sqrt(v_hat) + ADAM_EPS) + ADAM_WD * w_ref[0])
        m_out[0] = m1
        v_out[0] = v1

    row = pl.BlockSpec((1, tile, cols), lambda i: (0, i, 0))
    return _pcall(body, name, (rows // tile,),
                  [pl.BlockSpec((N_DEV, tile, cols), lambda i: (0, i, 0)), row, row, row],
                  [row, row, row, row], [_sds((1, rows, cols))] * 4)(parts, w, m, v)


BIG = {
    "w_in": ((D_MODEL, N_IN // N_DEV), True, 256),
    "w_glu": ((S5_WIDTH // N_DEV, S5_WIDTH), False, S5_WIDTH // N_DEV),
    "w_pa": ((S5_WIDTH, D_MODEL // N_DEV), True, S5_WIDTH),
    "w_pb": ((HG_WIDTH, D_MODEL // N_DEV), True, HG_WIDTH),
    "w_out": ((D_MODEL // N_DEV, D_MODEL), False, D_MODEL // N_DEV),
    "w_up": ((D_MODEL, 2 * D_FF // N_DEV), True, 256),
    "w_conv": ((CONV_W, 2 * D_FF // N_DEV), True, CONV_W),
    "w_down": ((D_FF // N_DEV, D_MODEL), False, D_FF // N_DEV // 2),
}
UNALIGNED_COLS = ("w_in", "w_up", "w_conv")


def _join_shards(n, parts):
    (a, b), by_cols, _ = BIG[n]
    if not by_cols:
        return parts.reshape(N_DEV * a, b)
    if n in UNALIGNED_COLS:
        return _join_cols(parts, "join_" + n, min(a, 256))
    return parts.transpose(1, 0, 2).reshape(a, N_DEV * b)


def _split_shards(n, full):
    (a, b), by_cols, _ = BIG[n]
    if not by_cols:
        return full.reshape(N_DEV, a, b)
    if n in UNALIGNED_COLS:
        return _split_cols(full, "split_" + n, min(a, 256))
    return full.reshape(a, N_DEV, b).transpose(1, 0, 2)


PACKED = SMALL + (("loss", (1,)),)


def _pack_small(d):
    flat = jnp.concatenate([d[n].reshape(-1) for n, _ in PACKED])
    return jnp.pad(flat, (0, SMALL_ROWS * PACK_W - flat.shape[0])).reshape(SMALL_ROWS, PACK_W)


def _unpack_small(p):
    flat = p.reshape(-1)
    out, off = {}, 0
    for n, shp in PACKED:
        size = math.prod(shp)
        out[n] = flat[off:off + size].reshape(shp)
        off += size
    return out


def kernel(x, g_mix, w_in, s5_a_re, s5_a_im, s5_log_dt, s5_b_re, s5_b_im, s5_c_re, s5_c_im, s5_d, w_glu, b_glu, hg_lb_logits, hg_norm_gain, w_pa, w_pb, w_out, g_ffn, w_up, w_conv, b_conv, w_down, g_final, loss_target, m_g_mix, m_w_in, m_s5_a_re, m_s5_a_im, m_s5_log_dt, m_s5_b_re, m_s5_b_im, m_s5_c_re, m_s5_c_im, m_s5_d, m_w_glu, m_b_glu, m_hg_lb_logits, m_hg_norm_gain, m_w_pa, m_w_pb, m_w_out, m_g_ffn, m_w_up, m_w_conv, m_b_conv, m_w_down, m_g_final, v_g_mix, v_w_in, v_s5_a_re, v_s5_a_im, v_s5_log_dt, v_s5_b_re, v_s5_b_im, v_s5_c_re, v_s5_c_im, v_s5_d, v_w_glu, v_b_glu, v_hg_lb_logits, v_hg_norm_gain, v_w_pa, v_w_pb, v_w_out, v_g_ffn, v_w_up, v_w_conv, v_b_conv, v_w_down, v_g_final):
    given = dict(locals())
    small_names = [n for n, _ in SMALL]

    pay = {n: given[n][0] if n == "w_conv" else given[n][0].astype(BF16) for n in BIG}
    groups = {"in": ["w_in"], "mix": ["w_glu", "w_pa", "w_pb", "w_out"], "ffn": ["w_up", "w_down", "w_conv"]}
    gathers, order = {}, pay["w_in"]
    for grp, names in groups.items():
        gathers[grp], order = _exchange_start("gather_" + grp + "_start", [(pay[n], False) for n in names], order)

    zero = jnp.zeros((1,), F32)
    packed = [_pack_small({**{k: given[pre + k] for k in small_names}, "loss": zero})[None] for pre in ("", "m_", "v_")]

    def weights(grp, *after):
        if grp == "in":
            after = (*after, order, *packed)
        got = _exchange_wait("gather_" + grp + "_wait", gathers[grp], *after)
        return {n: _join_shards(n, g) for n, g in zip(groups[grp], got)}

    in_flight, started = [], []

    def emit(grads):
        names = list(grads)
        state, token = _exchange_start("grads_" + names[0] + "_start",
                                       [(_split_shards(n, grads[n]), True) for n in names], grads[names[0]])
        in_flight.append((names, state))
        return token

    def emit_small(grads):
        pack = _pack_small(grads)
        state, token = _exchange_start("grads_small_start", [(pack, False)], pack)
        in_flight.append((["small"], state))
        started.append(token)

    sp = {n: (given[n] if n in ("g_final", "hg_lb_logits") else given[n][0]) for n in small_names}
    sp["g_mix"] = _after(sp["g_mix"], order)
    dx = _local_step(x, loss_target, weights, sp, emit, emit_small)

    res = {}
    after = [started[-1]]
    in_flight.insert(-1, in_flight.pop())
    for names, state in in_flight:
        parts = _exchange_wait("grads_" + names[0] + "_wait", state, *after)
        if names != ["small"]:
            for n, part in zip(names, parts):
                res[n] = _adamw(part, given[n], given["m_" + n], given["v_" + n], "adamw_" + n, BIG[n][2])
            after = [res[n][0] for n in names]
            continue
        rs4 = _adamw(parts[0], *packed, "adamw_small", SMALL_ROWS)
        small4 = [_unpack_small(r) for r in rs4]
        for k in small_names:
            res[k] = [us[k] for us in small4]
        total_loss = small4[0]["loss"][0]
        after = [res[k][j] for j, k in enumerate(("s5_b_re", "s5_b_im", "s5_c_re", "s5_c_im"))]
    return (total_loss, dx, *[res[n][0] for n in WEIGHT_ORDER], *[res[n][1] for n in WEIGHT_ORDER],
            *[res[n][2] for n in WEIGHT_ORDER], *[res[n][3] for n in WEIGHT_ORDER])
```

```python
import functools
import math

import jax
import jax.numpy as jnp
from jax import lax
from jax.experimental import pallas as pl
from jax.experimental.pallas import tpu as pltpu

F32 = jnp.float32
BF16 = jnp.bfloat16

D_MODEL = 1024
S5_WIDTH = 512
S5_GROUP = 16
S5_GROUPS = 32
S5_STATE = 64
S5_N = S5_GROUPS * S5_STATE
HG_WIDTH = 512
HG_HEAD = 128
HG_HEADS = 4
D_FF = 2816
CONV_W = 3
CHUNK = 64
N_IN = S5_WIDTH + 4 * HG_WIDTH + 2 * D_MODEL
EPS = 1e-6
QSCALE = HG_HEAD ** -0.5

ADAM_LR = 0.001
ADAM_B1 = 0.9
ADAM_B2 = 0.999
ADAM_EPS = 1e-08
ADAM_WD = 0.01
ADAM_STEP = 10

N_DEV = 8
V7X_VMEM_BYTES = 64 * 1024 * 1024
VMEM_LIMIT = V7X_VMEM_BYTES * 7 // 8
SUBLANES = 8
PACK_W = 1024

SMALL = (
    ("g_mix", (1, D_MODEL)),
    ("s5_a_re", (1, S5_GROUPS, S5_STATE)),
    ("s5_a_im", (1, S5_GROUPS, S5_STATE)),
    ("s5_log_dt", (1, S5_GROUPS)),
    ("s5_b_re", (1, S5_GROUPS, S5_STATE, S5_GROUP)),
    ("s5_b_im", (1, S5_GROUPS, S5_STATE, S5_GROUP)),
    ("s5_c_re", (1, S5_GROUPS, S5_GROUP, S5_STATE)),
    ("s5_c_im", (1, S5_GROUPS, S5_GROUP, S5_STATE)),
    ("s5_d", (1, S5_WIDTH)),
    ("b_glu", (1, S5_WIDTH)),
    ("hg_lb_logits", (2, HG_WIDTH)),
    ("hg_norm_gain", (1, HG_WIDTH)),
    ("g_ffn", (1, D_MODEL)),
    ("b_conv", (1, 2 * D_FF)),
    ("g_final", (D_MODEL,)),
)
WEIGHT_ORDER = ("g_mix", "w_in", "s5_a_re", "s5_a_im", "s5_log_dt", "s5_b_re", "s5_b_im", "s5_c_re", "s5_c_im",
                "s5_d", "w_glu", "b_glu", "hg_lb_logits", "hg_norm_gain", "w_pa", "w_pb", "w_out", "g_ffn",
                "w_up", "w_conv", "b_conv", "w_down", "g_final")


def _pcall(body, name, grid, in_specs, out_specs, out_shape, scratch=()):
    return pl.pallas_call(
        body, name=name, grid=grid, in_specs=in_specs, out_specs=out_specs, out_shape=out_shape,
        scratch_shapes=list(scratch),
        compiler_params=pltpu.CompilerParams(dimension_semantics=("arbitrary",) * len(grid),
                                             vmem_limit_bytes=VMEM_LIMIT),
    )


def _full(shape):
    return pl.BlockSpec(shape, lambda *_: (0,) * len(shape))


def _sds(shape, dtype=F32):
    return jax.ShapeDtypeStruct(shape, dtype)


def _dot(a, b):
    return jnp.dot(a.astype(BF16), b.astype(BF16), preferred_element_type=F32)


def _dot_nt(a, b):
    return lax.dot_general(a.astype(BF16), b.astype(BF16), (((1,), (1,)), ((), ())), preferred_element_type=F32)


def _dot_tn(a, b):
    return lax.dot_general(a.astype(BF16), b.astype(BF16), (((0,), (0,)), ((), ())), preferred_element_type=F32)


def _hdot(a, b):
    return jnp.dot(a, b, preferred_element_type=F32, precision=lax.Precision.HIGHEST)


def _hdot_tn(a, b):
    return lax.dot_general(a, b, (((0,), (0,)), ((), ())), preferred_element_type=F32,
                           precision=lax.Precision.HIGHEST)


def _sigmoid(x):
    return jax.nn.sigmoid(x)


GELU_C = math.sqrt(2.0 / math.pi)
GELU_A = 0.044715


def _gelu(x):
    return 0.5 * x * (1.0 + jnp.tanh(GELU_C * (x + GELU_A * (x * x * x))))


def _gelu_grad(x):
    t = jnp.tanh(GELU_C * (x + GELU_A * (x * x * x)))
    return 0.5 * (1.0 + t) + 0.5 * x * (1.0 - t * t) * (GELU_C * (1.0 + 3.0 * GELU_A * x * x))


def _cumsum_rows(v, reverse=False):
    n = v.shape[0]
    row = lax.broadcasted_iota(jnp.int32, v.shape, 0)
    s = 1
    while s < n:
        if reverse:
            v = v + jnp.where(row < n - s, pltpu.roll(v, n - s, axis=0), 0.0)
        else:
            v = v + jnp.where(row >= s, pltpu.roll(v, s, axis=0), 0.0)
        s *= 2
    return v


def _token_tile(seq):
    return min(256, seq)


def _s5_disc(a_re, a_im, ldt, b_re, b_im):
    dt = jnp.exp(ldt)
    mag = jnp.exp(a_re * dt)
    ang = a_im * dt
    lb_re = mag * jnp.cos(ang)
    lb_im = mag * jnp.sin(ang)
    den = a_re * a_re + a_im * a_im
    n_re = lb_re - 1.0
    n_im = lb_im
    co_re = (n_re * a_re + n_im * a_im) / den
    co_im = (n_im * a_re - n_re * a_im) / den
    bb_re = co_re * b_re - co_im * b_im
    bb_im = co_re * b_im + co_im * b_re
    return lb_re, lb_im, bb_re, bb_im


def _params_fwd(a_re, a_im, ldt, b_re, b_im, logits):
    def body(are, aim, ld, bre, bim, lg, lr_o, li_o, bbr_o, bbi_o, lb_o):
        lr, li, bbr, bbi = _s5_disc(are[...], aim[...], ld[...], bre[...], bim[...])
        lr_o[...] = lr
        li_o[...] = li
        bbr_o[...] = bbr
        bbi_o[...] = bbi
        lb_o[...] = _sigmoid(lg[0:1, :] - lg[1:2, :])

    col, mat = (S5_N, 1), (S5_N, S5_GROUP)
    return _pcall(body, "params_fwd", (1,),
                  [_full(col), _full(col), _full(col), _full(mat), _full(mat), _full((2, HG_WIDTH))],
                  [_full(col), _full(col), _full(mat), _full(mat), _full((1, HG_WIDTH))],
                  [_sds(col), _sds(col), _sds(mat), _sds(mat), _sds((1, HG_WIDTH))])(a_re, a_im, ldt, b_re, b_im, logits)


def _params_bwd(a_re, a_im, ldt, b_re, b_im, logits, dlr, dli, dbbr, dbbi, dlb):
    def body(are, aim, ld, bre, bim, lg, dlr_r, dli_r, dbbr_r, dbbi_r, dlb_r,
             dare_o, daim_o, dld_o, dbre_o, dbim_o, dlg_o):
        _, vjp = jax.vjp(_s5_disc, are[...], aim[...], ld[...], bre[...], bim[...])
        dare, daim, dld, dbre, dbim = vjp((dlr_r[...], dli_r[...], dbbr_r[...], dbbi_r[...]))
        dare_o[...] = dare
        daim_o[...] = daim
        dbre_o[...] = dbre
        dbim_o[...] = dbim
        for g in range(S5_GROUPS):
            dld_o[:, g:g + 1] = jnp.sum(dld[g * S5_STATE:(g + 1) * S5_STATE, :], axis=0, keepdims=True)
        lb = _sigmoid(lg[0:1, :] - lg[1:2, :])
        d0 = dlb_r[...] * lb * (1.0 - lb)
        dlg_o[0:1, :] = d0
        dlg_o[1:2, :] = -d0

    col, mat = (S5_N, 1), (S5_N, S5_GROUP)
    return _pcall(body, "params_bwd", (1,),
                  [_full(col), _full(col), _full(col), _full(mat), _full(mat), _full((2, HG_WIDTH)),
                   _full(col), _full(col), _full(mat), _full(mat), _full((1, HG_WIDTH))],
                  [_full(col), _full(col), _full((1, S5_GROUPS)), _full(mat), _full(mat), _full((2, HG_WIDTH))],
                  [_sds(col), _sds(col), _sds((1, S5_GROUPS)), _sds(mat), _sds(mat), _sds((2, HG_WIDTH))],
                  )(a_re, a_im, ldt, b_re, b_im, logits, dlr, dli, dbbr, dbbi, dlb)


def _band_blocks(m):
    g, r, c = m.shape
    gb = g // S5_BANDS
    m4 = m.astype(BF16).reshape(S5_BANDS, gb, r, c)
    on_diag = jnp.eye(gb, dtype=bool)[None, :, None, :, None]
    return jnp.where(on_diag, m4[:, :, :, None, :], 0).reshape(S5_BANDS, gb * r, gb * c)


def _diag_blocks(band, r, c):
    g, nb = band.shape[0] // r, band.shape[1] // c
    on_diag = (jnp.arange(g) % nb)[:, None, None, None] == jnp.arange(nb)[None, None, :, None]
    return jnp.sum(jnp.where(on_diag, band.reshape(g, r, nb, c), 0.0), axis=2)


def _in_proj(x, g_mix, w_in, tm):
    t = x.shape[0]

    def body(x_ref, g_ref, w_ref, u_ref, za_ref, zh_ref, zg_ref):
        xv = x_ref[...]
        r = lax.rsqrt(jnp.mean(xv * xv, axis=-1, keepdims=True) + EPS)
        u = (xv * r * g_ref[...]).astype(BF16)
        u_ref[...] = u
        za_ref[...] = jnp.dot(u, w_ref[:, 0:S5_WIDTH], preferred_element_type=F32)
        zh_ref[...] = jnp.dot(u, w_ref[:, S5_WIDTH:S5_WIDTH + 4 * HG_WIDTH], preferred_element_type=F32)
        zg_ref[...] = jnp.dot(u, w_ref[:, S5_WIDTH + 4 * HG_WIDTH:], preferred_element_type=F32).astype(BF16)

    row = lambda w: pl.BlockSpec((tm, w), lambda i: (i, 0))
    return _pcall(body, "in_proj", (t // tm,),
                  [row(D_MODEL), _full((1, D_MODEL)), _full((D_MODEL, N_IN))],
                  [row(D_MODEL), row(S5_WIDTH), row(4 * HG_WIDTH), row(2 * D_MODEL)],
                  [_sds((t, D_MODEL), BF16), _sds((t, S5_WIDTH)), _sds((t, 4 * HG_WIDTH)),
                   _sds((t, 2 * D_MODEL), BF16)],
                  )(x, g_mix, w_in)


S5_LANES = 512
S5_BANDS = 4


def _band(q):
    return (slice(q * S5_WIDTH // S5_BANDS, (q + 1) * S5_WIDTH // S5_BANDS),
            slice(q * S5_N // S5_BANDS, (q + 1) * S5_N // S5_BANDS))


def _im(st):
    return slice(S5_N + st.start, S5_N + st.stop)


SCAN_UNROLL = 8


def _complex_scan(buf_ref, lam_ref, st_ref, ts, reverse):
    chunks = [slice(cc * S5_LANES, (cc + 1) * S5_LANES) for cc in range(S5_N // S5_LANES)]
    nch = len(chunks)
    wr = [lam_ref[0:1, re] for re in chunks]
    wi = [-lam_ref[1:2, re] if reverse else lam_ref[1:2, re] for re in chunks]

    def block(ib, carry):
        vr, vi = list(carry[:nch]), list(carry[nch:])
        first = ts - SCAN_UNROLL - ib * SCAN_UNROLL if reverse else ib * SCAN_UNROLL
        first = pl.multiple_of(first, SCAN_UNROLL)
        for k in range(SCAN_UNROLL):
            row = pl.ds(first + (SCAN_UNROLL - 1 - k if reverse else k), 1)
            for cc, re in enumerate(chunks):
                nr = wr[cc] * vr[cc] - wi[cc] * vi[cc] + buf_ref[row, re]
                ni = wr[cc] * vi[cc] + wi[cc] * vr[cc] + buf_ref[row, _im(re)]
                buf_ref[row, re] = nr
                buf_ref[row, _im(re)] = ni
                vr[cc], vi[cc] = nr, ni
        return tuple(vr + vi)

    init = tuple(st_ref[0:1, re] for re in chunks) + tuple(st_ref[1:2, re] for re in chunks)
    last = lax.fori_loop(0, ts // SCAN_UNROLL, block, init)
    for cc, re in enumerate(chunks):
        st_ref[0:1, re] = last[cc]
        st_ref[1:2, re] = last[nch + cc]


BAND_CH = S5_WIDTH // S5_BANDS
BAND_ST = S5_N // S5_BANDS


def _s5_fwd(za, b_bands, lam, c_bands, dskip, nb, seq, ts):
    t = za.shape[0]
    nts = seq // ts

    def body(za_ref, br_ref, bi_ref, lam_ref, cr_ref, ci_ref, d_ref, xs_ref, y_ref, buf_ref, st_ref):
        @pl.when(pl.program_id(1) == 0)
        def _():
            st_ref[...] = jnp.zeros_like(st_ref)

        zav = za_ref[...]
        for q in range(S5_BANDS):
            ch, st = _band(q)
            buf_ref[:, st] = _dot(zav[:, ch], br_ref[q])
            buf_ref[:, _im(st)] = _dot(zav[:, ch], bi_ref[q])
        _complex_scan(buf_ref, lam_ref, st_ref, ts, reverse=False)
        xs_ref[...] = buf_ref[...].astype(BF16)
        for q in range(S5_BANDS):
            ch, st = _band(q)
            y_ref[:, ch] = (_dot(xs_ref[:, st], cr_ref[q]) + _dot(xs_ref[:, _im(st)], ci_ref[q])
                            + d_ref[:, ch] * zav[:, ch])

    tok = lambda w: pl.BlockSpec((ts, w), lambda b, j: (b * nts + j, 0))
    to_st, to_ch = _full((S5_BANDS, BAND_CH, BAND_ST)), _full((S5_BANDS, BAND_ST, BAND_CH))
    return _pcall(body, "s5_fwd", (nb, nts),
                  [tok(S5_WIDTH), to_st, to_st, _full((2, S5_N)), to_ch, to_ch, _full((1, S5_WIDTH))],
                  [tok(2 * S5_N), tok(S5_WIDTH)],
                  [_sds((t, 2 * S5_N), BF16), _sds((t, S5_WIDTH))],
                  scratch=[pltpu.VMEM((ts, 2 * S5_N), F32), pltpu.VMEM((2, S5_N), F32)],
                  )(za, *b_bands, lam, *c_bands, dskip)


def _hgrn_gates(zq, zf, lbh):
    sf = _sigmoid(zf)
    f = lbh + (1.0 - lbh) * sf
    sq = _sigmoid(zq)
    qa = zq * sq * QSCALE
    bc = _cumsum_rows(jnp.log(f))
    bm = bc[CHUNK // 2 - 1:CHUNK // 2, :]
    bl = bc[CHUNK - 1:CHUNK, :]
    return sf, f, sq, qa, bc, bm, bl


def _hgrn_fwd(zh, lb, nb, seq):
    nc = seq // CHUNK

    def body(zh_ref, lb_ref, o_ref, sts_ref, st_ref):
        @pl.when(pl.program_id(0) == 0)
        def _():
            st_ref[...] = jnp.zeros_like(st_ref)

        causal = (lax.broadcasted_iota(jnp.int32, (CHUNK, CHUNK), 0)
                  >= lax.broadcasted_iota(jnp.int32, (CHUNK, CHUNK), 1))
        for b in range(nb):
            for h in range(HG_HEADS):
                hs = slice(h * HG_HEAD, (h + 1) * HG_HEAD)
                zq = zh_ref[b, :, h * HG_HEAD:(h + 1) * HG_HEAD]
                zf = zh_ref[b, :, HG_WIDTH + h * HG_HEAD:HG_WIDTH + (h + 1) * HG_HEAD]
                zi = zh_ref[b, :, 2 * HG_WIDTH + h * HG_HEAD:2 * HG_WIDTH + (h + 1) * HG_HEAD]
                _, f, _, qa, bc, bm, bl = _hgrn_gates(zq, zf, lb_ref[:, hs])
                k = 1.0 - f
                qt = qa * jnp.exp(bc - bm)
                kt = k * jnp.exp(bm - bc)
                qb = qa * jnp.exp(bc)
                kd = k * jnp.exp(bl - bc)
                st = st_ref[b, h]
                sts_ref[b, 0, h] = st
                a = jnp.where(causal, _dot_nt(qt, kt), 0.0)
                o_ref[b, :, hs] = _dot(a, zi) + _dot_nt(qb, st)
                st_ref[b, h] = st * jnp.exp(bl) + _dot_tn(zi, kd)

    return _pcall(body, "hgrn_fwd", (nc,),
                  [pl.BlockSpec((nb, CHUNK, 4 * HG_WIDTH), lambda c: (0, c, 0)), _full((1, HG_WIDTH))],
                  [pl.BlockSpec((nb, CHUNK, HG_WIDTH), lambda c: (0, c, 0)),
                   pl.BlockSpec((nb, 1, HG_HEADS, HG_HEAD, HG_HEAD), lambda c: (0, c, 0, 0, 0))],
                  [_sds((nb, seq, HG_WIDTH)), _sds((nb, nc, HG_HEADS, HG_HEAD, HG_HEAD))],
                  scratch=[pltpu.VMEM((nb, HG_HEADS, HG_HEAD, HG_HEAD), F32)])(zh, lb)


def _head_rms(o):
    parts = []
    for h in range(HG_HEADS):
        oh = o[:, h * HG_HEAD:(h + 1) * HG_HEAD]
        r = lax.rsqrt(jnp.mean(oh * oh, axis=-1, keepdims=True) + EPS)
        parts.append(jnp.broadcast_to(r, oh.shape))
    return jnp.concatenate(parts, axis=1)


def _head_mean(v):
    parts = []
    for h in range(HG_HEADS):
        vh = v[:, h * HG_HEAD:(h + 1) * HG_HEAD]
        parts.append(jnp.broadcast_to(jnp.mean(vh, axis=-1, keepdims=True), vh.shape))
    return jnp.concatenate(parts, axis=1)


def _mix_fwd(x, y0, o, zh, zgt, w_glu, b_glu, gain, w_pa, w_pb, w_out, g_ffn, tm):
    t = x.shape[0]

    def body(x_ref, y0_ref, o_ref, zg_ref, zgt_ref, wglu_ref, bglu_ref, gain_ref, wpa_ref, wpb_ref, wout_ref,
             gffn_ref, x1_ref, u2_ref, pa_ref, pb_ref, ya2_ref, yb_ref):
        ya1 = _gelu(y0_ref[...])
        s = _sigmoid(_dot(ya1, wglu_ref[...]) + bglu_ref[...])
        ya2 = (ya1 * s).astype(BF16)
        ov = o_ref[...]
        zg = zg_ref[...]
        yb = (ov * _head_rms(ov) * gain_ref[...] * (zg * _sigmoid(zg))).astype(BF16)
        ya2_ref[...] = ya2
        yb_ref[...] = yb
        pa = jnp.dot(ya2, wpa_ref[...], preferred_element_type=F32)
        pb = jnp.dot(yb, wpb_ref[...], preferred_element_type=F32)
        pa_ref[...] = pa.astype(BF16)
        pb_ref[...] = pb.astype(BF16)
        m = (_sigmoid(zgt_ref[:, 0:D_MODEL].astype(F32)) * pa
             + _sigmoid(zgt_ref[:, D_MODEL:].astype(F32)) * pb)
        x1 = x_ref[...] + _dot(m, wout_ref[...])
        x1_ref[...] = x1
        r = lax.rsqrt(jnp.mean(x1 * x1, axis=-1, keepdims=True) + EPS)
        u2_ref[...] = (x1 * r * gffn_ref[...]).astype(BF16)

    row = lambda w: pl.BlockSpec((tm, w), lambda i: (i, 0))
    return _pcall(body, "mix_fwd", (t // tm,),
                  [row(D_MODEL), row(S5_WIDTH), row(HG_WIDTH), pl.BlockSpec((tm, HG_WIDTH), lambda i: (i, 3)),
                   row(2 * D_MODEL), _full((S5_WIDTH, S5_WIDTH)), _full((1, S5_WIDTH)), _full((1, HG_WIDTH)),
                   _full((S5_WIDTH, D_MODEL)), _full((HG_WIDTH, D_MODEL)), _full((D_MODEL, D_MODEL)),
                   _full((1, D_MODEL))],
                  [row(D_MODEL), row(D_MODEL), row(D_MODEL), row(D_MODEL), row(S5_WIDTH), row(HG_WIDTH)],
                  [_sds((t, D_MODEL)), _sds((t, D_MODEL), BF16), _sds((t, D_MODEL), BF16), _sds((t, D_MODEL), BF16),
                   _sds((t, S5_WIDTH), BF16), _sds((t, HG_WIDTH), BF16)],
                  )(x, y0, o, zh, zgt, w_glu, b_glu, gain, w_pa, w_pb, w_out, g_ffn)


FF_COLS = 256
FF_UP_TILE = 1408


def _ffn_up(u2, w_up, tm):
    t = u2.shape[0]
    n = 2 * D_FF

    def body(u_ref, w_ref, h_ref):
        h_ref[...] = jnp.dot(u_ref[...], w_ref[...], preferred_element_type=F32).astype(BF16)

    return _pcall(body, "ffn_up", (n // FF_UP_TILE, t // tm),
                  [pl.BlockSpec((tm, D_MODEL), lambda j, i: (i, 0)),
                   pl.BlockSpec((D_MODEL, FF_UP_TILE), lambda j, i: (0, j))],
                  pl.BlockSpec((tm, FF_UP_TILE), lambda j, i: (i, j)),
                  _sds((t, n), BF16))(u2, w_up)


HALO = 16


def _shift_matrix(tm):
    r = lax.broadcasted_iota(jnp.int32, (tm, tm), 0)
    c = lax.broadcasted_iota(jnp.int32, (tm, tm), 1)
    return jnp.where(r == c + 1, 1.0, 0.0).astype(BF16)


def _conv_cols(h_ref, halo_ref, valid, wc_ref, bc_ref, c0):
    cs = slice(c0, c0 + FF_COLS)
    cur = h_ref[:, cs].astype(F32)
    prev = jnp.where(valid, halo_ref[:, cs].astype(F32), 0.0)
    full = jnp.concatenate([prev, cur], axis=0)
    h1 = pltpu.roll(full, 1, axis=0)[HALO:]
    h2 = pltpu.roll(full, 2, axis=0)[HALO:]
    return h2 * wc_ref[0:1, cs] + h1 * wc_ref[1:2, cs] + cur * wc_ref[2:3, cs] + bc_ref[:, cs]


def _ffn_down_loss(h, x1, tgt, w_conv, b_conv, w_down, g_final, seq, tm):
    t = h.shape[0]
    tps = seq // tm
    n = 2 * D_FF

    def body(h_ref, halo_ref, x1_ref, tgt_ref, wc_ref, bc_ref, wd_ref, gf_ref,
             hc_ref, a_ref, dx2_ref, dx2b_ref, loss_ref, dgf_ref):
        i = pl.program_id(0)

        @pl.when(i == 0)
        def _():
            loss_ref[...] = jnp.zeros_like(loss_ref)
            dgf_ref[...] = jnp.zeros_like(dgf_ref)

        valid = (i % tps) != 0
        x2 = x1_ref[...]
        for j in range(D_FF // FF_COLS):
            gate = _conv_cols(h_ref, halo_ref, valid, wc_ref, bc_ref, j * FF_COLS)
            val = _conv_cols(h_ref, halo_ref, valid, wc_ref, bc_ref, D_FF + j * FF_COLS)
            hc_ref[:, j * FF_COLS:(j + 1) * FF_COLS] = gate.astype(BF16)
            hc_ref[:, D_FF + j * FF_COLS:D_FF + (j + 1) * FF_COLS] = val.astype(BF16)
            a = (gate * _sigmoid(gate) * val).astype(BF16)
            a_ref[:, j * FF_COLS:(j + 1) * FF_COLS] = a
            x2 = x2 + jnp.dot(a, wd_ref[j * FF_COLS:(j + 1) * FF_COLS, :], preferred_element_type=F32)
        r = lax.rsqrt(jnp.mean(x2 * x2, axis=-1, keepdims=True) + EPS)
        xn = x2 * r
        g = gf_ref[...]
        e = xn * g - tgt_ref[...]
        loss_ref[...] += (0.5 / D_MODEL) * jnp.sum(e * e).reshape(1, 1)
        dy = e * (1.0 / D_MODEL)
        dgf_ref[...] += jnp.sum(dy * xn, axis=0, keepdims=True)
        dxn = dy * g
        dx2 = r * (dxn - xn * jnp.mean(dxn * xn, axis=-1, keepdims=True))
        dx2_ref[...] = dx2
        dx2b_ref[...] = dx2.astype(BF16)

    row = lambda w: pl.BlockSpec((tm, w), lambda i: (i, 0))
    halo = pl.BlockSpec((HALO, n), lambda i: (jnp.maximum(i * (tm // HALO) - 1, 0), 0))
    return _pcall(body, "ffn_down_loss", (t // tm,),
                  [row(n), halo, row(D_MODEL), row(D_MODEL), _full((CONV_W, n)), _full((1, n)),
                   _full((D_FF, D_MODEL)), _full((1, D_MODEL))],
                  [row(n), row(D_FF), row(D_MODEL), row(D_MODEL), _full((1, 1)), _full((1, D_MODEL))],
                  [_sds((t, n), BF16), _sds((t, D_FF), BF16), _sds((t, D_MODEL)), _sds((t, D_MODEL), BF16),
                   _sds((1, 1)), _sds((1, D_MODEL))],
                  )(h, h, x1, tgt, w_conv, b_conv, w_down, g_final)


def _wgrad(a, b, name, tn, out_dtype=F32, transpose_out=False, band=None, after=None):
    t, m = a.shape
    n = b.shape[1] if band is None else band
    nbands = 1 if band is None else b.shape[1] // band
    after = b if after is None else after

    def body(a_ref, b_ref, after_ref, o_ref):
        r = _dot_tn(a_ref[...], b_ref[...])
        o_ref[...] = (r.T if transpose_out else r).astype(out_dtype)

    if transpose_out:
        out_spec, out_shape = pl.BlockSpec((n, tn), lambda i: (0, i)), _sds((n, m), out_dtype)
    else:
        out_spec, out_shape = pl.BlockSpec((tn, n), lambda i: (i, 0)), _sds((m, n), out_dtype)
    return _pcall(body, name, (m // tn,),
                  [pl.BlockSpec((t, tn), lambda i: (0, i)), pl.BlockSpec((t, n), lambda i: (0, i % nbands)),
                   pl.BlockSpec(memory_space=pl.ANY)],
                  out_spec, out_shape)(a, b, after)


def _ffn_bwd_act(dx2b, hc, w_down, tm):
    t = hc.shape[0]
    n = 2 * D_FF

    def body(dx2_ref, hc_ref, wd_ref, dhc_ref, dbc_ref):
        @pl.when(pl.program_id(0) == 0)
        def _():
            dbc_ref[...] = jnp.zeros_like(dbc_ref)

        dx2 = dx2_ref[...]
        for j in range(D_FF // FF_COLS):
            gs = slice(j * FF_COLS, (j + 1) * FF_COLS)
            vs = slice(D_FF + j * FF_COLS, D_FF + (j + 1) * FF_COLS)
            gate = hc_ref[:, gs].astype(F32)
            val = hc_ref[:, vs].astype(F32)
            da = _dot_nt(dx2, wd_ref[gs, :])
            sg = _sigmoid(gate)
            dgate = da * val * (sg * (1.0 + gate * (1.0 - sg)))
            dval = da * (gate * sg)
            dhc_ref[:, gs] = dgate.astype(BF16)
            dhc_ref[:, vs] = dval.astype(BF16)
            dbc_ref[:, gs] += jnp.sum(dgate, axis=0, keepdims=True)
            dbc_ref[:, vs] += jnp.sum(dval, axis=0, keepdims=True)

    row = lambda w: pl.BlockSpec((tm, w), lambda i: (i, 0))
    return _pcall(body, "ffn_bwd_act", (t // tm,),
                  [row(D_MODEL), row(n), _full((D_FF, D_MODEL))],
                  [row(n), _full((1, n))],
                  [_sds((t, n), BF16), _sds((1, n))],
                  )(dx2b, hc, w_down)


def _ffn_bwd_up(dhc, h, dx2, x1, w_conv, w_up, g_ffn, seq, tm):
    t = dhc.shape[0]
    tps = seq // tm
    n = 2 * D_FF
    last = t // HALO - 1

    def body(dhc_ref, halo_ref, h_ref, dx2_ref, x1_ref, wc_ref, wu_ref, gf_ref,
             dh_ref, dx1_ref, dx1b_ref, dgf_ref, dwc_ref):
        i = pl.program_id(0)

        @pl.when(i == 0)
        def _():
            dgf_ref[...] = jnp.zeros_like(dgf_ref)
            dwc_ref[...] = jnp.zeros_like(dwc_ref)

        valid = ((i + 1) % tps) != 0
        du2 = jnp.zeros((tm, D_MODEL), F32)
        for j in range(n // FF_COLS):
            cs = slice(j * FF_COLS, (j + 1) * FF_COLS)
            cur = dhc_ref[:, cs].astype(F32)
            nxt = jnp.where(valid, halo_ref[:, cs].astype(F32), 0.0)
            full = jnp.concatenate([cur, nxt], axis=0)
            d1 = pltpu.roll(full, tm + HALO - 1, axis=0)[:tm]
            d2 = pltpu.roll(full, tm + HALO - 2, axis=0)[:tm]
            dh = (cur * wc_ref[2:3, cs] + d1 * wc_ref[1:2, cs] + d2 * wc_ref[0:1, cs]).astype(BF16)
            dh_ref[:, cs] = dh
            du2 = du2 + _dot_nt(dh, wu_ref[:, cs])
            hv = h_ref[:, cs].astype(F32)
            dwc_ref[0:1, cs] += jnp.sum(hv * d2, axis=0, keepdims=True)
            dwc_ref[1:2, cs] += jnp.sum(hv * d1, axis=0, keepdims=True)
            dwc_ref[2:3, cs] += jnp.sum(hv * cur, axis=0, keepdims=True)
        x1 = x1_ref[...]
        r = lax.rsqrt(jnp.mean(x1 * x1, axis=-1, keepdims=True) + EPS)
        xn = x1 * r
        dgf_ref[...] += jnp.sum(du2 * xn, axis=0, keepdims=True)
        dxn = du2 * gf_ref[...]
        dx1 = dx2_ref[...] + r * (dxn - xn * jnp.mean(dxn * xn, axis=-1, keepdims=True))
        dx1_ref[...] = dx1
        dx1b_ref[...] = dx1.astype(BF16)

    row = lambda w: pl.BlockSpec((tm, w), lambda i: (i, 0))
    halo = pl.BlockSpec((HALO, n), lambda i: (jnp.minimum((i + 1) * (tm // HALO), last), 0))
    return _pcall(body, "ffn_bwd_up", (t // tm,),
                  [row(n), halo, row(n), row(D_MODEL), row(D_MODEL), _full((CONV_W, n)), _full((D_MODEL, n)),
                   _full((1, D_MODEL))],
                  [row(n), row(D_MODEL), row(D_MODEL), _full((1, D_MODEL)), _full((CONV_W, n))],
                  [_sds((t, n), BF16), _sds((t, D_MODEL)), _sds((t, D_MODEL), BF16), _sds((1, D_MODEL)),
                   _sds((CONV_W, n))],
                  )(dhc, dhc, h, dx2, x1, w_conv, w_up, g_ffn)


def _mix_bwd(dx1, y0, o, zh, zgt, pa, pb, w_glu, b_glu, gain, w_pa, w_pb, w_out, tm):
    t = dx1.shape[0]

    def body(dx1_ref, y0_ref, o_ref, zg_ref, zgt_ref, pa_ref, pb_ref, wglu_ref, bglu_ref, gain_ref, wpa_ref,
             wpb_ref, wout_ref,
             dy0_ref, do_ref, dzg_ref, dzgt_ref, m_ref, dpa_ref, dpb_ref, ya1_ref, dpre_ref, dbglu_ref, dgain_ref):
        @pl.when(pl.program_id(0) == 0)
        def _():
            dbglu_ref[...] = jnp.zeros_like(dbglu_ref)
            dgain_ref[...] = jnp.zeros_like(dgain_ref)

        dm = _dot_nt(dx1_ref[...], wout_ref[...])
        sga = _sigmoid(zgt_ref[:, 0:D_MODEL].astype(F32))
        sgb = _sigmoid(zgt_ref[:, D_MODEL:].astype(F32))
        pa = pa_ref[...].astype(F32)
        pb = pb_ref[...].astype(F32)
        m_ref[...] = (sga * pa + sgb * pb).astype(BF16)
        dzgt_ref[:, 0:D_MODEL] = (dm * pa * sga * (1.0 - sga)).astype(BF16)
        dzgt_ref[:, D_MODEL:] = (dm * pb * sgb * (1.0 - sgb)).astype(BF16)
        dpa = (dm * sga).astype(BF16)
        dpb = (dm * sgb).astype(BF16)
        dpa_ref[...] = dpa
        dpb_ref[...] = dpb
        dya2 = _dot_nt(dpa, wpa_ref[...])
        dyb = _dot_nt(dpb, wpb_ref[...])
        y0 = y0_ref[...]
        ya1 = _gelu(y0)
        ya1_ref[...] = ya1.astype(BF16)
        s = _sigmoid(_dot(ya1, wglu_ref[...]) + bglu_ref[...])
        dpre = dya2 * ya1 * s * (1.0 - s)
        dpre_ref[...] = dpre.astype(BF16)
        dbglu_ref[...] += jnp.sum(dpre, axis=0, keepdims=True)
        dya1 = dya2 * s + _dot_nt(dpre, wglu_ref[...])
        dy0_ref[...] = dya1 * _gelu_grad(y0)
        ov = o_ref[...]
        zg = zg_ref[...]
        oh = ov * _head_rms(ov)
        on = oh * gain_ref[...]
        sz = _sigmoid(zg)
        dzg_ref[...] = (dyb * on * (sz * (1.0 + zg * (1.0 - sz)))).astype(BF16)
        don = dyb * (zg * sz)
        dgain_ref[...] += jnp.sum(don * oh, axis=0, keepdims=True)
        doh = don * gain_ref[...]
        do_ref[...] = _head_rms(ov) * (doh - oh * _head_mean(doh * oh))

    row = lambda w: pl.BlockSpec((tm, w), lambda i: (i, 0))
    return _pcall(body, "mix_bwd", (t // tm,),
                  [row(D_MODEL), row(S5_WIDTH), row(HG_WIDTH), pl.BlockSpec((tm, HG_WIDTH), lambda i: (i, 3)),
                   row(2 * D_MODEL), row(D_MODEL), row(D_MODEL), _full((S5_WIDTH, S5_WIDTH)), _full((1, S5_WIDTH)),
                   _full((1, HG_WIDTH)), _full((S5_WIDTH, D_MODEL)), _full((HG_WIDTH, D_MODEL)),
                   _full((D_MODEL, D_MODEL))],
                  [row(S5_WIDTH), row(HG_WIDTH), row(HG_WIDTH), row(2 * D_MODEL), row(D_MODEL), row(D_MODEL),
                   row(D_MODEL), row(S5_WIDTH), row(S5_WIDTH), _full((1, S5_WIDTH)), _full((1, HG_WIDTH))],
                  [_sds((t, S5_WIDTH)), _sds((t, HG_WIDTH)), _sds((t, HG_WIDTH), BF16), _sds((t, 2 * D_MODEL), BF16),
                   _sds((t, D_MODEL), BF16), _sds((t, D_MODEL), BF16), _sds((t, D_MODEL), BF16),
                   _sds((t, S5_WIDTH), BF16), _sds((t, S5_WIDTH), BF16), _sds((1, S5_WIDTH)), _sds((1, HG_WIDTH))],
                  )(dx1, y0, o, zh, zgt, pa, pb, w_glu, b_glu, gain, w_pa, w_pb, w_out)


def _s5_bwd(dy0, za, xs, c_bands, b_bands, lam, dskip, nb, seq, ts):
    t = za.shape[0]
    nts = seq // ts

    def body(dy0_ref, za_ref, xs_ref, halo_ref, cr_ref, ci_ref, br_ref, bi_ref, lam_ref, d_ref,
             dza_ref, a_ref, dlam_ref, dd_ref, acc_ref, st_ref):
        b, j = pl.program_id(0), pl.program_id(1)

        @pl.when((b == 0) & (j == 0))
        def _():
            dlam_ref[...] = jnp.zeros_like(dlam_ref)
            dd_ref[...] = jnp.zeros_like(dd_ref)

        @pl.when(j == 0)
        def _():
            st_ref[...] = jnp.zeros_like(st_ref)

        dy0 = dy0_ref[...]
        for q in range(S5_BANDS):
            ch, st = _band(q)
            acc_ref[:, st] = _dot(dy0[:, ch], cr_ref[q])
            acc_ref[:, _im(st)] = _dot(dy0[:, ch], ci_ref[q])
        _complex_scan(acc_ref, lam_ref, st_ref, ts, reverse=True)
        a_ref[...] = acc_ref[...].astype(BF16)
        shift = _shift_matrix(ts)
        first = jnp.where(j == nts - 1, 0.0, halo_ref[HALO - 1:HALO, :].astype(F32))
        top = lax.broadcasted_iota(jnp.int32, (SUBLANES, S5_LANES), 0) == 0

        def shifted(cols):
            xp = jnp.dot(shift, xs_ref[:, cols], preferred_element_type=F32)
            return jnp.concatenate([xp[:SUBLANES] + jnp.where(top, first[:, cols], 0.0), xp[SUBLANES:]], axis=0)

        for cc in range(S5_N // S5_LANES):
            re = slice(cc * S5_LANES, (cc + 1) * S5_LANES)
            ar, ai, xr, xi = acc_ref[:, re], acc_ref[:, _im(re)], shifted(re), shifted(_im(re))
            dlam_ref[0:1, re] += jnp.sum(ar * xr + ai * xi, axis=0, keepdims=True)
            dlam_ref[1:2, re] += jnp.sum(ai * xr - ar * xi, axis=0, keepdims=True)
        for q in range(S5_BANDS):
            ch, st = _band(q)
            dza_ref[:, ch] = (_dot(a_ref[:, st], br_ref[q]) + _dot(a_ref[:, _im(st)], bi_ref[q])
                              + d_ref[:, ch] * dy0[:, ch]).astype(BF16)
        dd_ref[...] += jnp.sum(dy0 * za_ref[...], axis=0, keepdims=True)

    tile = lambda b, j: b * nts + (nts - 1 - j)
    tok = lambda w: pl.BlockSpec((ts, w), lambda b, j: (tile(b, j), 0))
    halo = pl.BlockSpec((HALO, 2 * S5_N), lambda b, j: (jnp.maximum(tile(b, j) * (ts // HALO) - 1, 0), 0))
    to_st, to_ch = _full((S5_BANDS, BAND_CH, BAND_ST)), _full((S5_BANDS, BAND_ST, BAND_CH))
    return _pcall(body, "s5_bwd", (nb, nts),
                  [tok(S5_WIDTH), tok(S5_WIDTH), tok(2 * S5_N), halo, to_st, to_st, to_ch, to_ch,
                   _full((2, S5_N)), _full((1, S5_WIDTH))],
                  [tok(S5_WIDTH), tok(2 * S5_N), _full((2, S5_N)), _full((1, S5_WIDTH))],
                  [_sds((t, S5_WIDTH), BF16), _sds((t, 2 * S5_N), BF16), _sds((2, S5_N)), _sds((1, S5_WIDTH))],
                  scratch=[pltpu.VMEM((ts, 2 * S5_N), F32), pltpu.VMEM((2, S5_N), F32)],
                  )(dy0, za, xs, xs, *c_bands, *b_bands, lam, dskip)


def _hgrn_bwd(zh, do, sts, lb, nb, seq):
    nc = seq // CHUNK

    def body(zh_ref, do_ref, sts_ref, lb_ref, dz_ref, dlb_ref, dst_ref):
        @pl.when(pl.program_id(0) == 0)
        def _():
            dst_ref[...] = jnp.zeros_like(dst_ref)
            dlb_ref[...] = jnp.zeros_like(dlb_ref)

        row = lax.broadcasted_iota(jnp.int32, (CHUNK, CHUNK), 0)
        causal = row >= lax.broadcasted_iota(jnp.int32, (CHUNK, CHUNK), 1)
        last_row = lax.broadcasted_iota(jnp.int32, (CHUNK, HG_HEAD), 0) == CHUNK - 1
        for b in range(nb):
            for h in range(HG_HEADS):
                hs = slice(h * HG_HEAD, (h + 1) * HG_HEAD)
                zq = zh_ref[b, :, h * HG_HEAD:(h + 1) * HG_HEAD]
                zf = zh_ref[b, :, HG_WIDTH + h * HG_HEAD:HG_WIDTH + (h + 1) * HG_HEAD]
                zi = zh_ref[b, :, 2 * HG_WIDTH + h * HG_HEAD:2 * HG_WIDTH + (h + 1) * HG_HEAD]
                lbh = lb_ref[:, hs]
                sf, f, sq, qa, bc, bm, bl = _hgrn_gates(zq, zf, lbh)
                k = 1.0 - f
                e_qt = jnp.exp(bc - bm)
                e_kt = jnp.exp(bm - bc)
                e_b = jnp.exp(bc)
                e_kd = jnp.exp(bl - bc)
                e_l = jnp.exp(bl)
                qt, kt, qb, kd = qa * e_qt, k * e_kt, qa * e_b, k * e_kd
                a = jnp.where(causal, _dot_nt(qt, kt), 0.0)
                st = sts_ref[b, 0, h]
                dst = dst_ref[b, h]
                dov = do_ref[b, :, hs]
                da = jnp.where(causal, _dot_nt(dov, zi), 0.0)
                dqt = _hdot(da, kt)
                dkt = _hdot_tn(da, qt)
                dqb = _dot(dov, st)
                di = _dot_tn(a, dov) + _dot_nt(kd, dst)
                dkd = _dot(zi, dst)
                de_l = jnp.sum(dst * st, axis=0, keepdims=True)
                dst_ref[b, h] = dst * e_l + _dot_tn(dov, qb)
                dqa = dqt * e_qt + dqb * e_b
                dk = dkt * e_kt + dkd * e_kd
                dbl = jnp.sum(dkd * kd, axis=0, keepdims=True) + de_l * e_l
                db = dqt * qt - dkt * kt + dqb * qb - dkd * kd + jnp.where(last_row, dbl, 0.0)
                df = _cumsum_rows(db, reverse=True) / f - dk
                dzq = dqa * QSCALE * (sq * (1.0 + zq * (1.0 - sq)))
                dzf = df * (1.0 - lbh) * sf * (1.0 - sf)
                dz_ref[b, :, h * HG_HEAD:(h + 1) * HG_HEAD] = dzq.astype(BF16)
                dz_ref[b, :, HG_WIDTH + h * HG_HEAD:HG_WIDTH + (h + 1) * HG_HEAD] = dzf.astype(BF16)
                dz_ref[b, :, 2 * HG_WIDTH + h * HG_HEAD:2 * HG_WIDTH + (h + 1) * HG_HEAD] = di.astype(BF16)
                dlb_ref[:, hs] += jnp.sum(df * (1.0 - sf), axis=0, keepdims=True)

    rev = lambda c: nc - 1 - c
    return _pcall(body, "hgrn_bwd", (nc,),
                  [pl.BlockSpec((nb, CHUNK, 4 * HG_WIDTH), lambda c: (0, rev(c), 0)),
                   pl.BlockSpec((nb, CHUNK, HG_WIDTH), lambda c: (0, rev(c), 0)),
                   pl.BlockSpec((nb, 1, HG_HEADS, HG_HEAD, HG_HEAD), lambda c: (0, rev(c), 0, 0, 0)),
                   _full((1, HG_WIDTH))],
                  [pl.BlockSpec((nb, CHUNK, 3 * HG_WIDTH), lambda c: (0, rev(c), 0)), _full((1, HG_WIDTH))],
                  [_sds((nb, seq, 3 * HG_WIDTH), BF16), _sds((1, HG_WIDTH))],
                  scratch=[pltpu.VMEM((nb, HG_HEADS, HG_HEAD, HG_HEAD), F32)])(zh, do, sts, lb)


def _in_proj_bwd(dza, dzh, dzg, dzgt, dx1, x, g_mix, w_in, tm):
    t = x.shape[0]

    def body(dza_ref, dzh_ref, dzg_ref, dzgt_ref, dx1_ref, x_ref, g_ref, w_ref, dz_ref, dx_ref, dg_ref):
        @pl.when(pl.program_id(0) == 0)
        def _():
            dg_ref[...] = jnp.zeros_like(dg_ref)

        c1, c2, c3 = S5_WIDTH, S5_WIDTH + 3 * HG_WIDTH, S5_WIDTH + 4 * HG_WIDTH
        dz_ref[:, 0:c1] = dza_ref[...]
        dz_ref[:, c1:c2] = dzh_ref[...]
        dz_ref[:, c2:c3] = dzg_ref[...]
        dz_ref[:, c3:] = dzgt_ref[...]
        du = _dot_nt(dz_ref[...], w_ref[...])
        xv = x_ref[...]
        r = lax.rsqrt(jnp.mean(xv * xv, axis=-1, keepdims=True) + EPS)
        xn = xv * r
        dg_ref[...] += jnp.sum(du * xn, axis=0, keepdims=True)
        dxn = du * g_ref[...]
        dx_ref[...] = dx1_ref[...] + r * (dxn - xn * jnp.mean(dxn * xn, axis=-1, keepdims=True))

    row = lambda w: pl.BlockSpec((tm, w), lambda i: (i, 0))
    return _pcall(body, "in_proj_bwd", (t // tm,),
                  [row(S5_WIDTH), row(3 * HG_WIDTH), row(HG_WIDTH), row(2 * D_MODEL), row(D_MODEL), row(D_MODEL),
                   _full((1, D_MODEL)), _full((D_MODEL, N_IN))],
                  [row(N_IN), row(D_MODEL), _full((1, D_MODEL))],
                  [_sds((t, N_IN), BF16), _sds((t, D_MODEL)), _sds((1, D_MODEL))],
                  )(dza, dzh, dzg, dzgt, dx1, x, g_mix, w_in)


def _tie(*arrays):
    return jnp.zeros((SUBLANES, 128), F32) + sum(a.reshape(-1)[0].astype(F32) for a in arrays)


def _after(value, token):
    return value + token[0, 0]


def _local_step(x3, tgt3, weights, sp, emit, emit_small):
    nb, seq, _ = x3.shape
    t = nb * seq
    tm = _token_tile(seq)
    x = x3.reshape(t, D_MODEL)
    tgt = tgt3.reshape(t, D_MODEL)
    row = lambda v: v.reshape(1, -1)

    a_re = sp["s5_a_re"].reshape(S5_N, 1)
    a_im = sp["s5_a_im"].reshape(S5_N, 1)
    ldt = jnp.repeat(sp["s5_log_dt"].reshape(S5_GROUPS), S5_STATE).reshape(S5_N, 1)
    b_re = sp["s5_b_re"].reshape(S5_N, S5_GROUP)
    b_im = sp["s5_b_im"].reshape(S5_N, S5_GROUP)
    lr, li, bb_re, bb_im, lb = _params_fwd(a_re, a_im, ldt, b_re, b_im, sp["hg_lb_logits"])
    lam = jnp.concatenate([lr.reshape(1, S5_N), li.reshape(1, S5_N)], axis=0)
    gps = lambda m: m.reshape(S5_GROUPS, S5_STATE, S5_GROUP)
    swap = lambda m: m.transpose(0, 2, 1)
    b_to_st = (_band_blocks(swap(gps(bb_re))), _band_blocks(swap(gps(bb_im))))
    b_to_ch = (_band_blocks(gps(bb_re)), _band_blocks(gps(bb_im)))
    c_to_ch = (_band_blocks(swap(sp["s5_c_re"])), _band_blocks(swap(-sp["s5_c_im"])))
    c_to_st = (_band_blocks(sp["s5_c_re"]), _band_blocks(-sp["s5_c_im"]))

    g_mix, g_ffn, g_final = row(sp["g_mix"]), row(sp["g_ffn"]), row(sp["g_final"])
    b_glu, gain, dskip, b_conv = row(sp["b_glu"]), row(sp["hg_norm_gain"]), row(sp["s5_d"]), row(sp["b_conv"])

    w_in = weights("in", lam, *b_to_st, *b_to_ch, *c_to_ch, *c_to_st)["w_in"]
    u, za, zh, zgt = _in_proj(x, g_mix, w_in, tm)
    xs, y0 = _s5_fwd(za, b_to_st, lam, c_to_ch, dskip, nb, seq, tm)
    o3, sts = _hgrn_fwd(zh.reshape(nb, seq, 4 * HG_WIDTH), lb, nb, seq)
    o = o3.reshape(t, HG_WIDTH)
    wm = weights("mix", y0, o3)
    x1, u2, pa, pb, ya2, yb = _mix_fwd(x, y0, o, zh, zgt, wm["w_glu"], b_glu, gain, wm["w_pa"], wm["w_pb"],
                                       wm["w_out"], g_ffn, tm)
    wf = weights("ffn", u2)
    h = _ffn_up(u2, wf["w_up"], min(4 * tm, t))
    hc, a, dx2, dx2b, loss, dg_final = _ffn_down_loss(h, x1, tgt, wf["w_conv"], b_conv, wf["w_down"], g_final,
                                                      seq, tm)

    wgrad = functools.partial(_wgrad, tn=256, out_dtype=BF16)
    dhc, db_conv = _ffn_bwd_act(dx2b, hc, wf["w_down"], tm)
    sent = emit({"w_down": wgrad(a, dx2b, "dw_down")})
    dh, dx1, dx1b, dg_ffn, dw_conv = _ffn_bwd_up(dhc, h, dx2, x1, wf["w_conv"], wf["w_up"], _after(g_ffn, sent),
                                                 seq, tm)
    sent = emit({"w_up": wgrad(dh, u2, "dw_up", transpose_out=True), "w_conv": dw_conv})
    (dy0, do, dzg, dzgt, m, dpa, dpb, ya1, dpre, db_glu, dgain) = _mix_bwd(
        dx1b, y0, o, zh, zgt, pa, pb, wm["w_glu"], _after(b_glu, sent), gain, wm["w_pa"], wm["w_pb"], wm["w_out"], tm)
    sent = emit({"w_out": wgrad(m, dx1b, "dw_out"), "w_pa": wgrad(ya2, dpa, "dw_pa"),
                 "w_pb": wgrad(yb, dpb, "dw_pb"), "w_glu": wgrad(ya1, dpre, "dw_glu")})
    dzh3, dlb = _hgrn_bwd(zh.reshape(nb, seq, 4 * HG_WIDTH), do.reshape(nb, seq, HG_WIDTH), sts, _after(lb, sent),
                          nb, seq)
    dza, a_s5, dlam, dd = _s5_bwd(dy0, za, xs, c_to_st, b_to_ch, lam, dskip, nb, seq, tm)
    dz, dx, dg_mix = _in_proj_bwd(dza, dzh3.reshape(t, 3 * HG_WIDTH), dzg, dzgt, dx1, x, g_mix, w_in, tm)
    sent = emit({"w_in": wgrad(dz, u, "dw_in", transpose_out=True)})

    band = HG_HEAD
    dbb_band = _wgrad(a_s5, za, "dbb_s5", 512, band=band, after=sent)
    dc_band = _wgrad(xs, dy0, "dc_s5", 512, band=band, after=sent)
    dbb_re = _diag_blocks(dbb_band[:S5_N], S5_STATE, S5_GROUP).reshape(S5_N, S5_GROUP)
    dbb_im = _diag_blocks(dbb_band[S5_N:], S5_STATE, S5_GROUP).reshape(S5_N, S5_GROUP)
    dc_re = _diag_blocks(dc_band[:S5_N], S5_STATE, S5_GROUP).transpose(0, 2, 1)
    dc_im = -_diag_blocks(dc_band[S5_N:], S5_STATE, S5_GROUP).transpose(0, 2, 1)
    da_re, da_im, dldt, db_re, db_im, dlogits = _params_bwd(
        a_re, a_im, ldt, b_re, b_im, sp["hg_lb_logits"],
        dlam[0].reshape(S5_N, 1), dlam[1].reshape(S5_N, 1), dbb_re, dbb_im, dlb)
    emit_small({"g_mix": dg_mix, "s5_a_re": da_re.reshape(S5_GROUPS, S5_STATE),
                "s5_a_im": da_im.reshape(S5_GROUPS, S5_STATE), "s5_log_dt": dldt, "s5_b_re": gps(db_re),
                "s5_b_im": gps(db_im), "s5_c_re": dc_re, "s5_c_im": dc_im, "s5_d": dd, "b_glu": db_glu,
                "hg_lb_logits": dlogits, "hg_norm_gain": dgain, "g_ffn": dg_ffn, "b_conv": db_conv,
                "g_final": dg_final, "loss": loss})
    return dx.reshape(nb, seq, D_MODEL)


def _mesh_peers():
    x, y, c = lax.axis_index("x"), lax.axis_index("y"), lax.axis_index("c")
    peers = []
    for k in range(1, N_DEV):
        px, py, pc = (1 - x if k & 4 else x), (1 - y if k & 2 else y), (1 - c if k & 1 else c)
        peers.append((k, (px, py, pc), 4 * px + 2 * py + pc))
    return 4 * x + 2 * y + c, peers


_HBM = pl.BlockSpec(memory_space=pltpu.HBM)
_SEM = pl.BlockSpec(memory_space=pltpu.SEMAPHORE)


def _exchange_start(name, operands, after):
    n = len(operands)
    me = 4 * lax.axis_index("x") + 2 * lax.axis_index("y") + lax.axis_index("c")
    flags = [per_peer for _, per_peer in operands]
    srcs, lands = [], []
    for arr, per_peer in operands:
        own = lax.dynamic_index_in_dim(arr, me, 0, keepdims=True) if per_peer else arr[None]
        land = lax.dynamic_update_slice_in_dim(lax.empty((N_DEV,) + own.shape[1:], arr.dtype), own, me, 0)
        srcs.append(pltpu.with_memory_space_constraint(arr, pltpu.HBM))
        lands.append(pltpu.with_memory_space_constraint(land, pltpu.HBM))
    copies = (N_DEV - 1) * n

    def body(*refs):
        src_refs, land_refs = refs[:n], refs[n:2 * n]
        send_sems, recv_sems = refs[2 * n + 1], refs[2 * n + 2]
        token = refs[-1]
        my_slab, peers = _mesh_peers()
        for k, peer, slab in peers:
            for i in range(n):
                s = (k - 1) * n + i
                pltpu.make_async_remote_copy(
                    src_ref=src_refs[i].at[slab] if flags[i] else src_refs[i], dst_ref=land_refs[i].at[my_slab],
                    send_sem=send_sems.at[s], recv_sem=recv_sems.at[s], device_id=peer,
                    device_id_type=pl.DeviceIdType.MESH).start()
        token[...] = jnp.zeros_like(token)

    outs = pl.pallas_call(
        body, name=name,
        out_shape=(pltpu.SemaphoreType.DMA((copies,)), pltpu.SemaphoreType.DMA((copies,)),
                   *[pltpu.HBM(a.shape, a.dtype) for a in srcs], *[pltpu.HBM(a.shape, a.dtype) for a in lands],
                   _sds((SUBLANES, 128))),
        in_specs=[_HBM] * (2 * n) + [pl.BlockSpec(memory_space=pl.ANY)],
        out_specs=(_SEM, _SEM, *[_HBM] * (2 * n), pl.BlockSpec(memory_space=pltpu.VMEM)),
        input_output_aliases={i: 2 + i for i in range(2 * n)},
        compiler_params=pltpu.CompilerParams(has_side_effects=pltpu.SideEffectType.DATAFLOW_SIDE_EFFECTING),
    )(*srcs, *lands, after)
    state = (flags, outs[0], outs[1], outs[2:2 + n], outs[2 + n:2 + 2 * n])
    return state, outs[-1]


def _exchange_wait(name, state, *after):
    flags, send_sems, recv_sems, srcs, lands = state
    n = len(flags)

    def body(*refs):
        src_refs, land_refs = refs[:n], refs[n:2 * n]
        send_ref, recv_ref = refs[2 * n], refs[2 * n + 1]
        _, peers = _mesh_peers()
        for k, peer, slab in peers:
            for i in range(n):
                s = (k - 1) * n + i
                copy = pltpu.make_async_remote_copy(
                    src_ref=src_refs[i].at[slab] if flags[i] else src_refs[i], dst_ref=land_refs[i].at[slab],
                    send_sem=send_ref.at[s], recv_sem=recv_ref.at[s], device_id=peer,
                    device_id_type=pl.DeviceIdType.MESH)
                copy.wait_send()
                copy.wait_recv()

    outs = pl.pallas_call(
        body, name=name,
        out_shape=(*[pltpu.HBM(a.shape, a.dtype) for a in srcs], *[pltpu.HBM(a.shape, a.dtype) for a in lands]),
        in_specs=[_HBM] * (2 * n) + [_SEM, _SEM] + [pl.BlockSpec(memory_space=pl.ANY)] * len(after),
        out_specs=tuple([_HBM] * (2 * n)),
        input_output_aliases={i: i for i in range(2 * n)},
        compiler_params=pltpu.CompilerParams(has_side_effects=pltpu.SideEffectType.DATAFLOW_SIDE_EFFECTING),
    )(*srcs, *lands, send_sems, recv_sems, *after)
    return list(outs[n:])


def _join_cols(parts, name, tr):
    _, r, c = parts.shape

    def body(p_ref, o_ref):
        for j in range(N_DEV):
            o_ref[:, j * c:(j + 1) * c] = p_ref[j]

    return _pcall(body, name, (r // tr,), [pl.BlockSpec((N_DEV, tr, c), lambda i: (0, i, 0))],
                  pl.BlockSpec((tr, N_DEV * c), lambda i: (i, 0)), _sds((r, N_DEV * c), parts.dtype))(parts)


def _split_cols(full, name, tr):
    r, c = full.shape[0], full.shape[1] // N_DEV

    def body(f_ref, o_ref):
        for j in range(N_DEV):
            o_ref[j] = f_ref[:, j * c:(j + 1) * c]

    return _pcall(body, name, (r // tr,), [pl.BlockSpec((tr, N_DEV * c), lambda i: (i, 0))],
                  pl.BlockSpec((N_DEV, tr, c), lambda i: (0, i, 0)), _sds((N_DEV, r, c), full.dtype))(full)


def _adamw(parts, w, m, v, name, tile):
    _, rows, cols = w.shape

    def body(p_ref, w_ref, m_ref, v_ref, g_out, d_out, m_out, v_out):
        g = p_ref[0].astype(F32)
        for k in range(1, N_DEV):
            g = g + p_ref[k].astype(F32)
        m1 = ADAM_B1 * m_ref[0] + (1.0 - ADAM_B1) * g
        v1 = ADAM_B2 * v_ref[0] + (1.0 - ADAM_B2) * (g * g)
        m_hat = m1 / (1.0 - ADAM_B1 ** ADAM_STEP)
        v_hat = v1 / (1.0 - ADAM_B2 ** ADAM_STEP)
        g_out[0] = g
        d_out[0] = -ADAM_LR * (m_hat / (jnp.sqrt(v_hat) + ADAM_EPS) + ADAM_WD * w_ref[0])
        m_out[0] = m1
        v_out[0] = v1

    row = pl.BlockSpec((1, tile, cols), lambda i: (0, i, 0))
    return _pcall(body, name, (rows // tile,),
                  [pl.BlockSpec((N_DEV, tile, cols), lambda i: (0, i, 0)), row, row, row],
                  [row, row, row, row], [_sds((1, rows, cols))] * 4)(parts, w, m, v)


BIG = {
    "w_in": ((D_MODEL, N_IN // N_DEV), True, 256),
    "w_glu": ((S5_WIDTH // N_DEV, S5_WIDTH), False, S5_WIDTH // N_DEV),
    "w_pa": ((S5_WIDTH, D_MODEL // N_DEV), True, S5_WIDTH),
    "w_pb": ((HG_WIDTH, D_MODEL // N_DEV), True, HG_WIDTH),
    "w_out": ((D_MODEL // N_DEV, D_MODEL), False, D_MODEL // N_DEV),
    "w_up": ((D_MODEL, 2 * D_FF // N_DEV), True, 256),
    "w_conv": ((CONV_W, 2 * D_FF // N_DEV), True, CONV_W),
    "w_down": ((D_FF // N_DEV, D_MODEL), False, D_FF // N_DEV // 2),
}
UNALIGNED_COLS = ("w_in", "w_up", "w_conv")


def _join_shards(n, parts):
    (a, b), by_cols, _ = BIG[n]
    if not by_cols:
        return parts.reshape(N_DEV * a, b)
    if n in UNALIGNED_COLS:
        return _join_cols(parts, "join_" + n, min(a, 256))
    return parts.transpose(1, 0, 2).reshape(a, N_DEV * b)


def _split_shards(n, full):
    (a, b), by_cols, _ = BIG[n]
    if not by_cols:
        return full.reshape(N_DEV, a, b)
    if n in UNALIGNED_COLS:
        return _split_cols(full, "split_" + n, min(a, 256))
    return full.reshape(a, N_DEV, b).transpose(1, 0, 2)


SMALL_CORE = {
    "s5_b_re": (S5_GROUPS, S5_STATE, S5_GROUP), "s5_b_im": (S5_GROUPS, S5_STATE, S5_GROUP),
    "s5_c_re": (S5_GROUPS, S5_GROUP, S5_STATE), "s5_c_im": (S5_GROUPS, S5_GROUP, S5_STATE),
    "g_mix": (1, D_MODEL), "g_ffn": (1, D_MODEL), "g_final": (1, D_MODEL), "s5_d": (1, S5_WIDTH),
    "b_glu": (1, S5_WIDTH), "hg_norm_gain": (1, HG_WIDTH), "hg_lb_logits": (2, HG_WIDTH), "b_conv": (1, 2 * D_FF),
    "s5_log_dt": (1, S5_GROUPS), "s5_a_re": (S5_GROUPS, S5_STATE), "s5_a_im": (S5_GROUPS, S5_STATE), "loss": (1, 1),
}
BLOCK_ROWS = 32


def _small_rows():
    rows, r = {}, 0
    for n, core in SMALL_CORE.items():
        rows[n] = r
        r += BLOCK_ROWS if len(core) == 3 else -(-math.prod(core) // PACK_W)
    return rows, -(-r // SUBLANES) * SUBLANES


SMALL_ROW, SMALL_ROWS = _small_rows()
SMALL_B_ROWS = 2 * BLOCK_ROWS


def _small_pieces(name):
    r, core = SMALL_ROW[name], SMALL_CORE[name]
    if name in ("s5_b_re", "s5_b_im"):
        return [((k // 2, slice(32 * (k % 2), 32 * (k % 2) + 32), slice(None)), slice(r, r + BLOCK_ROWS),
                 slice(S5_GROUP * k, S5_GROUP * (k + 1))) for k in range(2 * S5_GROUPS)]
    if name in ("s5_c_re", "s5_c_im"):
        return [((g, slice(None), slice(None)), slice(r + S5_GROUP * (g % 2), r + S5_GROUP * (g % 2 + 1)),
                 slice(S5_STATE * (g // 2), S5_STATE * (g // 2 + 1))) for g in range(S5_GROUPS)]
    pieces = []
    for i in range(core[0]):
        for c0 in range(0, core[1], PACK_W):
            w, flat = min(PACK_W, core[1] - c0), i * core[1] + c0
            pieces.append(((slice(i, i + 1), slice(c0, c0 + w)), slice(r + flat // PACK_W, r + flat // PACK_W + 1),
                           slice(flat % PACK_W, flat % PACK_W + w)))
    return pieces


def _core_index(ref, name, idx):
    return (0,) * (len(ref.shape) - len(SMALL_CORE[name])) + idx


def _pack_small_grads(grads):
    names = list(SMALL_CORE)

    def body(*refs):
        pack = refs[-1]
        pack[...] = jnp.zeros_like(pack)
        for ref, n in zip(refs, names):
            for idx, rows, lanes in _small_pieces(n):
                pack[rows, lanes] = ref[_core_index(ref, n, idx)]

    return _pcall(body, "pack_small_grads", (1,), [_full(grads[n].shape) for n in names],
                  _full((SMALL_ROWS, PACK_W)), _sds((SMALL_ROWS, PACK_W)))(*[grads[n] for n in names])


def _adamw_small(parts, names, rows, given, name):
    lo, hi = rows
    k = len(names)
    shapes = [given[n].shape for n in names]

    def body(*refs):
        p_ref, ins, outs = refs[0], refs[1:1 + 3 * k], refs[1 + 3 * k:1 + 7 * k]
        packs, results = refs[1 + 7 * k:4 + 7 * k], refs[4 + 7 * k:]
        for j, pack in enumerate(packs):
            pack[...] = jnp.zeros_like(pack)
            for ref, n in zip(ins[j * k:(j + 1) * k], names):
                for idx, prow, lanes in _small_pieces(n):
                    pack[slice(prow.start - lo, prow.stop - lo), lanes] = ref[_core_index(ref, n, idx)]
        g = p_ref[0, lo:hi, :]
        for d in range(1, N_DEV):
            g = g + p_ref[d, lo:hi, :]
        m1 = ADAM_B1 * packs[1][...] + (1.0 - ADAM_B1) * g
        v1 = ADAM_B2 * packs[2][...] + (1.0 - ADAM_B2) * (g * g)
        m_hat = m1 / (1.0 - ADAM_B1 ** ADAM_STEP)
        v_hat = v1 / (1.0 - ADAM_B2 ** ADAM_STEP)
        results[0][...] = g
        results[1][...] = -ADAM_LR * (m_hat / (jnp.sqrt(v_hat) + ADAM_EPS) + ADAM_WD * packs[0][...])
        results[2][...] = m1
        results[3][...] = v1
        for j, result in enumerate(results):
            for ref, n in zip(outs[j * k:(j + 1) * k], names):
                for idx, prow, lanes in _small_pieces(n):
                    ref[_core_index(ref, n, idx)] = result[slice(prow.start - lo, prow.stop - lo), lanes]

    flat = _pcall(body, name, (1,),
                  [_full(parts.shape)] + [_full(s) for s in shapes] * 3,
                  [_full(s) for s in shapes] * 4, [_sds(s) for s in shapes] * 4,
                  scratch=[pltpu.VMEM((hi - lo, PACK_W), F32)] * 7,
                  )(parts, *[given[pre + n] for pre in ("", "m_", "v_") for n in names])
    return {n: [flat[j * k + i] for j in range(4)] for i, n in enumerate(names)}


def kernel(x, g_mix, w_in, s5_a_re, s5_a_im, s5_log_dt, s5_b_re, s5_b_im, s5_c_re, s5_c_im, s5_d, w_glu, b_glu, hg_lb_logits, hg_norm_gain, w_pa, w_pb, w_out, g_ffn, w_up, w_conv, b_conv, w_down, g_final, loss_target, m_g_mix, m_w_in, m_s5_a_re, m_s5_a_im, m_s5_log_dt, m_s5_b_re, m_s5_b_im, m_s5_c_re, m_s5_c_im, m_s5_d, m_w_glu, m_b_glu, m_hg_lb_logits, m_hg_norm_gain, m_w_pa, m_w_pb, m_w_out, m_g_ffn, m_w_up, m_w_conv, m_b_conv, m_w_down, m_g_final, v_g_mix, v_w_in, v_s5_a_re, v_s5_a_im, v_s5_log_dt, v_s5_b_re, v_s5_b_im, v_s5_c_re, v_s5_c_im, v_s5_d, v_w_glu, v_b_glu, v_hg_lb_logits, v_hg_norm_gain, v_w_pa, v_w_pb, v_w_out, v_g_ffn, v_w_up, v_w_conv, v_b_conv, v_w_down, v_g_final):
    given = dict(locals())
    small_names = [n for n, _ in SMALL]

    pay = {n: given[n][0] if n == "w_conv" else given[n][0].astype(BF16) for n in BIG}
    groups = {"in": ["w_in"], "mix": ["w_glu", "w_pa", "w_pb", "w_out"], "ffn": ["w_up", "w_down", "w_conv"]}
    gathers, order = {}, pay["w_in"]
    for grp, names in groups.items():
        gathers[grp], order = _exchange_start("gather_" + grp + "_start", [(pay[n], False) for n in names], order)

    def weights(grp, *after):
        if grp == "in":
            after = (*after, order)
        got = _exchange_wait("gather_" + grp + "_wait", gathers[grp], *after)
        return {n: _join_shards(n, g) for n, g in zip(groups[grp], got)}

    in_flight, started = [], []

    def emit(grads):
        names = list(grads)
        state, token = _exchange_start("grads_" + names[0] + "_start",
                                       [(_split_shards(n, grads[n]), True) for n in names], grads[names[0]])
        in_flight.append((names, state))
        return token

    def emit_small(grads):
        pack = _pack_small_grads(grads)
        state, token = _exchange_start("grads_small_start", [(pack, False)], pack)
        in_flight.append((["small"], state))
        started.append(token)

    sp = {n: (given[n] if n in ("g_final", "hg_lb_logits") else given[n][0]) for n in small_names}
    sp["g_mix"] = _after(sp["g_mix"], order)
    dx = _local_step(x, loss_target, weights, sp, emit, emit_small)

    res = {}
    after = [started[-1]]
    in_flight.insert(-1, in_flight.pop())
    for names, state in in_flight:
        parts = _exchange_wait("grads_" + names[0] + "_wait", state, *after)
        if names != ["small"]:
            for n, part in zip(names, parts):
                res[n] = _adamw(part, given[n], given["m_" + n], given["v_" + n], "adamw_" + n, BIG[n][2])
            after = [res[n][0] for n in names]
            continue
        sgiven = dict(given)
        for pre in ("", "m_", "v_"):
            sgiven[pre + "g_final"] = given[pre + "g_final"].reshape(1, D_MODEL)
            sgiven[pre + "loss"] = jnp.zeros((1, 1), F32)
        b_names = ["s5_b_re", "s5_b_im"]
        res.update(_adamw_small(parts[0], b_names, (0, SMALL_B_ROWS), sgiven, "adamw_small_b"))
        res.update(_adamw_small(parts[0], [n for n in SMALL_CORE if n not in b_names], (SMALL_B_ROWS, SMALL_ROWS),
                                sgiven, "adamw_small_rest"))
        res["g_final"] = [r.reshape(D_MODEL) for r in res["g_final"]]
        total_loss = res["loss"][0].reshape(())
        after = [res["s5_b_re"][0], res["g_mix"][0]]
    return (total_loss, dx, *[res[n][0] for n in WEIGHT_ORDER], *[res[n][1] for n in WEIGHT_ORDER],
            *[res[n][2] for n in WEIGHT_ORDER], *[res[n][3] for n in WEIGHT_ORDER])
```

```python
import functools
import math

import jax
import jax.numpy as jnp
from jax import lax
from jax.experimental import pallas as pl
from jax.experimental.pallas import tpu as pltpu

F32 = jnp.float32
BF16 = jnp.bfloat16

D_MODEL = 1024
S5_WIDTH = 512
S5_GROUP = 16
S5_GROUPS = 32
S5_STATE = 64
S5_N = S5_GROUPS * S5_STATE
HG_WIDTH = 512
HG_HEAD = 128
HG_HEADS = 4
D_FF = 2816
CONV_W = 3
CHUNK = 64
N_IN = S5_WIDTH + 4 * HG_WIDTH + 2 * D_MODEL
EPS = 1e-6
QSCALE = HG_HEAD ** -0.5

ADAM_LR = 0.001
ADAM_B1 = 0.9
ADAM_B2 = 0.999
ADAM_EPS = 1e-08
ADAM_WD = 0.01
ADAM_STEP = 10

N_DEV = 8
V7X_VMEM_BYTES = 64 * 1024 * 1024
VMEM_LIMIT = V7X_VMEM_BYTES * 7 // 8
SUBLANES = 8
PACK_W = 1024

SMALL = (
    ("g_mix", (1, D_MODEL)),
    ("s5_a_re", (1, S5_GROUPS, S5_STATE)),
    ("s5_a_im", (1, S5_GROUPS, S5_STATE)),
    ("s5_log_dt", (1, S5_GROUPS)),
    ("s5_b_re", (1, S5_GROUPS, S5_STATE, S5_GROUP)),
    ("s5_b_im", (1, S5_GROUPS, S5_STATE, S5_GROUP)),
    ("s5_c_re", (1, S5_GROUPS, S5_GROUP, S5_STATE)),
    ("s5_c_im", (1, S5_GROUPS, S5_GROUP, S5_STATE)),
    ("s5_d", (1, S5_WIDTH)),
    ("b_glu", (1, S5_WIDTH)),
    ("hg_lb_logits", (2, HG_WIDTH)),
    ("hg_norm_gain", (1, HG_WIDTH)),
    ("g_ffn", (1, D_MODEL)),
    ("b_conv", (1, 2 * D_FF)),
    ("g_final", (D_MODEL,)),
)
WEIGHT_ORDER = ("g_mix", "w_in", "s5_a_re", "s5_a_im", "s5_log_dt", "s5_b_re", "s5_b_im", "s5_c_re", "s5_c_im",
                "s5_d", "w_glu", "b_glu", "hg_lb_logits", "hg_norm_gain", "w_pa", "w_pb", "w_out", "g_ffn",
                "w_up", "w_conv", "b_conv", "w_down", "g_final")


def _pcall(body, name, grid, in_specs, out_specs, out_shape, scratch=()):
    return pl.pallas_call(
        body, name=name, grid=grid, in_specs=in_specs, out_specs=out_specs, out_shape=out_shape,
        scratch_shapes=list(scratch),
        compiler_params=pltpu.CompilerParams(dimension_semantics=("arbitrary",) * len(grid),
                                             vmem_limit_bytes=VMEM_LIMIT),
    )


def _full(shape):
    return pl.BlockSpec(shape, lambda *_: (0,) * len(shape))


def _sds(shape, dtype=F32):
    return jax.ShapeDtypeStruct(shape, dtype)


def _dot(a, b):
    return jnp.dot(a.astype(BF16), b.astype(BF16), preferred_element_type=F32)


def _dot_nt(a, b):
    return lax.dot_general(a.astype(BF16), b.astype(BF16), (((1,), (1,)), ((), ())), preferred_element_type=F32)


def _dot_tn(a, b):
    return lax.dot_general(a.astype(BF16), b.astype(BF16), (((0,), (0,)), ((), ())), preferred_element_type=F32)


def _hdot(a, b):
    return jnp.dot(a, b, preferred_element_type=F32, precision=lax.Precision.HIGHEST)


def _hdot_tn(a, b):
    return lax.dot_general(a, b, (((0,), (0,)), ((), ())), preferred_element_type=F32,
                           precision=lax.Precision.HIGHEST)


def _sigmoid(x):
    return jax.nn.sigmoid(x)


GELU_C = math.sqrt(2.0 / math.pi)
GELU_A = 0.044715


def _gelu(x):
    return 0.5 * x * (1.0 + jnp.tanh(GELU_C * (x + GELU_A * (x * x * x))))


def _gelu_grad(x):
    t = jnp.tanh(GELU_C * (x + GELU_A * (x * x * x)))
    return 0.5 * (1.0 + t) + 0.5 * x * (1.0 - t * t) * (GELU_C * (1.0 + 3.0 * GELU_A * x * x))


def _cumsum_rows(v, reverse=False):
    n = v.shape[0]
    row = lax.broadcasted_iota(jnp.int32, v.shape, 0)
    s = 1
    while s < n:
        if reverse:
            v = v + jnp.where(row < n - s, pltpu.roll(v, n - s, axis=0), 0.0)
        else:
            v = v + jnp.where(row >= s, pltpu.roll(v, s, axis=0), 0.0)
        s *= 2
    return v


def _token_tile(seq):
    return min(256, seq)


def _s5_disc(a_re, a_im, ldt, b_re, b_im):
    dt = jnp.exp(ldt)
    mag = jnp.exp(a_re * dt)
    ang = a_im * dt
    lb_re = mag * jnp.cos(ang)
    lb_im = mag * jnp.sin(ang)
    den = a_re * a_re + a_im * a_im
    n_re = lb_re - 1.0
    n_im = lb_im
    co_re = (n_re * a_re + n_im * a_im) / den
    co_im = (n_im * a_re - n_re * a_im) / den
    bb_re = co_re * b_re - co_im * b_im
    bb_im = co_re * b_im + co_im * b_re
    return lb_re, lb_im, bb_re, bb_im


def _params_fwd(a_re, a_im, ldt, b_re, b_im, logits):
    def body(are, aim, ld, bre, bim, lg, lr_o, li_o, bbr_o, bbi_o, lb_o):
        lr, li, bbr, bbi = _s5_disc(are[...], aim[...], ld[...], bre[...], bim[...])
        lr_o[...] = lr
        li_o[...] = li
        bbr_o[...] = bbr
        bbi_o[...] = bbi
        lb_o[...] = _sigmoid(lg[0:1, :] - lg[1:2, :])

    col, mat = (S5_N, 1), (S5_N, S5_GROUP)
    return _pcall(body, "params_fwd", (1,),
                  [_full(col), _full(col), _full(col), _full(mat), _full(mat), _full((2, HG_WIDTH))],
                  [_full(col), _full(col), _full(mat), _full(mat), _full((1, HG_WIDTH))],
                  [_sds(col), _sds(col), _sds(mat), _sds(mat), _sds((1, HG_WIDTH))])(a_re, a_im, ldt, b_re, b_im, logits)


def _params_bwd(a_re, a_im, ldt, b_re, b_im, logits, dlr, dli, dbbr, dbbi, dlb):
    def body(are, aim, ld, bre, bim, lg, dlr_r, dli_r, dbbr_r, dbbi_r, dlb_r,
             dare_o, daim_o, dld_o, dbre_o, dbim_o, dlg_o):
        _, vjp = jax.vjp(_s5_disc, are[...], aim[...], ld[...], bre[...], bim[...])
        dare, daim, dld, dbre, dbim = vjp((dlr_r[...], dli_r[...], dbbr_r[...], dbbi_r[...]))
        dare_o[...] = dare
        daim_o[...] = daim
        dbre_o[...] = dbre
        dbim_o[...] = dbim
        for g in range(S5_GROUPS):
            dld_o[:, g:g + 1] = jnp.sum(dld[g * S5_STATE:(g + 1) * S5_STATE, :], axis=0, keepdims=True)
        lb = _sigmoid(lg[0:1, :] - lg[1:2, :])
        d0 = dlb_r[...] * lb * (1.0 - lb)
        dlg_o[0:1, :] = d0
        dlg_o[1:2, :] = -d0

    col, mat = (S5_N, 1), (S5_N, S5_GROUP)
    return _pcall(body, "params_bwd", (1,),
                  [_full(col), _full(col), _full(col), _full(mat), _full(mat), _full((2, HG_WIDTH)),
                   _full(col), _full(col), _full(mat), _full(mat), _full((1, HG_WIDTH))],
                  [_full(col), _full(col), _full((1, S5_GROUPS)), _full(mat), _full(mat), _full((2, HG_WIDTH))],
                  [_sds(col), _sds(col), _sds((1, S5_GROUPS)), _sds(mat), _sds(mat), _sds((2, HG_WIDTH))],
                  )(a_re, a_im, ldt, b_re, b_im, logits, dlr, dli, dbbr, dbbi, dlb)


def _band_blocks(m):
    g, r, c = m.shape
    gb = g // S5_BANDS
    m4 = m.astype(BF16).reshape(S5_BANDS, gb, r, c)
    on_diag = jnp.eye(gb, dtype=bool)[None, :, None, :, None]
    return jnp.where(on_diag, m4[:, :, :, None, :], 0).reshape(S5_BANDS, gb * r, gb * c)


def _diag_blocks(band, r, c):
    g, nb = band.shape[0] // r, band.shape[1] // c
    on_diag = (jnp.arange(g) % nb)[:, None, None, None] == jnp.arange(nb)[None, None, :, None]
    return jnp.sum(jnp.where(on_diag, band.reshape(g, r, nb, c), 0.0), axis=2)


def _in_proj(x, g_mix, w_in, tm):
    t = x.shape[0]

    def body(x_ref, g_ref, w_ref, u_ref, za_ref, zh_ref, zg_ref):
        xv = x_ref[...]
        r = lax.rsqrt(jnp.mean(xv * xv, axis=-1, keepdims=True) + EPS)
        u = (xv * r * g_ref[...]).astype(BF16)
        u_ref[...] = u
        za_ref[...] = _dot_nt(u, w_ref[0:S5_WIDTH, :])
        zh_ref[...] = _dot_nt(u, w_ref[S5_WIDTH:S5_WIDTH + 4 * HG_WIDTH, :])
        zg_ref[...] = _dot_nt(u, w_ref[S5_WIDTH + 4 * HG_WIDTH:, :]).astype(BF16)

    row = lambda w: pl.BlockSpec((tm, w), lambda i: (i, 0))
    return _pcall(body, "in_proj", (t // tm,),
                  [row(D_MODEL), _full((1, D_MODEL)), _full((N_IN, D_MODEL))],
                  [row(D_MODEL), row(S5_WIDTH), row(4 * HG_WIDTH), row(2 * D_MODEL)],
                  [_sds((t, D_MODEL), BF16), _sds((t, S5_WIDTH)), _sds((t, 4 * HG_WIDTH)),
                   _sds((t, 2 * D_MODEL), BF16)],
                  )(x, g_mix, w_in)


S5_LANES = 512
S5_BANDS = 4


def _band(q):
    return (slice(q * S5_WIDTH // S5_BANDS, (q + 1) * S5_WIDTH // S5_BANDS),
            slice(q * S5_N // S5_BANDS, (q + 1) * S5_N // S5_BANDS))


def _im(st):
    return slice(S5_N + st.start, S5_N + st.stop)


SCAN_UNROLL = 8


def _complex_scan(buf_ref, lam_ref, st_ref, ts, reverse):
    chunks = [slice(cc * S5_LANES, (cc + 1) * S5_LANES) for cc in range(S5_N // S5_LANES)]
    nch = len(chunks)
    wr = [lam_ref[0:1, re] for re in chunks]
    wi = [-lam_ref[1:2, re] if reverse else lam_ref[1:2, re] for re in chunks]

    def block(ib, carry):
        vr, vi = list(carry[:nch]), list(carry[nch:])
        first = ts - SCAN_UNROLL - ib * SCAN_UNROLL if reverse else ib * SCAN_UNROLL
        first = pl.multiple_of(first, SCAN_UNROLL)
        for k in range(SCAN_UNROLL):
            row = pl.ds(first + (SCAN_UNROLL - 1 - k if reverse else k), 1)
            for cc, re in enumerate(chunks):
                nr = wr[cc] * vr[cc] - wi[cc] * vi[cc] + buf_ref[row, re]
                ni = wr[cc] * vi[cc] + wi[cc] * vr[cc] + buf_ref[row, _im(re)]
                buf_ref[row, re] = nr
                buf_ref[row, _im(re)] = ni
                vr[cc], vi[cc] = nr, ni
        return tuple(vr + vi)

    init = tuple(st_ref[0:1, re] for re in chunks) + tuple(st_ref[1:2, re] for re in chunks)
    last = lax.fori_loop(0, ts // SCAN_UNROLL, block, init)
    for cc, re in enumerate(chunks):
        st_ref[0:1, re] = last[cc]
        st_ref[1:2, re] = last[nch + cc]


BAND_CH = S5_WIDTH // S5_BANDS
BAND_ST = S5_N // S5_BANDS


def _s5_fwd(za, b_bands, lam, c_bands, dskip, nb, seq, ts):
    t = za.shape[0]
    nts = seq // ts

    def body(za_ref, br_ref, bi_ref, lam_ref, cr_ref, ci_ref, d_ref, xs_ref, y_ref, buf_ref, st_ref):
        @pl.when(pl.program_id(1) == 0)
        def _():
            st_ref[...] = jnp.zeros_like(st_ref)

        zav = za_ref[...]
        for q in range(S5_BANDS):
            ch, st = _band(q)
            buf_ref[:, st] = _dot(zav[:, ch], br_ref[q])
            buf_ref[:, _im(st)] = _dot(zav[:, ch], bi_ref[q])
        _complex_scan(buf_ref, lam_ref, st_ref, ts, reverse=False)
        xs_ref[...] = buf_ref[...].astype(BF16)
        for q in range(S5_BANDS):
            ch, st = _band(q)
            y_ref[:, ch] = (_dot(xs_ref[:, st], cr_ref[q]) + _dot(xs_ref[:, _im(st)], ci_ref[q])
                            + d_ref[:, ch] * zav[:, ch])

    tok = lambda w: pl.BlockSpec((ts, w), lambda b, j: (b * nts + j, 0))
    to_st, to_ch = _full((S5_BANDS, BAND_CH, BAND_ST)), _full((S5_BANDS, BAND_ST, BAND_CH))
    return _pcall(body, "s5_fwd", (nb, nts),
                  [tok(S5_WIDTH), to_st, to_st, _full((2, S5_N)), to_ch, to_ch, _full((1, S5_WIDTH))],
                  [tok(2 * S5_N), tok(S5_WIDTH)],
                  [_sds((t, 2 * S5_N), BF16), _sds((t, S5_WIDTH))],
                  scratch=[pltpu.VMEM((ts, 2 * S5_N), F32), pltpu.VMEM((2, S5_N), F32)],
                  )(za, *b_bands, lam, *c_bands, dskip)


def _hgrn_gates(zq, zf, lbh):
    sf = _sigmoid(zf)
    f = lbh + (1.0 - lbh) * sf
    sq = _sigmoid(zq)
    qa = zq * sq * QSCALE
    bc = _cumsum_rows(jnp.log(f))
    bm = bc[CHUNK // 2 - 1:CHUNK // 2, :]
    bl = bc[CHUNK - 1:CHUNK, :]
    return sf, f, sq, qa, bc, bm, bl


def _hgrn_fwd(zh, lb, nb, seq):
    nc = seq // CHUNK

    def body(zh_ref, lb_ref, o_ref, sts_ref, st_ref):
        @pl.when(pl.program_id(0) == 0)
        def _():
            st_ref[...] = jnp.zeros_like(st_ref)

        causal = (lax.broadcasted_iota(jnp.int32, (CHUNK, CHUNK), 0)
                  >= lax.broadcasted_iota(jnp.int32, (CHUNK, CHUNK), 1))
        for b in range(nb):
            for h in range(HG_HEADS):
                hs = slice(h * HG_HEAD, (h + 1) * HG_HEAD)
                zq = zh_ref[b, :, h * HG_HEAD:(h + 1) * HG_HEAD]
                zf = zh_ref[b, :, HG_WIDTH + h * HG_HEAD:HG_WIDTH + (h + 1) * HG_HEAD]
                zi = zh_ref[b, :, 2 * HG_WIDTH + h * HG_HEAD:2 * HG_WIDTH + (h + 1) * HG_HEAD]
                _, f, _, qa, bc, bm, bl = _hgrn_gates(zq, zf, lb_ref[:, hs])
                k = 1.0 - f
                qt = qa * jnp.exp(bc - bm)
                kt = k * jnp.exp(bm - bc)
                qb = qa * jnp.exp(bc)
                kd = k * jnp.exp(bl - bc)
                st = st_ref[b, h]
                sts_ref[b, 0, h] = st
                a = jnp.where(causal, _dot_nt(qt, kt), 0.0)
                o_ref[b, :, hs] = _dot(a, zi) + _dot_nt(qb, st)
                st_ref[b, h] = st * jnp.exp(bl) + _dot_tn(zi, kd)

    return _pcall(body, "hgrn_fwd", (nc,),
                  [pl.BlockSpec((nb, CHUNK, 4 * HG_WIDTH), lambda c: (0, c, 0)), _full((1, HG_WIDTH))],
                  [pl.BlockSpec((nb, CHUNK, HG_WIDTH), lambda c: (0, c, 0)),
                   pl.BlockSpec((nb, 1, HG_HEADS, HG_HEAD, HG_HEAD), lambda c: (0, c, 0, 0, 0))],
                  [_sds((nb, seq, HG_WIDTH)), _sds((nb, nc, HG_HEADS, HG_HEAD, HG_HEAD))],
                  scratch=[pltpu.VMEM((nb, HG_HEADS, HG_HEAD, HG_HEAD), F32)])(zh, lb)


def _head_rms(o):
    parts = []
    for h in range(HG_HEADS):
        oh = o[:, h * HG_HEAD:(h + 1) * HG_HEAD]
        r = lax.rsqrt(jnp.mean(oh * oh, axis=-1, keepdims=True) + EPS)
        parts.append(jnp.broadcast_to(r, oh.shape))
    return jnp.concatenate(parts, axis=1)


def _head_mean(v):
    parts = []
    for h in range(HG_HEADS):
        vh = v[:, h * HG_HEAD:(h + 1) * HG_HEAD]
        parts.append(jnp.broadcast_to(jnp.mean(vh, axis=-1, keepdims=True), vh.shape))
    return jnp.concatenate(parts, axis=1)


def _mix_fwd(x, y0, o, zh, zgt, w_glu, b_glu, gain, w_pa, w_pb, w_out, g_ffn, tm):
    t = x.shape[0]

    def body(x_ref, y0_ref, o_ref, zg_ref, zgt_ref, wglu_ref, bglu_ref, gain_ref, wpa_ref, wpb_ref, wout_ref,
             gffn_ref, x1_ref, u2_ref, pa_ref, pb_ref, ya2_ref, yb_ref):
        ya1 = _gelu(y0_ref[...])
        s = _sigmoid(_dot(ya1, wglu_ref[...]) + bglu_ref[...])
        ya2 = (ya1 * s).astype(BF16)
        ov = o_ref[...]
        zg = zg_ref[...]
        yb = (ov * _head_rms(ov) * gain_ref[...] * (zg * _sigmoid(zg))).astype(BF16)
        ya2_ref[...] = ya2
        yb_ref[...] = yb
        pa = jnp.dot(ya2, wpa_ref[...], preferred_element_type=F32)
        pb = jnp.dot(yb, wpb_ref[...], preferred_element_type=F32)
        pa_ref[...] = pa.astype(BF16)
        pb_ref[...] = pb.astype(BF16)
        m = (_sigmoid(zgt_ref[:, 0:D_MODEL].astype(F32)) * pa
             + _sigmoid(zgt_ref[:, D_MODEL:].astype(F32)) * pb)
        x1 = x_ref[...] + _dot(m, wout_ref[...])
        x1_ref[...] = x1
        r = lax.rsqrt(jnp.mean(x1 * x1, axis=-1, keepdims=True) + EPS)
        u2_ref[...] = (x1 * r * gffn_ref[...]).astype(BF16)

    row = lambda w: pl.BlockSpec((tm, w), lambda i: (i, 0))
    return _pcall(body, "mix_fwd", (t // tm,),
                  [row(D_MODEL), row(S5_WIDTH), row(HG_WIDTH), pl.BlockSpec((tm, HG_WIDTH), lambda i: (i, 3)),
                   row(2 * D_MODEL), _full((S5_WIDTH, S5_WIDTH)), _full((1, S5_WIDTH)), _full((1, HG_WIDTH)),
                   _full((S5_WIDTH, D_MODEL)), _full((HG_WIDTH, D_MODEL)), _full((D_MODEL, D_MODEL)),
                   _full((1, D_MODEL))],
                  [row(D_MODEL), row(D_MODEL), row(D_MODEL), row(D_MODEL), row(S5_WIDTH), row(HG_WIDTH)],
                  [_sds((t, D_MODEL)), _sds((t, D_MODEL), BF16), _sds((t, D_MODEL), BF16), _sds((t, D_MODEL), BF16),
                   _sds((t, S5_WIDTH), BF16), _sds((t, HG_WIDTH), BF16)],
                  )(x, y0, o, zh, zgt, w_glu, b_glu, gain, w_pa, w_pb, w_out, g_ffn)


FF_COLS = 256
FF_UP_TILE = 1408


def _ffn_up(u2, w_up, tm):
    t = u2.shape[0]
    n = 2 * D_FF

    def body(u_ref, w_ref, h_ref):
        h_ref[...] = _dot_nt(u_ref[...], w_ref[...]).astype(BF16)

    return _pcall(body, "ffn_up", (n // FF_UP_TILE, t // tm),
                  [pl.BlockSpec((tm, D_MODEL), lambda j, i: (i, 0)),
                   pl.BlockSpec((FF_UP_TILE, D_MODEL), lambda j, i: (j, 0))],
                  pl.BlockSpec((tm, FF_UP_TILE), lambda j, i: (i, j)),
                  _sds((t, n), BF16))(u2, w_up)


HALO = 16


def _shift_matrix(tm):
    r = lax.broadcasted_iota(jnp.int32, (tm, tm), 0)
    c = lax.broadcasted_iota(jnp.int32, (tm, tm), 1)
    return jnp.where(r == c + 1, 1.0, 0.0).astype(BF16)


def _conv_cols(h_ref, halo_ref, valid, wc_ref, bc_ref, c0):
    cs = slice(c0, c0 + FF_COLS)
    cur = h_ref[:, cs].astype(F32)
    prev = jnp.where(valid, halo_ref[:, cs].astype(F32), 0.0)
    full = jnp.concatenate([prev, cur], axis=0)
    h1 = pltpu.roll(full, 1, axis=0)[HALO:]
    h2 = pltpu.roll(full, 2, axis=0)[HALO:]
    return h2 * wc_ref[0:1, cs] + h1 * wc_ref[1:2, cs] + cur * wc_ref[2:3, cs] + bc_ref[:, cs]


def _ffn_down_loss(h, x1, tgt, w_conv, b_conv, w_down, g_final, seq, tm):
    t = h.shape[0]
    tps = seq // tm
    n = 2 * D_FF

    def body(h_ref, halo_ref, x1_ref, tgt_ref, wc_ref, bc_ref, wd_ref, gf_ref,
             hc_ref, a_ref, dx2_ref, dx2b_ref, loss_ref, dgf_ref):
        i = pl.program_id(0)

        @pl.when(i == 0)
        def _():
            loss_ref[...] = jnp.zeros_like(loss_ref)
            dgf_ref[...] = jnp.zeros_like(dgf_ref)

        valid = (i % tps) != 0
        x2 = x1_ref[...]
        for j in range(D_FF // FF_COLS):
            gate = _conv_cols(h_ref, halo_ref, valid, wc_ref, bc_ref, j * FF_COLS)
            val = _conv_cols(h_ref, halo_ref, valid, wc_ref, bc_ref, D_FF + j * FF_COLS)
            hc_ref[:, j * FF_COLS:(j + 1) * FF_COLS] = gate.astype(BF16)
            hc_ref[:, D_FF + j * FF_COLS:D_FF + (j + 1) * FF_COLS] = val.astype(BF16)
            a = (gate * _sigmoid(gate) * val).astype(BF16)
            a_ref[:, j * FF_COLS:(j + 1) * FF_COLS] = a
            x2 = x2 + jnp.dot(a, wd_ref[j * FF_COLS:(j + 1) * FF_COLS, :], preferred_element_type=F32)
        r = lax.rsqrt(jnp.mean(x2 * x2, axis=-1, keepdims=True) + EPS)
        xn = x2 * r
        g = gf_ref[...]
        e = xn * g - tgt_ref[...]
        loss_ref[...] += (0.5 / D_MODEL) * jnp.sum(e * e).reshape(1, 1)
        dy = e * (1.0 / D_MODEL)
        dgf_ref[...] += jnp.sum(dy * xn, axis=0, keepdims=True)
        dxn = dy * g
        dx2 = r * (dxn - xn * jnp.mean(dxn * xn, axis=-1, keepdims=True))
        dx2_ref[...] = dx2
        dx2b_ref[...] = dx2.astype(BF16)

    row = lambda w: pl.BlockSpec((tm, w), lambda i: (i, 0))
    halo = pl.BlockSpec((HALO, n), lambda i: (jnp.maximum(i * (tm // HALO) - 1, 0), 0))
    return _pcall(body, "ffn_down_loss", (t // tm,),
                  [row(n), halo, row(D_MODEL), row(D_MODEL), _full((CONV_W, n)), _full((1, n)),
                   _full((D_FF, D_MODEL)), _full((1, D_MODEL))],
                  [row(n), row(D_FF), row(D_MODEL), row(D_MODEL), _full((1, 1)), _full((1, D_MODEL))],
                  [_sds((t, n), BF16), _sds((t, D_FF), BF16), _sds((t, D_MODEL)), _sds((t, D_MODEL), BF16),
                   _sds((1, 1)), _sds((1, D_MODEL))],
                  )(h, h, x1, tgt, w_conv, b_conv, w_down, g_final)


def _wgrad(a, b, name, tn, out_dtype=F32, band=None, after=None):
    t, m = a.shape
    n = b.shape[1] if band is None else band
    nbands = 1 if band is None else b.shape[1] // band
    after = b if after is None else after

    def body(a_ref, b_ref, after_ref, o_ref):
        o_ref[...] = _dot_tn(a_ref[...], b_ref[...]).astype(out_dtype)

    return _pcall(body, name, (m // tn,),
                  [pl.BlockSpec((t, tn), lambda i: (0, i)), pl.BlockSpec((t, n), lambda i: (0, i % nbands)),
                   pl.BlockSpec(memory_space=pl.ANY)],
                  pl.BlockSpec((tn, n), lambda i: (i, 0)), _sds((m, n), out_dtype))(a, b, after)


def _ffn_bwd_act(dx2b, hc, w_down, tm):
    t = hc.shape[0]
    n = 2 * D_FF

    def body(dx2_ref, hc_ref, wd_ref, dhc_ref, dbc_ref):
        @pl.when(pl.program_id(0) == 0)
        def _():
            dbc_ref[...] = jnp.zeros_like(dbc_ref)

        dx2 = dx2_ref[...]
        for j in range(D_FF // FF_COLS):
            gs = slice(j * FF_COLS, (j + 1) * FF_COLS)
            vs = slice(D_FF + j * FF_COLS, D_FF + (j + 1) * FF_COLS)
            gate = hc_ref[:, gs].astype(F32)
            val = hc_ref[:, vs].astype(F32)
            da = _dot_nt(dx2, wd_ref[gs, :])
            sg = _sigmoid(gate)
            dgate = da * val * (sg * (1.0 + gate * (1.0 - sg)))
            dval = da * (gate * sg)
            dhc_ref[:, gs] = dgate.astype(BF16)
            dhc_ref[:, vs] = dval.astype(BF16)
            dbc_ref[:, gs] += jnp.sum(dgate, axis=0, keepdims=True)
            dbc_ref[:, vs] += jnp.sum(dval, axis=0, keepdims=True)

    row = lambda w: pl.BlockSpec((tm, w), lambda i: (i, 0))
    return _pcall(body, "ffn_bwd_act", (t // tm,),
                  [row(D_MODEL), row(n), _full((D_FF, D_MODEL))],
                  [row(n), _full((1, n))],
                  [_sds((t, n), BF16), _sds((1, n))],
                  )(dx2b, hc, w_down)


def _ffn_bwd_up(dhc, h, dx2, x1, w_conv, w_up, g_ffn, seq, tm):
    t = dhc.shape[0]
    tps = seq // tm
    n = 2 * D_FF
    last = t // HALO - 1

    def body(dhc_ref, halo_ref, h_ref, dx2_ref, x1_ref, wc_ref, wu_ref, gf_ref,
             dh_ref, dx1_ref, dx1b_ref, dgf_ref, dwc_ref):
        i = pl.program_id(0)

        @pl.when(i == 0)
        def _():
            dgf_ref[...] = jnp.zeros_like(dgf_ref)
            dwc_ref[...] = jnp.zeros_like(dwc_ref)

        valid = ((i + 1) % tps) != 0
        du2 = jnp.zeros((tm, D_MODEL), F32)
        for j in range(n // FF_COLS):
            cs = slice(j * FF_COLS, (j + 1) * FF_COLS)
            cur = dhc_ref[:, cs].astype(F32)
            nxt = jnp.where(valid, halo_ref[:, cs].astype(F32), 0.0)
            full = jnp.concatenate([cur, nxt], axis=0)
            d1 = pltpu.roll(full, tm + HALO - 1, axis=0)[:tm]
            d2 = pltpu.roll(full, tm + HALO - 2, axis=0)[:tm]
            dh = (cur * wc_ref[2:3, cs] + d1 * wc_ref[1:2, cs] + d2 * wc_ref[0:1, cs]).astype(BF16)
            dh_ref[:, cs] = dh
            du2 = du2 + _dot(dh, wu_ref[cs, :])
            hv = h_ref[:, cs].astype(F32)
            dwc_ref[0:1, cs] += jnp.sum(hv * d2, axis=0, keepdims=True)
            dwc_ref[1:2, cs] += jnp.sum(hv * d1, axis=0, keepdims=True)
            dwc_ref[2:3, cs] += jnp.sum(hv * cur, axis=0, keepdims=True)
        x1 = x1_ref[...]
        r = lax.rsqrt(jnp.mean(x1 * x1, axis=-1, keepdims=True) + EPS)
        xn = x1 * r
        dgf_ref[...] += jnp.sum(du2 * xn, axis=0, keepdims=True)
        dxn = du2 * gf_ref[...]
        dx1 = dx2_ref[...] + r * (dxn - xn * jnp.mean(dxn * xn, axis=-1, keepdims=True))
        dx1_ref[...] = dx1
        dx1b_ref[...] = dx1.astype(BF16)

    row = lambda w: pl.BlockSpec((tm, w), lambda i: (i, 0))
    halo = pl.BlockSpec((HALO, n), lambda i: (jnp.minimum((i + 1) * (tm // HALO), last), 0))
    return _pcall(body, "ffn_bwd_up", (t // tm,),
                  [row(n), halo, row(n), row(D_MODEL), row(D_MODEL), _full((CONV_W, n)), _full((n, D_MODEL)),
                   _full((1, D_MODEL))],
                  [row(n), row(D_MODEL), row(D_MODEL), _full((1, D_MODEL)), _full((CONV_W, n))],
                  [_sds((t, n), BF16), _sds((t, D_MODEL)), _sds((t, D_MODEL), BF16), _sds((1, D_MODEL)),
                   _sds((CONV_W, n))],
                  )(dhc, dhc, h, dx2, x1, w_conv, w_up, g_ffn)


def _mix_bwd(dx1, y0, o, zh, zgt, pa, pb, w_glu, b_glu, gain, w_pa, w_pb, w_out, tm):
    t = dx1.shape[0]

    def body(dx1_ref, y0_ref, o_ref, zg_ref, zgt_ref, pa_ref, pb_ref, wglu_ref, bglu_ref, gain_ref, wpa_ref,
             wpb_ref, wout_ref,
             dy0_ref, do_ref, dzg_ref, dzgt_ref, m_ref, dpa_ref, dpb_ref, ya1_ref, dpre_ref, dbglu_ref, dgain_ref):
        @pl.when(pl.program_id(0) == 0)
        def _():
            dbglu_ref[...] = jnp.zeros_like(dbglu_ref)
            dgain_ref[...] = jnp.zeros_like(dgain_ref)

        dm = _dot_nt(dx1_ref[...], wout_ref[...])
        sga = _sigmoid(zgt_ref[:, 0:D_MODEL].astype(F32))
        sgb = _sigmoid(zgt_ref[:, D_MODEL:].astype(F32))
        pa = pa_ref[...].astype(F32)
        pb = pb_ref[...].astype(F32)
        m_ref[...] = (sga * pa + sgb * pb).astype(BF16)
        dzgt_ref[:, 0:D_MODEL] = (dm * pa * sga * (1.0 - sga)).astype(BF16)
        dzgt_ref[:, D_MODEL:] = (dm * pb * sgb * (1.0 - sgb)).astype(BF16)
        dpa = (dm * sga).astype(BF16)
        dpb = (dm * sgb).astype(BF16)
        dpa_ref[...] = dpa
        dpb_ref[...] = dpb
        dya2 = _dot_nt(dpa, wpa_ref[...])
        dyb = _dot_nt(dpb, wpb_ref[...])
        y0 = y0_ref[...]
        ya1 = _gelu(y0)
        ya1_ref[...] = ya1.astype(BF16)
        s = _sigmoid(_dot(ya1, wglu_ref[...]) + bglu_ref[...])
        dpre = dya2 * ya1 * s * (1.0 - s)
        dpre_ref[...] = dpre.astype(BF16)
        dbglu_ref[...] += jnp.sum(dpre, axis=0, keepdims=True)
        dya1 = dya2 * s + _dot_nt(dpre, wglu_ref[...])
        dy0_ref[...] = dya1 * _gelu_grad(y0)
        ov = o_ref[...]
        zg = zg_ref[...]
        oh = ov * _head_rms(ov)
        on = oh * gain_ref[...]
        sz = _sigmoid(zg)
        dzg_ref[...] = (dyb * on * (sz * (1.0 + zg * (1.0 - sz)))).astype(BF16)
        don = dyb * (zg * sz)
        dgain_ref[...] += jnp.sum(don * oh, axis=0, keepdims=True)
        doh = don * gain_ref[...]
        do_ref[...] = _head_rms(ov) * (doh - oh * _head_mean(doh * oh))

    row = lambda w: pl.BlockSpec((tm, w), lambda i: (i, 0))
    return _pcall(body, "mix_bwd", (t // tm,),
                  [row(D_MODEL), row(S5_WIDTH), row(HG_WIDTH), pl.BlockSpec((tm, HG_WIDTH), lambda i: (i, 3)),
                   row(2 * D_MODEL), row(D_MODEL), row(D_MODEL), _full((S5_WIDTH, S5_WIDTH)), _full((1, S5_WIDTH)),
                   _full((1, HG_WIDTH)), _full((S5_WIDTH, D_MODEL)), _full((HG_WIDTH, D_MODEL)),
                   _full((D_MODEL, D_MODEL))],
                  [row(S5_WIDTH), row(HG_WIDTH), row(HG_WIDTH), row(2 * D_MODEL), row(D_MODEL), row(D_MODEL),
                   row(D_MODEL), row(S5_WIDTH), row(S5_WIDTH), _full((1, S5_WIDTH)), _full((1, HG_WIDTH))],
                  [_sds((t, S5_WIDTH)), _sds((t, HG_WIDTH)), _sds((t, HG_WIDTH), BF16), _sds((t, 2 * D_MODEL), BF16),
                   _sds((t, D_MODEL), BF16), _sds((t, D_MODEL), BF16), _sds((t, D_MODEL), BF16),
                   _sds((t, S5_WIDTH), BF16), _sds((t, S5_WIDTH), BF16), _sds((1, S5_WIDTH)), _sds((1, HG_WIDTH))],
                  )(dx1, y0, o, zh, zgt, pa, pb, w_glu, b_glu, gain, w_pa, w_pb, w_out)


def _s5_bwd(dy0, za, xs, c_bands, b_bands, lam, dskip, nb, seq, ts):
    t = za.shape[0]
    nts = seq // ts

    def body(dy0_ref, za_ref, xs_ref, halo_ref, cr_ref, ci_ref, br_ref, bi_ref, lam_ref, d_ref,
             dza_ref, a_ref, dlam_ref, dd_ref, acc_ref, st_ref):
        b, j = pl.program_id(0), pl.program_id(1)

        @pl.when((b == 0) & (j == 0))
        def _():
            dlam_ref[...] = jnp.zeros_like(dlam_ref)
            dd_ref[...] = jnp.zeros_like(dd_ref)

        @pl.when(j == 0)
        def _():
            st_ref[...] = jnp.zeros_like(st_ref)

        dy0 = dy0_ref[...]
        for q in range(S5_BANDS):
            ch, st = _band(q)
            acc_ref[:, st] = _dot(dy0[:, ch], cr_ref[q])
            acc_ref[:, _im(st)] = _dot(dy0[:, ch], ci_ref[q])
        _complex_scan(acc_ref, lam_ref, st_ref, ts, reverse=True)
        a_ref[...] = acc_ref[...].astype(BF16)
        shift = _shift_matrix(ts)
        first = jnp.where(j == nts - 1, 0.0, halo_ref[HALO - 1:HALO, :].astype(F32))
        top = lax.broadcasted_iota(jnp.int32, (SUBLANES, S5_LANES), 0) == 0

        def shifted(cols):
            xp = jnp.dot(shift, xs_ref[:, cols], preferred_element_type=F32)
            return jnp.concatenate([xp[:SUBLANES] + jnp.where(top, first[:, cols], 0.0), xp[SUBLANES:]], axis=0)

        for cc in range(S5_N // S5_LANES):
            re = slice(cc * S5_LANES, (cc + 1) * S5_LANES)
            ar, ai, xr, xi = acc_ref[:, re], acc_ref[:, _im(re)], shifted(re), shifted(_im(re))
            dlam_ref[0:1, re] += jnp.sum(ar * xr + ai * xi, axis=0, keepdims=True)
            dlam_ref[1:2, re] += jnp.sum(ai * xr - ar * xi, axis=0, keepdims=True)
        for q in range(S5_BANDS):
            ch, st = _band(q)
            dza_ref[:, ch] = (_dot(a_ref[:, st], br_ref[q]) + _dot(a_ref[:, _im(st)], bi_ref[q])
                              + d_ref[:, ch] * dy0[:, ch]).astype(BF16)
        dd_ref[...] += jnp.sum(dy0 * za_ref[...], axis=0, keepdims=True)

    tile = lambda b, j: b * nts + (nts - 1 - j)
    tok = lambda w: pl.BlockSpec((ts, w), lambda b, j: (tile(b, j), 0))
    halo = pl.BlockSpec((HALO, 2 * S5_N), lambda b, j: (jnp.maximum(tile(b, j) * (ts // HALO) - 1, 0), 0))
    to_st, to_ch = _full((S5_BANDS, BAND_CH, BAND_ST)), _full((S5_BANDS, BAND_ST, BAND_CH))
    return _pcall(body, "s5_bwd", (nb, nts),
                  [tok(S5_WIDTH), tok(S5_WIDTH), tok(2 * S5_N), halo, to_st, to_st, to_ch, to_ch,
                   _full((2, S5_N)), _full((1, S5_WIDTH))],
                  [tok(S5_WIDTH), tok(2 * S5_N), _full((2, S5_N)), _full((1, S5_WIDTH))],
                  [_sds((t, S5_WIDTH), BF16), _sds((t, 2 * S5_N), BF16), _sds((2, S5_N)), _sds((1, S5_WIDTH))],
                  scratch=[pltpu.VMEM((ts, 2 * S5_N), F32), pltpu.VMEM((2, S5_N), F32)],
                  )(dy0, za, xs, xs, *c_bands, *b_bands, lam, dskip)


def _hgrn_bwd(zh, do, sts, lb, nb, seq):
    nc = seq // CHUNK

    def body(zh_ref, do_ref, sts_ref, lb_ref, dz_ref, dlb_ref, dst_ref):
        @pl.when(pl.program_id(0) == 0)
        def _():
            dst_ref[...] = jnp.zeros_like(dst_ref)
            dlb_ref[...] = jnp.zeros_like(dlb_ref)

        row = lax.broadcasted_iota(jnp.int32, (CHUNK, CHUNK), 0)
        causal = row >= lax.broadcasted_iota(jnp.int32, (CHUNK, CHUNK), 1)
        last_row = lax.broadcasted_iota(jnp.int32, (CHUNK, HG_HEAD), 0) == CHUNK - 1
        for b in range(nb):
            for h in range(HG_HEADS):
                hs = slice(h * HG_HEAD, (h + 1) * HG_HEAD)
                zq = zh_ref[b, :, h * HG_HEAD:(h + 1) * HG_HEAD]
                zf = zh_ref[b, :, HG_WIDTH + h * HG_HEAD:HG_WIDTH + (h + 1) * HG_HEAD]
                zi = zh_ref[b, :, 2 * HG_WIDTH + h * HG_HEAD:2 * HG_WIDTH + (h + 1) * HG_HEAD]
                lbh = lb_ref[:, hs]
                sf, f, sq, qa, bc, bm, bl = _hgrn_gates(zq, zf, lbh)
                k = 1.0 - f
                e_qt = jnp.exp(bc - bm)
                e_kt = jnp.exp(bm - bc)
                e_b = jnp.exp(bc)
                e_kd = jnp.exp(bl - bc)
                e_l = jnp.exp(bl)
                qt, kt, qb, kd = qa * e_qt, k * e_kt, qa * e_b, k * e_kd
                a = jnp.where(causal, _dot_nt(qt, kt), 0.0)
                st = sts_ref[b, 0, h]
                dst = dst_ref[b, h]
                dov = do_ref[b, :, hs]
                da = jnp.where(causal, _dot_nt(dov, zi), 0.0)
                dqt = _hdot(da, kt)
                dkt = _hdot_tn(da, qt)
                dqb = _dot(dov, st)
                di = _dot_tn(a, dov) + _dot_nt(kd, dst)
                dkd = _dot(zi, dst)
                de_l = jnp.sum(dst * st, axis=0, keepdims=True)
                dst_ref[b, h] = dst * e_l + _dot_tn(dov, qb)
                dqa = dqt * e_qt + dqb * e_b
                dk = dkt * e_kt + dkd * e_kd
                dbl = jnp.sum(dkd * kd, axis=0, keepdims=True) + de_l * e_l
                db = dqt * qt - dkt * kt + dqb * qb - dkd * kd + jnp.where(last_row, dbl, 0.0)
                df = _cumsum_rows(db, reverse=True) / f - dk
                dzq = dqa * QSCALE * (sq * (1.0 + zq * (1.0 - sq)))
                dzf = df * (1.0 - lbh) * sf * (1.0 - sf)
                dz_ref[b, :, h * HG_HEAD:(h + 1) * HG_HEAD] = dzq.astype(BF16)
                dz_ref[b, :, HG_WIDTH + h * HG_HEAD:HG_WIDTH + (h + 1) * HG_HEAD] = dzf.astype(BF16)
                dz_ref[b, :, 2 * HG_WIDTH + h * HG_HEAD:2 * HG_WIDTH + (h + 1) * HG_HEAD] = di.astype(BF16)
                dlb_ref[:, hs] += jnp.sum(df * (1.0 - sf), axis=0, keepdims=True)

    rev = lambda c: nc - 1 - c
    return _pcall(body, "hgrn_bwd", (nc,),
                  [pl.BlockSpec((nb, CHUNK, 4 * HG_WIDTH), lambda c: (0, rev(c), 0)),
                   pl.BlockSpec((nb, CHUNK, HG_WIDTH), lambda c: (0, rev(c), 0)),
                   pl.BlockSpec((nb, 1, HG_HEADS, HG_HEAD, HG_HEAD), lambda c: (0, rev(c), 0, 0, 0)),
                   _full((1, HG_WIDTH))],
                  [pl.BlockSpec((nb, CHUNK, 3 * HG_WIDTH), lambda c: (0, rev(c), 0)), _full((1, HG_WIDTH))],
                  [_sds((nb, seq, 3 * HG_WIDTH), BF16), _sds((1, HG_WIDTH))],
                  scratch=[pltpu.VMEM((nb, HG_HEADS, HG_HEAD, HG_HEAD), F32)])(zh, do, sts, lb)


def _in_proj_bwd(dza, dzh, dzg, dzgt, dx1, x, g_mix, w_in, tm):
    t = x.shape[0]

    def body(dza_ref, dzh_ref, dzg_ref, dzgt_ref, dx1_ref, x_ref, g_ref, w_ref, dz_ref, dx_ref, dg_ref):
        @pl.when(pl.program_id(0) == 0)
        def _():
            dg_ref[...] = jnp.zeros_like(dg_ref)

        c1, c2, c3 = S5_WIDTH, S5_WIDTH + 3 * HG_WIDTH, S5_WIDTH + 4 * HG_WIDTH
        dz_ref[:, 0:c1] = dza_ref[...]
        dz_ref[:, c1:c2] = dzh_ref[...]
        dz_ref[:, c2:c3] = dzg_ref[...]
        dz_ref[:, c3:] = dzgt_ref[...]
        du = _dot(dz_ref[...], w_ref[...])
        xv = x_ref[...]
        r = lax.rsqrt(jnp.mean(xv * xv, axis=-1, keepdims=True) + EPS)
        xn = xv * r
        dg_ref[...] += jnp.sum(du * xn, axis=0, keepdims=True)
        dxn = du * g_ref[...]
        dx_ref[...] = dx1_ref[...] + r * (dxn - xn * jnp.mean(dxn * xn, axis=-1, keepdims=True))

    row = lambda w: pl.BlockSpec((tm, w), lambda i: (i, 0))
    return _pcall(body, "in_proj_bwd", (t // tm,),
                  [row(S5_WIDTH), row(3 * HG_WIDTH), row(HG_WIDTH), row(2 * D_MODEL), row(D_MODEL), row(D_MODEL),
                   _full((1, D_MODEL)), _full((N_IN, D_MODEL))],
                  [row(N_IN), row(D_MODEL), _full((1, D_MODEL))],
                  [_sds((t, N_IN), BF16), _sds((t, D_MODEL)), _sds((1, D_MODEL))],
                  )(dza, dzh, dzg, dzgt, dx1, x, g_mix, w_in)


def _tie(*arrays):
    return jnp.zeros((SUBLANES, 128), F32) + sum(a.reshape(-1)[0].astype(F32) for a in arrays)


def _after(value, token):
    return value + token[0, 0]


def _local_step(x3, tgt3, weights, sp, emit, emit_small):
    nb, seq, _ = x3.shape
    t = nb * seq
    tm = _token_tile(seq)
    x = x3.reshape(t, D_MODEL)
    tgt = tgt3.reshape(t, D_MODEL)
    row = lambda v: v.reshape(1, -1)

    a_re = sp["s5_a_re"].reshape(S5_N, 1)
    a_im = sp["s5_a_im"].reshape(S5_N, 1)
    ldt = jnp.repeat(sp["s5_log_dt"].reshape(S5_GROUPS), S5_STATE).reshape(S5_N, 1)
    b_re = sp["s5_b_re"].reshape(S5_N, S5_GROUP)
    b_im = sp["s5_b_im"].reshape(S5_N, S5_GROUP)
    lr, li, bb_re, bb_im, lb = _params_fwd(a_re, a_im, ldt, b_re, b_im, sp["hg_lb_logits"])
    lam = jnp.concatenate([lr.reshape(1, S5_N), li.reshape(1, S5_N)], axis=0)
    gps = lambda m: m.reshape(S5_GROUPS, S5_STATE, S5_GROUP)
    swap = lambda m: m.transpose(0, 2, 1)
    b_to_st = (_band_blocks(swap(gps(bb_re))), _band_blocks(swap(gps(bb_im))))
    b_to_ch = (_band_blocks(gps(bb_re)), _band_blocks(gps(bb_im)))
    c_to_ch = (_band_blocks(swap(sp["s5_c_re"])), _band_blocks(swap(-sp["s5_c_im"])))
    c_to_st = (_band_blocks(sp["s5_c_re"]), _band_blocks(-sp["s5_c_im"]))

    g_mix, g_ffn, g_final = row(sp["g_mix"]), row(sp["g_ffn"]), row(sp["g_final"])
    b_glu, gain, dskip, b_conv = row(sp["b_glu"]), row(sp["hg_norm_gain"]), row(sp["s5_d"]), row(sp["b_conv"])

    w_in = weights("in", lam, *b_to_st, *b_to_ch, *c_to_ch, *c_to_st)["w_in"]
    u, za, zh, zgt = _in_proj(x, g_mix, w_in, tm)
    xs, y0 = _s5_fwd(za, b_to_st, lam, c_to_ch, dskip, nb, seq, tm)
    o3, sts = _hgrn_fwd(zh.reshape(nb, seq, 4 * HG_WIDTH), lb, nb, seq)
    o = o3.reshape(t, HG_WIDTH)
    wm = weights("mix", y0, o3)
    x1, u2, pa, pb, ya2, yb = _mix_fwd(x, y0, o, zh, zgt, wm["w_glu"], b_glu, gain, wm["w_pa"], wm["w_pb"],
                                       wm["w_out"], g_ffn, tm)
    wf = weights("ffn", u2)
    h = _ffn_up(u2, wf["w_up"], min(4 * tm, t))
    hc, a, dx2, dx2b, loss, dg_final = _ffn_down_loss(h, x1, tgt, wf["w_conv"], b_conv, wf["w_down"], g_final,
                                                      seq, tm)

    wgrad = functools.partial(_wgrad, tn=256, out_dtype=BF16)
    dhc, db_conv = _ffn_bwd_act(dx2b, hc, wf["w_down"], tm)
    sent = emit({"w_down": wgrad(a, dx2b, "dw_down")})
    dh, dx1, dx1b, dg_ffn, dw_conv = _ffn_bwd_up(dhc, h, dx2, x1, wf["w_conv"], wf["w_up"], _after(g_ffn, sent),
                                                 seq, tm)
    sent = emit({"w_up": wgrad(dh, u2, "dw_up"), "w_conv": dw_conv})
    (dy0, do, dzg, dzgt, m, dpa, dpb, ya1, dpre, db_glu, dgain) = _mix_bwd(
        dx1b, y0, o, zh, zgt, pa, pb, wm["w_glu"], _after(b_glu, sent), gain, wm["w_pa"], wm["w_pb"], wm["w_out"], tm)
    sent = emit({"w_out": wgrad(m, dx1b, "dw_out"), "w_pa": wgrad(ya2, dpa, "dw_pa"),
                 "w_pb": wgrad(yb, dpb, "dw_pb"), "w_glu": wgrad(ya1, dpre, "dw_glu")})
    dzh3, dlb = _hgrn_bwd(zh.reshape(nb, seq, 4 * HG_WIDTH), do.reshape(nb, seq, HG_WIDTH), sts, _after(lb, sent),
                          nb, seq)
    dza, a_s5, dlam, dd = _s5_bwd(dy0, za, xs, c_to_st, b_to_ch, lam, dskip, nb, seq, tm)
    dz, dx, dg_mix = _in_proj_bwd(dza, dzh3.reshape(t, 3 * HG_WIDTH), dzg, dzgt, dx1, x, g_mix, w_in, tm)
    sent = emit({"w_in": wgrad(dz, u, "dw_in")})

    band = HG_HEAD
    dbb_band = _wgrad(a_s5, za, "dbb_s5", 512, band=band, after=sent)
    dc_band = _wgrad(xs, dy0, "dc_s5", 512, band=band, after=sent)
    dbb_re = _diag_blocks(dbb_band[:S5_N], S5_STATE, S5_GROUP).reshape(S5_N, S5_GROUP)
    dbb_im = _diag_blocks(dbb_band[S5_N:], S5_STATE, S5_GROUP).reshape(S5_N, S5_GROUP)
    dc_re = _diag_blocks(dc_band[:S5_N], S5_STATE, S5_GROUP).transpose(0, 2, 1)
    dc_im = -_diag_blocks(dc_band[S5_N:], S5_STATE, S5_GROUP).transpose(0, 2, 1)
    da_re, da_im, dldt, db_re, db_im, dlogits = _params_bwd(
        a_re, a_im, ldt, b_re, b_im, sp["hg_lb_logits"],
        dlam[0].reshape(S5_N, 1), dlam[1].reshape(S5_N, 1), dbb_re, dbb_im, dlb)
    emit_small({"g_mix": dg_mix, "s5_a_re": da_re.reshape(S5_GROUPS, S5_STATE),
                "s5_a_im": da_im.reshape(S5_GROUPS, S5_STATE), "s5_log_dt": dldt, "s5_b_re": gps(db_re),
                "s5_b_im": gps(db_im), "s5_c_re": dc_re, "s5_c_im": dc_im, "s5_d": dd, "b_glu": db_glu,
                "hg_lb_logits": dlogits, "hg_norm_gain": dgain, "g_ffn": dg_ffn, "b_conv": db_conv,
                "g_final": dg_final, "loss": loss})
    return dx.reshape(nb, seq, D_MODEL)


def _mesh_peers():
    x, y, c = lax.axis_index("x"), lax.axis_index("y"), lax.axis_index("c")
    peers = []
    for k in range(1, N_DEV):
        px, py, pc = (1 - x if k & 4 else x), (1 - y if k & 2 else y), (1 - c if k & 1 else c)
        peers.append((k, (px, py, pc), 4 * px + 2 * py + pc))
    return 4 * x + 2 * y + c, peers


_HBM = pl.BlockSpec(memory_space=pltpu.HBM)
_SEM = pl.BlockSpec(memory_space=pltpu.SEMAPHORE)


def _exchange_start(name, operands, after):
    n = len(operands)
    me = 4 * lax.axis_index("x") + 2 * lax.axis_index("y") + lax.axis_index("c")
    flags = [per_peer for _, per_peer in operands]
    srcs, lands = [], []
    for arr, per_peer in operands:
        own = lax.dynamic_index_in_dim(arr, me, 0, keepdims=True) if per_peer else arr[None]
        land = lax.dynamic_update_slice_in_dim(lax.empty((N_DEV,) + own.shape[1:], arr.dtype), own, me, 0)
        srcs.append(pltpu.with_memory_space_constraint(arr, pltpu.HBM))
        lands.append(pltpu.with_memory_space_constraint(land, pltpu.HBM))
    copies = (N_DEV - 1) * n

    def body(*refs):
        src_refs, land_refs = refs[:n], refs[n:2 * n]
        send_sems, recv_sems = refs[2 * n + 1], refs[2 * n + 2]
        token = refs[-1]
        my_slab, peers = _mesh_peers()
        for k, peer, slab in peers:
            for i in range(n):
                s = (k - 1) * n + i
                pltpu.make_async_remote_copy(
                    src_ref=src_refs[i].at[slab] if flags[i] else src_refs[i], dst_ref=land_refs[i].at[my_slab],
                    send_sem=send_sems.at[s], recv_sem=recv_sems.at[s], device_id=peer,
                    device_id_type=pl.DeviceIdType.MESH).start()
        token[...] = jnp.zeros_like(token)

    outs = pl.pallas_call(
        body, name=name,
        out_shape=(pltpu.SemaphoreType.DMA((copies,)), pltpu.SemaphoreType.DMA((copies,)),
                   *[pltpu.HBM(a.shape, a.dtype) for a in srcs], *[pltpu.HBM(a.shape, a.dtype) for a in lands],
                   _sds((SUBLANES, 128))),
        in_specs=[_HBM] * (2 * n) + [pl.BlockSpec(memory_space=pl.ANY)],
        out_specs=(_SEM, _SEM, *[_HBM] * (2 * n), pl.BlockSpec(memory_space=pltpu.VMEM)),
        input_output_aliases={i: 2 + i for i in range(2 * n)},
        compiler_params=pltpu.CompilerParams(has_side_effects=pltpu.SideEffectType.DATAFLOW_SIDE_EFFECTING),
    )(*srcs, *lands, after)
    state = (flags, outs[0], outs[1], outs[2:2 + n], outs[2 + n:2 + 2 * n])
    return state, outs[-1]


def _exchange_wait(name, state, *after):
    flags, send_sems, recv_sems, srcs, lands = state
    n = len(flags)

    def body(*refs):
        src_refs, land_refs = refs[:n], refs[n:2 * n]
        send_ref, recv_ref = refs[2 * n], refs[2 * n + 1]
        _, peers = _mesh_peers()
        for k, peer, slab in peers:
            for i in range(n):
                s = (k - 1) * n + i
                copy = pltpu.make_async_remote_copy(
                    src_ref=src_refs[i].at[slab] if flags[i] else src_refs[i], dst_ref=land_refs[i].at[slab],
                    send_sem=send_ref.at[s], recv_sem=recv_ref.at[s], device_id=peer,
                    device_id_type=pl.DeviceIdType.MESH)
                copy.wait_send()
                copy.wait_recv()

    outs = pl.pallas_call(
        body, name=name,
        out_shape=(*[pltpu.HBM(a.shape, a.dtype) for a in srcs], *[pltpu.HBM(a.shape, a.dtype) for a in lands]),
        in_specs=[_HBM] * (2 * n) + [_SEM, _SEM] + [pl.BlockSpec(memory_space=pl.ANY)] * len(after),
        out_specs=tuple([_HBM] * (2 * n)),
        input_output_aliases={i: i for i in range(2 * n)},
        compiler_params=pltpu.CompilerParams(has_side_effects=pltpu.SideEffectType.DATAFLOW_SIDE_EFFECTING),
    )(*srcs, *lands, send_sems, recv_sems, *after)
    return list(outs[n:])


def _join_cols(parts, name, tr):
    _, r, c = parts.shape

    def body(p_ref, o_ref):
        for j in range(N_DEV):
            o_ref[:, j * c:(j + 1) * c] = p_ref[j]

    return _pcall(body, name, (r // tr,), [pl.BlockSpec((N_DEV, tr, c), lambda i: (0, i, 0))],
                  pl.BlockSpec((tr, N_DEV * c), lambda i: (i, 0)), _sds((r, N_DEV * c), parts.dtype))(parts)


def _split_cols(full, name, tr):
    r, c = full.shape[0], full.shape[1] // N_DEV

    def body(f_ref, o_ref):
        for j in range(N_DEV):
            o_ref[j] = f_ref[:, j * c:(j + 1) * c]

    return _pcall(body, name, (r // tr,), [pl.BlockSpec((tr, N_DEV * c), lambda i: (i, 0))],
                  pl.BlockSpec((N_DEV, tr, c), lambda i: (0, i, 0)), _sds((N_DEV, r, c), full.dtype))(full)


def _adamw(parts, w, m, v, name, tile):
    _, rows, cols = w.shape

    def body(p_ref, w_ref, m_ref, v_ref, g_out, d_out, m_out, v_out):
        g = p_ref[0].astype(F32)
        for k in range(1, N_DEV):
            g = g + p_ref[k].astype(F32)
        m1 = ADAM_B1 * m_ref[0] + (1.0 - ADAM_B1) * g
        v1 = ADAM_B2 * v_ref[0] + (1.0 - ADAM_B2) * (g * g)
        m_hat = m1 / (1.0 - ADAM_B1 ** ADAM_STEP)
        v_hat = v1 / (1.0 - ADAM_B2 ** ADAM_STEP)
        g_out[0] = g
        d_out[0] = -ADAM_LR * (m_hat / (jnp.sqrt(v_hat) + ADAM_EPS) + ADAM_WD * w_ref[0])
        m_out[0] = m1
        v_out[0] = v1

    row = pl.BlockSpec((1, tile, cols), lambda i: (0, i, 0))
    return _pcall(body, name, (rows // tile,),
                  [pl.BlockSpec((N_DEV, tile, cols), lambda i: (0, i, 0)), row, row, row],
                  [row, row, row, row], [_sds((1, rows, cols))] * 4)(parts, w, m, v)


BIG = {
    "w_in": ((N_IN // N_DEV, D_MODEL), False, N_IN // N_DEV // 3),
    "w_glu": ((S5_WIDTH // N_DEV, S5_WIDTH), False, S5_WIDTH // N_DEV),
    "w_pa": ((S5_WIDTH, D_MODEL // N_DEV), True, S5_WIDTH),
    "w_pb": ((HG_WIDTH, D_MODEL // N_DEV), True, HG_WIDTH),
    "w_out": ((D_MODEL // N_DEV, D_MODEL), False, D_MODEL // N_DEV),
    "w_up": ((2 * D_FF // N_DEV, D_MODEL), False, 2 * D_FF // N_DEV // 4),
    "w_conv": ((CONV_W, 2 * D_FF // N_DEV), True, CONV_W),
    "w_down": ((D_FF // N_DEV, D_MODEL), False, D_FF // N_DEV // 2),
}
TRANSPOSED = ("w_in", "w_up")
UNALIGNED_COLS = ("w_conv",)


def _stored(n, arr):
    return jnp.swapaxes(arr, 1, 2) if n in TRANSPOSED else arr


def _join_shards(n, parts):
    (a, b), by_cols, _ = BIG[n]
    if not by_cols:
        return parts.reshape(N_DEV * a, b)
    if n in UNALIGNED_COLS:
        return _join_cols(parts, "join_" + n, min(a, 256))
    return parts.transpose(1, 0, 2).reshape(a, N_DEV * b)


def _split_shards(n, full):
    (a, b), by_cols, _ = BIG[n]
    if not by_cols:
        return full.reshape(N_DEV, a, b)
    if n in UNALIGNED_COLS:
        return _split_cols(full, "split_" + n, min(a, 256))
    return full.reshape(a, N_DEV, b).transpose(1, 0, 2)


SMALL_CORE = {
    "s5_b_re": (S5_GROUPS, S5_STATE, S5_GROUP), "s5_b_im": (S5_GROUPS, S5_STATE, S5_GROUP),
    "s5_c_re": (S5_GROUPS, S5_GROUP, S5_STATE), "s5_c_im": (S5_GROUPS, S5_GROUP, S5_STATE),
    "g_mix": (1, D_MODEL), "g_ffn": (1, D_MODEL), "g_final": (1, D_MODEL), "s5_d": (1, S5_WIDTH),
    "b_glu": (1, S5_WIDTH), "hg_norm_gain": (1, HG_WIDTH), "hg_lb_logits": (2, HG_WIDTH), "b_conv": (1, 2 * D_FF),
    "s5_log_dt": (1, S5_GROUPS), "s5_a_re": (S5_GROUPS, S5_STATE), "s5_a_im": (S5_GROUPS, S5_STATE), "loss": (1, 1),
}
BLOCK_ROWS = 32


def _small_rows():
    rows, r = {}, 0
    for n, core in SMALL_CORE.items():
        rows[n] = r
        r += BLOCK_ROWS if len(core) == 3 else -(-math.prod(core) // PACK_W)
    return rows, -(-r // SUBLANES) * SUBLANES


SMALL_ROW, SMALL_ROWS = _small_rows()
SMALL_B_ROWS = 2 * BLOCK_ROWS


def _small_pieces(name):
    r, core = SMALL_ROW[name], SMALL_CORE[name]
    if name in ("s5_b_re", "s5_b_im"):
        return [((k // 2, slice(32 * (k % 2), 32 * (k % 2) + 32), slice(None)), slice(r, r + BLOCK_ROWS),
                 slice(S5_GROUP * k, S5_GROUP * (k + 1))) for k in range(2 * S5_GROUPS)]
    if name in ("s5_c_re", "s5_c_im"):
        return [((g, slice(None), slice(None)), slice(r + S5_GROUP * (g % 2), r + S5_GROUP * (g % 2 + 1)),
                 slice(S5_STATE * (g // 2), S5_STATE * (g // 2 + 1))) for g in range(S5_GROUPS)]
    pieces = []
    for i in range(core[0]):
        for c0 in range(0, core[1], PACK_W):
            w, flat = min(PACK_W, core[1] - c0), i * core[1] + c0
            pieces.append(((slice(i, i + 1), slice(c0, c0 + w)), slice(r + flat // PACK_W, r + flat // PACK_W + 1),
                           slice(flat % PACK_W, flat % PACK_W + w)))
    return pieces


def _core_index(ref, name, idx):
    return (0,) * (len(ref.shape) - len(SMALL_CORE[name])) + idx


def _pack_small_grads(grads):
    names = list(SMALL_CORE)

    def body(*refs):
        pack = refs[-1]
        pack[...] = jnp.zeros_like(pack)
        for ref, n in zip(refs, names):
            for idx, rows, lanes in _small_pieces(n):
                pack[rows, lanes] = ref[_core_index(ref, n, idx)]

    return _pcall(body, "pack_small_grads", (1,), [_full(grads[n].shape) for n in names],
                  _full((SMALL_ROWS, PACK_W)), _sds((SMALL_ROWS, PACK_W)))(*[grads[n] for n in names])


def _adamw_small(parts, names, rows, given, name):
    lo, hi = rows
    k = len(names)
    shapes = [given[n].shape for n in names]

    def body(*refs):
        p_ref, ins, outs = refs[0], refs[1:1 + 3 * k], refs[1 + 3 * k:1 + 7 * k]
        packs, results = refs[1 + 7 * k:4 + 7 * k], refs[4 + 7 * k:]
        for j, pack in enumerate(packs):
            pack[...] = jnp.zeros_like(pack)
            for ref, n in zip(ins[j * k:(j + 1) * k], names):
                for idx, prow, lanes in _small_pieces(n):
                    pack[slice(prow.start - lo, prow.stop - lo), lanes] = ref[_core_index(ref, n, idx)]
        g = p_ref[0, lo:hi, :]
        for d in range(1, N_DEV):
            g = g + p_ref[d, lo:hi, :]
        m1 = ADAM_B1 * packs[1][...] + (1.0 - ADAM_B1) * g
        v1 = ADAM_B2 * packs[2][...] + (1.0 - ADAM_B2) * (g * g)
        m_hat = m1 / (1.0 - ADAM_B1 ** ADAM_STEP)
        v_hat = v1 / (1.0 - ADAM_B2 ** ADAM_STEP)
        results[0][...] = g
        results[1][...] = -ADAM_LR * (m_hat / (jnp.sqrt(v_hat) + ADAM_EPS) + ADAM_WD * packs[0][...])
        results[2][...] = m1
        results[3][...] = v1
        for j, result in enumerate(results):
            for ref, n in zip(outs[j * k:(j + 1) * k], names):
                for idx, prow, lanes in _small_pieces(n):
                    ref[_core_index(ref, n, idx)] = result[slice(prow.start - lo, prow.stop - lo), lanes]

    flat = _pcall(body, name, (1,),
                  [_full(parts.shape)] + [_full(s) for s in shapes] * 3,
                  [_full(s) for s in shapes] * 4, [_sds(s) for s in shapes] * 4,
                  scratch=[pltpu.VMEM((hi - lo, PACK_W), F32)] * 7,
                  )(parts, *[given[pre + n] for pre in ("", "m_", "v_") for n in names])
    return {n: [flat[j * k + i] for j in range(4)] for i, n in enumerate(names)}


def kernel(x, g_mix, w_in, s5_a_re, s5_a_im, s5_log_dt, s5_b_re, s5_b_im, s5_c_re, s5_c_im, s5_d, w_glu, b_glu, hg_lb_logits, hg_norm_gain, w_pa, w_pb, w_out, g_ffn, w_up, w_conv, b_conv, w_down, g_final, loss_target, m_g_mix, m_w_in, m_s5_a_re, m_s5_a_im, m_s5_log_dt, m_s5_b_re, m_s5_b_im, m_s5_c_re, m_s5_c_im, m_s5_d, m_w_glu, m_b_glu, m_hg_lb_logits, m_hg_norm_gain, m_w_pa, m_w_pb, m_w_out, m_g_ffn, m_w_up, m_w_conv, m_b_conv, m_w_down, m_g_final, v_g_mix, v_w_in, v_s5_a_re, v_s5_a_im, v_s5_log_dt, v_s5_b_re, v_s5_b_im, v_s5_c_re, v_s5_c_im, v_s5_d, v_w_glu, v_b_glu, v_hg_lb_logits, v_hg_norm_gain, v_w_pa, v_w_pb, v_w_out, v_g_ffn, v_w_up, v_w_conv, v_b_conv, v_w_down, v_g_final):
    given = dict(locals())
    small_names = [n for n, _ in SMALL]

    pay = {n: given[n][0] if n == "w_conv" else _stored(n, given[n])[0].astype(BF16) for n in BIG}
    groups = {"in": ["w_in"], "mix": ["w_glu", "w_pa", "w_pb", "w_out"], "ffn": ["w_up", "w_down", "w_conv"]}
    gathers, order = {}, pay["w_in"]
    for grp, names in groups.items():
        gathers[grp], order = _exchange_start("gather_" + grp + "_start", [(pay[n], False) for n in names], order)

    def weights(grp, *after):
        if grp == "in":
            after = (*after, order)
        got = _exchange_wait("gather_" + grp + "_wait", gathers[grp], *after)
        return {n: _join_shards(n, g) for n, g in zip(groups[grp], got)}

    in_flight, started = [], []

    def emit(grads):
        names = list(grads)
        state, token = _exchange_start("grads_" + names[0] + "_start",
                                       [(_split_shards(n, grads[n]), True) for n in names], grads[names[0]])
        in_flight.append((names, state))
        return token

    def emit_small(grads):
        pack = _pack_small_grads(grads)
        state, token = _exchange_start("grads_small_start", [(pack, False)], pack)
        in_flight.append((["small"], state))
        started.append(token)

    sp = {n: (given[n] if n in ("g_final", "hg_lb_logits") else given[n][0]) for n in small_names}
    sp["g_mix"] = _after(sp["g_mix"], order)
    dx = _local_step(x, loss_target, weights, sp, emit, emit_small)

    res = {}
    after = [started[-1]]
    in_flight.insert(-1, in_flight.pop())
    for names, state in in_flight:
        parts = _exchange_wait("grads_" + names[0] + "_wait", state, *after)
        if names != ["small"]:
            after = []
            for n, part in zip(names, parts):
                raw = _adamw(part, *[_stored(n, given[pre + n]) for pre in ("", "m_", "v_")], "adamw_" + n, BIG[n][2])
                res[n] = [_stored(n, r) for r in raw]
                after.append(raw[0])
            continue
        sgiven = dict(given)
        for pre in ("", "m_", "v_"):
            sgiven[pre + "g_final"] = given[pre + "g_final"].reshape(1, D_MODEL)
            sgiven[pre + "loss"] = jnp.zeros((1, 1), F32)
        b_names = ["s5_b_re", "s5_b_im"]
        res.update(_adamw_small(parts[0], b_names, (0, SMALL_B_ROWS), sgiven, "adamw_small_b"))
        res.update(_adamw_small(parts[0], [n for n in SMALL_CORE if n not in b_names], (SMALL_B_ROWS, SMALL_ROWS),
                                sgiven, "adamw_small_rest"))
        res["g_final"] = [r.reshape(D_MODEL) for r in res["g_final"]]
        total_loss = res["loss"][0].reshape(())
        after = [res["s5_b_re"][0], res["g_mix"][0]]
    return (total_loss, dx, *[res[n][0] for n in WEIGHT_ORDER], *[res[n][1] for n in WEIGHT_ORDER],
            *[res[n][2] for n in WEIGHT_ORDER], *[res[n][3] for n in WEIGHT_ORDER])
```

```python
import functools
import math

import jax
import jax.numpy as jnp
from jax import lax
from jax.experimental import pallas as pl
from jax.experimental.pallas import tpu as pltpu

F32 = jnp.float32
BF16 = jnp.bfloat16

D_MODEL = 1024
S5_WIDTH = 512
S5_GROUP = 16
S5_GROUPS = 32
S5_STATE = 64
S5_N = S5_GROUPS * S5_STATE
HG_WIDTH = 512
HG_HEAD = 128
HG_HEADS = 4
D_FF = 2816
CONV_W = 3
CHUNK = 64
N_IN = S5_WIDTH + 4 * HG_WIDTH + 2 * D_MODEL
EPS = 1e-6
QSCALE = HG_HEAD ** -0.5

ADAM_LR = 0.001
ADAM_B1 = 0.9
ADAM_B2 = 0.999
ADAM_EPS = 1e-08
ADAM_WD = 0.01
ADAM_STEP = 10

N_DEV = 8
V7X_VMEM_BYTES = 64 * 1024 * 1024
VMEM_LIMIT = V7X_VMEM_BYTES * 7 // 8
SUBLANES = 8
PACK_W = 1024

SMALL = (
    ("g_mix", (1, D_MODEL)),
    ("s5_a_re", (1, S5_GROUPS, S5_STATE)),
    ("s5_a_im", (1, S5_GROUPS, S5_STATE)),
    ("s5_log_dt", (1, S5_GROUPS)),
    ("s5_b_re", (1, S5_GROUPS, S5_STATE, S5_GROUP)),
    ("s5_b_im", (1, S5_GROUPS, S5_STATE, S5_GROUP)),
    ("s5_c_re", (1, S5_GROUPS, S5_GROUP, S5_STATE)),
    ("s5_c_im", (1, S5_GROUPS, S5_GROUP, S5_STATE)),
    ("s5_d", (1, S5_WIDTH)),
    ("b_glu", (1, S5_WIDTH)),
    ("hg_lb_logits", (2, HG_WIDTH)),
    ("hg_norm_gain", (1, HG_WIDTH)),
    ("g_ffn", (1, D_MODEL)),
    ("b_conv", (1, 2 * D_FF)),
    ("g_final", (D_MODEL,)),
)
WEIGHT_ORDER = ("g_mix", "w_in", "s5_a_re", "s5_a_im", "s5_log_dt", "s5_b_re", "s5_b_im", "s5_c_re", "s5_c_im",
                "s5_d", "w_glu", "b_glu", "hg_lb_logits", "hg_norm_gain", "w_pa", "w_pb", "w_out", "g_ffn",
                "w_up", "w_conv", "b_conv", "w_down", "g_final")


def _pcall(body, name, grid, in_specs, out_specs, out_shape, scratch=()):
    return pl.pallas_call(
        body, name=name, grid=grid, in_specs=in_specs, out_specs=out_specs, out_shape=out_shape,
        scratch_shapes=list(scratch),
        compiler_params=pltpu.CompilerParams(dimension_semantics=("arbitrary",) * len(grid),
                                             vmem_limit_bytes=VMEM_LIMIT),
    )


def _full(shape):
    return pl.BlockSpec(shape, lambda *_: (0,) * len(shape))


def _sds(shape, dtype=F32):
    return jax.ShapeDtypeStruct(shape, dtype)


def _dot(a, b):
    return jnp.dot(a.astype(BF16), b.astype(BF16), preferred_element_type=F32)


def _dot_nt(a, b):
    return lax.dot_general(a.astype(BF16), b.astype(BF16), (((1,), (1,)), ((), ())), preferred_element_type=F32)


def _dot_tn(a, b):
    return lax.dot_general(a.astype(BF16), b.astype(BF16), (((0,), (0,)), ((), ())), preferred_element_type=F32)


def _hdot(a, b):
    return jnp.dot(a, b, preferred_element_type=F32, precision=lax.Precision.HIGHEST)


def _hdot_tn(a, b):
    return lax.dot_general(a, b, (((0,), (0,)), ((), ())), preferred_element_type=F32,
                           precision=lax.Precision.HIGHEST)


def _sigmoid(x):
    return jax.nn.sigmoid(x)


GELU_C = math.sqrt(2.0 / math.pi)
GELU_A = 0.044715


def _gelu(x):
    return 0.5 * x * (1.0 + jnp.tanh(GELU_C * (x + GELU_A * (x * x * x))))


def _gelu_grad(x):
    t = jnp.tanh(GELU_C * (x + GELU_A * (x * x * x)))
    return 0.5 * (1.0 + t) + 0.5 * x * (1.0 - t * t) * (GELU_C * (1.0 + 3.0 * GELU_A * x * x))


def _cumsum_rows(v, reverse=False):
    n = v.shape[0]
    row = lax.broadcasted_iota(jnp.int32, v.shape, 0)
    s = 1
    while s < n:
        if reverse:
            v = v + jnp.where(row < n - s, pltpu.roll(v, n - s, axis=0), 0.0)
        else:
            v = v + jnp.where(row >= s, pltpu.roll(v, s, axis=0), 0.0)
        s *= 2
    return v


def _token_tile(seq):
    return min(256, seq)


def _s5_coeffs(a_re, a_im, ldt):
    dt = jnp.exp(ldt)
    mag = jnp.exp(a_re * dt)
    ang = a_im * dt
    lb_re = mag * jnp.cos(ang)
    lb_im = mag * jnp.sin(ang)
    den = a_re * a_re + a_im * a_im
    n_re = lb_re - 1.0
    n_im = lb_im
    co_re = (n_re * a_re + n_im * a_im) / den
    co_im = (n_im * a_re - n_re * a_im) / den
    return lb_re, lb_im, co_re, co_im


GS, GSC = (S5_GROUPS, S5_STATE), (S5_GROUPS, S5_GROUP, S5_STATE)


def _params_fwd(a_re, a_im, ldt, bt_re, bt_im, logits):
    def body(are, aim, ld, bre, bim, lg, lr_o, li_o, bbr_o, bbi_o, lb_o):
        lr, li, co_re, co_im = _s5_coeffs(are[...], aim[...], ld[...])
        lr_o[...] = lr
        li_o[...] = li
        for g in range(S5_GROUPS):
            cr, ci = co_re[g:g + 1, :], co_im[g:g + 1, :]
            bbr_o[g] = cr * bre[g] - ci * bim[g]
            bbi_o[g] = cr * bim[g] + ci * bre[g]
        lb_o[...] = _sigmoid(lg[0:1, :] - lg[1:2, :])

    return _pcall(body, "params_fwd", (1,),
                  [_full(GS), _full(GS), _full((S5_GROUPS, 1)), _full(GSC), _full(GSC), _full((2, HG_WIDTH))],
                  [_full(GS), _full(GS), _full(GSC), _full(GSC), _full((1, HG_WIDTH))],
                  [_sds(GS), _sds(GS), _sds(GSC), _sds(GSC), _sds((1, HG_WIDTH))],
                  )(a_re, a_im, ldt, bt_re, bt_im, logits)


def _params_bwd(a_re, a_im, ldt, bt_re, bt_im, logits, dlr, dli, dbbr, dbbi, dlb):
    def body(are, aim, ld, bre, bim, lg, dlr_r, dli_r, dbbr_r, dbbi_r, dlb_r,
             dare_o, daim_o, dld_o, dbre_o, dbim_o, dlg_o, dcr_ref, dci_ref):
        (_, _, co_re, co_im), vjp = jax.vjp(_s5_coeffs, are[...], aim[...], ld[...])
        for g in range(S5_GROUPS):
            cr, ci = co_re[g:g + 1, :], co_im[g:g + 1, :]
            gr, gi, br, bi = dbbr_r[g], dbbi_r[g], bre[g], bim[g]
            dbre_o[g] = cr * gr + ci * gi
            dbim_o[g] = cr * gi - ci * gr
            dcr_ref[g:g + 1, :] = jnp.sum(gr * br + gi * bi, axis=0, keepdims=True)
            dci_ref[g:g + 1, :] = jnp.sum(gi * br - gr * bi, axis=0, keepdims=True)
        dare, daim, dld = vjp((dlr_r[...], dli_r[...], dcr_ref[...], dci_ref[...]))
        dare_o[...] = dare
        daim_o[...] = daim
        dld_o[...] = dld
        lb = _sigmoid(lg[0:1, :] - lg[1:2, :])
        d0 = dlb_r[...] * lb * (1.0 - lb)
        dlg_o[0:1, :] = d0
        dlg_o[1:2, :] = -d0

    return _pcall(body, "params_bwd", (1,),
                  [_full(GS), _full(GS), _full((S5_GROUPS, 1)), _full(GSC), _full(GSC), _full((2, HG_WIDTH)),
                   _full(GS), _full(GS), _full(GSC), _full(GSC), _full((1, HG_WIDTH))],
                  [_full(GS), _full(GS), _full((S5_GROUPS, 1)), _full(GSC), _full(GSC), _full((2, HG_WIDTH))],
                  [_sds(GS), _sds(GS), _sds((S5_GROUPS, 1)), _sds(GSC), _sds(GSC), _sds((2, HG_WIDTH))],
                  scratch=[pltpu.VMEM(GS, F32), pltpu.VMEM(GS, F32)],
                  )(a_re, a_im, ldt, bt_re, bt_im, logits, dlr, dli, dbbr, dbbi, dlb)


def _band_blocks(m):
    g, r, c = m.shape
    gb = g // S5_BANDS
    m4 = m.astype(BF16).reshape(S5_BANDS, gb, r, c)
    on_diag = jnp.eye(gb, dtype=bool)[None, :, None, :, None]
    return jnp.where(on_diag, m4[:, :, :, None, :], 0).reshape(S5_BANDS, gb * r, gb * c)


def _diag_blocks(band, r, c):
    g, nb = band.shape[0] // r, band.shape[1] // c
    on_diag = (jnp.arange(g) % nb)[:, None, None, None] == jnp.arange(nb)[None, None, :, None]
    return jnp.sum(jnp.where(on_diag, band.reshape(g, r, nb, c), 0.0), axis=2)


def _in_proj(x, g_mix, w_in, tm):
    t = x.shape[0]

    def body(x_ref, g_ref, w_ref, u_ref, za_ref, zh_ref, zg_ref):
        xv = x_ref[...]
        r = lax.rsqrt(jnp.mean(xv * xv, axis=-1, keepdims=True) + EPS)
        u = (xv * r * g_ref[...]).astype(BF16)
        u_ref[...] = u
        za_ref[...] = _dot_nt(u, w_ref[0:S5_WIDTH, :])
        zh_ref[...] = _dot_nt(u, w_ref[S5_WIDTH:S5_WIDTH + 4 * HG_WIDTH, :])
        zg_ref[...] = _dot_nt(u, w_ref[S5_WIDTH + 4 * HG_WIDTH:, :]).astype(BF16)

    row = lambda w: pl.BlockSpec((tm, w), lambda i: (i, 0))
    return _pcall(body, "in_proj", (t // tm,),
                  [row(D_MODEL), _full((1, D_MODEL)), _full((N_IN, D_MODEL))],
                  [row(D_MODEL), row(S5_WIDTH), row(4 * HG_WIDTH), row(2 * D_MODEL)],
                  [_sds((t, D_MODEL), BF16), _sds((t, S5_WIDTH)), _sds((t, 4 * HG_WIDTH)),
                   _sds((t, 2 * D_MODEL), BF16)],
                  )(x, g_mix, w_in)


S5_LANES = 512
S5_BANDS = 4


def _band(q):
    return (slice(q * S5_WIDTH // S5_BANDS, (q + 1) * S5_WIDTH // S5_BANDS),
            slice(q * S5_N // S5_BANDS, (q + 1) * S5_N // S5_BANDS))


def _im(st):
    return slice(S5_N + st.start, S5_N + st.stop)


SCAN_UNROLL = 8


def _complex_scan(buf_ref, lam_ref, st_ref, ts, reverse):
    chunks = [slice(cc * S5_LANES, (cc + 1) * S5_LANES) for cc in range(S5_N // S5_LANES)]
    nch = len(chunks)
    wr = [lam_ref[0:1, re] for re in chunks]
    wi = [-lam_ref[1:2, re] if reverse else lam_ref[1:2, re] for re in chunks]

    def block(ib, carry):
        vr, vi = list(carry[:nch]), list(carry[nch:])
        first = ts - SCAN_UNROLL - ib * SCAN_UNROLL if reverse else ib * SCAN_UNROLL
        first = pl.multiple_of(first, SCAN_UNROLL)
        for k in range(SCAN_UNROLL):
            row = pl.ds(first + (SCAN_UNROLL - 1 - k if reverse else k), 1)
            for cc, re in enumerate(chunks):
                nr = wr[cc] * vr[cc] - wi[cc] * vi[cc] + buf_ref[row, re]
                ni = wr[cc] * vi[cc] + wi[cc] * vr[cc] + buf_ref[row, _im(re)]
                buf_ref[row, re] = nr
                buf_ref[row, _im(re)] = ni
                vr[cc], vi[cc] = nr, ni
        return tuple(vr + vi)

    init = tuple(st_ref[0:1, re] for re in chunks) + tuple(st_ref[1:2, re] for re in chunks)
    last = lax.fori_loop(0, ts // SCAN_UNROLL, block, init)
    for cc, re in enumerate(chunks):
        st_ref[0:1, re] = last[cc]
        st_ref[1:2, re] = last[nch + cc]


BAND_CH = S5_WIDTH // S5_BANDS
BAND_ST = S5_N // S5_BANDS


def _s5_fwd(za, b_bands, lam, c_bands, dskip, nb, seq, ts):
    t = za.shape[0]
    nts = seq // ts

    def body(za_ref, br_ref, bi_ref, lam_ref, cr_ref, ci_ref, d_ref, xs_ref, y_ref, buf_ref, st_ref):
        @pl.when(pl.program_id(1) == 0)
        def _():
            st_ref[...] = jnp.zeros_like(st_ref)

        zav = za_ref[...]
        for q in range(S5_BANDS):
            ch, st = _band(q)
            buf_ref[:, st] = _dot(zav[:, ch], br_ref[q])
            buf_ref[:, _im(st)] = _dot(zav[:, ch], bi_ref[q])
        _complex_scan(buf_ref, lam_ref, st_ref, ts, reverse=False)
        xs_ref[...] = buf_ref[...].astype(BF16)
        for q in range(S5_BANDS):
            ch, st = _band(q)
            y_ref[:, ch] = (_dot(xs_ref[:, st], cr_ref[q]) + _dot(xs_ref[:, _im(st)], ci_ref[q])
                            + d_ref[:, ch] * zav[:, ch])

    tok = lambda w: pl.BlockSpec((ts, w), lambda b, j: (b * nts + j, 0))
    to_st, to_ch = _full((S5_BANDS, BAND_CH, BAND_ST)), _full((S5_BANDS, BAND_ST, BAND_CH))
    return _pcall(body, "s5_fwd", (nb, nts),
                  [tok(S5_WIDTH), to_st, to_st, _full((2, S5_N)), to_ch, to_ch, _full((1, S5_WIDTH))],
                  [tok(2 * S5_N), tok(S5_WIDTH)],
                  [_sds((t, 2 * S5_N), BF16), _sds((t, S5_WIDTH))],
                  scratch=[pltpu.VMEM((ts, 2 * S5_N), F32), pltpu.VMEM((2, S5_N), F32)],
                  )(za, *b_bands, lam, *c_bands, dskip)


def _hgrn_gates(zq, zf, lbh):
    sf = _sigmoid(zf)
    f = lbh + (1.0 - lbh) * sf
    sq = _sigmoid(zq)
    qa = zq * sq * QSCALE
    bc = _cumsum_rows(jnp.log(f))
    bm = bc[CHUNK // 2 - 1:CHUNK // 2, :]
    bl = bc[CHUNK - 1:CHUNK, :]
    return sf, f, sq, qa, bc, bm, bl


def _hgrn_fwd(zh, lb, nb, seq):
    nc = seq // CHUNK

    def body(zh_ref, lb_ref, o_ref, sts_ref, st_ref):
        @pl.when(pl.program_id(0) == 0)
        def _():
            st_ref[...] = jnp.zeros_like(st_ref)

        causal = (lax.broadcasted_iota(jnp.int32, (CHUNK, CHUNK), 0)
                  >= lax.broadcasted_iota(jnp.int32, (CHUNK, CHUNK), 1))
        for b in range(nb):
            for h in range(HG_HEADS):
                hs = slice(h * HG_HEAD, (h + 1) * HG_HEAD)
                zq = zh_ref[b, :, h * HG_HEAD:(h + 1) * HG_HEAD]
                zf = zh_ref[b, :, HG_WIDTH + h * HG_HEAD:HG_WIDTH + (h + 1) * HG_HEAD]
                zi = zh_ref[b, :, 2 * HG_WIDTH + h * HG_HEAD:2 * HG_WIDTH + (h + 1) * HG_HEAD]
                _, f, _, qa, bc, bm, bl = _hgrn_gates(zq, zf, lb_ref[:, hs])
                k = 1.0 - f
                qt = qa * jnp.exp(bc - bm)
                kt = k * jnp.exp(bm - bc)
                qb = qa * jnp.exp(bc)
                kd = k * jnp.exp(bl - bc)
                st = st_ref[b, h]
                sts_ref[b, 0, h] = st
                a = jnp.where(causal, _dot_nt(qt, kt), 0.0)
                o_ref[b, :, hs] = _dot(a, zi) + _dot_nt(qb, st)
                st_ref[b, h] = st * jnp.exp(bl) + _dot_tn(zi, kd)

    return _pcall(body, "hgrn_fwd", (nc,),
                  [pl.BlockSpec((nb, CHUNK, 4 * HG_WIDTH), lambda c: (0, c, 0)), _full((1, HG_WIDTH))],
                  [pl.BlockSpec((nb, CHUNK, HG_WIDTH), lambda c: (0, c, 0)),
                   pl.BlockSpec((nb, 1, HG_HEADS, HG_HEAD, HG_HEAD), lambda c: (0, c, 0, 0, 0))],
                  [_sds((nb, seq, HG_WIDTH)), _sds((nb, nc, HG_HEADS, HG_HEAD, HG_HEAD))],
                  scratch=[pltpu.VMEM((nb, HG_HEADS, HG_HEAD, HG_HEAD), F32)])(zh, lb)


def _head_rms(o):
    parts = []
    for h in range(HG_HEADS):
        oh = o[:, h * HG_HEAD:(h + 1) * HG_HEAD]
        r = lax.rsqrt(jnp.mean(oh * oh, axis=-1, keepdims=True) + EPS)
        parts.append(jnp.broadcast_to(r, oh.shape))
    return jnp.concatenate(parts, axis=1)


def _head_mean(v):
    parts = []
    for h in range(HG_HEADS):
        vh = v[:, h * HG_HEAD:(h + 1) * HG_HEAD]
        parts.append(jnp.broadcast_to(jnp.mean(vh, axis=-1, keepdims=True), vh.shape))
    return jnp.concatenate(parts, axis=1)


def _mix_fwd(x, y0, o, zh, zgt, w_glu, b_glu, gain, w_pa, w_pb, w_out, g_ffn, tm):
    t = x.shape[0]

    def body(x_ref, y0_ref, o_ref, zg_ref, zgt_ref, wglu_ref, bglu_ref, gain_ref, wpa_ref, wpb_ref, wout_ref,
             gffn_ref, x1_ref, u2_ref, pa_ref, pb_ref, ya2_ref, yb_ref):
        ya1 = _gelu(y0_ref[...])
        s = _sigmoid(_dot(ya1, wglu_ref[...]) + bglu_ref[...])
        ya2 = (ya1 * s).astype(BF16)
        ov = o_ref[...]
        zg = zg_ref[...]
        yb = (ov * _head_rms(ov) * gain_ref[...] * (zg * _sigmoid(zg))).astype(BF16)
        ya2_ref[...] = ya2
        yb_ref[...] = yb
        pa = jnp.dot(ya2, wpa_ref[...], preferred_element_type=F32)
        pb = jnp.dot(yb, wpb_ref[...], preferred_element_type=F32)
        pa_ref[...] = pa.astype(BF16)
        pb_ref[...] = pb.astype(BF16)
        m = (_sigmoid(zgt_ref[:, 0:D_MODEL].astype(F32)) * pa
             + _sigmoid(zgt_ref[:, D_MODEL:].astype(F32)) * pb)
        x1 = x_ref[...] + _dot(m, wout_ref[...])
        x1_ref[...] = x1
        r = lax.rsqrt(jnp.mean(x1 * x1, axis=-1, keepdims=True) + EPS)
        u2_ref[...] = (x1 * r * gffn_ref[...]).astype(BF16)

    row = lambda w: pl.BlockSpec((tm, w), lambda i: (i, 0))
    return _pcall(body, "mix_fwd", (t // tm,),
                  [row(D_MODEL), row(S5_WIDTH), row(HG_WIDTH), pl.BlockSpec((tm, HG_WIDTH), lambda i: (i, 3)),
                   row(2 * D_MODEL), _full((S5_WIDTH, S5_WIDTH)), _full((1, S5_WIDTH)), _full((1, HG_WIDTH)),
                   _full((S5_WIDTH, D_MODEL)), _full((HG_WIDTH, D_MODEL)), _full((D_MODEL, D_MODEL)),
                   _full((1, D_MODEL))],
                  [row(D_MODEL), row(D_MODEL), row(D_MODEL), row(D_MODEL), row(S5_WIDTH), row(HG_WIDTH)],
                  [_sds((t, D_MODEL)), _sds((t, D_MODEL), BF16), _sds((t, D_MODEL), BF16), _sds((t, D_MODEL), BF16),
                   _sds((t, S5_WIDTH), BF16), _sds((t, HG_WIDTH), BF16)],
                  )(x, y0, o, zh, zgt, w_glu, b_glu, gain, w_pa, w_pb, w_out, g_ffn)


FF_COLS = 256
FF_UP_TILE = 1408


def _ffn_up(u2, w_up, tm):
    t = u2.shape[0]
    n = 2 * D_FF

    def body(u_ref, w_ref, h_ref):
        h_ref[...] = _dot_nt(u_ref[...], w_ref[...]).astype(BF16)

    return _pcall(body, "ffn_up", (n // FF_UP_TILE, t // tm),
                  [pl.BlockSpec((tm, D_MODEL), lambda j, i: (i, 0)),
                   pl.BlockSpec((FF_UP_TILE, D_MODEL), lambda j, i: (j, 0))],
                  pl.BlockSpec((tm, FF_UP_TILE), lambda j, i: (i, j)),
                  _sds((t, n), BF16))(u2, w_up)


HALO = 16


def _shift_matrix(tm):
    r = lax.broadcasted_iota(jnp.int32, (tm, tm), 0)
    c = lax.broadcasted_iota(jnp.int32, (tm, tm), 1)
    return jnp.where(r == c + 1, 1.0, 0.0).astype(BF16)


def _conv_cols(h_ref, halo_ref, valid, wc_ref, bc_ref, c0):
    cs = slice(c0, c0 + FF_COLS)
    cur = h_ref[:, cs].astype(F32)
    prev = jnp.where(valid, halo_ref[:, cs].astype(F32), 0.0)
    full = jnp.concatenate([prev, cur], axis=0)
    h1 = pltpu.roll(full, 1, axis=0)[HALO:]
    h2 = pltpu.roll(full, 2, axis=0)[HALO:]
    return h2 * wc_ref[0:1, cs] + h1 * wc_ref[1:2, cs] + cur * wc_ref[2:3, cs] + bc_ref[:, cs]


def _ffn_down_loss(h, x1, tgt, w_conv, b_conv, w_down, g_final, seq, tm):
    t = h.shape[0]
    tps = seq // tm
    n = 2 * D_FF

    def body(h_ref, halo_ref, x1_ref, tgt_ref, wc_ref, bc_ref, wd_ref, gf_ref,
             hc_ref, a_ref, dx2_ref, dx2b_ref, loss_ref, dgf_ref):
        i = pl.program_id(0)

        @pl.when(i == 0)
        def _():
            loss_ref[...] = jnp.zeros_like(loss_ref)
            dgf_ref[...] = jnp.zeros_like(dgf_ref)

        valid = (i % tps) != 0
        x2 = x1_ref[...]
        for j in range(D_FF // FF_COLS):
            gate = _conv_cols(h_ref, halo_ref, valid, wc_ref, bc_ref, j * FF_COLS)
            val = _conv_cols(h_ref, halo_ref, valid, wc_ref, bc_ref, D_FF + j * FF_COLS)
            hc_ref[:, j * FF_COLS:(j + 1) * FF_COLS] = gate.astype(BF16)
            hc_ref[:, D_FF + j * FF_COLS:D_FF + (j + 1) * FF_COLS] = val.astype(BF16)
            a = (gate * _sigmoid(gate) * val).astype(BF16)
            a_ref[:, j * FF_COLS:(j + 1) * FF_COLS] = a
            x2 = x2 + jnp.dot(a, wd_ref[j * FF_COLS:(j + 1) * FF_COLS, :], preferred_element_type=F32)
        r = lax.rsqrt(jnp.mean(x2 * x2, axis=-1, keepdims=True) + EPS)
        xn = x2 * r
        g = gf_ref[...]
        e = xn * g - tgt_ref[...]
        loss_ref[...] += (0.5 / D_MODEL) * jnp.sum(e * e).reshape(1, 1)
        dy = e * (1.0 / D_MODEL)
        dgf_ref[...] += jnp.sum(dy * xn, axis=0, keepdims=True)
        dxn = dy * g
        dx2 = r * (dxn - xn * jnp.mean(dxn * xn, axis=-1, keepdims=True))
        dx2_ref[...] = dx2
        dx2b_ref[...] = dx2.astype(BF16)

    row = lambda w: pl.BlockSpec((tm, w), lambda i: (i, 0))
    halo = pl.BlockSpec((HALO, n), lambda i: (jnp.maximum(i * (tm // HALO) - 1, 0), 0))
    return _pcall(body, "ffn_down_loss", (t // tm,),
                  [row(n), halo, row(D_MODEL), row(D_MODEL), _full((CONV_W, n)), _full((1, n)),
                   _full((D_FF, D_MODEL)), _full((1, D_MODEL))],
                  [row(n), row(D_FF), row(D_MODEL), row(D_MODEL), _full((1, 1)), _full((1, D_MODEL))],
                  [_sds((t, n), BF16), _sds((t, D_FF), BF16), _sds((t, D_MODEL)), _sds((t, D_MODEL), BF16),
                   _sds((1, 1)), _sds((1, D_MODEL))],
                  )(h, h, x1, tgt, w_conv, b_conv, w_down, g_final)


def _wgrad(a, b, name, tn, out_dtype=F32, band=None, after=None):
    t, m = a.shape
    n = b.shape[1] if band is None else band
    nbands = 1 if band is None else b.shape[1] // band
    after = b if after is None else after

    def body(a_ref, b_ref, after_ref, o_ref):
        o_ref[...] = _dot_tn(a_ref[...], b_ref[...]).astype(out_dtype)

    return _pcall(body, name, (m // tn,),
                  [pl.BlockSpec((t, tn), lambda i: (0, i)), pl.BlockSpec((t, n), lambda i: (0, i % nbands)),
                   pl.BlockSpec(memory_space=pl.ANY)],
                  pl.BlockSpec((tn, n), lambda i: (i, 0)), _sds((m, n), out_dtype))(a, b, after)


def _ffn_bwd_act(dx2b, hc, w_down, tm):
    t = hc.shape[0]
    n = 2 * D_FF

    def body(dx2_ref, hc_ref, wd_ref, dhc_ref, dbc_ref):
        @pl.when(pl.program_id(0) == 0)
        def _():
            dbc_ref[...] = jnp.zeros_like(dbc_ref)

        dx2 = dx2_ref[...]
        for j in range(D_FF // FF_COLS):
            gs = slice(j * FF_COLS, (j + 1) * FF_COLS)
            vs = slice(D_FF + j * FF_COLS, D_FF + (j + 1) * FF_COLS)
            gate = hc_ref[:, gs].astype(F32)
            val = hc_ref[:, vs].astype(F32)
            da = _dot_nt(dx2, wd_ref[gs, :])
            sg = _sigmoid(gate)
            dgate = da * val * (sg * (1.0 + gate * (1.0 - sg)))
            dval = da * (gate * sg)
            dhc_ref[:, gs] = dgate.astype(BF16)
            dhc_ref[:, vs] = dval.astype(BF16)
            dbc_ref[:, gs] += jnp.sum(dgate, axis=0, keepdims=True)
            dbc_ref[:, vs] += jnp.sum(dval, axis=0, keepdims=True)

    row = lambda w: pl.BlockSpec((tm, w), lambda i: (i, 0))
    return _pcall(body, "ffn_bwd_act", (t // tm,),
                  [row(D_MODEL), row(n), _full((D_FF, D_MODEL))],
                  [row(n), _full((1, n))],
                  [_sds((t, n), BF16), _sds((1, n))],
                  )(dx2b, hc, w_down)


def _ffn_bwd_up(dhc, h, dx2, x1, w_conv, w_up, g_ffn, seq, tm):
    t = dhc.shape[0]
    tps = seq // tm
    n = 2 * D_FF
    last = t // HALO - 1

    def body(dhc_ref, halo_ref, h_ref, dx2_ref, x1_ref, wc_ref, wu_ref, gf_ref,
             dh_ref, dx1_ref, dx1b_ref, dgf_ref, dwc_ref):
        i = pl.program_id(0)

        @pl.when(i == 0)
        def _():
            dgf_ref[...] = jnp.zeros_like(dgf_ref)
            dwc_ref[...] = jnp.zeros_like(dwc_ref)

        valid = ((i + 1) % tps) != 0
        du2 = jnp.zeros((tm, D_MODEL), F32)
        for j in range(n // FF_COLS):
            cs = slice(j * FF_COLS, (j + 1) * FF_COLS)
            cur = dhc_ref[:, cs].astype(F32)
            nxt = jnp.where(valid, halo_ref[:, cs].astype(F32), 0.0)
            full = jnp.concatenate([cur, nxt], axis=0)
            d1 = pltpu.roll(full, tm + HALO - 1, axis=0)[:tm]
            d2 = pltpu.roll(full, tm + HALO - 2, axis=0)[:tm]
            dh = (cur * wc_ref[2:3, cs] + d1 * wc_ref[1:2, cs] + d2 * wc_ref[0:1, cs]).astype(BF16)
            dh_ref[:, cs] = dh
            du2 = du2 + _dot(dh, wu_ref[cs, :])
            hv = h_ref[:, cs].astype(F32)
            dwc_ref[0:1, cs] += jnp.sum(hv * d2, axis=0, keepdims=True)
            dwc_ref[1:2, cs] += jnp.sum(hv * d1, axis=0, keepdims=True)
            dwc_ref[2:3, cs] += jnp.sum(hv * cur, axis=0, keepdims=True)
        x1 = x1_ref[...]
        r = lax.rsqrt(jnp.mean(x1 * x1, axis=-1, keepdims=True) + EPS)
        xn = x1 * r
        dgf_ref[...] += jnp.sum(du2 * xn, axis=0, keepdims=True)
        dxn = du2 * gf_ref[...]
        dx1 = dx2_ref[...] + r * (dxn - xn * jnp.mean(dxn * xn, axis=-1, keepdims=True))
        dx1_ref[...] = dx1
        dx1b_ref[...] = dx1.astype(BF16)

    row = lambda w: pl.BlockSpec((tm, w), lambda i: (i, 0))
    halo = pl.BlockSpec((HALO, n), lambda i: (jnp.minimum((i + 1) * (tm // HALO), last), 0))
    return _pcall(body, "ffn_bwd_up", (t // tm,),
                  [row(n), halo, row(n), row(D_MODEL), row(D_MODEL), _full((CONV_W, n)), _full((n, D_MODEL)),
                   _full((1, D_MODEL))],
                  [row(n), row(D_MODEL), row(D_MODEL), _full((1, D_MODEL)), _full((CONV_W, n))],
                  [_sds((t, n), BF16), _sds((t, D_MODEL)), _sds((t, D_MODEL), BF16), _sds((1, D_MODEL)),
                   _sds((CONV_W, n))],
                  )(dhc, dhc, h, dx2, x1, w_conv, w_up, g_ffn)


def _mix_bwd(dx1, y0, o, zh, zgt, pa, pb, w_glu, b_glu, gain, w_pa, w_pb, w_out, tm):
    t = dx1.shape[0]

    def body(dx1_ref, y0_ref, o_ref, zg_ref, zgt_ref, pa_ref, pb_ref, wglu_ref, bglu_ref, gain_ref, wpa_ref,
             wpb_ref, wout_ref,
             dy0_ref, do_ref, dzg_ref, dzgt_ref, m_ref, dpa_ref, dpb_ref, ya1_ref, dpre_ref, dbglu_ref, dgain_ref):
        @pl.when(pl.program_id(0) == 0)
        def _():
            dbglu_ref[...] = jnp.zeros_like(dbglu_ref)
            dgain_ref[...] = jnp.zeros_like(dgain_ref)

        dm = _dot_nt(dx1_ref[...], wout_ref[...])
        sga = _sigmoid(zgt_ref[:, 0:D_MODEL].astype(F32))
        sgb = _sigmoid(zgt_ref[:, D_MODEL:].astype(F32))
        pa = pa_ref[...].astype(F32)
        pb = pb_ref[...].astype(F32)
        m_ref[...] = (sga * pa + sgb * pb).astype(BF16)
        dzgt_ref[:, 0:D_MODEL] = (dm * pa * sga * (1.0 - sga)).astype(BF16)
        dzgt_ref[:, D_MODEL:] = (dm * pb * sgb * (1.0 - sgb)).astype(BF16)
        dpa = (dm * sga).astype(BF16)
        dpb = (dm * sgb).astype(BF16)
        dpa_ref[...] = dpa
        dpb_ref[...] = dpb
        dya2 = _dot_nt(dpa, wpa_ref[...])
        dyb = _dot_nt(dpb, wpb_ref[...])
        y0 = y0_ref[...]
        ya1 = _gelu(y0)
        ya1_ref[...] = ya1.astype(BF16)
        s = _sigmoid(_dot(ya1, wglu_ref[...]) + bglu_ref[...])
        dpre = dya2 * ya1 * s * (1.0 - s)
        dpre_ref[...] = dpre.astype(BF16)
        dbglu_ref[...] += jnp.sum(dpre, axis=0, keepdims=True)
        dya1 = dya2 * s + _dot_nt(dpre, wglu_ref[...])
        dy0_ref[...] = dya1 * _gelu_grad(y0)
        ov = o_ref[...]
        zg = zg_ref[...]
        oh = ov * _head_rms(ov)
        on = oh * gain_ref[...]
        sz = _sigmoid(zg)
        dzg_ref[...] = (dyb * on * (sz * (1.0 + zg * (1.0 - sz)))).astype(BF16)
        don = dyb * (zg * sz)
        dgain_ref[...] += jnp.sum(don * oh, axis=0, keepdims=True)
        doh = don * gain_ref[...]
        do_ref[...] = _head_rms(ov) * (doh - oh * _head_mean(doh * oh))

    row = lambda w: pl.BlockSpec((tm, w), lambda i: (i, 0))
    return _pcall(body, "mix_bwd", (t // tm,),
                  [row(D_MODEL), row(S5_WIDTH), row(HG_WIDTH), pl.BlockSpec((tm, HG_WIDTH), lambda i: (i, 3)),
                   row(2 * D_MODEL), row(D_MODEL), row(D_MODEL), _full((S5_WIDTH, S5_WIDTH)), _full((1, S5_WIDTH)),
                   _full((1, HG_WIDTH)), _full((S5_WIDTH, D_MODEL)), _full((HG_WIDTH, D_MODEL)),
                   _full((D_MODEL, D_MODEL))],
                  [row(S5_WIDTH), row(HG_WIDTH), row(HG_WIDTH), row(2 * D_MODEL), row(D_MODEL), row(D_MODEL),
                   row(D_MODEL), row(S5_WIDTH), row(S5_WIDTH), _full((1, S5_WIDTH)), _full((1, HG_WIDTH))],
                  [_sds((t, S5_WIDTH)), _sds((t, HG_WIDTH)), _sds((t, HG_WIDTH), BF16), _sds((t, 2 * D_MODEL), BF16),
                   _sds((t, D_MODEL), BF16), _sds((t, D_MODEL), BF16), _sds((t, D_MODEL), BF16),
                   _sds((t, S5_WIDTH), BF16), _sds((t, S5_WIDTH), BF16), _sds((1, S5_WIDTH)), _sds((1, HG_WIDTH))],
                  )(dx1, y0, o, zh, zgt, pa, pb, w_glu, b_glu, gain, w_pa, w_pb, w_out)


def _s5_bwd(dy0, za, xs, c_bands, b_bands, lam, dskip, nb, seq, ts):
    t = za.shape[0]
    nts = seq // ts

    def body(dy0_ref, za_ref, xs_ref, halo_ref, cr_ref, ci_ref, br_ref, bi_ref, lam_ref, d_ref,
             dza_ref, a_ref, dlam_ref, dd_ref, acc_ref, st_ref):
        b, j = pl.program_id(0), pl.program_id(1)

        @pl.when((b == 0) & (j == 0))
        def _():
            dlam_ref[...] = jnp.zeros_like(dlam_ref)
            dd_ref[...] = jnp.zeros_like(dd_ref)

        @pl.when(j == 0)
        def _():
            st_ref[...] = jnp.zeros_like(st_ref)

        dy0 = dy0_ref[...]
        for q in range(S5_BANDS):
            ch, st = _band(q)
            acc_ref[:, st] = _dot(dy0[:, ch], cr_ref[q])
            acc_ref[:, _im(st)] = _dot(dy0[:, ch], ci_ref[q])
        _complex_scan(acc_ref, lam_ref, st_ref, ts, reverse=True)
        a_ref[...] = acc_ref[...].astype(BF16)
        shift = _shift_matrix(ts)
        first = jnp.where(j == nts - 1, 0.0, halo_ref[HALO - 1:HALO, :].astype(F32))
        top = lax.broadcasted_iota(jnp.int32, (SUBLANES, S5_LANES), 0) == 0

        def shifted(cols):
            xp = jnp.dot(shift, xs_ref[:, cols], preferred_element_type=F32)
            return jnp.concatenate([xp[:SUBLANES] + jnp.where(top, first[:, cols], 0.0), xp[SUBLANES:]], axis=0)

        for cc in range(S5_N // S5_LANES):
            re = slice(cc * S5_LANES, (cc + 1) * S5_LANES)
            ar, ai, xr, xi = acc_ref[:, re], acc_ref[:, _im(re)], shifted(re), shifted(_im(re))
            dlam_ref[0:1, re] += jnp.sum(ar * xr + ai * xi, axis=0, keepdims=True)
            dlam_ref[1:2, re] += jnp.sum(ai * xr - ar * xi, axis=0, keepdims=True)
        for q in range(S5_BANDS):
            ch, st = _band(q)
            dza_ref[:, ch] = (_dot(a_ref[:, st], br_ref[q]) + _dot(a_ref[:, _im(st)], bi_ref[q])
                              + d_ref[:, ch] * dy0[:, ch]).astype(BF16)
        dd_ref[...] += jnp.sum(dy0 * za_ref[...], axis=0, keepdims=True)

    tile = lambda b, j: b * nts + (nts - 1 - j)
    tok = lambda w: pl.BlockSpec((ts, w), lambda b, j: (tile(b, j), 0))
    halo = pl.BlockSpec((HALO, 2 * S5_N), lambda b, j: (jnp.maximum(tile(b, j) * (ts // HALO) - 1, 0), 0))
    to_st, to_ch = _full((S5_BANDS, BAND_CH, BAND_ST)), _full((S5_BANDS, BAND_ST, BAND_CH))
    return _pcall(body, "s5_bwd", (nb, nts),
                  [tok(S5_WIDTH), tok(S5_WIDTH), tok(2 * S5_N), halo, to_st, to_st, to_ch, to_ch,
                   _full((2, S5_N)), _full((1, S5_WIDTH))],
                  [tok(S5_WIDTH), tok(2 * S5_N), _full((2, S5_N)), _full((1, S5_WIDTH))],
                  [_sds((t, S5_WIDTH), BF16), _sds((t, 2 * S5_N), BF16), _sds((2, S5_N)), _sds((1, S5_WIDTH))],
                  scratch=[pltpu.VMEM((ts, 2 * S5_N), F32), pltpu.VMEM((2, S5_N), F32)],
                  )(dy0, za, xs, xs, *c_bands, *b_bands, lam, dskip)


def _hgrn_bwd(zh, do, sts, lb, nb, seq):
    nc = seq // CHUNK

    def body(zh_ref, do_ref, sts_ref, lb_ref, dz_ref, dlb_ref, dst_ref):
        @pl.when(pl.program_id(0) == 0)
        def _():
            dst_ref[...] = jnp.zeros_like(dst_ref)
            dlb_ref[...] = jnp.zeros_like(dlb_ref)

        row = lax.broadcasted_iota(jnp.int32, (CHUNK, CHUNK), 0)
        causal = row >= lax.broadcasted_iota(jnp.int32, (CHUNK, CHUNK), 1)
        last_row = lax.broadcasted_iota(jnp.int32, (CHUNK, HG_HEAD), 0) == CHUNK - 1
        for b in range(nb):
            for h in range(HG_HEADS):
                hs = slice(h * HG_HEAD, (h + 1) * HG_HEAD)
                zq = zh_ref[b, :, h * HG_HEAD:(h + 1) * HG_HEAD]
                zf = zh_ref[b, :, HG_WIDTH + h * HG_HEAD:HG_WIDTH + (h + 1) * HG_HEAD]
                zi = zh_ref[b, :, 2 * HG_WIDTH + h * HG_HEAD:2 * HG_WIDTH + (h + 1) * HG_HEAD]
                lbh = lb_ref[:, hs]
                sf, f, sq, qa, bc, bm, bl = _hgrn_gates(zq, zf, lbh)
                k = 1.0 - f
                e_qt = jnp.exp(bc - bm)
                e_kt = jnp.exp(bm - bc)
                e_b = jnp.exp(bc)
                e_kd = jnp.exp(bl - bc)
                e_l = jnp.exp(bl)
                qt, kt, qb, kd = qa * e_qt, k * e_kt, qa * e_b, k * e_kd
                a = jnp.where(causal, _dot_nt(qt, kt), 0.0)
                st = sts_ref[b, 0, h]
                dst = dst_ref[b, h]
                dov = do_ref[b, :, hs]
                da = jnp.where(causal, _dot_nt(dov, zi), 0.0)
                dqt = _hdot(da, kt)
                dkt = _hdot_tn(da, qt)
                dqb = _dot(dov, st)
                di = _dot_tn(a, dov) + _dot_nt(kd, dst)
                dkd = _dot(zi, dst)
                de_l = jnp.sum(dst * st, axis=0, keepdims=True)
                dst_ref[b, h] = dst * e_l + _dot_tn(dov, qb)
                dqa = dqt * e_qt + dqb * e_b
                dk = dkt * e_kt + dkd * e_kd
                dbl = jnp.sum(dkd * kd, axis=0, keepdims=True) + de_l * e_l
                db = dqt * qt - dkt * kt + dqb * qb - dkd * kd + jnp.where(last_row, dbl, 0.0)
                df = _cumsum_rows(db, reverse=True) / f - dk
                dzq = dqa * QSCALE * (sq * (1.0 + zq * (1.0 - sq)))
                dzf = df * (1.0 - lbh) * sf * (1.0 - sf)
                dz_ref[b, :, h * HG_HEAD:(h + 1) * HG_HEAD] = dzq.astype(BF16)
                dz_ref[b, :, HG_WIDTH + h * HG_HEAD:HG_WIDTH + (h + 1) * HG_HEAD] = dzf.astype(BF16)
                dz_ref[b, :, 2 * HG_WIDTH + h * HG_HEAD:2 * HG_WIDTH + (h + 1) * HG_HEAD] = di.astype(BF16)
                dlb_ref[:, hs] += jnp.sum(df * (1.0 - sf), axis=0, keepdims=True)

    rev = lambda c: nc - 1 - c
    return _pcall(body, "hgrn_bwd", (nc,),
                  [pl.BlockSpec((nb, CHUNK, 4 * HG_WIDTH), lambda c: (0, rev(c), 0)),
                   pl.BlockSpec((nb, CHUNK, HG_WIDTH), lambda c: (0, rev(c), 0)),
                   pl.BlockSpec((nb, 1, HG_HEADS, HG_HEAD, HG_HEAD), lambda c: (0, rev(c), 0, 0, 0)),
                   _full((1, HG_WIDTH))],
                  [pl.BlockSpec((nb, CHUNK, 3 * HG_WIDTH), lambda c: (0, rev(c), 0)), _full((1, HG_WIDTH))],
                  [_sds((nb, seq, 3 * HG_WIDTH), BF16), _sds((1, HG_WIDTH))],
                  scratch=[pltpu.VMEM((nb, HG_HEADS, HG_HEAD, HG_HEAD), F32)])(zh, do, sts, lb)


def _in_proj_bwd(dza, dzh, dzg, dzgt, dx1, x, g_mix, w_in, tm):
    t = x.shape[0]

    def body(dza_ref, dzh_ref, dzg_ref, dzgt_ref, dx1_ref, x_ref, g_ref, w_ref, dz_ref, dx_ref, dg_ref):
        @pl.when(pl.program_id(0) == 0)
        def _():
            dg_ref[...] = jnp.zeros_like(dg_ref)

        c1, c2, c3 = S5_WIDTH, S5_WIDTH + 3 * HG_WIDTH, S5_WIDTH + 4 * HG_WIDTH
        dz_ref[:, 0:c1] = dza_ref[...]
        dz_ref[:, c1:c2] = dzh_ref[...]
        dz_ref[:, c2:c3] = dzg_ref[...]
        dz_ref[:, c3:] = dzgt_ref[...]
        du = _dot(dz_ref[...], w_ref[...])
        xv = x_ref[...]
        r = lax.rsqrt(jnp.mean(xv * xv, axis=-1, keepdims=True) + EPS)
        xn = xv * r
        dg_ref[...] += jnp.sum(du * xn, axis=0, keepdims=True)
        dxn = du * g_ref[...]
        dx_ref[...] = dx1_ref[...] + r * (dxn - xn * jnp.mean(dxn * xn, axis=-1, keepdims=True))

    row = lambda w: pl.BlockSpec((tm, w), lambda i: (i, 0))
    return _pcall(body, "in_proj_bwd", (t // tm,),
                  [row(S5_WIDTH), row(3 * HG_WIDTH), row(HG_WIDTH), row(2 * D_MODEL), row(D_MODEL), row(D_MODEL),
                   _full((1, D_MODEL)), _full((N_IN, D_MODEL))],
                  [row(N_IN), row(D_MODEL), _full((1, D_MODEL))],
                  [_sds((t, N_IN), BF16), _sds((t, D_MODEL)), _sds((1, D_MODEL))],
                  )(dza, dzh, dzg, dzgt, dx1, x, g_mix, w_in)


def _tie(*arrays):
    return jnp.zeros((SUBLANES, 128), F32) + sum(a.reshape(-1)[0].astype(F32) for a in arrays)


def _after(value, token):
    return value + token[0, 0]


def _local_step(x3, tgt3, weights, sp, emit, emit_small):
    nb, seq, _ = x3.shape
    t = nb * seq
    tm = _token_tile(seq)
    x = x3.reshape(t, D_MODEL)
    tgt = tgt3.reshape(t, D_MODEL)
    row = lambda v: v.reshape(1, -1)

    a_re, a_im, b_re, b_im = sp["s5_a_re"], sp["s5_a_im"], sp["s5_b_re"], sp["s5_b_im"]
    ldt = sp["s5_log_dt"].reshape(S5_GROUPS, 1)
    lr, li, bb_re, bb_im, lb = _params_fwd(a_re, a_im, ldt, b_re, b_im, sp["hg_lb_logits"])
    lam = jnp.concatenate([lr.reshape(1, S5_N), li.reshape(1, S5_N)], axis=0)
    swap = lambda m: m.transpose(0, 2, 1)
    b_to_st = (_band_blocks(bb_re), _band_blocks(bb_im))
    b_to_ch = (_band_blocks(swap(bb_re)), _band_blocks(swap(bb_im)))
    c_to_ch = (_band_blocks(swap(sp["s5_c_re"])), _band_blocks(swap(-sp["s5_c_im"])))
    c_to_st = (_band_blocks(sp["s5_c_re"]), _band_blocks(-sp["s5_c_im"]))

    g_mix, g_ffn, g_final = row(sp["g_mix"]), row(sp["g_ffn"]), row(sp["g_final"])
    b_glu, gain, dskip, b_conv = row(sp["b_glu"]), row(sp["hg_norm_gain"]), row(sp["s5_d"]), row(sp["b_conv"])

    w_in = weights("in", lam, *b_to_st, *b_to_ch, *c_to_ch, *c_to_st)["w_in"]
    u, za, zh, zgt = _in_proj(x, g_mix, w_in, tm)
    xs, y0 = _s5_fwd(za, b_to_st, lam, c_to_ch, dskip, nb, seq, tm)
    o3, sts = _hgrn_fwd(zh.reshape(nb, seq, 4 * HG_WIDTH), lb, nb, seq)
    o = o3.reshape(t, HG_WIDTH)
    wm = weights("mix", y0, o3)
    x1, u2, pa, pb, ya2, yb = _mix_fwd(x, y0, o, zh, zgt, wm["w_glu"], b_glu, gain, wm["w_pa"], wm["w_pb"],
                                       wm["w_out"], g_ffn, tm)
    wf = weights("ffn", u2)
    h = _ffn_up(u2, wf["w_up"], min(4 * tm, t))
    hc, a, dx2, dx2b, loss, dg_final = _ffn_down_loss(h, x1, tgt, wf["w_conv"], b_conv, wf["w_down"], g_final,
                                                      seq, tm)

    wgrad = functools.partial(_wgrad, tn=256, out_dtype=BF16)
    dhc, db_conv = _ffn_bwd_act(dx2b, hc, wf["w_down"], tm)
    sent = emit({"w_down": wgrad(a, dx2b, "dw_down")})
    dh, dx1, dx1b, dg_ffn, dw_conv = _ffn_bwd_up(dhc, h, dx2, x1, wf["w_conv"], wf["w_up"], _after(g_ffn, sent),
                                                 seq, tm)
    sent = emit({"w_up": wgrad(dh, u2, "dw_up"), "w_conv": dw_conv})
    (dy0, do, dzg, dzgt, m, dpa, dpb, ya1, dpre, db_glu, dgain) = _mix_bwd(
        dx1b, y0, o, zh, zgt, pa, pb, wm["w_glu"], _after(b_glu, sent), gain, wm["w_pa"], wm["w_pb"], wm["w_out"], tm)
    sent = emit({"w_out": wgrad(m, dx1b, "dw_out"), "w_pa": wgrad(ya2, dpa, "dw_pa"),
                 "w_pb": wgrad(yb, dpb, "dw_pb"), "w_glu": wgrad(ya1, dpre, "dw_glu")})
    dzh3, dlb = _hgrn_bwd(zh.reshape(nb, seq, 4 * HG_WIDTH), do.reshape(nb, seq, HG_WIDTH), sts, _after(lb, sent),
                          nb, seq)
    dza, a_s5, dlam, dd = _s5_bwd(dy0, za, xs, c_to_st, b_to_ch, lam, dskip, nb, seq, tm)
    dz, dx, dg_mix = _in_proj_bwd(dza, dzh3.reshape(t, 3 * HG_WIDTH), dzg, dzgt, dx1, x, g_mix, w_in, tm)
    sent = emit({"w_in": wgrad(dz, u, "dw_in")})

    band = HG_HEAD
    dbb_band = _wgrad(a_s5, za, "dbb_s5", 512, band=band, after=sent)
    dc_band = _wgrad(xs, dy0, "dc_s5", 512, band=band, after=sent)
    dbb_re = swap(_diag_blocks(dbb_band[:S5_N], S5_STATE, S5_GROUP))
    dbb_im = swap(_diag_blocks(dbb_band[S5_N:], S5_STATE, S5_GROUP))
    dc_re = swap(_diag_blocks(dc_band[:S5_N], S5_STATE, S5_GROUP))
    dc_im = -swap(_diag_blocks(dc_band[S5_N:], S5_STATE, S5_GROUP))
    da_re, da_im, dldt, db_re, db_im, dlogits = _params_bwd(
        a_re, a_im, ldt, b_re, b_im, sp["hg_lb_logits"],
        dlam[0].reshape(S5_GROUPS, S5_STATE), dlam[1].reshape(S5_GROUPS, S5_STATE), dbb_re, dbb_im, dlb)
    emit_small({"g_mix": dg_mix, "s5_a_re": da_re, "s5_a_im": da_im, "s5_log_dt": dldt.reshape(1, S5_GROUPS),
                "s5_b_re": db_re, "s5_b_im": db_im, "s5_c_re": dc_re, "s5_c_im": dc_im, "s5_d": dd, "b_glu": db_glu,
                "hg_lb_logits": dlogits, "hg_norm_gain": dgain, "g_ffn": dg_ffn, "b_conv": db_conv,
                "g_final": dg_final, "loss": loss})
    return dx.reshape(nb, seq, D_MODEL)


def _mesh_peers():
    x, y, c = lax.axis_index("x"), lax.axis_index("y"), lax.axis_index("c")
    peers = []
    for k in range(1, N_DEV):
        px, py, pc = (1 - x if k & 4 else x), (1 - y if k & 2 else y), (1 - c if k & 1 else c)
        peers.append((k, (px, py, pc), 4 * px + 2 * py + pc))
    return 4 * x + 2 * y + c, peers


_HBM = pl.BlockSpec(memory_space=pltpu.HBM)
_SEM = pl.BlockSpec(memory_space=pltpu.SEMAPHORE)


def _exchange_start(name, operands, after):
    n = len(operands)
    me = 4 * lax.axis_index("x") + 2 * lax.axis_index("y") + lax.axis_index("c")
    flags = [per_peer for _, per_peer in operands]
    srcs, lands = [], []
    for arr, per_peer in operands:
        own = lax.dynamic_index_in_dim(arr, me, 0, keepdims=True) if per_peer else arr[None]
        land = lax.dynamic_update_slice_in_dim(lax.empty((N_DEV,) + own.shape[1:], arr.dtype), own, me, 0)
        srcs.append(pltpu.with_memory_space_constraint(arr, pltpu.HBM))
        lands.append(pltpu.with_memory_space_constraint(land, pltpu.HBM))
    copies = (N_DEV - 1) * n

    def body(*refs):
        src_refs, land_refs = refs[:n], refs[n:2 * n]
        send_sems, recv_sems = refs[2 * n + 1], refs[2 * n + 2]
        token = refs[-1]
        my_slab, peers = _mesh_peers()
        for k, peer, slab in peers:
            for i in range(n):
                s = (k - 1) * n + i
                pltpu.make_async_remote_copy(
                    src_ref=src_refs[i].at[slab] if flags[i] else src_refs[i], dst_ref=land_refs[i].at[my_slab],
                    send_sem=send_sems.at[s], recv_sem=recv_sems.at[s], device_id=peer,
                    device_id_type=pl.DeviceIdType.MESH).start()
        token[...] = jnp.zeros_like(token)

    outs = pl.pallas_call(
        body, name=name,
        out_shape=(pltpu.SemaphoreType.DMA((copies,)), pltpu.SemaphoreType.DMA((copies,)),
                   *[pltpu.HBM(a.shape, a.dtype) for a in srcs], *[pltpu.HBM(a.shape, a.dtype) for a in lands],
                   _sds((SUBLANES, 128))),
        in_specs=[_HBM] * (2 * n) + [pl.BlockSpec(memory_space=pl.ANY)],
        out_specs=(_SEM, _SEM, *[_HBM] * (2 * n), pl.BlockSpec(memory_space=pltpu.VMEM)),
        input_output_aliases={i: 2 + i for i in range(2 * n)},
        compiler_params=pltpu.CompilerParams(has_side_effects=pltpu.SideEffectType.DATAFLOW_SIDE_EFFECTING),
    )(*srcs, *lands, after)
    state = (flags, outs[0], outs[1], outs[2:2 + n], outs[2 + n:2 + 2 * n])
    return state, outs[-1]


def _exchange_wait(name, state, *after):
    flags, send_sems, recv_sems, srcs, lands = state
    n = len(flags)

    def body(*refs):
        src_refs, land_refs = refs[:n], refs[n:2 * n]
        send_ref, recv_ref = refs[2 * n], refs[2 * n + 1]
        _, peers = _mesh_peers()
        for k, peer, slab in peers:
            for i in range(n):
                s = (k - 1) * n + i
                copy = pltpu.make_async_remote_copy(
                    src_ref=src_refs[i].at[slab] if flags[i] else src_refs[i], dst_ref=land_refs[i].at[slab],
                    send_sem=send_ref.at[s], recv_sem=recv_ref.at[s], device_id=peer,
                    device_id_type=pl.DeviceIdType.MESH)
                copy.wait_send()
                copy.wait_recv()

    outs = pl.pallas_call(
        body, name=name,
        out_shape=(*[pltpu.HBM(a.shape, a.dtype) for a in srcs], *[pltpu.HBM(a.shape, a.dtype) for a in lands]),
        in_specs=[_HBM] * (2 * n) + [_SEM, _SEM] + [pl.BlockSpec(memory_space=pl.ANY)] * len(after),
        out_specs=tuple([_HBM] * (2 * n)),
        input_output_aliases={i: i for i in range(2 * n)},
        compiler_params=pltpu.CompilerParams(has_side_effects=pltpu.SideEffectType.DATAFLOW_SIDE_EFFECTING),
    )(*srcs, *lands, send_sems, recv_sems, *after)
    return list(outs[n:])


def _join_cols(parts, name, tr):
    _, r, c = parts.shape

    def body(p_ref, o_ref):
        for j in range(N_DEV):
            o_ref[:, j * c:(j + 1) * c] = p_ref[j]

    return _pcall(body, name, (r // tr,), [pl.BlockSpec((N_DEV, tr, c), lambda i: (0, i, 0))],
                  pl.BlockSpec((tr, N_DEV * c), lambda i: (i, 0)), _sds((r, N_DEV * c), parts.dtype))(parts)


def _split_cols(full, name, tr):
    r, c = full.shape[0], full.shape[1] // N_DEV

    def body(f_ref, o_ref):
        for j in range(N_DEV):
            o_ref[j] = f_ref[:, j * c:(j + 1) * c]

    return _pcall(body, name, (r // tr,), [pl.BlockSpec((tr, N_DEV * c), lambda i: (i, 0))],
                  pl.BlockSpec((N_DEV, tr, c), lambda i: (0, i, 0)), _sds((N_DEV, r, c), full.dtype))(full)


def _adamw(parts, w, m, v, name, tile):
    _, rows, cols = w.shape

    def body(p_ref, w_ref, m_ref, v_ref, g_out, d_out, m_out, v_out):
        g = p_ref[0].astype(F32)
        for k in range(1, N_DEV):
            g = g + p_ref[k].astype(F32)
        m1 = ADAM_B1 * m_ref[0] + (1.0 - ADAM_B1) * g
        v1 = ADAM_B2 * v_ref[0] + (1.0 - ADAM_B2) * (g * g)
        m_hat = m1 / (1.0 - ADAM_B1 ** ADAM_STEP)
        v_hat = v1 / (1.0 - ADAM_B2 ** ADAM_STEP)
        g_out[0] = g
        d_out[0] = -ADAM_LR * (m_hat / (jnp.sqrt(v_hat) + ADAM_EPS) + ADAM_WD * w_ref[0])
        m_out[0] = m1
        v_out[0] = v1

    row = pl.BlockSpec((1, tile, cols), lambda i: (0, i, 0))
    return _pcall(body, name, (rows // tile,),
                  [pl.BlockSpec((N_DEV, tile, cols), lambda i: (0, i, 0)), row, row, row],
                  [row, row, row, row], [_sds((1, rows, cols))] * 4)(parts, w, m, v)


BIG = {
    "w_in": ((N_IN // N_DEV, D_MODEL), False, N_IN // N_DEV // 3),
    "w_glu": ((S5_WIDTH // N_DEV, S5_WIDTH), False, S5_WIDTH // N_DEV),
    "w_pa": ((S5_WIDTH, D_MODEL // N_DEV), True, S5_WIDTH),
    "w_pb": ((HG_WIDTH, D_MODEL // N_DEV), True, HG_WIDTH),
    "w_out": ((D_MODEL // N_DEV, D_MODEL), False, D_MODEL // N_DEV),
    "w_up": ((2 * D_FF // N_DEV, D_MODEL), False, 2 * D_FF // N_DEV // 4),
    "w_conv": ((CONV_W, 2 * D_FF // N_DEV), True, CONV_W),
    "w_down": ((D_FF // N_DEV, D_MODEL), False, D_FF // N_DEV // 2),
}
TRANSPOSED = ("w_in", "w_up", "s5_b_re", "s5_b_im")
UNALIGNED_COLS = ("w_conv",)


def _stored(n, arr):
    return jnp.swapaxes(arr, -1, -2) if n in TRANSPOSED else arr


def _join_shards(n, parts):
    (a, b), by_cols, _ = BIG[n]
    if not by_cols:
        return parts.reshape(N_DEV * a, b)
    if n in UNALIGNED_COLS:
        return _join_cols(parts, "join_" + n, min(a, 256))
    return parts.transpose(1, 0, 2).reshape(a, N_DEV * b)


def _split_shards(n, full):
    (a, b), by_cols, _ = BIG[n]
    if not by_cols:
        return full.reshape(N_DEV, a, b)
    if n in UNALIGNED_COLS:
        return _split_cols(full, "split_" + n, min(a, 256))
    return full.reshape(a, N_DEV, b).transpose(1, 0, 2)


SMALL_CORE = {
    "s5_b_re": GSC, "s5_b_im": GSC, "s5_c_re": GSC, "s5_c_im": GSC,
    "g_mix": (1, D_MODEL), "g_ffn": (1, D_MODEL), "g_final": (1, D_MODEL), "s5_d": (1, S5_WIDTH),
    "b_glu": (1, S5_WIDTH), "hg_norm_gain": (1, HG_WIDTH), "hg_lb_logits": (2, HG_WIDTH), "b_conv": (1, 2 * D_FF),
    "s5_log_dt": (1, S5_GROUPS), "s5_a_re": (S5_GROUPS, S5_STATE), "s5_a_im": (S5_GROUPS, S5_STATE), "loss": (1, 1),
}
BLOCK_ROWS = 32


def _small_rows():
    rows, r = {}, 0
    for n, core in SMALL_CORE.items():
        rows[n] = r
        r += BLOCK_ROWS if len(core) == 3 else -(-math.prod(core) // PACK_W)
    return rows, -(-r // SUBLANES) * SUBLANES


SMALL_ROW, SMALL_ROWS = _small_rows()


def _small_pieces(name):
    r, core = SMALL_ROW[name], SMALL_CORE[name]
    if len(core) == 3:
        return [((g, slice(None), slice(None)), slice(r + S5_GROUP * (g % 2), r + S5_GROUP * (g % 2 + 1)),
                 slice(S5_STATE * (g // 2), S5_STATE * (g // 2 + 1))) for g in range(S5_GROUPS)]
    pieces = []
    for i in range(core[0]):
        for c0 in range(0, core[1], PACK_W):
            w, flat = min(PACK_W, core[1] - c0), i * core[1] + c0
            pieces.append(((slice(i, i + 1), slice(c0, c0 + w)), slice(r + flat // PACK_W, r + flat // PACK_W + 1),
                           slice(flat % PACK_W, flat % PACK_W + w)))
    return pieces


def _core_index(ref, name, idx):
    return (0,) * (len(ref.shape) - len(SMALL_CORE[name])) + idx


def _pack_small_grads(grads):
    names = list(SMALL_CORE)

    def body(*refs):
        pack = refs[-1]
        pack[...] = jnp.zeros_like(pack)
        for ref, n in zip(refs, names):
            for idx, rows, lanes in _small_pieces(n):
                pack[rows, lanes] = ref[_core_index(ref, n, idx)]

    return _pcall(body, "pack_small_grads", (1,), [_full(grads[n].shape) for n in names],
                  _full((SMALL_ROWS, PACK_W)), _sds((SMALL_ROWS, PACK_W)))(*[grads[n] for n in names])


def _adamw_small(parts, names, rows, given, name):
    lo, hi = rows
    k = len(names)
    shapes = [given[n].shape for n in names]

    def body(*refs):
        p_ref, ins, outs = refs[0], refs[1:1 + 3 * k], refs[1 + 3 * k:1 + 7 * k]
        packs, results = refs[1 + 7 * k:4 + 7 * k], refs[4 + 7 * k:]
        for j, pack in enumerate(packs):
            pack[...] = jnp.zeros_like(pack)
            for ref, n in zip(ins[j * k:(j + 1) * k], names):
                for idx, prow, lanes in _small_pieces(n):
                    pack[slice(prow.start - lo, prow.stop - lo), lanes] = ref[_core_index(ref, n, idx)]
        g = p_ref[0, lo:hi, :]
        for d in range(1, N_DEV):
            g = g + p_ref[d, lo:hi, :]
        m1 = ADAM_B1 * packs[1][...] + (1.0 - ADAM_B1) * g
        v1 = ADAM_B2 * packs[2][...] + (1.0 - ADAM_B2) * (g * g)
        m_hat = m1 / (1.0 - ADAM_B1 ** ADAM_STEP)
        v_hat = v1 / (1.0 - ADAM_B2 ** ADAM_STEP)
        results[0][...] = g
        results[1][...] = -ADAM_LR * (m_hat / (jnp.sqrt(v_hat) + ADAM_EPS) + ADAM_WD * packs[0][...])
        results[2][...] = m1
        results[3][...] = v1
        for j, result in enumerate(results):
            for ref, n in zip(outs[j * k:(j + 1) * k], names):
                for idx, prow, lanes in _small_pieces(n):
                    ref[_core_index(ref, n, idx)] = result[slice(prow.start - lo, prow.stop - lo), lanes]

    flat = _pcall(body, name, (1,),
                  [_full(parts.shape)] + [_full(s) for s in shapes] * 3,
                  [_full(s) for s in shapes] * 4, [_sds(s) for s in shapes] * 4,
                  scratch=[pltpu.VMEM((hi - lo, PACK_W), F32)] * 7,
                  )(parts, *[given[pre + n] for pre in ("", "m_", "v_") for n in names])
    return {n: [flat[j * k + i] for j in range(4)] for i, n in enumerate(names)}


def kernel(x, g_mix, w_in, s5_a_re, s5_a_im, s5_log_dt, s5_b_re, s5_b_im, s5_c_re, s5_c_im, s5_d, w_glu, b_glu, hg_lb_logits, hg_norm_gain, w_pa, w_pb, w_out, g_ffn, w_up, w_conv, b_conv, w_down, g_final, loss_target, m_g_mix, m_w_in, m_s5_a_re, m_s5_a_im, m_s5_log_dt, m_s5_b_re, m_s5_b_im, m_s5_c_re, m_s5_c_im, m_s5_d, m_w_glu, m_b_glu, m_hg_lb_logits, m_hg_norm_gain, m_w_pa, m_w_pb, m_w_out, m_g_ffn, m_w_up, m_w_conv, m_b_conv, m_w_down, m_g_final, v_g_mix, v_w_in, v_s5_a_re, v_s5_a_im, v_s5_log_dt, v_s5_b_re, v_s5_b_im, v_s5_c_re, v_s5_c_im, v_s5_d, v_w_glu, v_b_glu, v_hg_lb_logits, v_hg_norm_gain, v_w_pa, v_w_pb, v_w_out, v_g_ffn, v_w_up, v_w_conv, v_b_conv, v_w_down, v_g_final):
    given = dict(locals())
    small_names = [n for n, _ in SMALL]

    pay = {n: given[n][0] if n == "w_conv" else _stored(n, given[n])[0].astype(BF16) for n in BIG}
    groups = {"in": ["w_in"], "mix": ["w_glu", "w_pa", "w_pb", "w_out"], "ffn": ["w_up", "w_down", "w_conv"]}
    gathers, order = {}, pay["w_in"]
    for grp, names in groups.items():
        gathers[grp], order = _exchange_start("gather_" + grp + "_start", [(pay[n], False) for n in names], order)

    def weights(grp, *after):
        if grp == "in":
            after = (*after, order)
        got = _exchange_wait("gather_" + grp + "_wait", gathers[grp], *after)
        return {n: _join_shards(n, g) for n, g in zip(groups[grp], got)}

    in_flight, started = [], []

    def emit(grads):
        names = list(grads)
        state, token = _exchange_start("grads_" + names[0] + "_start",
                                       [(_split_shards(n, grads[n]), True) for n in names], grads[names[0]])
        in_flight.append((names, state))
        return token

    def emit_small(grads):
        pack = _pack_small_grads(grads)
        state, token = _exchange_start("grads_small_start", [(pack, False)], pack)
        in_flight.append((["small"], state))
        started.append(token)

    sp = {n: (given[n] if n in ("g_final", "hg_lb_logits") else _stored(n, given[n])[0]) for n in small_names}
    sp["g_mix"] = _after(sp["g_mix"], order)
    dx = _local_step(x, loss_target, weights, sp, emit, emit_small)

    res = {}
    after = [started[-1]]
    in_flight.insert(-1, in_flight.pop())
    for names, state in in_flight:
        parts = _exchange_wait("grads_" + names[0] + "_wait", state, *after)
        if names != ["small"]:
            after = []
            for n, part in zip(names, parts):
                raw = _adamw(part, *[_stored(n, given[pre + n]) for pre in ("", "m_", "v_")], "adamw_" + n, BIG[n][2])
                res[n] = [_stored(n, r) for r in raw]
                after.append(raw[0])
            continue
        sgiven = {pre + n: _stored(n, given[pre + n]) for pre in ("", "m_", "v_") for n in small_names}
        for pre in ("", "m_", "v_"):
            sgiven[pre + "g_final"] = given[pre + "g_final"].reshape(1, D_MODEL)
            sgiven[pre + "loss"] = jnp.zeros((1, 1), F32)
        raw = _adamw_small(parts[0], list(SMALL_CORE), (0, SMALL_ROWS), sgiven, "adamw_small")
        res.update({n: [_stored(n, r) for r in raw[n]] for n in small_names})
        res["g_final"] = [r.reshape(D_MODEL) for r in raw["g_final"]]
        total_loss = raw["loss"][0].reshape(())
        after = [raw["s5_b_re"][0], raw["g_mix"][0]]
    return (total_loss, dx, *[res[n][0] for n in WEIGHT_ORDER], *[res[n][1] for n in WEIGHT_ORDER],
            *[res[n][2] for n in WEIGHT_ORDER], *[res[n][3] for n in WEIGHT_ORDER])
```

```python
import functools
import math

import jax
import jax.numpy as jnp
from jax import lax
from jax.experimental import pallas as pl
from jax.experimental.pallas import tpu as pltpu

F32 = jnp.float32
BF16 = jnp.bfloat16

D_MODEL = 1024
S5_WIDTH = 512
S5_GROUP = 16
S5_GROUPS = 32
S5_STATE = 64
S5_N = S5_GROUPS * S5_STATE
HG_WIDTH = 512
HG_HEAD = 128
HG_HEADS = 4
D_FF = 2816
CONV_W = 3
CHUNK = 64
N_IN = S5_WIDTH + 4 * HG_WIDTH + 2 * D_MODEL
EPS = 1e-6
QSCALE = HG_HEAD ** -0.5

ADAM_LR = 0.001
ADAM_B1 = 0.9
ADAM_B2 = 0.999
ADAM_EPS = 1e-08
ADAM_WD = 0.01
ADAM_STEP = 10

N_DEV = 8
V7X_VMEM_BYTES = 64 * 1024 * 1024
VMEM_LIMIT = V7X_VMEM_BYTES * 7 // 8
SUBLANES = 8
PACK_W = 1024

SMALL = (
    ("g_mix", (1, D_MODEL)),
    ("s5_a_re", (1, S5_GROUPS, S5_STATE)),
    ("s5_a_im", (1, S5_GROUPS, S5_STATE)),
    ("s5_log_dt", (1, S5_GROUPS)),
    ("s5_b_re", (1, S5_GROUPS, S5_STATE, S5_GROUP)),
    ("s5_b_im", (1, S5_GROUPS, S5_STATE, S5_GROUP)),
    ("s5_c_re", (1, S5_GROUPS, S5_GROUP, S5_STATE)),
    ("s5_c_im", (1, S5_GROUPS, S5_GROUP, S5_STATE)),
    ("s5_d", (1, S5_WIDTH)),
    ("b_glu", (1, S5_WIDTH)),
    ("hg_lb_logits", (2, HG_WIDTH)),
    ("hg_norm_gain", (1, HG_WIDTH)),
    ("g_ffn", (1, D_MODEL)),
    ("b_conv", (1, 2 * D_FF)),
    ("g_final", (D_MODEL,)),
)
WEIGHT_ORDER = ("g_mix", "w_in", "s5_a_re", "s5_a_im", "s5_log_dt", "s5_b_re", "s5_b_im", "s5_c_re", "s5_c_im",
                "s5_d", "w_glu", "b_glu", "hg_lb_logits", "hg_norm_gain", "w_pa", "w_pb", "w_out", "g_ffn",
                "w_up", "w_conv", "b_conv", "w_down", "g_final")


def _pcall(body, name, grid, in_specs, out_specs, out_shape, scratch=()):
    return pl.pallas_call(
        body, name=name, grid=grid, in_specs=in_specs, out_specs=out_specs, out_shape=out_shape,
        scratch_shapes=list(scratch),
        compiler_params=pltpu.CompilerParams(dimension_semantics=("arbitrary",) * len(grid),
                                             vmem_limit_bytes=VMEM_LIMIT),
    )


def _full(shape):
    return pl.BlockSpec(shape, lambda *_: (0,) * len(shape))


def _sds(shape, dtype=F32):
    return jax.ShapeDtypeStruct(shape, dtype)


def _dot(a, b):
    return jnp.dot(a.astype(BF16), b.astype(BF16), preferred_element_type=F32)


def _dot_nt(a, b):
    return lax.dot_general(a.astype(BF16), b.astype(BF16), (((1,), (1,)), ((), ())), preferred_element_type=F32)


def _dot_tn(a, b):
    return lax.dot_general(a.astype(BF16), b.astype(BF16), (((0,), (0,)), ((), ())), preferred_element_type=F32)


def _hdot(a, b):
    return jnp.dot(a, b, preferred_element_type=F32, precision=lax.Precision.HIGHEST)


def _hdot_tn(a, b):
    return lax.dot_general(a, b, (((0,), (0,)), ((), ())), preferred_element_type=F32,
                           precision=lax.Precision.HIGHEST)


def _sigmoid(x):
    return jax.nn.sigmoid(x)


GELU_C = math.sqrt(2.0 / math.pi)
GELU_A = 0.044715


def _gelu(x):
    return 0.5 * x * (1.0 + jnp.tanh(GELU_C * (x + GELU_A * (x * x * x))))


def _gelu_grad(x):
    t = jnp.tanh(GELU_C * (x + GELU_A * (x * x * x)))
    return 0.5 * (1.0 + t) + 0.5 * x * (1.0 - t * t) * (GELU_C * (1.0 + 3.0 * GELU_A * x * x))


def _cumsum_rows(v, reverse=False):
    n = v.shape[0]
    row = lax.broadcasted_iota(jnp.int32, v.shape, 0)
    s = 1
    while s < n:
        if reverse:
            v = v + jnp.where(row < n - s, pltpu.roll(v, n - s, axis=0), 0.0)
        else:
            v = v + jnp.where(row >= s, pltpu.roll(v, s, axis=0), 0.0)
        s *= 2
    return v


def _token_tile(seq):
    return min(256, seq)


def _s5_coeffs(a_re, a_im, ldt):
    dt = jnp.exp(ldt)
    mag = jnp.exp(a_re * dt)
    ang = a_im * dt
    lb_re = mag * jnp.cos(ang)
    lb_im = mag * jnp.sin(ang)
    den = a_re * a_re + a_im * a_im
    n_re = lb_re - 1.0
    n_im = lb_im
    co_re = (n_re * a_re + n_im * a_im) / den
    co_im = (n_im * a_re - n_re * a_im) / den
    return lb_re, lb_im, co_re, co_im


GS, GSC = (S5_GROUPS, S5_STATE), (S5_GROUPS, S5_GROUP, S5_STATE)


def _params_fwd(a_re, a_im, ldt, bt_re, bt_im, logits):
    def body(are, aim, ld, bre, bim, lg, lr_o, li_o, bbr_o, bbi_o, lb_o):
        lr, li, co_re, co_im = _s5_coeffs(are[...], aim[...], ld[...])
        lr_o[...] = lr
        li_o[...] = li
        for g in range(S5_GROUPS):
            cr, ci = co_re[g:g + 1, :], co_im[g:g + 1, :]
            bbr_o[g] = cr * bre[g] - ci * bim[g]
            bbi_o[g] = cr * bim[g] + ci * bre[g]
        lb_o[...] = _sigmoid(lg[0:1, :] - lg[1:2, :])

    return _pcall(body, "params_fwd", (1,),
                  [_full(GS), _full(GS), _full((S5_GROUPS, 1)), _full(GSC), _full(GSC), _full((2, HG_WIDTH))],
                  [_full(GS), _full(GS), _full(GSC), _full(GSC), _full((1, HG_WIDTH))],
                  [_sds(GS), _sds(GS), _sds(GSC), _sds(GSC), _sds((1, HG_WIDTH))],
                  )(a_re, a_im, ldt, bt_re, bt_im, logits)


def _params_bwd(a_re, a_im, ldt, bt_re, bt_im, logits, dlr, dli, dbbr, dbbi, dlb):
    def body(are, aim, ld, bre, bim, lg, dlr_r, dli_r, dbbr_r, dbbi_r, dlb_r,
             dare_o, daim_o, dld_o, dbre_o, dbim_o, dlg_o, dcr_ref, dci_ref):
        (_, _, co_re, co_im), vjp = jax.vjp(_s5_coeffs, are[...], aim[...], ld[...])
        for g in range(S5_GROUPS):
            cr, ci = co_re[g:g + 1, :], co_im[g:g + 1, :]
            gr, gi, br, bi = dbbr_r[g], dbbi_r[g], bre[g], bim[g]
            dbre_o[g] = cr * gr + ci * gi
            dbim_o[g] = cr * gi - ci * gr
            dcr_ref[g:g + 1, :] = jnp.sum(gr * br + gi * bi, axis=0, keepdims=True)
            dci_ref[g:g + 1, :] = jnp.sum(gi * br - gr * bi, axis=0, keepdims=True)
        dare, daim, dld = vjp((dlr_r[...], dli_r[...], dcr_ref[...], dci_ref[...]))
        dare_o[...] = dare
        daim_o[...] = daim
        dld_o[...] = dld
        lb = _sigmoid(lg[0:1, :] - lg[1:2, :])
        d0 = dlb_r[...] * lb * (1.0 - lb)
        dlg_o[0:1, :] = d0
        dlg_o[1:2, :] = -d0

    return _pcall(body, "params_bwd", (1,),
                  [_full(GS), _full(GS), _full((S5_GROUPS, 1)), _full(GSC), _full(GSC), _full((2, HG_WIDTH)),
                   _full(GS), _full(GS), _full(GSC), _full(GSC), _full((1, HG_WIDTH))],
                  [_full(GS), _full(GS), _full((S5_GROUPS, 1)), _full(GSC), _full(GSC), _full((2, HG_WIDTH))],
                  [_sds(GS), _sds(GS), _sds((S5_GROUPS, 1)), _sds(GSC), _sds(GSC), _sds((2, HG_WIDTH))],
                  scratch=[pltpu.VMEM(GS, F32), pltpu.VMEM(GS, F32)],
                  )(a_re, a_im, ldt, bt_re, bt_im, logits, dlr, dli, dbbr, dbbi, dlb)


def _band_blocks(m):
    g, r, c = m.shape
    gb = g // S5_BANDS
    m4 = m.astype(BF16).reshape(S5_BANDS, gb, r, c)
    on_diag = jnp.eye(gb, dtype=bool)[None, :, None, :, None]
    return jnp.where(on_diag, m4[:, :, :, None, :], 0).reshape(S5_BANDS, gb * r, gb * c)


def _diag_blocks(band, r, c):
    g, nb = band.shape[0] // r, band.shape[1] // c
    on_diag = (jnp.arange(g) % nb)[:, None, None, None] == jnp.arange(nb)[None, None, :, None]
    return jnp.sum(jnp.where(on_diag, band.reshape(g, r, nb, c), 0.0), axis=2)


def _in_proj(x, g_mix, w_in, tm):
    t = x.shape[0]

    def body(x_ref, g_ref, w_ref, u_ref, za_ref, zh_ref, zg_ref):
        xv = x_ref[...]
        r = lax.rsqrt(jnp.mean(xv * xv, axis=-1, keepdims=True) + EPS)
        u = (xv * r * g_ref[...]).astype(BF16)
        u_ref[...] = u
        za_ref[...] = _dot_nt(u, w_ref[0:S5_WIDTH, :])
        zh_ref[...] = _dot_nt(u, w_ref[S5_WIDTH:S5_WIDTH + 4 * HG_WIDTH, :])
        zg_ref[...] = _dot_nt(u, w_ref[S5_WIDTH + 4 * HG_WIDTH:, :]).astype(BF16)

    row = lambda w: pl.BlockSpec((tm, w), lambda i: (i, 0))
    return _pcall(body, "in_proj", (t // tm,),
                  [row(D_MODEL), _full((1, D_MODEL)), _full((N_IN, D_MODEL))],
                  [row(D_MODEL), row(S5_WIDTH), row(4 * HG_WIDTH), row(2 * D_MODEL)],
                  [_sds((t, D_MODEL), BF16), _sds((t, S5_WIDTH)), _sds((t, 4 * HG_WIDTH)),
                   _sds((t, 2 * D_MODEL), BF16)],
                  )(x, g_mix, w_in)


S5_LANES = 512
S5_BANDS = 4


def _band(q):
    return (slice(q * S5_WIDTH // S5_BANDS, (q + 1) * S5_WIDTH // S5_BANDS),
            slice(q * S5_N // S5_BANDS, (q + 1) * S5_N // S5_BANDS))


def _im(st):
    return slice(S5_N + st.start, S5_N + st.stop)


SCAN_UNROLL = 8


def _complex_scan(buf_ref, lam_ref, st_ref, ts, reverse):
    chunks = [slice(cc * S5_LANES, (cc + 1) * S5_LANES) for cc in range(S5_N // S5_LANES)]
    nch = len(chunks)
    wr = [lam_ref[0:1, re] for re in chunks]
    wi = [-lam_ref[1:2, re] if reverse else lam_ref[1:2, re] for re in chunks]

    def block(ib, carry):
        vr, vi = list(carry[:nch]), list(carry[nch:])
        first = ts - SCAN_UNROLL - ib * SCAN_UNROLL if reverse else ib * SCAN_UNROLL
        first = pl.multiple_of(first, SCAN_UNROLL)
        for k in range(SCAN_UNROLL):
            row = pl.ds(first + (SCAN_UNROLL - 1 - k if reverse else k), 1)
            for cc, re in enumerate(chunks):
                nr = wr[cc] * vr[cc] - wi[cc] * vi[cc] + buf_ref[row, re]
                ni = wr[cc] * vi[cc] + wi[cc] * vr[cc] + buf_ref[row, _im(re)]
                buf_ref[row, re] = nr
                buf_ref[row, _im(re)] = ni
                vr[cc], vi[cc] = nr, ni
        return tuple(vr + vi)

    init = tuple(st_ref[0:1, re] for re in chunks) + tuple(st_ref[1:2, re] for re in chunks)
    last = lax.fori_loop(0, ts // SCAN_UNROLL, block, init)
    for cc, re in enumerate(chunks):
        st_ref[0:1, re] = last[cc]
        st_ref[1:2, re] = last[nch + cc]


BAND_CH = S5_WIDTH // S5_BANDS
BAND_ST = S5_N // S5_BANDS


def _s5_fwd(za, b_bands, lam, c_bands, dskip, nb, seq, ts):
    t = za.shape[0]
    nts = seq // ts

    def body(za_ref, br_ref, bi_ref, lam_ref, cr_ref, ci_ref, d_ref, xs_ref, y_ref, buf_ref, st_ref):
        @pl.when(pl.program_id(1) == 0)
        def _():
            st_ref[...] = jnp.zeros_like(st_ref)

        zav = za_ref[...]
        for q in range(S5_BANDS):
            ch, st = _band(q)
            buf_ref[:, st] = _dot(zav[:, ch], br_ref[q])
            buf_ref[:, _im(st)] = _dot(zav[:, ch], bi_ref[q])
        _complex_scan(buf_ref, lam_ref, st_ref, ts, reverse=False)
        xs_ref[...] = buf_ref[...].astype(BF16)
        for q in range(S5_BANDS):
            ch, st = _band(q)
            y_ref[:, ch] = (_dot(xs_ref[:, st], cr_ref[q]) + _dot(xs_ref[:, _im(st)], ci_ref[q])
                            + d_ref[:, ch] * zav[:, ch])

    tok = lambda w: pl.BlockSpec((ts, w), lambda b, j: (b * nts + j, 0))
    to_st, to_ch = _full((S5_BANDS, BAND_CH, BAND_ST)), _full((S5_BANDS, BAND_ST, BAND_CH))
    return _pcall(body, "s5_fwd", (nb, nts),
                  [tok(S5_WIDTH), to_st, to_st, _full((2, S5_N)), to_ch, to_ch, _full((1, S5_WIDTH))],
                  [tok(2 * S5_N), tok(S5_WIDTH)],
                  [_sds((t, 2 * S5_N), BF16), _sds((t, S5_WIDTH))],
                  scratch=[pltpu.VMEM((ts, 2 * S5_N), F32), pltpu.VMEM((2, S5_N), F32)],
                  )(za, *b_bands, lam, *c_bands, dskip)


def _hgrn_gates(zq, zf, lbh):
    sf = _sigmoid(zf)
    f = lbh + (1.0 - lbh) * sf
    sq = _sigmoid(zq)
    qa = zq * sq * QSCALE
    bc = _cumsum_rows(jnp.log(f))
    bm = bc[CHUNK // 2 - 1:CHUNK // 2, :]
    bl = bc[CHUNK - 1:CHUNK, :]
    return sf, f, sq, qa, bc, bm, bl


def _hgrn_fwd(zh, lb, nb, seq):
    nc = seq // CHUNK

    def body(zh_ref, lb_ref, o_ref, sts_ref, st_ref):
        @pl.when(pl.program_id(0) == 0)
        def _():
            st_ref[...] = jnp.zeros_like(st_ref)

        causal = (lax.broadcasted_iota(jnp.int32, (CHUNK, CHUNK), 0)
                  >= lax.broadcasted_iota(jnp.int32, (CHUNK, CHUNK), 1))
        for b in range(nb):
            for h in range(HG_HEADS):
                hs = slice(h * HG_HEAD, (h + 1) * HG_HEAD)
                zq = zh_ref[b, :, h * HG_HEAD:(h + 1) * HG_HEAD]
                zf = zh_ref[b, :, HG_WIDTH + h * HG_HEAD:HG_WIDTH + (h + 1) * HG_HEAD]
                zi = zh_ref[b, :, 2 * HG_WIDTH + h * HG_HEAD:2 * HG_WIDTH + (h + 1) * HG_HEAD]
                _, f, _, qa, bc, bm, bl = _hgrn_gates(zq, zf, lb_ref[:, hs])
                k = 1.0 - f
                qt = qa * jnp.exp(bc - bm)
                kt = k * jnp.exp(bm - bc)
                qb = qa * jnp.exp(bc)
                kd = k * jnp.exp(bl - bc)
                st = st_ref[b, h]
                sts_ref[b, 0, h] = st
                a = jnp.where(causal, _dot_nt(qt, kt), 0.0)
                o_ref[b, :, hs] = _dot(a, zi) + _dot_nt(qb, st)
                st_ref[b, h] = st * jnp.exp(bl) + _dot_tn(zi, kd)

    return _pcall(body, "hgrn_fwd", (nc,),
                  [pl.BlockSpec((nb, CHUNK, 4 * HG_WIDTH), lambda c: (0, c, 0)), _full((1, HG_WIDTH))],
                  [pl.BlockSpec((nb, CHUNK, HG_WIDTH), lambda c: (0, c, 0)),
                   pl.BlockSpec((nb, 1, HG_HEADS, HG_HEAD, HG_HEAD), lambda c: (0, c, 0, 0, 0))],
                  [_sds((nb, seq, HG_WIDTH)), _sds((nb, nc, HG_HEADS, HG_HEAD, HG_HEAD))],
                  scratch=[pltpu.VMEM((nb, HG_HEADS, HG_HEAD, HG_HEAD), F32)])(zh, lb)


def _head_rms(o):
    parts = []
    for h in range(HG_HEADS):
        oh = o[:, h * HG_HEAD:(h + 1) * HG_HEAD]
        r = lax.rsqrt(jnp.mean(oh * oh, axis=-1, keepdims=True) + EPS)
        parts.append(jnp.broadcast_to(r, oh.shape))
    return jnp.concatenate(parts, axis=1)


def _head_mean(v):
    parts = []
    for h in range(HG_HEADS):
        vh = v[:, h * HG_HEAD:(h + 1) * HG_HEAD]
        parts.append(jnp.broadcast_to(jnp.mean(vh, axis=-1, keepdims=True), vh.shape))
    return jnp.concatenate(parts, axis=1)


def _mix_fwd(x, y0, o, zh, zgt, w_glu, b_glu, gain, w_pa, w_pb, w_out, g_ffn, tm):
    t = x.shape[0]

    def body(x_ref, y0_ref, o_ref, zg_ref, zgt_ref, wglu_ref, bglu_ref, gain_ref, wpa_ref, wpb_ref, wout_ref,
             gffn_ref, x1_ref, u2_ref, pa_ref, pb_ref, ya2_ref, yb_ref):
        ya1 = _gelu(y0_ref[...])
        s = _sigmoid(_dot(ya1, wglu_ref[...]) + bglu_ref[...])
        ya2 = (ya1 * s).astype(BF16)
        ov = o_ref[...]
        zg = zg_ref[...]
        yb = (ov * _head_rms(ov) * gain_ref[...] * (zg * _sigmoid(zg))).astype(BF16)
        ya2_ref[...] = ya2
        yb_ref[...] = yb
        pa = jnp.dot(ya2, wpa_ref[...], preferred_element_type=F32)
        pb = jnp.dot(yb, wpb_ref[...], preferred_element_type=F32)
        pa_ref[...] = pa.astype(BF16)
        pb_ref[...] = pb.astype(BF16)
        m = (_sigmoid(zgt_ref[:, 0:D_MODEL].astype(F32)) * pa
             + _sigmoid(zgt_ref[:, D_MODEL:].astype(F32)) * pb)
        x1 = x_ref[...] + _dot(m, wout_ref[...])
        x1_ref[...] = x1
        r = lax.rsqrt(jnp.mean(x1 * x1, axis=-1, keepdims=True) + EPS)
        u2_ref[...] = (x1 * r * gffn_ref[...]).astype(BF16)

    row = lambda w: pl.BlockSpec((tm, w), lambda i: (i, 0))
    return _pcall(body, "mix_fwd", (t // tm,),
                  [row(D_MODEL), row(S5_WIDTH), row(HG_WIDTH), pl.BlockSpec((tm, HG_WIDTH), lambda i: (i, 3)),
                   row(2 * D_MODEL), _full((S5_WIDTH, S5_WIDTH)), _full((1, S5_WIDTH)), _full((1, HG_WIDTH)),
                   _full((S5_WIDTH, D_MODEL)), _full((HG_WIDTH, D_MODEL)), _full((D_MODEL, D_MODEL)),
                   _full((1, D_MODEL))],
                  [row(D_MODEL), row(D_MODEL), row(D_MODEL), row(D_MODEL), row(S5_WIDTH), row(HG_WIDTH)],
                  [_sds((t, D_MODEL)), _sds((t, D_MODEL), BF16), _sds((t, D_MODEL), BF16), _sds((t, D_MODEL), BF16),
                   _sds((t, S5_WIDTH), BF16), _sds((t, HG_WIDTH), BF16)],
                  )(x, y0, o, zh, zgt, w_glu, b_glu, gain, w_pa, w_pb, w_out, g_ffn)


FF_COLS = 256
FF_UP_TILE = 1408


def _ffn_up(u2, w_up, tm):
    t = u2.shape[0]
    n = 2 * D_FF

    def body(u_ref, w_ref, h_ref):
        h_ref[...] = _dot_nt(u_ref[...], w_ref[...]).astype(BF16)

    return _pcall(body, "ffn_up", (n // FF_UP_TILE, t // tm),
                  [pl.BlockSpec((tm, D_MODEL), lambda j, i: (i, 0)),
                   pl.BlockSpec((FF_UP_TILE, D_MODEL), lambda j, i: (j, 0))],
                  pl.BlockSpec((tm, FF_UP_TILE), lambda j, i: (i, j)),
                  _sds((t, n), BF16))(u2, w_up)


HALO = 16


def _shift_matrix(tm):
    r = lax.broadcasted_iota(jnp.int32, (tm, tm), 0)
    c = lax.broadcasted_iota(jnp.int32, (tm, tm), 1)
    return jnp.where(r == c + 1, 1.0, 0.0).astype(BF16)


def _conv_cols(h_ref, halo_ref, valid, wc_ref, bc_ref, c0):
    cs = slice(c0, c0 + FF_COLS)
    cur = h_ref[:, cs].astype(F32)
    prev = jnp.where(valid, halo_ref[:, cs].astype(F32), 0.0)
    full = jnp.concatenate([prev, cur], axis=0)
    h1 = pltpu.roll(full, 1, axis=0)[HALO:]
    h2 = pltpu.roll(full, 2, axis=0)[HALO:]
    return h2 * wc_ref[0:1, cs] + h1 * wc_ref[1:2, cs] + cur * wc_ref[2:3, cs] + bc_ref[:, cs]


def _ffn_down_loss(h, x1, tgt, w_conv, b_conv, w_down, g_final, seq, tm):
    t = h.shape[0]
    tps = seq // tm
    n = 2 * D_FF

    def body(h_ref, halo_ref, x1_ref, tgt_ref, wc_ref, bc_ref, wd_ref, gf_ref,
             hc_ref, a_ref, dx2_ref, dx2b_ref, loss_ref, dgf_ref):
        i = pl.program_id(0)

        @pl.when(i == 0)
        def _():
            loss_ref[...] = jnp.zeros_like(loss_ref)
            dgf_ref[...] = jnp.zeros_like(dgf_ref)

        valid = (i % tps) != 0
        x2 = x1_ref[...]
        for j in range(D_FF // FF_COLS):
            gate = _conv_cols(h_ref, halo_ref, valid, wc_ref, bc_ref, j * FF_COLS)
            val = _conv_cols(h_ref, halo_ref, valid, wc_ref, bc_ref, D_FF + j * FF_COLS)
            hc_ref[:, j * FF_COLS:(j + 1) * FF_COLS] = gate.astype(BF16)
            hc_ref[:, D_FF + j * FF_COLS:D_FF + (j + 1) * FF_COLS] = val.astype(BF16)
            a = (gate * _sigmoid(gate) * val).astype(BF16)
            a_ref[:, j * FF_COLS:(j + 1) * FF_COLS] = a
            x2 = x2 + jnp.dot(a, wd_ref[j * FF_COLS:(j + 1) * FF_COLS, :], preferred_element_type=F32)
        r = lax.rsqrt(jnp.mean(x2 * x2, axis=-1, keepdims=True) + EPS)
        xn = x2 * r
        g = gf_ref[...]
        e = xn * g - tgt_ref[...]
        loss_ref[...] += (0.5 / D_MODEL) * jnp.sum(e * e).reshape(1, 1)
        dy = e * (1.0 / D_MODEL)
        dgf_ref[...] += jnp.sum(dy * xn, axis=0, keepdims=True)
        dxn = dy * g
        dx2 = r * (dxn - xn * jnp.mean(dxn * xn, axis=-1, keepdims=True))
        dx2_ref[...] = dx2
        dx2b_ref[...] = dx2.astype(BF16)

    row = lambda w: pl.BlockSpec((tm, w), lambda i: (i, 0))
    halo = pl.BlockSpec((HALO, n), lambda i: (jnp.maximum(i * (tm // HALO) - 1, 0), 0))
    return _pcall(body, "ffn_down_loss", (t // tm,),
                  [row(n), halo, row(D_MODEL), row(D_MODEL), _full((CONV_W, n)), _full((1, n)),
                   _full((D_FF, D_MODEL)), _full((1, D_MODEL))],
                  [row(n), row(D_FF), row(D_MODEL), row(D_MODEL), _full((1, 1)), _full((1, D_MODEL))],
                  [_sds((t, n), BF16), _sds((t, D_FF), BF16), _sds((t, D_MODEL)), _sds((t, D_MODEL), BF16),
                   _sds((1, 1)), _sds((1, D_MODEL))],
                  )(h, h, x1, tgt, w_conv, b_conv, w_down, g_final)


def _wgrad(a, b, name, tn, out_dtype=F32, band=None, after=None):
    t, m = a.shape
    n = b.shape[1] if band is None else band
    nbands = 1 if band is None else b.shape[1] // band
    after = b if after is None else after

    def body(a_ref, b_ref, after_ref, o_ref):
        o_ref[...] = _dot_tn(a_ref[...], b_ref[...]).astype(out_dtype)

    return _pcall(body, name, (m // tn,),
                  [pl.BlockSpec((t, tn), lambda i: (0, i)), pl.BlockSpec((t, n), lambda i: (0, i % nbands)),
                   pl.BlockSpec(memory_space=pl.ANY)],
                  pl.BlockSpec((tn, n), lambda i: (i, 0)), _sds((m, n), out_dtype))(a, b, after)


def _ffn_bwd_act(dx2b, hc, w_down, tm):
    t = hc.shape[0]
    n = 2 * D_FF

    def body(dx2_ref, hc_ref, wd_ref, dhc_ref, dbc_ref):
        @pl.when(pl.program_id(0) == 0)
        def _():
            dbc_ref[...] = jnp.zeros_like(dbc_ref)

        dx2 = dx2_ref[...]
        for j in range(D_FF // FF_COLS):
            gs = slice(j * FF_COLS, (j + 1) * FF_COLS)
            vs = slice(D_FF + j * FF_COLS, D_FF + (j + 1) * FF_COLS)
            gate = hc_ref[:, gs].astype(F32)
            val = hc_ref[:, vs].astype(F32)
            da = _dot_nt(dx2, wd_ref[gs, :])
            sg = _sigmoid(gate)
            dgate = da * val * (sg * (1.0 + gate * (1.0 - sg)))
            dval = da * (gate * sg)
            dhc_ref[:, gs] = dgate.astype(BF16)
            dhc_ref[:, vs] = dval.astype(BF16)
            dbc_ref[:, gs] += jnp.sum(dgate, axis=0, keepdims=True)
            dbc_ref[:, vs] += jnp.sum(dval, axis=0, keepdims=True)

    row = lambda w: pl.BlockSpec((tm, w), lambda i: (i, 0))
    return _pcall(body, "ffn_bwd_act", (t // tm,),
                  [row(D_MODEL), row(n), _full((D_FF, D_MODEL))],
                  [row(n), _full((1, n))],
                  [_sds((t, n), BF16), _sds((1, n))],
                  )(dx2b, hc, w_down)


def _ffn_bwd_up(dhc, h, dx2, x1, w_conv, w_up, g_ffn, seq, tm):
    t = dhc.shape[0]
    tps = seq // tm
    n = 2 * D_FF
    last = t // HALO - 1

    def body(dhc_ref, halo_ref, h_ref, dx2_ref, x1_ref, wc_ref, wu_ref, gf_ref,
             dh_ref, dx1_ref, dx1b_ref, dgf_ref, dwc_ref):
        i = pl.program_id(0)

        @pl.when(i == 0)
        def _():
            dgf_ref[...] = jnp.zeros_like(dgf_ref)
            dwc_ref[...] = jnp.zeros_like(dwc_ref)

        valid = ((i + 1) % tps) != 0
        du2 = jnp.zeros((tm, D_MODEL), F32)
        for j in range(n // FF_COLS):
            cs = slice(j * FF_COLS, (j + 1) * FF_COLS)
            cur = dhc_ref[:, cs].astype(F32)
            nxt = jnp.where(valid, halo_ref[:, cs].astype(F32), 0.0)
            full = jnp.concatenate([cur, nxt], axis=0)
            d1 = pltpu.roll(full, tm + HALO - 1, axis=0)[:tm]
            d2 = pltpu.roll(full, tm + HALO - 2, axis=0)[:tm]
            dh = (cur * wc_ref[2:3, cs] + d1 * wc_ref[1:2, cs] + d2 * wc_ref[0:1, cs]).astype(BF16)
            dh_ref[:, cs] = dh
            du2 = du2 + _dot(dh, wu_ref[cs, :])
            hv = h_ref[:, cs].astype(F32)
            dwc_ref[0:1, cs] += jnp.sum(hv * d2, axis=0, keepdims=True)
            dwc_ref[1:2, cs] += jnp.sum(hv * d1, axis=0, keepdims=True)
            dwc_ref[2:3, cs] += jnp.sum(hv * cur, axis=0, keepdims=True)
        x1 = x1_ref[...]
        r = lax.rsqrt(jnp.mean(x1 * x1, axis=-1, keepdims=True) + EPS)
        xn = x1 * r
        dgf_ref[...] += jnp.sum(du2 * xn, axis=0, keepdims=True)
        dxn = du2 * gf_ref[...]
        dx1 = dx2_ref[...] + r * (dxn - xn * jnp.mean(dxn * xn, axis=-1, keepdims=True))
        dx1_ref[...] = dx1
        dx1b_ref[...] = dx1.astype(BF16)

    row = lambda w: pl.BlockSpec((tm, w), lambda i: (i, 0))
    halo = pl.BlockSpec((HALO, n), lambda i: (jnp.minimum((i + 1) * (tm // HALO), last), 0))
    return _pcall(body, "ffn_bwd_up", (t // tm,),
                  [row(n), halo, row(n), row(D_MODEL), row(D_MODEL), _full((CONV_W, n)), _full((n, D_MODEL)),
                   _full((1, D_MODEL))],
                  [row(n), row(D_MODEL), row(D_MODEL), _full((1, D_MODEL)), _full((CONV_W, n))],
                  [_sds((t, n), BF16), _sds((t, D_MODEL)), _sds((t, D_MODEL), BF16), _sds((1, D_MODEL)),
                   _sds((CONV_W, n))],
                  )(dhc, dhc, h, dx2, x1, w_conv, w_up, g_ffn)


def _mix_bwd(dx1, y0, o, zh, zgt, pa, pb, w_glu, b_glu, gain, w_pa, w_pb, w_out, tm):
    t = dx1.shape[0]

    def body(dx1_ref, y0_ref, o_ref, zg_ref, zgt_ref, pa_ref, pb_ref, wglu_ref, bglu_ref, gain_ref, wpa_ref,
             wpb_ref, wout_ref,
             dy0_ref, do_ref, dzg_ref, dzgt_ref, m_ref, dpa_ref, dpb_ref, ya1_ref, dpre_ref, dbglu_ref, dgain_ref):
        @pl.when(pl.program_id(0) == 0)
        def _():
            dbglu_ref[...] = jnp.zeros_like(dbglu_ref)
            dgain_ref[...] = jnp.zeros_like(dgain_ref)

        dm = _dot_nt(dx1_ref[...], wout_ref[...])
        sga = _sigmoid(zgt_ref[:, 0:D_MODEL].astype(F32))
        sgb = _sigmoid(zgt_ref[:, D_MODEL:].astype(F32))
        pa = pa_ref[...].astype(F32)
        pb = pb_ref[...].astype(F32)
        m_ref[...] = (sga * pa + sgb * pb).astype(BF16)
        dzgt_ref[:, 0:D_MODEL] = (dm * pa * sga * (1.0 - sga)).astype(BF16)
        dzgt_ref[:, D_MODEL:] = (dm * pb * sgb * (1.0 - sgb)).astype(BF16)
        dpa = (dm * sga).astype(BF16)
        dpb = (dm * sgb).astype(BF16)
        dpa_ref[...] = dpa
        dpb_ref[...] = dpb
        dya2 = _dot_nt(dpa, wpa_ref[...])
        dyb = _dot_nt(dpb, wpb_ref[...])
        y0 = y0_ref[...]
        ya1 = _gelu(y0)
        ya1_ref[...] = ya1.astype(BF16)
        s = _sigmoid(_dot(ya1, wglu_ref[...]) + bglu_ref[...])
        dpre = dya2 * ya1 * s * (1.0 - s)
        dpre_ref[...] = dpre.astype(BF16)
        dbglu_ref[...] += jnp.sum(dpre, axis=0, keepdims=True)
        dya1 = dya2 * s + _dot_nt(dpre, wglu_ref[...])
        dy0_ref[...] = dya1 * _gelu_grad(y0)
        ov = o_ref[...]
        zg = zg_ref[...]
        oh = ov * _head_rms(ov)
        on = oh * gain_ref[...]
        sz = _sigmoid(zg)
        dzg_ref[...] = (dyb * on * (sz * (1.0 + zg * (1.0 - sz)))).astype(BF16)
        don = dyb * (zg * sz)
        dgain_ref[...] += jnp.sum(don * oh, axis=0, keepdims=True)
        doh = don * gain_ref[...]
        do_ref[...] = _head_rms(ov) * (doh - oh * _head_mean(doh * oh))

    row = lambda w: pl.BlockSpec((tm, w), lambda i: (i, 0))
    return _pcall(body, "mix_bwd", (t // tm,),
                  [row(D_MODEL), row(S5_WIDTH), row(HG_WIDTH), pl.BlockSpec((tm, HG_WIDTH), lambda i: (i, 3)),
                   row(2 * D_MODEL), row(D_MODEL), row(D_MODEL), _full((S5_WIDTH, S5_WIDTH)), _full((1, S5_WIDTH)),
                   _full((1, HG_WIDTH)), _full((S5_WIDTH, D_MODEL)), _full((HG_WIDTH, D_MODEL)),
                   _full((D_MODEL, D_MODEL))],
                  [row(S5_WIDTH), row(HG_WIDTH), row(HG_WIDTH), row(2 * D_MODEL), row(D_MODEL), row(D_MODEL),
                   row(D_MODEL), row(S5_WIDTH), row(S5_WIDTH), _full((1, S5_WIDTH)), _full((1, HG_WIDTH))],
                  [_sds((t, S5_WIDTH)), _sds((t, HG_WIDTH)), _sds((t, HG_WIDTH), BF16), _sds((t, 2 * D_MODEL), BF16),
                   _sds((t, D_MODEL), BF16), _sds((t, D_MODEL), BF16), _sds((t, D_MODEL), BF16),
                   _sds((t, S5_WIDTH), BF16), _sds((t, S5_WIDTH), BF16), _sds((1, S5_WIDTH)), _sds((1, HG_WIDTH))],
                  )(dx1, y0, o, zh, zgt, pa, pb, w_glu, b_glu, gain, w_pa, w_pb, w_out)


def _s5_bwd(dy0, za, xs, c_bands, b_bands, lam, dskip, nb, seq, ts):
    t = za.shape[0]
    nts = seq // ts

    def body(dy0_ref, za_ref, xs_ref, halo_ref, cr_ref, ci_ref, br_ref, bi_ref, lam_ref, d_ref,
             dza_ref, a_ref, dlam_ref, dd_ref, acc_ref, st_ref):
        b, j = pl.program_id(0), pl.program_id(1)

        @pl.when((b == 0) & (j == 0))
        def _():
            dlam_ref[...] = jnp.zeros_like(dlam_ref)
            dd_ref[...] = jnp.zeros_like(dd_ref)

        @pl.when(j == 0)
        def _():
            st_ref[...] = jnp.zeros_like(st_ref)

        dy0 = dy0_ref[...]
        for q in range(S5_BANDS):
            ch, st = _band(q)
            acc_ref[:, st] = _dot(dy0[:, ch], cr_ref[q])
            acc_ref[:, _im(st)] = _dot(dy0[:, ch], ci_ref[q])
        _complex_scan(acc_ref, lam_ref, st_ref, ts, reverse=True)
        a_ref[...] = acc_ref[...].astype(BF16)
        shift = _shift_matrix(ts)
        first = jnp.where(j == nts - 1, 0.0, halo_ref[HALO - 1:HALO, :].astype(F32))
        top = lax.broadcasted_iota(jnp.int32, (SUBLANES, S5_LANES), 0) == 0

        def shifted(cols):
            xp = jnp.dot(shift, xs_ref[:, cols], preferred_element_type=F32)
            return jnp.concatenate([xp[:SUBLANES] + jnp.where(top, first[:, cols], 0.0), xp[SUBLANES:]], axis=0)

        for cc in range(S5_N // S5_LANES):
            re = slice(cc * S5_LANES, (cc + 1) * S5_LANES)
            ar, ai, xr, xi = acc_ref[:, re], acc_ref[:, _im(re)], shifted(re), shifted(_im(re))
            dlam_ref[0:1, re] += jnp.sum(ar * xr + ai * xi, axis=0, keepdims=True)
            dlam_ref[1:2, re] += jnp.sum(ai * xr - ar * xi, axis=0, keepdims=True)
        for q in range(S5_BANDS):
            ch, st = _band(q)
            dza_ref[:, ch] = (_dot(a_ref[:, st], br_ref[q]) + _dot(a_ref[:, _im(st)], bi_ref[q])
                              + d_ref[:, ch] * dy0[:, ch]).astype(BF16)
        dd_ref[...] += jnp.sum(dy0 * za_ref[...], axis=0, keepdims=True)

    tile = lambda b, j: b * nts + (nts - 1 - j)
    tok = lambda w: pl.BlockSpec((ts, w), lambda b, j: (tile(b, j), 0))
    halo = pl.BlockSpec((HALO, 2 * S5_N), lambda b, j: (jnp.maximum(tile(b, j) * (ts // HALO) - 1, 0), 0))
    to_st, to_ch = _full((S5_BANDS, BAND_CH, BAND_ST)), _full((S5_BANDS, BAND_ST, BAND_CH))
    return _pcall(body, "s5_bwd", (nb, nts),
                  [tok(S5_WIDTH), tok(S5_WIDTH), tok(2 * S5_N), halo, to_st, to_st, to_ch, to_ch,
                   _full((2, S5_N)), _full((1, S5_WIDTH))],
                  [tok(S5_WIDTH), tok(2 * S5_N), _full((2, S5_N)), _full((1, S5_WIDTH))],
                  [_sds((t, S5_WIDTH), BF16), _sds((t, 2 * S5_N), BF16), _sds((2, S5_N)), _sds((1, S5_WIDTH))],
                  scratch=[pltpu.VMEM((ts, 2 * S5_N), F32), pltpu.VMEM((2, S5_N), F32)],
                  )(dy0, za, xs, xs, *c_bands, *b_bands, lam, dskip)


def _hgrn_bwd(zh, do, sts, lb, nb, seq):
    nc = seq // CHUNK

    def body(zh_ref, do_ref, sts_ref, lb_ref, dz_ref, dlb_ref, dst_ref):
        @pl.when(pl.program_id(0) == 0)
        def _():
            dst_ref[...] = jnp.zeros_like(dst_ref)
            dlb_ref[...] = jnp.zeros_like(dlb_ref)

        row = lax.broadcasted_iota(jnp.int32, (CHUNK, CHUNK), 0)
        causal = row >= lax.broadcasted_iota(jnp.int32, (CHUNK, CHUNK), 1)
        last_row = lax.broadcasted_iota(jnp.int32, (CHUNK, HG_HEAD), 0) == CHUNK - 1
        for b in range(nb):
            for h in range(HG_HEADS):
                hs = slice(h * HG_HEAD, (h + 1) * HG_HEAD)
                zq = zh_ref[b, :, h * HG_HEAD:(h + 1) * HG_HEAD]
                zf = zh_ref[b, :, HG_WIDTH + h * HG_HEAD:HG_WIDTH + (h + 1) * HG_HEAD]
                zi = zh_ref[b, :, 2 * HG_WIDTH + h * HG_HEAD:2 * HG_WIDTH + (h + 1) * HG_HEAD]
                lbh = lb_ref[:, hs]
                sf, f, sq, qa, bc, bm, bl = _hgrn_gates(zq, zf, lbh)
                k = 1.0 - f
                e_qt = jnp.exp(bc - bm)
                e_kt = jnp.exp(bm - bc)
                e_b = jnp.exp(bc)
                e_kd = jnp.exp(bl - bc)
                e_l = jnp.exp(bl)
                qt, kt, qb, kd = qa * e_qt, k * e_kt, qa * e_b, k * e_kd
                a = jnp.where(causal, _dot_nt(qt, kt), 0.0)
                st = sts_ref[b, 0, h]
                dst = dst_ref[b, h]
                dov = do_ref[b, :, hs]
                da = jnp.where(causal, _dot_nt(dov, zi), 0.0)
                dqt = _hdot(da, kt)
                dkt = _hdot_tn(da, qt)
                dqb = _dot(dov, st)
                di = _dot_tn(a, dov) + _dot_nt(kd, dst)
                dkd = _dot(zi, dst)
                de_l = jnp.sum(dst * st, axis=0, keepdims=True)
                dst_ref[b, h] = dst * e_l + _dot_tn(dov, qb)
                dqa = dqt * e_qt + dqb * e_b
                dk = dkt * e_kt + dkd * e_kd
                dbl = jnp.sum(dkd * kd, axis=0, keepdims=True) + de_l * e_l
                db = dqt * qt - dkt * kt + dqb * qb - dkd * kd + jnp.where(last_row, dbl, 0.0)
                df = _cumsum_rows(db, reverse=True) / f - dk
                dzq = dqa * QSCALE * (sq * (1.0 + zq * (1.0 - sq)))
                dzf = df * (1.0 - lbh) * sf * (1.0 - sf)
                dz_ref[b, :, h * HG_HEAD:(h + 1) * HG_HEAD] = dzq.astype(BF16)
                dz_ref[b, :, HG_WIDTH + h * HG_HEAD:HG_WIDTH + (h + 1) * HG_HEAD] = dzf.astype(BF16)
                dz_ref[b, :, 2 * HG_WIDTH + h * HG_HEAD:2 * HG_WIDTH + (h + 1) * HG_HEAD] = di.astype(BF16)
                dlb_ref[:, hs] += jnp.sum(df * (1.0 - sf), axis=0, keepdims=True)

    rev = lambda c: nc - 1 - c
    return _pcall(body, "hgrn_bwd", (nc,),
                  [pl.BlockSpec((nb, CHUNK, 4 * HG_WIDTH), lambda c: (0, rev(c), 0)),
                   pl.BlockSpec((nb, CHUNK, HG_WIDTH), lambda c: (0, rev(c), 0)),
                   pl.BlockSpec((nb, 1, HG_HEADS, HG_HEAD, HG_HEAD), lambda c: (0, rev(c), 0, 0, 0)),
                   _full((1, HG_WIDTH))],
                  [pl.BlockSpec((nb, CHUNK, 3 * HG_WIDTH), lambda c: (0, rev(c), 0)), _full((1, HG_WIDTH))],
                  [_sds((nb, seq, 3 * HG_WIDTH), BF16), _sds((1, HG_WIDTH))],
                  scratch=[pltpu.VMEM((nb, HG_HEADS, HG_HEAD, HG_HEAD), F32)])(zh, do, sts, lb)


def _in_proj_bwd(dza, dzh, dzg, dzgt, dx1, x, g_mix, w_in, tm):
    t = x.shape[0]

    def body(dza_ref, dzh_ref, dzg_ref, dzgt_ref, dx1_ref, x_ref, g_ref, w_ref, dz_ref, dx_ref, dg_ref):
        @pl.when(pl.program_id(0) == 0)
        def _():
            dg_ref[...] = jnp.zeros_like(dg_ref)

        c1, c2, c3 = S5_WIDTH, S5_WIDTH + 3 * HG_WIDTH, S5_WIDTH + 4 * HG_WIDTH
        dz_ref[:, 0:c1] = dza_ref[...]
        dz_ref[:, c1:c2] = dzh_ref[...]
        dz_ref[:, c2:c3] = dzg_ref[...]
        dz_ref[:, c3:] = dzgt_ref[...]
        du = _dot(dz_ref[...], w_ref[...])
        xv = x_ref[...]
        r = lax.rsqrt(jnp.mean(xv * xv, axis=-1, keepdims=True) + EPS)
        xn = xv * r
        dg_ref[...] += jnp.sum(du * xn, axis=0, keepdims=True)
        dxn = du * g_ref[...]
        dx_ref[...] = dx1_ref[...] + r * (dxn - xn * jnp.mean(dxn * xn, axis=-1, keepdims=True))

    row = lambda w: pl.BlockSpec((tm, w), lambda i: (i, 0))
    return _pcall(body, "in_proj_bwd", (t // tm,),
                  [row(S5_WIDTH), row(3 * HG_WIDTH), row(HG_WIDTH), row(2 * D_MODEL), row(D_MODEL), row(D_MODEL),
                   _full((1, D_MODEL)), _full((N_IN, D_MODEL))],
                  [row(N_IN), row(D_MODEL), _full((1, D_MODEL))],
                  [_sds((t, N_IN), BF16), _sds((t, D_MODEL)), _sds((1, D_MODEL))],
                  )(dza, dzh, dzg, dzgt, dx1, x, g_mix, w_in)


def _tie(*arrays):
    return jnp.zeros((SUBLANES, 128), F32) + sum(a.reshape(-1)[0].astype(F32) for a in arrays)


def _after(value, token):
    return value + token[0, 0]


def _local_step(x3, tgt3, weights, sp, emit, emit_small):
    nb, seq, _ = x3.shape
    t = nb * seq
    tm = _token_tile(seq)
    x = x3.reshape(t, D_MODEL)
    tgt = tgt3.reshape(t, D_MODEL)
    row = lambda v: v.reshape(1, -1)

    a_re, a_im, b_re, b_im = sp["s5_a_re"], sp["s5_a_im"], sp["s5_b_re"], sp["s5_b_im"]
    ldt = sp["s5_log_dt"].reshape(S5_GROUPS, 1)
    lr, li, bb_re, bb_im, lb = _params_fwd(a_re, a_im, ldt, b_re, b_im, sp["hg_lb_logits"])
    lam = jnp.concatenate([lr.reshape(1, S5_N), li.reshape(1, S5_N)], axis=0)
    swap = lambda m: m.transpose(0, 2, 1)
    b_to_st = (_band_blocks(bb_re), _band_blocks(bb_im))
    b_to_ch = (_band_blocks(swap(bb_re)), _band_blocks(swap(bb_im)))
    c_to_ch = (_band_blocks(swap(sp["s5_c_re"])), _band_blocks(swap(-sp["s5_c_im"])))
    c_to_st = (_band_blocks(sp["s5_c_re"]), _band_blocks(-sp["s5_c_im"]))

    g_mix, g_ffn, g_final = row(sp["g_mix"]), row(sp["g_ffn"]), row(sp["g_final"])
    b_glu, gain, dskip, b_conv = row(sp["b_glu"]), row(sp["hg_norm_gain"]), row(sp["s5_d"]), row(sp["b_conv"])

    w_in = weights("in", lam, *b_to_st, *b_to_ch, *c_to_ch, *c_to_st)["w_in"]
    u, za, zh, zgt = _in_proj(x, g_mix, w_in, tm)
    xs, y0 = _s5_fwd(za, b_to_st, lam, c_to_ch, dskip, nb, seq, tm)
    o3, sts = _hgrn_fwd(zh.reshape(nb, seq, 4 * HG_WIDTH), lb, nb, seq)
    o = o3.reshape(t, HG_WIDTH)
    wm = weights("mix", y0, o3)
    x1, u2, pa, pb, ya2, yb = _mix_fwd(x, y0, o, zh, zgt, wm["w_glu"], b_glu, gain, wm["w_pa"], wm["w_pb"],
                                       wm["w_out"], g_ffn, tm)
    wf = weights("ffn", u2)
    h = _ffn_up(u2, wf["w_up"], min(4 * tm, t))
    hc, a, dx2, dx2b, loss, dg_final = _ffn_down_loss(h, x1, tgt, wf["w_conv"], b_conv, wf["w_down"], g_final,
                                                      seq, tm)

    wgrad = functools.partial(_wgrad, tn=256, out_dtype=BF16)
    dhc, db_conv = _ffn_bwd_act(dx2b, hc, wf["w_down"], tm)
    sent = emit({"w_down": wgrad(a, dx2b, "dw_down")})
    dh, dx1, dx1b, dg_ffn, dw_conv = _ffn_bwd_up(dhc, h, dx2, x1, wf["w_conv"], wf["w_up"], _after(g_ffn, sent),
                                                 seq, tm)
    sent = emit({"w_up": wgrad(dh, u2, "dw_up"), "w_conv": dw_conv})
    (dy0, do, dzg, dzgt, m, dpa, dpb, ya1, dpre, db_glu, dgain) = _mix_bwd(
        dx1b, y0, o, zh, zgt, pa, pb, wm["w_glu"], _after(b_glu, sent), gain, wm["w_pa"], wm["w_pb"], wm["w_out"], tm)
    sent = emit({"w_out": wgrad(m, dx1b, "dw_out"), "w_pa": wgrad(ya2, dpa, "dw_pa"),
                 "w_pb": wgrad(yb, dpb, "dw_pb"), "w_glu": wgrad(ya1, dpre, "dw_glu")})
    dzh3, dlb = _hgrn_bwd(zh.reshape(nb, seq, 4 * HG_WIDTH), do.reshape(nb, seq, HG_WIDTH), sts, _after(lb, sent),
                          nb, seq)
    dza, a_s5, dlam, dd = _s5_bwd(dy0, za, xs, c_to_st, b_to_ch, lam, dskip, nb, seq, tm)
    dz, dx, dg_mix = _in_proj_bwd(dza, dzh3.reshape(t, 3 * HG_WIDTH), dzg, dzgt, dx1, x, g_mix, w_in, tm)
    sent = emit({"w_in": wgrad(dz, u, "dw_in")})

    band = HG_HEAD
    dbb_band = _wgrad(a_s5, za, "dbb_s5", 512, band=band, after=sent)
    dc_band = _wgrad(xs, dy0, "dc_s5", 512, band=band, after=sent)
    dbb_re = swap(_diag_blocks(dbb_band[:S5_N], S5_STATE, S5_GROUP))
    dbb_im = swap(_diag_blocks(dbb_band[S5_N:], S5_STATE, S5_GROUP))
    dc_re = swap(_diag_blocks(dc_band[:S5_N], S5_STATE, S5_GROUP))
    dc_im = -swap(_diag_blocks(dc_band[S5_N:], S5_STATE, S5_GROUP))
    da_re, da_im, dldt, db_re, db_im, dlogits = _params_bwd(
        a_re, a_im, ldt, b_re, b_im, sp["hg_lb_logits"],
        dlam[0].reshape(S5_GROUPS, S5_STATE), dlam[1].reshape(S5_GROUPS, S5_STATE), dbb_re, dbb_im, dlb)
    emit_small({"g_mix": dg_mix, "s5_a_re": da_re, "s5_a_im": da_im, "s5_log_dt": dldt.reshape(1, S5_GROUPS),
                "s5_b_re": db_re, "s5_b_im": db_im, "s5_c_re": dc_re, "s5_c_im": dc_im, "s5_d": dd, "b_glu": db_glu,
                "hg_lb_logits": dlogits, "hg_norm_gain": dgain, "g_ffn": dg_ffn, "b_conv": db_conv,
                "g_final": dg_final, "loss": loss})
    return dx.reshape(nb, seq, D_MODEL)


def _mesh_peers():
    x, y, c = lax.axis_index("x"), lax.axis_index("y"), lax.axis_index("c")
    peers = []
    for k in range(1, N_DEV):
        px, py, pc = (1 - x if k & 4 else x), (1 - y if k & 2 else y), (1 - c if k & 1 else c)
        peers.append((k, (px, py, pc), 4 * px + 2 * py + pc))
    return 4 * x + 2 * y + c, peers


_HBM = pl.BlockSpec(memory_space=pltpu.HBM)
_SEM = pl.BlockSpec(memory_space=pltpu.SEMAPHORE)


def _exchange_start(name, operands, after, place_own=True):
    n = len(operands)
    me = 4 * lax.axis_index("x") + 2 * lax.axis_index("y") + lax.axis_index("c")
    flags = [per_peer for _, per_peer in operands]
    srcs, lands = [], []
    for arr, per_peer in operands:
        land = lax.empty((N_DEV,) + (arr.shape[1:] if per_peer else arr.shape), arr.dtype)
        if place_own:
            own = lax.dynamic_index_in_dim(arr, me, 0, keepdims=True) if per_peer else arr[None]
            land = lax.dynamic_update_slice_in_dim(land, own, me, 0)
        srcs.append(pltpu.with_memory_space_constraint(arr, pltpu.HBM))
        lands.append(pltpu.with_memory_space_constraint(land, pltpu.HBM))
    copies = (N_DEV - 1) * n

    def body(*refs):
        src_refs, land_refs = refs[:n], refs[n:2 * n]
        send_sems, recv_sems = refs[2 * n + 1], refs[2 * n + 2]
        token = refs[-1]
        my_slab, peers = _mesh_peers()
        for k, peer, slab in peers:
            for i in range(n):
                s = (k - 1) * n + i
                pltpu.make_async_remote_copy(
                    src_ref=src_refs[i].at[slab] if flags[i] else src_refs[i], dst_ref=land_refs[i].at[my_slab],
                    send_sem=send_sems.at[s], recv_sem=recv_sems.at[s], device_id=peer,
                    device_id_type=pl.DeviceIdType.MESH).start()
        token[...] = jnp.zeros_like(token)

    outs = pl.pallas_call(
        body, name=name,
        out_shape=(pltpu.SemaphoreType.DMA((copies,)), pltpu.SemaphoreType.DMA((copies,)),
                   *[pltpu.HBM(a.shape, a.dtype) for a in srcs], *[pltpu.HBM(a.shape, a.dtype) for a in lands],
                   _sds((SUBLANES, 128))),
        in_specs=[_HBM] * (2 * n) + [pl.BlockSpec(memory_space=pl.ANY)],
        out_specs=(_SEM, _SEM, *[_HBM] * (2 * n), pl.BlockSpec(memory_space=pltpu.VMEM)),
        input_output_aliases={i: 2 + i for i in range(2 * n)},
        compiler_params=pltpu.CompilerParams(has_side_effects=pltpu.SideEffectType.DATAFLOW_SIDE_EFFECTING),
    )(*srcs, *lands, after)
    state = (flags, outs[0], outs[1], outs[2:2 + n], outs[2 + n:2 + 2 * n])
    return state, outs[-1]


def _exchange_wait(name, state, *after):
    flags, send_sems, recv_sems, srcs, lands = state
    n = len(flags)

    def body(*refs):
        src_refs, land_refs = refs[:n], refs[n:2 * n]
        send_ref, recv_ref = refs[2 * n], refs[2 * n + 1]
        _, peers = _mesh_peers()
        for k, peer, slab in peers:
            for i in range(n):
                s = (k - 1) * n + i
                copy = pltpu.make_async_remote_copy(
                    src_ref=src_refs[i].at[slab] if flags[i] else src_refs[i], dst_ref=land_refs[i].at[slab],
                    send_sem=send_ref.at[s], recv_sem=recv_ref.at[s], device_id=peer,
                    device_id_type=pl.DeviceIdType.MESH)
                copy.wait_send()
                copy.wait_recv()

    outs = pl.pallas_call(
        body, name=name,
        out_shape=(*[pltpu.HBM(a.shape, a.dtype) for a in srcs], *[pltpu.HBM(a.shape, a.dtype) for a in lands]),
        in_specs=[_HBM] * (2 * n) + [_SEM, _SEM] + [pl.BlockSpec(memory_space=pl.ANY)] * len(after),
        out_specs=tuple([_HBM] * (2 * n)),
        input_output_aliases={i: i for i in range(2 * n)},
        compiler_params=pltpu.CompilerParams(has_side_effects=pltpu.SideEffectType.DATAFLOW_SIDE_EFFECTING),
    )(*srcs, *lands, send_sems, recv_sems, *after)
    return list(outs[n:]), list(outs[:n])


def _join_cols(parts, name, tr):
    _, r, c = parts.shape

    def body(p_ref, o_ref):
        for j in range(N_DEV):
            o_ref[:, j * c:(j + 1) * c] = p_ref[j]

    return _pcall(body, name, (r // tr,), [pl.BlockSpec((N_DEV, tr, c), lambda i: (0, i, 0))],
                  pl.BlockSpec((tr, N_DEV * c), lambda i: (i, 0)), _sds((r, N_DEV * c), parts.dtype))(parts)


def _split_cols(full, name, tr):
    r, c = full.shape[0], full.shape[1] // N_DEV

    def body(f_ref, o_ref):
        for j in range(N_DEV):
            o_ref[j] = f_ref[:, j * c:(j + 1) * c]

    return _pcall(body, name, (r // tr,), [pl.BlockSpec((tr, N_DEV * c), lambda i: (i, 0))],
                  pl.BlockSpec((N_DEV, tr, c), lambda i: (0, i, 0)), _sds((N_DEV, r, c), full.dtype))(full)


def _my_slab():
    return (4 * lax.axis_index("x") + 2 * lax.axis_index("y") + lax.axis_index("c")).astype(jnp.int32).reshape(1)


def _adamw(parts, sent, w, m, v, name, tile):
    _, rows, cols = w.shape

    def body(me_ref, p_ref, s_ref, w_ref, m_ref, v_ref, g_out, d_out, m_out, v_out):
        me = me_ref[0]
        g = jnp.where(me == 0, s_ref[0], p_ref[0]).astype(F32)
        for k in range(1, N_DEV):
            g = g + jnp.where(me == k, s_ref[0], p_ref[k]).astype(F32)
        m1 = ADAM_B1 * m_ref[0] + (1.0 - ADAM_B1) * g
        v1 = ADAM_B2 * v_ref[0] + (1.0 - ADAM_B2) * (g * g)
        m_hat = m1 / (1.0 - ADAM_B1 ** ADAM_STEP)
        v_hat = v1 / (1.0 - ADAM_B2 ** ADAM_STEP)
        g_out[0] = g
        d_out[0] = -ADAM_LR * (m_hat / (jnp.sqrt(v_hat) + ADAM_EPS) + ADAM_WD * w_ref[0])
        m_out[0] = m1
        v_out[0] = v1

    row = pl.BlockSpec((1, tile, cols), lambda i, me: (0, i, 0))
    return pl.pallas_call(
        body, name=name, out_shape=[_sds((1, rows, cols))] * 4,
        grid_spec=pltpu.PrefetchScalarGridSpec(
            num_scalar_prefetch=1, grid=(rows // tile,),
            in_specs=[pl.BlockSpec((N_DEV, tile, cols), lambda i, me: (0, i, 0)),
                      pl.BlockSpec((1, tile, cols), lambda i, me: (me[0], i, 0)), row, row, row],
            out_specs=[row, row, row, row]),
        compiler_params=pltpu.CompilerParams(dimension_semantics=("arbitrary",), vmem_limit_bytes=VMEM_LIMIT),
    )(_my_slab(), parts, sent, w, m, v)


BIG = {
    "w_in": ((N_IN // N_DEV, D_MODEL), False, N_IN // N_DEV // 3),
    "w_glu": ((S5_WIDTH // N_DEV, S5_WIDTH), False, S5_WIDTH // N_DEV),
    "w_pa": ((S5_WIDTH, D_MODEL // N_DEV), True, S5_WIDTH),
    "w_pb": ((HG_WIDTH, D_MODEL // N_DEV), True, HG_WIDTH),
    "w_out": ((D_MODEL // N_DEV, D_MODEL), False, D_MODEL // N_DEV),
    "w_up": ((2 * D_FF // N_DEV, D_MODEL), False, 2 * D_FF // N_DEV // 4),
    "w_conv": ((CONV_W, 2 * D_FF // N_DEV), True, CONV_W),
    "w_down": ((D_FF // N_DEV, D_MODEL), False, D_FF // N_DEV // 2),
}
TRANSPOSED = ("w_in", "w_up", "s5_b_re", "s5_b_im")
UNALIGNED_COLS = ("w_conv",)


def _stored(n, arr):
    return jnp.swapaxes(arr, -1, -2) if n in TRANSPOSED else arr


def _join_shards(n, parts):
    (a, b), by_cols, _ = BIG[n]
    if not by_cols:
        return parts.reshape(N_DEV * a, b)
    if n in UNALIGNED_COLS:
        return _join_cols(parts, "join_" + n, min(a, 256))
    return parts.transpose(1, 0, 2).reshape(a, N_DEV * b)


def _split_shards(n, full):
    (a, b), by_cols, _ = BIG[n]
    if not by_cols:
        return full.reshape(N_DEV, a, b)
    if n in UNALIGNED_COLS:
        return _split_cols(full, "split_" + n, min(a, 256))
    return full.reshape(a, N_DEV, b).transpose(1, 0, 2)


SMALL_CORE = {
    "s5_b_re": GSC, "s5_b_im": GSC, "s5_c_re": GSC, "s5_c_im": GSC,
    "g_mix": (1, D_MODEL), "g_ffn": (1, D_MODEL), "g_final": (1, D_MODEL), "s5_d": (1, S5_WIDTH),
    "b_glu": (1, S5_WIDTH), "hg_norm_gain": (1, HG_WIDTH), "hg_lb_logits": (2, HG_WIDTH), "b_conv": (1, 2 * D_FF),
    "s5_log_dt": (1, S5_GROUPS), "s5_a_re": (S5_GROUPS, S5_STATE), "s5_a_im": (S5_GROUPS, S5_STATE), "loss": (1, 1),
}
BLOCK_ROWS = 32


def _small_rows():
    rows, r = {}, 0
    for n, core in SMALL_CORE.items():
        rows[n] = r
        r += BLOCK_ROWS if len(core) == 3 else -(-math.prod(core) // PACK_W)
    return rows, -(-r // SUBLANES) * SUBLANES


SMALL_ROW, SMALL_ROWS = _small_rows()


def _small_pieces(name):
    r, core = SMALL_ROW[name], SMALL_CORE[name]
    if len(core) == 3:
        return [((g, slice(None), slice(None)), slice(r + S5_GROUP * (g % 2), r + S5_GROUP * (g % 2 + 1)),
                 slice(S5_STATE * (g // 2), S5_STATE * (g // 2 + 1))) for g in range(S5_GROUPS)]
    pieces = []
    for i in range(core[0]):
        for c0 in range(0, core[1], PACK_W):
            w, flat = min(PACK_W, core[1] - c0), i * core[1] + c0
            pieces.append(((slice(i, i + 1), slice(c0, c0 + w)), slice(r + flat // PACK_W, r + flat // PACK_W + 1),
                           slice(flat % PACK_W, flat % PACK_W + w)))
    return pieces


def _core_index(ref, name, idx):
    return (0,) * (len(ref.shape) - len(SMALL_CORE[name])) + idx


def _pack_small_grads(grads):
    names = list(SMALL_CORE)

    def body(*refs):
        pack = refs[-1]
        pack[...] = jnp.zeros_like(pack)
        for ref, n in zip(refs, names):
            for idx, rows, lanes in _small_pieces(n):
                pack[rows, lanes] = ref[_core_index(ref, n, idx)]

    return _pcall(body, "pack_small_grads", (1,), [_full(grads[n].shape) for n in names],
                  _full((SMALL_ROWS, PACK_W)), _sds((SMALL_ROWS, PACK_W)))(*[grads[n] for n in names])


def _adamw_small(parts, sent, names, rows, given, name):
    lo, hi = rows
    k = len(names)
    shapes = [given[n].shape for n in names]

    def body(*refs):
        me, p_ref, s_ref, ins, outs = refs[0][0], refs[1], refs[2], refs[3:3 + 3 * k], refs[3 + 3 * k:3 + 7 * k]
        packs, results = refs[3 + 7 * k:6 + 7 * k], refs[6 + 7 * k:]
        for j, pack in enumerate(packs):
            pack[...] = jnp.zeros_like(pack)
            for ref, n in zip(ins[j * k:(j + 1) * k], names):
                for idx, prow, lanes in _small_pieces(n):
                    pack[slice(prow.start - lo, prow.stop - lo), lanes] = ref[_core_index(ref, n, idx)]
        mine = s_ref[lo:hi, :]
        g = jnp.where(me == 0, mine, p_ref[0, lo:hi, :])
        for d in range(1, N_DEV):
            g = g + jnp.where(me == d, mine, p_ref[d, lo:hi, :])
        m1 = ADAM_B1 * packs[1][...] + (1.0 - ADAM_B1) * g
        v1 = ADAM_B2 * packs[2][...] + (1.0 - ADAM_B2) * (g * g)
        m_hat = m1 / (1.0 - ADAM_B1 ** ADAM_STEP)
        v_hat = v1 / (1.0 - ADAM_B2 ** ADAM_STEP)
        results[0][...] = g
        results[1][...] = -ADAM_LR * (m_hat / (jnp.sqrt(v_hat) + ADAM_EPS) + ADAM_WD * packs[0][...])
        results[2][...] = m1
        results[3][...] = v1
        for j, result in enumerate(results):
            for ref, n in zip(outs[j * k:(j + 1) * k], names):
                for idx, prow, lanes in _small_pieces(n):
                    ref[_core_index(ref, n, idx)] = result[slice(prow.start - lo, prow.stop - lo), lanes]

    flat = _pcall(body, name, (1,),
                  [pl.BlockSpec(memory_space=pltpu.SMEM), _full(parts.shape), _full(sent.shape)]
                  + [_full(s) for s in shapes] * 3,
                  [_full(s) for s in shapes] * 4, [_sds(s) for s in shapes] * 4,
                  scratch=[pltpu.VMEM((hi - lo, PACK_W), F32)] * 7,
                  )(_my_slab(), parts, sent, *[given[pre + n] for pre in ("", "m_", "v_") for n in names])
    return {n: [flat[j * k + i] for j in range(4)] for i, n in enumerate(names)}


def kernel(x, g_mix, w_in, s5_a_re, s5_a_im, s5_log_dt, s5_b_re, s5_b_im, s5_c_re, s5_c_im, s5_d, w_glu, b_glu, hg_lb_logits, hg_norm_gain, w_pa, w_pb, w_out, g_ffn, w_up, w_conv, b_conv, w_down, g_final, loss_target, m_g_mix, m_w_in, m_s5_a_re, m_s5_a_im, m_s5_log_dt, m_s5_b_re, m_s5_b_im, m_s5_c_re, m_s5_c_im, m_s5_d, m_w_glu, m_b_glu, m_hg_lb_logits, m_hg_norm_gain, m_w_pa, m_w_pb, m_w_out, m_g_ffn, m_w_up, m_w_conv, m_b_conv, m_w_down, m_g_final, v_g_mix, v_w_in, v_s5_a_re, v_s5_a_im, v_s5_log_dt, v_s5_b_re, v_s5_b_im, v_s5_c_re, v_s5_c_im, v_s5_d, v_w_glu, v_b_glu, v_hg_lb_logits, v_hg_norm_gain, v_w_pa, v_w_pb, v_w_out, v_g_ffn, v_w_up, v_w_conv, v_b_conv, v_w_down, v_g_final):
    given = dict(locals())
    small_names = [n for n, _ in SMALL]

    pay = {n: given[n][0] if n == "w_conv" else _stored(n, given[n])[0].astype(BF16) for n in BIG}
    groups = {"in": ["w_in"], "mix": ["w_glu", "w_pa", "w_pb", "w_out"], "ffn": ["w_up", "w_down", "w_conv"]}
    gathers, order = {}, pay["w_in"]
    for grp, names in groups.items():
        gathers[grp], order = _exchange_start("gather_" + grp + "_start", [(pay[n], False) for n in names], order)

    def weights(grp, *after):
        if grp == "in":
            after = (*after, order)
        got, _ = _exchange_wait("gather_" + grp + "_wait", gathers[grp], *after)
        return {n: _join_shards(n, g) for n, g in zip(groups[grp], got)}

    in_flight, started = [], []

    def emit(grads):
        names = list(grads)
        state, token = _exchange_start("grads_" + names[0] + "_start",
                                       [(_split_shards(n, grads[n]), True) for n in names], grads[names[0]],
                                       place_own=False)
        in_flight.append((names, state))
        return token

    def emit_small(grads):
        pack = _pack_small_grads(grads)
        state, token = _exchange_start("grads_small_start", [(pack, False)], pack, place_own=False)
        in_flight.append((["small"], state))
        started.append(token)

    sp = {n: (given[n] if n in ("g_final", "hg_lb_logits") else _stored(n, given[n])[0]) for n in small_names}
    sp["g_mix"] = _after(sp["g_mix"], order)
    dx = _local_step(x, loss_target, weights, sp, emit, emit_small)

    res = {}
    after = [started[-1]]
    in_flight.insert(-1, in_flight.pop())
    for names, state in in_flight:
        parts, sent = _exchange_wait("grads_" + names[0] + "_wait", state, *after)
        if names != ["small"]:
            after = []
            for n, part, mine in zip(names, parts, sent):
                raw = _adamw(part, mine, *[_stored(n, given[pre + n]) for pre in ("", "m_", "v_")], "adamw_" + n,
                             BIG[n][2])
                res[n] = [_stored(n, r) for r in raw]
                after.append(raw[0])
            continue
        sgiven = {pre + n: _stored(n, given[pre + n]) for pre in ("", "m_", "v_") for n in small_names}
        for pre in ("", "m_", "v_"):
            sgiven[pre + "g_final"] = given[pre + "g_final"].reshape(1, D_MODEL)
            sgiven[pre + "loss"] = jnp.zeros((1, 1), F32)
        raw = _adamw_small(parts[0], sent[0], list(SMALL_CORE), (0, SMALL_ROWS), sgiven, "adamw_small")
        res.update({n: [_stored(n, r) for r in raw[n]] for n in small_names})
        res["g_final"] = [r.reshape(D_MODEL) for r in raw["g_final"]]
        total_loss = raw["loss"][0].reshape(())
        after = [raw["s5_b_re"][0], raw["g_mix"][0]]
    return (total_loss, dx, *[res[n][0] for n in WEIGHT_ORDER], *[res[n][1] for n in WEIGHT_ORDER],
            *[res[n][2] for n in WEIGHT_ORDER], *[res[n][3] for n in WEIGHT_ORDER])
```

```python
import functools
import math

import jax
import jax.numpy as jnp
from jax import lax
from jax.experimental import pallas as pl
from jax.experimental.pallas import tpu as pltpu

F32 = jnp.float32
BF16 = jnp.bfloat16

D_MODEL = 1024
S5_WIDTH = 512
S5_GROUP = 16
S5_GROUPS = 32
S5_STATE = 64
S5_N = S5_GROUPS * S5_STATE
HG_WIDTH = 512
HG_HEAD = 128
HG_HEADS = 4
D_FF = 2816
CONV_W = 3
CHUNK = 64
N_IN = S5_WIDTH + 4 * HG_WIDTH + 2 * D_MODEL
EPS = 1e-6
QSCALE = HG_HEAD ** -0.5

ADAM_LR = 0.001
ADAM_B1 = 0.9
ADAM_B2 = 0.999
ADAM_EPS = 1e-08
ADAM_WD = 0.01
ADAM_STEP = 10

N_DEV = 8
V7X_VMEM_BYTES = 64 * 1024 * 1024
VMEM_LIMIT = V7X_VMEM_BYTES * 7 // 8
SUBLANES = 8
PACK_W = 1024

SMALL = (
    ("g_mix", (1, D_MODEL)),
    ("s5_a_re", (1, S5_GROUPS, S5_STATE)),
    ("s5_a_im", (1, S5_GROUPS, S5_STATE)),
    ("s5_log_dt", (1, S5_GROUPS)),
    ("s5_b_re", (1, S5_GROUPS, S5_STATE, S5_GROUP)),
    ("s5_b_im", (1, S5_GROUPS, S5_STATE, S5_GROUP)),
    ("s5_c_re", (1, S5_GROUPS, S5_GROUP, S5_STATE)),
    ("s5_c_im", (1, S5_GROUPS, S5_GROUP, S5_STATE)),
    ("s5_d", (1, S5_WIDTH)),
    ("b_glu", (1, S5_WIDTH)),
    ("hg_lb_logits", (2, HG_WIDTH)),
    ("hg_norm_gain", (1, HG_WIDTH)),
    ("g_ffn", (1, D_MODEL)),
    ("b_conv", (1, 2 * D_FF)),
    ("g_final", (D_MODEL,)),
)
WEIGHT_ORDER = ("g_mix", "w_in", "s5_a_re", "s5_a_im", "s5_log_dt", "s5_b_re", "s5_b_im", "s5_c_re", "s5_c_im",
                "s5_d", "w_glu", "b_glu", "hg_lb_logits", "hg_norm_gain", "w_pa", "w_pb", "w_out", "g_ffn",
                "w_up", "w_conv", "b_conv", "w_down", "g_final")


def _pcall(body, name, grid, in_specs, out_specs, out_shape, scratch=()):
    return pl.pallas_call(
        body, name=name, grid=grid, in_specs=in_specs, out_specs=out_specs, out_shape=out_shape,
        scratch_shapes=list(scratch),
        compiler_params=pltpu.CompilerParams(dimension_semantics=("arbitrary",) * len(grid),
                                             vmem_limit_bytes=VMEM_LIMIT),
    )


def _full(shape):
    return pl.BlockSpec(shape, lambda *_: (0,) * len(shape))


def _sds(shape, dtype=F32):
    return jax.ShapeDtypeStruct(shape, dtype)


def _dot(a, b):
    return jnp.dot(a.astype(BF16), b.astype(BF16), preferred_element_type=F32)


def _dot_nt(a, b):
    return lax.dot_general(a.astype(BF16), b.astype(BF16), (((1,), (1,)), ((), ())), preferred_element_type=F32)


def _dot_tn(a, b):
    return lax.dot_general(a.astype(BF16), b.astype(BF16), (((0,), (0,)), ((), ())), preferred_element_type=F32)


def _hdot(a, b):
    return jnp.dot(a, b, preferred_element_type=F32, precision=lax.Precision.HIGHEST)


def _hdot_tn(a, b):
    return lax.dot_general(a, b, (((0,), (0,)), ((), ())), preferred_element_type=F32,
                           precision=lax.Precision.HIGHEST)


def _sigmoid(x):
    return jax.nn.sigmoid(x)


GELU_C = math.sqrt(2.0 / math.pi)
GELU_A = 0.044715


def _gelu(x):
    return 0.5 * x * (1.0 + jnp.tanh(GELU_C * (x + GELU_A * (x * x * x))))


def _gelu_grad(x):
    t = jnp.tanh(GELU_C * (x + GELU_A * (x * x * x)))
    return 0.5 * (1.0 + t) + 0.5 * x * (1.0 - t * t) * (GELU_C * (1.0 + 3.0 * GELU_A * x * x))


def _cumsum_rows(v, reverse=False):
    n = v.shape[0]
    row = lax.broadcasted_iota(jnp.int32, v.shape, 0)
    s = 1
    while s < n:
        if reverse:
            v = v + jnp.where(row < n - s, pltpu.roll(v, n - s, axis=0), 0.0)
        else:
            v = v + jnp.where(row >= s, pltpu.roll(v, s, axis=0), 0.0)
        s *= 2
    return v


def _token_tile(seq):
    return min(256, seq)


def _s5_coeffs(a_re, a_im, ldt):
    dt = jnp.exp(ldt)
    mag = jnp.exp(a_re * dt)
    ang = a_im * dt
    lb_re = mag * jnp.cos(ang)
    lb_im = mag * jnp.sin(ang)
    den = a_re * a_re + a_im * a_im
    n_re = lb_re - 1.0
    n_im = lb_im
    co_re = (n_re * a_re + n_im * a_im) / den
    co_im = (n_im * a_re - n_re * a_im) / den
    return lb_re, lb_im, co_re, co_im


GS, GSC = (S5_GROUPS, S5_STATE), (S5_GROUPS, S5_GROUP, S5_STATE)


def _params_fwd(a_re, a_im, ldt, bt_re, bt_im, logits):
    def body(are, aim, ld, bre, bim, lg, lr_o, li_o, bbr_o, bbi_o, lb_o):
        lr, li, co_re, co_im = _s5_coeffs(are[...], aim[...], ld[...])
        lr_o[...] = lr
        li_o[...] = li
        for g in range(S5_GROUPS):
            cr, ci = co_re[g:g + 1, :], co_im[g:g + 1, :]
            bbr_o[g] = cr * bre[g] - ci * bim[g]
            bbi_o[g] = cr * bim[g] + ci * bre[g]
        lb_o[...] = _sigmoid(lg[0:1, :] - lg[1:2, :])

    return _pcall(body, "params_fwd", (1,),
                  [_full(GS), _full(GS), _full((S5_GROUPS, 1)), _full(GSC), _full(GSC), _full((2, HG_WIDTH))],
                  [_full(GS), _full(GS), _full(GSC), _full(GSC), _full((1, HG_WIDTH))],
                  [_sds(GS), _sds(GS), _sds(GSC), _sds(GSC), _sds((1, HG_WIDTH))],
                  )(a_re, a_im, ldt, bt_re, bt_im, logits)


def _params_bwd(a_re, a_im, ldt, bt_re, bt_im, logits, dlr, dli, dbbr, dbbi, dlb):
    def body(are, aim, ld, bre, bim, lg, dlr_r, dli_r, dbbr_r, dbbi_r, dlb_r,
             dare_o, daim_o, dld_o, dbre_o, dbim_o, dlg_o, dcr_ref, dci_ref):
        (_, _, co_re, co_im), vjp = jax.vjp(_s5_coeffs, are[...], aim[...], ld[...])
        for g in range(S5_GROUPS):
            cr, ci = co_re[g:g + 1, :], co_im[g:g + 1, :]
            gr, gi, br, bi = dbbr_r[g], dbbi_r[g], bre[g], bim[g]
            dbre_o[g] = cr * gr + ci * gi
            dbim_o[g] = cr * gi - ci * gr
            dcr_ref[g:g + 1, :] = jnp.sum(gr * br + gi * bi, axis=0, keepdims=True)
            dci_ref[g:g + 1, :] = jnp.sum(gi * br - gr * bi, axis=0, keepdims=True)
        dare, daim, dld = vjp((dlr_r[...], dli_r[...], dcr_ref[...], dci_ref[...]))
        dare_o[...] = dare
        daim_o[...] = daim
        dld_o[...] = dld
        lb = _sigmoid(lg[0:1, :] - lg[1:2, :])
        d0 = dlb_r[...] * lb * (1.0 - lb)
        dlg_o[0:1, :] = d0
        dlg_o[1:2, :] = -d0

    return _pcall(body, "params_bwd", (1,),
                  [_full(GS), _full(GS), _full((S5_GROUPS, 1)), _full(GSC), _full(GSC), _full((2, HG_WIDTH)),
                   _full(GS), _full(GS), _full(GSC), _full(GSC), _full((1, HG_WIDTH))],
                  [_full(GS), _full(GS), _full((S5_GROUPS, 1)), _full(GSC), _full(GSC), _full((2, HG_WIDTH))],
                  [_sds(GS), _sds(GS), _sds((S5_GROUPS, 1)), _sds(GSC), _sds(GSC), _sds((2, HG_WIDTH))],
                  scratch=[pltpu.VMEM(GS, F32), pltpu.VMEM(GS, F32)],
                  )(a_re, a_im, ldt, bt_re, bt_im, logits, dlr, dli, dbbr, dbbi, dlb)


def _band_blocks(m):
    g, r, c = m.shape
    gb = g // S5_BANDS
    m4 = m.astype(BF16).reshape(S5_BANDS, gb, r, c)
    on_diag = jnp.eye(gb, dtype=bool)[None, :, None, :, None]
    return jnp.where(on_diag, m4[:, :, :, None, :], 0).reshape(S5_BANDS, gb * r, gb * c)


def _diag_blocks(band, r, c):
    g, nb = band.shape[0] // r, band.shape[1] // c
    on_diag = (jnp.arange(g) % nb)[:, None, None, None] == jnp.arange(nb)[None, None, :, None]
    return jnp.sum(jnp.where(on_diag, band.reshape(g, r, nb, c), 0.0), axis=2)


def _in_proj(x, g_mix, w_in, tm):
    t = x.shape[0]

    def body(x_ref, g_ref, w_ref, u_ref, za_ref, zh_ref, zg_ref):
        xv = x_ref[...]
        r = lax.rsqrt(jnp.mean(xv * xv, axis=-1, keepdims=True) + EPS)
        u = (xv * r * g_ref[...]).astype(BF16)
        u_ref[...] = u
        za_ref[...] = _dot_nt(u, w_ref[0:S5_WIDTH, :])
        zh_ref[...] = _dot_nt(u, w_ref[S5_WIDTH:S5_WIDTH + 4 * HG_WIDTH, :])
        zg_ref[...] = _dot_nt(u, w_ref[S5_WIDTH + 4 * HG_WIDTH:, :]).astype(BF16)

    row = lambda w: pl.BlockSpec((tm, w), lambda i: (i, 0))
    return _pcall(body, "in_proj", (t // tm,),
                  [row(D_MODEL), _full((1, D_MODEL)), _full((N_IN, D_MODEL))],
                  [row(D_MODEL), row(S5_WIDTH), row(4 * HG_WIDTH), row(2 * D_MODEL)],
                  [_sds((t, D_MODEL), BF16), _sds((t, S5_WIDTH)), _sds((t, 4 * HG_WIDTH)),
                   _sds((t, 2 * D_MODEL), BF16)],
                  )(x, g_mix, w_in)


S5_LANES = 512
S5_BANDS = 4


def _band(q):
    return (slice(q * S5_WIDTH // S5_BANDS, (q + 1) * S5_WIDTH // S5_BANDS),
            slice(q * S5_N // S5_BANDS, (q + 1) * S5_N // S5_BANDS))


def _im(st):
    return slice(S5_N + st.start, S5_N + st.stop)


SCAN_UNROLL = 8


def _complex_scan(buf_ref, lam_ref, st_ref, nb, ts, reverse):
    lanes = [slice(cc * S5_LANES, (cc + 1) * S5_LANES) for cc in range(S5_N // S5_LANES)]
    chains = [(b, re) for b in range(nb) for re in lanes]
    nch = len(chains)
    wr = {re.start: lam_ref[0:1, re] for re in lanes}
    wi = {re.start: -lam_ref[1:2, re] if reverse else lam_ref[1:2, re] for re in lanes}

    def block(ib, carry):
        vr, vi = list(carry[:nch]), list(carry[nch:])
        first = ts - SCAN_UNROLL - ib * SCAN_UNROLL if reverse else ib * SCAN_UNROLL
        first = pl.multiple_of(first, SCAN_UNROLL)
        for k in range(SCAN_UNROLL):
            row = pl.ds(first + (SCAN_UNROLL - 1 - k if reverse else k), 1)
            for c, (b, re) in enumerate(chains):
                nr = wr[re.start] * vr[c] - wi[re.start] * vi[c] + buf_ref[b, row, re]
                ni = wr[re.start] * vi[c] + wi[re.start] * vr[c] + buf_ref[b, row, _im(re)]
                buf_ref[b, row, re] = nr
                buf_ref[b, row, _im(re)] = ni
                vr[c], vi[c] = nr, ni
        return tuple(vr + vi)

    init = tuple(st_ref[b, 0:1, re] for b, re in chains) + tuple(st_ref[b, 1:2, re] for b, re in chains)
    last = lax.fori_loop(0, ts // SCAN_UNROLL, block, init)
    for c, (b, re) in enumerate(chains):
        st_ref[b, 0:1, re] = last[c]
        st_ref[b, 1:2, re] = last[nch + c]


BAND_CH = S5_WIDTH // S5_BANDS
BAND_ST = S5_N // S5_BANDS


def _s5_fwd(za, b_bands, lam, c_bands, dskip, nb, seq, ts):
    nts = seq // ts

    def body(za_ref, br_ref, bi_ref, lam_ref, cr_ref, ci_ref, d_ref, xs_ref, y_ref, buf_ref, st_ref):
        @pl.when(pl.program_id(0) == 0)
        def _():
            st_ref[...] = jnp.zeros_like(st_ref)

        for b in range(nb):
            zav = za_ref[b]
            for q in range(S5_BANDS):
                ch, st = _band(q)
                buf_ref[b, :, st] = _dot(zav[:, ch], br_ref[q])
                buf_ref[b, :, _im(st)] = _dot(zav[:, ch], bi_ref[q])
        _complex_scan(buf_ref, lam_ref, st_ref, nb, ts, reverse=False)
        for b in range(nb):
            zav = za_ref[b]
            xs_ref[b] = buf_ref[b].astype(BF16)
            for q in range(S5_BANDS):
                ch, st = _band(q)
                y_ref[b, :, ch] = (_dot(xs_ref[b, :, st], cr_ref[q]) + _dot(xs_ref[b, :, _im(st)], ci_ref[q])
                                   + d_ref[:, ch] * zav[:, ch])

    tok = lambda w: pl.BlockSpec((nb, ts, w), lambda j: (0, j, 0))
    to_st, to_ch = _full((S5_BANDS, BAND_CH, BAND_ST)), _full((S5_BANDS, BAND_ST, BAND_CH))
    return _pcall(body, "s5_fwd", (nts,),
                  [tok(S5_WIDTH), to_st, to_st, _full((2, S5_N)), to_ch, to_ch, _full((1, S5_WIDTH))],
                  [tok(2 * S5_N), tok(S5_WIDTH)],
                  [_sds((nb, seq, 2 * S5_N), BF16), _sds((nb, seq, S5_WIDTH))],
                  scratch=[pltpu.VMEM((nb, ts, 2 * S5_N), F32), pltpu.VMEM((nb, 2, S5_N), F32)],
                  )(za, *b_bands, lam, *c_bands, dskip)


def _hgrn_gates(zq, zf, lbh):
    sf = _sigmoid(zf)
    f = lbh + (1.0 - lbh) * sf
    sq = _sigmoid(zq)
    qa = zq * sq * QSCALE
    bc = _cumsum_rows(jnp.log(f))
    bm = bc[CHUNK // 2 - 1:CHUNK // 2, :]
    bl = bc[CHUNK - 1:CHUNK, :]
    return sf, f, sq, qa, bc, bm, bl


def _hgrn_fwd(zh, lb, nb, seq):
    nc = seq // CHUNK

    def body(zh_ref, lb_ref, o_ref, sts_ref, st_ref):
        @pl.when(pl.program_id(0) == 0)
        def _():
            st_ref[...] = jnp.zeros_like(st_ref)

        causal = (lax.broadcasted_iota(jnp.int32, (CHUNK, CHUNK), 0)
                  >= lax.broadcasted_iota(jnp.int32, (CHUNK, CHUNK), 1))
        for b in range(nb):
            for h in range(HG_HEADS):
                hs = slice(h * HG_HEAD, (h + 1) * HG_HEAD)
                zq = zh_ref[b, :, h * HG_HEAD:(h + 1) * HG_HEAD]
                zf = zh_ref[b, :, HG_WIDTH + h * HG_HEAD:HG_WIDTH + (h + 1) * HG_HEAD]
                zi = zh_ref[b, :, 2 * HG_WIDTH + h * HG_HEAD:2 * HG_WIDTH + (h + 1) * HG_HEAD]
                _, f, _, qa, bc, bm, bl = _hgrn_gates(zq, zf, lb_ref[:, hs])
                k = 1.0 - f
                qt = qa * jnp.exp(bc - bm)
                kt = k * jnp.exp(bm - bc)
                qb = qa * jnp.exp(bc)
                kd = k * jnp.exp(bl - bc)
                st = st_ref[b, h]
                sts_ref[b, 0, h] = st
                a = jnp.where(causal, _dot_nt(qt, kt), 0.0)
                o_ref[b, :, hs] = _dot(a, zi) + _dot_nt(qb, st)
                st_ref[b, h] = st * jnp.exp(bl) + _dot_tn(zi, kd)

    return _pcall(body, "hgrn_fwd", (nc,),
                  [pl.BlockSpec((nb, CHUNK, 4 * HG_WIDTH), lambda c: (0, c, 0)), _full((1, HG_WIDTH))],
                  [pl.BlockSpec((nb, CHUNK, HG_WIDTH), lambda c: (0, c, 0)),
                   pl.BlockSpec((nb, 1, HG_HEADS, HG_HEAD, HG_HEAD), lambda c: (0, c, 0, 0, 0))],
                  [_sds((nb, seq, HG_WIDTH)), _sds((nb, nc, HG_HEADS, HG_HEAD, HG_HEAD))],
                  scratch=[pltpu.VMEM((nb, HG_HEADS, HG_HEAD, HG_HEAD), F32)])(zh, lb)


def _head_rms(o):
    parts = []
    for h in range(HG_HEADS):
        oh = o[:, h * HG_HEAD:(h + 1) * HG_HEAD]
        r = lax.rsqrt(jnp.mean(oh * oh, axis=-1, keepdims=True) + EPS)
        parts.append(jnp.broadcast_to(r, oh.shape))
    return jnp.concatenate(parts, axis=1)


def _head_mean(v):
    parts = []
    for h in range(HG_HEADS):
        vh = v[:, h * HG_HEAD:(h + 1) * HG_HEAD]
        parts.append(jnp.broadcast_to(jnp.mean(vh, axis=-1, keepdims=True), vh.shape))
    return jnp.concatenate(parts, axis=1)


def _mix_fwd(x, y0, o, zh, zgt, w_glu, b_glu, gain, w_pa, w_pb, w_out, g_ffn, tm):
    t = x.shape[0]

    def body(x_ref, y0_ref, o_ref, zg_ref, zgt_ref, wglu_ref, bglu_ref, gain_ref, wpa_ref, wpb_ref, wout_ref,
             gffn_ref, x1_ref, u2_ref, pa_ref, pb_ref, ya2_ref, yb_ref):
        ya1 = _gelu(y0_ref[...])
        s = _sigmoid(_dot(ya1, wglu_ref[...]) + bglu_ref[...])
        ya2 = (ya1 * s).astype(BF16)
        ov = o_ref[...]
        zg = zg_ref[...]
        yb = (ov * _head_rms(ov) * gain_ref[...] * (zg * _sigmoid(zg))).astype(BF16)
        ya2_ref[...] = ya2
        yb_ref[...] = yb
        pa = jnp.dot(ya2, wpa_ref[...], preferred_element_type=F32)
        pb = jnp.dot(yb, wpb_ref[...], preferred_element_type=F32)
        pa_ref[...] = pa.astype(BF16)
        pb_ref[...] = pb.astype(BF16)
        m = (_sigmoid(zgt_ref[:, 0:D_MODEL].astype(F32)) * pa
             + _sigmoid(zgt_ref[:, D_MODEL:].astype(F32)) * pb)
        x1 = x_ref[...] + _dot(m, wout_ref[...])
        x1_ref[...] = x1
        r = lax.rsqrt(jnp.mean(x1 * x1, axis=-1, keepdims=True) + EPS)
        u2_ref[...] = (x1 * r * gffn_ref[...]).astype(BF16)

    row = lambda w: pl.BlockSpec((tm, w), lambda i: (i, 0))
    return _pcall(body, "mix_fwd", (t // tm,),
                  [row(D_MODEL), row(S5_WIDTH), row(HG_WIDTH), pl.BlockSpec((tm, HG_WIDTH), lambda i: (i, 3)),
                   row(2 * D_MODEL), _full((S5_WIDTH, S5_WIDTH)), _full((1, S5_WIDTH)), _full((1, HG_WIDTH)),
                   _full((S5_WIDTH, D_MODEL)), _full((HG_WIDTH, D_MODEL)), _full((D_MODEL, D_MODEL)),
                   _full((1, D_MODEL))],
                  [row(D_MODEL), row(D_MODEL), row(D_MODEL), row(D_MODEL), row(S5_WIDTH), row(HG_WIDTH)],
                  [_sds((t, D_MODEL)), _sds((t, D_MODEL), BF16), _sds((t, D_MODEL), BF16), _sds((t, D_MODEL), BF16),
                   _sds((t, S5_WIDTH), BF16), _sds((t, HG_WIDTH), BF16)],
                  )(x, y0, o, zh, zgt, w_glu, b_glu, gain, w_pa, w_pb, w_out, g_ffn)


FF_COLS = 256
FF_UP_TILE = 1408


def _ffn_up(u2, w_up, tm):
    t = u2.shape[0]
    n = 2 * D_FF

    def body(u_ref, w_ref, h_ref):
        h_ref[...] = _dot_nt(u_ref[...], w_ref[...]).astype(BF16)

    return _pcall(body, "ffn_up", (n // FF_UP_TILE, t // tm),
                  [pl.BlockSpec((tm, D_MODEL), lambda j, i: (i, 0)),
                   pl.BlockSpec((FF_UP_TILE, D_MODEL), lambda j, i: (j, 0))],
                  pl.BlockSpec((tm, FF_UP_TILE), lambda j, i: (i, j)),
                  _sds((t, n), BF16))(u2, w_up)


HALO = 16


def _shift_matrix(tm):
    r = lax.broadcasted_iota(jnp.int32, (tm, tm), 0)
    c = lax.broadcasted_iota(jnp.int32, (tm, tm), 1)
    return jnp.where(r == c + 1, 1.0, 0.0).astype(BF16)


def _conv_cols(h_ref, halo_ref, valid, wc_ref, bc_ref, c0):
    cs = slice(c0, c0 + FF_COLS)
    cur = h_ref[:, cs].astype(F32)
    prev = jnp.where(valid, halo_ref[:, cs].astype(F32), 0.0)
    full = jnp.concatenate([prev, cur], axis=0)
    h1 = pltpu.roll(full, 1, axis=0)[HALO:]
    h2 = pltpu.roll(full, 2, axis=0)[HALO:]
    return h2 * wc_ref[0:1, cs] + h1 * wc_ref[1:2, cs] + cur * wc_ref[2:3, cs] + bc_ref[:, cs]


def _ffn_down_loss(h, x1, tgt, w_conv, b_conv, w_down, g_final, seq, tm):
    t = h.shape[0]
    tps = seq // tm
    n = 2 * D_FF

    def body(h_ref, halo_ref, x1_ref, tgt_ref, wc_ref, bc_ref, wd_ref, gf_ref,
             hc_ref, a_ref, dx2_ref, dx2b_ref, loss_ref, dgf_ref):
        i = pl.program_id(0)

        @pl.when(i == 0)
        def _():
            loss_ref[...] = jnp.zeros_like(loss_ref)
            dgf_ref[...] = jnp.zeros_like(dgf_ref)

        valid = (i % tps) != 0
        x2 = x1_ref[...]
        for j in range(D_FF // FF_COLS):
            gate = _conv_cols(h_ref, halo_ref, valid, wc_ref, bc_ref, j * FF_COLS)
            val = _conv_cols(h_ref, halo_ref, valid, wc_ref, bc_ref, D_FF + j * FF_COLS)
            hc_ref[:, j * FF_COLS:(j + 1) * FF_COLS] = gate.astype(BF16)
            hc_ref[:, D_FF + j * FF_COLS:D_FF + (j + 1) * FF_COLS] = val.astype(BF16)
            a = (gate * _sigmoid(gate) * val).astype(BF16)
            a_ref[:, j * FF_COLS:(j + 1) * FF_COLS] = a
            x2 = x2 + jnp.dot(a, wd_ref[j * FF_COLS:(j + 1) * FF_COLS, :], preferred_element_type=F32)
        r = lax.rsqrt(jnp.mean(x2 * x2, axis=-1, keepdims=True) + EPS)
        xn = x2 * r
        g = gf_ref[...]
        e = xn * g - tgt_ref[...]
        loss_ref[...] += (0.5 / D_MODEL) * jnp.sum(e * e).reshape(1, 1)
        dy = e * (1.0 / D_MODEL)
        dgf_ref[...] += jnp.sum(dy * xn, axis=0, keepdims=True)
        dxn = dy * g
        dx2 = r * (dxn - xn * jnp.mean(dxn * xn, axis=-1, keepdims=True))
        dx2_ref[...] = dx2
        dx2b_ref[...] = dx2.astype(BF16)

    row = lambda w: pl.BlockSpec((tm, w), lambda i: (i, 0))
    halo = pl.BlockSpec((HALO, n), lambda i: (jnp.maximum(i * (tm // HALO) - 1, 0), 0))
    return _pcall(body, "ffn_down_loss", (t // tm,),
                  [row(n), halo, row(D_MODEL), row(D_MODEL), _full((CONV_W, n)), _full((1, n)),
                   _full((D_FF, D_MODEL)), _full((1, D_MODEL))],
                  [row(n), row(D_FF), row(D_MODEL), row(D_MODEL), _full((1, 1)), _full((1, D_MODEL))],
                  [_sds((t, n), BF16), _sds((t, D_FF), BF16), _sds((t, D_MODEL)), _sds((t, D_MODEL), BF16),
                   _sds((1, 1)), _sds((1, D_MODEL))],
                  )(h, h, x1, tgt, w_conv, b_conv, w_down, g_final)


def _wgrad(a, b, name, tn, out_dtype=F32, band=None, after=None):
    t, m = a.shape
    n = b.shape[1] if band is None else band
    nbands = 1 if band is None else b.shape[1] // band
    after = b if after is None else after

    def body(a_ref, b_ref, after_ref, o_ref):
        o_ref[...] = _dot_tn(a_ref[...], b_ref[...]).astype(out_dtype)

    return _pcall(body, name, (m // tn,),
                  [pl.BlockSpec((t, tn), lambda i: (0, i)), pl.BlockSpec((t, n), lambda i: (0, i % nbands)),
                   pl.BlockSpec(memory_space=pl.ANY)],
                  pl.BlockSpec((tn, n), lambda i: (i, 0)), _sds((m, n), out_dtype))(a, b, after)


def _ffn_bwd_act(dx2b, hc, w_down, tm):
    t = hc.shape[0]
    n = 2 * D_FF

    def body(dx2_ref, hc_ref, wd_ref, dhc_ref, dbc_ref):
        @pl.when(pl.program_id(0) == 0)
        def _():
            dbc_ref[...] = jnp.zeros_like(dbc_ref)

        dx2 = dx2_ref[...]
        for j in range(D_FF // FF_COLS):
            gs = slice(j * FF_COLS, (j + 1) * FF_COLS)
            vs = slice(D_FF + j * FF_COLS, D_FF + (j + 1) * FF_COLS)
            gate = hc_ref[:, gs].astype(F32)
            val = hc_ref[:, vs].astype(F32)
            da = _dot_nt(dx2, wd_ref[gs, :])
            sg = _sigmoid(gate)
            dgate = da * val * (sg * (1.0 + gate * (1.0 - sg)))
            dval = da * (gate * sg)
            dhc_ref[:, gs] = dgate.astype(BF16)
            dhc_ref[:, vs] = dval.astype(BF16)
            dbc_ref[:, gs] += jnp.sum(dgate, axis=0, keepdims=True)
            dbc_ref[:, vs] += jnp.sum(dval, axis=0, keepdims=True)

    row = lambda w: pl.BlockSpec((tm, w), lambda i: (i, 0))
    return _pcall(body, "ffn_bwd_act", (t // tm,),
                  [row(D_MODEL), row(n), _full((D_FF, D_MODEL))],
                  [row(n), _full((1, n))],
                  [_sds((t, n), BF16), _sds((1, n))],
                  )(dx2b, hc, w_down)


def _ffn_bwd_up(dhc, h, dx2, x1, w_conv, w_up, g_ffn, seq, tm):
    t = dhc.shape[0]
    tps = seq // tm
    n = 2 * D_FF
    last = t // HALO - 1

    def body(dhc_ref, halo_ref, h_ref, dx2_ref, x1_ref, wc_ref, wu_ref, gf_ref,
             dh_ref, dx1_ref, dx1b_ref, dgf_ref, dwc_ref):
        i = pl.program_id(0)

        @pl.when(i == 0)
        def _():
            dgf_ref[...] = jnp.zeros_like(dgf_ref)
            dwc_ref[...] = jnp.zeros_like(dwc_ref)

        valid = ((i + 1) % tps) != 0
        du2 = jnp.zeros((tm, D_MODEL), F32)
        for j in range(n // FF_COLS):
            cs = slice(j * FF_COLS, (j + 1) * FF_COLS)
            cur = dhc_ref[:, cs].astype(F32)
            nxt = jnp.where(valid, halo_ref[:, cs].astype(F32), 0.0)
            full = jnp.concatenate([cur, nxt], axis=0)
            d1 = pltpu.roll(full, tm + HALO - 1, axis=0)[:tm]
            d2 = pltpu.roll(full, tm + HALO - 2, axis=0)[:tm]
            dh = (cur * wc_ref[2:3, cs] + d1 * wc_ref[1:2, cs] + d2 * wc_ref[0:1, cs]).astype(BF16)
            dh_ref[:, cs] = dh
            du2 = du2 + _dot(dh, wu_ref[cs, :])
            hv = h_ref[:, cs].astype(F32)
            dwc_ref[0:1, cs] += jnp.sum(hv * d2, axis=0, keepdims=True)
            dwc_ref[1:2, cs] += jnp.sum(hv * d1, axis=0, keepdims=True)
            dwc_ref[2:3, cs] += jnp.sum(hv * cur, axis=0, keepdims=True)
        x1 = x1_ref[...]
        r = lax.rsqrt(jnp.mean(x1 * x1, axis=-1, keepdims=True) + EPS)
        xn = x1 * r
        dgf_ref[...] += jnp.sum(du2 * xn, axis=0, keepdims=True)
        dxn = du2 * gf_ref[...]
        dx1 = dx2_ref[...] + r * (dxn - xn * jnp.mean(dxn * xn, axis=-1, keepdims=True))
        dx1_ref[...] = dx1
        dx1b_ref[...] = dx1.astype(BF16)

    row = lambda w: pl.BlockSpec((tm, w), lambda i: (i, 0))
    halo = pl.BlockSpec((HALO, n), lambda i: (jnp.minimum((i + 1) * (tm // HALO), last), 0))
    return _pcall(body, "ffn_bwd_up", (t // tm,),
                  [row(n), halo, row(n), row(D_MODEL), row(D_MODEL), _full((CONV_W, n)), _full((n, D_MODEL)),
                   _full((1, D_MODEL))],
                  [row(n), row(D_MODEL), row(D_MODEL), _full((1, D_MODEL)), _full((CONV_W, n))],
                  [_sds((t, n), BF16), _sds((t, D_MODEL)), _sds((t, D_MODEL), BF16), _sds((1, D_MODEL)),
                   _sds((CONV_W, n))],
                  )(dhc, dhc, h, dx2, x1, w_conv, w_up, g_ffn)


def _mix_bwd(dx1, y0, o, zh, zgt, pa, pb, w_glu, b_glu, gain, w_pa, w_pb, w_out, tm):
    t = dx1.shape[0]

    def body(dx1_ref, y0_ref, o_ref, zg_ref, zgt_ref, pa_ref, pb_ref, wglu_ref, bglu_ref, gain_ref, wpa_ref,
             wpb_ref, wout_ref,
             dy0_ref, do_ref, dzg_ref, dzgt_ref, m_ref, dpa_ref, dpb_ref, ya1_ref, dpre_ref, dbglu_ref, dgain_ref):
        @pl.when(pl.program_id(0) == 0)
        def _():
            dbglu_ref[...] = jnp.zeros_like(dbglu_ref)
            dgain_ref[...] = jnp.zeros_like(dgain_ref)

        dm = _dot_nt(dx1_ref[...], wout_ref[...])
        sga = _sigmoid(zgt_ref[:, 0:D_MODEL].astype(F32))
        sgb = _sigmoid(zgt_ref[:, D_MODEL:].astype(F32))
        pa = pa_ref[...].astype(F32)
        pb = pb_ref[...].astype(F32)
        m_ref[...] = (sga * pa + sgb * pb).astype(BF16)
        dzgt_ref[:, 0:D_MODEL] = (dm * pa * sga * (1.0 - sga)).astype(BF16)
        dzgt_ref[:, D_MODEL:] = (dm * pb * sgb * (1.0 - sgb)).astype(BF16)
        dpa = (dm * sga).astype(BF16)
        dpb = (dm * sgb).astype(BF16)
        dpa_ref[...] = dpa
        dpb_ref[...] = dpb
        dya2 = _dot_nt(dpa, wpa_ref[...])
        dyb = _dot_nt(dpb, wpb_ref[...])
        y0 = y0_ref[...]
        ya1 = _gelu(y0)
        ya1_ref[...] = ya1.astype(BF16)
        s = _sigmoid(_dot(ya1, wglu_ref[...]) + bglu_ref[...])
        dpre = dya2 * ya1 * s * (1.0 - s)
        dpre_ref[...] = dpre.astype(BF16)
        dbglu_ref[...] += jnp.sum(dpre, axis=0, keepdims=True)
        dya1 = dya2 * s + _dot_nt(dpre, wglu_ref[...])
        dy0_ref[...] = dya1 * _gelu_grad(y0)
        ov = o_ref[...]
        zg = zg_ref[...]
        oh = ov * _head_rms(ov)
        on = oh * gain_ref[...]
        sz = _sigmoid(zg)
        dzg_ref[...] = (dyb * on * (sz * (1.0 + zg * (1.0 - sz)))).astype(BF16)
        don = dyb * (zg * sz)
        dgain_ref[...] += jnp.sum(don * oh, axis=0, keepdims=True)
        doh = don * gain_ref[...]
        do_ref[...] = _head_rms(ov) * (doh - oh * _head_mean(doh * oh))

    row = lambda w: pl.BlockSpec((tm, w), lambda i: (i, 0))
    return _pcall(body, "mix_bwd", (t // tm,),
                  [row(D_MODEL), row(S5_WIDTH), row(HG_WIDTH), pl.BlockSpec((tm, HG_WIDTH), lambda i: (i, 3)),
                   row(2 * D_MODEL), row(D_MODEL), row(D_MODEL), _full((S5_WIDTH, S5_WIDTH)), _full((1, S5_WIDTH)),
                   _full((1, HG_WIDTH)), _full((S5_WIDTH, D_MODEL)), _full((HG_WIDTH, D_MODEL)),
                   _full((D_MODEL, D_MODEL))],
                  [row(S5_WIDTH), row(HG_WIDTH), row(HG_WIDTH), row(2 * D_MODEL), row(D_MODEL), row(D_MODEL),
                   row(D_MODEL), row(S5_WIDTH), row(S5_WIDTH), _full((1, S5_WIDTH)), _full((1, HG_WIDTH))],
                  [_sds((t, S5_WIDTH)), _sds((t, HG_WIDTH)), _sds((t, HG_WIDTH), BF16), _sds((t, 2 * D_MODEL), BF16),
                   _sds((t, D_MODEL), BF16), _sds((t, D_MODEL), BF16), _sds((t, D_MODEL), BF16),
                   _sds((t, S5_WIDTH), BF16), _sds((t, S5_WIDTH), BF16), _sds((1, S5_WIDTH)), _sds((1, HG_WIDTH))],
                  )(dx1, y0, o, zh, zgt, pa, pb, w_glu, b_glu, gain, w_pa, w_pb, w_out)


def _s5_bwd(dy0, za, xs, c_bands, b_bands, lam, dskip, nb, seq, ts):
    nts = seq // ts

    def body(dy0_ref, za_ref, xs_ref, halo_ref, cr_ref, ci_ref, br_ref, bi_ref, lam_ref, d_ref,
             dza_ref, a_ref, dlam_ref, dd_ref, acc_ref, st_ref):
        j = pl.program_id(0)

        @pl.when(j == 0)
        def _():
            dlam_ref[...] = jnp.zeros_like(dlam_ref)
            dd_ref[...] = jnp.zeros_like(dd_ref)
            st_ref[...] = jnp.zeros_like(st_ref)

        for b in range(nb):
            dy0 = dy0_ref[b]
            for q in range(S5_BANDS):
                ch, st = _band(q)
                acc_ref[b, :, st] = _dot(dy0[:, ch], cr_ref[q])
                acc_ref[b, :, _im(st)] = _dot(dy0[:, ch], ci_ref[q])
        _complex_scan(acc_ref, lam_ref, st_ref, nb, ts, reverse=True)
        shift = _shift_matrix(ts)
        top = lax.broadcasted_iota(jnp.int32, (SUBLANES, S5_LANES), 0) == 0
        for b in range(nb):
            a_ref[b] = acc_ref[b].astype(BF16)
            first = jnp.where(j == nts - 1, 0.0, halo_ref[b, HALO - 1:HALO, :].astype(F32))

            def shifted(cols):
                xp = jnp.dot(shift, xs_ref[b, :, cols], preferred_element_type=F32)
                return jnp.concatenate([xp[:SUBLANES] + jnp.where(top, first[:, cols], 0.0), xp[SUBLANES:]], axis=0)

            for cc in range(S5_N // S5_LANES):
                re = slice(cc * S5_LANES, (cc + 1) * S5_LANES)
                ar, ai, xr, xi = acc_ref[b, :, re], acc_ref[b, :, _im(re)], shifted(re), shifted(_im(re))
                dlam_ref[0:1, re] += jnp.sum(ar * xr + ai * xi, axis=0, keepdims=True)
                dlam_ref[1:2, re] += jnp.sum(ai * xr - ar * xi, axis=0, keepdims=True)
            dy0 = dy0_ref[b]
            for q in range(S5_BANDS):
                ch, st = _band(q)
                dza_ref[b, :, ch] = (_dot(a_ref[b, :, st], br_ref[q]) + _dot(a_ref[b, :, _im(st)], bi_ref[q])
                                     + d_ref[:, ch] * dy0[:, ch]).astype(BF16)
            dd_ref[...] += jnp.sum(dy0 * za_ref[b], axis=0, keepdims=True)

    tile = lambda j: nts - 1 - j
    tok = lambda w: pl.BlockSpec((nb, ts, w), lambda j: (0, tile(j), 0))
    halo = pl.BlockSpec((nb, HALO, 2 * S5_N), lambda j: (0, jnp.maximum(tile(j) * (ts // HALO) - 1, 0), 0))
    to_st, to_ch = _full((S5_BANDS, BAND_CH, BAND_ST)), _full((S5_BANDS, BAND_ST, BAND_CH))
    return _pcall(body, "s5_bwd", (nts,),
                  [tok(S5_WIDTH), tok(S5_WIDTH), tok(2 * S5_N), halo, to_st, to_st, to_ch, to_ch,
                   _full((2, S5_N)), _full((1, S5_WIDTH))],
                  [tok(S5_WIDTH), tok(2 * S5_N), _full((2, S5_N)), _full((1, S5_WIDTH))],
                  [_sds((nb, seq, S5_WIDTH), BF16), _sds((nb, seq, 2 * S5_N), BF16), _sds((2, S5_N)),
                   _sds((1, S5_WIDTH))],
                  scratch=[pltpu.VMEM((nb, ts, 2 * S5_N), F32), pltpu.VMEM((nb, 2, S5_N), F32)],
                  )(dy0, za, xs, xs, *c_bands, *b_bands, lam, dskip)


def _hgrn_bwd(zh, do, sts, lb, nb, seq):
    nc = seq // CHUNK

    def body(zh_ref, do_ref, sts_ref, lb_ref, dz_ref, dlb_ref, dst_ref):
        @pl.when(pl.program_id(0) == 0)
        def _():
            dst_ref[...] = jnp.zeros_like(dst_ref)
            dlb_ref[...] = jnp.zeros_like(dlb_ref)

        row = lax.broadcasted_iota(jnp.int32, (CHUNK, CHUNK), 0)
        causal = row >= lax.broadcasted_iota(jnp.int32, (CHUNK, CHUNK), 1)
        last_row = lax.broadcasted_iota(jnp.int32, (CHUNK, HG_HEAD), 0) == CHUNK - 1
        for b in range(nb):
            for h in range(HG_HEADS):
                hs = slice(h * HG_HEAD, (h + 1) * HG_HEAD)
                zq = zh_ref[b, :, h * HG_HEAD:(h + 1) * HG_HEAD]
                zf = zh_ref[b, :, HG_WIDTH + h * HG_HEAD:HG_WIDTH + (h + 1) * HG_HEAD]
                zi = zh_ref[b, :, 2 * HG_WIDTH + h * HG_HEAD:2 * HG_WIDTH + (h + 1) * HG_HEAD]
                lbh = lb_ref[:, hs]
                sf, f, sq, qa, bc, bm, bl = _hgrn_gates(zq, zf, lbh)
                k = 1.0 - f
                e_qt = jnp.exp(bc - bm)
                e_kt = jnp.exp(bm - bc)
                e_b = jnp.exp(bc)
                e_kd = jnp.exp(bl - bc)
                e_l = jnp.exp(bl)
                qt, kt, qb, kd = qa * e_qt, k * e_kt, qa * e_b, k * e_kd
                a = jnp.where(causal, _dot_nt(qt, kt), 0.0)
                st = sts_ref[b, 0, h]
                dst = dst_ref[b, h]
                dov = do_ref[b, :, hs]
                da = jnp.where(causal, _dot_nt(dov, zi), 0.0)
                dqt = _hdot(da, kt)
                dkt = _hdot_tn(da, qt)
                dqb = _dot(dov, st)
                di = _dot_tn(a, dov) + _dot_nt(kd, dst)
                dkd = _dot(zi, dst)
                de_l = jnp.sum(dst * st, axis=0, keepdims=True)
                dst_ref[b, h] = dst * e_l + _dot_tn(dov, qb)
                dqa = dqt * e_qt + dqb * e_b
                dk = dkt * e_kt + dkd * e_kd
                dbl = jnp.sum(dkd * kd, axis=0, keepdims=True) + de_l * e_l
                db = dqt * qt - dkt * kt + dqb * qb - dkd * kd + jnp.where(last_row, dbl, 0.0)
                df = _cumsum_rows(db, reverse=True) / f - dk
                dzq = dqa * QSCALE * (sq * (1.0 + zq * (1.0 - sq)))
                dzf = df * (1.0 - lbh) * sf * (1.0 - sf)
                dz_ref[b, :, h * HG_HEAD:(h + 1) * HG_HEAD] = dzq.astype(BF16)
                dz_ref[b, :, HG_WIDTH + h * HG_HEAD:HG_WIDTH + (h + 1) * HG_HEAD] = dzf.astype(BF16)
                dz_ref[b, :, 2 * HG_WIDTH + h * HG_HEAD:2 * HG_WIDTH + (h + 1) * HG_HEAD] = di.astype(BF16)
                dlb_ref[:, hs] += jnp.sum(df * (1.0 - sf), axis=0, keepdims=True)

    rev = lambda c: nc - 1 - c
    return _pcall(body, "hgrn_bwd", (nc,),
                  [pl.BlockSpec((nb, CHUNK, 4 * HG_WIDTH), lambda c: (0, rev(c), 0)),
                   pl.BlockSpec((nb, CHUNK, HG_WIDTH), lambda c: (0, rev(c), 0)),
                   pl.BlockSpec((nb, 1, HG_HEADS, HG_HEAD, HG_HEAD), lambda c: (0, rev(c), 0, 0, 0)),
                   _full((1, HG_WIDTH))],
                  [pl.BlockSpec((nb, CHUNK, 3 * HG_WIDTH), lambda c: (0, rev(c), 0)), _full((1, HG_WIDTH))],
                  [_sds((nb, seq, 3 * HG_WIDTH), BF16), _sds((1, HG_WIDTH))],
                  scratch=[pltpu.VMEM((nb, HG_HEADS, HG_HEAD, HG_HEAD), F32)])(zh, do, sts, lb)


def _in_proj_bwd(dza, dzh, dzg, dzgt, dx1, x, g_mix, w_in, tm):
    t = x.shape[0]

    def body(dza_ref, dzh_ref, dzg_ref, dzgt_ref, dx1_ref, x_ref, g_ref, w_ref, dz_ref, dx_ref, dg_ref):
        @pl.when(pl.program_id(0) == 0)
        def _():
            dg_ref[...] = jnp.zeros_like(dg_ref)

        c1, c2, c3 = S5_WIDTH, S5_WIDTH + 3 * HG_WIDTH, S5_WIDTH + 4 * HG_WIDTH
        dz_ref[:, 0:c1] = dza_ref[...]
        dz_ref[:, c1:c2] = dzh_ref[...]
        dz_ref[:, c2:c3] = dzg_ref[...]
        dz_ref[:, c3:] = dzgt_ref[...]
        du = _dot(dz_ref[...], w_ref[...])
        xv = x_ref[...]
        r = lax.rsqrt(jnp.mean(xv * xv, axis=-1, keepdims=True) + EPS)
        xn = xv * r
        dg_ref[...] += jnp.sum(du * xn, axis=0, keepdims=True)
        dxn = du * g_ref[...]
        dx_ref[...] = dx1_ref[...] + r * (dxn - xn * jnp.mean(dxn * xn, axis=-1, keepdims=True))

    row = lambda w: pl.BlockSpec((tm, w), lambda i: (i, 0))
    return _pcall(body, "in_proj_bwd", (t // tm,),
                  [row(S5_WIDTH), row(3 * HG_WIDTH), row(HG_WIDTH), row(2 * D_MODEL), row(D_MODEL), row(D_MODEL),
                   _full((1, D_MODEL)), _full((N_IN, D_MODEL))],
                  [row(N_IN), row(D_MODEL), _full((1, D_MODEL))],
                  [_sds((t, N_IN), BF16), _sds((t, D_MODEL)), _sds((1, D_MODEL))],
                  )(dza, dzh, dzg, dzgt, dx1, x, g_mix, w_in)


def _tie(*arrays):
    return jnp.zeros((SUBLANES, 128), F32) + sum(a.reshape(-1)[0].astype(F32) for a in arrays)


def _after(value, token):
    return value + token[0, 0]


def _local_step(x3, tgt3, weights, sp, emit, emit_small):
    nb, seq, _ = x3.shape
    t = nb * seq
    tm = _token_tile(seq)
    x = x3.reshape(t, D_MODEL)
    tgt = tgt3.reshape(t, D_MODEL)
    row = lambda v: v.reshape(1, -1)

    a_re, a_im, b_re, b_im = sp["s5_a_re"], sp["s5_a_im"], sp["s5_b_re"], sp["s5_b_im"]
    ldt = sp["s5_log_dt"].reshape(S5_GROUPS, 1)
    lr, li, bb_re, bb_im, lb = _params_fwd(a_re, a_im, ldt, b_re, b_im, sp["hg_lb_logits"])
    lam = jnp.concatenate([lr.reshape(1, S5_N), li.reshape(1, S5_N)], axis=0)
    swap = lambda m: m.transpose(0, 2, 1)
    b_to_st = (_band_blocks(bb_re), _band_blocks(bb_im))
    b_to_ch = (_band_blocks(swap(bb_re)), _band_blocks(swap(bb_im)))
    c_to_ch = (_band_blocks(swap(sp["s5_c_re"])), _band_blocks(swap(-sp["s5_c_im"])))
    c_to_st = (_band_blocks(sp["s5_c_re"]), _band_blocks(-sp["s5_c_im"]))

    g_mix, g_ffn, g_final = row(sp["g_mix"]), row(sp["g_ffn"]), row(sp["g_final"])
    b_glu, gain, dskip, b_conv = row(sp["b_glu"]), row(sp["hg_norm_gain"]), row(sp["s5_d"]), row(sp["b_conv"])

    w_in = weights("in", lam, *b_to_st, *b_to_ch, *c_to_ch, *c_to_st)["w_in"]
    u, za, zh, zgt = _in_proj(x, g_mix, w_in, tm)
    seqs = lambda v: v.reshape(nb, seq, v.shape[-1])
    toks = lambda v: v.reshape(t, v.shape[-1])
    xs3, y0 = _s5_fwd(seqs(za), b_to_st, lam, c_to_ch, dskip, nb, seq, tm)
    xs, y0 = toks(xs3), toks(y0)
    o3, sts = _hgrn_fwd(zh.reshape(nb, seq, 4 * HG_WIDTH), lb, nb, seq)
    o = o3.reshape(t, HG_WIDTH)
    wm = weights("mix", y0, o3)
    x1, u2, pa, pb, ya2, yb = _mix_fwd(x, y0, o, zh, zgt, wm["w_glu"], b_glu, gain, wm["w_pa"], wm["w_pb"],
                                       wm["w_out"], g_ffn, tm)
    wf = weights("ffn", u2)
    h = _ffn_up(u2, wf["w_up"], min(4 * tm, t))
    hc, a, dx2, dx2b, loss, dg_final = _ffn_down_loss(h, x1, tgt, wf["w_conv"], b_conv, wf["w_down"], g_final,
                                                      seq, tm)

    wgrad = functools.partial(_wgrad, tn=256, out_dtype=BF16)
    dhc, db_conv = _ffn_bwd_act(dx2b, hc, wf["w_down"], tm)
    sent = emit({"w_down": wgrad(a, dx2b, "dw_down")})
    dh, dx1, dx1b, dg_ffn, dw_conv = _ffn_bwd_up(dhc, h, dx2, x1, wf["w_conv"], wf["w_up"], _after(g_ffn, sent),
                                                 seq, tm)
    sent = emit({"w_up": wgrad(dh, u2, "dw_up"), "w_conv": dw_conv})
    (dy0, do, dzg, dzgt, m, dpa, dpb, ya1, dpre, db_glu, dgain) = _mix_bwd(
        dx1b, y0, o, zh, zgt, pa, pb, wm["w_glu"], _after(b_glu, sent), gain, wm["w_pa"], wm["w_pb"], wm["w_out"], tm)
    sent = emit({"w_out": wgrad(m, dx1b, "dw_out"), "w_pa": wgrad(ya2, dpa, "dw_pa"),
                 "w_pb": wgrad(yb, dpb, "dw_pb"), "w_glu": wgrad(ya1, dpre, "dw_glu")})
    dzh3, dlb = _hgrn_bwd(zh.reshape(nb, seq, 4 * HG_WIDTH), do.reshape(nb, seq, HG_WIDTH), sts, _after(lb, sent),
                          nb, seq)
    dza, a_s5, dlam, dd = _s5_bwd(seqs(dy0), seqs(za), xs3, c_to_st, b_to_ch, lam, dskip, nb, seq, tm)
    dza, a_s5 = toks(dza), toks(a_s5)
    dz, dx, dg_mix = _in_proj_bwd(dza, dzh3.reshape(t, 3 * HG_WIDTH), dzg, dzgt, dx1, x, g_mix, w_in, tm)
    sent = emit({"w_in": wgrad(dz, u, "dw_in")})

    band = HG_HEAD
    dbb_band = _wgrad(a_s5, za, "dbb_s5", 512, band=band, after=sent)
    dc_band = _wgrad(xs, dy0, "dc_s5", 512, band=band, after=sent)
    dbb_re = swap(_diag_blocks(dbb_band[:S5_N], S5_STATE, S5_GROUP))
    dbb_im = swap(_diag_blocks(dbb_band[S5_N:], S5_STATE, S5_GROUP))
    dc_re = swap(_diag_blocks(dc_band[:S5_N], S5_STATE, S5_GROUP))
    dc_im = -swap(_diag_blocks(dc_band[S5_N:], S5_STATE, S5_GROUP))
    da_re, da_im, dldt, db_re, db_im, dlogits = _params_bwd(
        a_re, a_im, ldt, b_re, b_im, sp["hg_lb_logits"],
        dlam[0].reshape(S5_GROUPS, S5_STATE), dlam[1].reshape(S5_GROUPS, S5_STATE), dbb_re, dbb_im, dlb)
    emit_small({"g_mix": dg_mix, "s5_a_re": da_re, "s5_a_im": da_im, "s5_log_dt": dldt.reshape(1, S5_GROUPS),
                "s5_b_re": db_re, "s5_b_im": db_im, "s5_c_re": dc_re, "s5_c_im": dc_im, "s5_d": dd, "b_glu": db_glu,
                "hg_lb_logits": dlogits, "hg_norm_gain": dgain, "g_ffn": dg_ffn, "b_conv": db_conv,
                "g_final": dg_final, "loss": loss})
    return dx.reshape(nb, seq, D_MODEL)


def _mesh_peers():
    x, y, c = lax.axis_index("x"), lax.axis_index("y"), lax.axis_index("c")
    peers = []
    for k in range(1, N_DEV):
        px, py, pc = (1 - x if k & 4 else x), (1 - y if k & 2 else y), (1 - c if k & 1 else c)
        peers.append((k, (px, py, pc), 4 * px + 2 * py + pc))
    return 4 * x + 2 * y + c, peers


_HBM = pl.BlockSpec(memory_space=pltpu.HBM)
_SEM = pl.BlockSpec(memory_space=pltpu.SEMAPHORE)


def _exchange_start(name, operands, after, place_own=True):
    n = len(operands)
    me = 4 * lax.axis_index("x") + 2 * lax.axis_index("y") + lax.axis_index("c")
    flags = [per_peer for _, per_peer in operands]
    srcs, lands = [], []
    for arr, per_peer in operands:
        land = lax.empty((N_DEV,) + (arr.shape[1:] if per_peer else arr.shape), arr.dtype)
        if place_own:
            own = lax.dynamic_index_in_dim(arr, me, 0, keepdims=True) if per_peer else arr[None]
            land = lax.dynamic_update_slice_in_dim(land, own, me, 0)
        srcs.append(pltpu.with_memory_space_constraint(arr, pltpu.HBM))
        lands.append(pltpu.with_memory_space_constraint(land, pltpu.HBM))
    copies = (N_DEV - 1) * n

    def body(*refs):
        src_refs, land_refs = refs[:n], refs[n:2 * n]
        send_sems, recv_sems = refs[2 * n + 1], refs[2 * n + 2]
        token = refs[-1]
        my_slab, peers = _mesh_peers()
        for k, peer, slab in peers:
            for i in range(n):
                s = (k - 1) * n + i
                pltpu.make_async_remote_copy(
                    src_ref=src_refs[i].at[slab] if flags[i] else src_refs[i], dst_ref=land_refs[i].at[my_slab],
                    send_sem=send_sems.at[s], recv_sem=recv_sems.at[s], device_id=peer,
                    device_id_type=pl.DeviceIdType.MESH).start()
        token[...] = jnp.zeros_like(token)

    outs = pl.pallas_call(
        body, name=name,
        out_shape=(pltpu.SemaphoreType.DMA((copies,)), pltpu.SemaphoreType.DMA((copies,)),
                   *[pltpu.HBM(a.shape, a.dtype) for a in srcs], *[pltpu.HBM(a.shape, a.dtype) for a in lands],
                   _sds((SUBLANES, 128))),
        in_specs=[_HBM] * (2 * n) + [pl.BlockSpec(memory_space=pl.ANY)],
        out_specs=(_SEM, _SEM, *[_HBM] * (2 * n), pl.BlockSpec(memory_space=pltpu.VMEM)),
        input_output_aliases={i: 2 + i for i in range(2 * n)},
        compiler_params=pltpu.CompilerParams(has_side_effects=pltpu.SideEffectType.DATAFLOW_SIDE_EFFECTING),
    )(*srcs, *lands, after)
    state = (flags, outs[0], outs[1], outs[2:2 + n], outs[2 + n:2 + 2 * n])
    return state, outs[-1]


def _exchange_wait(name, state, *after):
    flags, send_sems, recv_sems, srcs, lands = state
    n = len(flags)

    def body(*refs):
        src_refs, land_refs = refs[:n], refs[n:2 * n]
        send_ref, recv_ref = refs[2 * n], refs[2 * n + 1]
        _, peers = _mesh_peers()
        for k, peer, slab in peers:
            for i in range(n):
                s = (k - 1) * n + i
                copy = pltpu.make_async_remote_copy(
                    src_ref=src_refs[i].at[slab] if flags[i] else src_refs[i], dst_ref=land_refs[i].at[slab],
                    send_sem=send_ref.at[s], recv_sem=recv_ref.at[s], device_id=peer,
                    device_id_type=pl.DeviceIdType.MESH)
                copy.wait_send()
                copy.wait_recv()

    outs = pl.pallas_call(
        body, name=name,
        out_shape=(*[pltpu.HBM(a.shape, a.dtype) for a in srcs], *[pltpu.HBM(a.shape, a.dtype) for a in lands]),
        in_specs=[_HBM] * (2 * n) + [_SEM, _SEM] + [pl.BlockSpec(memory_space=pl.ANY)] * len(after),
        out_specs=tuple([_HBM] * (2 * n)),
        input_output_aliases={i: i for i in range(2 * n)},
        compiler_params=pltpu.CompilerParams(has_side_effects=pltpu.SideEffectType.DATAFLOW_SIDE_EFFECTING),
    )(*srcs, *lands, send_sems, recv_sems, *after)
    return list(outs[n:]), list(outs[:n])


def _join_cols(parts, name, tr):
    _, r, c = parts.shape

    def body(p_ref, o_ref):
        for j in range(N_DEV):
            o_ref[:, j * c:(j + 1) * c] = p_ref[j]

    return _pcall(body, name, (r // tr,), [pl.BlockSpec((N_DEV, tr, c), lambda i: (0, i, 0))],
                  pl.BlockSpec((tr, N_DEV * c), lambda i: (i, 0)), _sds((r, N_DEV * c), parts.dtype))(parts)


def _split_cols(full, name, tr):
    r, c = full.shape[0], full.shape[1] // N_DEV

    def body(f_ref, o_ref):
        for j in range(N_DEV):
            o_ref[j] = f_ref[:, j * c:(j + 1) * c]

    return _pcall(body, name, (r // tr,), [pl.BlockSpec((tr, N_DEV * c), lambda i: (i, 0))],
                  pl.BlockSpec((N_DEV, tr, c), lambda i: (0, i, 0)), _sds((N_DEV, r, c), full.dtype))(full)


def _my_slab():
    return (4 * lax.axis_index("x") + 2 * lax.axis_index("y") + lax.axis_index("c")).astype(jnp.int32).reshape(1)


def _adamw(parts, sent, w, m, v, name, tile):
    _, rows, cols = w.shape

    def body(me_ref, p_ref, s_ref, w_ref, m_ref, v_ref, g_out, d_out, m_out, v_out):
        me = me_ref[0]
        g = jnp.where(me == 0, s_ref[0], p_ref[0]).astype(F32)
        for k in range(1, N_DEV):
            g = g + jnp.where(me == k, s_ref[0], p_ref[k]).astype(F32)
        m1 = ADAM_B1 * m_ref[0] + (1.0 - ADAM_B1) * g
        v1 = ADAM_B2 * v_ref[0] + (1.0 - ADAM_B2) * (g * g)
        m_hat = m1 / (1.0 - ADAM_B1 ** ADAM_STEP)
        v_hat = v1 / (1.0 - ADAM_B2 ** ADAM_STEP)
        g_out[0] = g
        d_out[0] = -ADAM_LR * (m_hat / (jnp.sqrt(v_hat) + ADAM_EPS) + ADAM_WD * w_ref[0])
        m_out[0] = m1
        v_out[0] = v1

    row = pl.BlockSpec((1, tile, cols), lambda i, me: (0, i, 0))
    return pl.pallas_call(
        body, name=name, out_shape=[_sds((1, rows, cols))] * 4,
        grid_spec=pltpu.PrefetchScalarGridSpec(
            num_scalar_prefetch=1, grid=(rows // tile,),
            in_specs=[pl.BlockSpec((N_DEV, tile, cols), lambda i, me: (0, i, 0)),
                      pl.BlockSpec((1, tile, cols), lambda i, me: (me[0], i, 0)), row, row, row],
            out_specs=[row, row, row, row]),
        compiler_params=pltpu.CompilerParams(dimension_semantics=("arbitrary",), vmem_limit_bytes=VMEM_LIMIT),
    )(_my_slab(), parts, sent, w, m, v)


BIG = {
    "w_in": ((N_IN // N_DEV, D_MODEL), False, N_IN // N_DEV // 3),
    "w_glu": ((S5_WIDTH // N_DEV, S5_WIDTH), False, S5_WIDTH // N_DEV),
    "w_pa": ((S5_WIDTH, D_MODEL // N_DEV), True, S5_WIDTH),
    "w_pb": ((HG_WIDTH, D_MODEL // N_DEV), True, HG_WIDTH),
    "w_out": ((D_MODEL // N_DEV, D_MODEL), False, D_MODEL // N_DEV),
    "w_up": ((2 * D_FF // N_DEV, D_MODEL), False, 2 * D_FF // N_DEV // 4),
    "w_conv": ((CONV_W, 2 * D_FF // N_DEV), True, CONV_W),
    "w_down": ((D_FF // N_DEV, D_MODEL), False, D_FF // N_DEV // 2),
}
TRANSPOSED = ("w_in", "w_up", "s5_b_re", "s5_b_im")
UNALIGNED_COLS = ("w_conv",)


def _stored(n, arr):
    return jnp.swapaxes(arr, -1, -2) if n in TRANSPOSED else arr


def _join_shards(n, parts):
    (a, b), by_cols, _ = BIG[n]
    if not by_cols:
        return parts.reshape(N_DEV * a, b)
    if n in UNALIGNED_COLS:
        return _join_cols(parts, "join_" + n, min(a, 256))
    return parts.transpose(1, 0, 2).reshape(a, N_DEV * b)


def _split_shards(n, full):
    (a, b), by_cols, _ = BIG[n]
    if not by_cols:
        return full.reshape(N_DEV, a, b)
    if n in UNALIGNED_COLS:
        return _split_cols(full, "split_" + n, min(a, 256))
    return full.reshape(a, N_DEV, b).transpose(1, 0, 2)


SMALL_CORE = {
    "s5_b_re": GSC, "s5_b_im": GSC, "s5_c_re": GSC, "s5_c_im": GSC,
    "g_mix": (1, D_MODEL), "g_ffn": (1, D_MODEL), "g_final": (1, D_MODEL), "s5_d": (1, S5_WIDTH),
    "b_glu": (1, S5_WIDTH), "hg_norm_gain": (1, HG_WIDTH), "hg_lb_logits": (2, HG_WIDTH), "b_conv": (1, 2 * D_FF),
    "s5_log_dt": (1, S5_GROUPS), "s5_a_re": (S5_GROUPS, S5_STATE), "s5_a_im": (S5_GROUPS, S5_STATE), "loss": (1, 1),
}
BLOCK_ROWS = 32


def _small_rows():
    rows, r = {}, 0
    for n, core in SMALL_CORE.items():
        rows[n] = r
        r += BLOCK_ROWS if len(core) == 3 else -(-math.prod(core) // PACK_W)
    return rows, -(-r // SUBLANES) * SUBLANES


SMALL_ROW, SMALL_ROWS = _small_rows()


def _small_pieces(name):
    r, core = SMALL_ROW[name], SMALL_CORE[name]
    if len(core) == 3:
        return [((g, slice(None), slice(None)), slice(r + S5_GROUP * (g % 2), r + S5_GROUP * (g % 2 + 1)),
                 slice(S5_STATE * (g // 2), S5_STATE * (g // 2 + 1))) for g in range(S5_GROUPS)]
    pieces = []
    for i in range(core[0]):
        for c0 in range(0, core[1], PACK_W):
            w, flat = min(PACK_W, core[1] - c0), i * core[1] + c0
            pieces.append(((slice(i, i + 1), slice(c0, c0 + w)), slice(r + flat // PACK_W, r + flat // PACK_W + 1),
                           slice(flat % PACK_W, flat % PACK_W + w)))
    return pieces


def _core_index(ref, name, idx):
    return (0,) * (len(ref.shape) - len(SMALL_CORE[name])) + idx


def _pack_small_grads(grads):
    names = list(SMALL_CORE)

    def body(*refs):
        pack = refs[-1]
        pack[...] = jnp.zeros_like(pack)
        for ref, n in zip(refs, names):
            for idx, rows, lanes in _small_pieces(n):
                pack[rows, lanes] = ref[_core_index(ref, n, idx)]

    return _pcall(body, "pack_small_grads", (1,), [_full(grads[n].shape) for n in names],
                  _full((SMALL_ROWS, PACK_W)), _sds((SMALL_ROWS, PACK_W)))(*[grads[n] for n in names])


def _adamw_small(parts, sent, names, rows, given, name):
    lo, hi = rows
    k = len(names)
    shapes = [given[n].shape for n in names]

    def body(*refs):
        me, p_ref, s_ref, ins, outs = refs[0][0], refs[1], refs[2], refs[3:3 + 3 * k], refs[3 + 3 * k:3 + 7 * k]
        packs, results = refs[3 + 7 * k:6 + 7 * k], refs[6 + 7 * k:]
        for j, pack in enumerate(packs):
            pack[...] = jnp.zeros_like(pack)
            for ref, n in zip(ins[j * k:(j + 1) * k], names):
                for idx, prow, lanes in _small_pieces(n):
                    pack[slice(prow.start - lo, prow.stop - lo), lanes] = ref[_core_index(ref, n, idx)]
        mine = s_ref[lo:hi, :]
        g = jnp.where(me == 0, mine, p_ref[0, lo:hi, :])
        for d in range(1, N_DEV):
            g = g + jnp.where(me == d, mine, p_ref[d, lo:hi, :])
        m1 = ADAM_B1 * packs[1][...] + (1.0 - ADAM_B1) * g
        v1 = ADAM_B2 * packs[2][...] + (1.0 - ADAM_B2) * (g * g)
        m_hat = m1 / (1.0 - ADAM_B1 ** ADAM_STEP)
        v_hat = v1 / (1.0 - ADAM_B2 ** ADAM_STEP)
        results[0][...] = g
        results[1][...] = -ADAM_LR * (m_hat / (jnp.sqrt(v_hat) + ADAM_EPS) + ADAM_WD * packs[0][...])
        results[2][...] = m1
        results[3][...] = v1
        for j, result in enumerate(results):
            for ref, n in zip(outs[j * k:(j + 1) * k], names):
                for idx, prow, lanes in _small_pieces(n):
                    ref[_core_index(ref, n, idx)] = result[slice(prow.start - lo, prow.stop - lo), lanes]

    flat = _pcall(body, name, (1,),
                  [pl.BlockSpec(memory_space=pltpu.SMEM), _full(parts.shape), _full(sent.shape)]
                  + [_full(s) for s in shapes] * 3,
                  [_full(s) for s in shapes] * 4, [_sds(s) for s in shapes] * 4,
                  scratch=[pltpu.VMEM((hi - lo, PACK_W), F32)] * 7,
                  )(_my_slab(), parts, sent, *[given[pre + n] for pre in ("", "m_", "v_") for n in names])
    return {n: [flat[j * k + i] for j in range(4)] for i, n in enumerate(names)}


def kernel(x, g_mix, w_in, s5_a_re, s5_a_im, s5_log_dt, s5_b_re, s5_b_im, s5_c_re, s5_c_im, s5_d, w_glu, b_glu, hg_lb_logits, hg_norm_gain, w_pa, w_pb, w_out, g_ffn, w_up, w_conv, b_conv, w_down, g_final, loss_target, m_g_mix, m_w_in, m_s5_a_re, m_s5_a_im, m_s5_log_dt, m_s5_b_re, m_s5_b_im, m_s5_c_re, m_s5_c_im, m_s5_d, m_w_glu, m_b_glu, m_hg_lb_logits, m_hg_norm_gain, m_w_pa, m_w_pb, m_w_out, m_g_ffn, m_w_up, m_w_conv, m_b_conv, m_w_down, m_g_final, v_g_mix, v_w_in, v_s5_a_re, v_s5_a_im, v_s5_log_dt, v_s5_b_re, v_s5_b_im, v_s5_c_re, v_s5_c_im, v_s5_d, v_w_glu, v_b_glu, v_hg_lb_logits, v_hg_norm_gain, v_w_pa, v_w_pb, v_w_out, v_g_ffn, v_w_up, v_w_conv, v_b_conv, v_w_down, v_g_final):
    given = dict(locals())
    small_names = [n for n, _ in SMALL]

    pay = {n: given[n][0] if n == "w_conv" else _stored(n, given[n])[0].astype(BF16) for n in BIG}
    groups = {"in": ["w_in"], "mix": ["w_glu", "w_pa", "w_pb", "w_out"], "ffn": ["w_up", "w_down", "w_conv"]}
    gathers, order = {}, pay["w_in"]
    for grp, names in groups.items():
        gathers[grp], order = _exchange_start("gather_" + grp + "_start", [(pay[n], False) for n in names], order)

    def weights(grp, *after):
        if grp == "in":
            after = (*after, order)
        got, _ = _exchange_wait("gather_" + grp + "_wait", gathers[grp], *after)
        return {n: _join_shards(n, g) for n, g in zip(groups[grp], got)}

    in_flight, started = [], []

    def emit(grads):
        names = list(grads)
        state, token = _exchange_start("grads_" + names[0] + "_start",
                                       [(_split_shards(n, grads[n]), True) for n in names], grads[names[0]],
                                       place_own=False)
        in_flight.append((names, state))
        return token

    def emit_small(grads):
        pack = _pack_small_grads(grads)
        state, token = _exchange_start("grads_small_start", [(pack, False)], pack, place_own=False)
        in_flight.append((["small"], state))
        started.append(token)

    sp = {n: (given[n] if n in ("g_final", "hg_lb_logits") else _stored(n, given[n])[0]) for n in small_names}
    sp["g_mix"] = _after(sp["g_mix"], order)
    dx = _local_step(x, loss_target, weights, sp, emit, emit_small)

    res = {}
    after = [started[-1]]
    in_flight.insert(-1, in_flight.pop())
    for names, state in in_flight:
        parts, sent = _exchange_wait("grads_" + names[0] + "_wait", state, *after)
        if names != ["small"]:
            after = []
            for n, part, mine in zip(names, parts, sent):
                raw = _adamw(part, mine, *[_stored(n, given[pre + n]) for pre in ("", "m_", "v_")], "adamw_" + n,
                             BIG[n][2])
                res[n] = [_stored(n, r) for r in raw]
                after.append(raw[0])
            continue
        sgiven = {pre + n: _stored(n, given[pre + n]) for pre in ("", "m_", "v_") for n in small_names}
        for pre in ("", "m_", "v_"):
            sgiven[pre + "g_final"] = given[pre + "g_final"].reshape(1, D_MODEL)
            sgiven[pre + "loss"] = jnp.zeros((1, 1), F32)
        raw = _adamw_small(parts[0], sent[0], list(SMALL_CORE), (0, SMALL_ROWS), sgiven, "adamw_small")
        res.update({n: [_stored(n, r) for r in raw[n]] for n in small_names})
        res["g_final"] = [r.reshape(D_MODEL) for r in raw["g_final"]]
        total_loss = raw["loss"][0].reshape(())
        after = [raw["s5_b_re"][0], raw["g_mix"][0]]
    return (total_loss, dx, *[res[n][0] for n in WEIGHT_ORDER], *[res[n][1] for n in WEIGHT_ORDER],
            *[res[n][2] for n in WEIGHT_ORDER], *[res[n][3] for n in WEIGHT_ORDER])
```

```python
import functools
import math

import jax
import jax.numpy as jnp
from jax import lax
from jax.experimental import pallas as pl
from jax.experimental.pallas import tpu as pltpu

F32 = jnp.float32
BF16 = jnp.bfloat16

D_MODEL = 1024
S5_WIDTH = 512
S5_GROUP = 16
S5_GROUPS = 32
S5_STATE = 64
S5_N = S5_GROUPS * S5_STATE
HG_WIDTH = 512
HG_HEAD = 128
HG_HEADS = 4
D_FF = 2816
CONV_W = 3
CHUNK = 64
N_IN = S5_WIDTH + 4 * HG_WIDTH + 2 * D_MODEL
EPS = 1e-6
QSCALE = HG_HEAD ** -0.5

ADAM_LR = 0.001
ADAM_B1 = 0.9
ADAM_B2 = 0.999
ADAM_EPS = 1e-08
ADAM_WD = 0.01
ADAM_STEP = 10

N_DEV = 8
V7X_VMEM_BYTES = 64 * 1024 * 1024
VMEM_LIMIT = V7X_VMEM_BYTES * 7 // 8
SUBLANES = 8
PACK_W = 1024

SMALL = (
    ("g_mix", (1, D_MODEL)),
    ("s5_a_re", (1, S5_GROUPS, S5_STATE)),
    ("s5_a_im", (1, S5_GROUPS, S5_STATE)),
    ("s5_log_dt", (1, S5_GROUPS)),
    ("s5_b_re", (1, S5_GROUPS, S5_STATE, S5_GROUP)),
    ("s5_b_im", (1, S5_GROUPS, S5_STATE, S5_GROUP)),
    ("s5_c_re", (1, S5_GROUPS, S5_GROUP, S5_STATE)),
    ("s5_c_im", (1, S5_GROUPS, S5_GROUP, S5_STATE)),
    ("s5_d", (1, S5_WIDTH)),
    ("b_glu", (1, S5_WIDTH)),
    ("hg_lb_logits", (2, HG_WIDTH)),
    ("hg_norm_gain", (1, HG_WIDTH)),
    ("g_ffn", (1, D_MODEL)),
    ("b_conv", (1, 2 * D_FF)),
    ("g_final", (D_MODEL,)),
)
WEIGHT_ORDER = ("g_mix", "w_in", "s5_a_re", "s5_a_im", "s5_log_dt", "s5_b_re", "s5_b_im", "s5_c_re", "s5_c_im",
                "s5_d", "w_glu", "b_glu", "hg_lb_logits", "hg_norm_gain", "w_pa", "w_pb", "w_out", "g_ffn",
                "w_up", "w_conv", "b_conv", "w_down", "g_final")


def _pcall(body, name, grid, in_specs, out_specs, out_shape, scratch=()):
    return pl.pallas_call(
        body, name=name, grid=grid, in_specs=in_specs, out_specs=out_specs, out_shape=out_shape,
        scratch_shapes=list(scratch),
        compiler_params=pltpu.CompilerParams(dimension_semantics=("arbitrary",) * len(grid),
                                             vmem_limit_bytes=VMEM_LIMIT),
    )


def _full(shape):
    return pl.BlockSpec(shape, lambda *_: (0,) * len(shape))


def _sds(shape, dtype=F32):
    return jax.ShapeDtypeStruct(shape, dtype)


def _dot(a, b):
    return jnp.dot(a.astype(BF16), b.astype(BF16), preferred_element_type=F32)


def _dot_nt(a, b):
    return lax.dot_general(a.astype(BF16), b.astype(BF16), (((1,), (1,)), ((), ())), preferred_element_type=F32)


def _dot_tn(a, b):
    return lax.dot_general(a.astype(BF16), b.astype(BF16), (((0,), (0,)), ((), ())), preferred_element_type=F32)


def _hdot(a, b):
    return jnp.dot(a, b, preferred_element_type=F32, precision=lax.Precision.HIGHEST)


def _hdot_tn(a, b):
    return lax.dot_general(a, b, (((0,), (0,)), ((), ())), preferred_element_type=F32,
                           precision=lax.Precision.HIGHEST)


def _sigmoid(x):
    return jax.nn.sigmoid(x)


GELU_C = math.sqrt(2.0 / math.pi)
GELU_A = 0.044715


def _gelu(x):
    return 0.5 * x * (1.0 + jnp.tanh(GELU_C * (x + GELU_A * (x * x * x))))


def _gelu_grad(x):
    t = jnp.tanh(GELU_C * (x + GELU_A * (x * x * x)))
    return 0.5 * (1.0 + t) + 0.5 * x * (1.0 - t * t) * (GELU_C * (1.0 + 3.0 * GELU_A * x * x))


def _cumsum_rows(v, reverse=False):
    n = v.shape[0]
    row = lax.broadcasted_iota(jnp.int32, v.shape, 0)
    s = 1
    while s < n:
        if reverse:
            v = v + jnp.where(row < n - s, pltpu.roll(v, n - s, axis=0), 0.0)
        else:
            v = v + jnp.where(row >= s, pltpu.roll(v, s, axis=0), 0.0)
        s *= 2
    return v


def _token_tile(seq):
    return min(256, seq)


def _s5_coeffs(a_re, a_im, ldt):
    dt = jnp.exp(ldt)
    mag = jnp.exp(a_re * dt)
    ang = a_im * dt
    lb_re = mag * jnp.cos(ang)
    lb_im = mag * jnp.sin(ang)
    den = a_re * a_re + a_im * a_im
    n_re = lb_re - 1.0
    n_im = lb_im
    co_re = (n_re * a_re + n_im * a_im) / den
    co_im = (n_im * a_re - n_re * a_im) / den
    return lb_re, lb_im, co_re, co_im


GS, GSC = (S5_GROUPS, S5_STATE), (S5_GROUPS, S5_GROUP, S5_STATE)


def _params_fwd(a_re, a_im, ldt, bt_re, bt_im, logits):
    def body(are, aim, ld, bre, bim, lg, lr_o, li_o, bbr_o, bbi_o, lb_o):
        lr, li, co_re, co_im = _s5_coeffs(are[...], aim[...], ld[...])
        lr_o[...] = lr
        li_o[...] = li
        for g in range(S5_GROUPS):
            cr, ci = co_re[g:g + 1, :], co_im[g:g + 1, :]
            bbr_o[g] = cr * bre[g] - ci * bim[g]
            bbi_o[g] = cr * bim[g] + ci * bre[g]
        lb_o[...] = _sigmoid(lg[0:1, :] - lg[1:2, :])

    return _pcall(body, "params_fwd", (1,),
                  [_full(GS), _full(GS), _full((S5_GROUPS, 1)), _full(GSC), _full(GSC), _full((2, HG_WIDTH))],
                  [_full(GS), _full(GS), _full(GSC), _full(GSC), _full((1, HG_WIDTH))],
                  [_sds(GS), _sds(GS), _sds(GSC), _sds(GSC), _sds((1, HG_WIDTH))],
                  )(a_re, a_im, ldt, bt_re, bt_im, logits)


def _params_bwd(a_re, a_im, ldt, bt_re, bt_im, logits, dlr, dli, dbbr, dbbi, dlb):
    def body(are, aim, ld, bre, bim, lg, dlr_r, dli_r, dbbr_r, dbbi_r, dlb_r,
             dare_o, daim_o, dld_o, dbre_o, dbim_o, dlg_o, dcr_ref, dci_ref):
        (_, _, co_re, co_im), vjp = jax.vjp(_s5_coeffs, are[...], aim[...], ld[...])
        for g in range(S5_GROUPS):
            cr, ci = co_re[g:g + 1, :], co_im[g:g + 1, :]
            gr, gi, br, bi = dbbr_r[g], dbbi_r[g], bre[g], bim[g]
            dbre_o[g] = cr * gr + ci * gi
            dbim_o[g] = cr * gi - ci * gr
            dcr_ref[g:g + 1, :] = jnp.sum(gr * br + gi * bi, axis=0, keepdims=True)
            dci_ref[g:g + 1, :] = jnp.sum(gi * br - gr * bi, axis=0, keepdims=True)
        dare, daim, dld = vjp((dlr_r[...], dli_r[...], dcr_ref[...], dci_ref[...]))
        dare_o[...] = dare
        daim_o[...] = daim
        dld_o[...] = dld
        lb = _sigmoid(lg[0:1, :] - lg[1:2, :])
        d0 = dlb_r[...] * lb * (1.0 - lb)
        dlg_o[0:1, :] = d0
        dlg_o[1:2, :] = -d0

    return _pcall(body, "params_bwd", (1,),
                  [_full(GS), _full(GS), _full((S5_GROUPS, 1)), _full(GSC), _full(GSC), _full((2, HG_WIDTH)),
                   _full(GS), _full(GS), _full(GSC), _full(GSC), _full((1, HG_WIDTH))],
                  [_full(GS), _full(GS), _full((S5_GROUPS, 1)), _full(GSC), _full(GSC), _full((2, HG_WIDTH))],
                  [_sds(GS), _sds(GS), _sds((S5_GROUPS, 1)), _sds(GSC), _sds(GSC), _sds((2, HG_WIDTH))],
                  scratch=[pltpu.VMEM(GS, F32), pltpu.VMEM(GS, F32)],
                  )(a_re, a_im, ldt, bt_re, bt_im, logits, dlr, dli, dbbr, dbbi, dlb)


def _band_blocks(m):
    g, r, c = m.shape
    gb = g // S5_BANDS
    m4 = m.astype(BF16).reshape(S5_BANDS, gb, r, c)
    on_diag = jnp.eye(gb, dtype=bool)[None, :, None, :, None]
    return jnp.where(on_diag, m4[:, :, :, None, :], 0).reshape(S5_BANDS, gb * r, gb * c)


def _diag_blocks(band, r, c):
    g, nb = band.shape[0] // r, band.shape[1] // c
    on_diag = (jnp.arange(g) % nb)[:, None, None, None] == jnp.arange(nb)[None, None, :, None]
    return jnp.sum(jnp.where(on_diag, band.reshape(g, r, nb, c), 0.0), axis=2)


def _in_proj(x, g_mix, w_in, tm):
    t = x.shape[0]

    def body(x_ref, g_ref, w_ref, u_ref, za_ref, zh_ref, zg_ref):
        xv = x_ref[...]
        r = lax.rsqrt(jnp.mean(xv * xv, axis=-1, keepdims=True) + EPS)
        u = (xv * r * g_ref[...]).astype(BF16)
        u_ref[...] = u
        za_ref[...] = _dot_nt(u, w_ref[0:S5_WIDTH, :])
        zh_ref[...] = _dot_nt(u, w_ref[S5_WIDTH:S5_WIDTH + 4 * HG_WIDTH, :])
        zg_ref[...] = _dot_nt(u, w_ref[S5_WIDTH + 4 * HG_WIDTH:, :]).astype(BF16)

    row = lambda w: pl.BlockSpec((tm, w), lambda i: (i, 0))
    return _pcall(body, "in_proj", (t // tm,),
                  [row(D_MODEL), _full((1, D_MODEL)), _full((N_IN, D_MODEL))],
                  [row(D_MODEL), row(S5_WIDTH), row(4 * HG_WIDTH), row(2 * D_MODEL)],
                  [_sds((t, D_MODEL), BF16), _sds((t, S5_WIDTH)), _sds((t, 4 * HG_WIDTH)),
                   _sds((t, 2 * D_MODEL), BF16)],
                  )(x, g_mix, w_in)


S5_LANES = 512
S5_BANDS = 4


def _band(q):
    return (slice(q * S5_WIDTH // S5_BANDS, (q + 1) * S5_WIDTH // S5_BANDS),
            slice(q * S5_N // S5_BANDS, (q + 1) * S5_N // S5_BANDS))


def _im(st):
    return slice(S5_N + st.start, S5_N + st.stop)


SCAN_UNROLL = 8


def _complex_scan(buf_ref, lam_ref, st_ref, nb, ts, reverse):
    lanes = [slice(cc * S5_LANES, (cc + 1) * S5_LANES) for cc in range(S5_N // S5_LANES)]
    chains = [(b, re) for b in range(nb) for re in lanes]
    nch = len(chains)
    wr = {re.start: lam_ref[0:1, re] for re in lanes}
    wi = {re.start: -lam_ref[1:2, re] if reverse else lam_ref[1:2, re] for re in lanes}

    def block(ib, carry):
        vr, vi = list(carry[:nch]), list(carry[nch:])
        first = ts - SCAN_UNROLL - ib * SCAN_UNROLL if reverse else ib * SCAN_UNROLL
        first = pl.multiple_of(first, SCAN_UNROLL)
        for k in range(SCAN_UNROLL):
            row = pl.ds(first + (SCAN_UNROLL - 1 - k if reverse else k), 1)
            for c, (b, re) in enumerate(chains):
                nr = wr[re.start] * vr[c] - wi[re.start] * vi[c] + buf_ref[b, row, re]
                ni = wr[re.start] * vi[c] + wi[re.start] * vr[c] + buf_ref[b, row, _im(re)]
                buf_ref[b, row, re] = nr
                buf_ref[b, row, _im(re)] = ni
                vr[c], vi[c] = nr, ni
        return tuple(vr + vi)

    init = tuple(st_ref[b, 0:1, re] for b, re in chains) + tuple(st_ref[b, 1:2, re] for b, re in chains)
    last = lax.fori_loop(0, ts // SCAN_UNROLL, block, init)
    for c, (b, re) in enumerate(chains):
        st_ref[b, 0:1, re] = last[c]
        st_ref[b, 1:2, re] = last[nch + c]


BAND_CH = S5_WIDTH // S5_BANDS
BAND_ST = S5_N // S5_BANDS


def _s5_fwd(za, b_bands, lam, c_bands, dskip, nb, seq, ts):
    nts = seq // ts

    def body(za_ref, br_ref, bi_ref, lam_ref, cr_ref, ci_ref, d_ref, xs_ref, y_ref, buf_ref, st_ref):
        @pl.when(pl.program_id(0) == 0)
        def _():
            st_ref[...] = jnp.zeros_like(st_ref)

        for b in range(nb):
            zav = za_ref[b]
            for q in range(S5_BANDS):
                ch, st = _band(q)
                buf_ref[b, :, st] = _dot(zav[:, ch], br_ref[q])
                buf_ref[b, :, _im(st)] = _dot(zav[:, ch], bi_ref[q])
        _complex_scan(buf_ref, lam_ref, st_ref, nb, ts, reverse=False)
        for b in range(nb):
            zav = za_ref[b]
            xs_ref[b] = buf_ref[b].astype(BF16)
            for q in range(S5_BANDS):
                ch, st = _band(q)
                y_ref[b, :, ch] = (_dot(xs_ref[b, :, st], cr_ref[q]) + _dot(xs_ref[b, :, _im(st)], ci_ref[q])
                                   + d_ref[:, ch] * zav[:, ch])

    tok = lambda w: pl.BlockSpec((nb, ts, w), lambda j: (0, j, 0))
    to_st, to_ch = _full((S5_BANDS, BAND_CH, BAND_ST)), _full((S5_BANDS, BAND_ST, BAND_CH))
    return _pcall(body, "s5_fwd", (nts,),
                  [tok(S5_WIDTH), to_st, to_st, _full((2, S5_N)), to_ch, to_ch, _full((1, S5_WIDTH))],
                  [tok(2 * S5_N), tok(S5_WIDTH)],
                  [_sds((nb, seq, 2 * S5_N), BF16), _sds((nb, seq, S5_WIDTH))],
                  scratch=[pltpu.VMEM((nb, ts, 2 * S5_N), F32), pltpu.VMEM((nb, 2, S5_N), F32)],
                  )(za, *b_bands, lam, *c_bands, dskip)


def _hgrn_gates(zq, zf, lbh):
    sf = _sigmoid(zf)
    f = lbh + (1.0 - lbh) * sf
    sq = _sigmoid(zq)
    qa = zq * sq * QSCALE
    bc = _cumsum_rows(jnp.log(f))
    bm = bc[CHUNK // 2 - 1:CHUNK // 2, :]
    bl = bc[CHUNK - 1:CHUNK, :]
    return sf, f, sq, qa, bc, bm, bl


def _hgrn_fwd(zh, lb, nb, seq):
    nc = seq // CHUNK

    def body(zh_ref, lb_ref, o_ref, sts_ref, st_ref):
        @pl.when(pl.program_id(0) == 0)
        def _():
            st_ref[...] = jnp.zeros_like(st_ref)

        causal = (lax.broadcasted_iota(jnp.int32, (CHUNK, CHUNK), 0)
                  >= lax.broadcasted_iota(jnp.int32, (CHUNK, CHUNK), 1))
        for b in range(nb):
            for h in range(HG_HEADS):
                hs = slice(h * HG_HEAD, (h + 1) * HG_HEAD)
                zq = zh_ref[b, :, h * HG_HEAD:(h + 1) * HG_HEAD]
                zf = zh_ref[b, :, HG_WIDTH + h * HG_HEAD:HG_WIDTH + (h + 1) * HG_HEAD]
                zi = zh_ref[b, :, 2 * HG_WIDTH + h * HG_HEAD:2 * HG_WIDTH + (h + 1) * HG_HEAD]
                _, f, _, qa, bc, bm, bl = _hgrn_gates(zq, zf, lb_ref[:, hs])
                k = 1.0 - f
                qt = qa * jnp.exp(bc - bm)
                kt = k * jnp.exp(bm - bc)
                qb = qa * jnp.exp(bc)
                kd = k * jnp.exp(bl - bc)
                st = st_ref[b, h]
                sts_ref[b, 0, h] = st
                a = jnp.where(causal, _dot_nt(qt, kt), 0.0)
                o_ref[b, :, hs] = _dot(a, zi) + _dot_nt(qb, st)
                st_ref[b, h] = st * jnp.exp(bl) + _dot_tn(zi, kd)

    return _pcall(body, "hgrn_fwd", (nc,),
                  [pl.BlockSpec((nb, CHUNK, 4 * HG_WIDTH), lambda c: (0, c, 0)), _full((1, HG_WIDTH))],
                  [pl.BlockSpec((nb, CHUNK, HG_WIDTH), lambda c: (0, c, 0)),
                   pl.BlockSpec((nb, 1, HG_HEADS, HG_HEAD, HG_HEAD), lambda c: (0, c, 0, 0, 0))],
                  [_sds((nb, seq, HG_WIDTH)), _sds((nb, nc, HG_HEADS, HG_HEAD, HG_HEAD))],
                  scratch=[pltpu.VMEM((nb, HG_HEADS, HG_HEAD, HG_HEAD), F32)])(zh, lb)


def _head_rms(o):
    parts = []
    for h in range(HG_HEADS):
        oh = o[:, h * HG_HEAD:(h + 1) * HG_HEAD]
        r = lax.rsqrt(jnp.mean(oh * oh, axis=-1, keepdims=True) + EPS)
        parts.append(jnp.broadcast_to(r, oh.shape))
    return jnp.concatenate(parts, axis=1)


def _head_mean(v):
    parts = []
    for h in range(HG_HEADS):
        vh = v[:, h * HG_HEAD:(h + 1) * HG_HEAD]
        parts.append(jnp.broadcast_to(jnp.mean(vh, axis=-1, keepdims=True), vh.shape))
    return jnp.concatenate(parts, axis=1)


def _mix_fwd(x, y0, o, zh, zgt, w_glu, b_glu, gain, w_pa, w_pb, w_out, g_ffn, tm):
    t = x.shape[0]

    def body(x_ref, y0_ref, o_ref, zg_ref, zgt_ref, wglu_ref, bglu_ref, gain_ref, wpa_ref, wpb_ref, wout_ref,
             gffn_ref, x1_ref, u2_ref, pa_ref, pb_ref, ya2_ref, yb_ref):
        ya1 = _gelu(y0_ref[...])
        s = _sigmoid(_dot(ya1, wglu_ref[...]) + bglu_ref[...])
        ya2 = (ya1 * s).astype(BF16)
        ov = o_ref[...]
        zg = zg_ref[...]
        yb = (ov * _head_rms(ov) * gain_ref[...] * (zg * _sigmoid(zg))).astype(BF16)
        ya2_ref[...] = ya2
        yb_ref[...] = yb
        pa = jnp.dot(ya2, wpa_ref[...], preferred_element_type=F32)
        pb = jnp.dot(yb, wpb_ref[...], preferred_element_type=F32)
        pa_ref[...] = pa.astype(BF16)
        pb_ref[...] = pb.astype(BF16)
        m = (_sigmoid(zgt_ref[:, 0:D_MODEL].astype(F32)) * pa
             + _sigmoid(zgt_ref[:, D_MODEL:].astype(F32)) * pb)
        x1 = x_ref[...] + _dot(m, wout_ref[...])
        x1_ref[...] = x1
        r = lax.rsqrt(jnp.mean(x1 * x1, axis=-1, keepdims=True) + EPS)
        u2_ref[...] = (x1 * r * gffn_ref[...]).astype(BF16)

    row = lambda w: pl.BlockSpec((tm, w), lambda i: (i, 0))
    return _pcall(body, "mix_fwd", (t // tm,),
                  [row(D_MODEL), row(S5_WIDTH), row(HG_WIDTH), pl.BlockSpec((tm, HG_WIDTH), lambda i: (i, 3)),
                   row(2 * D_MODEL), _full((S5_WIDTH, S5_WIDTH)), _full((1, S5_WIDTH)), _full((1, HG_WIDTH)),
                   _full((S5_WIDTH, D_MODEL)), _full((HG_WIDTH, D_MODEL)), _full((D_MODEL, D_MODEL)),
                   _full((1, D_MODEL))],
                  [row(D_MODEL), row(D_MODEL), row(D_MODEL), row(D_MODEL), row(S5_WIDTH), row(HG_WIDTH)],
                  [_sds((t, D_MODEL)), _sds((t, D_MODEL), BF16), _sds((t, D_MODEL), BF16), _sds((t, D_MODEL), BF16),
                   _sds((t, S5_WIDTH), BF16), _sds((t, HG_WIDTH), BF16)],
                  )(x, y0, o, zh, zgt, w_glu, b_glu, gain, w_pa, w_pb, w_out, g_ffn)


FF_COLS = 256
FF_UP_TILE = 1408


def _ffn_up(u2, w_up, tm):
    t = u2.shape[0]
    n = 2 * D_FF

    def body(u_ref, w_ref, h_ref):
        h_ref[...] = _dot_nt(u_ref[...], w_ref[...]).astype(BF16)

    return _pcall(body, "ffn_up", (n // FF_UP_TILE, t // tm),
                  [pl.BlockSpec((tm, D_MODEL), lambda j, i: (i, 0)),
                   pl.BlockSpec((FF_UP_TILE, D_MODEL), lambda j, i: (j, 0))],
                  pl.BlockSpec((tm, FF_UP_TILE), lambda j, i: (i, j)),
                  _sds((t, n), BF16))(u2, w_up)


HALO = 16


def _shift_matrix(tm):
    r = lax.broadcasted_iota(jnp.int32, (tm, tm), 0)
    c = lax.broadcasted_iota(jnp.int32, (tm, tm), 1)
    return jnp.where(r == c + 1, 1.0, 0.0).astype(BF16)


def _conv_cols(h_ref, halo_ref, valid, wc_ref, bc_ref, c0):
    cs = slice(c0, c0 + FF_COLS)
    cur = h_ref[:, cs].astype(F32)
    prev = jnp.where(valid, halo_ref[:, cs].astype(F32), 0.0)
    full = jnp.concatenate([prev, cur], axis=0)
    h1 = pltpu.roll(full, 1, axis=0)[HALO:]
    h2 = pltpu.roll(full, 2, axis=0)[HALO:]
    return h2 * wc_ref[0:1, cs] + h1 * wc_ref[1:2, cs] + cur * wc_ref[2:3, cs] + bc_ref[:, cs]


def _ffn_down_loss(h, x1, tgt, w_conv, b_conv, w_down, g_final, seq, tm):
    t = h.shape[0]
    tps = seq // tm
    n = 2 * D_FF

    def body(h_ref, halo_ref, x1_ref, tgt_ref, wc_ref, bc_ref, wd_ref, gf_ref,
             hc_ref, a_ref, dx2_ref, dx2b_ref, loss_ref, dgf_ref):
        i = pl.program_id(0)

        @pl.when(i == 0)
        def _():
            loss_ref[...] = jnp.zeros_like(loss_ref)
            dgf_ref[...] = jnp.zeros_like(dgf_ref)

        valid = (i % tps) != 0
        x2 = x1_ref[...]
        for j in range(D_FF // FF_COLS):
            gate = _conv_cols(h_ref, halo_ref, valid, wc_ref, bc_ref, j * FF_COLS)
            val = _conv_cols(h_ref, halo_ref, valid, wc_ref, bc_ref, D_FF + j * FF_COLS)
            hc_ref[:, j * FF_COLS:(j + 1) * FF_COLS] = gate.astype(BF16)
            hc_ref[:, D_FF + j * FF_COLS:D_FF + (j + 1) * FF_COLS] = val.astype(BF16)
            a = (gate * _sigmoid(gate) * val).astype(BF16)
            a_ref[:, j * FF_COLS:(j + 1) * FF_COLS] = a
            x2 = x2 + jnp.dot(a, wd_ref[j * FF_COLS:(j + 1) * FF_COLS, :], preferred_element_type=F32)
        r = lax.rsqrt(jnp.mean(x2 * x2, axis=-1, keepdims=True) + EPS)
        xn = x2 * r
        g = gf_ref[...]
        e = xn * g - tgt_ref[...]
        loss_ref[...] += (0.5 / D_MODEL) * jnp.sum(e * e).reshape(1, 1)
        dy = e * (1.0 / D_MODEL)
        dgf_ref[...] += jnp.sum(dy * xn, axis=0, keepdims=True)
        dxn = dy * g
        dx2 = r * (dxn - xn * jnp.mean(dxn * xn, axis=-1, keepdims=True))
        dx2_ref[...] = dx2
        dx2b_ref[...] = dx2.astype(BF16)

    row = lambda w: pl.BlockSpec((tm, w), lambda i: (i, 0))
    halo = pl.BlockSpec((HALO, n), lambda i: (jnp.maximum(i * (tm // HALO) - 1, 0), 0))
    return _pcall(body, "ffn_down_loss", (t // tm,),
                  [row(n), halo, row(D_MODEL), row(D_MODEL), _full((CONV_W, n)), _full((1, n)),
                   _full((D_FF, D_MODEL)), _full((1, D_MODEL))],
                  [row(n), row(D_FF), row(D_MODEL), row(D_MODEL), _full((1, 1)), _full((1, D_MODEL))],
                  [_sds((t, n), BF16), _sds((t, D_FF), BF16), _sds((t, D_MODEL)), _sds((t, D_MODEL), BF16),
                   _sds((1, 1)), _sds((1, D_MODEL))],
                  )(h, h, x1, tgt, w_conv, b_conv, w_down, g_final)


def _wgrad(a, b, name, tn, out_dtype=F32, band=None, after=None):
    t, m = a.shape
    n = b.shape[1] if band is None else band
    nbands = 1 if band is None else b.shape[1] // band
    after = b if after is None else after

    def body(a_ref, b_ref, after_ref, o_ref):
        o_ref[...] = _dot_tn(a_ref[...], b_ref[...]).astype(out_dtype)

    return _pcall(body, name, (m // tn,),
                  [pl.BlockSpec((t, tn), lambda i: (0, i)), pl.BlockSpec((t, n), lambda i: (0, i % nbands)),
                   pl.BlockSpec(memory_space=pl.ANY)],
                  pl.BlockSpec((tn, n), lambda i: (i, 0)), _sds((m, n), out_dtype))(a, b, after)


def _ffn_bwd_act(dx2b, hc, w_down, tm):
    t = hc.shape[0]
    n = 2 * D_FF

    def body(dx2_ref, hc_ref, wd_ref, dhc_ref, dbc_ref):
        @pl.when(pl.program_id(0) == 0)
        def _():
            dbc_ref[...] = jnp.zeros_like(dbc_ref)

        dx2 = dx2_ref[...]
        for j in range(D_FF // FF_COLS):
            gs = slice(j * FF_COLS, (j + 1) * FF_COLS)
            vs = slice(D_FF + j * FF_COLS, D_FF + (j + 1) * FF_COLS)
            gate = hc_ref[:, gs].astype(F32)
            val = hc_ref[:, vs].astype(F32)
            da = _dot_nt(dx2, wd_ref[gs, :])
            sg = _sigmoid(gate)
            dgate = da * val * (sg * (1.0 + gate * (1.0 - sg)))
            dval = da * (gate * sg)
            dhc_ref[:, gs] = dgate.astype(BF16)
            dhc_ref[:, vs] = dval.astype(BF16)
            dbc_ref[:, gs] += jnp.sum(dgate, axis=0, keepdims=True)
            dbc_ref[:, vs] += jnp.sum(dval, axis=0, keepdims=True)

    row = lambda w: pl.BlockSpec((tm, w), lambda i: (i, 0))
    return _pcall(body, "ffn_bwd_act", (t // tm,),
                  [row(D_MODEL), row(n), _full((D_FF, D_MODEL))],
                  [row(n), _full((1, n))],
                  [_sds((t, n), BF16), _sds((1, n))],
                  )(dx2b, hc, w_down)


def _ffn_bwd_up(dhc, h, dx2, x1, w_conv, w_up, g_ffn, seq, tm):
    t = dhc.shape[0]
    tps = seq // tm
    n = 2 * D_FF
    last = t // HALO - 1

    def body(dhc_ref, halo_ref, h_ref, dx2_ref, x1_ref, wc_ref, wu_ref, gf_ref,
             dh_ref, dx1_ref, dx1b_ref, dgf_ref, dwc_ref):
        i = pl.program_id(0)

        @pl.when(i == 0)
        def _():
            dgf_ref[...] = jnp.zeros_like(dgf_ref)
            dwc_ref[...] = jnp.zeros_like(dwc_ref)

        valid = ((i + 1) % tps) != 0
        du2 = jnp.zeros((tm, D_MODEL), F32)
        for j in range(n // FF_COLS):
            cs = slice(j * FF_COLS, (j + 1) * FF_COLS)
            cur = dhc_ref[:, cs].astype(F32)
            nxt = jnp.where(valid, halo_ref[:, cs].astype(F32), 0.0)
            full = jnp.concatenate([cur, nxt], axis=0)
            d1 = pltpu.roll(full, tm + HALO - 1, axis=0)[:tm]
            d2 = pltpu.roll(full, tm + HALO - 2, axis=0)[:tm]
            dh = (cur * wc_ref[2:3, cs] + d1 * wc_ref[1:2, cs] + d2 * wc_ref[0:1, cs]).astype(BF16)
            dh_ref[:, cs] = dh
            du2 = du2 + _dot(dh, wu_ref[cs, :])
            hv = h_ref[:, cs].astype(F32)
            dwc_ref[0:1, cs] += jnp.sum(hv * d2, axis=0, keepdims=True)
            dwc_ref[1:2, cs] += jnp.sum(hv * d1, axis=0, keepdims=True)
            dwc_ref[2:3, cs] += jnp.sum(hv * cur, axis=0, keepdims=True)
        x1 = x1_ref[...]
        r = lax.rsqrt(jnp.mean(x1 * x1, axis=-1, keepdims=True) + EPS)
        xn = x1 * r
        dgf_ref[...] += jnp.sum(du2 * xn, axis=0, keepdims=True)
        dxn = du2 * gf_ref[...]
        dx1 = dx2_ref[...] + r * (dxn - xn * jnp.mean(dxn * xn, axis=-1, keepdims=True))
        dx1_ref[...] = dx1
        dx1b_ref[...] = dx1.astype(BF16)

    row = lambda w: pl.BlockSpec((tm, w), lambda i: (i, 0))
    halo = pl.BlockSpec((HALO, n), lambda i: (jnp.minimum((i + 1) * (tm // HALO), last), 0))
    return _pcall(body, "ffn_bwd_up", (t // tm,),
                  [row(n), halo, row(n), row(D_MODEL), row(D_MODEL), _full((CONV_W, n)), _full((n, D_MODEL)),
                   _full((1, D_MODEL))],
                  [row(n), row(D_MODEL), row(D_MODEL), _full((1, D_MODEL)), _full((CONV_W, n))],
                  [_sds((t, n), BF16), _sds((t, D_MODEL)), _sds((t, D_MODEL), BF16), _sds((1, D_MODEL)),
                   _sds((CONV_W, n))],
                  )(dhc, dhc, h, dx2, x1, w_conv, w_up, g_ffn)


def _mix_bwd(dx1, y0, o, zh, zgt, pa, pb, w_glu, b_glu, gain, w_pa, w_pb, w_out, tm):
    t = dx1.shape[0]

    def body(dx1_ref, y0_ref, o_ref, zg_ref, zgt_ref, pa_ref, pb_ref, wglu_ref, bglu_ref, gain_ref, wpa_ref,
             wpb_ref, wout_ref,
             dy0_ref, do_ref, dzg_ref, dzgt_ref, m_ref, dpa_ref, dpb_ref, ya1_ref, dpre_ref, dbglu_ref, dgain_ref):
        @pl.when(pl.program_id(0) == 0)
        def _():
            dbglu_ref[...] = jnp.zeros_like(dbglu_ref)
            dgain_ref[...] = jnp.zeros_like(dgain_ref)

        dm = _dot_nt(dx1_ref[...], wout_ref[...])
        sga = _sigmoid(zgt_ref[:, 0:D_MODEL].astype(F32))
        sgb = _sigmoid(zgt_ref[:, D_MODEL:].astype(F32))
        pa = pa_ref[...].astype(F32)
        pb = pb_ref[...].astype(F32)
        m_ref[...] = (sga * pa + sgb * pb).astype(BF16)
        dzgt_ref[:, 0:D_MODEL] = (dm * pa * sga * (1.0 - sga)).astype(BF16)
        dzgt_ref[:, D_MODEL:] = (dm * pb * sgb * (1.0 - sgb)).astype(BF16)
        dpa = (dm * sga).astype(BF16)
        dpb = (dm * sgb).astype(BF16)
        dpa_ref[...] = dpa
        dpb_ref[...] = dpb
        dya2 = _dot_nt(dpa, wpa_ref[...])
        dyb = _dot_nt(dpb, wpb_ref[...])
        y0 = y0_ref[...]
        ya1 = _gelu(y0)
        ya1_ref[...] = ya1.astype(BF16)
        s = _sigmoid(_dot(ya1, wglu_ref[...]) + bglu_ref[...])
        dpre = dya2 * ya1 * s * (1.0 - s)
        dpre_ref[...] = dpre.astype(BF16)
        dbglu_ref[...] += jnp.sum(dpre, axis=0, keepdims=True)
        dya1 = dya2 * s + _dot_nt(dpre, wglu_ref[...])
        dy0_ref[...] = dya1 * _gelu_grad(y0)
        ov = o_ref[...]
        zg = zg_ref[...]
        oh = ov * _head_rms(ov)
        on = oh * gain_ref[...]
        sz = _sigmoid(zg)
        dzg_ref[...] = (dyb * on * (sz * (1.0 + zg * (1.0 - sz)))).astype(BF16)
        don = dyb * (zg * sz)
        dgain_ref[...] += jnp.sum(don * oh, axis=0, keepdims=True)
        doh = don * gain_ref[...]
        do_ref[...] = _head_rms(ov) * (doh - oh * _head_mean(doh * oh))

    row = lambda w: pl.BlockSpec((tm, w), lambda i: (i, 0))
    return _pcall(body, "mix_bwd", (t // tm,),
                  [row(D_MODEL), row(S5_WIDTH), row(HG_WIDTH), pl.BlockSpec((tm, HG_WIDTH), lambda i: (i, 3)),
                   row(2 * D_MODEL), row(D_MODEL), row(D_MODEL), _full((S5_WIDTH, S5_WIDTH)), _full((1, S5_WIDTH)),
                   _full((1, HG_WIDTH)), _full((S5_WIDTH, D_MODEL)), _full((HG_WIDTH, D_MODEL)),
                   _full((D_MODEL, D_MODEL))],
                  [row(S5_WIDTH), row(HG_WIDTH), row(HG_WIDTH), row(2 * D_MODEL), row(D_MODEL), row(D_MODEL),
                   row(D_MODEL), row(S5_WIDTH), row(S5_WIDTH), _full((1, S5_WIDTH)), _full((1, HG_WIDTH))],
                  [_sds((t, S5_WIDTH)), _sds((t, HG_WIDTH)), _sds((t, HG_WIDTH), BF16), _sds((t, 2 * D_MODEL), BF16),
                   _sds((t, D_MODEL), BF16), _sds((t, D_MODEL), BF16), _sds((t, D_MODEL), BF16),
                   _sds((t, S5_WIDTH), BF16), _sds((t, S5_WIDTH), BF16), _sds((1, S5_WIDTH)), _sds((1, HG_WIDTH))],
                  )(dx1, y0, o, zh, zgt, pa, pb, w_glu, b_glu, gain, w_pa, w_pb, w_out)


def _s5_bwd(dy0, za, xs, c_bands, b_bands, lam, dskip, nb, seq, ts):
    nts = seq // ts

    def body(dy0_ref, za_ref, xs_ref, halo_ref, cr_ref, ci_ref, br_ref, bi_ref, lam_ref, d_ref,
             dza_ref, a_ref, dlam_ref, dd_ref, acc_ref, st_ref):
        j = pl.program_id(0)

        @pl.when(j == 0)
        def _():
            dlam_ref[...] = jnp.zeros_like(dlam_ref)
            dd_ref[...] = jnp.zeros_like(dd_ref)
            st_ref[...] = jnp.zeros_like(st_ref)

        for b in range(nb):
            dy0 = dy0_ref[b]
            for q in range(S5_BANDS):
                ch, st = _band(q)
                acc_ref[b, :, st] = _dot(dy0[:, ch], cr_ref[q])
                acc_ref[b, :, _im(st)] = _dot(dy0[:, ch], ci_ref[q])
        _complex_scan(acc_ref, lam_ref, st_ref, nb, ts, reverse=True)
        shift = _shift_matrix(ts)
        top = lax.broadcasted_iota(jnp.int32, (SUBLANES, S5_LANES), 0) == 0
        for b in range(nb):
            a_ref[b] = acc_ref[b].astype(BF16)
            first = jnp.where(j == nts - 1, 0.0, halo_ref[b, HALO - 1:HALO, :].astype(F32))

            def shifted(cols):
                xp = jnp.dot(shift, xs_ref[b, :, cols], preferred_element_type=F32)
                return jnp.concatenate([xp[:SUBLANES] + jnp.where(top, first[:, cols], 0.0), xp[SUBLANES:]], axis=0)

            for cc in range(S5_N // S5_LANES):
                re = slice(cc * S5_LANES, (cc + 1) * S5_LANES)
                ar, ai, xr, xi = acc_ref[b, :, re], acc_ref[b, :, _im(re)], shifted(re), shifted(_im(re))
                dlam_ref[0:1, re] += jnp.sum(ar * xr + ai * xi, axis=0, keepdims=True)
                dlam_ref[1:2, re] += jnp.sum(ai * xr - ar * xi, axis=0, keepdims=True)
            dy0 = dy0_ref[b]
            for q in range(S5_BANDS):
                ch, st = _band(q)
                dza_ref[b, :, ch] = (_dot(a_ref[b, :, st], br_ref[q]) + _dot(a_ref[b, :, _im(st)], bi_ref[q])
                                     + d_ref[:, ch] * dy0[:, ch]).astype(BF16)
            dd_ref[...] += jnp.sum(dy0 * za_ref[b], axis=0, keepdims=True)

    tile = lambda j: nts - 1 - j
    tok = lambda w: pl.BlockSpec((nb, ts, w), lambda j: (0, tile(j), 0))
    halo = pl.BlockSpec((nb, HALO, 2 * S5_N), lambda j: (0, jnp.maximum(tile(j) * (ts // HALO) - 1, 0), 0))
    to_st, to_ch = _full((S5_BANDS, BAND_CH, BAND_ST)), _full((S5_BANDS, BAND_ST, BAND_CH))
    return _pcall(body, "s5_bwd", (nts,),
                  [tok(S5_WIDTH), tok(S5_WIDTH), tok(2 * S5_N), halo, to_st, to_st, to_ch, to_ch,
                   _full((2, S5_N)), _full((1, S5_WIDTH))],
                  [tok(S5_WIDTH), tok(2 * S5_N), _full((2, S5_N)), _full((1, S5_WIDTH))],
                  [_sds((nb, seq, S5_WIDTH), BF16), _sds((nb, seq, 2 * S5_N), BF16), _sds((2, S5_N)),
                   _sds((1, S5_WIDTH))],
                  scratch=[pltpu.VMEM((nb, ts, 2 * S5_N), F32), pltpu.VMEM((nb, 2, S5_N), F32)],
                  )(dy0, za, xs, xs, *c_bands, *b_bands, lam, dskip)


def _hgrn_bwd(zh, do, sts, lb, nb, seq):
    nc = seq // CHUNK

    def body(zh_ref, do_ref, sts_ref, lb_ref, dz_ref, dlb_ref, dst_ref):
        @pl.when(pl.program_id(0) == 0)
        def _():
            dst_ref[...] = jnp.zeros_like(dst_ref)
            dlb_ref[...] = jnp.zeros_like(dlb_ref)

        row = lax.broadcasted_iota(jnp.int32, (CHUNK, CHUNK), 0)
        causal = row >= lax.broadcasted_iota(jnp.int32, (CHUNK, CHUNK), 1)
        last_row = lax.broadcasted_iota(jnp.int32, (CHUNK, HG_HEAD), 0) == CHUNK - 1
        for b in range(nb):
            for h in range(HG_HEADS):
                hs = slice(h * HG_HEAD, (h + 1) * HG_HEAD)
                zq = zh_ref[b, :, h * HG_HEAD:(h + 1) * HG_HEAD]
                zf = zh_ref[b, :, HG_WIDTH + h * HG_HEAD:HG_WIDTH + (h + 1) * HG_HEAD]
                zi = zh_ref[b, :, 2 * HG_WIDTH + h * HG_HEAD:2 * HG_WIDTH + (h + 1) * HG_HEAD]
                lbh = lb_ref[:, hs]
                sf, f, sq, qa, bc, bm, bl = _hgrn_gates(zq, zf, lbh)
                k = 1.0 - f
                e_qt = jnp.exp(bc - bm)
                e_kt = jnp.exp(bm - bc)
                e_b = jnp.exp(bc)
                e_kd = jnp.exp(bl - bc)
                e_l = jnp.exp(bl)
                qt, kt, qb, kd = qa * e_qt, k * e_kt, qa * e_b, k * e_kd
                a = jnp.where(causal, _dot_nt(qt, kt), 0.0)
                st = sts_ref[b, 0, h]
                dst = dst_ref[b, h]
                dov = do_ref[b, :, hs]
                da = jnp.where(causal, _dot_nt(dov, zi), 0.0)
                dqt = _hdot(da, kt)
                dkt = _hdot_tn(da, qt)
                dqb = _dot(dov, st)
                di = _dot_tn(a, dov) + _dot_nt(kd, dst)
                dkd = _dot(zi, dst)
                de_l = jnp.sum(dst * st, axis=0, keepdims=True)
                dst_ref[b, h] = dst * e_l + _dot_tn(dov, qb)
                dqa = dqt * e_qt + dqb * e_b
                dk = dkt * e_kt + dkd * e_kd
                dbl = jnp.sum(dkd * kd, axis=0, keepdims=True) + de_l * e_l
                db = dqt * qt - dkt * kt + dqb * qb - dkd * kd + jnp.where(last_row, dbl, 0.0)
                df = _cumsum_rows(db, reverse=True) / f - dk
                dzq = dqa * QSCALE * (sq * (1.0 + zq * (1.0 - sq)))
                dzf = df * (1.0 - lbh) * sf * (1.0 - sf)
                dz_ref[b, :, h * HG_HEAD:(h + 1) * HG_HEAD] = dzq.astype(BF16)
                dz_ref[b, :, HG_WIDTH + h * HG_HEAD:HG_WIDTH + (h + 1) * HG_HEAD] = dzf.astype(BF16)
                dz_ref[b, :, 2 * HG_WIDTH + h * HG_HEAD:2 * HG_WIDTH + (h + 1) * HG_HEAD] = di.astype(BF16)
                dlb_ref[:, hs] += jnp.sum(df * (1.0 - sf), axis=0, keepdims=True)

    rev = lambda c: nc - 1 - c
    return _pcall(body, "hgrn_bwd", (nc,),
                  [pl.BlockSpec((nb, CHUNK, 4 * HG_WIDTH), lambda c: (0, rev(c), 0)),
                   pl.BlockSpec((nb, CHUNK, HG_WIDTH), lambda c: (0, rev(c), 0)),
                   pl.BlockSpec((nb, 1, HG_HEADS, HG_HEAD, HG_HEAD), lambda c: (0, rev(c), 0, 0, 0)),
                   _full((1, HG_WIDTH))],
                  [pl.BlockSpec((nb, CHUNK, 3 * HG_WIDTH), lambda c: (0, rev(c), 0)), _full((1, HG_WIDTH))],
                  [_sds((nb, seq, 3 * HG_WIDTH), BF16), _sds((1, HG_WIDTH))],
                  scratch=[pltpu.VMEM((nb, HG_HEADS, HG_HEAD, HG_HEAD), F32)])(zh, do, sts, lb)


def _in_proj_bwd(dza, dzh, dzg, dzgt, dx1, x, g_mix, w_in, tm):
    t = x.shape[0]

    def body(dza_ref, dzh_ref, dzg_ref, dzgt_ref, dx1_ref, x_ref, g_ref, w_ref, dz_ref, dx_ref, dg_ref):
        @pl.when(pl.program_id(0) == 0)
        def _():
            dg_ref[...] = jnp.zeros_like(dg_ref)

        c1, c2, c3 = S5_WIDTH, S5_WIDTH + 3 * HG_WIDTH, S5_WIDTH + 4 * HG_WIDTH
        dz_ref[:, 0:c1] = dza_ref[...]
        dz_ref[:, c1:c2] = dzh_ref[...]
        dz_ref[:, c2:c3] = dzg_ref[...]
        dz_ref[:, c3:] = dzgt_ref[...]
        du = _dot(dz_ref[...], w_ref[...])
        xv = x_ref[...]
        r = lax.rsqrt(jnp.mean(xv * xv, axis=-1, keepdims=True) + EPS)
        xn = xv * r
        dg_ref[...] += jnp.sum(du * xn, axis=0, keepdims=True)
        dxn = du * g_ref[...]
        dx_ref[...] = dx1_ref[...] + r * (dxn - xn * jnp.mean(dxn * xn, axis=-1, keepdims=True))

    row = lambda w: pl.BlockSpec((tm, w), lambda i: (i, 0))
    return _pcall(body, "in_proj_bwd", (t // tm,),
                  [row(S5_WIDTH), row(3 * HG_WIDTH), row(HG_WIDTH), row(2 * D_MODEL), row(D_MODEL), row(D_MODEL),
                   _full((1, D_MODEL)), _full((N_IN, D_MODEL))],
                  [row(N_IN), row(D_MODEL), _full((1, D_MODEL))],
                  [_sds((t, N_IN), BF16), _sds((t, D_MODEL)), _sds((1, D_MODEL))],
                  )(dza, dzh, dzg, dzgt, dx1, x, g_mix, w_in)


def _tie(*arrays):
    return jnp.zeros((SUBLANES, 128), F32) + sum(a.reshape(-1)[0].astype(F32) for a in arrays)


def _after(value, token):
    return value + token[0, 0]


def _local_step(x3, tgt3, weights, sp, emit, emit_small):
    nb, seq, _ = x3.shape
    t = nb * seq
    tm = _token_tile(seq)
    x = x3.reshape(t, D_MODEL)
    tgt = tgt3.reshape(t, D_MODEL)
    row = lambda v: v.reshape(1, -1)

    a_re, a_im, b_re, b_im = sp["s5_a_re"], sp["s5_a_im"], sp["s5_b_re"], sp["s5_b_im"]
    ldt = sp["s5_log_dt"].reshape(S5_GROUPS, 1)
    lr, li, bb_re, bb_im, lb = _params_fwd(a_re, a_im, ldt, b_re, b_im, sp["hg_lb_logits"])
    lam = jnp.concatenate([lr.reshape(1, S5_N), li.reshape(1, S5_N)], axis=0)
    swap = lambda m: m.transpose(0, 2, 1)
    b_to_st = (_band_blocks(bb_re), _band_blocks(bb_im))
    b_to_ch = (_band_blocks(swap(bb_re)), _band_blocks(swap(bb_im)))
    c_to_ch = (_band_blocks(swap(sp["s5_c_re"])), _band_blocks(swap(-sp["s5_c_im"])))
    c_to_st = (_band_blocks(sp["s5_c_re"]), _band_blocks(-sp["s5_c_im"]))

    g_mix, g_ffn, g_final = row(sp["g_mix"]), row(sp["g_ffn"]), row(sp["g_final"])
    b_glu, gain, dskip, b_conv = row(sp["b_glu"]), row(sp["hg_norm_gain"]), row(sp["s5_d"]), row(sp["b_conv"])

    w_in = weights("in", lam, *b_to_st, *b_to_ch, *c_to_ch, *c_to_st)["w_in"]
    u, za, zh, zgt = _in_proj(x, g_mix, w_in, tm)
    seqs = lambda v: v.reshape(nb, seq, v.shape[-1])
    toks = lambda v: v.reshape(t, v.shape[-1])
    xs3, y0 = _s5_fwd(seqs(za), b_to_st, lam, c_to_ch, dskip, nb, seq, tm)
    xs, y0 = toks(xs3), toks(y0)
    o3, sts = _hgrn_fwd(zh.reshape(nb, seq, 4 * HG_WIDTH), lb, nb, seq)
    o = o3.reshape(t, HG_WIDTH)
    wm = weights("mix", y0, o3)
    x1, u2, pa, pb, ya2, yb = _mix_fwd(x, y0, o, zh, zgt, wm["w_glu"], b_glu, gain, wm["w_pa"], wm["w_pb"],
                                       wm["w_out"], g_ffn, tm)
    wf = weights("ffn", u2)
    h = _ffn_up(u2, wf["w_up"], min(4 * tm, t))
    hc, a, dx2, dx2b, loss, dg_final = _ffn_down_loss(h, x1, tgt, wf["w_conv"], b_conv, wf["w_down"], g_final,
                                                      seq, tm)

    wgrad = functools.partial(_wgrad, tn=256, out_dtype=BF16)
    dhc, db_conv = _ffn_bwd_act(dx2b, hc, wf["w_down"], tm)
    sent = emit({"w_down": wgrad(a, dx2b, "dw_down")})
    dh, dx1, dx1b, dg_ffn, dw_conv = _ffn_bwd_up(dhc, h, dx2, x1, wf["w_conv"], wf["w_up"], _after(g_ffn, sent),
                                                 seq, tm)
    sent = emit({"w_up": wgrad(dh, u2, "dw_up"), "w_conv": dw_conv})
    (dy0, do, dzg, dzgt, m, dpa, dpb, ya1, dpre, db_glu, dgain) = _mix_bwd(
        dx1b, y0, o, zh, zgt, pa, pb, wm["w_glu"], _after(b_glu, sent), gain, wm["w_pa"], wm["w_pb"], wm["w_out"], tm)
    sent = emit({"w_out": wgrad(m, dx1b, "dw_out"), "w_pa": wgrad(ya2, dpa, "dw_pa"),
                 "w_pb": wgrad(yb, dpb, "dw_pb"), "w_glu": wgrad(ya1, dpre, "dw_glu")})
    dzh3, dlb = _hgrn_bwd(zh.reshape(nb, seq, 4 * HG_WIDTH), do.reshape(nb, seq, HG_WIDTH), sts, _after(lb, sent),
                          nb, seq)
    dza, a_s5, dlam, dd = _s5_bwd(seqs(dy0), seqs(za), xs3, c_to_st, b_to_ch, lam, dskip, nb, seq, tm)
    dza, a_s5 = toks(dza), toks(a_s5)
    dz, dx, dg_mix = _in_proj_bwd(dza, dzh3.reshape(t, 3 * HG_WIDTH), dzg, dzgt, dx1, x, g_mix, w_in, tm)
    sent = emit({"w_in": wgrad(dz, u, "dw_in")})

    band = HG_HEAD
    dbb_band = _wgrad(a_s5, za, "dbb_s5", 512, band=band, after=sent)
    dc_band = _wgrad(xs, dy0, "dc_s5", 512, band=band, after=sent)
    dbb_re = swap(_diag_blocks(dbb_band[:S5_N], S5_STATE, S5_GROUP))
    dbb_im = swap(_diag_blocks(dbb_band[S5_N:], S5_STATE, S5_GROUP))
    dc_re = swap(_diag_blocks(dc_band[:S5_N], S5_STATE, S5_GROUP))
    dc_im = -swap(_diag_blocks(dc_band[S5_N:], S5_STATE, S5_GROUP))
    da_re, da_im, dldt, db_re, db_im, dlogits = _params_bwd(
        a_re, a_im, ldt, b_re, b_im, sp["hg_lb_logits"],
        dlam[0].reshape(S5_GROUPS, S5_STATE), dlam[1].reshape(S5_GROUPS, S5_STATE), dbb_re, dbb_im, dlb)
    emit_small({"g_mix": dg_mix, "s5_a_re": da_re, "s5_a_im": da_im, "s5_log_dt": dldt.reshape(1, S5_GROUPS),
                "s5_b_re": db_re, "s5_b_im": db_im, "s5_c_re": dc_re, "s5_c_im": dc_im, "s5_d": dd, "b_glu": db_glu,
                "hg_lb_logits": dlogits, "hg_norm_gain": dgain, "g_ffn": dg_ffn, "b_conv": db_conv,
                "g_final": dg_final, "loss": loss})
    return dx.reshape(nb, seq, D_MODEL)


def _mesh_peers():
    x, y, c = lax.axis_index("x"), lax.axis_index("y"), lax.axis_index("c")
    peers = []
    for k in range(1, N_DEV):
        px, py, pc = (1 - x if k & 4 else x), (1 - y if k & 2 else y), (1 - c if k & 1 else c)
        peers.append((k, (px, py, pc), 4 * px + 2 * py + pc))
    return 4 * x + 2 * y + c, peers


_HBM = pl.BlockSpec(memory_space=pltpu.HBM)
_SEM = pl.BlockSpec(memory_space=pltpu.SEMAPHORE)


def _exchange_start(name, operands, after, place_own=True):
    n = len(operands)
    me = 4 * lax.axis_index("x") + 2 * lax.axis_index("y") + lax.axis_index("c")
    flags = [per_peer for _, per_peer in operands]
    srcs, lands = [], []
    for arr, per_peer in operands:
        land = lax.empty((N_DEV,) + (arr.shape[1:] if per_peer else arr.shape), arr.dtype)
        if place_own:
            own = lax.dynamic_index_in_dim(arr, me, 0, keepdims=True) if per_peer else arr[None]
            land = lax.dynamic_update_slice_in_dim(land, own, me, 0)
        srcs.append(pltpu.with_memory_space_constraint(arr, pltpu.HBM))
        lands.append(pltpu.with_memory_space_constraint(land, pltpu.HBM))
    copies = (N_DEV - 1) * n

    def body(*refs):
        src_refs, land_refs = refs[:n], refs[n:2 * n]
        send_sems, recv_sems = refs[2 * n + 1], refs[2 * n + 2]
        token = refs[-1]
        my_slab, peers = _mesh_peers()
        for k, peer, slab in peers:
            for i in range(n):
                s = (k - 1) * n + i
                pltpu.make_async_remote_copy(
                    src_ref=src_refs[i].at[slab] if flags[i] else src_refs[i], dst_ref=land_refs[i].at[my_slab],
                    send_sem=send_sems.at[s], recv_sem=recv_sems.at[s], device_id=peer,
                    device_id_type=pl.DeviceIdType.MESH).start()
        token[...] = jnp.zeros_like(token)

    outs = pl.pallas_call(
        body, name=name,
        out_shape=(pltpu.SemaphoreType.DMA((copies,)), pltpu.SemaphoreType.DMA((copies,)),
                   *[pltpu.HBM(a.shape, a.dtype) for a in srcs], *[pltpu.HBM(a.shape, a.dtype) for a in lands],
                   _sds((SUBLANES, 128))),
        in_specs=[_HBM] * (2 * n) + [pl.BlockSpec(memory_space=pl.ANY)],
        out_specs=(_SEM, _SEM, *[_HBM] * (2 * n), pl.BlockSpec(memory_space=pltpu.VMEM)),
        input_output_aliases={i: 2 + i for i in range(2 * n)},
        compiler_params=pltpu.CompilerParams(has_side_effects=pltpu.SideEffectType.DATAFLOW_SIDE_EFFECTING),
    )(*srcs, *lands, after)
    state = (flags, outs[0], outs[1], outs[2:2 + n], outs[2 + n:2 + 2 * n])
    return state, outs[-1]


def _exchange_wait(name, state, *after):
    flags, send_sems, recv_sems, srcs, lands = state
    n = len(flags)

    def body(*refs):
        src_refs, land_refs = refs[:n], refs[n:2 * n]
        send_ref, recv_ref = refs[2 * n], refs[2 * n + 1]
        _, peers = _mesh_peers()
        for k, peer, slab in peers:
            for i in range(n):
                s = (k - 1) * n + i
                copy = pltpu.make_async_remote_copy(
                    src_ref=src_refs[i].at[slab] if flags[i] else src_refs[i], dst_ref=land_refs[i].at[slab],
                    send_sem=send_ref.at[s], recv_sem=recv_ref.at[s], device_id=peer,
                    device_id_type=pl.DeviceIdType.MESH)
                copy.wait_send()
                copy.wait_recv()

    outs = pl.pallas_call(
        body, name=name,
        out_shape=(*[pltpu.HBM(a.shape, a.dtype) for a in srcs], *[pltpu.HBM(a.shape, a.dtype) for a in lands]),
        in_specs=[_HBM] * (2 * n) + [_SEM, _SEM] + [pl.BlockSpec(memory_space=pl.ANY)] * len(after),
        out_specs=tuple([_HBM] * (2 * n)),
        input_output_aliases={i: i for i in range(2 * n)},
        compiler_params=pltpu.CompilerParams(has_side_effects=pltpu.SideEffectType.DATAFLOW_SIDE_EFFECTING),
    )(*srcs, *lands, send_sems, recv_sems, *after)
    return list(outs[n:]), list(outs[:n])


def _slab(pos):
    return 4 * pos[0] + 2 * pos[1] + pos[2]


def _chip_routes():
    x, y, c = lax.axis_index("x"), lax.axis_index("y"), lax.axis_index("c")
    return (x, y, c), (x, y, 1 - c), [(1 - x, y, c), (x, 1 - y, c), (1 - x, 1 - y, c)]


def _remote(src, dst, send_sem, recv_sem, to):
    return pltpu.make_async_remote_copy(src_ref=src, dst_ref=dst, send_sem=send_sem, recv_sem=recv_sem,
                                        device_id=to, device_id_type=pl.DeviceIdType.MESH)


_EFFECT = pltpu.CompilerParams(has_side_effects=pltpu.SideEffectType.DATAFLOW_SIDE_EFFECTING)


def _gather_start(name, arrays, after):
    n = len(arrays)
    me = 4 * lax.axis_index("x") + 2 * lax.axis_index("y") + lax.axis_index("c")
    srcs = [pltpu.with_memory_space_constraint(a, pltpu.HBM) for a in arrays]
    lands = [pltpu.with_memory_space_constraint(
        lax.dynamic_update_slice_in_dim(lax.empty((N_DEV,) + a.shape, a.dtype), a[None], me, 0), pltpu.HBM)
        for a in arrays]

    def body(*refs):
        src_refs, land_refs = refs[:n], refs[n:2 * n]
        send_sems, recv_sems, token = refs[2 * n + 1], refs[2 * n + 2], refs[-1]
        mine, sibling, chips = _chip_routes()
        for k, to in enumerate([sibling] + chips):
            for i in range(n):
                _remote(src_refs[i], land_refs[i].at[_slab(mine)], send_sems.at[k * n + i], recv_sems.at[k * n + i],
                        to).start()
        token[...] = jnp.zeros_like(token)

    outs = pl.pallas_call(
        body, name=name,
        out_shape=(pltpu.SemaphoreType.DMA((4 * n,)), pltpu.SemaphoreType.DMA((4 * n,)),
                   *[pltpu.HBM(a.shape, a.dtype) for a in srcs], *[pltpu.HBM(a.shape, a.dtype) for a in lands],
                   _sds((SUBLANES, 128))),
        in_specs=[_HBM] * (2 * n) + [pl.BlockSpec(memory_space=pl.ANY)],
        out_specs=(_SEM, _SEM, *[_HBM] * (2 * n), pl.BlockSpec(memory_space=pltpu.VMEM)),
        input_output_aliases={i: 2 + i for i in range(2 * n)}, compiler_params=_EFFECT,
    )(*srcs, *lands, after)
    return (outs[0], outs[1], outs[2:2 + n], outs[2 + n:2 + 2 * n]), outs[-1]


def _gather_forward(name, state, *after):
    send_a, recv_a, srcs, lands = state
    n = len(lands)

    def body(*refs):
        land_refs, recv_a_ref = refs[:n], refs[n]
        send_b, recv_b = refs[n + 1 + len(after)], refs[n + 2 + len(after)]
        mine, sibling, chips = _chip_routes()
        for j, chip in enumerate(chips):
            for i in range(n):
                block = land_refs[i].at[_slab(chip)]
                _remote(block, block, send_b.at[j * n + i], recv_a_ref.at[(1 + j) * n + i], chip).wait_recv()
                _remote(block, block, send_b.at[j * n + i], recv_b.at[j * n + i], sibling).start()

    outs = pl.pallas_call(
        body, name=name,
        out_shape=(pltpu.SemaphoreType.DMA((3 * n,)), pltpu.SemaphoreType.DMA((3 * n,)),
                   *[pltpu.HBM(a.shape, a.dtype) for a in lands]),
        in_specs=[_HBM] * n + [_SEM] + [pl.BlockSpec(memory_space=pl.ANY)] * len(after),
        out_specs=(_SEM, _SEM, *[_HBM] * n),
        input_output_aliases={i: 2 + i for i in range(n)}, compiler_params=_EFFECT,
    )(*lands, recv_a, *after)
    return (send_a, recv_a, srcs, list(outs[2:])), (outs[0], outs[1])


def _gather_wait(name, state, forwarded):
    send_a, recv_a, srcs, lands = state
    send_b, recv_b = forwarded
    n = len(lands)

    def body(*refs):
        src_refs, land_refs = refs[:n], refs[n:2 * n]
        sa, ra, sb, rb = refs[2 * n:2 * n + 4]
        mine, sibling, chips = _chip_routes()
        for i in range(n):
            for k, to in enumerate([sibling] + chips):
                _remote(src_refs[i], land_refs[i].at[_slab(mine)], sa.at[k * n + i], ra.at[k * n + i], to).wait_send()
            theirs = land_refs[i].at[_slab(sibling)]
            _remote(theirs, theirs, sa.at[i], ra.at[i], sibling).wait_recv()
            for j, chip in enumerate(chips):
                sent = land_refs[i].at[_slab(chip)]
                got = land_refs[i].at[_slab((chip[0], chip[1], sibling[2]))]
                _remote(sent, sent, sb.at[j * n + i], rb.at[j * n + i], sibling).wait_send()
                _remote(got, got, sb.at[j * n + i], rb.at[j * n + i], sibling).wait_recv()

    outs = pl.pallas_call(
        body, name=name,
        out_shape=(*[pltpu.HBM(a.shape, a.dtype) for a in srcs], *[pltpu.HBM(a.shape, a.dtype) for a in lands]),
        in_specs=[_HBM] * (2 * n) + [_SEM] * 4, out_specs=tuple([_HBM] * (2 * n)),
        input_output_aliases={i: i for i in range(2 * n)}, compiler_params=_EFFECT,
    )(*srcs, *lands, send_a, recv_a, send_b, recv_b)
    return list(outs[n:])


def _join_cols(parts, name, tr):
    _, r, c = parts.shape

    def body(p_ref, o_ref):
        for j in range(N_DEV):
            o_ref[:, j * c:(j + 1) * c] = p_ref[j]

    return _pcall(body, name, (r // tr,), [pl.BlockSpec((N_DEV, tr, c), lambda i: (0, i, 0))],
                  pl.BlockSpec((tr, N_DEV * c), lambda i: (i, 0)), _sds((r, N_DEV * c), parts.dtype))(parts)


def _split_cols(full, name, tr):
    r, c = full.shape[0], full.shape[1] // N_DEV

    def body(f_ref, o_ref):
        for j in range(N_DEV):
            o_ref[j] = f_ref[:, j * c:(j + 1) * c]

    return _pcall(body, name, (r // tr,), [pl.BlockSpec((tr, N_DEV * c), lambda i: (i, 0))],
                  pl.BlockSpec((N_DEV, tr, c), lambda i: (0, i, 0)), _sds((N_DEV, r, c), full.dtype))(full)


def _my_slab():
    return (4 * lax.axis_index("x") + 2 * lax.axis_index("y") + lax.axis_index("c")).astype(jnp.int32).reshape(1)


def _adamw(parts, sent, w, m, v, name, tile):
    _, rows, cols = w.shape

    def body(me_ref, p_ref, s_ref, w_ref, m_ref, v_ref, g_out, d_out, m_out, v_out):
        me = me_ref[0]
        g = jnp.where(me == 0, s_ref[0], p_ref[0]).astype(F32)
        for k in range(1, N_DEV):
            g = g + jnp.where(me == k, s_ref[0], p_ref[k]).astype(F32)
        m1 = ADAM_B1 * m_ref[0] + (1.0 - ADAM_B1) * g
        v1 = ADAM_B2 * v_ref[0] + (1.0 - ADAM_B2) * (g * g)
        m_hat = m1 / (1.0 - ADAM_B1 ** ADAM_STEP)
        v_hat = v1 / (1.0 - ADAM_B2 ** ADAM_STEP)
        g_out[0] = g
        d_out[0] = -ADAM_LR * (m_hat / (jnp.sqrt(v_hat) + ADAM_EPS) + ADAM_WD * w_ref[0])
        m_out[0] = m1
        v_out[0] = v1

    row = pl.BlockSpec((1, tile, cols), lambda i, me: (0, i, 0))
    return pl.pallas_call(
        body, name=name, out_shape=[_sds((1, rows, cols))] * 4,
        grid_spec=pltpu.PrefetchScalarGridSpec(
            num_scalar_prefetch=1, grid=(rows // tile,),
            in_specs=[pl.BlockSpec((N_DEV, tile, cols), lambda i, me: (0, i, 0)),
                      pl.BlockSpec((1, tile, cols), lambda i, me: (me[0], i, 0)), row, row, row],
            out_specs=[row, row, row, row]),
        compiler_params=pltpu.CompilerParams(dimension_semantics=("arbitrary",), vmem_limit_bytes=VMEM_LIMIT),
    )(_my_slab(), parts, sent, w, m, v)


BIG = {
    "w_in": ((N_IN // N_DEV, D_MODEL), False, N_IN // N_DEV // 3),
    "w_glu": ((S5_WIDTH // N_DEV, S5_WIDTH), False, S5_WIDTH // N_DEV),
    "w_pa": ((S5_WIDTH, D_MODEL // N_DEV), True, S5_WIDTH),
    "w_pb": ((HG_WIDTH, D_MODEL // N_DEV), True, HG_WIDTH),
    "w_out": ((D_MODEL // N_DEV, D_MODEL), False, D_MODEL // N_DEV),
    "w_up": ((2 * D_FF // N_DEV, D_MODEL), False, 2 * D_FF // N_DEV // 4),
    "w_conv": ((CONV_W, 2 * D_FF // N_DEV), True, CONV_W),
    "w_down": ((D_FF // N_DEV, D_MODEL), False, D_FF // N_DEV // 2),
}
TRANSPOSED = ("w_in", "w_up", "s5_b_re", "s5_b_im")
UNALIGNED_COLS = ("w_conv",)


def _stored(n, arr):
    return jnp.swapaxes(arr, -1, -2) if n in TRANSPOSED else arr


def _join_shards(n, parts):
    (a, b), by_cols, _ = BIG[n]
    if not by_cols:
        return parts.reshape(N_DEV * a, b)
    if n in UNALIGNED_COLS:
        return _join_cols(parts, "join_" + n, min(a, 256))
    return parts.transpose(1, 0, 2).reshape(a, N_DEV * b)


def _split_shards(n, full):
    (a, b), by_cols, _ = BIG[n]
    if not by_cols:
        return full.reshape(N_DEV, a, b)
    if n in UNALIGNED_COLS:
        return _split_cols(full, "split_" + n, min(a, 256))
    return full.reshape(a, N_DEV, b).transpose(1, 0, 2)


SMALL_CORE = {
    "s5_b_re": GSC, "s5_b_im": GSC, "s5_c_re": GSC, "s5_c_im": GSC,
    "g_mix": (1, D_MODEL), "g_ffn": (1, D_MODEL), "g_final": (1, D_MODEL), "s5_d": (1, S5_WIDTH),
    "b_glu": (1, S5_WIDTH), "hg_norm_gain": (1, HG_WIDTH), "hg_lb_logits": (2, HG_WIDTH), "b_conv": (1, 2 * D_FF),
    "s5_log_dt": (1, S5_GROUPS), "s5_a_re": (S5_GROUPS, S5_STATE), "s5_a_im": (S5_GROUPS, S5_STATE), "loss": (1, 1),
}
BLOCK_ROWS = 32


def _small_rows():
    rows, r = {}, 0
    for n, core in SMALL_CORE.items():
        rows[n] = r
        r += BLOCK_ROWS if len(core) == 3 else -(-math.prod(core) // PACK_W)
    return rows, -(-r // SUBLANES) * SUBLANES


SMALL_ROW, SMALL_ROWS = _small_rows()


def _small_pieces(name):
    r, core = SMALL_ROW[name], SMALL_CORE[name]
    if len(core) == 3:
        return [((g, slice(None), slice(None)), slice(r + S5_GROUP * (g % 2), r + S5_GROUP * (g % 2 + 1)),
                 slice(S5_STATE * (g // 2), S5_STATE * (g // 2 + 1))) for g in range(S5_GROUPS)]
    pieces = []
    for i in range(core[0]):
        for c0 in range(0, core[1], PACK_W):
            w, flat = min(PACK_W, core[1] - c0), i * core[1] + c0
            pieces.append(((slice(i, i + 1), slice(c0, c0 + w)), slice(r + flat // PACK_W, r + flat // PACK_W + 1),
                           slice(flat % PACK_W, flat % PACK_W + w)))
    return pieces


def _core_index(ref, name, idx):
    return (0,) * (len(ref.shape) - len(SMALL_CORE[name])) + idx


def _pack_small_grads(grads):
    names = list(SMALL_CORE)

    def body(*refs):
        pack = refs[-1]
        pack[...] = jnp.zeros_like(pack)
        for ref, n in zip(refs, names):
            for idx, rows, lanes in _small_pieces(n):
                pack[rows, lanes] = ref[_core_index(ref, n, idx)]

    return _pcall(body, "pack_small_grads", (1,), [_full(grads[n].shape) for n in names],
                  _full((SMALL_ROWS, PACK_W)), _sds((SMALL_ROWS, PACK_W)))(*[grads[n] for n in names])


def _adamw_small(parts, sent, names, rows, given, name):
    lo, hi = rows
    k = len(names)
    shapes = [given[n].shape for n in names]

    def body(*refs):
        me, p_ref, s_ref, ins, outs = refs[0][0], refs[1], refs[2], refs[3:3 + 3 * k], refs[3 + 3 * k:3 + 7 * k]
        packs, results = refs[3 + 7 * k:6 + 7 * k], refs[6 + 7 * k:]
        for j, pack in enumerate(packs):
            pack[...] = jnp.zeros_like(pack)
            for ref, n in zip(ins[j * k:(j + 1) * k], names):
                for idx, prow, lanes in _small_pieces(n):
                    pack[slice(prow.start - lo, prow.stop - lo), lanes] = ref[_core_index(ref, n, idx)]
        mine = s_ref[lo:hi, :]
        g = jnp.where(me == 0, mine, p_ref[0, lo:hi, :])
        for d in range(1, N_DEV):
            g = g + jnp.where(me == d, mine, p_ref[d, lo:hi, :])
        m1 = ADAM_B1 * packs[1][...] + (1.0 - ADAM_B1) * g
        v1 = ADAM_B2 * packs[2][...] + (1.0 - ADAM_B2) * (g * g)
        m_hat = m1 / (1.0 - ADAM_B1 ** ADAM_STEP)
        v_hat = v1 / (1.0 - ADAM_B2 ** ADAM_STEP)
        results[0][...] = g
        results[1][...] = -ADAM_LR * (m_hat / (jnp.sqrt(v_hat) + ADAM_EPS) + ADAM_WD * packs[0][...])
        results[2][...] = m1
        results[3][...] = v1
        for j, result in enumerate(results):
            for ref, n in zip(outs[j * k:(j + 1) * k], names):
                for idx, prow, lanes in _small_pieces(n):
                    ref[_core_index(ref, n, idx)] = result[slice(prow.start - lo, prow.stop - lo), lanes]

    flat = _pcall(body, name, (1,),
                  [pl.BlockSpec(memory_space=pltpu.SMEM), _full(parts.shape), _full(sent.shape)]
                  + [_full(s) for s in shapes] * 3,
                  [_full(s) for s in shapes] * 4, [_sds(s) for s in shapes] * 4,
                  scratch=[pltpu.VMEM((hi - lo, PACK_W), F32)] * 7,
                  )(_my_slab(), parts, sent, *[given[pre + n] for pre in ("", "m_", "v_") for n in names])
    return {n: [flat[j * k + i] for j in range(4)] for i, n in enumerate(names)}


def kernel(x, g_mix, w_in, s5_a_re, s5_a_im, s5_log_dt, s5_b_re, s5_b_im, s5_c_re, s5_c_im, s5_d, w_glu, b_glu, hg_lb_logits, hg_norm_gain, w_pa, w_pb, w_out, g_ffn, w_up, w_conv, b_conv, w_down, g_final, loss_target, m_g_mix, m_w_in, m_s5_a_re, m_s5_a_im, m_s5_log_dt, m_s5_b_re, m_s5_b_im, m_s5_c_re, m_s5_c_im, m_s5_d, m_w_glu, m_b_glu, m_hg_lb_logits, m_hg_norm_gain, m_w_pa, m_w_pb, m_w_out, m_g_ffn, m_w_up, m_w_conv, m_b_conv, m_w_down, m_g_final, v_g_mix, v_w_in, v_s5_a_re, v_s5_a_im, v_s5_log_dt, v_s5_b_re, v_s5_b_im, v_s5_c_re, v_s5_c_im, v_s5_d, v_w_glu, v_b_glu, v_hg_lb_logits, v_hg_norm_gain, v_w_pa, v_w_pb, v_w_out, v_g_ffn, v_w_up, v_w_conv, v_b_conv, v_w_down, v_g_final):
    given = dict(locals())
    small_names = [n for n, _ in SMALL]

    pay = {n: given[n][0] if n == "w_conv" else _stored(n, given[n])[0].astype(BF16) for n in BIG}
    groups = {"in": ["w_in"], "mix": ["w_glu", "w_pa", "w_pb", "w_out"], "ffn": ["w_up", "w_down", "w_conv"]}
    gathers, order = {}, pay["w_in"]
    for grp, names in groups.items():
        gathers[grp], order = _gather_start("gather_" + grp + "_start", [pay[n] for n in names], order)

    def weights(grp, *after):
        if grp == "in":
            after = (*after, order)
        state, forwarded = _gather_forward("gather_" + grp + "_forward", gathers[grp], *after)
        got = _gather_wait("gather_" + grp + "_wait", state, forwarded)
        return {n: _join_shards(n, g) for n, g in zip(groups[grp], got)}

    in_flight, started = [], []

    def emit(grads):
        names = list(grads)
        state, token = _exchange_start("grads_" + names[0] + "_start",
                                       [(_split_shards(n, grads[n]), True) for n in names], grads[names[0]],
                                       place_own=False)
        in_flight.append((names, state))
        return token

    def emit_small(grads):
        pack = _pack_small_grads(grads)
        state, token = _exchange_start("grads_small_start", [(pack, False)], pack, place_own=False)
        in_flight.append((["small"], state))
        started.append(token)

    sp = {n: (given[n] if n in ("g_final", "hg_lb_logits") else _stored(n, given[n])[0]) for n in small_names}
    sp["g_mix"] = _after(sp["g_mix"], order)
    dx = _local_step(x, loss_target, weights, sp, emit, emit_small)

    res = {}
    after = [started[-1]]
    in_flight.insert(-1, in_flight.pop())
    for names, state in in_flight:
        parts, sent = _exchange_wait("grads_" + names[0] + "_wait", state, *after)
        if names != ["small"]:
            after = []
            for n, part, mine in zip(names, parts, sent):
                raw = _adamw(part, mine, *[_stored(n, given[pre + n]) for pre in ("", "m_", "v_")], "adamw_" + n,
                             BIG[n][2])
                res[n] = [_stored(n, r) for r in raw]
                after.append(raw[0])
            continue
        sgiven = {pre + n: _stored(n, given[pre + n]) for pre in ("", "m_", "v_") for n in small_names}
        for pre in ("", "m_", "v_"):
            sgiven[pre + "g_final"] = given[pre + "g_final"].reshape(1, D_MODEL)
            sgiven[pre + "loss"] = jnp.zeros((1, 1), F32)
        raw = _adamw_small(parts[0], sent[0], list(SMALL_CORE), (0, SMALL_ROWS), sgiven, "adamw_small")
        res.update({n: [_stored(n, r) for r in raw[n]] for n in small_names})
        res["g_final"] = [r.reshape(D_MODEL) for r in raw["g_final"]]
        total_loss = raw["loss"][0].reshape(())
        after = [raw["s5_b_re"][0], raw["g_mix"][0]]
    return (total_loss, dx, *[res[n][0] for n in WEIGHT_ORDER], *[res[n][1] for n in WEIGHT_ORDER],
            *[res[n][2] for n in WEIGHT_ORDER], *[res[n][3] for n in WEIGHT_ORDER])
```

```python
import functools
import math

import jax
import jax.numpy as jnp
from jax import lax
from jax.experimental import pallas as pl
from jax.experimental.pallas import tpu as pltpu

F32 = jnp.float32
BF16 = jnp.bfloat16

D_MODEL = 1024
S5_WIDTH = 512
S5_GROUP = 16
S5_GROUPS = 32
S5_STATE = 64
S5_N = S5_GROUPS * S5_STATE
HG_WIDTH = 512
HG_HEAD = 128
HG_HEADS = 4
D_FF = 2816
CONV_W = 3
CHUNK = 64
N_IN = S5_WIDTH + 4 * HG_WIDTH + 2 * D_MODEL
EPS = 1e-6
QSCALE = HG_HEAD ** -0.5

ADAM_LR = 0.001
ADAM_B1 = 0.9
ADAM_B2 = 0.999
ADAM_EPS = 1e-08
ADAM_WD = 0.01
ADAM_STEP = 10

N_DEV = 8
V7X_VMEM_BYTES = 64 * 1024 * 1024
VMEM_LIMIT = V7X_VMEM_BYTES * 7 // 8
SUBLANES = 8
PACK_W = 1024

SMALL = (
    ("g_mix", (1, D_MODEL)),
    ("s5_a_re", (1, S5_GROUPS, S5_STATE)),
    ("s5_a_im", (1, S5_GROUPS, S5_STATE)),
    ("s5_log_dt", (1, S5_GROUPS)),
    ("s5_b_re", (1, S5_GROUPS, S5_STATE, S5_GROUP)),
    ("s5_b_im", (1, S5_GROUPS, S5_STATE, S5_GROUP)),
    ("s5_c_re", (1, S5_GROUPS, S5_GROUP, S5_STATE)),
    ("s5_c_im", (1, S5_GROUPS, S5_GROUP, S5_STATE)),
    ("s5_d", (1, S5_WIDTH)),
    ("b_glu", (1, S5_WIDTH)),
    ("hg_lb_logits", (2, HG_WIDTH)),
    ("hg_norm_gain", (1, HG_WIDTH)),
    ("g_ffn", (1, D_MODEL)),
    ("b_conv", (1, 2 * D_FF)),
    ("g_final", (D_MODEL,)),
)
WEIGHT_ORDER = ("g_mix", "w_in", "s5_a_re", "s5_a_im", "s5_log_dt", "s5_b_re", "s5_b_im", "s5_c_re", "s5_c_im",
                "s5_d", "w_glu", "b_glu", "hg_lb_logits", "hg_norm_gain", "w_pa", "w_pb", "w_out", "g_ffn",
                "w_up", "w_conv", "b_conv", "w_down", "g_final")


def _pcall(body, name, grid, in_specs, out_specs, out_shape, scratch=()):
    return pl.pallas_call(
        body, name=name, grid=grid, in_specs=in_specs, out_specs=out_specs, out_shape=out_shape,
        scratch_shapes=list(scratch),
        compiler_params=pltpu.CompilerParams(dimension_semantics=("arbitrary",) * len(grid),
                                             vmem_limit_bytes=VMEM_LIMIT),
    )


def _full(shape):
    return pl.BlockSpec(shape, lambda *_: (0,) * len(shape))


def _sds(shape, dtype=F32):
    return jax.ShapeDtypeStruct(shape, dtype)


def _dot(a, b):
    return jnp.dot(a.astype(BF16), b.astype(BF16), preferred_element_type=F32)


def _dot_nt(a, b):
    return lax.dot_general(a.astype(BF16), b.astype(BF16), (((1,), (1,)), ((), ())), preferred_element_type=F32)


def _dot_tn(a, b):
    return lax.dot_general(a.astype(BF16), b.astype(BF16), (((0,), (0,)), ((), ())), preferred_element_type=F32)


def _hdot(a, b):
    return jnp.dot(a, b, preferred_element_type=F32, precision=lax.Precision.HIGHEST)


def _hdot_tn(a, b):
    return lax.dot_general(a, b, (((0,), (0,)), ((), ())), preferred_element_type=F32,
                           precision=lax.Precision.HIGHEST)


def _sigmoid(x):
    return jax.nn.sigmoid(x)


GELU_C = math.sqrt(2.0 / math.pi)
GELU_A = 0.044715


def _gelu(x):
    return 0.5 * x * (1.0 + jnp.tanh(GELU_C * (x + GELU_A * (x * x * x))))


def _gelu_grad(x):
    t = jnp.tanh(GELU_C * (x + GELU_A * (x * x * x)))
    return 0.5 * (1.0 + t) + 0.5 * x * (1.0 - t * t) * (GELU_C * (1.0 + 3.0 * GELU_A * x * x))


def _cumsum_rows(v, reverse=False):
    n = v.shape[0]
    row = lax.broadcasted_iota(jnp.int32, v.shape, 0)
    s = 1
    while s < n:
        if reverse:
            v = v + jnp.where(row < n - s, pltpu.roll(v, n - s, axis=0), 0.0)
        else:
            v = v + jnp.where(row >= s, pltpu.roll(v, s, axis=0), 0.0)
        s *= 2
    return v


def _token_tile(seq):
    return min(256, seq)


def _s5_coeffs(a_re, a_im, ldt):
    dt = jnp.exp(ldt)
    mag = jnp.exp(a_re * dt)
    ang = a_im * dt
    lb_re = mag * jnp.cos(ang)
    lb_im = mag * jnp.sin(ang)
    den = a_re * a_re + a_im * a_im
    n_re = lb_re - 1.0
    n_im = lb_im
    co_re = (n_re * a_re + n_im * a_im) / den
    co_im = (n_im * a_re - n_re * a_im) / den
    return lb_re, lb_im, co_re, co_im


GS, GSC = (S5_GROUPS, S5_STATE), (S5_GROUPS, S5_GROUP, S5_STATE)


def _params_fwd(a_re, a_im, ldt, bt_re, bt_im, logits):
    def body(are, aim, ld, bre, bim, lg, lr_o, li_o, bbr_o, bbi_o, lb_o):
        lr, li, co_re, co_im = _s5_coeffs(are[...], aim[...], ld[...])
        lr_o[...] = lr
        li_o[...] = li
        for g in range(S5_GROUPS):
            cr, ci = co_re[g:g + 1, :], co_im[g:g + 1, :]
            bbr_o[g] = cr * bre[g] - ci * bim[g]
            bbi_o[g] = cr * bim[g] + ci * bre[g]
        lb_o[...] = _sigmoid(lg[0:1, :] - lg[1:2, :])

    return _pcall(body, "params_fwd", (1,),
                  [_full(GS), _full(GS), _full((S5_GROUPS, 1)), _full(GSC), _full(GSC), _full((2, HG_WIDTH))],
                  [_full(GS), _full(GS), _full(GSC), _full(GSC), _full((1, HG_WIDTH))],
                  [_sds(GS), _sds(GS), _sds(GSC), _sds(GSC), _sds((1, HG_WIDTH))],
                  )(a_re, a_im, ldt, bt_re, bt_im, logits)


def _params_bwd(a_re, a_im, ldt, bt_re, bt_im, logits, dlr, dli, dbbr, dbbi, dlb):
    def body(are, aim, ld, bre, bim, lg, dlr_r, dli_r, dbbr_r, dbbi_r, dlb_r,
             dare_o, daim_o, dld_o, dbre_o, dbim_o, dlg_o, dcr_ref, dci_ref):
        (_, _, co_re, co_im), vjp = jax.vjp(_s5_coeffs, are[...], aim[...], ld[...])
        for g in range(S5_GROUPS):
            cr, ci = co_re[g:g + 1, :], co_im[g:g + 1, :]
            gr, gi, br, bi = dbbr_r[g], dbbi_r[g], bre[g], bim[g]
            dbre_o[g] = cr * gr + ci * gi
            dbim_o[g] = cr * gi - ci * gr
            dcr_ref[g:g + 1, :] = jnp.sum(gr * br + gi * bi, axis=0, keepdims=True)
            dci_ref[g:g + 1, :] = jnp.sum(gi * br - gr * bi, axis=0, keepdims=True)
        dare, daim, dld = vjp((dlr_r[...], dli_r[...], dcr_ref[...], dci_ref[...]))
        dare_o[...] = dare
        daim_o[...] = daim
        dld_o[...] = dld
        lb = _sigmoid(lg[0:1, :] - lg[1:2, :])
        d0 = dlb_r[...] * lb * (1.0 - lb)
        dlg_o[0:1, :] = d0
        dlg_o[1:2, :] = -d0

    return _pcall(body, "params_bwd", (1,),
                  [_full(GS), _full(GS), _full((S5_GROUPS, 1)), _full(GSC), _full(GSC), _full((2, HG_WIDTH)),
                   _full(GS), _full(GS), _full(GSC), _full(GSC), _full((1, HG_WIDTH))],
                  [_full(GS), _full(GS), _full((S5_GROUPS, 1)), _full(GSC), _full(GSC), _full((2, HG_WIDTH))],
                  [_sds(GS), _sds(GS), _sds((S5_GROUPS, 1)), _sds(GSC), _sds(GSC), _sds((2, HG_WIDTH))],
                  scratch=[pltpu.VMEM(GS, F32), pltpu.VMEM(GS, F32)],
                  )(a_re, a_im, ldt, bt_re, bt_im, logits, dlr, dli, dbbr, dbbi, dlb)


def _band_blocks(m):
    g, r, c = m.shape
    gb = g // S5_BANDS
    m4 = m.astype(BF16).reshape(S5_BANDS, gb, r, c)
    on_diag = jnp.eye(gb, dtype=bool)[None, :, None, :, None]
    return jnp.where(on_diag, m4[:, :, :, None, :], 0).reshape(S5_BANDS, gb * r, gb * c)


def _diag_blocks(band, r, c):
    g, nb = band.shape[0] // r, band.shape[1] // c
    on_diag = (jnp.arange(g) % nb)[:, None, None, None] == jnp.arange(nb)[None, None, :, None]
    return jnp.sum(jnp.where(on_diag, band.reshape(g, r, nb, c), 0.0), axis=2)


def _in_proj(x, g_mix, w_in, tm):
    t = x.shape[0]

    def body(x_ref, g_ref, w_ref, u_ref, za_ref, zh_ref, zg_ref):
        xv = x_ref[...]
        r = lax.rsqrt(jnp.mean(xv * xv, axis=-1, keepdims=True) + EPS)
        u = (xv * r * g_ref[...]).astype(BF16)
        u_ref[...] = u
        za_ref[...] = _dot_nt(u, w_ref[0:S5_WIDTH, :])
        zh_ref[...] = _dot_nt(u, w_ref[S5_WIDTH:S5_WIDTH + 4 * HG_WIDTH, :])
        zg_ref[...] = _dot_nt(u, w_ref[S5_WIDTH + 4 * HG_WIDTH:, :]).astype(BF16)

    row = lambda w: pl.BlockSpec((tm, w), lambda i: (i, 0))
    return _pcall(body, "in_proj", (t // tm,),
                  [row(D_MODEL), _full((1, D_MODEL)), _full((N_IN, D_MODEL))],
                  [row(D_MODEL), row(S5_WIDTH), row(4 * HG_WIDTH), row(2 * D_MODEL)],
                  [_sds((t, D_MODEL), BF16), _sds((t, S5_WIDTH)), _sds((t, 4 * HG_WIDTH)),
                   _sds((t, 2 * D_MODEL), BF16)],
                  )(x, g_mix, w_in)


S5_LANES = 512
S5_BANDS = 4


def _band(q):
    return (slice(q * S5_WIDTH // S5_BANDS, (q + 1) * S5_WIDTH // S5_BANDS),
            slice(q * S5_N // S5_BANDS, (q + 1) * S5_N // S5_BANDS))


def _im(st):
    return slice(S5_N + st.start, S5_N + st.stop)


SCAN_UNROLL = 8


def _complex_scan(buf_ref, lam_ref, st_ref, nb, ts, reverse):
    lanes = [slice(cc * S5_LANES, (cc + 1) * S5_LANES) for cc in range(S5_N // S5_LANES)]
    chains = [(b, re) for b in range(nb) for re in lanes]
    nch = len(chains)
    wr = {re.start: lam_ref[0:1, re] for re in lanes}
    wi = {re.start: -lam_ref[1:2, re] if reverse else lam_ref[1:2, re] for re in lanes}

    def block(ib, carry):
        vr, vi = list(carry[:nch]), list(carry[nch:])
        first = ts - SCAN_UNROLL - ib * SCAN_UNROLL if reverse else ib * SCAN_UNROLL
        first = pl.multiple_of(first, SCAN_UNROLL)
        for k in range(SCAN_UNROLL):
            row = pl.ds(first + (SCAN_UNROLL - 1 - k if reverse else k), 1)
            for c, (b, re) in enumerate(chains):
                nr = wr[re.start] * vr[c] - wi[re.start] * vi[c] + buf_ref[b, row, re]
                ni = wr[re.start] * vi[c] + wi[re.start] * vr[c] + buf_ref[b, row, _im(re)]
                buf_ref[b, row, re] = nr
                buf_ref[b, row, _im(re)] = ni
                vr[c], vi[c] = nr, ni
        return tuple(vr + vi)

    init = tuple(st_ref[b, 0:1, re] for b, re in chains) + tuple(st_ref[b, 1:2, re] for b, re in chains)
    last = lax.fori_loop(0, ts // SCAN_UNROLL, block, init)
    for c, (b, re) in enumerate(chains):
        st_ref[b, 0:1, re] = last[c]
        st_ref[b, 1:2, re] = last[nch + c]


BAND_CH = S5_WIDTH // S5_BANDS
BAND_ST = S5_N // S5_BANDS


def _s5_fwd(za, b_bands, lam, c_bands, dskip, nb, seq, ts):
    nts = seq // ts

    def body(za_ref, br_ref, bi_ref, lam_ref, cr_ref, ci_ref, d_ref, xs_ref, y_ref, buf_ref, st_ref):
        @pl.when(pl.program_id(0) == 0)
        def _():
            st_ref[...] = jnp.zeros_like(st_ref)

        for b in range(nb):
            zav = za_ref[b]
            for q in range(S5_BANDS):
                ch, st = _band(q)
                buf_ref[b, :, st] = _dot(zav[:, ch], br_ref[q])
                buf_ref[b, :, _im(st)] = _dot(zav[:, ch], bi_ref[q])
        _complex_scan(buf_ref, lam_ref, st_ref, nb, ts, reverse=False)
        for b in range(nb):
            zav = za_ref[b]
            xs_ref[b] = buf_ref[b].astype(BF16)
            for q in range(S5_BANDS):
                ch, st = _band(q)
                y_ref[b, :, ch] = (_dot(xs_ref[b, :, st], cr_ref[q]) + _dot(xs_ref[b, :, _im(st)], ci_ref[q])
                                   + d_ref[:, ch] * zav[:, ch])

    tok = lambda w: pl.BlockSpec((nb, ts, w), lambda j: (0, j, 0))
    to_st, to_ch = _full((S5_BANDS, BAND_CH, BAND_ST)), _full((S5_BANDS, BAND_ST, BAND_CH))
    return _pcall(body, "s5_fwd", (nts,),
                  [tok(S5_WIDTH), to_st, to_st, _full((2, S5_N)), to_ch, to_ch, _full((1, S5_WIDTH))],
                  [tok(2 * S5_N), tok(S5_WIDTH)],
                  [_sds((nb, seq, 2 * S5_N), BF16), _sds((nb, seq, S5_WIDTH))],
                  scratch=[pltpu.VMEM((nb, ts, 2 * S5_N), F32), pltpu.VMEM((nb, 2, S5_N), F32)],
                  )(za, *b_bands, lam, *c_bands, dskip)


def _hgrn_gates(zq, zf, lbh):
    sf = _sigmoid(zf)
    f = lbh + (1.0 - lbh) * sf
    sq = _sigmoid(zq)
    qa = zq * sq * QSCALE
    bc = _cumsum_rows(jnp.log(f))
    bm = bc[CHUNK // 2 - 1:CHUNK // 2, :]
    bl = bc[CHUNK - 1:CHUNK, :]
    return sf, f, sq, qa, bc, bm, bl


def _hgrn_fwd(zh, lb, nb, seq):
    nc = seq // CHUNK

    def body(zh_ref, lb_ref, o_ref, sts_ref, st_ref):
        @pl.when(pl.program_id(0) == 0)
        def _():
            st_ref[...] = jnp.zeros_like(st_ref)

        causal = (lax.broadcasted_iota(jnp.int32, (CHUNK, CHUNK), 0)
                  >= lax.broadcasted_iota(jnp.int32, (CHUNK, CHUNK), 1))
        for b in range(nb):
            for h in range(HG_HEADS):
                hs = slice(h * HG_HEAD, (h + 1) * HG_HEAD)
                zq = zh_ref[b, :, h * HG_HEAD:(h + 1) * HG_HEAD]
                zf = zh_ref[b, :, HG_WIDTH + h * HG_HEAD:HG_WIDTH + (h + 1) * HG_HEAD]
                zi = zh_ref[b, :, 2 * HG_WIDTH + h * HG_HEAD:2 * HG_WIDTH + (h + 1) * HG_HEAD]
                _, f, _, qa, bc, bm, bl = _hgrn_gates(zq, zf, lb_ref[:, hs])
                k = 1.0 - f
                qt = qa * jnp.exp(bc - bm)
                kt = k * jnp.exp(bm - bc)
                qb = qa * jnp.exp(bc)
                kd = k * jnp.exp(bl - bc)
                st = st_ref[b, h]
                sts_ref[b, 0, h] = st
                a = jnp.where(causal, _dot_nt(qt, kt), 0.0)
                o_ref[b, :, hs] = _dot(a, zi) + _dot_nt(qb, st)
                st_ref[b, h] = st * jnp.exp(bl) + _dot_tn(zi, kd)

    return _pcall(body, "hgrn_fwd", (nc,),
                  [pl.BlockSpec((nb, CHUNK, 4 * HG_WIDTH), lambda c: (0, c, 0)), _full((1, HG_WIDTH))],
                  [pl.BlockSpec((nb, CHUNK, HG_WIDTH), lambda c: (0, c, 0)),
                   pl.BlockSpec((nb, 1, HG_HEADS, HG_HEAD, HG_HEAD), lambda c: (0, c, 0, 0, 0))],
                  [_sds((nb, seq, HG_WIDTH)), _sds((nb, nc, HG_HEADS, HG_HEAD, HG_HEAD))],
                  scratch=[pltpu.VMEM((nb, HG_HEADS, HG_HEAD, HG_HEAD), F32)])(zh, lb)


def _head_rms(o):
    parts = []
    for h in range(HG_HEADS):
        oh = o[:, h * HG_HEAD:(h + 1) * HG_HEAD]
        r = lax.rsqrt(jnp.mean(oh * oh, axis=-1, keepdims=True) + EPS)
        parts.append(jnp.broadcast_to(r, oh.shape))
    return jnp.concatenate(parts, axis=1)


def _head_mean(v):
    parts = []
    for h in range(HG_HEADS):
        vh = v[:, h * HG_HEAD:(h + 1) * HG_HEAD]
        parts.append(jnp.broadcast_to(jnp.mean(vh, axis=-1, keepdims=True), vh.shape))
    return jnp.concatenate(parts, axis=1)


def _mix_fwd(x, y0, o, zh, zgt, w_glu, b_glu, gain, w_pa, w_pb, w_out, g_ffn, tm):
    t = x.shape[0]

    def body(x_ref, y0_ref, o_ref, zg_ref, zgt_ref, wglu_ref, bglu_ref, gain_ref, wpa_ref, wpb_ref, wout_ref,
             gffn_ref, x1_ref, u2_ref, pa_ref, pb_ref, ya2_ref, yb_ref):
        ya1 = _gelu(y0_ref[...])
        s = _sigmoid(_dot(ya1, wglu_ref[...]) + bglu_ref[...])
        ya2 = (ya1 * s).astype(BF16)
        ov = o_ref[...]
        zg = zg_ref[...]
        yb = (ov * _head_rms(ov) * gain_ref[...] * (zg * _sigmoid(zg))).astype(BF16)
        ya2_ref[...] = ya2
        yb_ref[...] = yb
        pa = jnp.dot(ya2, wpa_ref[...], preferred_element_type=F32)
        pb = jnp.dot(yb, wpb_ref[...], preferred_element_type=F32)
        pa_ref[...] = pa.astype(BF16)
        pb_ref[...] = pb.astype(BF16)
        m = (_sigmoid(zgt_ref[:, 0:D_MODEL].astype(F32)) * pa
             + _sigmoid(zgt_ref[:, D_MODEL:].astype(F32)) * pb)
        x1 = x_ref[...] + _dot(m, wout_ref[...])
        x1_ref[...] = x1
        r = lax.rsqrt(jnp.mean(x1 * x1, axis=-1, keepdims=True) + EPS)
        u2_ref[...] = (x1 * r * gffn_ref[...]).astype(BF16)

    row = lambda w: pl.BlockSpec((tm, w), lambda i: (i, 0))
    return _pcall(body, "mix_fwd", (t // tm,),
                  [row(D_MODEL), row(S5_WIDTH), row(HG_WIDTH), pl.BlockSpec((tm, HG_WIDTH), lambda i: (i, 3)),
                   row(2 * D_MODEL), _full((S5_WIDTH, S5_WIDTH)), _full((1, S5_WIDTH)), _full((1, HG_WIDTH)),
                   _full((S5_WIDTH, D_MODEL)), _full((HG_WIDTH, D_MODEL)), _full((D_MODEL, D_MODEL)),
                   _full((1, D_MODEL))],
                  [row(D_MODEL), row(D_MODEL), row(D_MODEL), row(D_MODEL), row(S5_WIDTH), row(HG_WIDTH)],
                  [_sds((t, D_MODEL)), _sds((t, D_MODEL), BF16), _sds((t, D_MODEL), BF16), _sds((t, D_MODEL), BF16),
                   _sds((t, S5_WIDTH), BF16), _sds((t, HG_WIDTH), BF16)],
                  )(x, y0, o, zh, zgt, w_glu, b_glu, gain, w_pa, w_pb, w_out, g_ffn)


FF_COLS = 256
FF_UP_TILE = 1408


def _ffn_up(u2, w_up, tm):
    t = u2.shape[0]
    n = 2 * D_FF

    def body(u_ref, w_ref, h_ref):
        h_ref[...] = _dot_nt(u_ref[...], w_ref[...]).astype(BF16)

    return _pcall(body, "ffn_up", (n // FF_UP_TILE, t // tm),
                  [pl.BlockSpec((tm, D_MODEL), lambda j, i: (i, 0)),
                   pl.BlockSpec((FF_UP_TILE, D_MODEL), lambda j, i: (j, 0))],
                  pl.BlockSpec((tm, FF_UP_TILE), lambda j, i: (i, j)),
                  _sds((t, n), BF16))(u2, w_up)


HALO = 16


def _shift_matrix(tm):
    r = lax.broadcasted_iota(jnp.int32, (tm, tm), 0)
    c = lax.broadcasted_iota(jnp.int32, (tm, tm), 1)
    return jnp.where(r == c + 1, 1.0, 0.0).astype(BF16)


def _conv_cols(h_ref, halo_ref, valid, wc_ref, bc_ref, c0):
    cs = slice(c0, c0 + FF_COLS)
    cur = h_ref[:, cs].astype(F32)
    prev = jnp.where(valid, halo_ref[:, cs].astype(F32), 0.0)
    full = jnp.concatenate([prev, cur], axis=0)
    h1 = pltpu.roll(full, 1, axis=0)[HALO:]
    h2 = pltpu.roll(full, 2, axis=0)[HALO:]
    return h2 * wc_ref[0:1, cs] + h1 * wc_ref[1:2, cs] + cur * wc_ref[2:3, cs] + bc_ref[:, cs]


def _ffn_down_loss(h, x1, tgt, w_conv, b_conv, w_down, g_final, seq, tm):
    t = h.shape[0]
    tps = seq // tm
    n = 2 * D_FF

    def body(h_ref, halo_ref, x1_ref, tgt_ref, wc_ref, bc_ref, wd_ref, gf_ref,
             hc_ref, a_ref, dx2_ref, dx2b_ref, loss_ref, dgf_ref):
        i = pl.program_id(0)

        @pl.when(i == 0)
        def _():
            loss_ref[...] = jnp.zeros_like(loss_ref)
            dgf_ref[...] = jnp.zeros_like(dgf_ref)

        valid = (i % tps) != 0
        x2 = x1_ref[...]
        for j in range(D_FF // FF_COLS):
            gate = _conv_cols(h_ref, halo_ref, valid, wc_ref, bc_ref, j * FF_COLS)
            val = _conv_cols(h_ref, halo_ref, valid, wc_ref, bc_ref, D_FF + j * FF_COLS)
            hc_ref[:, j * FF_COLS:(j + 1) * FF_COLS] = gate.astype(BF16)
            hc_ref[:, D_FF + j * FF_COLS:D_FF + (j + 1) * FF_COLS] = val.astype(BF16)
            a = (gate * _sigmoid(gate) * val).astype(BF16)
            a_ref[:, j * FF_COLS:(j + 1) * FF_COLS] = a
            x2 = x2 + jnp.dot(a, wd_ref[j * FF_COLS:(j + 1) * FF_COLS, :], preferred_element_type=F32)
        r = lax.rsqrt(jnp.mean(x2 * x2, axis=-1, keepdims=True) + EPS)
        xn = x2 * r
        g = gf_ref[...]
        e = xn * g - tgt_ref[...]
        loss_ref[...] += (0.5 / D_MODEL) * jnp.sum(e * e).reshape(1, 1)
        dy = e * (1.0 / D_MODEL)
        dgf_ref[...] += jnp.sum(dy * xn, axis=0, keepdims=True)
        dxn = dy * g
        dx2 = r * (dxn - xn * jnp.mean(dxn * xn, axis=-1, keepdims=True))
        dx2_ref[...] = dx2
        dx2b_ref[...] = dx2.astype(BF16)

    row = lambda w: pl.BlockSpec((tm, w), lambda i: (i, 0))
    halo = pl.BlockSpec((HALO, n), lambda i: (jnp.maximum(i * (tm // HALO) - 1, 0), 0))
    return _pcall(body, "ffn_down_loss", (t // tm,),
                  [row(n), halo, row(D_MODEL), row(D_MODEL), _full((CONV_W, n)), _full((1, n)),
                   _full((D_FF, D_MODEL)), _full((1, D_MODEL))],
                  [row(n), row(D_FF), row(D_MODEL), row(D_MODEL), _full((1, 1)), _full((1, D_MODEL))],
                  [_sds((t, n), BF16), _sds((t, D_FF), BF16), _sds((t, D_MODEL)), _sds((t, D_MODEL), BF16),
                   _sds((1, 1)), _sds((1, D_MODEL))],
                  )(h, h, x1, tgt, w_conv, b_conv, w_down, g_final)


def _wgrad(a, b, name, tn, out_dtype=F32, band=None, after=None):
    t, m = a.shape
    n = b.shape[1] if band is None else band
    nbands = 1 if band is None else b.shape[1] // band
    after = b if after is None else after

    def body(a_ref, b_ref, after_ref, o_ref):
        o_ref[...] = _dot_tn(a_ref[...], b_ref[...]).astype(out_dtype)

    return _pcall(body, name, (m // tn,),
                  [pl.BlockSpec((t, tn), lambda i: (0, i)), pl.BlockSpec((t, n), lambda i: (0, i % nbands)),
                   pl.BlockSpec(memory_space=pl.ANY)],
                  pl.BlockSpec((tn, n), lambda i: (i, 0)), _sds((m, n), out_dtype))(a, b, after)


def _ffn_bwd_act(dx2b, hc, w_down, tm):
    t = hc.shape[0]
    n = 2 * D_FF

    def body(dx2_ref, hc_ref, wd_ref, dhc_ref, dbc_ref):
        @pl.when(pl.program_id(0) == 0)
        def _():
            dbc_ref[...] = jnp.zeros_like(dbc_ref)

        dx2 = dx2_ref[...]
        for j in range(D_FF // FF_COLS):
            gs = slice(j * FF_COLS, (j + 1) * FF_COLS)
            vs = slice(D_FF + j * FF_COLS, D_FF + (j + 1) * FF_COLS)
            gate = hc_ref[:, gs].astype(F32)
            val = hc_ref[:, vs].astype(F32)
            da = _dot_nt(dx2, wd_ref[gs, :])
            sg = _sigmoid(gate)
            dgate = da * val * (sg * (1.0 + gate * (1.0 - sg)))
            dval = da * (gate * sg)
            dhc_ref[:, gs] = dgate.astype(BF16)
            dhc_ref[:, vs] = dval.astype(BF16)
            dbc_ref[:, gs] += jnp.sum(dgate, axis=0, keepdims=True)
            dbc_ref[:, vs] += jnp.sum(dval, axis=0, keepdims=True)

    row = lambda w: pl.BlockSpec((tm, w), lambda i: (i, 0))
    return _pcall(body, "ffn_bwd_act", (t // tm,),
                  [row(D_MODEL), row(n), _full((D_FF, D_MODEL))],
                  [row(n), _full((1, n))],
                  [_sds((t, n), BF16), _sds((1, n))],
                  )(dx2b, hc, w_down)


def _ffn_bwd_up(dhc, h, dx2, x1, w_conv, w_up, g_ffn, seq, tm):
    t = dhc.shape[0]
    tps = seq // tm
    n = 2 * D_FF
    last = t // HALO - 1

    def body(dhc_ref, halo_ref, h_ref, dx2_ref, x1_ref, wc_ref, wu_ref, gf_ref,
             dh_ref, dx1_ref, dx1b_ref, dgf_ref, dwc_ref):
        i = pl.program_id(0)

        @pl.when(i == 0)
        def _():
            dgf_ref[...] = jnp.zeros_like(dgf_ref)
            dwc_ref[...] = jnp.zeros_like(dwc_ref)

        valid = ((i + 1) % tps) != 0
        du2 = jnp.zeros((tm, D_MODEL), F32)
        for j in range(n // FF_COLS):
            cs = slice(j * FF_COLS, (j + 1) * FF_COLS)
            cur = dhc_ref[:, cs].astype(F32)
            nxt = jnp.where(valid, halo_ref[:, cs].astype(F32), 0.0)
            full = jnp.concatenate([cur, nxt], axis=0)
            d1 = pltpu.roll(full, tm + HALO - 1, axis=0)[:tm]
            d2 = pltpu.roll(full, tm + HALO - 2, axis=0)[:tm]
            dh = (cur * wc_ref[2:3, cs] + d1 * wc_ref[1:2, cs] + d2 * wc_ref[0:1, cs]).astype(BF16)
            dh_ref[:, cs] = dh
            du2 = du2 + _dot(dh, wu_ref[cs, :])
            hv = h_ref[:, cs].astype(F32)
            dwc_ref[0:1, cs] += jnp.sum(hv * d2, axis=0, keepdims=True)
            dwc_ref[1:2, cs] += jnp.sum(hv * d1, axis=0, keepdims=True)
            dwc_ref[2:3, cs] += jnp.sum(hv * cur, axis=0, keepdims=True)
        x1 = x1_ref[...]
        r = lax.rsqrt(jnp.mean(x1 * x1, axis=-1, keepdims=True) + EPS)
        xn = x1 * r
        dgf_ref[...] += jnp.sum(du2 * xn, axis=0, keepdims=True)
        dxn = du2 * gf_ref[...]
        dx1 = dx2_ref[...] + r * (dxn - xn * jnp.mean(dxn * xn, axis=-1, keepdims=True))
        dx1_ref[...] = dx1
        dx1b_ref[...] = dx1.astype(BF16)

    row = lambda w: pl.BlockSpec((tm, w), lambda i: (i, 0))
    halo = pl.BlockSpec((HALO, n), lambda i: (jnp.minimum((i + 1) * (tm // HALO), last), 0))
    return _pcall(body, "ffn_bwd_up", (t // tm,),
                  [row(n), halo, row(n), row(D_MODEL), row(D_MODEL), _full((CONV_W, n)), _full((n, D_MODEL)),
                   _full((1, D_MODEL))],
                  [row(n), row(D_MODEL), row(D_MODEL), _full((1, D_MODEL)), _full((CONV_W, n))],
                  [_sds((t, n), BF16), _sds((t, D_MODEL)), _sds((t, D_MODEL), BF16), _sds((1, D_MODEL)),
                   _sds((CONV_W, n))],
                  )(dhc, dhc, h, dx2, x1, w_conv, w_up, g_ffn)


def _mix_bwd(dx1, y0, o, zh, zgt, pa, pb, w_glu, b_glu, gain, w_pa, w_pb, w_out, tm):
    t = dx1.shape[0]

    def body(dx1_ref, y0_ref, o_ref, zg_ref, zgt_ref, pa_ref, pb_ref, wglu_ref, bglu_ref, gain_ref, wpa_ref,
             wpb_ref, wout_ref,
             dy0_ref, do_ref, dzg_ref, dzgt_ref, m_ref, dpa_ref, dpb_ref, ya1_ref, dpre_ref, dbglu_ref, dgain_ref):
        @pl.when(pl.program_id(0) == 0)
        def _():
            dbglu_ref[...] = jnp.zeros_like(dbglu_ref)
            dgain_ref[...] = jnp.zeros_like(dgain_ref)

        dm = _dot_nt(dx1_ref[...], wout_ref[...])
        sga = _sigmoid(zgt_ref[:, 0:D_MODEL].astype(F32))
        sgb = _sigmoid(zgt_ref[:, D_MODEL:].astype(F32))
        pa = pa_ref[...].astype(F32)
        pb = pb_ref[...].astype(F32)
        m_ref[...] = (sga * pa + sgb * pb).astype(BF16)
        dzgt_ref[:, 0:D_MODEL] = (dm * pa * sga * (1.0 - sga)).astype(BF16)
        dzgt_ref[:, D_MODEL:] = (dm * pb * sgb * (1.0 - sgb)).astype(BF16)
        dpa = (dm * sga).astype(BF16)
        dpb = (dm * sgb).astype(BF16)
        dpa_ref[...] = dpa
        dpb_ref[...] = dpb
        dya2 = _dot_nt(dpa, wpa_ref[...])
        dyb = _dot_nt(dpb, wpb_ref[...])
        y0 = y0_ref[...]
        ya1 = _gelu(y0)
        ya1_ref[...] = ya1.astype(BF16)
        s = _sigmoid(_dot(ya1, wglu_ref[...]) + bglu_ref[...])
        dpre = dya2 * ya1 * s * (1.0 - s)
        dpre_ref[...] = dpre.astype(BF16)
        dbglu_ref[...] += jnp.sum(dpre, axis=0, keepdims=True)
        dya1 = dya2 * s + _dot_nt(dpre, wglu_ref[...])
        dy0_ref[...] = dya1 * _gelu_grad(y0)
        ov = o_ref[...]
        zg = zg_ref[...]
        oh = ov * _head_rms(ov)
        on = oh * gain_ref[...]
        sz = _sigmoid(zg)
        dzg_ref[...] = (dyb * on * (sz * (1.0 + zg * (1.0 - sz)))).astype(BF16)
        don = dyb * (zg * sz)
        dgain_ref[...] += jnp.sum(don * oh, axis=0, keepdims=True)
        doh = don * gain_ref[...]
        do_ref[...] = _head_rms(ov) * (doh - oh * _head_mean(doh * oh))

    row = lambda w: pl.BlockSpec((tm, w), lambda i: (i, 0))
    return _pcall(body, "mix_bwd", (t // tm,),
                  [row(D_MODEL), row(S5_WIDTH), row(HG_WIDTH), pl.BlockSpec((tm, HG_WIDTH), lambda i: (i, 3)),
                   row(2 * D_MODEL), row(D_MODEL), row(D_MODEL), _full((S5_WIDTH, S5_WIDTH)), _full((1, S5_WIDTH)),
                   _full((1, HG_WIDTH)), _full((S5_WIDTH, D_MODEL)), _full((HG_WIDTH, D_MODEL)),
                   _full((D_MODEL, D_MODEL))],
                  [row(S5_WIDTH), row(HG_WIDTH), row(HG_WIDTH), row(2 * D_MODEL), row(D_MODEL), row(D_MODEL),
                   row(D_MODEL), row(S5_WIDTH), row(S5_WIDTH), _full((1, S5_WIDTH)), _full((1, HG_WIDTH))],
                  [_sds((t, S5_WIDTH)), _sds((t, HG_WIDTH)), _sds((t, HG_WIDTH), BF16), _sds((t, 2 * D_MODEL), BF16),
                   _sds((t, D_MODEL), BF16), _sds((t, D_MODEL), BF16), _sds((t, D_MODEL), BF16),
                   _sds((t, S5_WIDTH), BF16), _sds((t, S5_WIDTH), BF16), _sds((1, S5_WIDTH)), _sds((1, HG_WIDTH))],
                  )(dx1, y0, o, zh, zgt, pa, pb, w_glu, b_glu, gain, w_pa, w_pb, w_out)


def _s5_bwd(dy0, za, xs, c_bands, b_bands, lam, dskip, nb, seq, ts):
    nts = seq // ts

    def body(dy0_ref, za_ref, xs_ref, halo_ref, cr_ref, ci_ref, br_ref, bi_ref, lam_ref, d_ref,
             dza_ref, a_ref, dlam_ref, dd_ref, acc_ref, st_ref):
        j = pl.program_id(0)

        @pl.when(j == 0)
        def _():
            dlam_ref[...] = jnp.zeros_like(dlam_ref)
            dd_ref[...] = jnp.zeros_like(dd_ref)
            st_ref[...] = jnp.zeros_like(st_ref)

        for b in range(nb):
            dy0 = dy0_ref[b]
            for q in range(S5_BANDS):
                ch, st = _band(q)
                acc_ref[b, :, st] = _dot(dy0[:, ch], cr_ref[q])
                acc_ref[b, :, _im(st)] = _dot(dy0[:, ch], ci_ref[q])
        _complex_scan(acc_ref, lam_ref, st_ref, nb, ts, reverse=True)
        shift = _shift_matrix(ts)
        top = lax.broadcasted_iota(jnp.int32, (SUBLANES, S5_LANES), 0) == 0
        for b in range(nb):
            a_ref[b] = acc_ref[b].astype(BF16)
            first = jnp.where(j == nts - 1, 0.0, halo_ref[b, HALO - 1:HALO, :].astype(F32))

            def shifted(cols):
                xp = jnp.dot(shift, xs_ref[b, :, cols], preferred_element_type=F32)
                return jnp.concatenate([xp[:SUBLANES] + jnp.where(top, first[:, cols], 0.0), xp[SUBLANES:]], axis=0)

            for cc in range(S5_N // S5_LANES):
                re = slice(cc * S5_LANES, (cc + 1) * S5_LANES)
                ar, ai, xr, xi = acc_ref[b, :, re], acc_ref[b, :, _im(re)], shifted(re), shifted(_im(re))
                dlam_ref[0:1, re] += jnp.sum(ar * xr + ai * xi, axis=0, keepdims=True)
                dlam_ref[1:2, re] += jnp.sum(ai * xr - ar * xi, axis=0, keepdims=True)
            dy0 = dy0_ref[b]
            for q in range(S5_BANDS):
                ch, st = _band(q)
                dza_ref[b, :, ch] = (_dot(a_ref[b, :, st], br_ref[q]) + _dot(a_ref[b, :, _im(st)], bi_ref[q])
                                     + d_ref[:, ch] * dy0[:, ch]).astype(BF16)
            dd_ref[...] += jnp.sum(dy0 * za_ref[b], axis=0, keepdims=True)

    tile = lambda j: nts - 1 - j
    tok = lambda w: pl.BlockSpec((nb, ts, w), lambda j: (0, tile(j), 0))
    halo = pl.BlockSpec((nb, HALO, 2 * S5_N), lambda j: (0, jnp.maximum(tile(j) * (ts // HALO) - 1, 0), 0))
    to_st, to_ch = _full((S5_BANDS, BAND_CH, BAND_ST)), _full((S5_BANDS, BAND_ST, BAND_CH))
    return _pcall(body, "s5_bwd", (nts,),
                  [tok(S5_WIDTH), tok(S5_WIDTH), tok(2 * S5_N), halo, to_st, to_st, to_ch, to_ch,
                   _full((2, S5_N)), _full((1, S5_WIDTH))],
                  [tok(S5_WIDTH), tok(2 * S5_N), _full((2, S5_N)), _full((1, S5_WIDTH))],
                  [_sds((nb, seq, S5_WIDTH), BF16), _sds((nb, seq, 2 * S5_N), BF16), _sds((2, S5_N)),
                   _sds((1, S5_WIDTH))],
                  scratch=[pltpu.VMEM((nb, ts, 2 * S5_N), F32), pltpu.VMEM((nb, 2, S5_N), F32)],
                  )(dy0, za, xs, xs, *c_bands, *b_bands, lam, dskip)


def _hgrn_bwd(zh, do, sts, lb, nb, seq):
    nc = seq // CHUNK

    def body(zh_ref, do_ref, sts_ref, lb_ref, dz_ref, dlb_ref, dst_ref):
        @pl.when(pl.program_id(0) == 0)
        def _():
            dst_ref[...] = jnp.zeros_like(dst_ref)
            dlb_ref[...] = jnp.zeros_like(dlb_ref)

        row = lax.broadcasted_iota(jnp.int32, (CHUNK, CHUNK), 0)
        causal = row >= lax.broadcasted_iota(jnp.int32, (CHUNK, CHUNK), 1)
        last_row = lax.broadcasted_iota(jnp.int32, (CHUNK, HG_HEAD), 0) == CHUNK - 1
        for b in range(nb):
            for h in range(HG_HEADS):
                hs = slice(h * HG_HEAD, (h + 1) * HG_HEAD)
                zq = zh_ref[b, :, h * HG_HEAD:(h + 1) * HG_HEAD]
                zf = zh_ref[b, :, HG_WIDTH + h * HG_HEAD:HG_WIDTH + (h + 1) * HG_HEAD]
                zi = zh_ref[b, :, 2 * HG_WIDTH + h * HG_HEAD:2 * HG_WIDTH + (h + 1) * HG_HEAD]
                lbh = lb_ref[:, hs]
                sf, f, sq, qa, bc, bm, bl = _hgrn_gates(zq, zf, lbh)
                k = 1.0 - f
                e_qt = jnp.exp(bc - bm)
                e_kt = jnp.exp(bm - bc)
                e_b = jnp.exp(bc)
                e_kd = jnp.exp(bl - bc)
                e_l = jnp.exp(bl)
                qt, kt, qb, kd = qa * e_qt, k * e_kt, qa * e_b, k * e_kd
                a = jnp.where(causal, _dot_nt(qt, kt), 0.0)
                st = sts_ref[b, 0, h]
                dst = dst_ref[b, h]
                dov = do_ref[b, :, hs]
                da = jnp.where(causal, _dot_nt(dov, zi), 0.0)
                dqt = _hdot(da, kt)
                dkt = _hdot_tn(da, qt)
                dqb = _dot(dov, st)
                di = _dot_tn(a, dov) + _dot_nt(kd, dst)
                dkd = _dot(zi, dst)
                de_l = jnp.sum(dst * st, axis=0, keepdims=True)
                dst_ref[b, h] = dst * e_l + _dot_tn(dov, qb)
                dqa = dqt * e_qt + dqb * e_b
                dk = dkt * e_kt + dkd * e_kd
                dbl = jnp.sum(dkd * kd, axis=0, keepdims=True) + de_l * e_l
                db = dqt * qt - dkt * kt + dqb * qb - dkd * kd + jnp.where(last_row, dbl, 0.0)
                df = _cumsum_rows(db, reverse=True) / f - dk
                dzq = dqa * QSCALE * (sq * (1.0 + zq * (1.0 - sq)))
                dzf = df * (1.0 - lbh) * sf * (1.0 - sf)
                dz_ref[b, :, h * HG_HEAD:(h + 1) * HG_HEAD] = dzq.astype(BF16)
                dz_ref[b, :, HG_WIDTH + h * HG_HEAD:HG_WIDTH + (h + 1) * HG_HEAD] = dzf.astype(BF16)
                dz_ref[b, :, 2 * HG_WIDTH + h * HG_HEAD:2 * HG_WIDTH + (h + 1) * HG_HEAD] = di.astype(BF16)
                dlb_ref[:, hs] += jnp.sum(df * (1.0 - sf), axis=0, keepdims=True)

    rev = lambda c: nc - 1 - c
    return _pcall(body, "hgrn_bwd", (nc,),
                  [pl.BlockSpec((nb, CHUNK, 4 * HG_WIDTH), lambda c: (0, rev(c), 0)),
                   pl.BlockSpec((nb, CHUNK, HG_WIDTH), lambda c: (0, rev(c), 0)),
                   pl.BlockSpec((nb, 1, HG_HEADS, HG_HEAD, HG_HEAD), lambda c: (0, rev(c), 0, 0, 0)),
                   _full((1, HG_WIDTH))],
                  [pl.BlockSpec((nb, CHUNK, 3 * HG_WIDTH), lambda c: (0, rev(c), 0)), _full((1, HG_WIDTH))],
                  [_sds((nb, seq, 3 * HG_WIDTH), BF16), _sds((1, HG_WIDTH))],
                  scratch=[pltpu.VMEM((nb, HG_HEADS, HG_HEAD, HG_HEAD), F32)])(zh, do, sts, lb)


def _in_proj_bwd(dza, dzh, dzg, dzgt, dx1, x, g_mix, w_in, tm):
    t = x.shape[0]

    def body(dza_ref, dzh_ref, dzg_ref, dzgt_ref, dx1_ref, x_ref, g_ref, w_ref, dz_ref, dx_ref, dg_ref):
        @pl.when(pl.program_id(0) == 0)
        def _():
            dg_ref[...] = jnp.zeros_like(dg_ref)

        c1, c2, c3 = S5_WIDTH, S5_WIDTH + 3 * HG_WIDTH, S5_WIDTH + 4 * HG_WIDTH
        dz_ref[:, 0:c1] = dza_ref[...]
        dz_ref[:, c1:c2] = dzh_ref[...]
        dz_ref[:, c2:c3] = dzg_ref[...]
        dz_ref[:, c3:] = dzgt_ref[...]
        du = _dot(dz_ref[...], w_ref[...])
        xv = x_ref[...]
        r = lax.rsqrt(jnp.mean(xv * xv, axis=-1, keepdims=True) + EPS)
        xn = xv * r
        dg_ref[...] += jnp.sum(du * xn, axis=0, keepdims=True)
        dxn = du * g_ref[...]
        dx_ref[...] = dx1_ref[...] + r * (dxn - xn * jnp.mean(dxn * xn, axis=-1, keepdims=True))

    row = lambda w: pl.BlockSpec((tm, w), lambda i: (i, 0))
    return _pcall(body, "in_proj_bwd", (t // tm,),
                  [row(S5_WIDTH), row(3 * HG_WIDTH), row(HG_WIDTH), row(2 * D_MODEL), row(D_MODEL), row(D_MODEL),
                   _full((1, D_MODEL)), _full((N_IN, D_MODEL))],
                  [row(N_IN), row(D_MODEL), _full((1, D_MODEL))],
                  [_sds((t, N_IN), BF16), _sds((t, D_MODEL)), _sds((1, D_MODEL))],
                  )(dza, dzh, dzg, dzgt, dx1, x, g_mix, w_in)


def _tie(*arrays):
    return jnp.zeros((SUBLANES, 128), F32) + sum(a.reshape(-1)[0].astype(F32) for a in arrays)


def _after(value, token):
    return value + token[0, 0]


def _local_step(x3, tgt3, weights, sp, emit, emit_small):
    nb, seq, _ = x3.shape
    t = nb * seq
    tm = _token_tile(seq)
    x = x3.reshape(t, D_MODEL)
    tgt = tgt3.reshape(t, D_MODEL)
    row = lambda v: v.reshape(1, -1)

    a_re, a_im, b_re, b_im = sp["s5_a_re"], sp["s5_a_im"], sp["s5_b_re"], sp["s5_b_im"]
    ldt = sp["s5_log_dt"].reshape(S5_GROUPS, 1)
    lr, li, bb_re, bb_im, lb = _params_fwd(a_re, a_im, ldt, b_re, b_im, sp["hg_lb_logits"])
    lam = jnp.concatenate([lr.reshape(1, S5_N), li.reshape(1, S5_N)], axis=0)
    swap = lambda m: m.transpose(0, 2, 1)
    b_to_st = (_band_blocks(bb_re), _band_blocks(bb_im))
    b_to_ch = (_band_blocks(swap(bb_re)), _band_blocks(swap(bb_im)))
    c_to_ch = (_band_blocks(swap(sp["s5_c_re"])), _band_blocks(swap(-sp["s5_c_im"])))
    c_to_st = (_band_blocks(sp["s5_c_re"]), _band_blocks(-sp["s5_c_im"]))

    g_mix, g_ffn, g_final = row(sp["g_mix"]), row(sp["g_ffn"]), row(sp["g_final"])
    b_glu, gain, dskip, b_conv = row(sp["b_glu"]), row(sp["hg_norm_gain"]), row(sp["s5_d"]), row(sp["b_conv"])

    w_in = weights("in", lam, *b_to_st, *b_to_ch, *c_to_ch, *c_to_st)["w_in"]
    u, za, zh, zgt = _in_proj(x, g_mix, w_in, tm)
    seqs = lambda v: v.reshape(nb, seq, v.shape[-1])
    toks = lambda v: v.reshape(t, v.shape[-1])
    xs3, y0 = _s5_fwd(seqs(za), b_to_st, lam, c_to_ch, dskip, nb, seq, tm)
    xs, y0 = toks(xs3), toks(y0)
    o3, sts = _hgrn_fwd(zh.reshape(nb, seq, 4 * HG_WIDTH), lb, nb, seq)
    o = o3.reshape(t, HG_WIDTH)
    wm = weights("mix", y0, o3)
    x1, u2, pa, pb, ya2, yb = _mix_fwd(x, y0, o, zh, zgt, wm["w_glu"], b_glu, gain, wm["w_pa"], wm["w_pb"],
                                       wm["w_out"], g_ffn, tm)
    wf = weights("ffn", u2)
    h = _ffn_up(u2, wf["w_up"], min(4 * tm, t))
    hc, a, dx2, dx2b, loss, dg_final = _ffn_down_loss(h, x1, tgt, wf["w_conv"], b_conv, wf["w_down"], g_final,
                                                      seq, tm)

    wgrad = functools.partial(_wgrad, tn=256, out_dtype=BF16)
    dhc, db_conv = _ffn_bwd_act(dx2b, hc, wf["w_down"], tm)
    sent = emit({"w_down": wgrad(a, dx2b, "dw_down")})
    dh, dx1, dx1b, dg_ffn, dw_conv = _ffn_bwd_up(dhc, h, dx2, x1, wf["w_conv"], wf["w_up"], _after(g_ffn, sent),
                                                 seq, tm)
    sent = emit({"w_up": wgrad(dh, u2, "dw_up"), "w_conv": dw_conv})
    (dy0, do, dzg, dzgt, m, dpa, dpb, ya1, dpre, db_glu, dgain) = _mix_bwd(
        dx1b, y0, o, zh, zgt, pa, pb, wm["w_glu"], _after(b_glu, sent), gain, wm["w_pa"], wm["w_pb"], wm["w_out"], tm)
    sent = emit({"w_out": wgrad(m, dx1b, "dw_out"), "w_pa": wgrad(ya2, dpa, "dw_pa"),
                 "w_pb": wgrad(yb, dpb, "dw_pb"), "w_glu": wgrad(ya1, dpre, "dw_glu")})
    dzh3, dlb = _hgrn_bwd(zh.reshape(nb, seq, 4 * HG_WIDTH), do.reshape(nb, seq, HG_WIDTH), sts, _after(lb, sent),
                          nb, seq)
    dza, a_s5, dlam, dd = _s5_bwd(seqs(dy0), seqs(za), xs3, c_to_st, b_to_ch, lam, dskip, nb, seq, tm)
    dza, a_s5 = toks(dza), toks(a_s5)
    dz, dx, dg_mix = _in_proj_bwd(dza, dzh3.reshape(t, 3 * HG_WIDTH), dzg, dzgt, dx1, x, g_mix, w_in, tm)
    sent = emit({"w_in": wgrad(dz, u, "dw_in")})

    band = HG_HEAD
    dbb_band = _wgrad(a_s5, za, "dbb_s5", 512, band=band, after=sent)
    dc_band = _wgrad(xs, dy0, "dc_s5", 512, band=band, after=sent)
    dbb_re = swap(_diag_blocks(dbb_band[:S5_N], S5_STATE, S5_GROUP))
    dbb_im = swap(_diag_blocks(dbb_band[S5_N:], S5_STATE, S5_GROUP))
    dc_re = swap(_diag_blocks(dc_band[:S5_N], S5_STATE, S5_GROUP))
    dc_im = -swap(_diag_blocks(dc_band[S5_N:], S5_STATE, S5_GROUP))
    da_re, da_im, dldt, db_re, db_im, dlogits = _params_bwd(
        a_re, a_im, ldt, b_re, b_im, sp["hg_lb_logits"],
        dlam[0].reshape(S5_GROUPS, S5_STATE), dlam[1].reshape(S5_GROUPS, S5_STATE), dbb_re, dbb_im, dlb)
    emit_small({"g_mix": dg_mix, "s5_a_re": da_re, "s5_a_im": da_im, "s5_log_dt": dldt.reshape(1, S5_GROUPS),
                "s5_b_re": db_re, "s5_b_im": db_im, "s5_c_re": dc_re, "s5_c_im": dc_im, "s5_d": dd, "b_glu": db_glu,
                "hg_lb_logits": dlogits, "hg_norm_gain": dgain, "g_ffn": dg_ffn, "b_conv": db_conv,
                "g_final": dg_final, "loss": loss})
    return dx.reshape(nb, seq, D_MODEL)


def _mesh_peers():
    x, y, c = lax.axis_index("x"), lax.axis_index("y"), lax.axis_index("c")
    peers = []
    for k in range(1, N_DEV):
        px, py, pc = (1 - x if k & 4 else x), (1 - y if k & 2 else y), (1 - c if k & 1 else c)
        peers.append((k, (px, py, pc), 4 * px + 2 * py + pc))
    return 4 * x + 2 * y + c, peers


_HBM = pl.BlockSpec(memory_space=pltpu.HBM)
_SEM = pl.BlockSpec(memory_space=pltpu.SEMAPHORE)


def _exchange_start(name, operands, after, place_own=True):
    n = len(operands)
    me = 4 * lax.axis_index("x") + 2 * lax.axis_index("y") + lax.axis_index("c")
    flags = [per_peer for _, per_peer in operands]
    srcs, lands = [], []
    for arr, per_peer in operands:
        land = lax.empty((N_DEV,) + (arr.shape[1:] if per_peer else arr.shape), arr.dtype)
        if place_own:
            own = lax.dynamic_index_in_dim(arr, me, 0, keepdims=True) if per_peer else arr[None]
            land = lax.dynamic_update_slice_in_dim(land, own, me, 0)
        srcs.append(pltpu.with_memory_space_constraint(arr, pltpu.HBM))
        lands.append(pltpu.with_memory_space_constraint(land, pltpu.HBM))
    copies = (N_DEV - 1) * n

    def body(*refs):
        src_refs, land_refs = refs[:n], refs[n:2 * n]
        send_sems, recv_sems = refs[2 * n + 1], refs[2 * n + 2]
        token = refs[-1]
        my_slab, peers = _mesh_peers()
        for k, peer, slab in peers:
            for i in range(n):
                s = (k - 1) * n + i
                pltpu.make_async_remote_copy(
                    src_ref=src_refs[i].at[slab] if flags[i] else src_refs[i], dst_ref=land_refs[i].at[my_slab],
                    send_sem=send_sems.at[s], recv_sem=recv_sems.at[s], device_id=peer,
                    device_id_type=pl.DeviceIdType.MESH).start()
        token[...] = jnp.zeros_like(token)

    outs = pl.pallas_call(
        body, name=name,
        out_shape=(pltpu.SemaphoreType.DMA((copies,)), pltpu.SemaphoreType.DMA((copies,)),
                   *[pltpu.HBM(a.shape, a.dtype) for a in srcs], *[pltpu.HBM(a.shape, a.dtype) for a in lands],
                   _sds((SUBLANES, 128))),
        in_specs=[_HBM] * (2 * n) + [pl.BlockSpec(memory_space=pl.ANY)],
        out_specs=(_SEM, _SEM, *[_HBM] * (2 * n), pl.BlockSpec(memory_space=pltpu.VMEM)),
        input_output_aliases={i: 2 + i for i in range(2 * n)},
        compiler_params=pltpu.CompilerParams(has_side_effects=pltpu.SideEffectType.DATAFLOW_SIDE_EFFECTING),
    )(*srcs, *lands, after)
    state = (flags, outs[0], outs[1], outs[2:2 + n], outs[2 + n:2 + 2 * n])
    return state, outs[-1]


def _exchange_wait(name, state, *after):
    flags, send_sems, recv_sems, srcs, lands = state
    n = len(flags)

    def body(*refs):
        src_refs, land_refs = refs[:n], refs[n:2 * n]
        send_ref, recv_ref = refs[2 * n], refs[2 * n + 1]
        _, peers = _mesh_peers()
        for k, peer, slab in peers:
            for i in range(n):
                s = (k - 1) * n + i
                copy = pltpu.make_async_remote_copy(
                    src_ref=src_refs[i].at[slab] if flags[i] else src_refs[i], dst_ref=land_refs[i].at[slab],
                    send_sem=send_ref.at[s], recv_sem=recv_ref.at[s], device_id=peer,
                    device_id_type=pl.DeviceIdType.MESH)
                copy.wait_send()
                copy.wait_recv()

    outs = pl.pallas_call(
        body, name=name,
        out_shape=(*[pltpu.HBM(a.shape, a.dtype) for a in srcs], *[pltpu.HBM(a.shape, a.dtype) for a in lands]),
        in_specs=[_HBM] * (2 * n) + [_SEM, _SEM] + [pl.BlockSpec(memory_space=pl.ANY)] * len(after),
        out_specs=tuple([_HBM] * (2 * n)),
        input_output_aliases={i: i for i in range(2 * n)},
        compiler_params=pltpu.CompilerParams(has_side_effects=pltpu.SideEffectType.DATAFLOW_SIDE_EFFECTING),
    )(*srcs, *lands, send_sems, recv_sems, *after)
    return list(outs[n:]), list(outs[:n])


def _slab(pos):
    return 4 * pos[0] + 2 * pos[1] + pos[2]


def _chip_routes():
    x, y, c = lax.axis_index("x"), lax.axis_index("y"), lax.axis_index("c")
    return (x, y, c), (x, y, 1 - c), [(1 - x, y, c), (x, 1 - y, c), (1 - x, 1 - y, c)]


def _remote(src, dst, send_sem, recv_sem, to):
    return pltpu.make_async_remote_copy(src_ref=src, dst_ref=dst, send_sem=send_sem, recv_sem=recv_sem,
                                        device_id=to, device_id_type=pl.DeviceIdType.MESH)


_EFFECT = pltpu.CompilerParams(has_side_effects=pltpu.SideEffectType.DATAFLOW_SIDE_EFFECTING)


def _gather_start(name, arrays, after):
    n = len(arrays)
    me = 4 * lax.axis_index("x") + 2 * lax.axis_index("y") + lax.axis_index("c")
    srcs = [pltpu.with_memory_space_constraint(a, pltpu.HBM) for a in arrays]
    lands = [pltpu.with_memory_space_constraint(
        lax.dynamic_update_slice_in_dim(lax.empty((N_DEV,) + a.shape, a.dtype), a[None], me, 0), pltpu.HBM)
        for a in arrays]

    def body(*refs):
        src_refs, land_refs = refs[:n], refs[n:2 * n]
        send_sems, recv_sems, token = refs[2 * n + 1], refs[2 * n + 2], refs[-1]
        mine, sibling, chips = _chip_routes()
        for k, to in enumerate([sibling] + chips):
            for i in range(n):
                _remote(src_refs[i], land_refs[i].at[_slab(mine)], send_sems.at[k * n + i], recv_sems.at[k * n + i],
                        to).start()
        token[...] = jnp.zeros_like(token)

    outs = pl.pallas_call(
        body, name=name,
        out_shape=(pltpu.SemaphoreType.DMA((4 * n,)), pltpu.SemaphoreType.DMA((4 * n,)),
                   *[pltpu.HBM(a.shape, a.dtype) for a in srcs], *[pltpu.HBM(a.shape, a.dtype) for a in lands],
                   _sds((SUBLANES, 128))),
        in_specs=[_HBM] * (2 * n) + [pl.BlockSpec(memory_space=pl.ANY)],
        out_specs=(_SEM, _SEM, *[_HBM] * (2 * n), pl.BlockSpec(memory_space=pltpu.VMEM)),
        input_output_aliases={i: 2 + i for i in range(2 * n)}, compiler_params=_EFFECT,
    )(*srcs, *lands, after)
    return (outs[0], outs[1], outs[2:2 + n], outs[2 + n:2 + 2 * n]), outs[-1]


def _gather_forward(name, state, *after):
    send_a, recv_a, srcs, lands = state
    n = len(lands)

    def body(*refs):
        land_refs, recv_a_ref = refs[:n], refs[n]
        send_b, recv_b = refs[n + 1 + len(after)], refs[n + 2 + len(after)]
        mine, sibling, chips = _chip_routes()
        for j, chip in enumerate(chips):
            for i in range(n):
                block = land_refs[i].at[_slab(chip)]
                _remote(block, block, send_b.at[j * n + i], recv_a_ref.at[(1 + j) * n + i], chip).wait_recv()
                _remote(block, block, send_b.at[j * n + i], recv_b.at[j * n + i], sibling).start()

    outs = pl.pallas_call(
        body, name=name,
        out_shape=(pltpu.SemaphoreType.DMA((3 * n,)), pltpu.SemaphoreType.DMA((3 * n,)),
                   *[pltpu.HBM(a.shape, a.dtype) for a in lands]),
        in_specs=[_HBM] * n + [_SEM] + [pl.BlockSpec(memory_space=pl.ANY)] * len(after),
        out_specs=(_SEM, _SEM, *[_HBM] * n),
        input_output_aliases={i: 2 + i for i in range(n)}, compiler_params=_EFFECT,
    )(*lands, recv_a, *after)
    return (send_a, recv_a, srcs, list(outs[2:])), (outs[0], outs[1])


def _gather_wait(name, state, forwarded):
    send_a, recv_a, srcs, lands = state
    send_b, recv_b = forwarded
    n = len(lands)

    def body(*refs):
        src_refs, land_refs = refs[:n], refs[n:2 * n]
        sa, ra, sb, rb = refs[2 * n:2 * n + 4]
        mine, sibling, chips = _chip_routes()
        for i in range(n):
            for k, to in enumerate([sibling] + chips):
                _remote(src_refs[i], land_refs[i].at[_slab(mine)], sa.at[k * n + i], ra.at[k * n + i], to).wait_send()
            theirs = land_refs[i].at[_slab(sibling)]
            _remote(theirs, theirs, sa.at[i], ra.at[i], sibling).wait_recv()
            for j, chip in enumerate(chips):
                sent = land_refs[i].at[_slab(chip)]
                got = land_refs[i].at[_slab((chip[0], chip[1], sibling[2]))]
                _remote(sent, sent, sb.at[j * n + i], rb.at[j * n + i], sibling).wait_send()
                _remote(got, got, sb.at[j * n + i], rb.at[j * n + i], sibling).wait_recv()

    outs = pl.pallas_call(
        body, name=name,
        out_shape=(*[pltpu.HBM(a.shape, a.dtype) for a in srcs], *[pltpu.HBM(a.shape, a.dtype) for a in lands]),
        in_specs=[_HBM] * (2 * n) + [_SEM] * 4, out_specs=tuple([_HBM] * (2 * n)),
        input_output_aliases={i: i for i in range(2 * n)}, compiler_params=_EFFECT,
    )(*srcs, *lands, send_a, recv_a, send_b, recv_b)
    return list(outs[n:]), list(outs[:n])


def _join_cols(parts, name, tr):
    _, r, c = parts.shape

    def body(p_ref, o_ref):
        for j in range(N_DEV):
            o_ref[:, j * c:(j + 1) * c] = p_ref[j]

    return _pcall(body, name, (r // tr,), [pl.BlockSpec((N_DEV, tr, c), lambda i: (0, i, 0))],
                  pl.BlockSpec((tr, N_DEV * c), lambda i: (i, 0)), _sds((r, N_DEV * c), parts.dtype))(parts)


def _split_cols(full, name, tr):
    r, c = full.shape[0], full.shape[1] // N_DEV

    def body(f_ref, o_ref):
        for j in range(N_DEV):
            o_ref[j] = f_ref[:, j * c:(j + 1) * c]

    return _pcall(body, name, (r // tr,), [pl.BlockSpec((tr, N_DEV * c), lambda i: (i, 0))],
                  pl.BlockSpec((N_DEV, tr, c), lambda i: (0, i, 0)), _sds((N_DEV, r, c), full.dtype))(full)


def _my_slab():
    return (4 * lax.axis_index("x") + 2 * lax.axis_index("y") + lax.axis_index("c")).astype(jnp.int32).reshape(1)


def _adamw(parts, sent, w, m, v, name, tile):
    _, rows, cols = w.shape

    def body(me_ref, p_ref, s_ref, w_ref, m_ref, v_ref, g_out, d_out, m_out, v_out):
        me = me_ref[0]
        g = jnp.where(me == 0, s_ref[0], p_ref[0]).astype(F32)
        for k in range(1, N_DEV):
            g = g + jnp.where(me == k, s_ref[0], p_ref[k]).astype(F32)
        m1 = ADAM_B1 * m_ref[0] + (1.0 - ADAM_B1) * g
        v1 = ADAM_B2 * v_ref[0] + (1.0 - ADAM_B2) * (g * g)
        m_hat = m1 / (1.0 - ADAM_B1 ** ADAM_STEP)
        v_hat = v1 / (1.0 - ADAM_B2 ** ADAM_STEP)
        g_out[0] = g
        d_out[0] = -ADAM_LR * (m_hat / (jnp.sqrt(v_hat) + ADAM_EPS) + ADAM_WD * w_ref[0])
        m_out[0] = m1
        v_out[0] = v1

    row = pl.BlockSpec((1, tile, cols), lambda i, me: (0, i, 0))
    return pl.pallas_call(
        body, name=name, out_shape=[_sds((1, rows, cols))] * 4,
        grid_spec=pltpu.PrefetchScalarGridSpec(
            num_scalar_prefetch=1, grid=(rows // tile,),
            in_specs=[pl.BlockSpec((N_DEV, tile, cols), lambda i, me: (0, i, 0)),
                      pl.BlockSpec((1, tile, cols), lambda i, me: (me[0], i, 0)), row, row, row],
            out_specs=[row, row, row, row]),
        compiler_params=pltpu.CompilerParams(dimension_semantics=("arbitrary",), vmem_limit_bytes=VMEM_LIMIT),
    )(_my_slab(), parts, sent, w, m, v)


BIG = {
    "w_in": ((N_IN // N_DEV, D_MODEL), False, N_IN // N_DEV // 3),
    "w_glu": ((S5_WIDTH // N_DEV, S5_WIDTH), False, S5_WIDTH // N_DEV),
    "w_pa": ((S5_WIDTH, D_MODEL // N_DEV), True, S5_WIDTH),
    "w_pb": ((HG_WIDTH, D_MODEL // N_DEV), True, HG_WIDTH),
    "w_out": ((D_MODEL // N_DEV, D_MODEL), False, D_MODEL // N_DEV),
    "w_up": ((2 * D_FF // N_DEV, D_MODEL), False, 2 * D_FF // N_DEV // 4),
    "w_conv": ((CONV_W, 2 * D_FF // N_DEV), True, CONV_W),
    "w_down": ((D_FF // N_DEV, D_MODEL), False, D_FF // N_DEV // 2),
}
TRANSPOSED = ("w_in", "w_up", "s5_b_re", "s5_b_im")
UNALIGNED_COLS = ("w_conv",)


def _stored(n, arr):
    return jnp.swapaxes(arr, -1, -2) if n in TRANSPOSED else arr


def _join_shards(n, parts):
    (a, b), by_cols, _ = BIG[n]
    if not by_cols:
        return parts.reshape(N_DEV * a, b)
    if n in UNALIGNED_COLS:
        return _join_cols(parts, "join_" + n, min(a, 256))
    return parts.transpose(1, 0, 2).reshape(a, N_DEV * b)


def _split_shards(n, full):
    (a, b), by_cols, _ = BIG[n]
    if not by_cols:
        return full.reshape(N_DEV, a, b)
    if n in UNALIGNED_COLS:
        return _split_cols(full, "split_" + n, min(a, 256))
    return full.reshape(a, N_DEV, b).transpose(1, 0, 2)


SMALL_CORE = {
    "s5_b_re": GSC, "s5_b_im": GSC, "s5_c_re": GSC, "s5_c_im": GSC,
    "g_mix": (1, D_MODEL), "g_ffn": (1, D_MODEL), "g_final": (1, D_MODEL), "s5_d": (1, S5_WIDTH),
    "b_glu": (1, S5_WIDTH), "hg_norm_gain": (1, HG_WIDTH), "hg_lb_logits": (2, HG_WIDTH), "b_conv": (1, 2 * D_FF),
    "s5_log_dt": (1, S5_GROUPS), "s5_a_re": (S5_GROUPS, S5_STATE), "s5_a_im": (S5_GROUPS, S5_STATE), "loss": (1, 1),
}
BLOCK_ROWS = 32


def _small_rows():
    rows, r = {}, 0
    for n, core in SMALL_CORE.items():
        rows[n] = r
        r += BLOCK_ROWS if len(core) == 3 else -(-math.prod(core) // PACK_W)
    return rows, -(-r // SUBLANES) * SUBLANES


SMALL_ROW, SMALL_ROWS = _small_rows()


def _small_pieces(name):
    r, core = SMALL_ROW[name], SMALL_CORE[name]
    if len(core) == 3:
        return [((g, slice(None), slice(None)), slice(r + S5_GROUP * (g % 2), r + S5_GROUP * (g % 2 + 1)),
                 slice(S5_STATE * (g // 2), S5_STATE * (g // 2 + 1))) for g in range(S5_GROUPS)]
    pieces = []
    for i in range(core[0]):
        for c0 in range(0, core[1], PACK_W):
            w, flat = min(PACK_W, core[1] - c0), i * core[1] + c0
            pieces.append(((slice(i, i + 1), slice(c0, c0 + w)), slice(r + flat // PACK_W, r + flat // PACK_W + 1),
                           slice(flat % PACK_W, flat % PACK_W + w)))
    return pieces


def _core_index(ref, name, idx):
    return (0,) * (len(ref.shape) - len(SMALL_CORE[name])) + idx


def _pack_small_grads(grads):
    names = list(SMALL_CORE)

    def body(*refs):
        pack = refs[-1]
        pack[...] = jnp.zeros_like(pack)
        for ref, n in zip(refs, names):
            for idx, rows, lanes in _small_pieces(n):
                pack[rows, lanes] = ref[_core_index(ref, n, idx)]

    return _pcall(body, "pack_small_grads", (1,), [_full(grads[n].shape) for n in names],
                  _full((SMALL_ROWS, PACK_W)), _sds((SMALL_ROWS, PACK_W)))(*[grads[n] for n in names])


def _adamw_small(parts, sent, names, rows, given, name):
    lo, hi = rows
    k = len(names)
    shapes = [given[n].shape for n in names]

    def body(*refs):
        me, p_ref, s_ref, ins, outs = refs[0][0], refs[1], refs[2], refs[3:3 + 3 * k], refs[3 + 3 * k:3 + 7 * k]
        packs, results = refs[3 + 7 * k:6 + 7 * k], refs[6 + 7 * k:]
        for j, pack in enumerate(packs):
            pack[...] = jnp.zeros_like(pack)
            for ref, n in zip(ins[j * k:(j + 1) * k], names):
                for idx, prow, lanes in _small_pieces(n):
                    pack[slice(prow.start - lo, prow.stop - lo), lanes] = ref[_core_index(ref, n, idx)]
        mine = s_ref[lo:hi, :]
        g = jnp.where(me == 0, mine, p_ref[0, lo:hi, :])
        for d in range(1, N_DEV):
            g = g + jnp.where(me == d, mine, p_ref[d, lo:hi, :])
        m1 = ADAM_B1 * packs[1][...] + (1.0 - ADAM_B1) * g
        v1 = ADAM_B2 * packs[2][...] + (1.0 - ADAM_B2) * (g * g)
        m_hat = m1 / (1.0 - ADAM_B1 ** ADAM_STEP)
        v_hat = v1 / (1.0 - ADAM_B2 ** ADAM_STEP)
        results[0][...] = g
        results[1][...] = -ADAM_LR * (m_hat / (jnp.sqrt(v_hat) + ADAM_EPS) + ADAM_WD * packs[0][...])
        results[2][...] = m1
        results[3][...] = v1
        for j, result in enumerate(results):
            for ref, n in zip(outs[j * k:(j + 1) * k], names):
                for idx, prow, lanes in _small_pieces(n):
                    ref[_core_index(ref, n, idx)] = result[slice(prow.start - lo, prow.stop - lo), lanes]

    flat = _pcall(body, name, (1,),
                  [pl.BlockSpec(memory_space=pltpu.SMEM), _full(parts.shape), _full(sent.shape)]
                  + [_full(s) for s in shapes] * 3,
                  [_full(s) for s in shapes] * 4, [_sds(s) for s in shapes] * 4,
                  scratch=[pltpu.VMEM((hi - lo, PACK_W), F32)] * 7,
                  )(_my_slab(), parts, sent, *[given[pre + n] for pre in ("", "m_", "v_") for n in names])
    return {n: [flat[j * k + i] for j in range(4)] for i, n in enumerate(names)}


def kernel(x, g_mix, w_in, s5_a_re, s5_a_im, s5_log_dt, s5_b_re, s5_b_im, s5_c_re, s5_c_im, s5_d, w_glu, b_glu, hg_lb_logits, hg_norm_gain, w_pa, w_pb, w_out, g_ffn, w_up, w_conv, b_conv, w_down, g_final, loss_target, m_g_mix, m_w_in, m_s5_a_re, m_s5_a_im, m_s5_log_dt, m_s5_b_re, m_s5_b_im, m_s5_c_re, m_s5_c_im, m_s5_d, m_w_glu, m_b_glu, m_hg_lb_logits, m_hg_norm_gain, m_w_pa, m_w_pb, m_w_out, m_g_ffn, m_w_up, m_w_conv, m_b_conv, m_w_down, m_g_final, v_g_mix, v_w_in, v_s5_a_re, v_s5_a_im, v_s5_log_dt, v_s5_b_re, v_s5_b_im, v_s5_c_re, v_s5_c_im, v_s5_d, v_w_glu, v_b_glu, v_hg_lb_logits, v_hg_norm_gain, v_w_pa, v_w_pb, v_w_out, v_g_ffn, v_w_up, v_w_conv, v_b_conv, v_w_down, v_g_final):
    given = dict(locals())
    small_names = [n for n, _ in SMALL]

    pay = {n: given[n][0] if n == "w_conv" else _stored(n, given[n])[0].astype(BF16) for n in BIG}
    groups = {"in": ["w_in"], "mix": ["w_glu", "w_pa", "w_pb", "w_out"], "ffn": ["w_up", "w_down", "w_conv"]}
    gathers, order = {}, pay["w_in"]
    for grp, names in groups.items():
        gathers[grp], order = _gather_start("gather_" + grp + "_start", [pay[n] for n in names], order)

    def weights(grp, *after):
        if grp == "in":
            after = (*after, order)
        state, forwarded = _gather_forward("gather_" + grp + "_forward", gathers[grp], *after)
        got, _ = _gather_wait("gather_" + grp + "_wait", state, forwarded)
        return {n: _join_shards(n, g) for n, g in zip(groups[grp], got)}

    in_flight, started = [], []

    def emit(grads):
        names = list(grads)
        state, token = _exchange_start("grads_" + names[0] + "_start",
                                       [(_split_shards(n, grads[n]), True) for n in names], grads[names[0]],
                                       place_own=False)
        in_flight.append((names, state))
        return token

    def emit_small(grads):
        pack = _pack_small_grads(grads)
        state, token = _gather_start("grads_small_start", [pack], pack)
        in_flight.append((["small"], state))
        started.append(token)

    sp = {n: (given[n] if n in ("g_final", "hg_lb_logits") else _stored(n, given[n])[0]) for n in small_names}
    sp["g_mix"] = _after(sp["g_mix"], order)
    dx = _local_step(x, loss_target, weights, sp, emit, emit_small)

    res = {}
    after = [started[-1]]
    in_flight.insert(-1, in_flight.pop())
    for names, state in in_flight:
        if names == ["small"]:
            state, forwarded = _gather_forward("grads_small_forward", state, *after)
            parts, sent = _gather_wait("grads_small_wait", state, forwarded)
        else:
            parts, sent = _exchange_wait("grads_" + names[0] + "_wait", state, *after)
        if names != ["small"]:
            after = []
            for n, part, mine in zip(names, parts, sent):
                raw = _adamw(part, mine, *[_stored(n, given[pre + n]) for pre in ("", "m_", "v_")], "adamw_" + n,
                             BIG[n][2])
                res[n] = [_stored(n, r) for r in raw]
                after.append(raw[0])
            continue
        sgiven = {pre + n: _stored(n, given[pre + n]) for pre in ("", "m_", "v_") for n in small_names}
        for pre in ("", "m_", "v_"):
            sgiven[pre + "g_final"] = given[pre + "g_final"].reshape(1, D_MODEL)
            sgiven[pre + "loss"] = jnp.zeros((1, 1), F32)
        raw = _adamw_small(parts[0], sent[0], list(SMALL_CORE), (0, SMALL_ROWS), sgiven, "adamw_small")
        res.update({n: [_stored(n, r) for r in raw[n]] for n in small_names})
        res["g_final"] = [r.reshape(D_MODEL) for r in raw["g_final"]]
        total_loss = raw["loss"][0].reshape(())
        after = [raw["s5_b_re"][0], raw["g_mix"][0]]
    return (total_loss, dx, *[res[n][0] for n in WEIGHT_ORDER], *[res[n][1] for n in WEIGHT_ORDER],
            *[res[n][2] for n in WEIGHT_ORDER], *[res[n][3] for n in WEIGHT_ORDER])
```

```python
import functools
import math

import jax
import jax.numpy as jnp
from jax import lax
from jax.experimental import pallas as pl
from jax.experimental.pallas import tpu as pltpu

F32 = jnp.float32
BF16 = jnp.bfloat16

D_MODEL = 1024
S5_WIDTH = 512
S5_GROUP = 16
S5_GROUPS = 32
S5_STATE = 64
S5_N = S5_GROUPS * S5_STATE
HG_WIDTH = 512
HG_HEAD = 128
HG_HEADS = 4
D_FF = 2816
CONV_W = 3
CHUNK = 64
N_IN = S5_WIDTH + 4 * HG_WIDTH + 2 * D_MODEL
EPS = 1e-6
QSCALE = HG_HEAD ** -0.5

ADAM_LR = 0.001
ADAM_B1 = 0.9
ADAM_B2 = 0.999
ADAM_EPS = 1e-08
ADAM_WD = 0.01
ADAM_STEP = 10

N_DEV = 8
V7X_VMEM_BYTES = 64 * 1024 * 1024
VMEM_LIMIT = V7X_VMEM_BYTES * 7 // 8
SUBLANES = 8
PACK_W = 1024

SMALL = (
    ("g_mix", (1, D_MODEL)),
    ("s5_a_re", (1, S5_GROUPS, S5_STATE)),
    ("s5_a_im", (1, S5_GROUPS, S5_STATE)),
    ("s5_log_dt", (1, S5_GROUPS)),
    ("s5_b_re", (1, S5_GROUPS, S5_STATE, S5_GROUP)),
    ("s5_b_im", (1, S5_GROUPS, S5_STATE, S5_GROUP)),
    ("s5_c_re", (1, S5_GROUPS, S5_GROUP, S5_STATE)),
    ("s5_c_im", (1, S5_GROUPS, S5_GROUP, S5_STATE)),
    ("s5_d", (1, S5_WIDTH)),
    ("b_glu", (1, S5_WIDTH)),
    ("hg_lb_logits", (2, HG_WIDTH)),
    ("hg_norm_gain", (1, HG_WIDTH)),
    ("g_ffn", (1, D_MODEL)),
    ("b_conv", (1, 2 * D_FF)),
    ("g_final", (D_MODEL,)),
)
WEIGHT_ORDER = ("g_mix", "w_in", "s5_a_re", "s5_a_im", "s5_log_dt", "s5_b_re", "s5_b_im", "s5_c_re", "s5_c_im",
                "s5_d", "w_glu", "b_glu", "hg_lb_logits", "hg_norm_gain", "w_pa", "w_pb", "w_out", "g_ffn",
                "w_up", "w_conv", "b_conv", "w_down", "g_final")


def _pcall(body, name, grid, in_specs, out_specs, out_shape, scratch=()):
    return pl.pallas_call(
        body, name=name, grid=grid, in_specs=in_specs, out_specs=out_specs, out_shape=out_shape,
        scratch_shapes=list(scratch),
        compiler_params=pltpu.CompilerParams(dimension_semantics=("arbitrary",) * len(grid),
                                             vmem_limit_bytes=VMEM_LIMIT),
    )


def _full(shape):
    return pl.BlockSpec(shape, lambda *_: (0,) * len(shape))


def _sds(shape, dtype=F32):
    return jax.ShapeDtypeStruct(shape, dtype)


def _dot(a, b):
    return jnp.dot(a.astype(BF16), b.astype(BF16), preferred_element_type=F32)


def _dot_nt(a, b):
    return lax.dot_general(a.astype(BF16), b.astype(BF16), (((1,), (1,)), ((), ())), preferred_element_type=F32)


def _dot_tn(a, b):
    return lax.dot_general(a.astype(BF16), b.astype(BF16), (((0,), (0,)), ((), ())), preferred_element_type=F32)


def _hdot(a, b):
    return jnp.dot(a, b, preferred_element_type=F32, precision=lax.Precision.HIGHEST)


def _hdot_tn(a, b):
    return lax.dot_general(a, b, (((0,), (0,)), ((), ())), preferred_element_type=F32,
                           precision=lax.Precision.HIGHEST)


def _sigmoid(x):
    return jax.nn.sigmoid(x)


GELU_C = math.sqrt(2.0 / math.pi)
GELU_A = 0.044715


def _gelu(x):
    return 0.5 * x * (1.0 + jnp.tanh(GELU_C * (x + GELU_A * (x * x * x))))


def _gelu_grad(x):
    t = jnp.tanh(GELU_C * (x + GELU_A * (x * x * x)))
    return 0.5 * (1.0 + t) + 0.5 * x * (1.0 - t * t) * (GELU_C * (1.0 + 3.0 * GELU_A * x * x))


def _cumsum_rows(v, reverse=False):
    n = v.shape[0]
    row = lax.broadcasted_iota(jnp.int32, v.shape, 0)
    s = 1
    while s < n:
        if reverse:
            v = v + jnp.where(row < n - s, pltpu.roll(v, n - s, axis=0), 0.0)
        else:
            v = v + jnp.where(row >= s, pltpu.roll(v, s, axis=0), 0.0)
        s *= 2
    return v


def _token_tile(seq):
    return min(256, seq)


def _s5_coeffs(a_re, a_im, ldt):
    dt = jnp.exp(ldt)
    mag = jnp.exp(a_re * dt)
    ang = a_im * dt
    lb_re = mag * jnp.cos(ang)
    lb_im = mag * jnp.sin(ang)
    den = a_re * a_re + a_im * a_im
    n_re = lb_re - 1.0
    n_im = lb_im
    co_re = (n_re * a_re + n_im * a_im) / den
    co_im = (n_im * a_re - n_re * a_im) / den
    return lb_re, lb_im, co_re, co_im


GS, GSC = (S5_GROUPS, S5_STATE), (S5_GROUPS, S5_GROUP, S5_STATE)


def _params_fwd(a_re, a_im, ldt, bt_re, bt_im, logits):
    def body(are, aim, ld, bre, bim, lg, lr_o, li_o, bbr_o, bbi_o, lb_o):
        lr, li, co_re, co_im = _s5_coeffs(are[...], aim[...], ld[...])
        lr_o[...] = lr
        li_o[...] = li
        for g in range(S5_GROUPS):
            cr, ci = co_re[g:g + 1, :], co_im[g:g + 1, :]
            bbr_o[g] = cr * bre[g] - ci * bim[g]
            bbi_o[g] = cr * bim[g] + ci * bre[g]
        lb_o[...] = _sigmoid(lg[0:1, :] - lg[1:2, :])

    return _pcall(body, "params_fwd", (1,),
                  [_full(GS), _full(GS), _full((S5_GROUPS, 1)), _full(GSC), _full(GSC), _full((2, HG_WIDTH))],
                  [_full(GS), _full(GS), _full(GSC), _full(GSC), _full((1, HG_WIDTH))],
                  [_sds(GS), _sds(GS), _sds(GSC), _sds(GSC), _sds((1, HG_WIDTH))],
                  )(a_re, a_im, ldt, bt_re, bt_im, logits)


def _params_bwd(a_re, a_im, ldt, bt_re, bt_im, logits, dlr, dli, dbbr, dbbi, dlb):
    def body(are, aim, ld, bre, bim, lg, dlr_r, dli_r, dbbr_r, dbbi_r, dlb_r,
             dare_o, daim_o, dld_o, dbre_o, dbim_o, dlg_o, dcr_ref, dci_ref):
        (_, _, co_re, co_im), vjp = jax.vjp(_s5_coeffs, are[...], aim[...], ld[...])
        for g in range(S5_GROUPS):
            cr, ci = co_re[g:g + 1, :], co_im[g:g + 1, :]
            gr, gi, br, bi = dbbr_r[g], dbbi_r[g], bre[g], bim[g]
            dbre_o[g] = cr * gr + ci * gi
            dbim_o[g] = cr * gi - ci * gr
            dcr_ref[g:g + 1, :] = jnp.sum(gr * br + gi * bi, axis=0, keepdims=True)
            dci_ref[g:g + 1, :] = jnp.sum(gi * br - gr * bi, axis=0, keepdims=True)
        dare, daim, dld = vjp((dlr_r[...], dli_r[...], dcr_ref[...], dci_ref[...]))
        dare_o[...] = dare
        daim_o[...] = daim
        dld_o[...] = dld
        lb = _sigmoid(lg[0:1, :] - lg[1:2, :])
        d0 = dlb_r[...] * lb * (1.0 - lb)
        dlg_o[0:1, :] = d0
        dlg_o[1:2, :] = -d0

    return _pcall(body, "params_bwd", (1,),
                  [_full(GS), _full(GS), _full((S5_GROUPS, 1)), _full(GSC), _full(GSC), _full((2, HG_WIDTH)),
                   _full(GS), _full(GS), _full(GSC), _full(GSC), _full((1, HG_WIDTH))],
                  [_full(GS), _full(GS), _full((S5_GROUPS, 1)), _full(GSC), _full(GSC), _full((2, HG_WIDTH))],
                  [_sds(GS), _sds(GS), _sds((S5_GROUPS, 1)), _sds(GSC), _sds(GSC), _sds((2, HG_WIDTH))],
                  scratch=[pltpu.VMEM(GS, F32), pltpu.VMEM(GS, F32)],
                  )(a_re, a_im, ldt, bt_re, bt_im, logits, dlr, dli, dbbr, dbbi, dlb)


def _band_blocks(m):
    g, r, c = m.shape
    gb = g // S5_BANDS
    m4 = m.astype(BF16).reshape(S5_BANDS, gb, r, c)
    on_diag = jnp.eye(gb, dtype=bool)[None, :, None, :, None]
    return jnp.where(on_diag, m4[:, :, :, None, :], 0).reshape(S5_BANDS, gb * r, gb * c)


def _diag_blocks(band, r, c):
    g, nb = band.shape[0] // r, band.shape[1] // c
    on_diag = (jnp.arange(g) % nb)[:, None, None, None] == jnp.arange(nb)[None, None, :, None]
    return jnp.sum(jnp.where(on_diag, band.reshape(g, r, nb, c), 0.0), axis=2)


def _in_proj(x, g_mix, w_in, tm):
    t = x.shape[0]

    def body(x_ref, g_ref, w_ref, u_ref, za_ref, zh_ref, zg_ref):
        xv = x_ref[...]
        r = lax.rsqrt(jnp.mean(xv * xv, axis=-1, keepdims=True) + EPS)
        u = (xv * r * g_ref[...]).astype(BF16)
        u_ref[...] = u
        za_ref[...] = _dot_nt(u, w_ref[0:S5_WIDTH, :])
        zh_ref[...] = _dot_nt(u, w_ref[S5_WIDTH:S5_WIDTH + 4 * HG_WIDTH, :])
        zg_ref[...] = _dot_nt(u, w_ref[S5_WIDTH + 4 * HG_WIDTH:, :]).astype(BF16)

    row = lambda w: pl.BlockSpec((tm, w), lambda i: (i, 0))
    return _pcall(body, "in_proj", (t // tm,),
                  [row(D_MODEL), _full((1, D_MODEL)), _full((N_IN, D_MODEL))],
                  [row(D_MODEL), row(S5_WIDTH), row(4 * HG_WIDTH), row(2 * D_MODEL)],
                  [_sds((t, D_MODEL), BF16), _sds((t, S5_WIDTH)), _sds((t, 4 * HG_WIDTH)),
                   _sds((t, 2 * D_MODEL), BF16)],
                  )(x, g_mix, w_in)


S5_LANES = 512
S5_BANDS = 4


def _band(q):
    return (slice(q * S5_WIDTH // S5_BANDS, (q + 1) * S5_WIDTH // S5_BANDS),
            slice(q * S5_N // S5_BANDS, (q + 1) * S5_N // S5_BANDS))


def _im(st):
    return slice(S5_N + st.start, S5_N + st.stop)


SCAN_UNROLL = 8


def _complex_scan(buf_ref, lam_ref, st_ref, nb, ts, reverse):
    lanes = [slice(cc * S5_LANES, (cc + 1) * S5_LANES) for cc in range(S5_N // S5_LANES)]
    chains = [(b, re) for b in range(nb) for re in lanes]
    nch = len(chains)
    wr = {re.start: lam_ref[0:1, re] for re in lanes}
    wi = {re.start: -lam_ref[1:2, re] if reverse else lam_ref[1:2, re] for re in lanes}

    def block(ib, carry):
        vr, vi = list(carry[:nch]), list(carry[nch:])
        first = ts - SCAN_UNROLL - ib * SCAN_UNROLL if reverse else ib * SCAN_UNROLL
        first = pl.multiple_of(first, SCAN_UNROLL)
        for k in range(SCAN_UNROLL):
            row = pl.ds(first + (SCAN_UNROLL - 1 - k if reverse else k), 1)
            for c, (b, re) in enumerate(chains):
                nr = wr[re.start] * vr[c] - wi[re.start] * vi[c] + buf_ref[b, row, re]
                ni = wr[re.start] * vi[c] + wi[re.start] * vr[c] + buf_ref[b, row, _im(re)]
                buf_ref[b, row, re] = nr
                buf_ref[b, row, _im(re)] = ni
                vr[c], vi[c] = nr, ni
        return tuple(vr + vi)

    init = tuple(st_ref[b, 0:1, re] for b, re in chains) + tuple(st_ref[b, 1:2, re] for b, re in chains)
    last = lax.fori_loop(0, ts // SCAN_UNROLL, block, init)
    for c, (b, re) in enumerate(chains):
        st_ref[b, 0:1, re] = last[c]
        st_ref[b, 1:2, re] = last[nch + c]


BAND_CH = S5_WIDTH // S5_BANDS
BAND_ST = S5_N // S5_BANDS


def _s5_fwd(za, b_bands, lam, c_bands, dskip, nb, seq, ts):
    nts = seq // ts

    def body(za_ref, br_ref, bi_ref, lam_ref, cr_ref, ci_ref, d_ref, xs_ref, y_ref, buf_ref, st_ref):
        @pl.when(pl.program_id(0) == 0)
        def _():
            st_ref[...] = jnp.zeros_like(st_ref)

        for b in range(nb):
            zav = za_ref[b]
            for q in range(S5_BANDS):
                ch, st = _band(q)
                buf_ref[b, :, st] = _dot(zav[:, ch], br_ref[q])
                buf_ref[b, :, _im(st)] = _dot(zav[:, ch], bi_ref[q])
        _complex_scan(buf_ref, lam_ref, st_ref, nb, ts, reverse=False)
        for b in range(nb):
            zav = za_ref[b]
            xs_ref[b] = buf_ref[b].astype(BF16)
            for q in range(S5_BANDS):
                ch, st = _band(q)
                y_ref[b, :, ch] = (_dot(xs_ref[b, :, st], cr_ref[q]) + _dot(xs_ref[b, :, _im(st)], ci_ref[q])
                                   + d_ref[:, ch] * zav[:, ch])

    tok = lambda w: pl.BlockSpec((nb, ts, w), lambda j: (0, j, 0))
    to_st, to_ch = _full((S5_BANDS, BAND_CH, BAND_ST)), _full((S5_BANDS, BAND_ST, BAND_CH))
    return _pcall(body, "s5_fwd", (nts,),
                  [tok(S5_WIDTH), to_st, to_st, _full((2, S5_N)), to_ch, to_ch, _full((1, S5_WIDTH))],
                  [tok(2 * S5_N), tok(S5_WIDTH)],
                  [_sds((nb, seq, 2 * S5_N), BF16), _sds((nb, seq, S5_WIDTH))],
                  scratch=[pltpu.VMEM((nb, ts, 2 * S5_N), F32), pltpu.VMEM((nb, 2, S5_N), F32)],
                  )(za, *b_bands, lam, *c_bands, dskip)


def _hgrn_gates(zq, zf, lbh):
    sf = _sigmoid(zf)
    f = lbh + (1.0 - lbh) * sf
    sq = _sigmoid(zq)
    qa = zq * sq * QSCALE
    bc = _cumsum_rows(jnp.log(f))
    bm = bc[CHUNK // 2 - 1:CHUNK // 2, :]
    bl = bc[CHUNK - 1:CHUNK, :]
    return sf, f, sq, qa, bc, bm, bl


def _hgrn_fwd(zh, lb, nb, seq):
    nc = seq // CHUNK

    def body(zh_ref, lb_ref, o_ref, sts_ref, st_ref):
        @pl.when(pl.program_id(0) == 0)
        def _():
            st_ref[...] = jnp.zeros_like(st_ref)

        causal = (lax.broadcasted_iota(jnp.int32, (CHUNK, CHUNK), 0)
                  >= lax.broadcasted_iota(jnp.int32, (CHUNK, CHUNK), 1))
        for b in range(nb):
            for h in range(HG_HEADS):
                hs = slice(h * HG_HEAD, (h + 1) * HG_HEAD)
                zq = zh_ref[b, :, h * HG_HEAD:(h + 1) * HG_HEAD]
                zf = zh_ref[b, :, HG_WIDTH + h * HG_HEAD:HG_WIDTH + (h + 1) * HG_HEAD]
                zi = zh_ref[b, :, 2 * HG_WIDTH + h * HG_HEAD:2 * HG_WIDTH + (h + 1) * HG_HEAD]
                _, f, _, qa, bc, bm, bl = _hgrn_gates(zq, zf, lb_ref[:, hs])
                k = 1.0 - f
                qt = qa * jnp.exp(bc - bm)
                kt = k * jnp.exp(bm - bc)
                qb = qa * jnp.exp(bc)
                kd = k * jnp.exp(bl - bc)
                st = st_ref[b, h]
                sts_ref[b, 0, h] = st
                a = jnp.where(causal, _dot_nt(qt, kt), 0.0)
                o_ref[b, :, hs] = _dot(a, zi) + _dot_nt(qb, st)
                st_ref[b, h] = st * jnp.exp(bl) + _dot_tn(zi, kd)

    return _pcall(body, "hgrn_fwd", (nc,),
                  [pl.BlockSpec((nb, CHUNK, 4 * HG_WIDTH), lambda c: (0, c, 0)), _full((1, HG_WIDTH))],
                  [pl.BlockSpec((nb, CHUNK, HG_WIDTH), lambda c: (0, c, 0)),
                   pl.BlockSpec((nb, 1, HG_HEADS, HG_HEAD, HG_HEAD), lambda c: (0, c, 0, 0, 0))],
                  [_sds((nb, seq, HG_WIDTH)), _sds((nb, nc, HG_HEADS, HG_HEAD, HG_HEAD))],
                  scratch=[pltpu.VMEM((nb, HG_HEADS, HG_HEAD, HG_HEAD), F32)])(zh, lb)


def _head_rms(o):
    parts = []
    for h in range(HG_HEADS):
        oh = o[:, h * HG_HEAD:(h + 1) * HG_HEAD]
        r = lax.rsqrt(jnp.mean(oh * oh, axis=-1, keepdims=True) + EPS)
        parts.append(jnp.broadcast_to(r, oh.shape))
    return jnp.concatenate(parts, axis=1)


def _head_mean(v):
    parts = []
    for h in range(HG_HEADS):
        vh = v[:, h * HG_HEAD:(h + 1) * HG_HEAD]
        parts.append(jnp.broadcast_to(jnp.mean(vh, axis=-1, keepdims=True), vh.shape))
    return jnp.concatenate(parts, axis=1)


def _mix_fwd(x, y0, o, zh, zgt, w_glu, b_glu, gain, w_pa, w_pb, w_out, g_ffn, tm):
    t = x.shape[0]

    def body(x_ref, y0_ref, o_ref, zg_ref, zgt_ref, wglu_ref, bglu_ref, gain_ref, wpa_ref, wpb_ref, wout_ref,
             gffn_ref, x1_ref, u2_ref, pa_ref, pb_ref, ya2_ref, yb_ref):
        ya1 = _gelu(y0_ref[...])
        s = _sigmoid(_dot(ya1, wglu_ref[...]) + bglu_ref[...])
        ya2 = (ya1 * s).astype(BF16)
        ov = o_ref[...]
        zg = zg_ref[...]
        yb = (ov * _head_rms(ov) * gain_ref[...] * (zg * _sigmoid(zg))).astype(BF16)
        ya2_ref[...] = ya2
        yb_ref[...] = yb
        pa = jnp.dot(ya2, wpa_ref[...], preferred_element_type=F32)
        pb = jnp.dot(yb, wpb_ref[...], preferred_element_type=F32)
        pa_ref[...] = pa.astype(BF16)
        pb_ref[...] = pb.astype(BF16)
        m = (_sigmoid(zgt_ref[:, 0:D_MODEL].astype(F32)) * pa
             + _sigmoid(zgt_ref[:, D_MODEL:].astype(F32)) * pb)
        x1 = x_ref[...] + _dot(m, wout_ref[...])
        x1_ref[...] = x1
        r = lax.rsqrt(jnp.mean(x1 * x1, axis=-1, keepdims=True) + EPS)
        u2_ref[...] = (x1 * r * gffn_ref[...]).astype(BF16)

    row = lambda w: pl.BlockSpec((tm, w), lambda i: (i, 0))
    return _pcall(body, "mix_fwd", (t // tm,),
                  [row(D_MODEL), row(S5_WIDTH), row(HG_WIDTH), pl.BlockSpec((tm, HG_WIDTH), lambda i: (i, 3)),
                   row(2 * D_MODEL), _full((S5_WIDTH, S5_WIDTH)), _full((1, S5_WIDTH)), _full((1, HG_WIDTH)),
                   _full((S5_WIDTH, D_MODEL)), _full((HG_WIDTH, D_MODEL)), _full((D_MODEL, D_MODEL)),
                   _full((1, D_MODEL))],
                  [row(D_MODEL), row(D_MODEL), row(D_MODEL), row(D_MODEL), row(S5_WIDTH), row(HG_WIDTH)],
                  [_sds((t, D_MODEL)), _sds((t, D_MODEL), BF16), _sds((t, D_MODEL), BF16), _sds((t, D_MODEL), BF16),
                   _sds((t, S5_WIDTH), BF16), _sds((t, HG_WIDTH), BF16)],
                  )(x, y0, o, zh, zgt, w_glu, b_glu, gain, w_pa, w_pb, w_out, g_ffn)


FF_COLS = 256
FF_UP_TILE = 1408


def _ffn_up(u2, w_up, tm):
    t = u2.shape[0]
    n = 2 * D_FF

    def body(u_ref, w_ref, h_ref):
        h_ref[...] = _dot_nt(u_ref[...], w_ref[...]).astype(BF16)

    return _pcall(body, "ffn_up", (n // FF_UP_TILE, t // tm),
                  [pl.BlockSpec((tm, D_MODEL), lambda j, i: (i, 0)),
                   pl.BlockSpec((FF_UP_TILE, D_MODEL), lambda j, i: (j, 0))],
                  pl.BlockSpec((tm, FF_UP_TILE), lambda j, i: (i, j)),
                  _sds((t, n), BF16))(u2, w_up)


HALO = 16


def _shift_matrix(tm):
    r = lax.broadcasted_iota(jnp.int32, (tm, tm), 0)
    c = lax.broadcasted_iota(jnp.int32, (tm, tm), 1)
    return jnp.where(r == c + 1, 1.0, 0.0).astype(BF16)


def _conv_cols(h_ref, halo_ref, valid, wc_ref, bc_ref, c0):
    cs = slice(c0, c0 + FF_COLS)
    cur = h_ref[:, cs].astype(F32)
    prev = jnp.where(valid, halo_ref[:, cs].astype(F32), 0.0)
    full = jnp.concatenate([prev, cur], axis=0)
    h1 = pltpu.roll(full, 1, axis=0)[HALO:]
    h2 = pltpu.roll(full, 2, axis=0)[HALO:]
    return h2 * wc_ref[0:1, cs] + h1 * wc_ref[1:2, cs] + cur * wc_ref[2:3, cs] + bc_ref[:, cs]


def _ffn_down_loss(h, x1, tgt, w_conv, b_conv, w_down, g_final, seq, tm):
    t = h.shape[0]
    tps = seq // tm
    n = 2 * D_FF

    def body(h_ref, halo_ref, x1_ref, tgt_ref, wc_ref, bc_ref, wd_ref, gf_ref,
             hc_ref, a_ref, dx2_ref, dx2b_ref, loss_ref, dgf_ref):
        i = pl.program_id(0)

        @pl.when(i == 0)
        def _():
            loss_ref[...] = jnp.zeros_like(loss_ref)
            dgf_ref[...] = jnp.zeros_like(dgf_ref)

        valid = (i % tps) != 0
        x2 = x1_ref[...]
        for j in range(D_FF // FF_COLS):
            gate = _conv_cols(h_ref, halo_ref, valid, wc_ref, bc_ref, j * FF_COLS)
            val = _conv_cols(h_ref, halo_ref, valid, wc_ref, bc_ref, D_FF + j * FF_COLS)
            hc_ref[:, j * FF_COLS:(j + 1) * FF_COLS] = gate.astype(BF16)
            hc_ref[:, D_FF + j * FF_COLS:D_FF + (j + 1) * FF_COLS] = val.astype(BF16)
            a = (gate * _sigmoid(gate) * val).astype(BF16)
            a_ref[:, j * FF_COLS:(j + 1) * FF_COLS] = a
            x2 = x2 + jnp.dot(a, wd_ref[j * FF_COLS:(j + 1) * FF_COLS, :], preferred_element_type=F32)
        r = lax.rsqrt(jnp.mean(x2 * x2, axis=-1, keepdims=True) + EPS)
        xn = x2 * r
        g = gf_ref[...]
        e = xn * g - tgt_ref[...]
        loss_ref[...] += (0.5 / D_MODEL) * jnp.sum(e * e).reshape(1, 1)
        dy = e * (1.0 / D_MODEL)
        dgf_ref[...] += jnp.sum(dy * xn, axis=0, keepdims=True)
        dxn = dy * g
        dx2 = r * (dxn - xn * jnp.mean(dxn * xn, axis=-1, keepdims=True))
        dx2_ref[...] = dx2
        dx2b_ref[...] = dx2.astype(BF16)

    row = lambda w: pl.BlockSpec((tm, w), lambda i: (i, 0))
    halo = pl.BlockSpec((HALO, n), lambda i: (jnp.maximum(i * (tm // HALO) - 1, 0), 0))
    return _pcall(body, "ffn_down_loss", (t // tm,),
                  [row(n), halo, row(D_MODEL), row(D_MODEL), _full((CONV_W, n)), _full((1, n)),
                   _full((D_FF, D_MODEL)), _full((1, D_MODEL))],
                  [row(n), row(D_FF), row(D_MODEL), row(D_MODEL), _full((1, 1)), _full((1, D_MODEL))],
                  [_sds((t, n), BF16), _sds((t, D_FF), BF16), _sds((t, D_MODEL)), _sds((t, D_MODEL), BF16),
                   _sds((1, 1)), _sds((1, D_MODEL))],
                  )(h, h, x1, tgt, w_conv, b_conv, w_down, g_final)


def _wgrad(a, b, name, tn, out_dtype=F32, band=None, after=None):
    t, m = a.shape
    n = b.shape[1] if band is None else band
    nbands = 1 if band is None else b.shape[1] // band
    after = b if after is None else after

    def body(a_ref, b_ref, after_ref, o_ref):
        o_ref[...] = _dot_tn(a_ref[...], b_ref[...]).astype(out_dtype)

    return _pcall(body, name, (m // tn,),
                  [pl.BlockSpec((t, tn), lambda i: (0, i)), pl.BlockSpec((t, n), lambda i: (0, i % nbands)),
                   pl.BlockSpec(memory_space=pl.ANY)],
                  pl.BlockSpec((tn, n), lambda i: (i, 0)), _sds((m, n), out_dtype))(a, b, after)


def _ffn_bwd_act(dx2b, hc, w_down, tm):
    t = hc.shape[0]
    n = 2 * D_FF

    def body(dx2_ref, hc_ref, wd_ref, dhc_ref, dbc_ref):
        @pl.when(pl.program_id(0) == 0)
        def _():
            dbc_ref[...] = jnp.zeros_like(dbc_ref)

        dx2 = dx2_ref[...]
        for j in range(D_FF // FF_COLS):
            gs = slice(j * FF_COLS, (j + 1) * FF_COLS)
            vs = slice(D_FF + j * FF_COLS, D_FF + (j + 1) * FF_COLS)
            gate = hc_ref[:, gs].astype(F32)
            val = hc_ref[:, vs].astype(F32)
            da = _dot_nt(dx2, wd_ref[gs, :])
            sg = _sigmoid(gate)
            dgate = da * val * (sg * (1.0 + gate * (1.0 - sg)))
            dval = da * (gate * sg)
            dhc_ref[:, gs] = dgate.astype(BF16)
            dhc_ref[:, vs] = dval.astype(BF16)
            dbc_ref[:, gs] += jnp.sum(dgate, axis=0, keepdims=True)
            dbc_ref[:, vs] += jnp.sum(dval, axis=0, keepdims=True)

    row = lambda w: pl.BlockSpec((tm, w), lambda i: (i, 0))
    return _pcall(body, "ffn_bwd_act", (t // tm,),
                  [row(D_MODEL), row(n), _full((D_FF, D_MODEL))],
                  [row(n), _full((1, n))],
                  [_sds((t, n), BF16), _sds((1, n))],
                  )(dx2b, hc, w_down)


def _ffn_bwd_up(dhc, h, dx2, x1, w_conv, w_up, g_ffn, seq, tm):
    t = dhc.shape[0]
    tps = seq // tm
    n = 2 * D_FF
    last = t // HALO - 1

    def body(dhc_ref, halo_ref, h_ref, dx2_ref, x1_ref, wc_ref, wu_ref, gf_ref,
             dh_ref, dx1_ref, dx1b_ref, dgf_ref, dwc_ref):
        i = pl.program_id(0)

        @pl.when(i == 0)
        def _():
            dgf_ref[...] = jnp.zeros_like(dgf_ref)
            dwc_ref[...] = jnp.zeros_like(dwc_ref)

        valid = ((i + 1) % tps) != 0
        du2 = jnp.zeros((tm, D_MODEL), F32)
        for j in range(n // FF_COLS):
            cs = slice(j * FF_COLS, (j + 1) * FF_COLS)
            cur = dhc_ref[:, cs].astype(F32)
            nxt = jnp.where(valid, halo_ref[:, cs].astype(F32), 0.0)
            full = jnp.concatenate([cur, nxt], axis=0)
            d1 = pltpu.roll(full, tm + HALO - 1, axis=0)[:tm]
            d2 = pltpu.roll(full, tm + HALO - 2, axis=0)[:tm]
            dh = (cur * wc_ref[2:3, cs] + d1 * wc_ref[1:2, cs] + d2 * wc_ref[0:1, cs]).astype(BF16)
            dh_ref[:, cs] = dh
            du2 = du2 + _dot(dh, wu_ref[cs, :])
            hv = h_ref[:, cs].astype(F32)
            dwc_ref[0:1, cs] += jnp.sum(hv * d2, axis=0, keepdims=True)
            dwc_ref[1:2, cs] += jnp.sum(hv * d1, axis=0, keepdims=True)
            dwc_ref[2:3, cs] += jnp.sum(hv * cur, axis=0, keepdims=True)
        x1 = x1_ref[...]
        r = lax.rsqrt(jnp.mean(x1 * x1, axis=-1, keepdims=True) + EPS)
        xn = x1 * r
        dgf_ref[...] += jnp.sum(du2 * xn, axis=0, keepdims=True)
        dxn = du2 * gf_ref[...]
        dx1 = dx2_ref[...] + r * (dxn - xn * jnp.mean(dxn * xn, axis=-1, keepdims=True))
        dx1_ref[...] = dx1
        dx1b_ref[...] = dx1.astype(BF16)

    row = lambda w: pl.BlockSpec((tm, w), lambda i: (i, 0))
    halo = pl.BlockSpec((HALO, n), lambda i: (jnp.minimum((i + 1) * (tm // HALO), last), 0))
    return _pcall(body, "ffn_bwd_up", (t // tm,),
                  [row(n), halo, row(n), row(D_MODEL), row(D_MODEL), _full((CONV_W, n)), _full((n, D_MODEL)),
                   _full((1, D_MODEL))],
                  [row(n), row(D_MODEL), row(D_MODEL), _full((1, D_MODEL)), _full((CONV_W, n))],
                  [_sds((t, n), BF16), _sds((t, D_MODEL)), _sds((t, D_MODEL), BF16), _sds((1, D_MODEL)),
                   _sds((CONV_W, n))],
                  )(dhc, dhc, h, dx2, x1, w_conv, w_up, g_ffn)


def _mix_bwd(dx1, y0, o, zh, zgt, pa, pb, w_glu, b_glu, gain, w_pa, w_pb, w_out, tm):
    t = dx1.shape[0]

    def body(dx1_ref, y0_ref, o_ref, zg_ref, zgt_ref, pa_ref, pb_ref, wglu_ref, bglu_ref, gain_ref, wpa_ref,
             wpb_ref, wout_ref,
             dy0_ref, do_ref, dzg_ref, dzgt_ref, m_ref, dpa_ref, dpb_ref, ya1_ref, dpre_ref, dbglu_ref, dgain_ref):
        @pl.when(pl.program_id(0) == 0)
        def _():
            dbglu_ref[...] = jnp.zeros_like(dbglu_ref)
            dgain_ref[...] = jnp.zeros_like(dgain_ref)

        dm = _dot_nt(dx1_ref[...], wout_ref[...])
        sga = _sigmoid(zgt_ref[:, 0:D_MODEL].astype(F32))
        sgb = _sigmoid(zgt_ref[:, D_MODEL:].astype(F32))
        pa = pa_ref[...].astype(F32)
        pb = pb_ref[...].astype(F32)
        m_ref[...] = (sga * pa + sgb * pb).astype(BF16)
        dzgt_ref[:, 0:D_MODEL] = (dm * pa * sga * (1.0 - sga)).astype(BF16)
        dzgt_ref[:, D_MODEL:] = (dm * pb * sgb * (1.0 - sgb)).astype(BF16)
        dpa = (dm * sga).astype(BF16)
        dpb = (dm * sgb).astype(BF16)
        dpa_ref[...] = dpa
        dpb_ref[...] = dpb
        dya2 = _dot_nt(dpa, wpa_ref[...])
        dyb = _dot_nt(dpb, wpb_ref[...])
        y0 = y0_ref[...]
        ya1 = _gelu(y0)
        ya1_ref[...] = ya1.astype(BF16)
        s = _sigmoid(_dot(ya1, wglu_ref[...]) + bglu_ref[...])
        dpre = dya2 * ya1 * s * (1.0 - s)
        dpre_ref[...] = dpre.astype(BF16)
        dbglu_ref[...] += jnp.sum(dpre, axis=0, keepdims=True)
        dya1 = dya2 * s + _dot_nt(dpre, wglu_ref[...])
        dy0_ref[...] = dya1 * _gelu_grad(y0)
        ov = o_ref[...]
        zg = zg_ref[...]
        oh = ov * _head_rms(ov)
        on = oh * gain_ref[...]
        sz = _sigmoid(zg)
        dzg_ref[...] = (dyb * on * (sz * (1.0 + zg * (1.0 - sz)))).astype(BF16)
        don = dyb * (zg * sz)
        dgain_ref[...] += jnp.sum(don * oh, axis=0, keepdims=True)
        doh = don * gain_ref[...]
        do_ref[...] = _head_rms(ov) * (doh - oh * _head_mean(doh * oh))

    row = lambda w: pl.BlockSpec((tm, w), lambda i: (i, 0))
    return _pcall(body, "mix_bwd", (t // tm,),
                  [row(D_MODEL), row(S5_WIDTH), row(HG_WIDTH), pl.BlockSpec((tm, HG_WIDTH), lambda i: (i, 3)),
                   row(2 * D_MODEL), row(D_MODEL), row(D_MODEL), _full((S5_WIDTH, S5_WIDTH)), _full((1, S5_WIDTH)),
                   _full((1, HG_WIDTH)), _full((S5_WIDTH, D_MODEL)), _full((HG_WIDTH, D_MODEL)),
                   _full((D_MODEL, D_MODEL))],
                  [row(S5_WIDTH), row(HG_WIDTH), row(HG_WIDTH), row(2 * D_MODEL), row(D_MODEL), row(D_MODEL),
                   row(D_MODEL), row(S5_WIDTH), row(S5_WIDTH), _full((1, S5_WIDTH)), _full((1, HG_WIDTH))],
                  [_sds((t, S5_WIDTH)), _sds((t, HG_WIDTH)), _sds((t, HG_WIDTH), BF16), _sds((t, 2 * D_MODEL), BF16),
                   _sds((t, D_MODEL), BF16), _sds((t, D_MODEL), BF16), _sds((t, D_MODEL), BF16),
                   _sds((t, S5_WIDTH), BF16), _sds((t, S5_WIDTH), BF16), _sds((1, S5_WIDTH)), _sds((1, HG_WIDTH))],
                  )(dx1, y0, o, zh, zgt, pa, pb, w_glu, b_glu, gain, w_pa, w_pb, w_out)


def _s5_bwd(dy0, za, xs, c_bands, b_bands, lam, dskip, nb, seq, ts):
    nts = seq // ts

    def body(dy0_ref, za_ref, xs_ref, halo_ref, cr_ref, ci_ref, br_ref, bi_ref, lam_ref, d_ref,
             dza_ref, a_ref, dlam_ref, dd_ref, acc_ref, st_ref):
        j = pl.program_id(0)

        @pl.when(j == 0)
        def _():
            dlam_ref[...] = jnp.zeros_like(dlam_ref)
            dd_ref[...] = jnp.zeros_like(dd_ref)
            st_ref[...] = jnp.zeros_like(st_ref)

        for b in range(nb):
            dy0 = dy0_ref[b]
            for q in range(S5_BANDS):
                ch, st = _band(q)
                acc_ref[b, :, st] = _dot(dy0[:, ch], cr_ref[q])
                acc_ref[b, :, _im(st)] = _dot(dy0[:, ch], ci_ref[q])
        _complex_scan(acc_ref, lam_ref, st_ref, nb, ts, reverse=True)
        shift = _shift_matrix(ts)
        top = lax.broadcasted_iota(jnp.int32, (SUBLANES, S5_LANES), 0) == 0
        for b in range(nb):
            a_ref[b] = acc_ref[b].astype(BF16)
            first = jnp.where(j == nts - 1, 0.0, halo_ref[b, HALO - 1:HALO, :].astype(F32))

            def shifted(cols):
                xp = jnp.dot(shift, xs_ref[b, :, cols], preferred_element_type=F32)
                return jnp.concatenate([xp[:SUBLANES] + jnp.where(top, first[:, cols], 0.0), xp[SUBLANES:]], axis=0)

            for cc in range(S5_N // S5_LANES):
                re = slice(cc * S5_LANES, (cc + 1) * S5_LANES)
                ar, ai, xr, xi = acc_ref[b, :, re], acc_ref[b, :, _im(re)], shifted(re), shifted(_im(re))
                dlam_ref[0:1, re] += jnp.sum(ar * xr + ai * xi, axis=0, keepdims=True)
                dlam_ref[1:2, re] += jnp.sum(ai * xr - ar * xi, axis=0, keepdims=True)
            dy0 = dy0_ref[b]
            for q in range(S5_BANDS):
                ch, st = _band(q)
                dza_ref[b, :, ch] = (_dot(a_ref[b, :, st], br_ref[q]) + _dot(a_ref[b, :, _im(st)], bi_ref[q])
                                     + d_ref[:, ch] * dy0[:, ch]).astype(BF16)
            dd_ref[...] += jnp.sum(dy0 * za_ref[b], axis=0, keepdims=True)

    tile = lambda j: nts - 1 - j
    tok = lambda w: pl.BlockSpec((nb, ts, w), lambda j: (0, tile(j), 0))
    halo = pl.BlockSpec((nb, HALO, 2 * S5_N), lambda j: (0, jnp.maximum(tile(j) * (ts // HALO) - 1, 0), 0))
    to_st, to_ch = _full((S5_BANDS, BAND_CH, BAND_ST)), _full((S5_BANDS, BAND_ST, BAND_CH))
    return _pcall(body, "s5_bwd", (nts,),
                  [tok(S5_WIDTH), tok(S5_WIDTH), tok(2 * S5_N), halo, to_st, to_st, to_ch, to_ch,
                   _full((2, S5_N)), _full((1, S5_WIDTH))],
                  [tok(S5_WIDTH), tok(2 * S5_N), _full((2, S5_N)), _full((1, S5_WIDTH))],
                  [_sds((nb, seq, S5_WIDTH), BF16), _sds((nb, seq, 2 * S5_N), BF16), _sds((2, S5_N)),
                   _sds((1, S5_WIDTH))],
                  scratch=[pltpu.VMEM((nb, ts, 2 * S5_N), F32), pltpu.VMEM((nb, 2, S5_N), F32)],
                  )(dy0, za, xs, xs, *c_bands, *b_bands, lam, dskip)


def _hgrn_bwd(zh, do, sts, lb, nb, seq):
    nc = seq // CHUNK

    def body(zh_ref, do_ref, sts_ref, lb_ref, dz_ref, dlb_ref, dst_ref):
        @pl.when(pl.program_id(0) == 0)
        def _():
            dst_ref[...] = jnp.zeros_like(dst_ref)
            dlb_ref[...] = jnp.zeros_like(dlb_ref)

        row = lax.broadcasted_iota(jnp.int32, (CHUNK, CHUNK), 0)
        causal = row >= lax.broadcasted_iota(jnp.int32, (CHUNK, CHUNK), 1)
        last_row = lax.broadcasted_iota(jnp.int32, (CHUNK, HG_HEAD), 0) == CHUNK - 1
        for b in range(nb):
            for h in range(HG_HEADS):
                hs = slice(h * HG_HEAD, (h + 1) * HG_HEAD)
                zq = zh_ref[b, :, h * HG_HEAD:(h + 1) * HG_HEAD]
                zf = zh_ref[b, :, HG_WIDTH + h * HG_HEAD:HG_WIDTH + (h + 1) * HG_HEAD]
                zi = zh_ref[b, :, 2 * HG_WIDTH + h * HG_HEAD:2 * HG_WIDTH + (h + 1) * HG_HEAD]
                lbh = lb_ref[:, hs]
                sf, f, sq, qa, bc, bm, bl = _hgrn_gates(zq, zf, lbh)
                k = 1.0 - f
                e_qt = jnp.exp(bc - bm)
                e_kt = jnp.exp(bm - bc)
                e_b = jnp.exp(bc)
                e_kd = jnp.exp(bl - bc)
                e_l = jnp.exp(bl)
                qt, kt, qb, kd = qa * e_qt, k * e_kt, qa * e_b, k * e_kd
                a = jnp.where(causal, _dot_nt(qt, kt), 0.0)
                st = sts_ref[b, 0, h]
                dst = dst_ref[b, h]
                dov = do_ref[b, :, hs]
                da = jnp.where(causal, _dot_nt(dov, zi), 0.0)
                dqt = _hdot(da, kt)
                dkt = _hdot_tn(da, qt)
                dqb = _dot(dov, st)
                di = _dot_tn(a, dov) + _dot_nt(kd, dst)
                dkd = _dot(zi, dst)
                de_l = jnp.sum(dst * st, axis=0, keepdims=True)
                dst_ref[b, h] = dst * e_l + _dot_tn(dov, qb)
                dqa = dqt * e_qt + dqb * e_b
                dk = dkt * e_kt + dkd * e_kd
                dbl = jnp.sum(dkd * kd, axis=0, keepdims=True) + de_l * e_l
                db = dqt * qt - dkt * kt + dqb * qb - dkd * kd + jnp.where(last_row, dbl, 0.0)
                df = _cumsum_rows(db, reverse=True) / f - dk
                dzq = dqa * QSCALE * (sq * (1.0 + zq * (1.0 - sq)))
                dzf = df * (1.0 - lbh) * sf * (1.0 - sf)
                dz_ref[b, :, h * HG_HEAD:(h + 1) * HG_HEAD] = dzq.astype(BF16)
                dz_ref[b, :, HG_WIDTH + h * HG_HEAD:HG_WIDTH + (h + 1) * HG_HEAD] = dzf.astype(BF16)
                dz_ref[b, :, 2 * HG_WIDTH + h * HG_HEAD:2 * HG_WIDTH + (h + 1) * HG_HEAD] = di.astype(BF16)
                dlb_ref[:, hs] += jnp.sum(df * (1.0 - sf), axis=0, keepdims=True)

    rev = lambda c: nc - 1 - c
    return _pcall(body, "hgrn_bwd", (nc,),
                  [pl.BlockSpec((nb, CHUNK, 4 * HG_WIDTH), lambda c: (0, rev(c), 0)),
                   pl.BlockSpec((nb, CHUNK, HG_WIDTH), lambda c: (0, rev(c), 0)),
                   pl.BlockSpec((nb, 1, HG_HEADS, HG_HEAD, HG_HEAD), lambda c: (0, rev(c), 0, 0, 0)),
                   _full((1, HG_WIDTH))],
                  [pl.BlockSpec((nb, CHUNK, 3 * HG_WIDTH), lambda c: (0, rev(c), 0)), _full((1, HG_WIDTH))],
                  [_sds((nb, seq, 3 * HG_WIDTH), BF16), _sds((1, HG_WIDTH))],
                  scratch=[pltpu.VMEM((nb, HG_HEADS, HG_HEAD, HG_HEAD), F32)])(zh, do, sts, lb)


def _in_proj_bwd(dza, dzh, dzg, dzgt, dx1, x, g_mix, w_in, tm):
    t = x.shape[0]

    def body(dza_ref, dzh_ref, dzg_ref, dzgt_ref, dx1_ref, x_ref, g_ref, w_ref, dz_ref, dx_ref, dg_ref):
        @pl.when(pl.program_id(0) == 0)
        def _():
            dg_ref[...] = jnp.zeros_like(dg_ref)

        c1, c2, c3 = S5_WIDTH, S5_WIDTH + 3 * HG_WIDTH, S5_WIDTH + 4 * HG_WIDTH
        dz_ref[:, 0:c1] = dza_ref[...]
        dz_ref[:, c1:c2] = dzh_ref[...]
        dz_ref[:, c2:c3] = dzg_ref[...]
        dz_ref[:, c3:] = dzgt_ref[...]
        du = _dot(dz_ref[...], w_ref[...])
        xv = x_ref[...]
        r = lax.rsqrt(jnp.mean(xv * xv, axis=-1, keepdims=True) + EPS)
        xn = xv * r
        dg_ref[...] += jnp.sum(du * xn, axis=0, keepdims=True)
        dxn = du * g_ref[...]
        dx_ref[...] = dx1_ref[...] + r * (dxn - xn * jnp.mean(dxn * xn, axis=-1, keepdims=True))

    row = lambda w: pl.BlockSpec((tm, w), lambda i: (i, 0))
    return _pcall(body, "in_proj_bwd", (t // tm,),
                  [row(S5_WIDTH), row(3 * HG_WIDTH), row(HG_WIDTH), row(2 * D_MODEL), row(D_MODEL), row(D_MODEL),
                   _full((1, D_MODEL)), _full((N_IN, D_MODEL))],
                  [row(N_IN), row(D_MODEL), _full((1, D_MODEL))],
                  [_sds((t, N_IN), BF16), _sds((t, D_MODEL)), _sds((1, D_MODEL))],
                  )(dza, dzh, dzg, dzgt, dx1, x, g_mix, w_in)


def _tie(*arrays):
    return jnp.zeros((SUBLANES, 128), F32) + sum(a.reshape(-1)[0].astype(F32) for a in arrays)


def _after(value, token):
    return value + token[0, 0]


def _local_step(x3, tgt3, weights, sp, emit, emit_small):
    nb, seq, _ = x3.shape
    t = nb * seq
    tm = _token_tile(seq)
    x = x3.reshape(t, D_MODEL)
    tgt = tgt3.reshape(t, D_MODEL)
    row = lambda v: v.reshape(1, -1)

    a_re, a_im, b_re, b_im = sp["s5_a_re"], sp["s5_a_im"], sp["s5_b_re"], sp["s5_b_im"]
    ldt = sp["s5_log_dt"].reshape(S5_GROUPS, 1)
    lr, li, bb_re, bb_im, lb = _params_fwd(a_re, a_im, ldt, b_re, b_im, sp["hg_lb_logits"])
    lam = jnp.concatenate([lr.reshape(1, S5_N), li.reshape(1, S5_N)], axis=0)
    swap = lambda m: m.transpose(0, 2, 1)
    b_to_st = (_band_blocks(bb_re), _band_blocks(bb_im))
    b_to_ch = (_band_blocks(swap(bb_re)), _band_blocks(swap(bb_im)))
    c_to_ch = (_band_blocks(swap(sp["s5_c_re"])), _band_blocks(swap(-sp["s5_c_im"])))
    c_to_st = (_band_blocks(sp["s5_c_re"]), _band_blocks(-sp["s5_c_im"]))

    g_mix, g_ffn, g_final = row(sp["g_mix"]), row(sp["g_ffn"]), row(sp["g_final"])
    b_glu, gain, dskip, b_conv = row(sp["b_glu"]), row(sp["hg_norm_gain"]), row(sp["s5_d"]), row(sp["b_conv"])

    w_in = weights("in", lam, *b_to_st, *b_to_ch, *c_to_ch, *c_to_st)["w_in"]
    u, za, zh, zgt = _in_proj(x, g_mix, w_in, tm)
    seqs = lambda v: v.reshape(nb, seq, v.shape[-1])
    toks = lambda v: v.reshape(t, v.shape[-1])
    xs3, y0 = _s5_fwd(seqs(za), b_to_st, lam, c_to_ch, dskip, nb, seq, tm)
    xs, y0 = toks(xs3), toks(y0)
    o3, sts = _hgrn_fwd(zh.reshape(nb, seq, 4 * HG_WIDTH), lb, nb, seq)
    o = o3.reshape(t, HG_WIDTH)
    wm = weights("mix", y0, o3)
    x1, u2, pa, pb, ya2, yb = _mix_fwd(x, y0, o, zh, zgt, wm["w_glu"], b_glu, gain, wm["w_pa"], wm["w_pb"],
                                       wm["w_out"], g_ffn, tm)
    wf = weights("ffn", u2)
    h = _ffn_up(u2, wf["w_up"], min(4 * tm, t))
    hc, a, dx2, dx2b, loss, dg_final = _ffn_down_loss(h, x1, tgt, wf["w_conv"], b_conv, wf["w_down"], g_final,
                                                      seq, tm)

    wgrad = functools.partial(_wgrad, tn=256, out_dtype=BF16)
    dhc, db_conv = _ffn_bwd_act(dx2b, hc, wf["w_down"], tm)
    sent = emit({"w_down": wgrad(a, dx2b, "dw_down")})
    dh, dx1, dx1b, dg_ffn, dw_conv = _ffn_bwd_up(dhc, h, dx2, x1, wf["w_conv"], wf["w_up"], _after(g_ffn, sent),
                                                 seq, tm)
    sent = emit({"w_up": wgrad(dh, u2, "dw_up"), "w_conv": dw_conv})
    (dy0, do, dzg, dzgt, m, dpa, dpb, ya1, dpre, db_glu, dgain) = _mix_bwd(
        dx1b, y0, o, zh, zgt, pa, pb, wm["w_glu"], _after(b_glu, sent), gain, wm["w_pa"], wm["w_pb"], wm["w_out"], tm)
    sent = emit({"w_out": wgrad(m, dx1b, "dw_out"), "w_pa": wgrad(ya2, dpa, "dw_pa"),
                 "w_pb": wgrad(yb, dpb, "dw_pb"), "w_glu": wgrad(ya1, dpre, "dw_glu")})
    dzh3, dlb = _hgrn_bwd(zh.reshape(nb, seq, 4 * HG_WIDTH), do.reshape(nb, seq, HG_WIDTH), sts, _after(lb, sent),
                          nb, seq)
    dza, a_s5, dlam, dd = _s5_bwd(seqs(dy0), seqs(za), xs3, c_to_st, b_to_ch, lam, dskip, nb, seq, tm)
    dza, a_s5 = toks(dza), toks(a_s5)
    dz, dx, dg_mix = _in_proj_bwd(dza, dzh3.reshape(t, 3 * HG_WIDTH), dzg, dzgt, dx1, x, g_mix, w_in, tm)
    sent = emit({"w_in": wgrad(dz, u, "dw_in")})

    band = HG_HEAD
    dbb_band = _wgrad(a_s5, za, "dbb_s5", 512, band=band, after=sent)
    dc_band = _wgrad(xs, dy0, "dc_s5", 512, band=band, after=sent)
    dbb_re = swap(_diag_blocks(dbb_band[:S5_N], S5_STATE, S5_GROUP))
    dbb_im = swap(_diag_blocks(dbb_band[S5_N:], S5_STATE, S5_GROUP))
    dc_re = swap(_diag_blocks(dc_band[:S5_N], S5_STATE, S5_GROUP))
    dc_im = -swap(_diag_blocks(dc_band[S5_N:], S5_STATE, S5_GROUP))
    da_re, da_im, dldt, db_re, db_im, dlogits = _params_bwd(
        a_re, a_im, ldt, b_re, b_im, sp["hg_lb_logits"],
        dlam[0].reshape(S5_GROUPS, S5_STATE), dlam[1].reshape(S5_GROUPS, S5_STATE), dbb_re, dbb_im, dlb)
    emit_small({"g_mix": dg_mix, "s5_a_re": da_re, "s5_a_im": da_im, "s5_log_dt": dldt.reshape(1, S5_GROUPS),
                "s5_b_re": db_re, "s5_b_im": db_im, "s5_c_re": dc_re, "s5_c_im": dc_im, "s5_d": dd, "b_glu": db_glu,
                "hg_lb_logits": dlogits, "hg_norm_gain": dgain, "g_ffn": dg_ffn, "b_conv": db_conv,
                "g_final": dg_final, "loss": loss})
    return dx.reshape(nb, seq, D_MODEL)


def _mesh_peers():
    x, y, c = lax.axis_index("x"), lax.axis_index("y"), lax.axis_index("c")
    peers = []
    for k in range(1, N_DEV):
        px, py, pc = (1 - x if k & 4 else x), (1 - y if k & 2 else y), (1 - c if k & 1 else c)
        peers.append((k, (px, py, pc), 4 * px + 2 * py + pc))
    return 4 * x + 2 * y + c, peers


_HBM = pl.BlockSpec(memory_space=pltpu.HBM)
_SEM = pl.BlockSpec(memory_space=pltpu.SEMAPHORE)


def _exchange_start(name, operands, after, place_own=True):
    n = len(operands)
    me = 4 * lax.axis_index("x") + 2 * lax.axis_index("y") + lax.axis_index("c")
    flags = [per_peer for _, per_peer in operands]
    srcs, lands = [], []
    for arr, per_peer in operands:
        land = lax.empty((N_DEV,) + (arr.shape[1:] if per_peer else arr.shape), arr.dtype)
        if place_own:
            own = lax.dynamic_index_in_dim(arr, me, 0, keepdims=True) if per_peer else arr[None]
            land = lax.dynamic_update_slice_in_dim(land, own, me, 0)
        srcs.append(pltpu.with_memory_space_constraint(arr, pltpu.HBM))
        lands.append(pltpu.with_memory_space_constraint(land, pltpu.HBM))
    copies = (N_DEV - 1) * n

    def body(*refs):
        src_refs, land_refs = refs[:n], refs[n:2 * n]
        send_sems, recv_sems = refs[2 * n + 1], refs[2 * n + 2]
        token = refs[-1]
        my_slab, peers = _mesh_peers()
        for k, peer, slab in peers:
            for i in range(n):
                s = (k - 1) * n + i
                pltpu.make_async_remote_copy(
                    src_ref=src_refs[i].at[slab] if flags[i] else src_refs[i], dst_ref=land_refs[i].at[my_slab],
                    send_sem=send_sems.at[s], recv_sem=recv_sems.at[s], device_id=peer,
                    device_id_type=pl.DeviceIdType.MESH).start()
        token[...] = jnp.zeros_like(token)

    outs = pl.pallas_call(
        body, name=name,
        out_shape=(pltpu.SemaphoreType.DMA((copies,)), pltpu.SemaphoreType.DMA((copies,)),
                   *[pltpu.HBM(a.shape, a.dtype) for a in lands], _sds((SUBLANES, 128))),
        in_specs=[_HBM] * (2 * n) + [pl.BlockSpec(memory_space=pl.ANY)],
        out_specs=(_SEM, _SEM, *[_HBM] * n, pl.BlockSpec(memory_space=pltpu.VMEM)),
        input_output_aliases={n + i: 2 + i for i in range(n)},
        compiler_params=_EFFECT,
    )(*srcs, *lands, after)
    state = (flags, outs[0], outs[1], srcs, outs[2:2 + n])
    return state, outs[-1]


def _exchange_wait(name, state, *after):
    flags, send_sems, recv_sems, srcs, lands = state
    n = len(flags)

    def body(*refs):
        src_refs, land_refs = refs[:n], refs[n:2 * n]
        send_ref, recv_ref = refs[2 * n], refs[2 * n + 1]
        _, peers = _mesh_peers()
        for k, peer, slab in peers:
            for i in range(n):
                s = (k - 1) * n + i
                copy = pltpu.make_async_remote_copy(
                    src_ref=src_refs[i].at[slab] if flags[i] else src_refs[i], dst_ref=land_refs[i].at[slab],
                    send_sem=send_ref.at[s], recv_sem=recv_ref.at[s], device_id=peer,
                    device_id_type=pl.DeviceIdType.MESH)
                copy.wait_send()
                copy.wait_recv()

    outs = pl.pallas_call(
        body, name=name,
        out_shape=tuple(pltpu.HBM(a.shape, a.dtype) for a in lands),
        in_specs=[_HBM] * (2 * n) + [_SEM, _SEM] + [pl.BlockSpec(memory_space=pl.ANY)] * len(after),
        out_specs=tuple([_HBM] * n),
        input_output_aliases={n + i: i for i in range(n)},
        compiler_params=_EFFECT,
    )(*srcs, *lands, send_sems, recv_sems, *after)
    return list(outs), list(srcs)


def _slab(pos):
    return 4 * pos[0] + 2 * pos[1] + pos[2]


def _chip_routes():
    x, y, c = lax.axis_index("x"), lax.axis_index("y"), lax.axis_index("c")
    return (x, y, c), (x, y, 1 - c), [(1 - x, y, c), (x, 1 - y, c), (1 - x, 1 - y, c)]


def _remote(src, dst, send_sem, recv_sem, to):
    return pltpu.make_async_remote_copy(src_ref=src, dst_ref=dst, send_sem=send_sem, recv_sem=recv_sem,
                                        device_id=to, device_id_type=pl.DeviceIdType.MESH)


_EFFECT = pltpu.CompilerParams(has_side_effects=pltpu.SideEffectType.DATAFLOW_SIDE_EFFECTING)


def _gather_start(name, arrays, after):
    n = len(arrays)
    me = 4 * lax.axis_index("x") + 2 * lax.axis_index("y") + lax.axis_index("c")
    srcs = [pltpu.with_memory_space_constraint(a, pltpu.HBM) for a in arrays]
    lands = [pltpu.with_memory_space_constraint(
        lax.dynamic_update_slice_in_dim(lax.empty((N_DEV,) + a.shape, a.dtype), a[None], me, 0), pltpu.HBM)
        for a in arrays]

    def body(*refs):
        src_refs, land_refs = refs[:n], refs[n:2 * n]
        send_sems, recv_sems, token = refs[2 * n + 1], refs[2 * n + 2], refs[-1]
        mine, sibling, chips = _chip_routes()
        for k, to in enumerate([sibling] + chips):
            for i in range(n):
                _remote(src_refs[i], land_refs[i].at[_slab(mine)], send_sems.at[k * n + i], recv_sems.at[k * n + i],
                        to).start()
        token[...] = jnp.zeros_like(token)

    outs = pl.pallas_call(
        body, name=name,
        out_shape=(pltpu.SemaphoreType.DMA((4 * n,)), pltpu.SemaphoreType.DMA((4 * n,)),
                   *[pltpu.HBM(a.shape, a.dtype) for a in srcs], *[pltpu.HBM(a.shape, a.dtype) for a in lands],
                   _sds((SUBLANES, 128))),
        in_specs=[_HBM] * (2 * n) + [pl.BlockSpec(memory_space=pl.ANY)],
        out_specs=(_SEM, _SEM, *[_HBM] * (2 * n), pl.BlockSpec(memory_space=pltpu.VMEM)),
        input_output_aliases={i: 2 + i for i in range(2 * n)}, compiler_params=_EFFECT,
    )(*srcs, *lands, after)
    return (outs[0], outs[1], outs[2:2 + n], outs[2 + n:2 + 2 * n]), outs[-1]


def _gather_forward(name, state, *after):
    send_a, recv_a, srcs, lands = state
    n = len(lands)

    def body(*refs):
        land_refs, recv_a_ref = refs[:n], refs[n]
        send_b, recv_b = refs[n + 1 + len(after)], refs[n + 2 + len(after)]
        mine, sibling, chips = _chip_routes()
        for j, chip in enumerate(chips):
            for i in range(n):
                block = land_refs[i].at[_slab(chip)]
                _remote(block, block, send_b.at[j * n + i], recv_a_ref.at[(1 + j) * n + i], chip).wait_recv()
                _remote(block, block, send_b.at[j * n + i], recv_b.at[j * n + i], sibling).start()

    outs = pl.pallas_call(
        body, name=name,
        out_shape=(pltpu.SemaphoreType.DMA((3 * n,)), pltpu.SemaphoreType.DMA((3 * n,)),
                   *[pltpu.HBM(a.shape, a.dtype) for a in lands]),
        in_specs=[_HBM] * n + [_SEM] + [pl.BlockSpec(memory_space=pl.ANY)] * len(after),
        out_specs=(_SEM, _SEM, *[_HBM] * n),
        input_output_aliases={i: 2 + i for i in range(n)}, compiler_params=_EFFECT,
    )(*lands, recv_a, *after)
    return (send_a, recv_a, srcs, list(outs[2:])), (outs[0], outs[1])


def _gather_wait(name, state, forwarded):
    send_a, recv_a, srcs, lands = state
    send_b, recv_b = forwarded
    n = len(lands)

    def body(*refs):
        src_refs, land_refs = refs[:n], refs[n:2 * n]
        sa, ra, sb, rb = refs[2 * n:2 * n + 4]
        mine, sibling, chips = _chip_routes()
        for i in range(n):
            for k, to in enumerate([sibling] + chips):
                _remote(src_refs[i], land_refs[i].at[_slab(mine)], sa.at[k * n + i], ra.at[k * n + i], to).wait_send()
            theirs = land_refs[i].at[_slab(sibling)]
            _remote(theirs, theirs, sa.at[i], ra.at[i], sibling).wait_recv()
            for j, chip in enumerate(chips):
                sent = land_refs[i].at[_slab(chip)]
                got = land_refs[i].at[_slab((chip[0], chip[1], sibling[2]))]
                _remote(sent, sent, sb.at[j * n + i], rb.at[j * n + i], sibling).wait_send()
                _remote(got, got, sb.at[j * n + i], rb.at[j * n + i], sibling).wait_recv()

    outs = pl.pallas_call(
        body, name=name,
        out_shape=(*[pltpu.HBM(a.shape, a.dtype) for a in srcs], *[pltpu.HBM(a.shape, a.dtype) for a in lands]),
        in_specs=[_HBM] * (2 * n) + [_SEM] * 4, out_specs=tuple([_HBM] * (2 * n)),
        input_output_aliases={i: i for i in range(2 * n)}, compiler_params=_EFFECT,
    )(*srcs, *lands, send_a, recv_a, send_b, recv_b)
    return list(outs[n:]), list(outs[:n])


def _join_cols(parts, name, tr):
    _, r, c = parts.shape

    def body(p_ref, o_ref):
        for j in range(N_DEV):
            o_ref[:, j * c:(j + 1) * c] = p_ref[j]

    return _pcall(body, name, (r // tr,), [pl.BlockSpec((N_DEV, tr, c), lambda i: (0, i, 0))],
                  pl.BlockSpec((tr, N_DEV * c), lambda i: (i, 0)), _sds((r, N_DEV * c), parts.dtype))(parts)


def _split_cols(full, name, tr):
    r, c = full.shape[0], full.shape[1] // N_DEV

    def body(f_ref, o_ref):
        for j in range(N_DEV):
            o_ref[j] = f_ref[:, j * c:(j + 1) * c]

    return _pcall(body, name, (r // tr,), [pl.BlockSpec((tr, N_DEV * c), lambda i: (i, 0))],
                  pl.BlockSpec((N_DEV, tr, c), lambda i: (0, i, 0)), _sds((N_DEV, r, c), full.dtype))(full)


def _my_slab():
    return (4 * lax.axis_index("x") + 2 * lax.axis_index("y") + lax.axis_index("c")).astype(jnp.int32).reshape(1)


def _adamw(parts, sent, w, m, v, name, tile):
    _, rows, cols = w.shape

    def body(me_ref, p_ref, s_ref, w_ref, m_ref, v_ref, g_out, d_out, m_out, v_out):
        me = me_ref[0]
        g = jnp.where(me == 0, s_ref[0], p_ref[0]).astype(F32)
        for k in range(1, N_DEV):
            g = g + jnp.where(me == k, s_ref[0], p_ref[k]).astype(F32)
        m1 = ADAM_B1 * m_ref[0] + (1.0 - ADAM_B1) * g
        v1 = ADAM_B2 * v_ref[0] + (1.0 - ADAM_B2) * (g * g)
        m_hat = m1 / (1.0 - ADAM_B1 ** ADAM_STEP)
        v_hat = v1 / (1.0 - ADAM_B2 ** ADAM_STEP)
        g_out[0] = g
        d_out[0] = -ADAM_LR * (m_hat / (jnp.sqrt(v_hat) + ADAM_EPS) + ADAM_WD * w_ref[0])
        m_out[0] = m1
        v_out[0] = v1

    row = pl.BlockSpec((1, tile, cols), lambda i, me: (0, i, 0))
    return pl.pallas_call(
        body, name=name, out_shape=[_sds((1, rows, cols))] * 4,
        grid_spec=pltpu.PrefetchScalarGridSpec(
            num_scalar_prefetch=1, grid=(rows // tile,),
            in_specs=[pl.BlockSpec((N_DEV, tile, cols), lambda i, me: (0, i, 0)),
                      pl.BlockSpec((1, tile, cols), lambda i, me: (me[0], i, 0)), row, row, row],
            out_specs=[row, row, row, row]),
        compiler_params=pltpu.CompilerParams(dimension_semantics=("arbitrary",), vmem_limit_bytes=VMEM_LIMIT),
    )(_my_slab(), parts, sent, w, m, v)


BIG = {
    "w_in": ((N_IN // N_DEV, D_MODEL), False, N_IN // N_DEV // 3),
    "w_glu": ((S5_WIDTH // N_DEV, S5_WIDTH), False, S5_WIDTH // N_DEV),
    "w_pa": ((S5_WIDTH, D_MODEL // N_DEV), True, S5_WIDTH),
    "w_pb": ((HG_WIDTH, D_MODEL // N_DEV), True, HG_WIDTH),
    "w_out": ((D_MODEL // N_DEV, D_MODEL), False, D_MODEL // N_DEV),
    "w_up": ((2 * D_FF // N_DEV, D_MODEL), False, 2 * D_FF // N_DEV // 4),
    "w_conv": ((CONV_W, 2 * D_FF // N_DEV), True, CONV_W),
    "w_down": ((D_FF // N_DEV, D_MODEL), False, D_FF // N_DEV // 2),
}
TRANSPOSED = ("w_in", "w_up", "s5_b_re", "s5_b_im")
UNALIGNED_COLS = ("w_conv",)


def _stored(n, arr):
    return jnp.swapaxes(arr, -1, -2) if n in TRANSPOSED else arr


def _join_shards(n, parts):
    (a, b), by_cols, _ = BIG[n]
    if not by_cols:
        return parts.reshape(N_DEV * a, b)
    if n in UNALIGNED_COLS:
        return _join_cols(parts, "join_" + n, min(a, 256))
    return parts.transpose(1, 0, 2).reshape(a, N_DEV * b)


def _split_shards(n, full):
    (a, b), by_cols, _ = BIG[n]
    if not by_cols:
        return full.reshape(N_DEV, a, b)
    if n in UNALIGNED_COLS:
        return _split_cols(full, "split_" + n, min(a, 256))
    return full.reshape(a, N_DEV, b).transpose(1, 0, 2)


SMALL_CORE = {
    "s5_b_re": GSC, "s5_b_im": GSC, "s5_c_re": GSC, "s5_c_im": GSC,
    "g_mix": (1, D_MODEL), "g_ffn": (1, D_MODEL), "g_final": (1, D_MODEL), "s5_d": (1, S5_WIDTH),
    "b_glu": (1, S5_WIDTH), "hg_norm_gain": (1, HG_WIDTH), "hg_lb_logits": (2, HG_WIDTH), "b_conv": (1, 2 * D_FF),
    "s5_log_dt": (1, S5_GROUPS), "s5_a_re": (S5_GROUPS, S5_STATE), "s5_a_im": (S5_GROUPS, S5_STATE), "loss": (1, 1),
}
BLOCK_ROWS = 32


def _small_rows():
    rows, r = {}, 0
    for n, core in SMALL_CORE.items():
        rows[n] = r
        r += BLOCK_ROWS if len(core) == 3 else -(-math.prod(core) // PACK_W)
    return rows, -(-r // SUBLANES) * SUBLANES


SMALL_ROW, SMALL_ROWS = _small_rows()


def _small_pieces(name):
    r, core = SMALL_ROW[name], SMALL_CORE[name]
    if len(core) == 3:
        return [((g, slice(None), slice(None)), slice(r + S5_GROUP * (g % 2), r + S5_GROUP * (g % 2 + 1)),
                 slice(S5_STATE * (g // 2), S5_STATE * (g // 2 + 1))) for g in range(S5_GROUPS)]
    pieces = []
    for i in range(core[0]):
        for c0 in range(0, core[1], PACK_W):
            w, flat = min(PACK_W, core[1] - c0), i * core[1] + c0
            pieces.append(((slice(i, i + 1), slice(c0, c0 + w)), slice(r + flat // PACK_W, r + flat // PACK_W + 1),
                           slice(flat % PACK_W, flat % PACK_W + w)))
    return pieces


def _core_index(ref, name, idx):
    return (0,) * (len(ref.shape) - len(SMALL_CORE[name])) + idx


def _pack_small_grads(grads):
    names = list(SMALL_CORE)

    def body(*refs):
        pack = refs[-1]
        pack[...] = jnp.zeros_like(pack)
        for ref, n in zip(refs, names):
            for idx, rows, lanes in _small_pieces(n):
                pack[rows, lanes] = ref[_core_index(ref, n, idx)]

    return _pcall(body, "pack_small_grads", (1,), [_full(grads[n].shape) for n in names],
                  _full((SMALL_ROWS, PACK_W)), _sds((SMALL_ROWS, PACK_W)))(*[grads[n] for n in names])


def _adamw_small(parts, sent, names, rows, given, name):
    lo, hi = rows
    k = len(names)
    shapes = [given[n].shape for n in names]

    def body(*refs):
        me, p_ref, s_ref, ins, outs = refs[0][0], refs[1], refs[2], refs[3:3 + 3 * k], refs[3 + 3 * k:3 + 7 * k]
        packs, results = refs[3 + 7 * k:6 + 7 * k], refs[6 + 7 * k:]
        for j, pack in enumerate(packs):
            pack[...] = jnp.zeros_like(pack)
            for ref, n in zip(ins[j * k:(j + 1) * k], names):
                for idx, prow, lanes in _small_pieces(n):
                    pack[slice(prow.start - lo, prow.stop - lo), lanes] = ref[_core_index(ref, n, idx)]
        mine = s_ref[lo:hi, :]
        g = jnp.where(me == 0, mine, p_ref[0, lo:hi, :])
        for d in range(1, N_DEV):
            g = g + jnp.where(me == d, mine, p_ref[d, lo:hi, :])
        m1 = ADAM_B1 * packs[1][...] + (1.0 - ADAM_B1) * g
        v1 = ADAM_B2 * packs[2][...] + (1.0 - ADAM_B2) * (g * g)
        m_hat = m1 / (1.0 - ADAM_B1 ** ADAM_STEP)
        v_hat = v1 / (1.0 - ADAM_B2 ** ADAM_STEP)
        results[0][...] = g
        results[1][...] = -ADAM_LR * (m_hat / (jnp.sqrt(v_hat) + ADAM_EPS) + ADAM_WD * packs[0][...])
        results[2][...] = m1
        results[3][...] = v1
        for j, result in enumerate(results):
            for ref, n in zip(outs[j * k:(j + 1) * k], names):
                for idx, prow, lanes in _small_pieces(n):
                    ref[_core_index(ref, n, idx)] = result[slice(prow.start - lo, prow.stop - lo), lanes]

    flat = _pcall(body, name, (1,),
                  [pl.BlockSpec(memory_space=pltpu.SMEM), _full(parts.shape), _full(sent.shape)]
                  + [_full(s) for s in shapes] * 3,
                  [_full(s) for s in shapes] * 4, [_sds(s) for s in shapes] * 4,
                  scratch=[pltpu.VMEM((hi - lo, PACK_W), F32)] * 7,
                  )(_my_slab(), parts, sent, *[given[pre + n] for pre in ("", "m_", "v_") for n in names])
    return {n: [flat[j * k + i] for j in range(4)] for i, n in enumerate(names)}


def kernel(x, g_mix, w_in, s5_a_re, s5_a_im, s5_log_dt, s5_b_re, s5_b_im, s5_c_re, s5_c_im, s5_d, w_glu, b_glu, hg_lb_logits, hg_norm_gain, w_pa, w_pb, w_out, g_ffn, w_up, w_conv, b_conv, w_down, g_final, loss_target, m_g_mix, m_w_in, m_s5_a_re, m_s5_a_im, m_s5_log_dt, m_s5_b_re, m_s5_b_im, m_s5_c_re, m_s5_c_im, m_s5_d, m_w_glu, m_b_glu, m_hg_lb_logits, m_hg_norm_gain, m_w_pa, m_w_pb, m_w_out, m_g_ffn, m_w_up, m_w_conv, m_b_conv, m_w_down, m_g_final, v_g_mix, v_w_in, v_s5_a_re, v_s5_a_im, v_s5_log_dt, v_s5_b_re, v_s5_b_im, v_s5_c_re, v_s5_c_im, v_s5_d, v_w_glu, v_b_glu, v_hg_lb_logits, v_hg_norm_gain, v_w_pa, v_w_pb, v_w_out, v_g_ffn, v_w_up, v_w_conv, v_b_conv, v_w_down, v_g_final):
    given = dict(locals())
    small_names = [n for n, _ in SMALL]

    pay = {n: given[n][0] if n == "w_conv" else _stored(n, given[n])[0].astype(BF16) for n in BIG}
    groups = {"in": ["w_in"], "mix": ["w_glu", "w_pa", "w_pb", "w_out"], "ffn": ["w_up", "w_down", "w_conv"]}
    gathers, order = {}, pay["w_in"]
    for grp, names in groups.items():
        gathers[grp], order = _gather_start("gather_" + grp + "_start", [pay[n] for n in names], order)

    def weights(grp, *after):
        if grp == "in":
            after = (*after, order)
        state, forwarded = _gather_forward("gather_" + grp + "_forward", gathers[grp], *after)
        got, _ = _gather_wait("gather_" + grp + "_wait", state, forwarded)
        return {n: _join_shards(n, g) for n, g in zip(groups[grp], got)}

    in_flight, started = [], []

    def emit(grads):
        names = list(grads)
        state, token = _exchange_start("grads_" + names[0] + "_start",
                                       [(_split_shards(n, grads[n]), True) for n in names], grads[names[0]],
                                       place_own=False)
        in_flight.append((names, state))
        return token

    def emit_small(grads):
        pack = _pack_small_grads(grads)
        state, token = _gather_start("grads_small_start", [pack], pack)
        in_flight.append((["small"], state))
        started.append(token)

    sp = {n: (given[n] if n in ("g_final", "hg_lb_logits") else _stored(n, given[n])[0]) for n in small_names}
    sp["g_mix"] = _after(sp["g_mix"], order)
    dx = _local_step(x, loss_target, weights, sp, emit, emit_small)

    res = {}
    after = [started[-1]]
    in_flight.insert(-1, in_flight.pop())
    for names, state in in_flight:
        if names == ["small"]:
            state, forwarded = _gather_forward("grads_small_forward", state, *after)
            parts, sent = _gather_wait("grads_small_wait", state, forwarded)
        else:
            parts, sent = _exchange_wait("grads_" + names[0] + "_wait", state, *after)
        if names != ["small"]:
            after = []
            for n, part, mine in zip(names, parts, sent):
                raw = _adamw(part, mine, *[_stored(n, given[pre + n]) for pre in ("", "m_", "v_")], "adamw_" + n,
                             BIG[n][2])
                res[n] = [_stored(n, r) for r in raw]
                after.append(raw[0])
            continue
        sgiven = {pre + n: _stored(n, given[pre + n]) for pre in ("", "m_", "v_") for n in small_names}
        for pre in ("", "m_", "v_"):
            sgiven[pre + "g_final"] = given[pre + "g_final"].reshape(1, D_MODEL)
            sgiven[pre + "loss"] = jnp.zeros((1, 1), F32)
        raw = _adamw_small(parts[0], sent[0], list(SMALL_CORE), (0, SMALL_ROWS), sgiven, "adamw_small")
        res.update({n: [_stored(n, r) for r in raw[n]] for n in small_names})
        res["g_final"] = [r.reshape(D_MODEL) for r in raw["g_final"]]
        total_loss = raw["loss"][0].reshape(())
        after = [raw["s5_b_re"][0], raw["g_mix"][0]]
    return (total_loss, dx, *[res[n][0] for n in WEIGHT_ORDER], *[res[n][1] for n in WEIGHT_ORDER],
            *[res[n][2] for n in WEIGHT_ORDER], *[res[n][3] for n in WEIGHT_ORDER])
```

```python
import functools
import math

import jax
import jax.numpy as jnp
from jax import lax
from jax.experimental import pallas as pl
from jax.experimental.pallas import tpu as pltpu

F32 = jnp.float32
BF16 = jnp.bfloat16

D_MODEL = 1024
S5_WIDTH = 512
S5_GROUP = 16
S5_GROUPS = 32
S5_STATE = 64
S5_N = S5_GROUPS * S5_STATE
HG_WIDTH = 512
HG_HEAD = 128
HG_HEADS = 4
D_FF = 2816
CONV_W = 3
CHUNK = 64
N_IN = S5_WIDTH + 4 * HG_WIDTH + 2 * D_MODEL
EPS = 1e-6
QSCALE = HG_HEAD ** -0.5

ADAM_LR = 0.001
ADAM_B1 = 0.9
ADAM_B2 = 0.999
ADAM_EPS = 1e-08
ADAM_WD = 0.01
ADAM_STEP = 10

N_DEV = 8
V7X_VMEM_BYTES = 64 * 1024 * 1024
VMEM_LIMIT = V7X_VMEM_BYTES * 7 // 8
SUBLANES = 8
PACK_W = 1024

SMALL = (
    ("g_mix", (1, D_MODEL)),
    ("s5_a_re", (1, S5_GROUPS, S5_STATE)),
    ("s5_a_im", (1, S5_GROUPS, S5_STATE)),
    ("s5_log_dt", (1, S5_GROUPS)),
    ("s5_b_re", (1, S5_GROUPS, S5_STATE, S5_GROUP)),
    ("s5_b_im", (1, S5_GROUPS, S5_STATE, S5_GROUP)),
    ("s5_c_re", (1, S5_GROUPS, S5_GROUP, S5_STATE)),
    ("s5_c_im", (1, S5_GROUPS, S5_GROUP, S5_STATE)),
    ("s5_d", (1, S5_WIDTH)),
    ("b_glu", (1, S5_WIDTH)),
    ("hg_lb_logits", (2, HG_WIDTH)),
    ("hg_norm_gain", (1, HG_WIDTH)),
    ("g_ffn", (1, D_MODEL)),
    ("b_conv", (1, 2 * D_FF)),
    ("g_final", (D_MODEL,)),
)
WEIGHT_ORDER = ("g_mix", "w_in", "s5_a_re", "s5_a_im", "s5_log_dt", "s5_b_re", "s5_b_im", "s5_c_re", "s5_c_im",
                "s5_d", "w_glu", "b_glu", "hg_lb_logits", "hg_norm_gain", "w_pa", "w_pb", "w_out", "g_ffn",
                "w_up", "w_conv", "b_conv", "w_down", "g_final")


def _pcall(body, name, grid, in_specs, out_specs, out_shape, scratch=()):
    return pl.pallas_call(
        body, name=name, grid=grid, in_specs=in_specs, out_specs=out_specs, out_shape=out_shape,
        scratch_shapes=list(scratch),
        compiler_params=pltpu.CompilerParams(dimension_semantics=("arbitrary",) * len(grid),
                                             vmem_limit_bytes=VMEM_LIMIT),
    )


def _full(shape):
    return pl.BlockSpec(shape, lambda *_: (0,) * len(shape))


def _sds(shape, dtype=F32):
    return jax.ShapeDtypeStruct(shape, dtype)


def _dot(a, b):
    return jnp.dot(a.astype(BF16), b.astype(BF16), preferred_element_type=F32)


def _dot_nt(a, b):
    return lax.dot_general(a.astype(BF16), b.astype(BF16), (((1,), (1,)), ((), ())), preferred_element_type=F32)


def _dot_tn(a, b):
    return lax.dot_general(a.astype(BF16), b.astype(BF16), (((0,), (0,)), ((), ())), preferred_element_type=F32)


def _hdot(a, b):
    return jnp.dot(a, b, preferred_element_type=F32, precision=lax.Precision.HIGHEST)


def _hdot_tn(a, b):
    return lax.dot_general(a, b, (((0,), (0,)), ((), ())), preferred_element_type=F32,
                           precision=lax.Precision.HIGHEST)


def _sigmoid(x):
    return jax.nn.sigmoid(x)


GELU_C = math.sqrt(2.0 / math.pi)
GELU_A = 0.044715


def _gelu(x):
    return 0.5 * x * (1.0 + jnp.tanh(GELU_C * (x + GELU_A * (x * x * x))))


def _gelu_grad(x):
    t = jnp.tanh(GELU_C * (x + GELU_A * (x * x * x)))
    return 0.5 * (1.0 + t) + 0.5 * x * (1.0 - t * t) * (GELU_C * (1.0 + 3.0 * GELU_A * x * x))


def _cumsum_rows(v, reverse=False):
    n = v.shape[0]
    row = lax.broadcasted_iota(jnp.int32, v.shape, 0)
    s = 1
    while s < n:
        if reverse:
            v = v + jnp.where(row < n - s, pltpu.roll(v, n - s, axis=0), 0.0)
        else:
            v = v + jnp.where(row >= s, pltpu.roll(v, s, axis=0), 0.0)
        s *= 2
    return v


def _token_tile(seq):
    return min(256, seq)


def _s5_coeffs(a_re, a_im, ldt):
    dt = jnp.exp(ldt)
    mag = jnp.exp(a_re * dt)
    ang = a_im * dt
    lb_re = mag * jnp.cos(ang)
    lb_im = mag * jnp.sin(ang)
    den = a_re * a_re + a_im * a_im
    n_re = lb_re - 1.0
    n_im = lb_im
    co_re = (n_re * a_re + n_im * a_im) / den
    co_im = (n_im * a_re - n_re * a_im) / den
    return lb_re, lb_im, co_re, co_im


GS, GSC = (S5_GROUPS, S5_STATE), (S5_GROUPS, S5_GROUP, S5_STATE)


def _params_fwd(a_re, a_im, ldt, bt_re, bt_im, logits):
    def body(are, aim, ld, bre, bim, lg, lr_o, li_o, bbr_o, bbi_o, lb_o):
        lr, li, co_re, co_im = _s5_coeffs(are[...], aim[...], ld[...])
        lr_o[...] = lr
        li_o[...] = li
        for g in range(S5_GROUPS):
            cr, ci = co_re[g:g + 1, :], co_im[g:g + 1, :]
            bbr_o[g] = cr * bre[g] - ci * bim[g]
            bbi_o[g] = cr * bim[g] + ci * bre[g]
        lb_o[...] = _sigmoid(lg[0:1, :] - lg[1:2, :])

    return _pcall(body, "params_fwd", (1,),
                  [_full(GS), _full(GS), _full((S5_GROUPS, 1)), _full(GSC), _full(GSC), _full((2, HG_WIDTH))],
                  [_full(GS), _full(GS), _full(GSC), _full(GSC), _full((1, HG_WIDTH))],
                  [_sds(GS), _sds(GS), _sds(GSC), _sds(GSC), _sds((1, HG_WIDTH))],
                  )(a_re, a_im, ldt, bt_re, bt_im, logits)


def _params_bwd(a_re, a_im, ldt, bt_re, bt_im, logits, dlr, dli, dbbr, dbbi, dlb):
    def body(are, aim, ld, bre, bim, lg, dlr_r, dli_r, dbbr_r, dbbi_r, dlb_r,
             dare_o, daim_o, dld_o, dbre_o, dbim_o, dlg_o, dcr_ref, dci_ref):
        (_, _, co_re, co_im), vjp = jax.vjp(_s5_coeffs, are[...], aim[...], ld[...])
        for g in range(S5_GROUPS):
            cr, ci = co_re[g:g + 1, :], co_im[g:g + 1, :]
            gr, gi, br, bi = dbbr_r[g], dbbi_r[g], bre[g], bim[g]
            dbre_o[g] = cr * gr + ci * gi
            dbim_o[g] = cr * gi - ci * gr
            dcr_ref[g:g + 1, :] = jnp.sum(gr * br + gi * bi, axis=0, keepdims=True)
            dci_ref[g:g + 1, :] = jnp.sum(gi * br - gr * bi, axis=0, keepdims=True)
        dare, daim, dld = vjp((dlr_r[...], dli_r[...], dcr_ref[...], dci_ref[...]))
        dare_o[...] = dare
        daim_o[...] = daim
        dld_o[...] = dld
        lb = _sigmoid(lg[0:1, :] - lg[1:2, :])
        d0 = dlb_r[...] * lb * (1.0 - lb)
        dlg_o[0:1, :] = d0
        dlg_o[1:2, :] = -d0

    return _pcall(body, "params_bwd", (1,),
                  [_full(GS), _full(GS), _full((S5_GROUPS, 1)), _full(GSC), _full(GSC), _full((2, HG_WIDTH)),
                   _full(GS), _full(GS), _full(GSC), _full(GSC), _full((1, HG_WIDTH))],
                  [_full(GS), _full(GS), _full((S5_GROUPS, 1)), _full(GSC), _full(GSC), _full((2, HG_WIDTH))],
                  [_sds(GS), _sds(GS), _sds((S5_GROUPS, 1)), _sds(GSC), _sds(GSC), _sds((2, HG_WIDTH))],
                  scratch=[pltpu.VMEM(GS, F32), pltpu.VMEM(GS, F32)],
                  )(a_re, a_im, ldt, bt_re, bt_im, logits, dlr, dli, dbbr, dbbi, dlb)


def _band_blocks(m):
    g, r, c = m.shape
    gb = g // S5_BANDS
    m4 = m.astype(BF16).reshape(S5_BANDS, gb, r, c)
    on_diag = jnp.eye(gb, dtype=bool)[None, :, None, :, None]
    return jnp.where(on_diag, m4[:, :, :, None, :], 0).reshape(S5_BANDS, gb * r, gb * c)


def _diag_blocks(band, r, c):
    g, nb = band.shape[0] // r, band.shape[1] // c
    on_diag = (jnp.arange(g) % nb)[:, None, None, None] == jnp.arange(nb)[None, None, :, None]
    return jnp.sum(jnp.where(on_diag, band.reshape(g, r, nb, c), 0.0), axis=2)


def _in_proj(x, g_mix, w_in, tm):
    t = x.shape[0]

    def body(x_ref, g_ref, w_ref, u_ref, za_ref, zh_ref, zg_ref):
        xv = x_ref[...]
        r = lax.rsqrt(jnp.mean(xv * xv, axis=-1, keepdims=True) + EPS)
        u = (xv * r * g_ref[...]).astype(BF16)
        u_ref[...] = u
        za_ref[...] = _dot_nt(u, w_ref[0:S5_WIDTH, :])
        zh_ref[...] = _dot_nt(u, w_ref[S5_WIDTH:S5_WIDTH + 4 * HG_WIDTH, :])
        zg_ref[...] = _dot_nt(u, w_ref[S5_WIDTH + 4 * HG_WIDTH:, :]).astype(BF16)

    row = lambda w: pl.BlockSpec((tm, w), lambda i: (i, 0))
    return _pcall(body, "in_proj", (t // tm,),
                  [row(D_MODEL), _full((1, D_MODEL)), _full((N_IN, D_MODEL))],
                  [row(D_MODEL), row(S5_WIDTH), row(4 * HG_WIDTH), row(2 * D_MODEL)],
                  [_sds((t, D_MODEL), BF16), _sds((t, S5_WIDTH)), _sds((t, 4 * HG_WIDTH)),
                   _sds((t, 2 * D_MODEL), BF16)],
                  )(x, g_mix, w_in)


S5_LANES = 512
S5_BANDS = 4


def _band(q):
    return (slice(q * S5_WIDTH // S5_BANDS, (q + 1) * S5_WIDTH // S5_BANDS),
            slice(q * S5_N // S5_BANDS, (q + 1) * S5_N // S5_BANDS))


def _im(st):
    return slice(S5_N + st.start, S5_N + st.stop)


SCAN_UNROLL = 8


def _complex_scan(buf_ref, lam_ref, st_ref, nb, ts, reverse):
    lanes = [slice(cc * S5_LANES, (cc + 1) * S5_LANES) for cc in range(S5_N // S5_LANES)]
    chains = [(b, re) for b in range(nb) for re in lanes]
    nch = len(chains)
    wr = {re.start: lam_ref[0:1, re] for re in lanes}
    wi = {re.start: -lam_ref[1:2, re] if reverse else lam_ref[1:2, re] for re in lanes}

    def block(ib, carry):
        vr, vi = list(carry[:nch]), list(carry[nch:])
        first = ts - SCAN_UNROLL - ib * SCAN_UNROLL if reverse else ib * SCAN_UNROLL
        first = pl.multiple_of(first, SCAN_UNROLL)
        for k in range(SCAN_UNROLL):
            row = pl.ds(first + (SCAN_UNROLL - 1 - k if reverse else k), 1)
            for c, (b, re) in enumerate(chains):
                nr = wr[re.start] * vr[c] - wi[re.start] * vi[c] + buf_ref[b, row, re]
                ni = wr[re.start] * vi[c] + wi[re.start] * vr[c] + buf_ref[b, row, _im(re)]
                buf_ref[b, row, re] = nr
                buf_ref[b, row, _im(re)] = ni
                vr[c], vi[c] = nr, ni
        return tuple(vr + vi)

    init = tuple(st_ref[b, 0:1, re] for b, re in chains) + tuple(st_ref[b, 1:2, re] for b, re in chains)
    last = lax.fori_loop(0, ts // SCAN_UNROLL, block, init)
    for c, (b, re) in enumerate(chains):
        st_ref[b, 0:1, re] = last[c]
        st_ref[b, 1:2, re] = last[nch + c]


BAND_CH = S5_WIDTH // S5_BANDS
BAND_ST = S5_N // S5_BANDS


def _s5_fwd(za, b_bands, lam, c_bands, dskip, nb, seq, ts):
    nts = seq // ts

    def body(za_ref, br_ref, bi_ref, lam_ref, cr_ref, ci_ref, d_ref, xs_ref, y_ref, buf_ref, st_ref):
        @pl.when(pl.program_id(0) == 0)
        def _():
            st_ref[...] = jnp.zeros_like(st_ref)

        for b in range(nb):
            zav = za_ref[b]
            for q in range(S5_BANDS):
                ch, st = _band(q)
                buf_ref[b, :, st] = _dot(zav[:, ch], br_ref[q])
                buf_ref[b, :, _im(st)] = _dot(zav[:, ch], bi_ref[q])
        _complex_scan(buf_ref, lam_ref, st_ref, nb, ts, reverse=False)
        for b in range(nb):
            zav = za_ref[b]
            xs_ref[b] = buf_ref[b].astype(BF16)
            for q in range(S5_BANDS):
                ch, st = _band(q)
                y_ref[b, :, ch] = (_dot(xs_ref[b, :, st], cr_ref[q]) + _dot(xs_ref[b, :, _im(st)], ci_ref[q])
                                   + d_ref[:, ch] * zav[:, ch])

    tok = lambda w: pl.BlockSpec((nb, ts, w), lambda j: (0, j, 0))
    to_st, to_ch = _full((S5_BANDS, BAND_CH, BAND_ST)), _full((S5_BANDS, BAND_ST, BAND_CH))
    return _pcall(body, "s5_fwd", (nts,),
                  [tok(S5_WIDTH), to_st, to_st, _full((2, S5_N)), to_ch, to_ch, _full((1, S5_WIDTH))],
                  [tok(2 * S5_N), tok(S5_WIDTH)],
                  [_sds((nb, seq, 2 * S5_N), BF16), _sds((nb, seq, S5_WIDTH))],
                  scratch=[pltpu.VMEM((nb, ts, 2 * S5_N), F32), pltpu.VMEM((nb, 2, S5_N), F32)],
                  )(za, *b_bands, lam, *c_bands, dskip)


def _hgrn_gates(zq, zf, lbh):
    sf = _sigmoid(zf)
    f = lbh + (1.0 - lbh) * sf
    sq = _sigmoid(zq)
    qa = zq * sq * QSCALE
    bc = _cumsum_rows(jnp.log(f))
    bm = bc[CHUNK // 2 - 1:CHUNK // 2, :]
    bl = bc[CHUNK - 1:CHUNK, :]
    return sf, f, sq, qa, bc, bm, bl


def _hgrn_fwd(zh, lb, nb, seq):
    nc = seq // CHUNK

    def body(zh_ref, lb_ref, o_ref, sts_ref, st_ref):
        @pl.when(pl.program_id(0) == 0)
        def _():
            st_ref[...] = jnp.zeros_like(st_ref)

        causal = (lax.broadcasted_iota(jnp.int32, (CHUNK, CHUNK), 0)
                  >= lax.broadcasted_iota(jnp.int32, (CHUNK, CHUNK), 1))
        for b in range(nb):
            for h in range(HG_HEADS):
                hs = slice(h * HG_HEAD, (h + 1) * HG_HEAD)
                zq = zh_ref[b, :, h * HG_HEAD:(h + 1) * HG_HEAD]
                zf = zh_ref[b, :, HG_WIDTH + h * HG_HEAD:HG_WIDTH + (h + 1) * HG_HEAD]
                zi = zh_ref[b, :, 2 * HG_WIDTH + h * HG_HEAD:2 * HG_WIDTH + (h + 1) * HG_HEAD]
                _, f, _, qa, bc, bm, bl = _hgrn_gates(zq, zf, lb_ref[:, hs])
                k = 1.0 - f
                qt = qa * jnp.exp(bc - bm)
                kt = k * jnp.exp(bm - bc)
                qb = qa * jnp.exp(bc)
                kd = k * jnp.exp(bl - bc)
                st = st_ref[b, h]
                sts_ref[b, 0, h] = st
                a = jnp.where(causal, _dot_nt(qt, kt), 0.0)
                o_ref[b, :, hs] = _dot(a, zi) + _dot_nt(qb, st)
                st_ref[b, h] = st * jnp.exp(bl) + _dot_tn(zi, kd)

    return _pcall(body, "hgrn_fwd", (nc,),
                  [pl.BlockSpec((nb, CHUNK, 4 * HG_WIDTH), lambda c: (0, c, 0)), _full((1, HG_WIDTH))],
                  [pl.BlockSpec((nb, CHUNK, HG_WIDTH), lambda c: (0, c, 0)),
                   pl.BlockSpec((nb, 1, HG_HEADS, HG_HEAD, HG_HEAD), lambda c: (0, c, 0, 0, 0))],
                  [_sds((nb, seq, HG_WIDTH)), _sds((nb, nc, HG_HEADS, HG_HEAD, HG_HEAD))],
                  scratch=[pltpu.VMEM((nb, HG_HEADS, HG_HEAD, HG_HEAD), F32)])(zh, lb)


def _head_rms(o):
    parts = []
    for h in range(HG_HEADS):
        oh = o[:, h * HG_HEAD:(h + 1) * HG_HEAD]
        r = lax.rsqrt(jnp.mean(oh * oh, axis=-1, keepdims=True) + EPS)
        parts.append(jnp.broadcast_to(r, oh.shape))
    return jnp.concatenate(parts, axis=1)


def _head_mean(v):
    parts = []
    for h in range(HG_HEADS):
        vh = v[:, h * HG_HEAD:(h + 1) * HG_HEAD]
        parts.append(jnp.broadcast_to(jnp.mean(vh, axis=-1, keepdims=True), vh.shape))
    return jnp.concatenate(parts, axis=1)


def _mix_fwd(x, y0, o, zh, zgt, w_glu, b_glu, gain, w_pa, w_pb, w_out, g_ffn, tm):
    t = x.shape[0]

    def body(x_ref, y0_ref, o_ref, zg_ref, zgt_ref, wglu_ref, bglu_ref, gain_ref, wpa_ref, wpb_ref, wout_ref,
             gffn_ref, x1_ref, u2_ref, pa_ref, pb_ref, ya2_ref, yb_ref):
        ya1 = _gelu(y0_ref[...])
        s = _sigmoid(_dot(ya1, wglu_ref[...]) + bglu_ref[...])
        ya2 = (ya1 * s).astype(BF16)
        ov = o_ref[...]
        zg = zg_ref[...]
        yb = (ov * _head_rms(ov) * gain_ref[...] * (zg * _sigmoid(zg))).astype(BF16)
        ya2_ref[...] = ya2
        yb_ref[...] = yb
        pa = jnp.dot(ya2, wpa_ref[...], preferred_element_type=F32)
        pb = jnp.dot(yb, wpb_ref[...], preferred_element_type=F32)
        pa_ref[...] = pa.astype(BF16)
        pb_ref[...] = pb.astype(BF16)
        m = (_sigmoid(zgt_ref[:, 0:D_MODEL].astype(F32)) * pa
             + _sigmoid(zgt_ref[:, D_MODEL:].astype(F32)) * pb)
        x1 = x_ref[...] + _dot(m, wout_ref[...])
        x1_ref[...] = x1
        r = lax.rsqrt(jnp.mean(x1 * x1, axis=-1, keepdims=True) + EPS)
        u2_ref[...] = (x1 * r * gffn_ref[...]).astype(BF16)

    row = lambda w: pl.BlockSpec((tm, w), lambda i: (i, 0))
    return _pcall(body, "mix_fwd", (t // tm,),
                  [row(D_MODEL), row(S5_WIDTH), row(HG_WIDTH), pl.BlockSpec((tm, HG_WIDTH), lambda i: (i, 3)),
                   row(2 * D_MODEL), _full((S5_WIDTH, S5_WIDTH)), _full((1, S5_WIDTH)), _full((1, HG_WIDTH)),
                   _full((S5_WIDTH, D_MODEL)), _full((HG_WIDTH, D_MODEL)), _full((D_MODEL, D_MODEL)),
                   _full((1, D_MODEL))],
                  [row(D_MODEL), row(D_MODEL), row(D_MODEL), row(D_MODEL), row(S5_WIDTH), row(HG_WIDTH)],
                  [_sds((t, D_MODEL)), _sds((t, D_MODEL), BF16), _sds((t, D_MODEL), BF16), _sds((t, D_MODEL), BF16),
                   _sds((t, S5_WIDTH), BF16), _sds((t, HG_WIDTH), BF16)],
                  )(x, y0, o, zh, zgt, w_glu, b_glu, gain, w_pa, w_pb, w_out, g_ffn)


FF_COLS = 256
FF_UP_TILE = 1408


def _ffn_up(u2, w_up, tm):
    t = u2.shape[0]
    n = 2 * D_FF

    def body(u_ref, w_ref, h_ref):
        h_ref[...] = _dot_nt(u_ref[...], w_ref[...]).astype(BF16)

    return _pcall(body, "ffn_up", (n // FF_UP_TILE, t // tm),
                  [pl.BlockSpec((tm, D_MODEL), lambda j, i: (i, 0)),
                   pl.BlockSpec((FF_UP_TILE, D_MODEL), lambda j, i: (j, 0))],
                  pl.BlockSpec((tm, FF_UP_TILE), lambda j, i: (i, j)),
                  _sds((t, n), BF16))(u2, w_up)


HALO = 16


def _shift_matrix(tm):
    r = lax.broadcasted_iota(jnp.int32, (tm, tm), 0)
    c = lax.broadcasted_iota(jnp.int32, (tm, tm), 1)
    return jnp.where(r == c + 1, 1.0, 0.0).astype(BF16)


def _conv_cols(h_ref, halo_ref, valid, wc_ref, bc_ref, c0):
    cs = slice(c0, c0 + FF_COLS)
    cur = h_ref[:, cs].astype(F32)
    prev = jnp.where(valid, halo_ref[:, cs].astype(F32), 0.0)
    full = jnp.concatenate([prev, cur], axis=0)
    h1 = pltpu.roll(full, 1, axis=0)[HALO:]
    h2 = pltpu.roll(full, 2, axis=0)[HALO:]
    return h2 * wc_ref[0:1, cs] + h1 * wc_ref[1:2, cs] + cur * wc_ref[2:3, cs] + bc_ref[:, cs]


def _ffn_down_loss(h, x1, tgt, w_conv, b_conv, w_down, g_final, seq, tm):
    t = h.shape[0]
    tps = seq // tm
    n = 2 * D_FF

    def body(h_ref, halo_ref, x1_ref, tgt_ref, wc_ref, bc_ref, wd_ref, gf_ref,
             hc_ref, a_ref, dx2_ref, dx2b_ref, loss_ref, dgf_ref):
        i = pl.program_id(0)

        @pl.when(i == 0)
        def _():
            loss_ref[...] = jnp.zeros_like(loss_ref)
            dgf_ref[...] = jnp.zeros_like(dgf_ref)

        valid = (i % tps) != 0
        x2 = x1_ref[...]
        for j in range(D_FF // FF_COLS):
            gate = _conv_cols(h_ref, halo_ref, valid, wc_ref, bc_ref, j * FF_COLS)
            val = _conv_cols(h_ref, halo_ref, valid, wc_ref, bc_ref, D_FF + j * FF_COLS)
            hc_ref[:, j * FF_COLS:(j + 1) * FF_COLS] = gate.astype(BF16)
            hc_ref[:, D_FF + j * FF_COLS:D_FF + (j + 1) * FF_COLS] = val.astype(BF16)
            a = (gate * _sigmoid(gate) * val).astype(BF16)
            a_ref[:, j * FF_COLS:(j + 1) * FF_COLS] = a
            x2 = x2 + jnp.dot(a, wd_ref[j * FF_COLS:(j + 1) * FF_COLS, :], preferred_element_type=F32)
        r = lax.rsqrt(jnp.mean(x2 * x2, axis=-1, keepdims=True) + EPS)
        xn = x2 * r
        g = gf_ref[...]
        e = xn * g - tgt_ref[...]
        loss_ref[...] += (0.5 / D_MODEL) * jnp.sum(e * e).reshape(1, 1)
        dy = e * (1.0 / D_MODEL)
        dgf_ref[...] += jnp.sum(dy * xn, axis=0, keepdims=True)
        dxn = dy * g
        dx2 = r * (dxn - xn * jnp.mean(dxn * xn, axis=-1, keepdims=True))
        dx2_ref[...] = dx2
        dx2b_ref[...] = dx2.astype(BF16)

    row = lambda w: pl.BlockSpec((tm, w), lambda i: (i, 0))
    halo = pl.BlockSpec((HALO, n), lambda i: (jnp.maximum(i * (tm // HALO) - 1, 0), 0))
    return _pcall(body, "ffn_down_loss", (t // tm,),
                  [row(n), halo, row(D_MODEL), row(D_MODEL), _full((CONV_W, n)), _full((1, n)),
                   _full((D_FF, D_MODEL)), _full((1, D_MODEL))],
                  [row(n), row(D_FF), row(D_MODEL), row(D_MODEL), _full((1, 1)), _full((1, D_MODEL))],
                  [_sds((t, n), BF16), _sds((t, D_FF), BF16), _sds((t, D_MODEL)), _sds((t, D_MODEL), BF16),
                   _sds((1, 1)), _sds((1, D_MODEL))],
                  )(h, h, x1, tgt, w_conv, b_conv, w_down, g_final)


def _wgrad(a, b, name, tn, out_dtype=F32, band=None, after=None):
    t, m = a.shape
    n = b.shape[1] if band is None else band
    nbands = 1 if band is None else b.shape[1] // band
    after = b if after is None else after

    def body(a_ref, b_ref, after_ref, o_ref):
        o_ref[...] = _dot_tn(a_ref[...], b_ref[...]).astype(out_dtype)

    return _pcall(body, name, (m // tn,),
                  [pl.BlockSpec((t, tn), lambda i: (0, i)), pl.BlockSpec((t, n), lambda i: (0, i % nbands)),
                   pl.BlockSpec(memory_space=pl.ANY)],
                  pl.BlockSpec((tn, n), lambda i: (i, 0)), _sds((m, n), out_dtype))(a, b, after)


def _ffn_bwd_act(dx2b, hc, w_down, tm):
    t = hc.shape[0]
    n = 2 * D_FF

    def body(dx2_ref, hc_ref, wd_ref, dhc_ref, dbc_ref):
        @pl.when(pl.program_id(0) == 0)
        def _():
            dbc_ref[...] = jnp.zeros_like(dbc_ref)

        dx2 = dx2_ref[...]
        for j in range(D_FF // FF_COLS):
            gs = slice(j * FF_COLS, (j + 1) * FF_COLS)
            vs = slice(D_FF + j * FF_COLS, D_FF + (j + 1) * FF_COLS)
            gate = hc_ref[:, gs].astype(F32)
            val = hc_ref[:, vs].astype(F32)
            da = _dot_nt(dx2, wd_ref[gs, :])
            sg = _sigmoid(gate)
            dgate = da * val * (sg * (1.0 + gate * (1.0 - sg)))
            dval = da * (gate * sg)
            dhc_ref[:, gs] = dgate.astype(BF16)
            dhc_ref[:, vs] = dval.astype(BF16)
            dbc_ref[:, gs] += jnp.sum(dgate, axis=0, keepdims=True)
            dbc_ref[:, vs] += jnp.sum(dval, axis=0, keepdims=True)

    row = lambda w: pl.BlockSpec((tm, w), lambda i: (i, 0))
    return _pcall(body, "ffn_bwd_act", (t // tm,),
                  [row(D_MODEL), row(n), _full((D_FF, D_MODEL))],
                  [row(n), _full((1, n))],
                  [_sds((t, n), BF16), _sds((1, n))],
                  )(dx2b, hc, w_down)


def _ffn_bwd_up(dhc, h, dx2, x1, w_conv, w_up, g_ffn, seq, tm):
    t = dhc.shape[0]
    tps = seq // tm
    n = 2 * D_FF
    last = t // HALO - 1

    def body(dhc_ref, halo_ref, h_ref, dx2_ref, x1_ref, wc_ref, wu_ref, gf_ref,
             dh_ref, dx1_ref, dx1b_ref, dgf_ref, dwc_ref):
        i = pl.program_id(0)

        @pl.when(i == 0)
        def _():
            dgf_ref[...] = jnp.zeros_like(dgf_ref)
            dwc_ref[...] = jnp.zeros_like(dwc_ref)

        valid = ((i + 1) % tps) != 0
        du2 = jnp.zeros((tm, D_MODEL), F32)
        for j in range(n // FF_COLS):
            cs = slice(j * FF_COLS, (j + 1) * FF_COLS)
            cur = dhc_ref[:, cs].astype(F32)
            nxt = jnp.where(valid, halo_ref[:, cs].astype(F32), 0.0)
            full = jnp.concatenate([cur, nxt], axis=0)
            d1 = pltpu.roll(full, tm + HALO - 1, axis=0)[:tm]
            d2 = pltpu.roll(full, tm + HALO - 2, axis=0)[:tm]
            dh = (cur * wc_ref[2:3, cs] + d1 * wc_ref[1:2, cs] + d2 * wc_ref[0:1, cs]).astype(BF16)
            dh_ref[:, cs] = dh
            du2 = du2 + _dot(dh, wu_ref[cs, :])
            hv = h_ref[:, cs].astype(F32)
            dwc_ref[0:1, cs] += jnp.sum(hv * d2, axis=0, keepdims=True)
            dwc_ref[1:2, cs] += jnp.sum(hv * d1, axis=0, keepdims=True)
            dwc_ref[2:3, cs] += jnp.sum(hv * cur, axis=0, keepdims=True)
        x1 = x1_ref[...]
        r = lax.rsqrt(jnp.mean(x1 * x1, axis=-1, keepdims=True) + EPS)
        xn = x1 * r
        dgf_ref[...] += jnp.sum(du2 * xn, axis=0, keepdims=True)
        dxn = du2 * gf_ref[...]
        dx1 = dx2_ref[...] + r * (dxn - xn * jnp.mean(dxn * xn, axis=-1, keepdims=True))
        dx1_ref[...] = dx1
        dx1b_ref[...] = dx1.astype(BF16)

    row = lambda w: pl.BlockSpec((tm, w), lambda i: (i, 0))
    halo = pl.BlockSpec((HALO, n), lambda i: (jnp.minimum((i + 1) * (tm // HALO), last), 0))
    return _pcall(body, "ffn_bwd_up", (t // tm,),
                  [row(n), halo, row(n), row(D_MODEL), row(D_MODEL), _full((CONV_W, n)), _full((n, D_MODEL)),
                   _full((1, D_MODEL))],
                  [row(n), row(D_MODEL), row(D_MODEL), _full((1, D_MODEL)), _full((CONV_W, n))],
                  [_sds((t, n), BF16), _sds((t, D_MODEL)), _sds((t, D_MODEL), BF16), _sds((1, D_MODEL)),
                   _sds((CONV_W, n))],
                  )(dhc, dhc, h, dx2, x1, w_conv, w_up, g_ffn)


def _mix_bwd(dx1, y0, o, zh, zgt, pa, pb, w_glu, b_glu, gain, w_pa, w_pb, w_out, tm):
    t = dx1.shape[0]

    def body(dx1_ref, y0_ref, o_ref, zg_ref, zgt_ref, pa_ref, pb_ref, wglu_ref, bglu_ref, gain_ref, wpa_ref,
             wpb_ref, wout_ref,
             dy0_ref, do_ref, dzg_ref, dzgt_ref, m_ref, dpa_ref, dpb_ref, ya1_ref, dpre_ref, dbglu_ref, dgain_ref):
        @pl.when(pl.program_id(0) == 0)
        def _():
            dbglu_ref[...] = jnp.zeros_like(dbglu_ref)
            dgain_ref[...] = jnp.zeros_like(dgain_ref)

        dm = _dot_nt(dx1_ref[...], wout_ref[...])
        sga = _sigmoid(zgt_ref[:, 0:D_MODEL].astype(F32))
        sgb = _sigmoid(zgt_ref[:, D_MODEL:].astype(F32))
        pa = pa_ref[...].astype(F32)
        pb = pb_ref[...].astype(F32)
        m_ref[...] = (sga * pa + sgb * pb).astype(BF16)
        dzgt_ref[:, 0:D_MODEL] = (dm * pa * sga * (1.0 - sga)).astype(BF16)
        dzgt_ref[:, D_MODEL:] = (dm * pb * sgb * (1.0 - sgb)).astype(BF16)
        dpa = (dm * sga).astype(BF16)
        dpb = (dm * sgb).astype(BF16)
        dpa_ref[...] = dpa
        dpb_ref[...] = dpb
        dya2 = _dot_nt(dpa, wpa_ref[...])
        dyb = _dot_nt(dpb, wpb_ref[...])
        y0 = y0_ref[...]
        ya1 = _gelu(y0)
        ya1_ref[...] = ya1.astype(BF16)
        s = _sigmoid(_dot(ya1, wglu_ref[...]) + bglu_ref[...])
        dpre = dya2 * ya1 * s * (1.0 - s)
        dpre_ref[...] = dpre.astype(BF16)
        dbglu_ref[...] += jnp.sum(dpre, axis=0, keepdims=True)
        dya1 = dya2 * s + _dot_nt(dpre, wglu_ref[...])
        dy0_ref[...] = dya1 * _gelu_grad(y0)
        ov = o_ref[...]
        zg = zg_ref[...]
        oh = ov * _head_rms(ov)
        on = oh * gain_ref[...]
        sz = _sigmoid(zg)
        dzg_ref[...] = (dyb * on * (sz * (1.0 + zg * (1.0 - sz)))).astype(BF16)
        don = dyb * (zg * sz)
        dgain_ref[...] += jnp.sum(don * oh, axis=0, keepdims=True)
        doh = don * gain_ref[...]
        do_ref[...] = _head_rms(ov) * (doh - oh * _head_mean(doh * oh))

    row = lambda w: pl.BlockSpec((tm, w), lambda i: (i, 0))
    return _pcall(body, "mix_bwd", (t // tm,),
                  [row(D_MODEL), row(S5_WIDTH), row(HG_WIDTH), pl.BlockSpec((tm, HG_WIDTH), lambda i: (i, 3)),
                   row(2 * D_MODEL), row(D_MODEL), row(D_MODEL), _full((S5_WIDTH, S5_WIDTH)), _full((1, S5_WIDTH)),
                   _full((1, HG_WIDTH)), _full((S5_WIDTH, D_MODEL)), _full((HG_WIDTH, D_MODEL)),
                   _full((D_MODEL, D_MODEL))],
                  [row(S5_WIDTH), row(HG_WIDTH), row(HG_WIDTH), row(2 * D_MODEL), row(D_MODEL), row(D_MODEL),
                   row(D_MODEL), row(S5_WIDTH), row(S5_WIDTH), _full((1, S5_WIDTH)), _full((1, HG_WIDTH))],
                  [_sds((t, S5_WIDTH)), _sds((t, HG_WIDTH)), _sds((t, HG_WIDTH), BF16), _sds((t, 2 * D_MODEL), BF16),
                   _sds((t, D_MODEL), BF16), _sds((t, D_MODEL), BF16), _sds((t, D_MODEL), BF16),
                   _sds((t, S5_WIDTH), BF16), _sds((t, S5_WIDTH), BF16), _sds((1, S5_WIDTH)), _sds((1, HG_WIDTH))],
                  )(dx1, y0, o, zh, zgt, pa, pb, w_glu, b_glu, gain, w_pa, w_pb, w_out)


def _s5_bwd(dy0, za, xs, c_bands, b_bands, lam, dskip, nb, seq, ts):
    nts = seq // ts

    def body(dy0_ref, za_ref, xs_ref, halo_ref, cr_ref, ci_ref, br_ref, bi_ref, lam_ref, d_ref,
             dza_ref, a_ref, dlam_ref, dd_ref, acc_ref, st_ref):
        j = pl.program_id(0)

        @pl.when(j == 0)
        def _():
            dlam_ref[...] = jnp.zeros_like(dlam_ref)
            dd_ref[...] = jnp.zeros_like(dd_ref)
            st_ref[...] = jnp.zeros_like(st_ref)

        for b in range(nb):
            dy0 = dy0_ref[b]
            for q in range(S5_BANDS):
                ch, st = _band(q)
                acc_ref[b, :, st] = _dot(dy0[:, ch], cr_ref[q])
                acc_ref[b, :, _im(st)] = _dot(dy0[:, ch], ci_ref[q])
        _complex_scan(acc_ref, lam_ref, st_ref, nb, ts, reverse=True)
        shift = _shift_matrix(ts)
        top = lax.broadcasted_iota(jnp.int32, (SUBLANES, S5_LANES), 0) == 0
        for b in range(nb):
            a_ref[b] = acc_ref[b].astype(BF16)
            first = jnp.where(j == nts - 1, 0.0, halo_ref[b, HALO - 1:HALO, :].astype(F32))

            def shifted(cols):
                xp = jnp.dot(shift, xs_ref[b, :, cols], preferred_element_type=F32)
                return jnp.concatenate([xp[:SUBLANES] + jnp.where(top, first[:, cols], 0.0), xp[SUBLANES:]], axis=0)

            for cc in range(S5_N // S5_LANES):
                re = slice(cc * S5_LANES, (cc + 1) * S5_LANES)
                ar, ai, xr, xi = acc_ref[b, :, re], acc_ref[b, :, _im(re)], shifted(re), shifted(_im(re))
                dlam_ref[0:1, re] += jnp.sum(ar * xr + ai * xi, axis=0, keepdims=True)
                dlam_ref[1:2, re] += jnp.sum(ai * xr - ar * xi, axis=0, keepdims=True)
            dy0 = dy0_ref[b]
            for q in range(S5_BANDS):
                ch, st = _band(q)
                dza_ref[b, :, ch] = (_dot(a_ref[b, :, st], br_ref[q]) + _dot(a_ref[b, :, _im(st)], bi_ref[q])
                                     + d_ref[:, ch] * dy0[:, ch]).astype(BF16)
            dd_ref[...] += jnp.sum(dy0 * za_ref[b], axis=0, keepdims=True)

    tile = lambda j: nts - 1 - j
    tok = lambda w: pl.BlockSpec((nb, ts, w), lambda j: (0, tile(j), 0))
    halo = pl.BlockSpec((nb, HALO, 2 * S5_N), lambda j: (0, jnp.maximum(tile(j) * (ts // HALO) - 1, 0), 0))
    to_st, to_ch = _full((S5_BANDS, BAND_CH, BAND_ST)), _full((S5_BANDS, BAND_ST, BAND_CH))
    return _pcall(body, "s5_bwd", (nts,),
                  [tok(S5_WIDTH), tok(S5_WIDTH), tok(2 * S5_N), halo, to_st, to_st, to_ch, to_ch,
                   _full((2, S5_N)), _full((1, S5_WIDTH))],
                  [tok(S5_WIDTH), tok(2 * S5_N), _full((2, S5_N)), _full((1, S5_WIDTH))],
                  [_sds((nb, seq, S5_WIDTH), BF16), _sds((nb, seq, 2 * S5_N), BF16), _sds((2, S5_N)),
                   _sds((1, S5_WIDTH))],
                  scratch=[pltpu.VMEM((nb, ts, 2 * S5_N), F32), pltpu.VMEM((nb, 2, S5_N), F32)],
                  )(dy0, za, xs, xs, *c_bands, *b_bands, lam, dskip)


def _hgrn_bwd(zh, do, sts, lb, nb, seq):
    nc = seq // CHUNK

    def body(zh_ref, do_ref, sts_ref, lb_ref, dz_ref, dlb_ref, dst_ref):
        @pl.when(pl.program_id(0) == 0)
        def _():
            dst_ref[...] = jnp.zeros_like(dst_ref)
            dlb_ref[...] = jnp.zeros_like(dlb_ref)

        row = lax.broadcasted_iota(jnp.int32, (CHUNK, CHUNK), 0)
        causal = row >= lax.broadcasted_iota(jnp.int32, (CHUNK, CHUNK), 1)
        last_row = lax.broadcasted_iota(jnp.int32, (CHUNK, HG_HEAD), 0) == CHUNK - 1
        for b in range(nb):
            for h in range(HG_HEADS):
                hs = slice(h * HG_HEAD, (h + 1) * HG_HEAD)
                zq = zh_ref[b, :, h * HG_HEAD:(h + 1) * HG_HEAD]
                zf = zh_ref[b, :, HG_WIDTH + h * HG_HEAD:HG_WIDTH + (h + 1) * HG_HEAD]
                zi = zh_ref[b, :, 2 * HG_WIDTH + h * HG_HEAD:2 * HG_WIDTH + (h + 1) * HG_HEAD]
                lbh = lb_ref[:, hs]
                sf, f, sq, qa, bc, bm, bl = _hgrn_gates(zq, zf, lbh)
                k = 1.0 - f
                e_qt = jnp.exp(bc - bm)
                e_kt = jnp.exp(bm - bc)
                e_b = jnp.exp(bc)
                e_kd = jnp.exp(bl - bc)
                e_l = jnp.exp(bl)
                qt, kt, qb, kd = qa * e_qt, k * e_kt, qa * e_b, k * e_kd
                a = jnp.where(causal, _dot_nt(qt, kt), 0.0)
                st = sts_ref[b, 0, h]
                dst = dst_ref[b, h]
                dov = do_ref[b, :, hs]
                da = jnp.where(causal, _dot_nt(dov, zi), 0.0)
                dqt = _hdot(da, kt)
                dkt = _hdot_tn(da, qt)
                dqb = _dot(dov, st)
                di = _dot_tn(a, dov) + _dot_nt(kd, dst)
                dkd = _dot(zi, dst)
                de_l = jnp.sum(dst * st, axis=0, keepdims=True)
                dst_ref[b, h] = dst * e_l + _dot_tn(dov, qb)
                dqa = dqt * e_qt + dqb * e_b
                dk = dkt * e_kt + dkd * e_kd
                dbl = jnp.sum(dkd * kd, axis=0, keepdims=True) + de_l * e_l
                db = dqt * qt - dkt * kt + dqb * qb - dkd * kd + jnp.where(last_row, dbl, 0.0)
                df = _cumsum_rows(db, reverse=True) / f - dk
                dzq = dqa * QSCALE * (sq * (1.0 + zq * (1.0 - sq)))
                dzf = df * (1.0 - lbh) * sf * (1.0 - sf)
                dz_ref[b, :, h * HG_HEAD:(h + 1) * HG_HEAD] = dzq.astype(BF16)
                dz_ref[b, :, HG_WIDTH + h * HG_HEAD:HG_WIDTH + (h + 1) * HG_HEAD] = dzf.astype(BF16)
                dz_ref[b, :, 2 * HG_WIDTH + h * HG_HEAD:2 * HG_WIDTH + (h + 1) * HG_HEAD] = di.astype(BF16)
                dlb_ref[:, hs] += jnp.sum(df * (1.0 - sf), axis=0, keepdims=True)

    rev = lambda c: nc - 1 - c
    return _pcall(body, "hgrn_bwd", (nc,),
                  [pl.BlockSpec((nb, CHUNK, 4 * HG_WIDTH), lambda c: (0, rev(c), 0)),
                   pl.BlockSpec((nb, CHUNK, HG_WIDTH), lambda c: (0, rev(c), 0)),
                   pl.BlockSpec((nb, 1, HG_HEADS, HG_HEAD, HG_HEAD), lambda c: (0, rev(c), 0, 0, 0)),
                   _full((1, HG_WIDTH))],
                  [pl.BlockSpec((nb, CHUNK, 3 * HG_WIDTH), lambda c: (0, rev(c), 0)), _full((1, HG_WIDTH))],
                  [_sds((nb, seq, 3 * HG_WIDTH), BF16), _sds((1, HG_WIDTH))],
                  scratch=[pltpu.VMEM((nb, HG_HEADS, HG_HEAD, HG_HEAD), F32)])(zh, do, sts, lb)


def _in_proj_bwd(dza, dzh, dzg, dzgt, dx1, x, g_mix, w_in, tm):
    t = x.shape[0]

    def body(dza_ref, dzh_ref, dzg_ref, dzgt_ref, dx1_ref, x_ref, g_ref, w_ref, dz_ref, dx_ref, dg_ref):
        @pl.when(pl.program_id(0) == 0)
        def _():
            dg_ref[...] = jnp.zeros_like(dg_ref)

        c1, c2, c3 = S5_WIDTH, S5_WIDTH + 3 * HG_WIDTH, S5_WIDTH + 4 * HG_WIDTH
        dz_ref[:, 0:c1] = dza_ref[...]
        dz_ref[:, c1:c2] = dzh_ref[...]
        dz_ref[:, c2:c3] = dzg_ref[...]
        dz_ref[:, c3:] = dzgt_ref[...]
        du = _dot(dz_ref[...], w_ref[...])
        xv = x_ref[...]
        r = lax.rsqrt(jnp.mean(xv * xv, axis=-1, keepdims=True) + EPS)
        xn = xv * r
        dg_ref[...] += jnp.sum(du * xn, axis=0, keepdims=True)
        dxn = du * g_ref[...]
        dx_ref[...] = dx1_ref[...] + r * (dxn - xn * jnp.mean(dxn * xn, axis=-1, keepdims=True))

    row = lambda w: pl.BlockSpec((tm, w), lambda i: (i, 0))
    return _pcall(body, "in_proj_bwd", (t // tm,),
                  [row(S5_WIDTH), row(3 * HG_WIDTH), row(HG_WIDTH), row(2 * D_MODEL), row(D_MODEL), row(D_MODEL),
                   _full((1, D_MODEL)), _full((N_IN, D_MODEL))],
                  [row(N_IN), row(D_MODEL), _full((1, D_MODEL))],
                  [_sds((t, N_IN), BF16), _sds((t, D_MODEL)), _sds((1, D_MODEL))],
                  )(dza, dzh, dzg, dzgt, dx1, x, g_mix, w_in)


def _tie(*arrays):
    return jnp.zeros((SUBLANES, 128), F32) + sum(a.reshape(-1)[0].astype(F32) for a in arrays)


def _after(value, token):
    return value + token[0, 0]


def _local_step(x3, tgt3, weights, sp, emit, emit_small):
    nb, seq, _ = x3.shape
    t = nb * seq
    tm = _token_tile(seq)
    x = x3.reshape(t, D_MODEL)
    tgt = tgt3.reshape(t, D_MODEL)
    row = lambda v: v.reshape(1, -1)

    a_re, a_im, b_re, b_im = sp["s5_a_re"], sp["s5_a_im"], sp["s5_b_re"], sp["s5_b_im"]
    ldt = sp["s5_log_dt"].reshape(S5_GROUPS, 1)
    lr, li, bb_re, bb_im, lb = _params_fwd(a_re, a_im, ldt, b_re, b_im, sp["hg_lb_logits"])
    lam = jnp.concatenate([lr.reshape(1, S5_N), li.reshape(1, S5_N)], axis=0)
    swap = lambda m: m.transpose(0, 2, 1)
    b_to_st = (_band_blocks(bb_re), _band_blocks(bb_im))
    b_to_ch = (_band_blocks(swap(bb_re)), _band_blocks(swap(bb_im)))
    c_to_ch = (_band_blocks(swap(sp["s5_c_re"])), _band_blocks(swap(-sp["s5_c_im"])))
    c_to_st = (_band_blocks(sp["s5_c_re"]), _band_blocks(-sp["s5_c_im"]))

    g_mix, g_ffn, g_final = row(sp["g_mix"]), row(sp["g_ffn"]), row(sp["g_final"])
    b_glu, gain, dskip, b_conv = row(sp["b_glu"]), row(sp["hg_norm_gain"]), row(sp["s5_d"]), row(sp["b_conv"])

    w_in = weights("in", lam, *b_to_st, *b_to_ch, *c_to_ch, *c_to_st)["w_in"]
    u, za, zh, zgt = _in_proj(x, g_mix, w_in, tm)
    seqs = lambda v: v.reshape(nb, seq, v.shape[-1])
    toks = lambda v: v.reshape(t, v.shape[-1])
    xs3, y0 = _s5_fwd(seqs(za), b_to_st, lam, c_to_ch, dskip, nb, seq, tm)
    xs, y0 = toks(xs3), toks(y0)
    o3, sts = _hgrn_fwd(zh.reshape(nb, seq, 4 * HG_WIDTH), lb, nb, seq)
    o = o3.reshape(t, HG_WIDTH)
    wm = weights("mix", y0, o3)
    weights.forward("ffn", wm["w_out"])
    x1, u2, pa, pb, ya2, yb = _mix_fwd(x, y0, o, zh, zgt, wm["w_glu"], b_glu, gain, wm["w_pa"], wm["w_pb"],
                                       wm["w_out"], g_ffn, tm)
    wf = weights("ffn", u2)
    h = _ffn_up(u2, wf["w_up"], min(4 * tm, t))
    hc, a, dx2, dx2b, loss, dg_final = _ffn_down_loss(h, x1, tgt, wf["w_conv"], b_conv, wf["w_down"], g_final,
                                                      seq, tm)

    wgrad = functools.partial(_wgrad, tn=256, out_dtype=BF16)
    dhc, db_conv = _ffn_bwd_act(dx2b, hc, wf["w_down"], tm)
    sent = emit({"w_down": wgrad(a, dx2b, "dw_down")})
    dh, dx1, dx1b, dg_ffn, dw_conv = _ffn_bwd_up(dhc, h, dx2, x1, wf["w_conv"], wf["w_up"], _after(g_ffn, sent),
                                                 seq, tm)
    sent = emit({"w_up": wgrad(dh, u2, "dw_up"), "w_conv": dw_conv})
    (dy0, do, dzg, dzgt, m, dpa, dpb, ya1, dpre, db_glu, dgain) = _mix_bwd(
        dx1b, y0, o, zh, zgt, pa, pb, wm["w_glu"], _after(b_glu, sent), gain, wm["w_pa"], wm["w_pb"], wm["w_out"], tm)
    sent = emit({"w_out": wgrad(m, dx1b, "dw_out"), "w_pa": wgrad(ya2, dpa, "dw_pa"),
                 "w_pb": wgrad(yb, dpb, "dw_pb"), "w_glu": wgrad(ya1, dpre, "dw_glu")})
    dzh3, dlb = _hgrn_bwd(zh.reshape(nb, seq, 4 * HG_WIDTH), do.reshape(nb, seq, HG_WIDTH), sts, _after(lb, sent),
                          nb, seq)
    dza, a_s5, dlam, dd = _s5_bwd(seqs(dy0), seqs(za), xs3, c_to_st, b_to_ch, lam, dskip, nb, seq, tm)
    dza, a_s5 = toks(dza), toks(a_s5)
    dz, dx, dg_mix = _in_proj_bwd(dza, dzh3.reshape(t, 3 * HG_WIDTH), dzg, dzgt, dx1, x, g_mix, w_in, tm)
    sent = emit({"w_in": wgrad(dz, u, "dw_in")})

    band = HG_HEAD
    dbb_band = _wgrad(a_s5, za, "dbb_s5", 512, band=band, after=sent)
    dc_band = _wgrad(xs, dy0, "dc_s5", 512, band=band, after=sent)
    dbb_re = swap(_diag_blocks(dbb_band[:S5_N], S5_STATE, S5_GROUP))
    dbb_im = swap(_diag_blocks(dbb_band[S5_N:], S5_STATE, S5_GROUP))
    dc_re = swap(_diag_blocks(dc_band[:S5_N], S5_STATE, S5_GROUP))
    dc_im = -swap(_diag_blocks(dc_band[S5_N:], S5_STATE, S5_GROUP))
    da_re, da_im, dldt, db_re, db_im, dlogits = _params_bwd(
        a_re, a_im, ldt, b_re, b_im, sp["hg_lb_logits"],
        dlam[0].reshape(S5_GROUPS, S5_STATE), dlam[1].reshape(S5_GROUPS, S5_STATE), dbb_re, dbb_im, dlb)
    emit_small({"g_mix": dg_mix, "s5_a_re": da_re, "s5_a_im": da_im, "s5_log_dt": dldt.reshape(1, S5_GROUPS),
                "s5_b_re": db_re, "s5_b_im": db_im, "s5_c_re": dc_re, "s5_c_im": dc_im, "s5_d": dd, "b_glu": db_glu,
                "hg_lb_logits": dlogits, "hg_norm_gain": dgain, "g_ffn": dg_ffn, "b_conv": db_conv,
                "g_final": dg_final, "loss": loss})
    return dx.reshape(nb, seq, D_MODEL)


def _mesh_peers():
    x, y, c = lax.axis_index("x"), lax.axis_index("y"), lax.axis_index("c")
    peers = []
    for k in range(1, N_DEV):
        px, py, pc = (1 - x if k & 4 else x), (1 - y if k & 2 else y), (1 - c if k & 1 else c)
        peers.append((k, (px, py, pc), 4 * px + 2 * py + pc))
    return 4 * x + 2 * y + c, peers


_HBM = pl.BlockSpec(memory_space=pltpu.HBM)
_SEM = pl.BlockSpec(memory_space=pltpu.SEMAPHORE)


def _exchange_start(name, operands, after, place_own=True):
    n = len(operands)
    me = 4 * lax.axis_index("x") + 2 * lax.axis_index("y") + lax.axis_index("c")
    flags = [per_peer for _, per_peer in operands]
    srcs, lands = [], []
    for arr, per_peer in operands:
        land = lax.empty((N_DEV,) + (arr.shape[1:] if per_peer else arr.shape), arr.dtype)
        if place_own:
            own = lax.dynamic_index_in_dim(arr, me, 0, keepdims=True) if per_peer else arr[None]
            land = lax.dynamic_update_slice_in_dim(land, own, me, 0)
        srcs.append(pltpu.with_memory_space_constraint(arr, pltpu.HBM))
        lands.append(pltpu.with_memory_space_constraint(land, pltpu.HBM))
    copies = (N_DEV - 1) * n

    def body(*refs):
        src_refs, land_refs = refs[:n], refs[n:2 * n]
        send_sems, recv_sems = refs[2 * n + 1], refs[2 * n + 2]
        token = refs[-1]
        my_slab, peers = _mesh_peers()
        for k, peer, slab in peers:
            for i in range(n):
                s = (k - 1) * n + i
                pltpu.make_async_remote_copy(
                    src_ref=src_refs[i].at[slab] if flags[i] else src_refs[i], dst_ref=land_refs[i].at[my_slab],
                    send_sem=send_sems.at[s], recv_sem=recv_sems.at[s], device_id=peer,
                    device_id_type=pl.DeviceIdType.MESH).start()
        token[...] = jnp.zeros_like(token)

    outs = pl.pallas_call(
        body, name=name,
        out_shape=(pltpu.SemaphoreType.DMA((copies,)), pltpu.SemaphoreType.DMA((copies,)),
                   *[pltpu.HBM(a.shape, a.dtype) for a in lands], _sds((SUBLANES, 128))),
        in_specs=[_HBM] * (2 * n) + [pl.BlockSpec(memory_space=pl.ANY)],
        out_specs=(_SEM, _SEM, *[_HBM] * n, pl.BlockSpec(memory_space=pltpu.VMEM)),
        input_output_aliases={n + i: 2 + i for i in range(n)},
        compiler_params=_EFFECT,
    )(*srcs, *lands, after)
    state = (flags, outs[0], outs[1], srcs, outs[2:2 + n])
    return state, outs[-1]


def _exchange_wait(name, state, *after):
    flags, send_sems, recv_sems, srcs, lands = state
    n = len(flags)

    def body(*refs):
        src_refs, land_refs = refs[:n], refs[n:2 * n]
        send_ref, recv_ref = refs[2 * n], refs[2 * n + 1]
        _, peers = _mesh_peers()
        for k, peer, slab in peers:
            for i in range(n):
                s = (k - 1) * n + i
                copy = pltpu.make_async_remote_copy(
                    src_ref=src_refs[i].at[slab] if flags[i] else src_refs[i], dst_ref=land_refs[i].at[slab],
                    send_sem=send_ref.at[s], recv_sem=recv_ref.at[s], device_id=peer,
                    device_id_type=pl.DeviceIdType.MESH)
                copy.wait_send()
                copy.wait_recv()

    outs = pl.pallas_call(
        body, name=name,
        out_shape=tuple(pltpu.HBM(a.shape, a.dtype) for a in lands),
        in_specs=[_HBM] * (2 * n) + [_SEM, _SEM] + [pl.BlockSpec(memory_space=pl.ANY)] * len(after),
        out_specs=tuple([_HBM] * n),
        input_output_aliases={n + i: i for i in range(n)},
        compiler_params=_EFFECT,
    )(*srcs, *lands, send_sems, recv_sems, *after)
    return list(outs), list(srcs)


def _slab(pos):
    return 4 * pos[0] + 2 * pos[1] + pos[2]


def _chip_routes():
    x, y, c = lax.axis_index("x"), lax.axis_index("y"), lax.axis_index("c")
    return (x, y, c), (x, y, 1 - c), [(1 - x, y, c), (x, 1 - y, c), (1 - x, 1 - y, c)]


def _remote(src, dst, send_sem, recv_sem, to):
    return pltpu.make_async_remote_copy(src_ref=src, dst_ref=dst, send_sem=send_sem, recv_sem=recv_sem,
                                        device_id=to, device_id_type=pl.DeviceIdType.MESH)


_EFFECT = pltpu.CompilerParams(has_side_effects=pltpu.SideEffectType.DATAFLOW_SIDE_EFFECTING)


def _gather_start(name, arrays, after):
    n = len(arrays)
    me = 4 * lax.axis_index("x") + 2 * lax.axis_index("y") + lax.axis_index("c")
    srcs = [pltpu.with_memory_space_constraint(a, pltpu.HBM) for a in arrays]
    lands = [pltpu.with_memory_space_constraint(
        lax.dynamic_update_slice_in_dim(lax.empty((N_DEV,) + a.shape, a.dtype), a[None], me, 0), pltpu.HBM)
        for a in arrays]

    def body(*refs):
        src_refs, land_refs = refs[:n], refs[n:2 * n]
        send_sems, recv_sems, token = refs[2 * n + 1], refs[2 * n + 2], refs[-1]
        mine, sibling, chips = _chip_routes()
        for k, to in enumerate([sibling] + chips):
            for i in range(n):
                _remote(src_refs[i], land_refs[i].at[_slab(mine)], send_sems.at[k * n + i], recv_sems.at[k * n + i],
                        to).start()
        token[...] = jnp.zeros_like(token)

    outs = pl.pallas_call(
        body, name=name,
        out_shape=(pltpu.SemaphoreType.DMA((4 * n,)), pltpu.SemaphoreType.DMA((4 * n,)),
                   *[pltpu.HBM(a.shape, a.dtype) for a in lands], _sds((SUBLANES, 128))),
        in_specs=[_HBM] * (2 * n) + [pl.BlockSpec(memory_space=pl.ANY)],
        out_specs=(_SEM, _SEM, *[_HBM] * n, pl.BlockSpec(memory_space=pltpu.VMEM)),
        input_output_aliases={n + i: 2 + i for i in range(n)}, compiler_params=_EFFECT,
    )(*srcs, *lands, after)
    return (outs[0], outs[1], srcs, outs[2:2 + n]), outs[-1]


def _gather_forward(name, state, *after):
    send_a, recv_a, srcs, lands = state
    n = len(lands)

    def body(*refs):
        land_refs, recv_a_ref = refs[:n], refs[n]
        send_b, recv_b = refs[n + 1 + len(after)], refs[n + 2 + len(after)]
        mine, sibling, chips = _chip_routes()
        for j, chip in enumerate(chips):
            for i in range(n):
                block = land_refs[i].at[_slab(chip)]
                _remote(block, block, send_b.at[j * n + i], recv_a_ref.at[(1 + j) * n + i], chip).wait_recv()
                _remote(block, block, send_b.at[j * n + i], recv_b.at[j * n + i], sibling).start()

    outs = pl.pallas_call(
        body, name=name,
        out_shape=(pltpu.SemaphoreType.DMA((3 * n,)), pltpu.SemaphoreType.DMA((3 * n,)),
                   *[pltpu.HBM(a.shape, a.dtype) for a in lands]),
        in_specs=[_HBM] * n + [_SEM] + [pl.BlockSpec(memory_space=pl.ANY)] * len(after),
        out_specs=(_SEM, _SEM, *[_HBM] * n),
        input_output_aliases={i: 2 + i for i in range(n)}, compiler_params=_EFFECT,
    )(*lands, recv_a, *after)
    return (send_a, recv_a, srcs, list(outs[2:])), (outs[0], outs[1])


def _gather_wait(name, state, forwarded, *after):
    send_a, recv_a, srcs, lands = state
    send_b, recv_b = forwarded
    n = len(lands)

    def body(*refs):
        src_refs, land_refs = refs[:n], refs[n:2 * n]
        sa, ra, sb, rb = refs[2 * n:2 * n + 4]
        mine, sibling, chips = _chip_routes()
        for i in range(n):
            for k, to in enumerate([sibling] + chips):
                _remote(src_refs[i], land_refs[i].at[_slab(mine)], sa.at[k * n + i], ra.at[k * n + i], to).wait_send()
            theirs = land_refs[i].at[_slab(sibling)]
            _remote(theirs, theirs, sa.at[i], ra.at[i], sibling).wait_recv()
            for j, chip in enumerate(chips):
                sent = land_refs[i].at[_slab(chip)]
                got = land_refs[i].at[_slab((chip[0], chip[1], sibling[2]))]
                _remote(sent, sent, sb.at[j * n + i], rb.at[j * n + i], sibling).wait_send()
                _remote(got, got, sb.at[j * n + i], rb.at[j * n + i], sibling).wait_recv()

    outs = pl.pallas_call(
        body, name=name,
        out_shape=tuple(pltpu.HBM(a.shape, a.dtype) for a in lands),
        in_specs=[_HBM] * (2 * n) + [_SEM] * 4 + [pl.BlockSpec(memory_space=pl.ANY)] * len(after),
        out_specs=tuple([_HBM] * n),
        input_output_aliases={n + i: i for i in range(n)}, compiler_params=_EFFECT,
    )(*srcs, *lands, send_a, recv_a, send_b, recv_b, *after)
    return list(outs), list(srcs)


def _join_cols(parts, name, tr):
    _, r, c = parts.shape

    def body(p_ref, o_ref):
        for j in range(N_DEV):
            o_ref[:, j * c:(j + 1) * c] = p_ref[j]

    return _pcall(body, name, (r // tr,), [pl.BlockSpec((N_DEV, tr, c), lambda i: (0, i, 0))],
                  pl.BlockSpec((tr, N_DEV * c), lambda i: (i, 0)), _sds((r, N_DEV * c), parts.dtype))(parts)


def _split_cols(full, name, tr):
    r, c = full.shape[0], full.shape[1] // N_DEV

    def body(f_ref, o_ref):
        for j in range(N_DEV):
            o_ref[j] = f_ref[:, j * c:(j + 1) * c]

    return _pcall(body, name, (r // tr,), [pl.BlockSpec((tr, N_DEV * c), lambda i: (i, 0))],
                  pl.BlockSpec((N_DEV, tr, c), lambda i: (0, i, 0)), _sds((N_DEV, r, c), full.dtype))(full)


def _my_slab():
    return (4 * lax.axis_index("x") + 2 * lax.axis_index("y") + lax.axis_index("c")).astype(jnp.int32).reshape(1)


def _adamw(parts, sent, w, m, v, name, tile):
    _, rows, cols = w.shape

    def body(me_ref, p_ref, s_ref, w_ref, m_ref, v_ref, g_out, d_out, m_out, v_out):
        me = me_ref[0]
        g = jnp.where(me == 0, s_ref[0], p_ref[0]).astype(F32)
        for k in range(1, N_DEV):
            g = g + jnp.where(me == k, s_ref[0], p_ref[k]).astype(F32)
        m1 = ADAM_B1 * m_ref[0] + (1.0 - ADAM_B1) * g
        v1 = ADAM_B2 * v_ref[0] + (1.0 - ADAM_B2) * (g * g)
        m_hat = m1 / (1.0 - ADAM_B1 ** ADAM_STEP)
        v_hat = v1 / (1.0 - ADAM_B2 ** ADAM_STEP)
        g_out[0] = g
        d_out[0] = -ADAM_LR * (m_hat / (jnp.sqrt(v_hat) + ADAM_EPS) + ADAM_WD * w_ref[0])
        m_out[0] = m1
        v_out[0] = v1

    row = pl.BlockSpec((1, tile, cols), lambda i, me: (0, i, 0))
    return pl.pallas_call(
        body, name=name, out_shape=[_sds((1, rows, cols))] * 4,
        grid_spec=pltpu.PrefetchScalarGridSpec(
            num_scalar_prefetch=1, grid=(rows // tile,),
            in_specs=[pl.BlockSpec((N_DEV, tile, cols), lambda i, me: (0, i, 0)),
                      pl.BlockSpec((1, tile, cols), lambda i, me: (me[0], i, 0)), row, row, row],
            out_specs=[row, row, row, row]),
        compiler_params=pltpu.CompilerParams(dimension_semantics=("arbitrary",), vmem_limit_bytes=VMEM_LIMIT),
    )(_my_slab(), parts, sent, w, m, v)


BIG = {
    "w_in": ((N_IN // N_DEV, D_MODEL), False, N_IN // N_DEV // 3),
    "w_glu": ((S5_WIDTH // N_DEV, S5_WIDTH), False, S5_WIDTH // N_DEV),
    "w_pa": ((S5_WIDTH, D_MODEL // N_DEV), True, S5_WIDTH),
    "w_pb": ((HG_WIDTH, D_MODEL // N_DEV), True, HG_WIDTH),
    "w_out": ((D_MODEL // N_DEV, D_MODEL), False, D_MODEL // N_DEV),
    "w_up": ((2 * D_FF // N_DEV, D_MODEL), False, 2 * D_FF // N_DEV // 4),
    "w_conv": ((CONV_W, 2 * D_FF // N_DEV), True, CONV_W),
    "w_down": ((D_FF // N_DEV, D_MODEL), False, D_FF // N_DEV // 2),
}
TRANSPOSED = ("w_in", "w_up", "s5_b_re", "s5_b_im")
UNALIGNED_COLS = ("w_conv",)


def _stored(n, arr):
    return jnp.swapaxes(arr, -1, -2) if n in TRANSPOSED else arr


def _join_shards(n, parts):
    (a, b), by_cols, _ = BIG[n]
    if not by_cols:
        return parts.reshape(N_DEV * a, b)
    if n in UNALIGNED_COLS:
        return _join_cols(parts, "join_" + n, min(a, 256))
    return parts.transpose(1, 0, 2).reshape(a, N_DEV * b)


def _split_shards(n, full):
    (a, b), by_cols, _ = BIG[n]
    if not by_cols:
        return full.reshape(N_DEV, a, b)
    if n in UNALIGNED_COLS:
        return _split_cols(full, "split_" + n, min(a, 256))
    return full.reshape(a, N_DEV, b).transpose(1, 0, 2)


SMALL_CORE = {
    "s5_b_re": GSC, "s5_b_im": GSC, "s5_c_re": GSC, "s5_c_im": GSC,
    "g_mix": (1, D_MODEL), "g_ffn": (1, D_MODEL), "g_final": (1, D_MODEL), "s5_d": (1, S5_WIDTH),
    "b_glu": (1, S5_WIDTH), "hg_norm_gain": (1, HG_WIDTH), "hg_lb_logits": (2, HG_WIDTH), "b_conv": (1, 2 * D_FF),
    "s5_log_dt": (1, S5_GROUPS), "s5_a_re": (S5_GROUPS, S5_STATE), "s5_a_im": (S5_GROUPS, S5_STATE), "loss": (1, 1),
}
BLOCK_ROWS = 32


def _small_rows():
    rows, r = {}, 0
    for n, core in SMALL_CORE.items():
        rows[n] = r
        r += BLOCK_ROWS if len(core) == 3 else -(-math.prod(core) // PACK_W)
    return rows, -(-r // SUBLANES) * SUBLANES


SMALL_ROW, SMALL_ROWS = _small_rows()


def _small_pieces(name):
    r, core = SMALL_ROW[name], SMALL_CORE[name]
    if len(core) == 3:
        return [((g, slice(None), slice(None)), slice(r + S5_GROUP * (g % 2), r + S5_GROUP * (g % 2 + 1)),
                 slice(S5_STATE * (g // 2), S5_STATE * (g // 2 + 1))) for g in range(S5_GROUPS)]
    pieces = []
    for i in range(core[0]):
        for c0 in range(0, core[1], PACK_W):
            w, flat = min(PACK_W, core[1] - c0), i * core[1] + c0
            pieces.append(((slice(i, i + 1), slice(c0, c0 + w)), slice(r + flat // PACK_W, r + flat // PACK_W + 1),
                           slice(flat % PACK_W, flat % PACK_W + w)))
    return pieces


def _core_index(ref, name, idx):
    return (0,) * (len(ref.shape) - len(SMALL_CORE[name])) + idx


def _pack_small_grads(grads):
    names = list(SMALL_CORE)

    def body(*refs):
        pack = refs[-1]
        pack[...] = jnp.zeros_like(pack)
        for ref, n in zip(refs, names):
            for idx, rows, lanes in _small_pieces(n):
                pack[rows, lanes] = ref[_core_index(ref, n, idx)]

    return _pcall(body, "pack_small_grads", (1,), [_full(grads[n].shape) for n in names],
                  _full((SMALL_ROWS, PACK_W)), _sds((SMALL_ROWS, PACK_W)))(*[grads[n] for n in names])


def _adamw_small(parts, sent, names, rows, given, name):
    lo, hi = rows
    k = len(names)
    shapes = [given[n].shape for n in names]

    def body(*refs):
        me, p_ref, s_ref, ins, outs = refs[0][0], refs[1], refs[2], refs[3:3 + 3 * k], refs[3 + 3 * k:3 + 7 * k]
        packs, results = refs[3 + 7 * k:6 + 7 * k], refs[6 + 7 * k:]
        for j, pack in enumerate(packs):
            pack[...] = jnp.zeros_like(pack)
            for ref, n in zip(ins[j * k:(j + 1) * k], names):
                for idx, prow, lanes in _small_pieces(n):
                    pack[slice(prow.start - lo, prow.stop - lo), lanes] = ref[_core_index(ref, n, idx)]
        mine = s_ref[lo:hi, :]
        g = jnp.where(me == 0, mine, p_ref[0, lo:hi, :])
        for d in range(1, N_DEV):
            g = g + jnp.where(me == d, mine, p_ref[d, lo:hi, :])
        m1 = ADAM_B1 * packs[1][...] + (1.0 - ADAM_B1) * g
        v1 = ADAM_B2 * packs[2][...] + (1.0 - ADAM_B2) * (g * g)
        m_hat = m1 / (1.0 - ADAM_B1 ** ADAM_STEP)
        v_hat = v1 / (1.0 - ADAM_B2 ** ADAM_STEP)
        results[0][...] = g
        results[1][...] = -ADAM_LR * (m_hat / (jnp.sqrt(v_hat) + ADAM_EPS) + ADAM_WD * packs[0][...])
        results[2][...] = m1
        results[3][...] = v1
        for j, result in enumerate(results):
            for ref, n in zip(outs[j * k:(j + 1) * k], names):
                for idx, prow, lanes in _small_pieces(n):
                    ref[_core_index(ref, n, idx)] = result[slice(prow.start - lo, prow.stop - lo), lanes]

    flat = _pcall(body, name, (1,),
                  [pl.BlockSpec(memory_space=pltpu.SMEM), _full(parts.shape), _full(sent.shape)]
                  + [_full(s) for s in shapes] * 3,
                  [_full(s) for s in shapes] * 4, [_sds(s) for s in shapes] * 4,
                  scratch=[pltpu.VMEM((hi - lo, PACK_W), F32)] * 7,
                  )(_my_slab(), parts, sent, *[given[pre + n] for pre in ("", "m_", "v_") for n in names])
    return {n: [flat[j * k + i] for j in range(4)] for i, n in enumerate(names)}


def kernel(x, g_mix, w_in, s5_a_re, s5_a_im, s5_log_dt, s5_b_re, s5_b_im, s5_c_re, s5_c_im, s5_d, w_glu, b_glu, hg_lb_logits, hg_norm_gain, w_pa, w_pb, w_out, g_ffn, w_up, w_conv, b_conv, w_down, g_final, loss_target, m_g_mix, m_w_in, m_s5_a_re, m_s5_a_im, m_s5_log_dt, m_s5_b_re, m_s5_b_im, m_s5_c_re, m_s5_c_im, m_s5_d, m_w_glu, m_b_glu, m_hg_lb_logits, m_hg_norm_gain, m_w_pa, m_w_pb, m_w_out, m_g_ffn, m_w_up, m_w_conv, m_b_conv, m_w_down, m_g_final, v_g_mix, v_w_in, v_s5_a_re, v_s5_a_im, v_s5_log_dt, v_s5_b_re, v_s5_b_im, v_s5_c_re, v_s5_c_im, v_s5_d, v_w_glu, v_b_glu, v_hg_lb_logits, v_hg_norm_gain, v_w_pa, v_w_pb, v_w_out, v_g_ffn, v_w_up, v_w_conv, v_b_conv, v_w_down, v_g_final):
    given = dict(locals())
    small_names = [n for n, _ in SMALL]

    pay = {n: given[n][0] if n == "w_conv" else _stored(n, given[n])[0].astype(BF16) for n in BIG}
    groups = {"in": ["w_in"], "mix": ["w_glu", "w_pa", "w_pb", "w_out"], "ffn": ["w_up", "w_down", "w_conv"]}
    gathers, order = {}, pay["w_in"]
    for grp, names in groups.items():
        gathers[grp], order = _gather_start("gather_" + grp + "_start", [pay[n] for n in names], order)

    forwards = {}

    def forward(grp, *after):
        if grp == "in":
            after = (*after, order)
        forwards[grp] = _gather_forward("gather_" + grp + "_forward", gathers[grp], *after)

    def weights(grp, *after):
        if grp not in forwards:
            forward(grp, *after)
        got, _ = _gather_wait("gather_" + grp + "_wait", *forwards[grp], *after)
        return {n: _join_shards(n, g) for n, g in zip(groups[grp], got)}

    weights.forward = forward

    in_flight, started = [], []

    def emit(grads):
        names = list(grads)
        state, token = _exchange_start("grads_" + names[0] + "_start",
                                       [(_split_shards(n, grads[n]), True) for n in names], grads[names[0]],
                                       place_own=False)
        in_flight.append((names, state))
        return token

    def emit_small(grads):
        pack = _pack_small_grads(grads)
        state, token = _gather_start("grads_small_start", [pack], pack)
        in_flight.append((["small"], state))
        started.append(token)

    sp = {n: (given[n] if n in ("g_final", "hg_lb_logits") else _stored(n, given[n])[0]) for n in small_names}
    sp["g_mix"] = _after(sp["g_mix"], order)
    dx = _local_step(x, loss_target, weights, sp, emit, emit_small)

    res = {}
    after = [started[-1]]
    in_flight.insert(-1, in_flight.pop())
    for names, state in in_flight:
        if names == ["small"]:
            state, forwarded = _gather_forward("grads_small_forward", state, *after)
            parts, sent = _gather_wait("grads_small_wait", state, forwarded)
        else:
            parts, sent = _exchange_wait("grads_" + names[0] + "_wait", state, *after)
        if names != ["small"]:
            after = []
            for n, part, mine in zip(names, parts, sent):
                raw = _adamw(part, mine, *[_stored(n, given[pre + n]) for pre in ("", "m_", "v_")], "adamw_" + n,
                             BIG[n][2])
                res[n] = [_stored(n, r) for r in raw]
                after.append(raw[0])
            continue
        sgiven = {pre + n: _stored(n, given[pre + n]) for pre in ("", "m_", "v_") for n in small_names}
        for pre in ("", "m_", "v_"):
            sgiven[pre + "g_final"] = given[pre + "g_final"].reshape(1, D_MODEL)
            sgiven[pre + "loss"] = jnp.zeros((1, 1), F32)
        raw = _adamw_small(parts[0], sent[0], list(SMALL_CORE), (0, SMALL_ROWS), sgiven, "adamw_small")
        res.update({n: [_stored(n, r) for r in raw[n]] for n in small_names})
        res["g_final"] = [r.reshape(D_MODEL) for r in raw["g_final"]]
        total_loss = raw["loss"][0].reshape(())
        after = [raw["s5_b_re"][0], raw["g_mix"][0]]
    return (total_loss, dx, *[res[n][0] for n in WEIGHT_ORDER], *[res[n][1] for n in WEIGHT_ORDER],
            *[res[n][2] for n in WEIGHT_ORDER], *[res[n][3] for n in WEIGHT_ORDER])
```

```python
import functools
import math

import jax
import jax.numpy as jnp
from jax import lax
from jax.experimental import pallas as pl
from jax.experimental.pallas import tpu as pltpu

F32 = jnp.float32
BF16 = jnp.bfloat16

D_MODEL = 1024
S5_WIDTH = 512
S5_GROUP = 16
S5_GROUPS = 32
S5_STATE = 64
S5_N = S5_GROUPS * S5_STATE
HG_WIDTH = 512
HG_HEAD = 128
HG_HEADS = 4
D_FF = 2816
CONV_W = 3
CHUNK = 64
N_IN = S5_WIDTH + 4 * HG_WIDTH + 2 * D_MODEL
EPS = 1e-6
QSCALE = HG_HEAD ** -0.5

ADAM_LR = 0.001
ADAM_B1 = 0.9
ADAM_B2 = 0.999
ADAM_EPS = 1e-08
ADAM_WD = 0.01
ADAM_STEP = 10

N_DEV = 8
V7X_VMEM_BYTES = 64 * 1024 * 1024
VMEM_LIMIT = V7X_VMEM_BYTES * 7 // 8
SUBLANES = 8
PACK_W = 1024

WEIGHT_ORDER = ("g_mix", "w_in", "s5_a_re", "s5_a_im", "s5_log_dt", "s5_b_re", "s5_b_im", "s5_c_re", "s5_c_im",
                "s5_d", "w_glu", "b_glu", "hg_lb_logits", "hg_norm_gain", "w_pa", "w_pb", "w_out", "g_ffn",
                "w_up", "w_conv", "b_conv", "w_down", "g_final")


def _pcall(body, name, grid, in_specs, out_specs, out_shape, scratch=()):
    return pl.pallas_call(
        body, name=name, grid=grid, in_specs=in_specs, out_specs=out_specs, out_shape=out_shape,
        scratch_shapes=list(scratch),
        compiler_params=pltpu.CompilerParams(dimension_semantics=("arbitrary",) * len(grid),
                                             vmem_limit_bytes=VMEM_LIMIT),
    )


def _full(shape):
    return pl.BlockSpec(shape, lambda *_: (0,) * len(shape))


def _sds(shape, dtype=F32):
    return jax.ShapeDtypeStruct(shape, dtype)


def _dot(a, b):
    return jnp.dot(a.astype(BF16), b.astype(BF16), preferred_element_type=F32)


def _dot_nt(a, b):
    return lax.dot_general(a.astype(BF16), b.astype(BF16), (((1,), (1,)), ((), ())), preferred_element_type=F32)


def _dot_tn(a, b):
    return lax.dot_general(a.astype(BF16), b.astype(BF16), (((0,), (0,)), ((), ())), preferred_element_type=F32)


def _hdot(a, b):
    return jnp.dot(a, b, preferred_element_type=F32, precision=lax.Precision.HIGHEST)


def _hdot_tn(a, b):
    return lax.dot_general(a, b, (((0,), (0,)), ((), ())), preferred_element_type=F32,
                           precision=lax.Precision.HIGHEST)


def _sigmoid(x):
    return jax.nn.sigmoid(x)


GELU_C = math.sqrt(2.0 / math.pi)
GELU_A = 0.044715


def _gelu(x):
    return 0.5 * x * (1.0 + jnp.tanh(GELU_C * (x + GELU_A * (x * x * x))))


def _gelu_grad(x):
    t = jnp.tanh(GELU_C * (x + GELU_A * (x * x * x)))
    return 0.5 * (1.0 + t) + 0.5 * x * (1.0 - t * t) * (GELU_C * (1.0 + 3.0 * GELU_A * x * x))


def _cumsum_rows(v, reverse=False):
    n = v.shape[0]
    row = lax.broadcasted_iota(jnp.int32, v.shape, 0)
    s = 1
    while s < n:
        if reverse:
            v = v + jnp.where(row < n - s, pltpu.roll(v, n - s, axis=0), 0.0)
        else:
            v = v + jnp.where(row >= s, pltpu.roll(v, s, axis=0), 0.0)
        s *= 2
    return v


def _token_tile(seq):
    return min(256, seq)


def _s5_coeffs(a_re, a_im, ldt):
    dt = jnp.exp(ldt)
    mag = jnp.exp(a_re * dt)
    ang = a_im * dt
    lb_re = mag * jnp.cos(ang)
    lb_im = mag * jnp.sin(ang)
    den = a_re * a_re + a_im * a_im
    n_re = lb_re - 1.0
    n_im = lb_im
    co_re = (n_re * a_re + n_im * a_im) / den
    co_im = (n_im * a_re - n_re * a_im) / den
    return lb_re, lb_im, co_re, co_im


GS, GSC = (S5_GROUPS, S5_STATE), (S5_GROUPS, S5_GROUP, S5_STATE)


def _params_fwd(a_re, a_im, ldt, bt_re, bt_im, logits):
    def body(are, aim, ld, bre, bim, lg, lr_o, li_o, bbr_o, bbi_o, lb_o):
        lr, li, co_re, co_im = _s5_coeffs(are[...], aim[...], ld[...])
        lr_o[...] = lr
        li_o[...] = li
        for g in range(S5_GROUPS):
            cr, ci = co_re[g:g + 1, :], co_im[g:g + 1, :]
            bbr_o[g] = cr * bre[g] - ci * bim[g]
            bbi_o[g] = cr * bim[g] + ci * bre[g]
        lb_o[...] = _sigmoid(lg[0:1, :] - lg[1:2, :])

    return _pcall(body, "params_fwd", (1,),
                  [_full(GS), _full(GS), _full((S5_GROUPS, 1)), _full(GSC), _full(GSC), _full((2, HG_WIDTH))],
                  [_full(GS), _full(GS), _full(GSC), _full(GSC), _full((1, HG_WIDTH))],
                  [_sds(GS), _sds(GS), _sds(GSC), _sds(GSC), _sds((1, HG_WIDTH))],
                  )(a_re, a_im, ldt, bt_re, bt_im, logits)


def _params_bwd(a_re, a_im, ldt, bt_re, bt_im, logits, dlr, dli, dbbr, dbbi, dlb):
    def body(are, aim, ld, bre, bim, lg, dlr_r, dli_r, dbbr_r, dbbi_r, dlb_r,
             dare_o, daim_o, dld_o, dbre_o, dbim_o, dlg_o, dcr_ref, dci_ref):
        (_, _, co_re, co_im), vjp = jax.vjp(_s5_coeffs, are[...], aim[...], ld[...])
        for g in range(S5_GROUPS):
            cr, ci = co_re[g:g + 1, :], co_im[g:g + 1, :]
            gr, gi, br, bi = dbbr_r[g], dbbi_r[g], bre[g], bim[g]
            dbre_o[g] = cr * gr + ci * gi
            dbim_o[g] = cr * gi - ci * gr
            dcr_ref[g:g + 1, :] = jnp.sum(gr * br + gi * bi, axis=0, keepdims=True)
            dci_ref[g:g + 1, :] = jnp.sum(gi * br - gr * bi, axis=0, keepdims=True)
        dare, daim, dld = vjp((dlr_r[...], dli_r[...], dcr_ref[...], dci_ref[...]))
        dare_o[...] = dare
        daim_o[...] = daim
        dld_o[...] = dld
        lb = _sigmoid(lg[0:1, :] - lg[1:2, :])
        d0 = dlb_r[...] * lb * (1.0 - lb)
        dlg_o[0:1, :] = d0
        dlg_o[1:2, :] = -d0

    return _pcall(body, "params_bwd", (1,),
                  [_full(GS), _full(GS), _full((S5_GROUPS, 1)), _full(GSC), _full(GSC), _full((2, HG_WIDTH)),
                   _full(GS), _full(GS), _full(GSC), _full(GSC), _full((1, HG_WIDTH))],
                  [_full(GS), _full(GS), _full((S5_GROUPS, 1)), _full(GSC), _full(GSC), _full((2, HG_WIDTH))],
                  [_sds(GS), _sds(GS), _sds((S5_GROUPS, 1)), _sds(GSC), _sds(GSC), _sds((2, HG_WIDTH))],
                  scratch=[pltpu.VMEM(GS, F32), pltpu.VMEM(GS, F32)],
                  )(a_re, a_im, ldt, bt_re, bt_im, logits, dlr, dli, dbbr, dbbi, dlb)


def _band_blocks(m):
    g, r, c = m.shape
    gb = g // S5_BANDS
    m4 = m.astype(BF16).reshape(S5_BANDS, gb, r, c)
    on_diag = jnp.eye(gb, dtype=bool)[None, :, None, :, None]
    return jnp.where(on_diag, m4[:, :, :, None, :], 0).reshape(S5_BANDS, gb * r, gb * c)


def _diag_blocks(band, r, c):
    g, nb = band.shape[0] // r, band.shape[1] // c
    on_diag = (jnp.arange(g) % nb)[:, None, None, None] == jnp.arange(nb)[None, None, :, None]
    return jnp.sum(jnp.where(on_diag, band.reshape(g, r, nb, c), 0.0), axis=2)


def _in_proj(x, g_mix, w_in, tm):
    t = x.shape[0]

    def body(x_ref, g_ref, w_ref, u_ref, za_ref, zh_ref, zg_ref):
        xv = x_ref[...]
        r = lax.rsqrt(jnp.mean(xv * xv, axis=-1, keepdims=True) + EPS)
        u = (xv * r * g_ref[...]).astype(BF16)
        u_ref[...] = u
        za_ref[...] = _dot_nt(u, w_ref[0:S5_WIDTH, :])
        zh_ref[...] = _dot_nt(u, w_ref[S5_WIDTH:S5_WIDTH + 4 * HG_WIDTH, :])
        zg_ref[...] = _dot_nt(u, w_ref[S5_WIDTH + 4 * HG_WIDTH:, :]).astype(BF16)

    row = lambda w: pl.BlockSpec((tm, w), lambda i: (i, 0))
    return _pcall(body, "in_proj", (t // tm,),
                  [row(D_MODEL), _full((1, D_MODEL)), _full((N_IN, D_MODEL))],
                  [row(D_MODEL), row(S5_WIDTH), row(4 * HG_WIDTH), row(2 * D_MODEL)],
                  [_sds((t, D_MODEL), BF16), _sds((t, S5_WIDTH)), _sds((t, 4 * HG_WIDTH)),
                   _sds((t, 2 * D_MODEL), BF16)],
                  )(x, g_mix, w_in)


S5_LANES = 512
S5_BANDS = 4


def _band(q):
    return (slice(q * S5_WIDTH // S5_BANDS, (q + 1) * S5_WIDTH // S5_BANDS),
            slice(q * S5_N // S5_BANDS, (q + 1) * S5_N // S5_BANDS))


def _im(st):
    return slice(S5_N + st.start, S5_N + st.stop)


SCAN_UNROLL = 8


def _complex_scan(buf_ref, lam_ref, st_ref, nb, ts, reverse):
    lanes = [slice(cc * S5_LANES, (cc + 1) * S5_LANES) for cc in range(S5_N // S5_LANES)]
    chains = [(b, re) for b in range(nb) for re in lanes]
    nch = len(chains)
    wr = {re.start: lam_ref[0:1, re] for re in lanes}
    wi = {re.start: -lam_ref[1:2, re] if reverse else lam_ref[1:2, re] for re in lanes}

    def block(ib, carry):
        vr, vi = list(carry[:nch]), list(carry[nch:])
        first = ts - SCAN_UNROLL - ib * SCAN_UNROLL if reverse else ib * SCAN_UNROLL
        first = pl.multiple_of(first, SCAN_UNROLL)
        for k in range(SCAN_UNROLL):
            row = pl.ds(first + (SCAN_UNROLL - 1 - k if reverse else k), 1)
            for c, (b, re) in enumerate(chains):
                nr = wr[re.start] * vr[c] - wi[re.start] * vi[c] + buf_ref[b, row, re]
                ni = wr[re.start] * vi[c] + wi[re.start] * vr[c] + buf_ref[b, row, _im(re)]
                buf_ref[b, row, re] = nr
                buf_ref[b, row, _im(re)] = ni
                vr[c], vi[c] = nr, ni
        return tuple(vr + vi)

    init = tuple(st_ref[b, 0:1, re] for b, re in chains) + tuple(st_ref[b, 1:2, re] for b, re in chains)
    last = lax.fori_loop(0, ts // SCAN_UNROLL, block, init)
    for c, (b, re) in enumerate(chains):
        st_ref[b, 0:1, re] = last[c]
        st_ref[b, 1:2, re] = last[nch + c]


BAND_CH = S5_WIDTH // S5_BANDS
BAND_ST = S5_N // S5_BANDS


def _s5_fwd(za, b_bands, lam, c_bands, dskip, nb, seq, ts):
    nts = seq // ts

    def body(za_ref, br_ref, bi_ref, lam_ref, cr_ref, ci_ref, d_ref, xs_ref, y_ref, buf_ref, st_ref):
        @pl.when(pl.program_id(0) == 0)
        def _():
            st_ref[...] = jnp.zeros_like(st_ref)

        for b in range(nb):
            zav = za_ref[b]
            for q in range(S5_BANDS):
                ch, st = _band(q)
                buf_ref[b, :, st] = _dot(zav[:, ch], br_ref[q])
                buf_ref[b, :, _im(st)] = _dot(zav[:, ch], bi_ref[q])
        _complex_scan(buf_ref, lam_ref, st_ref, nb, ts, reverse=False)
        for b in range(nb):
            zav = za_ref[b]
            xs_ref[b] = buf_ref[b].astype(BF16)
            for q in range(S5_BANDS):
                ch, st = _band(q)
                y_ref[b, :, ch] = (_dot(xs_ref[b, :, st], cr_ref[q]) + _dot(xs_ref[b, :, _im(st)], ci_ref[q])
                                   + d_ref[:, ch] * zav[:, ch])

    tok = lambda w: pl.BlockSpec((nb, ts, w), lambda j: (0, j, 0))
    to_st, to_ch = _full((S5_BANDS, BAND_CH, BAND_ST)), _full((S5_BANDS, BAND_ST, BAND_CH))
    return _pcall(body, "s5_fwd", (nts,),
                  [tok(S5_WIDTH), to_st, to_st, _full((2, S5_N)), to_ch, to_ch, _full((1, S5_WIDTH))],
                  [tok(2 * S5_N), tok(S5_WIDTH)],
                  [_sds((nb, seq, 2 * S5_N), BF16), _sds((nb, seq, S5_WIDTH))],
                  scratch=[pltpu.VMEM((nb, ts, 2 * S5_N), F32), pltpu.VMEM((nb, 2, S5_N), F32)],
                  )(za, *b_bands, lam, *c_bands, dskip)


def _hgrn_gates(zq, zf, lbh):
    sf = _sigmoid(zf)
    f = lbh + (1.0 - lbh) * sf
    sq = _sigmoid(zq)
    qa = zq * sq * QSCALE
    bc = _cumsum_rows(jnp.log(f))
    bm = bc[CHUNK // 2 - 1:CHUNK // 2, :]
    bl = bc[CHUNK - 1:CHUNK, :]
    return sf, f, sq, qa, bc, bm, bl


def _hgrn_fwd(zh, lb, nb, seq):
    nc = seq // CHUNK

    def body(zh_ref, lb_ref, o_ref, sts_ref, st_ref):
        @pl.when(pl.program_id(0) == 0)
        def _():
            st_ref[...] = jnp.zeros_like(st_ref)

        causal = (lax.broadcasted_iota(jnp.int32, (CHUNK, CHUNK), 0)
                  >= lax.broadcasted_iota(jnp.int32, (CHUNK, CHUNK), 1))
        for b in range(nb):
            for h in range(HG_HEADS):
                hs = slice(h * HG_HEAD, (h + 1) * HG_HEAD)
                zq = zh_ref[b, :, h * HG_HEAD:(h + 1) * HG_HEAD]
                zf = zh_ref[b, :, HG_WIDTH + h * HG_HEAD:HG_WIDTH + (h + 1) * HG_HEAD]
                zi = zh_ref[b, :, 2 * HG_WIDTH + h * HG_HEAD:2 * HG_WIDTH + (h + 1) * HG_HEAD]
                _, f, _, qa, bc, bm, bl = _hgrn_gates(zq, zf, lb_ref[:, hs])
                k = 1.0 - f
                qt = qa * jnp.exp(bc - bm)
                kt = k * jnp.exp(bm - bc)
                qb = qa * jnp.exp(bc)
                kd = k * jnp.exp(bl - bc)
                st = st_ref[b, h]
                sts_ref[b, 0, h] = st
                a = jnp.where(causal, _dot_nt(qt, kt), 0.0)
                o_ref[b, :, hs] = _dot(a, zi) + _dot_nt(qb, st)
                st_ref[b, h] = st * jnp.exp(bl) + _dot_tn(zi, kd)

    return _pcall(body, "hgrn_fwd", (nc,),
                  [pl.BlockSpec((nb, CHUNK, 4 * HG_WIDTH), lambda c: (0, c, 0)), _full((1, HG_WIDTH))],
                  [pl.BlockSpec((nb, CHUNK, HG_WIDTH), lambda c: (0, c, 0)),
                   pl.BlockSpec((nb, 1, HG_HEADS, HG_HEAD, HG_HEAD), lambda c: (0, c, 0, 0, 0))],
                  [_sds((nb, seq, HG_WIDTH)), _sds((nb, nc, HG_HEADS, HG_HEAD, HG_HEAD))],
                  scratch=[pltpu.VMEM((nb, HG_HEADS, HG_HEAD, HG_HEAD), F32)])(zh, lb)


def _head_rms(o):
    parts = []
    for h in range(HG_HEADS):
        oh = o[:, h * HG_HEAD:(h + 1) * HG_HEAD]
        r = lax.rsqrt(jnp.mean(oh * oh, axis=-1, keepdims=True) + EPS)
        parts.append(jnp.broadcast_to(r, oh.shape))
    return jnp.concatenate(parts, axis=1)


def _head_mean(v):
    parts = []
    for h in range(HG_HEADS):
        vh = v[:, h * HG_HEAD:(h + 1) * HG_HEAD]
        parts.append(jnp.broadcast_to(jnp.mean(vh, axis=-1, keepdims=True), vh.shape))
    return jnp.concatenate(parts, axis=1)


def _mix_fwd(x, y0, o, zh, zgt, w_glu, b_glu, gain, w_pa, w_pb, w_out, g_ffn, tm):
    t = x.shape[0]

    def body(x_ref, y0_ref, o_ref, zg_ref, zgt_ref, wglu_ref, bglu_ref, gain_ref, wpa_ref, wpb_ref, wout_ref,
             gffn_ref, x1_ref, u2_ref, pa_ref, pb_ref, ya2_ref, yb_ref):
        ya1 = _gelu(y0_ref[...])
        s = _sigmoid(_dot(ya1, wglu_ref[...]) + bglu_ref[...])
        ya2 = (ya1 * s).astype(BF16)
        ov = o_ref[...]
        zg = zg_ref[...]
        yb = (ov * _head_rms(ov) * gain_ref[...] * (zg * _sigmoid(zg))).astype(BF16)
        ya2_ref[...] = ya2
        yb_ref[...] = yb
        pa = jnp.dot(ya2, wpa_ref[...], preferred_element_type=F32)
        pb = jnp.dot(yb, wpb_ref[...], preferred_element_type=F32)
        pa_ref[...] = pa.astype(BF16)
        pb_ref[...] = pb.astype(BF16)
        m = (_sigmoid(zgt_ref[:, 0:D_MODEL].astype(F32)) * pa
             + _sigmoid(zgt_ref[:, D_MODEL:].astype(F32)) * pb)
        x1 = x_ref[...] + _dot(m, wout_ref[...])
        x1_ref[...] = x1
        r = lax.rsqrt(jnp.mean(x1 * x1, axis=-1, keepdims=True) + EPS)
        u2_ref[...] = (x1 * r * gffn_ref[...]).astype(BF16)

    row = lambda w: pl.BlockSpec((tm, w), lambda i: (i, 0))
    return _pcall(body, "mix_fwd", (t // tm,),
                  [row(D_MODEL), row(S5_WIDTH), row(HG_WIDTH), pl.BlockSpec((tm, HG_WIDTH), lambda i: (i, 3)),
                   row(2 * D_MODEL), _full((S5_WIDTH, S5_WIDTH)), _full((1, S5_WIDTH)), _full((1, HG_WIDTH)),
                   _full((S5_WIDTH, D_MODEL)), _full((HG_WIDTH, D_MODEL)), _full((D_MODEL, D_MODEL)),
                   _full((1, D_MODEL))],
                  [row(D_MODEL), row(D_MODEL), row(D_MODEL), row(D_MODEL), row(S5_WIDTH), row(HG_WIDTH)],
                  [_sds((t, D_MODEL)), _sds((t, D_MODEL), BF16), _sds((t, D_MODEL), BF16), _sds((t, D_MODEL), BF16),
                   _sds((t, S5_WIDTH), BF16), _sds((t, HG_WIDTH), BF16)],
                  )(x, y0, o, zh, zgt, w_glu, b_glu, gain, w_pa, w_pb, w_out, g_ffn)


FF_COLS = 256
FF_UP_TILE = 1408


def _ffn_up(u2, w_up, tm):
    t = u2.shape[0]
    n = 2 * D_FF

    def body(u_ref, w_ref, h_ref):
        h_ref[...] = _dot_nt(u_ref[...], w_ref[...]).astype(BF16)

    return _pcall(body, "ffn_up", (n // FF_UP_TILE, t // tm),
                  [pl.BlockSpec((tm, D_MODEL), lambda j, i: (i, 0)),
                   pl.BlockSpec((FF_UP_TILE, D_MODEL), lambda j, i: (j, 0))],
                  pl.BlockSpec((tm, FF_UP_TILE), lambda j, i: (i, j)),
                  _sds((t, n), BF16))(u2, w_up)


HALO = 16


def _shift_matrix(tm):
    r = lax.broadcasted_iota(jnp.int32, (tm, tm), 0)
    c = lax.broadcasted_iota(jnp.int32, (tm, tm), 1)
    return jnp.where(r == c + 1, 1.0, 0.0).astype(BF16)


def _conv_cols(h_ref, halo_ref, valid, wc_ref, bc_ref, c0):
    cs = slice(c0, c0 + FF_COLS)
    cur = h_ref[:, cs].astype(F32)
    prev = jnp.where(valid, halo_ref[:, cs].astype(F32), 0.0)
    full = jnp.concatenate([prev, cur], axis=0)
    h1 = pltpu.roll(full, 1, axis=0)[HALO:]
    h2 = pltpu.roll(full, 2, axis=0)[HALO:]
    return h2 * wc_ref[0:1, cs] + h1 * wc_ref[1:2, cs] + cur * wc_ref[2:3, cs] + bc_ref[:, cs]


def _ffn_down_loss(h, x1, tgt, w_conv, b_conv, w_down, g_final, seq, tm):
    t = h.shape[0]
    tps = seq // tm
    n = 2 * D_FF

    def body(h_ref, halo_ref, x1_ref, tgt_ref, wc_ref, bc_ref, wd_ref, gf_ref,
             hc_ref, a_ref, dx2_ref, dx2b_ref, loss_ref, dgf_ref):
        i = pl.program_id(0)

        @pl.when(i == 0)
        def _():
            loss_ref[...] = jnp.zeros_like(loss_ref)
            dgf_ref[...] = jnp.zeros_like(dgf_ref)

        valid = (i % tps) != 0
        x2 = x1_ref[...]
        for j in range(D_FF // FF_COLS):
            gate = _conv_cols(h_ref, halo_ref, valid, wc_ref, bc_ref, j * FF_COLS)
            val = _conv_cols(h_ref, halo_ref, valid, wc_ref, bc_ref, D_FF + j * FF_COLS)
            hc_ref[:, j * FF_COLS:(j + 1) * FF_COLS] = gate.astype(BF16)
            hc_ref[:, D_FF + j * FF_COLS:D_FF + (j + 1) * FF_COLS] = val.astype(BF16)
            a = (gate * _sigmoid(gate) * val).astype(BF16)
            a_ref[:, j * FF_COLS:(j + 1) * FF_COLS] = a
            x2 = x2 + jnp.dot(a, wd_ref[j * FF_COLS:(j + 1) * FF_COLS, :], preferred_element_type=F32)
        r = lax.rsqrt(jnp.mean(x2 * x2, axis=-1, keepdims=True) + EPS)
        xn = x2 * r
        g = gf_ref[...]
        e = xn * g - tgt_ref[...]
        loss_ref[...] += (0.5 / D_MODEL) * jnp.sum(e * e).reshape(1, 1)
        dy = e * (1.0 / D_MODEL)
        dgf_ref[...] += jnp.sum(dy * xn, axis=0, keepdims=True)
        dxn = dy * g
        dx2 = r * (dxn - xn * jnp.mean(dxn * xn, axis=-1, keepdims=True))
        dx2_ref[...] = dx2
        dx2b_ref[...] = dx2.astype(BF16)

    row = lambda w: pl.BlockSpec((tm, w), lambda i: (i, 0))
    halo = pl.BlockSpec((HALO, n), lambda i: (jnp.maximum(i * (tm // HALO) - 1, 0), 0))
    return _pcall(body, "ffn_down_loss", (t // tm,),
                  [row(n), halo, row(D_MODEL), row(D_MODEL), _full((CONV_W, n)), _full((1, n)),
                   _full((D_FF, D_MODEL)), _full((1, D_MODEL))],
                  [row(n), row(D_FF), row(D_MODEL), row(D_MODEL), _full((1, 1)), _full((1, D_MODEL))],
                  [_sds((t, n), BF16), _sds((t, D_FF), BF16), _sds((t, D_MODEL)), _sds((t, D_MODEL), BF16),
                   _sds((1, 1)), _sds((1, D_MODEL))],
                  )(h, h, x1, tgt, w_conv, b_conv, w_down, g_final)


def _wgrad(a, b, name, tn, out_dtype=F32, band=None, after=None):
    t, m = a.shape
    n = b.shape[1] if band is None else band
    nbands = 1 if band is None else b.shape[1] // band
    after = b if after is None else after

    def body(a_ref, b_ref, after_ref, o_ref):
        o_ref[...] = _dot_tn(a_ref[...], b_ref[...]).astype(out_dtype)

    return _pcall(body, name, (m // tn,),
                  [pl.BlockSpec((t, tn), lambda i: (0, i)), pl.BlockSpec((t, n), lambda i: (0, i % nbands)),
                   pl.BlockSpec(memory_space=pl.ANY)],
                  pl.BlockSpec((tn, n), lambda i: (i, 0)), _sds((m, n), out_dtype))(a, b, after)


def _ffn_bwd_act(dx2b, hc, w_down, tm):
    t = hc.shape[0]
    n = 2 * D_FF

    def body(dx2_ref, hc_ref, wd_ref, dhc_ref, dbc_ref):
        @pl.when(pl.program_id(0) == 0)
        def _():
            dbc_ref[...] = jnp.zeros_like(dbc_ref)

        dx2 = dx2_ref[...]
        for j in range(D_FF // FF_COLS):
            gs = slice(j * FF_COLS, (j + 1) * FF_COLS)
            vs = slice(D_FF + j * FF_COLS, D_FF + (j + 1) * FF_COLS)
            gate = hc_ref[:, gs].astype(F32)
            val = hc_ref[:, vs].astype(F32)
            da = _dot_nt(dx2, wd_ref[gs, :])
            sg = _sigmoid(gate)
            dgate = da * val * (sg * (1.0 + gate * (1.0 - sg)))
            dval = da * (gate * sg)
            dhc_ref[:, gs] = dgate.astype(BF16)
            dhc_ref[:, vs] = dval.astype(BF16)
            dbc_ref[:, gs] += jnp.sum(dgate, axis=0, keepdims=True)
            dbc_ref[:, vs] += jnp.sum(dval, axis=0, keepdims=True)

    row = lambda w: pl.BlockSpec((tm, w), lambda i: (i, 0))
    return _pcall(body, "ffn_bwd_act", (t // tm,),
                  [row(D_MODEL), row(n), _full((D_FF, D_MODEL))],
                  [row(n), _full((1, n))],
                  [_sds((t, n), BF16), _sds((1, n))],
                  )(dx2b, hc, w_down)


def _ffn_bwd_up(dhc, h, dx2, x1, w_conv, w_up, g_ffn, seq, tm):
    t = dhc.shape[0]
    tps = seq // tm
    n = 2 * D_FF
    last = t // HALO - 1

    def body(dhc_ref, halo_ref, h_ref, dx2_ref, x1_ref, wc_ref, wu_ref, gf_ref,
             dh_ref, dx1_ref, dx1b_ref, dgf_ref, dwc_ref):
        i = pl.program_id(0)

        @pl.when(i == 0)
        def _():
            dgf_ref[...] = jnp.zeros_like(dgf_ref)
            dwc_ref[...] = jnp.zeros_like(dwc_ref)

        valid = ((i + 1) % tps) != 0
        du2 = jnp.zeros((tm, D_MODEL), F32)
        for j in range(n // FF_COLS):
            cs = slice(j * FF_COLS, (j + 1) * FF_COLS)
            cur = dhc_ref[:, cs].astype(F32)
            nxt = jnp.where(valid, halo_ref[:, cs].astype(F32), 0.0)
            full = jnp.concatenate([cur, nxt], axis=0)
            d1 = pltpu.roll(full, tm + HALO - 1, axis=0)[:tm]
            d2 = pltpu.roll(full, tm + HALO - 2, axis=0)[:tm]
            dh = (cur * wc_ref[2:3, cs] + d1 * wc_ref[1:2, cs] + d2 * wc_ref[0:1, cs]).astype(BF16)
            dh_ref[:, cs] = dh
            du2 = du2 + _dot(dh, wu_ref[cs, :])
            hv = h_ref[:, cs].astype(F32)
            dwc_ref[0:1, cs] += jnp.sum(hv * d2, axis=0, keepdims=True)
            dwc_ref[1:2, cs] += jnp.sum(hv * d1, axis=0, keepdims=True)
            dwc_ref[2:3, cs] += jnp.sum(hv * cur, axis=0, keepdims=True)
        x1 = x1_ref[...]
        r = lax.rsqrt(jnp.mean(x1 * x1, axis=-1, keepdims=True) + EPS)
        xn = x1 * r
        dgf_ref[...] += jnp.sum(du2 * xn, axis=0, keepdims=True)
        dxn = du2 * gf_ref[...]
        dx1 = dx2_ref[...] + r * (dxn - xn * jnp.mean(dxn * xn, axis=-1, keepdims=True))
        dx1_ref[...] = dx1
        dx1b_ref[...] = dx1.astype(BF16)

    row = lambda w: pl.BlockSpec((tm, w), lambda i: (i, 0))
    halo = pl.BlockSpec((HALO, n), lambda i: (jnp.minimum((i + 1) * (tm // HALO), last), 0))
    return _pcall(body, "ffn_bwd_up", (t // tm,),
                  [row(n), halo, row(n), row(D_MODEL), row(D_MODEL), _full((CONV_W, n)), _full((n, D_MODEL)),
                   _full((1, D_MODEL))],
                  [row(n), row(D_MODEL), row(D_MODEL), _full((1, D_MODEL)), _full((CONV_W, n))],
                  [_sds((t, n), BF16), _sds((t, D_MODEL)), _sds((t, D_MODEL), BF16), _sds((1, D_MODEL)),
                   _sds((CONV_W, n))],
                  )(dhc, dhc, h, dx2, x1, w_conv, w_up, g_ffn)


def _mix_bwd(dx1, y0, o, zh, zgt, pa, pb, w_glu, b_glu, gain, w_pa, w_pb, w_out, tm):
    t = dx1.shape[0]

    def body(dx1_ref, y0_ref, o_ref, zg_ref, zgt_ref, pa_ref, pb_ref, wglu_ref, bglu_ref, gain_ref, wpa_ref,
             wpb_ref, wout_ref,
             dy0_ref, do_ref, dzg_ref, dzgt_ref, m_ref, dpa_ref, dpb_ref, ya1_ref, dpre_ref, dbglu_ref, dgain_ref):
        @pl.when(pl.program_id(0) == 0)
        def _():
            dbglu_ref[...] = jnp.zeros_like(dbglu_ref)
            dgain_ref[...] = jnp.zeros_like(dgain_ref)

        dm = _dot_nt(dx1_ref[...], wout_ref[...])
        sga = _sigmoid(zgt_ref[:, 0:D_MODEL].astype(F32))
        sgb = _sigmoid(zgt_ref[:, D_MODEL:].astype(F32))
        pa = pa_ref[...].astype(F32)
        pb = pb_ref[...].astype(F32)
        m_ref[...] = (sga * pa + sgb * pb).astype(BF16)
        dzgt_ref[:, 0:D_MODEL] = (dm * pa * sga * (1.0 - sga)).astype(BF16)
        dzgt_ref[:, D_MODEL:] = (dm * pb * sgb * (1.0 - sgb)).astype(BF16)
        dpa = (dm * sga).astype(BF16)
        dpb = (dm * sgb).astype(BF16)
        dpa_ref[...] = dpa
        dpb_ref[...] = dpb
        dya2 = _dot_nt(dpa, wpa_ref[...])
        dyb = _dot_nt(dpb, wpb_ref[...])
        y0 = y0_ref[...]
        ya1 = _gelu(y0)
        ya1_ref[...] = ya1.astype(BF16)
        s = _sigmoid(_dot(ya1, wglu_ref[...]) + bglu_ref[...])
        dpre = dya2 * ya1 * s * (1.0 - s)
        dpre_ref[...] = dpre.astype(BF16)
        dbglu_ref[...] += jnp.sum(dpre, axis=0, keepdims=True)
        dya1 = dya2 * s + _dot_nt(dpre, wglu_ref[...])
        dy0_ref[...] = dya1 * _gelu_grad(y0)
        ov = o_ref[...]
        zg = zg_ref[...]
        oh = ov * _head_rms(ov)
        on = oh * gain_ref[...]
        sz = _sigmoid(zg)
        dzg_ref[...] = (dyb * on * (sz * (1.0 + zg * (1.0 - sz)))).astype(BF16)
        don = dyb * (zg * sz)
        dgain_ref[...] += jnp.sum(don * oh, axis=0, keepdims=True)
        doh = don * gain_ref[...]
        do_ref[...] = _head_rms(ov) * (doh - oh * _head_mean(doh * oh))

    row = lambda w: pl.BlockSpec((tm, w), lambda i: (i, 0))
    return _pcall(body, "mix_bwd", (t // tm,),
                  [row(D_MODEL), row(S5_WIDTH), row(HG_WIDTH), pl.BlockSpec((tm, HG_WIDTH), lambda i: (i, 3)),
                   row(2 * D_MODEL), row(D_MODEL), row(D_MODEL), _full((S5_WIDTH, S5_WIDTH)), _full((1, S5_WIDTH)),
                   _full((1, HG_WIDTH)), _full((S5_WIDTH, D_MODEL)), _full((HG_WIDTH, D_MODEL)),
                   _full((D_MODEL, D_MODEL))],
                  [row(S5_WIDTH), row(HG_WIDTH), row(HG_WIDTH), row(2 * D_MODEL), row(D_MODEL), row(D_MODEL),
                   row(D_MODEL), row(S5_WIDTH), row(S5_WIDTH), _full((1, S5_WIDTH)), _full((1, HG_WIDTH))],
                  [_sds((t, S5_WIDTH)), _sds((t, HG_WIDTH)), _sds((t, HG_WIDTH), BF16), _sds((t, 2 * D_MODEL), BF16),
                   _sds((t, D_MODEL), BF16), _sds((t, D_MODEL), BF16), _sds((t, D_MODEL), BF16),
                   _sds((t, S5_WIDTH), BF16), _sds((t, S5_WIDTH), BF16), _sds((1, S5_WIDTH)), _sds((1, HG_WIDTH))],
                  )(dx1, y0, o, zh, zgt, pa, pb, w_glu, b_glu, gain, w_pa, w_pb, w_out)


def _s5_bwd(dy0, za, xs, c_bands, b_bands, lam, dskip, nb, seq, ts):
    nts = seq // ts

    def body(dy0_ref, za_ref, xs_ref, halo_ref, cr_ref, ci_ref, br_ref, bi_ref, lam_ref, d_ref,
             dza_ref, a_ref, dlam_ref, dd_ref, acc_ref, st_ref):
        j = pl.program_id(0)

        @pl.when(j == 0)
        def _():
            dlam_ref[...] = jnp.zeros_like(dlam_ref)
            dd_ref[...] = jnp.zeros_like(dd_ref)
            st_ref[...] = jnp.zeros_like(st_ref)

        for b in range(nb):
            dy0 = dy0_ref[b]
            for q in range(S5_BANDS):
                ch, st = _band(q)
                acc_ref[b, :, st] = _dot(dy0[:, ch], cr_ref[q])
                acc_ref[b, :, _im(st)] = _dot(dy0[:, ch], ci_ref[q])
        _complex_scan(acc_ref, lam_ref, st_ref, nb, ts, reverse=True)
        shift = _shift_matrix(ts)
        top = lax.broadcasted_iota(jnp.int32, (SUBLANES, S5_LANES), 0) == 0
        for b in range(nb):
            a_ref[b] = acc_ref[b].astype(BF16)
            first = jnp.where(j == nts - 1, 0.0, halo_ref[b, HALO - 1:HALO, :].astype(F32))

            def shifted(cols):
                xp = jnp.dot(shift, xs_ref[b, :, cols], preferred_element_type=F32)
                return jnp.concatenate([xp[:SUBLANES] + jnp.where(top, first[:, cols], 0.0), xp[SUBLANES:]], axis=0)

            for cc in range(S5_N // S5_LANES):
                re = slice(cc * S5_LANES, (cc + 1) * S5_LANES)
                ar, ai, xr, xi = acc_ref[b, :, re], acc_ref[b, :, _im(re)], shifted(re), shifted(_im(re))
                dlam_ref[0:1, re] += jnp.sum(ar * xr + ai * xi, axis=0, keepdims=True)
                dlam_ref[1:2, re] += jnp.sum(ai * xr - ar * xi, axis=0, keepdims=True)
            dy0 = dy0_ref[b]
            for q in range(S5_BANDS):
                ch, st = _band(q)
                dza_ref[b, :, ch] = (_dot(a_ref[b, :, st], br_ref[q]) + _dot(a_ref[b, :, _im(st)], bi_ref[q])
                                     + d_ref[:, ch] * dy0[:, ch]).astype(BF16)
            dd_ref[...] += jnp.sum(dy0 * za_ref[b], axis=0, keepdims=True)

    tile = lambda j: nts - 1 - j
    tok = lambda w: pl.BlockSpec((nb, ts, w), lambda j: (0, tile(j), 0))
    halo = pl.BlockSpec((nb, HALO, 2 * S5_N), lambda j: (0, jnp.maximum(tile(j) * (ts // HALO) - 1, 0), 0))
    to_st, to_ch = _full((S5_BANDS, BAND_CH, BAND_ST)), _full((S5_BANDS, BAND_ST, BAND_CH))
    return _pcall(body, "s5_bwd", (nts,),
                  [tok(S5_WIDTH), tok(S5_WIDTH), tok(2 * S5_N), halo, to_st, to_st, to_ch, to_ch,
                   _full((2, S5_N)), _full((1, S5_WIDTH))],
                  [tok(S5_WIDTH), tok(2 * S5_N), _full((2, S5_N)), _full((1, S5_WIDTH))],
                  [_sds((nb, seq, S5_WIDTH), BF16), _sds((nb, seq, 2 * S5_N), BF16), _sds((2, S5_N)),
                   _sds((1, S5_WIDTH))],
                  scratch=[pltpu.VMEM((nb, ts, 2 * S5_N), F32), pltpu.VMEM((nb, 2, S5_N), F32)],
                  )(dy0, za, xs, xs, *c_bands, *b_bands, lam, dskip)


def _hgrn_bwd(zh, do, sts, lb, nb, seq):
    nc = seq // CHUNK

    def body(zh_ref, do_ref, sts_ref, lb_ref, dz_ref, dlb_ref, dst_ref):
        @pl.when(pl.program_id(0) == 0)
        def _():
            dst_ref[...] = jnp.zeros_like(dst_ref)
            dlb_ref[...] = jnp.zeros_like(dlb_ref)

        row = lax.broadcasted_iota(jnp.int32, (CHUNK, CHUNK), 0)
        causal = row >= lax.broadcasted_iota(jnp.int32, (CHUNK, CHUNK), 1)
        last_row = lax.broadcasted_iota(jnp.int32, (CHUNK, HG_HEAD), 0) == CHUNK - 1
        for b in range(nb):
            for h in range(HG_HEADS):
                hs = slice(h * HG_HEAD, (h + 1) * HG_HEAD)
                zq = zh_ref[b, :, h * HG_HEAD:(h + 1) * HG_HEAD]
                zf = zh_ref[b, :, HG_WIDTH + h * HG_HEAD:HG_WIDTH + (h + 1) * HG_HEAD]
                zi = zh_ref[b, :, 2 * HG_WIDTH + h * HG_HEAD:2 * HG_WIDTH + (h + 1) * HG_HEAD]
                lbh = lb_ref[:, hs]
                sf, f, sq, qa, bc, bm, bl = _hgrn_gates(zq, zf, lbh)
                k = 1.0 - f
                e_qt = jnp.exp(bc - bm)
                e_kt = jnp.exp(bm - bc)
                e_b = jnp.exp(bc)
                e_kd = jnp.exp(bl - bc)
                e_l = jnp.exp(bl)
                qt, kt, qb, kd = qa * e_qt, k * e_kt, qa * e_b, k * e_kd
                a = jnp.where(causal, _dot_nt(qt, kt), 0.0)
                st = sts_ref[b, 0, h]
                dst = dst_ref[b, h]
                dov = do_ref[b, :, hs]
                da = jnp.where(causal, _dot_nt(dov, zi), 0.0)
                dqt = _hdot(da, kt)
                dkt = _hdot_tn(da, qt)
                dqb = _dot(dov, st)
                di = _dot_tn(a, dov) + _dot_nt(kd, dst)
                dkd = _dot(zi, dst)
                de_l = jnp.sum(dst * st, axis=0, keepdims=True)
                dst_ref[b, h] = dst * e_l + _dot_tn(dov, qb)
                dqa = dqt * e_qt + dqb * e_b
                dk = dkt * e_kt + dkd * e_kd
                dbl = jnp.sum(dkd * kd, axis=0, keepdims=True) + de_l * e_l
                db = dqt * qt - dkt * kt + dqb * qb - dkd * kd + jnp.where(last_row, dbl, 0.0)
                df = _cumsum_rows(db, reverse=True) / f - dk
                dzq = dqa * QSCALE * (sq * (1.0 + zq * (1.0 - sq)))
                dzf = df * (1.0 - lbh) * sf * (1.0 - sf)
                dz_ref[b, :, h * HG_HEAD:(h + 1) * HG_HEAD] = dzq.astype(BF16)
                dz_ref[b, :, HG_WIDTH + h * HG_HEAD:HG_WIDTH + (h + 1) * HG_HEAD] = dzf.astype(BF16)
                dz_ref[b, :, 2 * HG_WIDTH + h * HG_HEAD:2 * HG_WIDTH + (h + 1) * HG_HEAD] = di.astype(BF16)
                dlb_ref[:, hs] += jnp.sum(df * (1.0 - sf), axis=0, keepdims=True)

    rev = lambda c: nc - 1 - c
    return _pcall(body, "hgrn_bwd", (nc,),
                  [pl.BlockSpec((nb, CHUNK, 4 * HG_WIDTH), lambda c: (0, rev(c), 0)),
                   pl.BlockSpec((nb, CHUNK, HG_WIDTH), lambda c: (0, rev(c), 0)),
                   pl.BlockSpec((nb, 1, HG_HEADS, HG_HEAD, HG_HEAD), lambda c: (0, rev(c), 0, 0, 0)),
                   _full((1, HG_WIDTH))],
                  [pl.BlockSpec((nb, CHUNK, 3 * HG_WIDTH), lambda c: (0, rev(c), 0)), _full((1, HG_WIDTH))],
                  [_sds((nb, seq, 3 * HG_WIDTH), BF16), _sds((1, HG_WIDTH))],
                  scratch=[pltpu.VMEM((nb, HG_HEADS, HG_HEAD, HG_HEAD), F32)])(zh, do, sts, lb)


def _in_proj_bwd(dza, dzh, dzg, dzgt, dx1, x, g_mix, w_in, tm):
    t = x.shape[0]

    def body(dza_ref, dzh_ref, dzg_ref, dzgt_ref, dx1_ref, x_ref, g_ref, w_ref, dz_ref, dx_ref, dg_ref):
        @pl.when(pl.program_id(0) == 0)
        def _():
            dg_ref[...] = jnp.zeros_like(dg_ref)

        c1, c2, c3 = S5_WIDTH, S5_WIDTH + 3 * HG_WIDTH, S5_WIDTH + 4 * HG_WIDTH
        dz_ref[:, 0:c1] = dza_ref[...]
        dz_ref[:, c1:c2] = dzh_ref[...]
        dz_ref[:, c2:c3] = dzg_ref[...]
        dz_ref[:, c3:] = dzgt_ref[...]
        du = _dot(dz_ref[...], w_ref[...])
        xv = x_ref[...]
        r = lax.rsqrt(jnp.mean(xv * xv, axis=-1, keepdims=True) + EPS)
        xn = xv * r
        dg_ref[...] += jnp.sum(du * xn, axis=0, keepdims=True)
        dxn = du * g_ref[...]
        dx_ref[...] = dx1_ref[...] + r * (dxn - xn * jnp.mean(dxn * xn, axis=-1, keepdims=True))

    row = lambda w: pl.BlockSpec((tm, w), lambda i: (i, 0))
    return _pcall(body, "in_proj_bwd", (t // tm,),
                  [row(S5_WIDTH), row(3 * HG_WIDTH), row(HG_WIDTH), row(2 * D_MODEL), row(D_MODEL), row(D_MODEL),
                   _full((1, D_MODEL)), _full((N_IN, D_MODEL))],
                  [row(N_IN), row(D_MODEL), _full((1, D_MODEL))],
                  [_sds((t, N_IN), BF16), _sds((t, D_MODEL)), _sds((1, D_MODEL))],
                  )(dza, dzh, dzg, dzgt, dx1, x, g_mix, w_in)


def _after(value, token):
    return value + token[0, 0]


def _local_step(x3, tgt3, weights, sp, emit, emit_small):
    nb, seq, _ = x3.shape
    t = nb * seq
    tm = _token_tile(seq)
    x = x3.reshape(t, D_MODEL)
    tgt = tgt3.reshape(t, D_MODEL)
    row = lambda v: v.reshape(1, -1)

    a_re, a_im, b_re, b_im = sp["s5_a_re"], sp["s5_a_im"], sp["s5_b_re"], sp["s5_b_im"]
    ldt = sp["s5_log_dt"].reshape(S5_GROUPS, 1)
    lr, li, bb_re, bb_im, lb = _params_fwd(a_re, a_im, ldt, b_re, b_im, sp["hg_lb_logits"])
    lam = jnp.concatenate([lr.reshape(1, S5_N), li.reshape(1, S5_N)], axis=0)
    swap = lambda m: m.transpose(0, 2, 1)
    b_to_st = (_band_blocks(bb_re), _band_blocks(bb_im))
    b_to_ch = (_band_blocks(swap(bb_re)), _band_blocks(swap(bb_im)))
    c_to_ch = (_band_blocks(swap(sp["s5_c_re"])), _band_blocks(swap(-sp["s5_c_im"])))
    c_to_st = (_band_blocks(sp["s5_c_re"]), _band_blocks(-sp["s5_c_im"]))

    g_mix, g_ffn, g_final = row(sp["g_mix"]), row(sp["g_ffn"]), row(sp["g_final"])
    b_glu, gain, dskip, b_conv = row(sp["b_glu"]), row(sp["hg_norm_gain"]), row(sp["s5_d"]), row(sp["b_conv"])

    w_in = weights("in", lam, *b_to_st, *b_to_ch, *c_to_ch, *c_to_st)["w_in"]
    u, za, zh, zgt = _in_proj(x, g_mix, w_in, tm)
    seqs = lambda v: v.reshape(nb, seq, v.shape[-1])
    toks = lambda v: v.reshape(t, v.shape[-1])
    xs3, y0 = _s5_fwd(seqs(za), b_to_st, lam, c_to_ch, dskip, nb, seq, tm)
    xs, y0 = toks(xs3), toks(y0)
    o3, sts = _hgrn_fwd(zh.reshape(nb, seq, 4 * HG_WIDTH), lb, nb, seq)
    o = o3.reshape(t, HG_WIDTH)
    wm = weights("mix", y0, o3)
    weights.forward("ffn", wm["w_out"])
    x1, u2, pa, pb, ya2, yb = _mix_fwd(x, y0, o, zh, zgt, wm["w_glu"], b_glu, gain, wm["w_pa"], wm["w_pb"],
                                       wm["w_out"], g_ffn, tm)
    wf = weights("ffn", u2)
    h = _ffn_up(u2, wf["w_up"], min(4 * tm, t))
    hc, a, dx2, dx2b, loss, dg_final = _ffn_down_loss(h, x1, tgt, wf["w_conv"], b_conv, wf["w_down"], g_final,
                                                      seq, tm)

    wgrad = functools.partial(_wgrad, tn=256, out_dtype=BF16)
    dhc, db_conv = _ffn_bwd_act(dx2b, hc, wf["w_down"], tm)
    sent = emit({"w_down": wgrad(a, dx2b, "dw_down")})
    dh, dx1, dx1b, dg_ffn, dw_conv = _ffn_bwd_up(dhc, h, dx2, x1, wf["w_conv"], wf["w_up"], _after(g_ffn, sent),
                                                 seq, tm)
    sent = emit({"w_up": wgrad(dh, u2, "dw_up"), "w_conv": dw_conv})
    (dy0, do, dzg, dzgt, m, dpa, dpb, ya1, dpre, db_glu, dgain) = _mix_bwd(
        dx1b, y0, o, zh, zgt, pa, pb, wm["w_glu"], _after(b_glu, sent), gain, wm["w_pa"], wm["w_pb"], wm["w_out"], tm)
    sent = emit({"w_out": wgrad(m, dx1b, "dw_out"), "w_pa": wgrad(ya2, dpa, "dw_pa"),
                 "w_pb": wgrad(yb, dpb, "dw_pb"), "w_glu": wgrad(ya1, dpre, "dw_glu")})
    dzh3, dlb = _hgrn_bwd(zh.reshape(nb, seq, 4 * HG_WIDTH), do.reshape(nb, seq, HG_WIDTH), sts, _after(lb, sent),
                          nb, seq)
    dza, a_s5, dlam, dd = _s5_bwd(seqs(dy0), seqs(za), xs3, c_to_st, b_to_ch, lam, dskip, nb, seq, tm)
    dza, a_s5 = toks(dza), toks(a_s5)
    dz, dx, dg_mix = _in_proj_bwd(dza, dzh3.reshape(t, 3 * HG_WIDTH), dzg, dzgt, dx1, x, g_mix, w_in, tm)
    sent = emit({"w_in": wgrad(dz, u, "dw_in")})

    band = HG_HEAD
    dbb_band = _wgrad(a_s5, za, "dbb_s5", 512, band=band, after=sent)
    dc_band = _wgrad(xs, dy0, "dc_s5", 512, band=band, after=sent)
    dbb_re = swap(_diag_blocks(dbb_band[:S5_N], S5_STATE, S5_GROUP))
    dbb_im = swap(_diag_blocks(dbb_band[S5_N:], S5_STATE, S5_GROUP))
    dc_re = swap(_diag_blocks(dc_band[:S5_N], S5_STATE, S5_GROUP))
    dc_im = -swap(_diag_blocks(dc_band[S5_N:], S5_STATE, S5_GROUP))
    da_re, da_im, dldt, db_re, db_im, dlogits = _params_bwd(
        a_re, a_im, ldt, b_re, b_im, sp["hg_lb_logits"],
        dlam[0].reshape(S5_GROUPS, S5_STATE), dlam[1].reshape(S5_GROUPS, S5_STATE), dbb_re, dbb_im, dlb)
    emit_small({"g_mix": dg_mix, "s5_a_re": da_re, "s5_a_im": da_im, "s5_log_dt": dldt.reshape(1, S5_GROUPS),
                "s5_b_re": db_re, "s5_b_im": db_im, "s5_c_re": dc_re, "s5_c_im": dc_im, "s5_d": dd, "b_glu": db_glu,
                "hg_lb_logits": dlogits, "hg_norm_gain": dgain, "g_ffn": dg_ffn, "b_conv": db_conv,
                "g_final": dg_final, "loss": loss})
    return dx.reshape(nb, seq, D_MODEL)


def _mesh_peers():
    x, y, c = lax.axis_index("x"), lax.axis_index("y"), lax.axis_index("c")
    peers = []
    for k in range(1, N_DEV):
        px, py, pc = (1 - x if k & 4 else x), (1 - y if k & 2 else y), (1 - c if k & 1 else c)
        peers.append((k, (px, py, pc), 4 * px + 2 * py + pc))
    return 4 * x + 2 * y + c, peers


_HBM = pl.BlockSpec(memory_space=pltpu.HBM)
_SEM = pl.BlockSpec(memory_space=pltpu.SEMAPHORE)


_EFFECT = pltpu.CompilerParams(has_side_effects=pltpu.SideEffectType.DATAFLOW_SIDE_EFFECTING)


def _remote(src, dst, send_sem, recv_sem, to):
    return pltpu.make_async_remote_copy(src_ref=src, dst_ref=dst, send_sem=send_sem, recv_sem=recv_sem,
                                        device_id=to, device_id_type=pl.DeviceIdType.MESH)


def _exchange_start(name, arrays, after):
    n = len(arrays)
    srcs = [pltpu.with_memory_space_constraint(a, pltpu.HBM) for a in arrays]
    lands = [pltpu.with_memory_space_constraint(lax.empty(a.shape, a.dtype), pltpu.HBM) for a in arrays]
    copies = (N_DEV - 1) * n

    def body(*refs):
        src_refs, land_refs = refs[:n], refs[n:2 * n]
        send_sems, recv_sems, token = refs[2 * n + 1], refs[2 * n + 2], refs[-1]
        my_slab, peers = _mesh_peers()
        for k, peer, slab in peers:
            for i in range(n):
                s = (k - 1) * n + i
                _remote(src_refs[i].at[slab], land_refs[i].at[my_slab], send_sems.at[s], recv_sems.at[s], peer).start()
        token[...] = jnp.zeros_like(token)

    outs = pl.pallas_call(
        body, name=name,
        out_shape=(pltpu.SemaphoreType.DMA((copies,)), pltpu.SemaphoreType.DMA((copies,)),
                   *[pltpu.HBM(a.shape, a.dtype) for a in lands], _sds((SUBLANES, 128))),
        in_specs=[_HBM] * (2 * n) + [pl.BlockSpec(memory_space=pl.ANY)],
        out_specs=(_SEM, _SEM, *[_HBM] * n, pl.BlockSpec(memory_space=pltpu.VMEM)),
        input_output_aliases={n + i: 2 + i for i in range(n)}, compiler_params=_EFFECT,
    )(*srcs, *lands, after)
    return (outs[0], outs[1], srcs, outs[2:2 + n]), outs[-1]


def _exchange_wait(name, state, *after):
    send_sems, recv_sems, srcs, lands = state
    n = len(lands)

    def body(*refs):
        src_refs, land_refs = refs[:n], refs[n:2 * n]
        send_ref, recv_ref = refs[2 * n], refs[2 * n + 1]
        _, peers = _mesh_peers()
        for k, peer, slab in peers:
            for i in range(n):
                s = (k - 1) * n + i
                copy = _remote(src_refs[i].at[slab], land_refs[i].at[slab], send_ref.at[s], recv_ref.at[s], peer)
                copy.wait_send()
                copy.wait_recv()

    outs = pl.pallas_call(
        body, name=name,
        out_shape=tuple(pltpu.HBM(a.shape, a.dtype) for a in lands),
        in_specs=[_HBM] * (2 * n) + [_SEM, _SEM] + [pl.BlockSpec(memory_space=pl.ANY)] * len(after),
        out_specs=tuple([_HBM] * n),
        input_output_aliases={n + i: i for i in range(n)}, compiler_params=_EFFECT,
    )(*srcs, *lands, send_sems, recv_sems, *after)
    return list(outs), list(srcs)


def _slab(pos):
    return 4 * pos[0] + 2 * pos[1] + pos[2]


def _chip_routes():
    x, y, c = lax.axis_index("x"), lax.axis_index("y"), lax.axis_index("c")
    return (x, y, c), (x, y, 1 - c), [(1 - x, y, c), (x, 1 - y, c), (1 - x, 1 - y, c)]


def _gather_start(name, arrays, after):
    n = len(arrays)
    me = 4 * lax.axis_index("x") + 2 * lax.axis_index("y") + lax.axis_index("c")
    srcs = [pltpu.with_memory_space_constraint(a, pltpu.HBM) for a in arrays]
    lands = [pltpu.with_memory_space_constraint(
        lax.dynamic_update_slice_in_dim(lax.empty((N_DEV,) + a.shape, a.dtype), a[None], me, 0), pltpu.HBM)
        for a in arrays]

    def body(*refs):
        src_refs, land_refs = refs[:n], refs[n:2 * n]
        send_sems, recv_sems, token = refs[2 * n + 1], refs[2 * n + 2], refs[-1]
        mine, sibling, chips = _chip_routes()
        for k, to in enumerate([sibling] + chips):
            for i in range(n):
                _remote(src_refs[i], land_refs[i].at[_slab(mine)], send_sems.at[k * n + i], recv_sems.at[k * n + i],
                        to).start()
        token[...] = jnp.zeros_like(token)

    outs = pl.pallas_call(
        body, name=name,
        out_shape=(pltpu.SemaphoreType.DMA((4 * n,)), pltpu.SemaphoreType.DMA((4 * n,)),
                   *[pltpu.HBM(a.shape, a.dtype) for a in lands], _sds((SUBLANES, 128))),
        in_specs=[_HBM] * (2 * n) + [pl.BlockSpec(memory_space=pl.ANY)],
        out_specs=(_SEM, _SEM, *[_HBM] * n, pl.BlockSpec(memory_space=pltpu.VMEM)),
        input_output_aliases={n + i: 2 + i for i in range(n)}, compiler_params=_EFFECT,
    )(*srcs, *lands, after)
    return (outs[0], outs[1], srcs, outs[2:2 + n]), outs[-1]


def _gather_forward(name, state, *after):
    send_a, recv_a, srcs, lands = state
    n = len(lands)

    def body(*refs):
        land_refs, recv_a_ref = refs[:n], refs[n]
        send_b, recv_b = refs[n + 1 + len(after)], refs[n + 2 + len(after)]
        mine, sibling, chips = _chip_routes()
        for j, chip in enumerate(chips):
            for i in range(n):
                block = land_refs[i].at[_slab(chip)]
                _remote(block, block, send_b.at[j * n + i], recv_a_ref.at[(1 + j) * n + i], chip).wait_recv()
                _remote(block, block, send_b.at[j * n + i], recv_b.at[j * n + i], sibling).start()

    outs = pl.pallas_call(
        body, name=name,
        out_shape=(pltpu.SemaphoreType.DMA((3 * n,)), pltpu.SemaphoreType.DMA((3 * n,)),
                   *[pltpu.HBM(a.shape, a.dtype) for a in lands]),
        in_specs=[_HBM] * n + [_SEM] + [pl.BlockSpec(memory_space=pl.ANY)] * len(after),
        out_specs=(_SEM, _SEM, *[_HBM] * n),
        input_output_aliases={i: 2 + i for i in range(n)}, compiler_params=_EFFECT,
    )(*lands, recv_a, *after)
    return (send_a, recv_a, srcs, list(outs[2:])), (outs[0], outs[1])


def _gather_wait(name, state, forwarded, *after):
    send_a, recv_a, srcs, lands = state
    send_b, recv_b = forwarded
    n = len(lands)

    def body(*refs):
        src_refs, land_refs = refs[:n], refs[n:2 * n]
        sa, ra, sb, rb = refs[2 * n:2 * n + 4]
        mine, sibling, chips = _chip_routes()
        for i in range(n):
            for k, to in enumerate([sibling] + chips):
                _remote(src_refs[i], land_refs[i].at[_slab(mine)], sa.at[k * n + i], ra.at[k * n + i], to).wait_send()
            theirs = land_refs[i].at[_slab(sibling)]
            _remote(theirs, theirs, sa.at[i], ra.at[i], sibling).wait_recv()
            for j, chip in enumerate(chips):
                sent = land_refs[i].at[_slab(chip)]
                got = land_refs[i].at[_slab((chip[0], chip[1], sibling[2]))]
                _remote(sent, sent, sb.at[j * n + i], rb.at[j * n + i], sibling).wait_send()
                _remote(got, got, sb.at[j * n + i], rb.at[j * n + i], sibling).wait_recv()

    outs = pl.pallas_call(
        body, name=name,
        out_shape=tuple(pltpu.HBM(a.shape, a.dtype) for a in lands),
        in_specs=[_HBM] * (2 * n) + [_SEM] * 4 + [pl.BlockSpec(memory_space=pl.ANY)] * len(after),
        out_specs=tuple([_HBM] * n),
        input_output_aliases={n + i: i for i in range(n)}, compiler_params=_EFFECT,
    )(*srcs, *lands, send_a, recv_a, send_b, recv_b, *after)
    return list(outs), list(srcs)


def _join_cols(parts, name, tr):
    _, r, c = parts.shape

    def body(p_ref, o_ref):
        for j in range(N_DEV):
            o_ref[:, j * c:(j + 1) * c] = p_ref[j]

    return _pcall(body, name, (r // tr,), [pl.BlockSpec((N_DEV, tr, c), lambda i: (0, i, 0))],
                  pl.BlockSpec((tr, N_DEV * c), lambda i: (i, 0)), _sds((r, N_DEV * c), parts.dtype))(parts)


def _split_cols(full, name, tr):
    r, c = full.shape[0], full.shape[1] // N_DEV

    def body(f_ref, o_ref):
        for j in range(N_DEV):
            o_ref[j] = f_ref[:, j * c:(j + 1) * c]

    return _pcall(body, name, (r // tr,), [pl.BlockSpec((tr, N_DEV * c), lambda i: (i, 0))],
                  pl.BlockSpec((N_DEV, tr, c), lambda i: (0, i, 0)), _sds((N_DEV, r, c), full.dtype))(full)


def _my_slab():
    return (4 * lax.axis_index("x") + 2 * lax.axis_index("y") + lax.axis_index("c")).astype(jnp.int32).reshape(1)


def _adamw(parts, sent, w, m, v, name, tile):
    _, rows, cols = w.shape

    def body(me_ref, p_ref, s_ref, w_ref, m_ref, v_ref, g_out, d_out, m_out, v_out):
        me = me_ref[0]
        g = jnp.where(me == 0, s_ref[0], p_ref[0]).astype(F32)
        for k in range(1, N_DEV):
            g = g + jnp.where(me == k, s_ref[0], p_ref[k]).astype(F32)
        m1 = ADAM_B1 * m_ref[0] + (1.0 - ADAM_B1) * g
        v1 = ADAM_B2 * v_ref[0] + (1.0 - ADAM_B2) * (g * g)
        m_hat = m1 / (1.0 - ADAM_B1 ** ADAM_STEP)
        v_hat = v1 / (1.0 - ADAM_B2 ** ADAM_STEP)
        g_out[0] = g
        d_out[0] = -ADAM_LR * (m_hat / (jnp.sqrt(v_hat) + ADAM_EPS) + ADAM_WD * w_ref[0])
        m_out[0] = m1
        v_out[0] = v1

    row = pl.BlockSpec((1, tile, cols), lambda i, me: (0, i, 0))
    return pl.pallas_call(
        body, name=name, out_shape=[_sds((1, rows, cols))] * 4,
        grid_spec=pltpu.PrefetchScalarGridSpec(
            num_scalar_prefetch=1, grid=(rows // tile,),
            in_specs=[pl.BlockSpec((N_DEV, tile, cols), lambda i, me: (0, i, 0)),
                      pl.BlockSpec((1, tile, cols), lambda i, me: (me[0], i, 0)), row, row, row],
            out_specs=[row, row, row, row]),
        compiler_params=pltpu.CompilerParams(dimension_semantics=("arbitrary",), vmem_limit_bytes=VMEM_LIMIT),
    )(_my_slab(), parts, sent, w, m, v)


BIG = {
    "w_in": ((N_IN // N_DEV, D_MODEL), False, N_IN // N_DEV // 3),
    "w_glu": ((S5_WIDTH // N_DEV, S5_WIDTH), False, S5_WIDTH // N_DEV),
    "w_pa": ((S5_WIDTH, D_MODEL // N_DEV), True, S5_WIDTH),
    "w_pb": ((HG_WIDTH, D_MODEL // N_DEV), True, HG_WIDTH),
    "w_out": ((D_MODEL // N_DEV, D_MODEL), False, D_MODEL // N_DEV),
    "w_up": ((2 * D_FF // N_DEV, D_MODEL), False, 2 * D_FF // N_DEV // 4),
    "w_conv": ((CONV_W, 2 * D_FF // N_DEV), True, CONV_W),
    "w_down": ((D_FF // N_DEV, D_MODEL), False, D_FF // N_DEV // 2),
}
TRANSPOSED = ("w_in", "w_up", "s5_b_re", "s5_b_im")
UNALIGNED_COLS = ("w_conv",)


def _stored(n, arr):
    return jnp.swapaxes(arr, -1, -2) if n in TRANSPOSED else arr


def _join_shards(n, parts):
    (a, b), by_cols, _ = BIG[n]
    if not by_cols:
        return parts.reshape(N_DEV * a, b)
    if n in UNALIGNED_COLS:
        return _join_cols(parts, "join_" + n, min(a, 256))
    return parts.transpose(1, 0, 2).reshape(a, N_DEV * b)


def _split_shards(n, full):
    (a, b), by_cols, _ = BIG[n]
    if not by_cols:
        return full.reshape(N_DEV, a, b)
    if n in UNALIGNED_COLS:
        return _split_cols(full, "split_" + n, min(a, 256))
    return full.reshape(a, N_DEV, b).transpose(1, 0, 2)


SMALL_CORE = {
    "s5_b_re": GSC, "s5_b_im": GSC, "s5_c_re": GSC, "s5_c_im": GSC,
    "g_mix": (1, D_MODEL), "g_ffn": (1, D_MODEL), "g_final": (1, D_MODEL), "s5_d": (1, S5_WIDTH),
    "b_glu": (1, S5_WIDTH), "hg_norm_gain": (1, HG_WIDTH), "hg_lb_logits": (2, HG_WIDTH), "b_conv": (1, 2 * D_FF),
    "s5_log_dt": (1, S5_GROUPS), "s5_a_re": (S5_GROUPS, S5_STATE), "s5_a_im": (S5_GROUPS, S5_STATE), "loss": (1, 1),
}
BLOCK_ROWS = 32


def _small_rows():
    rows, r = {}, 0
    for n, core in SMALL_CORE.items():
        rows[n] = r
        r += BLOCK_ROWS if len(core) == 3 else -(-math.prod(core) // PACK_W)
    return rows, -(-r // SUBLANES) * SUBLANES


SMALL_ROW, SMALL_ROWS = _small_rows()


def _small_pieces(name):
    r, core = SMALL_ROW[name], SMALL_CORE[name]
    if len(core) == 3:
        return [((g, slice(None), slice(None)), slice(r + S5_GROUP * (g % 2), r + S5_GROUP * (g % 2 + 1)),
                 slice(S5_STATE * (g // 2), S5_STATE * (g // 2 + 1))) for g in range(S5_GROUPS)]
    pieces = []
    for i in range(core[0]):
        for c0 in range(0, core[1], PACK_W):
            w, flat = min(PACK_W, core[1] - c0), i * core[1] + c0
            pieces.append(((slice(i, i + 1), slice(c0, c0 + w)), slice(r + flat // PACK_W, r + flat // PACK_W + 1),
                           slice(flat % PACK_W, flat % PACK_W + w)))
    return pieces


def _core_index(ref, name, idx):
    return (0,) * (len(ref.shape) - len(SMALL_CORE[name])) + idx


def _pack_small_grads(grads):
    names = list(SMALL_CORE)

    def body(*refs):
        pack = refs[-1]
        pack[...] = jnp.zeros_like(pack)
        for ref, n in zip(refs, names):
            for idx, rows, lanes in _small_pieces(n):
                pack[rows, lanes] = ref[_core_index(ref, n, idx)]

    return _pcall(body, "pack_small_grads", (1,), [_full(grads[n].shape) for n in names],
                  _full((SMALL_ROWS, PACK_W)), _sds((SMALL_ROWS, PACK_W)))(*[grads[n] for n in names])


def _adamw_small(parts, sent, names, rows, given, name):
    lo, hi = rows
    k = len(names)
    shapes = [given[n].shape for n in names]

    def body(*refs):
        me, p_ref, s_ref, ins, outs = refs[0][0], refs[1], refs[2], refs[3:3 + 3 * k], refs[3 + 3 * k:3 + 7 * k]
        packs, results = refs[3 + 7 * k:6 + 7 * k], refs[6 + 7 * k:]
        for j, pack in enumerate(packs):
            pack[...] = jnp.zeros_like(pack)
            for ref, n in zip(ins[j * k:(j + 1) * k], names):
                for idx, prow, lanes in _small_pieces(n):
                    pack[slice(prow.start - lo, prow.stop - lo), lanes] = ref[_core_index(ref, n, idx)]
        mine = s_ref[lo:hi, :]
        g = jnp.where(me == 0, mine, p_ref[0, lo:hi, :])
        for d in range(1, N_DEV):
            g = g + jnp.where(me == d, mine, p_ref[d, lo:hi, :])
        m1 = ADAM_B1 * packs[1][...] + (1.0 - ADAM_B1) * g
        v1 = ADAM_B2 * packs[2][...] + (1.0 - ADAM_B2) * (g * g)
        m_hat = m1 / (1.0 - ADAM_B1 ** ADAM_STEP)
        v_hat = v1 / (1.0 - ADAM_B2 ** ADAM_STEP)
        results[0][...] = g
        results[1][...] = -ADAM_LR * (m_hat / (jnp.sqrt(v_hat) + ADAM_EPS) + ADAM_WD * packs[0][...])
        results[2][...] = m1
        results[3][...] = v1
        for j, result in enumerate(results):
            for ref, n in zip(outs[j * k:(j + 1) * k], names):
                for idx, prow, lanes in _small_pieces(n):
                    ref[_core_index(ref, n, idx)] = result[slice(prow.start - lo, prow.stop - lo), lanes]

    flat = _pcall(body, name, (1,),
                  [pl.BlockSpec(memory_space=pltpu.SMEM), _full(parts.shape), _full(sent.shape)]
                  + [_full(s) for s in shapes] * 3,
                  [_full(s) for s in shapes] * 4, [_sds(s) for s in shapes] * 4,
                  scratch=[pltpu.VMEM((hi - lo, PACK_W), F32)] * 7,
                  )(_my_slab(), parts, sent, *[given[pre + n] for pre in ("", "m_", "v_") for n in names])
    return {n: [flat[j * k + i] for j in range(4)] for i, n in enumerate(names)}


def kernel(x, g_mix, w_in, s5_a_re, s5_a_im, s5_log_dt, s5_b_re, s5_b_im, s5_c_re, s5_c_im, s5_d, w_glu, b_glu, hg_lb_logits, hg_norm_gain, w_pa, w_pb, w_out, g_ffn, w_up, w_conv, b_conv, w_down, g_final, loss_target, m_g_mix, m_w_in, m_s5_a_re, m_s5_a_im, m_s5_log_dt, m_s5_b_re, m_s5_b_im, m_s5_c_re, m_s5_c_im, m_s5_d, m_w_glu, m_b_glu, m_hg_lb_logits, m_hg_norm_gain, m_w_pa, m_w_pb, m_w_out, m_g_ffn, m_w_up, m_w_conv, m_b_conv, m_w_down, m_g_final, v_g_mix, v_w_in, v_s5_a_re, v_s5_a_im, v_s5_log_dt, v_s5_b_re, v_s5_b_im, v_s5_c_re, v_s5_c_im, v_s5_d, v_w_glu, v_b_glu, v_hg_lb_logits, v_hg_norm_gain, v_w_pa, v_w_pb, v_w_out, v_g_ffn, v_w_up, v_w_conv, v_b_conv, v_w_down, v_g_final):
    given = dict(locals())
    small_names = [n for n in SMALL_CORE if n != "loss"]

    pay = {n: given[n][0] if n == "w_conv" else _stored(n, given[n])[0].astype(BF16) for n in BIG}
    groups = {"in": ["w_in"], "mix": ["w_glu", "w_pa", "w_pb", "w_out"], "ffn": ["w_up", "w_down", "w_conv"]}
    gathers, order = {}, pay["w_in"]
    for grp, names in groups.items():
        gathers[grp], order = _gather_start("gather_" + grp + "_start", [pay[n] for n in names], order)

    forwards = {}

    def forward(grp, *after):
        if grp == "in":
            after = (*after, order)
        forwards[grp] = _gather_forward("gather_" + grp + "_forward", gathers[grp], *after)

    def weights(grp, *after):
        if grp not in forwards:
            forward(grp, *after)
        got, _ = _gather_wait("gather_" + grp + "_wait", *forwards[grp], *after)
        return {n: _join_shards(n, g) for n, g in zip(groups[grp], got)}

    weights.forward = forward

    in_flight, started = [], []

    def emit(grads):
        names = list(grads)
        state, token = _exchange_start("grads_" + names[0] + "_start", [_split_shards(n, grads[n]) for n in names],
                                       grads[names[0]])
        in_flight.append((names, state))
        return token

    def emit_small(grads):
        pack = _pack_small_grads(grads)
        state, token = _gather_start("grads_small_start", [pack], pack)
        in_flight.append((["small"], state))
        started.append(token)

    sp = {n: (given[n] if n in ("g_final", "hg_lb_logits") else _stored(n, given[n])[0]) for n in small_names}
    sp["g_mix"] = _after(sp["g_mix"], order)
    dx = _local_step(x, loss_target, weights, sp, emit, emit_small)

    res = {}
    after = [started[-1]]
    for names, state in in_flight:
        if names == ["small"]:
            state, forwarded = _gather_forward("grads_small_forward", state, *after)
            parts, sent = _gather_wait("grads_small_wait", state, forwarded)
        else:
            parts, sent = _exchange_wait("grads_" + names[0] + "_wait", state, *after)
        if names != ["small"]:
            after = []
            for n, part, mine in zip(names, parts, sent):
                raw = _adamw(part, mine, *[_stored(n, given[pre + n]) for pre in ("", "m_", "v_")], "adamw_" + n,
                             BIG[n][2])
                res[n] = [_stored(n, r) for r in raw]
                after.append(raw[0])
            continue
        sgiven = {pre + n: _stored(n, given[pre + n]) for pre in ("", "m_", "v_") for n in small_names}
        for pre in ("", "m_", "v_"):
            sgiven[pre + "g_final"] = given[pre + "g_final"].reshape(1, D_MODEL)
            sgiven[pre + "loss"] = jnp.zeros((1, 1), F32)
        raw = _adamw_small(parts[0], sent[0], list(SMALL_CORE), (0, SMALL_ROWS), sgiven, "adamw_small")
        res.update({n: [_stored(n, r) for r in raw[n]] for n in small_names})
        res["g_final"] = [r.reshape(D_MODEL) for r in raw["g_final"]]
        total_loss = raw["loss"][0].reshape(())
        after = [raw["s5_b_re"][0], raw["g_mix"][0]]
    return (total_loss, dx, *[res[n][0] for n in WEIGHT_ORDER], *[res[n][1] for n in WEIGHT_ORDER],
            *[res[n][2] for n in WEIGHT_ORDER], *[res[n][3] for n in WEIGHT_ORDER])
```

```python
import functools
import math

import jax
import jax.numpy as jnp
from jax import lax
from jax.experimental import pallas as pl
from jax.experimental.pallas import tpu as pltpu

F32 = jnp.float32
BF16 = jnp.bfloat16

D_MODEL = 1024
S5_WIDTH = 512
S5_GROUP = 16
S5_GROUPS = 32
S5_STATE = 64
S5_N = S5_GROUPS * S5_STATE
HG_WIDTH = 512
HG_HEAD = 128
HG_HEADS = 4
D_FF = 2816
CONV_W = 3
CHUNK = 64
N_IN = S5_WIDTH + 4 * HG_WIDTH + 2 * D_MODEL
EPS = 1e-6
QSCALE = HG_HEAD ** -0.5

ADAM_LR = 0.001
ADAM_B1 = 0.9
ADAM_B2 = 0.999
ADAM_EPS = 1e-08
ADAM_WD = 0.01
ADAM_STEP = 10

N_DEV = 8
V7X_VMEM_BYTES = 64 * 1024 * 1024
VMEM_LIMIT = V7X_VMEM_BYTES * 7 // 8
SUBLANES = 8
LANES = 128
PACK_W = 1024

WEIGHT_ORDER = ("g_mix", "w_in", "s5_a_re", "s5_a_im", "s5_log_dt", "s5_b_re", "s5_b_im", "s5_c_re", "s5_c_im",
                "s5_d", "w_glu", "b_glu", "hg_lb_logits", "hg_norm_gain", "w_pa", "w_pb", "w_out", "g_ffn",
                "w_up", "w_conv", "b_conv", "w_down", "g_final")


def _pcall(body, name, grid, in_specs, out_specs, out_shape, scratch=()):
    return pl.pallas_call(
        body, name=name, grid=grid, in_specs=in_specs, out_specs=out_specs, out_shape=out_shape,
        scratch_shapes=list(scratch),
        compiler_params=pltpu.CompilerParams(dimension_semantics=("arbitrary",) * len(grid),
                                             vmem_limit_bytes=VMEM_LIMIT),
    )


def _full(shape):
    return pl.BlockSpec(shape, lambda *_: (0,) * len(shape))


def _sds(shape, dtype=F32):
    return jax.ShapeDtypeStruct(shape, dtype)


def _dot(a, b):
    return jnp.dot(a.astype(BF16), b.astype(BF16), preferred_element_type=F32)


def _dot_nt(a, b):
    return lax.dot_general(a.astype(BF16), b.astype(BF16), (((1,), (1,)), ((), ())), preferred_element_type=F32)


def _dot_tn(a, b):
    return lax.dot_general(a.astype(BF16), b.astype(BF16), (((0,), (0,)), ((), ())), preferred_element_type=F32)


def _hdot(a, b):
    return jnp.dot(a, b, preferred_element_type=F32, precision=lax.Precision.HIGHEST)


def _hdot_tn(a, b):
    return lax.dot_general(a, b, (((0,), (0,)), ((), ())), preferred_element_type=F32,
                           precision=lax.Precision.HIGHEST)


def _sigmoid(x):
    return jax.nn.sigmoid(x)


GELU_C = math.sqrt(2.0 / math.pi)
GELU_A = 0.044715


def _gelu(x):
    return 0.5 * x * (1.0 + jnp.tanh(GELU_C * (x + GELU_A * (x * x * x))))


def _gelu_grad(x):
    t = jnp.tanh(GELU_C * (x + GELU_A * (x * x * x)))
    return 0.5 * (1.0 + t) + 0.5 * x * (1.0 - t * t) * (GELU_C * (1.0 + 3.0 * GELU_A * x * x))


def _cumsum_rows(v, reverse=False):
    n = v.shape[0]
    row = lax.broadcasted_iota(jnp.int32, v.shape, 0)
    s = 1
    while s < n:
        if reverse:
            v = v + jnp.where(row < n - s, pltpu.roll(v, n - s, axis=0), 0.0)
        else:
            v = v + jnp.where(row >= s, pltpu.roll(v, s, axis=0), 0.0)
        s *= 2
    return v


def _token_tile(seq):
    return min(256, seq)


def _s5_coeffs(a_re, a_im, ldt):
    dt = jnp.exp(ldt)
    mag = jnp.exp(a_re * dt)
    ang = a_im * dt
    lb_re = mag * jnp.cos(ang)
    lb_im = mag * jnp.sin(ang)
    den = a_re * a_re + a_im * a_im
    n_re = lb_re - 1.0
    n_im = lb_im
    co_re = (n_re * a_re + n_im * a_im) / den
    co_im = (n_im * a_re - n_re * a_im) / den
    return lb_re, lb_im, co_re, co_im


GS, GSC = (S5_GROUPS, S5_STATE), (S5_GROUPS, S5_GROUP, S5_STATE)


def _params_fwd(a_re, a_im, ldt, bt_re, bt_im, logits):
    def body(are, aim, ld, bre, bim, lg, lr_o, li_o, bbr_o, bbi_o, lb_o):
        lr, li, co_re, co_im = _s5_coeffs(are[...], aim[...], ld[...])
        lr_o[...] = lr
        li_o[...] = li
        for g in range(S5_GROUPS):
            cr, ci = co_re[g:g + 1, :], co_im[g:g + 1, :]
            bbr_o[g] = cr * bre[g] - ci * bim[g]
            bbi_o[g] = cr * bim[g] + ci * bre[g]
        lb_o[...] = _sigmoid(lg[0:1, :] - lg[1:2, :])

    return _pcall(body, "params_fwd", (1,),
                  [_full(GS), _full(GS), _full((S5_GROUPS, 1)), _full(GSC), _full(GSC), _full((2, HG_WIDTH))],
                  [_full(GS), _full(GS), _full(GSC), _full(GSC), _full((1, HG_WIDTH))],
                  [_sds(GS), _sds(GS), _sds(GSC), _sds(GSC), _sds((1, HG_WIDTH))],
                  )(a_re, a_im, ldt, bt_re, bt_im, logits)


def _params_bwd(a_re, a_im, ldt, bt_re, bt_im, logits, dlr, dli, dbbr, dbbi, dlb):
    def body(are, aim, ld, bre, bim, lg, dlr_r, dli_r, dbbr_r, dbbi_r, dlb_r,
             dare_o, daim_o, dld_o, dbre_o, dbim_o, dlg_o, dcr_ref, dci_ref):
        (_, _, co_re, co_im), vjp = jax.vjp(_s5_coeffs, are[...], aim[...], ld[...])
        for g in range(S5_GROUPS):
            cr, ci = co_re[g:g + 1, :], co_im[g:g + 1, :]
            gr, gi, br, bi = dbbr_r[g], dbbi_r[g], bre[g], bim[g]
            dbre_o[g] = cr * gr + ci * gi
            dbim_o[g] = cr * gi - ci * gr
            dcr_ref[g:g + 1, :] = jnp.sum(gr * br + gi * bi, axis=0, keepdims=True)
            dci_ref[g:g + 1, :] = jnp.sum(gi * br - gr * bi, axis=0, keepdims=True)
        dare, daim, dld = vjp((dlr_r[...], dli_r[...], dcr_ref[...], dci_ref[...]))
        dare_o[...] = dare
        daim_o[...] = daim
        dld_o[...] = dld
        lb = _sigmoid(lg[0:1, :] - lg[1:2, :])
        d0 = dlb_r[...] * lb * (1.0 - lb)
        dlg_o[0:1, :] = d0
        dlg_o[1:2, :] = -d0

    return _pcall(body, "params_bwd", (1,),
                  [_full(GS), _full(GS), _full((S5_GROUPS, 1)), _full(GSC), _full(GSC), _full((2, HG_WIDTH)),
                   _full(GS), _full(GS), _full(GSC), _full(GSC), _full((1, HG_WIDTH))],
                  [_full(GS), _full(GS), _full((S5_GROUPS, 1)), _full(GSC), _full(GSC), _full((2, HG_WIDTH))],
                  [_sds(GS), _sds(GS), _sds((S5_GROUPS, 1)), _sds(GSC), _sds(GSC), _sds((2, HG_WIDTH))],
                  scratch=[pltpu.VMEM(GS, F32), pltpu.VMEM(GS, F32)],
                  )(a_re, a_im, ldt, bt_re, bt_im, logits, dlr, dli, dbbr, dbbi, dlb)


def _band_blocks(m):
    g, r, c = m.shape
    gb = g // S5_BANDS
    m4 = m.astype(BF16).reshape(S5_BANDS, gb, r, c)
    on_diag = jnp.eye(gb, dtype=bool)[None, :, None, :, None]
    return jnp.where(on_diag, m4[:, :, :, None, :], 0).reshape(S5_BANDS, gb * r, gb * c)


def _diag_blocks(band, r, c):
    g, nb = band.shape[0] // r, band.shape[1] // c
    on_diag = (jnp.arange(g) % nb)[:, None, None, None] == jnp.arange(nb)[None, None, :, None]
    return jnp.sum(jnp.where(on_diag, band.reshape(g, r, nb, c), 0.0), axis=2)


def _in_proj(x, g_mix, w_in, tm):
    t = x.shape[0]

    def body(x_ref, g_ref, w_ref, u_ref, za_ref, zh_ref, zg_ref):
        xv = x_ref[...]
        r = lax.rsqrt(jnp.mean(xv * xv, axis=-1, keepdims=True) + EPS)
        u = (xv * r * g_ref[...]).astype(BF16)
        u_ref[...] = u
        za_ref[...] = _dot_nt(u, w_ref[0:S5_WIDTH, :])
        zh_ref[...] = _dot_nt(u, w_ref[S5_WIDTH:S5_WIDTH + 4 * HG_WIDTH, :])
        zg_ref[...] = _dot_nt(u, w_ref[S5_WIDTH + 4 * HG_WIDTH:, :]).astype(BF16)

    row = lambda w: pl.BlockSpec((tm, w), lambda i: (i, 0))
    return _pcall(body, "in_proj", (t // tm,),
                  [row(D_MODEL), _full((1, D_MODEL)), _full((N_IN, D_MODEL))],
                  [row(D_MODEL), row(S5_WIDTH), row(4 * HG_WIDTH), row(2 * D_MODEL)],
                  [_sds((t, D_MODEL), BF16), _sds((t, S5_WIDTH)), _sds((t, 4 * HG_WIDTH)),
                   _sds((t, 2 * D_MODEL), BF16)],
                  )(x, g_mix, w_in)


S5_LANES = 512
S5_BANDS = 4


def _band(q):
    return (slice(q * S5_WIDTH // S5_BANDS, (q + 1) * S5_WIDTH // S5_BANDS),
            slice(q * S5_N // S5_BANDS, (q + 1) * S5_N // S5_BANDS))


def _im(st):
    return slice(S5_N + st.start, S5_N + st.stop)


SCAN_UNROLL = 8


def _complex_scan(buf_ref, lam_ref, st_ref, nb, ts, reverse):
    lanes = [slice(cc * S5_LANES, (cc + 1) * S5_LANES) for cc in range(S5_N // S5_LANES)]
    chains = [(b, re) for b in range(nb) for re in lanes]
    nch = len(chains)
    wr = {re.start: lam_ref[0:1, re] for re in lanes}
    wi = {re.start: -lam_ref[1:2, re] if reverse else lam_ref[1:2, re] for re in lanes}

    def block(ib, carry):
        vr, vi = list(carry[:nch]), list(carry[nch:])
        first = ts - SCAN_UNROLL - ib * SCAN_UNROLL if reverse else ib * SCAN_UNROLL
        first = pl.multiple_of(first, SCAN_UNROLL)
        for k in range(SCAN_UNROLL):
            row = pl.ds(first + (SCAN_UNROLL - 1 - k if reverse else k), 1)
            for c, (b, re) in enumerate(chains):
                nr = wr[re.start] * vr[c] - wi[re.start] * vi[c] + buf_ref[b, row, re]
                ni = wr[re.start] * vi[c] + wi[re.start] * vr[c] + buf_ref[b, row, _im(re)]
                buf_ref[b, row, re] = nr
                buf_ref[b, row, _im(re)] = ni
                vr[c], vi[c] = nr, ni
        return tuple(vr + vi)

    init = tuple(st_ref[b, 0:1, re] for b, re in chains) + tuple(st_ref[b, 1:2, re] for b, re in chains)
    last = lax.fori_loop(0, ts // SCAN_UNROLL, block, init)
    for c, (b, re) in enumerate(chains):
        st_ref[b, 0:1, re] = last[c]
        st_ref[b, 1:2, re] = last[nch + c]


BAND_CH = S5_WIDTH // S5_BANDS
BAND_ST = S5_N // S5_BANDS


def _s5_fwd(za, b_bands, lam, c_bands, dskip, nb, seq, ts):
    nts = seq // ts

    def body(za_ref, br_ref, bi_ref, lam_ref, cr_ref, ci_ref, d_ref, xs_ref, y_ref, buf_ref, st_ref):
        @pl.when(pl.program_id(0) == 0)
        def _():
            st_ref[...] = jnp.zeros_like(st_ref)

        for b in range(nb):
            zav = za_ref[b]
            for q in range(S5_BANDS):
                ch, st = _band(q)
                buf_ref[b, :, st] = _dot(zav[:, ch], br_ref[q])
                buf_ref[b, :, _im(st)] = _dot(zav[:, ch], bi_ref[q])
        _complex_scan(buf_ref, lam_ref, st_ref, nb, ts, reverse=False)
        for b in range(nb):
            zav = za_ref[b]
            xs_ref[b] = buf_ref[b].astype(BF16)
            for q in range(S5_BANDS):
                ch, st = _band(q)
                y_ref[b, :, ch] = (_dot(xs_ref[b, :, st], cr_ref[q]) + _dot(xs_ref[b, :, _im(st)], ci_ref[q])
                                   + d_ref[:, ch] * zav[:, ch])

    tok = lambda w: pl.BlockSpec((nb, ts, w), lambda j: (0, j, 0))
    to_st, to_ch = _full((S5_BANDS, BAND_CH, BAND_ST)), _full((S5_BANDS, BAND_ST, BAND_CH))
    return _pcall(body, "s5_fwd", (nts,),
                  [tok(S5_WIDTH), to_st, to_st, _full((2, S5_N)), to_ch, to_ch, _full((1, S5_WIDTH))],
                  [tok(2 * S5_N), tok(S5_WIDTH)],
                  [_sds((nb, seq, 2 * S5_N), BF16), _sds((nb, seq, S5_WIDTH))],
                  scratch=[pltpu.VMEM((nb, ts, 2 * S5_N), F32), pltpu.VMEM((nb, 2, S5_N), F32)],
                  )(za, *b_bands, lam, *c_bands, dskip)


def _hgrn_gates(zq, zf, lbh):
    sf = _sigmoid(zf)
    f = lbh + (1.0 - lbh) * sf
    sq = _sigmoid(zq)
    qa = zq * sq * QSCALE
    bc = _cumsum_rows(jnp.log(f))
    bm = bc[CHUNK // 2 - 1:CHUNK // 2, :]
    bl = bc[CHUNK - 1:CHUNK, :]
    return sf, f, sq, qa, bc, bm, bl


def _hgrn_fwd(zh, lb, nb, seq):
    nc = seq // CHUNK

    def body(zh_ref, lb_ref, o_ref, sts_ref, st_ref):
        @pl.when(pl.program_id(0) == 0)
        def _():
            st_ref[...] = jnp.zeros_like(st_ref)

        causal = (lax.broadcasted_iota(jnp.int32, (CHUNK, CHUNK), 0)
                  >= lax.broadcasted_iota(jnp.int32, (CHUNK, CHUNK), 1))
        for b in range(nb):
            for h in range(HG_HEADS):
                hs = slice(h * HG_HEAD, (h + 1) * HG_HEAD)
                zq = zh_ref[b, :, h * HG_HEAD:(h + 1) * HG_HEAD]
                zf = zh_ref[b, :, HG_WIDTH + h * HG_HEAD:HG_WIDTH + (h + 1) * HG_HEAD]
                zi = zh_ref[b, :, 2 * HG_WIDTH + h * HG_HEAD:2 * HG_WIDTH + (h + 1) * HG_HEAD]
                _, f, _, qa, bc, bm, bl = _hgrn_gates(zq, zf, lb_ref[:, hs])
                k = 1.0 - f
                qt = qa * jnp.exp(bc - bm)
                kt = k * jnp.exp(bm - bc)
                qb = qa * jnp.exp(bc)
                kd = k * jnp.exp(bl - bc)
                st = st_ref[b, h]
                sts_ref[b, 0, h] = st
                a = jnp.where(causal, _dot_nt(qt, kt), 0.0)
                o_ref[b, :, hs] = _dot(a, zi) + _dot_nt(qb, st)
                st_ref[b, h] = st * jnp.exp(bl) + _dot_tn(zi, kd)

    return _pcall(body, "hgrn_fwd", (nc,),
                  [pl.BlockSpec((nb, CHUNK, 4 * HG_WIDTH), lambda c: (0, c, 0)), _full((1, HG_WIDTH))],
                  [pl.BlockSpec((nb, CHUNK, HG_WIDTH), lambda c: (0, c, 0)),
                   pl.BlockSpec((nb, 1, HG_HEADS, HG_HEAD, HG_HEAD), lambda c: (0, c, 0, 0, 0))],
                  [_sds((nb, seq, HG_WIDTH)), _sds((nb, nc, HG_HEADS, HG_HEAD, HG_HEAD))],
                  scratch=[pltpu.VMEM((nb, HG_HEADS, HG_HEAD, HG_HEAD), F32)])(zh, lb)


def _head_rms(o):
    parts = []
    for h in range(HG_HEADS):
        oh = o[:, h * HG_HEAD:(h + 1) * HG_HEAD]
        r = lax.rsqrt(jnp.mean(oh * oh, axis=-1, keepdims=True) + EPS)
        parts.append(jnp.broadcast_to(r, oh.shape))
    return jnp.concatenate(parts, axis=1)


def _head_mean(v):
    parts = []
    for h in range(HG_HEADS):
        vh = v[:, h * HG_HEAD:(h + 1) * HG_HEAD]
        parts.append(jnp.broadcast_to(jnp.mean(vh, axis=-1, keepdims=True), vh.shape))
    return jnp.concatenate(parts, axis=1)


def _mix_fwd(x, y0, o, zh, zgt, w_glu, b_glu, gain, w_pa, w_pb, w_out, g_ffn, tm):
    t = x.shape[0]

    def body(x_ref, y0_ref, o_ref, zg_ref, zgt_ref, wglu_ref, bglu_ref, gain_ref, wpa_ref, wpb_ref, wout_ref,
             gffn_ref, x1_ref, u2_ref, pa_ref, pb_ref, ya2_ref, yb_ref):
        ya1 = _gelu(y0_ref[...])
        s = _sigmoid(_dot(ya1, wglu_ref[...]) + bglu_ref[...])
        ya2 = (ya1 * s).astype(BF16)
        ov = o_ref[...]
        zg = zg_ref[...]
        yb = (ov * _head_rms(ov) * gain_ref[...] * (zg * _sigmoid(zg))).astype(BF16)
        ya2_ref[...] = ya2
        yb_ref[...] = yb
        pa = jnp.dot(ya2, wpa_ref[...], preferred_element_type=F32)
        pb = jnp.dot(yb, wpb_ref[...], preferred_element_type=F32)
        pa_ref[...] = pa.astype(BF16)
        pb_ref[...] = pb.astype(BF16)
        m = (_sigmoid(zgt_ref[:, 0:D_MODEL].astype(F32)) * pa
             + _sigmoid(zgt_ref[:, D_MODEL:].astype(F32)) * pb)
        x1 = x_ref[...] + _dot(m, wout_ref[...])
        x1_ref[...] = x1
        r = lax.rsqrt(jnp.mean(x1 * x1, axis=-1, keepdims=True) + EPS)
        u2_ref[...] = (x1 * r * gffn_ref[...]).astype(BF16)

    row = lambda w: pl.BlockSpec((tm, w), lambda i: (i, 0))
    return _pcall(body, "mix_fwd", (t // tm,),
                  [row(D_MODEL), row(S5_WIDTH), row(HG_WIDTH), pl.BlockSpec((tm, HG_WIDTH), lambda i: (i, 3)),
                   row(2 * D_MODEL), _full((S5_WIDTH, S5_WIDTH)), _full((1, S5_WIDTH)), _full((1, HG_WIDTH)),
                   _full((S5_WIDTH, D_MODEL)), _full((HG_WIDTH, D_MODEL)), _full((D_MODEL, D_MODEL)),
                   _full((1, D_MODEL))],
                  [row(D_MODEL), row(D_MODEL), row(D_MODEL), row(D_MODEL), row(S5_WIDTH), row(HG_WIDTH)],
                  [_sds((t, D_MODEL)), _sds((t, D_MODEL), BF16), _sds((t, D_MODEL), BF16), _sds((t, D_MODEL), BF16),
                   _sds((t, S5_WIDTH), BF16), _sds((t, HG_WIDTH), BF16)],
                  )(x, y0, o, zh, zgt, w_glu, b_glu, gain, w_pa, w_pb, w_out, g_ffn)


FF_COLS = 256
FF_UP_TILE = 2 * D_FF // 4


def _ffn_up(u2, w_up, tm):
    t = u2.shape[0]
    n = 2 * D_FF

    def body(u_ref, w_ref, h_ref):
        h_ref[...] = _dot_nt(u_ref[...], w_ref[...]).astype(BF16)

    return _pcall(body, "ffn_up", (n // FF_UP_TILE, t // tm),
                  [pl.BlockSpec((tm, D_MODEL), lambda j, i: (i, 0)),
                   pl.BlockSpec((FF_UP_TILE, D_MODEL), lambda j, i: (j, 0))],
                  pl.BlockSpec((tm, FF_UP_TILE), lambda j, i: (i, j)),
                  _sds((t, n), BF16))(u2, w_up)


HALO = 16


def _shift_matrix(tm):
    r = lax.broadcasted_iota(jnp.int32, (tm, tm), 0)
    c = lax.broadcasted_iota(jnp.int32, (tm, tm), 1)
    return jnp.where(r == c + 1, 1.0, 0.0).astype(BF16)


def _conv_cols(h_ref, halo_ref, valid, wc_ref, bc_ref, c0):
    cs = slice(c0, c0 + FF_COLS)
    cur = h_ref[:, cs].astype(F32)
    prev = jnp.where(valid, halo_ref[:, cs].astype(F32), 0.0)
    full = jnp.concatenate([prev, cur], axis=0)
    h1 = pltpu.roll(full, 1, axis=0)[HALO:]
    h2 = pltpu.roll(full, 2, axis=0)[HALO:]
    return h2 * wc_ref[0:1, cs] + h1 * wc_ref[1:2, cs] + cur * wc_ref[2:3, cs] + bc_ref[:, cs]


def _ffn_down_loss(h, x1, tgt, w_conv, b_conv, w_down, g_final, seq, tm):
    t = h.shape[0]
    tps = seq // tm
    n = 2 * D_FF

    def body(h_ref, halo_ref, x1_ref, tgt_ref, wc_ref, bc_ref, wd_ref, gf_ref,
             hc_ref, a_ref, dx2_ref, dx2b_ref, loss_ref, dgf_ref):
        i = pl.program_id(0)

        @pl.when(i == 0)
        def _():
            loss_ref[...] = jnp.zeros_like(loss_ref)
            dgf_ref[...] = jnp.zeros_like(dgf_ref)

        valid = (i % tps) != 0
        x2 = x1_ref[...]
        for j in range(D_FF // FF_COLS):
            gate = _conv_cols(h_ref, halo_ref, valid, wc_ref, bc_ref, j * FF_COLS)
            val = _conv_cols(h_ref, halo_ref, valid, wc_ref, bc_ref, D_FF + j * FF_COLS)
            hc_ref[:, j * FF_COLS:(j + 1) * FF_COLS] = gate.astype(BF16)
            hc_ref[:, D_FF + j * FF_COLS:D_FF + (j + 1) * FF_COLS] = val.astype(BF16)
            a = (gate * _sigmoid(gate) * val).astype(BF16)
            a_ref[:, j * FF_COLS:(j + 1) * FF_COLS] = a
            x2 = x2 + jnp.dot(a, wd_ref[j * FF_COLS:(j + 1) * FF_COLS, :], preferred_element_type=F32)
        r = lax.rsqrt(jnp.mean(x2 * x2, axis=-1, keepdims=True) + EPS)
        xn = x2 * r
        g = gf_ref[...]
        e = xn * g - tgt_ref[...]
        loss_ref[...] += (0.5 / D_MODEL) * jnp.sum(e * e).reshape(1, 1)
        dy = e * (1.0 / D_MODEL)
        dgf_ref[...] += jnp.sum(dy * xn, axis=0, keepdims=True)
        dxn = dy * g
        dx2 = r * (dxn - xn * jnp.mean(dxn * xn, axis=-1, keepdims=True))
        dx2_ref[...] = dx2
        dx2b_ref[...] = dx2.astype(BF16)

    row = lambda w: pl.BlockSpec((tm, w), lambda i: (i, 0))
    halo = pl.BlockSpec((HALO, n), lambda i: (jnp.maximum(i * (tm // HALO) - 1, 0), 0))
    return _pcall(body, "ffn_down_loss", (t // tm,),
                  [row(n), halo, row(D_MODEL), row(D_MODEL), _full((CONV_W, n)), _full((1, n)),
                   _full((D_FF, D_MODEL)), _full((1, D_MODEL))],
                  [row(n), row(D_FF), row(D_MODEL), row(D_MODEL), _full((1, 1)), _full((1, D_MODEL))],
                  [_sds((t, n), BF16), _sds((t, D_FF), BF16), _sds((t, D_MODEL)), _sds((t, D_MODEL), BF16),
                   _sds((1, 1)), _sds((1, D_MODEL))],
                  )(h, h, x1, tgt, w_conv, b_conv, w_down, g_final)


def _wgrad(a, b, name, tn, out_dtype=F32, band=None, after=None):
    t, m = a.shape
    n = b.shape[1] if band is None else band
    nbands = 1 if band is None else b.shape[1] // band
    after = b if after is None else after

    def body(a_ref, b_ref, after_ref, o_ref):
        o_ref[...] = _dot_tn(a_ref[...], b_ref[...]).astype(out_dtype)

    return _pcall(body, name, (m // tn,),
                  [pl.BlockSpec((t, tn), lambda i: (0, i)), pl.BlockSpec((t, n), lambda i: (0, i % nbands)),
                   pl.BlockSpec(memory_space=pl.ANY)],
                  pl.BlockSpec((tn, n), lambda i: (i, 0)), _sds((m, n), out_dtype))(a, b, after)


def _ffn_bwd_act(dx2b, hc, w_down, tm):
    t = hc.shape[0]
    n = 2 * D_FF

    def body(dx2_ref, hc_ref, wd_ref, dhc_ref, dbc_ref):
        @pl.when(pl.program_id(0) == 0)
        def _():
            dbc_ref[...] = jnp.zeros_like(dbc_ref)

        dx2 = dx2_ref[...]
        for j in range(D_FF // FF_COLS):
            gs = slice(j * FF_COLS, (j + 1) * FF_COLS)
            vs = slice(D_FF + j * FF_COLS, D_FF + (j + 1) * FF_COLS)
            gate = hc_ref[:, gs].astype(F32)
            val = hc_ref[:, vs].astype(F32)
            da = _dot_nt(dx2, wd_ref[gs, :])
            sg = _sigmoid(gate)
            dgate = da * val * (sg * (1.0 + gate * (1.0 - sg)))
            dval = da * (gate * sg)
            dhc_ref[:, gs] = dgate.astype(BF16)
            dhc_ref[:, vs] = dval.astype(BF16)
            dbc_ref[:, gs] += jnp.sum(dgate, axis=0, keepdims=True)
            dbc_ref[:, vs] += jnp.sum(dval, axis=0, keepdims=True)

    row = lambda w: pl.BlockSpec((tm, w), lambda i: (i, 0))
    return _pcall(body, "ffn_bwd_act", (t // tm,),
                  [row(D_MODEL), row(n), _full((D_FF, D_MODEL))],
                  [row(n), _full((1, n))],
                  [_sds((t, n), BF16), _sds((1, n))],
                  )(dx2b, hc, w_down)


def _ffn_bwd_up(dhc, h, dx2, x1, w_conv, w_up, g_ffn, seq, tm):
    t = dhc.shape[0]
    tps = seq // tm
    n = 2 * D_FF
    last = t // HALO - 1

    def body(dhc_ref, halo_ref, h_ref, dx2_ref, x1_ref, wc_ref, wu_ref, gf_ref,
             dh_ref, dx1_ref, dx1b_ref, dgf_ref, dwc_ref):
        i = pl.program_id(0)

        @pl.when(i == 0)
        def _():
            dgf_ref[...] = jnp.zeros_like(dgf_ref)
            dwc_ref[...] = jnp.zeros_like(dwc_ref)

        valid = ((i + 1) % tps) != 0
        du2 = jnp.zeros((tm, D_MODEL), F32)
        for j in range(n // FF_COLS):
            cs = slice(j * FF_COLS, (j + 1) * FF_COLS)
            cur = dhc_ref[:, cs].astype(F32)
            nxt = jnp.where(valid, halo_ref[:, cs].astype(F32), 0.0)
            full = jnp.concatenate([cur, nxt], axis=0)
            d1 = pltpu.roll(full, tm + HALO - 1, axis=0)[:tm]
            d2 = pltpu.roll(full, tm + HALO - 2, axis=0)[:tm]
            dh = (cur * wc_ref[2:3, cs] + d1 * wc_ref[1:2, cs] + d2 * wc_ref[0:1, cs]).astype(BF16)
            dh_ref[:, cs] = dh
            du2 = du2 + _dot(dh, wu_ref[cs, :])
            hv = h_ref[:, cs].astype(F32)
            dwc_ref[0:1, cs] += jnp.sum(hv * d2, axis=0, keepdims=True)
            dwc_ref[1:2, cs] += jnp.sum(hv * d1, axis=0, keepdims=True)
            dwc_ref[2:3, cs] += jnp.sum(hv * cur, axis=0, keepdims=True)
        x1 = x1_ref[...]
        r = lax.rsqrt(jnp.mean(x1 * x1, axis=-1, keepdims=True) + EPS)
        xn = x1 * r
        dgf_ref[...] += jnp.sum(du2 * xn, axis=0, keepdims=True)
        dxn = du2 * gf_ref[...]
        dx1 = dx2_ref[...] + r * (dxn - xn * jnp.mean(dxn * xn, axis=-1, keepdims=True))
        dx1_ref[...] = dx1
        dx1b_ref[...] = dx1.astype(BF16)

    row = lambda w: pl.BlockSpec((tm, w), lambda i: (i, 0))
    halo = pl.BlockSpec((HALO, n), lambda i: (jnp.minimum((i + 1) * (tm // HALO), last), 0))
    return _pcall(body, "ffn_bwd_up", (t // tm,),
                  [row(n), halo, row(n), row(D_MODEL), row(D_MODEL), _full((CONV_W, n)), _full((n, D_MODEL)),
                   _full((1, D_MODEL))],
                  [row(n), row(D_MODEL), row(D_MODEL), _full((1, D_MODEL)), _full((CONV_W, n))],
                  [_sds((t, n), BF16), _sds((t, D_MODEL)), _sds((t, D_MODEL), BF16), _sds((1, D_MODEL)),
                   _sds((CONV_W, n))],
                  )(dhc, dhc, h, dx2, x1, w_conv, w_up, g_ffn)


def _mix_bwd(dx1, y0, o, zh, zgt, pa, pb, w_glu, b_glu, gain, w_pa, w_pb, w_out, tm):
    t = dx1.shape[0]

    def body(dx1_ref, y0_ref, o_ref, zg_ref, zgt_ref, pa_ref, pb_ref, wglu_ref, bglu_ref, gain_ref, wpa_ref,
             wpb_ref, wout_ref,
             dy0_ref, do_ref, dzg_ref, dzgt_ref, m_ref, dpa_ref, dpb_ref, ya1_ref, dpre_ref, dbglu_ref, dgain_ref):
        @pl.when(pl.program_id(0) == 0)
        def _():
            dbglu_ref[...] = jnp.zeros_like(dbglu_ref)
            dgain_ref[...] = jnp.zeros_like(dgain_ref)

        dm = _dot_nt(dx1_ref[...], wout_ref[...])
        sga = _sigmoid(zgt_ref[:, 0:D_MODEL].astype(F32))
        sgb = _sigmoid(zgt_ref[:, D_MODEL:].astype(F32))
        pa = pa_ref[...].astype(F32)
        pb = pb_ref[...].astype(F32)
        m_ref[...] = (sga * pa + sgb * pb).astype(BF16)
        dzgt_ref[:, 0:D_MODEL] = (dm * pa * sga * (1.0 - sga)).astype(BF16)
        dzgt_ref[:, D_MODEL:] = (dm * pb * sgb * (1.0 - sgb)).astype(BF16)
        dpa = (dm * sga).astype(BF16)
        dpb = (dm * sgb).astype(BF16)
        dpa_ref[...] = dpa
        dpb_ref[...] = dpb
        dya2 = _dot_nt(dpa, wpa_ref[...])
        dyb = _dot_nt(dpb, wpb_ref[...])
        y0 = y0_ref[...]
        ya1 = _gelu(y0)
        ya1_ref[...] = ya1.astype(BF16)
        s = _sigmoid(_dot(ya1, wglu_ref[...]) + bglu_ref[...])
        dpre = dya2 * ya1 * s * (1.0 - s)
        dpre_ref[...] = dpre.astype(BF16)
        dbglu_ref[...] += jnp.sum(dpre, axis=0, keepdims=True)
        dya1 = dya2 * s + _dot_nt(dpre, wglu_ref[...])
        dy0_ref[...] = dya1 * _gelu_grad(y0)
        ov = o_ref[...]
        zg = zg_ref[...]
        oh = ov * _head_rms(ov)
        on = oh * gain_ref[...]
        sz = _sigmoid(zg)
        dzg_ref[...] = (dyb * on * (sz * (1.0 + zg * (1.0 - sz)))).astype(BF16)
        don = dyb * (zg * sz)
        dgain_ref[...] += jnp.sum(don * oh, axis=0, keepdims=True)
        doh = don * gain_ref[...]
        do_ref[...] = _head_rms(ov) * (doh - oh * _head_mean(doh * oh))

    row = lambda w: pl.BlockSpec((tm, w), lambda i: (i, 0))
    return _pcall(body, "mix_bwd", (t // tm,),
                  [row(D_MODEL), row(S5_WIDTH), row(HG_WIDTH), pl.BlockSpec((tm, HG_WIDTH), lambda i: (i, 3)),
                   row(2 * D_MODEL), row(D_MODEL), row(D_MODEL), _full((S5_WIDTH, S5_WIDTH)), _full((1, S5_WIDTH)),
                   _full((1, HG_WIDTH)), _full((S5_WIDTH, D_MODEL)), _full((HG_WIDTH, D_MODEL)),
                   _full((D_MODEL, D_MODEL))],
                  [row(S5_WIDTH), row(HG_WIDTH), row(HG_WIDTH), row(2 * D_MODEL), row(D_MODEL), row(D_MODEL),
                   row(D_MODEL), row(S5_WIDTH), row(S5_WIDTH), _full((1, S5_WIDTH)), _full((1, HG_WIDTH))],
                  [_sds((t, S5_WIDTH)), _sds((t, HG_WIDTH)), _sds((t, HG_WIDTH), BF16), _sds((t, 2 * D_MODEL), BF16),
                   _sds((t, D_MODEL), BF16), _sds((t, D_MODEL), BF16), _sds((t, D_MODEL), BF16),
                   _sds((t, S5_WIDTH), BF16), _sds((t, S5_WIDTH), BF16), _sds((1, S5_WIDTH)), _sds((1, HG_WIDTH))],
                  )(dx1, y0, o, zh, zgt, pa, pb, w_glu, b_glu, gain, w_pa, w_pb, w_out)


def _s5_bwd(dy0, za, xs, c_bands, b_bands, lam, dskip, nb, seq, ts):
    nts = seq // ts

    def body(dy0_ref, za_ref, xs_ref, halo_ref, cr_ref, ci_ref, br_ref, bi_ref, lam_ref, d_ref,
             dza_ref, a_ref, dlam_ref, dd_ref, acc_ref, st_ref):
        j = pl.program_id(0)

        @pl.when(j == 0)
        def _():
            dlam_ref[...] = jnp.zeros_like(dlam_ref)
            dd_ref[...] = jnp.zeros_like(dd_ref)
            st_ref[...] = jnp.zeros_like(st_ref)

        for b in range(nb):
            dy0 = dy0_ref[b]
            for q in range(S5_BANDS):
                ch, st = _band(q)
                acc_ref[b, :, st] = _dot(dy0[:, ch], cr_ref[q])
                acc_ref[b, :, _im(st)] = _dot(dy0[:, ch], ci_ref[q])
        _complex_scan(acc_ref, lam_ref, st_ref, nb, ts, reverse=True)
        shift = _shift_matrix(ts)
        top = lax.broadcasted_iota(jnp.int32, (SUBLANES, S5_LANES), 0) == 0
        for b in range(nb):
            a_ref[b] = acc_ref[b].astype(BF16)
            first = jnp.where(j == nts - 1, 0.0, halo_ref[b, HALO - 1:HALO, :].astype(F32))

            def shifted(cols):
                xp = jnp.dot(shift, xs_ref[b, :, cols], preferred_element_type=F32)
                return jnp.concatenate([xp[:SUBLANES] + jnp.where(top, first[:, cols], 0.0), xp[SUBLANES:]], axis=0)

            for cc in range(S5_N // S5_LANES):
                re = slice(cc * S5_LANES, (cc + 1) * S5_LANES)
                ar, ai, xr, xi = acc_ref[b, :, re], acc_ref[b, :, _im(re)], shifted(re), shifted(_im(re))
                dlam_ref[0:1, re] += jnp.sum(ar * xr + ai * xi, axis=0, keepdims=True)
                dlam_ref[1:2, re] += jnp.sum(ai * xr - ar * xi, axis=0, keepdims=True)
            dy0 = dy0_ref[b]
            for q in range(S5_BANDS):
                ch, st = _band(q)
                dza_ref[b, :, ch] = (_dot(a_ref[b, :, st], br_ref[q]) + _dot(a_ref[b, :, _im(st)], bi_ref[q])
                                     + d_ref[:, ch] * dy0[:, ch]).astype(BF16)
            dd_ref[...] += jnp.sum(dy0 * za_ref[b], axis=0, keepdims=True)

    tile = lambda j: nts - 1 - j
    tok = lambda w: pl.BlockSpec((nb, ts, w), lambda j: (0, tile(j), 0))
    halo = pl.BlockSpec((nb, HALO, 2 * S5_N), lambda j: (0, jnp.maximum(tile(j) * (ts // HALO) - 1, 0), 0))
    to_st, to_ch = _full((S5_BANDS, BAND_CH, BAND_ST)), _full((S5_BANDS, BAND_ST, BAND_CH))
    return _pcall(body, "s5_bwd", (nts,),
                  [tok(S5_WIDTH), tok(S5_WIDTH), tok(2 * S5_N), halo, to_st, to_st, to_ch, to_ch,
                   _full((2, S5_N)), _full((1, S5_WIDTH))],
                  [tok(S5_WIDTH), tok(2 * S5_N), _full((2, S5_N)), _full((1, S5_WIDTH))],
                  [_sds((nb, seq, S5_WIDTH), BF16), _sds((nb, seq, 2 * S5_N), BF16), _sds((2, S5_N)),
                   _sds((1, S5_WIDTH))],
                  scratch=[pltpu.VMEM((nb, ts, 2 * S5_N), F32), pltpu.VMEM((nb, 2, S5_N), F32)],
                  )(dy0, za, xs, xs, *c_bands, *b_bands, lam, dskip)


def _hgrn_bwd(zh, do, sts, lb, nb, seq):
    nc = seq // CHUNK

    def body(zh_ref, do_ref, sts_ref, lb_ref, dz_ref, dlb_ref, dst_ref):
        @pl.when(pl.program_id(0) == 0)
        def _():
            dst_ref[...] = jnp.zeros_like(dst_ref)
            dlb_ref[...] = jnp.zeros_like(dlb_ref)

        row = lax.broadcasted_iota(jnp.int32, (CHUNK, CHUNK), 0)
        causal = row >= lax.broadcasted_iota(jnp.int32, (CHUNK, CHUNK), 1)
        last_row = lax.broadcasted_iota(jnp.int32, (CHUNK, HG_HEAD), 0) == CHUNK - 1
        for b in range(nb):
            for h in range(HG_HEADS):
                hs = slice(h * HG_HEAD, (h + 1) * HG_HEAD)
                zq = zh_ref[b, :, h * HG_HEAD:(h + 1) * HG_HEAD]
                zf = zh_ref[b, :, HG_WIDTH + h * HG_HEAD:HG_WIDTH + (h + 1) * HG_HEAD]
                zi = zh_ref[b, :, 2 * HG_WIDTH + h * HG_HEAD:2 * HG_WIDTH + (h + 1) * HG_HEAD]
                lbh = lb_ref[:, hs]
                sf, f, sq, qa, bc, bm, bl = _hgrn_gates(zq, zf, lbh)
                k = 1.0 - f
                e_qt = jnp.exp(bc - bm)
                e_kt = jnp.exp(bm - bc)
                e_b = jnp.exp(bc)
                e_kd = jnp.exp(bl - bc)
                e_l = jnp.exp(bl)
                qt, kt, qb, kd = qa * e_qt, k * e_kt, qa * e_b, k * e_kd
                a = jnp.where(causal, _dot_nt(qt, kt), 0.0)
                st = sts_ref[b, 0, h]
                dst = dst_ref[b, h]
                dov = do_ref[b, :, hs]
                da = jnp.where(causal, _dot_nt(dov, zi), 0.0)
                dqt = _hdot(da, kt)
                dkt = _hdot_tn(da, qt)
                dqb = _dot(dov, st)
                di = _dot_tn(a, dov) + _dot_nt(kd, dst)
                dkd = _dot(zi, dst)
                de_l = jnp.sum(dst * st, axis=0, keepdims=True)
                dst_ref[b, h] = dst * e_l + _dot_tn(dov, qb)
                dqa = dqt * e_qt + dqb * e_b
                dk = dkt * e_kt + dkd * e_kd
                dbl = jnp.sum(dkd * kd, axis=0, keepdims=True) + de_l * e_l
                db = dqt * qt - dkt * kt + dqb * qb - dkd * kd + jnp.where(last_row, dbl, 0.0)
                df = _cumsum_rows(db, reverse=True) / f - dk
                dzq = dqa * QSCALE * (sq * (1.0 + zq * (1.0 - sq)))
                dzf = df * (1.0 - lbh) * sf * (1.0 - sf)
                dz_ref[b, :, h * HG_HEAD:(h + 1) * HG_HEAD] = dzq.astype(BF16)
                dz_ref[b, :, HG_WIDTH + h * HG_HEAD:HG_WIDTH + (h + 1) * HG_HEAD] = dzf.astype(BF16)
                dz_ref[b, :, 2 * HG_WIDTH + h * HG_HEAD:2 * HG_WIDTH + (h + 1) * HG_HEAD] = di.astype(BF16)
                dlb_ref[:, hs] += jnp.sum(df * (1.0 - sf), axis=0, keepdims=True)

    rev = lambda c: nc - 1 - c
    return _pcall(body, "hgrn_bwd", (nc,),
                  [pl.BlockSpec((nb, CHUNK, 4 * HG_WIDTH), lambda c: (0, rev(c), 0)),
                   pl.BlockSpec((nb, CHUNK, HG_WIDTH), lambda c: (0, rev(c), 0)),
                   pl.BlockSpec((nb, 1, HG_HEADS, HG_HEAD, HG_HEAD), lambda c: (0, rev(c), 0, 0, 0)),
                   _full((1, HG_WIDTH))],
                  [pl.BlockSpec((nb, CHUNK, 3 * HG_WIDTH), lambda c: (0, rev(c), 0)), _full((1, HG_WIDTH))],
                  [_sds((nb, seq, 3 * HG_WIDTH), BF16), _sds((1, HG_WIDTH))],
                  scratch=[pltpu.VMEM((nb, HG_HEADS, HG_HEAD, HG_HEAD), F32)])(zh, do, sts, lb)


def _in_proj_bwd(dza, dzh, dzg, dzgt, dx1, x, g_mix, w_in, tm):
    t = x.shape[0]

    def body(dza_ref, dzh_ref, dzg_ref, dzgt_ref, dx1_ref, x_ref, g_ref, w_ref, dz_ref, dx_ref, dg_ref):
        @pl.when(pl.program_id(0) == 0)
        def _():
            dg_ref[...] = jnp.zeros_like(dg_ref)

        c1, c2, c3 = S5_WIDTH, S5_WIDTH + 3 * HG_WIDTH, S5_WIDTH + 4 * HG_WIDTH
        dz_ref[:, 0:c1] = dza_ref[...]
        dz_ref[:, c1:c2] = dzh_ref[...]
        dz_ref[:, c2:c3] = dzg_ref[...]
        dz_ref[:, c3:] = dzgt_ref[...]
        du = _dot(dz_ref[...], w_ref[...])
        xv = x_ref[...]
        r = lax.rsqrt(jnp.mean(xv * xv, axis=-1, keepdims=True) + EPS)
        xn = xv * r
        dg_ref[...] += jnp.sum(du * xn, axis=0, keepdims=True)
        dxn = du * g_ref[...]
        dx_ref[...] = dx1_ref[...] + r * (dxn - xn * jnp.mean(dxn * xn, axis=-1, keepdims=True))

    row = lambda w: pl.BlockSpec((tm, w), lambda i: (i, 0))
    return _pcall(body, "in_proj_bwd", (t // tm,),
                  [row(S5_WIDTH), row(3 * HG_WIDTH), row(HG_WIDTH), row(2 * D_MODEL), row(D_MODEL), row(D_MODEL),
                   _full((1, D_MODEL)), _full((N_IN, D_MODEL))],
                  [row(N_IN), row(D_MODEL), _full((1, D_MODEL))],
                  [_sds((t, N_IN), BF16), _sds((t, D_MODEL)), _sds((1, D_MODEL))],
                  )(dza, dzh, dzg, dzgt, dx1, x, g_mix, w_in)


def _after(value, token):
    return value + token[0, 0]


def _local_step(x3, tgt3, weights, sp, emit, emit_small):
    nb, seq, _ = x3.shape
    t = nb * seq
    tm = _token_tile(seq)
    x = x3.reshape(t, D_MODEL)
    tgt = tgt3.reshape(t, D_MODEL)
    row = lambda v: v.reshape(1, -1)

    a_re, a_im, b_re, b_im = sp["s5_a_re"], sp["s5_a_im"], sp["s5_b_re"], sp["s5_b_im"]
    ldt = sp["s5_log_dt"].reshape(S5_GROUPS, 1)
    lr, li, bb_re, bb_im, lb = _params_fwd(a_re, a_im, ldt, b_re, b_im, sp["hg_lb_logits"])
    lam = jnp.concatenate([lr.reshape(1, S5_N), li.reshape(1, S5_N)], axis=0)
    swap = lambda m: m.transpose(0, 2, 1)
    b_to_st = (_band_blocks(bb_re), _band_blocks(bb_im))
    b_to_ch = (_band_blocks(swap(bb_re)), _band_blocks(swap(bb_im)))
    c_to_ch = (_band_blocks(swap(sp["s5_c_re"])), _band_blocks(swap(-sp["s5_c_im"])))
    c_to_st = (_band_blocks(sp["s5_c_re"]), _band_blocks(-sp["s5_c_im"]))

    g_mix, g_ffn, g_final = row(sp["g_mix"]), row(sp["g_ffn"]), row(sp["g_final"])
    b_glu, gain, dskip, b_conv = row(sp["b_glu"]), row(sp["hg_norm_gain"]), row(sp["s5_d"]), row(sp["b_conv"])

    w_in = weights("in", lam, *b_to_st, *b_to_ch, *c_to_ch, *c_to_st)["w_in"]
    u, za, zh, zgt = _in_proj(x, g_mix, w_in, tm)
    seqs = lambda v: v.reshape(nb, seq, v.shape[-1])
    toks = lambda v: v.reshape(t, v.shape[-1])
    xs3, y0 = _s5_fwd(seqs(za), b_to_st, lam, c_to_ch, dskip, nb, seq, tm)
    xs, y0 = toks(xs3), toks(y0)
    o3, sts = _hgrn_fwd(zh.reshape(nb, seq, 4 * HG_WIDTH), lb, nb, seq)
    o = o3.reshape(t, HG_WIDTH)
    wm = weights("mix", y0, o3)
    weights.forward("ffn", wm["w_out"])
    x1, u2, pa, pb, ya2, yb = _mix_fwd(x, y0, o, zh, zgt, wm["w_glu"], b_glu, gain, wm["w_pa"], wm["w_pb"],
                                       wm["w_out"], g_ffn, tm)
    wf = weights("ffn", u2)
    h = _ffn_up(u2, wf["w_up"], min(4 * tm, t))
    hc, a, dx2, dx2b, loss, dg_final = _ffn_down_loss(h, x1, tgt, wf["w_conv"], b_conv, wf["w_down"], g_final,
                                                      seq, tm)

    wgrad = functools.partial(_wgrad, tn=256, out_dtype=BF16)
    dhc, db_conv = _ffn_bwd_act(dx2b, hc, wf["w_down"], tm)
    sent = emit({"w_down": wgrad(a, dx2b, "dw_down")})
    dh, dx1, dx1b, dg_ffn, dw_conv = _ffn_bwd_up(dhc, h, dx2, x1, wf["w_conv"], wf["w_up"], _after(g_ffn, sent),
                                                 seq, tm)
    sent = emit({"w_up": wgrad(dh, u2, "dw_up"), "w_conv": dw_conv})
    (dy0, do, dzg, dzgt, m, dpa, dpb, ya1, dpre, db_glu, dgain) = _mix_bwd(
        dx1b, y0, o, zh, zgt, pa, pb, wm["w_glu"], _after(b_glu, sent), gain, wm["w_pa"], wm["w_pb"], wm["w_out"], tm)
    sent = emit({"w_out": wgrad(m, dx1b, "dw_out"), "w_pa": wgrad(ya2, dpa, "dw_pa"),
                 "w_pb": wgrad(yb, dpb, "dw_pb"), "w_glu": wgrad(ya1, dpre, "dw_glu")})
    dzh3, dlb = _hgrn_bwd(zh.reshape(nb, seq, 4 * HG_WIDTH), do.reshape(nb, seq, HG_WIDTH), sts, _after(lb, sent),
                          nb, seq)
    dza, a_s5, dlam, dd = _s5_bwd(seqs(dy0), seqs(za), xs3, c_to_st, b_to_ch, lam, dskip, nb, seq, tm)
    dza, a_s5 = toks(dza), toks(a_s5)
    dz, dx, dg_mix = _in_proj_bwd(dza, dzh3.reshape(t, 3 * HG_WIDTH), dzg, dzgt, dx1, x, g_mix, w_in, tm)
    sent = emit({"w_in": wgrad(dz, u, "dw_in")})

    band = HG_HEAD
    dbb_band = _wgrad(a_s5, za, "dbb_s5", 512, band=band, after=sent)
    dc_band = _wgrad(xs, dy0, "dc_s5", 512, band=band, after=sent)
    dbb_re = swap(_diag_blocks(dbb_band[:S5_N], S5_STATE, S5_GROUP))
    dbb_im = swap(_diag_blocks(dbb_band[S5_N:], S5_STATE, S5_GROUP))
    dc_re = swap(_diag_blocks(dc_band[:S5_N], S5_STATE, S5_GROUP))
    dc_im = -swap(_diag_blocks(dc_band[S5_N:], S5_STATE, S5_GROUP))
    da_re, da_im, dldt, db_re, db_im, dlogits = _params_bwd(
        a_re, a_im, ldt, b_re, b_im, sp["hg_lb_logits"],
        dlam[0].reshape(S5_GROUPS, S5_STATE), dlam[1].reshape(S5_GROUPS, S5_STATE), dbb_re, dbb_im, dlb)
    emit_small({"g_mix": dg_mix, "s5_a_re": da_re, "s5_a_im": da_im, "s5_log_dt": dldt.reshape(1, S5_GROUPS),
                "s5_b_re": db_re, "s5_b_im": db_im, "s5_c_re": dc_re, "s5_c_im": dc_im, "s5_d": dd, "b_glu": db_glu,
                "hg_lb_logits": dlogits, "hg_norm_gain": dgain, "g_ffn": dg_ffn, "b_conv": db_conv,
                "g_final": dg_final, "loss": loss})
    return dx.reshape(nb, seq, D_MODEL)


def _mesh_peers():
    x, y, c = lax.axis_index("x"), lax.axis_index("y"), lax.axis_index("c")
    peers = []
    for k in range(1, N_DEV):
        px, py, pc = (1 - x if k & 4 else x), (1 - y if k & 2 else y), (1 - c if k & 1 else c)
        peers.append((k, (px, py, pc), 4 * px + 2 * py + pc))
    return 4 * x + 2 * y + c, peers


_HBM = pl.BlockSpec(memory_space=pltpu.HBM)
_SEM = pl.BlockSpec(memory_space=pltpu.SEMAPHORE)


_EFFECT = pltpu.CompilerParams(has_side_effects=pltpu.SideEffectType.DATAFLOW_SIDE_EFFECTING)


def _remote(src, dst, send_sem, recv_sem, to):
    return pltpu.make_async_remote_copy(src_ref=src, dst_ref=dst, send_sem=send_sem, recv_sem=recv_sem,
                                        device_id=to, device_id_type=pl.DeviceIdType.MESH)


def _exchange_start(name, arrays, after):
    n = len(arrays)
    srcs = [pltpu.with_memory_space_constraint(a, pltpu.HBM) for a in arrays]
    lands = [pltpu.with_memory_space_constraint(lax.empty(a.shape, a.dtype), pltpu.HBM) for a in arrays]
    copies = (N_DEV - 1) * n

    def body(*refs):
        src_refs, land_refs = refs[:n], refs[n:2 * n]
        send_sems, recv_sems, token = refs[2 * n + 1], refs[2 * n + 2], refs[-1]
        my_slab, peers = _mesh_peers()
        for k, peer, slab in peers:
            for i in range(n):
                s = (k - 1) * n + i
                _remote(src_refs[i].at[slab], land_refs[i].at[my_slab], send_sems.at[s], recv_sems.at[s], peer).start()
        token[...] = jnp.zeros_like(token)

    outs = pl.pallas_call(
        body, name=name,
        out_shape=(pltpu.SemaphoreType.DMA((copies,)), pltpu.SemaphoreType.DMA((copies,)),
                   *[pltpu.HBM(a.shape, a.dtype) for a in lands], _sds((SUBLANES, LANES))),
        in_specs=[_HBM] * (2 * n) + [pl.BlockSpec(memory_space=pl.ANY)],
        out_specs=(_SEM, _SEM, *[_HBM] * n, pl.BlockSpec(memory_space=pltpu.VMEM)),
        input_output_aliases={n + i: 2 + i for i in range(n)}, compiler_params=_EFFECT,
    )(*srcs, *lands, after)
    return (outs[0], outs[1], srcs, outs[2:2 + n]), outs[-1]


def _exchange_wait(name, state, *after):
    send_sems, recv_sems, srcs, lands = state
    n = len(lands)

    def body(*refs):
        src_refs, land_refs = refs[:n], refs[n:2 * n]
        send_ref, recv_ref = refs[2 * n], refs[2 * n + 1]
        _, peers = _mesh_peers()
        for k, peer, slab in peers:
            for i in range(n):
                s = (k - 1) * n + i
                copy = _remote(src_refs[i].at[slab], land_refs[i].at[slab], send_ref.at[s], recv_ref.at[s], peer)
                copy.wait_send()
                copy.wait_recv()

    outs = pl.pallas_call(
        body, name=name,
        out_shape=tuple(pltpu.HBM(a.shape, a.dtype) for a in lands),
        in_specs=[_HBM] * (2 * n) + [_SEM, _SEM] + [pl.BlockSpec(memory_space=pl.ANY)] * len(after),
        out_specs=tuple([_HBM] * n),
        input_output_aliases={n + i: i for i in range(n)}, compiler_params=_EFFECT,
    )(*srcs, *lands, send_sems, recv_sems, *after)
    return list(outs), list(srcs)


def _slab(pos):
    return 4 * pos[0] + 2 * pos[1] + pos[2]


def _chip_routes():
    x, y, c = lax.axis_index("x"), lax.axis_index("y"), lax.axis_index("c")
    return (x, y, c), (x, y, 1 - c), [(1 - x, y, c), (x, 1 - y, c), (1 - x, 1 - y, c)]


def _gather_start(name, arrays, after):
    n = len(arrays)
    me = 4 * lax.axis_index("x") + 2 * lax.axis_index("y") + lax.axis_index("c")
    srcs = [pltpu.with_memory_space_constraint(a, pltpu.HBM) for a in arrays]
    lands = [pltpu.with_memory_space_constraint(
        lax.dynamic_update_slice_in_dim(lax.empty((N_DEV,) + a.shape, a.dtype), a[None], me, 0), pltpu.HBM)
        for a in arrays]

    def body(*refs):
        src_refs, land_refs = refs[:n], refs[n:2 * n]
        send_sems, recv_sems, token = refs[2 * n + 1], refs[2 * n + 2], refs[-1]
        mine, sibling, chips = _chip_routes()
        for k, to in enumerate([sibling] + chips):
            for i in range(n):
                _remote(src_refs[i], land_refs[i].at[_slab(mine)], send_sems.at[k * n + i], recv_sems.at[k * n + i],
                        to).start()
        token[...] = jnp.zeros_like(token)

    outs = pl.pallas_call(
        body, name=name,
        out_shape=(pltpu.SemaphoreType.DMA((4 * n,)), pltpu.SemaphoreType.DMA((4 * n,)),
                   *[pltpu.HBM(a.shape, a.dtype) for a in lands], _sds((SUBLANES, LANES))),
        in_specs=[_HBM] * (2 * n) + [pl.BlockSpec(memory_space=pl.ANY)],
        out_specs=(_SEM, _SEM, *[_HBM] * n, pl.BlockSpec(memory_space=pltpu.VMEM)),
        input_output_aliases={n + i: 2 + i for i in range(n)}, compiler_params=_EFFECT,
    )(*srcs, *lands, after)
    return (outs[0], outs[1], srcs, outs[2:2 + n]), outs[-1]


def _gather_forward(name, state, *after):
    send_a, recv_a, srcs, lands = state
    n = len(lands)

    def body(*refs):
        land_refs, recv_a_ref = refs[:n], refs[n]
        send_b, recv_b = refs[n + 1 + len(after)], refs[n + 2 + len(after)]
        mine, sibling, chips = _chip_routes()
        for j, chip in enumerate(chips):
            for i in range(n):
                block = land_refs[i].at[_slab(chip)]
                _remote(block, block, send_b.at[j * n + i], recv_a_ref.at[(1 + j) * n + i], chip).wait_recv()
                _remote(block, block, send_b.at[j * n + i], recv_b.at[j * n + i], sibling).start()

    outs = pl.pallas_call(
        body, name=name,
        out_shape=(pltpu.SemaphoreType.DMA((3 * n,)), pltpu.SemaphoreType.DMA((3 * n,)),
                   *[pltpu.HBM(a.shape, a.dtype) for a in lands]),
        in_specs=[_HBM] * n + [_SEM] + [pl.BlockSpec(memory_space=pl.ANY)] * len(after),
        out_specs=(_SEM, _SEM, *[_HBM] * n),
        input_output_aliases={i: 2 + i for i in range(n)}, compiler_params=_EFFECT,
    )(*lands, recv_a, *after)
    return (send_a, recv_a, srcs, list(outs[2:])), (outs[0], outs[1])


def _gather_wait(name, state, forwarded, *after):
    send_a, recv_a, srcs, lands = state
    send_b, recv_b = forwarded
    n = len(lands)

    def body(*refs):
        src_refs, land_refs = refs[:n], refs[n:2 * n]
        sa, ra, sb, rb = refs[2 * n:2 * n + 4]
        mine, sibling, chips = _chip_routes()
        for i in range(n):
            for k, to in enumerate([sibling] + chips):
                _remote(src_refs[i], land_refs[i].at[_slab(mine)], sa.at[k * n + i], ra.at[k * n + i], to).wait_send()
            theirs = land_refs[i].at[_slab(sibling)]
            _remote(theirs, theirs, sa.at[i], ra.at[i], sibling).wait_recv()
            for j, chip in enumerate(chips):
                sent = land_refs[i].at[_slab(chip)]
                got = land_refs[i].at[_slab((chip[0], chip[1], sibling[2]))]
                _remote(sent, sent, sb.at[j * n + i], rb.at[j * n + i], sibling).wait_send()
                _remote(got, got, sb.at[j * n + i], rb.at[j * n + i], sibling).wait_recv()

    outs = pl.pallas_call(
        body, name=name,
        out_shape=tuple(pltpu.HBM(a.shape, a.dtype) for a in lands),
        in_specs=[_HBM] * (2 * n) + [_SEM] * 4 + [pl.BlockSpec(memory_space=pl.ANY)] * len(after),
        out_specs=tuple([_HBM] * n),
        input_output_aliases={n + i: i for i in range(n)}, compiler_params=_EFFECT,
    )(*srcs, *lands, send_a, recv_a, send_b, recv_b, *after)
    return list(outs), list(srcs)


def _join_cols(parts, name, tr):
    _, r, c = parts.shape

    def body(p_ref, o_ref):
        for j in range(N_DEV):
            o_ref[:, j * c:(j + 1) * c] = p_ref[j]

    return _pcall(body, name, (r // tr,), [pl.BlockSpec((N_DEV, tr, c), lambda i: (0, i, 0))],
                  pl.BlockSpec((tr, N_DEV * c), lambda i: (i, 0)), _sds((r, N_DEV * c), parts.dtype))(parts)


def _split_cols(full, name, tr):
    r, c = full.shape[0], full.shape[1] // N_DEV

    def body(f_ref, o_ref):
        for j in range(N_DEV):
            o_ref[j] = f_ref[:, j * c:(j + 1) * c]

    return _pcall(body, name, (r // tr,), [pl.BlockSpec((tr, N_DEV * c), lambda i: (i, 0))],
                  pl.BlockSpec((N_DEV, tr, c), lambda i: (0, i, 0)), _sds((N_DEV, r, c), full.dtype))(full)


def _my_slab():
    return (4 * lax.axis_index("x") + 2 * lax.axis_index("y") + lax.axis_index("c")).astype(jnp.int32).reshape(1)


def _adamw(parts, sent, w, m, v, name, tile):
    _, rows, cols = w.shape

    def body(me_ref, p_ref, s_ref, w_ref, m_ref, v_ref, g_out, d_out, m_out, v_out):
        me = me_ref[0]
        g = jnp.where(me == 0, s_ref[0], p_ref[0]).astype(F32)
        for k in range(1, N_DEV):
            g = g + jnp.where(me == k, s_ref[0], p_ref[k]).astype(F32)
        m1 = ADAM_B1 * m_ref[0] + (1.0 - ADAM_B1) * g
        v1 = ADAM_B2 * v_ref[0] + (1.0 - ADAM_B2) * (g * g)
        m_hat = m1 / (1.0 - ADAM_B1 ** ADAM_STEP)
        v_hat = v1 / (1.0 - ADAM_B2 ** ADAM_STEP)
        g_out[0] = g
        d_out[0] = -ADAM_LR * (m_hat / (jnp.sqrt(v_hat) + ADAM_EPS) + ADAM_WD * w_ref[0])
        m_out[0] = m1
        v_out[0] = v1

    row = pl.BlockSpec((1, tile, cols), lambda i, me: (0, i, 0))
    return pl.pallas_call(
        body, name=name, out_shape=[_sds((1, rows, cols))] * 4,
        grid_spec=pltpu.PrefetchScalarGridSpec(
            num_scalar_prefetch=1, grid=(rows // tile,),
            in_specs=[pl.BlockSpec((N_DEV, tile, cols), lambda i, me: (0, i, 0)),
                      pl.BlockSpec((1, tile, cols), lambda i, me: (me[0], i, 0)), row, row, row],
            out_specs=[row, row, row, row]),
        compiler_params=pltpu.CompilerParams(dimension_semantics=("arbitrary",), vmem_limit_bytes=VMEM_LIMIT),
    )(_my_slab(), parts, sent, w, m, v)


BIG = {
    "w_in": ((N_IN // N_DEV, D_MODEL), False, N_IN // N_DEV // 3),
    "w_glu": ((S5_WIDTH // N_DEV, S5_WIDTH), False, S5_WIDTH // N_DEV),
    "w_pa": ((S5_WIDTH, D_MODEL // N_DEV), True, S5_WIDTH),
    "w_pb": ((HG_WIDTH, D_MODEL // N_DEV), True, HG_WIDTH),
    "w_out": ((D_MODEL // N_DEV, D_MODEL), False, D_MODEL // N_DEV),
    "w_up": ((2 * D_FF // N_DEV, D_MODEL), False, 2 * D_FF // N_DEV // 4),
    "w_conv": ((CONV_W, 2 * D_FF // N_DEV), True, CONV_W),
    "w_down": ((D_FF // N_DEV, D_MODEL), False, D_FF // N_DEV // 2),
}
TRANSPOSED = ("w_in", "w_up", "s5_b_re", "s5_b_im")
UNALIGNED_COLS = ("w_conv",)


def _stored(n, arr):
    return jnp.swapaxes(arr, -1, -2) if n in TRANSPOSED else arr


def _join_shards(n, parts):
    (a, b), by_cols, _ = BIG[n]
    if not by_cols:
        return parts.reshape(N_DEV * a, b)
    if n in UNALIGNED_COLS:
        return _join_cols(parts, "join_" + n, min(a, 256))
    return parts.transpose(1, 0, 2).reshape(a, N_DEV * b)


def _split_shards(n, full):
    (a, b), by_cols, _ = BIG[n]
    if not by_cols:
        return full.reshape(N_DEV, a, b)
    if n in UNALIGNED_COLS:
        return _split_cols(full, "split_" + n, min(a, 256))
    return full.reshape(a, N_DEV, b).transpose(1, 0, 2)


SMALL_CORE = {
    "s5_b_re": GSC, "s5_b_im": GSC, "s5_c_re": GSC, "s5_c_im": GSC,
    "g_mix": (1, D_MODEL), "g_ffn": (1, D_MODEL), "g_final": (1, D_MODEL), "s5_d": (1, S5_WIDTH),
    "b_glu": (1, S5_WIDTH), "hg_norm_gain": (1, HG_WIDTH), "hg_lb_logits": (2, HG_WIDTH), "b_conv": (1, 2 * D_FF),
    "s5_log_dt": (1, S5_GROUPS), "s5_a_re": (S5_GROUPS, S5_STATE), "s5_a_im": (S5_GROUPS, S5_STATE), "loss": (1, 1),
}
BLOCK_ROWS = 32


def _small_rows():
    rows, r = {}, 0
    for n, core in SMALL_CORE.items():
        rows[n] = r
        r += BLOCK_ROWS if len(core) == 3 else -(-math.prod(core) // PACK_W)
    return rows, -(-r // SUBLANES) * SUBLANES


SMALL_ROW, SMALL_ROWS = _small_rows()


def _small_pieces(name):
    r, core = SMALL_ROW[name], SMALL_CORE[name]
    if len(core) == 3:
        return [((g, slice(None), slice(None)), slice(r + S5_GROUP * (g % 2), r + S5_GROUP * (g % 2 + 1)),
                 slice(S5_STATE * (g // 2), S5_STATE * (g // 2 + 1))) for g in range(S5_GROUPS)]
    pieces = []
    for i in range(core[0]):
        for c0 in range(0, core[1], PACK_W):
            w, flat = min(PACK_W, core[1] - c0), i * core[1] + c0
            pieces.append(((slice(i, i + 1), slice(c0, c0 + w)), slice(r + flat // PACK_W, r + flat // PACK_W + 1),
                           slice(flat % PACK_W, flat % PACK_W + w)))
    return pieces


def _core_index(ref, name, idx):
    return (0,) * (len(ref.shape) - len(SMALL_CORE[name])) + idx


def _pack_small_grads(grads):
    names = list(SMALL_CORE)

    def body(*refs):
        pack = refs[-1]
        pack[...] = jnp.zeros_like(pack)
        for ref, n in zip(refs, names):
            for idx, rows, lanes in _small_pieces(n):
                pack[rows, lanes] = ref[_core_index(ref, n, idx)]

    return _pcall(body, "pack_small_grads", (1,), [_full(grads[n].shape) for n in names],
                  _full((SMALL_ROWS, PACK_W)), _sds((SMALL_ROWS, PACK_W)))(*[grads[n] for n in names])


def _adamw_small(parts, sent, names, rows, given, name):
    lo, hi = rows
    k = len(names)
    shapes = [given[n].shape for n in names]

    def body(*refs):
        me, p_ref, s_ref, ins, outs = refs[0][0], refs[1], refs[2], refs[3:3 + 3 * k], refs[3 + 3 * k:3 + 7 * k]
        packs, results = refs[3 + 7 * k:6 + 7 * k], refs[6 + 7 * k:]
        for j, pack in enumerate(packs):
            pack[...] = jnp.zeros_like(pack)
            for ref, n in zip(ins[j * k:(j + 1) * k], names):
                for idx, prow, lanes in _small_pieces(n):
                    pack[slice(prow.start - lo, prow.stop - lo), lanes] = ref[_core_index(ref, n, idx)]
        mine = s_ref[lo:hi, :]
        g = jnp.where(me == 0, mine, p_ref[0, lo:hi, :])
        for d in range(1, N_DEV):
            g = g + jnp.where(me == d, mine, p_ref[d, lo:hi, :])
        m1 = ADAM_B1 * packs[1][...] + (1.0 - ADAM_B1) * g
        v1 = ADAM_B2 * packs[2][...] + (1.0 - ADAM_B2) * (g * g)
        m_hat = m1 / (1.0 - ADAM_B1 ** ADAM_STEP)
        v_hat = v1 / (1.0 - ADAM_B2 ** ADAM_STEP)
        results[0][...] = g
        results[1][...] = -ADAM_LR * (m_hat / (jnp.sqrt(v_hat) + ADAM_EPS) + ADAM_WD * packs[0][...])
        results[2][...] = m1
        results[3][...] = v1
        for j, result in enumerate(results):
            for ref, n in zip(outs[j * k:(j + 1) * k], names):
                for idx, prow, lanes in _small_pieces(n):
                    ref[_core_index(ref, n, idx)] = result[slice(prow.start - lo, prow.stop - lo), lanes]

    flat = _pcall(body, name, (1,),
                  [pl.BlockSpec(memory_space=pltpu.SMEM), _full(parts.shape), _full(sent.shape)]
                  + [_full(s) for s in shapes] * 3,
                  [_full(s) for s in shapes] * 4, [_sds(s) for s in shapes] * 4,
                  scratch=[pltpu.VMEM((hi - lo, PACK_W), F32)] * 7,
                  )(_my_slab(), parts, sent, *[given[pre + n] for pre in ("", "m_", "v_") for n in names])
    return {n: [flat[j * k + i] for j in range(4)] for i, n in enumerate(names)}


def kernel(x, g_mix, w_in, s5_a_re, s5_a_im, s5_log_dt, s5_b_re, s5_b_im, s5_c_re, s5_c_im, s5_d, w_glu, b_glu, hg_lb_logits, hg_norm_gain, w_pa, w_pb, w_out, g_ffn, w_up, w_conv, b_conv, w_down, g_final, loss_target, m_g_mix, m_w_in, m_s5_a_re, m_s5_a_im, m_s5_log_dt, m_s5_b_re, m_s5_b_im, m_s5_c_re, m_s5_c_im, m_s5_d, m_w_glu, m_b_glu, m_hg_lb_logits, m_hg_norm_gain, m_w_pa, m_w_pb, m_w_out, m_g_ffn, m_w_up, m_w_conv, m_b_conv, m_w_down, m_g_final, v_g_mix, v_w_in, v_s5_a_re, v_s5_a_im, v_s5_log_dt, v_s5_b_re, v_s5_b_im, v_s5_c_re, v_s5_c_im, v_s5_d, v_w_glu, v_b_glu, v_hg_lb_logits, v_hg_norm_gain, v_w_pa, v_w_pb, v_w_out, v_g_ffn, v_w_up, v_w_conv, v_b_conv, v_w_down, v_g_final):
    given = dict(locals())
    small_names = [n for n in SMALL_CORE if n != "loss"]

    pay = {n: given[n][0] if n == "w_conv" else _stored(n, given[n])[0].astype(BF16) for n in BIG}
    groups = {"in": ["w_in"], "mix": ["w_glu", "w_pa", "w_pb", "w_out"], "ffn": ["w_up", "w_down", "w_conv"]}
    gathers, tokens, forwards = {}, {}, {}

    def start(grp, after):
        gathers[grp], tokens[grp] = _gather_start("gather_" + grp + "_start", [pay[n] for n in groups[grp]], after)

    start("in", pay["w_in"])

    def forward(grp, *after):
        forwards[grp] = _gather_forward("gather_" + grp + "_forward", gathers[grp], *after)
        if grp == "in":
            start("mix", forwards["in"][0][3][0])
            start("ffn", tokens["mix"])

    def weights(grp, *after):
        if grp not in forwards:
            forward(grp, *after)
        if grp == "in":
            after = (*after, tokens["mix"], tokens["ffn"])
        got, _ = _gather_wait("gather_" + grp + "_wait", *forwards[grp], *after)
        return {n: _join_shards(n, g) for n, g in zip(groups[grp], got)}

    weights.forward = forward

    in_flight, started = [], []

    def emit(grads):
        names = list(grads)
        state, token = _exchange_start("grads_" + names[0] + "_start", [_split_shards(n, grads[n]) for n in names],
                                       grads[names[0]])
        in_flight.append((names, state))
        return token

    def emit_small(grads):
        pack = _pack_small_grads(grads)
        state, token = _gather_start("grads_small_start", [pack], pack)
        in_flight.append((["small"], state))
        started.append(token)

    sp = {n: (given[n] if n in ("g_final", "hg_lb_logits") else _stored(n, given[n])[0]) for n in small_names}
    dx = _local_step(x, loss_target, weights, sp, emit, emit_small)

    res = {}
    after = [started[-1]]
    for names, state in in_flight:
        if names == ["small"]:
            state, forwarded = _gather_forward("grads_small_forward", state, *after)
            parts, sent = _gather_wait("grads_small_wait", state, forwarded)
        else:
            parts, sent = _exchange_wait("grads_" + names[0] + "_wait", state, *after)
        if names != ["small"]:
            after = []
            for n, part, mine in zip(names, parts, sent):
                raw = _adamw(part, mine, *[_stored(n, given[pre + n]) for pre in ("", "m_", "v_")], "adamw_" + n,
                             BIG[n][2])
                res[n] = [_stored(n, r) for r in raw]
                after.append(raw[0])
            continue
        sgiven = {pre + n: _stored(n, given[pre + n]) for pre in ("", "m_", "v_") for n in small_names}
        for pre in ("", "m_", "v_"):
            sgiven[pre + "g_final"] = given[pre + "g_final"].reshape(1, D_MODEL)
            sgiven[pre + "loss"] = jnp.zeros((1, 1), F32)
        raw = _adamw_small(parts[0], sent[0], list(SMALL_CORE), (0, SMALL_ROWS), sgiven, "adamw_small")
        res.update({n: [_stored(n, r) for r in raw[n]] for n in small_names})
        res["g_final"] = [r.reshape(D_MODEL) for r in raw["g_final"]]
        total_loss = raw["loss"][0].reshape(())
        after = [raw["s5_b_re"][0], raw["g_mix"][0]]
    return (total_loss, dx, *[res[n][0] for n in WEIGHT_ORDER], *[res[n][1] for n in WEIGHT_ORDER],
            *[res[n][2] for n in WEIGHT_ORDER], *[res[n][3] for n in WEIGHT_ORDER])
```

```python
import functools
import math

import jax
import jax.numpy as jnp
from jax import lax
from jax.experimental import pallas as pl
from jax.experimental.pallas import tpu as pltpu

F32 = jnp.float32
BF16 = jnp.bfloat16

D_MODEL = 1024
S5_WIDTH = 512
S5_GROUP = 16
S5_GROUPS = 32
S5_STATE = 64
S5_N = S5_GROUPS * S5_STATE
HG_WIDTH = 512
HG_HEAD = 128
HG_HEADS = 4
D_FF = 2816
CONV_W = 3
CHUNK = 64
N_IN = S5_WIDTH + 4 * HG_WIDTH + 2 * D_MODEL
EPS = 1e-6
QSCALE = HG_HEAD ** -0.5

ADAM_LR = 0.001
ADAM_B1 = 0.9
ADAM_B2 = 0.999
ADAM_EPS = 1e-08
ADAM_WD = 0.01
ADAM_STEP = 10

N_DEV = 8
V7X_VMEM_BYTES = 64 * 1024 * 1024
VMEM_LIMIT = V7X_VMEM_BYTES * 7 // 8
SUBLANES = 8
LANES = 128
PACK_W = 1024

WEIGHT_ORDER = ("g_mix", "w_in", "s5_a_re", "s5_a_im", "s5_log_dt", "s5_b_re", "s5_b_im", "s5_c_re", "s5_c_im",
                "s5_d", "w_glu", "b_glu", "hg_lb_logits", "hg_norm_gain", "w_pa", "w_pb", "w_out", "g_ffn",
                "w_up", "w_conv", "b_conv", "w_down", "g_final")


def _pcall(body, name, grid, in_specs, out_specs, out_shape, scratch=()):
    return pl.pallas_call(
        body, name=name, grid=grid, in_specs=in_specs, out_specs=out_specs, out_shape=out_shape,
        scratch_shapes=list(scratch),
        compiler_params=pltpu.CompilerParams(dimension_semantics=("arbitrary",) * len(grid),
                                             vmem_limit_bytes=VMEM_LIMIT),
    )


def _full(shape):
    return pl.BlockSpec(shape, lambda *_: (0,) * len(shape))


def _sds(shape, dtype=F32):
    return jax.ShapeDtypeStruct(shape, dtype)


def _dot(a, b):
    return jnp.dot(a.astype(BF16), b.astype(BF16), preferred_element_type=F32)


def _dot_nt(a, b):
    return lax.dot_general(a.astype(BF16), b.astype(BF16), (((1,), (1,)), ((), ())), preferred_element_type=F32)


def _dot_tn(a, b):
    return lax.dot_general(a.astype(BF16), b.astype(BF16), (((0,), (0,)), ((), ())), preferred_element_type=F32)


def _split(a):
    hi = a.astype(BF16)
    return hi, (a - hi.astype(F32)).astype(BF16)


def _hdot(a, b, dims=(((1,), (0,)), ((), ()))):
    (ah, al), (bh, bl) = _split(a), _split(b)
    dot = lambda p, q: lax.dot_general(p, q, dims, preferred_element_type=F32)
    return dot(ah, bh) + (dot(al, bh) + dot(ah, bl))


def _hdot_tn(a, b):
    return _hdot(a, b, (((0,), (0,)), ((), ())))


def _sigmoid(x):
    return jax.nn.sigmoid(x)


GELU_C = math.sqrt(2.0 / math.pi)
GELU_A = 0.044715


def _gelu(x):
    return 0.5 * x * (1.0 + jnp.tanh(GELU_C * (x + GELU_A * (x * x * x))))


def _gelu_grad(x):
    t = jnp.tanh(GELU_C * (x + GELU_A * (x * x * x)))
    return 0.5 * (1.0 + t) + 0.5 * x * (1.0 - t * t) * (GELU_C * (1.0 + 3.0 * GELU_A * x * x))


def _cumsum_rows(v, reverse=False):
    n = v.shape[0]
    row = lax.broadcasted_iota(jnp.int32, v.shape, 0)
    s = 1
    while s < n:
        if reverse:
            v = v + jnp.where(row < n - s, pltpu.roll(v, n - s, axis=0), 0.0)
        else:
            v = v + jnp.where(row >= s, pltpu.roll(v, s, axis=0), 0.0)
        s *= 2
    return v


def _token_tile(seq):
    return min(256, seq)


def _s5_coeffs(a_re, a_im, ldt):
    dt = jnp.exp(ldt)
    mag = jnp.exp(a_re * dt)
    ang = a_im * dt
    lb_re = mag * jnp.cos(ang)
    lb_im = mag * jnp.sin(ang)
    den = a_re * a_re + a_im * a_im
    n_re = lb_re - 1.0
    n_im = lb_im
    co_re = (n_re * a_re + n_im * a_im) / den
    co_im = (n_im * a_re - n_re * a_im) / den
    return lb_re, lb_im, co_re, co_im


GS, GSC = (S5_GROUPS, S5_STATE), (S5_GROUPS, S5_GROUP, S5_STATE)


def _params_fwd(a_re, a_im, ldt, bt_re, bt_im, logits):
    def body(are, aim, ld, bre, bim, lg, lr_o, li_o, bbr_o, bbi_o, lb_o):
        lr, li, co_re, co_im = _s5_coeffs(are[...], aim[...], ld[...])
        lr_o[...] = lr
        li_o[...] = li
        for g in range(S5_GROUPS):
            cr, ci = co_re[g:g + 1, :], co_im[g:g + 1, :]
            bbr_o[g] = cr * bre[g] - ci * bim[g]
            bbi_o[g] = cr * bim[g] + ci * bre[g]
        lb_o[...] = _sigmoid(lg[0:1, :] - lg[1:2, :])

    return _pcall(body, "params_fwd", (1,),
                  [_full(GS), _full(GS), _full((S5_GROUPS, 1)), _full(GSC), _full(GSC), _full((2, HG_WIDTH))],
                  [_full(GS), _full(GS), _full(GSC), _full(GSC), _full((1, HG_WIDTH))],
                  [_sds(GS), _sds(GS), _sds(GSC), _sds(GSC), _sds((1, HG_WIDTH))],
                  )(a_re, a_im, ldt, bt_re, bt_im, logits)


def _params_bwd(a_re, a_im, ldt, bt_re, bt_im, logits, dlr, dli, dbbr, dbbi, dlb):
    def body(are, aim, ld, bre, bim, lg, dlr_r, dli_r, dbbr_r, dbbi_r, dlb_r,
             dare_o, daim_o, dld_o, dbre_o, dbim_o, dlg_o, dcr_ref, dci_ref):
        (_, _, co_re, co_im), vjp = jax.vjp(_s5_coeffs, are[...], aim[...], ld[...])
        for g in range(S5_GROUPS):
            cr, ci = co_re[g:g + 1, :], co_im[g:g + 1, :]
            gr, gi, br, bi = dbbr_r[g], dbbi_r[g], bre[g], bim[g]
            dbre_o[g] = cr * gr + ci * gi
            dbim_o[g] = cr * gi - ci * gr
            dcr_ref[g:g + 1, :] = jnp.sum(gr * br + gi * bi, axis=0, keepdims=True)
            dci_ref[g:g + 1, :] = jnp.sum(gi * br - gr * bi, axis=0, keepdims=True)
        dare, daim, dld = vjp((dlr_r[...], dli_r[...], dcr_ref[...], dci_ref[...]))
        dare_o[...] = dare
        daim_o[...] = daim
        dld_o[...] = dld
        lb = _sigmoid(lg[0:1, :] - lg[1:2, :])
        d0 = dlb_r[...] * lb * (1.0 - lb)
        dlg_o[0:1, :] = d0
        dlg_o[1:2, :] = -d0

    return _pcall(body, "params_bwd", (1,),
                  [_full(GS), _full(GS), _full((S5_GROUPS, 1)), _full(GSC), _full(GSC), _full((2, HG_WIDTH)),
                   _full(GS), _full(GS), _full(GSC), _full(GSC), _full((1, HG_WIDTH))],
                  [_full(GS), _full(GS), _full((S5_GROUPS, 1)), _full(GSC), _full(GSC), _full((2, HG_WIDTH))],
                  [_sds(GS), _sds(GS), _sds((S5_GROUPS, 1)), _sds(GSC), _sds(GSC), _sds((2, HG_WIDTH))],
                  scratch=[pltpu.VMEM(GS, F32), pltpu.VMEM(GS, F32)],
                  )(a_re, a_im, ldt, bt_re, bt_im, logits, dlr, dli, dbbr, dbbi, dlb)


def _band_blocks(m):
    g, r, c = m.shape
    gb = g // S5_BANDS
    m4 = m.astype(BF16).reshape(S5_BANDS, gb, r, c)
    on_diag = jnp.eye(gb, dtype=bool)[None, :, None, :, None]
    return jnp.where(on_diag, m4[:, :, :, None, :], 0).reshape(S5_BANDS, gb * r, gb * c)


def _diag_blocks(band, r, c):
    g, nb = band.shape[0] // r, band.shape[1] // c
    on_diag = (jnp.arange(g) % nb)[:, None, None, None] == jnp.arange(nb)[None, None, :, None]
    return jnp.sum(jnp.where(on_diag, band.reshape(g, r, nb, c), 0.0), axis=2)


def _in_proj(x, g_mix, w_in, tm):
    t = x.shape[0]

    def body(x_ref, g_ref, w_ref, u_ref, za_ref, zh_ref, zg_ref):
        xv = x_ref[...]
        r = lax.rsqrt(jnp.mean(xv * xv, axis=-1, keepdims=True) + EPS)
        u = (xv * r * g_ref[...]).astype(BF16)
        u_ref[...] = u
        za_ref[...] = _dot_nt(u, w_ref[0:S5_WIDTH, :])
        zh_ref[...] = _dot_nt(u, w_ref[S5_WIDTH:S5_WIDTH + 4 * HG_WIDTH, :])
        zg_ref[...] = _dot_nt(u, w_ref[S5_WIDTH + 4 * HG_WIDTH:, :]).astype(BF16)

    row = lambda w: pl.BlockSpec((tm, w), lambda i: (i, 0))
    return _pcall(body, "in_proj", (t // tm,),
                  [row(D_MODEL), _full((1, D_MODEL)), _full((N_IN, D_MODEL))],
                  [row(D_MODEL), row(S5_WIDTH), row(4 * HG_WIDTH), row(2 * D_MODEL)],
                  [_sds((t, D_MODEL), BF16), _sds((t, S5_WIDTH)), _sds((t, 4 * HG_WIDTH)),
                   _sds((t, 2 * D_MODEL), BF16)],
                  )(x, g_mix, w_in)


S5_LANES = 512
S5_BANDS = 4


def _band(q):
    return (slice(q * S5_WIDTH // S5_BANDS, (q + 1) * S5_WIDTH // S5_BANDS),
            slice(q * S5_N // S5_BANDS, (q + 1) * S5_N // S5_BANDS))


def _im(st):
    return slice(S5_N + st.start, S5_N + st.stop)


SCAN_UNROLL = 8


def _complex_scan(buf_ref, lam_ref, st_ref, nb, ts, reverse):
    lanes = [slice(cc * S5_LANES, (cc + 1) * S5_LANES) for cc in range(S5_N // S5_LANES)]
    chains = [(b, re) for b in range(nb) for re in lanes]
    nch = len(chains)
    wr = {re.start: lam_ref[0:1, re] for re in lanes}
    wi = {re.start: -lam_ref[1:2, re] if reverse else lam_ref[1:2, re] for re in lanes}

    def block(ib, carry):
        vr, vi = list(carry[:nch]), list(carry[nch:])
        first = ts - SCAN_UNROLL - ib * SCAN_UNROLL if reverse else ib * SCAN_UNROLL
        first = pl.multiple_of(first, SCAN_UNROLL)
        for k in range(SCAN_UNROLL):
            row = pl.ds(first + (SCAN_UNROLL - 1 - k if reverse else k), 1)
            for c, (b, re) in enumerate(chains):
                nr = wr[re.start] * vr[c] - wi[re.start] * vi[c] + buf_ref[b, row, re]
                ni = wr[re.start] * vi[c] + wi[re.start] * vr[c] + buf_ref[b, row, _im(re)]
                buf_ref[b, row, re] = nr
                buf_ref[b, row, _im(re)] = ni
                vr[c], vi[c] = nr, ni
        return tuple(vr + vi)

    init = tuple(st_ref[b, 0:1, re] for b, re in chains) + tuple(st_ref[b, 1:2, re] for b, re in chains)
    last = lax.fori_loop(0, ts // SCAN_UNROLL, block, init)
    for c, (b, re) in enumerate(chains):
        st_ref[b, 0:1, re] = last[c]
        st_ref[b, 1:2, re] = last[nch + c]


BAND_CH = S5_WIDTH // S5_BANDS
BAND_ST = S5_N // S5_BANDS


def _s5_fwd(za, b_bands, lam, c_bands, dskip, nb, seq, ts):
    nts = seq // ts

    def body(za_ref, br_ref, bi_ref, lam_ref, cr_ref, ci_ref, d_ref, xs_ref, y_ref, buf_ref, st_ref):
        @pl.when(pl.program_id(0) == 0)
        def _():
            st_ref[...] = jnp.zeros_like(st_ref)

        for b in range(nb):
            zav = za_ref[b]
            for q in range(S5_BANDS):
                ch, st = _band(q)
                buf_ref[b, :, st] = _dot(zav[:, ch], br_ref[q])
                buf_ref[b, :, _im(st)] = _dot(zav[:, ch], bi_ref[q])
        _complex_scan(buf_ref, lam_ref, st_ref, nb, ts, reverse=False)
        for b in range(nb):
            zav = za_ref[b]
            xs_ref[b] = buf_ref[b].astype(BF16)
            for q in range(S5_BANDS):
                ch, st = _band(q)
                y_ref[b, :, ch] = (_dot(xs_ref[b, :, st], cr_ref[q]) + _dot(xs_ref[b, :, _im(st)], ci_ref[q])
                                   + d_ref[:, ch] * zav[:, ch])

    tok = lambda w: pl.BlockSpec((nb, ts, w), lambda j: (0, j, 0))
    to_st, to_ch = _full((S5_BANDS, BAND_CH, BAND_ST)), _full((S5_BANDS, BAND_ST, BAND_CH))
    return _pcall(body, "s5_fwd", (nts,),
                  [tok(S5_WIDTH), to_st, to_st, _full((2, S5_N)), to_ch, to_ch, _full((1, S5_WIDTH))],
                  [tok(2 * S5_N), tok(S5_WIDTH)],
                  [_sds((nb, seq, 2 * S5_N), BF16), _sds((nb, seq, S5_WIDTH))],
                  scratch=[pltpu.VMEM((nb, ts, 2 * S5_N), F32), pltpu.VMEM((nb, 2, S5_N), F32)],
                  )(za, *b_bands, lam, *c_bands, dskip)


def _hgrn_gates(zq, zf, lbh):
    sf = _sigmoid(zf)
    f = lbh + (1.0 - lbh) * sf
    sq = _sigmoid(zq)
    qa = zq * sq * QSCALE
    bc = _cumsum_rows(jnp.log(f))
    bm = bc[CHUNK // 2 - 1:CHUNK // 2, :]
    bl = bc[CHUNK - 1:CHUNK, :]
    return sf, f, sq, qa, bc, bm, bl


def _hgrn_fwd(zh, lb, nb, seq):
    nc = seq // CHUNK

    def body(zh_ref, lb_ref, o_ref, sts_ref, st_ref):
        @pl.when(pl.program_id(0) == 0)
        def _():
            st_ref[...] = jnp.zeros_like(st_ref)

        causal = (lax.broadcasted_iota(jnp.int32, (CHUNK, CHUNK), 0)
                  >= lax.broadcasted_iota(jnp.int32, (CHUNK, CHUNK), 1))
        for b in range(nb):
            for h in range(HG_HEADS):
                hs = slice(h * HG_HEAD, (h + 1) * HG_HEAD)
                zq = zh_ref[b, :, h * HG_HEAD:(h + 1) * HG_HEAD]
                zf = zh_ref[b, :, HG_WIDTH + h * HG_HEAD:HG_WIDTH + (h + 1) * HG_HEAD]
                zi = zh_ref[b, :, 2 * HG_WIDTH + h * HG_HEAD:2 * HG_WIDTH + (h + 1) * HG_HEAD]
                _, f, _, qa, bc, bm, bl = _hgrn_gates(zq, zf, lb_ref[:, hs])
                k = 1.0 - f
                qt = qa * jnp.exp(bc - bm)
                kt = k * jnp.exp(bm - bc)
                qb = qa * jnp.exp(bc)
                kd = k * jnp.exp(bl - bc)
                st = st_ref[b, h]
                sts_ref[b, 0, h] = st
                a = jnp.where(causal, _dot_nt(qt, kt), 0.0)
                o_ref[b, :, hs] = _dot(a, zi) + _dot_nt(qb, st)
                st_ref[b, h] = st * jnp.exp(bl) + _dot_tn(zi, kd)

    return _pcall(body, "hgrn_fwd", (nc,),
                  [pl.BlockSpec((nb, CHUNK, 4 * HG_WIDTH), lambda c: (0, c, 0)), _full((1, HG_WIDTH))],
                  [pl.BlockSpec((nb, CHUNK, HG_WIDTH), lambda c: (0, c, 0)),
                   pl.BlockSpec((nb, 1, HG_HEADS, HG_HEAD, HG_HEAD), lambda c: (0, c, 0, 0, 0))],
                  [_sds((nb, seq, HG_WIDTH)), _sds((nb, nc, HG_HEADS, HG_HEAD, HG_HEAD))],
                  scratch=[pltpu.VMEM((nb, HG_HEADS, HG_HEAD, HG_HEAD), F32)])(zh, lb)


def _head_rms(o):
    parts = []
    for h in range(HG_HEADS):
        oh = o[:, h * HG_HEAD:(h + 1) * HG_HEAD]
        r = lax.rsqrt(jnp.mean(oh * oh, axis=-1, keepdims=True) + EPS)
        parts.append(jnp.broadcast_to(r, oh.shape))
    return jnp.concatenate(parts, axis=1)


def _head_mean(v):
    parts = []
    for h in range(HG_HEADS):
        vh = v[:, h * HG_HEAD:(h + 1) * HG_HEAD]
        parts.append(jnp.broadcast_to(jnp.mean(vh, axis=-1, keepdims=True), vh.shape))
    return jnp.concatenate(parts, axis=1)


def _mix_fwd(x, y0, o, zh, zgt, w_glu, b_glu, gain, w_pa, w_pb, w_out, g_ffn, tm):
    t = x.shape[0]

    def body(x_ref, y0_ref, o_ref, zg_ref, zgt_ref, wglu_ref, bglu_ref, gain_ref, wpa_ref, wpb_ref, wout_ref,
             gffn_ref, x1_ref, u2_ref, pa_ref, pb_ref, ya2_ref, yb_ref):
        ya1 = _gelu(y0_ref[...])
        s = _sigmoid(_dot(ya1, wglu_ref[...]) + bglu_ref[...])
        ya2 = (ya1 * s).astype(BF16)
        ov = o_ref[...]
        zg = zg_ref[...]
        yb = (ov * _head_rms(ov) * gain_ref[...] * (zg * _sigmoid(zg))).astype(BF16)
        ya2_ref[...] = ya2
        yb_ref[...] = yb
        pa = jnp.dot(ya2, wpa_ref[...], preferred_element_type=F32)
        pb = jnp.dot(yb, wpb_ref[...], preferred_element_type=F32)
        pa_ref[...] = pa.astype(BF16)
        pb_ref[...] = pb.astype(BF16)
        m = (_sigmoid(zgt_ref[:, 0:D_MODEL].astype(F32)) * pa
             + _sigmoid(zgt_ref[:, D_MODEL:].astype(F32)) * pb)
        x1 = x_ref[...] + _dot(m, wout_ref[...])
        x1_ref[...] = x1
        r = lax.rsqrt(jnp.mean(x1 * x1, axis=-1, keepdims=True) + EPS)
        u2_ref[...] = (x1 * r * gffn_ref[...]).astype(BF16)

    row = lambda w: pl.BlockSpec((tm, w), lambda i: (i, 0))
    return _pcall(body, "mix_fwd", (t // tm,),
                  [row(D_MODEL), row(S5_WIDTH), row(HG_WIDTH), pl.BlockSpec((tm, HG_WIDTH), lambda i: (i, 3)),
                   row(2 * D_MODEL), _full((S5_WIDTH, S5_WIDTH)), _full((1, S5_WIDTH)), _full((1, HG_WIDTH)),
                   _full((S5_WIDTH, D_MODEL)), _full((HG_WIDTH, D_MODEL)), _full((D_MODEL, D_MODEL)),
                   _full((1, D_MODEL))],
                  [row(D_MODEL), row(D_MODEL), row(D_MODEL), row(D_MODEL), row(S5_WIDTH), row(HG_WIDTH)],
                  [_sds((t, D_MODEL)), _sds((t, D_MODEL), BF16), _sds((t, D_MODEL), BF16), _sds((t, D_MODEL), BF16),
                   _sds((t, S5_WIDTH), BF16), _sds((t, HG_WIDTH), BF16)],
                  )(x, y0, o, zh, zgt, w_glu, b_glu, gain, w_pa, w_pb, w_out, g_ffn)


FF_COLS = 256
FF_UP_TILE = 2 * D_FF // 4


def _ffn_up(u2, w_up, tm):
    t = u2.shape[0]
    n = 2 * D_FF

    def body(u_ref, w_ref, h_ref):
        h_ref[...] = _dot_nt(u_ref[...], w_ref[...]).astype(BF16)

    return _pcall(body, "ffn_up", (n // FF_UP_TILE, t // tm),
                  [pl.BlockSpec((tm, D_MODEL), lambda j, i: (i, 0)),
                   pl.BlockSpec((FF_UP_TILE, D_MODEL), lambda j, i: (j, 0))],
                  pl.BlockSpec((tm, FF_UP_TILE), lambda j, i: (i, j)),
                  _sds((t, n), BF16))(u2, w_up)


HALO = 16


def _shift_matrix(tm):
    r = lax.broadcasted_iota(jnp.int32, (tm, tm), 0)
    c = lax.broadcasted_iota(jnp.int32, (tm, tm), 1)
    return jnp.where(r == c + 1, 1.0, 0.0).astype(BF16)


def _conv_cols(h_ref, halo_ref, valid, wc_ref, bc_ref, c0):
    cs = slice(c0, c0 + FF_COLS)
    cur = h_ref[:, cs].astype(F32)
    prev = jnp.where(valid, halo_ref[:, cs].astype(F32), 0.0)
    full = jnp.concatenate([prev, cur], axis=0)
    h1 = pltpu.roll(full, 1, axis=0)[HALO:]
    h2 = pltpu.roll(full, 2, axis=0)[HALO:]
    return h2 * wc_ref[0:1, cs] + h1 * wc_ref[1:2, cs] + cur * wc_ref[2:3, cs] + bc_ref[:, cs]


def _ffn_down_loss(h, x1, tgt, w_conv, b_conv, w_down, g_final, seq, tm):
    t = h.shape[0]
    tps = seq // tm
    n = 2 * D_FF

    def body(h_ref, halo_ref, x1_ref, tgt_ref, wc_ref, bc_ref, wd_ref, gf_ref,
             hc_ref, a_ref, dx2_ref, dx2b_ref, loss_ref, dgf_ref):
        i = pl.program_id(0)

        @pl.when(i == 0)
        def _():
            loss_ref[...] = jnp.zeros_like(loss_ref)
            dgf_ref[...] = jnp.zeros_like(dgf_ref)

        valid = (i % tps) != 0
        x2 = x1_ref[...]
        for j in range(D_FF // FF_COLS):
            gate = _conv_cols(h_ref, halo_ref, valid, wc_ref, bc_ref, j * FF_COLS)
            val = _conv_cols(h_ref, halo_ref, valid, wc_ref, bc_ref, D_FF + j * FF_COLS)
            hc_ref[:, j * FF_COLS:(j + 1) * FF_COLS] = gate.astype(BF16)
            hc_ref[:, D_FF + j * FF_COLS:D_FF + (j + 1) * FF_COLS] = val.astype(BF16)
            a = (gate * _sigmoid(gate) * val).astype(BF16)
            a_ref[:, j * FF_COLS:(j + 1) * FF_COLS] = a
            x2 = x2 + jnp.dot(a, wd_ref[j * FF_COLS:(j + 1) * FF_COLS, :], preferred_element_type=F32)
        r = lax.rsqrt(jnp.mean(x2 * x2, axis=-1, keepdims=True) + EPS)
        xn = x2 * r
        g = gf_ref[...]
        e = xn * g - tgt_ref[...]
        loss_ref[...] += (0.5 / D_MODEL) * jnp.sum(e * e).reshape(1, 1)
        dy = e * (1.0 / D_MODEL)
        dgf_ref[...] += jnp.sum(dy * xn, axis=0, keepdims=True)
        dxn = dy * g
        dx2 = r * (dxn - xn * jnp.mean(dxn * xn, axis=-1, keepdims=True))
        dx2_ref[...] = dx2
        dx2b_ref[...] = dx2.astype(BF16)

    row = lambda w: pl.BlockSpec((tm, w), lambda i: (i, 0))
    halo = pl.BlockSpec((HALO, n), lambda i: (jnp.maximum(i * (tm // HALO) - 1, 0), 0))
    return _pcall(body, "ffn_down_loss", (t // tm,),
                  [row(n), halo, row(D_MODEL), row(D_MODEL), _full((CONV_W, n)), _full((1, n)),
                   _full((D_FF, D_MODEL)), _full((1, D_MODEL))],
                  [row(n), row(D_FF), row(D_MODEL), row(D_MODEL), _full((1, 1)), _full((1, D_MODEL))],
                  [_sds((t, n), BF16), _sds((t, D_FF), BF16), _sds((t, D_MODEL)), _sds((t, D_MODEL), BF16),
                   _sds((1, 1)), _sds((1, D_MODEL))],
                  )(h, h, x1, tgt, w_conv, b_conv, w_down, g_final)


def _wgrad(a, b, name, tn, out_dtype=F32, band=None, after=None):
    t, m = a.shape
    n = b.shape[1] if band is None else band
    nbands = 1 if band is None else b.shape[1] // band
    after = b if after is None else after

    def body(a_ref, b_ref, after_ref, o_ref):
        o_ref[...] = _dot_tn(a_ref[...], b_ref[...]).astype(out_dtype)

    return _pcall(body, name, (m // tn,),
                  [pl.BlockSpec((t, tn), lambda i: (0, i)), pl.BlockSpec((t, n), lambda i: (0, i % nbands)),
                   pl.BlockSpec(memory_space=pl.ANY)],
                  pl.BlockSpec((tn, n), lambda i: (i, 0)), _sds((m, n), out_dtype))(a, b, after)


def _ffn_bwd_act(dx2b, hc, w_down, tm):
    t = hc.shape[0]
    n = 2 * D_FF

    def body(dx2_ref, hc_ref, wd_ref, dhc_ref, dbc_ref):
        @pl.when(pl.program_id(0) == 0)
        def _():
            dbc_ref[...] = jnp.zeros_like(dbc_ref)

        dx2 = dx2_ref[...]
        for j in range(D_FF // FF_COLS):
            gs = slice(j * FF_COLS, (j + 1) * FF_COLS)
            vs = slice(D_FF + j * FF_COLS, D_FF + (j + 1) * FF_COLS)
            gate = hc_ref[:, gs].astype(F32)
            val = hc_ref[:, vs].astype(F32)
            da = _dot_nt(dx2, wd_ref[gs, :])
            sg = _sigmoid(gate)
            dgate = da * val * (sg * (1.0 + gate * (1.0 - sg)))
            dval = da * (gate * sg)
            dhc_ref[:, gs] = dgate.astype(BF16)
            dhc_ref[:, vs] = dval.astype(BF16)
            dbc_ref[:, gs] += jnp.sum(dgate, axis=0, keepdims=True)
            dbc_ref[:, vs] += jnp.sum(dval, axis=0, keepdims=True)

    row = lambda w: pl.BlockSpec((tm, w), lambda i: (i, 0))
    return _pcall(body, "ffn_bwd_act", (t // tm,),
                  [row(D_MODEL), row(n), _full((D_FF, D_MODEL))],
                  [row(n), _full((1, n))],
                  [_sds((t, n), BF16), _sds((1, n))],
                  )(dx2b, hc, w_down)


def _ffn_bwd_up(dhc, h, dx2, x1, w_conv, w_up, g_ffn, seq, tm):
    t = dhc.shape[0]
    tps = seq // tm
    n = 2 * D_FF
    last = t // HALO - 1

    def body(dhc_ref, halo_ref, h_ref, dx2_ref, x1_ref, wc_ref, wu_ref, gf_ref,
             dh_ref, dx1_ref, dx1b_ref, dgf_ref, dwc_ref):
        i = pl.program_id(0)

        @pl.when(i == 0)
        def _():
            dgf_ref[...] = jnp.zeros_like(dgf_ref)
            dwc_ref[...] = jnp.zeros_like(dwc_ref)

        valid = ((i + 1) % tps) != 0
        du2 = jnp.zeros((tm, D_MODEL), F32)
        for j in range(n // FF_COLS):
            cs = slice(j * FF_COLS, (j + 1) * FF_COLS)
            cur = dhc_ref[:, cs].astype(F32)
            nxt = jnp.where(valid, halo_ref[:, cs].astype(F32), 0.0)
            full = jnp.concatenate([cur, nxt], axis=0)
            d1 = pltpu.roll(full, tm + HALO - 1, axis=0)[:tm]
            d2 = pltpu.roll(full, tm + HALO - 2, axis=0)[:tm]
            dh = (cur * wc_ref[2:3, cs] + d1 * wc_ref[1:2, cs] + d2 * wc_ref[0:1, cs]).astype(BF16)
            dh_ref[:, cs] = dh
            du2 = du2 + _dot(dh, wu_ref[cs, :])
            hv = h_ref[:, cs].astype(F32)
            dwc_ref[0:1, cs] += jnp.sum(hv * d2, axis=0, keepdims=True)
            dwc_ref[1:2, cs] += jnp.sum(hv * d1, axis=0, keepdims=True)
            dwc_ref[2:3, cs] += jnp.sum(hv * cur, axis=0, keepdims=True)
        x1 = x1_ref[...]
        r = lax.rsqrt(jnp.mean(x1 * x1, axis=-1, keepdims=True) + EPS)
        xn = x1 * r
        dgf_ref[...] += jnp.sum(du2 * xn, axis=0, keepdims=True)
        dxn = du2 * gf_ref[...]
        dx1 = dx2_ref[...] + r * (dxn - xn * jnp.mean(dxn * xn, axis=-1, keepdims=True))
        dx1_ref[...] = dx1
        dx1b_ref[...] = dx1.astype(BF16)

    row = lambda w: pl.BlockSpec((tm, w), lambda i: (i, 0))
    halo = pl.BlockSpec((HALO, n), lambda i: (jnp.minimum((i + 1) * (tm // HALO), last), 0))
    return _pcall(body, "ffn_bwd_up", (t // tm,),
                  [row(n), halo, row(n), row(D_MODEL), row(D_MODEL), _full((CONV_W, n)), _full((n, D_MODEL)),
                   _full((1, D_MODEL))],
                  [row(n), row(D_MODEL), row(D_MODEL), _full((1, D_MODEL)), _full((CONV_W, n))],
                  [_sds((t, n), BF16), _sds((t, D_MODEL)), _sds((t, D_MODEL), BF16), _sds((1, D_MODEL)),
                   _sds((CONV_W, n))],
                  )(dhc, dhc, h, dx2, x1, w_conv, w_up, g_ffn)


def _mix_bwd(dx1, y0, o, zh, zgt, pa, pb, w_glu, b_glu, gain, w_pa, w_pb, w_out, tm):
    t = dx1.shape[0]

    def body(dx1_ref, y0_ref, o_ref, zg_ref, zgt_ref, pa_ref, pb_ref, wglu_ref, bglu_ref, gain_ref, wpa_ref,
             wpb_ref, wout_ref,
             dy0_ref, do_ref, dzg_ref, dzgt_ref, m_ref, dpa_ref, dpb_ref, ya1_ref, dpre_ref, dbglu_ref, dgain_ref):
        @pl.when(pl.program_id(0) == 0)
        def _():
            dbglu_ref[...] = jnp.zeros_like(dbglu_ref)
            dgain_ref[...] = jnp.zeros_like(dgain_ref)

        dm = _dot_nt(dx1_ref[...], wout_ref[...])
        sga = _sigmoid(zgt_ref[:, 0:D_MODEL].astype(F32))
        sgb = _sigmoid(zgt_ref[:, D_MODEL:].astype(F32))
        pa = pa_ref[...].astype(F32)
        pb = pb_ref[...].astype(F32)
        m_ref[...] = (sga * pa + sgb * pb).astype(BF16)
        dzgt_ref[:, 0:D_MODEL] = (dm * pa * sga * (1.0 - sga)).astype(BF16)
        dzgt_ref[:, D_MODEL:] = (dm * pb * sgb * (1.0 - sgb)).astype(BF16)
        dpa = (dm * sga).astype(BF16)
        dpb = (dm * sgb).astype(BF16)
        dpa_ref[...] = dpa
        dpb_ref[...] = dpb
        dya2 = _dot_nt(dpa, wpa_ref[...])
        dyb = _dot_nt(dpb, wpb_ref[...])
        y0 = y0_ref[...]
        ya1 = _gelu(y0)
        ya1_ref[...] = ya1.astype(BF16)
        s = _sigmoid(_dot(ya1, wglu_ref[...]) + bglu_ref[...])
        dpre = dya2 * ya1 * s * (1.0 - s)
        dpre_ref[...] = dpre.astype(BF16)
        dbglu_ref[...] += jnp.sum(dpre, axis=0, keepdims=True)
        dya1 = dya2 * s + _dot_nt(dpre, wglu_ref[...])
        dy0_ref[...] = dya1 * _gelu_grad(y0)
        ov = o_ref[...]
        zg = zg_ref[...]
        oh = ov * _head_rms(ov)
        on = oh * gain_ref[...]
        sz = _sigmoid(zg)
        dzg_ref[...] = (dyb * on * (sz * (1.0 + zg * (1.0 - sz)))).astype(BF16)
        don = dyb * (zg * sz)
        dgain_ref[...] += jnp.sum(don * oh, axis=0, keepdims=True)
        doh = don * gain_ref[...]
        do_ref[...] = _head_rms(ov) * (doh - oh * _head_mean(doh * oh))

    row = lambda w: pl.BlockSpec((tm, w), lambda i: (i, 0))
    return _pcall(body, "mix_bwd", (t // tm,),
                  [row(D_MODEL), row(S5_WIDTH), row(HG_WIDTH), pl.BlockSpec((tm, HG_WIDTH), lambda i: (i, 3)),
                   row(2 * D_MODEL), row(D_MODEL), row(D_MODEL), _full((S5_WIDTH, S5_WIDTH)), _full((1, S5_WIDTH)),
                   _full((1, HG_WIDTH)), _full((S5_WIDTH, D_MODEL)), _full((HG_WIDTH, D_MODEL)),
                   _full((D_MODEL, D_MODEL))],
                  [row(S5_WIDTH), row(HG_WIDTH), row(HG_WIDTH), row(2 * D_MODEL), row(D_MODEL), row(D_MODEL),
                   row(D_MODEL), row(S5_WIDTH), row(S5_WIDTH), _full((1, S5_WIDTH)), _full((1, HG_WIDTH))],
                  [_sds((t, S5_WIDTH)), _sds((t, HG_WIDTH)), _sds((t, HG_WIDTH), BF16), _sds((t, 2 * D_MODEL), BF16),
                   _sds((t, D_MODEL), BF16), _sds((t, D_MODEL), BF16), _sds((t, D_MODEL), BF16),
                   _sds((t, S5_WIDTH), BF16), _sds((t, S5_WIDTH), BF16), _sds((1, S5_WIDTH)), _sds((1, HG_WIDTH))],
                  )(dx1, y0, o, zh, zgt, pa, pb, w_glu, b_glu, gain, w_pa, w_pb, w_out)


def _s5_bwd(dy0, za, xs, c_bands, b_bands, lam, dskip, nb, seq, ts):
    nts = seq // ts

    def body(dy0_ref, za_ref, xs_ref, halo_ref, cr_ref, ci_ref, br_ref, bi_ref, lam_ref, d_ref,
             dza_ref, a_ref, dlam_ref, dd_ref, acc_ref, st_ref):
        j = pl.program_id(0)

        @pl.when(j == 0)
        def _():
            dlam_ref[...] = jnp.zeros_like(dlam_ref)
            dd_ref[...] = jnp.zeros_like(dd_ref)
            st_ref[...] = jnp.zeros_like(st_ref)

        for b in range(nb):
            dy0 = dy0_ref[b]
            for q in range(S5_BANDS):
                ch, st = _band(q)
                acc_ref[b, :, st] = _dot(dy0[:, ch], cr_ref[q])
                acc_ref[b, :, _im(st)] = _dot(dy0[:, ch], ci_ref[q])
        _complex_scan(acc_ref, lam_ref, st_ref, nb, ts, reverse=True)
        shift = _shift_matrix(ts)
        top = lax.broadcasted_iota(jnp.int32, (SUBLANES, S5_LANES), 0) == 0
        for b in range(nb):
            a_ref[b] = acc_ref[b].astype(BF16)
            first = jnp.where(j == nts - 1, 0.0, halo_ref[b, HALO - 1:HALO, :].astype(F32))

            def shifted(cols):
                xp = jnp.dot(shift, xs_ref[b, :, cols], preferred_element_type=F32)
                return jnp.concatenate([xp[:SUBLANES] + jnp.where(top, first[:, cols], 0.0), xp[SUBLANES:]], axis=0)

            for cc in range(S5_N // S5_LANES):
                re = slice(cc * S5_LANES, (cc + 1) * S5_LANES)
                ar, ai, xr, xi = acc_ref[b, :, re], acc_ref[b, :, _im(re)], shifted(re), shifted(_im(re))
                dlam_ref[0:1, re] += jnp.sum(ar * xr + ai * xi, axis=0, keepdims=True)
                dlam_ref[1:2, re] += jnp.sum(ai * xr - ar * xi, axis=0, keepdims=True)
            dy0 = dy0_ref[b]
            for q in range(S5_BANDS):
                ch, st = _band(q)
                dza_ref[b, :, ch] = (_dot(a_ref[b, :, st], br_ref[q]) + _dot(a_ref[b, :, _im(st)], bi_ref[q])
                                     + d_ref[:, ch] * dy0[:, ch]).astype(BF16)
            dd_ref[...] += jnp.sum(dy0 * za_ref[b], axis=0, keepdims=True)

    tile = lambda j: nts - 1 - j
    tok = lambda w: pl.BlockSpec((nb, ts, w), lambda j: (0, tile(j), 0))
    halo = pl.BlockSpec((nb, HALO, 2 * S5_N), lambda j: (0, jnp.maximum(tile(j) * (ts // HALO) - 1, 0), 0))
    to_st, to_ch = _full((S5_BANDS, BAND_CH, BAND_ST)), _full((S5_BANDS, BAND_ST, BAND_CH))
    return _pcall(body, "s5_bwd", (nts,),
                  [tok(S5_WIDTH), tok(S5_WIDTH), tok(2 * S5_N), halo, to_st, to_st, to_ch, to_ch,
                   _full((2, S5_N)), _full((1, S5_WIDTH))],
                  [tok(S5_WIDTH), tok(2 * S5_N), _full((2, S5_N)), _full((1, S5_WIDTH))],
                  [_sds((nb, seq, S5_WIDTH), BF16), _sds((nb, seq, 2 * S5_N), BF16), _sds((2, S5_N)),
                   _sds((1, S5_WIDTH))],
                  scratch=[pltpu.VMEM((nb, ts, 2 * S5_N), F32), pltpu.VMEM((nb, 2, S5_N), F32)],
                  )(dy0, za, xs, xs, *c_bands, *b_bands, lam, dskip)


def _hgrn_bwd(zh, do, sts, lb, nb, seq):
    nc = seq // CHUNK

    def body(zh_ref, do_ref, sts_ref, lb_ref, dz_ref, dlb_ref, dst_ref):
        @pl.when(pl.program_id(0) == 0)
        def _():
            dst_ref[...] = jnp.zeros_like(dst_ref)
            dlb_ref[...] = jnp.zeros_like(dlb_ref)

        row = lax.broadcasted_iota(jnp.int32, (CHUNK, CHUNK), 0)
        causal = row >= lax.broadcasted_iota(jnp.int32, (CHUNK, CHUNK), 1)
        last_row = lax.broadcasted_iota(jnp.int32, (CHUNK, HG_HEAD), 0) == CHUNK - 1
        for b in range(nb):
            for h in range(HG_HEADS):
                hs = slice(h * HG_HEAD, (h + 1) * HG_HEAD)
                zq = zh_ref[b, :, h * HG_HEAD:(h + 1) * HG_HEAD]
                zf = zh_ref[b, :, HG_WIDTH + h * HG_HEAD:HG_WIDTH + (h + 1) * HG_HEAD]
                zi = zh_ref[b, :, 2 * HG_WIDTH + h * HG_HEAD:2 * HG_WIDTH + (h + 1) * HG_HEAD]
                lbh = lb_ref[:, hs]
                sf, f, sq, qa, bc, bm, bl = _hgrn_gates(zq, zf, lbh)
                k = 1.0 - f
                e_qt = jnp.exp(bc - bm)
                e_kt = jnp.exp(bm - bc)
                e_b = jnp.exp(bc)
                e_kd = jnp.exp(bl - bc)
                e_l = jnp.exp(bl)
                qt, kt, qb, kd = qa * e_qt, k * e_kt, qa * e_b, k * e_kd
                a = jnp.where(causal, _dot_nt(qt, kt), 0.0)
                st = sts_ref[b, 0, h]
                dst = dst_ref[b, h]
                dov = do_ref[b, :, hs]
                da = jnp.where(causal, _dot_nt(dov, zi), 0.0)
                dqt = _hdot(da, kt)
                dkt = _hdot_tn(da, qt)
                dqb = _dot(dov, st)
                di = _dot_tn(a, dov) + _dot_nt(kd, dst)
                dkd = _dot(zi, dst)
                de_l = jnp.sum(dst * st, axis=0, keepdims=True)
                dst_ref[b, h] = dst * e_l + _dot_tn(dov, qb)
                dqa = dqt * e_qt + dqb * e_b
                dk = dkt * e_kt + dkd * e_kd
                dbl = jnp.sum(dkd * kd, axis=0, keepdims=True) + de_l * e_l
                db = dqt * qt - dkt * kt + dqb * qb - dkd * kd + jnp.where(last_row, dbl, 0.0)
                df = _cumsum_rows(db, reverse=True) / f - dk
                dzq = dqa * QSCALE * (sq * (1.0 + zq * (1.0 - sq)))
                dzf = df * (1.0 - lbh) * sf * (1.0 - sf)
                dz_ref[b, :, h * HG_HEAD:(h + 1) * HG_HEAD] = dzq.astype(BF16)
                dz_ref[b, :, HG_WIDTH + h * HG_HEAD:HG_WIDTH + (h + 1) * HG_HEAD] = dzf.astype(BF16)
                dz_ref[b, :, 2 * HG_WIDTH + h * HG_HEAD:2 * HG_WIDTH + (h + 1) * HG_HEAD] = di.astype(BF16)
                dlb_ref[:, hs] += jnp.sum(df * (1.0 - sf), axis=0, keepdims=True)

    rev = lambda c: nc - 1 - c
    return _pcall(body, "hgrn_bwd", (nc,),
                  [pl.BlockSpec((nb, CHUNK, 4 * HG_WIDTH), lambda c: (0, rev(c), 0)),
                   pl.BlockSpec((nb, CHUNK, HG_WIDTH), lambda c: (0, rev(c), 0)),
                   pl.BlockSpec((nb, 1, HG_HEADS, HG_HEAD, HG_HEAD), lambda c: (0, rev(c), 0, 0, 0)),
                   _full((1, HG_WIDTH))],
                  [pl.BlockSpec((nb, CHUNK, 3 * HG_WIDTH), lambda c: (0, rev(c), 0)), _full((1, HG_WIDTH))],
                  [_sds((nb, seq, 3 * HG_WIDTH), BF16), _sds((1, HG_WIDTH))],
                  scratch=[pltpu.VMEM((nb, HG_HEADS, HG_HEAD, HG_HEAD), F32)])(zh, do, sts, lb)


def _in_proj_bwd(dza, dzh, dzg, dzgt, dx1, x, g_mix, w_in, tm):
    t = x.shape[0]

    def body(dza_ref, dzh_ref, dzg_ref, dzgt_ref, dx1_ref, x_ref, g_ref, w_ref, dz_ref, dx_ref, dg_ref):
        @pl.when(pl.program_id(0) == 0)
        def _():
            dg_ref[...] = jnp.zeros_like(dg_ref)

        c1, c2, c3 = S5_WIDTH, S5_WIDTH + 3 * HG_WIDTH, S5_WIDTH + 4 * HG_WIDTH
        dz_ref[:, 0:c1] = dza_ref[...]
        dz_ref[:, c1:c2] = dzh_ref[...]
        dz_ref[:, c2:c3] = dzg_ref[...]
        dz_ref[:, c3:] = dzgt_ref[...]
        du = _dot(dz_ref[...], w_ref[...])
        xv = x_ref[...]
        r = lax.rsqrt(jnp.mean(xv * xv, axis=-1, keepdims=True) + EPS)
        xn = xv * r
        dg_ref[...] += jnp.sum(du * xn, axis=0, keepdims=True)
        dxn = du * g_ref[...]
        dx_ref[...] = dx1_ref[...] + r * (dxn - xn * jnp.mean(dxn * xn, axis=-1, keepdims=True))

    row = lambda w: pl.BlockSpec((tm, w), lambda i: (i, 0))
    return _pcall(body, "in_proj_bwd", (t // tm,),
                  [row(S5_WIDTH), row(3 * HG_WIDTH), row(HG_WIDTH), row(2 * D_MODEL), row(D_MODEL), row(D_MODEL),
                   _full((1, D_MODEL)), _full((N_IN, D_MODEL))],
                  [row(N_IN), row(D_MODEL), _full((1, D_MODEL))],
                  [_sds((t, N_IN), BF16), _sds((t, D_MODEL)), _sds((1, D_MODEL))],
                  )(dza, dzh, dzg, dzgt, dx1, x, g_mix, w_in)


def _after(value, token):
    return value + token[0, 0]


def _local_step(x3, tgt3, weights, sp, emit, emit_small):
    nb, seq, _ = x3.shape
    t = nb * seq
    tm = _token_tile(seq)
    x = x3.reshape(t, D_MODEL)
    tgt = tgt3.reshape(t, D_MODEL)
    row = lambda v: v.reshape(1, -1)

    a_re, a_im, b_re, b_im = sp["s5_a_re"], sp["s5_a_im"], sp["s5_b_re"], sp["s5_b_im"]
    ldt = sp["s5_log_dt"].reshape(S5_GROUPS, 1)
    lr, li, bb_re, bb_im, lb = _params_fwd(a_re, a_im, ldt, b_re, b_im, sp["hg_lb_logits"])
    lam = jnp.concatenate([lr.reshape(1, S5_N), li.reshape(1, S5_N)], axis=0)
    swap = lambda m: m.transpose(0, 2, 1)
    b_to_st = (_band_blocks(bb_re), _band_blocks(bb_im))
    b_to_ch = (_band_blocks(swap(bb_re)), _band_blocks(swap(bb_im)))
    c_to_ch = (_band_blocks(swap(sp["s5_c_re"])), _band_blocks(swap(-sp["s5_c_im"])))
    c_to_st = (_band_blocks(sp["s5_c_re"]), _band_blocks(-sp["s5_c_im"]))

    g_mix, g_ffn, g_final = row(sp["g_mix"]), row(sp["g_ffn"]), row(sp["g_final"])
    b_glu, gain, dskip, b_conv = row(sp["b_glu"]), row(sp["hg_norm_gain"]), row(sp["s5_d"]), row(sp["b_conv"])

    w_in = weights("in", lam, *b_to_st, *b_to_ch, *c_to_ch, *c_to_st)["w_in"]
    u, za, zh, zgt = _in_proj(x, g_mix, w_in, tm)
    seqs = lambda v: v.reshape(nb, seq, v.shape[-1])
    toks = lambda v: v.reshape(t, v.shape[-1])
    xs3, y0 = _s5_fwd(seqs(za), b_to_st, lam, c_to_ch, dskip, nb, seq, tm)
    xs, y0 = toks(xs3), toks(y0)
    o3, sts = _hgrn_fwd(zh.reshape(nb, seq, 4 * HG_WIDTH), lb, nb, seq)
    o = o3.reshape(t, HG_WIDTH)
    wm = weights("mix", y0, o3)
    weights.forward("ffn", wm["w_out"])
    x1, u2, pa, pb, ya2, yb = _mix_fwd(x, y0, o, zh, zgt, wm["w_glu"], b_glu, gain, wm["w_pa"], wm["w_pb"],
                                       wm["w_out"], g_ffn, tm)
    wf = weights("ffn", u2)
    h = _ffn_up(u2, wf["w_up"], min(4 * tm, t))
    hc, a, dx2, dx2b, loss, dg_final = _ffn_down_loss(h, x1, tgt, wf["w_conv"], b_conv, wf["w_down"], g_final,
                                                      seq, tm)

    wgrad = functools.partial(_wgrad, tn=256, out_dtype=BF16)
    dhc, db_conv = _ffn_bwd_act(dx2b, hc, wf["w_down"], tm)
    sent = emit({"w_down": wgrad(a, dx2b, "dw_down")})
    dh, dx1, dx1b, dg_ffn, dw_conv = _ffn_bwd_up(dhc, h, dx2, x1, wf["w_conv"], wf["w_up"], _after(g_ffn, sent),
                                                 seq, tm)
    sent = emit({"w_up": wgrad(dh, u2, "dw_up"), "w_conv": dw_conv})
    (dy0, do, dzg, dzgt, m, dpa, dpb, ya1, dpre, db_glu, dgain) = _mix_bwd(
        dx1b, y0, o, zh, zgt, pa, pb, wm["w_glu"], _after(b_glu, sent), gain, wm["w_pa"], wm["w_pb"], wm["w_out"], tm)
    sent = emit({"w_out": wgrad(m, dx1b, "dw_out"), "w_pa": wgrad(ya2, dpa, "dw_pa"),
                 "w_pb": wgrad(yb, dpb, "dw_pb"), "w_glu": wgrad(ya1, dpre, "dw_glu")})
    dzh3, dlb = _hgrn_bwd(zh.reshape(nb, seq, 4 * HG_WIDTH), do.reshape(nb, seq, HG_WIDTH), sts, _after(lb, sent),
                          nb, seq)
    dza, a_s5, dlam, dd = _s5_bwd(seqs(dy0), seqs(za), xs3, c_to_st, b_to_ch, lam, dskip, nb, seq, tm)
    dza, a_s5 = toks(dza), toks(a_s5)
    dz, dx, dg_mix = _in_proj_bwd(dza, dzh3.reshape(t, 3 * HG_WIDTH), dzg, dzgt, dx1, x, g_mix, w_in, tm)
    sent = emit({"w_in": wgrad(dz, u, "dw_in")})

    band = HG_HEAD
    dbb_band = _wgrad(a_s5, za, "dbb_s5", 512, band=band, after=sent)
    dc_band = _wgrad(xs, dy0, "dc_s5", 512, band=band, after=sent)
    dbb_re = swap(_diag_blocks(dbb_band[:S5_N], S5_STATE, S5_GROUP))
    dbb_im = swap(_diag_blocks(dbb_band[S5_N:], S5_STATE, S5_GROUP))
    dc_re = swap(_diag_blocks(dc_band[:S5_N], S5_STATE, S5_GROUP))
    dc_im = -swap(_diag_blocks(dc_band[S5_N:], S5_STATE, S5_GROUP))
    da_re, da_im, dldt, db_re, db_im, dlogits = _params_bwd(
        a_re, a_im, ldt, b_re, b_im, sp["hg_lb_logits"],
        dlam[0].reshape(S5_GROUPS, S5_STATE), dlam[1].reshape(S5_GROUPS, S5_STATE), dbb_re, dbb_im, dlb)
    emit_small({"g_mix": dg_mix, "s5_a_re": da_re, "s5_a_im": da_im, "s5_log_dt": dldt.reshape(1, S5_GROUPS),
                "s5_b_re": db_re, "s5_b_im": db_im, "s5_c_re": dc_re, "s5_c_im": dc_im, "s5_d": dd, "b_glu": db_glu,
                "hg_lb_logits": dlogits, "hg_norm_gain": dgain, "g_ffn": dg_ffn, "b_conv": db_conv,
                "g_final": dg_final, "loss": loss})
    return dx.reshape(nb, seq, D_MODEL)


def _mesh_peers():
    x, y, c = lax.axis_index("x"), lax.axis_index("y"), lax.axis_index("c")
    peers = []
    for k in range(1, N_DEV):
        px, py, pc = (1 - x if k & 4 else x), (1 - y if k & 2 else y), (1 - c if k & 1 else c)
        peers.append((k, (px, py, pc), 4 * px + 2 * py + pc))
    return 4 * x + 2 * y + c, peers


_HBM = pl.BlockSpec(memory_space=pltpu.HBM)
_SEM = pl.BlockSpec(memory_space=pltpu.SEMAPHORE)


_EFFECT = pltpu.CompilerParams(has_side_effects=pltpu.SideEffectType.DATAFLOW_SIDE_EFFECTING)


def _remote(src, dst, send_sem, recv_sem, to):
    return pltpu.make_async_remote_copy(src_ref=src, dst_ref=dst, send_sem=send_sem, recv_sem=recv_sem,
                                        device_id=to, device_id_type=pl.DeviceIdType.MESH)


def _exchange_start(name, arrays, after):
    n = len(arrays)
    srcs = [pltpu.with_memory_space_constraint(a, pltpu.HBM) for a in arrays]
    lands = [pltpu.with_memory_space_constraint(lax.empty(a.shape, a.dtype), pltpu.HBM) for a in arrays]
    copies = (N_DEV - 1) * n

    def body(*refs):
        src_refs, land_refs = refs[:n], refs[n:2 * n]
        send_sems, recv_sems, token = refs[2 * n + 1], refs[2 * n + 2], refs[-1]
        my_slab, peers = _mesh_peers()
        for k, peer, slab in peers:
            for i in range(n):
                s = (k - 1) * n + i
                _remote(src_refs[i].at[slab], land_refs[i].at[my_slab], send_sems.at[s], recv_sems.at[s], peer).start()
        token[...] = jnp.zeros_like(token)

    outs = pl.pallas_call(
        body, name=name,
        out_shape=(pltpu.SemaphoreType.DMA((copies,)), pltpu.SemaphoreType.DMA((copies,)),
                   *[pltpu.HBM(a.shape, a.dtype) for a in lands], _sds((SUBLANES, LANES))),
        in_specs=[_HBM] * (2 * n) + [pl.BlockSpec(memory_space=pl.ANY)],
        out_specs=(_SEM, _SEM, *[_HBM] * n, pl.BlockSpec(memory_space=pltpu.VMEM)),
        input_output_aliases={n + i: 2 + i for i in range(n)}, compiler_params=_EFFECT,
    )(*srcs, *lands, after)
    return (outs[0], outs[1], srcs, outs[2:2 + n]), outs[-1]


def _exchange_wait(name, state, *after):
    send_sems, recv_sems, srcs, lands = state
    n = len(lands)

    def body(*refs):
        src_refs, land_refs = refs[:n], refs[n:2 * n]
        send_ref, recv_ref = refs[2 * n], refs[2 * n + 1]
        _, peers = _mesh_peers()
        for k, peer, slab in peers:
            for i in range(n):
                s = (k - 1) * n + i
                copy = _remote(src_refs[i].at[slab], land_refs[i].at[slab], send_ref.at[s], recv_ref.at[s], peer)
                copy.wait_send()
                copy.wait_recv()

    outs = pl.pallas_call(
        body, name=name,
        out_shape=tuple(pltpu.HBM(a.shape, a.dtype) for a in lands),
        in_specs=[_HBM] * (2 * n) + [_SEM, _SEM] + [pl.BlockSpec(memory_space=pl.ANY)] * len(after),
        out_specs=tuple([_HBM] * n),
        input_output_aliases={n + i: i for i in range(n)}, compiler_params=_EFFECT,
    )(*srcs, *lands, send_sems, recv_sems, *after)
    return list(outs), list(srcs)


def _slab(pos):
    return 4 * pos[0] + 2 * pos[1] + pos[2]


def _chip_routes():
    x, y, c = lax.axis_index("x"), lax.axis_index("y"), lax.axis_index("c")
    return (x, y, c), (x, y, 1 - c), [(1 - x, y, c), (x, 1 - y, c), (1 - x, 1 - y, c)]


def _gather_start(name, arrays, after):
    n = len(arrays)
    me = 4 * lax.axis_index("x") + 2 * lax.axis_index("y") + lax.axis_index("c")
    srcs = [pltpu.with_memory_space_constraint(a, pltpu.HBM) for a in arrays]
    lands = [pltpu.with_memory_space_constraint(
        lax.dynamic_update_slice_in_dim(lax.empty((N_DEV,) + a.shape, a.dtype), a[None], me, 0), pltpu.HBM)
        for a in arrays]

    def body(*refs):
        src_refs, land_refs = refs[:n], refs[n:2 * n]
        send_sems, recv_sems, token = refs[2 * n + 1], refs[2 * n + 2], refs[-1]
        mine, sibling, chips = _chip_routes()
        for k, to in enumerate([sibling] + chips):
            for i in range(n):
                _remote(src_refs[i], land_refs[i].at[_slab(mine)], send_sems.at[k * n + i], recv_sems.at[k * n + i],
                        to).start()
        token[...] = jnp.zeros_like(token)

    outs = pl.pallas_call(
        body, name=name,
        out_shape=(pltpu.SemaphoreType.DMA((4 * n,)), pltpu.SemaphoreType.DMA((4 * n,)),
                   *[pltpu.HBM(a.shape, a.dtype) for a in lands], _sds((SUBLANES, LANES))),
        in_specs=[_HBM] * (2 * n) + [pl.BlockSpec(memory_space=pl.ANY)],
        out_specs=(_SEM, _SEM, *[_HBM] * n, pl.BlockSpec(memory_space=pltpu.VMEM)),
        input_output_aliases={n + i: 2 + i for i in range(n)}, compiler_params=_EFFECT,
    )(*srcs, *lands, after)
    return (outs[0], outs[1], srcs, outs[2:2 + n]), outs[-1]


def _gather_forward(name, state, *after):
    send_a, recv_a, srcs, lands = state
    n = len(lands)

    def body(*refs):
        land_refs, recv_a_ref = refs[:n], refs[n]
        send_b, recv_b = refs[n + 1 + len(after)], refs[n + 2 + len(after)]
        mine, sibling, chips = _chip_routes()
        for j, chip in enumerate(chips):
            for i in range(n):
                block = land_refs[i].at[_slab(chip)]
                _remote(block, block, send_b.at[j * n + i], recv_a_ref.at[(1 + j) * n + i], chip).wait_recv()
                _remote(block, block, send_b.at[j * n + i], recv_b.at[j * n + i], sibling).start()

    outs = pl.pallas_call(
        body, name=name,
        out_shape=(pltpu.SemaphoreType.DMA((3 * n,)), pltpu.SemaphoreType.DMA((3 * n,)),
                   *[pltpu.HBM(a.shape, a.dtype) for a in lands]),
        in_specs=[_HBM] * n + [_SEM] + [pl.BlockSpec(memory_space=pl.ANY)] * len(after),
        out_specs=(_SEM, _SEM, *[_HBM] * n),
        input_output_aliases={i: 2 + i for i in range(n)}, compiler_params=_EFFECT,
    )(*lands, recv_a, *after)
    return (send_a, recv_a, srcs, list(outs[2:])), (outs[0], outs[1])


def _gather_wait(name, state, forwarded, *after):
    send_a, recv_a, srcs, lands = state
    send_b, recv_b = forwarded
    n = len(lands)

    def body(*refs):
        src_refs, land_refs = refs[:n], refs[n:2 * n]
        sa, ra, sb, rb = refs[2 * n:2 * n + 4]
        mine, sibling, chips = _chip_routes()
        for i in range(n):
            for k, to in enumerate([sibling] + chips):
                _remote(src_refs[i], land_refs[i].at[_slab(mine)], sa.at[k * n + i], ra.at[k * n + i], to).wait_send()
            theirs = land_refs[i].at[_slab(sibling)]
            _remote(theirs, theirs, sa.at[i], ra.at[i], sibling).wait_recv()
            for j, chip in enumerate(chips):
                sent = land_refs[i].at[_slab(chip)]
                got = land_refs[i].at[_slab((chip[0], chip[1], sibling[2]))]
                _remote(sent, sent, sb.at[j * n + i], rb.at[j * n + i], sibling).wait_send()
                _remote(got, got, sb.at[j * n + i], rb.at[j * n + i], sibling).wait_recv()

    outs = pl.pallas_call(
        body, name=name,
        out_shape=tuple(pltpu.HBM(a.shape, a.dtype) for a in lands),
        in_specs=[_HBM] * (2 * n) + [_SEM] * 4 + [pl.BlockSpec(memory_space=pl.ANY)] * len(after),
        out_specs=tuple([_HBM] * n),
        input_output_aliases={n + i: i for i in range(n)}, compiler_params=_EFFECT,
    )(*srcs, *lands, send_a, recv_a, send_b, recv_b, *after)
    return list(outs), list(srcs)


def _join_cols(parts, name, tr):
    _, r, c = parts.shape

    def body(p_ref, o_ref):
        for j in range(N_DEV):
            o_ref[:, j * c:(j + 1) * c] = p_ref[j]

    return _pcall(body, name, (r // tr,), [pl.BlockSpec((N_DEV, tr, c), lambda i: (0, i, 0))],
                  pl.BlockSpec((tr, N_DEV * c), lambda i: (i, 0)), _sds((r, N_DEV * c), parts.dtype))(parts)


def _split_cols(full, name, tr):
    r, c = full.shape[0], full.shape[1] // N_DEV

    def body(f_ref, o_ref):
        for j in range(N_DEV):
            o_ref[j] = f_ref[:, j * c:(j + 1) * c]

    return _pcall(body, name, (r // tr,), [pl.BlockSpec((tr, N_DEV * c), lambda i: (i, 0))],
                  pl.BlockSpec((N_DEV, tr, c), lambda i: (0, i, 0)), _sds((N_DEV, r, c), full.dtype))(full)


def _my_slab():
    return (4 * lax.axis_index("x") + 2 * lax.axis_index("y") + lax.axis_index("c")).astype(jnp.int32).reshape(1)


def _adamw(parts, sent, w, m, v, name, tile):
    _, rows, cols = w.shape

    def body(me_ref, p_ref, s_ref, w_ref, m_ref, v_ref, g_out, d_out, m_out, v_out):
        me = me_ref[0]
        g = jnp.where(me == 0, s_ref[0], p_ref[0]).astype(F32)
        for k in range(1, N_DEV):
            g = g + jnp.where(me == k, s_ref[0], p_ref[k]).astype(F32)
        m1 = ADAM_B1 * m_ref[0] + (1.0 - ADAM_B1) * g
        v1 = ADAM_B2 * v_ref[0] + (1.0 - ADAM_B2) * (g * g)
        m_hat = m1 / (1.0 - ADAM_B1 ** ADAM_STEP)
        v_hat = v1 / (1.0 - ADAM_B2 ** ADAM_STEP)
        g_out[0] = g
        d_out[0] = -ADAM_LR * (m_hat / (jnp.sqrt(v_hat) + ADAM_EPS) + ADAM_WD * w_ref[0])
        m_out[0] = m1
        v_out[0] = v1

    row = pl.BlockSpec((1, tile, cols), lambda i, me: (0, i, 0))
    return pl.pallas_call(
        body, name=name, out_shape=[_sds((1, rows, cols))] * 4,
        grid_spec=pltpu.PrefetchScalarGridSpec(
            num_scalar_prefetch=1, grid=(rows // tile,),
            in_specs=[pl.BlockSpec((N_DEV, tile, cols), lambda i, me: (0, i, 0)),
                      pl.BlockSpec((1, tile, cols), lambda i, me: (me[0], i, 0)), row, row, row],
            out_specs=[row, row, row, row]),
        compiler_params=pltpu.CompilerParams(dimension_semantics=("arbitrary",), vmem_limit_bytes=VMEM_LIMIT),
    )(_my_slab(), parts, sent, w, m, v)


BIG = {
    "w_in": ((N_IN // N_DEV, D_MODEL), False, N_IN // N_DEV // 3),
    "w_glu": ((S5_WIDTH // N_DEV, S5_WIDTH), False, S5_WIDTH // N_DEV),
    "w_pa": ((S5_WIDTH, D_MODEL // N_DEV), True, S5_WIDTH),
    "w_pb": ((HG_WIDTH, D_MODEL // N_DEV), True, HG_WIDTH),
    "w_out": ((D_MODEL // N_DEV, D_MODEL), False, D_MODEL // N_DEV),
    "w_up": ((2 * D_FF // N_DEV, D_MODEL), False, 2 * D_FF // N_DEV // 4),
    "w_conv": ((CONV_W, 2 * D_FF // N_DEV), True, CONV_W),
    "w_down": ((D_FF // N_DEV, D_MODEL), False, D_FF // N_DEV // 2),
}
TRANSPOSED = ("w_in", "w_up", "s5_b_re", "s5_b_im")
UNALIGNED_COLS = ("w_conv",)


def _stored(n, arr):
    return jnp.swapaxes(arr, -1, -2) if n in TRANSPOSED else arr


def _join_shards(n, parts):
    (a, b), by_cols, _ = BIG[n]
    if not by_cols:
        return parts.reshape(N_DEV * a, b)
    if n in UNALIGNED_COLS:
        return _join_cols(parts, "join_" + n, min(a, 256))
    return parts.transpose(1, 0, 2).reshape(a, N_DEV * b)


def _split_shards(n, full):
    (a, b), by_cols, _ = BIG[n]
    if not by_cols:
        return full.reshape(N_DEV, a, b)
    if n in UNALIGNED_COLS:
        return _split_cols(full, "split_" + n, min(a, 256))
    return full.reshape(a, N_DEV, b).transpose(1, 0, 2)


SMALL_CORE = {
    "s5_b_re": GSC, "s5_b_im": GSC, "s5_c_re": GSC, "s5_c_im": GSC,
    "g_mix": (1, D_MODEL), "g_ffn": (1, D_MODEL), "g_final": (1, D_MODEL), "s5_d": (1, S5_WIDTH),
    "b_glu": (1, S5_WIDTH), "hg_norm_gain": (1, HG_WIDTH), "hg_lb_logits": (2, HG_WIDTH), "b_conv": (1, 2 * D_FF),
    "s5_log_dt": (1, S5_GROUPS), "s5_a_re": (S5_GROUPS, S5_STATE), "s5_a_im": (S5_GROUPS, S5_STATE), "loss": (1, 1),
}
BLOCK_ROWS = 32


def _small_rows():
    rows, r = {}, 0
    for n, core in SMALL_CORE.items():
        rows[n] = r
        r += BLOCK_ROWS if len(core) == 3 else -(-math.prod(core) // PACK_W)
    return rows, -(-r // SUBLANES) * SUBLANES


SMALL_ROW, SMALL_ROWS = _small_rows()


def _small_pieces(name):
    r, core = SMALL_ROW[name], SMALL_CORE[name]
    if len(core) == 3:
        return [((g, slice(None), slice(None)), slice(r + S5_GROUP * (g % 2), r + S5_GROUP * (g % 2 + 1)),
                 slice(S5_STATE * (g // 2), S5_STATE * (g // 2 + 1))) for g in range(S5_GROUPS)]
    pieces = []
    for i in range(core[0]):
        for c0 in range(0, core[1], PACK_W):
            w, flat = min(PACK_W, core[1] - c0), i * core[1] + c0
            pieces.append(((slice(i, i + 1), slice(c0, c0 + w)), slice(r + flat // PACK_W, r + flat // PACK_W + 1),
                           slice(flat % PACK_W, flat % PACK_W + w)))
    return pieces


def _core_index(ref, name, idx):
    return (0,) * (len(ref.shape) - len(SMALL_CORE[name])) + idx


def _pack_small_grads(grads):
    names = list(SMALL_CORE)

    def body(*refs):
        pack = refs[-1]
        pack[...] = jnp.zeros_like(pack)
        for ref, n in zip(refs, names):
            for idx, rows, lanes in _small_pieces(n):
                pack[rows, lanes] = ref[_core_index(ref, n, idx)]

    return _pcall(body, "pack_small_grads", (1,), [_full(grads[n].shape) for n in names],
                  _full((SMALL_ROWS, PACK_W)), _sds((SMALL_ROWS, PACK_W)))(*[grads[n] for n in names])


def _adamw_small(parts, sent, names, rows, given, name):
    lo, hi = rows
    k = len(names)
    shapes = [given[n].shape for n in names]

    def body(*refs):
        me, p_ref, s_ref, ins, outs = refs[0][0], refs[1], refs[2], refs[3:3 + 3 * k], refs[3 + 3 * k:3 + 7 * k]
        packs, results = refs[3 + 7 * k:6 + 7 * k], refs[6 + 7 * k:]
        for j, pack in enumerate(packs):
            pack[...] = jnp.zeros_like(pack)
            for ref, n in zip(ins[j * k:(j + 1) * k], names):
                for idx, prow, lanes in _small_pieces(n):
                    pack[slice(prow.start - lo, prow.stop - lo), lanes] = ref[_core_index(ref, n, idx)]
        mine = s_ref[lo:hi, :]
        g = jnp.where(me == 0, mine, p_ref[0, lo:hi, :])
        for d in range(1, N_DEV):
            g = g + jnp.where(me == d, mine, p_ref[d, lo:hi, :])
        m1 = ADAM_B1 * packs[1][...] + (1.0 - ADAM_B1) * g
        v1 = ADAM_B2 * packs[2][...] + (1.0 - ADAM_B2) * (g * g)
        m_hat = m1 / (1.0 - ADAM_B1 ** ADAM_STEP)
        v_hat = v1 / (1.0 - ADAM_B2 ** ADAM_STEP)
        results[0][...] = g
        results[1][...] = -ADAM_LR * (m_hat / (jnp.sqrt(v_hat) + ADAM_EPS) + ADAM_WD * packs[0][...])
        results[2][...] = m1
        results[3][...] = v1
        for j, result in enumerate(results):
            for ref, n in zip(outs[j * k:(j + 1) * k], names):
                for idx, prow, lanes in _small_pieces(n):
                    ref[_core_index(ref, n, idx)] = result[slice(prow.start - lo, prow.stop - lo), lanes]

    flat = _pcall(body, name, (1,),
                  [pl.BlockSpec(memory_space=pltpu.SMEM), _full(parts.shape), _full(sent.shape)]
                  + [_full(s) for s in shapes] * 3,
                  [_full(s) for s in shapes] * 4, [_sds(s) for s in shapes] * 4,
                  scratch=[pltpu.VMEM((hi - lo, PACK_W), F32)] * 7,
                  )(_my_slab(), parts, sent, *[given[pre + n] for pre in ("", "m_", "v_") for n in names])
    return {n: [flat[j * k + i] for j in range(4)] for i, n in enumerate(names)}


def kernel(x, g_mix, w_in, s5_a_re, s5_a_im, s5_log_dt, s5_b_re, s5_b_im, s5_c_re, s5_c_im, s5_d, w_glu, b_glu, hg_lb_logits, hg_norm_gain, w_pa, w_pb, w_out, g_ffn, w_up, w_conv, b_conv, w_down, g_final, loss_target, m_g_mix, m_w_in, m_s5_a_re, m_s5_a_im, m_s5_log_dt, m_s5_b_re, m_s5_b_im, m_s5_c_re, m_s5_c_im, m_s5_d, m_w_glu, m_b_glu, m_hg_lb_logits, m_hg_norm_gain, m_w_pa, m_w_pb, m_w_out, m_g_ffn, m_w_up, m_w_conv, m_b_conv, m_w_down, m_g_final, v_g_mix, v_w_in, v_s5_a_re, v_s5_a_im, v_s5_log_dt, v_s5_b_re, v_s5_b_im, v_s5_c_re, v_s5_c_im, v_s5_d, v_w_glu, v_b_glu, v_hg_lb_logits, v_hg_norm_gain, v_w_pa, v_w_pb, v_w_out, v_g_ffn, v_w_up, v_w_conv, v_b_conv, v_w_down, v_g_final):
    given = dict(locals())
    small_names = [n for n in SMALL_CORE if n != "loss"]

    pay = {n: given[n][0] if n == "w_conv" else _stored(n, given[n])[0].astype(BF16) for n in BIG}
    groups = {"in": ["w_in"], "mix": ["w_glu", "w_pa", "w_pb", "w_out"], "ffn": ["w_up", "w_down", "w_conv"]}
    gathers, order = {}, pay["w_in"]
    for grp, names in groups.items():
        gathers[grp], order = _gather_start("gather_" + grp + "_start", [pay[n] for n in names], order)

    forwards = {}

    def forward(grp, *after):
        if grp == "in":
            after = (*after, order)
        forwards[grp] = _gather_forward("gather_" + grp + "_forward", gathers[grp], *after)

    def weights(grp, *after):
        if grp not in forwards:
            forward(grp, *after)
        got, _ = _gather_wait("gather_" + grp + "_wait", *forwards[grp], *after)
        return {n: _join_shards(n, g) for n, g in zip(groups[grp], got)}

    weights.forward = forward

    in_flight, started = [], []

    def emit(grads):
        names = list(grads)
        state, token = _exchange_start("grads_" + names[0] + "_start", [_split_shards(n, grads[n]) for n in names],
                                       grads[names[0]])
        in_flight.append((names, state))
        return token

    def emit_small(grads):
        pack = _pack_small_grads(grads)
        state, token = _gather_start("grads_small_start", [pack], pack)
        in_flight.append((["small"], state))
        started.append(token)

    sp = {n: (given[n] if n in ("g_final", "hg_lb_logits") else _stored(n, given[n])[0]) for n in small_names}
    sp["g_mix"] = _after(sp["g_mix"], order)
    dx = _local_step(x, loss_target, weights, sp, emit, emit_small)

    res = {}
    after = [started[-1]]
    for names, state in in_flight:
        if names == ["small"]:
            state, forwarded = _gather_forward("grads_small_forward", state, *after)
            parts, sent = _gather_wait("grads_small_wait", state, forwarded)
        else:
            parts, sent = _exchange_wait("grads_" + names[0] + "_wait", state, *after)
        if names != ["small"]:
            after = []
            for n, part, mine in zip(names, parts, sent):
                raw = _adamw(part, mine, *[_stored(n, given[pre + n]) for pre in ("", "m_", "v_")], "adamw_" + n,
                             BIG[n][2])
                res[n] = [_stored(n, r) for r in raw]
                after.append(raw[0])
            continue
        sgiven = {pre + n: _stored(n, given[pre + n]) for pre in ("", "m_", "v_") for n in small_names}
        for pre in ("", "m_", "v_"):
            sgiven[pre + "g_final"] = given[pre + "g_final"].reshape(1, D_MODEL)
            sgiven[pre + "loss"] = jnp.zeros((1, 1), F32)
        raw = _adamw_small(parts[0], sent[0], list(SMALL_CORE), (0, SMALL_ROWS), sgiven, "adamw_small")
        res.update({n: [_stored(n, r) for r in raw[n]] for n in small_names})
        res["g_final"] = [r.reshape(D_MODEL) for r in raw["g_final"]]
        total_loss = raw["loss"][0].reshape(())
        after = [raw["s5_b_re"][0], raw["g_mix"][0]]
    return (total_loss, dx, *[res[n][0] for n in WEIGHT_ORDER], *[res[n][1] for n in WEIGHT_ORDER],
            *[res[n][2] for n in WEIGHT_ORDER], *[res[n][3] for n in WEIGHT_ORDER])
```

```python
import functools
import math

import jax
import jax.numpy as jnp
from jax import lax
from jax.experimental import pallas as pl
from jax.experimental.pallas import tpu as pltpu

F32 = jnp.float32
BF16 = jnp.bfloat16

D_MODEL = 1024
S5_WIDTH = 512
S5_GROUP = 16
S5_GROUPS = 32
S5_STATE = 64
S5_N = S5_GROUPS * S5_STATE
HG_WIDTH = 512
HG_HEAD = 128
HG_HEADS = 4
D_FF = 2816
CONV_W = 3
CHUNK = 64
N_IN = S5_WIDTH + 4 * HG_WIDTH + 2 * D_MODEL
EPS = 1e-6
QSCALE = HG_HEAD ** -0.5

ADAM_LR = 0.001
ADAM_B1 = 0.9
ADAM_B2 = 0.999
ADAM_EPS = 1e-08
ADAM_WD = 0.01
ADAM_STEP = 10

N_DEV = 8
V7X_VMEM_BYTES = 64 * 1024 * 1024
VMEM_LIMIT = V7X_VMEM_BYTES * 7 // 8
SUBLANES = 8
LANES = 128
PACK_W = 1024

WEIGHT_ORDER = ("g_mix", "w_in", "s5_a_re", "s5_a_im", "s5_log_dt", "s5_b_re", "s5_b_im", "s5_c_re", "s5_c_im",
                "s5_d", "w_glu", "b_glu", "hg_lb_logits", "hg_norm_gain", "w_pa", "w_pb", "w_out", "g_ffn",
                "w_up", "w_conv", "b_conv", "w_down", "g_final")


def _pcall(body, name, grid, in_specs, out_specs, out_shape, scratch=()):
    return pl.pallas_call(
        body, name=name, grid=grid, in_specs=in_specs, out_specs=out_specs, out_shape=out_shape,
        scratch_shapes=list(scratch),
        compiler_params=pltpu.CompilerParams(dimension_semantics=("arbitrary",) * len(grid),
                                             vmem_limit_bytes=VMEM_LIMIT),
    )


def _full(shape):
    return pl.BlockSpec(shape, lambda *_: (0,) * len(shape))


def _sds(shape, dtype=F32):
    return jax.ShapeDtypeStruct(shape, dtype)


def _dot(a, b):
    return jnp.dot(a.astype(BF16), b.astype(BF16), preferred_element_type=F32)


def _dot_nt(a, b):
    return lax.dot_general(a.astype(BF16), b.astype(BF16), (((1,), (1,)), ((), ())), preferred_element_type=F32)


def _dot_tn(a, b):
    return lax.dot_general(a.astype(BF16), b.astype(BF16), (((0,), (0,)), ((), ())), preferred_element_type=F32)


def _split(a):
    hi = a.astype(BF16)
    return hi, (a - hi.astype(F32)).astype(BF16)


def _hdot(a, b, dims=(((1,), (0,)), ((), ()))):
    (ah, al), (bh, bl) = _split(a), _split(b)
    dot = lambda p, q: lax.dot_general(p, q, dims, preferred_element_type=F32)
    return dot(ah, bh) + (dot(al, bh) + dot(ah, bl))


def _hdot_tn(a, b):
    return _hdot(a, b, (((0,), (0,)), ((), ())))


def _sigmoid(x):
    return jax.nn.sigmoid(x)


GELU_C = math.sqrt(2.0 / math.pi)
GELU_A = 0.044715


def _gelu(x):
    return 0.5 * x * (1.0 + jnp.tanh(GELU_C * (x + GELU_A * (x * x * x))))


def _gelu_grad(x):
    t = jnp.tanh(GELU_C * (x + GELU_A * (x * x * x)))
    return 0.5 * (1.0 + t) + 0.5 * x * (1.0 - t * t) * (GELU_C * (1.0 + 3.0 * GELU_A * x * x))


def _cumsum_rows(v, reverse=False):
    n = v.shape[0]
    row = lax.broadcasted_iota(jnp.int32, v.shape, 0)
    s = 1
    while s < n:
        if reverse:
            v = v + jnp.where(row < n - s, pltpu.roll(v, n - s, axis=0), 0.0)
        else:
            v = v + jnp.where(row >= s, pltpu.roll(v, s, axis=0), 0.0)
        s *= 2
    return v


def _token_tile(seq):
    return min(256, seq)


def _s5_coeffs(a_re, a_im, ldt):
    dt = jnp.exp(ldt)
    mag = jnp.exp(a_re * dt)
    ang = a_im * dt
    lb_re = mag * jnp.cos(ang)
    lb_im = mag * jnp.sin(ang)
    den = a_re * a_re + a_im * a_im
    n_re = lb_re - 1.0
    n_im = lb_im
    co_re = (n_re * a_re + n_im * a_im) / den
    co_im = (n_im * a_re - n_re * a_im) / den
    return lb_re, lb_im, co_re, co_im


GS, GSC = (S5_GROUPS, S5_STATE), (S5_GROUPS, S5_GROUP, S5_STATE)


def _params_fwd(a_re, a_im, ldt, bt_re, bt_im, logits):
    def body(are, aim, ld, bre, bim, lg, lr_o, li_o, bbr_o, bbi_o, lb_o):
        lr, li, co_re, co_im = _s5_coeffs(are[...], aim[...], ld[...])
        lr_o[...] = lr
        li_o[...] = li
        for g in range(S5_GROUPS):
            cr, ci = co_re[g:g + 1, :], co_im[g:g + 1, :]
            bbr_o[g] = cr * bre[g] - ci * bim[g]
            bbi_o[g] = cr * bim[g] + ci * bre[g]
        lb_o[...] = _sigmoid(lg[0:1, :] - lg[1:2, :])

    return _pcall(body, "params_fwd", (1,),
                  [_full(GS), _full(GS), _full((S5_GROUPS, 1)), _full(GSC), _full(GSC), _full((2, HG_WIDTH))],
                  [_full(GS), _full(GS), _full(GSC), _full(GSC), _full((1, HG_WIDTH))],
                  [_sds(GS), _sds(GS), _sds(GSC), _sds(GSC), _sds((1, HG_WIDTH))],
                  )(a_re, a_im, ldt, bt_re, bt_im, logits)


def _params_bwd(a_re, a_im, ldt, bt_re, bt_im, logits, dlr, dli, dbbr, dbbi, dlb):
    def body(are, aim, ld, bre, bim, lg, dlr_r, dli_r, dbbr_r, dbbi_r, dlb_r,
             dare_o, daim_o, dld_o, dbre_o, dbim_o, dlg_o, dcr_ref, dci_ref):
        (_, _, co_re, co_im), vjp = jax.vjp(_s5_coeffs, are[...], aim[...], ld[...])
        for g in range(S5_GROUPS):
            cr, ci = co_re[g:g + 1, :], co_im[g:g + 1, :]
            gr, gi, br, bi = dbbr_r[g], dbbi_r[g], bre[g], bim[g]
            dbre_o[g] = cr * gr + ci * gi
            dbim_o[g] = cr * gi - ci * gr
            dcr_ref[g:g + 1, :] = jnp.sum(gr * br + gi * bi, axis=0, keepdims=True)
            dci_ref[g:g + 1, :] = jnp.sum(gi * br - gr * bi, axis=0, keepdims=True)
        dare, daim, dld = vjp((dlr_r[...], dli_r[...], dcr_ref[...], dci_ref[...]))
        dare_o[...] = dare
        daim_o[...] = daim
        dld_o[...] = dld
        lb = _sigmoid(lg[0:1, :] - lg[1:2, :])
        d0 = dlb_r[...] * lb * (1.0 - lb)
        dlg_o[0:1, :] = d0
        dlg_o[1:2, :] = -d0

    return _pcall(body, "params_bwd", (1,),
                  [_full(GS), _full(GS), _full((S5_GROUPS, 1)), _full(GSC), _full(GSC), _full((2, HG_WIDTH)),
                   _full(GS), _full(GS), _full(GSC), _full(GSC), _full((1, HG_WIDTH))],
                  [_full(GS), _full(GS), _full((S5_GROUPS, 1)), _full(GSC), _full(GSC), _full((2, HG_WIDTH))],
                  [_sds(GS), _sds(GS), _sds((S5_GROUPS, 1)), _sds(GSC), _sds(GSC), _sds((2, HG_WIDTH))],
                  scratch=[pltpu.VMEM(GS, F32), pltpu.VMEM(GS, F32)],
                  )(a_re, a_im, ldt, bt_re, bt_im, logits, dlr, dli, dbbr, dbbi, dlb)


def _band_blocks(m):
    g, r, c = m.shape
    gb = g // S5_BANDS
    m4 = m.astype(BF16).reshape(S5_BANDS, gb, r, c)
    on_diag = jnp.eye(gb, dtype=bool)[None, :, None, :, None]
    return jnp.where(on_diag, m4[:, :, :, None, :], 0).reshape(S5_BANDS, gb * r, gb * c)


def _diag_blocks(band, r, c):
    g, nb = band.shape[0] // r, band.shape[1] // c
    on_diag = (jnp.arange(g) % nb)[:, None, None, None] == jnp.arange(nb)[None, None, :, None]
    return jnp.sum(jnp.where(on_diag, band.reshape(g, r, nb, c), 0.0), axis=2)


def _in_proj(x, g_mix, w_in, tm):
    t = x.shape[0]

    def body(x_ref, g_ref, w_ref, u_ref, za_ref, zh_ref, zg_ref):
        xv = x_ref[...]
        r = lax.rsqrt(jnp.mean(xv * xv, axis=-1, keepdims=True) + EPS)
        u = (xv * r * g_ref[...]).astype(BF16)
        u_ref[...] = u
        za_ref[...] = _dot_nt(u, w_ref[0:S5_WIDTH, :])
        zh_ref[...] = _dot_nt(u, w_ref[S5_WIDTH:S5_WIDTH + 4 * HG_WIDTH, :])
        zg_ref[...] = _dot_nt(u, w_ref[S5_WIDTH + 4 * HG_WIDTH:, :]).astype(BF16)

    row = lambda w: pl.BlockSpec((tm, w), lambda i: (i, 0))
    return _pcall(body, "in_proj", (t // tm,),
                  [row(D_MODEL), _full((1, D_MODEL)), _full((N_IN, D_MODEL))],
                  [row(D_MODEL), row(S5_WIDTH), row(4 * HG_WIDTH), row(2 * D_MODEL)],
                  [_sds((t, D_MODEL), BF16), _sds((t, S5_WIDTH)), _sds((t, 4 * HG_WIDTH)),
                   _sds((t, 2 * D_MODEL), BF16)],
                  )(x, g_mix, w_in)


S5_LANES = 512
S5_BANDS = 4


def _band(q):
    return (slice(q * S5_WIDTH // S5_BANDS, (q + 1) * S5_WIDTH // S5_BANDS),
            slice(q * S5_N // S5_BANDS, (q + 1) * S5_N // S5_BANDS))


def _im(st):
    return slice(S5_N + st.start, S5_N + st.stop)


SCAN_UNROLL = 8


def _complex_scan(buf_ref, lam_ref, st_ref, nb, ts, reverse):
    lanes = [slice(cc * S5_LANES, (cc + 1) * S5_LANES) for cc in range(S5_N // S5_LANES)]
    chains = [(b, re) for b in range(nb) for re in lanes]
    nch = len(chains)
    wr = {re.start: lam_ref[0:1, re] for re in lanes}
    wi = {re.start: -lam_ref[1:2, re] if reverse else lam_ref[1:2, re] for re in lanes}

    def block(ib, carry):
        vr, vi = list(carry[:nch]), list(carry[nch:])
        first = ts - SCAN_UNROLL - ib * SCAN_UNROLL if reverse else ib * SCAN_UNROLL
        first = pl.multiple_of(first, SCAN_UNROLL)
        for k in range(SCAN_UNROLL):
            row = pl.ds(first + (SCAN_UNROLL - 1 - k if reverse else k), 1)
            for c, (b, re) in enumerate(chains):
                nr = wr[re.start] * vr[c] - wi[re.start] * vi[c] + buf_ref[b, row, re]
                ni = wr[re.start] * vi[c] + wi[re.start] * vr[c] + buf_ref[b, row, _im(re)]
                buf_ref[b, row, re] = nr
                buf_ref[b, row, _im(re)] = ni
                vr[c], vi[c] = nr, ni
        return tuple(vr + vi)

    init = tuple(st_ref[b, 0:1, re] for b, re in chains) + tuple(st_ref[b, 1:2, re] for b, re in chains)
    last = lax.fori_loop(0, ts // SCAN_UNROLL, block, init)
    for c, (b, re) in enumerate(chains):
        st_ref[b, 0:1, re] = last[c]
        st_ref[b, 1:2, re] = last[nch + c]


BAND_CH = S5_WIDTH // S5_BANDS
BAND_ST = S5_N // S5_BANDS


def _s5_fwd(za, b_bands, lam, c_bands, dskip, nb, seq, ts):
    nts = seq // ts

    def body(za_ref, br_ref, bi_ref, lam_ref, cr_ref, ci_ref, d_ref, xs_ref, y_ref, buf_ref, st_ref):
        @pl.when(pl.program_id(0) == 0)
        def _():
            st_ref[...] = jnp.zeros_like(st_ref)

        for b in range(nb):
            zav = za_ref[b]
            for q in range(S5_BANDS):
                ch, st = _band(q)
                buf_ref[b, :, st] = _dot(zav[:, ch], br_ref[q])
                buf_ref[b, :, _im(st)] = _dot(zav[:, ch], bi_ref[q])
        _complex_scan(buf_ref, lam_ref, st_ref, nb, ts, reverse=False)
        for b in range(nb):
            zav = za_ref[b]
            xs_ref[b] = buf_ref[b].astype(BF16)
            for q in range(S5_BANDS):
                ch, st = _band(q)
                y_ref[b, :, ch] = (_dot(xs_ref[b, :, st], cr_ref[q]) + _dot(xs_ref[b, :, _im(st)], ci_ref[q])
                                   + d_ref[:, ch] * zav[:, ch])

    tok = lambda w: pl.BlockSpec((nb, ts, w), lambda j: (0, j, 0))
    to_st, to_ch = _full((S5_BANDS, BAND_CH, BAND_ST)), _full((S5_BANDS, BAND_ST, BAND_CH))
    return _pcall(body, "s5_fwd", (nts,),
                  [tok(S5_WIDTH), to_st, to_st, _full((2, S5_N)), to_ch, to_ch, _full((1, S5_WIDTH))],
                  [tok(2 * S5_N), tok(S5_WIDTH)],
                  [_sds((nb, seq, 2 * S5_N), BF16), _sds((nb, seq, S5_WIDTH))],
                  scratch=[pltpu.VMEM((nb, ts, 2 * S5_N), F32), pltpu.VMEM((nb, 2, S5_N), F32)],
                  )(za, *b_bands, lam, *c_bands, dskip)


def _hgrn_gates(zq, zf, lbh):
    sf = _sigmoid(zf)
    f = lbh + (1.0 - lbh) * sf
    sq = _sigmoid(zq)
    qa = zq * sq * QSCALE
    bc = _cumsum_rows(jnp.log(f))
    bm = bc[CHUNK // 2 - 1:CHUNK // 2, :]
    bl = bc[CHUNK - 1:CHUNK, :]
    return sf, f, sq, qa, bc, bm, bl


def _hgrn_fwd(zh, lb, nb, seq):
    nc = seq // CHUNK

    def body(zh_ref, lb_ref, o_ref, sts_ref, st_ref):
        @pl.when(pl.program_id(0) == 0)
        def _():
            st_ref[...] = jnp.zeros_like(st_ref)

        causal = (lax.broadcasted_iota(jnp.int32, (CHUNK, CHUNK), 0)
                  >= lax.broadcasted_iota(jnp.int32, (CHUNK, CHUNK), 1))
        for b in range(nb):
            for h in range(HG_HEADS):
                hs = slice(h * HG_HEAD, (h + 1) * HG_HEAD)
                zq = zh_ref[b, :, h * HG_HEAD:(h + 1) * HG_HEAD]
                zf = zh_ref[b, :, HG_WIDTH + h * HG_HEAD:HG_WIDTH + (h + 1) * HG_HEAD]
                zi = zh_ref[b, :, 2 * HG_WIDTH + h * HG_HEAD:2 * HG_WIDTH + (h + 1) * HG_HEAD]
                _, f, _, qa, bc, bm, bl = _hgrn_gates(zq, zf, lb_ref[:, hs])
                k = 1.0 - f
                qt = qa * jnp.exp(bc - bm)
                kt = k * jnp.exp(bm - bc)
                qb = qa * jnp.exp(bc)
                kd = k * jnp.exp(bl - bc)
                st = st_ref[b, h]
                sts_ref[b, 0, h] = st
                a = jnp.where(causal, _dot_nt(qt, kt), 0.0)
                o_ref[b, :, hs] = _dot(a, zi) + _dot_nt(qb, st)
                st_ref[b, h] = st * jnp.exp(bl) + _dot_tn(zi, kd)

    return _pcall(body, "hgrn_fwd", (nc,),
                  [pl.BlockSpec((nb, CHUNK, 4 * HG_WIDTH), lambda c: (0, c, 0)), _full((1, HG_WIDTH))],
                  [pl.BlockSpec((nb, CHUNK, HG_WIDTH), lambda c: (0, c, 0)),
                   pl.BlockSpec((nb, 1, HG_HEADS, HG_HEAD, HG_HEAD), lambda c: (0, c, 0, 0, 0))],
                  [_sds((nb, seq, HG_WIDTH)), _sds((nb, nc, HG_HEADS, HG_HEAD, HG_HEAD))],
                  scratch=[pltpu.VMEM((nb, HG_HEADS, HG_HEAD, HG_HEAD), F32)])(zh, lb)


def _head_rms(o):
    parts = []
    for h in range(HG_HEADS):
        oh = o[:, h * HG_HEAD:(h + 1) * HG_HEAD]
        r = lax.rsqrt(jnp.mean(oh * oh, axis=-1, keepdims=True) + EPS)
        parts.append(jnp.broadcast_to(r, oh.shape))
    return jnp.concatenate(parts, axis=1)


def _head_mean(v):
    parts = []
    for h in range(HG_HEADS):
        vh = v[:, h * HG_HEAD:(h + 1) * HG_HEAD]
        parts.append(jnp.broadcast_to(jnp.mean(vh, axis=-1, keepdims=True), vh.shape))
    return jnp.concatenate(parts, axis=1)


def _mix_fwd(x, y0, o, zh, zgt, w_glu, b_glu, gain, w_pa, w_pb, w_out, g_ffn, tm):
    t = x.shape[0]

    def body(x_ref, y0_ref, o_ref, zg_ref, zgt_ref, wglu_ref, bglu_ref, gain_ref, wpa_ref, wpb_ref, wout_ref,
             gffn_ref, x1_ref, u2_ref, pa_ref, pb_ref, ya2_ref, yb_ref):
        ya1 = _gelu(y0_ref[...])
        s = _sigmoid(_dot(ya1, wglu_ref[...]) + bglu_ref[...])
        ya2 = (ya1 * s).astype(BF16)
        ov = o_ref[...]
        zg = zg_ref[...]
        yb = (ov * _head_rms(ov) * gain_ref[...] * (zg * _sigmoid(zg))).astype(BF16)
        ya2_ref[...] = ya2
        yb_ref[...] = yb
        pa = jnp.dot(ya2, wpa_ref[...], preferred_element_type=F32)
        pb = jnp.dot(yb, wpb_ref[...], preferred_element_type=F32)
        pa_ref[...] = pa.astype(BF16)
        pb_ref[...] = pb.astype(BF16)
        m = (_sigmoid(zgt_ref[:, 0:D_MODEL].astype(F32)) * pa
             + _sigmoid(zgt_ref[:, D_MODEL:].astype(F32)) * pb)
        x1 = x_ref[...] + _dot(m, wout_ref[...])
        x1_ref[...] = x1
        r = lax.rsqrt(jnp.mean(x1 * x1, axis=-1, keepdims=True) + EPS)
        u2_ref[...] = (x1 * r * gffn_ref[...]).astype(BF16)

    row = lambda w: pl.BlockSpec((tm, w), lambda i: (i, 0))
    return _pcall(body, "mix_fwd", (t // tm,),
                  [row(D_MODEL), row(S5_WIDTH), row(HG_WIDTH), pl.BlockSpec((tm, HG_WIDTH), lambda i: (i, 3)),
                   row(2 * D_MODEL), _full((S5_WIDTH, S5_WIDTH)), _full((1, S5_WIDTH)), _full((1, HG_WIDTH)),
                   _full((S5_WIDTH, D_MODEL)), _full((HG_WIDTH, D_MODEL)), _full((D_MODEL, D_MODEL)),
                   _full((1, D_MODEL))],
                  [row(D_MODEL), row(D_MODEL), row(D_MODEL), row(D_MODEL), row(S5_WIDTH), row(HG_WIDTH)],
                  [_sds((t, D_MODEL)), _sds((t, D_MODEL), BF16), _sds((t, D_MODEL), BF16), _sds((t, D_MODEL), BF16),
                   _sds((t, S5_WIDTH), BF16), _sds((t, HG_WIDTH), BF16)],
                  )(x, y0, o, zh, zgt, w_glu, b_glu, gain, w_pa, w_pb, w_out, g_ffn)


FF_COLS = 256
FF_UP_TILE = 2 * D_FF // 4


def _ffn_up(u2, w_up, tm):
    t = u2.shape[0]
    n = 2 * D_FF

    def body(u_ref, w_ref, h_ref):
        h_ref[...] = _dot_nt(u_ref[...], w_ref[...]).astype(BF16)

    return _pcall(body, "ffn_up", (n // FF_UP_TILE, t // tm),
                  [pl.BlockSpec((tm, D_MODEL), lambda j, i: (i, 0)),
                   pl.BlockSpec((FF_UP_TILE, D_MODEL), lambda j, i: (j, 0))],
                  pl.BlockSpec((tm, FF_UP_TILE), lambda j, i: (i, j)),
                  _sds((t, n), BF16))(u2, w_up)


HALO = 16


def _shift_matrix(tm):
    r = lax.broadcasted_iota(jnp.int32, (tm, tm), 0)
    c = lax.broadcasted_iota(jnp.int32, (tm, tm), 1)
    return jnp.where(r == c + 1, 1.0, 0.0).astype(BF16)


def _conv_cols(h_ref, halo_ref, valid, wc_ref, bc_ref, c0):
    cs = slice(c0, c0 + FF_COLS)
    cur = h_ref[:, cs].astype(F32)
    prev = jnp.where(valid, halo_ref[:, cs].astype(F32), 0.0)
    full = jnp.concatenate([prev, cur], axis=0)
    h1 = pltpu.roll(full, 1, axis=0)[HALO:]
    h2 = pltpu.roll(full, 2, axis=0)[HALO:]
    return h2 * wc_ref[0:1, cs] + h1 * wc_ref[1:2, cs] + cur * wc_ref[2:3, cs] + bc_ref[:, cs]


def _ffn_down_loss(h, x1, tgt, w_conv, b_conv, w_down, g_final, seq, tm):
    t = h.shape[0]
    tps = seq // tm
    n = 2 * D_FF

    def body(h_ref, halo_ref, x1_ref, tgt_ref, wc_ref, bc_ref, wd_ref, gf_ref,
             hc_ref, a_ref, dx2_ref, dx2b_ref, loss_ref, dgf_ref):
        i = pl.program_id(0)

        @pl.when(i == 0)
        def _():
            loss_ref[...] = jnp.zeros_like(loss_ref)
            dgf_ref[...] = jnp.zeros_like(dgf_ref)

        valid = (i % tps) != 0
        x2 = x1_ref[...]
        for j in range(D_FF // FF_COLS):
            gate = _conv_cols(h_ref, halo_ref, valid, wc_ref, bc_ref, j * FF_COLS)
            val = _conv_cols(h_ref, halo_ref, valid, wc_ref, bc_ref, D_FF + j * FF_COLS)
            hc_ref[:, j * FF_COLS:(j + 1) * FF_COLS] = gate.astype(BF16)
            hc_ref[:, D_FF + j * FF_COLS:D_FF + (j + 1) * FF_COLS] = val.astype(BF16)
            a = (gate * _sigmoid(gate) * val).astype(BF16)
            a_ref[:, j * FF_COLS:(j + 1) * FF_COLS] = a
            x2 = x2 + jnp.dot(a, wd_ref[j * FF_COLS:(j + 1) * FF_COLS, :], preferred_element_type=F32)
        r = lax.rsqrt(jnp.mean(x2 * x2, axis=-1, keepdims=True) + EPS)
        xn = x2 * r
        g = gf_ref[...]
        e = xn * g - tgt_ref[...]
        loss_ref[...] += (0.5 / D_MODEL) * jnp.sum(e * e).reshape(1, 1)
        dy = e * (1.0 / D_MODEL)
        dgf_ref[...] += jnp.sum(dy * xn, axis=0, keepdims=True)
        dxn = dy * g
        dx2 = r * (dxn - xn * jnp.mean(dxn * xn, axis=-1, keepdims=True))
        dx2_ref[...] = dx2
        dx2b_ref[...] = dx2.astype(BF16)

    row = lambda w: pl.BlockSpec((tm, w), lambda i: (i, 0))
    halo = pl.BlockSpec((HALO, n), lambda i: (jnp.maximum(i * (tm // HALO) - 1, 0), 0))
    return _pcall(body, "ffn_down_loss", (t // tm,),
                  [row(n), halo, row(D_MODEL), row(D_MODEL), _full((CONV_W, n)), _full((1, n)),
                   _full((D_FF, D_MODEL)), _full((1, D_MODEL))],
                  [row(n), row(D_FF), row(D_MODEL), row(D_MODEL), _full((1, 1)), _full((1, D_MODEL))],
                  [_sds((t, n), BF16), _sds((t, D_FF), BF16), _sds((t, D_MODEL)), _sds((t, D_MODEL), BF16),
                   _sds((1, 1)), _sds((1, D_MODEL))],
                  )(h, h, x1, tgt, w_conv, b_conv, w_down, g_final)


def _wgrad(a, b, name, tn, out_dtype=F32, band=None, after=None):
    t, m = a.shape
    n = b.shape[1] if band is None else band
    nbands = 1 if band is None else b.shape[1] // band
    after = b if after is None else after

    def body(a_ref, b_ref, after_ref, o_ref):
        o_ref[...] = _dot_tn(a_ref[...], b_ref[...]).astype(out_dtype)

    return _pcall(body, name, (m // tn,),
                  [pl.BlockSpec((t, tn), lambda i: (0, i)), pl.BlockSpec((t, n), lambda i: (0, i % nbands)),
                   pl.BlockSpec(memory_space=pl.ANY)],
                  pl.BlockSpec((tn, n), lambda i: (i, 0)), _sds((m, n), out_dtype))(a, b, after)


def _ffn_bwd_act(dx2b, hc, w_down, tm):
    t = hc.shape[0]
    n = 2 * D_FF

    def body(dx2_ref, hc_ref, wd_ref, dhc_ref, dbc_ref):
        @pl.when(pl.program_id(0) == 0)
        def _():
            dbc_ref[...] = jnp.zeros_like(dbc_ref)

        dx2 = dx2_ref[...]
        for j in range(D_FF // FF_COLS):
            gs = slice(j * FF_COLS, (j + 1) * FF_COLS)
            vs = slice(D_FF + j * FF_COLS, D_FF + (j + 1) * FF_COLS)
            gate = hc_ref[:, gs].astype(F32)
            val = hc_ref[:, vs].astype(F32)
            da = _dot_nt(dx2, wd_ref[gs, :])
            sg = _sigmoid(gate)
            dgate = da * val * (sg * (1.0 + gate * (1.0 - sg)))
            dval = da * (gate * sg)
            dhc_ref[:, gs] = dgate.astype(BF16)
            dhc_ref[:, vs] = dval.astype(BF16)
            dbc_ref[:, gs] += jnp.sum(dgate, axis=0, keepdims=True)
            dbc_ref[:, vs] += jnp.sum(dval, axis=0, keepdims=True)

    row = lambda w: pl.BlockSpec((tm, w), lambda i: (i, 0))
    return _pcall(body, "ffn_bwd_act", (t // tm,),
                  [row(D_MODEL), row(n), _full((D_FF, D_MODEL))],
                  [row(n), _full((1, n))],
                  [_sds((t, n), BF16), _sds((1, n))],
                  )(dx2b, hc, w_down)


def _ffn_bwd_up(dhc, h, dx2, x1, w_conv, w_up, g_ffn, seq, tm):
    t = dhc.shape[0]
    tps = seq // tm
    n = 2 * D_FF
    last = t // HALO - 1

    def body(dhc_ref, halo_ref, h_ref, dx2_ref, x1_ref, wc_ref, wu_ref, gf_ref,
             dh_ref, dx1_ref, dx1b_ref, dgf_ref, dwc_ref):
        i = pl.program_id(0)

        @pl.when(i == 0)
        def _():
            dgf_ref[...] = jnp.zeros_like(dgf_ref)
            dwc_ref[...] = jnp.zeros_like(dwc_ref)

        valid = ((i + 1) % tps) != 0
        du2 = jnp.zeros((tm, D_MODEL), F32)
        for j in range(n // FF_COLS):
            cs = slice(j * FF_COLS, (j + 1) * FF_COLS)
            cur = dhc_ref[:, cs].astype(F32)
            nxt = jnp.where(valid, halo_ref[:, cs].astype(F32), 0.0)
            full = jnp.concatenate([cur, nxt], axis=0)
            d1 = pltpu.roll(full, tm + HALO - 1, axis=0)[:tm]
            d2 = pltpu.roll(full, tm + HALO - 2, axis=0)[:tm]
            dh = (cur * wc_ref[2:3, cs] + d1 * wc_ref[1:2, cs] + d2 * wc_ref[0:1, cs]).astype(BF16)
            dh_ref[:, cs] = dh
            du2 = du2 + _dot(dh, wu_ref[cs, :])
            hv = h_ref[:, cs].astype(F32)
            dwc_ref[0:1, cs] += jnp.sum(hv * d2, axis=0, keepdims=True)
            dwc_ref[1:2, cs] += jnp.sum(hv * d1, axis=0, keepdims=True)
            dwc_ref[2:3, cs] += jnp.sum(hv * cur, axis=0, keepdims=True)
        x1 = x1_ref[...]
        r = lax.rsqrt(jnp.mean(x1 * x1, axis=-1, keepdims=True) + EPS)
        xn = x1 * r
        dgf_ref[...] += jnp.sum(du2 * xn, axis=0, keepdims=True)
        dxn = du2 * gf_ref[...]
        dx1 = dx2_ref[...] + r * (dxn - xn * jnp.mean(dxn * xn, axis=-1, keepdims=True))
        dx1_ref[...] = dx1
        dx1b_ref[...] = dx1.astype(BF16)

    row = lambda w: pl.BlockSpec((tm, w), lambda i: (i, 0))
    halo = pl.BlockSpec((HALO, n), lambda i: (jnp.minimum((i + 1) * (tm // HALO), last), 0))
    return _pcall(body, "ffn_bwd_up", (t // tm,),
                  [row(n), halo, row(n), row(D_MODEL), row(D_MODEL), _full((CONV_W, n)), _full((n, D_MODEL)),
                   _full((1, D_MODEL))],
                  [row(n), row(D_MODEL), row(D_MODEL), _full((1, D_MODEL)), _full((CONV_W, n))],
                  [_sds((t, n), BF16), _sds((t, D_MODEL)), _sds((t, D_MODEL), BF16), _sds((1, D_MODEL)),
                   _sds((CONV_W, n))],
                  )(dhc, dhc, h, dx2, x1, w_conv, w_up, g_ffn)


def _mix_bwd(dx1, y0, o, zh, zgt, pa, pb, w_glu, b_glu, gain, w_pa, w_pb, w_out, tm):
    t = dx1.shape[0]

    def body(dx1_ref, y0_ref, o_ref, zg_ref, zgt_ref, pa_ref, pb_ref, wglu_ref, bglu_ref, gain_ref, wpa_ref,
             wpb_ref, wout_ref,
             dy0_ref, do_ref, dzg_ref, dzgt_ref, m_ref, dpa_ref, dpb_ref, ya1_ref, dpre_ref, dbglu_ref, dgain_ref):
        @pl.when(pl.program_id(0) == 0)
        def _():
            dbglu_ref[...] = jnp.zeros_like(dbglu_ref)
            dgain_ref[...] = jnp.zeros_like(dgain_ref)

        dm = _dot_nt(dx1_ref[...], wout_ref[...])
        sga = _sigmoid(zgt_ref[:, 0:D_MODEL].astype(F32))
        sgb = _sigmoid(zgt_ref[:, D_MODEL:].astype(F32))
        pa = pa_ref[...].astype(F32)
        pb = pb_ref[...].astype(F32)
        m_ref[...] = (sga * pa + sgb * pb).astype(BF16)
        dzgt_ref[:, 0:D_MODEL] = (dm * pa * sga * (1.0 - sga)).astype(BF16)
        dzgt_ref[:, D_MODEL:] = (dm * pb * sgb * (1.0 - sgb)).astype(BF16)
        dpa = (dm * sga).astype(BF16)
        dpb = (dm * sgb).astype(BF16)
        dpa_ref[...] = dpa
        dpb_ref[...] = dpb
        dya2 = _dot_nt(dpa, wpa_ref[...])
        dyb = _dot_nt(dpb, wpb_ref[...])
        y0 = y0_ref[...]
        ya1 = _gelu(y0)
        ya1_ref[...] = ya1.astype(BF16)
        s = _sigmoid(_dot(ya1, wglu_ref[...]) + bglu_ref[...])
        dpre = dya2 * ya1 * s * (1.0 - s)
        dpre_ref[...] = dpre.astype(BF16)
        dbglu_ref[...] += jnp.sum(dpre, axis=0, keepdims=True)
        dya1 = dya2 * s + _dot_nt(dpre, wglu_ref[...])
        dy0_ref[...] = dya1 * _gelu_grad(y0)
        ov = o_ref[...]
        zg = zg_ref[...]
        oh = ov * _head_rms(ov)
        on = oh * gain_ref[...]
        sz = _sigmoid(zg)
        dzg_ref[...] = (dyb * on * (sz * (1.0 + zg * (1.0 - sz)))).astype(BF16)
        don = dyb * (zg * sz)
        dgain_ref[...] += jnp.sum(don * oh, axis=0, keepdims=True)
        doh = don * gain_ref[...]
        do_ref[...] = _head_rms(ov) * (doh - oh * _head_mean(doh * oh))

    row = lambda w: pl.BlockSpec((tm, w), lambda i: (i, 0))
    return _pcall(body, "mix_bwd", (t // tm,),
                  [row(D_MODEL), row(S5_WIDTH), row(HG_WIDTH), pl.BlockSpec((tm, HG_WIDTH), lambda i: (i, 3)),
                   row(2 * D_MODEL), row(D_MODEL), row(D_MODEL), _full((S5_WIDTH, S5_WIDTH)), _full((1, S5_WIDTH)),
                   _full((1, HG_WIDTH)), _full((S5_WIDTH, D_MODEL)), _full((HG_WIDTH, D_MODEL)),
                   _full((D_MODEL, D_MODEL))],
                  [row(S5_WIDTH), row(HG_WIDTH), row(HG_WIDTH), row(2 * D_MODEL), row(D_MODEL), row(D_MODEL),
                   row(D_MODEL), row(S5_WIDTH), row(S5_WIDTH), _full((1, S5_WIDTH)), _full((1, HG_WIDTH))],
                  [_sds((t, S5_WIDTH)), _sds((t, HG_WIDTH)), _sds((t, HG_WIDTH), BF16), _sds((t, 2 * D_MODEL), BF16),
                   _sds((t, D_MODEL), BF16), _sds((t, D_MODEL), BF16), _sds((t, D_MODEL), BF16),
                   _sds((t, S5_WIDTH), BF16), _sds((t, S5_WIDTH), BF16), _sds((1, S5_WIDTH)), _sds((1, HG_WIDTH))],
                  )(dx1, y0, o, zh, zgt, pa, pb, w_glu, b_glu, gain, w_pa, w_pb, w_out)


def _s5_bwd(dy0, za, xs, c_bands, b_bands, lam, dskip, nb, seq, ts):
    nts = seq // ts

    def body(dy0_ref, za_ref, xs_ref, halo_ref, cr_ref, ci_ref, br_ref, bi_ref, lam_ref, d_ref,
             dza_ref, a_ref, dlam_ref, dd_ref, acc_ref, st_ref):
        j = pl.program_id(0)

        @pl.when(j == 0)
        def _():
            dlam_ref[...] = jnp.zeros_like(dlam_ref)
            dd_ref[...] = jnp.zeros_like(dd_ref)
            st_ref[...] = jnp.zeros_like(st_ref)

        for b in range(nb):
            dy0 = dy0_ref[b]
            for q in range(S5_BANDS):
                ch, st = _band(q)
                acc_ref[b, :, st] = _dot(dy0[:, ch], cr_ref[q])
                acc_ref[b, :, _im(st)] = _dot(dy0[:, ch], ci_ref[q])
        _complex_scan(acc_ref, lam_ref, st_ref, nb, ts, reverse=True)
        shift = _shift_matrix(ts)
        top = lax.broadcasted_iota(jnp.int32, (SUBLANES, S5_LANES), 0) == 0
        for b in range(nb):
            a_ref[b] = acc_ref[b].astype(BF16)
            first = jnp.where(j == nts - 1, 0.0, halo_ref[b, HALO - 1:HALO, :].astype(F32))

            def shifted(cols):
                xp = jnp.dot(shift, xs_ref[b, :, cols], preferred_element_type=F32)
                return jnp.concatenate([xp[:SUBLANES] + jnp.where(top, first[:, cols], 0.0), xp[SUBLANES:]], axis=0)

            for cc in range(S5_N // S5_LANES):
                re = slice(cc * S5_LANES, (cc + 1) * S5_LANES)
                ar, ai, xr, xi = acc_ref[b, :, re], acc_ref[b, :, _im(re)], shifted(re), shifted(_im(re))
                dlam_ref[0:1, re] += jnp.sum(ar * xr + ai * xi, axis=0, keepdims=True)
                dlam_ref[1:2, re] += jnp.sum(ai * xr - ar * xi, axis=0, keepdims=True)
            dy0 = dy0_ref[b]
            for q in range(S5_BANDS):
                ch, st = _band(q)
                dza_ref[b, :, ch] = (_dot(a_ref[b, :, st], br_ref[q]) + _dot(a_ref[b, :, _im(st)], bi_ref[q])
                                     + d_ref[:, ch] * dy0[:, ch]).astype(BF16)
            dd_ref[...] += jnp.sum(dy0 * za_ref[b], axis=0, keepdims=True)

    tile = lambda j: nts - 1 - j
    tok = lambda w: pl.BlockSpec((nb, ts, w), lambda j: (0, tile(j), 0))
    halo = pl.BlockSpec((nb, HALO, 2 * S5_N), lambda j: (0, jnp.maximum(tile(j) * (ts // HALO) - 1, 0), 0))
    to_st, to_ch = _full((S5_BANDS, BAND_CH, BAND_ST)), _full((S5_BANDS, BAND_ST, BAND_CH))
    return _pcall(body, "s5_bwd", (nts,),
                  [tok(S5_WIDTH), tok(S5_WIDTH), tok(2 * S5_N), halo, to_st, to_st, to_ch, to_ch,
                   _full((2, S5_N)), _full((1, S5_WIDTH))],
                  [tok(S5_WIDTH), tok(2 * S5_N), _full((2, S5_N)), _full((1, S5_WIDTH))],
                  [_sds((nb, seq, S5_WIDTH), BF16), _sds((nb, seq, 2 * S5_N), BF16), _sds((2, S5_N)),
                   _sds((1, S5_WIDTH))],
                  scratch=[pltpu.VMEM((nb, ts, 2 * S5_N), F32), pltpu.VMEM((nb, 2, S5_N), F32)],
                  )(dy0, za, xs, xs, *c_bands, *b_bands, lam, dskip)


def _hgrn_bwd(zh, do, sts, lb, nb, seq):
    nc = seq // CHUNK

    def body(zh_ref, do_ref, sts_ref, lb_ref, dz_ref, dlb_ref, dst_ref):
        @pl.when(pl.program_id(0) == 0)
        def _():
            dst_ref[...] = jnp.zeros_like(dst_ref)
            dlb_ref[...] = jnp.zeros_like(dlb_ref)

        row = lax.broadcasted_iota(jnp.int32, (CHUNK, CHUNK), 0)
        causal = row >= lax.broadcasted_iota(jnp.int32, (CHUNK, CHUNK), 1)
        last_row = lax.broadcasted_iota(jnp.int32, (CHUNK, HG_HEAD), 0) == CHUNK - 1
        for b in range(nb):
            for h in range(HG_HEADS):
                hs = slice(h * HG_HEAD, (h + 1) * HG_HEAD)
                zq = zh_ref[b, :, h * HG_HEAD:(h + 1) * HG_HEAD]
                zf = zh_ref[b, :, HG_WIDTH + h * HG_HEAD:HG_WIDTH + (h + 1) * HG_HEAD]
                zi = zh_ref[b, :, 2 * HG_WIDTH + h * HG_HEAD:2 * HG_WIDTH + (h + 1) * HG_HEAD]
                lbh = lb_ref[:, hs]
                sf, f, sq, qa, bc, bm, bl = _hgrn_gates(zq, zf, lbh)
                k = 1.0 - f
                e_qt = jnp.exp(bc - bm)
                e_kt = jnp.exp(bm - bc)
                e_b = jnp.exp(bc)
                e_kd = jnp.exp(bl - bc)
                e_l = jnp.exp(bl)
                qt, kt, qb, kd = qa * e_qt, k * e_kt, qa * e_b, k * e_kd
                a = jnp.where(causal, _dot_nt(qt, kt), 0.0)
                st = sts_ref[b, 0, h]
                dst = dst_ref[b, h]
                dov = do_ref[b, :, hs]
                da = jnp.where(causal, _dot_nt(dov, zi), 0.0)
                dqt = _hdot(da, kt)
                dkt = _hdot_tn(da, qt)
                dqb = _dot(dov, st)
                di = _dot_tn(a, dov) + _dot_nt(kd, dst)
                dkd = _dot(zi, dst)
                de_l = jnp.sum(dst * st, axis=0, keepdims=True)
                dst_ref[b, h] = dst * e_l + _dot_tn(dov, qb)
                dqa = dqt * e_qt + dqb * e_b
                dk = dkt * e_kt + dkd * e_kd
                dbl = jnp.sum(dkd * kd, axis=0, keepdims=True) + de_l * e_l
                db = dqt * qt - dkt * kt + dqb * qb - dkd * kd + jnp.where(last_row, dbl, 0.0)
                df = _cumsum_rows(db, reverse=True) / f - dk
                dzq = dqa * QSCALE * (sq * (1.0 + zq * (1.0 - sq)))
                dzf = df * (1.0 - lbh) * sf * (1.0 - sf)
                dz_ref[b, :, h * HG_HEAD:(h + 1) * HG_HEAD] = dzq.astype(BF16)
                dz_ref[b, :, HG_WIDTH + h * HG_HEAD:HG_WIDTH + (h + 1) * HG_HEAD] = dzf.astype(BF16)
                dz_ref[b, :, 2 * HG_WIDTH + h * HG_HEAD:2 * HG_WIDTH + (h + 1) * HG_HEAD] = di.astype(BF16)
                dlb_ref[:, hs] += jnp.sum(df * (1.0 - sf), axis=0, keepdims=True)

    rev = lambda c: nc - 1 - c
    return _pcall(body, "hgrn_bwd", (nc,),
                  [pl.BlockSpec((nb, CHUNK, 4 * HG_WIDTH), lambda c: (0, rev(c), 0)),
                   pl.BlockSpec((nb, CHUNK, HG_WIDTH), lambda c: (0, rev(c), 0)),
                   pl.BlockSpec((nb, 1, HG_HEADS, HG_HEAD, HG_HEAD), lambda c: (0, rev(c), 0, 0, 0)),
                   _full((1, HG_WIDTH))],
                  [pl.BlockSpec((nb, CHUNK, 3 * HG_WIDTH), lambda c: (0, rev(c), 0)), _full((1, HG_WIDTH))],
                  [_sds((nb, seq, 3 * HG_WIDTH), BF16), _sds((1, HG_WIDTH))],
                  scratch=[pltpu.VMEM((nb, HG_HEADS, HG_HEAD, HG_HEAD), F32)])(zh, do, sts, lb)


def _in_proj_bwd(dza, dzh, dzg, dzgt, dx1, x, g_mix, w_in, tm):
    t = x.shape[0]

    def body(dza_ref, dzh_ref, dzg_ref, dzgt_ref, dx1_ref, x_ref, g_ref, w_ref, dz_ref, dx_ref, dg_ref):
        @pl.when(pl.program_id(0) == 0)
        def _():
            dg_ref[...] = jnp.zeros_like(dg_ref)

        c1, c2, c3 = S5_WIDTH, S5_WIDTH + 3 * HG_WIDTH, S5_WIDTH + 4 * HG_WIDTH
        dz_ref[:, 0:c1] = dza_ref[...]
        dz_ref[:, c1:c2] = dzh_ref[...]
        dz_ref[:, c2:c3] = dzg_ref[...]
        dz_ref[:, c3:] = dzgt_ref[...]
        du = _dot(dz_ref[...], w_ref[...])
        xv = x_ref[...]
        r = lax.rsqrt(jnp.mean(xv * xv, axis=-1, keepdims=True) + EPS)
        xn = xv * r
        dg_ref[...] += jnp.sum(du * xn, axis=0, keepdims=True)
        dxn = du * g_ref[...]
        dx_ref[...] = dx1_ref[...] + r * (dxn - xn * jnp.mean(dxn * xn, axis=-1, keepdims=True))

    row = lambda w: pl.BlockSpec((tm, w), lambda i: (i, 0))
    return _pcall(body, "in_proj_bwd", (t // tm,),
                  [row(S5_WIDTH), row(3 * HG_WIDTH), row(HG_WIDTH), row(2 * D_MODEL), row(D_MODEL), row(D_MODEL),
                   _full((1, D_MODEL)), _full((N_IN, D_MODEL))],
                  [row(N_IN), row(D_MODEL), _full((1, D_MODEL))],
                  [_sds((t, N_IN), BF16), _sds((t, D_MODEL)), _sds((1, D_MODEL))],
                  )(dza, dzh, dzg, dzgt, dx1, x, g_mix, w_in)


def _after(value, token):
    return value + token[0, 0]


def _local_step(x3, tgt3, weights, sp, emit, emit_small):
    nb, seq, _ = x3.shape
    t = nb * seq
    tm = _token_tile(seq)
    x = x3.reshape(t, D_MODEL)
    tgt = tgt3.reshape(t, D_MODEL)
    row = lambda v: v.reshape(1, -1)

    a_re, a_im, b_re, b_im = sp["s5_a_re"], sp["s5_a_im"], sp["s5_b_re"], sp["s5_b_im"]
    ldt = sp["s5_log_dt"].reshape(S5_GROUPS, 1)
    lr, li, bb_re, bb_im, lb = _params_fwd(a_re, a_im, ldt, b_re, b_im, sp["hg_lb_logits"])
    lam = jnp.concatenate([lr.reshape(1, S5_N), li.reshape(1, S5_N)], axis=0)
    swap = lambda m: m.transpose(0, 2, 1)
    b_to_st = (_band_blocks(bb_re), _band_blocks(bb_im))
    b_to_ch = (_band_blocks(swap(bb_re)), _band_blocks(swap(bb_im)))
    c_to_ch = (_band_blocks(swap(sp["s5_c_re"])), _band_blocks(swap(-sp["s5_c_im"])))
    c_to_st = (_band_blocks(sp["s5_c_re"]), _band_blocks(-sp["s5_c_im"]))

    g_mix, g_ffn, g_final = row(sp["g_mix"]), row(sp["g_ffn"]), row(sp["g_final"])
    b_glu, gain, dskip, b_conv = row(sp["b_glu"]), row(sp["hg_norm_gain"]), row(sp["s5_d"]), row(sp["b_conv"])

    w_in = weights("in", lam, *b_to_st, *b_to_ch, *c_to_ch, *c_to_st)["w_in"]
    u, za, zh, zgt = _in_proj(x, g_mix, w_in, tm)
    seqs = lambda v: v.reshape(nb, seq, v.shape[-1])
    toks = lambda v: v.reshape(t, v.shape[-1])
    xs3, y0 = _s5_fwd(seqs(za), b_to_st, lam, c_to_ch, dskip, nb, seq, tm)
    xs, y0 = toks(xs3), toks(y0)
    o3, sts = _hgrn_fwd(zh.reshape(nb, seq, 4 * HG_WIDTH), lb, nb, seq)
    o = o3.reshape(t, HG_WIDTH)
    wm = weights("mix", y0, o3)
    weights.forward("ffn", wm["w_out"])
    x1, u2, pa, pb, ya2, yb = _mix_fwd(x, y0, o, zh, zgt, wm["w_glu"], b_glu, gain, wm["w_pa"], wm["w_pb"],
                                       wm["w_out"], g_ffn, tm)
    wf = weights("ffn", u2)
    h = _ffn_up(u2, wf["w_up"], min(4 * tm, t))
    hc, a, dx2, dx2b, loss, dg_final = _ffn_down_loss(h, x1, tgt, wf["w_conv"], b_conv, wf["w_down"], g_final,
                                                      seq, tm)

    wgrad = functools.partial(_wgrad, tn=256, out_dtype=BF16)
    dhc, db_conv = _ffn_bwd_act(dx2b, hc, wf["w_down"], tm)
    dw_down = wgrad(a, dx2b, "dw_down")
    dh, dx1, dx1b, dg_ffn, dw_conv = _ffn_bwd_up(dhc, h, dx2, x1, wf["w_conv"], wf["w_up"], g_ffn, seq, tm)
    sent = emit({"w_up": wgrad(dh, u2, "dw_up"), "w_conv": dw_conv, "w_down": dw_down})
    (dy0, do, dzg, dzgt, m, dpa, dpb, ya1, dpre, db_glu, dgain) = _mix_bwd(
        dx1b, y0, o, zh, zgt, pa, pb, wm["w_glu"], _after(b_glu, sent), gain, wm["w_pa"], wm["w_pb"], wm["w_out"], tm)
    sent = emit({"w_out": wgrad(m, dx1b, "dw_out"), "w_pa": wgrad(ya2, dpa, "dw_pa"),
                 "w_pb": wgrad(yb, dpb, "dw_pb"), "w_glu": wgrad(ya1, dpre, "dw_glu")})
    dzh3, dlb = _hgrn_bwd(zh.reshape(nb, seq, 4 * HG_WIDTH), do.reshape(nb, seq, HG_WIDTH), sts, _after(lb, sent),
                          nb, seq)
    dza, a_s5, dlam, dd = _s5_bwd(seqs(dy0), seqs(za), xs3, c_to_st, b_to_ch, lam, dskip, nb, seq, tm)
    dza, a_s5 = toks(dza), toks(a_s5)
    dz, dx, dg_mix = _in_proj_bwd(dza, dzh3.reshape(t, 3 * HG_WIDTH), dzg, dzgt, dx1, x, g_mix, w_in, tm)
    sent = emit({"w_in": wgrad(dz, u, "dw_in")})

    band = HG_HEAD
    dbb_band = _wgrad(a_s5, za, "dbb_s5", 512, band=band, after=sent)
    dc_band = _wgrad(xs, dy0, "dc_s5", 512, band=band, after=sent)
    dbb_re = swap(_diag_blocks(dbb_band[:S5_N], S5_STATE, S5_GROUP))
    dbb_im = swap(_diag_blocks(dbb_band[S5_N:], S5_STATE, S5_GROUP))
    dc_re = swap(_diag_blocks(dc_band[:S5_N], S5_STATE, S5_GROUP))
    dc_im = -swap(_diag_blocks(dc_band[S5_N:], S5_STATE, S5_GROUP))
    da_re, da_im, dldt, db_re, db_im, dlogits = _params_bwd(
        a_re, a_im, ldt, b_re, b_im, sp["hg_lb_logits"],
        dlam[0].reshape(S5_GROUPS, S5_STATE), dlam[1].reshape(S5_GROUPS, S5_STATE), dbb_re, dbb_im, dlb)
    emit_small({"g_mix": dg_mix, "s5_a_re": da_re, "s5_a_im": da_im, "s5_log_dt": dldt.reshape(1, S5_GROUPS),
                "s5_b_re": db_re, "s5_b_im": db_im, "s5_c_re": dc_re, "s5_c_im": dc_im, "s5_d": dd, "b_glu": db_glu,
                "hg_lb_logits": dlogits, "hg_norm_gain": dgain, "g_ffn": dg_ffn, "b_conv": db_conv,
                "g_final": dg_final, "loss": loss})
    return dx.reshape(nb, seq, D_MODEL)


def _mesh_peers():
    x, y, c = lax.axis_index("x"), lax.axis_index("y"), lax.axis_index("c")
    peers = []
    for k in range(1, N_DEV):
        px, py, pc = (1 - x if k & 4 else x), (1 - y if k & 2 else y), (1 - c if k & 1 else c)
        peers.append((k, (px, py, pc), 4 * px + 2 * py + pc))
    return 4 * x + 2 * y + c, peers


_HBM = pl.BlockSpec(memory_space=pltpu.HBM)
_SEM = pl.BlockSpec(memory_space=pltpu.SEMAPHORE)


_EFFECT = pltpu.CompilerParams(has_side_effects=pltpu.SideEffectType.DATAFLOW_SIDE_EFFECTING)


def _remote(src, dst, send_sem, recv_sem, to):
    return pltpu.make_async_remote_copy(src_ref=src, dst_ref=dst, send_sem=send_sem, recv_sem=recv_sem,
                                        device_id=to, device_id_type=pl.DeviceIdType.MESH)


def _exchange_start(name, arrays, after):
    n = len(arrays)
    srcs = [pltpu.with_memory_space_constraint(a, pltpu.HBM) for a in arrays]
    lands = [pltpu.with_memory_space_constraint(lax.empty(a.shape, a.dtype), pltpu.HBM) for a in arrays]
    copies = (N_DEV - 1) * n

    def body(*refs):
        src_refs, land_refs = refs[:n], refs[n:2 * n]
        send_sems, recv_sems, token = refs[2 * n + 1], refs[2 * n + 2], refs[-1]
        my_slab, peers = _mesh_peers()
        for k, peer, slab in peers:
            for i in range(n):
                s = (k - 1) * n + i
                _remote(src_refs[i].at[slab], land_refs[i].at[my_slab], send_sems.at[s], recv_sems.at[s], peer).start()
        token[...] = jnp.zeros_like(token)

    outs = pl.pallas_call(
        body, name=name,
        out_shape=(pltpu.SemaphoreType.DMA((copies,)), pltpu.SemaphoreType.DMA((copies,)),
                   *[pltpu.HBM(a.shape, a.dtype) for a in lands], _sds((SUBLANES, LANES))),
        in_specs=[_HBM] * (2 * n) + [pl.BlockSpec(memory_space=pl.ANY)],
        out_specs=(_SEM, _SEM, *[_HBM] * n, pl.BlockSpec(memory_space=pltpu.VMEM)),
        input_output_aliases={n + i: 2 + i for i in range(n)}, compiler_params=_EFFECT,
    )(*srcs, *lands, after)
    return (outs[0], outs[1], srcs, outs[2:2 + n]), outs[-1]


def _exchange_wait(name, state, *after):
    send_sems, recv_sems, srcs, lands = state
    n = len(lands)

    def body(*refs):
        src_refs, land_refs = refs[:n], refs[n:2 * n]
        send_ref, recv_ref = refs[2 * n], refs[2 * n + 1]
        _, peers = _mesh_peers()
        for k, peer, slab in peers:
            for i in range(n):
                s = (k - 1) * n + i
                copy = _remote(src_refs[i].at[slab], land_refs[i].at[slab], send_ref.at[s], recv_ref.at[s], peer)
                copy.wait_send()
                copy.wait_recv()

    outs = pl.pallas_call(
        body, name=name,
        out_shape=tuple(pltpu.HBM(a.shape, a.dtype) for a in lands),
        in_specs=[_HBM] * (2 * n) + [_SEM, _SEM] + [pl.BlockSpec(memory_space=pl.ANY)] * len(after),
        out_specs=tuple([_HBM] * n),
        input_output_aliases={n + i: i for i in range(n)}, compiler_params=_EFFECT,
    )(*srcs, *lands, send_sems, recv_sems, *after)
    return list(outs), list(srcs)


def _slab(pos):
    return 4 * pos[0] + 2 * pos[1] + pos[2]


def _chip_routes():
    x, y, c = lax.axis_index("x"), lax.axis_index("y"), lax.axis_index("c")
    return (x, y, c), (x, y, 1 - c), [(1 - x, y, c), (x, 1 - y, c), (1 - x, 1 - y, c)]


def _gather_start(name, arrays, after):
    n = len(arrays)
    me = 4 * lax.axis_index("x") + 2 * lax.axis_index("y") + lax.axis_index("c")
    srcs = [pltpu.with_memory_space_constraint(a, pltpu.HBM) for a in arrays]
    lands = [pltpu.with_memory_space_constraint(
        lax.dynamic_update_slice_in_dim(lax.empty((N_DEV,) + a.shape, a.dtype), a[None], me, 0), pltpu.HBM)
        for a in arrays]

    def body(*refs):
        src_refs, land_refs = refs[:n], refs[n:2 * n]
        send_sems, recv_sems, token = refs[2 * n + 1], refs[2 * n + 2], refs[-1]
        mine, sibling, chips = _chip_routes()
        for k, to in enumerate([sibling] + chips):
            for i in range(n):
                _remote(src_refs[i], land_refs[i].at[_slab(mine)], send_sems.at[k * n + i], recv_sems.at[k * n + i],
                        to).start()
        token[...] = jnp.zeros_like(token)

    outs = pl.pallas_call(
        body, name=name,
        out_shape=(pltpu.SemaphoreType.DMA((4 * n,)), pltpu.SemaphoreType.DMA((4 * n,)),
                   *[pltpu.HBM(a.shape, a.dtype) for a in lands], _sds((SUBLANES, LANES))),
        in_specs=[_HBM] * (2 * n) + [pl.BlockSpec(memory_space=pl.ANY)],
        out_specs=(_SEM, _SEM, *[_HBM] * n, pl.BlockSpec(memory_space=pltpu.VMEM)),
        input_output_aliases={n + i: 2 + i for i in range(n)}, compiler_params=_EFFECT,
    )(*srcs, *lands, after)
    return (outs[0], outs[1], srcs, outs[2:2 + n]), outs[-1]


def _gather_forward(name, state, *after):
    send_a, recv_a, srcs, lands = state
    n = len(lands)

    def body(*refs):
        land_refs, recv_a_ref = refs[:n], refs[n]
        send_b, recv_b = refs[n + 1 + len(after)], refs[n + 2 + len(after)]
        mine, sibling, chips = _chip_routes()
        for j, chip in enumerate(chips):
            for i in range(n):
                block = land_refs[i].at[_slab(chip)]
                _remote(block, block, send_b.at[j * n + i], recv_a_ref.at[(1 + j) * n + i], chip).wait_recv()
                _remote(block, block, send_b.at[j * n + i], recv_b.at[j * n + i], sibling).start()

    outs = pl.pallas_call(
        body, name=name,
        out_shape=(pltpu.SemaphoreType.DMA((3 * n,)), pltpu.SemaphoreType.DMA((3 * n,)),
                   *[pltpu.HBM(a.shape, a.dtype) for a in lands]),
        in_specs=[_HBM] * n + [_SEM] + [pl.BlockSpec(memory_space=pl.ANY)] * len(after),
        out_specs=(_SEM, _SEM, *[_HBM] * n),
        input_output_aliases={i: 2 + i for i in range(n)}, compiler_params=_EFFECT,
    )(*lands, recv_a, *after)
    return (send_a, recv_a, srcs, list(outs[2:])), (outs[0], outs[1])


def _gather_wait(name, state, forwarded, *after):
    send_a, recv_a, srcs, lands = state
    send_b, recv_b = forwarded
    n = len(lands)

    def body(*refs):
        src_refs, land_refs = refs[:n], refs[n:2 * n]
        sa, ra, sb, rb = refs[2 * n:2 * n + 4]
        mine, sibling, chips = _chip_routes()
        for i in range(n):
            for k, to in enumerate([sibling] + chips):
                _remote(src_refs[i], land_refs[i].at[_slab(mine)], sa.at[k * n + i], ra.at[k * n + i], to).wait_send()
            theirs = land_refs[i].at[_slab(sibling)]
            _remote(theirs, theirs, sa.at[i], ra.at[i], sibling).wait_recv()
            for j, chip in enumerate(chips):
                sent = land_refs[i].at[_slab(chip)]
                got = land_refs[i].at[_slab((chip[0], chip[1], sibling[2]))]
                _remote(sent, sent, sb.at[j * n + i], rb.at[j * n + i], sibling).wait_send()
                _remote(got, got, sb.at[j * n + i], rb.at[j * n + i], sibling).wait_recv()

    outs = pl.pallas_call(
        body, name=name,
        out_shape=tuple(pltpu.HBM(a.shape, a.dtype) for a in lands),
        in_specs=[_HBM] * (2 * n) + [_SEM] * 4 + [pl.BlockSpec(memory_space=pl.ANY)] * len(after),
        out_specs=tuple([_HBM] * n),
        input_output_aliases={n + i: i for i in range(n)}, compiler_params=_EFFECT,
    )(*srcs, *lands, send_a, recv_a, send_b, recv_b, *after)
    return list(outs), list(srcs)


def _join_cols(parts, name, tr):
    _, r, c = parts.shape

    def body(p_ref, o_ref):
        for j in range(N_DEV):
            o_ref[:, j * c:(j + 1) * c] = p_ref[j]

    return _pcall(body, name, (r // tr,), [pl.BlockSpec((N_DEV, tr, c), lambda i: (0, i, 0))],
                  pl.BlockSpec((tr, N_DEV * c), lambda i: (i, 0)), _sds((r, N_DEV * c), parts.dtype))(parts)


def _split_cols(full, name, tr):
    r, c = full.shape[0], full.shape[1] // N_DEV

    def body(f_ref, o_ref):
        for j in range(N_DEV):
            o_ref[j] = f_ref[:, j * c:(j + 1) * c]

    return _pcall(body, name, (r // tr,), [pl.BlockSpec((tr, N_DEV * c), lambda i: (i, 0))],
                  pl.BlockSpec((N_DEV, tr, c), lambda i: (0, i, 0)), _sds((N_DEV, r, c), full.dtype))(full)


def _my_slab():
    return (4 * lax.axis_index("x") + 2 * lax.axis_index("y") + lax.axis_index("c")).astype(jnp.int32).reshape(1)


def _adamw(parts, sent, w, m, v, name, tile):
    _, rows, cols = w.shape

    def body(me_ref, p_ref, s_ref, w_ref, m_ref, v_ref, g_out, d_out, m_out, v_out):
        me = me_ref[0]
        g = jnp.where(me == 0, s_ref[0], p_ref[0]).astype(F32)
        for k in range(1, N_DEV):
            g = g + jnp.where(me == k, s_ref[0], p_ref[k]).astype(F32)
        m1 = ADAM_B1 * m_ref[0] + (1.0 - ADAM_B1) * g
        v1 = ADAM_B2 * v_ref[0] + (1.0 - ADAM_B2) * (g * g)
        m_hat = m1 / (1.0 - ADAM_B1 ** ADAM_STEP)
        v_hat = v1 / (1.0 - ADAM_B2 ** ADAM_STEP)
        g_out[0] = g
        d_out[0] = -ADAM_LR * (m_hat / (jnp.sqrt(v_hat) + ADAM_EPS) + ADAM_WD * w_ref[0])
        m_out[0] = m1
        v_out[0] = v1

    row = pl.BlockSpec((1, tile, cols), lambda i, me: (0, i, 0))
    return pl.pallas_call(
        body, name=name, out_shape=[_sds((1, rows, cols))] * 4,
        grid_spec=pltpu.PrefetchScalarGridSpec(
            num_scalar_prefetch=1, grid=(rows // tile,),
            in_specs=[pl.BlockSpec((N_DEV, tile, cols), lambda i, me: (0, i, 0)),
                      pl.BlockSpec((1, tile, cols), lambda i, me: (me[0], i, 0)), row, row, row],
            out_specs=[row, row, row, row]),
        compiler_params=pltpu.CompilerParams(dimension_semantics=("arbitrary",), vmem_limit_bytes=VMEM_LIMIT),
    )(_my_slab(), parts, sent, w, m, v)


BIG = {
    "w_in": ((N_IN // N_DEV, D_MODEL), False, N_IN // N_DEV // 3),
    "w_glu": ((S5_WIDTH // N_DEV, S5_WIDTH), False, S5_WIDTH // N_DEV),
    "w_pa": ((S5_WIDTH, D_MODEL // N_DEV), True, S5_WIDTH),
    "w_pb": ((HG_WIDTH, D_MODEL // N_DEV), True, HG_WIDTH),
    "w_out": ((D_MODEL // N_DEV, D_MODEL), False, D_MODEL // N_DEV),
    "w_up": ((2 * D_FF // N_DEV, D_MODEL), False, 2 * D_FF // N_DEV // 4),
    "w_conv": ((CONV_W, 2 * D_FF // N_DEV), True, CONV_W),
    "w_down": ((D_FF // N_DEV, D_MODEL), False, D_FF // N_DEV // 2),
}
TRANSPOSED = ("w_in", "w_up", "s5_b_re", "s5_b_im")
UNALIGNED_COLS = ("w_conv",)


def _stored(n, arr):
    return jnp.swapaxes(arr, -1, -2) if n in TRANSPOSED else arr


def _join_shards(n, parts):
    (a, b), by_cols, _ = BIG[n]
    if not by_cols:
        return parts.reshape(N_DEV * a, b)
    if n in UNALIGNED_COLS:
        return _join_cols(parts, "join_" + n, min(a, 256))
    return parts.transpose(1, 0, 2).reshape(a, N_DEV * b)


def _split_shards(n, full):
    (a, b), by_cols, _ = BIG[n]
    if not by_cols:
        return full.reshape(N_DEV, a, b)
    if n in UNALIGNED_COLS:
        return _split_cols(full, "split_" + n, min(a, 256))
    return full.reshape(a, N_DEV, b).transpose(1, 0, 2)


SMALL_CORE = {
    "s5_b_re": GSC, "s5_b_im": GSC, "s5_c_re": GSC, "s5_c_im": GSC,
    "g_mix": (1, D_MODEL), "g_ffn": (1, D_MODEL), "g_final": (1, D_MODEL), "s5_d": (1, S5_WIDTH),
    "b_glu": (1, S5_WIDTH), "hg_norm_gain": (1, HG_WIDTH), "hg_lb_logits": (2, HG_WIDTH), "b_conv": (1, 2 * D_FF),
    "s5_log_dt": (1, S5_GROUPS), "s5_a_re": (S5_GROUPS, S5_STATE), "s5_a_im": (S5_GROUPS, S5_STATE), "loss": (1, 1),
}
BLOCK_ROWS = 32


def _small_rows():
    rows, r = {}, 0
    for n, core in SMALL_CORE.items():
        rows[n] = r
        r += BLOCK_ROWS if len(core) == 3 else -(-math.prod(core) // PACK_W)
    return rows, -(-r // SUBLANES) * SUBLANES


SMALL_ROW, SMALL_ROWS = _small_rows()


def _small_pieces(name):
    r, core = SMALL_ROW[name], SMALL_CORE[name]
    if len(core) == 3:
        return [((g, slice(None), slice(None)), slice(r + S5_GROUP * (g % 2), r + S5_GROUP * (g % 2 + 1)),
                 slice(S5_STATE * (g // 2), S5_STATE * (g // 2 + 1))) for g in range(S5_GROUPS)]
    pieces = []
    for i in range(core[0]):
        for c0 in range(0, core[1], PACK_W):
            w, flat = min(PACK_W, core[1] - c0), i * core[1] + c0
            pieces.append(((slice(i, i + 1), slice(c0, c0 + w)), slice(r + flat // PACK_W, r + flat // PACK_W + 1),
                           slice(flat % PACK_W, flat % PACK_W + w)))
    return pieces


def _core_index(ref, name, idx):
    return (0,) * (len(ref.shape) - len(SMALL_CORE[name])) + idx


def _pack_small_grads(grads):
    names = list(SMALL_CORE)

    def body(*refs):
        pack = refs[-1]
        pack[...] = jnp.zeros_like(pack)
        for ref, n in zip(refs, names):
            for idx, rows, lanes in _small_pieces(n):
                pack[rows, lanes] = ref[_core_index(ref, n, idx)]

    return _pcall(body, "pack_small_grads", (1,), [_full(grads[n].shape) for n in names],
                  _full((SMALL_ROWS, PACK_W)), _sds((SMALL_ROWS, PACK_W)))(*[grads[n] for n in names])


def _adamw_small(parts, sent, names, rows, given, name):
    lo, hi = rows
    k = len(names)
    shapes = [given[n].shape for n in names]

    def body(*refs):
        me, p_ref, s_ref, ins, outs = refs[0][0], refs[1], refs[2], refs[3:3 + 3 * k], refs[3 + 3 * k:3 + 7 * k]
        packs, results = refs[3 + 7 * k:6 + 7 * k], refs[6 + 7 * k:]
        for j, pack in enumerate(packs):
            pack[...] = jnp.zeros_like(pack)
            for ref, n in zip(ins[j * k:(j + 1) * k], names):
                for idx, prow, lanes in _small_pieces(n):
                    pack[slice(prow.start - lo, prow.stop - lo), lanes] = ref[_core_index(ref, n, idx)]
        mine = s_ref[lo:hi, :]
        g = jnp.where(me == 0, mine, p_ref[0, lo:hi, :])
        for d in range(1, N_DEV):
            g = g + jnp.where(me == d, mine, p_ref[d, lo:hi, :])
        m1 = ADAM_B1 * packs[1][...] + (1.0 - ADAM_B1) * g
        v1 = ADAM_B2 * packs[2][...] + (1.0 - ADAM_B2) * (g * g)
        m_hat = m1 / (1.0 - ADAM_B1 ** ADAM_STEP)
        v_hat = v1 / (1.0 - ADAM_B2 ** ADAM_STEP)
        results[0][...] = g
        results[1][...] = -ADAM_LR * (m_hat / (jnp.sqrt(v_hat) + ADAM_EPS) + ADAM_WD * packs[0][...])
        results[2][...] = m1
        results[3][...] = v1
        for j, result in enumerate(results):
            for ref, n in zip(outs[j * k:(j + 1) * k], names):
                for idx, prow, lanes in _small_pieces(n):
                    ref[_core_index(ref, n, idx)] = result[slice(prow.start - lo, prow.stop - lo), lanes]

    flat = _pcall(body, name, (1,),
                  [pl.BlockSpec(memory_space=pltpu.SMEM), _full(parts.shape), _full(sent.shape)]
                  + [_full(s) for s in shapes] * 3,
                  [_full(s) for s in shapes] * 4, [_sds(s) for s in shapes] * 4,
                  scratch=[pltpu.VMEM((hi - lo, PACK_W), F32)] * 7,
                  )(_my_slab(), parts, sent, *[given[pre + n] for pre in ("", "m_", "v_") for n in names])
    return {n: [flat[j * k + i] for j in range(4)] for i, n in enumerate(names)}


def kernel(x, g_mix, w_in, s5_a_re, s5_a_im, s5_log_dt, s5_b_re, s5_b_im, s5_c_re, s5_c_im, s5_d, w_glu, b_glu, hg_lb_logits, hg_norm_gain, w_pa, w_pb, w_out, g_ffn, w_up, w_conv, b_conv, w_down, g_final, loss_target, m_g_mix, m_w_in, m_s5_a_re, m_s5_a_im, m_s5_log_dt, m_s5_b_re, m_s5_b_im, m_s5_c_re, m_s5_c_im, m_s5_d, m_w_glu, m_b_glu, m_hg_lb_logits, m_hg_norm_gain, m_w_pa, m_w_pb, m_w_out, m_g_ffn, m_w_up, m_w_conv, m_b_conv, m_w_down, m_g_final, v_g_mix, v_w_in, v_s5_a_re, v_s5_a_im, v_s5_log_dt, v_s5_b_re, v_s5_b_im, v_s5_c_re, v_s5_c_im, v_s5_d, v_w_glu, v_b_glu, v_hg_lb_logits, v_hg_norm_gain, v_w_pa, v_w_pb, v_w_out, v_g_ffn, v_w_up, v_w_conv, v_b_conv, v_w_down, v_g_final):
    given = dict(locals())
    small_names = [n for n in SMALL_CORE if n != "loss"]

    pay = {n: given[n][0] if n == "w_conv" else _stored(n, given[n])[0].astype(BF16) for n in BIG}
    groups = {"in": ["w_in"], "mix": ["w_glu", "w_pa", "w_pb", "w_out"], "ffn": ["w_up", "w_down", "w_conv"]}
    gathers, order = {}, pay["w_in"]
    for grp, names in groups.items():
        gathers[grp], order = _gather_start("gather_" + grp + "_start", [pay[n] for n in names], order)

    forwards = {}

    def forward(grp, *after):
        if grp == "in":
            after = (*after, order)
        forwards[grp] = _gather_forward("gather_" + grp + "_forward", gathers[grp], *after)

    def weights(grp, *after):
        if grp not in forwards:
            forward(grp, *after)
        got, _ = _gather_wait("gather_" + grp + "_wait", *forwards[grp], *after)
        return {n: _join_shards(n, g) for n, g in zip(groups[grp], got)}

    weights.forward = forward

    in_flight, started = [], []

    def emit(grads):
        names = list(grads)
        state, token = _exchange_start("grads_" + names[0] + "_start", [_split_shards(n, grads[n]) for n in names],
                                       grads[names[0]])
        in_flight.append((names, state))
        return token

    def emit_small(grads):
        pack = _pack_small_grads(grads)
        state, token = _gather_start("grads_small_start", [pack], pack)
        in_flight.append((["small"], state))
        started.append(token)

    sp = {n: (given[n] if n in ("g_final", "hg_lb_logits") else _stored(n, given[n])[0]) for n in small_names}
    sp["g_mix"] = _after(sp["g_mix"], order)
    dx = _local_step(x, loss_target, weights, sp, emit, emit_small)

    res = {}
    after = [started[-1]]
    for names, state in in_flight:
        if names == ["small"]:
            state, forwarded = _gather_forward("grads_small_forward", state, *after)
            parts, sent = _gather_wait("grads_small_wait", state, forwarded)
        else:
            parts, sent = _exchange_wait("grads_" + names[0] + "_wait", state, *after)
        if names != ["small"]:
            after = []
            for n, part, mine in zip(names, parts, sent):
                raw = _adamw(part, mine, *[_stored(n, given[pre + n]) for pre in ("", "m_", "v_")], "adamw_" + n,
                             BIG[n][2])
                res[n] = [_stored(n, r) for r in raw]
                after.append(raw[0])
            continue
        sgiven = {pre + n: _stored(n, given[pre + n]) for pre in ("", "m_", "v_") for n in small_names}
        for pre in ("", "m_", "v_"):
            sgiven[pre + "g_final"] = given[pre + "g_final"].reshape(1, D_MODEL)
            sgiven[pre + "loss"] = jnp.zeros((1, 1), F32)
        raw = _adamw_small(parts[0], sent[0], list(SMALL_CORE), (0, SMALL_ROWS), sgiven, "adamw_small")
        res.update({n: [_stored(n, r) for r in raw[n]] for n in small_names})
        res["g_final"] = [r.reshape(D_MODEL) for r in raw["g_final"]]
        total_loss = raw["loss"][0].reshape(())
        after = [raw["s5_b_re"][0], raw["g_mix"][0]]
    return (total_loss, dx, *[res[n][0] for n in WEIGHT_ORDER], *[res[n][1] for n in WEIGHT_ORDER],
            *[res[n][2] for n in WEIGHT_ORDER], *[res[n][3] for n in WEIGHT_ORDER])
```

```python
import math

import jax
import jax.numpy as jnp
from jax import lax
from jax.experimental import pallas as pl
from jax.experimental.pallas import tpu as pltpu

F32 = jnp.float32
BF16 = jnp.bfloat16

D_MODEL = 1024
S5_WIDTH = 512
S5_GROUP = 16
S5_GROUPS = 32
S5_STATE = 64
S5_N = S5_GROUPS * S5_STATE
HG_WIDTH = 512
HG_HEAD = 128
HG_HEADS = 4
D_FF = 2816
CONV_W = 3
CHUNK = 64
N_IN = S5_WIDTH + 4 * HG_WIDTH + 2 * D_MODEL
EPS = 1e-6
QSCALE = HG_HEAD ** -0.5

ADAM_LR = 0.001
ADAM_B1 = 0.9
ADAM_B2 = 0.999
ADAM_EPS = 1e-08
ADAM_WD = 0.01
ADAM_STEP = 10

N_DEV = 8
V7X_VMEM_BYTES = 64 * 1024 * 1024
VMEM_LIMIT = V7X_VMEM_BYTES * 7 // 8
SUBLANES = 8
LANES = 128
PACK_W = 1024

WEIGHT_ORDER = ("g_mix", "w_in", "s5_a_re", "s5_a_im", "s5_log_dt", "s5_b_re", "s5_b_im", "s5_c_re", "s5_c_im",
                "s5_d", "w_glu", "b_glu", "hg_lb_logits", "hg_norm_gain", "w_pa", "w_pb", "w_out", "g_ffn",
                "w_up", "w_conv", "b_conv", "w_down", "g_final")


def _pcall(body, name, grid, in_specs, out_specs, out_shape, scratch=()):
    return pl.pallas_call(
        body, name=name, grid=grid, in_specs=in_specs, out_specs=out_specs, out_shape=out_shape,
        scratch_shapes=list(scratch),
        compiler_params=pltpu.CompilerParams(dimension_semantics=("arbitrary",) * len(grid),
                                             vmem_limit_bytes=VMEM_LIMIT),
    )


def _full(shape):
    return pl.BlockSpec(shape, lambda *_: (0,) * len(shape))


def _sds(shape, dtype=F32):
    return jax.ShapeDtypeStruct(shape, dtype)


def _dot(a, b):
    return jnp.dot(a.astype(BF16), b.astype(BF16), preferred_element_type=F32)


def _dot_nt(a, b):
    return lax.dot_general(a.astype(BF16), b.astype(BF16), (((1,), (1,)), ((), ())), preferred_element_type=F32)


def _dot_tn(a, b):
    return lax.dot_general(a.astype(BF16), b.astype(BF16), (((0,), (0,)), ((), ())), preferred_element_type=F32)


def _split(a):
    hi = a.astype(BF16)
    return hi, (a - hi.astype(F32)).astype(BF16)


def _hdot(a, b, dims=(((1,), (0,)), ((), ()))):
    (ah, al), (bh, bl) = _split(a), _split(b)
    dot = lambda p, q: lax.dot_general(p, q, dims, preferred_element_type=F32)
    return dot(ah, bh) + (dot(al, bh) + dot(ah, bl))


def _hdot_tn(a, b):
    return _hdot(a, b, (((0,), (0,)), ((), ())))


def _sigmoid(x):
    return jax.nn.sigmoid(x)


GELU_C = math.sqrt(2.0 / math.pi)
GELU_A = 0.044715


def _gelu(x):
    return 0.5 * x * (1.0 + jnp.tanh(GELU_C * (x + GELU_A * (x * x * x))))


def _gelu_grad(x):
    t = jnp.tanh(GELU_C * (x + GELU_A * (x * x * x)))
    return 0.5 * (1.0 + t) + 0.5 * x * (1.0 - t * t) * (GELU_C * (1.0 + 3.0 * GELU_A * x * x))


def _cumsum_rows(v, reverse=False):
    n = v.shape[0]
    row = lax.broadcasted_iota(jnp.int32, v.shape, 0)
    s = 1
    while s < n:
        if reverse:
            v = v + jnp.where(row < n - s, pltpu.roll(v, n - s, axis=0), 0.0)
        else:
            v = v + jnp.where(row >= s, pltpu.roll(v, s, axis=0), 0.0)
        s *= 2
    return v


def _token_tile(seq):
    return min(256, seq)


def _s5_coeffs(a_re, a_im, ldt):
    dt = jnp.exp(ldt)
    mag = jnp.exp(a_re * dt)
    ang = a_im * dt
    lb_re = mag * jnp.cos(ang)
    lb_im = mag * jnp.sin(ang)
    den = a_re * a_re + a_im * a_im
    n_re = lb_re - 1.0
    n_im = lb_im
    co_re = (n_re * a_re + n_im * a_im) / den
    co_im = (n_im * a_re - n_re * a_im) / den
    return lb_re, lb_im, co_re, co_im


GS, GSC = (S5_GROUPS, S5_STATE), (S5_GROUPS, S5_GROUP, S5_STATE)


def _params_fwd(a_re, a_im, ldt, bt_re, bt_im, logits):
    def body(are, aim, ld, bre, bim, lg, lr_o, li_o, bbr_o, bbi_o, lb_o):
        lr, li, co_re, co_im = _s5_coeffs(are[...], aim[...], ld[...])
        lr_o[...] = lr
        li_o[...] = li
        for g in range(S5_GROUPS):
            cr, ci = co_re[g:g + 1, :], co_im[g:g + 1, :]
            bbr_o[g] = cr * bre[g] - ci * bim[g]
            bbi_o[g] = cr * bim[g] + ci * bre[g]
        lb_o[...] = _sigmoid(lg[0:1, :] - lg[1:2, :])

    return _pcall(body, "params_fwd", (1,),
                  [_full(GS), _full(GS), _full((S5_GROUPS, 1)), _full(GSC), _full(GSC), _full((2, HG_WIDTH))],
                  [_full(GS), _full(GS), _full(GSC), _full(GSC), _full((1, HG_WIDTH))],
                  [_sds(GS), _sds(GS), _sds(GSC), _sds(GSC), _sds((1, HG_WIDTH))],
                  )(a_re, a_im, ldt, bt_re, bt_im, logits)


def _params_bwd(a_re, a_im, ldt, bt_re, bt_im, logits, dlr, dli, dbbr, dbbi, dlb):
    def body(are, aim, ld, bre, bim, lg, dlr_r, dli_r, dbbr_r, dbbi_r, dlb_r,
             dare_o, daim_o, dld_o, dbre_o, dbim_o, dlg_o, dcr_ref, dci_ref):
        (_, _, co_re, co_im), vjp = jax.vjp(_s5_coeffs, are[...], aim[...], ld[...])
        for g in range(S5_GROUPS):
            cr, ci = co_re[g:g + 1, :], co_im[g:g + 1, :]
            gr, gi, br, bi = dbbr_r[g], dbbi_r[g], bre[g], bim[g]
            dbre_o[g] = cr * gr + ci * gi
            dbim_o[g] = cr * gi - ci * gr
            dcr_ref[g:g + 1, :] = jnp.sum(gr * br + gi * bi, axis=0, keepdims=True)
            dci_ref[g:g + 1, :] = jnp.sum(gi * br - gr * bi, axis=0, keepdims=True)
        dare, daim, dld = vjp((dlr_r[...], dli_r[...], dcr_ref[...], dci_ref[...]))
        dare_o[...] = dare
        daim_o[...] = daim
        dld_o[...] = dld
        lb = _sigmoid(lg[0:1, :] - lg[1:2, :])
        d0 = dlb_r[...] * lb * (1.0 - lb)
        dlg_o[0:1, :] = d0
        dlg_o[1:2, :] = -d0

    return _pcall(body, "params_bwd", (1,),
                  [_full(GS), _full(GS), _full((S5_GROUPS, 1)), _full(GSC), _full(GSC), _full((2, HG_WIDTH)),
                   _full(GS), _full(GS), _full(GSC), _full(GSC), _full((1, HG_WIDTH))],
                  [_full(GS), _full(GS), _full((S5_GROUPS, 1)), _full(GSC), _full(GSC), _full((2, HG_WIDTH))],
                  [_sds(GS), _sds(GS), _sds((S5_GROUPS, 1)), _sds(GSC), _sds(GSC), _sds((2, HG_WIDTH))],
                  scratch=[pltpu.VMEM(GS, F32), pltpu.VMEM(GS, F32)],
                  )(a_re, a_im, ldt, bt_re, bt_im, logits, dlr, dli, dbbr, dbbi, dlb)


def _band_blocks(m):
    g, r, c = m.shape
    gb = g // S5_BANDS
    m4 = m.astype(BF16).reshape(S5_BANDS, gb, r, c)
    on_diag = jnp.eye(gb, dtype=bool)[None, :, None, :, None]
    return jnp.where(on_diag, m4[:, :, :, None, :], 0).reshape(S5_BANDS, gb * r, gb * c)


def _diag_blocks(band, r, c):
    g, nb = band.shape[0] // r, band.shape[1] // c
    on_diag = (jnp.arange(g) % nb)[:, None, None, None] == jnp.arange(nb)[None, None, :, None]
    return jnp.sum(jnp.where(on_diag, band.reshape(g, r, nb, c), 0.0), axis=2)


def _in_proj(x, g_mix, w_in, tm):
    t = x.shape[0]

    def body(x_ref, g_ref, w_ref, u_ref, za_ref, zh_ref, zg_ref):
        xv = x_ref[...]
        r = lax.rsqrt(jnp.mean(xv * xv, axis=-1, keepdims=True) + EPS)
        u = (xv * r * g_ref[...]).astype(BF16)
        u_ref[...] = u
        za_ref[...] = _dot_nt(u, w_ref[0:S5_WIDTH, :])
        zh_ref[...] = _dot_nt(u, w_ref[S5_WIDTH:S5_WIDTH + 4 * HG_WIDTH, :])
        zg_ref[...] = _dot_nt(u, w_ref[S5_WIDTH + 4 * HG_WIDTH:, :]).astype(BF16)

    row = lambda w: pl.BlockSpec((tm, w), lambda i: (i, 0))
    return _pcall(body, "in_proj", (t // tm,),
                  [row(D_MODEL), _full((1, D_MODEL)), _full((N_IN, D_MODEL))],
                  [row(D_MODEL), row(S5_WIDTH), row(4 * HG_WIDTH), row(2 * D_MODEL)],
                  [_sds((t, D_MODEL), BF16), _sds((t, S5_WIDTH)), _sds((t, 4 * HG_WIDTH)),
                   _sds((t, 2 * D_MODEL), BF16)],
                  )(x, g_mix, w_in)


S5_LANES = 512
S5_BANDS = 4


def _band(q):
    return (slice(q * S5_WIDTH // S5_BANDS, (q + 1) * S5_WIDTH // S5_BANDS),
            slice(q * S5_N // S5_BANDS, (q + 1) * S5_N // S5_BANDS))


def _im(st):
    return slice(S5_N + st.start, S5_N + st.stop)


SCAN_UNROLL = 8


def _complex_scan(buf_ref, lam_ref, st_ref, nb, ts, reverse):
    lanes = [slice(cc * S5_LANES, (cc + 1) * S5_LANES) for cc in range(S5_N // S5_LANES)]
    chains = [(b, re) for b in range(nb) for re in lanes]
    nch = len(chains)
    wr = {re.start: lam_ref[0:1, re] for re in lanes}
    wi = {re.start: -lam_ref[1:2, re] if reverse else lam_ref[1:2, re] for re in lanes}

    def block(ib, carry):
        vr, vi = list(carry[:nch]), list(carry[nch:])
        first = ts - SCAN_UNROLL - ib * SCAN_UNROLL if reverse else ib * SCAN_UNROLL
        first = pl.multiple_of(first, SCAN_UNROLL)
        for k in range(SCAN_UNROLL):
            row = pl.ds(first + (SCAN_UNROLL - 1 - k if reverse else k), 1)
            for c, (b, re) in enumerate(chains):
                nr = wr[re.start] * vr[c] - wi[re.start] * vi[c] + buf_ref[b, row, re]
                ni = wr[re.start] * vi[c] + wi[re.start] * vr[c] + buf_ref[b, row, _im(re)]
                buf_ref[b, row, re] = nr
                buf_ref[b, row, _im(re)] = ni
                vr[c], vi[c] = nr, ni
        return tuple(vr + vi)

    init = tuple(st_ref[b, 0:1, re] for b, re in chains) + tuple(st_ref[b, 1:2, re] for b, re in chains)
    last = lax.fori_loop(0, ts // SCAN_UNROLL, block, init)
    for c, (b, re) in enumerate(chains):
        st_ref[b, 0:1, re] = last[c]
        st_ref[b, 1:2, re] = last[nch + c]


BAND_CH = S5_WIDTH // S5_BANDS
BAND_ST = S5_N // S5_BANDS


def _s5_fwd(za, b_bands, lam, c_bands, dskip, nb, seq, ts):
    nts = seq // ts

    def body(za_ref, br_ref, bi_ref, lam_ref, cr_ref, ci_ref, d_ref, xs_ref, y_ref, buf_ref, st_ref):
        @pl.when(pl.program_id(0) == 0)
        def _():
            st_ref[...] = jnp.zeros_like(st_ref)

        for b in range(nb):
            zav = za_ref[b]
            for q in range(S5_BANDS):
                ch, st = _band(q)
                buf_ref[b, :, st] = _dot(zav[:, ch], br_ref[q])
                buf_ref[b, :, _im(st)] = _dot(zav[:, ch], bi_ref[q])
        _complex_scan(buf_ref, lam_ref, st_ref, nb, ts, reverse=False)
        for b in range(nb):
            zav = za_ref[b]
            xs_ref[b] = buf_ref[b].astype(BF16)
            for q in range(S5_BANDS):
                ch, st = _band(q)
                y_ref[b, :, ch] = (_dot(xs_ref[b, :, st], cr_ref[q]) + _dot(xs_ref[b, :, _im(st)], ci_ref[q])
                                   + d_ref[:, ch] * zav[:, ch])

    tok = lambda w: pl.BlockSpec((nb, ts, w), lambda j: (0, j, 0))
    to_st, to_ch = _full((S5_BANDS, BAND_CH, BAND_ST)), _full((S5_BANDS, BAND_ST, BAND_CH))
    return _pcall(body, "s5_fwd", (nts,),
                  [tok(S5_WIDTH), to_st, to_st, _full((2, S5_N)), to_ch, to_ch, _full((1, S5_WIDTH))],
                  [tok(2 * S5_N), tok(S5_WIDTH)],
                  [_sds((nb, seq, 2 * S5_N), BF16), _sds((nb, seq, S5_WIDTH))],
                  scratch=[pltpu.VMEM((nb, ts, 2 * S5_N), F32), pltpu.VMEM((nb, 2, S5_N), F32)],
                  )(za, *b_bands, lam, *c_bands, dskip)


def _hgrn_gates(zq, zf, lbh):
    sf = _sigmoid(zf)
    f = lbh + (1.0 - lbh) * sf
    sq = _sigmoid(zq)
    qa = zq * sq * QSCALE
    bc = _cumsum_rows(jnp.log(f))
    bm = bc[CHUNK // 2 - 1:CHUNK // 2, :]
    bl = bc[CHUNK - 1:CHUNK, :]
    return sf, f, sq, qa, bc, bm, bl


def _hgrn_fwd(zh, lb, nb, seq):
    nc = seq // CHUNK

    def body(zh_ref, lb_ref, o_ref, sts_ref, st_ref):
        @pl.when(pl.program_id(0) == 0)
        def _():
            st_ref[...] = jnp.zeros_like(st_ref)

        causal = (lax.broadcasted_iota(jnp.int32, (CHUNK, CHUNK), 0)
                  >= lax.broadcasted_iota(jnp.int32, (CHUNK, CHUNK), 1))
        for b in range(nb):
            for h in range(HG_HEADS):
                hs = slice(h * HG_HEAD, (h + 1) * HG_HEAD)
                zq = zh_ref[b, :, h * HG_HEAD:(h + 1) * HG_HEAD]
                zf = zh_ref[b, :, HG_WIDTH + h * HG_HEAD:HG_WIDTH + (h + 1) * HG_HEAD]
                zi = zh_ref[b, :, 2 * HG_WIDTH + h * HG_HEAD:2 * HG_WIDTH + (h + 1) * HG_HEAD]
                _, f, _, qa, bc, bm, bl = _hgrn_gates(zq, zf, lb_ref[:, hs])
                k = 1.0 - f
                qt = qa * jnp.exp(bc - bm)
                kt = k * jnp.exp(bm - bc)
                qb = qa * jnp.exp(bc)
                kd = k * jnp.exp(bl - bc)
                st = st_ref[b, h]
                sts_ref[b, 0, h] = st
                a = jnp.where(causal, _dot_nt(qt, kt), 0.0)
                o_ref[b, :, hs] = _dot(a, zi) + _dot_nt(qb, st)
                st_ref[b, h] = st * jnp.exp(bl) + _dot_tn(zi, kd)

    return _pcall(body, "hgrn_fwd", (nc,),
                  [pl.BlockSpec((nb, CHUNK, 4 * HG_WIDTH), lambda c: (0, c, 0)), _full((1, HG_WIDTH))],
                  [pl.BlockSpec((nb, CHUNK, HG_WIDTH), lambda c: (0, c, 0)),
                   pl.BlockSpec((nb, 1, HG_HEADS, HG_HEAD, HG_HEAD), lambda c: (0, c, 0, 0, 0))],
                  [_sds((nb, seq, HG_WIDTH)), _sds((nb, nc, HG_HEADS, HG_HEAD, HG_HEAD))],
                  scratch=[pltpu.VMEM((nb, HG_HEADS, HG_HEAD, HG_HEAD), F32)])(zh, lb)


def _head_rms(o):
    parts = []
    for h in range(HG_HEADS):
        oh = o[:, h * HG_HEAD:(h + 1) * HG_HEAD]
        r = lax.rsqrt(jnp.mean(oh * oh, axis=-1, keepdims=True) + EPS)
        parts.append(jnp.broadcast_to(r, oh.shape))
    return jnp.concatenate(parts, axis=1)


def _head_mean(v):
    parts = []
    for h in range(HG_HEADS):
        vh = v[:, h * HG_HEAD:(h + 1) * HG_HEAD]
        parts.append(jnp.broadcast_to(jnp.mean(vh, axis=-1, keepdims=True), vh.shape))
    return jnp.concatenate(parts, axis=1)


def _mix_fwd(x, y0, o, zh, zgt, w_glu, b_glu, gain, w_pa, w_pb, w_out, g_ffn, tm):
    t = x.shape[0]

    def body(x_ref, y0_ref, o_ref, zg_ref, zgt_ref, wglu_ref, bglu_ref, gain_ref, wpa_ref, wpb_ref, wout_ref,
             gffn_ref, x1_ref, u2_ref, pa_ref, pb_ref, ya2_ref, yb_ref):
        ya1 = _gelu(y0_ref[...])
        s = _sigmoid(_dot(ya1, wglu_ref[...]) + bglu_ref[...])
        ya2 = (ya1 * s).astype(BF16)
        ov = o_ref[...]
        zg = zg_ref[...]
        yb = (ov * _head_rms(ov) * gain_ref[...] * (zg * _sigmoid(zg))).astype(BF16)
        ya2_ref[...] = ya2
        yb_ref[...] = yb
        pa = jnp.dot(ya2, wpa_ref[...], preferred_element_type=F32)
        pb = jnp.dot(yb, wpb_ref[...], preferred_element_type=F32)
        pa_ref[...] = pa.astype(BF16)
        pb_ref[...] = pb.astype(BF16)
        m = (_sigmoid(zgt_ref[:, 0:D_MODEL].astype(F32)) * pa
             + _sigmoid(zgt_ref[:, D_MODEL:].astype(F32)) * pb)
        x1 = x_ref[...] + _dot(m, wout_ref[...])
        x1_ref[...] = x1
        r = lax.rsqrt(jnp.mean(x1 * x1, axis=-1, keepdims=True) + EPS)
        u2_ref[...] = (x1 * r * gffn_ref[...]).astype(BF16)

    row = lambda w: pl.BlockSpec((tm, w), lambda i: (i, 0))
    return _pcall(body, "mix_fwd", (t // tm,),
                  [row(D_MODEL), row(S5_WIDTH), row(HG_WIDTH), pl.BlockSpec((tm, HG_WIDTH), lambda i: (i, 3)),
                   row(2 * D_MODEL), _full((S5_WIDTH, S5_WIDTH)), _full((1, S5_WIDTH)), _full((1, HG_WIDTH)),
                   _full((S5_WIDTH, D_MODEL)), _full((HG_WIDTH, D_MODEL)), _full((D_MODEL, D_MODEL)),
                   _full((1, D_MODEL))],
                  [row(D_MODEL), row(D_MODEL), row(D_MODEL), row(D_MODEL), row(S5_WIDTH), row(HG_WIDTH)],
                  [_sds((t, D_MODEL)), _sds((t, D_MODEL), BF16), _sds((t, D_MODEL), BF16), _sds((t, D_MODEL), BF16),
                   _sds((t, S5_WIDTH), BF16), _sds((t, HG_WIDTH), BF16)],
                  )(x, y0, o, zh, zgt, w_glu, b_glu, gain, w_pa, w_pb, w_out, g_ffn)


FF_COLS = 256
FF_UP_TILE = 2 * D_FF // 2


def _ffn_up(u2, w_up, tm):
    t = u2.shape[0]
    n = 2 * D_FF

    def body(u_ref, w_ref, h_ref):
        h_ref[...] = _dot_nt(u_ref[...], w_ref[...]).astype(BF16)

    return _pcall(body, "ffn_up", (n // FF_UP_TILE, t // tm),
                  [pl.BlockSpec((tm, D_MODEL), lambda j, i: (i, 0)),
                   pl.BlockSpec((FF_UP_TILE, D_MODEL), lambda j, i: (j, 0))],
                  pl.BlockSpec((tm, FF_UP_TILE), lambda j, i: (i, j)),
                  _sds((t, n), BF16))(u2, w_up)


HALO = 16


def _shift_matrix(tm):
    r = lax.broadcasted_iota(jnp.int32, (tm, tm), 0)
    c = lax.broadcasted_iota(jnp.int32, (tm, tm), 1)
    return jnp.where(r == c + 1, 1.0, 0.0).astype(BF16)


def _conv_cols(h_ref, halo_ref, valid, wc_ref, bc_ref, c0):
    cs = slice(c0, c0 + FF_COLS)
    cur = h_ref[:, cs].astype(F32)
    prev = jnp.where(valid, halo_ref[:, cs].astype(F32), 0.0)
    full = jnp.concatenate([prev, cur], axis=0)
    h1 = pltpu.roll(full, 1, axis=0)[HALO:]
    h2 = pltpu.roll(full, 2, axis=0)[HALO:]
    return h2 * wc_ref[0:1, cs] + h1 * wc_ref[1:2, cs] + cur * wc_ref[2:3, cs] + bc_ref[:, cs]


def _ffn_down_loss(h, x1, tgt, w_conv, b_conv, w_down, g_final, seq, tm):
    t = h.shape[0]
    tps = seq // tm
    n = 2 * D_FF

    def body(h_ref, halo_ref, x1_ref, tgt_ref, wc_ref, bc_ref, wd_ref, gf_ref,
             hc_ref, a_ref, dx2_ref, dx2b_ref, loss_ref, dgf_ref):
        i = pl.program_id(0)

        @pl.when(i == 0)
        def _():
            loss_ref[...] = jnp.zeros_like(loss_ref)
            dgf_ref[...] = jnp.zeros_like(dgf_ref)

        valid = (i % tps) != 0
        x2 = x1_ref[...]
        for j in range(D_FF // FF_COLS):
            gate = _conv_cols(h_ref, halo_ref, valid, wc_ref, bc_ref, j * FF_COLS)
            val = _conv_cols(h_ref, halo_ref, valid, wc_ref, bc_ref, D_FF + j * FF_COLS)
            hc_ref[:, j * FF_COLS:(j + 1) * FF_COLS] = gate.astype(BF16)
            hc_ref[:, D_FF + j * FF_COLS:D_FF + (j + 1) * FF_COLS] = val.astype(BF16)
            a = (gate * _sigmoid(gate) * val).astype(BF16)
            a_ref[:, j * FF_COLS:(j + 1) * FF_COLS] = a
            x2 = x2 + jnp.dot(a, wd_ref[j * FF_COLS:(j + 1) * FF_COLS, :], preferred_element_type=F32)
        r = lax.rsqrt(jnp.mean(x2 * x2, axis=-1, keepdims=True) + EPS)
        xn = x2 * r
        g = gf_ref[...]
        e = xn * g - tgt_ref[...]
        loss_ref[...] += (0.5 / D_MODEL) * jnp.sum(e * e).reshape(1, 1)
        dy = e * (1.0 / D_MODEL)
        dgf_ref[...] += jnp.sum(dy * xn, axis=0, keepdims=True)
        dxn = dy * g
        dx2 = r * (dxn - xn * jnp.mean(dxn * xn, axis=-1, keepdims=True))
        dx2_ref[...] = dx2
        dx2b_ref[...] = dx2.astype(BF16)

    row = lambda w: pl.BlockSpec((tm, w), lambda i: (i, 0))
    halo = pl.BlockSpec((HALO, n), lambda i: (jnp.maximum(i * (tm // HALO) - 1, 0), 0))
    return _pcall(body, "ffn_down_loss", (t // tm,),
                  [row(n), halo, row(D_MODEL), row(D_MODEL), _full((CONV_W, n)), _full((1, n)),
                   _full((D_FF, D_MODEL)), _full((1, D_MODEL))],
                  [row(n), row(D_FF), row(D_MODEL), row(D_MODEL), _full((1, 1)), _full((1, D_MODEL))],
                  [_sds((t, n), BF16), _sds((t, D_FF), BF16), _sds((t, D_MODEL)), _sds((t, D_MODEL), BF16),
                   _sds((1, 1)), _sds((1, D_MODEL))],
                  )(h, h, x1, tgt, w_conv, b_conv, w_down, g_final)


def _wgrad(a, b, name, tn, out_dtype=F32, band=None, after=None):
    t, m = a.shape
    n = b.shape[1] if band is None else band
    nbands = 1 if band is None else b.shape[1] // band
    after = b if after is None else after

    def body(a_ref, b_ref, after_ref, o_ref):
        o_ref[...] = _dot_tn(a_ref[...], b_ref[...]).astype(out_dtype)

    return _pcall(body, name, (m // tn,),
                  [pl.BlockSpec((t, tn), lambda i: (0, i)), pl.BlockSpec((t, n), lambda i: (0, i % nbands)),
                   pl.BlockSpec(memory_space=pl.ANY)],
                  pl.BlockSpec((tn, n), lambda i: (i, 0)), _sds((m, n), out_dtype))(a, b, after)


def _ffn_bwd_act(dx2b, hc, w_down, tm):
    t = hc.shape[0]
    n = 2 * D_FF

    def body(dx2_ref, hc_ref, wd_ref, dhc_ref, dbc_ref):
        @pl.when(pl.program_id(0) == 0)
        def _():
            dbc_ref[...] = jnp.zeros_like(dbc_ref)

        dx2 = dx2_ref[...]
        for j in range(D_FF // FF_COLS):
            gs = slice(j * FF_COLS, (j + 1) * FF_COLS)
            vs = slice(D_FF + j * FF_COLS, D_FF + (j + 1) * FF_COLS)
            gate = hc_ref[:, gs].astype(F32)
            val = hc_ref[:, vs].astype(F32)
            da = _dot_nt(dx2, wd_ref[gs, :])
            sg = _sigmoid(gate)
            dgate = da * val * (sg * (1.0 + gate * (1.0 - sg)))
            dval = da * (gate * sg)
            dhc_ref[:, gs] = dgate.astype(BF16)
            dhc_ref[:, vs] = dval.astype(BF16)
            dbc_ref[:, gs] += jnp.sum(dgate, axis=0, keepdims=True)
            dbc_ref[:, vs] += jnp.sum(dval, axis=0, keepdims=True)

    row = lambda w: pl.BlockSpec((tm, w), lambda i: (i, 0))
    return _pcall(body, "ffn_bwd_act", (t // tm,),
                  [row(D_MODEL), row(n), _full((D_FF, D_MODEL))],
                  [row(n), _full((1, n))],
                  [_sds((t, n), BF16), _sds((1, n))],
                  )(dx2b, hc, w_down)


def _ffn_bwd_up(dhc, h, dx2, x1, w_conv, w_up, g_ffn, seq, tm):
    t = dhc.shape[0]
    tps = seq // tm
    n = 2 * D_FF
    last = t // HALO - 1

    def body(dhc_ref, halo_ref, h_ref, dx2_ref, x1_ref, wc_ref, wu_ref, gf_ref,
             dh_ref, dx1_ref, dx1b_ref, dgf_ref, dwc_ref):
        i = pl.program_id(0)

        @pl.when(i == 0)
        def _():
            dgf_ref[...] = jnp.zeros_like(dgf_ref)
            dwc_ref[...] = jnp.zeros_like(dwc_ref)

        valid = ((i + 1) % tps) != 0
        du2 = jnp.zeros((tm, D_MODEL), F32)
        for j in range(n // FF_COLS):
            cs = slice(j * FF_COLS, (j + 1) * FF_COLS)
            cur = dhc_ref[:, cs].astype(F32)
            nxt = jnp.where(valid, halo_ref[:, cs].astype(F32), 0.0)
            full = jnp.concatenate([cur, nxt], axis=0)
            d1 = pltpu.roll(full, tm + HALO - 1, axis=0)[:tm]
            d2 = pltpu.roll(full, tm + HALO - 2, axis=0)[:tm]
            dh = (cur * wc_ref[2:3, cs] + d1 * wc_ref[1:2, cs] + d2 * wc_ref[0:1, cs]).astype(BF16)
            dh_ref[:, cs] = dh
            du2 = du2 + _dot(dh, wu_ref[cs, :])
            hv = h_ref[:, cs].astype(F32)
            dwc_ref[0:1, cs] += jnp.sum(hv * d2, axis=0, keepdims=True)
            dwc_ref[1:2, cs] += jnp.sum(hv * d1, axis=0, keepdims=True)
            dwc_ref[2:3, cs] += jnp.sum(hv * cur, axis=0, keepdims=True)
        x1 = x1_ref[...]
        r = lax.rsqrt(jnp.mean(x1 * x1, axis=-1, keepdims=True) + EPS)
        xn = x1 * r
        dgf_ref[...] += jnp.sum(du2 * xn, axis=0, keepdims=True)
        dxn = du2 * gf_ref[...]
        dx1 = dx2_ref[...] + r * (dxn - xn * jnp.mean(dxn * xn, axis=-1, keepdims=True))
        dx1_ref[...] = dx1
        dx1b_ref[...] = dx1.astype(BF16)

    row = lambda w: pl.BlockSpec((tm, w), lambda i: (i, 0))
    halo = pl.BlockSpec((HALO, n), lambda i: (jnp.minimum((i + 1) * (tm // HALO), last), 0))
    return _pcall(body, "ffn_bwd_up", (t // tm,),
                  [row(n), halo, row(n), row(D_MODEL), row(D_MODEL), _full((CONV_W, n)), _full((n, D_MODEL)),
                   _full((1, D_MODEL))],
                  [row(n), row(D_MODEL), row(D_MODEL), _full((1, D_MODEL)), _full((CONV_W, n))],
                  [_sds((t, n), BF16), _sds((t, D_MODEL)), _sds((t, D_MODEL), BF16), _sds((1, D_MODEL)),
                   _sds((CONV_W, n))],
                  )(dhc, dhc, h, dx2, x1, w_conv, w_up, g_ffn)


def _mix_bwd(dx1, y0, o, zh, zgt, pa, pb, w_glu, b_glu, gain, w_pa, w_pb, w_out, tm):
    t = dx1.shape[0]

    def body(dx1_ref, y0_ref, o_ref, zg_ref, zgt_ref, pa_ref, pb_ref, wglu_ref, bglu_ref, gain_ref, wpa_ref,
             wpb_ref, wout_ref,
             dy0_ref, do_ref, dzg_ref, dzgt_ref, m_ref, dpa_ref, dpb_ref, ya1_ref, dpre_ref, dbglu_ref, dgain_ref):
        @pl.when(pl.program_id(0) == 0)
        def _():
            dbglu_ref[...] = jnp.zeros_like(dbglu_ref)
            dgain_ref[...] = jnp.zeros_like(dgain_ref)

        dm = _dot_nt(dx1_ref[...], wout_ref[...])
        sga = _sigmoid(zgt_ref[:, 0:D_MODEL].astype(F32))
        sgb = _sigmoid(zgt_ref[:, D_MODEL:].astype(F32))
        pa = pa_ref[...].astype(F32)
        pb = pb_ref[...].astype(F32)
        m_ref[...] = (sga * pa + sgb * pb).astype(BF16)
        dzgt_ref[:, 0:D_MODEL] = (dm * pa * sga * (1.0 - sga)).astype(BF16)
        dzgt_ref[:, D_MODEL:] = (dm * pb * sgb * (1.0 - sgb)).astype(BF16)
        dpa = (dm * sga).astype(BF16)
        dpb = (dm * sgb).astype(BF16)
        dpa_ref[...] = dpa
        dpb_ref[...] = dpb
        dya2 = _dot_nt(dpa, wpa_ref[...])
        dyb = _dot_nt(dpb, wpb_ref[...])
        y0 = y0_ref[...]
        ya1 = _gelu(y0)
        ya1_ref[...] = ya1.astype(BF16)
        s = _sigmoid(_dot(ya1, wglu_ref[...]) + bglu_ref[...])
        dpre = dya2 * ya1 * s * (1.0 - s)
        dpre_ref[...] = dpre.astype(BF16)
        dbglu_ref[...] += jnp.sum(dpre, axis=0, keepdims=True)
        dya1 = dya2 * s + _dot_nt(dpre, wglu_ref[...])
        dy0_ref[...] = dya1 * _gelu_grad(y0)
        ov = o_ref[...]
        zg = zg_ref[...]
        oh = ov * _head_rms(ov)
        on = oh * gain_ref[...]
        sz = _sigmoid(zg)
        dzg_ref[...] = (dyb * on * (sz * (1.0 + zg * (1.0 - sz)))).astype(BF16)
        don = dyb * (zg * sz)
        dgain_ref[...] += jnp.sum(don * oh, axis=0, keepdims=True)
        doh = don * gain_ref[...]
        do_ref[...] = _head_rms(ov) * (doh - oh * _head_mean(doh * oh))

    row = lambda w: pl.BlockSpec((tm, w), lambda i: (i, 0))
    return _pcall(body, "mix_bwd", (t // tm,),
                  [row(D_MODEL), row(S5_WIDTH), row(HG_WIDTH), pl.BlockSpec((tm, HG_WIDTH), lambda i: (i, 3)),
                   row(2 * D_MODEL), row(D_MODEL), row(D_MODEL), _full((S5_WIDTH, S5_WIDTH)), _full((1, S5_WIDTH)),
                   _full((1, HG_WIDTH)), _full((S5_WIDTH, D_MODEL)), _full((HG_WIDTH, D_MODEL)),
                   _full((D_MODEL, D_MODEL))],
                  [row(S5_WIDTH), row(HG_WIDTH), row(HG_WIDTH), row(2 * D_MODEL), row(D_MODEL), row(D_MODEL),
                   row(D_MODEL), row(S5_WIDTH), row(S5_WIDTH), _full((1, S5_WIDTH)), _full((1, HG_WIDTH))],
                  [_sds((t, S5_WIDTH)), _sds((t, HG_WIDTH)), _sds((t, HG_WIDTH), BF16), _sds((t, 2 * D_MODEL), BF16),
                   _sds((t, D_MODEL), BF16), _sds((t, D_MODEL), BF16), _sds((t, D_MODEL), BF16),
                   _sds((t, S5_WIDTH), BF16), _sds((t, S5_WIDTH), BF16), _sds((1, S5_WIDTH)), _sds((1, HG_WIDTH))],
                  )(dx1, y0, o, zh, zgt, pa, pb, w_glu, b_glu, gain, w_pa, w_pb, w_out)


def _s5_bwd(dy0, za, xs, c_bands, b_bands, lam, dskip, nb, seq, ts):
    nts = seq // ts

    def body(dy0_ref, za_ref, xs_ref, halo_ref, cr_ref, ci_ref, br_ref, bi_ref, lam_ref, d_ref,
             dza_ref, a_ref, dlam_ref, dd_ref, acc_ref, st_ref):
        j = pl.program_id(0)

        @pl.when(j == 0)
        def _():
            dlam_ref[...] = jnp.zeros_like(dlam_ref)
            dd_ref[...] = jnp.zeros_like(dd_ref)
            st_ref[...] = jnp.zeros_like(st_ref)

        for b in range(nb):
            dy0 = dy0_ref[b]
            for q in range(S5_BANDS):
                ch, st = _band(q)
                acc_ref[b, :, st] = _dot(dy0[:, ch], cr_ref[q])
                acc_ref[b, :, _im(st)] = _dot(dy0[:, ch], ci_ref[q])
        _complex_scan(acc_ref, lam_ref, st_ref, nb, ts, reverse=True)
        shift = _shift_matrix(ts)
        top = lax.broadcasted_iota(jnp.int32, (SUBLANES, S5_LANES), 0) == 0
        for b in range(nb):
            a_ref[b] = acc_ref[b].astype(BF16)
            first = jnp.where(j == nts - 1, 0.0, halo_ref[b, HALO - 1:HALO, :].astype(F32))

            def shifted(cols):
                xp = jnp.dot(shift, xs_ref[b, :, cols], preferred_element_type=F32)
                return jnp.concatenate([xp[:SUBLANES] + jnp.where(top, first[:, cols], 0.0), xp[SUBLANES:]], axis=0)

            for cc in range(S5_N // S5_LANES):
                re = slice(cc * S5_LANES, (cc + 1) * S5_LANES)
                ar, ai, xr, xi = acc_ref[b, :, re], acc_ref[b, :, _im(re)], shifted(re), shifted(_im(re))
                dlam_ref[0:1, re] += jnp.sum(ar * xr + ai * xi, axis=0, keepdims=True)
                dlam_ref[1:2, re] += jnp.sum(ai * xr - ar * xi, axis=0, keepdims=True)
            dy0 = dy0_ref[b]
            for q in range(S5_BANDS):
                ch, st = _band(q)
                dza_ref[b, :, ch] = (_dot(a_ref[b, :, st], br_ref[q]) + _dot(a_ref[b, :, _im(st)], bi_ref[q])
                                     + d_ref[:, ch] * dy0[:, ch]).astype(BF16)
            dd_ref[...] += jnp.sum(dy0 * za_ref[b], axis=0, keepdims=True)

    tile = lambda j: nts - 1 - j
    tok = lambda w: pl.BlockSpec((nb, ts, w), lambda j: (0, tile(j), 0))
    halo = pl.BlockSpec((nb, HALO, 2 * S5_N), lambda j: (0, jnp.maximum(tile(j) * (ts // HALO) - 1, 0), 0))
    to_st, to_ch = _full((S5_BANDS, BAND_CH, BAND_ST)), _full((S5_BANDS, BAND_ST, BAND_CH))
    return _pcall(body, "s5_bwd", (nts,),
                  [tok(S5_WIDTH), tok(S5_WIDTH), tok(2 * S5_N), halo, to_st, to_st, to_ch, to_ch,
                   _full((2, S5_N)), _full((1, S5_WIDTH))],
                  [tok(S5_WIDTH), tok(2 * S5_N), _full((2, S5_N)), _full((1, S5_WIDTH))],
                  [_sds((nb, seq, S5_WIDTH), BF16), _sds((nb, seq, 2 * S5_N), BF16), _sds((2, S5_N)),
                   _sds((1, S5_WIDTH))],
                  scratch=[pltpu.VMEM((nb, ts, 2 * S5_N), F32), pltpu.VMEM((nb, 2, S5_N), F32)],
                  )(dy0, za, xs, xs, *c_bands, *b_bands, lam, dskip)


def _hgrn_bwd(zh, do, sts, lb, nb, seq):
    nc = seq // CHUNK

    def body(zh_ref, do_ref, sts_ref, lb_ref, dz_ref, dlb_ref, dst_ref):
        @pl.when(pl.program_id(0) == 0)
        def _():
            dst_ref[...] = jnp.zeros_like(dst_ref)
            dlb_ref[...] = jnp.zeros_like(dlb_ref)

        row = lax.broadcasted_iota(jnp.int32, (CHUNK, CHUNK), 0)
        causal = row >= lax.broadcasted_iota(jnp.int32, (CHUNK, CHUNK), 1)
        last_row = lax.broadcasted_iota(jnp.int32, (CHUNK, HG_HEAD), 0) == CHUNK - 1
        for b in range(nb):
            for h in range(HG_HEADS):
                hs = slice(h * HG_HEAD, (h + 1) * HG_HEAD)
                zq = zh_ref[b, :, h * HG_HEAD:(h + 1) * HG_HEAD]
                zf = zh_ref[b, :, HG_WIDTH + h * HG_HEAD:HG_WIDTH + (h + 1) * HG_HEAD]
                zi = zh_ref[b, :, 2 * HG_WIDTH + h * HG_HEAD:2 * HG_WIDTH + (h + 1) * HG_HEAD]
                lbh = lb_ref[:, hs]
                sf, f, sq, qa, bc, bm, bl = _hgrn_gates(zq, zf, lbh)
                k = 1.0 - f
                e_qt = jnp.exp(bc - bm)
                e_kt = jnp.exp(bm - bc)
                e_b = jnp.exp(bc)
                e_kd = jnp.exp(bl - bc)
                e_l = jnp.exp(bl)
                qt, kt, qb, kd = qa * e_qt, k * e_kt, qa * e_b, k * e_kd
                a = jnp.where(causal, _dot_nt(qt, kt), 0.0)
                st = sts_ref[b, 0, h]
                dst = dst_ref[b, h]
                dov = do_ref[b, :, hs]
                da = jnp.where(causal, _dot_nt(dov, zi), 0.0)
                dqt = _hdot(da, kt)
                dkt = _hdot_tn(da, qt)
                dqb = _dot(dov, st)
                di = _dot_tn(a, dov) + _dot_nt(kd, dst)
                dkd = _dot(zi, dst)
                de_l = jnp.sum(dst * st, axis=0, keepdims=True)
                dst_ref[b, h] = dst * e_l + _dot_tn(dov, qb)
                dqa = dqt * e_qt + dqb * e_b
                dk = dkt * e_kt + dkd * e_kd
                dbl = jnp.sum(dkd * kd, axis=0, keepdims=True) + de_l * e_l
                db = dqt * qt - dkt * kt + dqb * qb - dkd * kd + jnp.where(last_row, dbl, 0.0)
                df = _cumsum_rows(db, reverse=True) / f - dk
                dzq = dqa * QSCALE * (sq * (1.0 + zq * (1.0 - sq)))
                dzf = df * (1.0 - lbh) * sf * (1.0 - sf)
                dz_ref[b, :, h * HG_HEAD:(h + 1) * HG_HEAD] = dzq.astype(BF16)
                dz_ref[b, :, HG_WIDTH + h * HG_HEAD:HG_WIDTH + (h + 1) * HG_HEAD] = dzf.astype(BF16)
                dz_ref[b, :, 2 * HG_WIDTH + h * HG_HEAD:2 * HG_WIDTH + (h + 1) * HG_HEAD] = di.astype(BF16)
                dlb_ref[:, hs] += jnp.sum(df * (1.0 - sf), axis=0, keepdims=True)

    rev = lambda c: nc - 1 - c
    return _pcall(body, "hgrn_bwd", (nc,),
                  [pl.BlockSpec((nb, CHUNK, 4 * HG_WIDTH), lambda c: (0, rev(c), 0)),
                   pl.BlockSpec((nb, CHUNK, HG_WIDTH), lambda c: (0, rev(c), 0)),
                   pl.BlockSpec((nb, 1, HG_HEADS, HG_HEAD, HG_HEAD), lambda c: (0, rev(c), 0, 0, 0)),
                   _full((1, HG_WIDTH))],
                  [pl.BlockSpec((nb, CHUNK, 3 * HG_WIDTH), lambda c: (0, rev(c), 0)), _full((1, HG_WIDTH))],
                  [_sds((nb, seq, 3 * HG_WIDTH), BF16), _sds((1, HG_WIDTH))],
                  scratch=[pltpu.VMEM((nb, HG_HEADS, HG_HEAD, HG_HEAD), F32)])(zh, do, sts, lb)


def _in_proj_bwd(dza, dzh, dzg, dzgt, dx1, x, g_mix, w_in, tm):
    t = x.shape[0]

    def body(dza_ref, dzh_ref, dzg_ref, dzgt_ref, dx1_ref, x_ref, g_ref, w_ref, dz_ref, dx_ref, dg_ref):
        @pl.when(pl.program_id(0) == 0)
        def _():
            dg_ref[...] = jnp.zeros_like(dg_ref)

        c1, c2, c3 = S5_WIDTH, S5_WIDTH + 3 * HG_WIDTH, S5_WIDTH + 4 * HG_WIDTH
        dz_ref[:, 0:c1] = dza_ref[...]
        dz_ref[:, c1:c2] = dzh_ref[...]
        dz_ref[:, c2:c3] = dzg_ref[...]
        dz_ref[:, c3:] = dzgt_ref[...]
        du = _dot(dz_ref[...], w_ref[...])
        xv = x_ref[...]
        r = lax.rsqrt(jnp.mean(xv * xv, axis=-1, keepdims=True) + EPS)
        xn = xv * r
        dg_ref[...] += jnp.sum(du * xn, axis=0, keepdims=True)
        dxn = du * g_ref[...]
        dx_ref[...] = dx1_ref[...] + r * (dxn - xn * jnp.mean(dxn * xn, axis=-1, keepdims=True))

    row = lambda w: pl.BlockSpec((tm, w), lambda i: (i, 0))
    return _pcall(body, "in_proj_bwd", (t // tm,),
                  [row(S5_WIDTH), row(3 * HG_WIDTH), row(HG_WIDTH), row(2 * D_MODEL), row(D_MODEL), row(D_MODEL),
                   _full((1, D_MODEL)), _full((N_IN, D_MODEL))],
                  [row(N_IN), row(D_MODEL), _full((1, D_MODEL))],
                  [_sds((t, N_IN), BF16), _sds((t, D_MODEL)), _sds((1, D_MODEL))],
                  )(dza, dzh, dzg, dzgt, dx1, x, g_mix, w_in)


def _after(value, token):
    return value + token[0, 0]


def _local_step(x3, tgt3, weights, sp, emit, emit_small):
    nb, seq, _ = x3.shape
    t = nb * seq
    tm = _token_tile(seq)
    x = x3.reshape(t, D_MODEL)
    tgt = tgt3.reshape(t, D_MODEL)
    row = lambda v: v.reshape(1, -1)

    a_re, a_im, b_re, b_im = sp["s5_a_re"], sp["s5_a_im"], sp["s5_b_re"], sp["s5_b_im"]
    ldt = sp["s5_log_dt"].reshape(S5_GROUPS, 1)
    lr, li, bb_re, bb_im, lb = _params_fwd(a_re, a_im, ldt, b_re, b_im, sp["hg_lb_logits"])
    lam = jnp.concatenate([lr.reshape(1, S5_N), li.reshape(1, S5_N)], axis=0)
    swap = lambda m: m.transpose(0, 2, 1)
    b_to_st = (_band_blocks(bb_re), _band_blocks(bb_im))
    b_to_ch = (_band_blocks(swap(bb_re)), _band_blocks(swap(bb_im)))
    c_to_ch = (_band_blocks(swap(sp["s5_c_re"])), _band_blocks(swap(-sp["s5_c_im"])))
    c_to_st = (_band_blocks(sp["s5_c_re"]), _band_blocks(-sp["s5_c_im"]))

    g_mix, g_ffn, g_final = row(sp["g_mix"]), row(sp["g_ffn"]), row(sp["g_final"])
    b_glu, gain, dskip, b_conv = row(sp["b_glu"]), row(sp["hg_norm_gain"]), row(sp["s5_d"]), row(sp["b_conv"])

    w_in = weights("in", lam, *b_to_st, *b_to_ch, *c_to_ch, *c_to_st)["w_in"]
    u, za, zh, zgt = _in_proj(x, g_mix, w_in, tm)
    seqs = lambda v: v.reshape(nb, seq, v.shape[-1])
    toks = lambda v: v.reshape(t, v.shape[-1])
    xs3, y0 = _s5_fwd(seqs(za), b_to_st, lam, c_to_ch, dskip, nb, seq, tm)
    xs, y0 = toks(xs3), toks(y0)
    o3, sts = _hgrn_fwd(zh.reshape(nb, seq, 4 * HG_WIDTH), lb, nb, seq)
    o = o3.reshape(t, HG_WIDTH)
    wm = weights("mix", y0, o3)
    weights.forward("ffn", wm["w_out"])
    x1, u2, pa, pb, ya2, yb = _mix_fwd(x, y0, o, zh, zgt, wm["w_glu"], b_glu, gain, wm["w_pa"], wm["w_pb"],
                                       wm["w_out"], g_ffn, tm)
    wf = weights("ffn", u2)
    h = _ffn_up(u2, wf["w_up"], min(4 * tm, t))
    hc, a, dx2, dx2b, loss, dg_final = _ffn_down_loss(h, x1, tgt, wf["w_conv"], b_conv, wf["w_down"], g_final,
                                                      seq, tm)

    def wgrad(a, b, name):
        return _wgrad(a, b, name, 512 if a.shape[1] % 512 == 0 else 256, out_dtype=BF16)

    dhc, db_conv = _ffn_bwd_act(dx2b, hc, wf["w_down"], tm)
    dw_down = wgrad(a, dx2b, "dw_down")
    dh, dx1, dx1b, dg_ffn, dw_conv = _ffn_bwd_up(dhc, h, dx2, x1, wf["w_conv"], wf["w_up"], g_ffn, seq, tm)
    sent = emit({"w_up": wgrad(dh, u2, "dw_up"), "w_conv": dw_conv, "w_down": dw_down})
    (dy0, do, dzg, dzgt, m, dpa, dpb, ya1, dpre, db_glu, dgain) = _mix_bwd(
        dx1b, y0, o, zh, zgt, pa, pb, wm["w_glu"], _after(b_glu, sent), gain, wm["w_pa"], wm["w_pb"], wm["w_out"], tm)
    sent = emit({"w_out": wgrad(m, dx1b, "dw_out"), "w_pa": wgrad(ya2, dpa, "dw_pa"),
                 "w_pb": wgrad(yb, dpb, "dw_pb"), "w_glu": wgrad(ya1, dpre, "dw_glu")})
    dzh3, dlb = _hgrn_bwd(zh.reshape(nb, seq, 4 * HG_WIDTH), do.reshape(nb, seq, HG_WIDTH), sts, _after(lb, sent),
                          nb, seq)
    dza, a_s5, dlam, dd = _s5_bwd(seqs(dy0), seqs(za), xs3, c_to_st, b_to_ch, lam, dskip, nb, seq, tm)
    dza, a_s5 = toks(dza), toks(a_s5)
    dz, dx, dg_mix = _in_proj_bwd(dza, dzh3.reshape(t, 3 * HG_WIDTH), dzg, dzgt, dx1, x, g_mix, w_in, tm)
    sent = emit({"w_in": wgrad(dz, u, "dw_in")})

    band = HG_HEAD
    dbb_band = _wgrad(a_s5, za, "dbb_s5", 512, band=band, after=sent)
    dc_band = _wgrad(xs, dy0, "dc_s5", 512, band=band, after=sent)
    dbb_re = swap(_diag_blocks(dbb_band[:S5_N], S5_STATE, S5_GROUP))
    dbb_im = swap(_diag_blocks(dbb_band[S5_N:], S5_STATE, S5_GROUP))
    dc_re = swap(_diag_blocks(dc_band[:S5_N], S5_STATE, S5_GROUP))
    dc_im = -swap(_diag_blocks(dc_band[S5_N:], S5_STATE, S5_GROUP))
    da_re, da_im, dldt, db_re, db_im, dlogits = _params_bwd(
        a_re, a_im, ldt, b_re, b_im, sp["hg_lb_logits"],
        dlam[0].reshape(S5_GROUPS, S5_STATE), dlam[1].reshape(S5_GROUPS, S5_STATE), dbb_re, dbb_im, dlb)
    emit_small({"g_mix": dg_mix, "s5_a_re": da_re, "s5_a_im": da_im, "s5_log_dt": dldt.reshape(1, S5_GROUPS),
                "s5_b_re": db_re, "s5_b_im": db_im, "s5_c_re": dc_re, "s5_c_im": dc_im, "s5_d": dd, "b_glu": db_glu,
                "hg_lb_logits": dlogits, "hg_norm_gain": dgain, "g_ffn": dg_ffn, "b_conv": db_conv,
                "g_final": dg_final, "loss": loss})
    return dx.reshape(nb, seq, D_MODEL)


def _mesh_peers():
    x, y, c = lax.axis_index("x"), lax.axis_index("y"), lax.axis_index("c")
    peers = []
    for k in range(1, N_DEV):
        px, py, pc = (1 - x if k & 4 else x), (1 - y if k & 2 else y), (1 - c if k & 1 else c)
        peers.append((k, (px, py, pc), 4 * px + 2 * py + pc))
    return 4 * x + 2 * y + c, peers


_HBM = pl.BlockSpec(memory_space=pltpu.HBM)
_SEM = pl.BlockSpec(memory_space=pltpu.SEMAPHORE)


_EFFECT = pltpu.CompilerParams(has_side_effects=pltpu.SideEffectType.DATAFLOW_SIDE_EFFECTING)


def _remote(src, dst, send_sem, recv_sem, to):
    return pltpu.make_async_remote_copy(src_ref=src, dst_ref=dst, send_sem=send_sem, recv_sem=recv_sem,
                                        device_id=to, device_id_type=pl.DeviceIdType.MESH)


def _exchange_start(name, arrays, after):
    n = len(arrays)
    srcs = [pltpu.with_memory_space_constraint(a, pltpu.HBM) for a in arrays]
    lands = [pltpu.with_memory_space_constraint(lax.empty(a.shape, a.dtype), pltpu.HBM) for a in arrays]
    copies = (N_DEV - 1) * n

    def body(*refs):
        src_refs, land_refs = refs[:n], refs[n:2 * n]
        send_sems, recv_sems, token = refs[2 * n + 1], refs[2 * n + 2], refs[-1]
        my_slab, peers = _mesh_peers()
        for k, peer, slab in peers:
            for i in range(n):
                s = (k - 1) * n + i
                _remote(src_refs[i].at[slab], land_refs[i].at[my_slab], send_sems.at[s], recv_sems.at[s], peer).start()
        token[...] = jnp.zeros_like(token)

    outs = pl.pallas_call(
        body, name=name,
        out_shape=(pltpu.SemaphoreType.DMA((copies,)), pltpu.SemaphoreType.DMA((copies,)),
                   *[pltpu.HBM(a.shape, a.dtype) for a in lands], _sds((SUBLANES, LANES))),
        in_specs=[_HBM] * (2 * n) + [pl.BlockSpec(memory_space=pl.ANY)],
        out_specs=(_SEM, _SEM, *[_HBM] * n, pl.BlockSpec(memory_space=pltpu.VMEM)),
        input_output_aliases={n + i: 2 + i for i in range(n)}, compiler_params=_EFFECT,
    )(*srcs, *lands, after)
    return (outs[0], outs[1], srcs, outs[2:2 + n]), outs[-1]


def _exchange_wait(name, state, *after):
    send_sems, recv_sems, srcs, lands = state
    n = len(lands)

    def body(*refs):
        src_refs, land_refs = refs[:n], refs[n:2 * n]
        send_ref, recv_ref = refs[2 * n], refs[2 * n + 1]
        _, peers = _mesh_peers()
        for k, peer, slab in peers:
            for i in range(n):
                s = (k - 1) * n + i
                copy = _remote(src_refs[i].at[slab], land_refs[i].at[slab], send_ref.at[s], recv_ref.at[s], peer)
                copy.wait_send()
                copy.wait_recv()

    outs = pl.pallas_call(
        body, name=name,
        out_shape=tuple(pltpu.HBM(a.shape, a.dtype) for a in lands),
        in_specs=[_HBM] * (2 * n) + [_SEM, _SEM] + [pl.BlockSpec(memory_space=pl.ANY)] * len(after),
        out_specs=tuple([_HBM] * n),
        input_output_aliases={n + i: i for i in range(n)}, compiler_params=_EFFECT,
    )(*srcs, *lands, send_sems, recv_sems, *after)
    return list(outs), list(srcs)


def _slab(pos):
    return 4 * pos[0] + 2 * pos[1] + pos[2]


def _chip_routes():
    x, y, c = lax.axis_index("x"), lax.axis_index("y"), lax.axis_index("c")
    return (x, y, c), (x, y, 1 - c), [(1 - x, y, c), (x, 1 - y, c), (1 - x, 1 - y, c)]


def _gather_start(name, arrays, after):
    n = len(arrays)
    me = 4 * lax.axis_index("x") + 2 * lax.axis_index("y") + lax.axis_index("c")
    srcs = [pltpu.with_memory_space_constraint(a, pltpu.HBM) for a in arrays]
    lands = [pltpu.with_memory_space_constraint(
        lax.dynamic_update_slice_in_dim(lax.empty((N_DEV,) + a.shape, a.dtype), a[None], me, 0), pltpu.HBM)
        for a in arrays]

    def body(*refs):
        src_refs, land_refs = refs[:n], refs[n:2 * n]
        send_sems, recv_sems, token = refs[2 * n + 1], refs[2 * n + 2], refs[-1]
        mine, sibling, chips = _chip_routes()
        for k, to in enumerate([sibling] + chips):
            for i in range(n):
                _remote(src_refs[i], land_refs[i].at[_slab(mine)], send_sems.at[k * n + i], recv_sems.at[k * n + i],
                        to).start()
        token[...] = jnp.zeros_like(token)

    outs = pl.pallas_call(
        body, name=name,
        out_shape=(pltpu.SemaphoreType.DMA((4 * n,)), pltpu.SemaphoreType.DMA((4 * n,)),
                   *[pltpu.HBM(a.shape, a.dtype) for a in lands], _sds((SUBLANES, LANES))),
        in_specs=[_HBM] * (2 * n) + [pl.BlockSpec(memory_space=pl.ANY)],
        out_specs=(_SEM, _SEM, *[_HBM] * n, pl.BlockSpec(memory_space=pltpu.VMEM)),
        input_output_aliases={n + i: 2 + i for i in range(n)}, compiler_params=_EFFECT,
    )(*srcs, *lands, after)
    return (outs[0], outs[1], srcs, outs[2:2 + n]), outs[-1]


def _gather_forward(name, state, *after):
    send_a, recv_a, srcs, lands = state
    n = len(lands)

    def body(*refs):
        land_refs, recv_a_ref = refs[:n], refs[n]
        send_b, recv_b = refs[n + 1 + len(after)], refs[n + 2 + len(after)]
        mine, sibling, chips = _chip_routes()
        for j, chip in enumerate(chips):
            for i in range(n):
                block = land_refs[i].at[_slab(chip)]
                _remote(block, block, send_b.at[j * n + i], recv_a_ref.at[(1 + j) * n + i], chip).wait_recv()
                _remote(block, block, send_b.at[j * n + i], recv_b.at[j * n + i], sibling).start()

    outs = pl.pallas_call(
        body, name=name,
        out_shape=(pltpu.SemaphoreType.DMA((3 * n,)), pltpu.SemaphoreType.DMA((3 * n,)),
                   *[pltpu.HBM(a.shape, a.dtype) for a in lands]),
        in_specs=[_HBM] * n + [_SEM] + [pl.BlockSpec(memory_space=pl.ANY)] * len(after),
        out_specs=(_SEM, _SEM, *[_HBM] * n),
        input_output_aliases={i: 2 + i for i in range(n)}, compiler_params=_EFFECT,
    )(*lands, recv_a, *after)
    return (send_a, recv_a, srcs, list(outs[2:])), (outs[0], outs[1])


def _gather_wait(name, state, forwarded, *after):
    send_a, recv_a, srcs, lands = state
    send_b, recv_b = forwarded
    n = len(lands)

    def body(*refs):
        src_refs, land_refs = refs[:n], refs[n:2 * n]
        sa, ra, sb, rb = refs[2 * n:2 * n + 4]
        mine, sibling, chips = _chip_routes()
        for i in range(n):
            for k, to in enumerate([sibling] + chips):
                _remote(src_refs[i], land_refs[i].at[_slab(mine)], sa.at[k * n + i], ra.at[k * n + i], to).wait_send()
            theirs = land_refs[i].at[_slab(sibling)]
            _remote(theirs, theirs, sa.at[i], ra.at[i], sibling).wait_recv()
            for j, chip in enumerate(chips):
                sent = land_refs[i].at[_slab(chip)]
                got = land_refs[i].at[_slab((chip[0], chip[1], sibling[2]))]
                _remote(sent, sent, sb.at[j * n + i], rb.at[j * n + i], sibling).wait_send()
                _remote(got, got, sb.at[j * n + i], rb.at[j * n + i], sibling).wait_recv()

    outs = pl.pallas_call(
        body, name=name,
        out_shape=tuple(pltpu.HBM(a.shape, a.dtype) for a in lands),
        in_specs=[_HBM] * (2 * n) + [_SEM] * 4 + [pl.BlockSpec(memory_space=pl.ANY)] * len(after),
        out_specs=tuple([_HBM] * n),
        input_output_aliases={n + i: i for i in range(n)}, compiler_params=_EFFECT,
    )(*srcs, *lands, send_a, recv_a, send_b, recv_b, *after)
    return list(outs), list(srcs)


def _join_cols(parts, name, tr):
    _, r, c = parts.shape

    def body(p_ref, o_ref):
        for j in range(N_DEV):
            o_ref[:, j * c:(j + 1) * c] = p_ref[j]

    return _pcall(body, name, (r // tr,), [pl.BlockSpec((N_DEV, tr, c), lambda i: (0, i, 0))],
                  pl.BlockSpec((tr, N_DEV * c), lambda i: (i, 0)), _sds((r, N_DEV * c), parts.dtype))(parts)


def _split_cols(full, name, tr):
    r, c = full.shape[0], full.shape[1] // N_DEV

    def body(f_ref, o_ref):
        for j in range(N_DEV):
            o_ref[j] = f_ref[:, j * c:(j + 1) * c]

    return _pcall(body, name, (r // tr,), [pl.BlockSpec((tr, N_DEV * c), lambda i: (i, 0))],
                  pl.BlockSpec((N_DEV, tr, c), lambda i: (0, i, 0)), _sds((N_DEV, r, c), full.dtype))(full)


def _my_slab():
    return (4 * lax.axis_index("x") + 2 * lax.axis_index("y") + lax.axis_index("c")).astype(jnp.int32).reshape(1)


def _adamw(parts, sent, w, m, v, name, tile):
    _, rows, cols = w.shape

    def body(me_ref, p_ref, s_ref, w_ref, m_ref, v_ref, g_out, d_out, m_out, v_out):
        me = me_ref[0]
        g = jnp.where(me == 0, s_ref[0], p_ref[0]).astype(F32)
        for k in range(1, N_DEV):
            g = g + jnp.where(me == k, s_ref[0], p_ref[k]).astype(F32)
        m1 = ADAM_B1 * m_ref[0] + (1.0 - ADAM_B1) * g
        v1 = ADAM_B2 * v_ref[0] + (1.0 - ADAM_B2) * (g * g)
        m_hat = m1 / (1.0 - ADAM_B1 ** ADAM_STEP)
        v_hat = v1 / (1.0 - ADAM_B2 ** ADAM_STEP)
        g_out[0] = g
        d_out[0] = -ADAM_LR * (m_hat / (jnp.sqrt(v_hat) + ADAM_EPS) + ADAM_WD * w_ref[0])
        m_out[0] = m1
        v_out[0] = v1

    row = pl.BlockSpec((1, tile, cols), lambda i, me: (0, i, 0))
    return pl.pallas_call(
        body, name=name, out_shape=[_sds((1, rows, cols))] * 4,
        grid_spec=pltpu.PrefetchScalarGridSpec(
            num_scalar_prefetch=1, grid=(rows // tile,),
            in_specs=[pl.BlockSpec((N_DEV, tile, cols), lambda i, me: (0, i, 0)),
                      pl.BlockSpec((1, tile, cols), lambda i, me: (me[0], i, 0)), row, row, row],
            out_specs=[row, row, row, row]),
        compiler_params=pltpu.CompilerParams(dimension_semantics=("arbitrary",), vmem_limit_bytes=VMEM_LIMIT),
    )(_my_slab(), parts, sent, w, m, v)


BIG = {
    "w_in": ((N_IN // N_DEV, D_MODEL), False, N_IN // N_DEV // 3),
    "w_glu": ((S5_WIDTH // N_DEV, S5_WIDTH), False, S5_WIDTH // N_DEV),
    "w_pa": ((S5_WIDTH, D_MODEL // N_DEV), True, S5_WIDTH),
    "w_pb": ((HG_WIDTH, D_MODEL // N_DEV), True, HG_WIDTH),
    "w_out": ((D_MODEL // N_DEV, D_MODEL), False, D_MODEL // N_DEV),
    "w_up": ((2 * D_FF // N_DEV, D_MODEL), False, 2 * D_FF // N_DEV // 4),
    "w_conv": ((CONV_W, 2 * D_FF // N_DEV), True, CONV_W),
    "w_down": ((D_FF // N_DEV, D_MODEL), False, D_FF // N_DEV // 2),
}
TRANSPOSED = ("w_in", "w_up", "s5_b_re", "s5_b_im")
UNALIGNED_COLS = ("w_conv",)


def _stored(n, arr):
    return jnp.swapaxes(arr, -1, -2) if n in TRANSPOSED else arr


def _join_shards(n, parts):
    (a, b), by_cols, _ = BIG[n]
    if not by_cols:
        return parts.reshape(N_DEV * a, b)
    if n in UNALIGNED_COLS:
        return _join_cols(parts, "join_" + n, min(a, 256))
    return parts.transpose(1, 0, 2).reshape(a, N_DEV * b)


def _split_shards(n, full):
    (a, b), by_cols, _ = BIG[n]
    if not by_cols:
        return full.reshape(N_DEV, a, b)
    if n in UNALIGNED_COLS:
        return _split_cols(full, "split_" + n, min(a, 256))
    return full.reshape(a, N_DEV, b).transpose(1, 0, 2)


SMALL_CORE = {
    "s5_b_re": GSC, "s5_b_im": GSC, "s5_c_re": GSC, "s5_c_im": GSC,
    "g_mix": (1, D_MODEL), "g_ffn": (1, D_MODEL), "g_final": (1, D_MODEL), "s5_d": (1, S5_WIDTH),
    "b_glu": (1, S5_WIDTH), "hg_norm_gain": (1, HG_WIDTH), "hg_lb_logits": (2, HG_WIDTH), "b_conv": (1, 2 * D_FF),
    "s5_log_dt": (1, S5_GROUPS), "s5_a_re": (S5_GROUPS, S5_STATE), "s5_a_im": (S5_GROUPS, S5_STATE), "loss": (1, 1),
}
BLOCK_ROWS = 32


def _small_rows():
    rows, r = {}, 0
    for n, core in SMALL_CORE.items():
        rows[n] = r
        r += BLOCK_ROWS if len(core) == 3 else -(-math.prod(core) // PACK_W)
    return rows, -(-r // SUBLANES) * SUBLANES


SMALL_ROW, SMALL_ROWS = _small_rows()


def _small_pieces(name):
    r, core = SMALL_ROW[name], SMALL_CORE[name]
    if len(core) == 3:
        return [((g, slice(None), slice(None)), slice(r + S5_GROUP * (g % 2), r + S5_GROUP * (g % 2 + 1)),
                 slice(S5_STATE * (g // 2), S5_STATE * (g // 2 + 1))) for g in range(S5_GROUPS)]
    pieces = []
    for i in range(core[0]):
        for c0 in range(0, core[1], PACK_W):
            w, flat = min(PACK_W, core[1] - c0), i * core[1] + c0
            pieces.append(((slice(i, i + 1), slice(c0, c0 + w)), slice(r + flat // PACK_W, r + flat // PACK_W + 1),
                           slice(flat % PACK_W, flat % PACK_W + w)))
    return pieces


def _core_index(ref, name, idx):
    return (0,) * (len(ref.shape) - len(SMALL_CORE[name])) + idx


def _pack_small_grads(grads):
    names = list(SMALL_CORE)

    def body(*refs):
        pack = refs[-1]
        pack[...] = jnp.zeros_like(pack)
        for ref, n in zip(refs, names):
            for idx, rows, lanes in _small_pieces(n):
                pack[rows, lanes] = ref[_core_index(ref, n, idx)]

    return _pcall(body, "pack_small_grads", (1,), [_full(grads[n].shape) for n in names],
                  _full((SMALL_ROWS, PACK_W)), _sds((SMALL_ROWS, PACK_W)))(*[grads[n] for n in names])


def _adamw_small(parts, sent, names, rows, given, name):
    lo, hi = rows
    k = len(names)
    shapes = [given[n].shape for n in names]

    def body(*refs):
        me, p_ref, s_ref, ins, outs = refs[0][0], refs[1], refs[2], refs[3:3 + 3 * k], refs[3 + 3 * k:3 + 7 * k]
        packs, results = refs[3 + 7 * k:6 + 7 * k], refs[6 + 7 * k:]
        for j, pack in enumerate(packs):
            pack[...] = jnp.zeros_like(pack)
            for ref, n in zip(ins[j * k:(j + 1) * k], names):
                for idx, prow, lanes in _small_pieces(n):
                    pack[slice(prow.start - lo, prow.stop - lo), lanes] = ref[_core_index(ref, n, idx)]
        mine = s_ref[lo:hi, :]
        g = jnp.where(me == 0, mine, p_ref[0, lo:hi, :])
        for d in range(1, N_DEV):
            g = g + jnp.where(me == d, mine, p_ref[d, lo:hi, :])
        m1 = ADAM_B1 * packs[1][...] + (1.0 - ADAM_B1) * g
        v1 = ADAM_B2 * packs[2][...] + (1.0 - ADAM_B2) * (g * g)
        m_hat = m1 / (1.0 - ADAM_B1 ** ADAM_STEP)
        v_hat = v1 / (1.0 - ADAM_B2 ** ADAM_STEP)
        results[0][...] = g
        results[1][...] = -ADAM_LR * (m_hat / (jnp.sqrt(v_hat) + ADAM_EPS) + ADAM_WD * packs[0][...])
        results[2][...] = m1
        results[3][...] = v1
        for j, result in enumerate(results):
            for ref, n in zip(outs[j * k:(j + 1) * k], names):
                for idx, prow, lanes in _small_pieces(n):
                    ref[_core_index(ref, n, idx)] = result[slice(prow.start - lo, prow.stop - lo), lanes]

    flat = _pcall(body, name, (1,),
                  [pl.BlockSpec(memory_space=pltpu.SMEM), _full(parts.shape), _full(sent.shape)]
                  + [_full(s) for s in shapes] * 3,
                  [_full(s) for s in shapes] * 4, [_sds(s) for s in shapes] * 4,
                  scratch=[pltpu.VMEM((hi - lo, PACK_W), F32)] * 7,
                  )(_my_slab(), parts, sent, *[given[pre + n] for pre in ("", "m_", "v_") for n in names])
    return {n: [flat[j * k + i] for j in range(4)] for i, n in enumerate(names)}


def kernel(x, g_mix, w_in, s5_a_re, s5_a_im, s5_log_dt, s5_b_re, s5_b_im, s5_c_re, s5_c_im, s5_d, w_glu, b_glu, hg_lb_logits, hg_norm_gain, w_pa, w_pb, w_out, g_ffn, w_up, w_conv, b_conv, w_down, g_final, loss_target, m_g_mix, m_w_in, m_s5_a_re, m_s5_a_im, m_s5_log_dt, m_s5_b_re, m_s5_b_im, m_s5_c_re, m_s5_c_im, m_s5_d, m_w_glu, m_b_glu, m_hg_lb_logits, m_hg_norm_gain, m_w_pa, m_w_pb, m_w_out, m_g_ffn, m_w_up, m_w_conv, m_b_conv, m_w_down, m_g_final, v_g_mix, v_w_in, v_s5_a_re, v_s5_a_im, v_s5_log_dt, v_s5_b_re, v_s5_b_im, v_s5_c_re, v_s5_c_im, v_s5_d, v_w_glu, v_b_glu, v_hg_lb_logits, v_hg_norm_gain, v_w_pa, v_w_pb, v_w_out, v_g_ffn, v_w_up, v_w_conv, v_b_conv, v_w_down, v_g_final):
    given = dict(locals())
    small_names = [n for n in SMALL_CORE if n != "loss"]

    pay = {n: given[n][0] if n == "w_conv" else _stored(n, given[n])[0].astype(BF16) for n in BIG}
    groups = {"in": ["w_in"], "mix": ["w_glu", "w_pa", "w_pb", "w_out"], "ffn": ["w_up", "w_down", "w_conv"]}
    gathers, order = {}, pay["w_in"]
    for grp, names in groups.items():
        gathers[grp], order = _gather_start("gather_" + grp + "_start", [pay[n] for n in names], order)

    forwards = {}

    def forward(grp, *after):
        if grp == "in":
            after = (*after, order)
        forwards[grp] = _gather_forward("gather_" + grp + "_forward", gathers[grp], *after)

    def weights(grp, *after):
        if grp not in forwards:
            forward(grp, *after)
        got, _ = _gather_wait("gather_" + grp + "_wait", *forwards[grp], *after)
        return {n: _join_shards(n, g) for n, g in zip(groups[grp], got)}

    weights.forward = forward

    in_flight, started = [], []

    def emit(grads):
        names = list(grads)
        state, token = _exchange_start("grads_" + names[0] + "_start", [_split_shards(n, grads[n]) for n in names],
                                       grads[names[0]])
        in_flight.append((names, state))
        return token

    def emit_small(grads):
        pack = _pack_small_grads(grads)
        state, token = _gather_start("grads_small_start", [pack], pack)
        in_flight.append((["small"], state))
        started.append(token)

    sp = {n: (given[n] if n in ("g_final", "hg_lb_logits") else _stored(n, given[n])[0]) for n in small_names}
    sp["g_mix"] = _after(sp["g_mix"], order)
    dx = _local_step(x, loss_target, weights, sp, emit, emit_small)

    res = {}
    after = [started[-1]]
    for names, state in in_flight:
        if names == ["small"]:
            state, forwarded = _gather_forward("grads_small_forward", state, *after)
            parts, sent = _gather_wait("grads_small_wait", state, forwarded)
        else:
            parts, sent = _exchange_wait("grads_" + names[0] + "_wait", state, *after)
        if names != ["small"]:
            after = []
            for n, part, mine in zip(names, parts, sent):
                raw = _adamw(part, mine, *[_stored(n, given[pre + n]) for pre in ("", "m_", "v_")], "adamw_" + n,
                             BIG[n][2])
                res[n] = [_stored(n, r) for r in raw]
                after.append(raw[0])
            continue
        sgiven = {pre + n: _stored(n, given[pre + n]) for pre in ("", "m_", "v_") for n in small_names}
        for pre in ("", "m_", "v_"):
            sgiven[pre + "g_final"] = given[pre + "g_final"].reshape(1, D_MODEL)
            sgiven[pre + "loss"] = jnp.zeros((1, 1), F32)
        raw = _adamw_small(parts[0], sent[0], list(SMALL_CORE), (0, SMALL_ROWS), sgiven, "adamw_small")
        res.update({n: [_stored(n, r) for r in raw[n]] for n in small_names})
        res["g_final"] = [r.reshape(D_MODEL) for r in raw["g_final"]]
        total_loss = raw["loss"][0].reshape(())
        after = [raw["s5_b_re"][0], raw["g_mix"][0]]
    return (total_loss, dx, *[res[n][0] for n in WEIGHT_ORDER], *[res[n][1] for n in WEIGHT_ORDER],
            *[res[n][2] for n in WEIGHT_ORDER], *[res[n][3] for n in WEIGHT_ORDER])
```

```python
import math

import jax
import jax.numpy as jnp
from jax import lax
from jax.experimental import pallas as pl
from jax.experimental.pallas import tpu as pltpu

F32 = jnp.float32
BF16 = jnp.bfloat16

D_MODEL = 1024
S5_WIDTH = 512
S5_GROUP = 16
S5_GROUPS = 32
S5_STATE = 64
S5_N = S5_GROUPS * S5_STATE
HG_WIDTH = 512
HG_HEAD = 128
HG_HEADS = 4
D_FF = 2816
CONV_W = 3
CHUNK = 64
N_IN = S5_WIDTH + 4 * HG_WIDTH + 2 * D_MODEL
EPS = 1e-6
QSCALE = HG_HEAD ** -0.5

ADAM_LR = 0.001
ADAM_B1 = 0.9
ADAM_B2 = 0.999
ADAM_EPS = 1e-08
ADAM_WD = 0.01
ADAM_STEP = 10

N_DEV = 8
V7X_VMEM_BYTES = 64 * 1024 * 1024
VMEM_LIMIT = V7X_VMEM_BYTES * 7 // 8
SUBLANES = 8
LANES = 128
PACK_W = 1024

WEIGHT_ORDER = ("g_mix", "w_in", "s5_a_re", "s5_a_im", "s5_log_dt", "s5_b_re", "s5_b_im", "s5_c_re", "s5_c_im",
                "s5_d", "w_glu", "b_glu", "hg_lb_logits", "hg_norm_gain", "w_pa", "w_pb", "w_out", "g_ffn",
                "w_up", "w_conv", "b_conv", "w_down", "g_final")


def _pcall(body, name, grid, in_specs, out_specs, out_shape, scratch=()):
    return pl.pallas_call(
        body, name=name, grid=grid, in_specs=in_specs, out_specs=out_specs, out_shape=out_shape,
        scratch_shapes=list(scratch),
        compiler_params=pltpu.CompilerParams(dimension_semantics=("arbitrary",) * len(grid),
                                             vmem_limit_bytes=VMEM_LIMIT),
    )


def _full(shape):
    return pl.BlockSpec(shape, lambda *_: (0,) * len(shape))


def _sds(shape, dtype=F32):
    return jax.ShapeDtypeStruct(shape, dtype)


def _dot(a, b):
    return jnp.dot(a.astype(BF16), b.astype(BF16), preferred_element_type=F32)


def _dot_nt(a, b):
    return lax.dot_general(a.astype(BF16), b.astype(BF16), (((1,), (1,)), ((), ())), preferred_element_type=F32)


def _dot_tn(a, b):
    return lax.dot_general(a.astype(BF16), b.astype(BF16), (((0,), (0,)), ((), ())), preferred_element_type=F32)


def _split(a):
    hi = a.astype(BF16)
    return hi, (a - hi.astype(F32)).astype(BF16)


def _hdot(a, b, dims=(((1,), (0,)), ((), ()))):
    (ah, al), (bh, bl) = _split(a), _split(b)
    dot = lambda p, q: lax.dot_general(p, q, dims, preferred_element_type=F32)
    return dot(ah, bh) + (dot(al, bh) + dot(ah, bl))


def _hdot_tn(a, b):
    return _hdot(a, b, (((0,), (0,)), ((), ())))


def _sigmoid(x):
    return jax.nn.sigmoid(x)


GELU_C = math.sqrt(2.0 / math.pi)
GELU_A = 0.044715


def _gelu(x):
    return 0.5 * x * (1.0 + jnp.tanh(GELU_C * (x + GELU_A * (x * x * x))))


def _gelu_grad(x):
    t = jnp.tanh(GELU_C * (x + GELU_A * (x * x * x)))
    return 0.5 * (1.0 + t) + 0.5 * x * (1.0 - t * t) * (GELU_C * (1.0 + 3.0 * GELU_A * x * x))


def _cumsum_rows(v, reverse=False):
    n = v.shape[0]
    row = lax.broadcasted_iota(jnp.int32, v.shape, 0)
    s = 1
    while s < n:
        if reverse:
            v = v + jnp.where(row < n - s, pltpu.roll(v, n - s, axis=0), 0.0)
        else:
            v = v + jnp.where(row >= s, pltpu.roll(v, s, axis=0), 0.0)
        s *= 2
    return v


def _token_tile(seq):
    return min(256, seq)


def _s5_coeffs(a_re, a_im, ldt):
    dt = jnp.exp(ldt)
    mag = jnp.exp(a_re * dt)
    ang = a_im * dt
    lb_re = mag * jnp.cos(ang)
    lb_im = mag * jnp.sin(ang)
    den = a_re * a_re + a_im * a_im
    n_re = lb_re - 1.0
    n_im = lb_im
    co_re = (n_re * a_re + n_im * a_im) / den
    co_im = (n_im * a_re - n_re * a_im) / den
    return lb_re, lb_im, co_re, co_im


GS, GSC = (S5_GROUPS, S5_STATE), (S5_GROUPS, S5_GROUP, S5_STATE)


def _params_fwd(a_re, a_im, ldt, bt_re, bt_im, logits):
    def body(are, aim, ld, bre, bim, lg, lr_o, li_o, bbr_o, bbi_o, lb_o):
        lr, li, co_re, co_im = _s5_coeffs(are[...], aim[...], ld[...])
        lr_o[...] = lr
        li_o[...] = li
        for g in range(S5_GROUPS):
            cr, ci = co_re[g:g + 1, :], co_im[g:g + 1, :]
            bbr_o[g] = cr * bre[g] - ci * bim[g]
            bbi_o[g] = cr * bim[g] + ci * bre[g]
        lb_o[...] = _sigmoid(lg[0:1, :] - lg[1:2, :])

    return _pcall(body, "params_fwd", (1,),
                  [_full(GS), _full(GS), _full((S5_GROUPS, 1)), _full(GSC), _full(GSC), _full((2, HG_WIDTH))],
                  [_full(GS), _full(GS), _full(GSC), _full(GSC), _full((1, HG_WIDTH))],
                  [_sds(GS), _sds(GS), _sds(GSC), _sds(GSC), _sds((1, HG_WIDTH))],
                  )(a_re, a_im, ldt, bt_re, bt_im, logits)


def _params_bwd(a_re, a_im, ldt, bt_re, bt_im, logits, dlr, dli, dbbr, dbbi, dlb):
    def body(are, aim, ld, bre, bim, lg, dlr_r, dli_r, dbbr_r, dbbi_r, dlb_r,
             dare_o, daim_o, dld_o, dbre_o, dbim_o, dlg_o, dcr_ref, dci_ref):
        (_, _, co_re, co_im), vjp = jax.vjp(_s5_coeffs, are[...], aim[...], ld[...])
        for g in range(S5_GROUPS):
            cr, ci = co_re[g:g + 1, :], co_im[g:g + 1, :]
            gr, gi, br, bi = dbbr_r[g], dbbi_r[g], bre[g], bim[g]
            dbre_o[g] = cr * gr + ci * gi
            dbim_o[g] = cr * gi - ci * gr
            dcr_ref[g:g + 1, :] = jnp.sum(gr * br + gi * bi, axis=0, keepdims=True)
            dci_ref[g:g + 1, :] = jnp.sum(gi * br - gr * bi, axis=0, keepdims=True)
        dare, daim, dld = vjp((dlr_r[...], dli_r[...], dcr_ref[...], dci_ref[...]))
        dare_o[...] = dare
        daim_o[...] = daim
        dld_o[...] = dld
        lb = _sigmoid(lg[0:1, :] - lg[1:2, :])
        d0 = dlb_r[...] * lb * (1.0 - lb)
        dlg_o[0:1, :] = d0
        dlg_o[1:2, :] = -d0

    return _pcall(body, "params_bwd", (1,),
                  [_full(GS), _full(GS), _full((S5_GROUPS, 1)), _full(GSC), _full(GSC), _full((2, HG_WIDTH)),
                   _full(GS), _full(GS), _full(GSC), _full(GSC), _full((1, HG_WIDTH))],
                  [_full(GS), _full(GS), _full((S5_GROUPS, 1)), _full(GSC), _full(GSC), _full((2, HG_WIDTH))],
                  [_sds(GS), _sds(GS), _sds((S5_GROUPS, 1)), _sds(GSC), _sds(GSC), _sds((2, HG_WIDTH))],
                  scratch=[pltpu.VMEM(GS, F32), pltpu.VMEM(GS, F32)],
                  )(a_re, a_im, ldt, bt_re, bt_im, logits, dlr, dli, dbbr, dbbi, dlb)


def _band_blocks(m):
    g, r, c = m.shape
    gb = g // S5_BANDS
    m4 = m.astype(BF16).reshape(S5_BANDS, gb, r, c)
    on_diag = jnp.eye(gb, dtype=bool)[None, :, None, :, None]
    return jnp.where(on_diag, m4[:, :, :, None, :], 0).reshape(S5_BANDS, gb * r, gb * c)


def _diag_blocks(band, r, c):
    g, nb = band.shape[0] // r, band.shape[1] // c
    on_diag = (jnp.arange(g) % nb)[:, None, None, None] == jnp.arange(nb)[None, None, :, None]
    return jnp.sum(jnp.where(on_diag, band.reshape(g, r, nb, c), 0.0), axis=2)


def _in_proj(x, g_mix, w_in, tm):
    t = x.shape[0]

    def body(x_ref, g_ref, w_ref, u_ref, za_ref, zh_ref, zg_ref):
        xv = x_ref[...]
        r = lax.rsqrt(jnp.mean(xv * xv, axis=-1, keepdims=True) + EPS)
        u = (xv * r * g_ref[...]).astype(BF16)
        u_ref[...] = u
        za_ref[...] = _dot_nt(u, w_ref[0:S5_WIDTH, :])
        zh_ref[...] = _dot_nt(u, w_ref[S5_WIDTH:S5_WIDTH + 4 * HG_WIDTH, :])
        zg_ref[...] = _dot_nt(u, w_ref[S5_WIDTH + 4 * HG_WIDTH:, :]).astype(BF16)

    row = lambda w: pl.BlockSpec((tm, w), lambda i: (i, 0))
    return _pcall(body, "in_proj", (t // tm,),
                  [row(D_MODEL), _full((1, D_MODEL)), _full((N_IN, D_MODEL))],
                  [row(D_MODEL), row(S5_WIDTH), row(4 * HG_WIDTH), row(2 * D_MODEL)],
                  [_sds((t, D_MODEL), BF16), _sds((t, S5_WIDTH)), _sds((t, 4 * HG_WIDTH)),
                   _sds((t, 2 * D_MODEL), BF16)],
                  )(x, g_mix, w_in)


S5_LANES = 512
S5_BANDS = 4


def _band(q):
    return (slice(q * S5_WIDTH // S5_BANDS, (q + 1) * S5_WIDTH // S5_BANDS),
            slice(q * S5_N // S5_BANDS, (q + 1) * S5_N // S5_BANDS))


def _im(st):
    return slice(S5_N + st.start, S5_N + st.stop)


SCAN_UNROLL = 8


def _complex_scan(buf_ref, lam_ref, st_ref, nb, ts, reverse):
    lanes = [slice(cc * S5_LANES, (cc + 1) * S5_LANES) for cc in range(S5_N // S5_LANES)]
    chains = [(b, re) for b in range(nb) for re in lanes]
    nch = len(chains)
    wr = {re.start: lam_ref[0:1, re] for re in lanes}
    wi = {re.start: -lam_ref[1:2, re] if reverse else lam_ref[1:2, re] for re in lanes}

    def block(ib, carry):
        vr, vi = list(carry[:nch]), list(carry[nch:])
        first = ts - SCAN_UNROLL - ib * SCAN_UNROLL if reverse else ib * SCAN_UNROLL
        first = pl.multiple_of(first, SCAN_UNROLL)
        for k in range(SCAN_UNROLL):
            row = pl.ds(first + (SCAN_UNROLL - 1 - k if reverse else k), 1)
            for c, (b, re) in enumerate(chains):
                nr = wr[re.start] * vr[c] - wi[re.start] * vi[c] + buf_ref[b, row, re]
                ni = wr[re.start] * vi[c] + wi[re.start] * vr[c] + buf_ref[b, row, _im(re)]
                buf_ref[b, row, re] = nr
                buf_ref[b, row, _im(re)] = ni
                vr[c], vi[c] = nr, ni
        return tuple(vr + vi)

    init = tuple(st_ref[b, 0:1, re] for b, re in chains) + tuple(st_ref[b, 1:2, re] for b, re in chains)
    last = lax.fori_loop(0, ts // SCAN_UNROLL, block, init)
    for c, (b, re) in enumerate(chains):
        st_ref[b, 0:1, re] = last[c]
        st_ref[b, 1:2, re] = last[nch + c]


BAND_CH = S5_WIDTH // S5_BANDS
BAND_ST = S5_N // S5_BANDS


def _s5_fwd(za, b_bands, lam, c_bands, dskip, nb, seq, ts):
    nts = seq // ts

    def body(za_ref, br_ref, bi_ref, lam_ref, cr_ref, ci_ref, d_ref, xs_ref, y_ref, buf_ref, st_ref):
        @pl.when(pl.program_id(0) == 0)
        def _():
            st_ref[...] = jnp.zeros_like(st_ref)

        for b in range(nb):
            zav = za_ref[b]
            for q in range(S5_BANDS):
                ch, st = _band(q)
                buf_ref[b, :, st] = _dot(zav[:, ch], br_ref[q])
                buf_ref[b, :, _im(st)] = _dot(zav[:, ch], bi_ref[q])
        _complex_scan(buf_ref, lam_ref, st_ref, nb, ts, reverse=False)
        for b in range(nb):
            zav = za_ref[b]
            xs_ref[b] = buf_ref[b].astype(BF16)
            for q in range(S5_BANDS):
                ch, st = _band(q)
                y_ref[b, :, ch] = (_dot(xs_ref[b, :, st], cr_ref[q]) + _dot(xs_ref[b, :, _im(st)], ci_ref[q])
                                   + d_ref[:, ch] * zav[:, ch])

    tok = lambda w: pl.BlockSpec((nb, ts, w), lambda j: (0, j, 0))
    to_st, to_ch = _full((S5_BANDS, BAND_CH, BAND_ST)), _full((S5_BANDS, BAND_ST, BAND_CH))
    return _pcall(body, "s5_fwd", (nts,),
                  [tok(S5_WIDTH), to_st, to_st, _full((2, S5_N)), to_ch, to_ch, _full((1, S5_WIDTH))],
                  [tok(2 * S5_N), tok(S5_WIDTH)],
                  [_sds((nb, seq, 2 * S5_N), BF16), _sds((nb, seq, S5_WIDTH))],
                  scratch=[pltpu.VMEM((nb, ts, 2 * S5_N), F32), pltpu.VMEM((nb, 2, S5_N), F32)],
                  )(za, *b_bands, lam, *c_bands, dskip)


def _hgrn_gates(zq, zf, lbh):
    sf = _sigmoid(zf)
    f = lbh + (1.0 - lbh) * sf
    sq = _sigmoid(zq)
    qa = zq * sq * QSCALE
    bc = _cumsum_rows(jnp.log(f))
    bm = bc[CHUNK // 2 - 1:CHUNK // 2, :]
    bl = bc[CHUNK - 1:CHUNK, :]
    return sf, f, sq, qa, bc, bm, bl


def _hgrn_fwd(zh, lb, nb, seq):
    nc = seq // CHUNK

    def body(zh_ref, lb_ref, o_ref, sts_ref, st_ref):
        @pl.when(pl.program_id(0) == 0)
        def _():
            st_ref[...] = jnp.zeros_like(st_ref)

        causal = (lax.broadcasted_iota(jnp.int32, (CHUNK, CHUNK), 0)
                  >= lax.broadcasted_iota(jnp.int32, (CHUNK, CHUNK), 1))
        for b in range(nb):
            for h in range(HG_HEADS):
                hs = slice(h * HG_HEAD, (h + 1) * HG_HEAD)
                zq = zh_ref[b, :, h * HG_HEAD:(h + 1) * HG_HEAD]
                zf = zh_ref[b, :, HG_WIDTH + h * HG_HEAD:HG_WIDTH + (h + 1) * HG_HEAD]
                zi = zh_ref[b, :, 2 * HG_WIDTH + h * HG_HEAD:2 * HG_WIDTH + (h + 1) * HG_HEAD]
                _, f, _, qa, bc, bm, bl = _hgrn_gates(zq, zf, lb_ref[:, hs])
                k = 1.0 - f
                qt = qa * jnp.exp(bc - bm)
                kt = k * jnp.exp(bm - bc)
                qb = qa * jnp.exp(bc)
                kd = k * jnp.exp(bl - bc)
                st = st_ref[b, h]
                sts_ref[b, 0, h] = st
                a = jnp.where(causal, _dot_nt(qt, kt), 0.0)
                o_ref[b, :, hs] = _dot(a, zi) + _dot_nt(qb, st)
                st_ref[b, h] = st * jnp.exp(bl) + _dot_tn(zi, kd)

    return _pcall(body, "hgrn_fwd", (nc,),
                  [pl.BlockSpec((nb, CHUNK, 4 * HG_WIDTH), lambda c: (0, c, 0)), _full((1, HG_WIDTH))],
                  [pl.BlockSpec((nb, CHUNK, HG_WIDTH), lambda c: (0, c, 0)),
                   pl.BlockSpec((nb, 1, HG_HEADS, HG_HEAD, HG_HEAD), lambda c: (0, c, 0, 0, 0))],
                  [_sds((nb, seq, HG_WIDTH)), _sds((nb, nc, HG_HEADS, HG_HEAD, HG_HEAD))],
                  scratch=[pltpu.VMEM((nb, HG_HEADS, HG_HEAD, HG_HEAD), F32)])(zh, lb)


def _head_rms(o):
    parts = []
    for h in range(HG_HEADS):
        oh = o[:, h * HG_HEAD:(h + 1) * HG_HEAD]
        r = lax.rsqrt(jnp.mean(oh * oh, axis=-1, keepdims=True) + EPS)
        parts.append(jnp.broadcast_to(r, oh.shape))
    return jnp.concatenate(parts, axis=1)


def _head_mean(v):
    parts = []
    for h in range(HG_HEADS):
        vh = v[:, h * HG_HEAD:(h + 1) * HG_HEAD]
        parts.append(jnp.broadcast_to(jnp.mean(vh, axis=-1, keepdims=True), vh.shape))
    return jnp.concatenate(parts, axis=1)


def _mix_fwd(x, y0, o, zh, zgt, w_glu, b_glu, gain, w_pa, w_pb, w_out, g_ffn, tm):
    t = x.shape[0]

    def body(x_ref, y0_ref, o_ref, zg_ref, zgt_ref, wglu_ref, bglu_ref, gain_ref, wpa_ref, wpb_ref, wout_ref,
             gffn_ref, x1_ref, u2_ref, pa_ref, pb_ref, ya2_ref, yb_ref):
        ya1 = _gelu(y0_ref[...])
        s = _sigmoid(_dot(ya1, wglu_ref[...]) + bglu_ref[...])
        ya2 = (ya1 * s).astype(BF16)
        ov = o_ref[...]
        zg = zg_ref[...]
        yb = (ov * _head_rms(ov) * gain_ref[...] * (zg * _sigmoid(zg))).astype(BF16)
        ya2_ref[...] = ya2
        yb_ref[...] = yb
        pa = jnp.dot(ya2, wpa_ref[...], preferred_element_type=F32)
        pb = jnp.dot(yb, wpb_ref[...], preferred_element_type=F32)
        pa_ref[...] = pa.astype(BF16)
        pb_ref[...] = pb.astype(BF16)
        m = (_sigmoid(zgt_ref[:, 0:D_MODEL].astype(F32)) * pa
             + _sigmoid(zgt_ref[:, D_MODEL:].astype(F32)) * pb)
        x1 = x_ref[...] + _dot(m, wout_ref[...])
        x1_ref[...] = x1
        r = lax.rsqrt(jnp.mean(x1 * x1, axis=-1, keepdims=True) + EPS)
        u2_ref[...] = (x1 * r * gffn_ref[...]).astype(BF16)

    row = lambda w: pl.BlockSpec((tm, w), lambda i: (i, 0))
    return _pcall(body, "mix_fwd", (t // tm,),
                  [row(D_MODEL), row(S5_WIDTH), row(HG_WIDTH), pl.BlockSpec((tm, HG_WIDTH), lambda i: (i, 3)),
                   row(2 * D_MODEL), _full((S5_WIDTH, S5_WIDTH)), _full((1, S5_WIDTH)), _full((1, HG_WIDTH)),
                   _full((S5_WIDTH, D_MODEL)), _full((HG_WIDTH, D_MODEL)), _full((D_MODEL, D_MODEL)),
                   _full((1, D_MODEL))],
                  [row(D_MODEL), row(D_MODEL), row(D_MODEL), row(D_MODEL), row(S5_WIDTH), row(HG_WIDTH)],
                  [_sds((t, D_MODEL)), _sds((t, D_MODEL), BF16), _sds((t, D_MODEL), BF16), _sds((t, D_MODEL), BF16),
                   _sds((t, S5_WIDTH), BF16), _sds((t, HG_WIDTH), BF16)],
                  )(x, y0, o, zh, zgt, w_glu, b_glu, gain, w_pa, w_pb, w_out, g_ffn)


FF_COLS = 256
FF_UP_TILE = 2 * D_FF // 2


def _ffn_up(u2, w_up, tm):
    t = u2.shape[0]
    n = 2 * D_FF

    def body(u_ref, w_ref, h_ref):
        h_ref[...] = _dot_nt(u_ref[...], w_ref[...]).astype(BF16)

    return _pcall(body, "ffn_up", (n // FF_UP_TILE, t // tm),
                  [pl.BlockSpec((tm, D_MODEL), lambda j, i: (i, 0)),
                   pl.BlockSpec((FF_UP_TILE, D_MODEL), lambda j, i: (j, 0))],
                  pl.BlockSpec((tm, FF_UP_TILE), lambda j, i: (i, j)),
                  _sds((t, n), BF16))(u2, w_up)


HALO = 16


def _shift_matrix(tm):
    r = lax.broadcasted_iota(jnp.int32, (tm, tm), 0)
    c = lax.broadcasted_iota(jnp.int32, (tm, tm), 1)
    return jnp.where(r == c + 1, 1.0, 0.0).astype(BF16)


def _conv_cols(h_ref, halo_ref, valid, wc_ref, bc_ref, c0):
    cs = slice(c0, c0 + FF_COLS)
    cur = h_ref[:, cs].astype(F32)
    prev = jnp.where(valid, halo_ref[:, cs].astype(F32), 0.0)
    full = jnp.concatenate([prev, cur], axis=0)
    h1 = pltpu.roll(full, 1, axis=0)[HALO:]
    h2 = pltpu.roll(full, 2, axis=0)[HALO:]
    return h2 * wc_ref[0:1, cs] + h1 * wc_ref[1:2, cs] + cur * wc_ref[2:3, cs] + bc_ref[:, cs]


def _ffn_down_loss(h, x1, tgt, w_conv, b_conv, w_down, g_final, seq, tm):
    t = h.shape[0]
    tps = seq // tm
    n = 2 * D_FF

    def body(h_ref, halo_ref, x1_ref, tgt_ref, wc_ref, bc_ref, wd_ref, gf_ref,
             hc_ref, a_ref, dx2_ref, dx2b_ref, loss_ref, dgf_ref):
        i = pl.program_id(0)

        @pl.when(i == 0)
        def _():
            loss_ref[...] = jnp.zeros_like(loss_ref)
            dgf_ref[...] = jnp.zeros_like(dgf_ref)

        valid = (i % tps) != 0
        x2 = x1_ref[...]
        for j in range(D_FF // FF_COLS):
            gate = _conv_cols(h_ref, halo_ref, valid, wc_ref, bc_ref, j * FF_COLS)
            val = _conv_cols(h_ref, halo_ref, valid, wc_ref, bc_ref, D_FF + j * FF_COLS)
            hc_ref[:, j * FF_COLS:(j + 1) * FF_COLS] = gate.astype(BF16)
            hc_ref[:, D_FF + j * FF_COLS:D_FF + (j + 1) * FF_COLS] = val.astype(BF16)
            a = (gate * _sigmoid(gate) * val).astype(BF16)
            a_ref[:, j * FF_COLS:(j + 1) * FF_COLS] = a
            x2 = x2 + jnp.dot(a, wd_ref[j * FF_COLS:(j + 1) * FF_COLS, :], preferred_element_type=F32)
        r = lax.rsqrt(jnp.mean(x2 * x2, axis=-1, keepdims=True) + EPS)
        xn = x2 * r
        g = gf_ref[...]
        e = xn * g - tgt_ref[...]
        loss_ref[...] += (0.5 / D_MODEL) * jnp.sum(e * e).reshape(1, 1)
        dy = e * (1.0 / D_MODEL)
        dgf_ref[...] += jnp.sum(dy * xn, axis=0, keepdims=True)
        dxn = dy * g
        dx2 = r * (dxn - xn * jnp.mean(dxn * xn, axis=-1, keepdims=True))
        dx2_ref[...] = dx2
        dx2b_ref[...] = dx2.astype(BF16)

    row = lambda w: pl.BlockSpec((tm, w), lambda i: (i, 0))
    halo = pl.BlockSpec((HALO, n), lambda i: (jnp.maximum(i * (tm // HALO) - 1, 0), 0))
    return _pcall(body, "ffn_down_loss", (t // tm,),
                  [row(n), halo, row(D_MODEL), row(D_MODEL), _full((CONV_W, n)), _full((1, n)),
                   _full((D_FF, D_MODEL)), _full((1, D_MODEL))],
                  [row(n), row(D_FF), row(D_MODEL), row(D_MODEL), _full((1, 1)), _full((1, D_MODEL))],
                  [_sds((t, n), BF16), _sds((t, D_FF), BF16), _sds((t, D_MODEL)), _sds((t, D_MODEL), BF16),
                   _sds((1, 1)), _sds((1, D_MODEL))],
                  )(h, h, x1, tgt, w_conv, b_conv, w_down, g_final)


def _wgrad(a, b, name, tn, out_dtype=F32, band=None, after=None):
    t, m = a.shape
    n = b.shape[1] if band is None else band
    nbands = 1 if band is None else b.shape[1] // band
    after = b if after is None else after

    def body(a_ref, b_ref, after_ref, o_ref):
        o_ref[...] = _dot_tn(a_ref[...], b_ref[...]).astype(out_dtype)

    return _pcall(body, name, (m // tn,),
                  [pl.BlockSpec((t, tn), lambda i: (0, i)), pl.BlockSpec((t, n), lambda i: (0, i % nbands)),
                   pl.BlockSpec(memory_space=pl.ANY)],
                  pl.BlockSpec((tn, n), lambda i: (i, 0)), _sds((m, n), out_dtype))(a, b, after)


def _ffn_bwd_act(dx2b, hc, w_down, tm):
    t = hc.shape[0]
    n = 2 * D_FF

    def body(dx2_ref, hc_ref, wd_ref, dhc_ref, dbc_ref):
        @pl.when(pl.program_id(0) == 0)
        def _():
            dbc_ref[...] = jnp.zeros_like(dbc_ref)

        dx2 = dx2_ref[...]
        for j in range(D_FF // FF_COLS):
            gs = slice(j * FF_COLS, (j + 1) * FF_COLS)
            vs = slice(D_FF + j * FF_COLS, D_FF + (j + 1) * FF_COLS)
            gate = hc_ref[:, gs].astype(F32)
            val = hc_ref[:, vs].astype(F32)
            da = _dot_nt(dx2, wd_ref[gs, :])
            sg = _sigmoid(gate)
            dgate = da * val * (sg * (1.0 + gate * (1.0 - sg)))
            dval = da * (gate * sg)
            dhc_ref[:, gs] = dgate.astype(BF16)
            dhc_ref[:, vs] = dval.astype(BF16)
            dbc_ref[:, gs] += jnp.sum(dgate, axis=0, keepdims=True)
            dbc_ref[:, vs] += jnp.sum(dval, axis=0, keepdims=True)

    row = lambda w: pl.BlockSpec((tm, w), lambda i: (i, 0))
    return _pcall(body, "ffn_bwd_act", (t // tm,),
                  [row(D_MODEL), row(n), _full((D_FF, D_MODEL))],
                  [row(n), _full((1, n))],
                  [_sds((t, n), BF16), _sds((1, n))],
                  )(dx2b, hc, w_down)


def _ffn_bwd_up(dhc, h, dx2, x1, w_conv, w_up, g_ffn, seq, tm):
    t = dhc.shape[0]
    tps = seq // tm
    n = 2 * D_FF
    last = t // HALO - 1

    def body(dhc_ref, halo_ref, h_ref, dx2_ref, x1_ref, wc_ref, wu_ref, gf_ref,
             dh_ref, dx1_ref, dx1b_ref, dgf_ref, dwc_ref):
        i = pl.program_id(0)

        @pl.when(i == 0)
        def _():
            dgf_ref[...] = jnp.zeros_like(dgf_ref)
            dwc_ref[...] = jnp.zeros_like(dwc_ref)

        valid = ((i + 1) % tps) != 0
        du2 = jnp.zeros((tm, D_MODEL), F32)
        for j in range(n // FF_COLS):
            cs = slice(j * FF_COLS, (j + 1) * FF_COLS)
            cur = dhc_ref[:, cs].astype(F32)
            nxt = jnp.where(valid, halo_ref[:, cs].astype(F32), 0.0)
            full = jnp.concatenate([cur, nxt], axis=0)
            d1 = pltpu.roll(full, tm + HALO - 1, axis=0)[:tm]
            d2 = pltpu.roll(full, tm + HALO - 2, axis=0)[:tm]
            dh = (cur * wc_ref[2:3, cs] + d1 * wc_ref[1:2, cs] + d2 * wc_ref[0:1, cs]).astype(BF16)
            dh_ref[:, cs] = dh
            du2 = du2 + _dot(dh, wu_ref[cs, :])
            hv = h_ref[:, cs].astype(F32)
            dwc_ref[0:1, cs] += jnp.sum(hv * d2, axis=0, keepdims=True)
            dwc_ref[1:2, cs] += jnp.sum(hv * d1, axis=0, keepdims=True)
            dwc_ref[2:3, cs] += jnp.sum(hv * cur, axis=0, keepdims=True)
        x1 = x1_ref[...]
        r = lax.rsqrt(jnp.mean(x1 * x1, axis=-1, keepdims=True) + EPS)
        xn = x1 * r
        dgf_ref[...] += jnp.sum(du2 * xn, axis=0, keepdims=True)
        dxn = du2 * gf_ref[...]
        dx1 = dx2_ref[...] + r * (dxn - xn * jnp.mean(dxn * xn, axis=-1, keepdims=True))
        dx1_ref[...] = dx1
        dx1b_ref[...] = dx1.astype(BF16)

    row = lambda w: pl.BlockSpec((tm, w), lambda i: (i, 0))
    halo = pl.BlockSpec((HALO, n), lambda i: (jnp.minimum((i + 1) * (tm // HALO), last), 0))
    return _pcall(body, "ffn_bwd_up", (t // tm,),
                  [row(n), halo, row(n), row(D_MODEL), row(D_MODEL), _full((CONV_W, n)), _full((n, D_MODEL)),
                   _full((1, D_MODEL))],
                  [row(n), row(D_MODEL), row(D_MODEL), _full((1, D_MODEL)), _full((CONV_W, n))],
                  [_sds((t, n), BF16), _sds((t, D_MODEL)), _sds((t, D_MODEL), BF16), _sds((1, D_MODEL)),
                   _sds((CONV_W, n))],
                  )(dhc, dhc, h, dx2, x1, w_conv, w_up, g_ffn)


def _mix_bwd(dx1, y0, o, zh, zgt, pa, pb, w_glu, b_glu, gain, w_pa, w_pb, w_out, tm):
    t = dx1.shape[0]

    def body(dx1_ref, y0_ref, o_ref, zg_ref, zgt_ref, pa_ref, pb_ref, wglu_ref, bglu_ref, gain_ref, wpa_ref,
             wpb_ref, wout_ref,
             dy0_ref, do_ref, dzg_ref, dzgt_ref, m_ref, dpa_ref, dpb_ref, ya1_ref, dpre_ref, dbglu_ref, dgain_ref):
        @pl.when(pl.program_id(0) == 0)
        def _():
            dbglu_ref[...] = jnp.zeros_like(dbglu_ref)
            dgain_ref[...] = jnp.zeros_like(dgain_ref)

        dm = _dot_nt(dx1_ref[...], wout_ref[...])
        sga = _sigmoid(zgt_ref[:, 0:D_MODEL].astype(F32))
        sgb = _sigmoid(zgt_ref[:, D_MODEL:].astype(F32))
        pa = pa_ref[...].astype(F32)
        pb = pb_ref[...].astype(F32)
        m_ref[...] = (sga * pa + sgb * pb).astype(BF16)
        dzgt_ref[:, 0:D_MODEL] = (dm * pa * sga * (1.0 - sga)).astype(BF16)
        dzgt_ref[:, D_MODEL:] = (dm * pb * sgb * (1.0 - sgb)).astype(BF16)
        dpa = (dm * sga).astype(BF16)
        dpb = (dm * sgb).astype(BF16)
        dpa_ref[...] = dpa
        dpb_ref[...] = dpb
        dya2 = _dot_nt(dpa, wpa_ref[...])
        dyb = _dot_nt(dpb, wpb_ref[...])
        y0 = y0_ref[...]
        ya1 = _gelu(y0)
        ya1_ref[...] = ya1.astype(BF16)
        s = _sigmoid(_dot(ya1, wglu_ref[...]) + bglu_ref[...])
        dpre = dya2 * ya1 * s * (1.0 - s)
        dpre_ref[...] = dpre.astype(BF16)
        dbglu_ref[...] += jnp.sum(dpre, axis=0, keepdims=True)
        dya1 = dya2 * s + _dot_nt(dpre, wglu_ref[...])
        dy0_ref[...] = dya1 * _gelu_grad(y0)
        ov = o_ref[...]
        zg = zg_ref[...]
        oh = ov * _head_rms(ov)
        on = oh * gain_ref[...]
        sz = _sigmoid(zg)
        dzg_ref[...] = (dyb * on * (sz * (1.0 + zg * (1.0 - sz)))).astype(BF16)
        don = dyb * (zg * sz)
        dgain_ref[...] += jnp.sum(don * oh, axis=0, keepdims=True)
        doh = don * gain_ref[...]
        do_ref[...] = _head_rms(ov) * (doh - oh * _head_mean(doh * oh))

    row = lambda w: pl.BlockSpec((tm, w), lambda i: (i, 0))
    return _pcall(body, "mix_bwd", (t // tm,),
                  [row(D_MODEL), row(S5_WIDTH), row(HG_WIDTH), pl.BlockSpec((tm, HG_WIDTH), lambda i: (i, 3)),
                   row(2 * D_MODEL), row(D_MODEL), row(D_MODEL), _full((S5_WIDTH, S5_WIDTH)), _full((1, S5_WIDTH)),
                   _full((1, HG_WIDTH)), _full((S5_WIDTH, D_MODEL)), _full((HG_WIDTH, D_MODEL)),
                   _full((D_MODEL, D_MODEL))],
                  [row(S5_WIDTH), row(HG_WIDTH), row(HG_WIDTH), row(2 * D_MODEL), row(D_MODEL), row(D_MODEL),
                   row(D_MODEL), row(S5_WIDTH), row(S5_WIDTH), _full((1, S5_WIDTH)), _full((1, HG_WIDTH))],
                  [_sds((t, S5_WIDTH)), _sds((t, HG_WIDTH)), _sds((t, HG_WIDTH), BF16), _sds((t, 2 * D_MODEL), BF16),
                   _sds((t, D_MODEL), BF16), _sds((t, D_MODEL), BF16), _sds((t, D_MODEL), BF16),
                   _sds((t, S5_WIDTH), BF16), _sds((t, S5_WIDTH), BF16), _sds((1, S5_WIDTH)), _sds((1, HG_WIDTH))],
                  )(dx1, y0, o, zh, zgt, pa, pb, w_glu, b_glu, gain, w_pa, w_pb, w_out)


def _s5_bwd(dy0, za, xs, c_bands, b_bands, lam, dskip, nb, seq, ts):
    nts = seq // ts

    def body(dy0_ref, za_ref, xs_ref, halo_ref, cr_ref, ci_ref, br_ref, bi_ref, lam_ref, d_ref,
             dza_ref, a_ref, dlam_ref, dd_ref, acc_ref, st_ref):
        j = pl.program_id(0)

        @pl.when(j == 0)
        def _():
            dlam_ref[...] = jnp.zeros_like(dlam_ref)
            dd_ref[...] = jnp.zeros_like(dd_ref)
            st_ref[...] = jnp.zeros_like(st_ref)

        for b in range(nb):
            dy0 = dy0_ref[b]
            for q in range(S5_BANDS):
                ch, st = _band(q)
                acc_ref[b, :, st] = _dot(dy0[:, ch], cr_ref[q])
                acc_ref[b, :, _im(st)] = _dot(dy0[:, ch], ci_ref[q])
        _complex_scan(acc_ref, lam_ref, st_ref, nb, ts, reverse=True)
        shift = _shift_matrix(ts)
        top = lax.broadcasted_iota(jnp.int32, (SUBLANES, S5_LANES), 0) == 0
        for b in range(nb):
            a_ref[b] = acc_ref[b].astype(BF16)
            first = jnp.where(j == nts - 1, 0.0, halo_ref[b, HALO - 1:HALO, :].astype(F32))

            def shifted(cols):
                xp = jnp.dot(shift, xs_ref[b, :, cols], preferred_element_type=F32)
                return jnp.concatenate([xp[:SUBLANES] + jnp.where(top, first[:, cols], 0.0), xp[SUBLANES:]], axis=0)

            for cc in range(S5_N // S5_LANES):
                re = slice(cc * S5_LANES, (cc + 1) * S5_LANES)
                ar, ai, xr, xi = acc_ref[b, :, re], acc_ref[b, :, _im(re)], shifted(re), shifted(_im(re))
                dlam_ref[0:1, re] += jnp.sum(ar * xr + ai * xi, axis=0, keepdims=True)
                dlam_ref[1:2, re] += jnp.sum(ai * xr - ar * xi, axis=0, keepdims=True)
            dy0 = dy0_ref[b]
            for q in range(S5_BANDS):
                ch, st = _band(q)
                dza_ref[b, :, ch] = (_dot(a_ref[b, :, st], br_ref[q]) + _dot(a_ref[b, :, _im(st)], bi_ref[q])
                                     + d_ref[:, ch] * dy0[:, ch]).astype(BF16)
            dd_ref[...] += jnp.sum(dy0 * za_ref[b], axis=0, keepdims=True)

    tile = lambda j: nts - 1 - j
    tok = lambda w: pl.BlockSpec((nb, ts, w), lambda j: (0, tile(j), 0))
    halo = pl.BlockSpec((nb, HALO, 2 * S5_N), lambda j: (0, jnp.maximum(tile(j) * (ts // HALO) - 1, 0), 0))
    to_st, to_ch = _full((S5_BANDS, BAND_CH, BAND_ST)), _full((S5_BANDS, BAND_ST, BAND_CH))
    return _pcall(body, "s5_bwd", (nts,),
                  [tok(S5_WIDTH), tok(S5_WIDTH), tok(2 * S5_N), halo, to_st, to_st, to_ch, to_ch,
                   _full((2, S5_N)), _full((1, S5_WIDTH))],
                  [tok(S5_WIDTH), tok(2 * S5_N), _full((2, S5_N)), _full((1, S5_WIDTH))],
                  [_sds((nb, seq, S5_WIDTH), BF16), _sds((nb, seq, 2 * S5_N), BF16), _sds((2, S5_N)),
                   _sds((1, S5_WIDTH))],
                  scratch=[pltpu.VMEM((nb, ts, 2 * S5_N), F32), pltpu.VMEM((nb, 2, S5_N), F32)],
                  )(dy0, za, xs, xs, *c_bands, *b_bands, lam, dskip)


def _hgrn_bwd(zh, do, sts, lb, nb, seq):
    nc = seq // CHUNK

    def body(zh_ref, do_ref, sts_ref, lb_ref, dz_ref, dlb_ref, dst_ref):
        @pl.when(pl.program_id(0) == 0)
        def _():
            dst_ref[...] = jnp.zeros_like(dst_ref)
            dlb_ref[...] = jnp.zeros_like(dlb_ref)

        row = lax.broadcasted_iota(jnp.int32, (CHUNK, CHUNK), 0)
        causal = row >= lax.broadcasted_iota(jnp.int32, (CHUNK, CHUNK), 1)
        last_row = lax.broadcasted_iota(jnp.int32, (CHUNK, HG_HEAD), 0) == CHUNK - 1
        for b in range(nb):
            for h in range(HG_HEADS):
                hs = slice(h * HG_HEAD, (h + 1) * HG_HEAD)
                zq = zh_ref[b, :, h * HG_HEAD:(h + 1) * HG_HEAD]
                zf = zh_ref[b, :, HG_WIDTH + h * HG_HEAD:HG_WIDTH + (h + 1) * HG_HEAD]
                zi = zh_ref[b, :, 2 * HG_WIDTH + h * HG_HEAD:2 * HG_WIDTH + (h + 1) * HG_HEAD]
                lbh = lb_ref[:, hs]
                sf, f, sq, qa, bc, bm, bl = _hgrn_gates(zq, zf, lbh)
                k = 1.0 - f
                e_qt = jnp.exp(bc - bm)
                e_kt = jnp.exp(bm - bc)
                e_b = jnp.exp(bc)
                e_kd = jnp.exp(bl - bc)
                e_l = jnp.exp(bl)
                qt, kt, qb, kd = qa * e_qt, k * e_kt, qa * e_b, k * e_kd
                a = jnp.where(causal, _dot_nt(qt, kt), 0.0)
                st = sts_ref[b, 0, h]
                dst = dst_ref[b, h]
                dov = do_ref[b, :, hs]
                da = jnp.where(causal, _dot_nt(dov, zi), 0.0)
                dqt = _hdot(da, kt)
                dkt = _hdot_tn(da, qt)
                dqb = _dot(dov, st)
                di = _dot_tn(a, dov) + _dot_nt(kd, dst)
                dkd = _dot(zi, dst)
                de_l = jnp.sum(dst * st, axis=0, keepdims=True)
                dst_ref[b, h] = dst * e_l + _dot_tn(dov, qb)
                dqa = dqt * e_qt + dqb * e_b
                dk = dkt * e_kt + dkd * e_kd
                dbl = jnp.sum(dkd * kd, axis=0, keepdims=True) + de_l * e_l
                db = dqt * qt - dkt * kt + dqb * qb - dkd * kd + jnp.where(last_row, dbl, 0.0)
                df = _cumsum_rows(db, reverse=True) / f - dk
                dzq = dqa * QSCALE * (sq * (1.0 + zq * (1.0 - sq)))
                dzf = df * (1.0 - lbh) * sf * (1.0 - sf)
                dz_ref[b, :, h * HG_HEAD:(h + 1) * HG_HEAD] = dzq.astype(BF16)
                dz_ref[b, :, HG_WIDTH + h * HG_HEAD:HG_WIDTH + (h + 1) * HG_HEAD] = dzf.astype(BF16)
                dz_ref[b, :, 2 * HG_WIDTH + h * HG_HEAD:2 * HG_WIDTH + (h + 1) * HG_HEAD] = di.astype(BF16)
                dlb_ref[:, hs] += jnp.sum(df * (1.0 - sf), axis=0, keepdims=True)

    rev = lambda c: nc - 1 - c
    return _pcall(body, "hgrn_bwd", (nc,),
                  [pl.BlockSpec((nb, CHUNK, 4 * HG_WIDTH), lambda c: (0, rev(c), 0)),
                   pl.BlockSpec((nb, CHUNK, HG_WIDTH), lambda c: (0, rev(c), 0)),
                   pl.BlockSpec((nb, 1, HG_HEADS, HG_HEAD, HG_HEAD), lambda c: (0, rev(c), 0, 0, 0)),
                   _full((1, HG_WIDTH))],
                  [pl.BlockSpec((nb, CHUNK, 3 * HG_WIDTH), lambda c: (0, rev(c), 0)), _full((1, HG_WIDTH))],
                  [_sds((nb, seq, 3 * HG_WIDTH), BF16), _sds((1, HG_WIDTH))],
                  scratch=[pltpu.VMEM((nb, HG_HEADS, HG_HEAD, HG_HEAD), F32)])(zh, do, sts, lb)


def _in_proj_bwd(dza, dzh, dzg, dzgt, dx1, x, g_mix, w_in, tm):
    t = x.shape[0]

    def body(dza_ref, dzh_ref, dzg_ref, dzgt_ref, dx1_ref, x_ref, g_ref, w_ref, dz_ref, dx_ref, dg_ref):
        @pl.when(pl.program_id(0) == 0)
        def _():
            dg_ref[...] = jnp.zeros_like(dg_ref)

        c1, c2, c3 = S5_WIDTH, S5_WIDTH + 3 * HG_WIDTH, S5_WIDTH + 4 * HG_WIDTH
        dz_ref[:, 0:c1] = dza_ref[...]
        dz_ref[:, c1:c2] = dzh_ref[...]
        dz_ref[:, c2:c3] = dzg_ref[...]
        dz_ref[:, c3:] = dzgt_ref[...]
        du = _dot(dz_ref[...], w_ref[...])
        xv = x_ref[...]
        r = lax.rsqrt(jnp.mean(xv * xv, axis=-1, keepdims=True) + EPS)
        xn = xv * r
        dg_ref[...] += jnp.sum(du * xn, axis=0, keepdims=True)
        dxn = du * g_ref[...]
        dx_ref[...] = dx1_ref[...] + r * (dxn - xn * jnp.mean(dxn * xn, axis=-1, keepdims=True))

    row = lambda w: pl.BlockSpec((tm, w), lambda i: (i, 0))
    return _pcall(body, "in_proj_bwd", (t // tm,),
                  [row(S5_WIDTH), row(3 * HG_WIDTH), row(HG_WIDTH), row(2 * D_MODEL), row(D_MODEL), row(D_MODEL),
                   _full((1, D_MODEL)), _full((N_IN, D_MODEL))],
                  [row(N_IN), row(D_MODEL), _full((1, D_MODEL))],
                  [_sds((t, N_IN), BF16), _sds((t, D_MODEL)), _sds((1, D_MODEL))],
                  )(dza, dzh, dzg, dzgt, dx1, x, g_mix, w_in)


def _after(value, token):
    return value + token[0, 0]


def _local_step(x3, tgt3, weights, sp, emit, emit_small):
    nb, seq, _ = x3.shape
    t = nb * seq
    tm = _token_tile(seq)
    x = x3.reshape(t, D_MODEL)
    tgt = tgt3.reshape(t, D_MODEL)
    row = lambda v: v.reshape(1, -1)

    a_re, a_im, b_re, b_im = sp["s5_a_re"], sp["s5_a_im"], sp["s5_b_re"], sp["s5_b_im"]
    ldt = sp["s5_log_dt"].reshape(S5_GROUPS, 1)
    lr, li, bb_re, bb_im, lb = _params_fwd(a_re, a_im, ldt, b_re, b_im, sp["hg_lb_logits"])
    lam = jnp.concatenate([lr.reshape(1, S5_N), li.reshape(1, S5_N)], axis=0)
    swap = lambda m: m.transpose(0, 2, 1)
    b_to_st = (_band_blocks(bb_re), _band_blocks(bb_im))
    b_to_ch = (_band_blocks(swap(bb_re)), _band_blocks(swap(bb_im)))
    c_to_ch = (_band_blocks(swap(sp["s5_c_re"])), _band_blocks(swap(-sp["s5_c_im"])))
    c_to_st = (_band_blocks(sp["s5_c_re"]), _band_blocks(-sp["s5_c_im"]))

    g_mix, g_ffn, g_final = row(sp["g_mix"]), row(sp["g_ffn"]), row(sp["g_final"])
    b_glu, gain, dskip, b_conv = row(sp["b_glu"]), row(sp["hg_norm_gain"]), row(sp["s5_d"]), row(sp["b_conv"])

    w_in = weights("in", lam, *b_to_st, *b_to_ch, *c_to_ch, *c_to_st)["w_in"]
    wide = min(2 * tm, seq)
    u, za, zh, zgt = _in_proj(x, g_mix, w_in, wide)
    seqs = lambda v: v.reshape(nb, seq, v.shape[-1])
    toks = lambda v: v.reshape(t, v.shape[-1])
    xs3, y0 = _s5_fwd(seqs(za), b_to_st, lam, c_to_ch, dskip, nb, seq, tm)
    xs, y0 = toks(xs3), toks(y0)
    o3, sts = _hgrn_fwd(zh.reshape(nb, seq, 4 * HG_WIDTH), lb, nb, seq)
    o = o3.reshape(t, HG_WIDTH)
    wm = weights("mix", y0, o3)
    weights.forward("ffn", wm["w_out"])
    x1, u2, pa, pb, ya2, yb = _mix_fwd(x, y0, o, zh, zgt, wm["w_glu"], b_glu, gain, wm["w_pa"], wm["w_pb"],
                                       wm["w_out"], g_ffn, wide)
    wf = weights("ffn", u2)
    h = _ffn_up(u2, wf["w_up"], min(4 * tm, t))
    hc, a, dx2, dx2b, loss, dg_final = _ffn_down_loss(h, x1, tgt, wf["w_conv"], b_conv, wf["w_down"], g_final,
                                                      seq, tm)

    def wgrad(a, b, name):
        return _wgrad(a, b, name, 512 if a.shape[1] % 512 == 0 else 256, out_dtype=BF16)

    dhc, db_conv = _ffn_bwd_act(dx2b, hc, wf["w_down"], tm)
    dw_down = wgrad(a, dx2b, "dw_down")
    dh, dx1, dx1b, dg_ffn, dw_conv = _ffn_bwd_up(dhc, h, dx2, x1, wf["w_conv"], wf["w_up"], g_ffn, seq, tm)
    sent = emit({"w_up": wgrad(dh, u2, "dw_up"), "w_conv": dw_conv, "w_down": dw_down})
    (dy0, do, dzg, dzgt, m, dpa, dpb, ya1, dpre, db_glu, dgain) = _mix_bwd(
        dx1b, y0, o, zh, zgt, pa, pb, wm["w_glu"], _after(b_glu, sent), gain, wm["w_pa"], wm["w_pb"], wm["w_out"], tm)
    sent = emit({"w_out": wgrad(m, dx1b, "dw_out"), "w_pa": wgrad(ya2, dpa, "dw_pa"),
                 "w_pb": wgrad(yb, dpb, "dw_pb"), "w_glu": wgrad(ya1, dpre, "dw_glu")})
    dzh3, dlb = _hgrn_bwd(zh.reshape(nb, seq, 4 * HG_WIDTH), do.reshape(nb, seq, HG_WIDTH), sts, _after(lb, sent),
                          nb, seq)
    dza, a_s5, dlam, dd = _s5_bwd(seqs(dy0), seqs(za), xs3, c_to_st, b_to_ch, lam, dskip, nb, seq, tm)
    dza, a_s5 = toks(dza), toks(a_s5)
    dz, dx, dg_mix = _in_proj_bwd(dza, dzh3.reshape(t, 3 * HG_WIDTH), dzg, dzgt, dx1, x, g_mix, w_in, tm)
    sent = emit({"w_in": wgrad(dz, u, "dw_in")})

    band = HG_HEAD
    dbb_band = _wgrad(a_s5, za, "dbb_s5", 512, band=band, after=sent)
    dc_band = _wgrad(xs, dy0, "dc_s5", 512, band=band, after=sent)
    dbb_re = swap(_diag_blocks(dbb_band[:S5_N], S5_STATE, S5_GROUP))
    dbb_im = swap(_diag_blocks(dbb_band[S5_N:], S5_STATE, S5_GROUP))
    dc_re = swap(_diag_blocks(dc_band[:S5_N], S5_STATE, S5_GROUP))
    dc_im = -swap(_diag_blocks(dc_band[S5_N:], S5_STATE, S5_GROUP))
    da_re, da_im, dldt, db_re, db_im, dlogits = _params_bwd(
        a_re, a_im, ldt, b_re, b_im, sp["hg_lb_logits"],
        dlam[0].reshape(S5_GROUPS, S5_STATE), dlam[1].reshape(S5_GROUPS, S5_STATE), dbb_re, dbb_im, dlb)
    emit_small({"g_mix": dg_mix, "s5_a_re": da_re, "s5_a_im": da_im, "s5_log_dt": dldt.reshape(1, S5_GROUPS),
                "s5_b_re": db_re, "s5_b_im": db_im, "s5_c_re": dc_re, "s5_c_im": dc_im, "s5_d": dd, "b_glu": db_glu,
                "hg_lb_logits": dlogits, "hg_norm_gain": dgain, "g_ffn": dg_ffn, "b_conv": db_conv,
                "g_final": dg_final, "loss": loss})
    return dx.reshape(nb, seq, D_MODEL)


def _mesh_peers():
    x, y, c = lax.axis_index("x"), lax.axis_index("y"), lax.axis_index("c")
    peers = []
    for k in range(1, N_DEV):
        px, py, pc = (1 - x if k & 4 else x), (1 - y if k & 2 else y), (1 - c if k & 1 else c)
        peers.append((k, (px, py, pc), 4 * px + 2 * py + pc))
    return 4 * x + 2 * y + c, peers


_HBM = pl.BlockSpec(memory_space=pltpu.HBM)
_SEM = pl.BlockSpec(memory_space=pltpu.SEMAPHORE)


_EFFECT = pltpu.CompilerParams(has_side_effects=pltpu.SideEffectType.DATAFLOW_SIDE_EFFECTING)


def _remote(src, dst, send_sem, recv_sem, to):
    return pltpu.make_async_remote_copy(src_ref=src, dst_ref=dst, send_sem=send_sem, recv_sem=recv_sem,
                                        device_id=to, device_id_type=pl.DeviceIdType.MESH)


def _exchange_start(name, arrays, after):
    n = len(arrays)
    srcs = [pltpu.with_memory_space_constraint(a, pltpu.HBM) for a in arrays]
    lands = [pltpu.with_memory_space_constraint(lax.empty(a.shape, a.dtype), pltpu.HBM) for a in arrays]
    copies = (N_DEV - 1) * n

    def body(*refs):
        src_refs, land_refs = refs[:n], refs[n:2 * n]
        send_sems, recv_sems, token = refs[2 * n + 1], refs[2 * n + 2], refs[-1]
        my_slab, peers = _mesh_peers()
        for k, peer, slab in peers:
            for i in range(n):
                s = (k - 1) * n + i
                _remote(src_refs[i].at[slab], land_refs[i].at[my_slab], send_sems.at[s], recv_sems.at[s], peer).start()
        token[...] = jnp.zeros_like(token)

    outs = pl.pallas_call(
        body, name=name,
        out_shape=(pltpu.SemaphoreType.DMA((copies,)), pltpu.SemaphoreType.DMA((copies,)),
                   *[pltpu.HBM(a.shape, a.dtype) for a in lands], _sds((SUBLANES, LANES))),
        in_specs=[_HBM] * (2 * n) + [pl.BlockSpec(memory_space=pl.ANY)],
        out_specs=(_SEM, _SEM, *[_HBM] * n, pl.BlockSpec(memory_space=pltpu.VMEM)),
        input_output_aliases={n + i: 2 + i for i in range(n)}, compiler_params=_EFFECT,
    )(*srcs, *lands, after)
    return (outs[0], outs[1], srcs, outs[2:2 + n]), outs[-1]


def _exchange_wait(name, state, *after):
    send_sems, recv_sems, srcs, lands = state
    n = len(lands)

    def body(*refs):
        src_refs, land_refs = refs[:n], refs[n:2 * n]
        send_ref, recv_ref = refs[2 * n], refs[2 * n + 1]
        _, peers = _mesh_peers()
        for k, peer, slab in peers:
            for i in range(n):
                s = (k - 1) * n + i
                copy = _remote(src_refs[i].at[slab], land_refs[i].at[slab], send_ref.at[s], recv_ref.at[s], peer)
                copy.wait_send()
                copy.wait_recv()

    outs = pl.pallas_call(
        body, name=name,
        out_shape=tuple(pltpu.HBM(a.shape, a.dtype) for a in lands),
        in_specs=[_HBM] * (2 * n) + [_SEM, _SEM] + [pl.BlockSpec(memory_space=pl.ANY)] * len(after),
        out_specs=tuple([_HBM] * n),
        input_output_aliases={n + i: i for i in range(n)}, compiler_params=_EFFECT,
    )(*srcs, *lands, send_sems, recv_sems, *after)
    return list(outs), list(srcs)


def _slab(pos):
    return 4 * pos[0] + 2 * pos[1] + pos[2]


def _chip_routes():
    x, y, c = lax.axis_index("x"), lax.axis_index("y"), lax.axis_index("c")
    return (x, y, c), (x, y, 1 - c), [(1 - x, y, c), (x, 1 - y, c), (1 - x, 1 - y, c)]


def _gather_start(name, arrays, after):
    n = len(arrays)
    me = 4 * lax.axis_index("x") + 2 * lax.axis_index("y") + lax.axis_index("c")
    srcs = [pltpu.with_memory_space_constraint(a, pltpu.HBM) for a in arrays]
    lands = [pltpu.with_memory_space_constraint(
        lax.dynamic_update_slice_in_dim(lax.empty((N_DEV,) + a.shape, a.dtype), a[None], me, 0), pltpu.HBM)
        for a in arrays]

    def body(*refs):
        src_refs, land_refs = refs[:n], refs[n:2 * n]
        send_sems, recv_sems, token = refs[2 * n + 1], refs[2 * n + 2], refs[-1]
        mine, sibling, chips = _chip_routes()
        for k, to in enumerate([sibling] + chips):
            for i in range(n):
                _remote(src_refs[i], land_refs[i].at[_slab(mine)], send_sems.at[k * n + i], recv_sems.at[k * n + i],
                        to).start()
        token[...] = jnp.zeros_like(token)

    outs = pl.pallas_call(
        body, name=name,
        out_shape=(pltpu.SemaphoreType.DMA((4 * n,)), pltpu.SemaphoreType.DMA((4 * n,)),
                   *[pltpu.HBM(a.shape, a.dtype) for a in lands], _sds((SUBLANES, LANES))),
        in_specs=[_HBM] * (2 * n) + [pl.BlockSpec(memory_space=pl.ANY)],
        out_specs=(_SEM, _SEM, *[_HBM] * n, pl.BlockSpec(memory_space=pltpu.VMEM)),
        input_output_aliases={n + i: 2 + i for i in range(n)}, compiler_params=_EFFECT,
    )(*srcs, *lands, after)
    return (outs[0], outs[1], srcs, outs[2:2 + n]), outs[-1]


def _gather_forward(name, state, *after):
    send_a, recv_a, srcs, lands = state
    n = len(lands)

    def body(*refs):
        land_refs, recv_a_ref = refs[:n], refs[n]
        send_b, recv_b = refs[n + 1 + len(after)], refs[n + 2 + len(after)]
        mine, sibling, chips = _chip_routes()
        for j, chip in enumerate(chips):
            for i in range(n):
                block = land_refs[i].at[_slab(chip)]
                _remote(block, block, send_b.at[j * n + i], recv_a_ref.at[(1 + j) * n + i], chip).wait_recv()
                _remote(block, block, send_b.at[j * n + i], recv_b.at[j * n + i], sibling).start()

    outs = pl.pallas_call(
        body, name=name,
        out_shape=(pltpu.SemaphoreType.DMA((3 * n,)), pltpu.SemaphoreType.DMA((3 * n,)),
                   *[pltpu.HBM(a.shape, a.dtype) for a in lands]),
        in_specs=[_HBM] * n + [_SEM] + [pl.BlockSpec(memory_space=pl.ANY)] * len(after),
        out_specs=(_SEM, _SEM, *[_HBM] * n),
        input_output_aliases={i: 2 + i for i in range(n)}, compiler_params=_EFFECT,
    )(*lands, recv_a, *after)
    return (send_a, recv_a, srcs, list(outs[2:])), (outs[0], outs[1])


def _gather_wait(name, state, forwarded, *after):
    send_a, recv_a, srcs, lands = state
    send_b, recv_b = forwarded
    n = len(lands)

    def body(*refs):
        src_refs, land_refs = refs[:n], refs[n:2 * n]
        sa, ra, sb, rb = refs[2 * n:2 * n + 4]
        mine, sibling, chips = _chip_routes()
        for i in range(n):
            for k, to in enumerate([sibling] + chips):
                _remote(src_refs[i], land_refs[i].at[_slab(mine)], sa.at[k * n + i], ra.at[k * n + i], to).wait_send()
            theirs = land_refs[i].at[_slab(sibling)]
            _remote(theirs, theirs, sa.at[i], ra.at[i], sibling).wait_recv()
            for j, chip in enumerate(chips):
                sent = land_refs[i].at[_slab(chip)]
                got = land_refs[i].at[_slab((chip[0], chip[1], sibling[2]))]
                _remote(sent, sent, sb.at[j * n + i], rb.at[j * n + i], sibling).wait_send()
                _remote(got, got, sb.at[j * n + i], rb.at[j * n + i], sibling).wait_recv()

    outs = pl.pallas_call(
        body, name=name,
        out_shape=tuple(pltpu.HBM(a.shape, a.dtype) for a in lands),
        in_specs=[_HBM] * (2 * n) + [_SEM] * 4 + [pl.BlockSpec(memory_space=pl.ANY)] * len(after),
        out_specs=tuple([_HBM] * n),
        input_output_aliases={n + i: i for i in range(n)}, compiler_params=_EFFECT,
    )(*srcs, *lands, send_a, recv_a, send_b, recv_b, *after)
    return list(outs), list(srcs)


def _join_cols(parts, name, tr):
    _, r, c = parts.shape

    def body(p_ref, o_ref):
        for j in range(N_DEV):
            o_ref[:, j * c:(j + 1) * c] = p_ref[j]

    return _pcall(body, name, (r // tr,), [pl.BlockSpec((N_DEV, tr, c), lambda i: (0, i, 0))],
                  pl.BlockSpec((tr, N_DEV * c), lambda i: (i, 0)), _sds((r, N_DEV * c), parts.dtype))(parts)


def _split_cols(full, name, tr):
    r, c = full.shape[0], full.shape[1] // N_DEV

    def body(f_ref, o_ref):
        for j in range(N_DEV):
            o_ref[j] = f_ref[:, j * c:(j + 1) * c]

    return _pcall(body, name, (r // tr,), [pl.BlockSpec((tr, N_DEV * c), lambda i: (i, 0))],
                  pl.BlockSpec((N_DEV, tr, c), lambda i: (0, i, 0)), _sds((N_DEV, r, c), full.dtype))(full)


def _my_slab():
    return (4 * lax.axis_index("x") + 2 * lax.axis_index("y") + lax.axis_index("c")).astype(jnp.int32).reshape(1)


def _adamw(parts, sent, w, m, v, name, tile):
    _, rows, cols = w.shape

    def body(me_ref, p_ref, s_ref, w_ref, m_ref, v_ref, g_out, d_out, m_out, v_out):
        me = me_ref[0]
        g = jnp.where(me == 0, s_ref[0], p_ref[0]).astype(F32)
        for k in range(1, N_DEV):
            g = g + jnp.where(me == k, s_ref[0], p_ref[k]).astype(F32)
        m1 = ADAM_B1 * m_ref[0] + (1.0 - ADAM_B1) * g
        v1 = ADAM_B2 * v_ref[0] + (1.0 - ADAM_B2) * (g * g)
        m_hat = m1 / (1.0 - ADAM_B1 ** ADAM_STEP)
        v_hat = v1 / (1.0 - ADAM_B2 ** ADAM_STEP)
        g_out[0] = g
        d_out[0] = -ADAM_LR * (m_hat / (jnp.sqrt(v_hat) + ADAM_EPS) + ADAM_WD * w_ref[0])
        m_out[0] = m1
        v_out[0] = v1

    row = pl.BlockSpec((1, tile, cols), lambda i, me: (0, i, 0))
    return pl.pallas_call(
        body, name=name, out_shape=[_sds((1, rows, cols))] * 4,
        grid_spec=pltpu.PrefetchScalarGridSpec(
            num_scalar_prefetch=1, grid=(rows // tile,),
            in_specs=[pl.BlockSpec((N_DEV, tile, cols), lambda i, me: (0, i, 0)),
                      pl.BlockSpec((1, tile, cols), lambda i, me: (me[0], i, 0)), row, row, row],
            out_specs=[row, row, row, row]),
        compiler_params=pltpu.CompilerParams(dimension_semantics=("arbitrary",), vmem_limit_bytes=VMEM_LIMIT),
    )(_my_slab(), parts, sent, w, m, v)


BIG = {
    "w_in": ((N_IN // N_DEV, D_MODEL), False, N_IN // N_DEV // 3),
    "w_glu": ((S5_WIDTH // N_DEV, S5_WIDTH), False, S5_WIDTH // N_DEV),
    "w_pa": ((S5_WIDTH, D_MODEL // N_DEV), True, S5_WIDTH),
    "w_pb": ((HG_WIDTH, D_MODEL // N_DEV), True, HG_WIDTH),
    "w_out": ((D_MODEL // N_DEV, D_MODEL), False, D_MODEL // N_DEV),
    "w_up": ((2 * D_FF // N_DEV, D_MODEL), False, 2 * D_FF // N_DEV // 4),
    "w_conv": ((CONV_W, 2 * D_FF // N_DEV), True, CONV_W),
    "w_down": ((D_FF // N_DEV, D_MODEL), False, D_FF // N_DEV // 2),
}
TRANSPOSED = ("w_in", "w_up", "s5_b_re", "s5_b_im")
UNALIGNED_COLS = ("w_conv",)


def _stored(n, arr):
    return jnp.swapaxes(arr, -1, -2) if n in TRANSPOSED else arr


def _join_shards(n, parts):
    (a, b), by_cols, _ = BIG[n]
    if not by_cols:
        return parts.reshape(N_DEV * a, b)
    if n in UNALIGNED_COLS:
        return _join_cols(parts, "join_" + n, min(a, 256))
    return parts.transpose(1, 0, 2).reshape(a, N_DEV * b)


def _split_shards(n, full):
    (a, b), by_cols, _ = BIG[n]
    if not by_cols:
        return full.reshape(N_DEV, a, b)
    if n in UNALIGNED_COLS:
        return _split_cols(full, "split_" + n, min(a, 256))
    return full.reshape(a, N_DEV, b).transpose(1, 0, 2)


SMALL_CORE = {
    "s5_b_re": GSC, "s5_b_im": GSC, "s5_c_re": GSC, "s5_c_im": GSC,
    "g_mix": (1, D_MODEL), "g_ffn": (1, D_MODEL), "g_final": (1, D_MODEL), "s5_d": (1, S5_WIDTH),
    "b_glu": (1, S5_WIDTH), "hg_norm_gain": (1, HG_WIDTH), "hg_lb_logits": (2, HG_WIDTH), "b_conv": (1, 2 * D_FF),
    "s5_log_dt": (1, S5_GROUPS), "s5_a_re": (S5_GROUPS, S5_STATE), "s5_a_im": (S5_GROUPS, S5_STATE), "loss": (1, 1),
}
BLOCK_ROWS = 32


def _small_rows():
    rows, r = {}, 0
    for n, core in SMALL_CORE.items():
        rows[n] = r
        r += BLOCK_ROWS if len(core) == 3 else -(-math.prod(core) // PACK_W)
    return rows, -(-r // SUBLANES) * SUBLANES


SMALL_ROW, SMALL_ROWS = _small_rows()


def _small_pieces(name):
    r, core = SMALL_ROW[name], SMALL_CORE[name]
    if len(core) == 3:
        return [((g, slice(None), slice(None)), slice(r + S5_GROUP * (g % 2), r + S5_GROUP * (g % 2 + 1)),
                 slice(S5_STATE * (g // 2), S5_STATE * (g // 2 + 1))) for g in range(S5_GROUPS)]
    pieces = []
    for i in range(core[0]):
        for c0 in range(0, core[1], PACK_W):
            w, flat = min(PACK_W, core[1] - c0), i * core[1] + c0
            pieces.append(((slice(i, i + 1), slice(c0, c0 + w)), slice(r + flat // PACK_W, r + flat // PACK_W + 1),
                           slice(flat % PACK_W, flat % PACK_W + w)))
    return pieces


def _core_index(ref, name, idx):
    return (0,) * (len(ref.shape) - len(SMALL_CORE[name])) + idx


def _pack_small_grads(grads):
    names = list(SMALL_CORE)

    def body(*refs):
        pack = refs[-1]
        pack[...] = jnp.zeros_like(pack)
        for ref, n in zip(refs, names):
            for idx, rows, lanes in _small_pieces(n):
                pack[rows, lanes] = ref[_core_index(ref, n, idx)]

    return _pcall(body, "pack_small_grads", (1,), [_full(grads[n].shape) for n in names],
                  _full((SMALL_ROWS, PACK_W)), _sds((SMALL_ROWS, PACK_W)))(*[grads[n] for n in names])


def _adamw_small(parts, sent, names, rows, given, name):
    lo, hi = rows
    k = len(names)
    shapes = [given[n].shape for n in names]

    def body(*refs):
        me, p_ref, s_ref, ins, outs = refs[0][0], refs[1], refs[2], refs[3:3 + 3 * k], refs[3 + 3 * k:3 + 7 * k]
        packs, results = refs[3 + 7 * k:6 + 7 * k], refs[6 + 7 * k:]
        for j, pack in enumerate(packs):
            pack[...] = jnp.zeros_like(pack)
            for ref, n in zip(ins[j * k:(j + 1) * k], names):
                for idx, prow, lanes in _small_pieces(n):
                    pack[slice(prow.start - lo, prow.stop - lo), lanes] = ref[_core_index(ref, n, idx)]
        mine = s_ref[lo:hi, :]
        g = jnp.where(me == 0, mine, p_ref[0, lo:hi, :])
        for d in range(1, N_DEV):
            g = g + jnp.where(me == d, mine, p_ref[d, lo:hi, :])
        m1 = ADAM_B1 * packs[1][...] + (1.0 - ADAM_B1) * g
        v1 = ADAM_B2 * packs[2][...] + (1.0 - ADAM_B2) * (g * g)
        m_hat = m1 / (1.0 - ADAM_B1 ** ADAM_STEP)
        v_hat = v1 / (1.0 - ADAM_B2 ** ADAM_STEP)
        results[0][...] = g
        results[1][...] = -ADAM_LR * (m_hat / (jnp.sqrt(v_hat) + ADAM_EPS) + ADAM_WD * packs[0][...])
        results[2][...] = m1
        results[3][...] = v1
        for j, result in enumerate(results):
            for ref, n in zip(outs[j * k:(j + 1) * k], names):
                for idx, prow, lanes in _small_pieces(n):
                    ref[_core_index(ref, n, idx)] = result[slice(prow.start - lo, prow.stop - lo), lanes]

    flat = _pcall(body, name, (1,),
                  [pl.BlockSpec(memory_space=pltpu.SMEM), _full(parts.shape), _full(sent.shape)]
                  + [_full(s) for s in shapes] * 3,
                  [_full(s) for s in shapes] * 4, [_sds(s) for s in shapes] * 4,
                  scratch=[pltpu.VMEM((hi - lo, PACK_W), F32)] * 7,
                  )(_my_slab(), parts, sent, *[given[pre + n] for pre in ("", "m_", "v_") for n in names])
    return {n: [flat[j * k + i] for j in range(4)] for i, n in enumerate(names)}


def kernel(x, g_mix, w_in, s5_a_re, s5_a_im, s5_log_dt, s5_b_re, s5_b_im, s5_c_re, s5_c_im, s5_d, w_glu, b_glu, hg_lb_logits, hg_norm_gain, w_pa, w_pb, w_out, g_ffn, w_up, w_conv, b_conv, w_down, g_final, loss_target, m_g_mix, m_w_in, m_s5_a_re, m_s5_a_im, m_s5_log_dt, m_s5_b_re, m_s5_b_im, m_s5_c_re, m_s5_c_im, m_s5_d, m_w_glu, m_b_glu, m_hg_lb_logits, m_hg_norm_gain, m_w_pa, m_w_pb, m_w_out, m_g_ffn, m_w_up, m_w_conv, m_b_conv, m_w_down, m_g_final, v_g_mix, v_w_in, v_s5_a_re, v_s5_a_im, v_s5_log_dt, v_s5_b_re, v_s5_b_im, v_s5_c_re, v_s5_c_im, v_s5_d, v_w_glu, v_b_glu, v_hg_lb_logits, v_hg_norm_gain, v_w_pa, v_w_pb, v_w_out, v_g_ffn, v_w_up, v_w_conv, v_b_conv, v_w_down, v_g_final):
    given = dict(locals())
    small_names = [n for n in SMALL_CORE if n != "loss"]

    pay = {n: given[n][0] if n == "w_conv" else _stored(n, given[n])[0].astype(BF16) for n in BIG}
    groups = {"in": ["w_in"], "mix": ["w_glu", "w_pa", "w_pb", "w_out"], "ffn": ["w_up", "w_down", "w_conv"]}
    gathers, order = {}, pay["w_in"]
    for grp, names in groups.items():
        gathers[grp], order = _gather_start("gather_" + grp + "_start", [pay[n] for n in names], order)

    forwards = {}

    def forward(grp, *after):
        if grp == "in":
            after = (*after, order)
        forwards[grp] = _gather_forward("gather_" + grp + "_forward", gathers[grp], *after)

    def weights(grp, *after):
        if grp not in forwards:
            forward(grp, *after)
        got, _ = _gather_wait("gather_" + grp + "_wait", *forwards[grp], *after)
        return {n: _join_shards(n, g) for n, g in zip(groups[grp], got)}

    weights.forward = forward

    in_flight, started = [], []

    def emit(grads):
        names = list(grads)
        state, token = _exchange_start("grads_" + names[0] + "_start", [_split_shards(n, grads[n]) for n in names],
                                       grads[names[0]])
        in_flight.append((names, state))
        return token

    def emit_small(grads):
        pack = _pack_small_grads(grads)
        state, token = _gather_start("grads_small_start", [pack], pack)
        in_flight.append((["small"], state))
        started.append(token)

    sp = {n: (given[n] if n in ("g_final", "hg_lb_logits") else _stored(n, given[n])[0]) for n in small_names}
    sp["g_mix"] = _after(sp["g_mix"], order)
    dx = _local_step(x, loss_target, weights, sp, emit, emit_small)

    res = {}
    after = [started[-1]]
    for names, state in in_flight:
        if names == ["small"]:
            state, forwarded = _gather_forward("grads_small_forward", state, *after)
            parts, sent = _gather_wait("grads_small_wait", state, forwarded)
        else:
            parts, sent = _exchange_wait("grads_" + names[0] + "_wait", state, *after)
        if names != ["small"]:
            after = []
            for n, part, mine in zip(names, parts, sent):
                raw = _adamw(part, mine, *[_stored(n, given[pre + n]) for pre in ("", "m_", "v_")], "adamw_" + n,
                             BIG[n][2])
                res[n] = [_stored(n, r) for r in raw]
                after.append(raw[0])
            continue
        sgiven = {pre + n: _stored(n, given[pre + n]) for pre in ("", "m_", "v_") for n in small_names}
        for pre in ("", "m_", "v_"):
            sgiven[pre + "g_final"] = given[pre + "g_final"].reshape(1, D_MODEL)
            sgiven[pre + "loss"] = jnp.zeros((1, 1), F32)
        raw = _adamw_small(parts[0], sent[0], list(SMALL_CORE), (0, SMALL_ROWS), sgiven, "adamw_small")
        res.update({n: [_stored(n, r) for r in raw[n]] for n in small_names})
        res["g_final"] = [r.reshape(D_MODEL) for r in raw["g_final"]]
        total_loss = raw["loss"][0].reshape(())
        after = [raw["s5_b_re"][0], raw["g_mix"][0]]
    return (total_loss, dx, *[res[n][0] for n in WEIGHT_ORDER], *[res[n][1] for n in WEIGHT_ORDER],
            *[res[n][2] for n in WEIGHT_ORDER], *[res[n][3] for n in WEIGHT_ORDER])
```

```python
import math

import jax
import jax.numpy as jnp
from jax import lax
from jax.experimental import pallas as pl
from jax.experimental.pallas import tpu as pltpu

F32 = jnp.float32
BF16 = jnp.bfloat16

D_MODEL = 1024
S5_WIDTH = 512
S5_GROUP = 16
S5_GROUPS = 32
S5_STATE = 64
S5_N = S5_GROUPS * S5_STATE
HG_WIDTH = 512
HG_HEAD = 128
HG_HEADS = 4
D_FF = 2816
CONV_W = 3
CHUNK = 64
N_IN = S5_WIDTH + 4 * HG_WIDTH + 2 * D_MODEL
EPS = 1e-6
QSCALE = HG_HEAD ** -0.5

ADAM_LR = 0.001
ADAM_B1 = 0.9
ADAM_B2 = 0.999
ADAM_EPS = 1e-08
ADAM_WD = 0.01
ADAM_STEP = 10

N_DEV = 8
V7X_VMEM_BYTES = 64 * 1024 * 1024
VMEM_LIMIT = V7X_VMEM_BYTES * 7 // 8
SUBLANES = 8
LANES = 128
PACK_W = 1024

WEIGHT_ORDER = ("g_mix", "w_in", "s5_a_re", "s5_a_im", "s5_log_dt", "s5_b_re", "s5_b_im", "s5_c_re", "s5_c_im",
                "s5_d", "w_glu", "b_glu", "hg_lb_logits", "hg_norm_gain", "w_pa", "w_pb", "w_out", "g_ffn",
                "w_up", "w_conv", "b_conv", "w_down", "g_final")


def _pcall(body, name, grid, in_specs, out_specs, out_shape, scratch=()):
    return pl.pallas_call(
        body, name=name, grid=grid, in_specs=in_specs, out_specs=out_specs, out_shape=out_shape,
        scratch_shapes=list(scratch),
        compiler_params=pltpu.CompilerParams(dimension_semantics=("arbitrary",) * len(grid),
                                             vmem_limit_bytes=VMEM_LIMIT),
    )


def _full(shape):
    return pl.BlockSpec(shape, lambda *_: (0,) * len(shape))


def _sds(shape, dtype=F32):
    return jax.ShapeDtypeStruct(shape, dtype)


def _dot(a, b):
    return jnp.dot(a.astype(BF16), b.astype(BF16), preferred_element_type=F32)


def _dot_nt(a, b):
    return lax.dot_general(a.astype(BF16), b.astype(BF16), (((1,), (1,)), ((), ())), preferred_element_type=F32)


def _dot_tn(a, b):
    return lax.dot_general(a.astype(BF16), b.astype(BF16), (((0,), (0,)), ((), ())), preferred_element_type=F32)


def _split(a):
    hi = a.astype(BF16)
    return hi, (a - hi.astype(F32)).astype(BF16)


def _hdot(a, b, dims=(((1,), (0,)), ((), ()))):
    (ah, al), (bh, bl) = _split(a), _split(b)
    dot = lambda p, q: lax.dot_general(p, q, dims, preferred_element_type=F32)
    return dot(ah, bh) + (dot(al, bh) + dot(ah, bl))


def _hdot_tn(a, b):
    return _hdot(a, b, (((0,), (0,)), ((), ())))


def _sigmoid(x):
    return jax.nn.sigmoid(x)


GELU_C = math.sqrt(2.0 / math.pi)
GELU_A = 0.044715


def _gelu(x):
    return 0.5 * x * (1.0 + jnp.tanh(GELU_C * (x + GELU_A * (x * x * x))))


def _gelu_grad(x):
    t = jnp.tanh(GELU_C * (x + GELU_A * (x * x * x)))
    return 0.5 * (1.0 + t) + 0.5 * x * (1.0 - t * t) * (GELU_C * (1.0 + 3.0 * GELU_A * x * x))


def _cumsum_rows(v, reverse=False):
    n = v.shape[0]
    row = lax.broadcasted_iota(jnp.int32, v.shape, 0)
    s = 1
    while s < n:
        if reverse:
            v = v + jnp.where(row < n - s, pltpu.roll(v, n - s, axis=0), 0.0)
        else:
            v = v + jnp.where(row >= s, pltpu.roll(v, s, axis=0), 0.0)
        s *= 2
    return v


def _token_tile(seq):
    return min(256, seq)


def _s5_coeffs(a_re, a_im, ldt):
    dt = jnp.exp(ldt)
    mag = jnp.exp(a_re * dt)
    ang = a_im * dt
    lb_re = mag * jnp.cos(ang)
    lb_im = mag * jnp.sin(ang)
    den = a_re * a_re + a_im * a_im
    n_re = lb_re - 1.0
    n_im = lb_im
    co_re = (n_re * a_re + n_im * a_im) / den
    co_im = (n_im * a_re - n_re * a_im) / den
    return lb_re, lb_im, co_re, co_im


GS, GSC = (S5_GROUPS, S5_STATE), (S5_GROUPS, S5_GROUP, S5_STATE)


def _params_fwd(a_re, a_im, ldt, bt_re, bt_im, logits):
    def body(are, aim, ld, bre, bim, lg, lr_o, li_o, bbr_o, bbi_o, lb_o):
        lr, li, co_re, co_im = _s5_coeffs(are[...], aim[...], ld[...])
        lr_o[...] = lr
        li_o[...] = li
        for g in range(S5_GROUPS):
            cr, ci = co_re[g:g + 1, :], co_im[g:g + 1, :]
            bbr_o[g] = cr * bre[g] - ci * bim[g]
            bbi_o[g] = cr * bim[g] + ci * bre[g]
        lb_o[...] = _sigmoid(lg[0:1, :] - lg[1:2, :])

    return _pcall(body, "params_fwd", (1,),
                  [_full(GS), _full(GS), _full((S5_GROUPS, 1)), _full(GSC), _full(GSC), _full((2, HG_WIDTH))],
                  [_full(GS), _full(GS), _full(GSC), _full(GSC), _full((1, HG_WIDTH))],
                  [_sds(GS), _sds(GS), _sds(GSC), _sds(GSC), _sds((1, HG_WIDTH))],
                  )(a_re, a_im, ldt, bt_re, bt_im, logits)


def _params_bwd(a_re, a_im, ldt, bt_re, bt_im, logits, dlr, dli, dbbr, dbbi, dlb):
    def body(are, aim, ld, bre, bim, lg, dlr_r, dli_r, dbbr_r, dbbi_r, dlb_r,
             dare_o, daim_o, dld_o, dbre_o, dbim_o, dlg_o, dcr_ref, dci_ref):
        (_, _, co_re, co_im), vjp = jax.vjp(_s5_coeffs, are[...], aim[...], ld[...])
        for g in range(S5_GROUPS):
            cr, ci = co_re[g:g + 1, :], co_im[g:g + 1, :]
            gr, gi, br, bi = dbbr_r[g], dbbi_r[g], bre[g], bim[g]
            dbre_o[g] = cr * gr + ci * gi
            dbim_o[g] = cr * gi - ci * gr
            dcr_ref[g:g + 1, :] = jnp.sum(gr * br + gi * bi, axis=0, keepdims=True)
            dci_ref[g:g + 1, :] = jnp.sum(gi * br - gr * bi, axis=0, keepdims=True)
        dare, daim, dld = vjp((dlr_r[...], dli_r[...], dcr_ref[...], dci_ref[...]))
        dare_o[...] = dare
        daim_o[...] = daim
        dld_o[...] = dld
        lb = _sigmoid(lg[0:1, :] - lg[1:2, :])
        d0 = dlb_r[...] * lb * (1.0 - lb)
        dlg_o[0:1, :] = d0
        dlg_o[1:2, :] = -d0

    return _pcall(body, "params_bwd", (1,),
                  [_full(GS), _full(GS), _full((S5_GROUPS, 1)), _full(GSC), _full(GSC), _full((2, HG_WIDTH)),
                   _full(GS), _full(GS), _full(GSC), _full(GSC), _full((1, HG_WIDTH))],
                  [_full(GS), _full(GS), _full((S5_GROUPS, 1)), _full(GSC), _full(GSC), _full((2, HG_WIDTH))],
                  [_sds(GS), _sds(GS), _sds((S5_GROUPS, 1)), _sds(GSC), _sds(GSC), _sds((2, HG_WIDTH))],
                  scratch=[pltpu.VMEM(GS, F32), pltpu.VMEM(GS, F32)],
                  )(a_re, a_im, ldt, bt_re, bt_im, logits, dlr, dli, dbbr, dbbi, dlb)


def _band_blocks(m):
    g, r, c = m.shape
    gb = g // S5_BANDS
    m4 = m.astype(BF16).reshape(S5_BANDS, gb, r, c)
    on_diag = jnp.eye(gb, dtype=bool)[None, :, None, :, None]
    return jnp.where(on_diag, m4[:, :, :, None, :], 0).reshape(S5_BANDS, gb * r, gb * c)


def _diag_blocks(band, r, c):
    g, nb = band.shape[0] // r, band.shape[1] // c
    on_diag = (jnp.arange(g) % nb)[:, None, None, None] == jnp.arange(nb)[None, None, :, None]
    return jnp.sum(jnp.where(on_diag, band.reshape(g, r, nb, c), 0.0), axis=2)


def _in_proj(x, g_mix, w_in, tm):
    t = x.shape[0]

    def body(x_ref, g_ref, w_ref, u_ref, za_ref, zh_ref, zg_ref):
        xv = x_ref[...]
        r = lax.rsqrt(jnp.mean(xv * xv, axis=-1, keepdims=True) + EPS)
        u = (xv * r * g_ref[...]).astype(BF16)
        u_ref[...] = u
        za_ref[...] = _dot_nt(u, w_ref[0:S5_WIDTH, :])
        zh_ref[...] = _dot_nt(u, w_ref[S5_WIDTH:S5_WIDTH + 4 * HG_WIDTH, :])
        zg_ref[...] = _dot_nt(u, w_ref[S5_WIDTH + 4 * HG_WIDTH:, :]).astype(BF16)

    row = lambda w: pl.BlockSpec((tm, w), lambda i: (i, 0))
    return _pcall(body, "in_proj", (t // tm,),
                  [row(D_MODEL), _full((1, D_MODEL)), _full((N_IN, D_MODEL))],
                  [row(D_MODEL), row(S5_WIDTH), row(4 * HG_WIDTH), row(2 * D_MODEL)],
                  [_sds((t, D_MODEL), BF16), _sds((t, S5_WIDTH)), _sds((t, 4 * HG_WIDTH)),
                   _sds((t, 2 * D_MODEL), BF16)],
                  )(x, g_mix, w_in)


S5_LANES = 512
S5_BANDS = 4


def _band(q):
    return (slice(q * S5_WIDTH // S5_BANDS, (q + 1) * S5_WIDTH // S5_BANDS),
            slice(q * S5_N // S5_BANDS, (q + 1) * S5_N // S5_BANDS))


def _im(st):
    return slice(S5_N + st.start, S5_N + st.stop)


SCAN_UNROLL = 8


def _complex_scan(buf_ref, lam_ref, st_ref, nb, ts, reverse):
    lanes = [slice(cc * S5_LANES, (cc + 1) * S5_LANES) for cc in range(S5_N // S5_LANES)]
    chains = [(b, re) for b in range(nb) for re in lanes]
    nch = len(chains)
    wr = {re.start: lam_ref[0:1, re] for re in lanes}
    wi = {re.start: -lam_ref[1:2, re] if reverse else lam_ref[1:2, re] for re in lanes}

    def block(ib, carry):
        vr, vi = list(carry[:nch]), list(carry[nch:])
        first = ts - SCAN_UNROLL - ib * SCAN_UNROLL if reverse else ib * SCAN_UNROLL
        first = pl.multiple_of(first, SCAN_UNROLL)
        for k in range(SCAN_UNROLL):
            row = pl.ds(first + (SCAN_UNROLL - 1 - k if reverse else k), 1)
            for c, (b, re) in enumerate(chains):
                nr = wr[re.start] * vr[c] - wi[re.start] * vi[c] + buf_ref[b, row, re]
                ni = wr[re.start] * vi[c] + wi[re.start] * vr[c] + buf_ref[b, row, _im(re)]
                buf_ref[b, row, re] = nr
                buf_ref[b, row, _im(re)] = ni
                vr[c], vi[c] = nr, ni
        return tuple(vr + vi)

    init = tuple(st_ref[b, 0:1, re] for b, re in chains) + tuple(st_ref[b, 1:2, re] for b, re in chains)
    last = lax.fori_loop(0, ts // SCAN_UNROLL, block, init)
    for c, (b, re) in enumerate(chains):
        st_ref[b, 0:1, re] = last[c]
        st_ref[b, 1:2, re] = last[nch + c]


BAND_CH = S5_WIDTH // S5_BANDS
BAND_ST = S5_N // S5_BANDS


def _s5_fwd(za, b_bands, lam, c_bands, dskip, nb, seq, ts):
    nts = seq // ts

    def body(za_ref, br_ref, bi_ref, lam_ref, cr_ref, ci_ref, d_ref, xs_ref, y_ref, buf_ref, st_ref):
        @pl.when(pl.program_id(0) == 0)
        def _():
            st_ref[...] = jnp.zeros_like(st_ref)

        for b in range(nb):
            zav = za_ref[b]
            for q in range(S5_BANDS):
                ch, st = _band(q)
                buf_ref[b, :, st] = _dot(zav[:, ch], br_ref[q])
                buf_ref[b, :, _im(st)] = _dot(zav[:, ch], bi_ref[q])
        _complex_scan(buf_ref, lam_ref, st_ref, nb, ts, reverse=False)
        for b in range(nb):
            zav = za_ref[b]
            xs_ref[b] = buf_ref[b].astype(BF16)
            for q in range(S5_BANDS):
                ch, st = _band(q)
                y_ref[b, :, ch] = (_dot(xs_ref[b, :, st], cr_ref[q]) + _dot(xs_ref[b, :, _im(st)], ci_ref[q])
                                   + d_ref[:, ch] * zav[:, ch])

    tok = lambda w: pl.BlockSpec((nb, ts, w), lambda j: (0, j, 0))
    to_st, to_ch = _full((S5_BANDS, BAND_CH, BAND_ST)), _full((S5_BANDS, BAND_ST, BAND_CH))
    return _pcall(body, "s5_fwd", (nts,),
                  [tok(S5_WIDTH), to_st, to_st, _full((2, S5_N)), to_ch, to_ch, _full((1, S5_WIDTH))],
                  [tok(2 * S5_N), tok(S5_WIDTH)],
                  [_sds((nb, seq, 2 * S5_N), BF16), _sds((nb, seq, S5_WIDTH))],
                  scratch=[pltpu.VMEM((nb, ts, 2 * S5_N), F32), pltpu.VMEM((nb, 2, S5_N), F32)],
                  )(za, *b_bands, lam, *c_bands, dskip)


def _hgrn_gates(zq, zf, lbh):
    sf = _sigmoid(zf)
    f = lbh + (1.0 - lbh) * sf
    sq = _sigmoid(zq)
    qa = zq * sq * QSCALE
    bc = _cumsum_rows(jnp.log(f))
    bm = bc[CHUNK // 2 - 1:CHUNK // 2, :]
    bl = bc[CHUNK - 1:CHUNK, :]
    return sf, f, sq, qa, bc, bm, bl


def _hgrn_fwd(zh, lb, nb, seq):
    nc = seq // CHUNK

    def body(zh_ref, lb_ref, o_ref, sts_ref, st_ref):
        @pl.when(pl.program_id(0) == 0)
        def _():
            st_ref[...] = jnp.zeros_like(st_ref)

        causal = (lax.broadcasted_iota(jnp.int32, (CHUNK, CHUNK), 0)
                  >= lax.broadcasted_iota(jnp.int32, (CHUNK, CHUNK), 1))
        for b in range(nb):
            for h in range(HG_HEADS):
                hs = slice(h * HG_HEAD, (h + 1) * HG_HEAD)
                zq = zh_ref[b, :, h * HG_HEAD:(h + 1) * HG_HEAD]
                zf = zh_ref[b, :, HG_WIDTH + h * HG_HEAD:HG_WIDTH + (h + 1) * HG_HEAD]
                zi = zh_ref[b, :, 2 * HG_WIDTH + h * HG_HEAD:2 * HG_WIDTH + (h + 1) * HG_HEAD]
                _, f, _, qa, bc, bm, bl = _hgrn_gates(zq, zf, lb_ref[:, hs])
                k = 1.0 - f
                qt = qa * jnp.exp(bc - bm)
                kt = k * jnp.exp(bm - bc)
                qb = qa * jnp.exp(bc)
                kd = k * jnp.exp(bl - bc)
                st = st_ref[b, h]
                sts_ref[b, 0, h] = st
                a = jnp.where(causal, _dot_nt(qt, kt), 0.0)
                o_ref[b, :, hs] = _dot(a, zi) + _dot_nt(qb, st)
                st_ref[b, h] = st * jnp.exp(bl) + _dot_tn(zi, kd)

    return _pcall(body, "hgrn_fwd", (nc,),
                  [pl.BlockSpec((nb, CHUNK, 4 * HG_WIDTH), lambda c: (0, c, 0)), _full((1, HG_WIDTH))],
                  [pl.BlockSpec((nb, CHUNK, HG_WIDTH), lambda c: (0, c, 0)),
                   pl.BlockSpec((nb, 1, HG_HEADS, HG_HEAD, HG_HEAD), lambda c: (0, c, 0, 0, 0))],
                  [_sds((nb, seq, HG_WIDTH)), _sds((nb, nc, HG_HEADS, HG_HEAD, HG_HEAD))],
                  scratch=[pltpu.VMEM((nb, HG_HEADS, HG_HEAD, HG_HEAD), F32)])(zh, lb)


def _head_rms(o):
    parts = []
    for h in range(HG_HEADS):
        oh = o[:, h * HG_HEAD:(h + 1) * HG_HEAD]
        r = lax.rsqrt(jnp.mean(oh * oh, axis=-1, keepdims=True) + EPS)
        parts.append(jnp.broadcast_to(r, oh.shape))
    return jnp.concatenate(parts, axis=1)


def _head_mean(v):
    parts = []
    for h in range(HG_HEADS):
        vh = v[:, h * HG_HEAD:(h + 1) * HG_HEAD]
        parts.append(jnp.broadcast_to(jnp.mean(vh, axis=-1, keepdims=True), vh.shape))
    return jnp.concatenate(parts, axis=1)


def _mix_fwd(x, y0, o, zh, zgt, w_glu, b_glu, gain, w_pa, w_pb, w_out, g_ffn, tm):
    t = x.shape[0]

    def body(x_ref, y0_ref, o_ref, zg_ref, zgt_ref, wglu_ref, bglu_ref, gain_ref, wpa_ref, wpb_ref, wout_ref,
             gffn_ref, x1_ref, u2_ref, pa_ref, pb_ref, ya2_ref, yb_ref):
        ya1 = _gelu(y0_ref[...])
        s = _sigmoid(_dot(ya1, wglu_ref[...]) + bglu_ref[...])
        ya2 = (ya1 * s).astype(BF16)
        ov = o_ref[...]
        zg = zg_ref[...]
        yb = (ov * _head_rms(ov) * gain_ref[...] * (zg * _sigmoid(zg))).astype(BF16)
        ya2_ref[...] = ya2
        yb_ref[...] = yb
        pa = jnp.dot(ya2, wpa_ref[...], preferred_element_type=F32)
        pb = jnp.dot(yb, wpb_ref[...], preferred_element_type=F32)
        pa_ref[...] = pa.astype(BF16)
        pb_ref[...] = pb.astype(BF16)
        m = (_sigmoid(zgt_ref[:, 0:D_MODEL].astype(F32)) * pa
             + _sigmoid(zgt_ref[:, D_MODEL:].astype(F32)) * pb)
        x1 = x_ref[...] + _dot(m, wout_ref[...])
        x1_ref[...] = x1
        r = lax.rsqrt(jnp.mean(x1 * x1, axis=-1, keepdims=True) + EPS)
        u2_ref[...] = (x1 * r * gffn_ref[...]).astype(BF16)

    row = lambda w: pl.BlockSpec((tm, w), lambda i: (i, 0))
    return _pcall(body, "mix_fwd", (t // tm,),
                  [row(D_MODEL), row(S5_WIDTH), row(HG_WIDTH), pl.BlockSpec((tm, HG_WIDTH), lambda i: (i, 3)),
                   row(2 * D_MODEL), _full((S5_WIDTH, S5_WIDTH)), _full((1, S5_WIDTH)), _full((1, HG_WIDTH)),
                   _full((S5_WIDTH, D_MODEL)), _full((HG_WIDTH, D_MODEL)), _full((D_MODEL, D_MODEL)),
                   _full((1, D_MODEL))],
                  [row(D_MODEL), row(D_MODEL), row(D_MODEL), row(D_MODEL), row(S5_WIDTH), row(HG_WIDTH)],
                  [_sds((t, D_MODEL)), _sds((t, D_MODEL), BF16), _sds((t, D_MODEL), BF16), _sds((t, D_MODEL), BF16),
                   _sds((t, S5_WIDTH), BF16), _sds((t, HG_WIDTH), BF16)],
                  )(x, y0, o, zh, zgt, w_glu, b_glu, gain, w_pa, w_pb, w_out, g_ffn)


FF_COLS = 256
FF_UP_TILE = 2 * D_FF // 2


def _ffn_up(u2, w_up, tm):
    t = u2.shape[0]
    n = 2 * D_FF

    def body(u_ref, w_ref, h_ref):
        h_ref[...] = _dot_nt(u_ref[...], w_ref[...]).astype(BF16)

    return _pcall(body, "ffn_up", (n // FF_UP_TILE, t // tm),
                  [pl.BlockSpec((tm, D_MODEL), lambda j, i: (i, 0)),
                   pl.BlockSpec((FF_UP_TILE, D_MODEL), lambda j, i: (j, 0))],
                  pl.BlockSpec((tm, FF_UP_TILE), lambda j, i: (i, j)),
                  _sds((t, n), BF16))(u2, w_up)


HALO = 16


def _shift_matrix(tm):
    r = lax.broadcasted_iota(jnp.int32, (tm, tm), 0)
    c = lax.broadcasted_iota(jnp.int32, (tm, tm), 1)
    return jnp.where(r == c + 1, 1.0, 0.0).astype(BF16)


def _conv_cols(h_ref, halo_ref, valid, wc_ref, bc_ref, c0):
    cs = slice(c0, c0 + FF_COLS)
    cur = h_ref[:, cs].astype(F32)
    prev = jnp.where(valid, halo_ref[:, cs].astype(F32), 0.0)
    full = jnp.concatenate([prev, cur], axis=0)
    h1 = pltpu.roll(full, 1, axis=0)[HALO:]
    h2 = pltpu.roll(full, 2, axis=0)[HALO:]
    return h2 * wc_ref[0:1, cs] + h1 * wc_ref[1:2, cs] + cur * wc_ref[2:3, cs] + bc_ref[:, cs]


def _ffn_down_loss(h, x1, tgt, w_conv, b_conv, w_down, g_final, seq, tm):
    t = h.shape[0]
    tps = seq // tm
    n = 2 * D_FF

    def body(h_ref, halo_ref, x1_ref, tgt_ref, wc_ref, bc_ref, wd_ref, gf_ref,
             hc_ref, a_ref, dx2_ref, dx2b_ref, loss_ref, dgf_ref):
        i = pl.program_id(0)

        @pl.when(i == 0)
        def _():
            loss_ref[...] = jnp.zeros_like(loss_ref)
            dgf_ref[...] = jnp.zeros_like(dgf_ref)

        valid = (i % tps) != 0
        x2 = x1_ref[...]
        for j in range(D_FF // FF_COLS):
            gate = _conv_cols(h_ref, halo_ref, valid, wc_ref, bc_ref, j * FF_COLS)
            val = _conv_cols(h_ref, halo_ref, valid, wc_ref, bc_ref, D_FF + j * FF_COLS)
            hc_ref[:, j * FF_COLS:(j + 1) * FF_COLS] = gate.astype(BF16)
            hc_ref[:, D_FF + j * FF_COLS:D_FF + (j + 1) * FF_COLS] = val.astype(BF16)
            a = (gate * _sigmoid(gate) * val).astype(BF16)
            a_ref[:, j * FF_COLS:(j + 1) * FF_COLS] = a
            x2 = x2 + jnp.dot(a, wd_ref[j * FF_COLS:(j + 1) * FF_COLS, :], preferred_element_type=F32)
        r = lax.rsqrt(jnp.mean(x2 * x2, axis=-1, keepdims=True) + EPS)
        xn = x2 * r
        g = gf_ref[...]
        e = xn * g - tgt_ref[...]
        loss_ref[...] += (0.5 / D_MODEL) * jnp.sum(e * e).reshape(1, 1)
        dy = e * (1.0 / D_MODEL)
        dgf_ref[...] += jnp.sum(dy * xn, axis=0, keepdims=True)
        dxn = dy * g
        dx2 = r * (dxn - xn * jnp.mean(dxn * xn, axis=-1, keepdims=True))
        dx2_ref[...] = dx2
        dx2b_ref[...] = dx2.astype(BF16)

    row = lambda w: pl.BlockSpec((tm, w), lambda i: (i, 0))
    halo = pl.BlockSpec((HALO, n), lambda i: (jnp.maximum(i * (tm // HALO) - 1, 0), 0))
    return _pcall(body, "ffn_down_loss", (t // tm,),
                  [row(n), halo, row(D_MODEL), row(D_MODEL), _full((CONV_W, n)), _full((1, n)),
                   _full((D_FF, D_MODEL)), _full((1, D_MODEL))],
                  [row(n), row(D_FF), row(D_MODEL), row(D_MODEL), _full((1, 1)), _full((1, D_MODEL))],
                  [_sds((t, n), BF16), _sds((t, D_FF), BF16), _sds((t, D_MODEL)), _sds((t, D_MODEL), BF16),
                   _sds((1, 1)), _sds((1, D_MODEL))],
                  )(h, h, x1, tgt, w_conv, b_conv, w_down, g_final)


def _wgrad(a, b, name, tn, out_dtype=F32, band=None, after=None):
    t, m = a.shape
    n = b.shape[1] if band is None else band
    nbands = 1 if band is None else b.shape[1] // band
    after = b if after is None else after

    def body(a_ref, b_ref, after_ref, o_ref):
        o_ref[...] = _dot_tn(a_ref[...], b_ref[...]).astype(out_dtype)

    return _pcall(body, name, (m // tn,),
                  [pl.BlockSpec((t, tn), lambda i: (0, i)), pl.BlockSpec((t, n), lambda i: (0, i % nbands)),
                   pl.BlockSpec(memory_space=pl.ANY)],
                  pl.BlockSpec((tn, n), lambda i: (i, 0)), _sds((m, n), out_dtype))(a, b, after)


def _ffn_bwd_act(dx2b, hc, w_down, tm):
    t = hc.shape[0]
    n = 2 * D_FF

    def body(dx2_ref, hc_ref, wd_ref, dhc_ref, dbc_ref):
        @pl.when(pl.program_id(0) == 0)
        def _():
            dbc_ref[...] = jnp.zeros_like(dbc_ref)

        dx2 = dx2_ref[...]
        for j in range(D_FF // FF_COLS):
            gs = slice(j * FF_COLS, (j + 1) * FF_COLS)
            vs = slice(D_FF + j * FF_COLS, D_FF + (j + 1) * FF_COLS)
            gate = hc_ref[:, gs].astype(F32)
            val = hc_ref[:, vs].astype(F32)
            da = _dot_nt(dx2, wd_ref[gs, :])
            sg = _sigmoid(gate)
            dgate = da * val * (sg * (1.0 + gate * (1.0 - sg)))
            dval = da * (gate * sg)
            dhc_ref[:, gs] = dgate.astype(BF16)
            dhc_ref[:, vs] = dval.astype(BF16)
            dbc_ref[:, gs] += jnp.sum(dgate, axis=0, keepdims=True)
            dbc_ref[:, vs] += jnp.sum(dval, axis=0, keepdims=True)

    row = lambda w: pl.BlockSpec((tm, w), lambda i: (i, 0))
    return _pcall(body, "ffn_bwd_act", (t // tm,),
                  [row(D_MODEL), row(n), _full((D_FF, D_MODEL))],
                  [row(n), _full((1, n))],
                  [_sds((t, n), BF16), _sds((1, n))],
                  )(dx2b, hc, w_down)


def _ffn_bwd_up(dhc, h, dx2, x1, w_conv, w_up, g_ffn, seq, tm):
    t = dhc.shape[0]
    tps = seq // tm
    n = 2 * D_FF
    last = t // HALO - 1

    def body(dhc_ref, halo_ref, h_ref, dx2_ref, x1_ref, wc_ref, wu_ref, gf_ref,
             dh_ref, dx1_ref, dx1b_ref, dgf_ref, dwc_ref):
        i = pl.program_id(0)

        @pl.when(i == 0)
        def _():
            dgf_ref[...] = jnp.zeros_like(dgf_ref)
            dwc_ref[...] = jnp.zeros_like(dwc_ref)

        valid = ((i + 1) % tps) != 0
        du2 = jnp.zeros((tm, D_MODEL), F32)
        for j in range(n // FF_COLS):
            cs = slice(j * FF_COLS, (j + 1) * FF_COLS)
            cur = dhc_ref[:, cs].astype(F32)
            nxt = jnp.where(valid, halo_ref[:, cs].astype(F32), 0.0)
            full = jnp.concatenate([cur, nxt], axis=0)
            d1 = pltpu.roll(full, tm + HALO - 1, axis=0)[:tm]
            d2 = pltpu.roll(full, tm + HALO - 2, axis=0)[:tm]
            dh = (cur * wc_ref[2:3, cs] + d1 * wc_ref[1:2, cs] + d2 * wc_ref[0:1, cs]).astype(BF16)
            dh_ref[:, cs] = dh
            du2 = du2 + _dot(dh, wu_ref[cs, :])
            hv = h_ref[:, cs].astype(F32)
            dwc_ref[0:1, cs] += jnp.sum(hv * d2, axis=0, keepdims=True)
            dwc_ref[1:2, cs] += jnp.sum(hv * d1, axis=0, keepdims=True)
            dwc_ref[2:3, cs] += jnp.sum(hv * cur, axis=0, keepdims=True)
        x1 = x1_ref[...]
        r = lax.rsqrt(jnp.mean(x1 * x1, axis=-1, keepdims=True) + EPS)
        xn = x1 * r
        dgf_ref[...] += jnp.sum(du2 * xn, axis=0, keepdims=True)
        dxn = du2 * gf_ref[...]
        dx1 = dx2_ref[...] + r * (dxn - xn * jnp.mean(dxn * xn, axis=-1, keepdims=True))
        dx1_ref[...] = dx1
        dx1b_ref[...] = dx1.astype(BF16)

    row = lambda w: pl.BlockSpec((tm, w), lambda i: (i, 0))
    halo = pl.BlockSpec((HALO, n), lambda i: (jnp.minimum((i + 1) * (tm // HALO), last), 0))
    return _pcall(body, "ffn_bwd_up", (t // tm,),
                  [row(n), halo, row(n), row(D_MODEL), row(D_MODEL), _full((CONV_W, n)), _full((n, D_MODEL)),
                   _full((1, D_MODEL))],
                  [row(n), row(D_MODEL), row(D_MODEL), _full((1, D_MODEL)), _full((CONV_W, n))],
                  [_sds((t, n), BF16), _sds((t, D_MODEL)), _sds((t, D_MODEL), BF16), _sds((1, D_MODEL)),
                   _sds((CONV_W, n))],
                  )(dhc, dhc, h, dx2, x1, w_conv, w_up, g_ffn)


def _mix_bwd(dx1, y0, o, zh, zgt, pa, pb, w_glu, b_glu, gain, w_pa, w_pb, w_out, tm):
    t = dx1.shape[0]

    def body(dx1_ref, y0_ref, o_ref, zg_ref, zgt_ref, pa_ref, pb_ref, wglu_ref, bglu_ref, gain_ref, wpa_ref,
             wpb_ref, wout_ref,
             dy0_ref, do_ref, dzg_ref, dzgt_ref, m_ref, dpa_ref, dpb_ref, ya1_ref, dpre_ref, dbglu_ref, dgain_ref):
        @pl.when(pl.program_id(0) == 0)
        def _():
            dbglu_ref[...] = jnp.zeros_like(dbglu_ref)
            dgain_ref[...] = jnp.zeros_like(dgain_ref)

        dm = _dot_nt(dx1_ref[...], wout_ref[...])
        sga = _sigmoid(zgt_ref[:, 0:D_MODEL].astype(F32))
        sgb = _sigmoid(zgt_ref[:, D_MODEL:].astype(F32))
        pa = pa_ref[...].astype(F32)
        pb = pb_ref[...].astype(F32)
        m_ref[...] = (sga * pa + sgb * pb).astype(BF16)
        dzgt_ref[:, 0:D_MODEL] = (dm * pa * sga * (1.0 - sga)).astype(BF16)
        dzgt_ref[:, D_MODEL:] = (dm * pb * sgb * (1.0 - sgb)).astype(BF16)
        dpa = (dm * sga).astype(BF16)
        dpb = (dm * sgb).astype(BF16)
        dpa_ref[...] = dpa
        dpb_ref[...] = dpb
        dya2 = _dot_nt(dpa, wpa_ref[...])
        dyb = _dot_nt(dpb, wpb_ref[...])
        y0 = y0_ref[...]
        ya1 = _gelu(y0)
        ya1_ref[...] = ya1.astype(BF16)
        s = _sigmoid(_dot(ya1, wglu_ref[...]) + bglu_ref[...])
        dpre = dya2 * ya1 * s * (1.0 - s)
        dpre_ref[...] = dpre.astype(BF16)
        dbglu_ref[...] += jnp.sum(dpre, axis=0, keepdims=True)
        dya1 = dya2 * s + _dot_nt(dpre, wglu_ref[...])
        dy0_ref[...] = dya1 * _gelu_grad(y0)
        ov = o_ref[...]
        zg = zg_ref[...]
        oh = ov * _head_rms(ov)
        on = oh * gain_ref[...]
        sz = _sigmoid(zg)
        dzg_ref[...] = (dyb * on * (sz * (1.0 + zg * (1.0 - sz)))).astype(BF16)
        don = dyb * (zg * sz)
        dgain_ref[...] += jnp.sum(don * oh, axis=0, keepdims=True)
        doh = don * gain_ref[...]
        do_ref[...] = _head_rms(ov) * (doh - oh * _head_mean(doh * oh))

    row = lambda w: pl.BlockSpec((tm, w), lambda i: (i, 0))
    return _pcall(body, "mix_bwd", (t // tm,),
                  [row(D_MODEL), row(S5_WIDTH), row(HG_WIDTH), pl.BlockSpec((tm, HG_WIDTH), lambda i: (i, 3)),
                   row(2 * D_MODEL), row(D_MODEL), row(D_MODEL), _full((S5_WIDTH, S5_WIDTH)), _full((1, S5_WIDTH)),
                   _full((1, HG_WIDTH)), _full((S5_WIDTH, D_MODEL)), _full((HG_WIDTH, D_MODEL)),
                   _full((D_MODEL, D_MODEL))],
                  [row(S5_WIDTH), row(HG_WIDTH), row(HG_WIDTH), row(2 * D_MODEL), row(D_MODEL), row(D_MODEL),
                   row(D_MODEL), row(S5_WIDTH), row(S5_WIDTH), _full((1, S5_WIDTH)), _full((1, HG_WIDTH))],
                  [_sds((t, S5_WIDTH)), _sds((t, HG_WIDTH)), _sds((t, HG_WIDTH), BF16), _sds((t, 2 * D_MODEL), BF16),
                   _sds((t, D_MODEL), BF16), _sds((t, D_MODEL), BF16), _sds((t, D_MODEL), BF16),
                   _sds((t, S5_WIDTH), BF16), _sds((t, S5_WIDTH), BF16), _sds((1, S5_WIDTH)), _sds((1, HG_WIDTH))],
                  )(dx1, y0, o, zh, zgt, pa, pb, w_glu, b_glu, gain, w_pa, w_pb, w_out)


def _s5_bwd(dy0, za, xs, c_bands, b_bands, lam, dskip, nb, seq, ts):
    nts = seq // ts

    def body(dy0_ref, za_ref, xs_ref, halo_ref, cr_ref, ci_ref, br_ref, bi_ref, lam_ref, d_ref,
             dza_ref, a_ref, dlam_ref, dd_ref, acc_ref, st_ref):
        j = pl.program_id(0)

        @pl.when(j == 0)
        def _():
            dlam_ref[...] = jnp.zeros_like(dlam_ref)
            dd_ref[...] = jnp.zeros_like(dd_ref)
            st_ref[...] = jnp.zeros_like(st_ref)

        for b in range(nb):
            dy0 = dy0_ref[b]
            for q in range(S5_BANDS):
                ch, st = _band(q)
                acc_ref[b, :, st] = _dot(dy0[:, ch], cr_ref[q])
                acc_ref[b, :, _im(st)] = _dot(dy0[:, ch], ci_ref[q])
        _complex_scan(acc_ref, lam_ref, st_ref, nb, ts, reverse=True)
        shift = _shift_matrix(ts)
        top = lax.broadcasted_iota(jnp.int32, (SUBLANES, S5_LANES), 0) == 0
        for b in range(nb):
            a_ref[b] = acc_ref[b].astype(BF16)
            first = jnp.where(j == nts - 1, 0.0, halo_ref[b, HALO - 1:HALO, :].astype(F32))

            def shifted(cols):
                xp = jnp.dot(shift, xs_ref[b, :, cols], preferred_element_type=F32)
                return jnp.concatenate([xp[:SUBLANES] + jnp.where(top, first[:, cols], 0.0), xp[SUBLANES:]], axis=0)

            for cc in range(S5_N // S5_LANES):
                re = slice(cc * S5_LANES, (cc + 1) * S5_LANES)
                ar, ai, xr, xi = acc_ref[b, :, re], acc_ref[b, :, _im(re)], shifted(re), shifted(_im(re))
                dlam_ref[0:1, re] += jnp.sum(ar * xr + ai * xi, axis=0, keepdims=True)
                dlam_ref[1:2, re] += jnp.sum(ai * xr - ar * xi, axis=0, keepdims=True)
            dy0 = dy0_ref[b]
            for q in range(S5_BANDS):
                ch, st = _band(q)
                dza_ref[b, :, ch] = (_dot(a_ref[b, :, st], br_ref[q]) + _dot(a_ref[b, :, _im(st)], bi_ref[q])
                                     + d_ref[:, ch] * dy0[:, ch]).astype(BF16)
            dd_ref[...] += jnp.sum(dy0 * za_ref[b], axis=0, keepdims=True)

    tile = lambda j: nts - 1 - j
    tok = lambda w: pl.BlockSpec((nb, ts, w), lambda j: (0, tile(j), 0))
    halo = pl.BlockSpec((nb, HALO, 2 * S5_N), lambda j: (0, jnp.maximum(tile(j) * (ts // HALO) - 1, 0), 0))
    to_st, to_ch = _full((S5_BANDS, BAND_CH, BAND_ST)), _full((S5_BANDS, BAND_ST, BAND_CH))
    return _pcall(body, "s5_bwd", (nts,),
                  [tok(S5_WIDTH), tok(S5_WIDTH), tok(2 * S5_N), halo, to_st, to_st, to_ch, to_ch,
                   _full((2, S5_N)), _full((1, S5_WIDTH))],
                  [tok(S5_WIDTH), tok(2 * S5_N), _full((2, S5_N)), _full((1, S5_WIDTH))],
                  [_sds((nb, seq, S5_WIDTH), BF16), _sds((nb, seq, 2 * S5_N), BF16), _sds((2, S5_N)),
                   _sds((1, S5_WIDTH))],
                  scratch=[pltpu.VMEM((nb, ts, 2 * S5_N), F32), pltpu.VMEM((nb, 2, S5_N), F32)],
                  )(dy0, za, xs, xs, *c_bands, *b_bands, lam, dskip)


def _hgrn_bwd(zh, do, sts, lb, nb, seq):
    nc = seq // CHUNK

    def body(zh_ref, do_ref, sts_ref, lb_ref, dz_ref, dlb_ref, dst_ref):
        @pl.when(pl.program_id(0) == 0)
        def _():
            dst_ref[...] = jnp.zeros_like(dst_ref)
            dlb_ref[...] = jnp.zeros_like(dlb_ref)

        row = lax.broadcasted_iota(jnp.int32, (CHUNK, CHUNK), 0)
        causal = row >= lax.broadcasted_iota(jnp.int32, (CHUNK, CHUNK), 1)
        last_row = lax.broadcasted_iota(jnp.int32, (CHUNK, HG_HEAD), 0) == CHUNK - 1
        for b in range(nb):
            for h in range(HG_HEADS):
                hs = slice(h * HG_HEAD, (h + 1) * HG_HEAD)
                zq = zh_ref[b, :, h * HG_HEAD:(h + 1) * HG_HEAD]
                zf = zh_ref[b, :, HG_WIDTH + h * HG_HEAD:HG_WIDTH + (h + 1) * HG_HEAD]
                zi = zh_ref[b, :, 2 * HG_WIDTH + h * HG_HEAD:2 * HG_WIDTH + (h + 1) * HG_HEAD]
                lbh = lb_ref[:, hs]
                sf, f, sq, qa, bc, bm, bl = _hgrn_gates(zq, zf, lbh)
                k = 1.0 - f
                e_qt = jnp.exp(bc - bm)
                e_kt = jnp.exp(bm - bc)
                e_b = jnp.exp(bc)
                e_kd = jnp.exp(bl - bc)
                e_l = jnp.exp(bl)
                qt, kt, qb, kd = qa * e_qt, k * e_kt, qa * e_b, k * e_kd
                a = jnp.where(causal, _dot_nt(qt, kt), 0.0)
                st = sts_ref[b, 0, h]
                dst = dst_ref[b, h]
                dov = do_ref[b, :, hs]
                da = jnp.where(causal, _dot_nt(dov, zi), 0.0)
                dqt = _hdot(da, kt)
                dkt = _hdot_tn(da, qt)
                dqb = _dot(dov, st)
                di = _dot_tn(a, dov) + _dot_nt(kd, dst)
                dkd = _dot(zi, dst)
                de_l = jnp.sum(dst * st, axis=0, keepdims=True)
                dst_ref[b, h] = dst * e_l + _dot_tn(dov, qb)
                dqa = dqt * e_qt + dqb * e_b
                dk = dkt * e_kt + dkd * e_kd
                dbl = jnp.sum(dkd * kd, axis=0, keepdims=True) + de_l * e_l
                db = dqt * qt - dkt * kt + dqb * qb - dkd * kd + jnp.where(last_row, dbl, 0.0)
                df = _cumsum_rows(db, reverse=True) / f - dk
                dzq = dqa * QSCALE * (sq * (1.0 + zq * (1.0 - sq)))
                dzf = df * (1.0 - lbh) * sf * (1.0 - sf)
                dz_ref[b, :, h * HG_HEAD:(h + 1) * HG_HEAD] = dzq.astype(BF16)
                dz_ref[b, :, HG_WIDTH + h * HG_HEAD:HG_WIDTH + (h + 1) * HG_HEAD] = dzf.astype(BF16)
                dz_ref[b, :, 2 * HG_WIDTH + h * HG_HEAD:2 * HG_WIDTH + (h + 1) * HG_HEAD] = di.astype(BF16)
                dlb_ref[:, hs] += jnp.sum(df * (1.0 - sf), axis=0, keepdims=True)

    rev = lambda c: nc - 1 - c
    return _pcall(body, "hgrn_bwd", (nc,),
                  [pl.BlockSpec((nb, CHUNK, 4 * HG_WIDTH), lambda c: (0, rev(c), 0)),
                   pl.BlockSpec((nb, CHUNK, HG_WIDTH), lambda c: (0, rev(c), 0)),
                   pl.BlockSpec((nb, 1, HG_HEADS, HG_HEAD, HG_HEAD), lambda c: (0, rev(c), 0, 0, 0)),
                   _full((1, HG_WIDTH))],
                  [pl.BlockSpec((nb, CHUNK, 3 * HG_WIDTH), lambda c: (0, rev(c), 0)), _full((1, HG_WIDTH))],
                  [_sds((nb, seq, 3 * HG_WIDTH), BF16), _sds((1, HG_WIDTH))],
                  scratch=[pltpu.VMEM((nb, HG_HEADS, HG_HEAD, HG_HEAD), F32)])(zh, do, sts, lb)


def _in_proj_bwd(dza, dzh, dzg, dzgt, dx1, x, g_mix, w_in, tm):
    t = x.shape[0]

    def body(dza_ref, dzh_ref, dzg_ref, dzgt_ref, dx1_ref, x_ref, g_ref, w_ref, dz_ref, dx_ref, dg_ref):
        @pl.when(pl.program_id(0) == 0)
        def _():
            dg_ref[...] = jnp.zeros_like(dg_ref)

        c1, c2, c3 = S5_WIDTH, S5_WIDTH + 3 * HG_WIDTH, S5_WIDTH + 4 * HG_WIDTH
        dz_ref[:, 0:c1] = dza_ref[...]
        dz_ref[:, c1:c2] = dzh_ref[...]
        dz_ref[:, c2:c3] = dzg_ref[...]
        dz_ref[:, c3:] = dzgt_ref[...]
        du = _dot(dz_ref[...], w_ref[...])
        xv = x_ref[...]
        r = lax.rsqrt(jnp.mean(xv * xv, axis=-1, keepdims=True) + EPS)
        xn = xv * r
        dg_ref[...] += jnp.sum(du * xn, axis=0, keepdims=True)
        dxn = du * g_ref[...]
        dx_ref[...] = dx1_ref[...] + r * (dxn - xn * jnp.mean(dxn * xn, axis=-1, keepdims=True))

    row = lambda w: pl.BlockSpec((tm, w), lambda i: (i, 0))
    return _pcall(body, "in_proj_bwd", (t // tm,),
                  [row(S5_WIDTH), row(3 * HG_WIDTH), row(HG_WIDTH), row(2 * D_MODEL), row(D_MODEL), row(D_MODEL),
                   _full((1, D_MODEL)), _full((N_IN, D_MODEL))],
                  [row(N_IN), row(D_MODEL), _full((1, D_MODEL))],
                  [_sds((t, N_IN), BF16), _sds((t, D_MODEL)), _sds((1, D_MODEL))],
                  )(dza, dzh, dzg, dzgt, dx1, x, g_mix, w_in)


def _after(value, token):
    return value + token[0, 0]


def _local_step(x3, tgt3, weights, sp, emit, emit_small):
    nb, seq, _ = x3.shape
    t = nb * seq
    tm = _token_tile(seq)
    x = x3.reshape(t, D_MODEL)
    tgt = tgt3.reshape(t, D_MODEL)
    row = lambda v: v.reshape(1, -1)

    a_re, a_im, b_re, b_im = sp["s5_a_re"], sp["s5_a_im"], sp["s5_b_re"], sp["s5_b_im"]
    ldt = sp["s5_log_dt"].reshape(S5_GROUPS, 1)
    lr, li, bb_re, bb_im, lb = _params_fwd(a_re, a_im, ldt, b_re, b_im, sp["hg_lb_logits"])
    lam = jnp.concatenate([lr.reshape(1, S5_N), li.reshape(1, S5_N)], axis=0)
    swap = lambda m: m.transpose(0, 2, 1)
    b_to_st = (_band_blocks(bb_re), _band_blocks(bb_im))
    b_to_ch = (_band_blocks(swap(bb_re)), _band_blocks(swap(bb_im)))
    c_to_ch = (_band_blocks(swap(sp["s5_c_re"])), _band_blocks(swap(-sp["s5_c_im"])))
    c_to_st = (_band_blocks(sp["s5_c_re"]), _band_blocks(-sp["s5_c_im"]))

    g_mix, g_ffn, g_final = row(sp["g_mix"]), row(sp["g_ffn"]), row(sp["g_final"])
    b_glu, gain, dskip, b_conv = row(sp["b_glu"]), row(sp["hg_norm_gain"]), row(sp["s5_d"]), row(sp["b_conv"])

    w_in = weights("in", lam, *b_to_st, *b_to_ch, *c_to_ch, *c_to_st)["w_in"]
    wide = min(2 * tm, seq)
    u, za, zh, zgt = _in_proj(x, g_mix, w_in, wide)
    seqs = lambda v: v.reshape(nb, seq, v.shape[-1])
    toks = lambda v: v.reshape(t, v.shape[-1])
    xs3, y0 = _s5_fwd(seqs(za), b_to_st, lam, c_to_ch, dskip, nb, seq, tm)
    xs, y0 = toks(xs3), toks(y0)
    o3, sts = _hgrn_fwd(zh.reshape(nb, seq, 4 * HG_WIDTH), lb, nb, seq)
    o = o3.reshape(t, HG_WIDTH)
    wm = weights("mix", y0, o3)
    weights.forward("ffn", wm["w_out"])
    x1, u2, pa, pb, ya2, yb = _mix_fwd(x, y0, o, zh, zgt, wm["w_glu"], b_glu, gain, wm["w_pa"], wm["w_pb"],
                                       wm["w_out"], g_ffn, wide)
    wf = weights("ffn", u2)
    h = _ffn_up(u2, wf["w_up"], min(4 * tm, t))
    hc, a, dx2, dx2b, loss, dg_final = _ffn_down_loss(h, x1, tgt, wf["w_conv"], b_conv, wf["w_down"], g_final,
                                                      seq, tm)

    def wgrad(a, b, name):
        return _wgrad(a, b, name, 512 if a.shape[1] % 512 == 0 else 256, out_dtype=BF16)

    dhc, db_conv = _ffn_bwd_act(dx2b, hc, wf["w_down"], tm)
    dw_down = wgrad(a, dx2b, "dw_down")
    dh, dx1, dx1b, dg_ffn, dw_conv = _ffn_bwd_up(dhc, h, dx2, x1, wf["w_conv"], wf["w_up"], g_ffn, seq, tm)
    sent = emit({"w_up": wgrad(dh, u2, "dw_up"), "w_conv": dw_conv, "w_down": dw_down})
    (dy0, do, dzg, dzgt, m, dpa, dpb, ya1, dpre, db_glu, dgain) = _mix_bwd(
        dx1b, y0, o, zh, zgt, pa, pb, wm["w_glu"], _after(b_glu, sent), gain, wm["w_pa"], wm["w_pb"], wm["w_out"],
        wide)
    sent = emit({"w_out": wgrad(m, dx1b, "dw_out"), "w_pa": wgrad(ya2, dpa, "dw_pa"),
                 "w_pb": wgrad(yb, dpb, "dw_pb"), "w_glu": wgrad(ya1, dpre, "dw_glu")})
    dzh3, dlb = _hgrn_bwd(zh.reshape(nb, seq, 4 * HG_WIDTH), do.reshape(nb, seq, HG_WIDTH), sts, _after(lb, sent),
                          nb, seq)
    dza, a_s5, dlam, dd = _s5_bwd(seqs(dy0), seqs(za), xs3, c_to_st, b_to_ch, lam, dskip, nb, seq, tm)
    dza, a_s5 = toks(dza), toks(a_s5)
    dz, dx, dg_mix = _in_proj_bwd(dza, dzh3.reshape(t, 3 * HG_WIDTH), dzg, dzgt, dx1, x, g_mix, w_in, wide)
    sent = emit({"w_in": wgrad(dz, u, "dw_in")})

    band = HG_HEAD
    dbb_band = _wgrad(a_s5, za, "dbb_s5", 512, band=band, after=sent)
    dc_band = _wgrad(xs, dy0, "dc_s5", 512, band=band, after=sent)
    dbb_re = swap(_diag_blocks(dbb_band[:S5_N], S5_STATE, S5_GROUP))
    dbb_im = swap(_diag_blocks(dbb_band[S5_N:], S5_STATE, S5_GROUP))
    dc_re = swap(_diag_blocks(dc_band[:S5_N], S5_STATE, S5_GROUP))
    dc_im = -swap(_diag_blocks(dc_band[S5_N:], S5_STATE, S5_GROUP))
    da_re, da_im, dldt, db_re, db_im, dlogits = _params_bwd(
        a_re, a_im, ldt, b_re, b_im, sp["hg_lb_logits"],
        dlam[0].reshape(S5_GROUPS, S5_STATE), dlam[1].reshape(S5_GROUPS, S5_STATE), dbb_re, dbb_im, dlb)
    emit_small({"g_mix": dg_mix, "s5_a_re": da_re, "s5_a_im": da_im, "s5_log_dt": dldt.reshape(1, S5_GROUPS),
                "s5_b_re": db_re, "s5_b_im": db_im, "s5_c_re": dc_re, "s5_c_im": dc_im, "s5_d": dd, "b_glu": db_glu,
                "hg_lb_logits": dlogits, "hg_norm_gain": dgain, "g_ffn": dg_ffn, "b_conv": db_conv,
                "g_final": dg_final, "loss": loss})
    return dx.reshape(nb, seq, D_MODEL)


def _mesh_peers():
    x, y, c = lax.axis_index("x"), lax.axis_index("y"), lax.axis_index("c")
    peers = []
    for k in range(1, N_DEV):
        px, py, pc = (1 - x if k & 4 else x), (1 - y if k & 2 else y), (1 - c if k & 1 else c)
        peers.append((k, (px, py, pc), 4 * px + 2 * py + pc))
    return 4 * x + 2 * y + c, peers


_HBM = pl.BlockSpec(memory_space=pltpu.HBM)
_SEM = pl.BlockSpec(memory_space=pltpu.SEMAPHORE)


_EFFECT = pltpu.CompilerParams(has_side_effects=pltpu.SideEffectType.DATAFLOW_SIDE_EFFECTING)


def _remote(src, dst, send_sem, recv_sem, to):
    return pltpu.make_async_remote_copy(src_ref=src, dst_ref=dst, send_sem=send_sem, recv_sem=recv_sem,
                                        device_id=to, device_id_type=pl.DeviceIdType.MESH)


def _exchange_start(name, arrays, after):
    n = len(arrays)
    srcs = [pltpu.with_memory_space_constraint(a, pltpu.HBM) for a in arrays]
    lands = [pltpu.with_memory_space_constraint(lax.empty(a.shape, a.dtype), pltpu.HBM) for a in arrays]
    copies = (N_DEV - 1) * n

    def body(*refs):
        src_refs, land_refs = refs[:n], refs[n:2 * n]
        send_sems, recv_sems, token = refs[2 * n + 1], refs[2 * n + 2], refs[-1]
        my_slab, peers = _mesh_peers()
        for k, peer, slab in peers:
            for i in range(n):
                s = (k - 1) * n + i
                _remote(src_refs[i].at[slab], land_refs[i].at[my_slab], send_sems.at[s], recv_sems.at[s], peer).start()
        token[...] = jnp.zeros_like(token)

    outs = pl.pallas_call(
        body, name=name,
        out_shape=(pltpu.SemaphoreType.DMA((copies,)), pltpu.SemaphoreType.DMA((copies,)),
                   *[pltpu.HBM(a.shape, a.dtype) for a in lands], _sds((SUBLANES, LANES))),
        in_specs=[_HBM] * (2 * n) + [pl.BlockSpec(memory_space=pl.ANY)],
        out_specs=(_SEM, _SEM, *[_HBM] * n, pl.BlockSpec(memory_space=pltpu.VMEM)),
        input_output_aliases={n + i: 2 + i for i in range(n)}, compiler_params=_EFFECT,
    )(*srcs, *lands, after)
    return (outs[0], outs[1], srcs, outs[2:2 + n]), outs[-1]


def _exchange_wait(name, state, *after):
    send_sems, recv_sems, srcs, lands = state
    n = len(lands)

    def body(*refs):
        src_refs, land_refs = refs[:n], refs[n:2 * n]
        send_ref, recv_ref = refs[2 * n], refs[2 * n + 1]
        _, peers = _mesh_peers()
        for k, peer, slab in peers:
            for i in range(n):
                s = (k - 1) * n + i
                copy = _remote(src_refs[i].at[slab], land_refs[i].at[slab], send_ref.at[s], recv_ref.at[s], peer)
                copy.wait_send()
                copy.wait_recv()

    outs = pl.pallas_call(
        body, name=name,
        out_shape=tuple(pltpu.HBM(a.shape, a.dtype) for a in lands),
        in_specs=[_HBM] * (2 * n) + [_SEM, _SEM] + [pl.BlockSpec(memory_space=pl.ANY)] * len(after),
        out_specs=tuple([_HBM] * n),
        input_output_aliases={n + i: i for i in range(n)}, compiler_params=_EFFECT,
    )(*srcs, *lands, send_sems, recv_sems, *after)
    return list(outs), list(srcs)


def _slab(pos):
    return 4 * pos[0] + 2 * pos[1] + pos[2]


def _chip_routes():
    x, y, c = lax.axis_index("x"), lax.axis_index("y"), lax.axis_index("c")
    return (x, y, c), (x, y, 1 - c), [(1 - x, y, c), (x, 1 - y, c), (1 - x, 1 - y, c)]


def _gather_start(name, arrays, after):
    n = len(arrays)
    me = 4 * lax.axis_index("x") + 2 * lax.axis_index("y") + lax.axis_index("c")
    srcs = [pltpu.with_memory_space_constraint(a, pltpu.HBM) for a in arrays]
    lands = [pltpu.with_memory_space_constraint(
        lax.dynamic_update_slice_in_dim(lax.empty((N_DEV,) + a.shape, a.dtype), a[None], me, 0), pltpu.HBM)
        for a in arrays]

    def body(*refs):
        src_refs, land_refs = refs[:n], refs[n:2 * n]
        send_sems, recv_sems, token = refs[2 * n + 1], refs[2 * n + 2], refs[-1]
        mine, sibling, chips = _chip_routes()
        for k, to in enumerate([sibling] + chips):
            for i in range(n):
                _remote(src_refs[i], land_refs[i].at[_slab(mine)], send_sems.at[k * n + i], recv_sems.at[k * n + i],
                        to).start()
        token[...] = jnp.zeros_like(token)

    outs = pl.pallas_call(
        body, name=name,
        out_shape=(pltpu.SemaphoreType.DMA((4 * n,)), pltpu.SemaphoreType.DMA((4 * n,)),
                   *[pltpu.HBM(a.shape, a.dtype) for a in lands], _sds((SUBLANES, LANES))),
        in_specs=[_HBM] * (2 * n) + [pl.BlockSpec(memory_space=pl.ANY)],
        out_specs=(_SEM, _SEM, *[_HBM] * n, pl.BlockSpec(memory_space=pltpu.VMEM)),
        input_output_aliases={n + i: 2 + i for i in range(n)}, compiler_params=_EFFECT,
    )(*srcs, *lands, after)
    return (outs[0], outs[1], srcs, outs[2:2 + n]), outs[-1]


def _gather_forward(name, state, *after):
    send_a, recv_a, srcs, lands = state
    n = len(lands)

    def body(*refs):
        land_refs, recv_a_ref = refs[:n], refs[n]
        send_b, recv_b = refs[n + 1 + len(after)], refs[n + 2 + len(after)]
        mine, sibling, chips = _chip_routes()
        for j, chip in enumerate(chips):
            for i in range(n):
                block = land_refs[i].at[_slab(chip)]
                _remote(block, block, send_b.at[j * n + i], recv_a_ref.at[(1 + j) * n + i], chip).wait_recv()
                _remote(block, block, send_b.at[j * n + i], recv_b.at[j * n + i], sibling).start()

    outs = pl.pallas_call(
        body, name=name,
        out_shape=(pltpu.SemaphoreType.DMA((3 * n,)), pltpu.SemaphoreType.DMA((3 * n,)),
                   *[pltpu.HBM(a.shape, a.dtype) for a in lands]),
        in_specs=[_HBM] * n + [_SEM] + [pl.BlockSpec(memory_space=pl.ANY)] * len(after),
        out_specs=(_SEM, _SEM, *[_HBM] * n),
        input_output_aliases={i: 2 + i for i in range(n)}, compiler_params=_EFFECT,
    )(*lands, recv_a, *after)
    return (send_a, recv_a, srcs, list(outs[2:])), (outs[0], outs[1])


def _gather_wait(name, state, forwarded, *after):
    send_a, recv_a, srcs, lands = state
    send_b, recv_b = forwarded
    n = len(lands)

    def body(*refs):
        src_refs, land_refs = refs[:n], refs[n:2 * n]
        sa, ra, sb, rb = refs[2 * n:2 * n + 4]
        mine, sibling, chips = _chip_routes()
        for i in range(n):
            for k, to in enumerate([sibling] + chips):
                _remote(src_refs[i], land_refs[i].at[_slab(mine)], sa.at[k * n + i], ra.at[k * n + i], to).wait_send()
            theirs = land_refs[i].at[_slab(sibling)]
            _remote(theirs, theirs, sa.at[i], ra.at[i], sibling).wait_recv()
            for j, chip in enumerate(chips):
                sent = land_refs[i].at[_slab(chip)]
                got = land_refs[i].at[_slab((chip[0], chip[1], sibling[2]))]
                _remote(sent, sent, sb.at[j * n + i], rb.at[j * n + i], sibling).wait_send()
                _remote(got, got, sb.at[j * n + i], rb.at[j * n + i], sibling).wait_recv()

    outs = pl.pallas_call(
        body, name=name,
        out_shape=tuple(pltpu.HBM(a.shape, a.dtype) for a in lands),
        in_specs=[_HBM] * (2 * n) + [_SEM] * 4 + [pl.BlockSpec(memory_space=pl.ANY)] * len(after),
        out_specs=tuple([_HBM] * n),
        input_output_aliases={n + i: i for i in range(n)}, compiler_params=_EFFECT,
    )(*srcs, *lands, send_a, recv_a, send_b, recv_b, *after)
    return list(outs), list(srcs)


def _join_cols(parts, name, tr):
    _, r, c = parts.shape

    def body(p_ref, o_ref):
        for j in range(N_DEV):
            o_ref[:, j * c:(j + 1) * c] = p_ref[j]

    return _pcall(body, name, (r // tr,), [pl.BlockSpec((N_DEV, tr, c), lambda i: (0, i, 0))],
                  pl.BlockSpec((tr, N_DEV * c), lambda i: (i, 0)), _sds((r, N_DEV * c), parts.dtype))(parts)


def _split_cols(full, name, tr):
    r, c = full.shape[0], full.shape[1] // N_DEV

    def body(f_ref, o_ref):
        for j in range(N_DEV):
            o_ref[j] = f_ref[:, j * c:(j + 1) * c]

    return _pcall(body, name, (r // tr,), [pl.BlockSpec((tr, N_DEV * c), lambda i: (i, 0))],
                  pl.BlockSpec((N_DEV, tr, c), lambda i: (0, i, 0)), _sds((N_DEV, r, c), full.dtype))(full)


def _my_slab():
    return (4 * lax.axis_index("x") + 2 * lax.axis_index("y") + lax.axis_index("c")).astype(jnp.int32).reshape(1)


def _adamw(parts, sent, w, m, v, name, tile):
    _, rows, cols = w.shape

    def body(me_ref, p_ref, s_ref, w_ref, m_ref, v_ref, g_out, d_out, m_out, v_out):
        me = me_ref[0]
        g = jnp.where(me == 0, s_ref[0], p_ref[0]).astype(F32)
        for k in range(1, N_DEV):
            g = g + jnp.where(me == k, s_ref[0], p_ref[k]).astype(F32)
        m1 = ADAM_B1 * m_ref[0] + (1.0 - ADAM_B1) * g
        v1 = ADAM_B2 * v_ref[0] + (1.0 - ADAM_B2) * (g * g)
        m_hat = m1 / (1.0 - ADAM_B1 ** ADAM_STEP)
        v_hat = v1 / (1.0 - ADAM_B2 ** ADAM_STEP)
        g_out[0] = g
        d_out[0] = -ADAM_LR * (m_hat / (jnp.sqrt(v_hat) + ADAM_EPS) + ADAM_WD * w_ref[0])
        m_out[0] = m1
        v_out[0] = v1

    row = pl.BlockSpec((1, tile, cols), lambda i, me: (0, i, 0))
    return pl.pallas_call(
        body, name=name, out_shape=[_sds((1, rows, cols))] * 4,
        grid_spec=pltpu.PrefetchScalarGridSpec(
            num_scalar_prefetch=1, grid=(rows // tile,),
            in_specs=[pl.BlockSpec((N_DEV, tile, cols), lambda i, me: (0, i, 0)),
                      pl.BlockSpec((1, tile, cols), lambda i, me: (me[0], i, 0)), row, row, row],
            out_specs=[row, row, row, row]),
        compiler_params=pltpu.CompilerParams(dimension_semantics=("arbitrary",), vmem_limit_bytes=VMEM_LIMIT),
    )(_my_slab(), parts, sent, w, m, v)


BIG = {
    "w_in": ((N_IN // N_DEV, D_MODEL), False, N_IN // N_DEV // 3),
    "w_glu": ((S5_WIDTH // N_DEV, S5_WIDTH), False, S5_WIDTH // N_DEV),
    "w_pa": ((S5_WIDTH, D_MODEL // N_DEV), True, S5_WIDTH),
    "w_pb": ((HG_WIDTH, D_MODEL // N_DEV), True, HG_WIDTH),
    "w_out": ((D_MODEL // N_DEV, D_MODEL), False, D_MODEL // N_DEV),
    "w_up": ((2 * D_FF // N_DEV, D_MODEL), False, 2 * D_FF // N_DEV // 4),
    "w_conv": ((CONV_W, 2 * D_FF // N_DEV), True, CONV_W),
    "w_down": ((D_FF // N_DEV, D_MODEL), False, D_FF // N_DEV // 2),
}
TRANSPOSED = ("w_in", "w_up", "s5_b_re", "s5_b_im")
UNALIGNED_COLS = ("w_conv",)


def _stored(n, arr):
    return jnp.swapaxes(arr, -1, -2) if n in TRANSPOSED else arr


def _join_shards(n, parts):
    (a, b), by_cols, _ = BIG[n]
    if not by_cols:
        return parts.reshape(N_DEV * a, b)
    if n in UNALIGNED_COLS:
        return _join_cols(parts, "join_" + n, min(a, 256))
    return parts.transpose(1, 0, 2).reshape(a, N_DEV * b)


def _split_shards(n, full):
    (a, b), by_cols, _ = BIG[n]
    if not by_cols:
        return full.reshape(N_DEV, a, b)
    if n in UNALIGNED_COLS:
        return _split_cols(full, "split_" + n, min(a, 256))
    return full.reshape(a, N_DEV, b).transpose(1, 0, 2)


SMALL_CORE = {
    "s5_b_re": GSC, "s5_b_im": GSC, "s5_c_re": GSC, "s5_c_im": GSC,
    "g_mix": (1, D_MODEL), "g_ffn": (1, D_MODEL), "g_final": (1, D_MODEL), "s5_d": (1, S5_WIDTH),
    "b_glu": (1, S5_WIDTH), "hg_norm_gain": (1, HG_WIDTH), "hg_lb_logits": (2, HG_WIDTH), "b_conv": (1, 2 * D_FF),
    "s5_log_dt": (1, S5_GROUPS), "s5_a_re": (S5_GROUPS, S5_STATE), "s5_a_im": (S5_GROUPS, S5_STATE), "loss": (1, 1),
}
BLOCK_ROWS = 32


def _small_rows():
    rows, r = {}, 0
    for n, core in SMALL_CORE.items():
        rows[n] = r
        r += BLOCK_ROWS if len(core) == 3 else -(-math.prod(core) // PACK_W)
    return rows, -(-r // SUBLANES) * SUBLANES


SMALL_ROW, SMALL_ROWS = _small_rows()


def _small_pieces(name):
    r, core = SMALL_ROW[name], SMALL_CORE[name]
    if len(core) == 3:
        return [((g, slice(None), slice(None)), slice(r + S5_GROUP * (g % 2), r + S5_GROUP * (g % 2 + 1)),
                 slice(S5_STATE * (g // 2), S5_STATE * (g // 2 + 1))) for g in range(S5_GROUPS)]
    pieces = []
    for i in range(core[0]):
        for c0 in range(0, core[1], PACK_W):
            w, flat = min(PACK_W, core[1] - c0), i * core[1] + c0
            pieces.append(((slice(i, i + 1), slice(c0, c0 + w)), slice(r + flat // PACK_W, r + flat // PACK_W + 1),
                           slice(flat % PACK_W, flat % PACK_W + w)))
    return pieces


def _core_index(ref, name, idx):
    return (0,) * (len(ref.shape) - len(SMALL_CORE[name])) + idx


def _pack_small_grads(grads):
    names = list(SMALL_CORE)

    def body(*refs):
        pack = refs[-1]
        pack[...] = jnp.zeros_like(pack)
        for ref, n in zip(refs, names):
            for idx, rows, lanes in _small_pieces(n):
                pack[rows, lanes] = ref[_core_index(ref, n, idx)]

    return _pcall(body, "pack_small_grads", (1,), [_full(grads[n].shape) for n in names],
                  _full((SMALL_ROWS, PACK_W)), _sds((SMALL_ROWS, PACK_W)))(*[grads[n] for n in names])


def _adamw_small(parts, sent, names, rows, given, name):
    lo, hi = rows
    k = len(names)
    shapes = [given[n].shape for n in names]

    def body(*refs):
        me, p_ref, s_ref, ins, outs = refs[0][0], refs[1], refs[2], refs[3:3 + 3 * k], refs[3 + 3 * k:3 + 7 * k]
        packs, results = refs[3 + 7 * k:6 + 7 * k], refs[6 + 7 * k:]
        for j, pack in enumerate(packs):
            pack[...] = jnp.zeros_like(pack)
            for ref, n in zip(ins[j * k:(j + 1) * k], names):
                for idx, prow, lanes in _small_pieces(n):
                    pack[slice(prow.start - lo, prow.stop - lo), lanes] = ref[_core_index(ref, n, idx)]
        mine = s_ref[lo:hi, :]
        g = jnp.where(me == 0, mine, p_ref[0, lo:hi, :])
        for d in range(1, N_DEV):
            g = g + jnp.where(me == d, mine, p_ref[d, lo:hi, :])
        m1 = ADAM_B1 * packs[1][...] + (1.0 - ADAM_B1) * g
        v1 = ADAM_B2 * packs[2][...] + (1.0 - ADAM_B2) * (g * g)
        m_hat = m1 / (1.0 - ADAM_B1 ** ADAM_STEP)
        v_hat = v1 / (1.0 - ADAM_B2 ** ADAM_STEP)
        results[0][...] = g
        results[1][...] = -ADAM_LR * (m_hat / (jnp.sqrt(v_hat) + ADAM_EPS) + ADAM_WD * packs[0][...])
        results[2][...] = m1
        results[3][...] = v1
        for j, result in enumerate(results):
            for ref, n in zip(outs[j * k:(j + 1) * k], names):
                for idx, prow, lanes in _small_pieces(n):
                    ref[_core_index(ref, n, idx)] = result[slice(prow.start - lo, prow.stop - lo), lanes]

    flat = _pcall(body, name, (1,),
                  [pl.BlockSpec(memory_space=pltpu.SMEM), _full(parts.shape), _full(sent.shape)]
                  + [_full(s) for s in shapes] * 3,
                  [_full(s) for s in shapes] * 4, [_sds(s) for s in shapes] * 4,
                  scratch=[pltpu.VMEM((hi - lo, PACK_W), F32)] * 7,
                  )(_my_slab(), parts, sent, *[given[pre + n] for pre in ("", "m_", "v_") for n in names])
    return {n: [flat[j * k + i] for j in range(4)] for i, n in enumerate(names)}


def kernel(x, g_mix, w_in, s5_a_re, s5_a_im, s5_log_dt, s5_b_re, s5_b_im, s5_c_re, s5_c_im, s5_d, w_glu, b_glu, hg_lb_logits, hg_norm_gain, w_pa, w_pb, w_out, g_ffn, w_up, w_conv, b_conv, w_down, g_final, loss_target, m_g_mix, m_w_in, m_s5_a_re, m_s5_a_im, m_s5_log_dt, m_s5_b_re, m_s5_b_im, m_s5_c_re, m_s5_c_im, m_s5_d, m_w_glu, m_b_glu, m_hg_lb_logits, m_hg_norm_gain, m_w_pa, m_w_pb, m_w_out, m_g_ffn, m_w_up, m_w_conv, m_b_conv, m_w_down, m_g_final, v_g_mix, v_w_in, v_s5_a_re, v_s5_a_im, v_s5_log_dt, v_s5_b_re, v_s5_b_im, v_s5_c_re, v_s5_c_im, v_s5_d, v_w_glu, v_b_glu, v_hg_lb_logits, v_hg_norm_gain, v_w_pa, v_w_pb, v_w_out, v_g_ffn, v_w_up, v_w_conv, v_b_conv, v_w_down, v_g_final):
    given = dict(locals())
    small_names = [n for n in SMALL_CORE if n != "loss"]

    pay = {n: given[n][0] if n == "w_conv" else _stored(n, given[n])[0].astype(BF16) for n in BIG}
    groups = {"in": ["w_in"], "mix": ["w_glu", "w_pa", "w_pb", "w_out"], "ffn": ["w_up", "w_down", "w_conv"]}
    gathers, order = {}, pay["w_in"]
    for grp, names in groups.items():
        gathers[grp], order = _gather_start("gather_" + grp + "_start", [pay[n] for n in names], order)

    forwards = {}

    def forward(grp, *after):
        if grp == "in":
            after = (*after, order)
        forwards[grp] = _gather_forward("gather_" + grp + "_forward", gathers[grp], *after)

    def weights(grp, *after):
        if grp not in forwards:
            forward(grp, *after)
        got, _ = _gather_wait("gather_" + grp + "_wait", *forwards[grp], *after)
        return {n: _join_shards(n, g) for n, g in zip(groups[grp], got)}

    weights.forward = forward

    in_flight, started = [], []

    def emit(grads):
        names = list(grads)
        state, token = _exchange_start("grads_" + names[0] + "_start", [_split_shards(n, grads[n]) for n in names],
                                       grads[names[0]])
        in_flight.append((names, state))
        return token

    def emit_small(grads):
        pack = _pack_small_grads(grads)
        state, token = _gather_start("grads_small_start", [pack], pack)
        in_flight.append((["small"], state))
        started.append(token)

    sp = {n: (given[n] if n in ("g_final", "hg_lb_logits") else _stored(n, given[n])[0]) for n in small_names}
    sp["g_mix"] = _after(sp["g_mix"], order)
    dx = _local_step(x, loss_target, weights, sp, emit, emit_small)

    res = {}
    after = [started[-1]]
    for names, state in in_flight:
        if names == ["small"]:
            state, forwarded = _gather_forward("grads_small_forward", state, *after)
            parts, sent = _gather_wait("grads_small_wait", state, forwarded)
        else:
            parts, sent = _exchange_wait("grads_" + names[0] + "_wait", state, *after)
        if names != ["small"]:
            after = []
            for n, part, mine in zip(names, parts, sent):
                raw = _adamw(part, mine, *[_stored(n, given[pre + n]) for pre in ("", "m_", "v_")], "adamw_" + n,
                             BIG[n][2])
                res[n] = [_stored(n, r) for r in raw]
                after.append(raw[0])
            continue
        sgiven = {pre + n: _stored(n, given[pre + n]) for pre in ("", "m_", "v_") for n in small_names}
        for pre in ("", "m_", "v_"):
            sgiven[pre + "g_final"] = given[pre + "g_final"].reshape(1, D_MODEL)
            sgiven[pre + "loss"] = jnp.zeros((1, 1), F32)
        raw = _adamw_small(parts[0], sent[0], list(SMALL_CORE), (0, SMALL_ROWS), sgiven, "adamw_small")
        res.update({n: [_stored(n, r) for r in raw[n]] for n in small_names})
        res["g_final"] = [r.reshape(D_MODEL) for r in raw["g_final"]]
        total_loss = raw["loss"][0].reshape(())
        after = [raw["s5_b_re"][0], raw["g_mix"][0]]
    return (total_loss, dx, *[res[n][0] for n in WEIGHT_ORDER], *[res[n][1] for n in WEIGHT_ORDER],
            *[res[n][2] for n in WEIGHT_ORDER], *[res[n][3] for n in WEIGHT_ORDER])
```

```python
import math

import jax
import jax.numpy as jnp
from jax import lax
from jax.experimental import pallas as pl
from jax.experimental.pallas import tpu as pltpu

F32 = jnp.float32
BF16 = jnp.bfloat16

D_MODEL = 1024
S5_WIDTH = 512
S5_GROUP = 16
S5_GROUPS = 32
S5_STATE = 64
S5_N = S5_GROUPS * S5_STATE
HG_WIDTH = 512
HG_HEAD = 128
HG_HEADS = 4
D_FF = 2816
CONV_W = 3
CHUNK = 64
N_IN = S5_WIDTH + 4 * HG_WIDTH + 2 * D_MODEL
EPS = 1e-6
QSCALE = HG_HEAD ** -0.5

ADAM_LR = 0.001
ADAM_B1 = 0.9
ADAM_B2 = 0.999
ADAM_EPS = 1e-08
ADAM_WD = 0.01
ADAM_STEP = 10

N_DEV = 8
V7X_VMEM_BYTES = 64 * 1024 * 1024
VMEM_LIMIT = V7X_VMEM_BYTES * 7 // 8
SUBLANES = 8
LANES = 128
PACK_W = 1024

WEIGHT_ORDER = ("g_mix", "w_in", "s5_a_re", "s5_a_im", "s5_log_dt", "s5_b_re", "s5_b_im", "s5_c_re", "s5_c_im",
                "s5_d", "w_glu", "b_glu", "hg_lb_logits", "hg_norm_gain", "w_pa", "w_pb", "w_out", "g_ffn",
                "w_up", "w_conv", "b_conv", "w_down", "g_final")


def _pcall(body, name, grid, in_specs, out_specs, out_shape, scratch=()):
    return pl.pallas_call(
        body, name=name, grid=grid, in_specs=in_specs, out_specs=out_specs, out_shape=out_shape,
        scratch_shapes=list(scratch),
        compiler_params=pltpu.CompilerParams(dimension_semantics=("arbitrary",) * len(grid),
                                             vmem_limit_bytes=VMEM_LIMIT),
    )


def _full(shape):
    return pl.BlockSpec(shape, lambda *_: (0,) * len(shape))


def _sds(shape, dtype=F32):
    return jax.ShapeDtypeStruct(shape, dtype)


def _dot(a, b):
    return jnp.dot(a.astype(BF16), b.astype(BF16), preferred_element_type=F32)


def _dot_nt(a, b):
    return lax.dot_general(a.astype(BF16), b.astype(BF16), (((1,), (1,)), ((), ())), preferred_element_type=F32)


def _dot_tn(a, b):
    return lax.dot_general(a.astype(BF16), b.astype(BF16), (((0,), (0,)), ((), ())), preferred_element_type=F32)


def _split(a):
    hi = a.astype(BF16)
    return hi, (a - hi.astype(F32)).astype(BF16)


def _hdot(a, b, dims=(((1,), (0,)), ((), ()))):
    (ah, al), (bh, bl) = _split(a), _split(b)
    dot = lambda p, q: lax.dot_general(p, q, dims, preferred_element_type=F32)
    return dot(ah, bh) + (dot(al, bh) + dot(ah, bl))


def _hdot_tn(a, b):
    return _hdot(a, b, (((0,), (0,)), ((), ())))


def _sigmoid(x):
    return jax.nn.sigmoid(x)


GELU_C = math.sqrt(2.0 / math.pi)
GELU_A = 0.044715


def _gelu(x):
    return 0.5 * x * (1.0 + jnp.tanh(GELU_C * (x + GELU_A * (x * x * x))))


def _gelu_grad(x):
    t = jnp.tanh(GELU_C * (x + GELU_A * (x * x * x)))
    return 0.5 * (1.0 + t) + 0.5 * x * (1.0 - t * t) * (GELU_C * (1.0 + 3.0 * GELU_A * x * x))


def _cumsum_rows(v, reverse=False):
    n = v.shape[0]
    row = lax.broadcasted_iota(jnp.int32, v.shape, 0)
    s = 1
    while s < n:
        if reverse:
            v = v + jnp.where(row < n - s, pltpu.roll(v, n - s, axis=0), 0.0)
        else:
            v = v + jnp.where(row >= s, pltpu.roll(v, s, axis=0), 0.0)
        s *= 2
    return v


def _token_tile(seq):
    return min(256, seq)


def _s5_coeffs(a_re, a_im, ldt):
    dt = jnp.exp(ldt)
    mag = jnp.exp(a_re * dt)
    ang = a_im * dt
    lb_re = mag * jnp.cos(ang)
    lb_im = mag * jnp.sin(ang)
    den = a_re * a_re + a_im * a_im
    n_re = lb_re - 1.0
    n_im = lb_im
    co_re = (n_re * a_re + n_im * a_im) / den
    co_im = (n_im * a_re - n_re * a_im) / den
    return lb_re, lb_im, co_re, co_im


GS, GSC = (S5_GROUPS, S5_STATE), (S5_GROUPS, S5_GROUP, S5_STATE)


def _params_fwd(a_re, a_im, ldt, bt_re, bt_im, logits):
    def body(are, aim, ld, bre, bim, lg, lr_o, li_o, bbr_o, bbi_o, lb_o):
        lr, li, co_re, co_im = _s5_coeffs(are[...], aim[...], ld[...])
        lr_o[...] = lr
        li_o[...] = li
        for g in range(S5_GROUPS):
            cr, ci = co_re[g:g + 1, :], co_im[g:g + 1, :]
            bbr_o[g] = cr * bre[g] - ci * bim[g]
            bbi_o[g] = cr * bim[g] + ci * bre[g]
        lb_o[...] = _sigmoid(lg[0:1, :] - lg[1:2, :])

    return _pcall(body, "params_fwd", (1,),
                  [_full(GS), _full(GS), _full((S5_GROUPS, 1)), _full(GSC), _full(GSC), _full((2, HG_WIDTH))],
                  [_full(GS), _full(GS), _full(GSC), _full(GSC), _full((1, HG_WIDTH))],
                  [_sds(GS), _sds(GS), _sds(GSC), _sds(GSC), _sds((1, HG_WIDTH))],
                  )(a_re, a_im, ldt, bt_re, bt_im, logits)


def _params_bwd(a_re, a_im, ldt, bt_re, bt_im, logits, dlr, dli, dbbr, dbbi, dlb):
    def body(are, aim, ld, bre, bim, lg, dlr_r, dli_r, dbbr_r, dbbi_r, dlb_r,
             dare_o, daim_o, dld_o, dbre_o, dbim_o, dlg_o, dcr_ref, dci_ref):
        (_, _, co_re, co_im), vjp = jax.vjp(_s5_coeffs, are[...], aim[...], ld[...])
        for g in range(S5_GROUPS):
            cr, ci = co_re[g:g + 1, :], co_im[g:g + 1, :]
            gr, gi, br, bi = dbbr_r[g], dbbi_r[g], bre[g], bim[g]
            dbre_o[g] = cr * gr + ci * gi
            dbim_o[g] = cr * gi - ci * gr
            dcr_ref[g:g + 1, :] = jnp.sum(gr * br + gi * bi, axis=0, keepdims=True)
            dci_ref[g:g + 1, :] = jnp.sum(gi * br - gr * bi, axis=0, keepdims=True)
        dare, daim, dld = vjp((dlr_r[...], dli_r[...], dcr_ref[...], dci_ref[...]))
        dare_o[...] = dare
        daim_o[...] = daim
        dld_o[...] = dld
        lb = _sigmoid(lg[0:1, :] - lg[1:2, :])
        d0 = dlb_r[...] * lb * (1.0 - lb)
        dlg_o[0:1, :] = d0
        dlg_o[1:2, :] = -d0

    return _pcall(body, "params_bwd", (1,),
                  [_full(GS), _full(GS), _full((S5_GROUPS, 1)), _full(GSC), _full(GSC), _full((2, HG_WIDTH)),
                   _full(GS), _full(GS), _full(GSC), _full(GSC), _full((1, HG_WIDTH))],
                  [_full(GS), _full(GS), _full((S5_GROUPS, 1)), _full(GSC), _full(GSC), _full((2, HG_WIDTH))],
                  [_sds(GS), _sds(GS), _sds((S5_GROUPS, 1)), _sds(GSC), _sds(GSC), _sds((2, HG_WIDTH))],
                  scratch=[pltpu.VMEM(GS, F32), pltpu.VMEM(GS, F32)],
                  )(a_re, a_im, ldt, bt_re, bt_im, logits, dlr, dli, dbbr, dbbi, dlb)


def _band_blocks(m):
    g, r, c = m.shape
    gb = g // S5_BANDS
    m4 = m.astype(BF16).reshape(S5_BANDS, gb, r, c)
    on_diag = jnp.eye(gb, dtype=bool)[None, :, None, :, None]
    return jnp.where(on_diag, m4[:, :, :, None, :], 0).reshape(S5_BANDS, gb * r, gb * c)


def _diag_blocks(band, r, c):
    g, nb = band.shape[0] // r, band.shape[1] // c
    on_diag = (jnp.arange(g) % nb)[:, None, None, None] == jnp.arange(nb)[None, None, :, None]
    return jnp.sum(jnp.where(on_diag, band.reshape(g, r, nb, c), 0.0), axis=2)


def _in_proj(x, g_mix, w_in, tm):
    t = x.shape[0]

    def body(x_ref, g_ref, w_ref, u_ref, za_ref, zh_ref, zg_ref):
        xv = x_ref[...]
        r = lax.rsqrt(jnp.mean(xv * xv, axis=-1, keepdims=True) + EPS)
        u = (xv * r * g_ref[...]).astype(BF16)
        u_ref[...] = u
        za_ref[...] = _dot_nt(u, w_ref[0:S5_WIDTH, :])
        zh_ref[...] = _dot_nt(u, w_ref[S5_WIDTH:S5_WIDTH + 4 * HG_WIDTH, :])
        zg_ref[...] = _dot_nt(u, w_ref[S5_WIDTH + 4 * HG_WIDTH:, :]).astype(BF16)

    row = lambda w: pl.BlockSpec((tm, w), lambda i: (i, 0))
    return _pcall(body, "in_proj", (t // tm,),
                  [row(D_MODEL), _full((1, D_MODEL)), _full((N_IN, D_MODEL))],
                  [row(D_MODEL), row(S5_WIDTH), row(4 * HG_WIDTH), row(2 * D_MODEL)],
                  [_sds((t, D_MODEL), BF16), _sds((t, S5_WIDTH)), _sds((t, 4 * HG_WIDTH)),
                   _sds((t, 2 * D_MODEL), BF16)],
                  )(x, g_mix, w_in)


S5_LANES = 512
S5_BANDS = 4


def _band(q):
    return (slice(q * S5_WIDTH // S5_BANDS, (q + 1) * S5_WIDTH // S5_BANDS),
            slice(q * S5_N // S5_BANDS, (q + 1) * S5_N // S5_BANDS))


def _im(st):
    return slice(S5_N + st.start, S5_N + st.stop)


SCAN_UNROLL = 8


def _complex_scan(buf_ref, lam_ref, st_ref, nb, ts, reverse):
    lanes = [slice(cc * S5_LANES, (cc + 1) * S5_LANES) for cc in range(S5_N // S5_LANES)]
    chains = [(b, re) for b in range(nb) for re in lanes]
    nch = len(chains)
    wr = {re.start: lam_ref[0:1, re] for re in lanes}
    wi = {re.start: -lam_ref[1:2, re] if reverse else lam_ref[1:2, re] for re in lanes}

    def block(ib, carry):
        vr, vi = list(carry[:nch]), list(carry[nch:])
        first = ts - SCAN_UNROLL - ib * SCAN_UNROLL if reverse else ib * SCAN_UNROLL
        first = pl.multiple_of(first, SCAN_UNROLL)
        for k in range(SCAN_UNROLL):
            row = pl.ds(first + (SCAN_UNROLL - 1 - k if reverse else k), 1)
            for c, (b, re) in enumerate(chains):
                nr = wr[re.start] * vr[c] - wi[re.start] * vi[c] + buf_ref[b, row, re]
                ni = wr[re.start] * vi[c] + wi[re.start] * vr[c] + buf_ref[b, row, _im(re)]
                buf_ref[b, row, re] = nr
                buf_ref[b, row, _im(re)] = ni
                vr[c], vi[c] = nr, ni
        return tuple(vr + vi)

    init = tuple(st_ref[b, 0:1, re] for b, re in chains) + tuple(st_ref[b, 1:2, re] for b, re in chains)
    last = lax.fori_loop(0, ts // SCAN_UNROLL, block, init)
    for c, (b, re) in enumerate(chains):
        st_ref[b, 0:1, re] = last[c]
        st_ref[b, 1:2, re] = last[nch + c]


BAND_CH = S5_WIDTH // S5_BANDS
BAND_ST = S5_N // S5_BANDS


def _s5_fwd(za, b_bands, lam, c_bands, dskip, nb, seq, ts):
    nts = seq // ts

    def body(za_ref, br_ref, bi_ref, lam_ref, cr_ref, ci_ref, d_ref, xs_ref, y_ref, buf_ref, st_ref):
        @pl.when(pl.program_id(0) == 0)
        def _():
            st_ref[...] = jnp.zeros_like(st_ref)

        for b in range(nb):
            zav = za_ref[b]
            for q in range(S5_BANDS):
                ch, st = _band(q)
                buf_ref[b, :, st] = _dot(zav[:, ch], br_ref[q])
                buf_ref[b, :, _im(st)] = _dot(zav[:, ch], bi_ref[q])
        _complex_scan(buf_ref, lam_ref, st_ref, nb, ts, reverse=False)
        for b in range(nb):
            zav = za_ref[b]
            xs_ref[b] = buf_ref[b].astype(BF16)
            for q in range(S5_BANDS):
                ch, st = _band(q)
                y_ref[b, :, ch] = (_dot(xs_ref[b, :, st], cr_ref[q]) + _dot(xs_ref[b, :, _im(st)], ci_ref[q])
                                   + d_ref[:, ch] * zav[:, ch])

    tok = lambda w: pl.BlockSpec((nb, ts, w), lambda j: (0, j, 0))
    to_st, to_ch = _full((S5_BANDS, BAND_CH, BAND_ST)), _full((S5_BANDS, BAND_ST, BAND_CH))
    return _pcall(body, "s5_fwd", (nts,),
                  [tok(S5_WIDTH), to_st, to_st, _full((2, S5_N)), to_ch, to_ch, _full((1, S5_WIDTH))],
                  [tok(2 * S5_N), tok(S5_WIDTH)],
                  [_sds((nb, seq, 2 * S5_N), BF16), _sds((nb, seq, S5_WIDTH))],
                  scratch=[pltpu.VMEM((nb, ts, 2 * S5_N), F32), pltpu.VMEM((nb, 2, S5_N), F32)],
                  )(za, *b_bands, lam, *c_bands, dskip)


def _hgrn_gates(zq, zf, lbh):
    sf = _sigmoid(zf)
    f = lbh + (1.0 - lbh) * sf
    sq = _sigmoid(zq)
    qa = zq * sq * QSCALE
    bc = _cumsum_rows(jnp.log(f))
    bm = bc[CHUNK // 2 - 1:CHUNK // 2, :]
    bl = bc[CHUNK - 1:CHUNK, :]
    return sf, f, sq, qa, bc, bm, bl


def _hgrn_fwd(zh, lb, nb, seq):
    nc = seq // CHUNK

    def body(zh_ref, lb_ref, o_ref, sts_ref, st_ref):
        @pl.when(pl.program_id(0) == 0)
        def _():
            st_ref[...] = jnp.zeros_like(st_ref)

        causal = (lax.broadcasted_iota(jnp.int32, (CHUNK, CHUNK), 0)
                  >= lax.broadcasted_iota(jnp.int32, (CHUNK, CHUNK), 1))
        for b in range(nb):
            for h in range(HG_HEADS):
                hs = slice(h * HG_HEAD, (h + 1) * HG_HEAD)
                zq = zh_ref[b, :, h * HG_HEAD:(h + 1) * HG_HEAD]
                zf = zh_ref[b, :, HG_WIDTH + h * HG_HEAD:HG_WIDTH + (h + 1) * HG_HEAD]
                zi = zh_ref[b, :, 2 * HG_WIDTH + h * HG_HEAD:2 * HG_WIDTH + (h + 1) * HG_HEAD]
                _, f, _, qa, bc, bm, bl = _hgrn_gates(zq, zf, lb_ref[:, hs])
                k = 1.0 - f
                qt = qa * jnp.exp(bc - bm)
                kt = k * jnp.exp(bm - bc)
                qb = qa * jnp.exp(bc)
                kd = k * jnp.exp(bl - bc)
                st = st_ref[b, h]
                sts_ref[b, 0, h] = st
                a = jnp.where(causal, _dot_nt(qt, kt), 0.0)
                o_ref[b, :, hs] = _dot(a, zi) + _dot_nt(qb, st)
                st_ref[b, h] = st * jnp.exp(bl) + _dot_tn(zi, kd)

    return _pcall(body, "hgrn_fwd", (nc,),
                  [pl.BlockSpec((nb, CHUNK, 4 * HG_WIDTH), lambda c: (0, c, 0)), _full((1, HG_WIDTH))],
                  [pl.BlockSpec((nb, CHUNK, HG_WIDTH), lambda c: (0, c, 0)),
                   pl.BlockSpec((nb, 1, HG_HEADS, HG_HEAD, HG_HEAD), lambda c: (0, c, 0, 0, 0))],
                  [_sds((nb, seq, HG_WIDTH)), _sds((nb, nc, HG_HEADS, HG_HEAD, HG_HEAD))],
                  scratch=[pltpu.VMEM((nb, HG_HEADS, HG_HEAD, HG_HEAD), F32)])(zh, lb)


def _head_rms(o):
    parts = []
    for h in range(HG_HEADS):
        oh = o[:, h * HG_HEAD:(h + 1) * HG_HEAD]
        r = lax.rsqrt(jnp.mean(oh * oh, axis=-1, keepdims=True) + EPS)
        parts.append(jnp.broadcast_to(r, oh.shape))
    return jnp.concatenate(parts, axis=1)


def _head_mean(v):
    parts = []
    for h in range(HG_HEADS):
        vh = v[:, h * HG_HEAD:(h + 1) * HG_HEAD]
        parts.append(jnp.broadcast_to(jnp.mean(vh, axis=-1, keepdims=True), vh.shape))
    return jnp.concatenate(parts, axis=1)


def _mix_fwd(x, y0, o, zh, zgt, w_glu, b_glu, gain, w_pa, w_pb, w_out, g_ffn, tm):
    t = x.shape[0]

    def body(x_ref, y0_ref, o_ref, zg_ref, zgt_ref, wglu_ref, bglu_ref, gain_ref, wpa_ref, wpb_ref, wout_ref,
             gffn_ref, x1_ref, u2_ref, pa_ref, pb_ref, ya2_ref, yb_ref):
        ya1 = _gelu(y0_ref[...])
        s = _sigmoid(_dot(ya1, wglu_ref[...]) + bglu_ref[...])
        ya2 = (ya1 * s).astype(BF16)
        ov = o_ref[...]
        zg = zg_ref[...]
        yb = (ov * _head_rms(ov) * gain_ref[...] * (zg * _sigmoid(zg))).astype(BF16)
        ya2_ref[...] = ya2
        yb_ref[...] = yb
        pa = jnp.dot(ya2, wpa_ref[...], preferred_element_type=F32)
        pb = jnp.dot(yb, wpb_ref[...], preferred_element_type=F32)
        pa_ref[...] = pa.astype(BF16)
        pb_ref[...] = pb.astype(BF16)
        m = (_sigmoid(zgt_ref[:, 0:D_MODEL].astype(F32)) * pa
             + _sigmoid(zgt_ref[:, D_MODEL:].astype(F32)) * pb)
        x1 = x_ref[...] + _dot(m, wout_ref[...])
        x1_ref[...] = x1
        r = lax.rsqrt(jnp.mean(x1 * x1, axis=-1, keepdims=True) + EPS)
        u2_ref[...] = (x1 * r * gffn_ref[...]).astype(BF16)

    row = lambda w: pl.BlockSpec((tm, w), lambda i: (i, 0))
    return _pcall(body, "mix_fwd", (t // tm,),
                  [row(D_MODEL), row(S5_WIDTH), row(HG_WIDTH), pl.BlockSpec((tm, HG_WIDTH), lambda i: (i, 3)),
                   row(2 * D_MODEL), _full((S5_WIDTH, S5_WIDTH)), _full((1, S5_WIDTH)), _full((1, HG_WIDTH)),
                   _full((S5_WIDTH, D_MODEL)), _full((HG_WIDTH, D_MODEL)), _full((D_MODEL, D_MODEL)),
                   _full((1, D_MODEL))],
                  [row(D_MODEL), row(D_MODEL), row(D_MODEL), row(D_MODEL), row(S5_WIDTH), row(HG_WIDTH)],
                  [_sds((t, D_MODEL)), _sds((t, D_MODEL), BF16), _sds((t, D_MODEL), BF16), _sds((t, D_MODEL), BF16),
                   _sds((t, S5_WIDTH), BF16), _sds((t, HG_WIDTH), BF16)],
                  )(x, y0, o, zh, zgt, w_glu, b_glu, gain, w_pa, w_pb, w_out, g_ffn)


FF_COLS = 256
FF_UP_TILE = 2 * D_FF // 2


def _ffn_up(u2, w_up, tm):
    t = u2.shape[0]
    n = 2 * D_FF

    def body(u_ref, w_ref, h_ref):
        h_ref[...] = _dot_nt(u_ref[...], w_ref[...]).astype(BF16)

    return _pcall(body, "ffn_up", (n // FF_UP_TILE, t // tm),
                  [pl.BlockSpec((tm, D_MODEL), lambda j, i: (i, 0)),
                   pl.BlockSpec((FF_UP_TILE, D_MODEL), lambda j, i: (j, 0))],
                  pl.BlockSpec((tm, FF_UP_TILE), lambda j, i: (i, j)),
                  _sds((t, n), BF16))(u2, w_up)


HALO = 16


def _shift_matrix(tm):
    r = lax.broadcasted_iota(jnp.int32, (tm, tm), 0)
    c = lax.broadcasted_iota(jnp.int32, (tm, tm), 1)
    return jnp.where(r == c + 1, 1.0, 0.0).astype(BF16)


def _conv_cols(h_ref, halo_ref, valid, wc_ref, bc_ref, c0):
    cs = slice(c0, c0 + FF_COLS)
    cur = h_ref[:, cs].astype(F32)
    prev = jnp.where(valid, halo_ref[:, cs].astype(F32), 0.0)
    full = jnp.concatenate([prev, cur], axis=0)
    h1 = pltpu.roll(full, 1, axis=0)[HALO:]
    h2 = pltpu.roll(full, 2, axis=0)[HALO:]
    return h2 * wc_ref[0:1, cs] + h1 * wc_ref[1:2, cs] + cur * wc_ref[2:3, cs] + bc_ref[:, cs]


def _ffn_down_loss(h, x1, tgt, w_conv, b_conv, w_down, g_final, seq, tm):
    t = h.shape[0]
    tps = seq // tm
    n = 2 * D_FF

    def body(h_ref, halo_ref, x1_ref, tgt_ref, wc_ref, bc_ref, wd_ref, gf_ref,
             hc_ref, a_ref, dx2_ref, dx2b_ref, loss_ref, dgf_ref):
        i = pl.program_id(0)

        @pl.when(i == 0)
        def _():
            loss_ref[...] = jnp.zeros_like(loss_ref)
            dgf_ref[...] = jnp.zeros_like(dgf_ref)

        valid = (i % tps) != 0
        x2 = x1_ref[...]
        for j in range(D_FF // FF_COLS):
            gate = _conv_cols(h_ref, halo_ref, valid, wc_ref, bc_ref, j * FF_COLS)
            val = _conv_cols(h_ref, halo_ref, valid, wc_ref, bc_ref, D_FF + j * FF_COLS)
            hc_ref[:, j * FF_COLS:(j + 1) * FF_COLS] = gate.astype(BF16)
            hc_ref[:, D_FF + j * FF_COLS:D_FF + (j + 1) * FF_COLS] = val.astype(BF16)
            a = (gate * _sigmoid(gate) * val).astype(BF16)
            a_ref[:, j * FF_COLS:(j + 1) * FF_COLS] = a
            x2 = x2 + jnp.dot(a, wd_ref[j * FF_COLS:(j + 1) * FF_COLS, :], preferred_element_type=F32)
        r = lax.rsqrt(jnp.mean(x2 * x2, axis=-1, keepdims=True) + EPS)
        xn = x2 * r
        g = gf_ref[...]
        e = xn * g - tgt_ref[...]
        loss_ref[...] += (0.5 / D_MODEL) * jnp.sum(e * e).reshape(1, 1)
        dy = e * (1.0 / D_MODEL)
        dgf_ref[...] += jnp.sum(dy * xn, axis=0, keepdims=True)
        dxn = dy * g
        dx2 = r * (dxn - xn * jnp.mean(dxn * xn, axis=-1, keepdims=True))
        dx2_ref[...] = dx2
        dx2b_ref[...] = dx2.astype(BF16)

    row = lambda w: pl.BlockSpec((tm, w), lambda i: (i, 0))
    halo = pl.BlockSpec((HALO, n), lambda i: (jnp.maximum(i * (tm // HALO) - 1, 0), 0))
    return _pcall(body, "ffn_down_loss", (t // tm,),
                  [row(n), halo, row(D_MODEL), row(D_MODEL), _full((CONV_W, n)), _full((1, n)),
                   _full((D_FF, D_MODEL)), _full((1, D_MODEL))],
                  [row(n), row(D_FF), row(D_MODEL), row(D_MODEL), _full((1, 1)), _full((1, D_MODEL))],
                  [_sds((t, n), BF16), _sds((t, D_FF), BF16), _sds((t, D_MODEL)), _sds((t, D_MODEL), BF16),
                   _sds((1, 1)), _sds((1, D_MODEL))],
                  )(h, h, x1, tgt, w_conv, b_conv, w_down, g_final)


def _wgrad(a, b, name, tn, out_dtype=F32, band=None, after=None):
    t, m = a.shape
    n = b.shape[1] if band is None else band
    nbands = 1 if band is None else b.shape[1] // band
    after = b if after is None else after

    def body(a_ref, b_ref, after_ref, o_ref):
        o_ref[...] = _dot_tn(a_ref[...], b_ref[...]).astype(out_dtype)

    return _pcall(body, name, (m // tn,),
                  [pl.BlockSpec((t, tn), lambda i: (0, i)), pl.BlockSpec((t, n), lambda i: (0, i % nbands)),
                   pl.BlockSpec(memory_space=pl.ANY)],
                  pl.BlockSpec((tn, n), lambda i: (i, 0)), _sds((m, n), out_dtype))(a, b, after)


def _ffn_bwd_act(dx2b, hc, w_down, tm):
    t = hc.shape[0]
    n = 2 * D_FF

    def body(dx2_ref, hc_ref, wd_ref, dhc_ref, dbc_ref):
        @pl.when(pl.program_id(0) == 0)
        def _():
            dbc_ref[...] = jnp.zeros_like(dbc_ref)

        dx2 = dx2_ref[...]
        for j in range(D_FF // FF_COLS):
            gs = slice(j * FF_COLS, (j + 1) * FF_COLS)
            vs = slice(D_FF + j * FF_COLS, D_FF + (j + 1) * FF_COLS)
            gate = hc_ref[:, gs].astype(F32)
            val = hc_ref[:, vs].astype(F32)
            da = _dot_nt(dx2, wd_ref[gs, :])
            sg = _sigmoid(gate)
            dgate = da * val * (sg * (1.0 + gate * (1.0 - sg)))
            dval = da * (gate * sg)
            dhc_ref[:, gs] = dgate.astype(BF16)
            dhc_ref[:, vs] = dval.astype(BF16)
            dbc_ref[:, gs] += jnp.sum(dgate, axis=0, keepdims=True)
            dbc_ref[:, vs] += jnp.sum(dval, axis=0, keepdims=True)

    row = lambda w: pl.BlockSpec((tm, w), lambda i: (i, 0))
    return _pcall(body, "ffn_bwd_act", (t // tm,),
                  [row(D_MODEL), row(n), _full((D_FF, D_MODEL))],
                  [row(n), _full((1, n))],
                  [_sds((t, n), BF16), _sds((1, n))],
                  )(dx2b, hc, w_down)


def _ffn_bwd_up(dhc, h, dx2, x1, w_conv, w_up, g_ffn, seq, tm):
    t = dhc.shape[0]
    tps = seq // tm
    n = 2 * D_FF
    last = t // HALO - 1

    def body(dhc_ref, halo_ref, h_ref, dx2_ref, x1_ref, wc_ref, wu_ref, gf_ref,
             dh_ref, dx1_ref, dx1b_ref, dgf_ref, dwc_ref):
        i = pl.program_id(0)

        @pl.when(i == 0)
        def _():
            dgf_ref[...] = jnp.zeros_like(dgf_ref)
            dwc_ref[...] = jnp.zeros_like(dwc_ref)

        valid = ((i + 1) % tps) != 0
        du2 = jnp.zeros((tm, D_MODEL), F32)
        for j in range(n // FF_COLS):
            cs = slice(j * FF_COLS, (j + 1) * FF_COLS)
            cur = dhc_ref[:, cs].astype(F32)
            nxt = jnp.where(valid, halo_ref[:, cs].astype(F32), 0.0)
            full = jnp.concatenate([cur, nxt], axis=0)
            d1 = pltpu.roll(full, tm + HALO - 1, axis=0)[:tm]
            d2 = pltpu.roll(full, tm + HALO - 2, axis=0)[:tm]
            dh = (cur * wc_ref[2:3, cs] + d1 * wc_ref[1:2, cs] + d2 * wc_ref[0:1, cs]).astype(BF16)
            dh_ref[:, cs] = dh
            du2 = du2 + _dot(dh, wu_ref[cs, :])
            hv = h_ref[:, cs].astype(F32)
            dwc_ref[0:1, cs] += jnp.sum(hv * d2, axis=0, keepdims=True)
            dwc_ref[1:2, cs] += jnp.sum(hv * d1, axis=0, keepdims=True)
            dwc_ref[2:3, cs] += jnp.sum(hv * cur, axis=0, keepdims=True)
        x1 = x1_ref[...]
        r = lax.rsqrt(jnp.mean(x1 * x1, axis=-1, keepdims=True) + EPS)
        xn = x1 * r
        dgf_ref[...] += jnp.sum(du2 * xn, axis=0, keepdims=True)
        dxn = du2 * gf_ref[...]
        dx1 = dx2_ref[...] + r * (dxn - xn * jnp.mean(dxn * xn, axis=-1, keepdims=True))
        dx1_ref[...] = dx1
        dx1b_ref[...] = dx1.astype(BF16)

    row = lambda w: pl.BlockSpec((tm, w), lambda i: (i, 0))
    halo = pl.BlockSpec((HALO, n), lambda i: (jnp.minimum((i + 1) * (tm // HALO), last), 0))
    return _pcall(body, "ffn_bwd_up", (t // tm,),
                  [row(n), halo, row(n), row(D_MODEL), row(D_MODEL), _full((CONV_W, n)), _full((n, D_MODEL)),
                   _full((1, D_MODEL))],
                  [row(n), row(D_MODEL), row(D_MODEL), _full((1, D_MODEL)), _full((CONV_W, n))],
                  [_sds((t, n), BF16), _sds((t, D_MODEL)), _sds((t, D_MODEL), BF16), _sds((1, D_MODEL)),
                   _sds((CONV_W, n))],
                  )(dhc, dhc, h, dx2, x1, w_conv, w_up, g_ffn)


def _mix_bwd(dx1, y0, o, zh, zgt, pa, pb, w_glu, b_glu, gain, w_pa, w_pb, w_out, tm):
    t = dx1.shape[0]

    def body(dx1_ref, y0_ref, o_ref, zg_ref, zgt_ref, pa_ref, pb_ref, wglu_ref, bglu_ref, gain_ref, wpa_ref,
             wpb_ref, wout_ref,
             dy0_ref, do_ref, dzg_ref, dzgt_ref, m_ref, dpa_ref, dpb_ref, ya1_ref, dpre_ref, dbglu_ref, dgain_ref):
        @pl.when(pl.program_id(0) == 0)
        def _():
            dbglu_ref[...] = jnp.zeros_like(dbglu_ref)
            dgain_ref[...] = jnp.zeros_like(dgain_ref)

        dm = _dot_nt(dx1_ref[...], wout_ref[...])
        sga = _sigmoid(zgt_ref[:, 0:D_MODEL].astype(F32))
        sgb = _sigmoid(zgt_ref[:, D_MODEL:].astype(F32))
        pa = pa_ref[...].astype(F32)
        pb = pb_ref[...].astype(F32)
        m_ref[...] = (sga * pa + sgb * pb).astype(BF16)
        dzgt_ref[:, 0:D_MODEL] = (dm * pa * sga * (1.0 - sga)).astype(BF16)
        dzgt_ref[:, D_MODEL:] = (dm * pb * sgb * (1.0 - sgb)).astype(BF16)
        dpa = (dm * sga).astype(BF16)
        dpb = (dm * sgb).astype(BF16)
        dpa_ref[...] = dpa
        dpb_ref[...] = dpb
        dya2 = _dot_nt(dpa, wpa_ref[...])
        dyb = _dot_nt(dpb, wpb_ref[...])
        y0 = y0_ref[...]
        ya1 = _gelu(y0)
        ya1_ref[...] = ya1.astype(BF16)
        s = _sigmoid(_dot(ya1, wglu_ref[...]) + bglu_ref[...])
        dpre = dya2 * ya1 * s * (1.0 - s)
        dpre_ref[...] = dpre.astype(BF16)
        dbglu_ref[...] += jnp.sum(dpre, axis=0, keepdims=True)
        dya1 = dya2 * s + _dot_nt(dpre, wglu_ref[...])
        dy0_ref[...] = dya1 * _gelu_grad(y0)
        ov = o_ref[...]
        zg = zg_ref[...]
        oh = ov * _head_rms(ov)
        on = oh * gain_ref[...]
        sz = _sigmoid(zg)
        dzg_ref[...] = (dyb * on * (sz * (1.0 + zg * (1.0 - sz)))).astype(BF16)
        don = dyb * (zg * sz)
        dgain_ref[...] += jnp.sum(don * oh, axis=0, keepdims=True)
        doh = don * gain_ref[...]
        do_ref[...] = _head_rms(ov) * (doh - oh * _head_mean(doh * oh))

    row = lambda w: pl.BlockSpec((tm, w), lambda i: (i, 0))
    return _pcall(body, "mix_bwd", (t // tm,),
                  [row(D_MODEL), row(S5_WIDTH), row(HG_WIDTH), pl.BlockSpec((tm, HG_WIDTH), lambda i: (i, 3)),
                   row(2 * D_MODEL), row(D_MODEL), row(D_MODEL), _full((S5_WIDTH, S5_WIDTH)), _full((1, S5_WIDTH)),
                   _full((1, HG_WIDTH)), _full((S5_WIDTH, D_MODEL)), _full((HG_WIDTH, D_MODEL)),
                   _full((D_MODEL, D_MODEL))],
                  [row(S5_WIDTH), row(HG_WIDTH), row(HG_WIDTH), row(2 * D_MODEL), row(D_MODEL), row(D_MODEL),
                   row(D_MODEL), row(S5_WIDTH), row(S5_WIDTH), _full((1, S5_WIDTH)), _full((1, HG_WIDTH))],
                  [_sds((t, S5_WIDTH)), _sds((t, HG_WIDTH)), _sds((t, HG_WIDTH), BF16), _sds((t, 2 * D_MODEL), BF16),
                   _sds((t, D_MODEL), BF16), _sds((t, D_MODEL), BF16), _sds((t, D_MODEL), BF16),
                   _sds((t, S5_WIDTH), BF16), _sds((t, S5_WIDTH), BF16), _sds((1, S5_WIDTH)), _sds((1, HG_WIDTH))],
                  )(dx1, y0, o, zh, zgt, pa, pb, w_glu, b_glu, gain, w_pa, w_pb, w_out)


def _s5_bwd(dy0, za, xs, c_bands, b_bands, lam, dskip, nb, seq, ts):
    nts = seq // ts

    def body(dy0_ref, za_ref, xs_ref, halo_ref, cr_ref, ci_ref, br_ref, bi_ref, lam_ref, d_ref,
             dza_ref, a_ref, dlam_ref, dd_ref, acc_ref, st_ref):
        j = pl.program_id(0)

        @pl.when(j == 0)
        def _():
            dlam_ref[...] = jnp.zeros_like(dlam_ref)
            dd_ref[...] = jnp.zeros_like(dd_ref)
            st_ref[...] = jnp.zeros_like(st_ref)

        for b in range(nb):
            dy0 = dy0_ref[b]
            for q in range(S5_BANDS):
                ch, st = _band(q)
                acc_ref[b, :, st] = _dot(dy0[:, ch], cr_ref[q])
                acc_ref[b, :, _im(st)] = _dot(dy0[:, ch], ci_ref[q])
        _complex_scan(acc_ref, lam_ref, st_ref, nb, ts, reverse=True)
        shift = _shift_matrix(ts)
        top = lax.broadcasted_iota(jnp.int32, (SUBLANES, S5_LANES), 0) == 0
        for b in range(nb):
            a_ref[b] = acc_ref[b].astype(BF16)
            first = jnp.where(j == nts - 1, 0.0, halo_ref[b, HALO - 1:HALO, :].astype(F32))

            def shifted(cols):
                xp = jnp.dot(shift, xs_ref[b, :, cols], preferred_element_type=F32)
                return jnp.concatenate([xp[:SUBLANES] + jnp.where(top, first[:, cols], 0.0), xp[SUBLANES:]], axis=0)

            for cc in range(S5_N // S5_LANES):
                re = slice(cc * S5_LANES, (cc + 1) * S5_LANES)
                ar, ai, xr, xi = acc_ref[b, :, re], acc_ref[b, :, _im(re)], shifted(re), shifted(_im(re))
                dlam_ref[0:1, re] += jnp.sum(ar * xr + ai * xi, axis=0, keepdims=True)
                dlam_ref[1:2, re] += jnp.sum(ai * xr - ar * xi, axis=0, keepdims=True)
            dy0 = dy0_ref[b]
            for q in range(S5_BANDS):
                ch, st = _band(q)
                dza_ref[b, :, ch] = (_dot(a_ref[b, :, st], br_ref[q]) + _dot(a_ref[b, :, _im(st)], bi_ref[q])
                                     + d_ref[:, ch] * dy0[:, ch]).astype(BF16)
            dd_ref[...] += jnp.sum(dy0 * za_ref[b], axis=0, keepdims=True)

    tile = lambda j: nts - 1 - j
    tok = lambda w: pl.BlockSpec((nb, ts, w), lambda j: (0, tile(j), 0))
    halo = pl.BlockSpec((nb, HALO, 2 * S5_N), lambda j: (0, jnp.maximum(tile(j) * (ts // HALO) - 1, 0), 0))
    to_st, to_ch = _full((S5_BANDS, BAND_CH, BAND_ST)), _full((S5_BANDS, BAND_ST, BAND_CH))
    return _pcall(body, "s5_bwd", (nts,),
                  [tok(S5_WIDTH), tok(S5_WIDTH), tok(2 * S5_N), halo, to_st, to_st, to_ch, to_ch,
                   _full((2, S5_N)), _full((1, S5_WIDTH))],
                  [tok(S5_WIDTH), tok(2 * S5_N), _full((2, S5_N)), _full((1, S5_WIDTH))],
                  [_sds((nb, seq, S5_WIDTH), BF16), _sds((nb, seq, 2 * S5_N), BF16), _sds((2, S5_N)),
                   _sds((1, S5_WIDTH))],
                  scratch=[pltpu.VMEM((nb, ts, 2 * S5_N), F32), pltpu.VMEM((nb, 2, S5_N), F32)],
                  )(dy0, za, xs, xs, *c_bands, *b_bands, lam, dskip)


def _hgrn_bwd(zh, do, sts, lb, nb, seq):
    nc = seq // CHUNK

    def body(zh_ref, do_ref, sts_ref, lb_ref, dz_ref, dlb_ref, dst_ref):
        @pl.when(pl.program_id(0) == 0)
        def _():
            dst_ref[...] = jnp.zeros_like(dst_ref)
            dlb_ref[...] = jnp.zeros_like(dlb_ref)

        row = lax.broadcasted_iota(jnp.int32, (CHUNK, CHUNK), 0)
        causal = row >= lax.broadcasted_iota(jnp.int32, (CHUNK, CHUNK), 1)
        last_row = lax.broadcasted_iota(jnp.int32, (CHUNK, HG_HEAD), 0) == CHUNK - 1
        for b in range(nb):
            for h in range(HG_HEADS):
                hs = slice(h * HG_HEAD, (h + 1) * HG_HEAD)
                zq = zh_ref[b, :, h * HG_HEAD:(h + 1) * HG_HEAD]
                zf = zh_ref[b, :, HG_WIDTH + h * HG_HEAD:HG_WIDTH + (h + 1) * HG_HEAD]
                zi = zh_ref[b, :, 2 * HG_WIDTH + h * HG_HEAD:2 * HG_WIDTH + (h + 1) * HG_HEAD]
                lbh = lb_ref[:, hs]
                sf, f, sq, qa, bc, bm, bl = _hgrn_gates(zq, zf, lbh)
                k = 1.0 - f
                e_qt = jnp.exp(bc - bm)
                e_kt = jnp.exp(bm - bc)
                e_b = jnp.exp(bc)
                e_kd = jnp.exp(bl - bc)
                e_l = jnp.exp(bl)
                qt, kt, qb, kd = qa * e_qt, k * e_kt, qa * e_b, k * e_kd
                a = jnp.where(causal, _dot_nt(qt, kt), 0.0)
                st = sts_ref[b, 0, h]
                dst = dst_ref[b, h]
                dov = do_ref[b, :, hs]
                da = jnp.where(causal, _dot_nt(dov, zi), 0.0)
                dqt = _hdot(da, kt)
                dkt = _hdot_tn(da, qt)
                dqb = _dot(dov, st)
                di = _dot_tn(a, dov) + _dot_nt(kd, dst)
                dkd = _dot(zi, dst)
                de_l = jnp.sum(dst * st, axis=0, keepdims=True)
                dst_ref[b, h] = dst * e_l + _dot_tn(dov, qb)
                dqa = dqt * e_qt + dqb * e_b
                dk = dkt * e_kt + dkd * e_kd
                dbl = jnp.sum(dkd * kd, axis=0, keepdims=True) + de_l * e_l
                db = dqt * qt - dkt * kt + dqb * qb - dkd * kd + jnp.where(last_row, dbl, 0.0)
                df = _cumsum_rows(db, reverse=True) / f - dk
                dzq = dqa * QSCALE * (sq * (1.0 + zq * (1.0 - sq)))
                dzf = df * (1.0 - lbh) * sf * (1.0 - sf)
                dz_ref[b, :, h * HG_HEAD:(h + 1) * HG_HEAD] = dzq.astype(BF16)
                dz_ref[b, :, HG_WIDTH + h * HG_HEAD:HG_WIDTH + (h + 1) * HG_HEAD] = dzf.astype(BF16)
                dz_ref[b, :, 2 * HG_WIDTH + h * HG_HEAD:2 * HG_WIDTH + (h + 1) * HG_HEAD] = di.astype(BF16)
                dlb_ref[:, hs] += jnp.sum(df * (1.0 - sf), axis=0, keepdims=True)

    rev = lambda c: nc - 1 - c
    return _pcall(body, "hgrn_bwd", (nc,),
                  [pl.BlockSpec((nb, CHUNK, 4 * HG_WIDTH), lambda c: (0, rev(c), 0)),
                   pl.BlockSpec((nb, CHUNK, HG_WIDTH), lambda c: (0, rev(c), 0)),
                   pl.BlockSpec((nb, 1, HG_HEADS, HG_HEAD, HG_HEAD), lambda c: (0, rev(c), 0, 0, 0)),
                   _full((1, HG_WIDTH))],
                  [pl.BlockSpec((nb, CHUNK, 3 * HG_WIDTH), lambda c: (0, rev(c), 0)), _full((1, HG_WIDTH))],
                  [_sds((nb, seq, 3 * HG_WIDTH), BF16), _sds((1, HG_WIDTH))],
                  scratch=[pltpu.VMEM((nb, HG_HEADS, HG_HEAD, HG_HEAD), F32)])(zh, do, sts, lb)


def _in_proj_bwd(dza, dzh, dzg, dzgt, dx1, x, g_mix, w_in, tm):
    t = x.shape[0]

    def body(dza_ref, dzh_ref, dzg_ref, dzgt_ref, dx1_ref, x_ref, g_ref, w_ref, dz_ref, dx_ref, dg_ref):
        @pl.when(pl.program_id(0) == 0)
        def _():
            dg_ref[...] = jnp.zeros_like(dg_ref)

        c1, c2, c3 = S5_WIDTH, S5_WIDTH + 3 * HG_WIDTH, S5_WIDTH + 4 * HG_WIDTH
        dz_ref[:, 0:c1] = dza_ref[...]
        dz_ref[:, c1:c2] = dzh_ref[...]
        dz_ref[:, c2:c3] = dzg_ref[...]
        dz_ref[:, c3:] = dzgt_ref[...]
        du = _dot(dz_ref[...], w_ref[...])
        xv = x_ref[...]
        r = lax.rsqrt(jnp.mean(xv * xv, axis=-1, keepdims=True) + EPS)
        xn = xv * r
        dg_ref[...] += jnp.sum(du * xn, axis=0, keepdims=True)
        dxn = du * g_ref[...]
        dx_ref[...] = dx1_ref[...] + r * (dxn - xn * jnp.mean(dxn * xn, axis=-1, keepdims=True))

    row = lambda w: pl.BlockSpec((tm, w), lambda i: (i, 0))
    return _pcall(body, "in_proj_bwd", (t // tm,),
                  [row(S5_WIDTH), row(3 * HG_WIDTH), row(HG_WIDTH), row(2 * D_MODEL), row(D_MODEL), row(D_MODEL),
                   _full((1, D_MODEL)), _full((N_IN, D_MODEL))],
                  [row(N_IN), row(D_MODEL), _full((1, D_MODEL))],
                  [_sds((t, N_IN), BF16), _sds((t, D_MODEL)), _sds((1, D_MODEL))],
                  )(dza, dzh, dzg, dzgt, dx1, x, g_mix, w_in)


def _after(value, token):
    return value + token[0, 0]


def _local_step(x3, tgt3, weights, sp, emit, emit_small):
    nb, seq, _ = x3.shape
    t = nb * seq
    tm = _token_tile(seq)
    x = x3.reshape(t, D_MODEL)
    tgt = tgt3.reshape(t, D_MODEL)
    row = lambda v: v.reshape(1, -1)

    a_re, a_im, b_re, b_im = sp["s5_a_re"], sp["s5_a_im"], sp["s5_b_re"], sp["s5_b_im"]
    ldt = sp["s5_log_dt"].reshape(S5_GROUPS, 1)
    lr, li, bb_re, bb_im, lb = _params_fwd(a_re, a_im, ldt, b_re, b_im, sp["hg_lb_logits"])
    lam = jnp.concatenate([lr.reshape(1, S5_N), li.reshape(1, S5_N)], axis=0)
    swap = lambda m: m.transpose(0, 2, 1)
    b_to_st = (_band_blocks(bb_re), _band_blocks(bb_im))
    b_to_ch = (_band_blocks(swap(bb_re)), _band_blocks(swap(bb_im)))
    c_to_ch = (_band_blocks(swap(sp["s5_c_re"])), _band_blocks(swap(-sp["s5_c_im"])))
    c_to_st = (_band_blocks(sp["s5_c_re"]), _band_blocks(-sp["s5_c_im"]))

    g_mix, g_ffn, g_final = row(sp["g_mix"]), row(sp["g_ffn"]), row(sp["g_final"])
    b_glu, gain, dskip, b_conv = row(sp["b_glu"]), row(sp["hg_norm_gain"]), row(sp["s5_d"]), row(sp["b_conv"])

    w_in = weights("in", lam, *b_to_st, *b_to_ch, *c_to_ch, *c_to_st)["w_in"]
    wide = min(2 * tm, seq)
    u, za, zh, zgt = _in_proj(x, g_mix, w_in, wide)
    seqs = lambda v: v.reshape(nb, seq, v.shape[-1])
    toks = lambda v: v.reshape(t, v.shape[-1])
    xs3, y0 = _s5_fwd(seqs(za), b_to_st, lam, c_to_ch, dskip, nb, seq, tm)
    xs, y0 = toks(xs3), toks(y0)
    o3, sts = _hgrn_fwd(zh.reshape(nb, seq, 4 * HG_WIDTH), lb, nb, seq)
    o = o3.reshape(t, HG_WIDTH)
    wm = weights("mix", y0, o3)
    weights.forward("ffn", wm["w_out"])
    x1, u2, pa, pb, ya2, yb = _mix_fwd(x, y0, o, zh, zgt, wm["w_glu"], b_glu, gain, wm["w_pa"], wm["w_pb"],
                                       wm["w_out"], g_ffn, wide)
    wf = weights("ffn", u2)
    h = _ffn_up(u2, wf["w_up"], min(4 * tm, t))
    hc, a, dx2, dx2b, loss, dg_final = _ffn_down_loss(h, x1, tgt, wf["w_conv"], b_conv, wf["w_down"], g_final,
                                                      seq, tm)

    def wgrad(a, b, name):
        return _wgrad(a, b, name, 512 if a.shape[1] % 512 == 0 else 256, out_dtype=BF16)

    dhc, db_conv = _ffn_bwd_act(dx2b, hc, wf["w_down"], wide)
    dw_down = wgrad(a, dx2b, "dw_down")
    dh, dx1, dx1b, dg_ffn, dw_conv = _ffn_bwd_up(dhc, h, dx2, x1, wf["w_conv"], wf["w_up"], g_ffn, seq, tm)
    sent = emit({"w_up": wgrad(dh, u2, "dw_up"), "w_conv": dw_conv, "w_down": dw_down})
    (dy0, do, dzg, dzgt, m, dpa, dpb, ya1, dpre, db_glu, dgain) = _mix_bwd(
        dx1b, y0, o, zh, zgt, pa, pb, wm["w_glu"], _after(b_glu, sent), gain, wm["w_pa"], wm["w_pb"], wm["w_out"],
        wide)
    sent = emit({"w_out": wgrad(m, dx1b, "dw_out"), "w_pa": wgrad(ya2, dpa, "dw_pa"),
                 "w_pb": wgrad(yb, dpb, "dw_pb"), "w_glu": wgrad(ya1, dpre, "dw_glu")})
    dzh3, dlb = _hgrn_bwd(zh.reshape(nb, seq, 4 * HG_WIDTH), do.reshape(nb, seq, HG_WIDTH), sts, _after(lb, sent),
                          nb, seq)
    dza, a_s5, dlam, dd = _s5_bwd(seqs(dy0), seqs(za), xs3, c_to_st, b_to_ch, lam, dskip, nb, seq, tm)
    dza, a_s5 = toks(dza), toks(a_s5)
    dz, dx, dg_mix = _in_proj_bwd(dza, dzh3.reshape(t, 3 * HG_WIDTH), dzg, dzgt, dx1, x, g_mix, w_in, wide)
    sent = emit({"w_in": wgrad(dz, u, "dw_in")})

    band = HG_HEAD
    dbb_band = _wgrad(a_s5, za, "dbb_s5", 512, band=band, after=sent)
    dc_band = _wgrad(xs, dy0, "dc_s5", 512, band=band, after=sent)
    dbb_re = swap(_diag_blocks(dbb_band[:S5_N], S5_STATE, S5_GROUP))
    dbb_im = swap(_diag_blocks(dbb_band[S5_N:], S5_STATE, S5_GROUP))
    dc_re = swap(_diag_blocks(dc_band[:S5_N], S5_STATE, S5_GROUP))
    dc_im = -swap(_diag_blocks(dc_band[S5_N:], S5_STATE, S5_GROUP))
    da_re, da_im, dldt, db_re, db_im, dlogits = _params_bwd(
        a_re, a_im, ldt, b_re, b_im, sp["hg_lb_logits"],
        dlam[0].reshape(S5_GROUPS, S5_STATE), dlam[1].reshape(S5_GROUPS, S5_STATE), dbb_re, dbb_im, dlb)
    emit_small({"g_mix": dg_mix, "s5_a_re": da_re, "s5_a_im": da_im, "s5_log_dt": dldt.reshape(1, S5_GROUPS),
                "s5_b_re": db_re, "s5_b_im": db_im, "s5_c_re": dc_re, "s5_c_im": dc_im, "s5_d": dd, "b_glu": db_glu,
                "hg_lb_logits": dlogits, "hg_norm_gain": dgain, "g_ffn": dg_ffn, "b_conv": db_conv,
                "g_final": dg_final, "loss": loss})
    return dx.reshape(nb, seq, D_MODEL)


def _mesh_peers():
    x, y, c = lax.axis_index("x"), lax.axis_index("y"), lax.axis_index("c")
    peers = []
    for k in range(1, N_DEV):
        px, py, pc = (1 - x if k & 4 else x), (1 - y if k & 2 else y), (1 - c if k & 1 else c)
        peers.append((k, (px, py, pc), 4 * px + 2 * py + pc))
    return 4 * x + 2 * y + c, peers


_HBM = pl.BlockSpec(memory_space=pltpu.HBM)
_SEM = pl.BlockSpec(memory_space=pltpu.SEMAPHORE)


_EFFECT = pltpu.CompilerParams(has_side_effects=pltpu.SideEffectType.DATAFLOW_SIDE_EFFECTING)


def _remote(src, dst, send_sem, recv_sem, to):
    return pltpu.make_async_remote_copy(src_ref=src, dst_ref=dst, send_sem=send_sem, recv_sem=recv_sem,
                                        device_id=to, device_id_type=pl.DeviceIdType.MESH)


def _exchange_start(name, arrays, after):
    n = len(arrays)
    srcs = [pltpu.with_memory_space_constraint(a, pltpu.HBM) for a in arrays]
    lands = [pltpu.with_memory_space_constraint(lax.empty(a.shape, a.dtype), pltpu.HBM) for a in arrays]
    copies = (N_DEV - 1) * n

    def body(*refs):
        src_refs, land_refs = refs[:n], refs[n:2 * n]
        send_sems, recv_sems, token = refs[2 * n + 1], refs[2 * n + 2], refs[-1]
        my_slab, peers = _mesh_peers()
        for k, peer, slab in peers:
            for i in range(n):
                s = (k - 1) * n + i
                _remote(src_refs[i].at[slab], land_refs[i].at[my_slab], send_sems.at[s], recv_sems.at[s], peer).start()
        token[...] = jnp.zeros_like(token)

    outs = pl.pallas_call(
        body, name=name,
        out_shape=(pltpu.SemaphoreType.DMA((copies,)), pltpu.SemaphoreType.DMA((copies,)),
                   *[pltpu.HBM(a.shape, a.dtype) for a in lands], _sds((SUBLANES, LANES))),
        in_specs=[_HBM] * (2 * n) + [pl.BlockSpec(memory_space=pl.ANY)],
        out_specs=(_SEM, _SEM, *[_HBM] * n, pl.BlockSpec(memory_space=pltpu.VMEM)),
        input_output_aliases={n + i: 2 + i for i in range(n)}, compiler_params=_EFFECT,
    )(*srcs, *lands, after)
    return (outs[0], outs[1], srcs, outs[2:2 + n]), outs[-1]


def _exchange_wait(name, state, *after):
    send_sems, recv_sems, srcs, lands = state
    n = len(lands)

    def body(*refs):
        src_refs, land_refs = refs[:n], refs[n:2 * n]
        send_ref, recv_ref = refs[2 * n], refs[2 * n + 1]
        _, peers = _mesh_peers()
        for k, peer, slab in peers:
            for i in range(n):
                s = (k - 1) * n + i
                copy = _remote(src_refs[i].at[slab], land_refs[i].at[slab], send_ref.at[s], recv_ref.at[s], peer)
                copy.wait_send()
                copy.wait_recv()

    outs = pl.pallas_call(
        body, name=name,
        out_shape=tuple(pltpu.HBM(a.shape, a.dtype) for a in lands),
        in_specs=[_HBM] * (2 * n) + [_SEM, _SEM] + [pl.BlockSpec(memory_space=pl.ANY)] * len(after),
        out_specs=tuple([_HBM] * n),
        input_output_aliases={n + i: i for i in range(n)}, compiler_params=_EFFECT,
    )(*srcs, *lands, send_sems, recv_sems, *after)
    return list(outs), list(srcs)


def _slab(pos):
    return 4 * pos[0] + 2 * pos[1] + pos[2]


def _chip_routes():
    x, y, c = lax.axis_index("x"), lax.axis_index("y"), lax.axis_index("c")
    return (x, y, c), (x, y, 1 - c), [(1 - x, y, c), (x, 1 - y, c), (1 - x, 1 - y, c)]


def _gather_start(name, arrays, after):
    n = len(arrays)
    me = 4 * lax.axis_index("x") + 2 * lax.axis_index("y") + lax.axis_index("c")
    srcs = [pltpu.with_memory_space_constraint(a, pltpu.HBM) for a in arrays]
    lands = [pltpu.with_memory_space_constraint(
        lax.dynamic_update_slice_in_dim(lax.empty((N_DEV,) + a.shape, a.dtype), a[None], me, 0), pltpu.HBM)
        for a in arrays]

    def body(*refs):
        src_refs, land_refs = refs[:n], refs[n:2 * n]
        send_sems, recv_sems, token = refs[2 * n + 1], refs[2 * n + 2], refs[-1]
        mine, sibling, chips = _chip_routes()
        for k, to in enumerate([sibling] + chips):
            for i in range(n):
                _remote(src_refs[i], land_refs[i].at[_slab(mine)], send_sems.at[k * n + i], recv_sems.at[k * n + i],
                        to).start()
        token[...] = jnp.zeros_like(token)

    outs = pl.pallas_call(
        body, name=name,
        out_shape=(pltpu.SemaphoreType.DMA((4 * n,)), pltpu.SemaphoreType.DMA((4 * n,)),
                   *[pltpu.HBM(a.shape, a.dtype) for a in lands], _sds((SUBLANES, LANES))),
        in_specs=[_HBM] * (2 * n) + [pl.BlockSpec(memory_space=pl.ANY)],
        out_specs=(_SEM, _SEM, *[_HBM] * n, pl.BlockSpec(memory_space=pltpu.VMEM)),
        input_output_aliases={n + i: 2 + i for i in range(n)}, compiler_params=_EFFECT,
    )(*srcs, *lands, after)
    return (outs[0], outs[1], srcs, outs[2:2 + n]), outs[-1]


def _gather_forward(name, state, *after):
    send_a, recv_a, srcs, lands = state
    n = len(lands)

    def body(*refs):
        land_refs, recv_a_ref = refs[:n], refs[n]
        send_b, recv_b = refs[n + 1 + len(after)], refs[n + 2 + len(after)]
        mine, sibling, chips = _chip_routes()
        for j, chip in enumerate(chips):
            for i in range(n):
                block = land_refs[i].at[_slab(chip)]
                _remote(block, block, send_b.at[j * n + i], recv_a_ref.at[(1 + j) * n + i], chip).wait_recv()
                _remote(block, block, send_b.at[j * n + i], recv_b.at[j * n + i], sibling).start()

    outs = pl.pallas_call(
        body, name=name,
        out_shape=(pltpu.SemaphoreType.DMA((3 * n,)), pltpu.SemaphoreType.DMA((3 * n,)),
                   *[pltpu.HBM(a.shape, a.dtype) for a in lands]),
        in_specs=[_HBM] * n + [_SEM] + [pl.BlockSpec(memory_space=pl.ANY)] * len(after),
        out_specs=(_SEM, _SEM, *[_HBM] * n),
        input_output_aliases={i: 2 + i for i in range(n)}, compiler_params=_EFFECT,
    )(*lands, recv_a, *after)
    return (send_a, recv_a, srcs, list(outs[2:])), (outs[0], outs[1])


def _gather_wait(name, state, forwarded, *after):
    send_a, recv_a, srcs, lands = state
    send_b, recv_b = forwarded
    n = len(lands)

    def body(*refs):
        src_refs, land_refs = refs[:n], refs[n:2 * n]
        sa, ra, sb, rb = refs[2 * n:2 * n + 4]
        mine, sibling, chips = _chip_routes()
        for i in range(n):
            for k, to in enumerate([sibling] + chips):
                _remote(src_refs[i], land_refs[i].at[_slab(mine)], sa.at[k * n + i], ra.at[k * n + i], to).wait_send()
            theirs = land_refs[i].at[_slab(sibling)]
            _remote(theirs, theirs, sa.at[i], ra.at[i], sibling).wait_recv()
            for j, chip in enumerate(chips):
                sent = land_refs[i].at[_slab(chip)]
                got = land_refs[i].at[_slab((chip[0], chip[1], sibling[2]))]
                _remote(sent, sent, sb.at[j * n + i], rb.at[j * n + i], sibling).wait_send()
                _remote(got, got, sb.at[j * n + i], rb.at[j * n + i], sibling).wait_recv()

    outs = pl.pallas_call(
        body, name=name,
        out_shape=tuple(pltpu.HBM(a.shape, a.dtype) for a in lands),
        in_specs=[_HBM] * (2 * n) + [_SEM] * 4 + [pl.BlockSpec(memory_space=pl.ANY)] * len(after),
        out_specs=tuple([_HBM] * n),
        input_output_aliases={n + i: i for i in range(n)}, compiler_params=_EFFECT,
    )(*srcs, *lands, send_a, recv_a, send_b, recv_b, *after)
    return list(outs), list(srcs)


def _join_cols(parts, name, tr):
    _, r, c = parts.shape

    def body(p_ref, o_ref):
        for j in range(N_DEV):
            o_ref[:, j * c:(j + 1) * c] = p_ref[j]

    return _pcall(body, name, (r // tr,), [pl.BlockSpec((N_DEV, tr, c), lambda i: (0, i, 0))],
                  pl.BlockSpec((tr, N_DEV * c), lambda i: (i, 0)), _sds((r, N_DEV * c), parts.dtype))(parts)


def _split_cols(full, name, tr):
    r, c = full.shape[0], full.shape[1] // N_DEV

    def body(f_ref, o_ref):
        for j in range(N_DEV):
            o_ref[j] = f_ref[:, j * c:(j + 1) * c]

    return _pcall(body, name, (r // tr,), [pl.BlockSpec((tr, N_DEV * c), lambda i: (i, 0))],
                  pl.BlockSpec((N_DEV, tr, c), lambda i: (0, i, 0)), _sds((N_DEV, r, c), full.dtype))(full)


def _my_slab():
    return (4 * lax.axis_index("x") + 2 * lax.axis_index("y") + lax.axis_index("c")).astype(jnp.int32).reshape(1)


def _adamw(parts, sent, w, m, v, name, tile):
    _, rows, cols = w.shape

    def body(me_ref, p_ref, s_ref, w_ref, m_ref, v_ref, g_out, d_out, m_out, v_out):
        me = me_ref[0]
        g = jnp.where(me == 0, s_ref[0], p_ref[0]).astype(F32)
        for k in range(1, N_DEV):
            g = g + jnp.where(me == k, s_ref[0], p_ref[k]).astype(F32)
        m1 = ADAM_B1 * m_ref[0] + (1.0 - ADAM_B1) * g
        v1 = ADAM_B2 * v_ref[0] + (1.0 - ADAM_B2) * (g * g)
        m_hat = m1 / (1.0 - ADAM_B1 ** ADAM_STEP)
        v_hat = v1 / (1.0 - ADAM_B2 ** ADAM_STEP)
        g_out[0] = g
        d_out[0] = -ADAM_LR * (m_hat / (jnp.sqrt(v_hat) + ADAM_EPS) + ADAM_WD * w_ref[0])
        m_out[0] = m1
        v_out[0] = v1

    row = pl.BlockSpec((1, tile, cols), lambda i, me: (0, i, 0))
    return pl.pallas_call(
        body, name=name, out_shape=[_sds((1, rows, cols))] * 4,
        grid_spec=pltpu.PrefetchScalarGridSpec(
            num_scalar_prefetch=1, grid=(rows // tile,),
            in_specs=[pl.BlockSpec((N_DEV, tile, cols), lambda i, me: (0, i, 0)),
                      pl.BlockSpec((1, tile, cols), lambda i, me: (me[0], i, 0)), row, row, row],
            out_specs=[row, row, row, row]),
        compiler_params=pltpu.CompilerParams(dimension_semantics=("arbitrary",), vmem_limit_bytes=VMEM_LIMIT),
    )(_my_slab(), parts, sent, w, m, v)


BIG = {
    "w_in": ((N_IN // N_DEV, D_MODEL), False, N_IN // N_DEV // 3),
    "w_glu": ((S5_WIDTH // N_DEV, S5_WIDTH), False, S5_WIDTH // N_DEV),
    "w_pa": ((S5_WIDTH, D_MODEL // N_DEV), True, S5_WIDTH),
    "w_pb": ((HG_WIDTH, D_MODEL // N_DEV), True, HG_WIDTH),
    "w_out": ((D_MODEL // N_DEV, D_MODEL), False, D_MODEL // N_DEV),
    "w_up": ((2 * D_FF // N_DEV, D_MODEL), False, 2 * D_FF // N_DEV // 4),
    "w_conv": ((CONV_W, 2 * D_FF // N_DEV), True, CONV_W),
    "w_down": ((D_FF // N_DEV, D_MODEL), False, D_FF // N_DEV // 2),
}
TRANSPOSED = ("w_in", "w_up", "s5_b_re", "s5_b_im")
UNALIGNED_COLS = ("w_conv",)


def _stored(n, arr):
    return jnp.swapaxes(arr, -1, -2) if n in TRANSPOSED else arr


def _join_shards(n, parts):
    (a, b), by_cols, _ = BIG[n]
    if not by_cols:
        return parts.reshape(N_DEV * a, b)
    if n in UNALIGNED_COLS:
        return _join_cols(parts, "join_" + n, min(a, 256))
    return parts.transpose(1, 0, 2).reshape(a, N_DEV * b)


def _split_shards(n, full):
    (a, b), by_cols, _ = BIG[n]
    if not by_cols:
        return full.reshape(N_DEV, a, b)
    if n in UNALIGNED_COLS:
        return _split_cols(full, "split_" + n, min(a, 256))
    return full.reshape(a, N_DEV, b).transpose(1, 0, 2)


SMALL_CORE = {
    "s5_b_re": GSC, "s5_b_im": GSC, "s5_c_re": GSC, "s5_c_im": GSC,
    "g_mix": (1, D_MODEL), "g_ffn": (1, D_MODEL), "g_final": (1, D_MODEL), "s5_d": (1, S5_WIDTH),
    "b_glu": (1, S5_WIDTH), "hg_norm_gain": (1, HG_WIDTH), "hg_lb_logits": (2, HG_WIDTH), "b_conv": (1, 2 * D_FF),
    "s5_log_dt": (1, S5_GROUPS), "s5_a_re": (S5_GROUPS, S5_STATE), "s5_a_im": (S5_GROUPS, S5_STATE), "loss": (1, 1),
}
BLOCK_ROWS = 32


def _small_rows():
    rows, r = {}, 0
    for n, core in SMALL_CORE.items():
        rows[n] = r
        r += BLOCK_ROWS if len(core) == 3 else -(-math.prod(core) // PACK_W)
    return rows, -(-r // SUBLANES) * SUBLANES


SMALL_ROW, SMALL_ROWS = _small_rows()


def _small_pieces(name):
    r, core = SMALL_ROW[name], SMALL_CORE[name]
    if len(core) == 3:
        return [((g, slice(None), slice(None)), slice(r + S5_GROUP * (g % 2), r + S5_GROUP * (g % 2 + 1)),
                 slice(S5_STATE * (g // 2), S5_STATE * (g // 2 + 1))) for g in range(S5_GROUPS)]
    pieces = []
    for i in range(core[0]):
        for c0 in range(0, core[1], PACK_W):
            w, flat = min(PACK_W, core[1] - c0), i * core[1] + c0
            pieces.append(((slice(i, i + 1), slice(c0, c0 + w)), slice(r + flat // PACK_W, r + flat // PACK_W + 1),
                           slice(flat % PACK_W, flat % PACK_W + w)))
    return pieces


def _core_index(ref, name, idx):
    return (0,) * (len(ref.shape) - len(SMALL_CORE[name])) + idx


def _pack_small_grads(grads):
    names = list(SMALL_CORE)

    def body(*refs):
        pack = refs[-1]
        pack[...] = jnp.zeros_like(pack)
        for ref, n in zip(refs, names):
            for idx, rows, lanes in _small_pieces(n):
                pack[rows, lanes] = ref[_core_index(ref, n, idx)]

    return _pcall(body, "pack_small_grads", (1,), [_full(grads[n].shape) for n in names],
                  _full((SMALL_ROWS, PACK_W)), _sds((SMALL_ROWS, PACK_W)))(*[grads[n] for n in names])


def _adamw_small(parts, sent, names, rows, given, name):
    lo, hi = rows
    k = len(names)
    shapes = [given[n].shape for n in names]

    def body(*refs):
        me, p_ref, s_ref, ins, outs = refs[0][0], refs[1], refs[2], refs[3:3 + 3 * k], refs[3 + 3 * k:3 + 7 * k]
        packs, results = refs[3 + 7 * k:6 + 7 * k], refs[6 + 7 * k:]
        for j, pack in enumerate(packs):
            pack[...] = jnp.zeros_like(pack)
            for ref, n in zip(ins[j * k:(j + 1) * k], names):
                for idx, prow, lanes in _small_pieces(n):
                    pack[slice(prow.start - lo, prow.stop - lo), lanes] = ref[_core_index(ref, n, idx)]
        mine = s_ref[lo:hi, :]
        g = jnp.where(me == 0, mine, p_ref[0, lo:hi, :])
        for d in range(1, N_DEV):
            g = g + jnp.where(me == d, mine, p_ref[d, lo:hi, :])
        m1 = ADAM_B1 * packs[1][...] + (1.0 - ADAM_B1) * g
        v1 = ADAM_B2 * packs[2][...] + (1.0 - ADAM_B2) * (g * g)
        m_hat = m1 / (1.0 - ADAM_B1 ** ADAM_STEP)
        v_hat = v1 / (1.0 - ADAM_B2 ** ADAM_STEP)
        results[0][...] = g
        results[1][...] = -ADAM_LR * (m_hat / (jnp.sqrt(v_hat) + ADAM_EPS) + ADAM_WD * packs[0][...])
        results[2][...] = m1
        results[3][...] = v1
        for j, result in enumerate(results):
            for ref, n in zip(outs[j * k:(j + 1) * k], names):
                for idx, prow, lanes in _small_pieces(n):
                    ref[_core_index(ref, n, idx)] = result[slice(prow.start - lo, prow.stop - lo), lanes]

    flat = _pcall(body, name, (1,),
                  [pl.BlockSpec(memory_space=pltpu.SMEM), _full(parts.shape), _full(sent.shape)]
                  + [_full(s) for s in shapes] * 3,
                  [_full(s) for s in shapes] * 4, [_sds(s) for s in shapes] * 4,
                  scratch=[pltpu.VMEM((hi - lo, PACK_W), F32)] * 7,
                  )(_my_slab(), parts, sent, *[given[pre + n] for pre in ("", "m_", "v_") for n in names])
    return {n: [flat[j * k + i] for j in range(4)] for i, n in enumerate(names)}


def kernel(x, g_mix, w_in, s5_a_re, s5_a_im, s5_log_dt, s5_b_re, s5_b_im, s5_c_re, s5_c_im, s5_d, w_glu, b_glu, hg_lb_logits, hg_norm_gain, w_pa, w_pb, w_out, g_ffn, w_up, w_conv, b_conv, w_down, g_final, loss_target, m_g_mix, m_w_in, m_s5_a_re, m_s5_a_im, m_s5_log_dt, m_s5_b_re, m_s5_b_im, m_s5_c_re, m_s5_c_im, m_s5_d, m_w_glu, m_b_glu, m_hg_lb_logits, m_hg_norm_gain, m_w_pa, m_w_pb, m_w_out, m_g_ffn, m_w_up, m_w_conv, m_b_conv, m_w_down, m_g_final, v_g_mix, v_w_in, v_s5_a_re, v_s5_a_im, v_s5_log_dt, v_s5_b_re, v_s5_b_im, v_s5_c_re, v_s5_c_im, v_s5_d, v_w_glu, v_b_glu, v_hg_lb_logits, v_hg_norm_gain, v_w_pa, v_w_pb, v_w_out, v_g_ffn, v_w_up, v_w_conv, v_b_conv, v_w_down, v_g_final):
    given = dict(locals())
    small_names = [n for n in SMALL_CORE if n != "loss"]

    pay = {n: given[n][0] if n == "w_conv" else _stored(n, given[n])[0].astype(BF16) for n in BIG}
    groups = {"in": ["w_in"], "mix": ["w_glu", "w_pa", "w_pb", "w_out"], "ffn": ["w_up", "w_down", "w_conv"]}
    gathers, order = {}, pay["w_in"]
    for grp, names in groups.items():
        gathers[grp], order = _gather_start("gather_" + grp + "_start", [pay[n] for n in names], order)

    forwards = {}

    def forward(grp, *after):
        if grp == "in":
            after = (*after, order)
        forwards[grp] = _gather_forward("gather_" + grp + "_forward", gathers[grp], *after)

    def weights(grp, *after):
        if grp not in forwards:
            forward(grp, *after)
        got, _ = _gather_wait("gather_" + grp + "_wait", *forwards[grp], *after)
        return {n: _join_shards(n, g) for n, g in zip(groups[grp], got)}

    weights.forward = forward

    in_flight, started = [], []

    def emit(grads):
        names = list(grads)
        state, token = _exchange_start("grads_" + names[0] + "_start", [_split_shards(n, grads[n]) for n in names],
                                       grads[names[0]])
        in_flight.append((names, state))
        return token

    def emit_small(grads):
        pack = _pack_small_grads(grads)
        state, token = _gather_start("grads_small_start", [pack], pack)
        in_flight.append((["small"], state))
        started.append(token)

    sp = {n: (given[n] if n in ("g_final", "hg_lb_logits") else _stored(n, given[n])[0]) for n in small_names}
    sp["g_mix"] = _after(sp["g_mix"], order)
    dx = _local_step(x, loss_target, weights, sp, emit, emit_small)

    res = {}
    after = [started[-1]]
    for names, state in in_flight:
        if names == ["small"]:
            state, forwarded = _gather_forward("grads_small_forward", state, *after)
            parts, sent = _gather_wait("grads_small_wait", state, forwarded)
        else:
            parts, sent = _exchange_wait("grads_" + names[0] + "_wait", state, *after)
        if names != ["small"]:
            after = []
            for n, part, mine in zip(names, parts, sent):
                raw = _adamw(part, mine, *[_stored(n, given[pre + n]) for pre in ("", "m_", "v_")], "adamw_" + n,
                             BIG[n][2])
                res[n] = [_stored(n, r) for r in raw]
                after.append(raw[0])
            continue
        sgiven = {pre + n: _stored(n, given[pre + n]) for pre in ("", "m_", "v_") for n in small_names}
        for pre in ("", "m_", "v_"):
            sgiven[pre + "g_final"] = given[pre + "g_final"].reshape(1, D_MODEL)
            sgiven[pre + "loss"] = jnp.zeros((1, 1), F32)
        raw = _adamw_small(parts[0], sent[0], list(SMALL_CORE), (0, SMALL_ROWS), sgiven, "adamw_small")
        res.update({n: [_stored(n, r) for r in raw[n]] for n in small_names})
        res["g_final"] = [r.reshape(D_MODEL) for r in raw["g_final"]]
        total_loss = raw["loss"][0].reshape(())
        after = [raw["s5_b_re"][0], raw["g_mix"][0]]
    return (total_loss, dx, *[res[n][0] for n in WEIGHT_ORDER], *[res[n][1] for n in WEIGHT_ORDER],
            *[res[n][2] for n in WEIGHT_ORDER], *[res[n][3] for n in WEIGHT_ORDER])
```

```python
import math

import jax
import jax.numpy as jnp
from jax import lax
from jax.experimental import pallas as pl
from jax.experimental.pallas import tpu as pltpu

F32 = jnp.float32
BF16 = jnp.bfloat16

D_MODEL = 1024
S5_WIDTH = 512
S5_GROUP = 16
S5_GROUPS = 32
S5_STATE = 64
S5_N = S5_GROUPS * S5_STATE
HG_WIDTH = 512
HG_HEAD = 128
HG_HEADS = 4
D_FF = 2816
CONV_W = 3
CHUNK = 64
N_IN = S5_WIDTH + 4 * HG_WIDTH + 2 * D_MODEL
EPS = 1e-6
QSCALE = HG_HEAD ** -0.5

ADAM_LR = 0.001
ADAM_B1 = 0.9
ADAM_B2 = 0.999
ADAM_EPS = 1e-08
ADAM_WD = 0.01
ADAM_STEP = 10

N_DEV = 8
V7X_VMEM_BYTES = 64 * 1024 * 1024
VMEM_LIMIT = V7X_VMEM_BYTES * 7 // 8
SUBLANES = 8
LANES = 128
PACK_W = 1024

WEIGHT_ORDER = ("g_mix", "w_in", "s5_a_re", "s5_a_im", "s5_log_dt", "s5_b_re", "s5_b_im", "s5_c_re", "s5_c_im",
                "s5_d", "w_glu", "b_glu", "hg_lb_logits", "hg_norm_gain", "w_pa", "w_pb", "w_out", "g_ffn",
                "w_up", "w_conv", "b_conv", "w_down", "g_final")


def _pcall(body, name, grid, in_specs, out_specs, out_shape, scratch=()):
    return pl.pallas_call(
        body, name=name, grid=grid, in_specs=in_specs, out_specs=out_specs, out_shape=out_shape,
        scratch_shapes=list(scratch),
        compiler_params=pltpu.CompilerParams(dimension_semantics=("arbitrary",) * len(grid),
                                             vmem_limit_bytes=VMEM_LIMIT),
    )


def _full(shape):
    return pl.BlockSpec(shape, lambda *_: (0,) * len(shape))


def _sds(shape, dtype=F32):
    return jax.ShapeDtypeStruct(shape, dtype)


def _dot(a, b):
    return jnp.dot(a.astype(BF16), b.astype(BF16), preferred_element_type=F32)


def _dot_nt(a, b):
    return lax.dot_general(a.astype(BF16), b.astype(BF16), (((1,), (1,)), ((), ())), preferred_element_type=F32)


def _dot_tn(a, b):
    return lax.dot_general(a.astype(BF16), b.astype(BF16), (((0,), (0,)), ((), ())), preferred_element_type=F32)


def _sigmoid(x):
    return jax.nn.sigmoid(x)


GELU_C = math.sqrt(2.0 / math.pi)
GELU_A = 0.044715


def _gelu(x):
    return 0.5 * x * (1.0 + jnp.tanh(GELU_C * (x + GELU_A * (x * x * x))))


def _gelu_grad(x):
    t = jnp.tanh(GELU_C * (x + GELU_A * (x * x * x)))
    return 0.5 * (1.0 + t) + 0.5 * x * (1.0 - t * t) * (GELU_C * (1.0 + 3.0 * GELU_A * x * x))


def _cumsum_rows(v, reverse=False):
    n = v.shape[0]
    row = lax.broadcasted_iota(jnp.int32, v.shape, 0)
    s = 1
    while s < n:
        if reverse:
            v = v + jnp.where(row < n - s, pltpu.roll(v, n - s, axis=0), 0.0)
        else:
            v = v + jnp.where(row >= s, pltpu.roll(v, s, axis=0), 0.0)
        s *= 2
    return v


def _token_tile(seq):
    return min(256, seq)


def _s5_coeffs(a_re, a_im, ldt):
    dt = jnp.exp(ldt)
    mag = jnp.exp(a_re * dt)
    ang = a_im * dt
    lb_re = mag * jnp.cos(ang)
    lb_im = mag * jnp.sin(ang)
    den = a_re * a_re + a_im * a_im
    n_re = lb_re - 1.0
    n_im = lb_im
    co_re = (n_re * a_re + n_im * a_im) / den
    co_im = (n_im * a_re - n_re * a_im) / den
    return lb_re, lb_im, co_re, co_im


GS, GSC = (S5_GROUPS, S5_STATE), (S5_GROUPS, S5_GROUP, S5_STATE)


def _params_fwd(a_re, a_im, ldt, bt_re, bt_im, logits):
    def body(are, aim, ld, bre, bim, lg, lr_o, li_o, bbr_o, bbi_o, lb_o):
        lr, li, co_re, co_im = _s5_coeffs(are[...], aim[...], ld[...])
        lr_o[...] = lr
        li_o[...] = li
        for g in range(S5_GROUPS):
            cr, ci = co_re[g:g + 1, :], co_im[g:g + 1, :]
            bbr_o[g] = cr * bre[g] - ci * bim[g]
            bbi_o[g] = cr * bim[g] + ci * bre[g]
        lb_o[...] = _sigmoid(lg[0:1, :] - lg[1:2, :])

    return _pcall(body, "params_fwd", (1,),
                  [_full(GS), _full(GS), _full((S5_GROUPS, 1)), _full(GSC), _full(GSC), _full((2, HG_WIDTH))],
                  [_full(GS), _full(GS), _full(GSC), _full(GSC), _full((1, HG_WIDTH))],
                  [_sds(GS), _sds(GS), _sds(GSC), _sds(GSC), _sds((1, HG_WIDTH))],
                  )(a_re, a_im, ldt, bt_re, bt_im, logits)


def _params_bwd(a_re, a_im, ldt, bt_re, bt_im, logits, dlr, dli, dbbr, dbbi, dlb):
    def body(are, aim, ld, bre, bim, lg, dlr_r, dli_r, dbbr_r, dbbi_r, dlb_r,
             dare_o, daim_o, dld_o, dbre_o, dbim_o, dlg_o, dcr_ref, dci_ref):
        (_, _, co_re, co_im), vjp = jax.vjp(_s5_coeffs, are[...], aim[...], ld[...])
        for g in range(S5_GROUPS):
            cr, ci = co_re[g:g + 1, :], co_im[g:g + 1, :]
            gr, gi, br, bi = dbbr_r[g], dbbi_r[g], bre[g], bim[g]
            dbre_o[g] = cr * gr + ci * gi
            dbim_o[g] = cr * gi - ci * gr
            dcr_ref[g:g + 1, :] = jnp.sum(gr * br + gi * bi, axis=0, keepdims=True)
            dci_ref[g:g + 1, :] = jnp.sum(gi * br - gr * bi, axis=0, keepdims=True)
        dare, daim, dld = vjp((dlr_r[...], dli_r[...], dcr_ref[...], dci_ref[...]))
        dare_o[...] = dare
        daim_o[...] = daim
        dld_o[...] = dld
        lb = _sigmoid(lg[0:1, :] - lg[1:2, :])
        d0 = dlb_r[...] * lb * (1.0 - lb)
        dlg_o[0:1, :] = d0
        dlg_o[1:2, :] = -d0

    return _pcall(body, "params_bwd", (1,),
                  [_full(GS), _full(GS), _full((S5_GROUPS, 1)), _full(GSC), _full(GSC), _full((2, HG_WIDTH)),
                   _full(GS), _full(GS), _full(GSC), _full(GSC), _full((1, HG_WIDTH))],
                  [_full(GS), _full(GS), _full((S5_GROUPS, 1)), _full(GSC), _full(GSC), _full((2, HG_WIDTH))],
                  [_sds(GS), _sds(GS), _sds((S5_GROUPS, 1)), _sds(GSC), _sds(GSC), _sds((2, HG_WIDTH))],
                  scratch=[pltpu.VMEM(GS, F32), pltpu.VMEM(GS, F32)],
                  )(a_re, a_im, ldt, bt_re, bt_im, logits, dlr, dli, dbbr, dbbi, dlb)


def _band_blocks(m):
    g, r, c = m.shape
    gb = g // S5_BANDS
    m4 = m.astype(BF16).reshape(S5_BANDS, gb, r, c)
    on_diag = jnp.eye(gb, dtype=bool)[None, :, None, :, None]
    return jnp.where(on_diag, m4[:, :, :, None, :], 0).reshape(S5_BANDS, gb * r, gb * c)


def _diag_blocks(band, r, c):
    g, nb = band.shape[0] // r, band.shape[1] // c
    on_diag = (jnp.arange(g) % nb)[:, None, None, None] == jnp.arange(nb)[None, None, :, None]
    return jnp.sum(jnp.where(on_diag, band.reshape(g, r, nb, c), 0.0), axis=2)


def _in_proj(x, g_mix, w_in, tm):
    t = x.shape[0]

    def body(x_ref, g_ref, w_ref, u_ref, za_ref, zh_ref, zg_ref):
        xv = x_ref[...]
        r = lax.rsqrt(jnp.mean(xv * xv, axis=-1, keepdims=True) + EPS)
        u = (xv * r * g_ref[...]).astype(BF16)
        u_ref[...] = u
        za_ref[...] = _dot_nt(u, w_ref[0:S5_WIDTH, :])
        zh_ref[...] = _dot_nt(u, w_ref[S5_WIDTH:S5_WIDTH + 4 * HG_WIDTH, :])
        zg_ref[...] = _dot_nt(u, w_ref[S5_WIDTH + 4 * HG_WIDTH:, :]).astype(BF16)

    row = lambda w: pl.BlockSpec((tm, w), lambda i: (i, 0))
    return _pcall(body, "in_proj", (t // tm,),
                  [row(D_MODEL), _full((1, D_MODEL)), _full((N_IN, D_MODEL))],
                  [row(D_MODEL), row(S5_WIDTH), row(4 * HG_WIDTH), row(2 * D_MODEL)],
                  [_sds((t, D_MODEL), BF16), _sds((t, S5_WIDTH)), _sds((t, 4 * HG_WIDTH)),
                   _sds((t, 2 * D_MODEL), BF16)],
                  )(x, g_mix, w_in)


S5_LANES = 512
S5_BANDS = 4


def _band(q):
    return (slice(q * S5_WIDTH // S5_BANDS, (q + 1) * S5_WIDTH // S5_BANDS),
            slice(q * S5_N // S5_BANDS, (q + 1) * S5_N // S5_BANDS))


def _im(st):
    return slice(S5_N + st.start, S5_N + st.stop)


SCAN_UNROLL = 8


def _complex_scan(buf_ref, lam_ref, st_ref, nb, ts, reverse):
    lanes = [slice(cc * S5_LANES, (cc + 1) * S5_LANES) for cc in range(S5_N // S5_LANES)]
    chains = [(b, re) for b in range(nb) for re in lanes]
    nch = len(chains)
    wr = {re.start: lam_ref[0:1, re] for re in lanes}
    wi = {re.start: -lam_ref[1:2, re] if reverse else lam_ref[1:2, re] for re in lanes}

    def block(ib, carry):
        vr, vi = list(carry[:nch]), list(carry[nch:])
        first = ts - SCAN_UNROLL - ib * SCAN_UNROLL if reverse else ib * SCAN_UNROLL
        first = pl.multiple_of(first, SCAN_UNROLL)
        for k in range(SCAN_UNROLL):
            row = pl.ds(first + (SCAN_UNROLL - 1 - k if reverse else k), 1)
            for c, (b, re) in enumerate(chains):
                nr = wr[re.start] * vr[c] - wi[re.start] * vi[c] + buf_ref[b, row, re]
                ni = wr[re.start] * vi[c] + wi[re.start] * vr[c] + buf_ref[b, row, _im(re)]
                buf_ref[b, row, re] = nr
                buf_ref[b, row, _im(re)] = ni
                vr[c], vi[c] = nr, ni
        return tuple(vr + vi)

    init = tuple(st_ref[b, 0:1, re] for b, re in chains) + tuple(st_ref[b, 1:2, re] for b, re in chains)
    last = lax.fori_loop(0, ts // SCAN_UNROLL, block, init)
    for c, (b, re) in enumerate(chains):
        st_ref[b, 0:1, re] = last[c]
        st_ref[b, 1:2, re] = last[nch + c]


BAND_CH = S5_WIDTH // S5_BANDS
BAND_ST = S5_N // S5_BANDS


def _s5_fwd(za, b_bands, lam, c_bands, dskip, nb, seq, ts):
    nts = seq // ts

    def body(za_ref, br_ref, bi_ref, lam_ref, cr_ref, ci_ref, d_ref, xs_ref, y_ref, buf_ref, st_ref):
        @pl.when(pl.program_id(0) == 0)
        def _():
            st_ref[...] = jnp.zeros_like(st_ref)

        for b in range(nb):
            zav = za_ref[b]
            for q in range(S5_BANDS):
                ch, st = _band(q)
                buf_ref[b, :, st] = _dot(zav[:, ch], br_ref[q])
                buf_ref[b, :, _im(st)] = _dot(zav[:, ch], bi_ref[q])
        _complex_scan(buf_ref, lam_ref, st_ref, nb, ts, reverse=False)
        for b in range(nb):
            zav = za_ref[b]
            xs_ref[b] = buf_ref[b].astype(BF16)
            for q in range(S5_BANDS):
                ch, st = _band(q)
                y_ref[b, :, ch] = (_dot(xs_ref[b, :, st], cr_ref[q]) + _dot(xs_ref[b, :, _im(st)], ci_ref[q])
                                   + d_ref[:, ch] * zav[:, ch])

    tok = lambda w: pl.BlockSpec((nb, ts, w), lambda j: (0, j, 0))
    to_st, to_ch = _full((S5_BANDS, BAND_CH, BAND_ST)), _full((S5_BANDS, BAND_ST, BAND_CH))
    return _pcall(body, "s5_fwd", (nts,),
                  [tok(S5_WIDTH), to_st, to_st, _full((2, S5_N)), to_ch, to_ch, _full((1, S5_WIDTH))],
                  [tok(2 * S5_N), tok(S5_WIDTH)],
                  [_sds((nb, seq, 2 * S5_N), BF16), _sds((nb, seq, S5_WIDTH))],
                  scratch=[pltpu.VMEM((nb, ts, 2 * S5_N), F32), pltpu.VMEM((nb, 2, S5_N), F32)],
                  )(za, *b_bands, lam, *c_bands, dskip)


def _hgrn_gates(zq, zf, lbh):
    sf = _sigmoid(zf)
    f = lbh + (1.0 - lbh) * sf
    sq = _sigmoid(zq)
    qa = zq * sq * QSCALE
    bc = _cumsum_rows(jnp.log(f))
    bm = bc[CHUNK // 2 - 1:CHUNK // 2, :]
    bl = bc[CHUNK - 1:CHUNK, :]
    return sf, f, sq, qa, bc, bm, bl


def _hgrn_fwd(zh, lb, nb, seq):
    nc = seq // CHUNK

    def body(zh_ref, lb_ref, o_ref, sts_ref, st_ref):
        @pl.when(pl.program_id(0) == 0)
        def _():
            st_ref[...] = jnp.zeros_like(st_ref)

        causal = (lax.broadcasted_iota(jnp.int32, (CHUNK, CHUNK), 0)
                  >= lax.broadcasted_iota(jnp.int32, (CHUNK, CHUNK), 1))
        for b in range(nb):
            for h in range(HG_HEADS):
                hs = slice(h * HG_HEAD, (h + 1) * HG_HEAD)
                zq = zh_ref[b, :, h * HG_HEAD:(h + 1) * HG_HEAD]
                zf = zh_ref[b, :, HG_WIDTH + h * HG_HEAD:HG_WIDTH + (h + 1) * HG_HEAD]
                zi = zh_ref[b, :, 2 * HG_WIDTH + h * HG_HEAD:2 * HG_WIDTH + (h + 1) * HG_HEAD]
                _, f, _, qa, bc, bm, bl = _hgrn_gates(zq, zf, lb_ref[:, hs])
                k = 1.0 - f
                qt = qa * jnp.exp(bc - bm)
                kt = k * jnp.exp(bm - bc)
                qb = qa * jnp.exp(bc)
                kd = k * jnp.exp(bl - bc)
                st = st_ref[b, h]
                sts_ref[b, 0, h] = st
                a = jnp.where(causal, _dot_nt(qt, kt), 0.0)
                o_ref[b, :, hs] = _dot(a, zi) + _dot_nt(qb, st)
                st_ref[b, h] = st * jnp.exp(bl) + _dot_tn(zi, kd)

    return _pcall(body, "hgrn_fwd", (nc,),
                  [pl.BlockSpec((nb, CHUNK, 4 * HG_WIDTH), lambda c: (0, c, 0)), _full((1, HG_WIDTH))],
                  [pl.BlockSpec((nb, CHUNK, HG_WIDTH), lambda c: (0, c, 0)),
                   pl.BlockSpec((nb, 1, HG_HEADS, HG_HEAD, HG_HEAD), lambda c: (0, c, 0, 0, 0))],
                  [_sds((nb, seq, HG_WIDTH)), _sds((nb, nc, HG_HEADS, HG_HEAD, HG_HEAD))],
                  scratch=[pltpu.VMEM((nb, HG_HEADS, HG_HEAD, HG_HEAD), F32)])(zh, lb)


def _head_rms(o):
    parts = []
    for h in range(HG_HEADS):
        oh = o[:, h * HG_HEAD:(h + 1) * HG_HEAD]
        r = lax.rsqrt(jnp.mean(oh * oh, axis=-1, keepdims=True) + EPS)
        parts.append(jnp.broadcast_to(r, oh.shape))
    return jnp.concatenate(parts, axis=1)


def _head_mean(v):
    parts = []
    for h in range(HG_HEADS):
        vh = v[:, h * HG_HEAD:(h + 1) * HG_HEAD]
        parts.append(jnp.broadcast_to(jnp.mean(vh, axis=-1, keepdims=True), vh.shape))
    return jnp.concatenate(parts, axis=1)


def _mix_fwd(x, y0, o, zh, zgt, w_glu, b_glu, gain, w_pa, w_pb, w_out, g_ffn, tm):
    t = x.shape[0]

    def body(x_ref, y0_ref, o_ref, zg_ref, zgt_ref, wglu_ref, bglu_ref, gain_ref, wpa_ref, wpb_ref, wout_ref,
             gffn_ref, x1_ref, u2_ref, pa_ref, pb_ref, ya2_ref, yb_ref):
        ya1 = _gelu(y0_ref[...])
        s = _sigmoid(_dot(ya1, wglu_ref[...]) + bglu_ref[...])
        ya2 = (ya1 * s).astype(BF16)
        ov = o_ref[...]
        zg = zg_ref[...]
        yb = (ov * _head_rms(ov) * gain_ref[...] * (zg * _sigmoid(zg))).astype(BF16)
        ya2_ref[...] = ya2
        yb_ref[...] = yb
        pa = jnp.dot(ya2, wpa_ref[...], preferred_element_type=F32)
        pb = jnp.dot(yb, wpb_ref[...], preferred_element_type=F32)
        pa_ref[...] = pa.astype(BF16)
        pb_ref[...] = pb.astype(BF16)
        m = (_sigmoid(zgt_ref[:, 0:D_MODEL].astype(F32)) * pa
             + _sigmoid(zgt_ref[:, D_MODEL:].astype(F32)) * pb)
        x1 = x_ref[...] + _dot(m, wout_ref[...])
        x1_ref[...] = x1
        r = lax.rsqrt(jnp.mean(x1 * x1, axis=-1, keepdims=True) + EPS)
        u2_ref[...] = (x1 * r * gffn_ref[...]).astype(BF16)

    row = lambda w: pl.BlockSpec((tm, w), lambda i: (i, 0))
    return _pcall(body, "mix_fwd", (t // tm,),
                  [row(D_MODEL), row(S5_WIDTH), row(HG_WIDTH), pl.BlockSpec((tm, HG_WIDTH), lambda i: (i, 3)),
                   row(2 * D_MODEL), _full((S5_WIDTH, S5_WIDTH)), _full((1, S5_WIDTH)), _full((1, HG_WIDTH)),
                   _full((S5_WIDTH, D_MODEL)), _full((HG_WIDTH, D_MODEL)), _full((D_MODEL, D_MODEL)),
                   _full((1, D_MODEL))],
                  [row(D_MODEL), row(D_MODEL), row(D_MODEL), row(D_MODEL), row(S5_WIDTH), row(HG_WIDTH)],
                  [_sds((t, D_MODEL)), _sds((t, D_MODEL), BF16), _sds((t, D_MODEL), BF16), _sds((t, D_MODEL), BF16),
                   _sds((t, S5_WIDTH), BF16), _sds((t, HG_WIDTH), BF16)],
                  )(x, y0, o, zh, zgt, w_glu, b_glu, gain, w_pa, w_pb, w_out, g_ffn)


FF_COLS = 256
FF_UP_TILE = 2 * D_FF // 2


def _ffn_up(u2, w_up, tm):
    t = u2.shape[0]
    n = 2 * D_FF

    def body(u_ref, w_ref, h_ref):
        h_ref[...] = _dot_nt(u_ref[...], w_ref[...]).astype(BF16)

    return _pcall(body, "ffn_up", (n // FF_UP_TILE, t // tm),
                  [pl.BlockSpec((tm, D_MODEL), lambda j, i: (i, 0)),
                   pl.BlockSpec((FF_UP_TILE, D_MODEL), lambda j, i: (j, 0))],
                  pl.BlockSpec((tm, FF_UP_TILE), lambda j, i: (i, j)),
                  _sds((t, n), BF16))(u2, w_up)


HALO = 16


def _shift_matrix(tm):
    r = lax.broadcasted_iota(jnp.int32, (tm, tm), 0)
    c = lax.broadcasted_iota(jnp.int32, (tm, tm), 1)
    return jnp.where(r == c + 1, 1.0, 0.0).astype(BF16)


def _conv_cols(h_ref, halo_ref, valid, wc_ref, bc_ref, c0):
    cs = slice(c0, c0 + FF_COLS)
    cur = h_ref[:, cs].astype(F32)
    prev = jnp.where(valid, halo_ref[:, cs].astype(F32), 0.0)
    full = jnp.concatenate([prev, cur], axis=0)
    h1 = pltpu.roll(full, 1, axis=0)[HALO:]
    h2 = pltpu.roll(full, 2, axis=0)[HALO:]
    return h2 * wc_ref[0:1, cs] + h1 * wc_ref[1:2, cs] + cur * wc_ref[2:3, cs] + bc_ref[:, cs]


def _ffn_down_loss(h, x1, tgt, w_conv, b_conv, w_down, g_final, seq, tm):
    t = h.shape[0]
    tps = seq // tm
    n = 2 * D_FF

    def body(h_ref, halo_ref, x1_ref, tgt_ref, wc_ref, bc_ref, wd_ref, gf_ref,
             hc_ref, a_ref, dx2_ref, dx2b_ref, loss_ref, dgf_ref):
        i = pl.program_id(0)

        @pl.when(i == 0)
        def _():
            loss_ref[...] = jnp.zeros_like(loss_ref)
            dgf_ref[...] = jnp.zeros_like(dgf_ref)

        valid = (i % tps) != 0
        x2 = x1_ref[...]
        for j in range(D_FF // FF_COLS):
            gate = _conv_cols(h_ref, halo_ref, valid, wc_ref, bc_ref, j * FF_COLS)
            val = _conv_cols(h_ref, halo_ref, valid, wc_ref, bc_ref, D_FF + j * FF_COLS)
            hc_ref[:, j * FF_COLS:(j + 1) * FF_COLS] = gate.astype(BF16)
            hc_ref[:, D_FF + j * FF_COLS:D_FF + (j + 1) * FF_COLS] = val.astype(BF16)
            a = (gate * _sigmoid(gate) * val).astype(BF16)
            a_ref[:, j * FF_COLS:(j + 1) * FF_COLS] = a
            x2 = x2 + jnp.dot(a, wd_ref[j * FF_COLS:(j + 1) * FF_COLS, :], preferred_element_type=F32)
        r = lax.rsqrt(jnp.mean(x2 * x2, axis=-1, keepdims=True) + EPS)
        xn = x2 * r
        g = gf_ref[...]
        e = xn * g - tgt_ref[...]
        loss_ref[...] += (0.5 / D_MODEL) * jnp.sum(e * e).reshape(1, 1)
        dy = e * (1.0 / D_MODEL)
        dgf_ref[...] += jnp.sum(dy * xn, axis=0, keepdims=True)
        dxn = dy * g
        dx2 = r * (dxn - xn * jnp.mean(dxn * xn, axis=-1, keepdims=True))
        dx2_ref[...] = dx2
        dx2b_ref[...] = dx2.astype(BF16)

    row = lambda w: pl.BlockSpec((tm, w), lambda i: (i, 0))
    halo = pl.BlockSpec((HALO, n), lambda i: (jnp.maximum(i * (tm // HALO) - 1, 0), 0))
    return _pcall(body, "ffn_down_loss", (t // tm,),
                  [row(n), halo, row(D_MODEL), row(D_MODEL), _full((CONV_W, n)), _full((1, n)),
                   _full((D_FF, D_MODEL)), _full((1, D_MODEL))],
                  [row(n), row(D_FF), row(D_MODEL), row(D_MODEL), _full((1, 1)), _full((1, D_MODEL))],
                  [_sds((t, n), BF16), _sds((t, D_FF), BF16), _sds((t, D_MODEL)), _sds((t, D_MODEL), BF16),
                   _sds((1, 1)), _sds((1, D_MODEL))],
                  )(h, h, x1, tgt, w_conv, b_conv, w_down, g_final)


def _wgrad(a, b, name, tn, out_dtype=F32, band=None, after=None):
    t, m = a.shape
    n = b.shape[1] if band is None else band
    nbands = 1 if band is None else b.shape[1] // band
    after = b if after is None else after

    def body(a_ref, b_ref, after_ref, o_ref):
        o_ref[...] = _dot_tn(a_ref[...], b_ref[...]).astype(out_dtype)

    return _pcall(body, name, (m // tn,),
                  [pl.BlockSpec((t, tn), lambda i: (0, i)), pl.BlockSpec((t, n), lambda i: (0, i % nbands)),
                   pl.BlockSpec(memory_space=pl.ANY)],
                  pl.BlockSpec((tn, n), lambda i: (i, 0)), _sds((m, n), out_dtype))(a, b, after)


def _ffn_bwd_act(dx2b, hc, w_down, tm):
    t = hc.shape[0]
    n = 2 * D_FF

    def body(dx2_ref, hc_ref, wd_ref, dhc_ref, dbc_ref):
        @pl.when(pl.program_id(0) == 0)
        def _():
            dbc_ref[...] = jnp.zeros_like(dbc_ref)

        dx2 = dx2_ref[...]
        for j in range(D_FF // FF_COLS):
            gs = slice(j * FF_COLS, (j + 1) * FF_COLS)
            vs = slice(D_FF + j * FF_COLS, D_FF + (j + 1) * FF_COLS)
            gate = hc_ref[:, gs].astype(F32)
            val = hc_ref[:, vs].astype(F32)
            da = _dot_nt(dx2, wd_ref[gs, :])
            sg = _sigmoid(gate)
            dgate = da * val * (sg * (1.0 + gate * (1.0 - sg)))
            dval = da * (gate * sg)
            dhc_ref[:, gs] = dgate.astype(BF16)
            dhc_ref[:, vs] = dval.astype(BF16)
            dbc_ref[:, gs] += jnp.sum(dgate, axis=0, keepdims=True)
            dbc_ref[:, vs] += jnp.sum(dval, axis=0, keepdims=True)

    row = lambda w: pl.BlockSpec((tm, w), lambda i: (i, 0))
    return _pcall(body, "ffn_bwd_act", (t // tm,),
                  [row(D_MODEL), row(n), _full((D_FF, D_MODEL))],
                  [row(n), _full((1, n))],
                  [_sds((t, n), BF16), _sds((1, n))],
                  )(dx2b, hc, w_down)


def _ffn_bwd_up(dhc, h, dx2, x1, w_conv, w_up, g_ffn, seq, tm):
    t = dhc.shape[0]
    tps = seq // tm
    n = 2 * D_FF
    last = t // HALO - 1

    def body(dhc_ref, halo_ref, h_ref, dx2_ref, x1_ref, wc_ref, wu_ref, gf_ref,
             dh_ref, dx1_ref, dx1b_ref, dgf_ref, dwc_ref):
        i = pl.program_id(0)

        @pl.when(i == 0)
        def _():
            dgf_ref[...] = jnp.zeros_like(dgf_ref)
            dwc_ref[...] = jnp.zeros_like(dwc_ref)

        valid = ((i + 1) % tps) != 0
        du2 = jnp.zeros((tm, D_MODEL), F32)
        for j in range(n // FF_COLS):
            cs = slice(j * FF_COLS, (j + 1) * FF_COLS)
            cur = dhc_ref[:, cs].astype(F32)
            nxt = jnp.where(valid, halo_ref[:, cs].astype(F32), 0.0)
            full = jnp.concatenate([cur, nxt], axis=0)
            d1 = pltpu.roll(full, tm + HALO - 1, axis=0)[:tm]
            d2 = pltpu.roll(full, tm + HALO - 2, axis=0)[:tm]
            dh = (cur * wc_ref[2:3, cs] + d1 * wc_ref[1:2, cs] + d2 * wc_ref[0:1, cs]).astype(BF16)
            dh_ref[:, cs] = dh
            du2 = du2 + _dot(dh, wu_ref[cs, :])
            hv = h_ref[:, cs].astype(F32)
            dwc_ref[0:1, cs] += jnp.sum(hv * d2, axis=0, keepdims=True)
            dwc_ref[1:2, cs] += jnp.sum(hv * d1, axis=0, keepdims=True)
            dwc_ref[2:3, cs] += jnp.sum(hv * cur, axis=0, keepdims=True)
        x1 = x1_ref[...]
        r = lax.rsqrt(jnp.mean(x1 * x1, axis=-1, keepdims=True) + EPS)
        xn = x1 * r
        dgf_ref[...] += jnp.sum(du2 * xn, axis=0, keepdims=True)
        dxn = du2 * gf_ref[...]
        dx1 = dx2_ref[...] + r * (dxn - xn * jnp.mean(dxn * xn, axis=-1, keepdims=True))
        dx1_ref[...] = dx1
        dx1b_ref[...] = dx1.astype(BF16)

    row = lambda w: pl.BlockSpec((tm, w), lambda i: (i, 0))
    halo = pl.BlockSpec((HALO, n), lambda i: (jnp.minimum((i + 1) * (tm // HALO), last), 0))
    return _pcall(body, "ffn_bwd_up", (t // tm,),
                  [row(n), halo, row(n), row(D_MODEL), row(D_MODEL), _full((CONV_W, n)), _full((n, D_MODEL)),
                   _full((1, D_MODEL))],
                  [row(n), row(D_MODEL), row(D_MODEL), _full((1, D_MODEL)), _full((CONV_W, n))],
                  [_sds((t, n), BF16), _sds((t, D_MODEL)), _sds((t, D_MODEL), BF16), _sds((1, D_MODEL)),
                   _sds((CONV_W, n))],
                  )(dhc, dhc, h, dx2, x1, w_conv, w_up, g_ffn)


def _mix_bwd(dx1, y0, o, zh, zgt, pa, pb, w_glu, b_glu, gain, w_pa, w_pb, w_out, tm):
    t = dx1.shape[0]

    def body(dx1_ref, y0_ref, o_ref, zg_ref, zgt_ref, pa_ref, pb_ref, wglu_ref, bglu_ref, gain_ref, wpa_ref,
             wpb_ref, wout_ref,
             dy0_ref, do_ref, dzg_ref, dzgt_ref, m_ref, dpa_ref, dpb_ref, ya1_ref, dpre_ref, dbglu_ref, dgain_ref):
        @pl.when(pl.program_id(0) == 0)
        def _():
            dbglu_ref[...] = jnp.zeros_like(dbglu_ref)
            dgain_ref[...] = jnp.zeros_like(dgain_ref)

        dm = _dot_nt(dx1_ref[...], wout_ref[...])
        sga = _sigmoid(zgt_ref[:, 0:D_MODEL].astype(F32))
        sgb = _sigmoid(zgt_ref[:, D_MODEL:].astype(F32))
        pa = pa_ref[...].astype(F32)
        pb = pb_ref[...].astype(F32)
        m_ref[...] = (sga * pa + sgb * pb).astype(BF16)
        dzgt_ref[:, 0:D_MODEL] = (dm * pa * sga * (1.0 - sga)).astype(BF16)
        dzgt_ref[:, D_MODEL:] = (dm * pb * sgb * (1.0 - sgb)).astype(BF16)
        dpa = (dm * sga).astype(BF16)
        dpb = (dm * sgb).astype(BF16)
        dpa_ref[...] = dpa
        dpb_ref[...] = dpb
        dya2 = _dot_nt(dpa, wpa_ref[...])
        dyb = _dot_nt(dpb, wpb_ref[...])
        y0 = y0_ref[...]
        ya1 = _gelu(y0)
        ya1_ref[...] = ya1.astype(BF16)
        s = _sigmoid(_dot(ya1, wglu_ref[...]) + bglu_ref[...])
        dpre = dya2 * ya1 * s * (1.0 - s)
        dpre_ref[...] = dpre.astype(BF16)
        dbglu_ref[...] += jnp.sum(dpre, axis=0, keepdims=True)
        dya1 = dya2 * s + _dot_nt(dpre, wglu_ref[...])
        dy0_ref[...] = dya1 * _gelu_grad(y0)
        ov = o_ref[...]
        zg = zg_ref[...]
        oh = ov * _head_rms(ov)
        on = oh * gain_ref[...]
        sz = _sigmoid(zg)
        dzg_ref[...] = (dyb * on * (sz * (1.0 + zg * (1.0 - sz)))).astype(BF16)
        don = dyb * (zg * sz)
        dgain_ref[...] += jnp.sum(don * oh, axis=0, keepdims=True)
        doh = don * gain_ref[...]
        do_ref[...] = _head_rms(ov) * (doh - oh * _head_mean(doh * oh))

    row = lambda w: pl.BlockSpec((tm, w), lambda i: (i, 0))
    return _pcall(body, "mix_bwd", (t // tm,),
                  [row(D_MODEL), row(S5_WIDTH), row(HG_WIDTH), pl.BlockSpec((tm, HG_WIDTH), lambda i: (i, 3)),
                   row(2 * D_MODEL), row(D_MODEL), row(D_MODEL), _full((S5_WIDTH, S5_WIDTH)), _full((1, S5_WIDTH)),
                   _full((1, HG_WIDTH)), _full((S5_WIDTH, D_MODEL)), _full((HG_WIDTH, D_MODEL)),
                   _full((D_MODEL, D_MODEL))],
                  [row(S5_WIDTH), row(HG_WIDTH), row(HG_WIDTH), row(2 * D_MODEL), row(D_MODEL), row(D_MODEL),
                   row(D_MODEL), row(S5_WIDTH), row(S5_WIDTH), _full((1, S5_WIDTH)), _full((1, HG_WIDTH))],
                  [_sds((t, S5_WIDTH)), _sds((t, HG_WIDTH)), _sds((t, HG_WIDTH), BF16), _sds((t, 2 * D_MODEL), BF16),
                   _sds((t, D_MODEL), BF16), _sds((t, D_MODEL), BF16), _sds((t, D_MODEL), BF16),
                   _sds((t, S5_WIDTH), BF16), _sds((t, S5_WIDTH), BF16), _sds((1, S5_WIDTH)), _sds((1, HG_WIDTH))],
                  )(dx1, y0, o, zh, zgt, pa, pb, w_glu, b_glu, gain, w_pa, w_pb, w_out)


def _s5_bwd(dy0, za, xs, c_bands, b_bands, lam, dskip, nb, seq, ts):
    nts = seq // ts

    def body(dy0_ref, za_ref, xs_ref, halo_ref, cr_ref, ci_ref, br_ref, bi_ref, lam_ref, d_ref,
             dza_ref, a_ref, dlam_ref, dd_ref, acc_ref, st_ref):
        j = pl.program_id(0)

        @pl.when(j == 0)
        def _():
            dlam_ref[...] = jnp.zeros_like(dlam_ref)
            dd_ref[...] = jnp.zeros_like(dd_ref)
            st_ref[...] = jnp.zeros_like(st_ref)

        for b in range(nb):
            dy0 = dy0_ref[b]
            for q in range(S5_BANDS):
                ch, st = _band(q)
                acc_ref[b, :, st] = _dot(dy0[:, ch], cr_ref[q])
                acc_ref[b, :, _im(st)] = _dot(dy0[:, ch], ci_ref[q])
        _complex_scan(acc_ref, lam_ref, st_ref, nb, ts, reverse=True)
        shift = _shift_matrix(ts)
        top = lax.broadcasted_iota(jnp.int32, (SUBLANES, S5_LANES), 0) == 0
        for b in range(nb):
            a_ref[b] = acc_ref[b].astype(BF16)
            first = jnp.where(j == nts - 1, 0.0, halo_ref[b, HALO - 1:HALO, :].astype(F32))

            def shifted(cols):
                xp = jnp.dot(shift, xs_ref[b, :, cols], preferred_element_type=F32)
                return jnp.concatenate([xp[:SUBLANES] + jnp.where(top, first[:, cols], 0.0), xp[SUBLANES:]], axis=0)

            for cc in range(S5_N // S5_LANES):
                re = slice(cc * S5_LANES, (cc + 1) * S5_LANES)
                ar, ai, xr, xi = acc_ref[b, :, re], acc_ref[b, :, _im(re)], shifted(re), shifted(_im(re))
                dlam_ref[0:1, re] += jnp.sum(ar * xr + ai * xi, axis=0, keepdims=True)
                dlam_ref[1:2, re] += jnp.sum(ai * xr - ar * xi, axis=0, keepdims=True)
            dy0 = dy0_ref[b]
            for q in range(S5_BANDS):
                ch, st = _band(q)
                dza_ref[b, :, ch] = (_dot(a_ref[b, :, st], br_ref[q]) + _dot(a_ref[b, :, _im(st)], bi_ref[q])
                                     + d_ref[:, ch] * dy0[:, ch]).astype(BF16)
            dd_ref[...] += jnp.sum(dy0 * za_ref[b], axis=0, keepdims=True)

    tile = lambda j: nts - 1 - j
    tok = lambda w: pl.BlockSpec((nb, ts, w), lambda j: (0, tile(j), 0))
    halo = pl.BlockSpec((nb, HALO, 2 * S5_N), lambda j: (0, jnp.maximum(tile(j) * (ts // HALO) - 1, 0), 0))
    to_st, to_ch = _full((S5_BANDS, BAND_CH, BAND_ST)), _full((S5_BANDS, BAND_ST, BAND_CH))
    return _pcall(body, "s5_bwd", (nts,),
                  [tok(S5_WIDTH), tok(S5_WIDTH), tok(2 * S5_N), halo, to_st, to_st, to_ch, to_ch,
                   _full((2, S5_N)), _full((1, S5_WIDTH))],
                  [tok(S5_WIDTH), tok(2 * S5_N), _full((2, S5_N)), _full((1, S5_WIDTH))],
                  [_sds((nb, seq, S5_WIDTH), BF16), _sds((nb, seq, 2 * S5_N), BF16), _sds((2, S5_N)),
                   _sds((1, S5_WIDTH))],
                  scratch=[pltpu.VMEM((nb, ts, 2 * S5_N), F32), pltpu.VMEM((nb, 2, S5_N), F32)],
                  )(dy0, za, xs, xs, *c_bands, *b_bands, lam, dskip)


def _hgrn_bwd(zh, do, sts, lb, nb, seq):
    nc = seq // CHUNK

    def body(zh_ref, do_ref, sts_ref, lb_ref, dz_ref, dlb_ref, dst_ref):
        @pl.when(pl.program_id(0) == 0)
        def _():
            dst_ref[...] = jnp.zeros_like(dst_ref)
            dlb_ref[...] = jnp.zeros_like(dlb_ref)

        row = lax.broadcasted_iota(jnp.int32, (CHUNK, CHUNK), 0)
        causal = row >= lax.broadcasted_iota(jnp.int32, (CHUNK, CHUNK), 1)
        last_row = lax.broadcasted_iota(jnp.int32, (CHUNK, HG_HEAD), 0) == CHUNK - 1
        for b in range(nb):
            for h in range(HG_HEADS):
                hs = slice(h * HG_HEAD, (h + 1) * HG_HEAD)
                zq = zh_ref[b, :, h * HG_HEAD:(h + 1) * HG_HEAD]
                zf = zh_ref[b, :, HG_WIDTH + h * HG_HEAD:HG_WIDTH + (h + 1) * HG_HEAD]
                zi = zh_ref[b, :, 2 * HG_WIDTH + h * HG_HEAD:2 * HG_WIDTH + (h + 1) * HG_HEAD]
                lbh = lb_ref[:, hs]
                sf, f, sq, qa, bc, bm, bl = _hgrn_gates(zq, zf, lbh)
                k = 1.0 - f
                e_qt = jnp.exp(bc - bm)
                e_kt = jnp.exp(bm - bc)
                e_b = jnp.exp(bc)
                e_kd = jnp.exp(bl - bc)
                e_l = jnp.exp(bl)
                qt, kt, qb, kd = qa * e_qt, k * e_kt, qa * e_b, k * e_kd
                a = jnp.where(causal, _dot_nt(qt, kt), 0.0)
                st = sts_ref[b, 0, h]
                dst = dst_ref[b, h]
                dov = do_ref[b, :, hs]
                da = jnp.where(causal, _dot_nt(dov, zi), 0.0)
                qt_r, kt_r = qt.astype(BF16).astype(F32), kt.astype(BF16).astype(F32)
                dqt = _dot(da, kt)
                dkt = _dot_tn(da, qt)
                dqb = _dot(dov, st)
                di = _dot_tn(a, dov) + _dot_nt(kd, dst)
                dkd = _dot(zi, dst)
                de_l = jnp.sum(dst * st, axis=0, keepdims=True)
                dst_ref[b, h] = dst * e_l + _dot_tn(dov, qb)
                dqa = dqt * e_qt + dqb * e_b
                dk = dkt * e_kt + dkd * e_kd
                dbl = jnp.sum(dkd * kd, axis=0, keepdims=True) + de_l * e_l
                db = dqt * qt_r - dkt * kt_r + dqb * qb - dkd * kd + jnp.where(last_row, dbl, 0.0)
                df = _cumsum_rows(db, reverse=True) / f - dk
                dzq = dqa * QSCALE * (sq * (1.0 + zq * (1.0 - sq)))
                dzf = df * (1.0 - lbh) * sf * (1.0 - sf)
                dz_ref[b, :, h * HG_HEAD:(h + 1) * HG_HEAD] = dzq.astype(BF16)
                dz_ref[b, :, HG_WIDTH + h * HG_HEAD:HG_WIDTH + (h + 1) * HG_HEAD] = dzf.astype(BF16)
                dz_ref[b, :, 2 * HG_WIDTH + h * HG_HEAD:2 * HG_WIDTH + (h + 1) * HG_HEAD] = di.astype(BF16)
                dlb_ref[:, hs] += jnp.sum(df * (1.0 - sf), axis=0, keepdims=True)

    rev = lambda c: nc - 1 - c
    return _pcall(body, "hgrn_bwd", (nc,),
                  [pl.BlockSpec((nb, CHUNK, 4 * HG_WIDTH), lambda c: (0, rev(c), 0)),
                   pl.BlockSpec((nb, CHUNK, HG_WIDTH), lambda c: (0, rev(c), 0)),
                   pl.BlockSpec((nb, 1, HG_HEADS, HG_HEAD, HG_HEAD), lambda c: (0, rev(c), 0, 0, 0)),
                   _full((1, HG_WIDTH))],
                  [pl.BlockSpec((nb, CHUNK, 3 * HG_WIDTH), lambda c: (0, rev(c), 0)), _full((1, HG_WIDTH))],
                  [_sds((nb, seq, 3 * HG_WIDTH), BF16), _sds((1, HG_WIDTH))],
                  scratch=[pltpu.VMEM((nb, HG_HEADS, HG_HEAD, HG_HEAD), F32)])(zh, do, sts, lb)


def _in_proj_bwd(dza, dzh, dzg, dzgt, dx1, x, g_mix, w_in, tm):
    t = x.shape[0]

    def body(dza_ref, dzh_ref, dzg_ref, dzgt_ref, dx1_ref, x_ref, g_ref, w_ref, dz_ref, dx_ref, dg_ref):
        @pl.when(pl.program_id(0) == 0)
        def _():
            dg_ref[...] = jnp.zeros_like(dg_ref)

        c1, c2, c3 = S5_WIDTH, S5_WIDTH + 3 * HG_WIDTH, S5_WIDTH + 4 * HG_WIDTH
        dz_ref[:, 0:c1] = dza_ref[...]
        dz_ref[:, c1:c2] = dzh_ref[...]
        dz_ref[:, c2:c3] = dzg_ref[...]
        dz_ref[:, c3:] = dzgt_ref[...]
        du = _dot(dz_ref[...], w_ref[...])
        xv = x_ref[...]
        r = lax.rsqrt(jnp.mean(xv * xv, axis=-1, keepdims=True) + EPS)
        xn = xv * r
        dg_ref[...] += jnp.sum(du * xn, axis=0, keepdims=True)
        dxn = du * g_ref[...]
        dx_ref[...] = dx1_ref[...] + r * (dxn - xn * jnp.mean(dxn * xn, axis=-1, keepdims=True))

    row = lambda w: pl.BlockSpec((tm, w), lambda i: (i, 0))
    return _pcall(body, "in_proj_bwd", (t // tm,),
                  [row(S5_WIDTH), row(3 * HG_WIDTH), row(HG_WIDTH), row(2 * D_MODEL), row(D_MODEL), row(D_MODEL),
                   _full((1, D_MODEL)), _full((N_IN, D_MODEL))],
                  [row(N_IN), row(D_MODEL), _full((1, D_MODEL))],
                  [_sds((t, N_IN), BF16), _sds((t, D_MODEL)), _sds((1, D_MODEL))],
                  )(dza, dzh, dzg, dzgt, dx1, x, g_mix, w_in)


def _after(value, token):
    return value + token[0, 0]


def _local_step(x3, tgt3, weights, sp, emit, emit_small):
    nb, seq, _ = x3.shape
    t = nb * seq
    tm = _token_tile(seq)
    x = x3.reshape(t, D_MODEL)
    tgt = tgt3.reshape(t, D_MODEL)
    row = lambda v: v.reshape(1, -1)

    a_re, a_im, b_re, b_im = sp["s5_a_re"], sp["s5_a_im"], sp["s5_b_re"], sp["s5_b_im"]
    ldt = sp["s5_log_dt"].reshape(S5_GROUPS, 1)
    lr, li, bb_re, bb_im, lb = _params_fwd(a_re, a_im, ldt, b_re, b_im, sp["hg_lb_logits"])
    lam = jnp.concatenate([lr.reshape(1, S5_N), li.reshape(1, S5_N)], axis=0)
    swap = lambda m: m.transpose(0, 2, 1)
    b_to_st = (_band_blocks(bb_re), _band_blocks(bb_im))
    b_to_ch = (_band_blocks(swap(bb_re)), _band_blocks(swap(bb_im)))
    c_to_ch = (_band_blocks(swap(sp["s5_c_re"])), _band_blocks(swap(-sp["s5_c_im"])))
    c_to_st = (_band_blocks(sp["s5_c_re"]), _band_blocks(-sp["s5_c_im"]))

    g_mix, g_ffn, g_final = row(sp["g_mix"]), row(sp["g_ffn"]), row(sp["g_final"])
    b_glu, gain, dskip, b_conv = row(sp["b_glu"]), row(sp["hg_norm_gain"]), row(sp["s5_d"]), row(sp["b_conv"])

    w_in = weights("in", lam, *b_to_st, *b_to_ch, *c_to_ch, *c_to_st)["w_in"]
    wide = min(2 * tm, seq)
    u, za, zh, zgt = _in_proj(x, g_mix, w_in, wide)
    seqs = lambda v: v.reshape(nb, seq, v.shape[-1])
    toks = lambda v: v.reshape(t, v.shape[-1])
    xs3, y0 = _s5_fwd(seqs(za), b_to_st, lam, c_to_ch, dskip, nb, seq, tm)
    xs, y0 = toks(xs3), toks(y0)
    o3, sts = _hgrn_fwd(zh.reshape(nb, seq, 4 * HG_WIDTH), lb, nb, seq)
    o = o3.reshape(t, HG_WIDTH)
    wm = weights("mix", y0, o3)
    weights.forward("ffn", wm["w_out"])
    x1, u2, pa, pb, ya2, yb = _mix_fwd(x, y0, o, zh, zgt, wm["w_glu"], b_glu, gain, wm["w_pa"], wm["w_pb"],
                                       wm["w_out"], g_ffn, wide)
    wf = weights("ffn", u2)
    h = _ffn_up(u2, wf["w_up"], min(4 * tm, t))
    hc, a, dx2, dx2b, loss, dg_final = _ffn_down_loss(h, x1, tgt, wf["w_conv"], b_conv, wf["w_down"], g_final,
                                                      seq, tm)

    def wgrad(a, b, name):
        return _wgrad(a, b, name, 512 if a.shape[1] % 512 == 0 else 256, out_dtype=BF16)

    dhc, db_conv = _ffn_bwd_act(dx2b, hc, wf["w_down"], tm)
    dw_down = wgrad(a, dx2b, "dw_down")
    dh, dx1, dx1b, dg_ffn, dw_conv = _ffn_bwd_up(dhc, h, dx2, x1, wf["w_conv"], wf["w_up"], g_ffn, seq, tm)
    sent = emit({"w_up": wgrad(dh, u2, "dw_up"), "w_conv": dw_conv, "w_down": dw_down})
    (dy0, do, dzg, dzgt, m, dpa, dpb, ya1, dpre, db_glu, dgain) = _mix_bwd(
        dx1b, y0, o, zh, zgt, pa, pb, wm["w_glu"], _after(b_glu, sent), gain, wm["w_pa"], wm["w_pb"], wm["w_out"],
        wide)
    sent = emit({"w_out": wgrad(m, dx1b, "dw_out"), "w_pa": wgrad(ya2, dpa, "dw_pa"),
                 "w_pb": wgrad(yb, dpb, "dw_pb"), "w_glu": wgrad(ya1, dpre, "dw_glu")})
    dzh3, dlb = _hgrn_bwd(zh.reshape(nb, seq, 4 * HG_WIDTH), do.reshape(nb, seq, HG_WIDTH), sts, _after(lb, sent),
                          nb, seq)
    dza, a_s5, dlam, dd = _s5_bwd(seqs(dy0), seqs(za), xs3, c_to_st, b_to_ch, lam, dskip, nb, seq, tm)
    dza, a_s5 = toks(dza), toks(a_s5)
    dz, dx, dg_mix = _in_proj_bwd(dza, dzh3.reshape(t, 3 * HG_WIDTH), dzg, dzgt, dx1, x, g_mix, w_in, wide)
    sent = emit({"w_in": wgrad(dz, u, "dw_in")})

    band = HG_HEAD
    dbb_band = _wgrad(a_s5, za, "dbb_s5", 512, band=band, after=sent)
    dc_band = _wgrad(xs, dy0, "dc_s5", 512, band=band, after=sent)
    dbb_re = swap(_diag_blocks(dbb_band[:S5_N], S5_STATE, S5_GROUP))
    dbb_im = swap(_diag_blocks(dbb_band[S5_N:], S5_STATE, S5_GROUP))
    dc_re = swap(_diag_blocks(dc_band[:S5_N], S5_STATE, S5_GROUP))
    dc_im = -swap(_diag_blocks(dc_band[S5_N:], S5_STATE, S5_GROUP))
    da_re, da_im, dldt, db_re, db_im, dlogits = _params_bwd(
        a_re, a_im, ldt, b_re, b_im, sp["hg_lb_logits"],
        dlam[0].reshape(S5_GROUPS, S5_STATE), dlam[1].reshape(S5_GROUPS, S5_STATE), dbb_re, dbb_im, dlb)
    emit_small({"g_mix": dg_mix, "s5_a_re": da_re, "s5_a_im": da_im, "s5_log_dt": dldt.reshape(1, S5_GROUPS),
                "s5_b_re": db_re, "s5_b_im": db_im, "s5_c_re": dc_re, "s5_c_im": dc_im, "s5_d": dd, "b_glu": db_glu,
                "hg_lb_logits": dlogits, "hg_norm_gain": dgain, "g_ffn": dg_ffn, "b_conv": db_conv,
                "g_final": dg_final, "loss": loss})
    return dx.reshape(nb, seq, D_MODEL)


def _mesh_peers():
    x, y, c = lax.axis_index("x"), lax.axis_index("y"), lax.axis_index("c")
    peers = []
    for k in range(1, N_DEV):
        px, py, pc = (1 - x if k & 4 else x), (1 - y if k & 2 else y), (1 - c if k & 1 else c)
        peers.append((k, (px, py, pc), 4 * px + 2 * py + pc))
    return 4 * x + 2 * y + c, peers


_HBM = pl.BlockSpec(memory_space=pltpu.HBM)
_SEM = pl.BlockSpec(memory_space=pltpu.SEMAPHORE)


_EFFECT = pltpu.CompilerParams(has_side_effects=pltpu.SideEffectType.DATAFLOW_SIDE_EFFECTING)


def _remote(src, dst, send_sem, recv_sem, to):
    return pltpu.make_async_remote_copy(src_ref=src, dst_ref=dst, send_sem=send_sem, recv_sem=recv_sem,
                                        device_id=to, device_id_type=pl.DeviceIdType.MESH)


def _exchange_start(name, arrays, after):
    n = len(arrays)
    srcs = [pltpu.with_memory_space_constraint(a, pltpu.HBM) for a in arrays]
    lands = [pltpu.with_memory_space_constraint(lax.empty(a.shape, a.dtype), pltpu.HBM) for a in arrays]
    copies = (N_DEV - 1) * n

    def body(*refs):
        src_refs, land_refs = refs[:n], refs[n:2 * n]
        send_sems, recv_sems, token = refs[2 * n + 1], refs[2 * n + 2], refs[-1]
        my_slab, peers = _mesh_peers()
        for k, peer, slab in peers:
            for i in range(n):
                s = (k - 1) * n + i
                _remote(src_refs[i].at[slab], land_refs[i].at[my_slab], send_sems.at[s], recv_sems.at[s], peer).start()
        token[...] = jnp.zeros_like(token)

    outs = pl.pallas_call(
        body, name=name,
        out_shape=(pltpu.SemaphoreType.DMA((copies,)), pltpu.SemaphoreType.DMA((copies,)),
                   *[pltpu.HBM(a.shape, a.dtype) for a in lands], _sds((SUBLANES, LANES))),
        in_specs=[_HBM] * (2 * n) + [pl.BlockSpec(memory_space=pl.ANY)],
        out_specs=(_SEM, _SEM, *[_HBM] * n, pl.BlockSpec(memory_space=pltpu.VMEM)),
        input_output_aliases={n + i: 2 + i for i in range(n)}, compiler_params=_EFFECT,
    )(*srcs, *lands, after)
    return (outs[0], outs[1], srcs, outs[2:2 + n]), outs[-1]


def _exchange_wait(name, state, *after):
    send_sems, recv_sems, srcs, lands = state
    n = len(lands)

    def body(*refs):
        src_refs, land_refs = refs[:n], refs[n:2 * n]
        send_ref, recv_ref = refs[2 * n], refs[2 * n + 1]
        _, peers = _mesh_peers()
        for k, peer, slab in peers:
            for i in range(n):
                s = (k - 1) * n + i
                copy = _remote(src_refs[i].at[slab], land_refs[i].at[slab], send_ref.at[s], recv_ref.at[s], peer)
                copy.wait_send()
                copy.wait_recv()

    outs = pl.pallas_call(
        body, name=name,
        out_shape=tuple(pltpu.HBM(a.shape, a.dtype) for a in lands),
        in_specs=[_HBM] * (2 * n) + [_SEM, _SEM] + [pl.BlockSpec(memory_space=pl.ANY)] * len(after),
        out_specs=tuple([_HBM] * n),
        input_output_aliases={n + i: i for i in range(n)}, compiler_params=_EFFECT,
    )(*srcs, *lands, send_sems, recv_sems, *after)
    return list(outs), list(srcs)


def _slab(pos):
    return 4 * pos[0] + 2 * pos[1] + pos[2]


def _chip_routes():
    x, y, c = lax.axis_index("x"), lax.axis_index("y"), lax.axis_index("c")
    return (x, y, c), (x, y, 1 - c), [(1 - x, y, c), (x, 1 - y, c), (1 - x, 1 - y, c)]


def _gather_start(name, arrays, after):
    n = len(arrays)
    me = 4 * lax.axis_index("x") + 2 * lax.axis_index("y") + lax.axis_index("c")
    srcs = [pltpu.with_memory_space_constraint(a, pltpu.HBM) for a in arrays]
    lands = [pltpu.with_memory_space_constraint(
        lax.dynamic_update_slice_in_dim(lax.empty((N_DEV,) + a.shape, a.dtype), a[None], me, 0), pltpu.HBM)
        for a in arrays]

    def body(*refs):
        src_refs, land_refs = refs[:n], refs[n:2 * n]
        send_sems, recv_sems, token = refs[2 * n + 1], refs[2 * n + 2], refs[-1]
        mine, sibling, chips = _chip_routes()
        for k, to in enumerate([sibling] + chips):
            for i in range(n):
                _remote(src_refs[i], land_refs[i].at[_slab(mine)], send_sems.at[k * n + i], recv_sems.at[k * n + i],
                        to).start()
        token[...] = jnp.zeros_like(token)

    outs = pl.pallas_call(
        body, name=name,
        out_shape=(pltpu.SemaphoreType.DMA((4 * n,)), pltpu.SemaphoreType.DMA((4 * n,)),
                   *[pltpu.HBM(a.shape, a.dtype) for a in lands], _sds((SUBLANES, LANES))),
        in_specs=[_HBM] * (2 * n) + [pl.BlockSpec(memory_space=pl.ANY)],
        out_specs=(_SEM, _SEM, *[_HBM] * n, pl.BlockSpec(memory_space=pltpu.VMEM)),
        input_output_aliases={n + i: 2 + i for i in range(n)}, compiler_params=_EFFECT,
    )(*srcs, *lands, after)
    return (outs[0], outs[1], srcs, outs[2:2 + n]), outs[-1]


def _gather_forward(name, state, *after):
    send_a, recv_a, srcs, lands = state
    n = len(lands)

    def body(*refs):
        land_refs, recv_a_ref = refs[:n], refs[n]
        send_b, recv_b = refs[n + 1 + len(after)], refs[n + 2 + len(after)]
        mine, sibling, chips = _chip_routes()
        for j, chip in enumerate(chips):
            for i in range(n):
                block = land_refs[i].at[_slab(chip)]
                _remote(block, block, send_b.at[j * n + i], recv_a_ref.at[(1 + j) * n + i], chip).wait_recv()
                _remote(block, block, send_b.at[j * n + i], recv_b.at[j * n + i], sibling).start()

    outs = pl.pallas_call(
        body, name=name,
        out_shape=(pltpu.SemaphoreType.DMA((3 * n,)), pltpu.SemaphoreType.DMA((3 * n,)),
                   *[pltpu.HBM(a.shape, a.dtype) for a in lands]),
        in_specs=[_HBM] * n + [_SEM] + [pl.BlockSpec(memory_space=pl.ANY)] * len(after),
        out_specs=(_SEM, _SEM, *[_HBM] * n),
        input_output_aliases={i: 2 + i for i in range(n)}, compiler_params=_EFFECT,
    )(*lands, recv_a, *after)
    return (send_a, recv_a, srcs, list(outs[2:])), (outs[0], outs[1])


def _gather_wait(name, state, forwarded, *after):
    send_a, recv_a, srcs, lands = state
    send_b, recv_b = forwarded
    n = len(lands)

    def body(*refs):
        src_refs, land_refs = refs[:n], refs[n:2 * n]
        sa, ra, sb, rb = refs[2 * n:2 * n + 4]
        mine, sibling, chips = _chip_routes()
        for i in range(n):
            for k, to in enumerate([sibling] + chips):
                _remote(src_refs[i], land_refs[i].at[_slab(mine)], sa.at[k * n + i], ra.at[k * n + i], to).wait_send()
            theirs = land_refs[i].at[_slab(sibling)]
            _remote(theirs, theirs, sa.at[i], ra.at[i], sibling).wait_recv()
            for j, chip in enumerate(chips):
                sent = land_refs[i].at[_slab(chip)]
                got = land_refs[i].at[_slab((chip[0], chip[1], sibling[2]))]
                _remote(sent, sent, sb.at[j * n + i], rb.at[j * n + i], sibling).wait_send()
                _remote(got, got, sb.at[j * n + i], rb.at[j * n + i], sibling).wait_recv()

    outs = pl.pallas_call(
        body, name=name,
        out_shape=tuple(pltpu.HBM(a.shape, a.dtype) for a in lands),
        in_specs=[_HBM] * (2 * n) + [_SEM] * 4 + [pl.BlockSpec(memory_space=pl.ANY)] * len(after),
        out_specs=tuple([_HBM] * n),
        input_output_aliases={n + i: i for i in range(n)}, compiler_params=_EFFECT,
    )(*srcs, *lands, send_a, recv_a, send_b, recv_b, *after)
    return list(outs), list(srcs)


def _join_cols(parts, name, tr):
    _, r, c = parts.shape

    def body(p_ref, o_ref):
        for j in range(N_DEV):
            o_ref[:, j * c:(j + 1) * c] = p_ref[j]

    return _pcall(body, name, (r // tr,), [pl.BlockSpec((N_DEV, tr, c), lambda i: (0, i, 0))],
                  pl.BlockSpec((tr, N_DEV * c), lambda i: (i, 0)), _sds((r, N_DEV * c), parts.dtype))(parts)


def _split_cols(full, name, tr):
    r, c = full.shape[0], full.shape[1] // N_DEV

    def body(f_ref, o_ref):
        for j in range(N_DEV):
            o_ref[j] = f_ref[:, j * c:(j + 1) * c]

    return _pcall(body, name, (r // tr,), [pl.BlockSpec((tr, N_DEV * c), lambda i: (i, 0))],
                  pl.BlockSpec((N_DEV, tr, c), lambda i: (0, i, 0)), _sds((N_DEV, r, c), full.dtype))(full)


def _my_slab():
    return (4 * lax.axis_index("x") + 2 * lax.axis_index("y") + lax.axis_index("c")).astype(jnp.int32).reshape(1)


def _adamw(parts, sent, w, m, v, name, tile):
    _, rows, cols = w.shape

    def body(me_ref, p_ref, s_ref, w_ref, m_ref, v_ref, g_out, d_out, m_out, v_out):
        me = me_ref[0]
        g = jnp.where(me == 0, s_ref[0], p_ref[0]).astype(F32)
        for k in range(1, N_DEV):
            g = g + jnp.where(me == k, s_ref[0], p_ref[k]).astype(F32)
        m1 = ADAM_B1 * m_ref[0] + (1.0 - ADAM_B1) * g
        v1 = ADAM_B2 * v_ref[0] + (1.0 - ADAM_B2) * (g * g)
        m_hat = m1 / (1.0 - ADAM_B1 ** ADAM_STEP)
        v_hat = v1 / (1.0 - ADAM_B2 ** ADAM_STEP)
        g_out[0] = g
        d_out[0] = -ADAM_LR * (m_hat / (jnp.sqrt(v_hat) + ADAM_EPS) + ADAM_WD * w_ref[0])
        m_out[0] = m1
        v_out[0] = v1

    row = pl.BlockSpec((1, tile, cols), lambda i, me: (0, i, 0))
    return pl.pallas_call(
        body, name=name, out_shape=[_sds((1, rows, cols))] * 4,
        grid_spec=pltpu.PrefetchScalarGridSpec(
            num_scalar_prefetch=1, grid=(rows // tile,),
            in_specs=[pl.BlockSpec((N_DEV, tile, cols), lambda i, me: (0, i, 0)),
                      pl.BlockSpec((1, tile, cols), lambda i, me: (me[0], i, 0)), row, row, row],
            out_specs=[row, row, row, row]),
        compiler_params=pltpu.CompilerParams(dimension_semantics=("arbitrary",), vmem_limit_bytes=VMEM_LIMIT),
    )(_my_slab(), parts, sent, w, m, v)


BIG = {
    "w_in": ((N_IN // N_DEV, D_MODEL), False, N_IN // N_DEV // 3),
    "w_glu": ((S5_WIDTH // N_DEV, S5_WIDTH), False, S5_WIDTH // N_DEV),
    "w_pa": ((S5_WIDTH, D_MODEL // N_DEV), True, S5_WIDTH),
    "w_pb": ((HG_WIDTH, D_MODEL // N_DEV), True, HG_WIDTH),
    "w_out": ((D_MODEL // N_DEV, D_MODEL), False, D_MODEL // N_DEV),
    "w_up": ((2 * D_FF // N_DEV, D_MODEL), False, 2 * D_FF // N_DEV // 4),
    "w_conv": ((CONV_W, 2 * D_FF // N_DEV), True, CONV_W),
    "w_down": ((D_FF // N_DEV, D_MODEL), False, D_FF // N_DEV // 2),
}
TRANSPOSED = ("w_in", "w_up", "s5_b_re", "s5_b_im")
UNALIGNED_COLS = ("w_conv",)


def _stored(n, arr):
    return jnp.swapaxes(arr, -1, -2) if n in TRANSPOSED else arr


def _join_shards(n, parts):
    (a, b), by_cols, _ = BIG[n]
    if not by_cols:
        return parts.reshape(N_DEV * a, b)
    if n in UNALIGNED_COLS:
        return _join_cols(parts, "join_" + n, min(a, 256))
    return parts.transpose(1, 0, 2).reshape(a, N_DEV * b)


def _split_shards(n, full):
    (a, b), by_cols, _ = BIG[n]
    if not by_cols:
        return full.reshape(N_DEV, a, b)
    if n in UNALIGNED_COLS:
        return _split_cols(full, "split_" + n, min(a, 256))
    return full.reshape(a, N_DEV, b).transpose(1, 0, 2)


SMALL_CORE = {
    "s5_b_re": GSC, "s5_b_im": GSC, "s5_c_re": GSC, "s5_c_im": GSC,
    "g_mix": (1, D_MODEL), "g_ffn": (1, D_MODEL), "g_final": (1, D_MODEL), "s5_d": (1, S5_WIDTH),
    "b_glu": (1, S5_WIDTH), "hg_norm_gain": (1, HG_WIDTH), "hg_lb_logits": (2, HG_WIDTH), "b_conv": (1, 2 * D_FF),
    "s5_log_dt": (1, S5_GROUPS), "s5_a_re": (S5_GROUPS, S5_STATE), "s5_a_im": (S5_GROUPS, S5_STATE), "loss": (1, 1),
}
BLOCK_ROWS = 32


def _small_rows():
    rows, r = {}, 0
    for n, core in SMALL_CORE.items():
        rows[n] = r
        r += BLOCK_ROWS if len(core) == 3 else -(-math.prod(core) // PACK_W)
    return rows, -(-r // SUBLANES) * SUBLANES


SMALL_ROW, SMALL_ROWS = _small_rows()


def _small_pieces(name):
    r, core = SMALL_ROW[name], SMALL_CORE[name]
    if len(core) == 3:
        return [((g, slice(None), slice(None)), slice(r + S5_GROUP * (g % 2), r + S5_GROUP * (g % 2 + 1)),
                 slice(S5_STATE * (g // 2), S5_STATE * (g // 2 + 1))) for g in range(S5_GROUPS)]
    pieces = []
    for i in range(core[0]):
        for c0 in range(0, core[1], PACK_W):
            w, flat = min(PACK_W, core[1] - c0), i * core[1] + c0
            pieces.append(((slice(i, i + 1), slice(c0, c0 + w)), slice(r + flat // PACK_W, r + flat // PACK_W + 1),
                           slice(flat % PACK_W, flat % PACK_W + w)))
    return pieces


def _core_index(ref, name, idx):
    return (0,) * (len(ref.shape) - len(SMALL_CORE[name])) + idx


def _pack_small_grads(grads):
    names = list(SMALL_CORE)

    def body(*refs):
        pack = refs[-1]
        pack[...] = jnp.zeros_like(pack)
        for ref, n in zip(refs, names):
            for idx, rows, lanes in _small_pieces(n):
                pack[rows, lanes] = ref[_core_index(ref, n, idx)]

    return _pcall(body, "pack_small_grads", (1,), [_full(grads[n].shape) for n in names],
                  _full((SMALL_ROWS, PACK_W)), _sds((SMALL_ROWS, PACK_W)))(*[grads[n] for n in names])


def _adamw_small(parts, sent, names, rows, given, name):
    lo, hi = rows
    k = len(names)
    shapes = [given[n].shape for n in names]

    def body(*refs):
        me, p_ref, s_ref, ins, outs = refs[0][0], refs[1], refs[2], refs[3:3 + 3 * k], refs[3 + 3 * k:3 + 7 * k]
        packs, results = refs[3 + 7 * k:6 + 7 * k], refs[6 + 7 * k:]
        for j, pack in enumerate(packs):
            pack[...] = jnp.zeros_like(pack)
            for ref, n in zip(ins[j * k:(j + 1) * k], names):
                for idx, prow, lanes in _small_pieces(n):
                    pack[slice(prow.start - lo, prow.stop - lo), lanes] = ref[_core_index(ref, n, idx)]
        mine = s_ref[lo:hi, :]
        g = jnp.where(me == 0, mine, p_ref[0, lo:hi, :])
        for d in range(1, N_DEV):
            g = g + jnp.where(me == d, mine, p_ref[d, lo:hi, :])
        m1 = ADAM_B1 * packs[1][...] + (1.0 - ADAM_B1) * g
        v1 = ADAM_B2 * packs[2][...] + (1.0 - ADAM_B2) * (g * g)
        m_hat = m1 / (1.0 - ADAM_B1 ** ADAM_STEP)
        v_hat = v1 / (1.0 - ADAM_B2 ** ADAM_STEP)
        results[0][...] = g
        results[1][...] = -ADAM_LR * (m_hat / (jnp.sqrt(v_hat) + ADAM_EPS) + ADAM_WD * packs[0][...])
        results[2][...] = m1
        results[3][...] = v1
        for j, result in enumerate(results):
            for ref, n in zip(outs[j * k:(j + 1) * k], names):
                for idx, prow, lanes in _small_pieces(n):
                    ref[_core_index(ref, n, idx)] = result[slice(prow.start - lo, prow.stop - lo), lanes]

    flat = _pcall(body, name, (1,),
                  [pl.BlockSpec(memory_space=pltpu.SMEM), _full(parts.shape), _full(sent.shape)]
                  + [_full(s) for s in shapes] * 3,
                  [_full(s) for s in shapes] * 4, [_sds(s) for s in shapes] * 4,
                  scratch=[pltpu.VMEM((hi - lo, PACK_W), F32)] * 7,
                  )(_my_slab(), parts, sent, *[given[pre + n] for pre in ("", "m_", "v_") for n in names])
    return {n: [flat[j * k + i] for j in range(4)] for i, n in enumerate(names)}


def kernel(x, g_mix, w_in, s5_a_re, s5_a_im, s5_log_dt, s5_b_re, s5_b_im, s5_c_re, s5_c_im, s5_d, w_glu, b_glu, hg_lb_logits, hg_norm_gain, w_pa, w_pb, w_out, g_ffn, w_up, w_conv, b_conv, w_down, g_final, loss_target, m_g_mix, m_w_in, m_s5_a_re, m_s5_a_im, m_s5_log_dt, m_s5_b_re, m_s5_b_im, m_s5_c_re, m_s5_c_im, m_s5_d, m_w_glu, m_b_glu, m_hg_lb_logits, m_hg_norm_gain, m_w_pa, m_w_pb, m_w_out, m_g_ffn, m_w_up, m_w_conv, m_b_conv, m_w_down, m_g_final, v_g_mix, v_w_in, v_s5_a_re, v_s5_a_im, v_s5_log_dt, v_s5_b_re, v_s5_b_im, v_s5_c_re, v_s5_c_im, v_s5_d, v_w_glu, v_b_glu, v_hg_lb_logits, v_hg_norm_gain, v_w_pa, v_w_pb, v_w_out, v_g_ffn, v_w_up, v_w_conv, v_b_conv, v_w_down, v_g_final):
    given = dict(locals())
    small_names = [n for n in SMALL_CORE if n != "loss"]

    pay = {n: given[n][0] if n == "w_conv" else _stored(n, given[n])[0].astype(BF16) for n in BIG}
    groups = {"in": ["w_in"], "mix": ["w_glu", "w_pa", "w_pb", "w_out"], "ffn": ["w_up", "w_down", "w_conv"]}
    gathers, order = {}, pay["w_in"]
    for grp, names in groups.items():
        gathers[grp], order = _gather_start("gather_" + grp + "_start", [pay[n] for n in names], order)

    forwards = {}

    def forward(grp, *after):
        if grp == "in":
            after = (*after, order)
        forwards[grp] = _gather_forward("gather_" + grp + "_forward", gathers[grp], *after)

    def weights(grp, *after):
        if grp not in forwards:
            forward(grp, *after)
        got, _ = _gather_wait("gather_" + grp + "_wait", *forwards[grp], *after)
        return {n: _join_shards(n, g) for n, g in zip(groups[grp], got)}

    weights.forward = forward

    in_flight, started = [], []

    def emit(grads):
        names = list(grads)
        state, token = _exchange_start("grads_" + names[0] + "_start", [_split_shards(n, grads[n]) for n in names],
                                       grads[names[0]])
        in_flight.append((names, state))
        return token

    def emit_small(grads):
        pack = _pack_small_grads(grads)
        state, token = _gather_start("grads_small_start", [pack], pack)
        in_flight.append((["small"], state))
        started.append(token)

    sp = {n: (given[n] if n in ("g_final", "hg_lb_logits") else _stored(n, given[n])[0]) for n in small_names}
    sp["g_mix"] = _after(sp["g_mix"], order)
    dx = _local_step(x, loss_target, weights, sp, emit, emit_small)

    res = {}
    after = [started[-1]]
    for names, state in in_flight:
        if names == ["small"]:
            state, forwarded = _gather_forward("grads_small_forward", state, *after)
            parts, sent = _gather_wait("grads_small_wait", state, forwarded)
        else:
            parts, sent = _exchange_wait("grads_" + names[0] + "_wait", state, *after)
        if names != ["small"]:
            after = []
            for n, part, mine in zip(names, parts, sent):
                raw = _adamw(part, mine, *[_stored(n, given[pre + n]) for pre in ("", "m_", "v_")], "adamw_" + n,
                             BIG[n][2])
                res[n] = [_stored(n, r) for r in raw]
                after.append(raw[0])
            continue
        sgiven = {pre + n: _stored(n, given[pre + n]) for pre in ("", "m_", "v_") for n in small_names}
        for pre in ("", "m_", "v_"):
            sgiven[pre + "g_final"] = given[pre + "g_final"].reshape(1, D_MODEL)
            sgiven[pre + "loss"] = jnp.zeros((1, 1), F32)
        raw = _adamw_small(parts[0], sent[0], list(SMALL_CORE), (0, SMALL_ROWS), sgiven, "adamw_small")
        res.update({n: [_stored(n, r) for r in raw[n]] for n in small_names})
        res["g_final"] = [r.reshape(D_MODEL) for r in raw["g_final"]]
        total_loss = raw["loss"][0].reshape(())
        after = [raw["s5_b_re"][0], raw["g_mix"][0]]
    return (total_loss, dx, *[res[n][0] for n in WEIGHT_ORDER], *[res[n][1] for n in WEIGHT_ORDER],
            *[res[n][2] for n in WEIGHT_ORDER], *[res[n][3] for n in WEIGHT_ORDER])
```

```python
import math

import jax
import jax.numpy as jnp
from jax import lax
from jax.experimental import pallas as pl
from jax.experimental.pallas import tpu as pltpu

F32 = jnp.float32
BF16 = jnp.bfloat16

D_MODEL = 1024
S5_WIDTH = 512
S5_GROUP = 16
S5_GROUPS = 32
S5_STATE = 64
S5_N = S5_GROUPS * S5_STATE
HG_WIDTH = 512
HG_HEAD = 128
HG_HEADS = 4
D_FF = 2816
CONV_W = 3
CHUNK = 64
N_IN = S5_WIDTH + 4 * HG_WIDTH + 2 * D_MODEL
EPS = 1e-6
QSCALE = HG_HEAD ** -0.5

ADAM_LR = 0.001
ADAM_B1 = 0.9
ADAM_B2 = 0.999
ADAM_EPS = 1e-08
ADAM_WD = 0.01
ADAM_STEP = 10

N_DEV = 8
V7X_VMEM_BYTES = 64 * 1024 * 1024
VMEM_LIMIT = V7X_VMEM_BYTES * 7 // 8
SUBLANES = 8
LANES = 128
PACK_W = 1024

WEIGHT_ORDER = ("g_mix", "w_in", "s5_a_re", "s5_a_im", "s5_log_dt", "s5_b_re", "s5_b_im", "s5_c_re", "s5_c_im",
                "s5_d", "w_glu", "b_glu", "hg_lb_logits", "hg_norm_gain", "w_pa", "w_pb", "w_out", "g_ffn",
                "w_up", "w_conv", "b_conv", "w_down", "g_final")


def _pcall(body, name, grid, in_specs, out_specs, out_shape, scratch=()):
    return pl.pallas_call(
        body, name=name, grid=grid, in_specs=in_specs, out_specs=out_specs, out_shape=out_shape,
        scratch_shapes=list(scratch),
        compiler_params=pltpu.CompilerParams(dimension_semantics=("arbitrary",) * len(grid),
                                             vmem_limit_bytes=VMEM_LIMIT),
    )


def _full(shape):
    return pl.BlockSpec(shape, lambda *_: (0,) * len(shape))


def _sds(shape, dtype=F32):
    return jax.ShapeDtypeStruct(shape, dtype)


def _dot(a, b):
    return jnp.dot(a.astype(BF16), b.astype(BF16), preferred_element_type=F32)


def _dot_nt(a, b):
    return lax.dot_general(a.astype(BF16), b.astype(BF16), (((1,), (1,)), ((), ())), preferred_element_type=F32)


def _dot_tn(a, b):
    return lax.dot_general(a.astype(BF16), b.astype(BF16), (((0,), (0,)), ((), ())), preferred_element_type=F32)


def _sigmoid(x):
    return jax.nn.sigmoid(x)


GELU_C = math.sqrt(2.0 / math.pi)
GELU_A = 0.044715


def _gelu(x):
    return 0.5 * x * (1.0 + jnp.tanh(GELU_C * (x + GELU_A * (x * x * x))))


def _gelu_grad(x):
    t = jnp.tanh(GELU_C * (x + GELU_A * (x * x * x)))
    return 0.5 * (1.0 + t) + 0.5 * x * (1.0 - t * t) * (GELU_C * (1.0 + 3.0 * GELU_A * x * x))


def _cumsum_rows(v, reverse=False):
    n = v.shape[0]
    row = lax.broadcasted_iota(jnp.int32, v.shape, 0)
    s = 1
    while s < n:
        if reverse:
            v = v + jnp.where(row < n - s, pltpu.roll(v, n - s, axis=0), 0.0)
        else:
            v = v + jnp.where(row >= s, pltpu.roll(v, s, axis=0), 0.0)
        s *= 2
    return v


def _token_tile(seq):
    return min(256, seq)


def _s5_coeffs(a_re, a_im, ldt):
    dt = jnp.exp(ldt)
    mag = jnp.exp(a_re * dt)
    ang = a_im * dt
    lb_re = mag * jnp.cos(ang)
    lb_im = mag * jnp.sin(ang)
    den = a_re * a_re + a_im * a_im
    n_re = lb_re - 1.0
    n_im = lb_im
    co_re = (n_re * a_re + n_im * a_im) / den
    co_im = (n_im * a_re - n_re * a_im) / den
    return lb_re, lb_im, co_re, co_im


GS, GSC = (S5_GROUPS, S5_STATE), (S5_GROUPS, S5_GROUP, S5_STATE)


def _params_fwd(a_re, a_im, ldt, bt_re, bt_im, logits):
    def body(are, aim, ld, bre, bim, lg, lr_o, li_o, bbr_o, bbi_o, lb_o):
        lr, li, co_re, co_im = _s5_coeffs(are[...], aim[...], ld[...])
        lr_o[...] = lr
        li_o[...] = li
        for g in range(S5_GROUPS):
            cr, ci = co_re[g:g + 1, :], co_im[g:g + 1, :]
            bbr_o[g] = cr * bre[g] - ci * bim[g]
            bbi_o[g] = cr * bim[g] + ci * bre[g]
        lb_o[...] = _sigmoid(lg[0:1, :] - lg[1:2, :])

    return _pcall(body, "params_fwd", (1,),
                  [_full(GS), _full(GS), _full((S5_GROUPS, 1)), _full(GSC), _full(GSC), _full((2, HG_WIDTH))],
                  [_full(GS), _full(GS), _full(GSC), _full(GSC), _full((1, HG_WIDTH))],
                  [_sds(GS), _sds(GS), _sds(GSC), _sds(GSC), _sds((1, HG_WIDTH))],
                  )(a_re, a_im, ldt, bt_re, bt_im, logits)


def _params_bwd(a_re, a_im, ldt, bt_re, bt_im, logits, dlr, dli, dbbr, dbbi, dlb):
    def body(are, aim, ld, bre, bim, lg, dlr_r, dli_r, dbbr_r, dbbi_r, dlb_r,
             dare_o, daim_o, dld_o, dbre_o, dbim_o, dlg_o, dcr_ref, dci_ref):
        (_, _, co_re, co_im), vjp = jax.vjp(_s5_coeffs, are[...], aim[...], ld[...])
        for g in range(S5_GROUPS):
            cr, ci = co_re[g:g + 1, :], co_im[g:g + 1, :]
            gr, gi, br, bi = dbbr_r[g], dbbi_r[g], bre[g], bim[g]
            dbre_o[g] = cr * gr + ci * gi
            dbim_o[g] = cr * gi - ci * gr
            dcr_ref[g:g + 1, :] = jnp.sum(gr * br + gi * bi, axis=0, keepdims=True)
            dci_ref[g:g + 1, :] = jnp.sum(gi * br - gr * bi, axis=0, keepdims=True)
        dare, daim, dld = vjp((dlr_r[...], dli_r[...], dcr_ref[...], dci_ref[...]))
        dare_o[...] = dare
        daim_o[...] = daim
        dld_o[...] = dld
        lb = _sigmoid(lg[0:1, :] - lg[1:2, :])
        d0 = dlb_r[...] * lb * (1.0 - lb)
        dlg_o[0:1, :] = d0
        dlg_o[1:2, :] = -d0

    return _pcall(body, "params_bwd", (1,),
                  [_full(GS), _full(GS), _full((S5_GROUPS, 1)), _full(GSC), _full(GSC), _full((2, HG_WIDTH)),
                   _full(GS), _full(GS), _full(GSC), _full(GSC), _full((1, HG_WIDTH))],
                  [_full(GS), _full(GS), _full((S5_GROUPS, 1)), _full(GSC), _full(GSC), _full((2, HG_WIDTH))],
                  [_sds(GS), _sds(GS), _sds((S5_GROUPS, 1)), _sds(GSC), _sds(GSC), _sds((2, HG_WIDTH))],
                  scratch=[pltpu.VMEM(GS, F32), pltpu.VMEM(GS, F32)],
                  )(a_re, a_im, ldt, bt_re, bt_im, logits, dlr, dli, dbbr, dbbi, dlb)


def _band_blocks(m):
    g, r, c = m.shape
    gb = g // S5_BANDS
    m4 = m.astype(BF16).reshape(S5_BANDS, gb, r, c)
    on_diag = jnp.eye(gb, dtype=bool)[None, :, None, :, None]
    return jnp.where(on_diag, m4[:, :, :, None, :], 0).reshape(S5_BANDS, gb * r, gb * c)


def _diag_blocks(band, r, c):
    g, nb = band.shape[0] // r, band.shape[1] // c
    on_diag = (jnp.arange(g) % nb)[:, None, None, None] == jnp.arange(nb)[None, None, :, None]
    return jnp.sum(jnp.where(on_diag, band.reshape(g, r, nb, c), 0.0), axis=2)


def _in_proj(x, g_mix, w_in, tm):
    t = x.shape[0]

    def body(x_ref, g_ref, w_ref, u_ref, za_ref, zh_ref, zg_ref):
        xv = x_ref[...]
        r = lax.rsqrt(jnp.mean(xv * xv, axis=-1, keepdims=True) + EPS)
        u = (xv * r * g_ref[...]).astype(BF16)
        u_ref[...] = u
        za_ref[...] = _dot_nt(u, w_ref[0:S5_WIDTH, :])
        zh_ref[...] = _dot_nt(u, w_ref[S5_WIDTH:S5_WIDTH + 4 * HG_WIDTH, :])
        zg_ref[...] = _dot_nt(u, w_ref[S5_WIDTH + 4 * HG_WIDTH:, :]).astype(BF16)

    row = lambda w: pl.BlockSpec((tm, w), lambda i: (i, 0))
    return _pcall(body, "in_proj", (t // tm,),
                  [row(D_MODEL), _full((1, D_MODEL)), _full((N_IN, D_MODEL))],
                  [row(D_MODEL), row(S5_WIDTH), row(4 * HG_WIDTH), row(2 * D_MODEL)],
                  [_sds((t, D_MODEL), BF16), _sds((t, S5_WIDTH)), _sds((t, 4 * HG_WIDTH)),
                   _sds((t, 2 * D_MODEL), BF16)],
                  )(x, g_mix, w_in)


S5_LANES = 512
S5_BANDS = 4


def _band(q):
    return (slice(q * S5_WIDTH // S5_BANDS, (q + 1) * S5_WIDTH // S5_BANDS),
            slice(q * S5_N // S5_BANDS, (q + 1) * S5_N // S5_BANDS))


def _im(st):
    return slice(S5_N + st.start, S5_N + st.stop)


SCAN_UNROLL = 8


def _complex_scan(buf_ref, lam_ref, st_ref, nb, ts, reverse):
    lanes = [slice(cc * S5_LANES, (cc + 1) * S5_LANES) for cc in range(S5_N // S5_LANES)]
    chains = [(b, re) for b in range(nb) for re in lanes]
    nch = len(chains)
    wr = {re.start: lam_ref[0:1, re] for re in lanes}
    wi = {re.start: -lam_ref[1:2, re] if reverse else lam_ref[1:2, re] for re in lanes}

    def block(ib, carry):
        vr, vi = list(carry[:nch]), list(carry[nch:])
        first = ts - SCAN_UNROLL - ib * SCAN_UNROLL if reverse else ib * SCAN_UNROLL
        first = pl.multiple_of(first, SCAN_UNROLL)
        for k in range(SCAN_UNROLL):
            row = pl.ds(first + (SCAN_UNROLL - 1 - k if reverse else k), 1)
            for c, (b, re) in enumerate(chains):
                nr = wr[re.start] * vr[c] - wi[re.start] * vi[c] + buf_ref[b, row, re]
                ni = wr[re.start] * vi[c] + wi[re.start] * vr[c] + buf_ref[b, row, _im(re)]
                buf_ref[b, row, re] = nr
                buf_ref[b, row, _im(re)] = ni
                vr[c], vi[c] = nr, ni
        return tuple(vr + vi)

    init = tuple(st_ref[b, 0:1, re] for b, re in chains) + tuple(st_ref[b, 1:2, re] for b, re in chains)
    last = lax.fori_loop(0, ts // SCAN_UNROLL, block, init)
    for c, (b, re) in enumerate(chains):
        st_ref[b, 0:1, re] = last[c]
        st_ref[b, 1:2, re] = last[nch + c]


BAND_CH = S5_WIDTH // S5_BANDS
BAND_ST = S5_N // S5_BANDS


def _s5_fwd(za, b_bands, lam, c_bands, dskip, nb, seq, ts):
    nts = seq // ts

    def body(za_ref, br_ref, bi_ref, lam_ref, cr_ref, ci_ref, d_ref, xs_ref, y_ref, buf_ref, st_ref):
        @pl.when(pl.program_id(0) == 0)
        def _():
            st_ref[...] = jnp.zeros_like(st_ref)

        for b in range(nb):
            zav = za_ref[b]
            for q in range(S5_BANDS):
                ch, st = _band(q)
                buf_ref[b, :, st] = _dot(zav[:, ch], br_ref[q])
                buf_ref[b, :, _im(st)] = _dot(zav[:, ch], bi_ref[q])
        _complex_scan(buf_ref, lam_ref, st_ref, nb, ts, reverse=False)
        for b in range(nb):
            zav = za_ref[b]
            xs_ref[b] = buf_ref[b].astype(BF16)
            for q in range(S5_BANDS):
                ch, st = _band(q)
                y_ref[b, :, ch] = (_dot(xs_ref[b, :, st], cr_ref[q]) + _dot(xs_ref[b, :, _im(st)], ci_ref[q])
                                   + d_ref[:, ch] * zav[:, ch])

    tok = lambda w: pl.BlockSpec((nb, ts, w), lambda j: (0, j, 0))
    to_st, to_ch = _full((S5_BANDS, BAND_CH, BAND_ST)), _full((S5_BANDS, BAND_ST, BAND_CH))
    return _pcall(body, "s5_fwd", (nts,),
                  [tok(S5_WIDTH), to_st, to_st, _full((2, S5_N)), to_ch, to_ch, _full((1, S5_WIDTH))],
                  [tok(2 * S5_N), tok(S5_WIDTH)],
                  [_sds((nb, seq, 2 * S5_N), BF16), _sds((nb, seq, S5_WIDTH))],
                  scratch=[pltpu.VMEM((nb, ts, 2 * S5_N), F32), pltpu.VMEM((nb, 2, S5_N), F32)],
                  )(za, *b_bands, lam, *c_bands, dskip)


def _hgrn_gates(zq, zf, lbh):
    sf = _sigmoid(zf)
    f = lbh + (1.0 - lbh) * sf
    sq = _sigmoid(zq)
    qa = zq * sq * QSCALE
    bc = _cumsum_rows(jnp.log(f))
    bm = bc[CHUNK // 2 - 1:CHUNK // 2, :]
    bl = bc[CHUNK - 1:CHUNK, :]
    return sf, f, sq, qa, bc, bm, bl


HG_CHUNKS_PER_STEP = 2


def _hgrn_fwd(zh, lb, nb, seq):
    nc = seq // CHUNK
    cps = HG_CHUNKS_PER_STEP

    def body(zh_ref, lb_ref, o_ref, sts_ref, st_ref):
        @pl.when(pl.program_id(0) == 0)
        def _():
            st_ref[...] = jnp.zeros_like(st_ref)

        causal = (lax.broadcasted_iota(jnp.int32, (CHUNK, CHUNK), 0)
                  >= lax.broadcasted_iota(jnp.int32, (CHUNK, CHUNK), 1))
        for cc in range(cps):
            rows = slice(cc * CHUNK, (cc + 1) * CHUNK)
            for b in range(nb):
                for h in range(HG_HEADS):
                    hs = slice(h * HG_HEAD, (h + 1) * HG_HEAD)
                    zq = zh_ref[b, rows, h * HG_HEAD:(h + 1) * HG_HEAD]
                    zf = zh_ref[b, rows, HG_WIDTH + h * HG_HEAD:HG_WIDTH + (h + 1) * HG_HEAD]
                    zi = zh_ref[b, rows, 2 * HG_WIDTH + h * HG_HEAD:2 * HG_WIDTH + (h + 1) * HG_HEAD]
                    _, f, _, qa, bc, bm, bl = _hgrn_gates(zq, zf, lb_ref[:, hs])
                    k = 1.0 - f
                    qt = qa * jnp.exp(bc - bm)
                    kt = k * jnp.exp(bm - bc)
                    qb = qa * jnp.exp(bc)
                    kd = k * jnp.exp(bl - bc)
                    st = st_ref[b, h]
                    sts_ref[b, cc, h] = st
                    a = jnp.where(causal, _dot_nt(qt, kt), 0.0)
                    o_ref[b, rows, hs] = _dot(a, zi) + _dot_nt(qb, st)
                    st_ref[b, h] = st * jnp.exp(bl) + _dot_tn(zi, kd)

    return _pcall(body, "hgrn_fwd", (nc // cps,),
                  [pl.BlockSpec((nb, cps * CHUNK, 4 * HG_WIDTH), lambda c: (0, c, 0)), _full((1, HG_WIDTH))],
                  [pl.BlockSpec((nb, cps * CHUNK, HG_WIDTH), lambda c: (0, c, 0)),
                   pl.BlockSpec((nb, cps, HG_HEADS, HG_HEAD, HG_HEAD), lambda c: (0, c, 0, 0, 0))],
                  [_sds((nb, seq, HG_WIDTH)), _sds((nb, nc, HG_HEADS, HG_HEAD, HG_HEAD))],
                  scratch=[pltpu.VMEM((nb, HG_HEADS, HG_HEAD, HG_HEAD), F32)])(zh, lb)


def _head_rms(o):
    parts = []
    for h in range(HG_HEADS):
        oh = o[:, h * HG_HEAD:(h + 1) * HG_HEAD]
        r = lax.rsqrt(jnp.mean(oh * oh, axis=-1, keepdims=True) + EPS)
        parts.append(jnp.broadcast_to(r, oh.shape))
    return jnp.concatenate(parts, axis=1)


def _head_mean(v):
    parts = []
    for h in range(HG_HEADS):
        vh = v[:, h * HG_HEAD:(h + 1) * HG_HEAD]
        parts.append(jnp.broadcast_to(jnp.mean(vh, axis=-1, keepdims=True), vh.shape))
    return jnp.concatenate(parts, axis=1)


def _mix_fwd(x, y0, o, zh, zgt, w_glu, b_glu, gain, w_pa, w_pb, w_out, g_ffn, tm):
    t = x.shape[0]

    def body(x_ref, y0_ref, o_ref, zg_ref, zgt_ref, wglu_ref, bglu_ref, gain_ref, wpa_ref, wpb_ref, wout_ref,
             gffn_ref, x1_ref, u2_ref, pa_ref, pb_ref, ya2_ref, yb_ref):
        ya1 = _gelu(y0_ref[...])
        s = _sigmoid(_dot(ya1, wglu_ref[...]) + bglu_ref[...])
        ya2 = (ya1 * s).astype(BF16)
        ov = o_ref[...]
        zg = zg_ref[...]
        yb = (ov * _head_rms(ov) * gain_ref[...] * (zg * _sigmoid(zg))).astype(BF16)
        ya2_ref[...] = ya2
        yb_ref[...] = yb
        pa = jnp.dot(ya2, wpa_ref[...], preferred_element_type=F32)
        pb = jnp.dot(yb, wpb_ref[...], preferred_element_type=F32)
        pa_ref[...] = pa.astype(BF16)
        pb_ref[...] = pb.astype(BF16)
        m = (_sigmoid(zgt_ref[:, 0:D_MODEL].astype(F32)) * pa
             + _sigmoid(zgt_ref[:, D_MODEL:].astype(F32)) * pb)
        x1 = x_ref[...] + _dot(m, wout_ref[...])
        x1_ref[...] = x1
        r = lax.rsqrt(jnp.mean(x1 * x1, axis=-1, keepdims=True) + EPS)
        u2_ref[...] = (x1 * r * gffn_ref[...]).astype(BF16)

    row = lambda w: pl.BlockSpec((tm, w), lambda i: (i, 0))
    return _pcall(body, "mix_fwd", (t // tm,),
                  [row(D_MODEL), row(S5_WIDTH), row(HG_WIDTH), pl.BlockSpec((tm, HG_WIDTH), lambda i: (i, 3)),
                   row(2 * D_MODEL), _full((S5_WIDTH, S5_WIDTH)), _full((1, S5_WIDTH)), _full((1, HG_WIDTH)),
                   _full((S5_WIDTH, D_MODEL)), _full((HG_WIDTH, D_MODEL)), _full((D_MODEL, D_MODEL)),
                   _full((1, D_MODEL))],
                  [row(D_MODEL), row(D_MODEL), row(D_MODEL), row(D_MODEL), row(S5_WIDTH), row(HG_WIDTH)],
                  [_sds((t, D_MODEL)), _sds((t, D_MODEL), BF16), _sds((t, D_MODEL), BF16), _sds((t, D_MODEL), BF16),
                   _sds((t, S5_WIDTH), BF16), _sds((t, HG_WIDTH), BF16)],
                  )(x, y0, o, zh, zgt, w_glu, b_glu, gain, w_pa, w_pb, w_out, g_ffn)


FF_COLS = 256
FF_UP_TILE = 2 * D_FF // 2


def _ffn_up(u2, w_up, tm):
    t = u2.shape[0]
    n = 2 * D_FF

    def body(u_ref, w_ref, h_ref):
        h_ref[...] = _dot_nt(u_ref[...], w_ref[...]).astype(BF16)

    return _pcall(body, "ffn_up", (n // FF_UP_TILE, t // tm),
                  [pl.BlockSpec((tm, D_MODEL), lambda j, i: (i, 0)),
                   pl.BlockSpec((FF_UP_TILE, D_MODEL), lambda j, i: (j, 0))],
                  pl.BlockSpec((tm, FF_UP_TILE), lambda j, i: (i, j)),
                  _sds((t, n), BF16))(u2, w_up)


HALO = 16


def _shift_matrix(tm):
    r = lax.broadcasted_iota(jnp.int32, (tm, tm), 0)
    c = lax.broadcasted_iota(jnp.int32, (tm, tm), 1)
    return jnp.where(r == c + 1, 1.0, 0.0).astype(BF16)


def _conv_cols(h_ref, halo_ref, valid, wc_ref, bc_ref, c0):
    cs = slice(c0, c0 + FF_COLS)
    cur = h_ref[:, cs].astype(F32)
    prev = jnp.where(valid, halo_ref[:, cs].astype(F32), 0.0)
    full = jnp.concatenate([prev, cur], axis=0)
    h1 = pltpu.roll(full, 1, axis=0)[HALO:]
    h2 = pltpu.roll(full, 2, axis=0)[HALO:]
    return h2 * wc_ref[0:1, cs] + h1 * wc_ref[1:2, cs] + cur * wc_ref[2:3, cs] + bc_ref[:, cs]


def _ffn_down_loss(h, x1, tgt, w_conv, b_conv, w_down, g_final, seq, tm):
    t = h.shape[0]
    tps = seq // tm
    n = 2 * D_FF

    def body(h_ref, halo_ref, x1_ref, tgt_ref, wc_ref, bc_ref, wd_ref, gf_ref,
             hc_ref, a_ref, dx2_ref, dx2b_ref, loss_ref, dgf_ref):
        i = pl.program_id(0)

        @pl.when(i == 0)
        def _():
            loss_ref[...] = jnp.zeros_like(loss_ref)
            dgf_ref[...] = jnp.zeros_like(dgf_ref)

        valid = (i % tps) != 0
        x2 = x1_ref[...]
        for j in range(D_FF // FF_COLS):
            gate = _conv_cols(h_ref, halo_ref, valid, wc_ref, bc_ref, j * FF_COLS)
            val = _conv_cols(h_ref, halo_ref, valid, wc_ref, bc_ref, D_FF + j * FF_COLS)
            hc_ref[:, j * FF_COLS:(j + 1) * FF_COLS] = gate.astype(BF16)
            hc_ref[:, D_FF + j * FF_COLS:D_FF + (j + 1) * FF_COLS] = val.astype(BF16)
            a = (gate * _sigmoid(gate) * val).astype(BF16)
            a_ref[:, j * FF_COLS:(j + 1) * FF_COLS] = a
            x2 = x2 + jnp.dot(a, wd_ref[j * FF_COLS:(j + 1) * FF_COLS, :], preferred_element_type=F32)
        r = lax.rsqrt(jnp.mean(x2 * x2, axis=-1, keepdims=True) + EPS)
        xn = x2 * r
        g = gf_ref[...]
        e = xn * g - tgt_ref[...]
        loss_ref[...] += (0.5 / D_MODEL) * jnp.sum(e * e).reshape(1, 1)
        dy = e * (1.0 / D_MODEL)
        dgf_ref[...] += jnp.sum(dy * xn, axis=0, keepdims=True)
        dxn = dy * g
        dx2 = r * (dxn - xn * jnp.mean(dxn * xn, axis=-1, keepdims=True))
        dx2_ref[...] = dx2
        dx2b_ref[...] = dx2.astype(BF16)

    row = lambda w: pl.BlockSpec((tm, w), lambda i: (i, 0))
    halo = pl.BlockSpec((HALO, n), lambda i: (jnp.maximum(i * (tm // HALO) - 1, 0), 0))
    return _pcall(body, "ffn_down_loss", (t // tm,),
                  [row(n), halo, row(D_MODEL), row(D_MODEL), _full((CONV_W, n)), _full((1, n)),
                   _full((D_FF, D_MODEL)), _full((1, D_MODEL))],
                  [row(n), row(D_FF), row(D_MODEL), row(D_MODEL), _full((1, 1)), _full((1, D_MODEL))],
                  [_sds((t, n), BF16), _sds((t, D_FF), BF16), _sds((t, D_MODEL)), _sds((t, D_MODEL), BF16),
                   _sds((1, 1)), _sds((1, D_MODEL))],
                  )(h, h, x1, tgt, w_conv, b_conv, w_down, g_final)


def _wgrad(a, b, name, tn, out_dtype=F32, band=None, after=None):
    t, m = a.shape
    n = b.shape[1] if band is None else band
    nbands = 1 if band is None else b.shape[1] // band
    after = b if after is None else after

    def body(a_ref, b_ref, after_ref, o_ref):
        o_ref[...] = _dot_tn(a_ref[...], b_ref[...]).astype(out_dtype)

    return _pcall(body, name, (m // tn,),
                  [pl.BlockSpec((t, tn), lambda i: (0, i)), pl.BlockSpec((t, n), lambda i: (0, i % nbands)),
                   pl.BlockSpec(memory_space=pl.ANY)],
                  pl.BlockSpec((tn, n), lambda i: (i, 0)), _sds((m, n), out_dtype))(a, b, after)


def _ffn_bwd_act(dx2b, hc, w_down, tm):
    t = hc.shape[0]
    n = 2 * D_FF

    def body(dx2_ref, hc_ref, wd_ref, dhc_ref, dbc_ref):
        @pl.when(pl.program_id(0) == 0)
        def _():
            dbc_ref[...] = jnp.zeros_like(dbc_ref)

        dx2 = dx2_ref[...]
        for j in range(D_FF // FF_COLS):
            gs = slice(j * FF_COLS, (j + 1) * FF_COLS)
            vs = slice(D_FF + j * FF_COLS, D_FF + (j + 1) * FF_COLS)
            gate = hc_ref[:, gs].astype(F32)
            val = hc_ref[:, vs].astype(F32)
            da = _dot_nt(dx2, wd_ref[gs, :])
            sg = _sigmoid(gate)
            dgate = da * val * (sg * (1.0 + gate * (1.0 - sg)))
            dval = da * (gate * sg)
            dhc_ref[:, gs] = dgate.astype(BF16)
            dhc_ref[:, vs] = dval.astype(BF16)
            dbc_ref[:, gs] += jnp.sum(dgate, axis=0, keepdims=True)
            dbc_ref[:, vs] += jnp.sum(dval, axis=0, keepdims=True)

    row = lambda w: pl.BlockSpec((tm, w), lambda i: (i, 0))
    return _pcall(body, "ffn_bwd_act", (t // tm,),
                  [row(D_MODEL), row(n), _full((D_FF, D_MODEL))],
                  [row(n), _full((1, n))],
                  [_sds((t, n), BF16), _sds((1, n))],
                  )(dx2b, hc, w_down)


def _ffn_bwd_up(dhc, h, dx2, x1, w_conv, w_up, g_ffn, seq, tm):
    t = dhc.shape[0]
    tps = seq // tm
    n = 2 * D_FF
    last = t // HALO - 1

    def body(dhc_ref, halo_ref, h_ref, dx2_ref, x1_ref, wc_ref, wu_ref, gf_ref,
             dh_ref, dx1_ref, dx1b_ref, dgf_ref, dwc_ref):
        i = pl.program_id(0)

        @pl.when(i == 0)
        def _():
            dgf_ref[...] = jnp.zeros_like(dgf_ref)
            dwc_ref[...] = jnp.zeros_like(dwc_ref)

        valid = ((i + 1) % tps) != 0
        du2 = jnp.zeros((tm, D_MODEL), F32)
        for j in range(n // FF_COLS):
            cs = slice(j * FF_COLS, (j + 1) * FF_COLS)
            cur = dhc_ref[:, cs].astype(F32)
            nxt = jnp.where(valid, halo_ref[:, cs].astype(F32), 0.0)
            full = jnp.concatenate([cur, nxt], axis=0)
            d1 = pltpu.roll(full, tm + HALO - 1, axis=0)[:tm]
            d2 = pltpu.roll(full, tm + HALO - 2, axis=0)[:tm]
            dh = (cur * wc_ref[2:3, cs] + d1 * wc_ref[1:2, cs] + d2 * wc_ref[0:1, cs]).astype(BF16)
            dh_ref[:, cs] = dh
            du2 = du2 + _dot(dh, wu_ref[cs, :])
            hv = h_ref[:, cs].astype(F32)
            dwc_ref[0:1, cs] += jnp.sum(hv * d2, axis=0, keepdims=True)
            dwc_ref[1:2, cs] += jnp.sum(hv * d1, axis=0, keepdims=True)
            dwc_ref[2:3, cs] += jnp.sum(hv * cur, axis=0, keepdims=True)
        x1 = x1_ref[...]
        r = lax.rsqrt(jnp.mean(x1 * x1, axis=-1, keepdims=True) + EPS)
        xn = x1 * r
        dgf_ref[...] += jnp.sum(du2 * xn, axis=0, keepdims=True)
        dxn = du2 * gf_ref[...]
        dx1 = dx2_ref[...] + r * (dxn - xn * jnp.mean(dxn * xn, axis=-1, keepdims=True))
        dx1_ref[...] = dx1
        dx1b_ref[...] = dx1.astype(BF16)

    row = lambda w: pl.BlockSpec((tm, w), lambda i: (i, 0))
    halo = pl.BlockSpec((HALO, n), lambda i: (jnp.minimum((i + 1) * (tm // HALO), last), 0))
    return _pcall(body, "ffn_bwd_up", (t // tm,),
                  [row(n), halo, row(n), row(D_MODEL), row(D_MODEL), _full((CONV_W, n)), _full((n, D_MODEL)),
                   _full((1, D_MODEL))],
                  [row(n), row(D_MODEL), row(D_MODEL), _full((1, D_MODEL)), _full((CONV_W, n))],
                  [_sds((t, n), BF16), _sds((t, D_MODEL)), _sds((t, D_MODEL), BF16), _sds((1, D_MODEL)),
                   _sds((CONV_W, n))],
                  )(dhc, dhc, h, dx2, x1, w_conv, w_up, g_ffn)


def _mix_bwd(dx1, y0, o, zh, zgt, pa, pb, w_glu, b_glu, gain, w_pa, w_pb, w_out, tm):
    t = dx1.shape[0]

    def body(dx1_ref, y0_ref, o_ref, zg_ref, zgt_ref, pa_ref, pb_ref, wglu_ref, bglu_ref, gain_ref, wpa_ref,
             wpb_ref, wout_ref,
             dy0_ref, do_ref, dzg_ref, dzgt_ref, m_ref, dpa_ref, dpb_ref, ya1_ref, dpre_ref, dbglu_ref, dgain_ref):
        @pl.when(pl.program_id(0) == 0)
        def _():
            dbglu_ref[...] = jnp.zeros_like(dbglu_ref)
            dgain_ref[...] = jnp.zeros_like(dgain_ref)

        dm = _dot_nt(dx1_ref[...], wout_ref[...])
        sga = _sigmoid(zgt_ref[:, 0:D_MODEL].astype(F32))
        sgb = _sigmoid(zgt_ref[:, D_MODEL:].astype(F32))
        pa = pa_ref[...].astype(F32)
        pb = pb_ref[...].astype(F32)
        m_ref[...] = (sga * pa + sgb * pb).astype(BF16)
        dzgt_ref[:, 0:D_MODEL] = (dm * pa * sga * (1.0 - sga)).astype(BF16)
        dzgt_ref[:, D_MODEL:] = (dm * pb * sgb * (1.0 - sgb)).astype(BF16)
        dpa = (dm * sga).astype(BF16)
        dpb = (dm * sgb).astype(BF16)
        dpa_ref[...] = dpa
        dpb_ref[...] = dpb
        dya2 = _dot_nt(dpa, wpa_ref[...])
        dyb = _dot_nt(dpb, wpb_ref[...])
        y0 = y0_ref[...]
        ya1 = _gelu(y0)
        ya1_ref[...] = ya1.astype(BF16)
        s = _sigmoid(_dot(ya1, wglu_ref[...]) + bglu_ref[...])
        dpre = dya2 * ya1 * s * (1.0 - s)
        dpre_ref[...] = dpre.astype(BF16)
        dbglu_ref[...] += jnp.sum(dpre, axis=0, keepdims=True)
        dya1 = dya2 * s + _dot_nt(dpre, wglu_ref[...])
        dy0_ref[...] = dya1 * _gelu_grad(y0)
        ov = o_ref[...]
        zg = zg_ref[...]
        oh = ov * _head_rms(ov)
        on = oh * gain_ref[...]
        sz = _sigmoid(zg)
        dzg_ref[...] = (dyb * on * (sz * (1.0 + zg * (1.0 - sz)))).astype(BF16)
        don = dyb * (zg * sz)
        dgain_ref[...] += jnp.sum(don * oh, axis=0, keepdims=True)
        doh = don * gain_ref[...]
        do_ref[...] = _head_rms(ov) * (doh - oh * _head_mean(doh * oh))

    row = lambda w: pl.BlockSpec((tm, w), lambda i: (i, 0))
    return _pcall(body, "mix_bwd", (t // tm,),
                  [row(D_MODEL), row(S5_WIDTH), row(HG_WIDTH), pl.BlockSpec((tm, HG_WIDTH), lambda i: (i, 3)),
                   row(2 * D_MODEL), row(D_MODEL), row(D_MODEL), _full((S5_WIDTH, S5_WIDTH)), _full((1, S5_WIDTH)),
                   _full((1, HG_WIDTH)), _full((S5_WIDTH, D_MODEL)), _full((HG_WIDTH, D_MODEL)),
                   _full((D_MODEL, D_MODEL))],
                  [row(S5_WIDTH), row(HG_WIDTH), row(HG_WIDTH), row(2 * D_MODEL), row(D_MODEL), row(D_MODEL),
                   row(D_MODEL), row(S5_WIDTH), row(S5_WIDTH), _full((1, S5_WIDTH)), _full((1, HG_WIDTH))],
                  [_sds((t, S5_WIDTH)), _sds((t, HG_WIDTH)), _sds((t, HG_WIDTH), BF16), _sds((t, 2 * D_MODEL), BF16),
                   _sds((t, D_MODEL), BF16), _sds((t, D_MODEL), BF16), _sds((t, D_MODEL), BF16),
                   _sds((t, S5_WIDTH), BF16), _sds((t, S5_WIDTH), BF16), _sds((1, S5_WIDTH)), _sds((1, HG_WIDTH))],
                  )(dx1, y0, o, zh, zgt, pa, pb, w_glu, b_glu, gain, w_pa, w_pb, w_out)


def _s5_bwd(dy0, za, xs, c_bands, b_bands, lam, dskip, nb, seq, ts):
    nts = seq // ts

    def body(dy0_ref, za_ref, xs_ref, halo_ref, cr_ref, ci_ref, br_ref, bi_ref, lam_ref, d_ref,
             dza_ref, a_ref, dlam_ref, dd_ref, acc_ref, st_ref):
        j = pl.program_id(0)

        @pl.when(j == 0)
        def _():
            dlam_ref[...] = jnp.zeros_like(dlam_ref)
            dd_ref[...] = jnp.zeros_like(dd_ref)
            st_ref[...] = jnp.zeros_like(st_ref)

        for b in range(nb):
            dy0 = dy0_ref[b]
            for q in range(S5_BANDS):
                ch, st = _band(q)
                acc_ref[b, :, st] = _dot(dy0[:, ch], cr_ref[q])
                acc_ref[b, :, _im(st)] = _dot(dy0[:, ch], ci_ref[q])
        _complex_scan(acc_ref, lam_ref, st_ref, nb, ts, reverse=True)
        shift = _shift_matrix(ts)
        top = lax.broadcasted_iota(jnp.int32, (SUBLANES, S5_LANES), 0) == 0
        for b in range(nb):
            a_ref[b] = acc_ref[b].astype(BF16)
            first = jnp.where(j == nts - 1, 0.0, halo_ref[b, HALO - 1:HALO, :].astype(F32))

            def shifted(cols):
                xp = jnp.dot(shift, xs_ref[b, :, cols], preferred_element_type=F32)
                return jnp.concatenate([xp[:SUBLANES] + jnp.where(top, first[:, cols], 0.0), xp[SUBLANES:]], axis=0)

            for cc in range(S5_N // S5_LANES):
                re = slice(cc * S5_LANES, (cc + 1) * S5_LANES)
                ar, ai, xr, xi = acc_ref[b, :, re], acc_ref[b, :, _im(re)], shifted(re), shifted(_im(re))
                dlam_ref[0:1, re] += jnp.sum(ar * xr + ai * xi, axis=0, keepdims=True)
                dlam_ref[1:2, re] += jnp.sum(ai * xr - ar * xi, axis=0, keepdims=True)
            dy0 = dy0_ref[b]
            for q in range(S5_BANDS):
                ch, st = _band(q)
                dza_ref[b, :, ch] = (_dot(a_ref[b, :, st], br_ref[q]) + _dot(a_ref[b, :, _im(st)], bi_ref[q])
                                     + d_ref[:, ch] * dy0[:, ch]).astype(BF16)
            dd_ref[...] += jnp.sum(dy0 * za_ref[b], axis=0, keepdims=True)

    tile = lambda j: nts - 1 - j
    tok = lambda w: pl.BlockSpec((nb, ts, w), lambda j: (0, tile(j), 0))
    halo = pl.BlockSpec((nb, HALO, 2 * S5_N), lambda j: (0, jnp.maximum(tile(j) * (ts // HALO) - 1, 0), 0))
    to_st, to_ch = _full((S5_BANDS, BAND_CH, BAND_ST)), _full((S5_BANDS, BAND_ST, BAND_CH))
    return _pcall(body, "s5_bwd", (nts,),
                  [tok(S5_WIDTH), tok(S5_WIDTH), tok(2 * S5_N), halo, to_st, to_st, to_ch, to_ch,
                   _full((2, S5_N)), _full((1, S5_WIDTH))],
                  [tok(S5_WIDTH), tok(2 * S5_N), _full((2, S5_N)), _full((1, S5_WIDTH))],
                  [_sds((nb, seq, S5_WIDTH), BF16), _sds((nb, seq, 2 * S5_N), BF16), _sds((2, S5_N)),
                   _sds((1, S5_WIDTH))],
                  scratch=[pltpu.VMEM((nb, ts, 2 * S5_N), F32), pltpu.VMEM((nb, 2, S5_N), F32)],
                  )(dy0, za, xs, xs, *c_bands, *b_bands, lam, dskip)


def _hgrn_bwd(zh, do, sts, lb, nb, seq):
    nc = seq // CHUNK
    cps = HG_CHUNKS_PER_STEP

    def body(zh_ref, do_ref, sts_ref, lb_ref, dz_ref, dlb_ref, dst_ref):
        @pl.when(pl.program_id(0) == 0)
        def _():
            dst_ref[...] = jnp.zeros_like(dst_ref)
            dlb_ref[...] = jnp.zeros_like(dlb_ref)

        row = lax.broadcasted_iota(jnp.int32, (CHUNK, CHUNK), 0)
        causal = row >= lax.broadcasted_iota(jnp.int32, (CHUNK, CHUNK), 1)
        last_row = lax.broadcasted_iota(jnp.int32, (CHUNK, HG_HEAD), 0) == CHUNK - 1
        for cc in reversed(range(cps)):
            rows = slice(cc * CHUNK, (cc + 1) * CHUNK)
            for b in range(nb):
                for h in range(HG_HEADS):
                    hs = slice(h * HG_HEAD, (h + 1) * HG_HEAD)
                    zq = zh_ref[b, rows, h * HG_HEAD:(h + 1) * HG_HEAD]
                    zf = zh_ref[b, rows, HG_WIDTH + h * HG_HEAD:HG_WIDTH + (h + 1) * HG_HEAD]
                    zi = zh_ref[b, rows, 2 * HG_WIDTH + h * HG_HEAD:2 * HG_WIDTH + (h + 1) * HG_HEAD]
                    lbh = lb_ref[:, hs]
                    sf, f, sq, qa, bc, bm, bl = _hgrn_gates(zq, zf, lbh)
                    k = 1.0 - f
                    e_qt = jnp.exp(bc - bm)
                    e_kt = jnp.exp(bm - bc)
                    e_b = jnp.exp(bc)
                    e_kd = jnp.exp(bl - bc)
                    e_l = jnp.exp(bl)
                    qt, kt, qb, kd = qa * e_qt, k * e_kt, qa * e_b, k * e_kd
                    a = jnp.where(causal, _dot_nt(qt, kt), 0.0)
                    st = sts_ref[b, cc, h]
                    dst = dst_ref[b, h]
                    dov = do_ref[b, rows, hs]
                    da = jnp.where(causal, _dot_nt(dov, zi), 0.0)
                    qt_r, kt_r = qt.astype(BF16).astype(F32), kt.astype(BF16).astype(F32)
                    dqt = _dot(da, kt)
                    dkt = _dot_tn(da, qt)
                    dqb = _dot(dov, st)
                    di = _dot_tn(a, dov) + _dot_nt(kd, dst)
                    dkd = _dot(zi, dst)
                    de_l = jnp.sum(dst * st, axis=0, keepdims=True)
                    dst_ref[b, h] = dst * e_l + _dot_tn(dov, qb)
                    dqa = dqt * e_qt + dqb * e_b
                    dk = dkt * e_kt + dkd * e_kd
                    dbl = jnp.sum(dkd * kd, axis=0, keepdims=True) + de_l * e_l
                    db = dqt * qt_r - dkt * kt_r + dqb * qb - dkd * kd + jnp.where(last_row, dbl, 0.0)
                    df = _cumsum_rows(db, reverse=True) / f - dk
                    dzq = dqa * QSCALE * (sq * (1.0 + zq * (1.0 - sq)))
                    dzf = df * (1.0 - lbh) * sf * (1.0 - sf)
                    dz_ref[b, rows, h * HG_HEAD:(h + 1) * HG_HEAD] = dzq.astype(BF16)
                    dz_ref[b, rows, HG_WIDTH + h * HG_HEAD:HG_WIDTH + (h + 1) * HG_HEAD] = dzf.astype(BF16)
                    dz_ref[b, rows, 2 * HG_WIDTH + h * HG_HEAD:2 * HG_WIDTH + (h + 1) * HG_HEAD] = di.astype(BF16)
                    dlb_ref[:, hs] += jnp.sum(df * (1.0 - sf), axis=0, keepdims=True)

    rev = lambda c: nc // cps - 1 - c
    return _pcall(body, "hgrn_bwd", (nc // cps,),
                  [pl.BlockSpec((nb, cps * CHUNK, 4 * HG_WIDTH), lambda c: (0, rev(c), 0)),
                   pl.BlockSpec((nb, cps * CHUNK, HG_WIDTH), lambda c: (0, rev(c), 0)),
                   pl.BlockSpec((nb, cps, HG_HEADS, HG_HEAD, HG_HEAD), lambda c: (0, rev(c), 0, 0, 0)),
                   _full((1, HG_WIDTH))],
                  [pl.BlockSpec((nb, cps * CHUNK, 3 * HG_WIDTH), lambda c: (0, rev(c), 0)), _full((1, HG_WIDTH))],
                  [_sds((nb, seq, 3 * HG_WIDTH), BF16), _sds((1, HG_WIDTH))],
                  scratch=[pltpu.VMEM((nb, HG_HEADS, HG_HEAD, HG_HEAD), F32)])(zh, do, sts, lb)


def _in_proj_bwd(dza, dzh, dzg, dzgt, dx1, x, g_mix, w_in, tm):
    t = x.shape[0]

    def body(dza_ref, dzh_ref, dzg_ref, dzgt_ref, dx1_ref, x_ref, g_ref, w_ref, dz_ref, dx_ref, dg_ref):
        @pl.when(pl.program_id(0) == 0)
        def _():
            dg_ref[...] = jnp.zeros_like(dg_ref)

        c1, c2, c3 = S5_WIDTH, S5_WIDTH + 3 * HG_WIDTH, S5_WIDTH + 4 * HG_WIDTH
        dz_ref[:, 0:c1] = dza_ref[...]
        dz_ref[:, c1:c2] = dzh_ref[...]
        dz_ref[:, c2:c3] = dzg_ref[...]
        dz_ref[:, c3:] = dzgt_ref[...]
        du = _dot(dz_ref[...], w_ref[...])
        xv = x_ref[...]
        r = lax.rsqrt(jnp.mean(xv * xv, axis=-1, keepdims=True) + EPS)
        xn = xv * r
        dg_ref[...] += jnp.sum(du * xn, axis=0, keepdims=True)
        dxn = du * g_ref[...]
        dx_ref[...] = dx1_ref[...] + r * (dxn - xn * jnp.mean(dxn * xn, axis=-1, keepdims=True))

    row = lambda w: pl.BlockSpec((tm, w), lambda i: (i, 0))
    return _pcall(body, "in_proj_bwd", (t // tm,),
                  [row(S5_WIDTH), row(3 * HG_WIDTH), row(HG_WIDTH), row(2 * D_MODEL), row(D_MODEL), row(D_MODEL),
                   _full((1, D_MODEL)), _full((N_IN, D_MODEL))],
                  [row(N_IN), row(D_MODEL), _full((1, D_MODEL))],
                  [_sds((t, N_IN), BF16), _sds((t, D_MODEL)), _sds((1, D_MODEL))],
                  )(dza, dzh, dzg, dzgt, dx1, x, g_mix, w_in)


def _after(value, token):
    return value + token[0, 0]


def _local_step(x3, tgt3, weights, sp, emit, emit_small):
    nb, seq, _ = x3.shape
    t = nb * seq
    tm = _token_tile(seq)
    x = x3.reshape(t, D_MODEL)
    tgt = tgt3.reshape(t, D_MODEL)
    row = lambda v: v.reshape(1, -1)

    a_re, a_im, b_re, b_im = sp["s5_a_re"], sp["s5_a_im"], sp["s5_b_re"], sp["s5_b_im"]
    ldt = sp["s5_log_dt"].reshape(S5_GROUPS, 1)
    lr, li, bb_re, bb_im, lb = _params_fwd(a_re, a_im, ldt, b_re, b_im, sp["hg_lb_logits"])
    lam = jnp.concatenate([lr.reshape(1, S5_N), li.reshape(1, S5_N)], axis=0)
    swap = lambda m: m.transpose(0, 2, 1)
    b_to_st = (_band_blocks(bb_re), _band_blocks(bb_im))
    b_to_ch = (_band_blocks(swap(bb_re)), _band_blocks(swap(bb_im)))
    c_to_ch = (_band_blocks(swap(sp["s5_c_re"])), _band_blocks(swap(-sp["s5_c_im"])))
    c_to_st = (_band_blocks(sp["s5_c_re"]), _band_blocks(-sp["s5_c_im"]))

    g_mix, g_ffn, g_final = row(sp["g_mix"]), row(sp["g_ffn"]), row(sp["g_final"])
    b_glu, gain, dskip, b_conv = row(sp["b_glu"]), row(sp["hg_norm_gain"]), row(sp["s5_d"]), row(sp["b_conv"])

    w_in = weights("in", lam, *b_to_st, *b_to_ch, *c_to_ch, *c_to_st)["w_in"]
    wide = min(2 * tm, seq)
    u, za, zh, zgt = _in_proj(x, g_mix, w_in, wide)
    seqs = lambda v: v.reshape(nb, seq, v.shape[-1])
    toks = lambda v: v.reshape(t, v.shape[-1])
    xs3, y0 = _s5_fwd(seqs(za), b_to_st, lam, c_to_ch, dskip, nb, seq, tm)
    xs, y0 = toks(xs3), toks(y0)
    o3, sts = _hgrn_fwd(zh.reshape(nb, seq, 4 * HG_WIDTH), lb, nb, seq)
    o = o3.reshape(t, HG_WIDTH)
    wm = weights("mix", y0, o3)
    weights.forward("ffn", wm["w_out"])
    x1, u2, pa, pb, ya2, yb = _mix_fwd(x, y0, o, zh, zgt, wm["w_glu"], b_glu, gain, wm["w_pa"], wm["w_pb"],
                                       wm["w_out"], g_ffn, wide)
    wf = weights("ffn", u2)
    h = _ffn_up(u2, wf["w_up"], min(4 * tm, t))
    hc, a, dx2, dx2b, loss, dg_final = _ffn_down_loss(h, x1, tgt, wf["w_conv"], b_conv, wf["w_down"], g_final,
                                                      seq, tm)

    def wgrad(a, b, name):
        return _wgrad(a, b, name, 512 if a.shape[1] % 512 == 0 else 256, out_dtype=BF16)

    dhc, db_conv = _ffn_bwd_act(dx2b, hc, wf["w_down"], tm)
    dw_down = wgrad(a, dx2b, "dw_down")
    dh, dx1, dx1b, dg_ffn, dw_conv = _ffn_bwd_up(dhc, h, dx2, x1, wf["w_conv"], wf["w_up"], g_ffn, seq, tm)
    sent = emit({"w_up": wgrad(dh, u2, "dw_up"), "w_conv": dw_conv, "w_down": dw_down})
    (dy0, do, dzg, dzgt, m, dpa, dpb, ya1, dpre, db_glu, dgain) = _mix_bwd(
        dx1b, y0, o, zh, zgt, pa, pb, wm["w_glu"], _after(b_glu, sent), gain, wm["w_pa"], wm["w_pb"], wm["w_out"],
        wide)
    sent = emit({"w_out": wgrad(m, dx1b, "dw_out"), "w_pa": wgrad(ya2, dpa, "dw_pa"),
                 "w_pb": wgrad(yb, dpb, "dw_pb"), "w_glu": wgrad(ya1, dpre, "dw_glu")})
    dzh3, dlb = _hgrn_bwd(zh.reshape(nb, seq, 4 * HG_WIDTH), do.reshape(nb, seq, HG_WIDTH), sts, _after(lb, sent),
                          nb, seq)
    dza, a_s5, dlam, dd = _s5_bwd(seqs(dy0), seqs(za), xs3, c_to_st, b_to_ch, lam, dskip, nb, seq, tm)
    dza, a_s5 = toks(dza), toks(a_s5)
    dz, dx, dg_mix = _in_proj_bwd(dza, dzh3.reshape(t, 3 * HG_WIDTH), dzg, dzgt, dx1, x, g_mix, w_in, wide)
    sent = emit({"w_in": wgrad(dz, u, "dw_in")})

    band = HG_HEAD
    dbb_band = _wgrad(a_s5, za, "dbb_s5", 512, band=band, after=sent)
    dc_band = _wgrad(xs, dy0, "dc_s5", 512, band=band, after=sent)
    dbb_re = swap(_diag_blocks(dbb_band[:S5_N], S5_STATE, S5_GROUP))
    dbb_im = swap(_diag_blocks(dbb_band[S5_N:], S5_STATE, S5_GROUP))
    dc_re = swap(_diag_blocks(dc_band[:S5_N], S5_STATE, S5_GROUP))
    dc_im = -swap(_diag_blocks(dc_band[S5_N:], S5_STATE, S5_GROUP))
    da_re, da_im, dldt, db_re, db_im, dlogits = _params_bwd(
        a_re, a_im, ldt, b_re, b_im, sp["hg_lb_logits"],
        dlam[0].reshape(S5_GROUPS, S5_STATE), dlam[1].reshape(S5_GROUPS, S5_STATE), dbb_re, dbb_im, dlb)
    emit_small({"g_mix": dg_mix, "s5_a_re": da_re, "s5_a_im": da_im, "s5_log_dt": dldt.reshape(1, S5_GROUPS),
                "s5_b_re": db_re, "s5_b_im": db_im, "s5_c_re": dc_re, "s5_c_im": dc_im, "s5_d": dd, "b_glu": db_glu,
                "hg_lb_logits": dlogits, "hg_norm_gain": dgain, "g_ffn": dg_ffn, "b_conv": db_conv,
                "g_final": dg_final, "loss": loss})
    return dx.reshape(nb, seq, D_MODEL)


def _mesh_peers():
    x, y, c = lax.axis_index("x"), lax.axis_index("y"), lax.axis_index("c")
    peers = []
    for k in range(1, N_DEV):
        px, py, pc = (1 - x if k & 4 else x), (1 - y if k & 2 else y), (1 - c if k & 1 else c)
        peers.append((k, (px, py, pc), 4 * px + 2 * py + pc))
    return 4 * x + 2 * y + c, peers


_HBM = pl.BlockSpec(memory_space=pltpu.HBM)
_SEM = pl.BlockSpec(memory_space=pltpu.SEMAPHORE)


_EFFECT = pltpu.CompilerParams(has_side_effects=pltpu.SideEffectType.DATAFLOW_SIDE_EFFECTING)


def _remote(src, dst, send_sem, recv_sem, to):
    return pltpu.make_async_remote_copy(src_ref=src, dst_ref=dst, send_sem=send_sem, recv_sem=recv_sem,
                                        device_id=to, device_id_type=pl.DeviceIdType.MESH)


def _exchange_start(name, arrays, after):
    n = len(arrays)
    srcs = [pltpu.with_memory_space_constraint(a, pltpu.HBM) for a in arrays]
    lands = [pltpu.with_memory_space_constraint(lax.empty(a.shape, a.dtype), pltpu.HBM) for a in arrays]
    copies = (N_DEV - 1) * n

    def body(*refs):
        src_refs, land_refs = refs[:n], refs[n:2 * n]
        send_sems, recv_sems, token = refs[2 * n + 1], refs[2 * n + 2], refs[-1]
        my_slab, peers = _mesh_peers()
        for k, peer, slab in peers:
            for i in range(n):
                s = (k - 1) * n + i
                _remote(src_refs[i].at[slab], land_refs[i].at[my_slab], send_sems.at[s], recv_sems.at[s], peer).start()
        token[...] = jnp.zeros_like(token)

    outs = pl.pallas_call(
        body, name=name,
        out_shape=(pltpu.SemaphoreType.DMA((copies,)), pltpu.SemaphoreType.DMA((copies,)),
                   *[pltpu.HBM(a.shape, a.dtype) for a in lands], _sds((SUBLANES, LANES))),
        in_specs=[_HBM] * (2 * n) + [pl.BlockSpec(memory_space=pl.ANY)],
        out_specs=(_SEM, _SEM, *[_HBM] * n, pl.BlockSpec(memory_space=pltpu.VMEM)),
        input_output_aliases={n + i: 2 + i for i in range(n)}, compiler_params=_EFFECT,
    )(*srcs, *lands, after)
    return (outs[0], outs[1], srcs, outs[2:2 + n]), outs[-1]


def _exchange_wait(name, state, *after):
    send_sems, recv_sems, srcs, lands = state
    n = len(lands)

    def body(*refs):
        src_refs, land_refs = refs[:n], refs[n:2 * n]
        send_ref, recv_ref = refs[2 * n], refs[2 * n + 1]
        _, peers = _mesh_peers()
        for k, peer, slab in peers:
            for i in range(n):
                s = (k - 1) * n + i
                copy = _remote(src_refs[i].at[slab], land_refs[i].at[slab], send_ref.at[s], recv_ref.at[s], peer)
                copy.wait_send()
                copy.wait_recv()

    outs = pl.pallas_call(
        body, name=name,
        out_shape=tuple(pltpu.HBM(a.shape, a.dtype) for a in lands),
        in_specs=[_HBM] * (2 * n) + [_SEM, _SEM] + [pl.BlockSpec(memory_space=pl.ANY)] * len(after),
        out_specs=tuple([_HBM] * n),
        input_output_aliases={n + i: i for i in range(n)}, compiler_params=_EFFECT,
    )(*srcs, *lands, send_sems, recv_sems, *after)
    return list(outs), list(srcs)


def _slab(pos):
    return 4 * pos[0] + 2 * pos[1] + pos[2]


def _chip_routes():
    x, y, c = lax.axis_index("x"), lax.axis_index("y"), lax.axis_index("c")
    return (x, y, c), (x, y, 1 - c), [(1 - x, y, c), (x, 1 - y, c), (1 - x, 1 - y, c)]


def _gather_start(name, arrays, after):
    n = len(arrays)
    me = 4 * lax.axis_index("x") + 2 * lax.axis_index("y") + lax.axis_index("c")
    srcs = [pltpu.with_memory_space_constraint(a, pltpu.HBM) for a in arrays]
    lands = [pltpu.with_memory_space_constraint(
        lax.dynamic_update_slice_in_dim(lax.empty((N_DEV,) + a.shape, a.dtype), a[None], me, 0), pltpu.HBM)
        for a in arrays]

    def body(*refs):
        src_refs, land_refs = refs[:n], refs[n:2 * n]
        send_sems, recv_sems, token = refs[2 * n + 1], refs[2 * n + 2], refs[-1]
        mine, sibling, chips = _chip_routes()
        for k, to in enumerate([sibling] + chips):
            for i in range(n):
                _remote(src_refs[i], land_refs[i].at[_slab(mine)], send_sems.at[k * n + i], recv_sems.at[k * n + i],
                        to).start()
        token[...] = jnp.zeros_like(token)

    outs = pl.pallas_call(
        body, name=name,
        out_shape=(pltpu.SemaphoreType.DMA((4 * n,)), pltpu.SemaphoreType.DMA((4 * n,)),
                   *[pltpu.HBM(a.shape, a.dtype) for a in lands], _sds((SUBLANES, LANES))),
        in_specs=[_HBM] * (2 * n) + [pl.BlockSpec(memory_space=pl.ANY)],
        out_specs=(_SEM, _SEM, *[_HBM] * n, pl.BlockSpec(memory_space=pltpu.VMEM)),
        input_output_aliases={n + i: 2 + i for i in range(n)}, compiler_params=_EFFECT,
    )(*srcs, *lands, after)
    return (outs[0], outs[1], srcs, outs[2:2 + n]), outs[-1]


def _gather_forward(name, state, *after):
    send_a, recv_a, srcs, lands = state
    n = len(lands)

    def body(*refs):
        land_refs, recv_a_ref = refs[:n], refs[n]
        send_b, recv_b = refs[n + 1 + len(after)], refs[n + 2 + len(after)]
        mine, sibling, chips = _chip_routes()
        for j, chip in enumerate(chips):
            for i in range(n):
                block = land_refs[i].at[_slab(chip)]
                _remote(block, block, send_b.at[j * n + i], recv_a_ref.at[(1 + j) * n + i], chip).wait_recv()
                _remote(block, block, send_b.at[j * n + i], recv_b.at[j * n + i], sibling).start()

    outs = pl.pallas_call(
        body, name=name,
        out_shape=(pltpu.SemaphoreType.DMA((3 * n,)), pltpu.SemaphoreType.DMA((3 * n,)),
                   *[pltpu.HBM(a.shape, a.dtype) for a in lands]),
        in_specs=[_HBM] * n + [_SEM] + [pl.BlockSpec(memory_space=pl.ANY)] * len(after),
        out_specs=(_SEM, _SEM, *[_HBM] * n),
        input_output_aliases={i: 2 + i for i in range(n)}, compiler_params=_EFFECT,
    )(*lands, recv_a, *after)
    return (send_a, recv_a, srcs, list(outs[2:])), (outs[0], outs[1])


def _gather_wait(name, state, forwarded, *after):
    send_a, recv_a, srcs, lands = state
    send_b, recv_b = forwarded
    n = len(lands)

    def body(*refs):
        src_refs, land_refs = refs[:n], refs[n:2 * n]
        sa, ra, sb, rb = refs[2 * n:2 * n + 4]
        mine, sibling, chips = _chip_routes()
        for i in range(n):
            for k, to in enumerate([sibling] + chips):
                _remote(src_refs[i], land_refs[i].at[_slab(mine)], sa.at[k * n + i], ra.at[k * n + i], to).wait_send()
            theirs = land_refs[i].at[_slab(sibling)]
            _remote(theirs, theirs, sa.at[i], ra.at[i], sibling).wait_recv()
            for j, chip in enumerate(chips):
                sent = land_refs[i].at[_slab(chip)]
                got = land_refs[i].at[_slab((chip[0], chip[1], sibling[2]))]
                _remote(sent, sent, sb.at[j * n + i], rb.at[j * n + i], sibling).wait_send()
                _remote(got, got, sb.at[j * n + i], rb.at[j * n + i], sibling).wait_recv()

    outs = pl.pallas_call(
        body, name=name,
        out_shape=tuple(pltpu.HBM(a.shape, a.dtype) for a in lands),
        in_specs=[_HBM] * (2 * n) + [_SEM] * 4 + [pl.BlockSpec(memory_space=pl.ANY)] * len(after),
        out_specs=tuple([_HBM] * n),
        input_output_aliases={n + i: i for i in range(n)}, compiler_params=_EFFECT,
    )(*srcs, *lands, send_a, recv_a, send_b, recv_b, *after)
    return list(outs), list(srcs)


def _join_cols(parts, name, tr):
    _, r, c = parts.shape

    def body(p_ref, o_ref):
        for j in range(N_DEV):
            o_ref[:, j * c:(j + 1) * c] = p_ref[j]

    return _pcall(body, name, (r // tr,), [pl.BlockSpec((N_DEV, tr, c), lambda i: (0, i, 0))],
                  pl.BlockSpec((tr, N_DEV * c), lambda i: (i, 0)), _sds((r, N_DEV * c), parts.dtype))(parts)


def _split_cols(full, name, tr):
    r, c = full.shape[0], full.shape[1] // N_DEV

    def body(f_ref, o_ref):
        for j in range(N_DEV):
            o_ref[j] = f_ref[:, j * c:(j + 1) * c]

    return _pcall(body, name, (r // tr,), [pl.BlockSpec((tr, N_DEV * c), lambda i: (i, 0))],
                  pl.BlockSpec((N_DEV, tr, c), lambda i: (0, i, 0)), _sds((N_DEV, r, c), full.dtype))(full)


def _my_slab():
    return (4 * lax.axis_index("x") + 2 * lax.axis_index("y") + lax.axis_index("c")).astype(jnp.int32).reshape(1)


def _adamw(parts, sent, w, m, v, name, tile):
    _, rows, cols = w.shape

    def body(me_ref, p_ref, s_ref, w_ref, m_ref, v_ref, g_out, d_out, m_out, v_out):
        me = me_ref[0]
        g = jnp.where(me == 0, s_ref[0], p_ref[0]).astype(F32)
        for k in range(1, N_DEV):
            g = g + jnp.where(me == k, s_ref[0], p_ref[k]).astype(F32)
        m1 = ADAM_B1 * m_ref[0] + (1.0 - ADAM_B1) * g
        v1 = ADAM_B2 * v_ref[0] + (1.0 - ADAM_B2) * (g * g)
        m_hat = m1 / (1.0 - ADAM_B1 ** ADAM_STEP)
        v_hat = v1 / (1.0 - ADAM_B2 ** ADAM_STEP)
        g_out[0] = g
        d_out[0] = -ADAM_LR * (m_hat / (jnp.sqrt(v_hat) + ADAM_EPS) + ADAM_WD * w_ref[0])
        m_out[0] = m1
        v_out[0] = v1

    row = pl.BlockSpec((1, tile, cols), lambda i, me: (0, i, 0))
    return pl.pallas_call(
        body, name=name, out_shape=[_sds((1, rows, cols))] * 4,
        grid_spec=pltpu.PrefetchScalarGridSpec(
            num_scalar_prefetch=1, grid=(rows // tile,),
            in_specs=[pl.BlockSpec((N_DEV, tile, cols), lambda i, me: (0, i, 0)),
                      pl.BlockSpec((1, tile, cols), lambda i, me: (me[0], i, 0)), row, row, row],
            out_specs=[row, row, row, row]),
        compiler_params=pltpu.CompilerParams(dimension_semantics=("arbitrary",), vmem_limit_bytes=VMEM_LIMIT),
    )(_my_slab(), parts, sent, w, m, v)


BIG = {
    "w_in": ((N_IN // N_DEV, D_MODEL), False, N_IN // N_DEV // 3),
    "w_glu": ((S5_WIDTH // N_DEV, S5_WIDTH), False, S5_WIDTH // N_DEV),
    "w_pa": ((S5_WIDTH, D_MODEL // N_DEV), True, S5_WIDTH),
    "w_pb": ((HG_WIDTH, D_MODEL // N_DEV), True, HG_WIDTH),
    "w_out": ((D_MODEL // N_DEV, D_MODEL), False, D_MODEL // N_DEV),
    "w_up": ((2 * D_FF // N_DEV, D_MODEL), False, 2 * D_FF // N_DEV // 4),
    "w_conv": ((CONV_W, 2 * D_FF // N_DEV), True, CONV_W),
    "w_down": ((D_FF // N_DEV, D_MODEL), False, D_FF // N_DEV // 2),
}
TRANSPOSED = ("w_in", "w_up", "s5_b_re", "s5_b_im")
UNALIGNED_COLS = ("w_conv",)


def _stored(n, arr):
    return jnp.swapaxes(arr, -1, -2) if n in TRANSPOSED else arr


def _join_shards(n, parts):
    (a, b), by_cols, _ = BIG[n]
    if not by_cols:
        return parts.reshape(N_DEV * a, b)
    if n in UNALIGNED_COLS:
        return _join_cols(parts, "join_" + n, min(a, 256))
    return parts.transpose(1, 0, 2).reshape(a, N_DEV * b)


def _split_shards(n, full):
    (a, b), by_cols, _ = BIG[n]
    if not by_cols:
        return full.reshape(N_DEV, a, b)
    if n in UNALIGNED_COLS:
        return _split_cols(full, "split_" + n, min(a, 256))
    return full.reshape(a, N_DEV, b).transpose(1, 0, 2)


SMALL_CORE = {
    "s5_b_re": GSC, "s5_b_im": GSC, "s5_c_re": GSC, "s5_c_im": GSC,
    "g_mix": (1, D_MODEL), "g_ffn": (1, D_MODEL), "g_final": (1, D_MODEL), "s5_d": (1, S5_WIDTH),
    "b_glu": (1, S5_WIDTH), "hg_norm_gain": (1, HG_WIDTH), "hg_lb_logits": (2, HG_WIDTH), "b_conv": (1, 2 * D_FF),
    "s5_log_dt": (1, S5_GROUPS), "s5_a_re": (S5_GROUPS, S5_STATE), "s5_a_im": (S5_GROUPS, S5_STATE), "loss": (1, 1),
}
BLOCK_ROWS = 32


def _small_rows():
    rows, r = {}, 0
    for n, core in SMALL_CORE.items():
        rows[n] = r
        r += BLOCK_ROWS if len(core) == 3 else -(-math.prod(core) // PACK_W)
    return rows, -(-r // SUBLANES) * SUBLANES


SMALL_ROW, SMALL_ROWS = _small_rows()


def _small_pieces(name):
    r, core = SMALL_ROW[name], SMALL_CORE[name]
    if len(core) == 3:
        return [((g, slice(None), slice(None)), slice(r + S5_GROUP * (g % 2), r + S5_GROUP * (g % 2 + 1)),
                 slice(S5_STATE * (g // 2), S5_STATE * (g // 2 + 1))) for g in range(S5_GROUPS)]
    pieces = []
    for i in range(core[0]):
        for c0 in range(0, core[1], PACK_W):
            w, flat = min(PACK_W, core[1] - c0), i * core[1] + c0
            pieces.append(((slice(i, i + 1), slice(c0, c0 + w)), slice(r + flat // PACK_W, r + flat // PACK_W + 1),
                           slice(flat % PACK_W, flat % PACK_W + w)))
    return pieces


def _core_index(ref, name, idx):
    return (0,) * (len(ref.shape) - len(SMALL_CORE[name])) + idx


def _pack_small_grads(grads):
    names = list(SMALL_CORE)

    def body(*refs):
        pack = refs[-1]
        pack[...] = jnp.zeros_like(pack)
        for ref, n in zip(refs, names):
            for idx, rows, lanes in _small_pieces(n):
                pack[rows, lanes] = ref[_core_index(ref, n, idx)]

    return _pcall(body, "pack_small_grads", (1,), [_full(grads[n].shape) for n in names],
                  _full((SMALL_ROWS, PACK_W)), _sds((SMALL_ROWS, PACK_W)))(*[grads[n] for n in names])


def _adamw_small(parts, sent, names, rows, given, name):
    lo, hi = rows
    k = len(names)
    shapes = [given[n].shape for n in names]

    def body(*refs):
        me, p_ref, s_ref, ins, outs = refs[0][0], refs[1], refs[2], refs[3:3 + 3 * k], refs[3 + 3 * k:3 + 7 * k]
        packs, results = refs[3 + 7 * k:6 + 7 * k], refs[6 + 7 * k:]
        for j, pack in enumerate(packs):
            pack[...] = jnp.zeros_like(pack)
            for ref, n in zip(ins[j * k:(j + 1) * k], names):
                for idx, prow, lanes in _small_pieces(n):
                    pack[slice(prow.start - lo, prow.stop - lo), lanes] = ref[_core_index(ref, n, idx)]
        mine = s_ref[lo:hi, :]
        g = jnp.where(me == 0, mine, p_ref[0, lo:hi, :])
        for d in range(1, N_DEV):
            g = g + jnp.where(me == d, mine, p_ref[d, lo:hi, :])
        m1 = ADAM_B1 * packs[1][...] + (1.0 - ADAM_B1) * g
        v1 = ADAM_B2 * packs[2][...] + (1.0 - ADAM_B2) * (g * g)
        m_hat = m1 / (1.0 - ADAM_B1 ** ADAM_STEP)
        v_hat = v1 / (1.0 - ADAM_B2 ** ADAM_STEP)
        results[0][...] = g
        results[1][...] = -ADAM_LR * (m_hat / (jnp.sqrt(v_hat) + ADAM_EPS) + ADAM_WD * packs[0][...])
        results[2][...] = m1
        results[3][...] = v1
        for j, result in enumerate(results):
            for ref, n in zip(outs[j * k:(j + 1) * k], names):
                for idx, prow, lanes in _small_pieces(n):
                    ref[_core_index(ref, n, idx)] = result[slice(prow.start - lo, prow.stop - lo), lanes]

    flat = _pcall(body, name, (1,),
                  [pl.BlockSpec(memory_space=pltpu.SMEM), _full(parts.shape), _full(sent.shape)]
                  + [_full(s) for s in shapes] * 3,
                  [_full(s) for s in shapes] * 4, [_sds(s) for s in shapes] * 4,
                  scratch=[pltpu.VMEM((hi - lo, PACK_W), F32)] * 7,
                  )(_my_slab(), parts, sent, *[given[pre + n] for pre in ("", "m_", "v_") for n in names])
    return {n: [flat[j * k + i] for j in range(4)] for i, n in enumerate(names)}


def kernel(x, g_mix, w_in, s5_a_re, s5_a_im, s5_log_dt, s5_b_re, s5_b_im, s5_c_re, s5_c_im, s5_d, w_glu, b_glu, hg_lb_logits, hg_norm_gain, w_pa, w_pb, w_out, g_ffn, w_up, w_conv, b_conv, w_down, g_final, loss_target, m_g_mix, m_w_in, m_s5_a_re, m_s5_a_im, m_s5_log_dt, m_s5_b_re, m_s5_b_im, m_s5_c_re, m_s5_c_im, m_s5_d, m_w_glu, m_b_glu, m_hg_lb_logits, m_hg_norm_gain, m_w_pa, m_w_pb, m_w_out, m_g_ffn, m_w_up, m_w_conv, m_b_conv, m_w_down, m_g_final, v_g_mix, v_w_in, v_s5_a_re, v_s5_a_im, v_s5_log_dt, v_s5_b_re, v_s5_b_im, v_s5_c_re, v_s5_c_im, v_s5_d, v_w_glu, v_b_glu, v_hg_lb_logits, v_hg_norm_gain, v_w_pa, v_w_pb, v_w_out, v_g_ffn, v_w_up, v_w_conv, v_b_conv, v_w_down, v_g_final):
    given = dict(locals())
    small_names = [n for n in SMALL_CORE if n != "loss"]

    pay = {n: given[n][0] if n == "w_conv" else _stored(n, given[n])[0].astype(BF16) for n in BIG}
    groups = {"in": ["w_in"], "mix": ["w_glu", "w_pa", "w_pb", "w_out"], "ffn": ["w_up", "w_down", "w_conv"]}
    gathers, order = {}, pay["w_in"]
    for grp, names in groups.items():
        gathers[grp], order = _gather_start("gather_" + grp + "_start", [pay[n] for n in names], order)

    forwards = {}

    def forward(grp, *after):
        if grp == "in":
            after = (*after, order)
        forwards[grp] = _gather_forward("gather_" + grp + "_forward", gathers[grp], *after)

    def weights(grp, *after):
        if grp not in forwards:
            forward(grp, *after)
        got, _ = _gather_wait("gather_" + grp + "_wait", *forwards[grp], *after)
        return {n: _join_shards(n, g) for n, g in zip(groups[grp], got)}

    weights.forward = forward

    in_flight, started = [], []

    def emit(grads):
        names = list(grads)
        state, token = _exchange_start("grads_" + names[0] + "_start", [_split_shards(n, grads[n]) for n in names],
                                       grads[names[0]])
        in_flight.append((names, state))
        return token

    def emit_small(grads):
        pack = _pack_small_grads(grads)
        state, token = _gather_start("grads_small_start", [pack], pack)
        in_flight.append((["small"], state))
        started.append(token)

    sp = {n: (given[n] if n in ("g_final", "hg_lb_logits") else _stored(n, given[n])[0]) for n in small_names}
    sp["g_mix"] = _after(sp["g_mix"], order)
    dx = _local_step(x, loss_target, weights, sp, emit, emit_small)

    res = {}
    after = [started[-1]]
    for names, state in in_flight:
        if names == ["small"]:
            state, forwarded = _gather_forward("grads_small_forward", state, *after)
            parts, sent = _gather_wait("grads_small_wait", state, forwarded)
        else:
            parts, sent = _exchange_wait("grads_" + names[0] + "_wait", state, *after)
        if names != ["small"]:
            after = []
            for n, part, mine in zip(names, parts, sent):
                raw = _adamw(part, mine, *[_stored(n, given[pre + n]) for pre in ("", "m_", "v_")], "adamw_" + n,
                             BIG[n][2])
                res[n] = [_stored(n, r) for r in raw]
                after.append(raw[0])
            continue
        sgiven = {pre + n: _stored(n, given[pre + n]) for pre in ("", "m_", "v_") for n in small_names}
        for pre in ("", "m_", "v_"):
            sgiven[pre + "g_final"] = given[pre + "g_final"].reshape(1, D_MODEL)
            sgiven[pre + "loss"] = jnp.zeros((1, 1), F32)
        raw = _adamw_small(parts[0], sent[0], list(SMALL_CORE), (0, SMALL_ROWS), sgiven, "adamw_small")
        res.update({n: [_stored(n, r) for r in raw[n]] for n in small_names})
        res["g_final"] = [r.reshape(D_MODEL) for r in raw["g_final"]]
        total_loss = raw["loss"][0].reshape(())
        after = [raw["s5_b_re"][0], raw["g_mix"][0]]
    return (total_loss, dx, *[res[n][0] for n in WEIGHT_ORDER], *[res[n][1] for n in WEIGHT_ORDER],
            *[res[n][2] for n in WEIGHT_ORDER], *[res[n][3] for n in WEIGHT_ORDER])
```

```python
import math

import jax
import jax.numpy as jnp
from jax import lax
from jax.experimental import pallas as pl
from jax.experimental.pallas import tpu as pltpu

F32 = jnp.float32
BF16 = jnp.bfloat16

D_MODEL = 1024
S5_WIDTH = 512
S5_GROUP = 16
S5_GROUPS = 32
S5_STATE = 64
S5_N = S5_GROUPS * S5_STATE
HG_WIDTH = 512
HG_HEAD = 128
HG_HEADS = 4
D_FF = 2816
CONV_W = 3
CHUNK = 64
N_IN = S5_WIDTH + 4 * HG_WIDTH + 2 * D_MODEL
EPS = 1e-6
QSCALE = HG_HEAD ** -0.5

ADAM_LR = 0.001
ADAM_B1 = 0.9
ADAM_B2 = 0.999
ADAM_EPS = 1e-08
ADAM_WD = 0.01
ADAM_STEP = 10

N_DEV = 8
V7X_VMEM_BYTES = 64 * 1024 * 1024
VMEM_LIMIT = V7X_VMEM_BYTES * 7 // 8
SUBLANES = 8
LANES = 128
PACK_W = 1024

WEIGHT_ORDER = ("g_mix", "w_in", "s5_a_re", "s5_a_im", "s5_log_dt", "s5_b_re", "s5_b_im", "s5_c_re", "s5_c_im",
                "s5_d", "w_glu", "b_glu", "hg_lb_logits", "hg_norm_gain", "w_pa", "w_pb", "w_out", "g_ffn",
                "w_up", "w_conv", "b_conv", "w_down", "g_final")


def _pcall(body, name, grid, in_specs, out_specs, out_shape, scratch=()):
    return pl.pallas_call(
        body, name=name, grid=grid, in_specs=in_specs, out_specs=out_specs, out_shape=out_shape,
        scratch_shapes=list(scratch),
        compiler_params=pltpu.CompilerParams(dimension_semantics=("arbitrary",) * len(grid),
                                             vmem_limit_bytes=VMEM_LIMIT),
    )


def _full(shape):
    return pl.BlockSpec(shape, lambda *_: (0,) * len(shape))


def _sds(shape, dtype=F32):
    return jax.ShapeDtypeStruct(shape, dtype)


def _dot(a, b):
    return jnp.dot(a.astype(BF16), b.astype(BF16), preferred_element_type=F32)


def _dot_nt(a, b):
    return lax.dot_general(a.astype(BF16), b.astype(BF16), (((1,), (1,)), ((), ())), preferred_element_type=F32)


def _dot_tn(a, b):
    return lax.dot_general(a.astype(BF16), b.astype(BF16), (((0,), (0,)), ((), ())), preferred_element_type=F32)


def _sigmoid(x):
    return jax.nn.sigmoid(x)


GELU_C = math.sqrt(2.0 / math.pi)
GELU_A = 0.044715


def _gelu(x):
    return 0.5 * x * (1.0 + jnp.tanh(GELU_C * (x + GELU_A * (x * x * x))))


def _gelu_grad(x):
    t = jnp.tanh(GELU_C * (x + GELU_A * (x * x * x)))
    return 0.5 * (1.0 + t) + 0.5 * x * (1.0 - t * t) * (GELU_C * (1.0 + 3.0 * GELU_A * x * x))


def _cumsum_rows(v, reverse=False):
    n = v.shape[0]
    row = lax.broadcasted_iota(jnp.int32, v.shape, 0)
    s = 1
    while s < n:
        if reverse:
            v = v + jnp.where(row < n - s, pltpu.roll(v, n - s, axis=0), 0.0)
        else:
            v = v + jnp.where(row >= s, pltpu.roll(v, s, axis=0), 0.0)
        s *= 2
    return v


def _token_tile(seq):
    return min(256, seq)


def _s5_coeffs(a_re, a_im, ldt):
    dt = jnp.exp(ldt)
    mag = jnp.exp(a_re * dt)
    ang = a_im * dt
    lb_re = mag * jnp.cos(ang)
    lb_im = mag * jnp.sin(ang)
    den = a_re * a_re + a_im * a_im
    n_re = lb_re - 1.0
    n_im = lb_im
    co_re = (n_re * a_re + n_im * a_im) / den
    co_im = (n_im * a_re - n_re * a_im) / den
    return lb_re, lb_im, co_re, co_im


GS, GSC = (S5_GROUPS, S5_STATE), (S5_GROUPS, S5_GROUP, S5_STATE)


def _params_fwd(a_re, a_im, ldt, bt_re, bt_im, logits):
    def body(are, aim, ld, bre, bim, lg, lr_o, li_o, bbr_o, bbi_o, lb_o):
        lr, li, co_re, co_im = _s5_coeffs(are[...], aim[...], ld[...])
        lr_o[...] = lr
        li_o[...] = li
        for g in range(S5_GROUPS):
            cr, ci = co_re[g:g + 1, :], co_im[g:g + 1, :]
            bbr_o[g] = cr * bre[g] - ci * bim[g]
            bbi_o[g] = cr * bim[g] + ci * bre[g]
        lb_o[...] = _sigmoid(lg[0:1, :] - lg[1:2, :])

    return _pcall(body, "params_fwd", (1,),
                  [_full(GS), _full(GS), _full((S5_GROUPS, 1)), _full(GSC), _full(GSC), _full((2, HG_WIDTH))],
                  [_full(GS), _full(GS), _full(GSC), _full(GSC), _full((1, HG_WIDTH))],
                  [_sds(GS), _sds(GS), _sds(GSC), _sds(GSC), _sds((1, HG_WIDTH))],
                  )(a_re, a_im, ldt, bt_re, bt_im, logits)


def _params_bwd(a_re, a_im, ldt, bt_re, bt_im, logits, dlr, dli, dbbr, dbbi, dlb):
    def body(are, aim, ld, bre, bim, lg, dlr_r, dli_r, dbbr_r, dbbi_r, dlb_r,
             dare_o, daim_o, dld_o, dbre_o, dbim_o, dlg_o, dcr_ref, dci_ref):
        (_, _, co_re, co_im), vjp = jax.vjp(_s5_coeffs, are[...], aim[...], ld[...])
        for g in range(S5_GROUPS):
            cr, ci = co_re[g:g + 1, :], co_im[g:g + 1, :]
            gr, gi, br, bi = dbbr_r[g], dbbi_r[g], bre[g], bim[g]
            dbre_o[g] = cr * gr + ci * gi
            dbim_o[g] = cr * gi - ci * gr
            dcr_ref[g:g + 1, :] = jnp.sum(gr * br + gi * bi, axis=0, keepdims=True)
            dci_ref[g:g + 1, :] = jnp.sum(gi * br - gr * bi, axis=0, keepdims=True)
        dare, daim, dld = vjp((dlr_r[...], dli_r[...], dcr_ref[...], dci_ref[...]))
        dare_o[...] = dare
        daim_o[...] = daim
        dld_o[...] = dld
        lb = _sigmoid(lg[0:1, :] - lg[1:2, :])
        d0 = dlb_r[...] * lb * (1.0 - lb)
        dlg_o[0:1, :] = d0
        dlg_o[1:2, :] = -d0

    return _pcall(body, "params_bwd", (1,),
                  [_full(GS), _full(GS), _full((S5_GROUPS, 1)), _full(GSC), _full(GSC), _full((2, HG_WIDTH)),
                   _full(GS), _full(GS), _full(GSC), _full(GSC), _full((1, HG_WIDTH))],
                  [_full(GS), _full(GS), _full((S5_GROUPS, 1)), _full(GSC), _full(GSC), _full((2, HG_WIDTH))],
                  [_sds(GS), _sds(GS), _sds((S5_GROUPS, 1)), _sds(GSC), _sds(GSC), _sds((2, HG_WIDTH))],
                  scratch=[pltpu.VMEM(GS, F32), pltpu.VMEM(GS, F32)],
                  )(a_re, a_im, ldt, bt_re, bt_im, logits, dlr, dli, dbbr, dbbi, dlb)


def _band_blocks(m):
    g, r, c = m.shape
    gb = g // S5_BANDS
    m4 = m.astype(BF16).reshape(S5_BANDS, gb, r, c)
    on_diag = jnp.eye(gb, dtype=bool)[None, :, None, :, None]
    return jnp.where(on_diag, m4[:, :, :, None, :], 0).reshape(S5_BANDS, gb * r, gb * c)


def _diag_blocks(band, r, c):
    g, nb = band.shape[0] // r, band.shape[1] // c
    on_diag = (jnp.arange(g) % nb)[:, None, None, None] == jnp.arange(nb)[None, None, :, None]
    return jnp.sum(jnp.where(on_diag, band.reshape(g, r, nb, c), 0.0), axis=2)


def _in_proj(x, g_mix, w_in, tm):
    t = x.shape[0]

    def body(x_ref, g_ref, w_ref, u_ref, za_ref, zh_ref, zg_ref):
        xv = x_ref[...]
        r = lax.rsqrt(jnp.mean(xv * xv, axis=-1, keepdims=True) + EPS)
        u = (xv * r * g_ref[...]).astype(BF16)
        u_ref[...] = u
        za_ref[...] = _dot_nt(u, w_ref[0:S5_WIDTH, :])
        zh_ref[...] = _dot_nt(u, w_ref[S5_WIDTH:S5_WIDTH + 4 * HG_WIDTH, :])
        zg_ref[...] = _dot_nt(u, w_ref[S5_WIDTH + 4 * HG_WIDTH:, :]).astype(BF16)

    row = lambda w: pl.BlockSpec((tm, w), lambda i: (i, 0))
    return _pcall(body, "in_proj", (t // tm,),
                  [row(D_MODEL), _full((1, D_MODEL)), _full((N_IN, D_MODEL))],
                  [row(D_MODEL), row(S5_WIDTH), row(4 * HG_WIDTH), row(2 * D_MODEL)],
                  [_sds((t, D_MODEL), BF16), _sds((t, S5_WIDTH)), _sds((t, 4 * HG_WIDTH)),
                   _sds((t, 2 * D_MODEL), BF16)],
                  )(x, g_mix, w_in)


S5_LANES = 512
S5_BANDS = 4


def _band(q):
    return (slice(q * S5_WIDTH // S5_BANDS, (q + 1) * S5_WIDTH // S5_BANDS),
            slice(q * S5_N // S5_BANDS, (q + 1) * S5_N // S5_BANDS))


def _im(st):
    return slice(S5_N + st.start, S5_N + st.stop)


SCAN_UNROLL = 8


def _complex_scan(buf_ref, lam_ref, st_ref, nb, ts, reverse):
    lanes = [slice(cc * S5_LANES, (cc + 1) * S5_LANES) for cc in range(S5_N // S5_LANES)]
    chains = [(b, re) for b in range(nb) for re in lanes]
    nch = len(chains)
    wr = {re.start: lam_ref[0:1, re] for re in lanes}
    wi = {re.start: -lam_ref[1:2, re] if reverse else lam_ref[1:2, re] for re in lanes}

    def block(ib, carry):
        vr, vi = list(carry[:nch]), list(carry[nch:])
        first = ts - SCAN_UNROLL - ib * SCAN_UNROLL if reverse else ib * SCAN_UNROLL
        first = pl.multiple_of(first, SCAN_UNROLL)
        for k in range(SCAN_UNROLL):
            row = pl.ds(first + (SCAN_UNROLL - 1 - k if reverse else k), 1)
            for c, (b, re) in enumerate(chains):
                nr = wr[re.start] * vr[c] - wi[re.start] * vi[c] + buf_ref[b, row, re]
                ni = wr[re.start] * vi[c] + wi[re.start] * vr[c] + buf_ref[b, row, _im(re)]
                buf_ref[b, row, re] = nr
                buf_ref[b, row, _im(re)] = ni
                vr[c], vi[c] = nr, ni
        return tuple(vr + vi)

    init = tuple(st_ref[b, 0:1, re] for b, re in chains) + tuple(st_ref[b, 1:2, re] for b, re in chains)
    last = lax.fori_loop(0, ts // SCAN_UNROLL, block, init)
    for c, (b, re) in enumerate(chains):
        st_ref[b, 0:1, re] = last[c]
        st_ref[b, 1:2, re] = last[nch + c]


BAND_CH = S5_WIDTH // S5_BANDS
BAND_ST = S5_N // S5_BANDS


def _s5_fwd(za, b_bands, lam, c_bands, dskip, nb, seq, ts):
    nts = seq // ts

    def body(za_ref, br_ref, bi_ref, lam_ref, cr_ref, ci_ref, d_ref, xs_ref, y_ref, buf_ref, st_ref):
        @pl.when(pl.program_id(0) == 0)
        def _():
            st_ref[...] = jnp.zeros_like(st_ref)

        for b in range(nb):
            zav = za_ref[b]
            for q in range(S5_BANDS):
                ch, st = _band(q)
                buf_ref[b, :, st] = _dot(zav[:, ch], br_ref[q])
                buf_ref[b, :, _im(st)] = _dot(zav[:, ch], bi_ref[q])
        _complex_scan(buf_ref, lam_ref, st_ref, nb, ts, reverse=False)
        for b in range(nb):
            zav = za_ref[b]
            xs_ref[b] = buf_ref[b].astype(BF16)
            for q in range(S5_BANDS):
                ch, st = _band(q)
                y_ref[b, :, ch] = (_dot(xs_ref[b, :, st], cr_ref[q]) + _dot(xs_ref[b, :, _im(st)], ci_ref[q])
                                   + d_ref[:, ch] * zav[:, ch])

    tok = lambda w: pl.BlockSpec((nb, ts, w), lambda j: (0, j, 0))
    to_st, to_ch = _full((S5_BANDS, BAND_CH, BAND_ST)), _full((S5_BANDS, BAND_ST, BAND_CH))
    return _pcall(body, "s5_fwd", (nts,),
                  [tok(S5_WIDTH), to_st, to_st, _full((2, S5_N)), to_ch, to_ch, _full((1, S5_WIDTH))],
                  [tok(2 * S5_N), tok(S5_WIDTH)],
                  [_sds((nb, seq, 2 * S5_N), BF16), _sds((nb, seq, S5_WIDTH))],
                  scratch=[pltpu.VMEM((nb, ts, 2 * S5_N), F32), pltpu.VMEM((nb, 2, S5_N), F32)],
                  )(za, *b_bands, lam, *c_bands, dskip)


def _hgrn_gates(zq, zf, lbh):
    sf = _sigmoid(zf)
    f = lbh + (1.0 - lbh) * sf
    sq = _sigmoid(zq)
    qa = zq * sq * QSCALE
    bc = _cumsum_rows(jnp.log(f))
    bm = bc[CHUNK // 2 - 1:CHUNK // 2, :]
    bl = bc[CHUNK - 1:CHUNK, :]
    return sf, f, sq, qa, bc, bm, bl


HG_CHUNKS_PER_STEP = 4


def _hgrn_fwd(zh, lb, nb, seq):
    nc = seq // CHUNK
    cps = HG_CHUNKS_PER_STEP

    def body(zh_ref, lb_ref, o_ref, sts_ref, st_ref):
        @pl.when(pl.program_id(0) == 0)
        def _():
            st_ref[...] = jnp.zeros_like(st_ref)

        causal = (lax.broadcasted_iota(jnp.int32, (CHUNK, CHUNK), 0)
                  >= lax.broadcasted_iota(jnp.int32, (CHUNK, CHUNK), 1))
        for cc in range(cps):
            rows = slice(cc * CHUNK, (cc + 1) * CHUNK)
            for b in range(nb):
                for h in range(HG_HEADS):
                    hs = slice(h * HG_HEAD, (h + 1) * HG_HEAD)
                    zq = zh_ref[b, rows, h * HG_HEAD:(h + 1) * HG_HEAD]
                    zf = zh_ref[b, rows, HG_WIDTH + h * HG_HEAD:HG_WIDTH + (h + 1) * HG_HEAD]
                    zi = zh_ref[b, rows, 2 * HG_WIDTH + h * HG_HEAD:2 * HG_WIDTH + (h + 1) * HG_HEAD]
                    _, f, _, qa, bc, bm, bl = _hgrn_gates(zq, zf, lb_ref[:, hs])
                    k = 1.0 - f
                    qt = qa * jnp.exp(bc - bm)
                    kt = k * jnp.exp(bm - bc)
                    qb = qa * jnp.exp(bc)
                    kd = k * jnp.exp(bl - bc)
                    st = st_ref[b, h]
                    sts_ref[b, cc, h] = st
                    a = jnp.where(causal, _dot_nt(qt, kt), 0.0)
                    o_ref[b, rows, hs] = _dot(a, zi) + _dot_nt(qb, st)
                    st_ref[b, h] = st * jnp.exp(bl) + _dot_tn(zi, kd)

    return _pcall(body, "hgrn_fwd", (nc // cps,),
                  [pl.BlockSpec((nb, cps * CHUNK, 4 * HG_WIDTH), lambda c: (0, c, 0)), _full((1, HG_WIDTH))],
                  [pl.BlockSpec((nb, cps * CHUNK, HG_WIDTH), lambda c: (0, c, 0)),
                   pl.BlockSpec((nb, cps, HG_HEADS, HG_HEAD, HG_HEAD), lambda c: (0, c, 0, 0, 0))],
                  [_sds((nb, seq, HG_WIDTH)), _sds((nb, nc, HG_HEADS, HG_HEAD, HG_HEAD))],
                  scratch=[pltpu.VMEM((nb, HG_HEADS, HG_HEAD, HG_HEAD), F32)])(zh, lb)


def _head_rms(o):
    parts = []
    for h in range(HG_HEADS):
        oh = o[:, h * HG_HEAD:(h + 1) * HG_HEAD]
        r = lax.rsqrt(jnp.mean(oh * oh, axis=-1, keepdims=True) + EPS)
        parts.append(jnp.broadcast_to(r, oh.shape))
    return jnp.concatenate(parts, axis=1)


def _head_mean(v):
    parts = []
    for h in range(HG_HEADS):
        vh = v[:, h * HG_HEAD:(h + 1) * HG_HEAD]
        parts.append(jnp.broadcast_to(jnp.mean(vh, axis=-1, keepdims=True), vh.shape))
    return jnp.concatenate(parts, axis=1)


def _mix_fwd(x, y0, o, zh, zgt, w_glu, b_glu, gain, w_pa, w_pb, w_out, g_ffn, tm):
    t = x.shape[0]

    def body(x_ref, y0_ref, o_ref, zg_ref, zgt_ref, wglu_ref, bglu_ref, gain_ref, wpa_ref, wpb_ref, wout_ref,
             gffn_ref, x1_ref, u2_ref, pa_ref, pb_ref, ya2_ref, yb_ref):
        ya1 = _gelu(y0_ref[...])
        s = _sigmoid(_dot(ya1, wglu_ref[...]) + bglu_ref[...])
        ya2 = (ya1 * s).astype(BF16)
        ov = o_ref[...]
        zg = zg_ref[...]
        yb = (ov * _head_rms(ov) * gain_ref[...] * (zg * _sigmoid(zg))).astype(BF16)
        ya2_ref[...] = ya2
        yb_ref[...] = yb
        pa = jnp.dot(ya2, wpa_ref[...], preferred_element_type=F32)
        pb = jnp.dot(yb, wpb_ref[...], preferred_element_type=F32)
        pa_ref[...] = pa.astype(BF16)
        pb_ref[...] = pb.astype(BF16)
        m = (_sigmoid(zgt_ref[:, 0:D_MODEL].astype(F32)) * pa
             + _sigmoid(zgt_ref[:, D_MODEL:].astype(F32)) * pb)
        x1 = x_ref[...] + _dot(m, wout_ref[...])
        x1_ref[...] = x1
        r = lax.rsqrt(jnp.mean(x1 * x1, axis=-1, keepdims=True) + EPS)
        u2_ref[...] = (x1 * r * gffn_ref[...]).astype(BF16)

    row = lambda w: pl.BlockSpec((tm, w), lambda i: (i, 0))
    return _pcall(body, "mix_fwd", (t // tm,),
                  [row(D_MODEL), row(S5_WIDTH), row(HG_WIDTH), pl.BlockSpec((tm, HG_WIDTH), lambda i: (i, 3)),
                   row(2 * D_MODEL), _full((S5_WIDTH, S5_WIDTH)), _full((1, S5_WIDTH)), _full((1, HG_WIDTH)),
                   _full((S5_WIDTH, D_MODEL)), _full((HG_WIDTH, D_MODEL)), _full((D_MODEL, D_MODEL)),
                   _full((1, D_MODEL))],
                  [row(D_MODEL), row(D_MODEL), row(D_MODEL), row(D_MODEL), row(S5_WIDTH), row(HG_WIDTH)],
                  [_sds((t, D_MODEL)), _sds((t, D_MODEL), BF16), _sds((t, D_MODEL), BF16), _sds((t, D_MODEL), BF16),
                   _sds((t, S5_WIDTH), BF16), _sds((t, HG_WIDTH), BF16)],
                  )(x, y0, o, zh, zgt, w_glu, b_glu, gain, w_pa, w_pb, w_out, g_ffn)


FF_COLS = 256
FF_UP_TILE = 2 * D_FF // 2


def _ffn_up(u2, w_up, tm):
    t = u2.shape[0]
    n = 2 * D_FF

    def body(u_ref, w_ref, h_ref):
        h_ref[...] = _dot_nt(u_ref[...], w_ref[...]).astype(BF16)

    return _pcall(body, "ffn_up", (n // FF_UP_TILE, t // tm),
                  [pl.BlockSpec((tm, D_MODEL), lambda j, i: (i, 0)),
                   pl.BlockSpec((FF_UP_TILE, D_MODEL), lambda j, i: (j, 0))],
                  pl.BlockSpec((tm, FF_UP_TILE), lambda j, i: (i, j)),
                  _sds((t, n), BF16))(u2, w_up)


HALO = 16


def _shift_matrix(tm):
    r = lax.broadcasted_iota(jnp.int32, (tm, tm), 0)
    c = lax.broadcasted_iota(jnp.int32, (tm, tm), 1)
    return jnp.where(r == c + 1, 1.0, 0.0).astype(BF16)


def _conv_cols(h_ref, halo_ref, valid, wc_ref, bc_ref, c0):
    cs = slice(c0, c0 + FF_COLS)
    cur = h_ref[:, cs].astype(F32)
    prev = jnp.where(valid, halo_ref[:, cs].astype(F32), 0.0)
    full = jnp.concatenate([prev, cur], axis=0)
    h1 = pltpu.roll(full, 1, axis=0)[HALO:]
    h2 = pltpu.roll(full, 2, axis=0)[HALO:]
    return h2 * wc_ref[0:1, cs] + h1 * wc_ref[1:2, cs] + cur * wc_ref[2:3, cs] + bc_ref[:, cs]


def _ffn_down_loss(h, x1, tgt, w_conv, b_conv, w_down, g_final, seq, tm):
    t = h.shape[0]
    tps = seq // tm
    n = 2 * D_FF

    def body(h_ref, halo_ref, x1_ref, tgt_ref, wc_ref, bc_ref, wd_ref, gf_ref,
             hc_ref, a_ref, dx2_ref, dx2b_ref, loss_ref, dgf_ref):
        i = pl.program_id(0)

        @pl.when(i == 0)
        def _():
            loss_ref[...] = jnp.zeros_like(loss_ref)
            dgf_ref[...] = jnp.zeros_like(dgf_ref)

        valid = (i % tps) != 0
        x2 = x1_ref[...]
        for j in range(D_FF // FF_COLS):
            gate = _conv_cols(h_ref, halo_ref, valid, wc_ref, bc_ref, j * FF_COLS)
            val = _conv_cols(h_ref, halo_ref, valid, wc_ref, bc_ref, D_FF + j * FF_COLS)
            hc_ref[:, j * FF_COLS:(j + 1) * FF_COLS] = gate.astype(BF16)
            hc_ref[:, D_FF + j * FF_COLS:D_FF + (j + 1) * FF_COLS] = val.astype(BF16)
            a = (gate * _sigmoid(gate) * val).astype(BF16)
            a_ref[:, j * FF_COLS:(j + 1) * FF_COLS] = a
            x2 = x2 + jnp.dot(a, wd_ref[j * FF_COLS:(j + 1) * FF_COLS, :], preferred_element_type=F32)
        r = lax.rsqrt(jnp.mean(x2 * x2, axis=-1, keepdims=True) + EPS)
        xn = x2 * r
        g = gf_ref[...]
        e = xn * g - tgt_ref[...]
        loss_ref[...] += (0.5 / D_MODEL) * jnp.sum(e * e).reshape(1, 1)
        dy = e * (1.0 / D_MODEL)
        dgf_ref[...] += jnp.sum(dy * xn, axis=0, keepdims=True)
        dxn = dy * g
        dx2 = r * (dxn - xn * jnp.mean(dxn * xn, axis=-1, keepdims=True))
        dx2_ref[...] = dx2
        dx2b_ref[...] = dx2.astype(BF16)

    row = lambda w: pl.BlockSpec((tm, w), lambda i: (i, 0))
    halo = pl.BlockSpec((HALO, n), lambda i: (jnp.maximum(i * (tm // HALO) - 1, 0), 0))
    return _pcall(body, "ffn_down_loss", (t // tm,),
                  [row(n), halo, row(D_MODEL), row(D_MODEL), _full((CONV_W, n)), _full((1, n)),
                   _full((D_FF, D_MODEL)), _full((1, D_MODEL))],
                  [row(n), row(D_FF), row(D_MODEL), row(D_MODEL), _full((1, 1)), _full((1, D_MODEL))],
                  [_sds((t, n), BF16), _sds((t, D_FF), BF16), _sds((t, D_MODEL)), _sds((t, D_MODEL), BF16),
                   _sds((1, 1)), _sds((1, D_MODEL))],
                  )(h, h, x1, tgt, w_conv, b_conv, w_down, g_final)


def _wgrad(a, b, name, tn, out_dtype=F32, band=None, after=None):
    t, m = a.shape
    n = b.shape[1] if band is None else band
    nbands = 1 if band is None else b.shape[1] // band
    after = b if after is None else after

    def body(a_ref, b_ref, after_ref, o_ref):
        o_ref[...] = _dot_tn(a_ref[...], b_ref[...]).astype(out_dtype)

    return _pcall(body, name, (m // tn,),
                  [pl.BlockSpec((t, tn), lambda i: (0, i)), pl.BlockSpec((t, n), lambda i: (0, i % nbands)),
                   pl.BlockSpec(memory_space=pl.ANY)],
                  pl.BlockSpec((tn, n), lambda i: (i, 0)), _sds((m, n), out_dtype))(a, b, after)


def _ffn_bwd_act(dx2b, hc, w_down, tm):
    t = hc.shape[0]
    n = 2 * D_FF

    def body(dx2_ref, hc_ref, wd_ref, dhc_ref, dbc_ref):
        @pl.when(pl.program_id(0) == 0)
        def _():
            dbc_ref[...] = jnp.zeros_like(dbc_ref)

        dx2 = dx2_ref[...]
        for j in range(D_FF // FF_COLS):
            gs = slice(j * FF_COLS, (j + 1) * FF_COLS)
            vs = slice(D_FF + j * FF_COLS, D_FF + (j + 1) * FF_COLS)
            gate = hc_ref[:, gs].astype(F32)
            val = hc_ref[:, vs].astype(F32)
            da = _dot_nt(dx2, wd_ref[gs, :])
            sg = _sigmoid(gate)
            dgate = da * val * (sg * (1.0 + gate * (1.0 - sg)))
            dval = da * (gate * sg)
            dhc_ref[:, gs] = dgate.astype(BF16)
            dhc_ref[:, vs] = dval.astype(BF16)
            dbc_ref[:, gs] += jnp.sum(dgate, axis=0, keepdims=True)
            dbc_ref[:, vs] += jnp.sum(dval, axis=0, keepdims=True)

    row = lambda w: pl.BlockSpec((tm, w), lambda i: (i, 0))
    return _pcall(body, "ffn_bwd_act", (t // tm,),
                  [row(D_MODEL), row(n), _full((D_FF, D_MODEL))],
                  [row(n), _full((1, n))],
                  [_sds((t, n), BF16), _sds((1, n))],
                  )(dx2b, hc, w_down)


def _ffn_bwd_up(dhc, h, dx2, x1, w_conv, w_up, g_ffn, seq, tm):
    t = dhc.shape[0]
    tps = seq // tm
    n = 2 * D_FF
    last = t // HALO - 1

    def body(dhc_ref, halo_ref, h_ref, dx2_ref, x1_ref, wc_ref, wu_ref, gf_ref,
             dh_ref, dx1_ref, dx1b_ref, dgf_ref, dwc_ref):
        i = pl.program_id(0)

        @pl.when(i == 0)
        def _():
            dgf_ref[...] = jnp.zeros_like(dgf_ref)
            dwc_ref[...] = jnp.zeros_like(dwc_ref)

        valid = ((i + 1) % tps) != 0
        du2 = jnp.zeros((tm, D_MODEL), F32)
        for j in range(n // FF_COLS):
            cs = slice(j * FF_COLS, (j + 1) * FF_COLS)
            cur = dhc_ref[:, cs].astype(F32)
            nxt = jnp.where(valid, halo_ref[:, cs].astype(F32), 0.0)
            full = jnp.concatenate([cur, nxt], axis=0)
            d1 = pltpu.roll(full, tm + HALO - 1, axis=0)[:tm]
            d2 = pltpu.roll(full, tm + HALO - 2, axis=0)[:tm]
            dh = (cur * wc_ref[2:3, cs] + d1 * wc_ref[1:2, cs] + d2 * wc_ref[0:1, cs]).astype(BF16)
            dh_ref[:, cs] = dh
            du2 = du2 + _dot(dh, wu_ref[cs, :])
            hv = h_ref[:, cs].astype(F32)
            dwc_ref[0:1, cs] += jnp.sum(hv * d2, axis=0, keepdims=True)
            dwc_ref[1:2, cs] += jnp.sum(hv * d1, axis=0, keepdims=True)
            dwc_ref[2:3, cs] += jnp.sum(hv * cur, axis=0, keepdims=True)
        x1 = x1_ref[...]
        r = lax.rsqrt(jnp.mean(x1 * x1, axis=-1, keepdims=True) + EPS)
        xn = x1 * r
        dgf_ref[...] += jnp.sum(du2 * xn, axis=0, keepdims=True)
        dxn = du2 * gf_ref[...]
        dx1 = dx2_ref[...] + r * (dxn - xn * jnp.mean(dxn * xn, axis=-1, keepdims=True))
        dx1_ref[...] = dx1
        dx1b_ref[...] = dx1.astype(BF16)

    row = lambda w: pl.BlockSpec((tm, w), lambda i: (i, 0))
    halo = pl.BlockSpec((HALO, n), lambda i: (jnp.minimum((i + 1) * (tm // HALO), last), 0))
    return _pcall(body, "ffn_bwd_up", (t // tm,),
                  [row(n), halo, row(n), row(D_MODEL), row(D_MODEL), _full((CONV_W, n)), _full((n, D_MODEL)),
                   _full((1, D_MODEL))],
                  [row(n), row(D_MODEL), row(D_MODEL), _full((1, D_MODEL)), _full((CONV_W, n))],
                  [_sds((t, n), BF16), _sds((t, D_MODEL)), _sds((t, D_MODEL), BF16), _sds((1, D_MODEL)),
                   _sds((CONV_W, n))],
                  )(dhc, dhc, h, dx2, x1, w_conv, w_up, g_ffn)


def _mix_bwd(dx1, y0, o, zh, zgt, pa, pb, w_glu, b_glu, gain, w_pa, w_pb, w_out, tm):
    t = dx1.shape[0]

    def body(dx1_ref, y0_ref, o_ref, zg_ref, zgt_ref, pa_ref, pb_ref, wglu_ref, bglu_ref, gain_ref, wpa_ref,
             wpb_ref, wout_ref,
             dy0_ref, do_ref, dzg_ref, dzgt_ref, m_ref, dpa_ref, dpb_ref, ya1_ref, dpre_ref, dbglu_ref, dgain_ref):
        @pl.when(pl.program_id(0) == 0)
        def _():
            dbglu_ref[...] = jnp.zeros_like(dbglu_ref)
            dgain_ref[...] = jnp.zeros_like(dgain_ref)

        dm = _dot_nt(dx1_ref[...], wout_ref[...])
        sga = _sigmoid(zgt_ref[:, 0:D_MODEL].astype(F32))
        sgb = _sigmoid(zgt_ref[:, D_MODEL:].astype(F32))
        pa = pa_ref[...].astype(F32)
        pb = pb_ref[...].astype(F32)
        m_ref[...] = (sga * pa + sgb * pb).astype(BF16)
        dzgt_ref[:, 0:D_MODEL] = (dm * pa * sga * (1.0 - sga)).astype(BF16)
        dzgt_ref[:, D_MODEL:] = (dm * pb * sgb * (1.0 - sgb)).astype(BF16)
        dpa = (dm * sga).astype(BF16)
        dpb = (dm * sgb).astype(BF16)
        dpa_ref[...] = dpa
        dpb_ref[...] = dpb
        dya2 = _dot_nt(dpa, wpa_ref[...])
        dyb = _dot_nt(dpb, wpb_ref[...])
        y0 = y0_ref[...]
        ya1 = _gelu(y0)
        ya1_ref[...] = ya1.astype(BF16)
        s = _sigmoid(_dot(ya1, wglu_ref[...]) + bglu_ref[...])
        dpre = dya2 * ya1 * s * (1.0 - s)
        dpre_ref[...] = dpre.astype(BF16)
        dbglu_ref[...] += jnp.sum(dpre, axis=0, keepdims=True)
        dya1 = dya2 * s + _dot_nt(dpre, wglu_ref[...])
        dy0_ref[...] = dya1 * _gelu_grad(y0)
        ov = o_ref[...]
        zg = zg_ref[...]
        oh = ov * _head_rms(ov)
        on = oh * gain_ref[...]
        sz = _sigmoid(zg)
        dzg_ref[...] = (dyb * on * (sz * (1.0 + zg * (1.0 - sz)))).astype(BF16)
        don = dyb * (zg * sz)
        dgain_ref[...] += jnp.sum(don * oh, axis=0, keepdims=True)
        doh = don * gain_ref[...]
        do_ref[...] = _head_rms(ov) * (doh - oh * _head_mean(doh * oh))

    row = lambda w: pl.BlockSpec((tm, w), lambda i: (i, 0))
    return _pcall(body, "mix_bwd", (t // tm,),
                  [row(D_MODEL), row(S5_WIDTH), row(HG_WIDTH), pl.BlockSpec((tm, HG_WIDTH), lambda i: (i, 3)),
                   row(2 * D_MODEL), row(D_MODEL), row(D_MODEL), _full((S5_WIDTH, S5_WIDTH)), _full((1, S5_WIDTH)),
                   _full((1, HG_WIDTH)), _full((S5_WIDTH, D_MODEL)), _full((HG_WIDTH, D_MODEL)),
                   _full((D_MODEL, D_MODEL))],
                  [row(S5_WIDTH), row(HG_WIDTH), row(HG_WIDTH), row(2 * D_MODEL), row(D_MODEL), row(D_MODEL),
                   row(D_MODEL), row(S5_WIDTH), row(S5_WIDTH), _full((1, S5_WIDTH)), _full((1, HG_WIDTH))],
                  [_sds((t, S5_WIDTH)), _sds((t, HG_WIDTH)), _sds((t, HG_WIDTH), BF16), _sds((t, 2 * D_MODEL), BF16),
                   _sds((t, D_MODEL), BF16), _sds((t, D_MODEL), BF16), _sds((t, D_MODEL), BF16),
                   _sds((t, S5_WIDTH), BF16), _sds((t, S5_WIDTH), BF16), _sds((1, S5_WIDTH)), _sds((1, HG_WIDTH))],
                  )(dx1, y0, o, zh, zgt, pa, pb, w_glu, b_glu, gain, w_pa, w_pb, w_out)


def _s5_bwd(dy0, za, xs, c_bands, b_bands, lam, dskip, nb, seq, ts):
    nts = seq // ts

    def body(dy0_ref, za_ref, xs_ref, halo_ref, cr_ref, ci_ref, br_ref, bi_ref, lam_ref, d_ref,
             dza_ref, a_ref, dlam_ref, dd_ref, acc_ref, st_ref):
        j = pl.program_id(0)

        @pl.when(j == 0)
        def _():
            dlam_ref[...] = jnp.zeros_like(dlam_ref)
            dd_ref[...] = jnp.zeros_like(dd_ref)
            st_ref[...] = jnp.zeros_like(st_ref)

        for b in range(nb):
            dy0 = dy0_ref[b]
            for q in range(S5_BANDS):
                ch, st = _band(q)
                acc_ref[b, :, st] = _dot(dy0[:, ch], cr_ref[q])
                acc_ref[b, :, _im(st)] = _dot(dy0[:, ch], ci_ref[q])
        _complex_scan(acc_ref, lam_ref, st_ref, nb, ts, reverse=True)
        shift = _shift_matrix(ts)
        top = lax.broadcasted_iota(jnp.int32, (SUBLANES, S5_LANES), 0) == 0
        for b in range(nb):
            a_ref[b] = acc_ref[b].astype(BF16)
            first = jnp.where(j == nts - 1, 0.0, halo_ref[b, HALO - 1:HALO, :].astype(F32))

            def shifted(cols):
                xp = jnp.dot(shift, xs_ref[b, :, cols], preferred_element_type=F32)
                return jnp.concatenate([xp[:SUBLANES] + jnp.where(top, first[:, cols], 0.0), xp[SUBLANES:]], axis=0)

            for cc in range(S5_N // S5_LANES):
                re = slice(cc * S5_LANES, (cc + 1) * S5_LANES)
                ar, ai, xr, xi = acc_ref[b, :, re], acc_ref[b, :, _im(re)], shifted(re), shifted(_im(re))
                dlam_ref[0:1, re] += jnp.sum(ar * xr + ai * xi, axis=0, keepdims=True)
                dlam_ref[1:2, re] += jnp.sum(ai * xr - ar * xi, axis=0, keepdims=True)
            dy0 = dy0_ref[b]
            for q in range(S5_BANDS):
                ch, st = _band(q)
                dza_ref[b, :, ch] = (_dot(a_ref[b, :, st], br_ref[q]) + _dot(a_ref[b, :, _im(st)], bi_ref[q])
                                     + d_ref[:, ch] * dy0[:, ch]).astype(BF16)
            dd_ref[...] += jnp.sum(dy0 * za_ref[b], axis=0, keepdims=True)

    tile = lambda j: nts - 1 - j
    tok = lambda w: pl.BlockSpec((nb, ts, w), lambda j: (0, tile(j), 0))
    halo = pl.BlockSpec((nb, HALO, 2 * S5_N), lambda j: (0, jnp.maximum(tile(j) * (ts // HALO) - 1, 0), 0))
    to_st, to_ch = _full((S5_BANDS, BAND_CH, BAND_ST)), _full((S5_BANDS, BAND_ST, BAND_CH))
    return _pcall(body, "s5_bwd", (nts,),
                  [tok(S5_WIDTH), tok(S5_WIDTH), tok(2 * S5_N), halo, to_st, to_st, to_ch, to_ch,
                   _full((2, S5_N)), _full((1, S5_WIDTH))],
                  [tok(S5_WIDTH), tok(2 * S5_N), _full((2, S5_N)), _full((1, S5_WIDTH))],
                  [_sds((nb, seq, S5_WIDTH), BF16), _sds((nb, seq, 2 * S5_N), BF16), _sds((2, S5_N)),
                   _sds((1, S5_WIDTH))],
                  scratch=[pltpu.VMEM((nb, ts, 2 * S5_N), F32), pltpu.VMEM((nb, 2, S5_N), F32)],
                  )(dy0, za, xs, xs, *c_bands, *b_bands, lam, dskip)


def _hgrn_bwd(zh, do, sts, lb, nb, seq):
    nc = seq // CHUNK
    cps = HG_CHUNKS_PER_STEP

    def body(zh_ref, do_ref, sts_ref, lb_ref, dz_ref, dlb_ref, dst_ref):
        @pl.when(pl.program_id(0) == 0)
        def _():
            dst_ref[...] = jnp.zeros_like(dst_ref)
            dlb_ref[...] = jnp.zeros_like(dlb_ref)

        row = lax.broadcasted_iota(jnp.int32, (CHUNK, CHUNK), 0)
        causal = row >= lax.broadcasted_iota(jnp.int32, (CHUNK, CHUNK), 1)
        last_row = lax.broadcasted_iota(jnp.int32, (CHUNK, HG_HEAD), 0) == CHUNK - 1
        for cc in reversed(range(cps)):
            rows = slice(cc * CHUNK, (cc + 1) * CHUNK)
            for b in range(nb):
                for h in range(HG_HEADS):
                    hs = slice(h * HG_HEAD, (h + 1) * HG_HEAD)
                    zq = zh_ref[b, rows, h * HG_HEAD:(h + 1) * HG_HEAD]
                    zf = zh_ref[b, rows, HG_WIDTH + h * HG_HEAD:HG_WIDTH + (h + 1) * HG_HEAD]
                    zi = zh_ref[b, rows, 2 * HG_WIDTH + h * HG_HEAD:2 * HG_WIDTH + (h + 1) * HG_HEAD]
                    lbh = lb_ref[:, hs]
                    sf, f, sq, qa, bc, bm, bl = _hgrn_gates(zq, zf, lbh)
                    k = 1.0 - f
                    e_qt = jnp.exp(bc - bm)
                    e_kt = jnp.exp(bm - bc)
                    e_b = jnp.exp(bc)
                    e_kd = jnp.exp(bl - bc)
                    e_l = jnp.exp(bl)
                    qt, kt, qb, kd = qa * e_qt, k * e_kt, qa * e_b, k * e_kd
                    a = jnp.where(causal, _dot_nt(qt, kt), 0.0)
                    st = sts_ref[b, cc, h]
                    dst = dst_ref[b, h]
                    dov = do_ref[b, rows, hs]
                    da = jnp.where(causal, _dot_nt(dov, zi), 0.0)
                    qt_r, kt_r = qt.astype(BF16).astype(F32), kt.astype(BF16).astype(F32)
                    dqt = _dot(da, kt)
                    dkt = _dot_tn(da, qt)
                    dqb = _dot(dov, st)
                    di = _dot_tn(a, dov) + _dot_nt(kd, dst)
                    dkd = _dot(zi, dst)
                    de_l = jnp.sum(dst * st, axis=0, keepdims=True)
                    dst_ref[b, h] = dst * e_l + _dot_tn(dov, qb)
                    dqa = dqt * e_qt + dqb * e_b
                    dk = dkt * e_kt + dkd * e_kd
                    dbl = jnp.sum(dkd * kd, axis=0, keepdims=True) + de_l * e_l
                    db = dqt * qt_r - dkt * kt_r + dqb * qb - dkd * kd + jnp.where(last_row, dbl, 0.0)
                    df = _cumsum_rows(db, reverse=True) / f - dk
                    dzq = dqa * QSCALE * (sq * (1.0 + zq * (1.0 - sq)))
                    dzf = df * (1.0 - lbh) * sf * (1.0 - sf)
                    dz_ref[b, rows, h * HG_HEAD:(h + 1) * HG_HEAD] = dzq.astype(BF16)
                    dz_ref[b, rows, HG_WIDTH + h * HG_HEAD:HG_WIDTH + (h + 1) * HG_HEAD] = dzf.astype(BF16)
                    dz_ref[b, rows, 2 * HG_WIDTH + h * HG_HEAD:2 * HG_WIDTH + (h + 1) * HG_HEAD] = di.astype(BF16)
                    dlb_ref[:, hs] += jnp.sum(df * (1.0 - sf), axis=0, keepdims=True)

    rev = lambda c: nc // cps - 1 - c
    return _pcall(body, "hgrn_bwd", (nc // cps,),
                  [pl.BlockSpec((nb, cps * CHUNK, 4 * HG_WIDTH), lambda c: (0, rev(c), 0)),
                   pl.BlockSpec((nb, cps * CHUNK, HG_WIDTH), lambda c: (0, rev(c), 0)),
                   pl.BlockSpec((nb, cps, HG_HEADS, HG_HEAD, HG_HEAD), lambda c: (0, rev(c), 0, 0, 0)),
                   _full((1, HG_WIDTH))],
                  [pl.BlockSpec((nb, cps * CHUNK, 3 * HG_WIDTH), lambda c: (0, rev(c), 0)), _full((1, HG_WIDTH))],
                  [_sds((nb, seq, 3 * HG_WIDTH), BF16), _sds((1, HG_WIDTH))],
                  scratch=[pltpu.VMEM((nb, HG_HEADS, HG_HEAD, HG_HEAD), F32)])(zh, do, sts, lb)


def _in_proj_bwd(dza, dzh, dzg, dzgt, dx1, x, g_mix, w_in, tm):
    t = x.shape[0]

    def body(dza_ref, dzh_ref, dzg_ref, dzgt_ref, dx1_ref, x_ref, g_ref, w_ref, dz_ref, dx_ref, dg_ref):
        @pl.when(pl.program_id(0) == 0)
        def _():
            dg_ref[...] = jnp.zeros_like(dg_ref)

        c1, c2, c3 = S5_WIDTH, S5_WIDTH + 3 * HG_WIDTH, S5_WIDTH + 4 * HG_WIDTH
        dz_ref[:, 0:c1] = dza_ref[...]
        dz_ref[:, c1:c2] = dzh_ref[...]
        dz_ref[:, c2:c3] = dzg_ref[...]
        dz_ref[:, c3:] = dzgt_ref[...]
        du = _dot(dz_ref[...], w_ref[...])
        xv = x_ref[...]
        r = lax.rsqrt(jnp.mean(xv * xv, axis=-1, keepdims=True) + EPS)
        xn = xv * r
        dg_ref[...] += jnp.sum(du * xn, axis=0, keepdims=True)
        dxn = du * g_ref[...]
        dx_ref[...] = dx1_ref[...] + r * (dxn - xn * jnp.mean(dxn * xn, axis=-1, keepdims=True))

    row = lambda w: pl.BlockSpec((tm, w), lambda i: (i, 0))
    return _pcall(body, "in_proj_bwd", (t // tm,),
                  [row(S5_WIDTH), row(3 * HG_WIDTH), row(HG_WIDTH), row(2 * D_MODEL), row(D_MODEL), row(D_MODEL),
                   _full((1, D_MODEL)), _full((N_IN, D_MODEL))],
                  [row(N_IN), row(D_MODEL), _full((1, D_MODEL))],
                  [_sds((t, N_IN), BF16), _sds((t, D_MODEL)), _sds((1, D_MODEL))],
                  )(dza, dzh, dzg, dzgt, dx1, x, g_mix, w_in)


def _after(value, token):
    return value + token[0, 0]


def _local_step(x3, tgt3, weights, sp, emit, emit_small):
    nb, seq, _ = x3.shape
    t = nb * seq
    tm = _token_tile(seq)
    x = x3.reshape(t, D_MODEL)
    tgt = tgt3.reshape(t, D_MODEL)
    row = lambda v: v.reshape(1, -1)

    a_re, a_im, b_re, b_im = sp["s5_a_re"], sp["s5_a_im"], sp["s5_b_re"], sp["s5_b_im"]
    ldt = sp["s5_log_dt"].reshape(S5_GROUPS, 1)
    lr, li, bb_re, bb_im, lb = _params_fwd(a_re, a_im, ldt, b_re, b_im, sp["hg_lb_logits"])
    lam = jnp.concatenate([lr.reshape(1, S5_N), li.reshape(1, S5_N)], axis=0)
    swap = lambda m: m.transpose(0, 2, 1)
    b_to_st = (_band_blocks(bb_re), _band_blocks(bb_im))
    b_to_ch = (_band_blocks(swap(bb_re)), _band_blocks(swap(bb_im)))
    c_to_ch = (_band_blocks(swap(sp["s5_c_re"])), _band_blocks(swap(-sp["s5_c_im"])))
    c_to_st = (_band_blocks(sp["s5_c_re"]), _band_blocks(-sp["s5_c_im"]))

    g_mix, g_ffn, g_final = row(sp["g_mix"]), row(sp["g_ffn"]), row(sp["g_final"])
    b_glu, gain, dskip, b_conv = row(sp["b_glu"]), row(sp["hg_norm_gain"]), row(sp["s5_d"]), row(sp["b_conv"])

    w_in = weights("in", lam, *b_to_st, *b_to_ch, *c_to_ch, *c_to_st)["w_in"]
    wide = min(2 * tm, seq)
    u, za, zh, zgt = _in_proj(x, g_mix, w_in, wide)
    seqs = lambda v: v.reshape(nb, seq, v.shape[-1])
    toks = lambda v: v.reshape(t, v.shape[-1])
    xs3, y0 = _s5_fwd(seqs(za), b_to_st, lam, c_to_ch, dskip, nb, seq, tm)
    xs, y0 = toks(xs3), toks(y0)
    o3, sts = _hgrn_fwd(zh.reshape(nb, seq, 4 * HG_WIDTH), lb, nb, seq)
    o = o3.reshape(t, HG_WIDTH)
    wm = weights("mix", y0, o3)
    weights.forward("ffn", wm["w_out"])
    x1, u2, pa, pb, ya2, yb = _mix_fwd(x, y0, o, zh, zgt, wm["w_glu"], b_glu, gain, wm["w_pa"], wm["w_pb"],
                                       wm["w_out"], g_ffn, wide)
    wf = weights("ffn", u2)
    h = _ffn_up(u2, wf["w_up"], min(4 * tm, t))
    hc, a, dx2, dx2b, loss, dg_final = _ffn_down_loss(h, x1, tgt, wf["w_conv"], b_conv, wf["w_down"], g_final,
                                                      seq, tm)

    def wgrad(a, b, name):
        return _wgrad(a, b, name, 512 if a.shape[1] % 512 == 0 else 256, out_dtype=BF16)

    dhc, db_conv = _ffn_bwd_act(dx2b, hc, wf["w_down"], tm)
    dw_down = wgrad(a, dx2b, "dw_down")
    dh, dx1, dx1b, dg_ffn, dw_conv = _ffn_bwd_up(dhc, h, dx2, x1, wf["w_conv"], wf["w_up"], g_ffn, seq, tm)
    sent = emit({"w_up": wgrad(dh, u2, "dw_up"), "w_conv": dw_conv, "w_down": dw_down})
    (dy0, do, dzg, dzgt, m, dpa, dpb, ya1, dpre, db_glu, dgain) = _mix_bwd(
        dx1b, y0, o, zh, zgt, pa, pb, wm["w_glu"], _after(b_glu, sent), gain, wm["w_pa"], wm["w_pb"], wm["w_out"],
        wide)
    sent = emit({"w_out": wgrad(m, dx1b, "dw_out"), "w_pa": wgrad(ya2, dpa, "dw_pa"),
                 "w_pb": wgrad(yb, dpb, "dw_pb"), "w_glu": wgrad(ya1, dpre, "dw_glu")})
    dzh3, dlb = _hgrn_bwd(zh.reshape(nb, seq, 4 * HG_WIDTH), do.reshape(nb, seq, HG_WIDTH), sts, _after(lb, sent),
                          nb, seq)
    dza, a_s5, dlam, dd = _s5_bwd(seqs(dy0), seqs(za), xs3, c_to_st, b_to_ch, lam, dskip, nb, seq, tm)
    dza, a_s5 = toks(dza), toks(a_s5)
    dz, dx, dg_mix = _in_proj_bwd(dza, dzh3.reshape(t, 3 * HG_WIDTH), dzg, dzgt, dx1, x, g_mix, w_in, wide)
    sent = emit({"w_in": wgrad(dz, u, "dw_in")})

    band = HG_HEAD
    dbb_band = _wgrad(a_s5, za, "dbb_s5", 512, band=band, after=sent)
    dc_band = _wgrad(xs, dy0, "dc_s5", 512, band=band, after=sent)
    dbb_re = swap(_diag_blocks(dbb_band[:S5_N], S5_STATE, S5_GROUP))
    dbb_im = swap(_diag_blocks(dbb_band[S5_N:], S5_STATE, S5_GROUP))
    dc_re = swap(_diag_blocks(dc_band[:S5_N], S5_STATE, S5_GROUP))
    dc_im = -swap(_diag_blocks(dc_band[S5_N:], S5_STATE, S5_GROUP))
    da_re, da_im, dldt, db_re, db_im, dlogits = _params_bwd(
        a_re, a_im, ldt, b_re, b_im, sp["hg_lb_logits"],
        dlam[0].reshape(S5_GROUPS, S5_STATE), dlam[1].reshape(S5_GROUPS, S5_STATE), dbb_re, dbb_im, dlb)
    emit_small({"g_mix": dg_mix, "s5_a_re": da_re, "s5_a_im": da_im, "s5_log_dt": dldt.reshape(1, S5_GROUPS),
                "s5_b_re": db_re, "s5_b_im": db_im, "s5_c_re": dc_re, "s5_c_im": dc_im, "s5_d": dd, "b_glu": db_glu,
                "hg_lb_logits": dlogits, "hg_norm_gain": dgain, "g_ffn": dg_ffn, "b_conv": db_conv,
                "g_final": dg_final, "loss": loss})
    return dx.reshape(nb, seq, D_MODEL)


def _mesh_peers():
    x, y, c = lax.axis_index("x"), lax.axis_index("y"), lax.axis_index("c")
    peers = []
    for k in range(1, N_DEV):
        px, py, pc = (1 - x if k & 4 else x), (1 - y if k & 2 else y), (1 - c if k & 1 else c)
        peers.append((k, (px, py, pc), 4 * px + 2 * py + pc))
    return 4 * x + 2 * y + c, peers


_HBM = pl.BlockSpec(memory_space=pltpu.HBM)
_SEM = pl.BlockSpec(memory_space=pltpu.SEMAPHORE)


_EFFECT = pltpu.CompilerParams(has_side_effects=pltpu.SideEffectType.DATAFLOW_SIDE_EFFECTING)


def _remote(src, dst, send_sem, recv_sem, to):
    return pltpu.make_async_remote_copy(src_ref=src, dst_ref=dst, send_sem=send_sem, recv_sem=recv_sem,
                                        device_id=to, device_id_type=pl.DeviceIdType.MESH)


def _exchange_start(name, arrays, after):
    n = len(arrays)
    srcs = [pltpu.with_memory_space_constraint(a, pltpu.HBM) for a in arrays]
    lands = [pltpu.with_memory_space_constraint(lax.empty(a.shape, a.dtype), pltpu.HBM) for a in arrays]
    copies = (N_DEV - 1) * n

    def body(*refs):
        src_refs, land_refs = refs[:n], refs[n:2 * n]
        send_sems, recv_sems, token = refs[2 * n + 1], refs[2 * n + 2], refs[-1]
        my_slab, peers = _mesh_peers()
        for k, peer, slab in peers:
            for i in range(n):
                s = (k - 1) * n + i
                _remote(src_refs[i].at[slab], land_refs[i].at[my_slab], send_sems.at[s], recv_sems.at[s], peer).start()
        token[...] = jnp.zeros_like(token)

    outs = pl.pallas_call(
        body, name=name,
        out_shape=(pltpu.SemaphoreType.DMA((copies,)), pltpu.SemaphoreType.DMA((copies,)),
                   *[pltpu.HBM(a.shape, a.dtype) for a in lands], _sds((SUBLANES, LANES))),
        in_specs=[_HBM] * (2 * n) + [pl.BlockSpec(memory_space=pl.ANY)],
        out_specs=(_SEM, _SEM, *[_HBM] * n, pl.BlockSpec(memory_space=pltpu.VMEM)),
        input_output_aliases={n + i: 2 + i for i in range(n)}, compiler_params=_EFFECT,
    )(*srcs, *lands, after)
    return (outs[0], outs[1], srcs, outs[2:2 + n]), outs[-1]


def _exchange_wait(name, state, *after):
    send_sems, recv_sems, srcs, lands = state
    n = len(lands)

    def body(*refs):
        src_refs, land_refs = refs[:n], refs[n:2 * n]
        send_ref, recv_ref = refs[2 * n], refs[2 * n + 1]
        _, peers = _mesh_peers()
        for k, peer, slab in peers:
            for i in range(n):
                s = (k - 1) * n + i
                copy = _remote(src_refs[i].at[slab], land_refs[i].at[slab], send_ref.at[s], recv_ref.at[s], peer)
                copy.wait_send()
                copy.wait_recv()

    outs = pl.pallas_call(
        body, name=name,
        out_shape=tuple(pltpu.HBM(a.shape, a.dtype) for a in lands),
        in_specs=[_HBM] * (2 * n) + [_SEM, _SEM] + [pl.BlockSpec(memory_space=pl.ANY)] * len(after),
        out_specs=tuple([_HBM] * n),
        input_output_aliases={n + i: i for i in range(n)}, compiler_params=_EFFECT,
    )(*srcs, *lands, send_sems, recv_sems, *after)
    return list(outs), list(srcs)


def _slab(pos):
    return 4 * pos[0] + 2 * pos[1] + pos[2]


def _chip_routes():
    x, y, c = lax.axis_index("x"), lax.axis_index("y"), lax.axis_index("c")
    return (x, y, c), (x, y, 1 - c), [(1 - x, y, c), (x, 1 - y, c), (1 - x, 1 - y, c)]


def _gather_start(name, arrays, after):
    n = len(arrays)
    me = 4 * lax.axis_index("x") + 2 * lax.axis_index("y") + lax.axis_index("c")
    srcs = [pltpu.with_memory_space_constraint(a, pltpu.HBM) for a in arrays]
    lands = [pltpu.with_memory_space_constraint(
        lax.dynamic_update_slice_in_dim(lax.empty((N_DEV,) + a.shape, a.dtype), a[None], me, 0), pltpu.HBM)
        for a in arrays]

    def body(*refs):
        src_refs, land_refs = refs[:n], refs[n:2 * n]
        send_sems, recv_sems, token = refs[2 * n + 1], refs[2 * n + 2], refs[-1]
        mine, sibling, chips = _chip_routes()
        for k, to in enumerate([sibling] + chips):
            for i in range(n):
                _remote(src_refs[i], land_refs[i].at[_slab(mine)], send_sems.at[k * n + i], recv_sems.at[k * n + i],
                        to).start()
        token[...] = jnp.zeros_like(token)

    outs = pl.pallas_call(
        body, name=name,
        out_shape=(pltpu.SemaphoreType.DMA((4 * n,)), pltpu.SemaphoreType.DMA((4 * n,)),
                   *[pltpu.HBM(a.shape, a.dtype) for a in lands], _sds((SUBLANES, LANES))),
        in_specs=[_HBM] * (2 * n) + [pl.BlockSpec(memory_space=pl.ANY)],
        out_specs=(_SEM, _SEM, *[_HBM] * n, pl.BlockSpec(memory_space=pltpu.VMEM)),
        input_output_aliases={n + i: 2 + i for i in range(n)}, compiler_params=_EFFECT,
    )(*srcs, *lands, after)
    return (outs[0], outs[1], srcs, outs[2:2 + n]), outs[-1]


def _gather_forward(name, state, *after):
    send_a, recv_a, srcs, lands = state
    n = len(lands)

    def body(*refs):
        land_refs, recv_a_ref = refs[:n], refs[n]
        send_b, recv_b = refs[n + 1 + len(after)], refs[n + 2 + len(after)]
        mine, sibling, chips = _chip_routes()
        for j, chip in enumerate(chips):
            for i in range(n):
                block = land_refs[i].at[_slab(chip)]
                _remote(block, block, send_b.at[j * n + i], recv_a_ref.at[(1 + j) * n + i], chip).wait_recv()
                _remote(block, block, send_b.at[j * n + i], recv_b.at[j * n + i], sibling).start()

    outs = pl.pallas_call(
        body, name=name,
        out_shape=(pltpu.SemaphoreType.DMA((3 * n,)), pltpu.SemaphoreType.DMA((3 * n,)),
                   *[pltpu.HBM(a.shape, a.dtype) for a in lands]),
        in_specs=[_HBM] * n + [_SEM] + [pl.BlockSpec(memory_space=pl.ANY)] * len(after),
        out_specs=(_SEM, _SEM, *[_HBM] * n),
        input_output_aliases={i: 2 + i for i in range(n)}, compiler_params=_EFFECT,
    )(*lands, recv_a, *after)
    return (send_a, recv_a, srcs, list(outs[2:])), (outs[0], outs[1])


def _gather_wait(name, state, forwarded, *after):
    send_a, recv_a, srcs, lands = state
    send_b, recv_b = forwarded
    n = len(lands)

    def body(*refs):
        src_refs, land_refs = refs[:n], refs[n:2 * n]
        sa, ra, sb, rb = refs[2 * n:2 * n + 4]
        mine, sibling, chips = _chip_routes()
        for i in range(n):
            for k, to in enumerate([sibling] + chips):
                _remote(src_refs[i], land_refs[i].at[_slab(mine)], sa.at[k * n + i], ra.at[k * n + i], to).wait_send()
            theirs = land_refs[i].at[_slab(sibling)]
            _remote(theirs, theirs, sa.at[i], ra.at[i], sibling).wait_recv()
            for j, chip in enumerate(chips):
                sent = land_refs[i].at[_slab(chip)]
                got = land_refs[i].at[_slab((chip[0], chip[1], sibling[2]))]
                _remote(sent, sent, sb.at[j * n + i], rb.at[j * n + i], sibling).wait_send()
                _remote(got, got, sb.at[j * n + i], rb.at[j * n + i], sibling).wait_recv()

    outs = pl.pallas_call(
        body, name=name,
        out_shape=tuple(pltpu.HBM(a.shape, a.dtype) for a in lands),
        in_specs=[_HBM] * (2 * n) + [_SEM] * 4 + [pl.BlockSpec(memory_space=pl.ANY)] * len(after),
        out_specs=tuple([_HBM] * n),
        input_output_aliases={n + i: i for i in range(n)}, compiler_params=_EFFECT,
    )(*srcs, *lands, send_a, recv_a, send_b, recv_b, *after)
    return list(outs), list(srcs)


def _join_cols(parts, name, tr):
    _, r, c = parts.shape

    def body(p_ref, o_ref):
        for j in range(N_DEV):
            o_ref[:, j * c:(j + 1) * c] = p_ref[j]

    return _pcall(body, name, (r // tr,), [pl.BlockSpec((N_DEV, tr, c), lambda i: (0, i, 0))],
                  pl.BlockSpec((tr, N_DEV * c), lambda i: (i, 0)), _sds((r, N_DEV * c), parts.dtype))(parts)


def _split_cols(full, name, tr):
    r, c = full.shape[0], full.shape[1] // N_DEV

    def body(f_ref, o_ref):
        for j in range(N_DEV):
            o_ref[j] = f_ref[:, j * c:(j + 1) * c]

    return _pcall(body, name, (r // tr,), [pl.BlockSpec((tr, N_DEV * c), lambda i: (i, 0))],
                  pl.BlockSpec((N_DEV, tr, c), lambda i: (0, i, 0)), _sds((N_DEV, r, c), full.dtype))(full)


def _my_slab():
    return (4 * lax.axis_index("x") + 2 * lax.axis_index("y") + lax.axis_index("c")).astype(jnp.int32).reshape(1)


def _adamw(parts, sent, w, m, v, name, tile):
    _, rows, cols = w.shape

    def body(me_ref, p_ref, s_ref, w_ref, m_ref, v_ref, g_out, d_out, m_out, v_out):
        me = me_ref[0]
        g = jnp.where(me == 0, s_ref[0], p_ref[0]).astype(F32)
        for k in range(1, N_DEV):
            g = g + jnp.where(me == k, s_ref[0], p_ref[k]).astype(F32)
        m1 = ADAM_B1 * m_ref[0] + (1.0 - ADAM_B1) * g
        v1 = ADAM_B2 * v_ref[0] + (1.0 - ADAM_B2) * (g * g)
        m_hat = m1 / (1.0 - ADAM_B1 ** ADAM_STEP)
        v_hat = v1 / (1.0 - ADAM_B2 ** ADAM_STEP)
        g_out[0] = g
        d_out[0] = -ADAM_LR * (m_hat / (jnp.sqrt(v_hat) + ADAM_EPS) + ADAM_WD * w_ref[0])
        m_out[0] = m1
        v_out[0] = v1

    row = pl.BlockSpec((1, tile, cols), lambda i, me: (0, i, 0))
    return pl.pallas_call(
        body, name=name, out_shape=[_sds((1, rows, cols))] * 4,
        grid_spec=pltpu.PrefetchScalarGridSpec(
            num_scalar_prefetch=1, grid=(rows // tile,),
            in_specs=[pl.BlockSpec((N_DEV, tile, cols), lambda i, me: (0, i, 0)),
                      pl.BlockSpec((1, tile, cols), lambda i, me: (me[0], i, 0)), row, row, row],
            out_specs=[row, row, row, row]),
        compiler_params=pltpu.CompilerParams(dimension_semantics=("arbitrary",), vmem_limit_bytes=VMEM_LIMIT),
    )(_my_slab(), parts, sent, w, m, v)


BIG = {
    "w_in": ((N_IN // N_DEV, D_MODEL), False, N_IN // N_DEV // 3),
    "w_glu": ((S5_WIDTH // N_DEV, S5_WIDTH), False, S5_WIDTH // N_DEV),
    "w_pa": ((S5_WIDTH, D_MODEL // N_DEV), True, S5_WIDTH),
    "w_pb": ((HG_WIDTH, D_MODEL // N_DEV), True, HG_WIDTH),
    "w_out": ((D_MODEL // N_DEV, D_MODEL), False, D_MODEL // N_DEV),
    "w_up": ((2 * D_FF // N_DEV, D_MODEL), False, 2 * D_FF // N_DEV // 4),
    "w_conv": ((CONV_W, 2 * D_FF // N_DEV), True, CONV_W),
    "w_down": ((D_FF // N_DEV, D_MODEL), False, D_FF // N_DEV // 2),
}
TRANSPOSED = ("w_in", "w_up", "s5_b_re", "s5_b_im")
UNALIGNED_COLS = ("w_conv",)


def _stored(n, arr):
    return jnp.swapaxes(arr, -1, -2) if n in TRANSPOSED else arr


def _join_shards(n, parts):
    (a, b), by_cols, _ = BIG[n]
    if not by_cols:
        return parts.reshape(N_DEV * a, b)
    if n in UNALIGNED_COLS:
        return _join_cols(parts, "join_" + n, min(a, 256))
    return parts.transpose(1, 0, 2).reshape(a, N_DEV * b)


def _split_shards(n, full):
    (a, b), by_cols, _ = BIG[n]
    if not by_cols:
        return full.reshape(N_DEV, a, b)
    if n in UNALIGNED_COLS:
        return _split_cols(full, "split_" + n, min(a, 256))
    return full.reshape(a, N_DEV, b).transpose(1, 0, 2)


SMALL_CORE = {
    "s5_b_re": GSC, "s5_b_im": GSC, "s5_c_re": GSC, "s5_c_im": GSC,
    "g_mix": (1, D_MODEL), "g_ffn": (1, D_MODEL), "g_final": (1, D_MODEL), "s5_d": (1, S5_WIDTH),
    "b_glu": (1, S5_WIDTH), "hg_norm_gain": (1, HG_WIDTH), "hg_lb_logits": (2, HG_WIDTH), "b_conv": (1, 2 * D_FF),
    "s5_log_dt": (1, S5_GROUPS), "s5_a_re": (S5_GROUPS, S5_STATE), "s5_a_im": (S5_GROUPS, S5_STATE), "loss": (1, 1),
}
BLOCK_ROWS = 32


def _small_rows():
    rows, r = {}, 0
    for n, core in SMALL_CORE.items():
        rows[n] = r
        r += BLOCK_ROWS if len(core) == 3 else -(-math.prod(core) // PACK_W)
    return rows, -(-r // SUBLANES) * SUBLANES


SMALL_ROW, SMALL_ROWS = _small_rows()


def _small_pieces(name):
    r, core = SMALL_ROW[name], SMALL_CORE[name]
    if len(core) == 3:
        return [((g, slice(None), slice(None)), slice(r + S5_GROUP * (g % 2), r + S5_GROUP * (g % 2 + 1)),
                 slice(S5_STATE * (g // 2), S5_STATE * (g // 2 + 1))) for g in range(S5_GROUPS)]
    pieces = []
    for i in range(core[0]):
        for c0 in range(0, core[1], PACK_W):
            w, flat = min(PACK_W, core[1] - c0), i * core[1] + c0
            pieces.append(((slice(i, i + 1), slice(c0, c0 + w)), slice(r + flat // PACK_W, r + flat // PACK_W + 1),
                           slice(flat % PACK_W, flat % PACK_W + w)))
    return pieces


def _core_index(ref, name, idx):
    return (0,) * (len(ref.shape) - len(SMALL_CORE[name])) + idx


def _pack_small_grads(grads):
    names = list(SMALL_CORE)

    def body(*refs):
        pack = refs[-1]
        pack[...] = jnp.zeros_like(pack)
        for ref, n in zip(refs, names):
            for idx, rows, lanes in _small_pieces(n):
                pack[rows, lanes] = ref[_core_index(ref, n, idx)]

    return _pcall(body, "pack_small_grads", (1,), [_full(grads[n].shape) for n in names],
                  _full((SMALL_ROWS, PACK_W)), _sds((SMALL_ROWS, PACK_W)))(*[grads[n] for n in names])


def _adamw_small(parts, sent, names, rows, given, name):
    lo, hi = rows
    k = len(names)
    shapes = [given[n].shape for n in names]

    def body(*refs):
        me, p_ref, s_ref, ins, outs = refs[0][0], refs[1], refs[2], refs[3:3 + 3 * k], refs[3 + 3 * k:3 + 7 * k]
        packs, results = refs[3 + 7 * k:6 + 7 * k], refs[6 + 7 * k:]
        for j, pack in enumerate(packs):
            pack[...] = jnp.zeros_like(pack)
            for ref, n in zip(ins[j * k:(j + 1) * k], names):
                for idx, prow, lanes in _small_pieces(n):
                    pack[slice(prow.start - lo, prow.stop - lo), lanes] = ref[_core_index(ref, n, idx)]
        mine = s_ref[lo:hi, :]
        g = jnp.where(me == 0, mine, p_ref[0, lo:hi, :])
        for d in range(1, N_DEV):
            g = g + jnp.where(me == d, mine, p_ref[d, lo:hi, :])
        m1 = ADAM_B1 * packs[1][...] + (1.0 - ADAM_B1) * g
        v1 = ADAM_B2 * packs[2][...] + (1.0 - ADAM_B2) * (g * g)
        m_hat = m1 / (1.0 - ADAM_B1 ** ADAM_STEP)
        v_hat = v1 / (1.0 - ADAM_B2 ** ADAM_STEP)
        results[0][...] = g
        results[1][...] = -ADAM_LR * (m_hat / (jnp.sqrt(v_hat) + ADAM_EPS) + ADAM_WD * packs[0][...])
        results[2][...] = m1
        results[3][...] = v1
        for j, result in enumerate(results):
            for ref, n in zip(outs[j * k:(j + 1) * k], names):
                for idx, prow, lanes in _small_pieces(n):
                    ref[_core_index(ref, n, idx)] = result[slice(prow.start - lo, prow.stop - lo), lanes]

    flat = _pcall(body, name, (1,),
                  [pl.BlockSpec(memory_space=pltpu.SMEM), _full(parts.shape), _full(sent.shape)]
                  + [_full(s) for s in shapes] * 3,
                  [_full(s) for s in shapes] * 4, [_sds(s) for s in shapes] * 4,
                  scratch=[pltpu.VMEM((hi - lo, PACK_W), F32)] * 7,
                  )(_my_slab(), parts, sent, *[given[pre + n] for pre in ("", "m_", "v_") for n in names])
    return {n: [flat[j * k + i] for j in range(4)] for i, n in enumerate(names)}


def kernel(x, g_mix, w_in, s5_a_re, s5_a_im, s5_log_dt, s5_b_re, s5_b_im, s5_c_re, s5_c_im, s5_d, w_glu, b_glu, hg_lb_logits, hg_norm_gain, w_pa, w_pb, w_out, g_ffn, w_up, w_conv, b_conv, w_down, g_final, loss_target, m_g_mix, m_w_in, m_s5_a_re, m_s5_a_im, m_s5_log_dt, m_s5_b_re, m_s5_b_im, m_s5_c_re, m_s5_c_im, m_s5_d, m_w_glu, m_b_glu, m_hg_lb_logits, m_hg_norm_gain, m_w_pa, m_w_pb, m_w_out, m_g_ffn, m_w_up, m_w_conv, m_b_conv, m_w_down, m_g_final, v_g_mix, v_w_in, v_s5_a_re, v_s5_a_im, v_s5_log_dt, v_s5_b_re, v_s5_b_im, v_s5_c_re, v_s5_c_im, v_s5_d, v_w_glu, v_b_glu, v_hg_lb_logits, v_hg_norm_gain, v_w_pa, v_w_pb, v_w_out, v_g_ffn, v_w_up, v_w_conv, v_b_conv, v_w_down, v_g_final):
    given = dict(locals())
    small_names = [n for n in SMALL_CORE if n != "loss"]

    pay = {n: given[n][0] if n == "w_conv" else _stored(n, given[n])[0].astype(BF16) for n in BIG}
    groups = {"in": ["w_in"], "mix": ["w_glu", "w_pa", "w_pb", "w_out"], "ffn": ["w_up", "w_down", "w_conv"]}
    gathers, order = {}, pay["w_in"]
    for grp, names in groups.items():
        gathers[grp], order = _gather_start("gather_" + grp + "_start", [pay[n] for n in names], order)

    forwards = {}

    def forward(grp, *after):
        if grp == "in":
            after = (*after, order)
        forwards[grp] = _gather_forward("gather_" + grp + "_forward", gathers[grp], *after)

    def weights(grp, *after):
        if grp not in forwards:
            forward(grp, *after)
        got, _ = _gather_wait("gather_" + grp + "_wait", *forwards[grp], *after)
        return {n: _join_shards(n, g) for n, g in zip(groups[grp], got)}

    weights.forward = forward

    in_flight, started = [], []

    def emit(grads):
        names = list(grads)
        state, token = _exchange_start("grads_" + names[0] + "_start", [_split_shards(n, grads[n]) for n in names],
                                       grads[names[0]])
        in_flight.append((names, state))
        return token

    def emit_small(grads):
        pack = _pack_small_grads(grads)
        state, token = _gather_start("grads_small_start", [pack], pack)
        in_flight.append((["small"], state))
        started.append(token)

    sp = {n: (given[n] if n in ("g_final", "hg_lb_logits") else _stored(n, given[n])[0]) for n in small_names}
    sp["g_mix"] = _after(sp["g_mix"], order)
    dx = _local_step(x, loss_target, weights, sp, emit, emit_small)

    res = {}
    after = [started[-1]]
    for names, state in in_flight:
        if names == ["small"]:
            state, forwarded = _gather_forward("grads_small_forward", state, *after)
            parts, sent = _gather_wait("grads_small_wait", state, forwarded)
        else:
            parts, sent = _exchange_wait("grads_" + names[0] + "_wait", state, *after)
        if names != ["small"]:
            after = []
            for n, part, mine in zip(names, parts, sent):
                raw = _adamw(part, mine, *[_stored(n, given[pre + n]) for pre in ("", "m_", "v_")], "adamw_" + n,
                             BIG[n][2])
                res[n] = [_stored(n, r) for r in raw]
                after.append(raw[0])
            continue
        sgiven = {pre + n: _stored(n, given[pre + n]) for pre in ("", "m_", "v_") for n in small_names}
        for pre in ("", "m_", "v_"):
            sgiven[pre + "g_final"] = given[pre + "g_final"].reshape(1, D_MODEL)
            sgiven[pre + "loss"] = jnp.zeros((1, 1), F32)
        raw = _adamw_small(parts[0], sent[0], list(SMALL_CORE), (0, SMALL_ROWS), sgiven, "adamw_small")
        res.update({n: [_stored(n, r) for r in raw[n]] for n in small_names})
        res["g_final"] = [r.reshape(D_MODEL) for r in raw["g_final"]]
        total_loss = raw["loss"][0].reshape(())
        after = [raw["s5_b_re"][0], raw["g_mix"][0]]
    return (total_loss, dx, *[res[n][0] for n in WEIGHT_ORDER], *[res[n][1] for n in WEIGHT_ORDER],
            *[res[n][2] for n in WEIGHT_ORDER], *[res[n][3] for n in WEIGHT_ORDER])
```

```python
import math

import jax
import jax.numpy as jnp
from jax import lax
from jax.experimental import pallas as pl
from jax.experimental.pallas import tpu as pltpu

F32 = jnp.float32
BF16 = jnp.bfloat16

D_MODEL = 1024
S5_WIDTH = 512
S5_GROUP = 16
S5_GROUPS = 32
S5_STATE = 64
S5_N = S5_GROUPS * S5_STATE
HG_WIDTH = 512
HG_HEAD = 128
HG_HEADS = 4
D_FF = 2816
CONV_W = 3
CHUNK = 64
N_IN = S5_WIDTH + 4 * HG_WIDTH + 2 * D_MODEL
EPS = 1e-6
QSCALE = HG_HEAD ** -0.5

ADAM_LR = 0.001
ADAM_B1 = 0.9
ADAM_B2 = 0.999
ADAM_EPS = 1e-08
ADAM_WD = 0.01
ADAM_STEP = 10

N_DEV = 8
V7X_VMEM_BYTES = 64 * 1024 * 1024
VMEM_LIMIT = V7X_VMEM_BYTES * 7 // 8
SUBLANES = 8
LANES = 128
PACK_W = 1024

WEIGHT_ORDER = ("g_mix", "w_in", "s5_a_re", "s5_a_im", "s5_log_dt", "s5_b_re", "s5_b_im", "s5_c_re", "s5_c_im",
                "s5_d", "w_glu", "b_glu", "hg_lb_logits", "hg_norm_gain", "w_pa", "w_pb", "w_out", "g_ffn",
                "w_up", "w_conv", "b_conv", "w_down", "g_final")


def _pcall(body, name, grid, in_specs, out_specs, out_shape, scratch=()):
    return pl.pallas_call(
        body, name=name, grid=grid, in_specs=in_specs, out_specs=out_specs, out_shape=out_shape,
        scratch_shapes=list(scratch),
        compiler_params=pltpu.CompilerParams(dimension_semantics=("arbitrary",) * len(grid),
                                             vmem_limit_bytes=VMEM_LIMIT),
    )


def _full(shape):
    return pl.BlockSpec(shape, lambda *_: (0,) * len(shape))


def _sds(shape, dtype=F32):
    return jax.ShapeDtypeStruct(shape, dtype)


def _dot(a, b):
    return jnp.dot(a.astype(BF16), b.astype(BF16), preferred_element_type=F32)


def _dot_nt(a, b):
    return lax.dot_general(a.astype(BF16), b.astype(BF16), (((1,), (1,)), ((), ())), preferred_element_type=F32)


def _dot_tn(a, b):
    return lax.dot_general(a.astype(BF16), b.astype(BF16), (((0,), (0,)), ((), ())), preferred_element_type=F32)


def _sigmoid(x):
    return jax.nn.sigmoid(x)


GELU_C = math.sqrt(2.0 / math.pi)
GELU_A = 0.044715


def _gelu(x):
    return 0.5 * x * (1.0 + jnp.tanh(GELU_C * (x + GELU_A * (x * x * x))))


def _gelu_grad(x):
    t = jnp.tanh(GELU_C * (x + GELU_A * (x * x * x)))
    return 0.5 * (1.0 + t) + 0.5 * x * (1.0 - t * t) * (GELU_C * (1.0 + 3.0 * GELU_A * x * x))


def _cumsum_rows(v, reverse=False):
    n = v.shape[0]
    row = lax.broadcasted_iota(jnp.int32, v.shape, 0)
    s = 1
    while s < n:
        if reverse:
            v = v + jnp.where(row < n - s, pltpu.roll(v, n - s, axis=0), 0.0)
        else:
            v = v + jnp.where(row >= s, pltpu.roll(v, s, axis=0), 0.0)
        s *= 2
    return v


def _token_tile(seq):
    return min(256, seq)


def _s5_coeffs(a_re, a_im, ldt):
    dt = jnp.exp(ldt)
    mag = jnp.exp(a_re * dt)
    ang = a_im * dt
    lb_re = mag * jnp.cos(ang)
    lb_im = mag * jnp.sin(ang)
    den = a_re * a_re + a_im * a_im
    n_re = lb_re - 1.0
    n_im = lb_im
    co_re = (n_re * a_re + n_im * a_im) / den
    co_im = (n_im * a_re - n_re * a_im) / den
    return lb_re, lb_im, co_re, co_im


GS, GSC = (S5_GROUPS, S5_STATE), (S5_GROUPS, S5_GROUP, S5_STATE)


def _params_fwd(a_re, a_im, ldt, bt_re, bt_im, logits):
    def body(are, aim, ld, bre, bim, lg, lr_o, li_o, bbr_o, bbi_o, lb_o):
        lr, li, co_re, co_im = _s5_coeffs(are[...], aim[...], ld[...])
        lr_o[...] = lr
        li_o[...] = li
        for g in range(S5_GROUPS):
            cr, ci = co_re[g:g + 1, :], co_im[g:g + 1, :]
            bbr_o[g] = cr * bre[g] - ci * bim[g]
            bbi_o[g] = cr * bim[g] + ci * bre[g]
        lb_o[...] = _sigmoid(lg[0:1, :] - lg[1:2, :])

    return _pcall(body, "params_fwd", (1,),
                  [_full(GS), _full(GS), _full((S5_GROUPS, 1)), _full(GSC), _full(GSC), _full((2, HG_WIDTH))],
                  [_full(GS), _full(GS), _full(GSC), _full(GSC), _full((1, HG_WIDTH))],
                  [_sds(GS), _sds(GS), _sds(GSC), _sds(GSC), _sds((1, HG_WIDTH))],
                  )(a_re, a_im, ldt, bt_re, bt_im, logits)


def _params_bwd(a_re, a_im, ldt, bt_re, bt_im, logits, dlr, dli, dbbr, dbbi, dlb):
    def body(are, aim, ld, bre, bim, lg, dlr_r, dli_r, dbbr_r, dbbi_r, dlb_r,
             dare_o, daim_o, dld_o, dbre_o, dbim_o, dlg_o, dcr_ref, dci_ref):
        (_, _, co_re, co_im), vjp = jax.vjp(_s5_coeffs, are[...], aim[...], ld[...])
        for g in range(S5_GROUPS):
            cr, ci = co_re[g:g + 1, :], co_im[g:g + 1, :]
            gr, gi, br, bi = dbbr_r[g], dbbi_r[g], bre[g], bim[g]
            dbre_o[g] = cr * gr + ci * gi
            dbim_o[g] = cr * gi - ci * gr
            dcr_ref[g:g + 1, :] = jnp.sum(gr * br + gi * bi, axis=0, keepdims=True)
            dci_ref[g:g + 1, :] = jnp.sum(gi * br - gr * bi, axis=0, keepdims=True)
        dare, daim, dld = vjp((dlr_r[...], dli_r[...], dcr_ref[...], dci_ref[...]))
        dare_o[...] = dare
        daim_o[...] = daim
        dld_o[...] = dld
        lb = _sigmoid(lg[0:1, :] - lg[1:2, :])
        d0 = dlb_r[...] * lb * (1.0 - lb)
        dlg_o[0:1, :] = d0
        dlg_o[1:2, :] = -d0

    return _pcall(body, "params_bwd", (1,),
                  [_full(GS), _full(GS), _full((S5_GROUPS, 1)), _full(GSC), _full(GSC), _full((2, HG_WIDTH)),
                   _full(GS), _full(GS), _full(GSC), _full(GSC), _full((1, HG_WIDTH))],
                  [_full(GS), _full(GS), _full((S5_GROUPS, 1)), _full(GSC), _full(GSC), _full((2, HG_WIDTH))],
                  [_sds(GS), _sds(GS), _sds((S5_GROUPS, 1)), _sds(GSC), _sds(GSC), _sds((2, HG_WIDTH))],
                  scratch=[pltpu.VMEM(GS, F32), pltpu.VMEM(GS, F32)],
                  )(a_re, a_im, ldt, bt_re, bt_im, logits, dlr, dli, dbbr, dbbi, dlb)


def _band_blocks(m):
    g, r, c = m.shape
    gb = g // S5_BANDS
    m4 = m.astype(BF16).reshape(S5_BANDS, gb, r, c)
    on_diag = jnp.eye(gb, dtype=bool)[None, :, None, :, None]
    return jnp.where(on_diag, m4[:, :, :, None, :], 0).reshape(S5_BANDS, gb * r, gb * c)


def _diag_blocks(band, r, c):
    g, nb = band.shape[0] // r, band.shape[1] // c
    on_diag = (jnp.arange(g) % nb)[:, None, None, None] == jnp.arange(nb)[None, None, :, None]
    return jnp.sum(jnp.where(on_diag, band.reshape(g, r, nb, c), 0.0), axis=2)


def _in_proj(x, g_mix, w_in, tm):
    t = x.shape[0]

    def body(x_ref, g_ref, w_ref, u_ref, za_ref, zh_ref, zg_ref):
        xv = x_ref[...]
        r = lax.rsqrt(jnp.mean(xv * xv, axis=-1, keepdims=True) + EPS)
        u = (xv * r * g_ref[...]).astype(BF16)
        u_ref[...] = u
        za_ref[...] = _dot_nt(u, w_ref[0:S5_WIDTH, :])
        zh_ref[...] = _dot_nt(u, w_ref[S5_WIDTH:S5_WIDTH + 4 * HG_WIDTH, :])
        zg_ref[...] = _dot_nt(u, w_ref[S5_WIDTH + 4 * HG_WIDTH:, :]).astype(BF16)

    row = lambda w: pl.BlockSpec((tm, w), lambda i: (i, 0))
    return _pcall(body, "in_proj", (t // tm,),
                  [row(D_MODEL), _full((1, D_MODEL)), _full((N_IN, D_MODEL))],
                  [row(D_MODEL), row(S5_WIDTH), row(4 * HG_WIDTH), row(2 * D_MODEL)],
                  [_sds((t, D_MODEL), BF16), _sds((t, S5_WIDTH)), _sds((t, 4 * HG_WIDTH)),
                   _sds((t, 2 * D_MODEL), BF16)],
                  )(x, g_mix, w_in)


S5_LANES = 512
S5_BANDS = 4


def _band(q):
    return (slice(q * S5_WIDTH // S5_BANDS, (q + 1) * S5_WIDTH // S5_BANDS),
            slice(q * S5_N // S5_BANDS, (q + 1) * S5_N // S5_BANDS))


def _im(st):
    return slice(S5_N + st.start, S5_N + st.stop)


SCAN_UNROLL = 8


def _complex_scan(buf_ref, lam_ref, st_ref, nb, ts, reverse):
    lanes = [slice(cc * S5_LANES, (cc + 1) * S5_LANES) for cc in range(S5_N // S5_LANES)]
    chains = [(b, re) for b in range(nb) for re in lanes]
    nch = len(chains)
    wr = {re.start: lam_ref[0:1, re] for re in lanes}
    wi = {re.start: -lam_ref[1:2, re] if reverse else lam_ref[1:2, re] for re in lanes}

    def block(ib, carry):
        vr, vi = list(carry[:nch]), list(carry[nch:])
        first = ts - SCAN_UNROLL - ib * SCAN_UNROLL if reverse else ib * SCAN_UNROLL
        first = pl.multiple_of(first, SCAN_UNROLL)
        for k in range(SCAN_UNROLL):
            row = pl.ds(first + (SCAN_UNROLL - 1 - k if reverse else k), 1)
            for c, (b, re) in enumerate(chains):
                nr = wr[re.start] * vr[c] - wi[re.start] * vi[c] + buf_ref[b, row, re]
                ni = wr[re.start] * vi[c] + wi[re.start] * vr[c] + buf_ref[b, row, _im(re)]
                buf_ref[b, row, re] = nr
                buf_ref[b, row, _im(re)] = ni
                vr[c], vi[c] = nr, ni
        return tuple(vr + vi)

    init = tuple(st_ref[b, 0:1, re] for b, re in chains) + tuple(st_ref[b, 1:2, re] for b, re in chains)
    last = lax.fori_loop(0, ts // SCAN_UNROLL, block, init)
    for c, (b, re) in enumerate(chains):
        st_ref[b, 0:1, re] = last[c]
        st_ref[b, 1:2, re] = last[nch + c]


BAND_CH = S5_WIDTH // S5_BANDS
BAND_ST = S5_N // S5_BANDS


def _s5_fwd(za, b_bands, lam, c_bands, dskip, nb, seq, ts):
    nts = seq // ts

    def body(za_ref, br_ref, bi_ref, lam_ref, cr_ref, ci_ref, d_ref, xs_ref, y_ref, buf_ref, st_ref):
        @pl.when(pl.program_id(0) == 0)
        def _():
            st_ref[...] = jnp.zeros_like(st_ref)

        for b in range(nb):
            zav = za_ref[b]
            for q in range(S5_BANDS):
                ch, st = _band(q)
                buf_ref[b, :, st] = _dot(zav[:, ch], br_ref[q])
                buf_ref[b, :, _im(st)] = _dot(zav[:, ch], bi_ref[q])
        _complex_scan(buf_ref, lam_ref, st_ref, nb, ts, reverse=False)
        for b in range(nb):
            zav = za_ref[b]
            xs_ref[b] = buf_ref[b].astype(BF16)
            for q in range(S5_BANDS):
                ch, st = _band(q)
                y_ref[b, :, ch] = (_dot(xs_ref[b, :, st], cr_ref[q]) + _dot(xs_ref[b, :, _im(st)], ci_ref[q])
                                   + d_ref[:, ch] * zav[:, ch])

    tok = lambda w: pl.BlockSpec((nb, ts, w), lambda j: (0, j, 0))
    to_st, to_ch = _full((S5_BANDS, BAND_CH, BAND_ST)), _full((S5_BANDS, BAND_ST, BAND_CH))
    return _pcall(body, "s5_fwd", (nts,),
                  [tok(S5_WIDTH), to_st, to_st, _full((2, S5_N)), to_ch, to_ch, _full((1, S5_WIDTH))],
                  [tok(2 * S5_N), tok(S5_WIDTH)],
                  [_sds((nb, seq, 2 * S5_N), BF16), _sds((nb, seq, S5_WIDTH))],
                  scratch=[pltpu.VMEM((nb, ts, 2 * S5_N), F32), pltpu.VMEM((nb, 2, S5_N), F32)],
                  )(za, *b_bands, lam, *c_bands, dskip)


def _hgrn_gates(zq, zf, lbh):
    sf = _sigmoid(zf)
    f = lbh + (1.0 - lbh) * sf
    sq = _sigmoid(zq)
    qa = zq * sq * QSCALE
    bc = _cumsum_rows(jnp.log(f))
    bm = bc[CHUNK // 2 - 1:CHUNK // 2, :]
    bl = bc[CHUNK - 1:CHUNK, :]
    return sf, f, sq, qa, bc, bm, bl


HG_CHUNKS_PER_STEP = 4


def _hgrn_fwd(zh, lb, nb, seq):
    nc = seq // CHUNK
    cps = HG_CHUNKS_PER_STEP

    def body(zh_ref, lb_ref, o_ref, sts_ref, st_ref):
        @pl.when(pl.program_id(0) == 0)
        def _():
            st_ref[...] = jnp.zeros_like(st_ref)

        causal = (lax.broadcasted_iota(jnp.int32, (CHUNK, CHUNK), 0)
                  >= lax.broadcasted_iota(jnp.int32, (CHUNK, CHUNK), 1))
        for cc in range(cps):
            rows = slice(cc * CHUNK, (cc + 1) * CHUNK)
            for b in range(nb):
                for h in range(HG_HEADS):
                    hs = slice(h * HG_HEAD, (h + 1) * HG_HEAD)
                    zq = zh_ref[b, rows, h * HG_HEAD:(h + 1) * HG_HEAD]
                    zf = zh_ref[b, rows, HG_WIDTH + h * HG_HEAD:HG_WIDTH + (h + 1) * HG_HEAD]
                    zi = zh_ref[b, rows, 2 * HG_WIDTH + h * HG_HEAD:2 * HG_WIDTH + (h + 1) * HG_HEAD]
                    _, f, _, qa, bc, bm, bl = _hgrn_gates(zq, zf, lb_ref[:, hs])
                    k = 1.0 - f
                    qt = qa * jnp.exp(bc - bm)
                    kt = k * jnp.exp(bm - bc)
                    qb = qa * jnp.exp(bc)
                    kd = k * jnp.exp(bl - bc)
                    st = st_ref[b, h]
                    sts_ref[b, cc, h] = st
                    a = jnp.where(causal, _dot_nt(qt, kt), 0.0)
                    o_ref[b, rows, hs] = _dot(a, zi) + _dot_nt(qb, st)
                    st_ref[b, h] = st * jnp.exp(bl) + _dot_tn(zi, kd)

    return _pcall(body, "hgrn_fwd", (nc // cps,),
                  [pl.BlockSpec((nb, cps * CHUNK, 4 * HG_WIDTH), lambda c: (0, c, 0)), _full((1, HG_WIDTH))],
                  [pl.BlockSpec((nb, cps * CHUNK, HG_WIDTH), lambda c: (0, c, 0)),
                   pl.BlockSpec((nb, cps, HG_HEADS, HG_HEAD, HG_HEAD), lambda c: (0, c, 0, 0, 0))],
                  [_sds((nb, seq, HG_WIDTH)), _sds((nb, nc, HG_HEADS, HG_HEAD, HG_HEAD))],
                  scratch=[pltpu.VMEM((nb, HG_HEADS, HG_HEAD, HG_HEAD), F32)])(zh, lb)


def _head_rms(o):
    parts = []
    for h in range(HG_HEADS):
        oh = o[:, h * HG_HEAD:(h + 1) * HG_HEAD]
        r = lax.rsqrt(jnp.mean(oh * oh, axis=-1, keepdims=True) + EPS)
        parts.append(jnp.broadcast_to(r, oh.shape))
    return jnp.concatenate(parts, axis=1)


def _head_mean(v):
    parts = []
    for h in range(HG_HEADS):
        vh = v[:, h * HG_HEAD:(h + 1) * HG_HEAD]
        parts.append(jnp.broadcast_to(jnp.mean(vh, axis=-1, keepdims=True), vh.shape))
    return jnp.concatenate(parts, axis=1)


def _mix_fwd(x, y0, o, zh, zgt, w_glu, b_glu, gain, w_pa, w_pb, w_out, g_ffn, tm):
    t = x.shape[0]

    def body(x_ref, y0_ref, o_ref, zg_ref, zgt_ref, wglu_ref, bglu_ref, gain_ref, wpa_ref, wpb_ref, wout_ref,
             gffn_ref, x1_ref, u2_ref, pa_ref, pb_ref, ya2_ref, yb_ref):
        ya1 = _gelu(y0_ref[...])
        s = _sigmoid(_dot(ya1, wglu_ref[...]) + bglu_ref[...])
        ya2 = (ya1 * s).astype(BF16)
        ov = o_ref[...]
        zg = zg_ref[...]
        yb = (ov * _head_rms(ov) * gain_ref[...] * (zg * _sigmoid(zg))).astype(BF16)
        ya2_ref[...] = ya2
        yb_ref[...] = yb
        pa = jnp.dot(ya2, wpa_ref[...], preferred_element_type=F32)
        pb = jnp.dot(yb, wpb_ref[...], preferred_element_type=F32)
        pa_ref[...] = pa.astype(BF16)
        pb_ref[...] = pb.astype(BF16)
        m = (_sigmoid(zgt_ref[:, 0:D_MODEL].astype(F32)) * pa
             + _sigmoid(zgt_ref[:, D_MODEL:].astype(F32)) * pb)
        x1 = x_ref[...] + _dot(m, wout_ref[...])
        x1_ref[...] = x1
        r = lax.rsqrt(jnp.mean(x1 * x1, axis=-1, keepdims=True) + EPS)
        u2_ref[...] = (x1 * r * gffn_ref[...]).astype(BF16)

    row = lambda w: pl.BlockSpec((tm, w), lambda i: (i, 0))
    return _pcall(body, "mix_fwd", (t // tm,),
                  [row(D_MODEL), row(S5_WIDTH), row(HG_WIDTH), pl.BlockSpec((tm, HG_WIDTH), lambda i: (i, 3)),
                   row(2 * D_MODEL), _full((S5_WIDTH, S5_WIDTH)), _full((1, S5_WIDTH)), _full((1, HG_WIDTH)),
                   _full((S5_WIDTH, D_MODEL)), _full((HG_WIDTH, D_MODEL)), _full((D_MODEL, D_MODEL)),
                   _full((1, D_MODEL))],
                  [row(D_MODEL), row(D_MODEL), row(D_MODEL), row(D_MODEL), row(S5_WIDTH), row(HG_WIDTH)],
                  [_sds((t, D_MODEL)), _sds((t, D_MODEL), BF16), _sds((t, D_MODEL), BF16), _sds((t, D_MODEL), BF16),
                   _sds((t, S5_WIDTH), BF16), _sds((t, HG_WIDTH), BF16)],
                  )(x, y0, o, zh, zgt, w_glu, b_glu, gain, w_pa, w_pb, w_out, g_ffn)


FF_COLS = 256
FF_UP_TILE = 2 * D_FF // 2


def _ffn_up(u2, w_up, tm):
    t = u2.shape[0]
    n = 2 * D_FF

    def body(u_ref, w_ref, h_ref):
        h_ref[...] = _dot_nt(u_ref[...], w_ref[...]).astype(BF16)

    return _pcall(body, "ffn_up", (n // FF_UP_TILE, t // tm),
                  [pl.BlockSpec((tm, D_MODEL), lambda j, i: (i, 0)),
                   pl.BlockSpec((FF_UP_TILE, D_MODEL), lambda j, i: (j, 0))],
                  pl.BlockSpec((tm, FF_UP_TILE), lambda j, i: (i, j)),
                  _sds((t, n), BF16))(u2, w_up)


HALO = 16


def _shift_matrix(tm):
    r = lax.broadcasted_iota(jnp.int32, (tm, tm), 0)
    c = lax.broadcasted_iota(jnp.int32, (tm, tm), 1)
    return jnp.where(r == c + 1, 1.0, 0.0).astype(BF16)


def _conv_cols(h_ref, halo_ref, valid, wc_ref, bc_ref, c0):
    cs = slice(c0, c0 + FF_COLS)
    cur = h_ref[:, cs].astype(F32)
    prev = jnp.where(valid, halo_ref[:, cs].astype(F32), 0.0)
    full = jnp.concatenate([prev, cur], axis=0)
    h1 = pltpu.roll(full, 1, axis=0)[HALO:]
    h2 = pltpu.roll(full, 2, axis=0)[HALO:]
    return h2 * wc_ref[0:1, cs] + h1 * wc_ref[1:2, cs] + cur * wc_ref[2:3, cs] + bc_ref[:, cs]


def _ffn_down_loss(h, x1, tgt, w_conv, b_conv, w_down, g_final, seq, tm):
    t = h.shape[0]
    tps = seq // tm
    n = 2 * D_FF

    def body(h_ref, halo_ref, x1_ref, tgt_ref, wc_ref, bc_ref, wd_ref, gf_ref,
             hc_ref, a_ref, dx2_ref, dx2b_ref, loss_ref, dgf_ref):
        i = pl.program_id(0)

        @pl.when(i == 0)
        def _():
            loss_ref[...] = jnp.zeros_like(loss_ref)
            dgf_ref[...] = jnp.zeros_like(dgf_ref)

        valid = (i % tps) != 0
        x2 = x1_ref[...]
        for j in range(D_FF // FF_COLS):
            gate = _conv_cols(h_ref, halo_ref, valid, wc_ref, bc_ref, j * FF_COLS)
            val = _conv_cols(h_ref, halo_ref, valid, wc_ref, bc_ref, D_FF + j * FF_COLS)
            hc_ref[:, j * FF_COLS:(j + 1) * FF_COLS] = gate.astype(BF16)
            hc_ref[:, D_FF + j * FF_COLS:D_FF + (j + 1) * FF_COLS] = val.astype(BF16)
            a = (gate * _sigmoid(gate) * val).astype(BF16)
            a_ref[:, j * FF_COLS:(j + 1) * FF_COLS] = a
            x2 = x2 + jnp.dot(a, wd_ref[j * FF_COLS:(j + 1) * FF_COLS, :], preferred_element_type=F32)
        r = lax.rsqrt(jnp.mean(x2 * x2, axis=-1, keepdims=True) + EPS)
        xn = x2 * r
        g = gf_ref[...]
        e = xn * g - tgt_ref[...]
        loss_ref[...] += (0.5 / D_MODEL) * jnp.sum(e * e).reshape(1, 1)
        dy = e * (1.0 / D_MODEL)
        dgf_ref[...] += jnp.sum(dy * xn, axis=0, keepdims=True)
        dxn = dy * g
        dx2 = r * (dxn - xn * jnp.mean(dxn * xn, axis=-1, keepdims=True))
        dx2_ref[...] = dx2
        dx2b_ref[...] = dx2.astype(BF16)

    row = lambda w: pl.BlockSpec((tm, w), lambda i: (i, 0))
    halo = pl.BlockSpec((HALO, n), lambda i: (jnp.maximum(i * (tm // HALO) - 1, 0), 0))
    return _pcall(body, "ffn_down_loss", (t // tm,),
                  [row(n), halo, row(D_MODEL), row(D_MODEL), _full((CONV_W, n)), _full((1, n)),
                   _full((D_FF, D_MODEL)), _full((1, D_MODEL))],
                  [row(n), row(D_FF), row(D_MODEL), row(D_MODEL), _full((1, 1)), _full((1, D_MODEL))],
                  [_sds((t, n), BF16), _sds((t, D_FF), BF16), _sds((t, D_MODEL)), _sds((t, D_MODEL), BF16),
                   _sds((1, 1)), _sds((1, D_MODEL))],
                  )(h, h, x1, tgt, w_conv, b_conv, w_down, g_final)


def _wgrad(a, b, name, tn, out_dtype=F32, band=None, after=None):
    t, m = a.shape
    n = b.shape[1] if band is None else band
    nbands = 1 if band is None else b.shape[1] // band
    after = b if after is None else after

    def body(a_ref, b_ref, after_ref, o_ref):
        o_ref[...] = _dot_tn(a_ref[...], b_ref[...]).astype(out_dtype)

    return _pcall(body, name, (m // tn,),
                  [pl.BlockSpec((t, tn), lambda i: (0, i)), pl.BlockSpec((t, n), lambda i: (0, i % nbands)),
                   pl.BlockSpec(memory_space=pl.ANY)],
                  pl.BlockSpec((tn, n), lambda i: (i, 0)), _sds((m, n), out_dtype))(a, b, after)


def _ffn_bwd_act(dx2b, hc, w_down, tm):
    t = hc.shape[0]
    n = 2 * D_FF

    def body(dx2_ref, hc_ref, wd_ref, dhc_ref, dbc_ref):
        @pl.when(pl.program_id(0) == 0)
        def _():
            dbc_ref[...] = jnp.zeros_like(dbc_ref)

        dx2 = dx2_ref[...]
        for j in range(D_FF // FF_COLS):
            gs = slice(j * FF_COLS, (j + 1) * FF_COLS)
            vs = slice(D_FF + j * FF_COLS, D_FF + (j + 1) * FF_COLS)
            gate = hc_ref[:, gs].astype(F32)
            val = hc_ref[:, vs].astype(F32)
            da = _dot_nt(dx2, wd_ref[gs, :])
            sg = _sigmoid(gate)
            dgate = da * val * (sg * (1.0 + gate * (1.0 - sg)))
            dval = da * (gate * sg)
            dhc_ref[:, gs] = dgate.astype(BF16)
            dhc_ref[:, vs] = dval.astype(BF16)
            dbc_ref[:, gs] += jnp.sum(dgate, axis=0, keepdims=True)
            dbc_ref[:, vs] += jnp.sum(dval, axis=0, keepdims=True)

    row = lambda w: pl.BlockSpec((tm, w), lambda i: (i, 0))
    return _pcall(body, "ffn_bwd_act", (t // tm,),
                  [row(D_MODEL), row(n), _full((D_FF, D_MODEL))],
                  [row(n), _full((1, n))],
                  [_sds((t, n), BF16), _sds((1, n))],
                  )(dx2b, hc, w_down)


def _ffn_bwd_up(dhc, h, dx2, x1, w_conv, w_up, g_ffn, seq, tm):
    t = dhc.shape[0]
    tps = seq // tm
    n = 2 * D_FF
    last = t // HALO - 1

    def body(dhc_ref, halo_ref, h_ref, dx2_ref, x1_ref, wc_ref, wu_ref, gf_ref,
             dh_ref, dx1_ref, dx1b_ref, dgf_ref, dwc_ref):
        i = pl.program_id(0)

        @pl.when(i == 0)
        def _():
            dgf_ref[...] = jnp.zeros_like(dgf_ref)
            dwc_ref[...] = jnp.zeros_like(dwc_ref)

        valid = ((i + 1) % tps) != 0
        du2 = jnp.zeros((tm, D_MODEL), F32)
        for j in range(n // FF_COLS):
            cs = slice(j * FF_COLS, (j + 1) * FF_COLS)
            cur = dhc_ref[:, cs].astype(F32)
            nxt = jnp.where(valid, halo_ref[:, cs].astype(F32), 0.0)
            full = jnp.concatenate([cur, nxt], axis=0)
            d1 = pltpu.roll(full, tm + HALO - 1, axis=0)[:tm]
            d2 = pltpu.roll(full, tm + HALO - 2, axis=0)[:tm]
            dh = (cur * wc_ref[2:3, cs] + d1 * wc_ref[1:2, cs] + d2 * wc_ref[0:1, cs]).astype(BF16)
            dh_ref[:, cs] = dh
            du2 = du2 + _dot(dh, wu_ref[cs, :])
            hv = h_ref[:, cs].astype(F32)
            dwc_ref[0:1, cs] += jnp.sum(hv * d2, axis=0, keepdims=True)
            dwc_ref[1:2, cs] += jnp.sum(hv * d1, axis=0, keepdims=True)
            dwc_ref[2:3, cs] += jnp.sum(hv * cur, axis=0, keepdims=True)
        x1 = x1_ref[...]
        r = lax.rsqrt(jnp.mean(x1 * x1, axis=-1, keepdims=True) + EPS)
        xn = x1 * r
        dgf_ref[...] += jnp.sum(du2 * xn, axis=0, keepdims=True)
        dxn = du2 * gf_ref[...]
        dx1 = dx2_ref[...] + r * (dxn - xn * jnp.mean(dxn * xn, axis=-1, keepdims=True))
        dx1_ref[...] = dx1
        dx1b_ref[...] = dx1.astype(BF16)

    row = lambda w: pl.BlockSpec((tm, w), lambda i: (i, 0))
    halo = pl.BlockSpec((HALO, n), lambda i: (jnp.minimum((i + 1) * (tm // HALO), last), 0))
    return _pcall(body, "ffn_bwd_up", (t // tm,),
                  [row(n), halo, row(n), row(D_MODEL), row(D_MODEL), _full((CONV_W, n)), _full((n, D_MODEL)),
                   _full((1, D_MODEL))],
                  [row(n), row(D_MODEL), row(D_MODEL), _full((1, D_MODEL)), _full((CONV_W, n))],
                  [_sds((t, n), BF16), _sds((t, D_MODEL)), _sds((t, D_MODEL), BF16), _sds((1, D_MODEL)),
                   _sds((CONV_W, n))],
                  )(dhc, dhc, h, dx2, x1, w_conv, w_up, g_ffn)


def _mix_bwd(dx1, y0, o, zh, zgt, pa, pb, w_glu, b_glu, gain, w_pa, w_pb, w_out, tm):
    t = dx1.shape[0]

    def body(dx1_ref, y0_ref, o_ref, zg_ref, zgt_ref, pa_ref, pb_ref, wglu_ref, bglu_ref, gain_ref, wpa_ref,
             wpb_ref, wout_ref,
             dy0_ref, do_ref, dzg_ref, dzgt_ref, m_ref, dpa_ref, dpb_ref, ya1_ref, dpre_ref, dbglu_ref, dgain_ref):
        @pl.when(pl.program_id(0) == 0)
        def _():
            dbglu_ref[...] = jnp.zeros_like(dbglu_ref)
            dgain_ref[...] = jnp.zeros_like(dgain_ref)

        dm = _dot_nt(dx1_ref[...], wout_ref[...])
        sga = _sigmoid(zgt_ref[:, 0:D_MODEL].astype(F32))
        sgb = _sigmoid(zgt_ref[:, D_MODEL:].astype(F32))
        pa = pa_ref[...].astype(F32)
        pb = pb_ref[...].astype(F32)
        m_ref[...] = (sga * pa + sgb * pb).astype(BF16)
        dzgt_ref[:, 0:D_MODEL] = (dm * pa * sga * (1.0 - sga)).astype(BF16)
        dzgt_ref[:, D_MODEL:] = (dm * pb * sgb * (1.0 - sgb)).astype(BF16)
        dpa = (dm * sga).astype(BF16)
        dpb = (dm * sgb).astype(BF16)
        dpa_ref[...] = dpa
        dpb_ref[...] = dpb
        dya2 = _dot_nt(dpa, wpa_ref[...])
        dyb = _dot_nt(dpb, wpb_ref[...])
        y0 = y0_ref[...]
        ya1 = _gelu(y0)
        ya1_ref[...] = ya1.astype(BF16)
        s = _sigmoid(_dot(ya1, wglu_ref[...]) + bglu_ref[...])
        dpre = dya2 * ya1 * s * (1.0 - s)
        dpre_ref[...] = dpre.astype(BF16)
        dbglu_ref[...] += jnp.sum(dpre, axis=0, keepdims=True)
        dya1 = dya2 * s + _dot_nt(dpre, wglu_ref[...])
        dy0_ref[...] = dya1 * _gelu_grad(y0)
        ov = o_ref[...]
        zg = zg_ref[...]
        oh = ov * _head_rms(ov)
        on = oh * gain_ref[...]
        sz = _sigmoid(zg)
        dzg_ref[...] = (dyb * on * (sz * (1.0 + zg * (1.0 - sz)))).astype(BF16)
        don = dyb * (zg * sz)
        dgain_ref[...] += jnp.sum(don * oh, axis=0, keepdims=True)
        doh = don * gain_ref[...]
        do_ref[...] = _head_rms(ov) * (doh - oh * _head_mean(doh * oh))

    row = lambda w: pl.BlockSpec((tm, w), lambda i: (i, 0))
    return _pcall(body, "mix_bwd", (t // tm,),
                  [row(D_MODEL), row(S5_WIDTH), row(HG_WIDTH), pl.BlockSpec((tm, HG_WIDTH), lambda i: (i, 3)),
                   row(2 * D_MODEL), row(D_MODEL), row(D_MODEL), _full((S5_WIDTH, S5_WIDTH)), _full((1, S5_WIDTH)),
                   _full((1, HG_WIDTH)), _full((S5_WIDTH, D_MODEL)), _full((HG_WIDTH, D_MODEL)),
                   _full((D_MODEL, D_MODEL))],
                  [row(S5_WIDTH), row(HG_WIDTH), row(HG_WIDTH), row(2 * D_MODEL), row(D_MODEL), row(D_MODEL),
                   row(D_MODEL), row(S5_WIDTH), row(S5_WIDTH), _full((1, S5_WIDTH)), _full((1, HG_WIDTH))],
                  [_sds((t, S5_WIDTH)), _sds((t, HG_WIDTH)), _sds((t, HG_WIDTH), BF16), _sds((t, 2 * D_MODEL), BF16),
                   _sds((t, D_MODEL), BF16), _sds((t, D_MODEL), BF16), _sds((t, D_MODEL), BF16),
                   _sds((t, S5_WIDTH), BF16), _sds((t, S5_WIDTH), BF16), _sds((1, S5_WIDTH)), _sds((1, HG_WIDTH))],
                  )(dx1, y0, o, zh, zgt, pa, pb, w_glu, b_glu, gain, w_pa, w_pb, w_out)


def _s5_bwd(dy0, za, xs, c_bands, b_bands, lam, dskip, nb, seq, ts):
    nts = seq // ts

    def body(dy0_ref, za_ref, xs_ref, halo_ref, cr_ref, ci_ref, br_ref, bi_ref, lam_ref, d_ref,
             dza_ref, a_ref, dlam_ref, dd_ref, dc_ref, acc_ref, st_ref):
        j = pl.program_id(0)

        @pl.when(j == 0)
        def _():
            dlam_ref[...] = jnp.zeros_like(dlam_ref)
            dd_ref[...] = jnp.zeros_like(dd_ref)
            dc_ref[...] = jnp.zeros_like(dc_ref)
            st_ref[...] = jnp.zeros_like(st_ref)

        for b in range(nb):
            dy0 = dy0_ref[b]
            for q in range(S5_BANDS):
                ch, st = _band(q)
                acc_ref[b, :, st] = _dot(dy0[:, ch], cr_ref[q])
                acc_ref[b, :, _im(st)] = _dot(dy0[:, ch], ci_ref[q])
                dc_ref[st, :] += _dot_tn(xs_ref[b, :, st], dy0[:, ch])
                dc_ref[_im(st), :] += _dot_tn(xs_ref[b, :, _im(st)], dy0[:, ch])
        _complex_scan(acc_ref, lam_ref, st_ref, nb, ts, reverse=True)
        shift = _shift_matrix(ts)
        top = lax.broadcasted_iota(jnp.int32, (SUBLANES, S5_LANES), 0) == 0
        for b in range(nb):
            a_ref[b] = acc_ref[b].astype(BF16)
            first = jnp.where(j == nts - 1, 0.0, halo_ref[b, HALO - 1:HALO, :].astype(F32))

            def shifted(cols):
                xp = jnp.dot(shift, xs_ref[b, :, cols], preferred_element_type=F32)
                return jnp.concatenate([xp[:SUBLANES] + jnp.where(top, first[:, cols], 0.0), xp[SUBLANES:]], axis=0)

            for cc in range(S5_N // S5_LANES):
                re = slice(cc * S5_LANES, (cc + 1) * S5_LANES)
                ar, ai, xr, xi = acc_ref[b, :, re], acc_ref[b, :, _im(re)], shifted(re), shifted(_im(re))
                dlam_ref[0:1, re] += jnp.sum(ar * xr + ai * xi, axis=0, keepdims=True)
                dlam_ref[1:2, re] += jnp.sum(ai * xr - ar * xi, axis=0, keepdims=True)
            dy0 = dy0_ref[b]
            for q in range(S5_BANDS):
                ch, st = _band(q)
                dza_ref[b, :, ch] = (_dot(a_ref[b, :, st], br_ref[q]) + _dot(a_ref[b, :, _im(st)], bi_ref[q])
                                     + d_ref[:, ch] * dy0[:, ch]).astype(BF16)
            dd_ref[...] += jnp.sum(dy0 * za_ref[b], axis=0, keepdims=True)

    tile = lambda j: nts - 1 - j
    tok = lambda w: pl.BlockSpec((nb, ts, w), lambda j: (0, tile(j), 0))
    halo = pl.BlockSpec((nb, HALO, 2 * S5_N), lambda j: (0, jnp.maximum(tile(j) * (ts // HALO) - 1, 0), 0))
    to_st, to_ch = _full((S5_BANDS, BAND_CH, BAND_ST)), _full((S5_BANDS, BAND_ST, BAND_CH))
    return _pcall(body, "s5_bwd", (nts,),
                  [tok(S5_WIDTH), tok(S5_WIDTH), tok(2 * S5_N), halo, to_st, to_st, to_ch, to_ch,
                   _full((2, S5_N)), _full((1, S5_WIDTH))],
                  [tok(S5_WIDTH), tok(2 * S5_N), _full((2, S5_N)), _full((1, S5_WIDTH)), _full((2 * S5_N, BAND_CH))],
                  [_sds((nb, seq, S5_WIDTH), BF16), _sds((nb, seq, 2 * S5_N), BF16), _sds((2, S5_N)),
                   _sds((1, S5_WIDTH)), _sds((2 * S5_N, BAND_CH))],
                  scratch=[pltpu.VMEM((nb, ts, 2 * S5_N), F32), pltpu.VMEM((nb, 2, S5_N), F32)],
                  )(dy0, za, xs, xs, *c_bands, *b_bands, lam, dskip)


def _hgrn_bwd(zh, do, sts, lb, nb, seq):
    nc = seq // CHUNK
    cps = HG_CHUNKS_PER_STEP

    def body(zh_ref, do_ref, sts_ref, lb_ref, dz_ref, dlb_ref, dst_ref):
        @pl.when(pl.program_id(0) == 0)
        def _():
            dst_ref[...] = jnp.zeros_like(dst_ref)
            dlb_ref[...] = jnp.zeros_like(dlb_ref)

        row = lax.broadcasted_iota(jnp.int32, (CHUNK, CHUNK), 0)
        causal = row >= lax.broadcasted_iota(jnp.int32, (CHUNK, CHUNK), 1)
        last_row = lax.broadcasted_iota(jnp.int32, (CHUNK, HG_HEAD), 0) == CHUNK - 1
        for cc in reversed(range(cps)):
            rows = slice(cc * CHUNK, (cc + 1) * CHUNK)
            for b in range(nb):
                for h in range(HG_HEADS):
                    hs = slice(h * HG_HEAD, (h + 1) * HG_HEAD)
                    zq = zh_ref[b, rows, h * HG_HEAD:(h + 1) * HG_HEAD]
                    zf = zh_ref[b, rows, HG_WIDTH + h * HG_HEAD:HG_WIDTH + (h + 1) * HG_HEAD]
                    zi = zh_ref[b, rows, 2 * HG_WIDTH + h * HG_HEAD:2 * HG_WIDTH + (h + 1) * HG_HEAD]
                    lbh = lb_ref[:, hs]
                    sf, f, sq, qa, bc, bm, bl = _hgrn_gates(zq, zf, lbh)
                    k = 1.0 - f
                    e_qt = jnp.exp(bc - bm)
                    e_kt = jnp.exp(bm - bc)
                    e_b = jnp.exp(bc)
                    e_kd = jnp.exp(bl - bc)
                    e_l = jnp.exp(bl)
                    qt, kt, qb, kd = qa * e_qt, k * e_kt, qa * e_b, k * e_kd
                    a = jnp.where(causal, _dot_nt(qt, kt), 0.0)
                    st = sts_ref[b, cc, h]
                    dst = dst_ref[b, h]
                    dov = do_ref[b, rows, hs]
                    da = jnp.where(causal, _dot_nt(dov, zi), 0.0)
                    qt_r, kt_r = qt.astype(BF16).astype(F32), kt.astype(BF16).astype(F32)
                    dqt = _dot(da, kt)
                    dkt = _dot_tn(da, qt)
                    dqb = _dot(dov, st)
                    di = _dot_tn(a, dov) + _dot_nt(kd, dst)
                    dkd = _dot(zi, dst)
                    de_l = jnp.sum(dst * st, axis=0, keepdims=True)
                    dst_ref[b, h] = dst * e_l + _dot_tn(dov, qb)
                    dqa = dqt * e_qt + dqb * e_b
                    dk = dkt * e_kt + dkd * e_kd
                    dbl = jnp.sum(dkd * kd, axis=0, keepdims=True) + de_l * e_l
                    db = dqt * qt_r - dkt * kt_r + dqb * qb - dkd * kd + jnp.where(last_row, dbl, 0.0)
                    df = _cumsum_rows(db, reverse=True) / f - dk
                    dzq = dqa * QSCALE * (sq * (1.0 + zq * (1.0 - sq)))
                    dzf = df * (1.0 - lbh) * sf * (1.0 - sf)
                    dz_ref[b, rows, h * HG_HEAD:(h + 1) * HG_HEAD] = dzq.astype(BF16)
                    dz_ref[b, rows, HG_WIDTH + h * HG_HEAD:HG_WIDTH + (h + 1) * HG_HEAD] = dzf.astype(BF16)
                    dz_ref[b, rows, 2 * HG_WIDTH + h * HG_HEAD:2 * HG_WIDTH + (h + 1) * HG_HEAD] = di.astype(BF16)
                    dlb_ref[:, hs] += jnp.sum(df * (1.0 - sf), axis=0, keepdims=True)

    rev = lambda c: nc // cps - 1 - c
    return _pcall(body, "hgrn_bwd", (nc // cps,),
                  [pl.BlockSpec((nb, cps * CHUNK, 4 * HG_WIDTH), lambda c: (0, rev(c), 0)),
                   pl.BlockSpec((nb, cps * CHUNK, HG_WIDTH), lambda c: (0, rev(c), 0)),
                   pl.BlockSpec((nb, cps, HG_HEADS, HG_HEAD, HG_HEAD), lambda c: (0, rev(c), 0, 0, 0)),
                   _full((1, HG_WIDTH))],
                  [pl.BlockSpec((nb, cps * CHUNK, 3 * HG_WIDTH), lambda c: (0, rev(c), 0)), _full((1, HG_WIDTH))],
                  [_sds((nb, seq, 3 * HG_WIDTH), BF16), _sds((1, HG_WIDTH))],
                  scratch=[pltpu.VMEM((nb, HG_HEADS, HG_HEAD, HG_HEAD), F32)])(zh, do, sts, lb)


def _in_proj_bwd(dza, dzh, dzg, dzgt, dx1, x, g_mix, w_in, tm):
    t = x.shape[0]

    def body(dza_ref, dzh_ref, dzg_ref, dzgt_ref, dx1_ref, x_ref, g_ref, w_ref, dz_ref, dx_ref, dg_ref):
        @pl.when(pl.program_id(0) == 0)
        def _():
            dg_ref[...] = jnp.zeros_like(dg_ref)

        c1, c2, c3 = S5_WIDTH, S5_WIDTH + 3 * HG_WIDTH, S5_WIDTH + 4 * HG_WIDTH
        dz_ref[:, 0:c1] = dza_ref[...]
        dz_ref[:, c1:c2] = dzh_ref[...]
        dz_ref[:, c2:c3] = dzg_ref[...]
        dz_ref[:, c3:] = dzgt_ref[...]
        du = _dot(dz_ref[...], w_ref[...])
        xv = x_ref[...]
        r = lax.rsqrt(jnp.mean(xv * xv, axis=-1, keepdims=True) + EPS)
        xn = xv * r
        dg_ref[...] += jnp.sum(du * xn, axis=0, keepdims=True)
        dxn = du * g_ref[...]
        dx_ref[...] = dx1_ref[...] + r * (dxn - xn * jnp.mean(dxn * xn, axis=-1, keepdims=True))

    row = lambda w: pl.BlockSpec((tm, w), lambda i: (i, 0))
    return _pcall(body, "in_proj_bwd", (t // tm,),
                  [row(S5_WIDTH), row(3 * HG_WIDTH), row(HG_WIDTH), row(2 * D_MODEL), row(D_MODEL), row(D_MODEL),
                   _full((1, D_MODEL)), _full((N_IN, D_MODEL))],
                  [row(N_IN), row(D_MODEL), _full((1, D_MODEL))],
                  [_sds((t, N_IN), BF16), _sds((t, D_MODEL)), _sds((1, D_MODEL))],
                  )(dza, dzh, dzg, dzgt, dx1, x, g_mix, w_in)


def _after(value, token):
    return value + token[0, 0]


def _local_step(x3, tgt3, weights, sp, emit, emit_small):
    nb, seq, _ = x3.shape
    t = nb * seq
    tm = _token_tile(seq)
    x = x3.reshape(t, D_MODEL)
    tgt = tgt3.reshape(t, D_MODEL)
    row = lambda v: v.reshape(1, -1)

    a_re, a_im, b_re, b_im = sp["s5_a_re"], sp["s5_a_im"], sp["s5_b_re"], sp["s5_b_im"]
    ldt = sp["s5_log_dt"].reshape(S5_GROUPS, 1)
    lr, li, bb_re, bb_im, lb = _params_fwd(a_re, a_im, ldt, b_re, b_im, sp["hg_lb_logits"])
    lam = jnp.concatenate([lr.reshape(1, S5_N), li.reshape(1, S5_N)], axis=0)
    swap = lambda m: m.transpose(0, 2, 1)
    b_to_st = (_band_blocks(bb_re), _band_blocks(bb_im))
    b_to_ch = (_band_blocks(swap(bb_re)), _band_blocks(swap(bb_im)))
    c_to_ch = (_band_blocks(swap(sp["s5_c_re"])), _band_blocks(swap(-sp["s5_c_im"])))
    c_to_st = (_band_blocks(sp["s5_c_re"]), _band_blocks(-sp["s5_c_im"]))

    g_mix, g_ffn, g_final = row(sp["g_mix"]), row(sp["g_ffn"]), row(sp["g_final"])
    b_glu, gain, dskip, b_conv = row(sp["b_glu"]), row(sp["hg_norm_gain"]), row(sp["s5_d"]), row(sp["b_conv"])

    w_in = weights("in", lam, *b_to_st, *b_to_ch, *c_to_ch, *c_to_st)["w_in"]
    wide = min(2 * tm, seq)
    u, za, zh, zgt = _in_proj(x, g_mix, w_in, wide)
    seqs = lambda v: v.reshape(nb, seq, v.shape[-1])
    toks = lambda v: v.reshape(t, v.shape[-1])
    xs3, y0 = _s5_fwd(seqs(za), b_to_st, lam, c_to_ch, dskip, nb, seq, tm)
    xs, y0 = toks(xs3), toks(y0)
    o3, sts = _hgrn_fwd(zh.reshape(nb, seq, 4 * HG_WIDTH), lb, nb, seq)
    o = o3.reshape(t, HG_WIDTH)
    wm = weights("mix", y0, o3)
    weights.forward("ffn", wm["w_out"])
    x1, u2, pa, pb, ya2, yb = _mix_fwd(x, y0, o, zh, zgt, wm["w_glu"], b_glu, gain, wm["w_pa"], wm["w_pb"],
                                       wm["w_out"], g_ffn, wide)
    wf = weights("ffn", u2)
    h = _ffn_up(u2, wf["w_up"], min(4 * tm, t))
    hc, a, dx2, dx2b, loss, dg_final = _ffn_down_loss(h, x1, tgt, wf["w_conv"], b_conv, wf["w_down"], g_final,
                                                      seq, tm)

    def wgrad(a, b, name):
        return _wgrad(a, b, name, 512 if a.shape[1] % 512 == 0 else 256, out_dtype=BF16)

    dhc, db_conv = _ffn_bwd_act(dx2b, hc, wf["w_down"], tm)
    dw_down = wgrad(a, dx2b, "dw_down")
    dh, dx1, dx1b, dg_ffn, dw_conv = _ffn_bwd_up(dhc, h, dx2, x1, wf["w_conv"], wf["w_up"], g_ffn, seq, tm)
    sent = emit({"w_up": wgrad(dh, u2, "dw_up"), "w_conv": dw_conv, "w_down": dw_down})
    (dy0, do, dzg, dzgt, m, dpa, dpb, ya1, dpre, db_glu, dgain) = _mix_bwd(
        dx1b, y0, o, zh, zgt, pa, pb, wm["w_glu"], _after(b_glu, sent), gain, wm["w_pa"], wm["w_pb"], wm["w_out"],
        wide)
    sent = emit({"w_out": wgrad(m, dx1b, "dw_out"), "w_pa": wgrad(ya2, dpa, "dw_pa"),
                 "w_pb": wgrad(yb, dpb, "dw_pb"), "w_glu": wgrad(ya1, dpre, "dw_glu")})
    dzh3, dlb = _hgrn_bwd(zh.reshape(nb, seq, 4 * HG_WIDTH), do.reshape(nb, seq, HG_WIDTH), sts, _after(lb, sent),
                          nb, seq)
    dza, a_s5, dlam, dd, dc_band = _s5_bwd(seqs(dy0), seqs(za), xs3, c_to_st, b_to_ch, lam, dskip, nb, seq, tm)
    dza, a_s5 = toks(dza), toks(a_s5)
    dz, dx, dg_mix = _in_proj_bwd(dza, dzh3.reshape(t, 3 * HG_WIDTH), dzg, dzgt, dx1, x, g_mix, w_in, wide)
    sent = emit({"w_in": wgrad(dz, u, "dw_in")})

    band = HG_HEAD
    dbb_band = _wgrad(a_s5, za, "dbb_s5", 512, band=band, after=sent)
    dbb_re = swap(_diag_blocks(dbb_band[:S5_N], S5_STATE, S5_GROUP))
    dbb_im = swap(_diag_blocks(dbb_band[S5_N:], S5_STATE, S5_GROUP))
    dc_re = swap(_diag_blocks(dc_band[:S5_N], S5_STATE, S5_GROUP))
    dc_im = -swap(_diag_blocks(dc_band[S5_N:], S5_STATE, S5_GROUP))
    da_re, da_im, dldt, db_re, db_im, dlogits = _params_bwd(
        a_re, a_im, ldt, b_re, b_im, sp["hg_lb_logits"],
        dlam[0].reshape(S5_GROUPS, S5_STATE), dlam[1].reshape(S5_GROUPS, S5_STATE), dbb_re, dbb_im, dlb)
    emit_small({"g_mix": dg_mix, "s5_a_re": da_re, "s5_a_im": da_im, "s5_log_dt": dldt.reshape(1, S5_GROUPS),
                "s5_b_re": db_re, "s5_b_im": db_im, "s5_c_re": dc_re, "s5_c_im": dc_im, "s5_d": dd, "b_glu": db_glu,
                "hg_lb_logits": dlogits, "hg_norm_gain": dgain, "g_ffn": dg_ffn, "b_conv": db_conv,
                "g_final": dg_final, "loss": loss})
    return dx.reshape(nb, seq, D_MODEL)


def _mesh_peers():
    x, y, c = lax.axis_index("x"), lax.axis_index("y"), lax.axis_index("c")
    peers = []
    for k in range(1, N_DEV):
        px, py, pc = (1 - x if k & 4 else x), (1 - y if k & 2 else y), (1 - c if k & 1 else c)
        peers.append((k, (px, py, pc), 4 * px + 2 * py + pc))
    return 4 * x + 2 * y + c, peers


_HBM = pl.BlockSpec(memory_space=pltpu.HBM)
_SEM = pl.BlockSpec(memory_space=pltpu.SEMAPHORE)


_EFFECT = pltpu.CompilerParams(has_side_effects=pltpu.SideEffectType.DATAFLOW_SIDE_EFFECTING)


def _remote(src, dst, send_sem, recv_sem, to):
    return pltpu.make_async_remote_copy(src_ref=src, dst_ref=dst, send_sem=send_sem, recv_sem=recv_sem,
                                        device_id=to, device_id_type=pl.DeviceIdType.MESH)


def _exchange_start(name, arrays, after):
    n = len(arrays)
    srcs = [pltpu.with_memory_space_constraint(a, pltpu.HBM) for a in arrays]
    lands = [pltpu.with_memory_space_constraint(lax.empty(a.shape, a.dtype), pltpu.HBM) for a in arrays]
    copies = (N_DEV - 1) * n

    def body(*refs):
        src_refs, land_refs = refs[:n], refs[n:2 * n]
        send_sems, recv_sems, token = refs[2 * n + 1], refs[2 * n + 2], refs[-1]
        my_slab, peers = _mesh_peers()
        for k, peer, slab in peers:
            for i in range(n):
                s = (k - 1) * n + i
                _remote(src_refs[i].at[slab], land_refs[i].at[my_slab], send_sems.at[s], recv_sems.at[s], peer).start()
        token[...] = jnp.zeros_like(token)

    outs = pl.pallas_call(
        body, name=name,
        out_shape=(pltpu.SemaphoreType.DMA((copies,)), pltpu.SemaphoreType.DMA((copies,)),
                   *[pltpu.HBM(a.shape, a.dtype) for a in lands], _sds((SUBLANES, LANES))),
        in_specs=[_HBM] * (2 * n) + [pl.BlockSpec(memory_space=pl.ANY)],
        out_specs=(_SEM, _SEM, *[_HBM] * n, pl.BlockSpec(memory_space=pltpu.VMEM)),
        input_output_aliases={n + i: 2 + i for i in range(n)}, compiler_params=_EFFECT,
    )(*srcs, *lands, after)
    return (outs[0], outs[1], srcs, outs[2:2 + n]), outs[-1]


def _exchange_wait(name, state, *after):
    send_sems, recv_sems, srcs, lands = state
    n = len(lands)

    def body(*refs):
        src_refs, land_refs = refs[:n], refs[n:2 * n]
        send_ref, recv_ref = refs[2 * n], refs[2 * n + 1]
        _, peers = _mesh_peers()
        for k, peer, slab in peers:
            for i in range(n):
                s = (k - 1) * n + i
                copy = _remote(src_refs[i].at[slab], land_refs[i].at[slab], send_ref.at[s], recv_ref.at[s], peer)
                copy.wait_send()
                copy.wait_recv()

    outs = pl.pallas_call(
        body, name=name,
        out_shape=tuple(pltpu.HBM(a.shape, a.dtype) for a in lands),
        in_specs=[_HBM] * (2 * n) + [_SEM, _SEM] + [pl.BlockSpec(memory_space=pl.ANY)] * len(after),
        out_specs=tuple([_HBM] * n),
        input_output_aliases={n + i: i for i in range(n)}, compiler_params=_EFFECT,
    )(*srcs, *lands, send_sems, recv_sems, *after)
    return list(outs), list(srcs)


def _slab(pos):
    return 4 * pos[0] + 2 * pos[1] + pos[2]


def _chip_routes():
    x, y, c = lax.axis_index("x"), lax.axis_index("y"), lax.axis_index("c")
    return (x, y, c), (x, y, 1 - c), [(1 - x, y, c), (x, 1 - y, c), (1 - x, 1 - y, c)]


def _gather_start(name, arrays, after):
    n = len(arrays)
    me = 4 * lax.axis_index("x") + 2 * lax.axis_index("y") + lax.axis_index("c")
    srcs = [pltpu.with_memory_space_constraint(a, pltpu.HBM) for a in arrays]
    lands = [pltpu.with_memory_space_constraint(
        lax.dynamic_update_slice_in_dim(lax.empty((N_DEV,) + a.shape, a.dtype), a[None], me, 0), pltpu.HBM)
        for a in arrays]

    def body(*refs):
        src_refs, land_refs = refs[:n], refs[n:2 * n]
        send_sems, recv_sems, token = refs[2 * n + 1], refs[2 * n + 2], refs[-1]
        mine, sibling, chips = _chip_routes()
        for k, to in enumerate([sibling] + chips):
            for i in range(n):
                _remote(src_refs[i], land_refs[i].at[_slab(mine)], send_sems.at[k * n + i], recv_sems.at[k * n + i],
                        to).start()
        token[...] = jnp.zeros_like(token)

    outs = pl.pallas_call(
        body, name=name,
        out_shape=(pltpu.SemaphoreType.DMA((4 * n,)), pltpu.SemaphoreType.DMA((4 * n,)),
                   *[pltpu.HBM(a.shape, a.dtype) for a in lands], _sds((SUBLANES, LANES))),
        in_specs=[_HBM] * (2 * n) + [pl.BlockSpec(memory_space=pl.ANY)],
        out_specs=(_SEM, _SEM, *[_HBM] * n, pl.BlockSpec(memory_space=pltpu.VMEM)),
        input_output_aliases={n + i: 2 + i for i in range(n)}, compiler_params=_EFFECT,
    )(*srcs, *lands, after)
    return (outs[0], outs[1], srcs, outs[2:2 + n]), outs[-1]


def _gather_forward(name, state, *after):
    send_a, recv_a, srcs, lands = state
    n = len(lands)

    def body(*refs):
        land_refs, recv_a_ref = refs[:n], refs[n]
        send_b, recv_b = refs[n + 1 + len(after)], refs[n + 2 + len(after)]
        mine, sibling, chips = _chip_routes()
        for j, chip in enumerate(chips):
            for i in range(n):
                block = land_refs[i].at[_slab(chip)]
                _remote(block, block, send_b.at[j * n + i], recv_a_ref.at[(1 + j) * n + i], chip).wait_recv()
                _remote(block, block, send_b.at[j * n + i], recv_b.at[j * n + i], sibling).start()

    outs = pl.pallas_call(
        body, name=name,
        out_shape=(pltpu.SemaphoreType.DMA((3 * n,)), pltpu.SemaphoreType.DMA((3 * n,)),
                   *[pltpu.HBM(a.shape, a.dtype) for a in lands]),
        in_specs=[_HBM] * n + [_SEM] + [pl.BlockSpec(memory_space=pl.ANY)] * len(after),
        out_specs=(_SEM, _SEM, *[_HBM] * n),
        input_output_aliases={i: 2 + i for i in range(n)}, compiler_params=_EFFECT,
    )(*lands, recv_a, *after)
    return (send_a, recv_a, srcs, list(outs[2:])), (outs[0], outs[1])


def _gather_wait(name, state, forwarded, *after):
    send_a, recv_a, srcs, lands = state
    send_b, recv_b = forwarded
    n = len(lands)

    def body(*refs):
        src_refs, land_refs = refs[:n], refs[n:2 * n]
        sa, ra, sb, rb = refs[2 * n:2 * n + 4]
        mine, sibling, chips = _chip_routes()
        for i in range(n):
            for k, to in enumerate([sibling] + chips):
                _remote(src_refs[i], land_refs[i].at[_slab(mine)], sa.at[k * n + i], ra.at[k * n + i], to).wait_send()
            theirs = land_refs[i].at[_slab(sibling)]
            _remote(theirs, theirs, sa.at[i], ra.at[i], sibling).wait_recv()
            for j, chip in enumerate(chips):
                sent = land_refs[i].at[_slab(chip)]
                got = land_refs[i].at[_slab((chip[0], chip[1], sibling[2]))]
                _remote(sent, sent, sb.at[j * n + i], rb.at[j * n + i], sibling).wait_send()
                _remote(got, got, sb.at[j * n + i], rb.at[j * n + i], sibling).wait_recv()

    outs = pl.pallas_call(
        body, name=name,
        out_shape=tuple(pltpu.HBM(a.shape, a.dtype) for a in lands),
        in_specs=[_HBM] * (2 * n) + [_SEM] * 4 + [pl.BlockSpec(memory_space=pl.ANY)] * len(after),
        out_specs=tuple([_HBM] * n),
        input_output_aliases={n + i: i for i in range(n)}, compiler_params=_EFFECT,
    )(*srcs, *lands, send_a, recv_a, send_b, recv_b, *after)
    return list(outs), list(srcs)


def _join_cols(parts, name, tr):
    _, r, c = parts.shape

    def body(p_ref, o_ref):
        for j in range(N_DEV):
            o_ref[:, j * c:(j + 1) * c] = p_ref[j]

    return _pcall(body, name, (r // tr,), [pl.BlockSpec((N_DEV, tr, c), lambda i: (0, i, 0))],
                  pl.BlockSpec((tr, N_DEV * c), lambda i: (i, 0)), _sds((r, N_DEV * c), parts.dtype))(parts)


def _split_cols(full, name, tr):
    r, c = full.shape[0], full.shape[1] // N_DEV

    def body(f_ref, o_ref):
        for j in range(N_DEV):
            o_ref[j] = f_ref[:, j * c:(j + 1) * c]

    return _pcall(body, name, (r // tr,), [pl.BlockSpec((tr, N_DEV * c), lambda i: (i, 0))],
                  pl.BlockSpec((N_DEV, tr, c), lambda i: (0, i, 0)), _sds((N_DEV, r, c), full.dtype))(full)


def _my_slab():
    return (4 * lax.axis_index("x") + 2 * lax.axis_index("y") + lax.axis_index("c")).astype(jnp.int32).reshape(1)


def _adamw(parts, sent, w, m, v, name, tile):
    _, rows, cols = w.shape

    def body(me_ref, p_ref, s_ref, w_ref, m_ref, v_ref, g_out, d_out, m_out, v_out):
        me = me_ref[0]
        g = jnp.where(me == 0, s_ref[0], p_ref[0]).astype(F32)
        for k in range(1, N_DEV):
            g = g + jnp.where(me == k, s_ref[0], p_ref[k]).astype(F32)
        m1 = ADAM_B1 * m_ref[0] + (1.0 - ADAM_B1) * g
        v1 = ADAM_B2 * v_ref[0] + (1.0 - ADAM_B2) * (g * g)
        m_hat = m1 / (1.0 - ADAM_B1 ** ADAM_STEP)
        v_hat = v1 / (1.0 - ADAM_B2 ** ADAM_STEP)
        g_out[0] = g
        d_out[0] = -ADAM_LR * (m_hat / (jnp.sqrt(v_hat) + ADAM_EPS) + ADAM_WD * w_ref[0])
        m_out[0] = m1
        v_out[0] = v1

    row = pl.BlockSpec((1, tile, cols), lambda i, me: (0, i, 0))
    return pl.pallas_call(
        body, name=name, out_shape=[_sds((1, rows, cols))] * 4,
        grid_spec=pltpu.PrefetchScalarGridSpec(
            num_scalar_prefetch=1, grid=(rows // tile,),
            in_specs=[pl.BlockSpec((N_DEV, tile, cols), lambda i, me: (0, i, 0)),
                      pl.BlockSpec((1, tile, cols), lambda i, me: (me[0], i, 0)), row, row, row],
            out_specs=[row, row, row, row]),
        compiler_params=pltpu.CompilerParams(dimension_semantics=("arbitrary",), vmem_limit_bytes=VMEM_LIMIT),
    )(_my_slab(), parts, sent, w, m, v)


BIG = {
    "w_in": ((N_IN // N_DEV, D_MODEL), False, N_IN // N_DEV // 3),
    "w_glu": ((S5_WIDTH // N_DEV, S5_WIDTH), False, S5_WIDTH // N_DEV),
    "w_pa": ((S5_WIDTH, D_MODEL // N_DEV), True, S5_WIDTH),
    "w_pb": ((HG_WIDTH, D_MODEL // N_DEV), True, HG_WIDTH),
    "w_out": ((D_MODEL // N_DEV, D_MODEL), False, D_MODEL // N_DEV),
    "w_up": ((2 * D_FF // N_DEV, D_MODEL), False, 2 * D_FF // N_DEV // 4),
    "w_conv": ((CONV_W, 2 * D_FF // N_DEV), True, CONV_W),
    "w_down": ((D_FF // N_DEV, D_MODEL), False, D_FF // N_DEV // 2),
}
TRANSPOSED = ("w_in", "w_up", "s5_b_re", "s5_b_im")
UNALIGNED_COLS = ("w_conv",)


def _stored(n, arr):
    return jnp.swapaxes(arr, -1, -2) if n in TRANSPOSED else arr


def _join_shards(n, parts):
    (a, b), by_cols, _ = BIG[n]
    if not by_cols:
        return parts.reshape(N_DEV * a, b)
    if n in UNALIGNED_COLS:
        return _join_cols(parts, "join_" + n, min(a, 256))
    return parts.transpose(1, 0, 2).reshape(a, N_DEV * b)


def _split_shards(n, full):
    (a, b), by_cols, _ = BIG[n]
    if not by_cols:
        return full.reshape(N_DEV, a, b)
    if n in UNALIGNED_COLS:
        return _split_cols(full, "split_" + n, min(a, 256))
    return full.reshape(a, N_DEV, b).transpose(1, 0, 2)


SMALL_CORE = {
    "s5_b_re": GSC, "s5_b_im": GSC, "s5_c_re": GSC, "s5_c_im": GSC,
    "g_mix": (1, D_MODEL), "g_ffn": (1, D_MODEL), "g_final": (1, D_MODEL), "s5_d": (1, S5_WIDTH),
    "b_glu": (1, S5_WIDTH), "hg_norm_gain": (1, HG_WIDTH), "hg_lb_logits": (2, HG_WIDTH), "b_conv": (1, 2 * D_FF),
    "s5_log_dt": (1, S5_GROUPS), "s5_a_re": (S5_GROUPS, S5_STATE), "s5_a_im": (S5_GROUPS, S5_STATE), "loss": (1, 1),
}
BLOCK_ROWS = 32


def _small_rows():
    rows, r = {}, 0
    for n, core in SMALL_CORE.items():
        rows[n] = r
        r += BLOCK_ROWS if len(core) == 3 else -(-math.prod(core) // PACK_W)
    return rows, -(-r // SUBLANES) * SUBLANES


SMALL_ROW, SMALL_ROWS = _small_rows()


def _small_pieces(name):
    r, core = SMALL_ROW[name], SMALL_CORE[name]
    if len(core) == 3:
        return [((g, slice(None), slice(None)), slice(r + S5_GROUP * (g % 2), r + S5_GROUP * (g % 2 + 1)),
                 slice(S5_STATE * (g // 2), S5_STATE * (g // 2 + 1))) for g in range(S5_GROUPS)]
    pieces = []
    for i in range(core[0]):
        for c0 in range(0, core[1], PACK_W):
            w, flat = min(PACK_W, core[1] - c0), i * core[1] + c0
            pieces.append(((slice(i, i + 1), slice(c0, c0 + w)), slice(r + flat // PACK_W, r + flat // PACK_W + 1),
                           slice(flat % PACK_W, flat % PACK_W + w)))
    return pieces


def _core_index(ref, name, idx):
    return (0,) * (len(ref.shape) - len(SMALL_CORE[name])) + idx


def _pack_small_grads(grads):
    names = list(SMALL_CORE)

    def body(*refs):
        pack = refs[-1]
        pack[...] = jnp.zeros_like(pack)
        for ref, n in zip(refs, names):
            for idx, rows, lanes in _small_pieces(n):
                pack[rows, lanes] = ref[_core_index(ref, n, idx)]

    return _pcall(body, "pack_small_grads", (1,), [_full(grads[n].shape) for n in names],
                  _full((SMALL_ROWS, PACK_W)), _sds((SMALL_ROWS, PACK_W)))(*[grads[n] for n in names])


def _adamw_small(parts, sent, names, rows, given, name):
    lo, hi = rows
    k = len(names)
    shapes = [given[n].shape for n in names]

    def body(*refs):
        me, p_ref, s_ref, ins, outs = refs[0][0], refs[1], refs[2], refs[3:3 + 3 * k], refs[3 + 3 * k:3 + 7 * k]
        packs, results = refs[3 + 7 * k:6 + 7 * k], refs[6 + 7 * k:]
        for j, pack in enumerate(packs):
            pack[...] = jnp.zeros_like(pack)
            for ref, n in zip(ins[j * k:(j + 1) * k], names):
                for idx, prow, lanes in _small_pieces(n):
                    pack[slice(prow.start - lo, prow.stop - lo), lanes] = ref[_core_index(ref, n, idx)]
        mine = s_ref[lo:hi, :]
        g = jnp.where(me == 0, mine, p_ref[0, lo:hi, :])
        for d in range(1, N_DEV):
            g = g + jnp.where(me == d, mine, p_ref[d, lo:hi, :])
        m1 = ADAM_B1 * packs[1][...] + (1.0 - ADAM_B1) * g
        v1 = ADAM_B2 * packs[2][...] + (1.0 - ADAM_B2) * (g * g)
        m_hat = m1 / (1.0 - ADAM_B1 ** ADAM_STEP)
        v_hat = v1 / (1.0 - ADAM_B2 ** ADAM_STEP)
        results[0][...] = g
        results[1][...] = -ADAM_LR * (m_hat / (jnp.sqrt(v_hat) + ADAM_EPS) + ADAM_WD * packs[0][...])
        results[2][...] = m1
        results[3][...] = v1
        for j, result in enumerate(results):
            for ref, n in zip(outs[j * k:(j + 1) * k], names):
                for idx, prow, lanes in _small_pieces(n):
                    ref[_core_index(ref, n, idx)] = result[slice(prow.start - lo, prow.stop - lo), lanes]

    flat = _pcall(body, name, (1,),
                  [pl.BlockSpec(memory_space=pltpu.SMEM), _full(parts.shape), _full(sent.shape)]
                  + [_full(s) for s in shapes] * 3,
                  [_full(s) for s in shapes] * 4, [_sds(s) for s in shapes] * 4,
                  scratch=[pltpu.VMEM((hi - lo, PACK_W), F32)] * 7,
                  )(_my_slab(), parts, sent, *[given[pre + n] for pre in ("", "m_", "v_") for n in names])
    return {n: [flat[j * k + i] for j in range(4)] for i, n in enumerate(names)}


def kernel(x, g_mix, w_in, s5_a_re, s5_a_im, s5_log_dt, s5_b_re, s5_b_im, s5_c_re, s5_c_im, s5_d, w_glu, b_glu, hg_lb_logits, hg_norm_gain, w_pa, w_pb, w_out, g_ffn, w_up, w_conv, b_conv, w_down, g_final, loss_target, m_g_mix, m_w_in, m_s5_a_re, m_s5_a_im, m_s5_log_dt, m_s5_b_re, m_s5_b_im, m_s5_c_re, m_s5_c_im, m_s5_d, m_w_glu, m_b_glu, m_hg_lb_logits, m_hg_norm_gain, m_w_pa, m_w_pb, m_w_out, m_g_ffn, m_w_up, m_w_conv, m_b_conv, m_w_down, m_g_final, v_g_mix, v_w_in, v_s5_a_re, v_s5_a_im, v_s5_log_dt, v_s5_b_re, v_s5_b_im, v_s5_c_re, v_s5_c_im, v_s5_d, v_w_glu, v_b_glu, v_hg_lb_logits, v_hg_norm_gain, v_w_pa, v_w_pb, v_w_out, v_g_ffn, v_w_up, v_w_conv, v_b_conv, v_w_down, v_g_final):
    given = dict(locals())
    small_names = [n for n in SMALL_CORE if n != "loss"]

    pay = {n: given[n][0] if n == "w_conv" else _stored(n, given[n])[0].astype(BF16) for n in BIG}
    groups = {"in": ["w_in"], "mix": ["w_glu", "w_pa", "w_pb", "w_out"], "ffn": ["w_up", "w_down", "w_conv"]}
    gathers, order = {}, pay["w_in"]
    for grp, names in groups.items():
        gathers[grp], order = _gather_start("gather_" + grp + "_start", [pay[n] for n in names], order)

    forwards = {}

    def forward(grp, *after):
        if grp == "in":
            after = (*after, order)
        forwards[grp] = _gather_forward("gather_" + grp + "_forward", gathers[grp], *after)

    def weights(grp, *after):
        if grp not in forwards:
            forward(grp, *after)
        got, _ = _gather_wait("gather_" + grp + "_wait", *forwards[grp], *after)
        return {n: _join_shards(n, g) for n, g in zip(groups[grp], got)}

    weights.forward = forward

    in_flight, started = [], []

    def emit(grads):
        names = list(grads)
        state, token = _exchange_start("grads_" + names[0] + "_start", [_split_shards(n, grads[n]) for n in names],
                                       grads[names[0]])
        in_flight.append((names, state))
        return token

    def emit_small(grads):
        pack = _pack_small_grads(grads)
        state, token = _gather_start("grads_small_start", [pack], pack)
        in_flight.append((["small"], state))
        started.append(token)

    sp = {n: (given[n] if n in ("g_final", "hg_lb_logits") else _stored(n, given[n])[0]) for n in small_names}
    sp["g_mix"] = _after(sp["g_mix"], order)
    dx = _local_step(x, loss_target, weights, sp, emit, emit_small)

    res = {}
    after = [started[-1]]
    for names, state in in_flight:
        if names == ["small"]:
            state, forwarded = _gather_forward("grads_small_forward", state, *after)
            parts, sent = _gather_wait("grads_small_wait", state, forwarded)
        else:
            parts, sent = _exchange_wait("grads_" + names[0] + "_wait", state, *after)
        if names != ["small"]:
            after = []
            for n, part, mine in zip(names, parts, sent):
                raw = _adamw(part, mine, *[_stored(n, given[pre + n]) for pre in ("", "m_", "v_")], "adamw_" + n,
                             BIG[n][2])
                res[n] = [_stored(n, r) for r in raw]
                after.append(raw[0])
            continue
        sgiven = {pre + n: _stored(n, given[pre + n]) for pre in ("", "m_", "v_") for n in small_names}
        for pre in ("", "m_", "v_"):
            sgiven[pre + "g_final"] = given[pre + "g_final"].reshape(1, D_MODEL)
            sgiven[pre + "loss"] = jnp.zeros((1, 1), F32)
        raw = _adamw_small(parts[0], sent[0], list(SMALL_CORE), (0, SMALL_ROWS), sgiven, "adamw_small")
        res.update({n: [_stored(n, r) for r in raw[n]] for n in small_names})
        res["g_final"] = [r.reshape(D_MODEL) for r in raw["g_final"]]
        total_loss = raw["loss"][0].reshape(())
        after = [raw["s5_b_re"][0], raw["g_mix"][0]]
    return (total_loss, dx, *[res[n][0] for n in WEIGHT_ORDER], *[res[n][1] for n in WEIGHT_ORDER],
            *[res[n][2] for n in WEIGHT_ORDER], *[res[n][3] for n in WEIGHT_ORDER])
```

```python
import math

import jax
import jax.numpy as jnp
from jax import lax
from jax.experimental import pallas as pl
from jax.experimental.pallas import tpu as pltpu

F32 = jnp.float32
BF16 = jnp.bfloat16

D_MODEL = 1024
S5_WIDTH = 512
S5_GROUP = 16
S5_GROUPS = 32
S5_STATE = 64
S5_N = S5_GROUPS * S5_STATE
HG_WIDTH = 512
HG_HEAD = 128
HG_HEADS = 4
D_FF = 2816
CONV_W = 3
CHUNK = 64
N_IN = S5_WIDTH + 4 * HG_WIDTH + 2 * D_MODEL
EPS = 1e-6
QSCALE = HG_HEAD ** -0.5

ADAM_LR = 0.001
ADAM_B1 = 0.9
ADAM_B2 = 0.999
ADAM_EPS = 1e-08
ADAM_WD = 0.01
ADAM_STEP = 10

N_DEV = 8
V7X_VMEM_BYTES = 64 * 1024 * 1024
VMEM_LIMIT = V7X_VMEM_BYTES * 7 // 8
SUBLANES = 8
LANES = 128
PACK_W = 1024

WEIGHT_ORDER = ("g_mix", "w_in", "s5_a_re", "s5_a_im", "s5_log_dt", "s5_b_re", "s5_b_im", "s5_c_re", "s5_c_im",
                "s5_d", "w_glu", "b_glu", "hg_lb_logits", "hg_norm_gain", "w_pa", "w_pb", "w_out", "g_ffn",
                "w_up", "w_conv", "b_conv", "w_down", "g_final")


def _pcall(body, name, grid, in_specs, out_specs, out_shape, scratch=()):
    return pl.pallas_call(
        body, name=name, grid=grid, in_specs=in_specs, out_specs=out_specs, out_shape=out_shape,
        scratch_shapes=list(scratch),
        compiler_params=pltpu.CompilerParams(dimension_semantics=("arbitrary",) * len(grid),
                                             vmem_limit_bytes=VMEM_LIMIT),
    )


def _full(shape):
    return pl.BlockSpec(shape, lambda *_: (0,) * len(shape))


def _sds(shape, dtype=F32):
    return jax.ShapeDtypeStruct(shape, dtype)


def _dot(a, b):
    return jnp.dot(a.astype(BF16), b.astype(BF16), preferred_element_type=F32)


def _dot_nt(a, b):
    return lax.dot_general(a.astype(BF16), b.astype(BF16), (((1,), (1,)), ((), ())), preferred_element_type=F32)


def _dot_tn(a, b):
    return lax.dot_general(a.astype(BF16), b.astype(BF16), (((0,), (0,)), ((), ())), preferred_element_type=F32)


def _sigmoid(x):
    return jax.nn.sigmoid(x)


GELU_C = math.sqrt(2.0 / math.pi)
GELU_A = 0.044715


def _gelu(x):
    return 0.5 * x * (1.0 + jnp.tanh(GELU_C * (x + GELU_A * (x * x * x))))


def _gelu_grad(x):
    t = jnp.tanh(GELU_C * (x + GELU_A * (x * x * x)))
    return 0.5 * (1.0 + t) + 0.5 * x * (1.0 - t * t) * (GELU_C * (1.0 + 3.0 * GELU_A * x * x))


def _cumsum_rows(v, reverse=False):
    n = v.shape[0]
    row = lax.broadcasted_iota(jnp.int32, v.shape, 0)
    s = 1
    while s < n:
        if reverse:
            v = v + jnp.where(row < n - s, pltpu.roll(v, n - s, axis=0), 0.0)
        else:
            v = v + jnp.where(row >= s, pltpu.roll(v, s, axis=0), 0.0)
        s *= 2
    return v


def _token_tile(seq):
    return min(256, seq)


def _s5_coeffs(a_re, a_im, ldt):
    dt = jnp.exp(ldt)
    mag = jnp.exp(a_re * dt)
    ang = a_im * dt
    lb_re = mag * jnp.cos(ang)
    lb_im = mag * jnp.sin(ang)
    den = a_re * a_re + a_im * a_im
    n_re = lb_re - 1.0
    n_im = lb_im
    co_re = (n_re * a_re + n_im * a_im) / den
    co_im = (n_im * a_re - n_re * a_im) / den
    return lb_re, lb_im, co_re, co_im


GS, GSC = (S5_GROUPS, S5_STATE), (S5_GROUPS, S5_GROUP, S5_STATE)


def _params_fwd(a_re, a_im, ldt, bt_re, bt_im, logits):
    def body(are, aim, ld, bre, bim, lg, lr_o, li_o, bbr_o, bbi_o, lb_o):
        lr, li, co_re, co_im = _s5_coeffs(are[...], aim[...], ld[...])
        lr_o[...] = lr
        li_o[...] = li
        for g in range(S5_GROUPS):
            cr, ci = co_re[g:g + 1, :], co_im[g:g + 1, :]
            bbr_o[g] = cr * bre[g] - ci * bim[g]
            bbi_o[g] = cr * bim[g] + ci * bre[g]
        lb_o[...] = _sigmoid(lg[0:1, :] - lg[1:2, :])

    return _pcall(body, "params_fwd", (1,),
                  [_full(GS), _full(GS), _full((S5_GROUPS, 1)), _full(GSC), _full(GSC), _full((2, HG_WIDTH))],
                  [_full(GS), _full(GS), _full(GSC), _full(GSC), _full((1, HG_WIDTH))],
                  [_sds(GS), _sds(GS), _sds(GSC), _sds(GSC), _sds((1, HG_WIDTH))],
                  )(a_re, a_im, ldt, bt_re, bt_im, logits)


def _params_bwd(a_re, a_im, ldt, bt_re, bt_im, logits, dlr, dli, dbbr, dbbi, dlb):
    def body(are, aim, ld, bre, bim, lg, dlr_r, dli_r, dbbr_r, dbbi_r, dlb_r,
             dare_o, daim_o, dld_o, dbre_o, dbim_o, dlg_o, dcr_ref, dci_ref):
        (_, _, co_re, co_im), vjp = jax.vjp(_s5_coeffs, are[...], aim[...], ld[...])
        for g in range(S5_GROUPS):
            cr, ci = co_re[g:g + 1, :], co_im[g:g + 1, :]
            gr, gi, br, bi = dbbr_r[g], dbbi_r[g], bre[g], bim[g]
            dbre_o[g] = cr * gr + ci * gi
            dbim_o[g] = cr * gi - ci * gr
            dcr_ref[g:g + 1, :] = jnp.sum(gr * br + gi * bi, axis=0, keepdims=True)
            dci_ref[g:g + 1, :] = jnp.sum(gi * br - gr * bi, axis=0, keepdims=True)
        dare, daim, dld = vjp((dlr_r[...], dli_r[...], dcr_ref[...], dci_ref[...]))
        dare_o[...] = dare
        daim_o[...] = daim
        dld_o[...] = dld
        lb = _sigmoid(lg[0:1, :] - lg[1:2, :])
        d0 = dlb_r[...] * lb * (1.0 - lb)
        dlg_o[0:1, :] = d0
        dlg_o[1:2, :] = -d0

    return _pcall(body, "params_bwd", (1,),
                  [_full(GS), _full(GS), _full((S5_GROUPS, 1)), _full(GSC), _full(GSC), _full((2, HG_WIDTH)),
                   _full(GS), _full(GS), _full(GSC), _full(GSC), _full((1, HG_WIDTH))],
                  [_full(GS), _full(GS), _full((S5_GROUPS, 1)), _full(GSC), _full(GSC), _full((2, HG_WIDTH))],
                  [_sds(GS), _sds(GS), _sds((S5_GROUPS, 1)), _sds(GSC), _sds(GSC), _sds((2, HG_WIDTH))],
                  scratch=[pltpu.VMEM(GS, F32), pltpu.VMEM(GS, F32)],
                  )(a_re, a_im, ldt, bt_re, bt_im, logits, dlr, dli, dbbr, dbbi, dlb)


def _band_blocks(m):
    g, r, c = m.shape
    gb = g // S5_BANDS
    m4 = m.astype(BF16).reshape(S5_BANDS, gb, r, c)
    on_diag = jnp.eye(gb, dtype=bool)[None, :, None, :, None]
    return jnp.where(on_diag, m4[:, :, :, None, :], 0).reshape(S5_BANDS, gb * r, gb * c)


def _diag_blocks(band, r, c):
    g, nb = band.shape[0] // r, band.shape[1] // c
    on_diag = (jnp.arange(g) % nb)[:, None, None, None] == jnp.arange(nb)[None, None, :, None]
    return jnp.sum(jnp.where(on_diag, band.reshape(g, r, nb, c), 0.0), axis=2)


def _in_proj(x, g_mix, w_in, tm):
    t = x.shape[0]

    def body(x_ref, g_ref, w_ref, u_ref, za_ref, zh_ref, zg_ref):
        xv = x_ref[...]
        r = lax.rsqrt(jnp.mean(xv * xv, axis=-1, keepdims=True) + EPS)
        u = (xv * r * g_ref[...]).astype(BF16)
        u_ref[...] = u
        za_ref[...] = _dot_nt(u, w_ref[0:S5_WIDTH, :])
        zh_ref[...] = _dot_nt(u, w_ref[S5_WIDTH:S5_WIDTH + 4 * HG_WIDTH, :])
        zg_ref[...] = _dot_nt(u, w_ref[S5_WIDTH + 4 * HG_WIDTH:, :]).astype(BF16)

    row = lambda w: pl.BlockSpec((tm, w), lambda i: (i, 0))
    return _pcall(body, "in_proj", (t // tm,),
                  [row(D_MODEL), _full((1, D_MODEL)), _full((N_IN, D_MODEL))],
                  [row(D_MODEL), row(S5_WIDTH), row(4 * HG_WIDTH), row(2 * D_MODEL)],
                  [_sds((t, D_MODEL), BF16), _sds((t, S5_WIDTH)), _sds((t, 4 * HG_WIDTH)),
                   _sds((t, 2 * D_MODEL), BF16)],
                  )(x, g_mix, w_in)


S5_LANES = 512
S5_BANDS = 4


def _band(q):
    return (slice(q * S5_WIDTH // S5_BANDS, (q + 1) * S5_WIDTH // S5_BANDS),
            slice(q * S5_N // S5_BANDS, (q + 1) * S5_N // S5_BANDS))


def _im(st):
    return slice(S5_N + st.start, S5_N + st.stop)


SCAN_UNROLL = 8


def _complex_scan(buf_ref, lam_ref, st_ref, nb, ts, reverse):
    lanes = [slice(cc * S5_LANES, (cc + 1) * S5_LANES) for cc in range(S5_N // S5_LANES)]
    chains = [(b, re) for b in range(nb) for re in lanes]
    nch = len(chains)
    wr = {re.start: lam_ref[0:1, re] for re in lanes}
    wi = {re.start: -lam_ref[1:2, re] if reverse else lam_ref[1:2, re] for re in lanes}

    def block(ib, carry):
        vr, vi = list(carry[:nch]), list(carry[nch:])
        first = ts - SCAN_UNROLL - ib * SCAN_UNROLL if reverse else ib * SCAN_UNROLL
        first = pl.multiple_of(first, SCAN_UNROLL)
        for k in range(SCAN_UNROLL):
            row = pl.ds(first + (SCAN_UNROLL - 1 - k if reverse else k), 1)
            for c, (b, re) in enumerate(chains):
                nr = wr[re.start] * vr[c] - wi[re.start] * vi[c] + buf_ref[b, row, re]
                ni = wr[re.start] * vi[c] + wi[re.start] * vr[c] + buf_ref[b, row, _im(re)]
                buf_ref[b, row, re] = nr
                buf_ref[b, row, _im(re)] = ni
                vr[c], vi[c] = nr, ni
        return tuple(vr + vi)

    init = tuple(st_ref[b, 0:1, re] for b, re in chains) + tuple(st_ref[b, 1:2, re] for b, re in chains)
    last = lax.fori_loop(0, ts // SCAN_UNROLL, block, init)
    for c, (b, re) in enumerate(chains):
        st_ref[b, 0:1, re] = last[c]
        st_ref[b, 1:2, re] = last[nch + c]


BAND_CH = S5_WIDTH // S5_BANDS
BAND_ST = S5_N // S5_BANDS


def _s5_fwd(za, b_bands, lam, c_bands, dskip, nb, seq, ts):
    nts = seq // ts

    def body(za_ref, br_ref, bi_ref, lam_ref, cr_ref, ci_ref, d_ref, xs_ref, y_ref, buf_ref, st_ref):
        @pl.when(pl.program_id(0) == 0)
        def _():
            st_ref[...] = jnp.zeros_like(st_ref)

        for b in range(nb):
            zav = za_ref[b]
            for q in range(S5_BANDS):
                ch, st = _band(q)
                buf_ref[b, :, st] = _dot(zav[:, ch], br_ref[q])
                buf_ref[b, :, _im(st)] = _dot(zav[:, ch], bi_ref[q])
        _complex_scan(buf_ref, lam_ref, st_ref, nb, ts, reverse=False)
        for b in range(nb):
            zav = za_ref[b]
            xs_ref[b] = buf_ref[b].astype(BF16)
            for q in range(S5_BANDS):
                ch, st = _band(q)
                y_ref[b, :, ch] = (_dot(xs_ref[b, :, st], cr_ref[q]) + _dot(xs_ref[b, :, _im(st)], ci_ref[q])
                                   + d_ref[:, ch] * zav[:, ch])

    tok = lambda w: pl.BlockSpec((nb, ts, w), lambda j: (0, j, 0))
    to_st, to_ch = _full((S5_BANDS, BAND_CH, BAND_ST)), _full((S5_BANDS, BAND_ST, BAND_CH))
    return _pcall(body, "s5_fwd", (nts,),
                  [tok(S5_WIDTH), to_st, to_st, _full((2, S5_N)), to_ch, to_ch, _full((1, S5_WIDTH))],
                  [tok(2 * S5_N), tok(S5_WIDTH)],
                  [_sds((nb, seq, 2 * S5_N), BF16), _sds((nb, seq, S5_WIDTH))],
                  scratch=[pltpu.VMEM((nb, ts, 2 * S5_N), F32), pltpu.VMEM((nb, 2, S5_N), F32)],
                  )(za, *b_bands, lam, *c_bands, dskip)


def _hgrn_gates(zq, zf, lbh):
    sf = _sigmoid(zf)
    f = lbh + (1.0 - lbh) * sf
    sq = _sigmoid(zq)
    qa = zq * sq * QSCALE
    bc = _cumsum_rows(jnp.log(f))
    bm = bc[CHUNK // 2 - 1:CHUNK // 2, :]
    bl = bc[CHUNK - 1:CHUNK, :]
    return sf, f, sq, qa, bc, bm, bl


HG_CHUNKS_PER_STEP = 4


def _hgrn_fwd(zh, lb, nb, seq):
    nc = seq // CHUNK
    cps = HG_CHUNKS_PER_STEP

    def body(zh_ref, lb_ref, o_ref, sts_ref, st_ref):
        @pl.when(pl.program_id(0) == 0)
        def _():
            st_ref[...] = jnp.zeros_like(st_ref)

        causal = (lax.broadcasted_iota(jnp.int32, (CHUNK, CHUNK), 0)
                  >= lax.broadcasted_iota(jnp.int32, (CHUNK, CHUNK), 1))
        for cc in range(cps):
            rows = slice(cc * CHUNK, (cc + 1) * CHUNK)
            for b in range(nb):
                for h in range(HG_HEADS):
                    hs = slice(h * HG_HEAD, (h + 1) * HG_HEAD)
                    zq = zh_ref[b, rows, h * HG_HEAD:(h + 1) * HG_HEAD]
                    zf = zh_ref[b, rows, HG_WIDTH + h * HG_HEAD:HG_WIDTH + (h + 1) * HG_HEAD]
                    zi = zh_ref[b, rows, 2 * HG_WIDTH + h * HG_HEAD:2 * HG_WIDTH + (h + 1) * HG_HEAD]
                    _, f, _, qa, bc, bm, bl = _hgrn_gates(zq, zf, lb_ref[:, hs])
                    k = 1.0 - f
                    qt = qa * jnp.exp(bc - bm)
                    kt = k * jnp.exp(bm - bc)
                    qb = qa * jnp.exp(bc)
                    kd = k * jnp.exp(bl - bc)
                    st = st_ref[b, h]
                    sts_ref[b, cc, h] = st
                    a = jnp.where(causal, _dot_nt(qt, kt), 0.0)
                    o_ref[b, rows, hs] = _dot(a, zi) + _dot_nt(qb, st)
                    st_ref[b, h] = st * jnp.exp(bl) + _dot_tn(zi, kd)

    return _pcall(body, "hgrn_fwd", (nc // cps,),
                  [pl.BlockSpec((nb, cps * CHUNK, 4 * HG_WIDTH), lambda c: (0, c, 0)), _full((1, HG_WIDTH))],
                  [pl.BlockSpec((nb, cps * CHUNK, HG_WIDTH), lambda c: (0, c, 0)),
                   pl.BlockSpec((nb, cps, HG_HEADS, HG_HEAD, HG_HEAD), lambda c: (0, c, 0, 0, 0))],
                  [_sds((nb, seq, HG_WIDTH)), _sds((nb, nc, HG_HEADS, HG_HEAD, HG_HEAD))],
                  scratch=[pltpu.VMEM((nb, HG_HEADS, HG_HEAD, HG_HEAD), F32)])(zh, lb)


def _head_rms(o):
    parts = []
    for h in range(HG_HEADS):
        oh = o[:, h * HG_HEAD:(h + 1) * HG_HEAD]
        r = lax.rsqrt(jnp.mean(oh * oh, axis=-1, keepdims=True) + EPS)
        parts.append(jnp.broadcast_to(r, oh.shape))
    return jnp.concatenate(parts, axis=1)


def _head_mean(v):
    parts = []
    for h in range(HG_HEADS):
        vh = v[:, h * HG_HEAD:(h + 1) * HG_HEAD]
        parts.append(jnp.broadcast_to(jnp.mean(vh, axis=-1, keepdims=True), vh.shape))
    return jnp.concatenate(parts, axis=1)


def _mix_fwd(x, y0, o, zh, zgt, w_glu, b_glu, gain, w_pa, w_pb, w_out, g_ffn, tm):
    t = x.shape[0]

    def body(x_ref, y0_ref, o_ref, zg_ref, zgt_ref, wglu_ref, bglu_ref, gain_ref, wpa_ref, wpb_ref, wout_ref,
             gffn_ref, x1_ref, u2_ref, pa_ref, pb_ref, ya2_ref, yb_ref):
        ya1 = _gelu(y0_ref[...])
        s = _sigmoid(_dot(ya1, wglu_ref[...]) + bglu_ref[...])
        ya2 = (ya1 * s).astype(BF16)
        ov = o_ref[...]
        zg = zg_ref[...]
        yb = (ov * _head_rms(ov) * gain_ref[...] * (zg * _sigmoid(zg))).astype(BF16)
        ya2_ref[...] = ya2
        yb_ref[...] = yb
        pa = jnp.dot(ya2, wpa_ref[...], preferred_element_type=F32)
        pb = jnp.dot(yb, wpb_ref[...], preferred_element_type=F32)
        pa_ref[...] = pa.astype(BF16)
        pb_ref[...] = pb.astype(BF16)
        m = (_sigmoid(zgt_ref[:, 0:D_MODEL].astype(F32)) * pa
             + _sigmoid(zgt_ref[:, D_MODEL:].astype(F32)) * pb)
        x1 = x_ref[...] + _dot(m, wout_ref[...])
        x1_ref[...] = x1
        r = lax.rsqrt(jnp.mean(x1 * x1, axis=-1, keepdims=True) + EPS)
        u2_ref[...] = (x1 * r * gffn_ref[...]).astype(BF16)

    row = lambda w: pl.BlockSpec((tm, w), lambda i: (i, 0))
    return _pcall(body, "mix_fwd", (t // tm,),
                  [row(D_MODEL), row(S5_WIDTH), row(HG_WIDTH), pl.BlockSpec((tm, HG_WIDTH), lambda i: (i, 3)),
                   row(2 * D_MODEL), _full((S5_WIDTH, S5_WIDTH)), _full((1, S5_WIDTH)), _full((1, HG_WIDTH)),
                   _full((S5_WIDTH, D_MODEL)), _full((HG_WIDTH, D_MODEL)), _full((D_MODEL, D_MODEL)),
                   _full((1, D_MODEL))],
                  [row(D_MODEL), row(D_MODEL), row(D_MODEL), row(D_MODEL), row(S5_WIDTH), row(HG_WIDTH)],
                  [_sds((t, D_MODEL)), _sds((t, D_MODEL), BF16), _sds((t, D_MODEL), BF16), _sds((t, D_MODEL), BF16),
                   _sds((t, S5_WIDTH), BF16), _sds((t, HG_WIDTH), BF16)],
                  )(x, y0, o, zh, zgt, w_glu, b_glu, gain, w_pa, w_pb, w_out, g_ffn)


FF_COLS = 256
FF_UP_TILE = 2 * D_FF // 2


def _ffn_up(u2, w_up, tm):
    t = u2.shape[0]
    n = 2 * D_FF

    def body(u_ref, w_ref, h_ref):
        h_ref[...] = _dot_nt(u_ref[...], w_ref[...]).astype(BF16)

    return _pcall(body, "ffn_up", (n // FF_UP_TILE, t // tm),
                  [pl.BlockSpec((tm, D_MODEL), lambda j, i: (i, 0)),
                   pl.BlockSpec((FF_UP_TILE, D_MODEL), lambda j, i: (j, 0))],
                  pl.BlockSpec((tm, FF_UP_TILE), lambda j, i: (i, j)),
                  _sds((t, n), BF16))(u2, w_up)


HALO = 16


def _shift_matrix(tm):
    r = lax.broadcasted_iota(jnp.int32, (tm, tm), 0)
    c = lax.broadcasted_iota(jnp.int32, (tm, tm), 1)
    return jnp.where(r == c + 1, 1.0, 0.0).astype(BF16)


def _conv_cols(h_ref, halo_ref, valid, wc_ref, bc_ref, c0):
    cs = slice(c0, c0 + FF_COLS)
    cur = h_ref[:, cs].astype(F32)
    prev = jnp.where(valid, halo_ref[:, cs].astype(F32), 0.0)
    full = jnp.concatenate([prev, cur], axis=0)
    h1 = pltpu.roll(full, 1, axis=0)[HALO:]
    h2 = pltpu.roll(full, 2, axis=0)[HALO:]
    return h2 * wc_ref[0:1, cs] + h1 * wc_ref[1:2, cs] + cur * wc_ref[2:3, cs] + bc_ref[:, cs]


def _ffn_down_loss(h, x1, tgt, w_conv, b_conv, w_down, g_final, seq, tm):
    t = h.shape[0]
    tps = seq // tm
    n = 2 * D_FF

    def body(h_ref, halo_ref, x1_ref, tgt_ref, wc_ref, bc_ref, wd_ref, gf_ref,
             hc_ref, a_ref, dx2_ref, dx2b_ref, loss_ref, dgf_ref):
        i = pl.program_id(0)

        @pl.when(i == 0)
        def _():
            loss_ref[...] = jnp.zeros_like(loss_ref)
            dgf_ref[...] = jnp.zeros_like(dgf_ref)

        valid = (i % tps) != 0
        x2 = x1_ref[...]
        for j in range(D_FF // FF_COLS):
            gate = _conv_cols(h_ref, halo_ref, valid, wc_ref, bc_ref, j * FF_COLS)
            val = _conv_cols(h_ref, halo_ref, valid, wc_ref, bc_ref, D_FF + j * FF_COLS)
            hc_ref[:, j * FF_COLS:(j + 1) * FF_COLS] = gate.astype(BF16)
            hc_ref[:, D_FF + j * FF_COLS:D_FF + (j + 1) * FF_COLS] = val.astype(BF16)
            a = (gate * _sigmoid(gate) * val).astype(BF16)
            a_ref[:, j * FF_COLS:(j + 1) * FF_COLS] = a
            x2 = x2 + jnp.dot(a, wd_ref[j * FF_COLS:(j + 1) * FF_COLS, :], preferred_element_type=F32)
        r = lax.rsqrt(jnp.mean(x2 * x2, axis=-1, keepdims=True) + EPS)
        xn = x2 * r
        g = gf_ref[...]
        e = xn * g - tgt_ref[...]
        loss_ref[...] += (0.5 / D_MODEL) * jnp.sum(e * e).reshape(1, 1)
        dy = e * (1.0 / D_MODEL)
        dgf_ref[...] += jnp.sum(dy * xn, axis=0, keepdims=True)
        dxn = dy * g
        dx2 = r * (dxn - xn * jnp.mean(dxn * xn, axis=-1, keepdims=True))
        dx2_ref[...] = dx2
        dx2b_ref[...] = dx2.astype(BF16)

    row = lambda w: pl.BlockSpec((tm, w), lambda i: (i, 0))
    halo = pl.BlockSpec((HALO, n), lambda i: (jnp.maximum(i * (tm // HALO) - 1, 0), 0))
    return _pcall(body, "ffn_down_loss", (t // tm,),
                  [row(n), halo, row(D_MODEL), row(D_MODEL), _full((CONV_W, n)), _full((1, n)),
                   _full((D_FF, D_MODEL)), _full((1, D_MODEL))],
                  [row(n), row(D_FF), row(D_MODEL), row(D_MODEL), _full((1, 1)), _full((1, D_MODEL))],
                  [_sds((t, n), BF16), _sds((t, D_FF), BF16), _sds((t, D_MODEL)), _sds((t, D_MODEL), BF16),
                   _sds((1, 1)), _sds((1, D_MODEL))],
                  )(h, h, x1, tgt, w_conv, b_conv, w_down, g_final)


def _wgrad(a, b, name, tn, out_dtype=F32, band=None, after=None):
    t, m = a.shape
    n = b.shape[1] if band is None else band
    nbands = 1 if band is None else b.shape[1] // band
    after = b if after is None else after

    def body(a_ref, b_ref, after_ref, o_ref):
        o_ref[...] = _dot_tn(a_ref[...], b_ref[...]).astype(out_dtype)

    return _pcall(body, name, (m // tn,),
                  [pl.BlockSpec((t, tn), lambda i: (0, i)), pl.BlockSpec((t, n), lambda i: (0, i % nbands)),
                   pl.BlockSpec(memory_space=pl.ANY)],
                  pl.BlockSpec((tn, n), lambda i: (i, 0)), _sds((m, n), out_dtype))(a, b, after)


def _ffn_bwd_act(dx2b, hc, w_down, tm):
    t = hc.shape[0]
    n = 2 * D_FF

    def body(dx2_ref, hc_ref, wd_ref, dhc_ref, dbc_ref):
        @pl.when(pl.program_id(0) == 0)
        def _():
            dbc_ref[...] = jnp.zeros_like(dbc_ref)

        dx2 = dx2_ref[...]
        for j in range(D_FF // FF_COLS):
            gs = slice(j * FF_COLS, (j + 1) * FF_COLS)
            vs = slice(D_FF + j * FF_COLS, D_FF + (j + 1) * FF_COLS)
            gate = hc_ref[:, gs].astype(F32)
            val = hc_ref[:, vs].astype(F32)
            da = _dot_nt(dx2, wd_ref[gs, :])
            sg = _sigmoid(gate)
            dgate = da * val * (sg * (1.0 + gate * (1.0 - sg)))
            dval = da * (gate * sg)
            dhc_ref[:, gs] = dgate.astype(BF16)
            dhc_ref[:, vs] = dval.astype(BF16)
            dbc_ref[:, gs] += jnp.sum(dgate, axis=0, keepdims=True)
            dbc_ref[:, vs] += jnp.sum(dval, axis=0, keepdims=True)

    row = lambda w: pl.BlockSpec((tm, w), lambda i: (i, 0))
    return _pcall(body, "ffn_bwd_act", (t // tm,),
                  [row(D_MODEL), row(n), _full((D_FF, D_MODEL))],
                  [row(n), _full((1, n))],
                  [_sds((t, n), BF16), _sds((1, n))],
                  )(dx2b, hc, w_down)


def _ffn_bwd_up(dhc, h, dx2, x1, w_conv, w_up, g_ffn, seq, tm):
    t = dhc.shape[0]
    tps = seq // tm
    n = 2 * D_FF
    last = t // HALO - 1

    def body(dhc_ref, halo_ref, h_ref, dx2_ref, x1_ref, wc_ref, wu_ref, gf_ref,
             dh_ref, dx1_ref, dx1b_ref, dgf_ref, dwc_ref):
        i = pl.program_id(0)

        @pl.when(i == 0)
        def _():
            dgf_ref[...] = jnp.zeros_like(dgf_ref)
            dwc_ref[...] = jnp.zeros_like(dwc_ref)

        valid = ((i + 1) % tps) != 0
        du2 = jnp.zeros((tm, D_MODEL), F32)
        for j in range(n // FF_COLS):
            cs = slice(j * FF_COLS, (j + 1) * FF_COLS)
            cur = dhc_ref[:, cs].astype(F32)
            nxt = jnp.where(valid, halo_ref[:, cs].astype(F32), 0.0)
            full = jnp.concatenate([cur, nxt], axis=0)
            d1 = pltpu.roll(full, tm + HALO - 1, axis=0)[:tm]
            d2 = pltpu.roll(full, tm + HALO - 2, axis=0)[:tm]
            dh = (cur * wc_ref[2:3, cs] + d1 * wc_ref[1:2, cs] + d2 * wc_ref[0:1, cs]).astype(BF16)
            dh_ref[:, cs] = dh
            du2 = du2 + _dot(dh, wu_ref[cs, :])
            hv = h_ref[:, cs].astype(F32)
            dwc_ref[0:1, cs] += jnp.sum(hv * d2, axis=0, keepdims=True)
            dwc_ref[1:2, cs] += jnp.sum(hv * d1, axis=0, keepdims=True)
            dwc_ref[2:3, cs] += jnp.sum(hv * cur, axis=0, keepdims=True)
        x1 = x1_ref[...]
        r = lax.rsqrt(jnp.mean(x1 * x1, axis=-1, keepdims=True) + EPS)
        xn = x1 * r
        dgf_ref[...] += jnp.sum(du2 * xn, axis=0, keepdims=True)
        dxn = du2 * gf_ref[...]
        dx1 = dx2_ref[...] + r * (dxn - xn * jnp.mean(dxn * xn, axis=-1, keepdims=True))
        dx1_ref[...] = dx1
        dx1b_ref[...] = dx1.astype(BF16)

    row = lambda w: pl.BlockSpec((tm, w), lambda i: (i, 0))
    halo = pl.BlockSpec((HALO, n), lambda i: (jnp.minimum((i + 1) * (tm // HALO), last), 0))
    return _pcall(body, "ffn_bwd_up", (t // tm,),
                  [row(n), halo, row(n), row(D_MODEL), row(D_MODEL), _full((CONV_W, n)), _full((n, D_MODEL)),
                   _full((1, D_MODEL))],
                  [row(n), row(D_MODEL), row(D_MODEL), _full((1, D_MODEL)), _full((CONV_W, n))],
                  [_sds((t, n), BF16), _sds((t, D_MODEL)), _sds((t, D_MODEL), BF16), _sds((1, D_MODEL)),
                   _sds((CONV_W, n))],
                  )(dhc, dhc, h, dx2, x1, w_conv, w_up, g_ffn)


def _mix_bwd(dx1, y0, o, zh, zgt, pa, pb, w_glu, b_glu, gain, w_pa, w_pb, w_out, tm):
    t = dx1.shape[0]

    def body(dx1_ref, y0_ref, o_ref, zg_ref, zgt_ref, pa_ref, pb_ref, wglu_ref, bglu_ref, gain_ref, wpa_ref,
             wpb_ref, wout_ref,
             dy0_ref, do_ref, dzg_ref, dzgt_ref, m_ref, dpa_ref, dpb_ref, ya1_ref, dpre_ref, dbglu_ref, dgain_ref):
        @pl.when(pl.program_id(0) == 0)
        def _():
            dbglu_ref[...] = jnp.zeros_like(dbglu_ref)
            dgain_ref[...] = jnp.zeros_like(dgain_ref)

        dm = _dot_nt(dx1_ref[...], wout_ref[...])
        sga = _sigmoid(zgt_ref[:, 0:D_MODEL].astype(F32))
        sgb = _sigmoid(zgt_ref[:, D_MODEL:].astype(F32))
        pa = pa_ref[...].astype(F32)
        pb = pb_ref[...].astype(F32)
        m_ref[...] = (sga * pa + sgb * pb).astype(BF16)
        dzgt_ref[:, 0:D_MODEL] = (dm * pa * sga * (1.0 - sga)).astype(BF16)
        dzgt_ref[:, D_MODEL:] = (dm * pb * sgb * (1.0 - sgb)).astype(BF16)
        dpa = (dm * sga).astype(BF16)
        dpb = (dm * sgb).astype(BF16)
        dpa_ref[...] = dpa
        dpb_ref[...] = dpb
        dya2 = _dot_nt(dpa, wpa_ref[...])
        dyb = _dot_nt(dpb, wpb_ref[...])
        y0 = y0_ref[...]
        ya1 = _gelu(y0)
        ya1_ref[...] = ya1.astype(BF16)
        s = _sigmoid(_dot(ya1, wglu_ref[...]) + bglu_ref[...])
        dpre = dya2 * ya1 * s * (1.0 - s)
        dpre_ref[...] = dpre.astype(BF16)
        dbglu_ref[...] += jnp.sum(dpre, axis=0, keepdims=True)
        dya1 = dya2 * s + _dot_nt(dpre, wglu_ref[...])
        dy0_ref[...] = dya1 * _gelu_grad(y0)
        ov = o_ref[...]
        zg = zg_ref[...]
        oh = ov * _head_rms(ov)
        on = oh * gain_ref[...]
        sz = _sigmoid(zg)
        dzg_ref[...] = (dyb * on * (sz * (1.0 + zg * (1.0 - sz)))).astype(BF16)
        don = dyb * (zg * sz)
        dgain_ref[...] += jnp.sum(don * oh, axis=0, keepdims=True)
        doh = don * gain_ref[...]
        do_ref[...] = _head_rms(ov) * (doh - oh * _head_mean(doh * oh))

    row = lambda w: pl.BlockSpec((tm, w), lambda i: (i, 0))
    return _pcall(body, "mix_bwd", (t // tm,),
                  [row(D_MODEL), row(S5_WIDTH), row(HG_WIDTH), pl.BlockSpec((tm, HG_WIDTH), lambda i: (i, 3)),
                   row(2 * D_MODEL), row(D_MODEL), row(D_MODEL), _full((S5_WIDTH, S5_WIDTH)), _full((1, S5_WIDTH)),
                   _full((1, HG_WIDTH)), _full((S5_WIDTH, D_MODEL)), _full((HG_WIDTH, D_MODEL)),
                   _full((D_MODEL, D_MODEL))],
                  [row(S5_WIDTH), row(HG_WIDTH), row(HG_WIDTH), row(2 * D_MODEL), row(D_MODEL), row(D_MODEL),
                   row(D_MODEL), row(S5_WIDTH), row(S5_WIDTH), _full((1, S5_WIDTH)), _full((1, HG_WIDTH))],
                  [_sds((t, S5_WIDTH)), _sds((t, HG_WIDTH)), _sds((t, HG_WIDTH), BF16), _sds((t, 2 * D_MODEL), BF16),
                   _sds((t, D_MODEL), BF16), _sds((t, D_MODEL), BF16), _sds((t, D_MODEL), BF16),
                   _sds((t, S5_WIDTH), BF16), _sds((t, S5_WIDTH), BF16), _sds((1, S5_WIDTH)), _sds((1, HG_WIDTH))],
                  )(dx1, y0, o, zh, zgt, pa, pb, w_glu, b_glu, gain, w_pa, w_pb, w_out)


def _s5_bwd(dy0, za, xs, c_bands, b_bands, lam, dskip, nb, seq, ts):
    nts = seq // ts

    def body(dy0_ref, za_ref, xs_ref, halo_ref, cr_ref, ci_ref, br_ref, bi_ref, lam_ref, d_ref,
             dza_ref, a_ref, dlam_ref, dd_ref, acc_ref, st_ref):
        j = pl.program_id(0)

        @pl.when(j == 0)
        def _():
            dlam_ref[...] = jnp.zeros_like(dlam_ref)
            dd_ref[...] = jnp.zeros_like(dd_ref)
            st_ref[...] = jnp.zeros_like(st_ref)

        for b in range(nb):
            dy0 = dy0_ref[b]
            for q in range(S5_BANDS):
                ch, st = _band(q)
                acc_ref[b, :, st] = _dot(dy0[:, ch], cr_ref[q])
                acc_ref[b, :, _im(st)] = _dot(dy0[:, ch], ci_ref[q])
        _complex_scan(acc_ref, lam_ref, st_ref, nb, ts, reverse=True)
        shift = _shift_matrix(ts)
        top = lax.broadcasted_iota(jnp.int32, (SUBLANES, S5_LANES), 0) == 0
        for b in range(nb):
            a_ref[b] = acc_ref[b].astype(BF16)
            first = jnp.where(j == nts - 1, 0.0, halo_ref[b, HALO - 1:HALO, :].astype(F32))

            def shifted(cols):
                xp = jnp.dot(shift, xs_ref[b, :, cols], preferred_element_type=F32)
                return jnp.concatenate([xp[:SUBLANES] + jnp.where(top, first[:, cols], 0.0), xp[SUBLANES:]], axis=0)

            for cc in range(S5_N // S5_LANES):
                re = slice(cc * S5_LANES, (cc + 1) * S5_LANES)
                ar, ai, xr, xi = acc_ref[b, :, re], acc_ref[b, :, _im(re)], shifted(re), shifted(_im(re))
                dlam_ref[0:1, re] += jnp.sum(ar * xr + ai * xi, axis=0, keepdims=True)
                dlam_ref[1:2, re] += jnp.sum(ai * xr - ar * xi, axis=0, keepdims=True)
            dy0 = dy0_ref[b]
            for q in range(S5_BANDS):
                ch, st = _band(q)
                dza_ref[b, :, ch] = (_dot(a_ref[b, :, st], br_ref[q]) + _dot(a_ref[b, :, _im(st)], bi_ref[q])
                                     + d_ref[:, ch] * dy0[:, ch]).astype(BF16)
            dd_ref[...] += jnp.sum(dy0 * za_ref[b], axis=0, keepdims=True)

    tile = lambda j: nts - 1 - j
    tok = lambda w: pl.BlockSpec((nb, ts, w), lambda j: (0, tile(j), 0))
    halo = pl.BlockSpec((nb, HALO, 2 * S5_N), lambda j: (0, jnp.maximum(tile(j) * (ts // HALO) - 1, 0), 0))
    to_st, to_ch = _full((S5_BANDS, BAND_CH, BAND_ST)), _full((S5_BANDS, BAND_ST, BAND_CH))
    return _pcall(body, "s5_bwd", (nts,),
                  [tok(S5_WIDTH), tok(S5_WIDTH), tok(2 * S5_N), halo, to_st, to_st, to_ch, to_ch,
                   _full((2, S5_N)), _full((1, S5_WIDTH))],
                  [tok(S5_WIDTH), tok(2 * S5_N), _full((2, S5_N)), _full((1, S5_WIDTH))],
                  [_sds((nb, seq, S5_WIDTH), BF16), _sds((nb, seq, 2 * S5_N), BF16), _sds((2, S5_N)),
                   _sds((1, S5_WIDTH))],
                  scratch=[pltpu.VMEM((nb, ts, 2 * S5_N), F32), pltpu.VMEM((nb, 2, S5_N), F32)],
                  )(dy0, za, xs, xs, *c_bands, *b_bands, lam, dskip)


def _hgrn_bwd(zh, do, sts, lb, nb, seq):
    nc = seq // CHUNK
    cps = HG_CHUNKS_PER_STEP

    def body(zh_ref, do_ref, sts_ref, lb_ref, dz_ref, dlb_ref, dst_ref):
        @pl.when(pl.program_id(0) == 0)
        def _():
            dst_ref[...] = jnp.zeros_like(dst_ref)
            dlb_ref[...] = jnp.zeros_like(dlb_ref)

        row = lax.broadcasted_iota(jnp.int32, (CHUNK, CHUNK), 0)
        causal = row >= lax.broadcasted_iota(jnp.int32, (CHUNK, CHUNK), 1)
        last_row = lax.broadcasted_iota(jnp.int32, (CHUNK, HG_HEAD), 0) == CHUNK - 1
        for cc in reversed(range(cps)):
            rows = slice(cc * CHUNK, (cc + 1) * CHUNK)
            for b in range(nb):
                for h in range(HG_HEADS):
                    hs = slice(h * HG_HEAD, (h + 1) * HG_HEAD)
                    zq = zh_ref[b, rows, h * HG_HEAD:(h + 1) * HG_HEAD]
                    zf = zh_ref[b, rows, HG_WIDTH + h * HG_HEAD:HG_WIDTH + (h + 1) * HG_HEAD]
                    zi = zh_ref[b, rows, 2 * HG_WIDTH + h * HG_HEAD:2 * HG_WIDTH + (h + 1) * HG_HEAD]
                    lbh = lb_ref[:, hs]
                    sf, f, sq, qa, bc, bm, bl = _hgrn_gates(zq, zf, lbh)
                    k = 1.0 - f
                    e_qt = jnp.exp(bc - bm)
                    e_kt = jnp.exp(bm - bc)
                    e_b = jnp.exp(bc)
                    e_kd = jnp.exp(bl - bc)
                    e_l = jnp.exp(bl)
                    qt, kt, qb, kd = qa * e_qt, k * e_kt, qa * e_b, k * e_kd
                    a = jnp.where(causal, _dot_nt(qt, kt), 0.0)
                    st = sts_ref[b, cc, h]
                    dst = dst_ref[b, h]
                    dov = do_ref[b, rows, hs]
                    da = jnp.where(causal, _dot_nt(dov, zi), 0.0)
                    qt_r, kt_r = qt.astype(BF16).astype(F32), kt.astype(BF16).astype(F32)
                    dqt = _dot(da, kt)
                    dkt = _dot_tn(da, qt)
                    dqb = _dot(dov, st)
                    di = _dot_tn(a, dov) + _dot_nt(kd, dst)
                    dkd = _dot(zi, dst)
                    de_l = jnp.sum(dst * st, axis=0, keepdims=True)
                    dst_ref[b, h] = dst * e_l + _dot_tn(dov, qb)
                    dqa = dqt * e_qt + dqb * e_b
                    dk = dkt * e_kt + dkd * e_kd
                    dbl = jnp.sum(dkd * kd, axis=0, keepdims=True) + de_l * e_l
                    db = dqt * qt_r - dkt * kt_r + dqb * qb - dkd * kd + jnp.where(last_row, dbl, 0.0)
                    df = _cumsum_rows(db, reverse=True) / f - dk
                    dzq = dqa * QSCALE * (sq * (1.0 + zq * (1.0 - sq)))
                    dzf = df * (1.0 - lbh) * sf * (1.0 - sf)
                    dz_ref[b, rows, h * HG_HEAD:(h + 1) * HG_HEAD] = dzq.astype(BF16)
                    dz_ref[b, rows, HG_WIDTH + h * HG_HEAD:HG_WIDTH + (h + 1) * HG_HEAD] = dzf.astype(BF16)
                    dz_ref[b, rows, 2 * HG_WIDTH + h * HG_HEAD:2 * HG_WIDTH + (h + 1) * HG_HEAD] = di.astype(BF16)
                    dlb_ref[:, hs] += jnp.sum(df * (1.0 - sf), axis=0, keepdims=True)

    rev = lambda c: nc // cps - 1 - c
    return _pcall(body, "hgrn_bwd", (nc // cps,),
                  [pl.BlockSpec((nb, cps * CHUNK, 4 * HG_WIDTH), lambda c: (0, rev(c), 0)),
                   pl.BlockSpec((nb, cps * CHUNK, HG_WIDTH), lambda c: (0, rev(c), 0)),
                   pl.BlockSpec((nb, cps, HG_HEADS, HG_HEAD, HG_HEAD), lambda c: (0, rev(c), 0, 0, 0)),
                   _full((1, HG_WIDTH))],
                  [pl.BlockSpec((nb, cps * CHUNK, 3 * HG_WIDTH), lambda c: (0, rev(c), 0)), _full((1, HG_WIDTH))],
                  [_sds((nb, seq, 3 * HG_WIDTH), BF16), _sds((1, HG_WIDTH))],
                  scratch=[pltpu.VMEM((nb, HG_HEADS, HG_HEAD, HG_HEAD), F32)])(zh, do, sts, lb)


def _in_proj_bwd(dza, dzh, dzg, dzgt, dx1, x, g_mix, w_in, tm):
    t = x.shape[0]

    def body(dza_ref, dzh_ref, dzg_ref, dzgt_ref, dx1_ref, x_ref, g_ref, w_ref, dz_ref, dx_ref, dg_ref):
        @pl.when(pl.program_id(0) == 0)
        def _():
            dg_ref[...] = jnp.zeros_like(dg_ref)

        c1, c2, c3 = S5_WIDTH, S5_WIDTH + 3 * HG_WIDTH, S5_WIDTH + 4 * HG_WIDTH
        dz_ref[:, 0:c1] = dza_ref[...]
        dz_ref[:, c1:c2] = dzh_ref[...]
        dz_ref[:, c2:c3] = dzg_ref[...]
        dz_ref[:, c3:] = dzgt_ref[...]
        du = _dot(dz_ref[...], w_ref[...])
        xv = x_ref[...]
        r = lax.rsqrt(jnp.mean(xv * xv, axis=-1, keepdims=True) + EPS)
        xn = xv * r
        dg_ref[...] += jnp.sum(du * xn, axis=0, keepdims=True)
        dxn = du * g_ref[...]
        dx_ref[...] = dx1_ref[...] + r * (dxn - xn * jnp.mean(dxn * xn, axis=-1, keepdims=True))

    row = lambda w: pl.BlockSpec((tm, w), lambda i: (i, 0))
    return _pcall(body, "in_proj_bwd", (t // tm,),
                  [row(S5_WIDTH), row(3 * HG_WIDTH), row(HG_WIDTH), row(2 * D_MODEL), row(D_MODEL), row(D_MODEL),
                   _full((1, D_MODEL)), _full((N_IN, D_MODEL))],
                  [row(N_IN), row(D_MODEL), _full((1, D_MODEL))],
                  [_sds((t, N_IN), BF16), _sds((t, D_MODEL)), _sds((1, D_MODEL))],
                  )(dza, dzh, dzg, dzgt, dx1, x, g_mix, w_in)


def _after(value, token):
    return value + token[0, 0]


def _local_step(x3, tgt3, weights, sp, emit, emit_small):
    nb, seq, _ = x3.shape
    t = nb * seq
    tm = _token_tile(seq)
    x = x3.reshape(t, D_MODEL)
    tgt = tgt3.reshape(t, D_MODEL)
    row = lambda v: v.reshape(1, -1)

    a_re, a_im, b_re, b_im = sp["s5_a_re"], sp["s5_a_im"], sp["s5_b_re"], sp["s5_b_im"]
    ldt = sp["s5_log_dt"].reshape(S5_GROUPS, 1)
    lr, li, bb_re, bb_im, lb = _params_fwd(a_re, a_im, ldt, b_re, b_im, sp["hg_lb_logits"])
    lam = jnp.concatenate([lr.reshape(1, S5_N), li.reshape(1, S5_N)], axis=0)
    swap = lambda m: m.transpose(0, 2, 1)
    b_to_st = (_band_blocks(bb_re), _band_blocks(bb_im))
    b_to_ch = (_band_blocks(swap(bb_re)), _band_blocks(swap(bb_im)))
    c_to_ch = (_band_blocks(swap(sp["s5_c_re"])), _band_blocks(swap(-sp["s5_c_im"])))
    c_to_st = (_band_blocks(sp["s5_c_re"]), _band_blocks(-sp["s5_c_im"]))

    g_mix, g_ffn, g_final = row(sp["g_mix"]), row(sp["g_ffn"]), row(sp["g_final"])
    b_glu, gain, dskip, b_conv = row(sp["b_glu"]), row(sp["hg_norm_gain"]), row(sp["s5_d"]), row(sp["b_conv"])

    w_in = weights("in", lam, *b_to_st, *b_to_ch, *c_to_ch, *c_to_st)["w_in"]
    wide = min(2 * tm, seq)
    u, za, zh, zgt = _in_proj(x, g_mix, w_in, wide)
    seqs = lambda v: v.reshape(nb, seq, v.shape[-1])
    toks = lambda v: v.reshape(t, v.shape[-1])
    xs3, y0 = _s5_fwd(seqs(za), b_to_st, lam, c_to_ch, dskip, nb, seq, tm)
    xs, y0 = toks(xs3), toks(y0)
    o3, sts = _hgrn_fwd(zh.reshape(nb, seq, 4 * HG_WIDTH), lb, nb, seq)
    o = o3.reshape(t, HG_WIDTH)
    wm = weights("mix", y0, o3)
    weights.forward("ffn", wm["w_out"])
    x1, u2, pa, pb, ya2, yb = _mix_fwd(x, y0, o, zh, zgt, wm["w_glu"], b_glu, gain, wm["w_pa"], wm["w_pb"],
                                       wm["w_out"], g_ffn, wide)
    wf = weights("ffn", u2)
    h = _ffn_up(u2, wf["w_up"], min(4 * tm, t))
    hc, a, dx2, dx2b, loss, dg_final = _ffn_down_loss(h, x1, tgt, wf["w_conv"], b_conv, wf["w_down"], g_final,
                                                      seq, tm)

    def wgrad(a, b, name):
        return _wgrad(a, b, name, 512 if a.shape[1] % 512 == 0 else 256, out_dtype=BF16)

    dhc, db_conv = _ffn_bwd_act(dx2b, hc, wf["w_down"], tm)
    dw_down = wgrad(a, dx2b, "dw_down")
    dh, dx1, dx1b, dg_ffn, dw_conv = _ffn_bwd_up(dhc, h, dx2, x1, wf["w_conv"], wf["w_up"], g_ffn, seq, tm)
    sent = emit({"w_up": wgrad(dh, u2, "dw_up"), "w_conv": dw_conv, "w_down": dw_down})
    (dy0, do, dzg, dzgt, m, dpa, dpb, ya1, dpre, db_glu, dgain) = _mix_bwd(
        dx1b, y0, o, zh, zgt, pa, pb, wm["w_glu"], _after(b_glu, sent), gain, wm["w_pa"], wm["w_pb"], wm["w_out"],
        wide)
    dzh3, dlb = _hgrn_bwd(zh.reshape(nb, seq, 4 * HG_WIDTH), do.reshape(nb, seq, HG_WIDTH), sts, lb, nb, seq)
    dza, a_s5, dlam, dd = _s5_bwd(seqs(dy0), seqs(za), xs3, c_to_st, b_to_ch, lam, dskip, nb, seq, tm)
    dza, a_s5 = toks(dza), toks(a_s5)
    dz, dx, dg_mix = _in_proj_bwd(dza, dzh3.reshape(t, 3 * HG_WIDTH), dzg, dzgt, dx1, x, g_mix, w_in, wide)
    sent = emit({"w_in": wgrad(dz, u, "dw_in")})

    def wgrad_after(a, b, name):
        return _wgrad(a, b, name, 512 if a.shape[1] % 512 == 0 else 256, out_dtype=BF16, after=sent)

    sent = emit({"w_out": wgrad_after(m, dx1b, "dw_out"), "w_pa": wgrad_after(ya2, dpa, "dw_pa"),
                 "w_pb": wgrad_after(yb, dpb, "dw_pb"), "w_glu": wgrad_after(ya1, dpre, "dw_glu")})

    band = HG_HEAD
    dbb_band = _wgrad(a_s5, za, "dbb_s5", 512, band=band, after=sent)
    dc_band = _wgrad(xs, dy0, "dc_s5", 512, band=band, after=sent)
    dbb_re = swap(_diag_blocks(dbb_band[:S5_N], S5_STATE, S5_GROUP))
    dbb_im = swap(_diag_blocks(dbb_band[S5_N:], S5_STATE, S5_GROUP))
    dc_re = swap(_diag_blocks(dc_band[:S5_N], S5_STATE, S5_GROUP))
    dc_im = -swap(_diag_blocks(dc_band[S5_N:], S5_STATE, S5_GROUP))
    da_re, da_im, dldt, db_re, db_im, dlogits = _params_bwd(
        a_re, a_im, ldt, b_re, b_im, sp["hg_lb_logits"],
        dlam[0].reshape(S5_GROUPS, S5_STATE), dlam[1].reshape(S5_GROUPS, S5_STATE), dbb_re, dbb_im, dlb)
    emit_small({"g_mix": dg_mix, "s5_a_re": da_re, "s5_a_im": da_im, "s5_log_dt": dldt.reshape(1, S5_GROUPS),
                "s5_b_re": db_re, "s5_b_im": db_im, "s5_c_re": dc_re, "s5_c_im": dc_im, "s5_d": dd, "b_glu": db_glu,
                "hg_lb_logits": dlogits, "hg_norm_gain": dgain, "g_ffn": dg_ffn, "b_conv": db_conv,
                "g_final": dg_final, "loss": loss})
    return dx.reshape(nb, seq, D_MODEL)


def _mesh_peers():
    x, y, c = lax.axis_index("x"), lax.axis_index("y"), lax.axis_index("c")
    peers = []
    for k in range(1, N_DEV):
        px, py, pc = (1 - x if k & 4 else x), (1 - y if k & 2 else y), (1 - c if k & 1 else c)
        peers.append((k, (px, py, pc), 4 * px + 2 * py + pc))
    return 4 * x + 2 * y + c, peers


_HBM = pl.BlockSpec(memory_space=pltpu.HBM)
_SEM = pl.BlockSpec(memory_space=pltpu.SEMAPHORE)


_EFFECT = pltpu.CompilerParams(has_side_effects=pltpu.SideEffectType.DATAFLOW_SIDE_EFFECTING)


def _remote(src, dst, send_sem, recv_sem, to):
    return pltpu.make_async_remote_copy(src_ref=src, dst_ref=dst, send_sem=send_sem, recv_sem=recv_sem,
                                        device_id=to, device_id_type=pl.DeviceIdType.MESH)


def _exchange_start(name, arrays, after):
    n = len(arrays)
    srcs = [pltpu.with_memory_space_constraint(a, pltpu.HBM) for a in arrays]
    lands = [pltpu.with_memory_space_constraint(lax.empty(a.shape, a.dtype), pltpu.HBM) for a in arrays]
    copies = (N_DEV - 1) * n

    def body(*refs):
        src_refs, land_refs = refs[:n], refs[n:2 * n]
        send_sems, recv_sems, token = refs[2 * n + 1], refs[2 * n + 2], refs[-1]
        my_slab, peers = _mesh_peers()
        for k, peer, slab in peers:
            for i in range(n):
                s = (k - 1) * n + i
                _remote(src_refs[i].at[slab], land_refs[i].at[my_slab], send_sems.at[s], recv_sems.at[s], peer).start()
        token[...] = jnp.zeros_like(token)

    outs = pl.pallas_call(
        body, name=name,
        out_shape=(pltpu.SemaphoreType.DMA((copies,)), pltpu.SemaphoreType.DMA((copies,)),
                   *[pltpu.HBM(a.shape, a.dtype) for a in lands], _sds((SUBLANES, LANES))),
        in_specs=[_HBM] * (2 * n) + [pl.BlockSpec(memory_space=pl.ANY)],
        out_specs=(_SEM, _SEM, *[_HBM] * n, pl.BlockSpec(memory_space=pltpu.VMEM)),
        input_output_aliases={n + i: 2 + i for i in range(n)}, compiler_params=_EFFECT,
    )(*srcs, *lands, after)
    return (outs[0], outs[1], srcs, outs[2:2 + n]), outs[-1]


def _exchange_wait(name, state, *after):
    send_sems, recv_sems, srcs, lands = state
    n = len(lands)

    def body(*refs):
        src_refs, land_refs = refs[:n], refs[n:2 * n]
        send_ref, recv_ref = refs[2 * n], refs[2 * n + 1]
        _, peers = _mesh_peers()
        for k, peer, slab in peers:
            for i in range(n):
                s = (k - 1) * n + i
                copy = _remote(src_refs[i].at[slab], land_refs[i].at[slab], send_ref.at[s], recv_ref.at[s], peer)
                copy.wait_send()
                copy.wait_recv()

    outs = pl.pallas_call(
        body, name=name,
        out_shape=tuple(pltpu.HBM(a.shape, a.dtype) for a in lands),
        in_specs=[_HBM] * (2 * n) + [_SEM, _SEM] + [pl.BlockSpec(memory_space=pl.ANY)] * len(after),
        out_specs=tuple([_HBM] * n),
        input_output_aliases={n + i: i for i in range(n)}, compiler_params=_EFFECT,
    )(*srcs, *lands, send_sems, recv_sems, *after)
    return list(outs), list(srcs)


def _slab(pos):
    return 4 * pos[0] + 2 * pos[1] + pos[2]


def _chip_routes():
    x, y, c = lax.axis_index("x"), lax.axis_index("y"), lax.axis_index("c")
    return (x, y, c), (x, y, 1 - c), [(1 - x, y, c), (x, 1 - y, c), (1 - x, 1 - y, c)]


def _gather_start(name, arrays, after):
    n = len(arrays)
    me = 4 * lax.axis_index("x") + 2 * lax.axis_index("y") + lax.axis_index("c")
    srcs = [pltpu.with_memory_space_constraint(a, pltpu.HBM) for a in arrays]
    lands = [pltpu.with_memory_space_constraint(
        lax.dynamic_update_slice_in_dim(lax.empty((N_DEV,) + a.shape, a.dtype), a[None], me, 0), pltpu.HBM)
        for a in arrays]

    def body(*refs):
        src_refs, land_refs = refs[:n], refs[n:2 * n]
        send_sems, recv_sems, token = refs[2 * n + 1], refs[2 * n + 2], refs[-1]
        mine, sibling, chips = _chip_routes()
        for k, to in enumerate([sibling] + chips):
            for i in range(n):
                _remote(src_refs[i], land_refs[i].at[_slab(mine)], send_sems.at[k * n + i], recv_sems.at[k * n + i],
                        to).start()
        token[...] = jnp.zeros_like(token)

    outs = pl.pallas_call(
        body, name=name,
        out_shape=(pltpu.SemaphoreType.DMA((4 * n,)), pltpu.SemaphoreType.DMA((4 * n,)),
                   *[pltpu.HBM(a.shape, a.dtype) for a in lands], _sds((SUBLANES, LANES))),
        in_specs=[_HBM] * (2 * n) + [pl.BlockSpec(memory_space=pl.ANY)],
        out_specs=(_SEM, _SEM, *[_HBM] * n, pl.BlockSpec(memory_space=pltpu.VMEM)),
        input_output_aliases={n + i: 2 + i for i in range(n)}, compiler_params=_EFFECT,
    )(*srcs, *lands, after)
    return (outs[0], outs[1], srcs, outs[2:2 + n]), outs[-1]


def _gather_forward(name, state, *after):
    send_a, recv_a, srcs, lands = state
    n = len(lands)

    def body(*refs):
        land_refs, recv_a_ref = refs[:n], refs[n]
        send_b, recv_b = refs[n + 1 + len(after)], refs[n + 2 + len(after)]
        mine, sibling, chips = _chip_routes()
        for j, chip in enumerate(chips):
            for i in range(n):
                block = land_refs[i].at[_slab(chip)]
                _remote(block, block, send_b.at[j * n + i], recv_a_ref.at[(1 + j) * n + i], chip).wait_recv()
                _remote(block, block, send_b.at[j * n + i], recv_b.at[j * n + i], sibling).start()

    outs = pl.pallas_call(
        body, name=name,
        out_shape=(pltpu.SemaphoreType.DMA((3 * n,)), pltpu.SemaphoreType.DMA((3 * n,)),
                   *[pltpu.HBM(a.shape, a.dtype) for a in lands]),
        in_specs=[_HBM] * n + [_SEM] + [pl.BlockSpec(memory_space=pl.ANY)] * len(after),
        out_specs=(_SEM, _SEM, *[_HBM] * n),
        input_output_aliases={i: 2 + i for i in range(n)}, compiler_params=_EFFECT,
    )(*lands, recv_a, *after)
    return (send_a, recv_a, srcs, list(outs[2:])), (outs[0], outs[1])


def _gather_wait(name, state, forwarded, *after):
    send_a, recv_a, srcs, lands = state
    send_b, recv_b = forwarded
    n = len(lands)

    def body(*refs):
        src_refs, land_refs = refs[:n], refs[n:2 * n]
        sa, ra, sb, rb = refs[2 * n:2 * n + 4]
        mine, sibling, chips = _chip_routes()
        for i in range(n):
            for k, to in enumerate([sibling] + chips):
                _remote(src_refs[i], land_refs[i].at[_slab(mine)], sa.at[k * n + i], ra.at[k * n + i], to).wait_send()
            theirs = land_refs[i].at[_slab(sibling)]
            _remote(theirs, theirs, sa.at[i], ra.at[i], sibling).wait_recv()
            for j, chip in enumerate(chips):
                sent = land_refs[i].at[_slab(chip)]
                got = land_refs[i].at[_slab((chip[0], chip[1], sibling[2]))]
                _remote(sent, sent, sb.at[j * n + i], rb.at[j * n + i], sibling).wait_send()
                _remote(got, got, sb.at[j * n + i], rb.at[j * n + i], sibling).wait_recv()

    outs = pl.pallas_call(
        body, name=name,
        out_shape=tuple(pltpu.HBM(a.shape, a.dtype) for a in lands),
        in_specs=[_HBM] * (2 * n) + [_SEM] * 4 + [pl.BlockSpec(memory_space=pl.ANY)] * len(after),
        out_specs=tuple([_HBM] * n),
        input_output_aliases={n + i: i for i in range(n)}, compiler_params=_EFFECT,
    )(*srcs, *lands, send_a, recv_a, send_b, recv_b, *after)
    return list(outs), list(srcs)


def _join_cols(parts, name, tr):
    _, r, c = parts.shape

    def body(p_ref, o_ref):
        for j in range(N_DEV):
            o_ref[:, j * c:(j + 1) * c] = p_ref[j]

    return _pcall(body, name, (r // tr,), [pl.BlockSpec((N_DEV, tr, c), lambda i: (0, i, 0))],
                  pl.BlockSpec((tr, N_DEV * c), lambda i: (i, 0)), _sds((r, N_DEV * c), parts.dtype))(parts)


def _split_cols(full, name, tr):
    r, c = full.shape[0], full.shape[1] // N_DEV

    def body(f_ref, o_ref):
        for j in range(N_DEV):
            o_ref[j] = f_ref[:, j * c:(j + 1) * c]

    return _pcall(body, name, (r // tr,), [pl.BlockSpec((tr, N_DEV * c), lambda i: (i, 0))],
                  pl.BlockSpec((N_DEV, tr, c), lambda i: (0, i, 0)), _sds((N_DEV, r, c), full.dtype))(full)


def _my_slab():
    return (4 * lax.axis_index("x") + 2 * lax.axis_index("y") + lax.axis_index("c")).astype(jnp.int32).reshape(1)


def _adamw(parts, sent, w, m, v, name, tile):
    _, rows, cols = w.shape

    def body(me_ref, p_ref, s_ref, w_ref, m_ref, v_ref, g_out, d_out, m_out, v_out):
        me = me_ref[0]
        g = jnp.where(me == 0, s_ref[0], p_ref[0]).astype(F32)
        for k in range(1, N_DEV):
            g = g + jnp.where(me == k, s_ref[0], p_ref[k]).astype(F32)
        m1 = ADAM_B1 * m_ref[0] + (1.0 - ADAM_B1) * g
        v1 = ADAM_B2 * v_ref[0] + (1.0 - ADAM_B2) * (g * g)
        m_hat = m1 / (1.0 - ADAM_B1 ** ADAM_STEP)
        v_hat = v1 / (1.0 - ADAM_B2 ** ADAM_STEP)
        g_out[0] = g
        d_out[0] = -ADAM_LR * (m_hat / (jnp.sqrt(v_hat) + ADAM_EPS) + ADAM_WD * w_ref[0])
        m_out[0] = m1
        v_out[0] = v1

    row = pl.BlockSpec((1, tile, cols), lambda i, me: (0, i, 0))
    return pl.pallas_call(
        body, name=name, out_shape=[_sds((1, rows, cols))] * 4,
        grid_spec=pltpu.PrefetchScalarGridSpec(
            num_scalar_prefetch=1, grid=(rows // tile,),
            in_specs=[pl.BlockSpec((N_DEV, tile, cols), lambda i, me: (0, i, 0)),
                      pl.BlockSpec((1, tile, cols), lambda i, me: (me[0], i, 0)), row, row, row],
            out_specs=[row, row, row, row]),
        compiler_params=pltpu.CompilerParams(dimension_semantics=("arbitrary",), vmem_limit_bytes=VMEM_LIMIT),
    )(_my_slab(), parts, sent, w, m, v)


BIG = {
    "w_in": ((N_IN // N_DEV, D_MODEL), False, N_IN // N_DEV // 3),
    "w_glu": ((S5_WIDTH // N_DEV, S5_WIDTH), False, S5_WIDTH // N_DEV),
    "w_pa": ((S5_WIDTH, D_MODEL // N_DEV), True, S5_WIDTH),
    "w_pb": ((HG_WIDTH, D_MODEL // N_DEV), True, HG_WIDTH),
    "w_out": ((D_MODEL // N_DEV, D_MODEL), False, D_MODEL // N_DEV),
    "w_up": ((2 * D_FF // N_DEV, D_MODEL), False, 2 * D_FF // N_DEV // 4),
    "w_conv": ((CONV_W, 2 * D_FF // N_DEV), True, CONV_W),
    "w_down": ((D_FF // N_DEV, D_MODEL), False, D_FF // N_DEV // 2),
}
TRANSPOSED = ("w_in", "w_up", "s5_b_re", "s5_b_im")
UNALIGNED_COLS = ("w_conv",)


def _stored(n, arr):
    return jnp.swapaxes(arr, -1, -2) if n in TRANSPOSED else arr


def _join_shards(n, parts):
    (a, b), by_cols, _ = BIG[n]
    if not by_cols:
        return parts.reshape(N_DEV * a, b)
    if n in UNALIGNED_COLS:
        return _join_cols(parts, "join_" + n, min(a, 256))
    return parts.transpose(1, 0, 2).reshape(a, N_DEV * b)


def _split_shards(n, full):
    (a, b), by_cols, _ = BIG[n]
    if not by_cols:
        return full.reshape(N_DEV, a, b)
    if n in UNALIGNED_COLS:
        return _split_cols(full, "split_" + n, min(a, 256))
    return full.reshape(a, N_DEV, b).transpose(1, 0, 2)


SMALL_CORE = {
    "s5_b_re": GSC, "s5_b_im": GSC, "s5_c_re": GSC, "s5_c_im": GSC,
    "g_mix": (1, D_MODEL), "g_ffn": (1, D_MODEL), "g_final": (1, D_MODEL), "s5_d": (1, S5_WIDTH),
    "b_glu": (1, S5_WIDTH), "hg_norm_gain": (1, HG_WIDTH), "hg_lb_logits": (2, HG_WIDTH), "b_conv": (1, 2 * D_FF),
    "s5_log_dt": (1, S5_GROUPS), "s5_a_re": (S5_GROUPS, S5_STATE), "s5_a_im": (S5_GROUPS, S5_STATE), "loss": (1, 1),
}
BLOCK_ROWS = 32


def _small_rows():
    rows, r = {}, 0
    for n, core in SMALL_CORE.items():
        rows[n] = r
        r += BLOCK_ROWS if len(core) == 3 else -(-math.prod(core) // PACK_W)
    return rows, -(-r // SUBLANES) * SUBLANES


SMALL_ROW, SMALL_ROWS = _small_rows()


def _small_pieces(name):
    r, core = SMALL_ROW[name], SMALL_CORE[name]
    if len(core) == 3:
        return [((g, slice(None), slice(None)), slice(r + S5_GROUP * (g % 2), r + S5_GROUP * (g % 2 + 1)),
                 slice(S5_STATE * (g // 2), S5_STATE * (g // 2 + 1))) for g in range(S5_GROUPS)]
    pieces = []
    for i in range(core[0]):
        for c0 in range(0, core[1], PACK_W):
            w, flat = min(PACK_W, core[1] - c0), i * core[1] + c0
            pieces.append(((slice(i, i + 1), slice(c0, c0 + w)), slice(r + flat // PACK_W, r + flat // PACK_W + 1),
                           slice(flat % PACK_W, flat % PACK_W + w)))
    return pieces


def _core_index(ref, name, idx):
    return (0,) * (len(ref.shape) - len(SMALL_CORE[name])) + idx


def _pack_small_grads(grads):
    names = list(SMALL_CORE)

    def body(*refs):
        pack = refs[-1]
        pack[...] = jnp.zeros_like(pack)
        for ref, n in zip(refs, names):
            for idx, rows, lanes in _small_pieces(n):
                pack[rows, lanes] = ref[_core_index(ref, n, idx)]

    return _pcall(body, "pack_small_grads", (1,), [_full(grads[n].shape) for n in names],
                  _full((SMALL_ROWS, PACK_W)), _sds((SMALL_ROWS, PACK_W)))(*[grads[n] for n in names])


def _adamw_small(parts, sent, names, rows, given, name):
    lo, hi = rows
    k = len(names)
    shapes = [given[n].shape for n in names]

    def body(*refs):
        me, p_ref, s_ref, ins, outs = refs[0][0], refs[1], refs[2], refs[3:3 + 3 * k], refs[3 + 3 * k:3 + 7 * k]
        packs, results = refs[3 + 7 * k:6 + 7 * k], refs[6 + 7 * k:]
        for j, pack in enumerate(packs):
            pack[...] = jnp.zeros_like(pack)
            for ref, n in zip(ins[j * k:(j + 1) * k], names):
                for idx, prow, lanes in _small_pieces(n):
                    pack[slice(prow.start - lo, prow.stop - lo), lanes] = ref[_core_index(ref, n, idx)]
        mine = s_ref[lo:hi, :]
        g = jnp.where(me == 0, mine, p_ref[0, lo:hi, :])
        for d in range(1, N_DEV):
            g = g + jnp.where(me == d, mine, p_ref[d, lo:hi, :])
        m1 = ADAM_B1 * packs[1][...] + (1.0 - ADAM_B1) * g
        v1 = ADAM_B2 * packs[2][...] + (1.0 - ADAM_B2) * (g * g)
        m_hat = m1 / (1.0 - ADAM_B1 ** ADAM_STEP)
        v_hat = v1 / (1.0 - ADAM_B2 ** ADAM_STEP)
        results[0][...] = g
        results[1][...] = -ADAM_LR * (m_hat / (jnp.sqrt(v_hat) + ADAM_EPS) + ADAM_WD * packs[0][...])
        results[2][...] = m1
        results[3][...] = v1
        for j, result in enumerate(results):
            for ref, n in zip(outs[j * k:(j + 1) * k], names):
                for idx, prow, lanes in _small_pieces(n):
                    ref[_core_index(ref, n, idx)] = result[slice(prow.start - lo, prow.stop - lo), lanes]

    flat = _pcall(body, name, (1,),
                  [pl.BlockSpec(memory_space=pltpu.SMEM), _full(parts.shape), _full(sent.shape)]
                  + [_full(s) for s in shapes] * 3,
                  [_full(s) for s in shapes] * 4, [_sds(s) for s in shapes] * 4,
                  scratch=[pltpu.VMEM((hi - lo, PACK_W), F32)] * 7,
                  )(_my_slab(), parts, sent, *[given[pre + n] for pre in ("", "m_", "v_") for n in names])
    return {n: [flat[j * k + i] for j in range(4)] for i, n in enumerate(names)}


def kernel(x, g_mix, w_in, s5_a_re, s5_a_im, s5_log_dt, s5_b_re, s5_b_im, s5_c_re, s5_c_im, s5_d, w_glu, b_glu, hg_lb_logits, hg_norm_gain, w_pa, w_pb, w_out, g_ffn, w_up, w_conv, b_conv, w_down, g_final, loss_target, m_g_mix, m_w_in, m_s5_a_re, m_s5_a_im, m_s5_log_dt, m_s5_b_re, m_s5_b_im, m_s5_c_re, m_s5_c_im, m_s5_d, m_w_glu, m_b_glu, m_hg_lb_logits, m_hg_norm_gain, m_w_pa, m_w_pb, m_w_out, m_g_ffn, m_w_up, m_w_conv, m_b_conv, m_w_down, m_g_final, v_g_mix, v_w_in, v_s5_a_re, v_s5_a_im, v_s5_log_dt, v_s5_b_re, v_s5_b_im, v_s5_c_re, v_s5_c_im, v_s5_d, v_w_glu, v_b_glu, v_hg_lb_logits, v_hg_norm_gain, v_w_pa, v_w_pb, v_w_out, v_g_ffn, v_w_up, v_w_conv, v_b_conv, v_w_down, v_g_final):
    given = dict(locals())
    small_names = [n for n in SMALL_CORE if n != "loss"]

    pay = {n: given[n][0] if n == "w_conv" else _stored(n, given[n])[0].astype(BF16) for n in BIG}
    groups = {"in": ["w_in"], "mix": ["w_glu", "w_pa", "w_pb", "w_out"], "ffn": ["w_up", "w_down", "w_conv"]}
    gathers, order = {}, pay["w_in"]
    for grp, names in groups.items():
        gathers[grp], order = _gather_start("gather_" + grp + "_start", [pay[n] for n in names], order)

    forwards = {}

    def forward(grp, *after):
        if grp == "in":
            after = (*after, order)
        forwards[grp] = _gather_forward("gather_" + grp + "_forward", gathers[grp], *after)

    def weights(grp, *after):
        if grp not in forwards:
            forward(grp, *after)
        got, _ = _gather_wait("gather_" + grp + "_wait", *forwards[grp], *after)
        return {n: _join_shards(n, g) for n, g in zip(groups[grp], got)}

    weights.forward = forward

    in_flight, started = [], []

    def emit(grads):
        names = list(grads)
        state, token = _exchange_start("grads_" + names[0] + "_start", [_split_shards(n, grads[n]) for n in names],
                                       grads[names[0]])
        in_flight.append((names, state))
        return token

    def emit_small(grads):
        pack = _pack_small_grads(grads)
        state, token = _gather_start("grads_small_start", [pack], pack)
        in_flight.append((["small"], state))
        started.append(token)

    sp = {n: (given[n] if n in ("g_final", "hg_lb_logits") else _stored(n, given[n])[0]) for n in small_names}
    sp["g_mix"] = _after(sp["g_mix"], order)
    dx = _local_step(x, loss_target, weights, sp, emit, emit_small)

    res = {}
    after = [started[-1]]
    for names, state in in_flight:
        if names == ["small"]:
            state, forwarded = _gather_forward("grads_small_forward", state, *after)
            parts, sent = _gather_wait("grads_small_wait", state, forwarded)
        else:
            parts, sent = _exchange_wait("grads_" + names[0] + "_wait", state, *after)
        if names != ["small"]:
            after = []
            for n, part, mine in zip(names, parts, sent):
                raw = _adamw(part, mine, *[_stored(n, given[pre + n]) for pre in ("", "m_", "v_")], "adamw_" + n,
                             BIG[n][2])
                res[n] = [_stored(n, r) for r in raw]
                after.append(raw[0])
            continue
        sgiven = {pre + n: _stored(n, given[pre + n]) for pre in ("", "m_", "v_") for n in small_names}
        for pre in ("", "m_", "v_"):
            sgiven[pre + "g_final"] = given[pre + "g_final"].reshape(1, D_MODEL)
            sgiven[pre + "loss"] = jnp.zeros((1, 1), F32)
        raw = _adamw_small(parts[0], sent[0], list(SMALL_CORE), (0, SMALL_ROWS), sgiven, "adamw_small")
        res.update({n: [_stored(n, r) for r in raw[n]] for n in small_names})
        res["g_final"] = [r.reshape(D_MODEL) for r in raw["g_final"]]
        total_loss = raw["loss"][0].reshape(())
        after = [raw["s5_b_re"][0], raw["g_mix"][0]]
    return (total_loss, dx, *[res[n][0] for n in WEIGHT_ORDER], *[res[n][1] for n in WEIGHT_ORDER],
            *[res[n][2] for n in WEIGHT_ORDER], *[res[n][3] for n in WEIGHT_ORDER])
```

```python
import math

import jax
import jax.numpy as jnp
from jax import lax
from jax.experimental import pallas as pl
from jax.experimental.pallas import tpu as pltpu

F32 = jnp.float32
BF16 = jnp.bfloat16

D_MODEL = 1024
S5_WIDTH = 512
S5_GROUP = 16
S5_GROUPS = 32
S5_STATE = 64
S5_N = S5_GROUPS * S5_STATE
HG_WIDTH = 512
HG_HEAD = 128
HG_HEADS = 4
D_FF = 2816
CONV_W = 3
CHUNK = 64
N_IN = S5_WIDTH + 4 * HG_WIDTH + 2 * D_MODEL
EPS = 1e-6
QSCALE = HG_HEAD ** -0.5

ADAM_LR = 0.001
ADAM_B1 = 0.9
ADAM_B2 = 0.999
ADAM_EPS = 1e-08
ADAM_WD = 0.01
ADAM_STEP = 10

N_DEV = 8
V7X_VMEM_BYTES = 64 * 1024 * 1024
VMEM_LIMIT = V7X_VMEM_BYTES * 7 // 8
SUBLANES = 8
LANES = 128
PACK_W = 1024

WEIGHT_ORDER = ("g_mix", "w_in", "s5_a_re", "s5_a_im", "s5_log_dt", "s5_b_re", "s5_b_im", "s5_c_re", "s5_c_im",
                "s5_d", "w_glu", "b_glu", "hg_lb_logits", "hg_norm_gain", "w_pa", "w_pb", "w_out", "g_ffn",
                "w_up", "w_conv", "b_conv", "w_down", "g_final")


def _pcall(body, name, grid, in_specs, out_specs, out_shape, scratch=()):
    return pl.pallas_call(
        body, name=name, grid=grid, in_specs=in_specs, out_specs=out_specs, out_shape=out_shape,
        scratch_shapes=list(scratch),
        compiler_params=pltpu.CompilerParams(dimension_semantics=("arbitrary",) * len(grid),
                                             vmem_limit_bytes=VMEM_LIMIT),
    )


def _full(shape):
    return pl.BlockSpec(shape, lambda *_: (0,) * len(shape))


def _sds(shape, dtype=F32):
    return jax.ShapeDtypeStruct(shape, dtype)


def _dot(a, b):
    return jnp.dot(a.astype(BF16), b.astype(BF16), preferred_element_type=F32)


def _dot_nt(a, b):
    return lax.dot_general(a.astype(BF16), b.astype(BF16), (((1,), (1,)), ((), ())), preferred_element_type=F32)


def _dot_tn(a, b):
    return lax.dot_general(a.astype(BF16), b.astype(BF16), (((0,), (0,)), ((), ())), preferred_element_type=F32)


def _sigmoid(x):
    return jax.nn.sigmoid(x)


GELU_C = math.sqrt(2.0 / math.pi)
GELU_A = 0.044715


def _gelu(x):
    return 0.5 * x * (1.0 + jnp.tanh(GELU_C * (x + GELU_A * (x * x * x))))


def _gelu_grad(x):
    t = jnp.tanh(GELU_C * (x + GELU_A * (x * x * x)))
    return 0.5 * (1.0 + t) + 0.5 * x * (1.0 - t * t) * (GELU_C * (1.0 + 3.0 * GELU_A * x * x))


def _cumsum_rows(v, reverse=False):
    n = v.shape[0]
    row = lax.broadcasted_iota(jnp.int32, v.shape, 0)
    s = 1
    while s < n:
        if reverse:
            v = v + jnp.where(row < n - s, pltpu.roll(v, n - s, axis=0), 0.0)
        else:
            v = v + jnp.where(row >= s, pltpu.roll(v, s, axis=0), 0.0)
        s *= 2
    return v


def _token_tile(seq):
    return min(256, seq)


def _s5_coeffs(a_re, a_im, ldt):
    dt = jnp.exp(ldt)
    mag = jnp.exp(a_re * dt)
    ang = a_im * dt
    lb_re = mag * jnp.cos(ang)
    lb_im = mag * jnp.sin(ang)
    den = a_re * a_re + a_im * a_im
    n_re = lb_re - 1.0
    n_im = lb_im
    co_re = (n_re * a_re + n_im * a_im) / den
    co_im = (n_im * a_re - n_re * a_im) / den
    return lb_re, lb_im, co_re, co_im


GS, GSC = (S5_GROUPS, S5_STATE), (S5_GROUPS, S5_GROUP, S5_STATE)


def _params_fwd(a_re, a_im, ldt, bt_re, bt_im, logits):
    def body(are, aim, ld, bre, bim, lg, lr_o, li_o, bbr_o, bbi_o, lb_o):
        lr, li, co_re, co_im = _s5_coeffs(are[...], aim[...], ld[...])
        lr_o[...] = lr
        li_o[...] = li
        for g in range(S5_GROUPS):
            cr, ci = co_re[g:g + 1, :], co_im[g:g + 1, :]
            bbr_o[g] = cr * bre[g] - ci * bim[g]
            bbi_o[g] = cr * bim[g] + ci * bre[g]
        lb_o[...] = _sigmoid(lg[0:1, :] - lg[1:2, :])

    return _pcall(body, "params_fwd", (1,),
                  [_full(GS), _full(GS), _full((S5_GROUPS, 1)), _full(GSC), _full(GSC), _full((2, HG_WIDTH))],
                  [_full(GS), _full(GS), _full(GSC), _full(GSC), _full((1, HG_WIDTH))],
                  [_sds(GS), _sds(GS), _sds(GSC), _sds(GSC), _sds((1, HG_WIDTH))],
                  )(a_re, a_im, ldt, bt_re, bt_im, logits)


def _params_bwd(a_re, a_im, ldt, bt_re, bt_im, logits, dlr, dli, dbbr, dbbi, dlb):
    def body(are, aim, ld, bre, bim, lg, dlr_r, dli_r, dbbr_r, dbbi_r, dlb_r,
             dare_o, daim_o, dld_o, dbre_o, dbim_o, dlg_o, dcr_ref, dci_ref):
        (_, _, co_re, co_im), vjp = jax.vjp(_s5_coeffs, are[...], aim[...], ld[...])
        for g in range(S5_GROUPS):
            cr, ci = co_re[g:g + 1, :], co_im[g:g + 1, :]
            gr, gi, br, bi = dbbr_r[g], dbbi_r[g], bre[g], bim[g]
            dbre_o[g] = cr * gr + ci * gi
            dbim_o[g] = cr * gi - ci * gr
            dcr_ref[g:g + 1, :] = jnp.sum(gr * br + gi * bi, axis=0, keepdims=True)
            dci_ref[g:g + 1, :] = jnp.sum(gi * br - gr * bi, axis=0, keepdims=True)
        dare, daim, dld = vjp((dlr_r[...], dli_r[...], dcr_ref[...], dci_ref[...]))
        dare_o[...] = dare
        daim_o[...] = daim
        dld_o[...] = dld
        lb = _sigmoid(lg[0:1, :] - lg[1:2, :])
        d0 = dlb_r[...] * lb * (1.0 - lb)
        dlg_o[0:1, :] = d0
        dlg_o[1:2, :] = -d0

    return _pcall(body, "params_bwd", (1,),
                  [_full(GS), _full(GS), _full((S5_GROUPS, 1)), _full(GSC), _full(GSC), _full((2, HG_WIDTH)),
                   _full(GS), _full(GS), _full(GSC), _full(GSC), _full((1, HG_WIDTH))],
                  [_full(GS), _full(GS), _full((S5_GROUPS, 1)), _full(GSC), _full(GSC), _full((2, HG_WIDTH))],
                  [_sds(GS), _sds(GS), _sds((S5_GROUPS, 1)), _sds(GSC), _sds(GSC), _sds((2, HG_WIDTH))],
                  scratch=[pltpu.VMEM(GS, F32), pltpu.VMEM(GS, F32)],
                  )(a_re, a_im, ldt, bt_re, bt_im, logits, dlr, dli, dbbr, dbbi, dlb)


def _band_blocks(m):
    g, r, c = m.shape
    gb = g // S5_BANDS
    m4 = m.astype(BF16).reshape(S5_BANDS, gb, r, c)
    on_diag = jnp.eye(gb, dtype=bool)[None, :, None, :, None]
    return jnp.where(on_diag, m4[:, :, :, None, :], 0).reshape(S5_BANDS, gb * r, gb * c)


def _diag_blocks(band, r, c):
    g, nb = band.shape[0] // r, band.shape[1] // c
    on_diag = (jnp.arange(g) % nb)[:, None, None, None] == jnp.arange(nb)[None, None, :, None]
    return jnp.sum(jnp.where(on_diag, band.reshape(g, r, nb, c), 0.0), axis=2)


def _in_proj(x, g_mix, w_in, tm):
    t = x.shape[0]

    def body(x_ref, g_ref, w_ref, u_ref, za_ref, zh_ref, zg_ref):
        xv = x_ref[...]
        r = lax.rsqrt(jnp.mean(xv * xv, axis=-1, keepdims=True) + EPS)
        u = (xv * r * g_ref[...]).astype(BF16)
        u_ref[...] = u
        za_ref[...] = _dot_nt(u, w_ref[0:S5_WIDTH, :])
        zh_ref[...] = _dot_nt(u, w_ref[S5_WIDTH:S5_WIDTH + 4 * HG_WIDTH, :])
        zg_ref[...] = _dot_nt(u, w_ref[S5_WIDTH + 4 * HG_WIDTH:, :]).astype(BF16)

    row = lambda w: pl.BlockSpec((tm, w), lambda i: (i, 0))
    return _pcall(body, "in_proj", (t // tm,),
                  [row(D_MODEL), _full((1, D_MODEL)), _full((N_IN, D_MODEL))],
                  [row(D_MODEL), row(S5_WIDTH), row(4 * HG_WIDTH), row(2 * D_MODEL)],
                  [_sds((t, D_MODEL), BF16), _sds((t, S5_WIDTH)), _sds((t, 4 * HG_WIDTH)),
                   _sds((t, 2 * D_MODEL), BF16)],
                  )(x, g_mix, w_in)


S5_LANES = 512
S5_BANDS = 4


def _band(q):
    return (slice(q * S5_WIDTH // S5_BANDS, (q + 1) * S5_WIDTH // S5_BANDS),
            slice(q * S5_N // S5_BANDS, (q + 1) * S5_N // S5_BANDS))


def _im(st):
    return slice(S5_N + st.start, S5_N + st.stop)


SCAN_UNROLL = 8


def _complex_scan(buf_ref, lam_ref, st_ref, nb, ts, reverse):
    lanes = [slice(cc * S5_LANES, (cc + 1) * S5_LANES) for cc in range(S5_N // S5_LANES)]
    chains = [(b, re) for b in range(nb) for re in lanes]
    nch = len(chains)
    wr = {re.start: lam_ref[0:1, re] for re in lanes}
    wi = {re.start: -lam_ref[1:2, re] if reverse else lam_ref[1:2, re] for re in lanes}

    def block(ib, carry):
        vr, vi = list(carry[:nch]), list(carry[nch:])
        first = ts - SCAN_UNROLL - ib * SCAN_UNROLL if reverse else ib * SCAN_UNROLL
        first = pl.multiple_of(first, SCAN_UNROLL)
        for k in range(SCAN_UNROLL):
            row = pl.ds(first + (SCAN_UNROLL - 1 - k if reverse else k), 1)
            for c, (b, re) in enumerate(chains):
                nr = wr[re.start] * vr[c] - wi[re.start] * vi[c] + buf_ref[b, row, re]
                ni = wr[re.start] * vi[c] + wi[re.start] * vr[c] + buf_ref[b, row, _im(re)]
                buf_ref[b, row, re] = nr
                buf_ref[b, row, _im(re)] = ni
                vr[c], vi[c] = nr, ni
        return tuple(vr + vi)

    init = tuple(st_ref[b, 0:1, re] for b, re in chains) + tuple(st_ref[b, 1:2, re] for b, re in chains)
    last = lax.fori_loop(0, ts // SCAN_UNROLL, block, init)
    for c, (b, re) in enumerate(chains):
        st_ref[b, 0:1, re] = last[c]
        st_ref[b, 1:2, re] = last[nch + c]


BAND_CH = S5_WIDTH // S5_BANDS
BAND_ST = S5_N // S5_BANDS


def _s5_fwd(za, b_bands, lam, c_bands, dskip, nb, seq, ts):
    nts = seq // ts

    def body(za_ref, br_ref, bi_ref, lam_ref, cr_ref, ci_ref, d_ref, xs_ref, y_ref, buf_ref, st_ref):
        @pl.when(pl.program_id(0) == 0)
        def _():
            st_ref[...] = jnp.zeros_like(st_ref)

        for b in range(nb):
            zav = za_ref[b]
            for q in range(S5_BANDS):
                ch, st = _band(q)
                buf_ref[b, :, st] = _dot(zav[:, ch], br_ref[q])
                buf_ref[b, :, _im(st)] = _dot(zav[:, ch], bi_ref[q])
        _complex_scan(buf_ref, lam_ref, st_ref, nb, ts, reverse=False)
        for b in range(nb):
            zav = za_ref[b]
            xs_ref[b] = buf_ref[b].astype(BF16)
            for q in range(S5_BANDS):
                ch, st = _band(q)
                y_ref[b, :, ch] = (_dot(xs_ref[b, :, st], cr_ref[q]) + _dot(xs_ref[b, :, _im(st)], ci_ref[q])
                                   + d_ref[:, ch] * zav[:, ch])

    tok = lambda w: pl.BlockSpec((nb, ts, w), lambda j: (0, j, 0))
    to_st, to_ch = _full((S5_BANDS, BAND_CH, BAND_ST)), _full((S5_BANDS, BAND_ST, BAND_CH))
    return _pcall(body, "s5_fwd", (nts,),
                  [tok(S5_WIDTH), to_st, to_st, _full((2, S5_N)), to_ch, to_ch, _full((1, S5_WIDTH))],
                  [tok(2 * S5_N), tok(S5_WIDTH)],
                  [_sds((nb, seq, 2 * S5_N), BF16), _sds((nb, seq, S5_WIDTH))],
                  scratch=[pltpu.VMEM((nb, ts, 2 * S5_N), F32), pltpu.VMEM((nb, 2, S5_N), F32)],
                  )(za, *b_bands, lam, *c_bands, dskip)


def _hgrn_gates(zq, zf, lbh):
    sf = _sigmoid(zf)
    f = lbh + (1.0 - lbh) * sf
    sq = _sigmoid(zq)
    qa = zq * sq * QSCALE
    bc = _cumsum_rows(jnp.log(f))
    bm = bc[CHUNK // 2 - 1:CHUNK // 2, :]
    bl = bc[CHUNK - 1:CHUNK, :]
    return sf, f, sq, qa, bc, bm, bl


HG_CHUNKS_PER_STEP = 4


def _hgrn_fwd(zh, lb, nb, seq):
    nc = seq // CHUNK
    cps = HG_CHUNKS_PER_STEP

    def body(zh_ref, lb_ref, o_ref, sts_ref, st_ref):
        @pl.when(pl.program_id(0) == 0)
        def _():
            st_ref[...] = jnp.zeros_like(st_ref)

        causal = (lax.broadcasted_iota(jnp.int32, (CHUNK, CHUNK), 0)
                  >= lax.broadcasted_iota(jnp.int32, (CHUNK, CHUNK), 1))
        for cc in range(cps):
            rows = slice(cc * CHUNK, (cc + 1) * CHUNK)
            for b in range(nb):
                for h in range(HG_HEADS):
                    hs = slice(h * HG_HEAD, (h + 1) * HG_HEAD)
                    zq = zh_ref[b, rows, h * HG_HEAD:(h + 1) * HG_HEAD]
                    zf = zh_ref[b, rows, HG_WIDTH + h * HG_HEAD:HG_WIDTH + (h + 1) * HG_HEAD]
                    zi = zh_ref[b, rows, 2 * HG_WIDTH + h * HG_HEAD:2 * HG_WIDTH + (h + 1) * HG_HEAD]
                    _, f, _, qa, bc, bm, bl = _hgrn_gates(zq, zf, lb_ref[:, hs])
                    k = 1.0 - f
                    qt = qa * jnp.exp(bc - bm)
                    kt = k * jnp.exp(bm - bc)
                    qb = qa * jnp.exp(bc)
                    kd = k * jnp.exp(bl - bc)
                    st = st_ref[b, h]
                    sts_ref[b, cc, h] = st
                    a = jnp.where(causal, _dot_nt(qt, kt), 0.0)
                    o_ref[b, rows, hs] = _dot(a, zi) + _dot_nt(qb, st)
                    st_ref[b, h] = st * jnp.exp(bl) + _dot_tn(zi, kd)

    return _pcall(body, "hgrn_fwd", (nc // cps,),
                  [pl.BlockSpec((nb, cps * CHUNK, 4 * HG_WIDTH), lambda c: (0, c, 0)), _full((1, HG_WIDTH))],
                  [pl.BlockSpec((nb, cps * CHUNK, HG_WIDTH), lambda c: (0, c, 0)),
                   pl.BlockSpec((nb, cps, HG_HEADS, HG_HEAD, HG_HEAD), lambda c: (0, c, 0, 0, 0))],
                  [_sds((nb, seq, HG_WIDTH)), _sds((nb, nc, HG_HEADS, HG_HEAD, HG_HEAD))],
                  scratch=[pltpu.VMEM((nb, HG_HEADS, HG_HEAD, HG_HEAD), F32)])(zh, lb)


def _head_rms(o):
    parts = []
    for h in range(HG_HEADS):
        oh = o[:, h * HG_HEAD:(h + 1) * HG_HEAD]
        r = lax.rsqrt(jnp.mean(oh * oh, axis=-1, keepdims=True) + EPS)
        parts.append(jnp.broadcast_to(r, oh.shape))
    return jnp.concatenate(parts, axis=1)


def _head_mean(v):
    parts = []
    for h in range(HG_HEADS):
        vh = v[:, h * HG_HEAD:(h + 1) * HG_HEAD]
        parts.append(jnp.broadcast_to(jnp.mean(vh, axis=-1, keepdims=True), vh.shape))
    return jnp.concatenate(parts, axis=1)


def _mix_fwd(x, y0, o, zh, zgt, w_glu, b_glu, gain, w_pa, w_pb, w_out, g_ffn, tm):
    t = x.shape[0]

    def body(x_ref, y0_ref, o_ref, zg_ref, zgt_ref, wglu_ref, bglu_ref, gain_ref, wpa_ref, wpb_ref, wout_ref,
             gffn_ref, x1_ref, u2_ref, pa_ref, pb_ref, ya2_ref, yb_ref):
        ya1 = _gelu(y0_ref[...])
        s = _sigmoid(_dot(ya1, wglu_ref[...]) + bglu_ref[...])
        ya2 = (ya1 * s).astype(BF16)
        ov = o_ref[...]
        zg = zg_ref[...]
        yb = (ov * _head_rms(ov) * gain_ref[...] * (zg * _sigmoid(zg))).astype(BF16)
        ya2_ref[...] = ya2
        yb_ref[...] = yb
        pa = jnp.dot(ya2, wpa_ref[...], preferred_element_type=F32)
        pb = jnp.dot(yb, wpb_ref[...], preferred_element_type=F32)
        pa_ref[...] = pa.astype(BF16)
        pb_ref[...] = pb.astype(BF16)
        m = (_sigmoid(zgt_ref[:, 0:D_MODEL].astype(F32)) * pa
             + _sigmoid(zgt_ref[:, D_MODEL:].astype(F32)) * pb)
        x1 = x_ref[...] + _dot(m, wout_ref[...])
        x1_ref[...] = x1
        r = lax.rsqrt(jnp.mean(x1 * x1, axis=-1, keepdims=True) + EPS)
        u2_ref[...] = (x1 * r * gffn_ref[...]).astype(BF16)

    row = lambda w: pl.BlockSpec((tm, w), lambda i: (i, 0))
    return _pcall(body, "mix_fwd", (t // tm,),
                  [row(D_MODEL), row(S5_WIDTH), row(HG_WIDTH), pl.BlockSpec((tm, HG_WIDTH), lambda i: (i, 3)),
                   row(2 * D_MODEL), _full((S5_WIDTH, S5_WIDTH)), _full((1, S5_WIDTH)), _full((1, HG_WIDTH)),
                   _full((S5_WIDTH, D_MODEL)), _full((HG_WIDTH, D_MODEL)), _full((D_MODEL, D_MODEL)),
                   _full((1, D_MODEL))],
                  [row(D_MODEL), row(D_MODEL), row(D_MODEL), row(D_MODEL), row(S5_WIDTH), row(HG_WIDTH)],
                  [_sds((t, D_MODEL)), _sds((t, D_MODEL), BF16), _sds((t, D_MODEL), BF16), _sds((t, D_MODEL), BF16),
                   _sds((t, S5_WIDTH), BF16), _sds((t, HG_WIDTH), BF16)],
                  )(x, y0, o, zh, zgt, w_glu, b_glu, gain, w_pa, w_pb, w_out, g_ffn)


FF_COLS = 256
FF_UP_TILE = 2 * D_FF // 2


def _ffn_up(u2, w_up, tm):
    t = u2.shape[0]
    n = 2 * D_FF

    def body(u_ref, w_ref, h_ref):
        h_ref[...] = _dot_nt(u_ref[...], w_ref[...]).astype(BF16)

    return _pcall(body, "ffn_up", (n // FF_UP_TILE, t // tm),
                  [pl.BlockSpec((tm, D_MODEL), lambda j, i: (i, 0)),
                   pl.BlockSpec((FF_UP_TILE, D_MODEL), lambda j, i: (j, 0))],
                  pl.BlockSpec((tm, FF_UP_TILE), lambda j, i: (i, j)),
                  _sds((t, n), BF16))(u2, w_up)


HALO = 16


def _shift_matrix(tm):
    r = lax.broadcasted_iota(jnp.int32, (tm, tm), 0)
    c = lax.broadcasted_iota(jnp.int32, (tm, tm), 1)
    return jnp.where(r == c + 1, 1.0, 0.0).astype(BF16)


def _conv_cols(h_ref, halo_ref, valid, wc_ref, bc_ref, c0):
    cs = slice(c0, c0 + FF_COLS)
    cur = h_ref[:, cs].astype(F32)
    prev = jnp.where(valid, halo_ref[:, cs].astype(F32), 0.0)
    full = jnp.concatenate([prev, cur], axis=0)
    h1 = pltpu.roll(full, 1, axis=0)[HALO:]
    h2 = pltpu.roll(full, 2, axis=0)[HALO:]
    return h2 * wc_ref[0:1, cs] + h1 * wc_ref[1:2, cs] + cur * wc_ref[2:3, cs] + bc_ref[:, cs]


def _ffn_down_loss(h, x1, tgt, w_conv, b_conv, w_down, g_final, seq, tm):
    t = h.shape[0]
    tps = seq // tm
    n = 2 * D_FF

    def body(h_ref, halo_ref, x1_ref, tgt_ref, wc_ref, bc_ref, wd_ref, gf_ref,
             hc_ref, a_ref, dx2_ref, dx2b_ref, loss_ref, dgf_ref):
        i = pl.program_id(0)

        @pl.when(i == 0)
        def _():
            loss_ref[...] = jnp.zeros_like(loss_ref)
            dgf_ref[...] = jnp.zeros_like(dgf_ref)

        valid = (i % tps) != 0
        x2 = x1_ref[...]
        for j in range(D_FF // FF_COLS):
            gate = _conv_cols(h_ref, halo_ref, valid, wc_ref, bc_ref, j * FF_COLS)
            val = _conv_cols(h_ref, halo_ref, valid, wc_ref, bc_ref, D_FF + j * FF_COLS)
            hc_ref[:, j * FF_COLS:(j + 1) * FF_COLS] = gate.astype(BF16)
            hc_ref[:, D_FF + j * FF_COLS:D_FF + (j + 1) * FF_COLS] = val.astype(BF16)
            a = (gate * _sigmoid(gate) * val).astype(BF16)
            a_ref[:, j * FF_COLS:(j + 1) * FF_COLS] = a
            x2 = x2 + jnp.dot(a, wd_ref[j * FF_COLS:(j + 1) * FF_COLS, :], preferred_element_type=F32)
        r = lax.rsqrt(jnp.mean(x2 * x2, axis=-1, keepdims=True) + EPS)
        xn = x2 * r
        g = gf_ref[...]
        e = xn * g - tgt_ref[...]
        loss_ref[...] += (0.5 / D_MODEL) * jnp.sum(e * e).reshape(1, 1)
        dy = e * (1.0 / D_MODEL)
        dgf_ref[...] += jnp.sum(dy * xn, axis=0, keepdims=True)
        dxn = dy * g
        dx2 = r * (dxn - xn * jnp.mean(dxn * xn, axis=-1, keepdims=True))
        dx2_ref[...] = dx2
        dx2b_ref[...] = dx2.astype(BF16)

    row = lambda w: pl.BlockSpec((tm, w), lambda i: (i, 0))
    halo = pl.BlockSpec((HALO, n), lambda i: (jnp.maximum(i * (tm // HALO) - 1, 0), 0))
    return _pcall(body, "ffn_down_loss", (t // tm,),
                  [row(n), halo, row(D_MODEL), row(D_MODEL), _full((CONV_W, n)), _full((1, n)),
                   _full((D_FF, D_MODEL)), _full((1, D_MODEL))],
                  [row(n), row(D_FF), row(D_MODEL), row(D_MODEL), _full((1, 1)), _full((1, D_MODEL))],
                  [_sds((t, n), BF16), _sds((t, D_FF), BF16), _sds((t, D_MODEL)), _sds((t, D_MODEL), BF16),
                   _sds((1, 1)), _sds((1, D_MODEL))],
                  )(h, h, x1, tgt, w_conv, b_conv, w_down, g_final)


def _wgrad(a, b, name, tn, out_dtype=F32, band=None, after=None):
    t, m = a.shape
    n = b.shape[1] if band is None else band
    nbands = 1 if band is None else b.shape[1] // band
    after = b if after is None else after

    def body(a_ref, b_ref, after_ref, o_ref):
        o_ref[...] = _dot_tn(a_ref[...], b_ref[...]).astype(out_dtype)

    return _pcall(body, name, (m // tn,),
                  [pl.BlockSpec((t, tn), lambda i: (0, i)), pl.BlockSpec((t, n), lambda i: (0, i % nbands)),
                   pl.BlockSpec(memory_space=pl.ANY)],
                  pl.BlockSpec((tn, n), lambda i: (i, 0)), _sds((m, n), out_dtype))(a, b, after)


def _ffn_bwd_act(dx2b, hc, w_down, tm):
    t = hc.shape[0]
    n = 2 * D_FF

    def body(dx2_ref, hc_ref, wd_ref, dhc_ref, dbc_ref):
        @pl.when(pl.program_id(0) == 0)
        def _():
            dbc_ref[...] = jnp.zeros_like(dbc_ref)

        dx2 = dx2_ref[...]
        for j in range(D_FF // FF_COLS):
            gs = slice(j * FF_COLS, (j + 1) * FF_COLS)
            vs = slice(D_FF + j * FF_COLS, D_FF + (j + 1) * FF_COLS)
            gate = hc_ref[:, gs].astype(F32)
            val = hc_ref[:, vs].astype(F32)
            da = _dot_nt(dx2, wd_ref[gs, :])
            sg = _sigmoid(gate)
            dgate = da * val * (sg * (1.0 + gate * (1.0 - sg)))
            dval = da * (gate * sg)
            dhc_ref[:, gs] = dgate.astype(BF16)
            dhc_ref[:, vs] = dval.astype(BF16)
            dbc_ref[:, gs] += jnp.sum(dgate, axis=0, keepdims=True)
            dbc_ref[:, vs] += jnp.sum(dval, axis=0, keepdims=True)

    row = lambda w: pl.BlockSpec((tm, w), lambda i: (i, 0))
    return _pcall(body, "ffn_bwd_act", (t // tm,),
                  [row(D_MODEL), row(n), _full((D_FF, D_MODEL))],
                  [row(n), _full((1, n))],
                  [_sds((t, n), BF16), _sds((1, n))],
                  )(dx2b, hc, w_down)


def _ffn_bwd_up(dhc, h, dx2, x1, w_conv, w_up, g_ffn, seq, tm):
    t = dhc.shape[0]
    tps = seq // tm
    n = 2 * D_FF
    last = t // HALO - 1

    def body(dhc_ref, halo_ref, h_ref, dx2_ref, x1_ref, wc_ref, wu_ref, gf_ref,
             dh_ref, dx1_ref, dx1b_ref, dgf_ref, dwc_ref):
        i = pl.program_id(0)

        @pl.when(i == 0)
        def _():
            dgf_ref[...] = jnp.zeros_like(dgf_ref)
            dwc_ref[...] = jnp.zeros_like(dwc_ref)

        valid = ((i + 1) % tps) != 0
        du2 = jnp.zeros((tm, D_MODEL), F32)
        for j in range(n // FF_COLS):
            cs = slice(j * FF_COLS, (j + 1) * FF_COLS)
            cur = dhc_ref[:, cs].astype(F32)
            nxt = jnp.where(valid, halo_ref[:, cs].astype(F32), 0.0)
            full = jnp.concatenate([cur, nxt], axis=0)
            d1 = pltpu.roll(full, tm + HALO - 1, axis=0)[:tm]
            d2 = pltpu.roll(full, tm + HALO - 2, axis=0)[:tm]
            dh = (cur * wc_ref[2:3, cs] + d1 * wc_ref[1:2, cs] + d2 * wc_ref[0:1, cs]).astype(BF16)
            dh_ref[:, cs] = dh
            du2 = du2 + _dot(dh, wu_ref[cs, :])
            hv = h_ref[:, cs].astype(F32)
            dwc_ref[0:1, cs] += jnp.sum(hv * d2, axis=0, keepdims=True)
            dwc_ref[1:2, cs] += jnp.sum(hv * d1, axis=0, keepdims=True)
            dwc_ref[2:3, cs] += jnp.sum(hv * cur, axis=0, keepdims=True)
        x1 = x1_ref[...]
        r = lax.rsqrt(jnp.mean(x1 * x1, axis=-1, keepdims=True) + EPS)
        xn = x1 * r
        dgf_ref[...] += jnp.sum(du2 * xn, axis=0, keepdims=True)
        dxn = du2 * gf_ref[...]
        dx1 = dx2_ref[...] + r * (dxn - xn * jnp.mean(dxn * xn, axis=-1, keepdims=True))
        dx1_ref[...] = dx1
        dx1b_ref[...] = dx1.astype(BF16)

    row = lambda w: pl.BlockSpec((tm, w), lambda i: (i, 0))
    halo = pl.BlockSpec((HALO, n), lambda i: (jnp.minimum((i + 1) * (tm // HALO), last), 0))
    return _pcall(body, "ffn_bwd_up", (t // tm,),
                  [row(n), halo, row(n), row(D_MODEL), row(D_MODEL), _full((CONV_W, n)), _full((n, D_MODEL)),
                   _full((1, D_MODEL))],
                  [row(n), row(D_MODEL), row(D_MODEL), _full((1, D_MODEL)), _full((CONV_W, n))],
                  [_sds((t, n), BF16), _sds((t, D_MODEL)), _sds((t, D_MODEL), BF16), _sds((1, D_MODEL)),
                   _sds((CONV_W, n))],
                  )(dhc, dhc, h, dx2, x1, w_conv, w_up, g_ffn)


def _mix_bwd(dx1, y0, o, zh, zgt, pa, pb, w_glu, b_glu, gain, w_pa, w_pb, w_out, tm):
    t = dx1.shape[0]

    def body(dx1_ref, y0_ref, o_ref, zg_ref, zgt_ref, pa_ref, pb_ref, wglu_ref, bglu_ref, gain_ref, wpa_ref,
             wpb_ref, wout_ref,
             dy0_ref, do_ref, dzg_ref, dzgt_ref, m_ref, dpa_ref, dpb_ref, ya1_ref, dpre_ref, dbglu_ref, dgain_ref):
        @pl.when(pl.program_id(0) == 0)
        def _():
            dbglu_ref[...] = jnp.zeros_like(dbglu_ref)
            dgain_ref[...] = jnp.zeros_like(dgain_ref)

        dm = _dot_nt(dx1_ref[...], wout_ref[...])
        sga = _sigmoid(zgt_ref[:, 0:D_MODEL].astype(F32))
        sgb = _sigmoid(zgt_ref[:, D_MODEL:].astype(F32))
        pa = pa_ref[...].astype(F32)
        pb = pb_ref[...].astype(F32)
        m_ref[...] = (sga * pa + sgb * pb).astype(BF16)
        dzgt_ref[:, 0:D_MODEL] = (dm * pa * sga * (1.0 - sga)).astype(BF16)
        dzgt_ref[:, D_MODEL:] = (dm * pb * sgb * (1.0 - sgb)).astype(BF16)
        dpa = (dm * sga).astype(BF16)
        dpb = (dm * sgb).astype(BF16)
        dpa_ref[...] = dpa
        dpb_ref[...] = dpb
        dya2 = _dot_nt(dpa, wpa_ref[...])
        dyb = _dot_nt(dpb, wpb_ref[...])
        y0 = y0_ref[...]
        ya1 = _gelu(y0)
        ya1_ref[...] = ya1.astype(BF16)
        s = _sigmoid(_dot(ya1, wglu_ref[...]) + bglu_ref[...])
        dpre = dya2 * ya1 * s * (1.0 - s)
        dpre_ref[...] = dpre.astype(BF16)
        dbglu_ref[...] += jnp.sum(dpre, axis=0, keepdims=True)
        dya1 = dya2 * s + _dot_nt(dpre, wglu_ref[...])
        dy0_ref[...] = dya1 * _gelu_grad(y0)
        ov = o_ref[...]
        zg = zg_ref[...]
        oh = ov * _head_rms(ov)
        on = oh * gain_ref[...]
        sz = _sigmoid(zg)
        dzg_ref[...] = (dyb * on * (sz * (1.0 + zg * (1.0 - sz)))).astype(BF16)
        don = dyb * (zg * sz)
        dgain_ref[...] += jnp.sum(don * oh, axis=0, keepdims=True)
        doh = don * gain_ref[...]
        do_ref[...] = _head_rms(ov) * (doh - oh * _head_mean(doh * oh))

    row = lambda w: pl.BlockSpec((tm, w), lambda i: (i, 0))
    return _pcall(body, "mix_bwd", (t // tm,),
                  [row(D_MODEL), row(S5_WIDTH), row(HG_WIDTH), pl.BlockSpec((tm, HG_WIDTH), lambda i: (i, 3)),
                   row(2 * D_MODEL), row(D_MODEL), row(D_MODEL), _full((S5_WIDTH, S5_WIDTH)), _full((1, S5_WIDTH)),
                   _full((1, HG_WIDTH)), _full((S5_WIDTH, D_MODEL)), _full((HG_WIDTH, D_MODEL)),
                   _full((D_MODEL, D_MODEL))],
                  [row(S5_WIDTH), row(HG_WIDTH), row(HG_WIDTH), row(2 * D_MODEL), row(D_MODEL), row(D_MODEL),
                   row(D_MODEL), row(S5_WIDTH), row(S5_WIDTH), _full((1, S5_WIDTH)), _full((1, HG_WIDTH))],
                  [_sds((t, S5_WIDTH)), _sds((t, HG_WIDTH)), _sds((t, HG_WIDTH), BF16), _sds((t, 2 * D_MODEL), BF16),
                   _sds((t, D_MODEL), BF16), _sds((t, D_MODEL), BF16), _sds((t, D_MODEL), BF16),
                   _sds((t, S5_WIDTH), BF16), _sds((t, S5_WIDTH), BF16), _sds((1, S5_WIDTH)), _sds((1, HG_WIDTH))],
                  )(dx1, y0, o, zh, zgt, pa, pb, w_glu, b_glu, gain, w_pa, w_pb, w_out)


def _s5_bwd(dy0, za, xs, c_bands, b_bands, lam, dskip, nb, seq, ts):
    nts = seq // ts

    def body(dy0_ref, za_ref, xs_ref, halo_ref, cr_ref, ci_ref, br_ref, bi_ref, lam_ref, d_ref,
             dza_ref, a_ref, dlam_ref, dd_ref, dc_ref, acc_ref, st_ref):
        j = pl.program_id(0)

        @pl.when(j == 0)
        def _():
            dlam_ref[...] = jnp.zeros_like(dlam_ref)
            dd_ref[...] = jnp.zeros_like(dd_ref)
            dc_ref[...] = jnp.zeros_like(dc_ref)
            st_ref[...] = jnp.zeros_like(st_ref)

        for b in range(nb):
            dy0 = dy0_ref[b]
            for q in range(S5_BANDS):
                ch, st = _band(q)
                acc_ref[b, :, st] = _dot(dy0[:, ch], cr_ref[q])
                acc_ref[b, :, _im(st)] = _dot(dy0[:, ch], ci_ref[q])
                dc_ref[st, :] += _dot_tn(xs_ref[b, :, st], dy0[:, ch])
                dc_ref[_im(st), :] += _dot_tn(xs_ref[b, :, _im(st)], dy0[:, ch])
        _complex_scan(acc_ref, lam_ref, st_ref, nb, ts, reverse=True)
        shift = _shift_matrix(ts)
        top = lax.broadcasted_iota(jnp.int32, (SUBLANES, S5_LANES), 0) == 0
        for b in range(nb):
            a_ref[b] = acc_ref[b].astype(BF16)
            first = jnp.where(j == nts - 1, 0.0, halo_ref[b, HALO - 1:HALO, :].astype(F32))

            def shifted(cols):
                xp = jnp.dot(shift, xs_ref[b, :, cols], preferred_element_type=F32)
                return jnp.concatenate([xp[:SUBLANES] + jnp.where(top, first[:, cols], 0.0), xp[SUBLANES:]], axis=0)

            for cc in range(S5_N // S5_LANES):
                re = slice(cc * S5_LANES, (cc + 1) * S5_LANES)
                ar, ai, xr, xi = acc_ref[b, :, re], acc_ref[b, :, _im(re)], shifted(re), shifted(_im(re))
                dlam_ref[0:1, re] += jnp.sum(ar * xr + ai * xi, axis=0, keepdims=True)
                dlam_ref[1:2, re] += jnp.sum(ai * xr - ar * xi, axis=0, keepdims=True)
            dy0 = dy0_ref[b]
            for q in range(S5_BANDS):
                ch, st = _band(q)
                dza_ref[b, :, ch] = (_dot(a_ref[b, :, st], br_ref[q]) + _dot(a_ref[b, :, _im(st)], bi_ref[q])
                                     + d_ref[:, ch] * dy0[:, ch]).astype(BF16)
            dd_ref[...] += jnp.sum(dy0 * za_ref[b], axis=0, keepdims=True)

    tile = lambda j: nts - 1 - j
    tok = lambda w: pl.BlockSpec((nb, ts, w), lambda j: (0, tile(j), 0))
    halo = pl.BlockSpec((nb, HALO, 2 * S5_N), lambda j: (0, jnp.maximum(tile(j) * (ts // HALO) - 1, 0), 0))
    to_st, to_ch = _full((S5_BANDS, BAND_CH, BAND_ST)), _full((S5_BANDS, BAND_ST, BAND_CH))
    return _pcall(body, "s5_bwd", (nts,),
                  [tok(S5_WIDTH), tok(S5_WIDTH), tok(2 * S5_N), halo, to_st, to_st, to_ch, to_ch,
                   _full((2, S5_N)), _full((1, S5_WIDTH))],
                  [tok(S5_WIDTH), tok(2 * S5_N), _full((2, S5_N)), _full((1, S5_WIDTH)), _full((2 * S5_N, BAND_CH))],
                  [_sds((nb, seq, S5_WIDTH), BF16), _sds((nb, seq, 2 * S5_N), BF16), _sds((2, S5_N)),
                   _sds((1, S5_WIDTH)), _sds((2 * S5_N, BAND_CH))],
                  scratch=[pltpu.VMEM((nb, ts, 2 * S5_N), F32), pltpu.VMEM((nb, 2, S5_N), F32)],
                  )(dy0, za, xs, xs, *c_bands, *b_bands, lam, dskip)


def _hgrn_bwd(zh, do, sts, lb, nb, seq):
    nc = seq // CHUNK
    cps = HG_CHUNKS_PER_STEP

    def body(zh_ref, do_ref, sts_ref, lb_ref, dz_ref, dlb_ref, dst_ref):
        @pl.when(pl.program_id(0) == 0)
        def _():
            dst_ref[...] = jnp.zeros_like(dst_ref)
            dlb_ref[...] = jnp.zeros_like(dlb_ref)

        row = lax.broadcasted_iota(jnp.int32, (CHUNK, CHUNK), 0)
        causal = row >= lax.broadcasted_iota(jnp.int32, (CHUNK, CHUNK), 1)
        last_row = lax.broadcasted_iota(jnp.int32, (CHUNK, HG_HEAD), 0) == CHUNK - 1
        for cc in reversed(range(cps)):
            rows = slice(cc * CHUNK, (cc + 1) * CHUNK)
            for b in range(nb):
                for h in range(HG_HEADS):
                    hs = slice(h * HG_HEAD, (h + 1) * HG_HEAD)
                    zq = zh_ref[b, rows, h * HG_HEAD:(h + 1) * HG_HEAD]
                    zf = zh_ref[b, rows, HG_WIDTH + h * HG_HEAD:HG_WIDTH + (h + 1) * HG_HEAD]
                    zi = zh_ref[b, rows, 2 * HG_WIDTH + h * HG_HEAD:2 * HG_WIDTH + (h + 1) * HG_HEAD]
                    lbh = lb_ref[:, hs]
                    sf, f, sq, qa, bc, bm, bl = _hgrn_gates(zq, zf, lbh)
                    k = 1.0 - f
                    e_qt = jnp.exp(bc - bm)
                    e_kt = jnp.exp(bm - bc)
                    e_b = jnp.exp(bc)
                    e_kd = jnp.exp(bl - bc)
                    e_l = jnp.exp(bl)
                    qt, kt, qb, kd = qa * e_qt, k * e_kt, qa * e_b, k * e_kd
                    a = jnp.where(causal, _dot_nt(qt, kt), 0.0)
                    st = sts_ref[b, cc, h]
                    dst = dst_ref[b, h]
                    dov = do_ref[b, rows, hs]
                    da = jnp.where(causal, _dot_nt(dov, zi), 0.0)
                    qt_r, kt_r = qt.astype(BF16).astype(F32), kt.astype(BF16).astype(F32)
                    dqt = _dot(da, kt)
                    dkt = _dot_tn(da, qt)
                    dqb = _dot(dov, st)
                    di = _dot_tn(a, dov) + _dot_nt(kd, dst)
                    dkd = _dot(zi, dst)
                    de_l = jnp.sum(dst * st, axis=0, keepdims=True)
                    dst_ref[b, h] = dst * e_l + _dot_tn(dov, qb)
                    dqa = dqt * e_qt + dqb * e_b
                    dk = dkt * e_kt + dkd * e_kd
                    dbl = jnp.sum(dkd * kd, axis=0, keepdims=True) + de_l * e_l
                    db = dqt * qt_r - dkt * kt_r + dqb * qb - dkd * kd + jnp.where(last_row, dbl, 0.0)
                    df = _cumsum_rows(db, reverse=True) / f - dk
                    dzq = dqa * QSCALE * (sq * (1.0 + zq * (1.0 - sq)))
                    dzf = df * (1.0 - lbh) * sf * (1.0 - sf)
                    dz_ref[b, rows, h * HG_HEAD:(h + 1) * HG_HEAD] = dzq.astype(BF16)
                    dz_ref[b, rows, HG_WIDTH + h * HG_HEAD:HG_WIDTH + (h + 1) * HG_HEAD] = dzf.astype(BF16)
                    dz_ref[b, rows, 2 * HG_WIDTH + h * HG_HEAD:2 * HG_WIDTH + (h + 1) * HG_HEAD] = di.astype(BF16)
                    dlb_ref[:, hs] += jnp.sum(df * (1.0 - sf), axis=0, keepdims=True)

    rev = lambda c: nc // cps - 1 - c
    return _pcall(body, "hgrn_bwd", (nc // cps,),
                  [pl.BlockSpec((nb, cps * CHUNK, 4 * HG_WIDTH), lambda c: (0, rev(c), 0)),
                   pl.BlockSpec((nb, cps * CHUNK, HG_WIDTH), lambda c: (0, rev(c), 0)),
                   pl.BlockSpec((nb, cps, HG_HEADS, HG_HEAD, HG_HEAD), lambda c: (0, rev(c), 0, 0, 0)),
                   _full((1, HG_WIDTH))],
                  [pl.BlockSpec((nb, cps * CHUNK, 3 * HG_WIDTH), lambda c: (0, rev(c), 0)), _full((1, HG_WIDTH))],
                  [_sds((nb, seq, 3 * HG_WIDTH), BF16), _sds((1, HG_WIDTH))],
                  scratch=[pltpu.VMEM((nb, HG_HEADS, HG_HEAD, HG_HEAD), F32)])(zh, do, sts, lb)


def _in_proj_bwd(dza, dzh, dzg, dzgt, dx1, x, g_mix, w_in, tm):
    t = x.shape[0]

    def body(dza_ref, dzh_ref, dzg_ref, dzgt_ref, dx1_ref, x_ref, g_ref, w_ref, dz_ref, dx_ref, dg_ref):
        @pl.when(pl.program_id(0) == 0)
        def _():
            dg_ref[...] = jnp.zeros_like(dg_ref)

        c1, c2, c3 = S5_WIDTH, S5_WIDTH + 3 * HG_WIDTH, S5_WIDTH + 4 * HG_WIDTH
        dz_ref[:, 0:c1] = dza_ref[...]
        dz_ref[:, c1:c2] = dzh_ref[...]
        dz_ref[:, c2:c3] = dzg_ref[...]
        dz_ref[:, c3:] = dzgt_ref[...]
        du = _dot(dz_ref[...], w_ref[...])
        xv = x_ref[...]
        r = lax.rsqrt(jnp.mean(xv * xv, axis=-1, keepdims=True) + EPS)
        xn = xv * r
        dg_ref[...] += jnp.sum(du * xn, axis=0, keepdims=True)
        dxn = du * g_ref[...]
        dx_ref[...] = dx1_ref[...] + r * (dxn - xn * jnp.mean(dxn * xn, axis=-1, keepdims=True))

    row = lambda w: pl.BlockSpec((tm, w), lambda i: (i, 0))
    return _pcall(body, "in_proj_bwd", (t // tm,),
                  [row(S5_WIDTH), row(3 * HG_WIDTH), row(HG_WIDTH), row(2 * D_MODEL), row(D_MODEL), row(D_MODEL),
                   _full((1, D_MODEL)), _full((N_IN, D_MODEL))],
                  [row(N_IN), row(D_MODEL), _full((1, D_MODEL))],
                  [_sds((t, N_IN), BF16), _sds((t, D_MODEL)), _sds((1, D_MODEL))],
                  )(dza, dzh, dzg, dzgt, dx1, x, g_mix, w_in)


def _after(value, token):
    return value + token[0, 0]


def _local_step(x3, tgt3, weights, sp, emit, emit_small):
    nb, seq, _ = x3.shape
    t = nb * seq
    tm = _token_tile(seq)
    x = x3.reshape(t, D_MODEL)
    tgt = tgt3.reshape(t, D_MODEL)
    row = lambda v: v.reshape(1, -1)

    a_re, a_im, b_re, b_im = sp["s5_a_re"], sp["s5_a_im"], sp["s5_b_re"], sp["s5_b_im"]
    ldt = sp["s5_log_dt"].reshape(S5_GROUPS, 1)
    lr, li, bb_re, bb_im, lb = _params_fwd(a_re, a_im, ldt, b_re, b_im, sp["hg_lb_logits"])
    lam = jnp.concatenate([lr.reshape(1, S5_N), li.reshape(1, S5_N)], axis=0)
    swap = lambda m: m.transpose(0, 2, 1)
    b_to_st = (_band_blocks(bb_re), _band_blocks(bb_im))
    b_to_ch = (_band_blocks(swap(bb_re)), _band_blocks(swap(bb_im)))
    c_to_ch = (_band_blocks(swap(sp["s5_c_re"])), _band_blocks(swap(-sp["s5_c_im"])))
    c_to_st = (_band_blocks(sp["s5_c_re"]), _band_blocks(-sp["s5_c_im"]))

    g_mix, g_ffn, g_final = row(sp["g_mix"]), row(sp["g_ffn"]), row(sp["g_final"])
    b_glu, gain, dskip, b_conv = row(sp["b_glu"]), row(sp["hg_norm_gain"]), row(sp["s5_d"]), row(sp["b_conv"])

    w_in = weights("in", lam, *b_to_st, *b_to_ch, *c_to_ch, *c_to_st)["w_in"]
    wide = min(2 * tm, seq)
    u, za, zh, zgt = _in_proj(x, g_mix, w_in, wide)
    seqs = lambda v: v.reshape(nb, seq, v.shape[-1])
    toks = lambda v: v.reshape(t, v.shape[-1])
    xs3, y0 = _s5_fwd(seqs(za), b_to_st, lam, c_to_ch, dskip, nb, seq, tm)
    xs, y0 = toks(xs3), toks(y0)
    o3, sts = _hgrn_fwd(zh.reshape(nb, seq, 4 * HG_WIDTH), lb, nb, seq)
    o = o3.reshape(t, HG_WIDTH)
    wm = weights("mix", y0, o3)
    weights.forward("ffn", wm["w_out"])
    x1, u2, pa, pb, ya2, yb = _mix_fwd(x, y0, o, zh, zgt, wm["w_glu"], b_glu, gain, wm["w_pa"], wm["w_pb"],
                                       wm["w_out"], g_ffn, wide)
    wf = weights("ffn", u2)
    h = _ffn_up(u2, wf["w_up"], min(4 * tm, t))
    hc, a, dx2, dx2b, loss, dg_final = _ffn_down_loss(h, x1, tgt, wf["w_conv"], b_conv, wf["w_down"], g_final,
                                                      seq, tm)

    def wgrad(a, b, name):
        return _wgrad(a, b, name, 512 if a.shape[1] % 512 == 0 else 256, out_dtype=BF16)

    dhc, db_conv = _ffn_bwd_act(dx2b, hc, wf["w_down"], tm)
    dw_down = wgrad(a, dx2b, "dw_down")
    dh, dx1, dx1b, dg_ffn, dw_conv = _ffn_bwd_up(dhc, h, dx2, x1, wf["w_conv"], wf["w_up"], g_ffn, seq, tm)
    sent = emit({"w_up": wgrad(dh, u2, "dw_up"), "w_conv": dw_conv, "w_down": dw_down})
    (dy0, do, dzg, dzgt, m, dpa, dpb, ya1, dpre, db_glu, dgain) = _mix_bwd(
        dx1b, y0, o, zh, zgt, pa, pb, wm["w_glu"], _after(b_glu, sent), gain, wm["w_pa"], wm["w_pb"], wm["w_out"],
        wide)
    dzh3, dlb = _hgrn_bwd(zh.reshape(nb, seq, 4 * HG_WIDTH), do.reshape(nb, seq, HG_WIDTH), sts, lb, nb, seq)
    dza, a_s5, dlam, dd, dc_band = _s5_bwd(seqs(dy0), seqs(za), xs3, c_to_st, b_to_ch, lam, dskip, nb, seq, tm)
    dza, a_s5 = toks(dza), toks(a_s5)
    dz, dx, dg_mix = _in_proj_bwd(dza, dzh3.reshape(t, 3 * HG_WIDTH), dzg, dzgt, dx1, x, g_mix, w_in, wide)
    sent = emit({"w_in": wgrad(dz, u, "dw_in")})

    def wgrad_after(a, b, name):
        return _wgrad(a, b, name, 512 if a.shape[1] % 512 == 0 else 256, out_dtype=BF16, after=sent)

    sent = emit({"w_out": wgrad_after(m, dx1b, "dw_out"), "w_pa": wgrad_after(ya2, dpa, "dw_pa"),
                 "w_pb": wgrad_after(yb, dpb, "dw_pb"), "w_glu": wgrad_after(ya1, dpre, "dw_glu")})

    band = HG_HEAD
    dbb_band = _wgrad(a_s5, za, "dbb_s5", 512, band=band, after=sent)
    dbb_re = swap(_diag_blocks(dbb_band[:S5_N], S5_STATE, S5_GROUP))
    dbb_im = swap(_diag_blocks(dbb_band[S5_N:], S5_STATE, S5_GROUP))
    dc_re = swap(_diag_blocks(dc_band[:S5_N], S5_STATE, S5_GROUP))
    dc_im = -swap(_diag_blocks(dc_band[S5_N:], S5_STATE, S5_GROUP))
    da_re, da_im, dldt, db_re, db_im, dlogits = _params_bwd(
        a_re, a_im, ldt, b_re, b_im, sp["hg_lb_logits"],
        dlam[0].reshape(S5_GROUPS, S5_STATE), dlam[1].reshape(S5_GROUPS, S5_STATE), dbb_re, dbb_im, dlb)
    emit_small({"g_mix": dg_mix, "s5_a_re": da_re, "s5_a_im": da_im, "s5_log_dt": dldt.reshape(1, S5_GROUPS),
                "s5_b_re": db_re, "s5_b_im": db_im, "s5_c_re": dc_re, "s5_c_im": dc_im, "s5_d": dd, "b_glu": db_glu,
                "hg_lb_logits": dlogits, "hg_norm_gain": dgain, "g_ffn": dg_ffn, "b_conv": db_conv,
                "g_final": dg_final, "loss": loss})
    return dx.reshape(nb, seq, D_MODEL)


def _mesh_peers():
    x, y, c = lax.axis_index("x"), lax.axis_index("y"), lax.axis_index("c")
    peers = []
    for k in range(1, N_DEV):
        px, py, pc = (1 - x if k & 4 else x), (1 - y if k & 2 else y), (1 - c if k & 1 else c)
        peers.append((k, (px, py, pc), 4 * px + 2 * py + pc))
    return 4 * x + 2 * y + c, peers


_HBM = pl.BlockSpec(memory_space=pltpu.HBM)
_SEM = pl.BlockSpec(memory_space=pltpu.SEMAPHORE)


_EFFECT = pltpu.CompilerParams(has_side_effects=pltpu.SideEffectType.DATAFLOW_SIDE_EFFECTING)


def _remote(src, dst, send_sem, recv_sem, to):
    return pltpu.make_async_remote_copy(src_ref=src, dst_ref=dst, send_sem=send_sem, recv_sem=recv_sem,
                                        device_id=to, device_id_type=pl.DeviceIdType.MESH)


def _exchange_start(name, arrays, after):
    n = len(arrays)
    srcs = [pltpu.with_memory_space_constraint(a, pltpu.HBM) for a in arrays]
    lands = [pltpu.with_memory_space_constraint(lax.empty(a.shape, a.dtype), pltpu.HBM) for a in arrays]
    copies = (N_DEV - 1) * n

    def body(*refs):
        src_refs, land_refs = refs[:n], refs[n:2 * n]
        send_sems, recv_sems, token = refs[2 * n + 1], refs[2 * n + 2], refs[-1]
        my_slab, peers = _mesh_peers()
        for k, peer, slab in peers:
            for i in range(n):
                s = (k - 1) * n + i
                _remote(src_refs[i].at[slab], land_refs[i].at[my_slab], send_sems.at[s], recv_sems.at[s], peer).start()
        token[...] = jnp.zeros_like(token)

    outs = pl.pallas_call(
        body, name=name,
        out_shape=(pltpu.SemaphoreType.DMA((copies,)), pltpu.SemaphoreType.DMA((copies,)),
                   *[pltpu.HBM(a.shape, a.dtype) for a in lands], _sds((SUBLANES, LANES))),
        in_specs=[_HBM] * (2 * n) + [pl.BlockSpec(memory_space=pl.ANY)],
        out_specs=(_SEM, _SEM, *[_HBM] * n, pl.BlockSpec(memory_space=pltpu.VMEM)),
        input_output_aliases={n + i: 2 + i for i in range(n)}, compiler_params=_EFFECT,
    )(*srcs, *lands, after)
    return (outs[0], outs[1], srcs, outs[2:2 + n]), outs[-1]


def _exchange_wait(name, state, *after):
    send_sems, recv_sems, srcs, lands = state
    n = len(lands)

    def body(*refs):
        src_refs, land_refs = refs[:n], refs[n:2 * n]
        send_ref, recv_ref = refs[2 * n], refs[2 * n + 1]
        _, peers = _mesh_peers()
        for k, peer, slab in peers:
            for i in range(n):
                s = (k - 1) * n + i
                copy = _remote(src_refs[i].at[slab], land_refs[i].at[slab], send_ref.at[s], recv_ref.at[s], peer)
                copy.wait_send()
                copy.wait_recv()

    outs = pl.pallas_call(
        body, name=name,
        out_shape=tuple(pltpu.HBM(a.shape, a.dtype) for a in lands),
        in_specs=[_HBM] * (2 * n) + [_SEM, _SEM] + [pl.BlockSpec(memory_space=pl.ANY)] * len(after),
        out_specs=tuple([_HBM] * n),
        input_output_aliases={n + i: i for i in range(n)}, compiler_params=_EFFECT,
    )(*srcs, *lands, send_sems, recv_sems, *after)
    return list(outs), list(srcs)


def _slab(pos):
    return 4 * pos[0] + 2 * pos[1] + pos[2]


def _chip_routes():
    x, y, c = lax.axis_index("x"), lax.axis_index("y"), lax.axis_index("c")
    return (x, y, c), (x, y, 1 - c), [(1 - x, y, c), (x, 1 - y, c), (1 - x, 1 - y, c)]


def _gather_start(name, arrays, after):
    n = len(arrays)
    me = 4 * lax.axis_index("x") + 2 * lax.axis_index("y") + lax.axis_index("c")
    srcs = [pltpu.with_memory_space_constraint(a, pltpu.HBM) for a in arrays]
    lands = [pltpu.with_memory_space_constraint(
        lax.dynamic_update_slice_in_dim(lax.empty((N_DEV,) + a.shape, a.dtype), a[None], me, 0), pltpu.HBM)
        for a in arrays]

    def body(*refs):
        src_refs, land_refs = refs[:n], refs[n:2 * n]
        send_sems, recv_sems, token = refs[2 * n + 1], refs[2 * n + 2], refs[-1]
        mine, sibling, chips = _chip_routes()
        for k, to in enumerate([sibling] + chips):
            for i in range(n):
                _remote(src_refs[i], land_refs[i].at[_slab(mine)], send_sems.at[k * n + i], recv_sems.at[k * n + i],
                        to).start()
        token[...] = jnp.zeros_like(token)

    outs = pl.pallas_call(
        body, name=name,
        out_shape=(pltpu.SemaphoreType.DMA((4 * n,)), pltpu.SemaphoreType.DMA((4 * n,)),
                   *[pltpu.HBM(a.shape, a.dtype) for a in lands], _sds((SUBLANES, LANES))),
        in_specs=[_HBM] * (2 * n) + [pl.BlockSpec(memory_space=pl.ANY)],
        out_specs=(_SEM, _SEM, *[_HBM] * n, pl.BlockSpec(memory_space=pltpu.VMEM)),
        input_output_aliases={n + i: 2 + i for i in range(n)}, compiler_params=_EFFECT,
    )(*srcs, *lands, after)
    return (outs[0], outs[1], srcs, outs[2:2 + n]), outs[-1]


def _gather_forward(name, state, *after):
    send_a, recv_a, srcs, lands = state
    n = len(lands)

    def body(*refs):
        land_refs, recv_a_ref = refs[:n], refs[n]
        send_b, recv_b = refs[n + 1 + len(after)], refs[n + 2 + len(after)]
        mine, sibling, chips = _chip_routes()
        for j, chip in enumerate(chips):
            for i in range(n):
                block = land_refs[i].at[_slab(chip)]
                _remote(block, block, send_b.at[j * n + i], recv_a_ref.at[(1 + j) * n + i], chip).wait_recv()
                _remote(block, block, send_b.at[j * n + i], recv_b.at[j * n + i], sibling).start()

    outs = pl.pallas_call(
        body, name=name,
        out_shape=(pltpu.SemaphoreType.DMA((3 * n,)), pltpu.SemaphoreType.DMA((3 * n,)),
                   *[pltpu.HBM(a.shape, a.dtype) for a in lands]),
        in_specs=[_HBM] * n + [_SEM] + [pl.BlockSpec(memory_space=pl.ANY)] * len(after),
        out_specs=(_SEM, _SEM, *[_HBM] * n),
        input_output_aliases={i: 2 + i for i in range(n)}, compiler_params=_EFFECT,
    )(*lands, recv_a, *after)
    return (send_a, recv_a, srcs, list(outs[2:])), (outs[0], outs[1])


def _gather_wait(name, state, forwarded, *after):
    send_a, recv_a, srcs, lands = state
    send_b, recv_b = forwarded
    n = len(lands)

    def body(*refs):
        src_refs, land_refs = refs[:n], refs[n:2 * n]
        sa, ra, sb, rb = refs[2 * n:2 * n + 4]
        mine, sibling, chips = _chip_routes()
        for i in range(n):
            for k, to in enumerate([sibling] + chips):
                _remote(src_refs[i], land_refs[i].at[_slab(mine)], sa.at[k * n + i], ra.at[k * n + i], to).wait_send()
            theirs = land_refs[i].at[_slab(sibling)]
            _remote(theirs, theirs, sa.at[i], ra.at[i], sibling).wait_recv()
            for j, chip in enumerate(chips):
                sent = land_refs[i].at[_slab(chip)]
                got = land_refs[i].at[_slab((chip[0], chip[1], sibling[2]))]
                _remote(sent, sent, sb.at[j * n + i], rb.at[j * n + i], sibling).wait_send()
                _remote(got, got, sb.at[j * n + i], rb.at[j * n + i], sibling).wait_recv()

    outs = pl.pallas_call(
        body, name=name,
        out_shape=tuple(pltpu.HBM(a.shape, a.dtype) for a in lands),
        in_specs=[_HBM] * (2 * n) + [_SEM] * 4 + [pl.BlockSpec(memory_space=pl.ANY)] * len(after),
        out_specs=tuple([_HBM] * n),
        input_output_aliases={n + i: i for i in range(n)}, compiler_params=_EFFECT,
    )(*srcs, *lands, send_a, recv_a, send_b, recv_b, *after)
    return list(outs), list(srcs)


def _join_cols(parts, name, tr):
    _, r, c = parts.shape

    def body(p_ref, o_ref):
        for j in range(N_DEV):
            o_ref[:, j * c:(j + 1) * c] = p_ref[j]

    return _pcall(body, name, (r // tr,), [pl.BlockSpec((N_DEV, tr, c), lambda i: (0, i, 0))],
                  pl.BlockSpec((tr, N_DEV * c), lambda i: (i, 0)), _sds((r, N_DEV * c), parts.dtype))(parts)


def _split_cols(full, name, tr):
    r, c = full.shape[0], full.shape[1] // N_DEV

    def body(f_ref, o_ref):
        for j in range(N_DEV):
            o_ref[j] = f_ref[:, j * c:(j + 1) * c]

    return _pcall(body, name, (r // tr,), [pl.BlockSpec((tr, N_DEV * c), lambda i: (i, 0))],
                  pl.BlockSpec((N_DEV, tr, c), lambda i: (0, i, 0)), _sds((N_DEV, r, c), full.dtype))(full)


def _my_slab():
    return (4 * lax.axis_index("x") + 2 * lax.axis_index("y") + lax.axis_index("c")).astype(jnp.int32).reshape(1)


def _adamw(parts, sent, w, m, v, name, tile):
    _, rows, cols = w.shape

    def body(me_ref, p_ref, s_ref, w_ref, m_ref, v_ref, g_out, d_out, m_out, v_out):
        me = me_ref[0]
        g = jnp.where(me == 0, s_ref[0], p_ref[0]).astype(F32)
        for k in range(1, N_DEV):
            g = g + jnp.where(me == k, s_ref[0], p_ref[k]).astype(F32)
        m1 = ADAM_B1 * m_ref[0] + (1.0 - ADAM_B1) * g
        v1 = ADAM_B2 * v_ref[0] + (1.0 - ADAM_B2) * (g * g)
        m_hat = m1 / (1.0 - ADAM_B1 ** ADAM_STEP)
        v_hat = v1 / (1.0 - ADAM_B2 ** ADAM_STEP)
        g_out[0] = g
        d_out[0] = -ADAM_LR * (m_hat / (jnp.sqrt(v_hat) + ADAM_EPS) + ADAM_WD * w_ref[0])
        m_out[0] = m1
        v_out[0] = v1

    row = pl.BlockSpec((1, tile, cols), lambda i, me: (0, i, 0))
    return pl.pallas_call(
        body, name=name, out_shape=[_sds((1, rows, cols))] * 4,
        grid_spec=pltpu.PrefetchScalarGridSpec(
            num_scalar_prefetch=1, grid=(rows // tile,),
            in_specs=[pl.BlockSpec((N_DEV, tile, cols), lambda i, me: (0, i, 0)),
                      pl.BlockSpec((1, tile, cols), lambda i, me: (me[0], i, 0)), row, row, row],
            out_specs=[row, row, row, row]),
        compiler_params=pltpu.CompilerParams(dimension_semantics=("arbitrary",), vmem_limit_bytes=VMEM_LIMIT),
    )(_my_slab(), parts, sent, w, m, v)


BIG = {
    "w_in": ((N_IN // N_DEV, D_MODEL), False, N_IN // N_DEV // 3),
    "w_glu": ((S5_WIDTH // N_DEV, S5_WIDTH), False, S5_WIDTH // N_DEV),
    "w_pa": ((S5_WIDTH, D_MODEL // N_DEV), True, S5_WIDTH),
    "w_pb": ((HG_WIDTH, D_MODEL // N_DEV), True, HG_WIDTH),
    "w_out": ((D_MODEL // N_DEV, D_MODEL), False, D_MODEL // N_DEV),
    "w_up": ((2 * D_FF // N_DEV, D_MODEL), False, 2 * D_FF // N_DEV // 4),
    "w_conv": ((CONV_W, 2 * D_FF // N_DEV), True, CONV_W),
    "w_down": ((D_FF // N_DEV, D_MODEL), False, D_FF // N_DEV // 2),
}
TRANSPOSED = ("w_in", "w_up", "s5_b_re", "s5_b_im")
UNALIGNED_COLS = ("w_conv",)


def _stored(n, arr):
    return jnp.swapaxes(arr, -1, -2) if n in TRANSPOSED else arr


def _join_shards(n, parts):
    (a, b), by_cols, _ = BIG[n]
    if not by_cols:
        return parts.reshape(N_DEV * a, b)
    if n in UNALIGNED_COLS:
        return _join_cols(parts, "join_" + n, min(a, 256))
    return parts.transpose(1, 0, 2).reshape(a, N_DEV * b)


def _split_shards(n, full):
    (a, b), by_cols, _ = BIG[n]
    if not by_cols:
        return full.reshape(N_DEV, a, b)
    if n in UNALIGNED_COLS:
        return _split_cols(full, "split_" + n, min(a, 256))
    return full.reshape(a, N_DEV, b).transpose(1, 0, 2)


SMALL_CORE = {
    "s5_b_re": GSC, "s5_b_im": GSC, "s5_c_re": GSC, "s5_c_im": GSC,
    "g_mix": (1, D_MODEL), "g_ffn": (1, D_MODEL), "g_final": (1, D_MODEL), "s5_d": (1, S5_WIDTH),
    "b_glu": (1, S5_WIDTH), "hg_norm_gain": (1, HG_WIDTH), "hg_lb_logits": (2, HG_WIDTH), "b_conv": (1, 2 * D_FF),
    "s5_log_dt": (1, S5_GROUPS), "s5_a_re": (S5_GROUPS, S5_STATE), "s5_a_im": (S5_GROUPS, S5_STATE), "loss": (1, 1),
}
BLOCK_ROWS = 32


def _small_rows():
    rows, r = {}, 0
    for n, core in SMALL_CORE.items():
        rows[n] = r
        r += BLOCK_ROWS if len(core) == 3 else -(-math.prod(core) // PACK_W)
    return rows, -(-r // SUBLANES) * SUBLANES


SMALL_ROW, SMALL_ROWS = _small_rows()


def _small_pieces(name):
    r, core = SMALL_ROW[name], SMALL_CORE[name]
    if len(core) == 3:
        return [((g, slice(None), slice(None)), slice(r + S5_GROUP * (g % 2), r + S5_GROUP * (g % 2 + 1)),
                 slice(S5_STATE * (g // 2), S5_STATE * (g // 2 + 1))) for g in range(S5_GROUPS)]
    pieces = []
    for i in range(core[0]):
        for c0 in range(0, core[1], PACK_W):
            w, flat = min(PACK_W, core[1] - c0), i * core[1] + c0
            pieces.append(((slice(i, i + 1), slice(c0, c0 + w)), slice(r + flat // PACK_W, r + flat // PACK_W + 1),
                           slice(flat % PACK_W, flat % PACK_W + w)))
    return pieces


def _core_index(ref, name, idx):
    return (0,) * (len(ref.shape) - len(SMALL_CORE[name])) + idx


def _pack_small_grads(grads):
    names = list(SMALL_CORE)

    def body(*refs):
        pack = refs[-1]
        pack[...] = jnp.zeros_like(pack)
        for ref, n in zip(refs, names):
            for idx, rows, lanes in _small_pieces(n):
                pack[rows, lanes] = ref[_core_index(ref, n, idx)]

    return _pcall(body, "pack_small_grads", (1,), [_full(grads[n].shape) for n in names],
                  _full((SMALL_ROWS, PACK_W)), _sds((SMALL_ROWS, PACK_W)))(*[grads[n] for n in names])


def _adamw_small(parts, sent, names, rows, given, name):
    lo, hi = rows
    k = len(names)
    shapes = [given[n].shape for n in names]

    def body(*refs):
        me, p_ref, s_ref, ins, outs = refs[0][0], refs[1], refs[2], refs[3:3 + 3 * k], refs[3 + 3 * k:3 + 7 * k]
        packs, results = refs[3 + 7 * k:6 + 7 * k], refs[6 + 7 * k:]
        for j, pack in enumerate(packs):
            pack[...] = jnp.zeros_like(pack)
            for ref, n in zip(ins[j * k:(j + 1) * k], names):
                for idx, prow, lanes in _small_pieces(n):
                    pack[slice(prow.start - lo, prow.stop - lo), lanes] = ref[_core_index(ref, n, idx)]
        mine = s_ref[lo:hi, :]
        g = jnp.where(me == 0, mine, p_ref[0, lo:hi, :])
        for d in range(1, N_DEV):
            g = g + jnp.where(me == d, mine, p_ref[d, lo:hi, :])
        m1 = ADAM_B1 * packs[1][...] + (1.0 - ADAM_B1) * g
        v1 = ADAM_B2 * packs[2][...] + (1.0 - ADAM_B2) * (g * g)
        m_hat = m1 / (1.0 - ADAM_B1 ** ADAM_STEP)
        v_hat = v1 / (1.0 - ADAM_B2 ** ADAM_STEP)
        results[0][...] = g
        results[1][...] = -ADAM_LR * (m_hat / (jnp.sqrt(v_hat) + ADAM_EPS) + ADAM_WD * packs[0][...])
        results[2][...] = m1
        results[3][...] = v1
        for j, result in enumerate(results):
            for ref, n in zip(outs[j * k:(j + 1) * k], names):
                for idx, prow, lanes in _small_pieces(n):
                    ref[_core_index(ref, n, idx)] = result[slice(prow.start - lo, prow.stop - lo), lanes]

    flat = _pcall(body, name, (1,),
                  [pl.BlockSpec(memory_space=pltpu.SMEM), _full(parts.shape), _full(sent.shape)]
                  + [_full(s) for s in shapes] * 3,
                  [_full(s) for s in shapes] * 4, [_sds(s) for s in shapes] * 4,
                  scratch=[pltpu.VMEM((hi - lo, PACK_W), F32)] * 7,
                  )(_my_slab(), parts, sent, *[given[pre + n] for pre in ("", "m_", "v_") for n in names])
    return {n: [flat[j * k + i] for j in range(4)] for i, n in enumerate(names)}


def kernel(x, g_mix, w_in, s5_a_re, s5_a_im, s5_log_dt, s5_b_re, s5_b_im, s5_c_re, s5_c_im, s5_d, w_glu, b_glu, hg_lb_logits, hg_norm_gain, w_pa, w_pb, w_out, g_ffn, w_up, w_conv, b_conv, w_down, g_final, loss_target, m_g_mix, m_w_in, m_s5_a_re, m_s5_a_im, m_s5_log_dt, m_s5_b_re, m_s5_b_im, m_s5_c_re, m_s5_c_im, m_s5_d, m_w_glu, m_b_glu, m_hg_lb_logits, m_hg_norm_gain, m_w_pa, m_w_pb, m_w_out, m_g_ffn, m_w_up, m_w_conv, m_b_conv, m_w_down, m_g_final, v_g_mix, v_w_in, v_s5_a_re, v_s5_a_im, v_s5_log_dt, v_s5_b_re, v_s5_b_im, v_s5_c_re, v_s5_c_im, v_s5_d, v_w_glu, v_b_glu, v_hg_lb_logits, v_hg_norm_gain, v_w_pa, v_w_pb, v_w_out, v_g_ffn, v_w_up, v_w_conv, v_b_conv, v_w_down, v_g_final):
    given = dict(locals())
    small_names = [n for n in SMALL_CORE if n != "loss"]

    pay = {n: given[n][0] if n == "w_conv" else _stored(n, given[n])[0].astype(BF16) for n in BIG}
    groups = {"in": ["w_in"], "mix": ["w_glu", "w_pa", "w_pb", "w_out"], "ffn": ["w_up", "w_down", "w_conv"]}
    gathers, order = {}, pay["w_in"]
    for grp, names in groups.items():
        gathers[grp], order = _gather_start("gather_" + grp + "_start", [pay[n] for n in names], order)

    forwards = {}

    def forward(grp, *after):
        if grp == "in":
            after = (*after, order)
        forwards[grp] = _gather_forward("gather_" + grp + "_forward", gathers[grp], *after)

    def weights(grp, *after):
        if grp not in forwards:
            forward(grp, *after)
        got, _ = _gather_wait("gather_" + grp + "_wait", *forwards[grp], *after)
        return {n: _join_shards(n, g) for n, g in zip(groups[grp], got)}

    weights.forward = forward

    in_flight, started = [], []

    def emit(grads):
        names = list(grads)
        state, token = _exchange_start("grads_" + names[0] + "_start", [_split_shards(n, grads[n]) for n in names],
                                       grads[names[0]])
        in_flight.append((names, state))
        return token

    def emit_small(grads):
        pack = _pack_small_grads(grads)
        state, token = _gather_start("grads_small_start", [pack], pack)
        in_flight.append((["small"], state))
        started.append(token)

    sp = {n: (given[n] if n in ("g_final", "hg_lb_logits") else _stored(n, given[n])[0]) for n in small_names}
    sp["g_mix"] = _after(sp["g_mix"], order)
    dx = _local_step(x, loss_target, weights, sp, emit, emit_small)

    res = {}
    after = [started[-1]]
    for names, state in in_flight:
        if names == ["small"]:
            state, forwarded = _gather_forward("grads_small_forward", state, *after)
            parts, sent = _gather_wait("grads_small_wait", state, forwarded)
        else:
            parts, sent = _exchange_wait("grads_" + names[0] + "_wait", state, *after)
        if names != ["small"]:
            after = []
            for n, part, mine in zip(names, parts, sent):
                raw = _adamw(part, mine, *[_stored(n, given[pre + n]) for pre in ("", "m_", "v_")], "adamw_" + n,
                             BIG[n][2])
                res[n] = [_stored(n, r) for r in raw]
                after.append(raw[0])
            continue
        sgiven = {pre + n: _stored(n, given[pre + n]) for pre in ("", "m_", "v_") for n in small_names}
        for pre in ("", "m_", "v_"):
            sgiven[pre + "g_final"] = given[pre + "g_final"].reshape(1, D_MODEL)
            sgiven[pre + "loss"] = jnp.zeros((1, 1), F32)
        raw = _adamw_small(parts[0], sent[0], list(SMALL_CORE), (0, SMALL_ROWS), sgiven, "adamw_small")
        res.update({n: [_stored(n, r) for r in raw[n]] for n in small_names})
        res["g_final"] = [r.reshape(D_MODEL) for r in raw["g_final"]]
        total_loss = raw["loss"][0].reshape(())
        after = [raw["s5_b_re"][0], raw["g_mix"][0]]
    return (total_loss, dx, *[res[n][0] for n in WEIGHT_ORDER], *[res[n][1] for n in WEIGHT_ORDER],
            *[res[n][2] for n in WEIGHT_ORDER], *[res[n][3] for n in WEIGHT_ORDER])
```
